```python
import jax, jax.numpy as jnp
from jax import lax
import numpy as np

D_MODEL = 1024
BATCH = 8
SEQ = 2048
DEPTH = 1

PLE_DIM = 256
HEAD_DIM = 64
RWKV_HEADS = 8
RWKV_DIM = RWKV_HEADS * HEAD_DIM
DECAY_LORA = 64
ICLR_LORA = 64
GATE_LORA = 128
GN_EPS = 64e-5
ATTN_GROUPS = ((128, 1), (512, 4), (2048, 16))
HEADS_PER_GROUP = 4
ATTN_HEADS = HEADS_PER_GROUP * len(ATTN_GROUPS)
ATTN_DIM = ATTN_HEADS * HEAD_DIM
ATTN_OUT_DIM = HEADS_PER_GROUP * HEAD_DIM
BAND_BLOCK = 128
ROPE_THETA = 10000.0
NEG_INF = -1e30
D_FF = 2816
RMS_EPS = 1e-6
N_BRANCHES = 2
RWKV_COLS = 3 * RWKV_DIM + DECAY_LORA + ICLR_LORA + GATE_LORA
ATTN_COLS = 3 * ATTN_DIM
GATE_COLS = N_BRANCHES * D_MODEL
IN_COLS = RWKV_COLS + ATTN_COLS + GATE_COLS

kernel_name = 'hybrid_rwkv7_dilated_attn_macaron_block'


def rms_norm(x, gain):
    xf = x.astype(jnp.float32)
    y = xf * lax.rsqrt(jnp.mean(xf * xf, axis=-1, keepdims=True) + RMS_EPS)
    return (y * gain.astype(jnp.float32)).astype(x.dtype)


def swiglu(h, w_gate, w_up, w_down):
    return (jax.nn.silu(h @ w_gate) * (h @ w_up)) @ w_down


def token_shift(z):
    return jnp.pad(z, ((0, 0), (1, 0), (0, 0)))[:, :-1]


def apply_rope(x, cos, sin):
    x1, x2 = jnp.split(x.astype(jnp.float32), 2, axis=-1)
    return jnp.concatenate([x1 * cos - x2 * sin, x2 * cos + x1 * sin], axis=-1).astype(x.dtype)


def wkv7_scan(r, decay, k, v, kk, a):
    B, S, H, N = r.shape

    def step(state, inp):
        r_t, w_t, k_t, v_t, kk_t, a_t = inp
        sa = jnp.einsum('bhvk,bhk->bhv', state, -kk_t)
        state = (state * w_t[:, :, None, :]
                 + sa[..., None] * (kk_t * a_t)[:, :, None, :]
                 + v_t[..., None] * k_t[:, :, None, :])
        y_t = jnp.einsum('bhvk,bhk->bhv', state, r_t)
        return state, y_t

    xs = tuple(jnp.moveaxis(t, 1, 0) for t in (r, decay, k, v, kk, a))
    s0 = jnp.zeros((B, H, N, N), jnp.float32)
    _, y = lax.scan(step, s0, xs)
    return jnp.moveaxis(y, 0, 1)


def rwkv7_time_mix(z, mu, w0, w2, a0, a2, g2, k_k, k_a, r_k, gn_w, gn_b):
    B, S, _ = z.shape
    z = z + (token_shift(z) - z) * mu
    r, k, v, wd, ad, gd = jnp.split(
        z, [RWKV_DIM, 2 * RWKV_DIM, 3 * RWKV_DIM, 3 * RWKV_DIM + DECAY_LORA,
            3 * RWKV_DIM + DECAY_LORA + ICLR_LORA], axis=-1)
    w = -jax.nn.softplus(-(w0 + jnp.tanh(wd) @ w2)) - 0.5
    a = jax.nn.sigmoid(a0 + ad @ a2)
    g = jax.nn.sigmoid(gd) @ g2
    kk = k * k_k
    k = k * (1.0 + (a - 1.0) * k_a)

    def heads(t):
        return t.reshape(B, S, RWKV_HEADS, HEAD_DIM).astype(jnp.float32)

    kk = heads(kk)
    kk = kk * lax.rsqrt(jnp.maximum(jnp.sum(kk * kk, axis=-1, keepdims=True), 1e-24))
    decay = jnp.exp(-jnp.exp(heads(w)))
    r_h, k_h, v_h, a_h = heads(r), heads(k), heads(v), heads(a)
    y = wkv7_scan(r_h, decay, k_h, v_h, kk, a_h)
    mean = jnp.mean(y, axis=-1, keepdims=True)
    var = jnp.mean(jnp.square(y - mean), axis=-1, keepdims=True)
    y = ((y - mean) * lax.rsqrt(var + GN_EPS)).reshape(B, S, RWKV_DIM)
    y = y * gn_w.astype(jnp.float32) + gn_b.astype(jnp.float32)
    bonus = jnp.sum(r_h * k_h * r_k.astype(jnp.float32), axis=-1, keepdims=True) * v_h
    y = y + bonus.reshape(B, S, RWKV_DIM)
    return (y * g.astype(jnp.float32)).astype(z.dtype)


def banded_causal_attention(q, k, v, band):
    N, L, H, Dh = q.shape
    nb = -(-L // BAND_BLOCK)
    Lp = nb * BAND_BLOCK
    pad = Lp - L
    qb = jnp.pad(q, ((0, 0), (0, pad), (0, 0), (0, 0))).reshape(N, nb, BAND_BLOCK, H, Dh)

    def key_blocks(t):
        tp = jnp.pad(t, ((0, 0), (BAND_BLOCK, pad), (0, 0), (0, 0))).reshape(N, nb + 1, BAND_BLOCK, H, Dh)
        return jnp.concatenate([tp[:, :-1], tp[:, 1:]], axis=2)

    kb, vb = key_blocks(k), key_blocks(v)
    s = jnp.einsum('nbqhd,nbkhd->nbhqk', qb.astype(jnp.float32), kb.astype(jnp.float32)) * (Dh ** -0.5)
    blk = jnp.arange(nb)[:, None]
    qpos = blk * BAND_BLOCK + jnp.arange(BAND_BLOCK)[None, :]
    kpos = blk * BAND_BLOCK - BAND_BLOCK + jnp.arange(2 * BAND_BLOCK)[None, :]
    dist = qpos[:, :, None] - kpos[:, None, :]
    valid = (dist >= 0) & (dist <= band) & (kpos[:, None, :] >= 0)
    s = jnp.where(valid[None, :, None], s, NEG_INF)
    m = jnp.max(s, axis=-1, keepdims=True)
    e = jnp.exp(s - m)
    l = jnp.sum(e, axis=-1, keepdims=True)
    o = jnp.einsum('nbhqk,nbkhd->nbqhd', e, vb.astype(jnp.float32)) / jnp.swapaxes(l, 2, 3)
    lse = jnp.swapaxes((m + jnp.log(l))[..., 0], 2, 3)
    return o.reshape(N, Lp, H, Dh)[:, :L], lse.reshape(N, Lp, H)[:, :L]


def dilated_causal_attention(q, k, v, window, dilation):
    B, S, H, Dh = q.shape
    L = S // dilation

    def fold(t):
        return t.reshape(B, L, dilation, H, Dh).transpose(0, 2, 1, 3, 4).reshape(B * dilation, L, H, Dh)

    o, lse = banded_causal_attention(fold(q), fold(k), fold(v), window // dilation)
    o = o.reshape(B, dilation, L, H, Dh).transpose(0, 2, 1, 3, 4).reshape(B, S, H, Dh)
    lse = lse.reshape(B, dilation, L, H).transpose(0, 2, 1, 3).reshape(B, S, H)
    return o, lse


def dilated_attention_mix(z, cos, sin, q_gain, k_gain):
    B, S, _ = z.shape
    q, k, v = jnp.split(z, 3, axis=-1)
    q = q.reshape(B, S, ATTN_HEADS, HEAD_DIM)
    k = k.reshape(B, S, ATTN_HEADS, HEAD_DIM)
    v = v.reshape(B, S, ATTN_HEADS, HEAD_DIM)
    q = apply_rope(rms_norm(q, q_gain), cos, sin)
    k = apply_rope(rms_norm(k, k_gain), cos, sin)
    outs, lses = [], []
    for gi, (window, dilation) in enumerate(ATTN_GROUPS):
        hs = slice(gi * HEADS_PER_GROUP, (gi + 1) * HEADS_PER_GROUP)
        o, lse = dilated_causal_attention(q[:, :, hs], k[:, :, hs], v[:, :, hs], window, dilation)
        outs.append(o)
        lses.append(lse)
    wts = jax.nn.softmax(jnp.stack(lses, axis=0), axis=0)
    o = jnp.sum(wts[..., None] * jnp.stack(outs, axis=0), axis=0)
    return o.reshape(B, S, ATTN_OUT_DIM).astype(z.dtype)


def hybrid_layer(x, p_i, cos, sin, ffn1_norm, ffn1_w_gate, ffn1_w_up, ffn1_w_down, mix_norm, w_in,
                 rwkv_mu, rwkv_w0, rwkv_w2, rwkv_a0, rwkv_a2, rwkv_g2, rwkv_k_k, rwkv_k_a, rwkv_r_k,
                 rwkv_gn_w, rwkv_gn_b, q_norm, k_norm, w_br_rwkv, w_br_attn, w_out,
                 ffn2_norm, ffn2_w_gate, ffn2_w_up, ffn2_w_down, ple_norm, ple_w_gate, ple_w_proj):
    x = x + 0.5 * swiglu(rms_norm(x, ffn1_norm), ffn1_w_gate, ffn1_w_up, ffn1_w_down)
    h = rms_norm(x, mix_norm)
    z = h @ w_in
    z_rwkv, z_attn, z_gate = jnp.split(z, [RWKV_COLS, RWKV_COLS + ATTN_COLS], axis=-1)
    y_rwkv = rwkv7_time_mix(z_rwkv, rwkv_mu, rwkv_w0, rwkv_w2, rwkv_a0, rwkv_a2, rwkv_g2,
                            rwkv_k_k, rwkv_k_a, rwkv_r_k, rwkv_gn_w, rwkv_gn_b)
    y_attn = dilated_attention_mix(z_attn, cos, sin, q_norm, k_norm)
    g_rwkv, g_attn = jnp.split(jax.nn.sigmoid(z_gate), N_BRANCHES, axis=-1)
    merged = g_rwkv * (y_rwkv @ w_br_rwkv) + g_attn * (y_attn @ w_br_attn)
    x = x + merged @ w_out
    x = x + 0.5 * swiglu(rms_norm(x, ffn2_norm), ffn2_w_gate, ffn2_w_up, ffn2_w_down)
    x = x + jax.nn.sigmoid(rms_norm(x, ple_norm) @ ple_w_gate) * (p_i @ ple_w_proj)
    return x


def _fwd_setup_inputs(seed: int = 0) -> dict:
    key = jax.random.key(seed)
    ks = jax.random.split(key, 40)
    f32 = jnp.float32

    def nrm(i, shape, scale):
        return jax.random.normal(ks[i], shape, f32) * scale

    def gain(i, shape):
        return 1.0 + 0.02 * jax.random.normal(ks[i], shape, f32)

    L = DEPTH
    return {
        'x': nrm(0, (BATCH, SEQ, D_MODEL), 1.0),
        'p': nrm(1, (DEPTH, BATCH, SEQ, PLE_DIM), 1.0),
        'positions': (jnp.arange(SEQ, dtype=jnp.int32)[None, :]
                      + jax.random.randint(ks[2], (BATCH, 1), 0, 4096, dtype=jnp.int32)),
        'ffn1_norm': gain(3, (L, D_MODEL)),
        'ffn1_w_gate': nrm(4, (L, D_MODEL, D_FF), D_MODEL ** -0.5),
        'ffn1_w_up': nrm(5, (L, D_MODEL, D_FF), D_MODEL ** -0.5),
        'ffn1_w_down': nrm(6, (L, D_FF, D_MODEL), D_FF ** -0.5),
        'mix_norm': gain(7, (L, D_MODEL)),
        'w_in': nrm(8, (L, D_MODEL, IN_COLS), D_MODEL ** -0.5),
        'rwkv_mu': jax.random.uniform(ks[9], (L, RWKV_COLS), f32),
        'rwkv_w0': -0.5 - 4.5 * jax.random.uniform(ks[10], (L, RWKV_DIM), f32),
        'rwkv_w2': nrm(11, (L, DECAY_LORA, RWKV_DIM), 0.1),
        'rwkv_a0': nrm(12, (L, RWKV_DIM), 0.1),
        'rwkv_a2': nrm(13, (L, ICLR_LORA, RWKV_DIM), 0.1),
        'rwkv_g2': nrm(14, (L, GATE_LORA, RWKV_DIM), GATE_LORA ** -0.5),
        'rwkv_k_k': 0.85 + nrm(15, (L, RWKV_DIM), 0.02),
        'rwkv_k_a': 1.0 + nrm(16, (L, RWKV_DIM), 0.02),
        'rwkv_r_k': nrm(17, (L, RWKV_HEADS, HEAD_DIM), 0.1),
        'rwkv_gn_w': gain(18, (L, RWKV_DIM)),
        'rwkv_gn_b': nrm(19, (L, RWKV_DIM), 0.01),
        'q_norm': gain(20, (L, HEAD_DIM)),
        'k_norm': gain(21, (L, HEAD_DIM)),
        'w_br_rwkv': nrm(22, (L, RWKV_DIM, D_MODEL), RWKV_DIM ** -0.5),
        'w_br_attn': nrm(23, (L, ATTN_OUT_DIM, D_MODEL), ATTN_OUT_DIM ** -0.5),
        'w_out': nrm(24, (L, D_MODEL, D_MODEL), D_MODEL ** -0.5),
        'ffn2_norm': gain(25, (L, D_MODEL)),
        'ffn2_w_gate': nrm(26, (L, D_MODEL, D_FF), D_MODEL ** -0.5),
        'ffn2_w_up': nrm(27, (L, D_MODEL, D_FF), D_MODEL ** -0.5),
        'ffn2_w_down': nrm(28, (L, D_FF, D_MODEL), D_FF ** -0.5),
        'ple_norm': gain(29, (L, D_MODEL)),
        'ple_w_gate': nrm(30, (L, D_MODEL, D_MODEL), D_MODEL ** -0.5),
        'ple_w_proj': nrm(31, (L, PLE_DIM, D_MODEL), PLE_DIM ** -0.5),
    }


def _fwd_reference(x, p, positions, ffn1_norm, ffn1_w_gate, ffn1_w_up, ffn1_w_down, mix_norm, w_in,
              rwkv_mu, rwkv_w0, rwkv_w2, rwkv_a0, rwkv_a2, rwkv_g2, rwkv_k_k, rwkv_k_a, rwkv_r_k,
              rwkv_gn_w, rwkv_gn_b, q_norm, k_norm, w_br_rwkv, w_br_attn, w_out,
              ffn2_norm, ffn2_w_gate, ffn2_w_up, ffn2_w_down, ple_norm, ple_w_gate, ple_w_proj):
    inv_freq = 1.0 / (ROPE_THETA ** (jnp.arange(0, HEAD_DIM, 2, dtype=jnp.float32) / HEAD_DIM))
    ang = positions.astype(jnp.float32)[..., None] * inv_freq
    cos = jnp.cos(ang)[:, :, None, :]
    sin = jnp.sin(ang)[:, :, None, :]
    for i in range(DEPTH):
        x = hybrid_layer(x, p[i], cos, sin, ffn1_norm[i], ffn1_w_gate[i], ffn1_w_up[i], ffn1_w_down[i],
                         mix_norm[i], w_in[i], rwkv_mu[i], rwkv_w0[i], rwkv_w2[i], rwkv_a0[i], rwkv_a2[i],
                         rwkv_g2[i], rwkv_k_k[i], rwkv_k_a[i], rwkv_r_k[i], rwkv_gn_w[i], rwkv_gn_b[i],
                         q_norm[i], k_norm[i], w_br_rwkv[i], w_br_attn[i], w_out[i],
                         ffn2_norm[i], ffn2_w_gate[i], ffn2_w_up[i], ffn2_w_down[i],
                         ple_norm[i], ple_w_gate[i], ple_w_proj[i])
    return x


import jax as _jax
import jax.numpy as _jnp

TWIN_FORMAT = 'train_step'
FWD_PARAMS = ['x', 'p', 'positions', 'ffn1_norm', 'ffn1_w_gate', 'ffn1_w_up', 'ffn1_w_down', 'mix_norm', 'w_in', 'rwkv_mu', 'rwkv_w0', 'rwkv_w2', 'rwkv_a0', 'rwkv_a2', 'rwkv_g2', 'rwkv_k_k', 'rwkv_k_a', 'rwkv_r_k', 'rwkv_gn_w', 'rwkv_gn_b', 'q_norm', 'k_norm', 'w_br_rwkv', 'w_br_attn', 'w_out', 'ffn2_norm', 'ffn2_w_gate', 'ffn2_w_up', 'ffn2_w_down', 'ple_norm', 'ple_w_gate', 'ple_w_proj']
TWIN_WEIGHTS = ['ffn1_norm', 'ffn1_w_gate', 'ffn1_w_up', 'ffn1_w_down', 'mix_norm', 'w_in', 'rwkv_mu', 'rwkv_w0', 'rwkv_w2', 'rwkv_a0', 'rwkv_a2', 'rwkv_g2', 'rwkv_k_k', 'rwkv_k_a', 'rwkv_r_k', 'rwkv_gn_w', 'rwkv_gn_b', 'q_norm', 'k_norm', 'w_br_rwkv', 'w_br_attn', 'w_out', 'ffn2_norm', 'ffn2_w_gate', 'ffn2_w_up', 'ffn2_w_down', 'ple_norm', 'ple_w_gate', 'ple_w_proj']
TWIN_DIFF_INPUT = 'x'
TWIN_INPUTS = ['x', 'p', 'positions', 'ffn1_norm', 'ffn1_w_gate', 'ffn1_w_up', 'ffn1_w_down', 'mix_norm', 'w_in', 'rwkv_mu', 'rwkv_w0', 'rwkv_w2', 'rwkv_a0', 'rwkv_a2', 'rwkv_g2', 'rwkv_k_k', 'rwkv_k_a', 'rwkv_r_k', 'rwkv_gn_w', 'rwkv_gn_b', 'q_norm', 'k_norm', 'w_br_rwkv', 'w_br_attn', 'w_out', 'ffn2_norm', 'ffn2_w_gate', 'ffn2_w_up', 'ffn2_w_down', 'ple_norm', 'ple_w_gate', 'ple_w_proj', 'loss_target', 'm_ffn1_norm', 'm_ffn1_w_gate', 'm_ffn1_w_up', 'm_ffn1_w_down', 'm_mix_norm', 'm_w_in', 'm_rwkv_mu', 'm_rwkv_w0', 'm_rwkv_w2', 'm_rwkv_a0', 'm_rwkv_a2', 'm_rwkv_g2', 'm_rwkv_k_k', 'm_rwkv_k_a', 'm_rwkv_r_k', 'm_rwkv_gn_w', 'm_rwkv_gn_b', 'm_q_norm', 'm_k_norm', 'm_w_br_rwkv', 'm_w_br_attn', 'm_w_out', 'm_ffn2_norm', 'm_ffn2_w_gate', 'm_ffn2_w_up', 'm_ffn2_w_down', 'm_ple_norm', 'm_ple_w_gate', 'm_ple_w_proj', 'v_ffn1_norm', 'v_ffn1_w_gate', 'v_ffn1_w_up', 'v_ffn1_w_down', 'v_mix_norm', 'v_w_in', 'v_rwkv_mu', 'v_rwkv_w0', 'v_rwkv_w2', 'v_rwkv_a0', 'v_rwkv_a2', 'v_rwkv_g2', 'v_rwkv_k_k', 'v_rwkv_k_a', 'v_rwkv_r_k', 'v_rwkv_gn_w', 'v_rwkv_gn_b', 'v_q_norm', 'v_k_norm', 'v_w_br_rwkv', 'v_w_br_attn', 'v_w_out', 'v_ffn2_norm', 'v_ffn2_w_gate', 'v_ffn2_w_up', 'v_ffn2_w_down', 'v_ple_norm', 'v_ple_w_gate', 'v_ple_w_proj']
TWIN_OUTPUTS = ['loss', 'grad_x', 'grad_ffn1_norm', 'grad_ffn1_w_gate', 'grad_ffn1_w_up', 'grad_ffn1_w_down', 'grad_mix_norm', 'grad_w_in', 'grad_rwkv_mu', 'grad_rwkv_w0', 'grad_rwkv_w2', 'grad_rwkv_a0', 'grad_rwkv_a2', 'grad_rwkv_g2', 'grad_rwkv_k_k', 'grad_rwkv_k_a', 'grad_rwkv_r_k', 'grad_rwkv_gn_w', 'grad_rwkv_gn_b', 'grad_q_norm', 'grad_k_norm', 'grad_w_br_rwkv', 'grad_w_br_attn', 'grad_w_out', 'grad_ffn2_norm', 'grad_ffn2_w_gate', 'grad_ffn2_w_up', 'grad_ffn2_w_down', 'grad_ple_norm', 'grad_ple_w_gate', 'grad_ple_w_proj', 'delta_ffn1_norm', 'delta_ffn1_w_gate', 'delta_ffn1_w_up', 'delta_ffn1_w_down', 'delta_mix_norm', 'delta_w_in', 'delta_rwkv_mu', 'delta_rwkv_w0', 'delta_rwkv_w2', 'delta_rwkv_a0', 'delta_rwkv_a2', 'delta_rwkv_g2', 'delta_rwkv_k_k', 'delta_rwkv_k_a', 'delta_rwkv_r_k', 'delta_rwkv_gn_w', 'delta_rwkv_gn_b', 'delta_q_norm', 'delta_k_norm', 'delta_w_br_rwkv', 'delta_w_br_attn', 'delta_w_out', 'delta_ffn2_norm', 'delta_ffn2_w_gate', 'delta_ffn2_w_up', 'delta_ffn2_w_down', 'delta_ple_norm', 'delta_ple_w_gate', 'delta_ple_w_proj', 'new_m_ffn1_norm', 'new_m_ffn1_w_gate', 'new_m_ffn1_w_up', 'new_m_ffn1_w_down', 'new_m_mix_norm', 'new_m_w_in', 'new_m_rwkv_mu', 'new_m_rwkv_w0', 'new_m_rwkv_w2', 'new_m_rwkv_a0', 'new_m_rwkv_a2', 'new_m_rwkv_g2', 'new_m_rwkv_k_k', 'new_m_rwkv_k_a', 'new_m_rwkv_r_k', 'new_m_rwkv_gn_w', 'new_m_rwkv_gn_b', 'new_m_q_norm', 'new_m_k_norm', 'new_m_w_br_rwkv', 'new_m_w_br_attn', 'new_m_w_out', 'new_m_ffn2_norm', 'new_m_ffn2_w_gate', 'new_m_ffn2_w_up', 'new_m_ffn2_w_down', 'new_m_ple_norm', 'new_m_ple_w_gate', 'new_m_ple_w_proj', 'new_v_ffn1_norm', 'new_v_ffn1_w_gate', 'new_v_ffn1_w_up', 'new_v_ffn1_w_down', 'new_v_mix_norm', 'new_v_w_in', 'new_v_rwkv_mu', 'new_v_rwkv_w0', 'new_v_rwkv_w2', 'new_v_rwkv_a0', 'new_v_rwkv_a2', 'new_v_rwkv_g2', 'new_v_rwkv_k_k', 'new_v_rwkv_k_a', 'new_v_rwkv_r_k', 'new_v_rwkv_gn_w', 'new_v_rwkv_gn_b', 'new_v_q_norm', 'new_v_k_norm', 'new_v_w_br_rwkv', 'new_v_w_br_attn', 'new_v_w_out', 'new_v_ffn2_norm', 'new_v_ffn2_w_gate', 'new_v_ffn2_w_up', 'new_v_ffn2_w_down', 'new_v_ple_norm', 'new_v_ple_w_gate', 'new_v_ple_w_proj']
TWIN_LEAF_KINDS = {'loss': 'loss', 'grad_x': 'grad_x', 'grad_ffn1_norm': 'grad_w', 'grad_ffn1_w_gate': 'grad_w', 'grad_ffn1_w_up': 'grad_w', 'grad_ffn1_w_down': 'grad_w', 'grad_mix_norm': 'grad_w', 'grad_w_in': 'grad_w', 'grad_rwkv_mu': 'grad_w', 'grad_rwkv_w0': 'grad_w', 'grad_rwkv_w2': 'grad_w', 'grad_rwkv_a0': 'grad_w', 'grad_rwkv_a2': 'grad_w', 'grad_rwkv_g2': 'grad_w', 'grad_rwkv_k_k': 'grad_w', 'grad_rwkv_k_a': 'grad_w', 'grad_rwkv_r_k': 'grad_w', 'grad_rwkv_gn_w': 'grad_w', 'grad_rwkv_gn_b': 'grad_w', 'grad_q_norm': 'grad_w', 'grad_k_norm': 'grad_w', 'grad_w_br_rwkv': 'grad_w', 'grad_w_br_attn': 'grad_w', 'grad_w_out': 'grad_w', 'grad_ffn2_norm': 'grad_w', 'grad_ffn2_w_gate': 'grad_w', 'grad_ffn2_w_up': 'grad_w', 'grad_ffn2_w_down': 'grad_w', 'grad_ple_norm': 'grad_w', 'grad_ple_w_gate': 'grad_w', 'grad_ple_w_proj': 'grad_w', 'delta_ffn1_norm': 'delta_w', 'delta_ffn1_w_gate': 'delta_w', 'delta_ffn1_w_up': 'delta_w', 'delta_ffn1_w_down': 'delta_w', 'delta_mix_norm': 'delta_w', 'delta_w_in': 'delta_w', 'delta_rwkv_mu': 'delta_w', 'delta_rwkv_w0': 'delta_w', 'delta_rwkv_w2': 'delta_w', 'delta_rwkv_a0': 'delta_w', 'delta_rwkv_a2': 'delta_w', 'delta_rwkv_g2': 'delta_w', 'delta_rwkv_k_k': 'delta_w', 'delta_rwkv_k_a': 'delta_w', 'delta_rwkv_r_k': 'delta_w', 'delta_rwkv_gn_w': 'delta_w', 'delta_rwkv_gn_b': 'delta_w', 'delta_q_norm': 'delta_w', 'delta_k_norm': 'delta_w', 'delta_w_br_rwkv': 'delta_w', 'delta_w_br_attn': 'delta_w', 'delta_w_out': 'delta_w', 'delta_ffn2_norm': 'delta_w', 'delta_ffn2_w_gate': 'delta_w', 'delta_ffn2_w_up': 'delta_w', 'delta_ffn2_w_down': 'delta_w', 'delta_ple_norm': 'delta_w', 'delta_ple_w_gate': 'delta_w', 'delta_ple_w_proj': 'delta_w', 'new_m_ffn1_norm': 'new_m', 'new_m_ffn1_w_gate': 'new_m', 'new_m_ffn1_w_up': 'new_m', 'new_m_ffn1_w_down': 'new_m', 'new_m_mix_norm': 'new_m', 'new_m_w_in': 'new_m', 'new_m_rwkv_mu': 'new_m', 'new_m_rwkv_w0': 'new_m', 'new_m_rwkv_w2': 'new_m', 'new_m_rwkv_a0': 'new_m', 'new_m_rwkv_a2': 'new_m', 'new_m_rwkv_g2': 'new_m', 'new_m_rwkv_k_k': 'new_m', 'new_m_rwkv_k_a': 'new_m', 'new_m_rwkv_r_k': 'new_m', 'new_m_rwkv_gn_w': 'new_m', 'new_m_rwkv_gn_b': 'new_m', 'new_m_q_norm': 'new_m', 'new_m_k_norm': 'new_m', 'new_m_w_br_rwkv': 'new_m', 'new_m_w_br_attn': 'new_m', 'new_m_w_out': 'new_m', 'new_m_ffn2_norm': 'new_m', 'new_m_ffn2_w_gate': 'new_m', 'new_m_ffn2_w_up': 'new_m', 'new_m_ffn2_w_down': 'new_m', 'new_m_ple_norm': 'new_m', 'new_m_ple_w_gate': 'new_m', 'new_m_ple_w_proj': 'new_m', 'new_v_ffn1_norm': 'new_v', 'new_v_ffn1_w_gate': 'new_v', 'new_v_ffn1_w_up': 'new_v', 'new_v_ffn1_w_down': 'new_v', 'new_v_mix_norm': 'new_v', 'new_v_w_in': 'new_v', 'new_v_rwkv_mu': 'new_v', 'new_v_rwkv_w0': 'new_v', 'new_v_rwkv_w2': 'new_v', 'new_v_rwkv_a0': 'new_v', 'new_v_rwkv_a2': 'new_v', 'new_v_rwkv_g2': 'new_v', 'new_v_rwkv_k_k': 'new_v', 'new_v_rwkv_k_a': 'new_v', 'new_v_rwkv_r_k': 'new_v', 'new_v_rwkv_gn_w': 'new_v', 'new_v_rwkv_gn_b': 'new_v', 'new_v_q_norm': 'new_v', 'new_v_k_norm': 'new_v', 'new_v_w_br_rwkv': 'new_v', 'new_v_w_br_attn': 'new_v', 'new_v_w_out': 'new_v', 'new_v_ffn2_norm': 'new_v', 'new_v_ffn2_w_gate': 'new_v', 'new_v_ffn2_w_up': 'new_v', 'new_v_ffn2_w_down': 'new_v', 'new_v_ple_norm': 'new_v', 'new_v_ple_w_gate': 'new_v', 'new_v_ple_w_proj': 'new_v'}


def _forward(args):
    return _fwd_reference(*[args[k] for k in FWD_PARAMS])


def _output_shape():
    out = _jax.eval_shape(lambda: _forward(_fwd_setup_inputs(0)))
    return out.shape, out.dtype

N_MICROBATCH = 1
ADAM_LR = 0.001
ADAM_B1 = 0.9
ADAM_B2 = 0.999
ADAM_EPS = 1e-08
ADAM_WD = 0.01
ADAM_STEP = 10
PER_EXAMPLE_BATCH_AXIS = {'x': 0, 'p': 1, 'positions': 0, 'loss_target': 0}
SHARED_INPUTS = []
_WEIGHT_DTYPES = {'ffn1_norm': _jnp.float32, 'ffn1_w_gate': _jnp.float32, 'ffn1_w_up': _jnp.float32, 'ffn1_w_down': _jnp.float32, 'mix_norm': _jnp.float32, 'w_in': _jnp.float32, 'rwkv_mu': _jnp.float32, 'rwkv_w0': _jnp.float32, 'rwkv_w2': _jnp.float32, 'rwkv_a0': _jnp.float32, 'rwkv_a2': _jnp.float32, 'rwkv_g2': _jnp.float32, 'rwkv_k_k': _jnp.float32, 'rwkv_k_a': _jnp.float32, 'rwkv_r_k': _jnp.float32, 'rwkv_gn_w': _jnp.float32, 'rwkv_gn_b': _jnp.float32, 'q_norm': _jnp.float32, 'k_norm': _jnp.float32, 'w_br_rwkv': _jnp.float32, 'w_br_attn': _jnp.float32, 'w_out': _jnp.float32, 'ffn2_norm': _jnp.float32, 'ffn2_w_gate': _jnp.float32, 'ffn2_w_up': _jnp.float32, 'ffn2_w_down': _jnp.float32, 'ple_norm': _jnp.float32, 'ple_w_gate': _jnp.float32, 'ple_w_proj': _jnp.float32}
MOMENT_SCALE = {'ffn1_norm': 3.056958e+00, 'ffn1_w_gate': 5.103074e-02, 'ffn1_w_up': 5.355578e-02, 'ffn1_w_down': 8.703433e-02, 'mix_norm': 4.164138e-01, 'w_in': 6.117058e-02, 'rwkv_mu': 1.199124e+00, 'rwkv_w0': 5.166685e-02, 'rwkv_w2': 6.651812e-03, 'rwkv_a0': 1.109689e-01, 'rwkv_a2': 3.905693e-02, 'rwkv_g2': 2.392734e+00, 'rwkv_k_k': 9.372988e-02, 'rwkv_k_a': 2.307538e-01, 'rwkv_r_k': 1.755495e+00, 'rwkv_gn_w': 4.147841e+00, 'rwkv_gn_b': 5.909700e-01, 'q_norm': 4.182704e-01, 'k_norm': 4.241186e-01, 'w_br_rwkv': 1.181527e-01, 'w_br_attn': 2.479528e-02, 'w_out': 9.318068e-02, 'ffn2_norm': 3.096397e+00, 'ffn2_w_gate': 4.223866e-02, 'ffn2_w_up': 4.693917e-02, 'ffn2_w_down': 7.579576e-02, 'ple_norm': 4.803548e-01, 'ple_w_gate': 4.941844e-02, 'ple_w_proj': 2.529332e-01}


def _to_microbatches(a, axis):
    t = _jnp.moveaxis(a, axis, 0)
    t = t.reshape((N_MICROBATCH, t.shape[0] // N_MICROBATCH) + t.shape[1:])
    return _jnp.moveaxis(t, 1, axis + 1)


def setup_inputs(seed: int = 0) -> dict:
    inp = _fwd_setup_inputs(seed)
    key = _jax.random.fold_in(_jax.random.key(seed), 7919)
    shape, _ = _output_shape()
    out = dict(inp)
    out["loss_target"] = _jax.random.normal(_jax.random.fold_in(key, 0), shape, _jnp.float32)
    for i, name in enumerate(TWIN_WEIGHTS):
        w = inp[name].astype(_jnp.float32)
        if MOMENT_SCALE is None:
            s = _jnp.sqrt(_jnp.mean(_jnp.square(w)) + 1e-30)
        else:
            s = MOMENT_SCALE[name]
        km, kv = _jax.random.split(_jax.random.fold_in(key, i + 1))
        out[name] = w
        out["m_" + name] = s * _jax.random.normal(km, w.shape, _jnp.float32)
        out["v_" + name] = (s * s) * _jax.random.uniform(kv, w.shape, _jnp.float32, 0.5, 1.5)
    if N_MICROBATCH > 1:
        for name, axis in PER_EXAMPLE_BATCH_AXIS.items():
            out[name] = _to_microbatches(out[name], axis)
    return {'x': out['x'], 'p': out['p'], 'positions': out['positions'], 'ffn1_norm': out['ffn1_norm'], 'ffn1_w_gate': out['ffn1_w_gate'], 'ffn1_w_up': out['ffn1_w_up'], 'ffn1_w_down': out['ffn1_w_down'], 'mix_norm': out['mix_norm'], 'w_in': out['w_in'], 'rwkv_mu': out['rwkv_mu'], 'rwkv_w0': out['rwkv_w0'], 'rwkv_w2': out['rwkv_w2'], 'rwkv_a0': out['rwkv_a0'], 'rwkv_a2': out['rwkv_a2'], 'rwkv_g2': out['rwkv_g2'], 'rwkv_k_k': out['rwkv_k_k'], 'rwkv_k_a': out['rwkv_k_a'], 'rwkv_r_k': out['rwkv_r_k'], 'rwkv_gn_w': out['rwkv_gn_w'], 'rwkv_gn_b': out['rwkv_gn_b'], 'q_norm': out['q_norm'], 'k_norm': out['k_norm'], 'w_br_rwkv': out['w_br_rwkv'], 'w_br_attn': out['w_br_attn'], 'w_out': out['w_out'], 'ffn2_norm': out['ffn2_norm'], 'ffn2_w_gate': out['ffn2_w_gate'], 'ffn2_w_up': out['ffn2_w_up'], 'ffn2_w_down': out['ffn2_w_down'], 'ple_norm': out['ple_norm'], 'ple_w_gate': out['ple_w_gate'], 'ple_w_proj': out['ple_w_proj'], 'loss_target': out['loss_target'], 'm_ffn1_norm': out['m_ffn1_norm'], 'm_ffn1_w_gate': out['m_ffn1_w_gate'], 'm_ffn1_w_up': out['m_ffn1_w_up'], 'm_ffn1_w_down': out['m_ffn1_w_down'], 'm_mix_norm': out['m_mix_norm'], 'm_w_in': out['m_w_in'], 'm_rwkv_mu': out['m_rwkv_mu'], 'm_rwkv_w0': out['m_rwkv_w0'], 'm_rwkv_w2': out['m_rwkv_w2'], 'm_rwkv_a0': out['m_rwkv_a0'], 'm_rwkv_a2': out['m_rwkv_a2'], 'm_rwkv_g2': out['m_rwkv_g2'], 'm_rwkv_k_k': out['m_rwkv_k_k'], 'm_rwkv_k_a': out['m_rwkv_k_a'], 'm_rwkv_r_k': out['m_rwkv_r_k'], 'm_rwkv_gn_w': out['m_rwkv_gn_w'], 'm_rwkv_gn_b': out['m_rwkv_gn_b'], 'm_q_norm': out['m_q_norm'], 'm_k_norm': out['m_k_norm'], 'm_w_br_rwkv': out['m_w_br_rwkv'], 'm_w_br_attn': out['m_w_br_attn'], 'm_w_out': out['m_w_out'], 'm_ffn2_norm': out['m_ffn2_norm'], 'm_ffn2_w_gate': out['m_ffn2_w_gate'], 'm_ffn2_w_up': out['m_ffn2_w_up'], 'm_ffn2_w_down': out['m_ffn2_w_down'], 'm_ple_norm': out['m_ple_norm'], 'm_ple_w_gate': out['m_ple_w_gate'], 'm_ple_w_proj': out['m_ple_w_proj'], 'v_ffn1_norm': out['v_ffn1_norm'], 'v_ffn1_w_gate': out['v_ffn1_w_gate'], 'v_ffn1_w_up': out['v_ffn1_w_up'], 'v_ffn1_w_down': out['v_ffn1_w_down'], 'v_mix_norm': out['v_mix_norm'], 'v_w_in': out['v_w_in'], 'v_rwkv_mu': out['v_rwkv_mu'], 'v_rwkv_w0': out['v_rwkv_w0'], 'v_rwkv_w2': out['v_rwkv_w2'], 'v_rwkv_a0': out['v_rwkv_a0'], 'v_rwkv_a2': out['v_rwkv_a2'], 'v_rwkv_g2': out['v_rwkv_g2'], 'v_rwkv_k_k': out['v_rwkv_k_k'], 'v_rwkv_k_a': out['v_rwkv_k_a'], 'v_rwkv_r_k': out['v_rwkv_r_k'], 'v_rwkv_gn_w': out['v_rwkv_gn_w'], 'v_rwkv_gn_b': out['v_rwkv_gn_b'], 'v_q_norm': out['v_q_norm'], 'v_k_norm': out['v_k_norm'], 'v_w_br_rwkv': out['v_w_br_rwkv'], 'v_w_br_attn': out['v_w_br_attn'], 'v_w_out': out['v_w_out'], 'v_ffn2_norm': out['v_ffn2_norm'], 'v_ffn2_w_gate': out['v_ffn2_w_gate'], 'v_ffn2_w_up': out['v_ffn2_w_up'], 'v_ffn2_w_down': out['v_ffn2_w_down'], 'v_ple_norm': out['v_ple_norm'], 'v_ple_w_gate': out['v_ple_w_gate'], 'v_ple_w_proj': out['v_ple_w_proj']}


def _loss(weights, diff, rest, loss_target):
    with _jax.named_scope("forward"):
        args = {**rest, TWIN_DIFF_INPUT: diff, **{k: w.astype(_WEIGHT_DTYPES[k]) for k, w in weights.items()}}
        y = _forward(args)
    with _jax.named_scope("loss_head"):
        err = _jnp.square(y.astype(_jnp.float32) - loss_target)
        return 0.5 * _jnp.sum(_jnp.mean(err, axis=-1)) if err.ndim else 0.5 * err


def _adamw(w, g, m, v):
    m = ADAM_B1 * m + (1.0 - ADAM_B1) * g
    v = ADAM_B2 * v + (1.0 - ADAM_B2) * _jnp.square(g)
    m_hat = m / (1.0 - ADAM_B1 ** ADAM_STEP)
    v_hat = v / (1.0 - ADAM_B2 ** ADAM_STEP)
    delta = -ADAM_LR * (m_hat / (_jnp.sqrt(v_hat) + ADAM_EPS) + ADAM_WD * w)
    return delta, m, v


def reference(x, p, positions, ffn1_norm, ffn1_w_gate, ffn1_w_up, ffn1_w_down, mix_norm, w_in, rwkv_mu, rwkv_w0, rwkv_w2, rwkv_a0, rwkv_a2, rwkv_g2, rwkv_k_k, rwkv_k_a, rwkv_r_k, rwkv_gn_w, rwkv_gn_b, q_norm, k_norm, w_br_rwkv, w_br_attn, w_out, ffn2_norm, ffn2_w_gate, ffn2_w_up, ffn2_w_down, ple_norm, ple_w_gate, ple_w_proj, loss_target, m_ffn1_norm, m_ffn1_w_gate, m_ffn1_w_up, m_ffn1_w_down, m_mix_norm, m_w_in, m_rwkv_mu, m_rwkv_w0, m_rwkv_w2, m_rwkv_a0, m_rwkv_a2, m_rwkv_g2, m_rwkv_k_k, m_rwkv_k_a, m_rwkv_r_k, m_rwkv_gn_w, m_rwkv_gn_b, m_q_norm, m_k_norm, m_w_br_rwkv, m_w_br_attn, m_w_out, m_ffn2_norm, m_ffn2_w_gate, m_ffn2_w_up, m_ffn2_w_down, m_ple_norm, m_ple_w_gate, m_ple_w_proj, v_ffn1_norm, v_ffn1_w_gate, v_ffn1_w_up, v_ffn1_w_down, v_mix_norm, v_w_in, v_rwkv_mu, v_rwkv_w0, v_rwkv_w2, v_rwkv_a0, v_rwkv_a2, v_rwkv_g2, v_rwkv_k_k, v_rwkv_k_a, v_rwkv_r_k, v_rwkv_gn_w, v_rwkv_gn_b, v_q_norm, v_k_norm, v_w_br_rwkv, v_w_br_attn, v_w_out, v_ffn2_norm, v_ffn2_w_gate, v_ffn2_w_up, v_ffn2_w_down, v_ple_norm, v_ple_w_gate, v_ple_w_proj):
    given = dict(x=x, p=p, positions=positions, ffn1_norm=ffn1_norm, ffn1_w_gate=ffn1_w_gate, ffn1_w_up=ffn1_w_up, ffn1_w_down=ffn1_w_down, mix_norm=mix_norm, w_in=w_in, rwkv_mu=rwkv_mu, rwkv_w0=rwkv_w0, rwkv_w2=rwkv_w2, rwkv_a0=rwkv_a0, rwkv_a2=rwkv_a2, rwkv_g2=rwkv_g2, rwkv_k_k=rwkv_k_k, rwkv_k_a=rwkv_k_a, rwkv_r_k=rwkv_r_k, rwkv_gn_w=rwkv_gn_w, rwkv_gn_b=rwkv_gn_b, q_norm=q_norm, k_norm=k_norm, w_br_rwkv=w_br_rwkv, w_br_attn=w_br_attn, w_out=w_out, ffn2_norm=ffn2_norm, ffn2_w_gate=ffn2_w_gate, ffn2_w_up=ffn2_w_up, ffn2_w_down=ffn2_w_down, ple_norm=ple_norm, ple_w_gate=ple_w_gate, ple_w_proj=ple_w_proj, loss_target=loss_target, m_ffn1_norm=m_ffn1_norm, m_ffn1_w_gate=m_ffn1_w_gate, m_ffn1_w_up=m_ffn1_w_up, m_ffn1_w_down=m_ffn1_w_down, m_mix_norm=m_mix_norm, m_w_in=m_w_in, m_rwkv_mu=m_rwkv_mu, m_rwkv_w0=m_rwkv_w0, m_rwkv_w2=m_rwkv_w2, m_rwkv_a0=m_rwkv_a0, m_rwkv_a2=m_rwkv_a2, m_rwkv_g2=m_rwkv_g2, m_rwkv_k_k=m_rwkv_k_k, m_rwkv_k_a=m_rwkv_k_a, m_rwkv_r_k=m_rwkv_r_k, m_rwkv_gn_w=m_rwkv_gn_w, m_rwkv_gn_b=m_rwkv_gn_b, m_q_norm=m_q_norm, m_k_norm=m_k_norm, m_w_br_rwkv=m_w_br_rwkv, m_w_br_attn=m_w_br_attn, m_w_out=m_w_out, m_ffn2_norm=m_ffn2_norm, m_ffn2_w_gate=m_ffn2_w_gate, m_ffn2_w_up=m_ffn2_w_up, m_ffn2_w_down=m_ffn2_w_down, m_ple_norm=m_ple_norm, m_ple_w_gate=m_ple_w_gate, m_ple_w_proj=m_ple_w_proj, v_ffn1_norm=v_ffn1_norm, v_ffn1_w_gate=v_ffn1_w_gate, v_ffn1_w_up=v_ffn1_w_up, v_ffn1_w_down=v_ffn1_w_down, v_mix_norm=v_mix_norm, v_w_in=v_w_in, v_rwkv_mu=v_rwkv_mu, v_rwkv_w0=v_rwkv_w0, v_rwkv_w2=v_rwkv_w2, v_rwkv_a0=v_rwkv_a0, v_rwkv_a2=v_rwkv_a2, v_rwkv_g2=v_rwkv_g2, v_rwkv_k_k=v_rwkv_k_k, v_rwkv_k_a=v_rwkv_k_a, v_rwkv_r_k=v_rwkv_r_k, v_rwkv_gn_w=v_rwkv_gn_w, v_rwkv_gn_b=v_rwkv_gn_b, v_q_norm=v_q_norm, v_k_norm=v_k_norm, v_w_br_rwkv=v_w_br_rwkv, v_w_br_attn=v_w_br_attn, v_w_out=v_w_out, v_ffn2_norm=v_ffn2_norm, v_ffn2_w_gate=v_ffn2_w_gate, v_ffn2_w_up=v_ffn2_w_up, v_ffn2_w_down=v_ffn2_w_down, v_ple_norm=v_ple_norm, v_ple_w_gate=v_ple_w_gate, v_ple_w_proj=v_ple_w_proj)
    weights = {n: given[n] for n in TWIN_WEIGHTS}
    shared = {n: given[n] for n in SHARED_INPUTS}
    per_example = {n: given[n] for n in ['x', 'p', 'positions']}
    grad_fn = _jax.value_and_grad(_loss, argnums=(0, 1))

    def one_microbatch(ex, loss_target):
        ex = dict(ex)
        diff = ex.pop(TWIN_DIFF_INPUT)
        return grad_fn(weights, diff, {**shared, **ex}, loss_target)

    if N_MICROBATCH == 1:
        loss, (grad_w, grad_x) = one_microbatch(per_example, given["loss_target"])
    else:
        def body(carry, xs):
            loss_sum, grad_sum = carry
            l_k, (gw_k, gx_k) = one_microbatch(xs[0], xs[1])
            with _jax.named_scope("update"):
                return (loss_sum + l_k, _jax.tree.map(_jnp.add, grad_sum, gw_k)), gx_k

        init = (_jnp.zeros((), _jnp.float32), _jax.tree.map(_jnp.zeros_like, weights))
        (loss, grad_w), grad_x = _jax.lax.scan(body, init, (per_example, given["loss_target"]))
    with _jax.named_scope("update"):
        delta_w, new_m, new_v = {}, {}, {}
        for n in TWIN_WEIGHTS:
            delta_w[n], new_m[n], new_v[n] = _adamw(weights[n], grad_w[n], given["m_" + n], given["v_" + n])
    return (loss, grad_x, *[grad_w[n] for n in TWIN_WEIGHTS], *[delta_w[n] for n in TWIN_WEIGHTS],
            *[new_m[n] for n in TWIN_WEIGHTS], *[new_v[n] for n in TWIN_WEIGHTS])
```

```python
import functools

import jax
import jax.numpy as jnp
from jax import lax
from jax.experimental import pallas as pl
from jax.experimental.pallas import tpu as pltpu

F32, BF16 = jnp.float32, jnp.bfloat16
HI = lax.Precision.HIGHEST
MESH = pl.DeviceIdType.MESH
SDS = jax.ShapeDtypeStruct

D_MODEL = 1024
HEAD = 64
RWKV_HEADS = 8
RWKV_DIM = RWKV_HEADS * HEAD
DECAY_LORA, ICLR_LORA, GATE_LORA = 64, 64, 128
GN_EPS = 64e-5
RMS_EPS = 1e-6
ATTN_DILATIONS = (1, 4, 16)
BAND = 128
ATTN_DIM = 768
GROUP_DIM = 256
ROPE_THETA = 10000.0
NEG_INF = -1e30
RWKV_COLS = 3 * RWKV_DIM + DECAY_LORA + ICLR_LORA + GATE_LORA
ATTN_COLS = 3 * ATTN_DIM
ADAM_LR, ADAM_B1, ADAM_B2, ADAM_EPS, ADAM_WD, ADAM_STEP = 0.001, 0.9, 0.999, 1e-08, 0.01, 10

WKV_CHUNK = 64
WKV_HEADS_PER_STEP = 2
N_CHIPS = 4
PACK_COLS = 1024
VMEM_LIMIT = 48 * 1024 * 1024

BIG = (
    ("ffn1_w_gate", (1024, 704), "col"), ("ffn1_w_up", (1024, 704), "col"), ("ffn1_w_down", (704, 1024), "row"),
    ("w_in", (1024, 1536), "col"), ("rwkv_w2", (64, 128), "col"), ("rwkv_a2", (64, 128), "col"),
    ("rwkv_g2", (128, 128), "col"), ("w_br_rwkv", (512, 256), "col"), ("w_br_attn", (256, 256), "col"),
    ("w_out", (256, 1024), "row"), ("ffn2_w_gate", (1024, 704), "col"), ("ffn2_w_up", (1024, 704), "col"),
    ("ffn2_w_down", (704, 1024), "row"), ("ple_w_gate", (256, 1024), "row"), ("ple_w_proj", (256, 256), "col"),
)
SMALL = (
    ("ffn1_norm", 1024), ("mix_norm", 1024), ("ffn2_norm", 1024), ("ple_norm", 1024), ("rwkv_mu", 1792),
    ("rwkv_w0", 512), ("rwkv_a0", 512), ("rwkv_k_k", 512), ("rwkv_k_a", 512), ("rwkv_r_k", 512),
    ("rwkv_gn_w", 512), ("rwkv_gn_b", 512), ("q_norm", 64), ("k_norm", 64),
)
SMALL_ROWS = 16
WEIGHTS = (
    "ffn1_norm", "ffn1_w_gate", "ffn1_w_up", "ffn1_w_down", "mix_norm", "w_in", "rwkv_mu", "rwkv_w0", "rwkv_w2",
    "rwkv_a0", "rwkv_a2", "rwkv_g2", "rwkv_k_k", "rwkv_k_a", "rwkv_r_k", "rwkv_gn_w", "rwkv_gn_b", "q_norm", "k_norm",
    "w_br_rwkv", "w_br_attn", "w_out", "ffn2_norm", "ffn2_w_gate", "ffn2_w_up", "ffn2_w_down", "ple_norm",
    "ple_w_gate", "ple_w_proj",
)


def _pick(n, cands):
    for c in cands:
        if n % c == 0:
            return c
    return n


def _mm(a, b, *, ta=False, tb=False, sum_blocks=False, out_dtype=F32, res=None, alpha=1.0, name):
    flat = a.ndim == 2 and b.ndim == 2
    a3 = a if a.ndim == 3 else a[None]
    b3 = b if b.ndim == 3 else b[None]
    na, nbb = a3.shape[0], b3.shape[0]
    nblk = max(na, nbb)
    kdim, m = (a3.shape[1], a3.shape[2]) if ta else (a3.shape[2], a3.shape[1])
    n = b3.shape[1] if tb else b3.shape[2]
    assert (b3.shape[2] if tb else b3.shape[1]) == kdim
    tm = _pick(m, (512, 256, 128))
    tn = _pick(n, (1024, 896, 768, 512, 256, 128))
    tk = _pick(kdim, (512, 256, 128))
    nk = kdim // tk

    if sum_blocks:
        grid = (m // tm, n // tn, nblk, nk)

        def ids(i, c, j, k):
            return i, c, j, k
    else:
        grid = (nblk, m // tm, n // tn, nk)

        def ids(j, i, c, k):
            return i, c, j, k

    def amap(*g):
        i, c, j, k = ids(*g)
        jj = j if na > 1 else 0
        return (jj, k, i) if ta else (jj, i, k)

    def bmap(*g):
        i, c, j, k = ids(*g)
        jj = j if nbb > 1 else 0
        return (jj, c, k) if tb else (jj, k, c)

    if sum_blocks:
        oshape, oblk = (m, n), (tm, tn)

        def omap(*g):
            i, c, j, k = ids(*g)
            return i, c
    else:
        oshape, oblk = (nblk, m, n), (1, tm, tn)

        def omap(*g):
            i, c, j, k = ids(*g)
            return j, i, c

    dn = (((0 if ta else 1,), (1 if tb else 0,)), ((), ()))
    has_res = res is not None

    def body(*refs):
        if has_res:
            a_ref, b_ref, r_ref, o_ref, acc = refs
        else:
            a_ref, b_ref, o_ref, acc = refs
        k = pl.program_id(3)
        if sum_blocks:
            j = pl.program_id(2)
            first = jnp.logical_and(j == 0, k == 0)
            last = jnp.logical_and(j == nblk - 1, k == nk - 1)
        else:
            first, last = k == 0, k == nk - 1

        @pl.when(first)
        def _():
            acc[...] = jnp.zeros_like(acc)

        acc[...] += lax.dot_general(a_ref[0].astype(BF16), b_ref[0].astype(BF16), dn, preferred_element_type=F32)

        @pl.when(last)
        def _():
            v = acc[...]
            if alpha != 1.0:
                v = v * alpha
            if has_res:
                v = v + r_ref[...].reshape(v.shape).astype(F32)
            o_ref[...] = v.reshape(o_ref.shape).astype(o_ref.dtype)

    in_specs = [pl.BlockSpec((1, tk, tm) if ta else (1, tm, tk), amap), pl.BlockSpec((1, tn, tk) if tb else (1, tk, tn), bmap)]
    args = [a3, b3]
    if has_res:
        res3 = res if (sum_blocks or res.ndim == 3) else res[None]
        in_specs.append(pl.BlockSpec(oblk, omap))
        args.append(res3)
    out = pl.pallas_call(
        body,
        name=name,
        grid=grid,
        in_specs=in_specs,
        out_specs=pl.BlockSpec(oblk, omap),
        out_shape=SDS(oshape, out_dtype),
        scratch_shapes=[pltpu.VMEM((tm, tn), F32)],
        compiler_params=pltpu.CompilerParams(
            dimension_semantics=("parallel", "parallel", "arbitrary", "arbitrary") if sum_blocks
            else ("parallel", "parallel", "parallel", "arbitrary"),
            vmem_limit_bytes=VMEM_LIMIT),
    )(*args)
    if flat and not sum_blocks:
        out = out[0]
    return out


def _rows_call(f, rows, params, outs, accs=(), *, tm, name):
    s = rows[0][0].shape[0]
    nr, npar, no = len(rows), len(params), len(outs)
    in_specs = [pl.BlockSpec((tm, w), functools.partial(lambda i, cb: (i, cb), cb=cb)) for (_, cb, w) in rows]
    in_specs += [pl.BlockSpec(p.shape, functools.partial(lambda i, nd: (0,) * nd, nd=p.ndim)) for p in params]
    out_shape = [SDS((s, w), dt) for (w, dt) in outs] + [SDS(tuple(sh), F32) for sh in accs]
    out_specs = [pl.BlockSpec((tm, w), lambda i: (i, 0)) for (w, _) in outs]
    out_specs += [pl.BlockSpec(tuple(sh), functools.partial(lambda i, nd: (0,) * nd, nd=len(sh))) for sh in accs]

    def body(*refs):
        rin, pin = refs[:nr], refs[nr:nr + npar]
        oo, ao = refs[nr + npar:nr + npar + no], refs[nr + npar + no:]
        res = f(*[r[...] for r in rin], *[p[...] for p in pin])
        if not isinstance(res, (tuple, list)):
            res = (res,)
        for o_ref, v in zip(oo, res[:no]):
            o_ref[...] = v.astype(o_ref.dtype)
        i = pl.program_id(0)
        for a_ref, v in zip(ao, res[no:]):
            @pl.when(i == 0)
            def _():
                a_ref[...] = jnp.zeros_like(a_ref)

            a_ref[...] += v.reshape(a_ref.shape)

    res = pl.pallas_call(
        body,
        name=name,
        grid=(s // tm,),
        in_specs=in_specs,
        out_specs=out_specs,
        out_shape=out_shape,
        compiler_params=pltpu.CompilerParams(dimension_semantics=("arbitrary",), vmem_limit_bytes=VMEM_LIMIT),
    )(*[r[0] for r in rows], *params)
    return res


def _mmv(a, b, mode):
    ca = 0 if mode[0] == "t" else 1
    cb = 1 if mode[1] == "t" else 0
    return lax.dot_general(a.astype(BF16), b.astype(BF16), (((ca,), (cb,)), ((), ())), preferred_element_type=F32)


@functools.partial(jax.custom_vjp, nondiff_argnums=(2,))
def _bdot(a, b, mode):
    return _mmv(a, b, mode)


def _bdot_fwd(a, b, mode):
    return _mmv(a, b, mode), (a, b)


def _bdot_bwd(mode, saved, g):
    a, b = saved
    if mode == "nn":
        return _mmv(g, b, "nt"), _mmv(a, g, "tn")
    if mode == "nt":
        return _mmv(g, b, "nn"), _mmv(g, a, "tn")
    return _mmv(b, g, "nt"), _mmv(a, g, "nn")


_bdot.defvjp(_bdot_fwd, _bdot_bwd)


def _hdot(a, b, mode="nn"):
    ca = 0 if mode[0] == "t" else 1
    cb = 1 if mode[1] == "t" else 0
    return lax.dot_general(a, b, (((ca,), (cb,)), ((), ())), precision=HI, preferred_element_type=F32)


def _segsum(x):
    c = x.shape[-1]
    r = lax.broadcasted_iota(jnp.int32, (c, c), 0) >> 6
    q = lax.broadcasted_iota(jnp.int32, (c, c), 1) >> 6
    return _hdot(x, jnp.where(r == q, 1.0, 0.0).astype(F32))


def _sigmoid(x):
    return jax.nn.sigmoid(x)


def _softplus(x):
    return jnp.maximum(x, 0.0) + jnp.log(1.0 + jnp.exp(-jnp.abs(x)))


def _rms(x, gain):
    return x * lax.rsqrt(jnp.mean(x * x, axis=-1, keepdims=True) + RMS_EPS) * gain


def _swiglu_act(gate, up):
    return gate * _sigmoid(gate) * up


def _rwkv_pre(zs, w0, w2, a0, a2, g2, k_k, k_a):
    r, k, v = zs[:, 0:512], zs[:, 512:1024], zs[:, 1024:1536]
    lora = zs[:, 1536:1792]
    wd, ad, gd = lora[:, 0:64], lora[:, 64:128], lora[:, 128:256]
    w = -_softplus(-(w0 + _bdot(jnp.tanh(wd), w2, "nn"))) - 0.5
    a = _sigmoid(a0 + _bdot(ad, a2, "nn"))
    g = _bdot(_sigmoid(gd), g2, "nn")
    kk = k * k_k
    kk = kk * lax.rsqrt(jnp.maximum(_segsum(kk * kk), 1e-24))
    k2 = k * (1.0 + (a - 1.0) * k_a)
    return r, -jnp.exp(w), k2, v, -kk, kk * a, g


def _rwkv_post(y, r, k2, v, g, gn_w, gn_b, r_k):
    mean = _segsum(y) * (1.0 / HEAD)
    yc = y - mean
    var = _segsum(yc * yc) * (1.0 / HEAD)
    yn = yc * lax.rsqrt(var + GN_EPS) * gn_w + gn_b
    bonus = _segsum(r * k2 * r_k) * v
    return (yn + bonus) * g


def _swap_halves(x):
    lane = lax.broadcasted_iota(jnp.int32, x.shape, 1)
    return jnp.where((lane & 32) == 0, jnp.roll(x, -32, axis=1), jnp.roll(x, 32, axis=1))


def _norm_rope(x, gain, cos, sin):
    heads = x.shape[1] // HEAD
    def rep(t):
        return jnp.concatenate([t] * heads, axis=1)

    xn = x * lax.rsqrt(_segsum(x * x) * (1.0 / HEAD) + RMS_EPS) * rep(gain)
    return xn * rep(cos) + _swap_halves(xn) * rep(sin)


def _attn_combine(o0, o1, o2, l0, l1, l2):
    m = jnp.maximum(jnp.maximum(l0, l1), l2)
    e0, e1, e2 = jnp.exp(l0 - m), jnp.exp(l1 - m), jnp.exp(l2 - m)
    return (e0 * o0 + e1 * o1 + e2 * o2) / (e0 + e1 + e2)


def _merge(zgr, zga, br, ba):
    return _sigmoid(zgr) * br + _sigmoid(zga) * ba


def _attn_block(q, kp, kc, vp, vc, has_prev):
    iq = lax.broadcasted_iota(jnp.int32, (BAND, BAND), 0)
    ik = lax.broadcasted_iota(jnp.int32, (BAND, BAND), 1)
    s_c = jnp.where(iq >= ik, _bdot(q, kc, "nt") * (HEAD ** -0.5), NEG_INF)
    s_p = jnp.where(jnp.logical_and(iq <= ik, has_prev), _bdot(q, kp, "nt") * (HEAD ** -0.5), NEG_INF)
    m = lax.stop_gradient(jnp.maximum(jnp.max(s_c, axis=-1, keepdims=True), jnp.max(s_p, axis=-1, keepdims=True)))
    e_c, e_p = jnp.exp(s_c - m), jnp.exp(s_p - m)
    l = jnp.sum(e_c, axis=-1, keepdims=True) + jnp.sum(e_p, axis=-1, keepdims=True)
    o = (_bdot(e_c, vc, "nn") + _bdot(e_p, vp, "nn")) / l
    return o, jnp.broadcast_to(m + jnp.log(l), o.shape)


def _tri_inv(a):
    t = a.shape[0]
    row = lax.broadcasted_iota(jnp.int32, (t, t), 0)
    col = lax.broadcasted_iota(jnp.int32, (t, t), 1)
    x = jnp.where(row == col, 1.0, 0.0).astype(F32)
    sh = 0
    while (1 << sh) < t:
        m = jnp.logical_and((row >> sh) == (col >> sh) + 1, (row >> (sh + 1)) == (col >> (sh + 1)))
        x = x + _hdot(_hdot(x, jnp.where(m, a, 0.0)), x)
        sh += 1
    return x


def _wkv_chunk(s0, r, lw, k, v, a, b):
    t = r.shape[0]
    row = lax.broadcasted_iota(jnp.int32, (t, t), 0)
    col = lax.broadcasted_iota(jnp.int32, (t, t), 1)
    incl, strict = row >= col, row > col
    cum = _hdot(jnp.where(incl, 1.0, 0.0).astype(F32), lw)
    c_end = cum[t - 1:t, :]
    e_in, e_ex, e_inv = jnp.exp(cum), jnp.exp(cum - lw), jnp.exp(-cum)
    at, rt, bt, kt = a * e_ex, r * e_in, b * e_inv, k * e_inv
    a_ab = jnp.where(strict, _hdot(at, bt, "nt"), 0.0)
    a_ak = jnp.where(strict, _hdot(at, kt, "nt"), 0.0)
    u = _hdot(_tri_inv(a_ab), _hdot(at, s0, "nt") + _hdot(a_ak, v))
    y = (_hdot(rt, s0, "nt") + _hdot(jnp.where(incl, _hdot(rt, bt, "nt"), 0.0), u)
         + _hdot(jnp.where(incl, _hdot(rt, kt, "nt"), 0.0), v))
    w_end = jnp.exp(c_end - cum)
    s1 = s0 * jnp.exp(c_end) + _hdot(u, b * w_end, "tn") + _hdot(v, k * w_end, "tn")
    return y, s1


def _shift_fwd(z, mu):
    s, c = z.shape
    tc = 256

    def body(z_ref, mu_ref, o_ref):
        zz = z_ref[...]
        row = lax.broadcasted_iota(jnp.int32, zz.shape, 0)
        prev = jnp.where(row == 0, 0.0, pltpu.roll(zz, 1, 0))
        o_ref[...] = zz + (prev - zz) * mu_ref[...]

    return pl.pallas_call(
        body, name="shift_fwd", grid=(c // tc,),
        in_specs=[pl.BlockSpec((s, tc), lambda j: (0, j)), pl.BlockSpec((1, tc), lambda j: (0, j))],
        out_specs=pl.BlockSpec((s, tc), lambda j: (0, j)), out_shape=SDS((s, c), F32),
        compiler_params=pltpu.CompilerParams(dimension_semantics=("parallel",), vmem_limit_bytes=VMEM_LIMIT),
    )(z, mu)


def _shift_bwd(z, mu, dzs):
    s, c = z.shape
    tc = 256

    def body(z_ref, mu_ref, d_ref, dz_ref, dmu_ref):
        zz, d, m = z_ref[...], d_ref[...], mu_ref[...]
        row = lax.broadcasted_iota(jnp.int32, zz.shape, 0)
        prev = jnp.where(row == 0, 0.0, pltpu.roll(zz, 1, 0))
        t = d * m
        nxt = jnp.where(row == s - 1, 0.0, pltpu.roll(t, s - 1, 0))
        dz_ref[...] = (d - t + nxt).astype(dz_ref.dtype)
        dmu_ref[...] = jnp.sum(d * (prev - zz), axis=0, keepdims=True)

    return pl.pallas_call(
        body, name="shift_bwd", grid=(c // tc,),
        in_specs=[pl.BlockSpec((s, tc), lambda j: (0, j)), pl.BlockSpec((1, tc), lambda j: (0, j)),
                  pl.BlockSpec((s, tc), lambda j: (0, j))],
        out_specs=[pl.BlockSpec((s, tc), lambda j: (0, j)), pl.BlockSpec((1, tc), lambda j: (0, j))],
        out_shape=[SDS((s, c), BF16), SDS((1, c), F32)],
        compiler_params=pltpu.CompilerParams(dimension_semantics=("parallel",), vmem_limit_bytes=VMEM_LIMIT),
    )(z, mu, dzs)


def _wkv_fwd(zs, lw, k2, na, b):
    s = lw.shape[0]
    t, hb = WKV_CHUNK, WKV_HEADS_PER_STEP
    w = hb * HEAD
    nc, ng = s // t, RWKV_HEADS // hb

    def body(r_ref, v_ref, lw_ref, k_ref, a_ref, b_ref, y_ref, s0_ref, state):
        @pl.when(pl.program_id(1) == 0)
        def _():
            state[...] = jnp.zeros_like(state)

        for h in range(hb):
            sl = slice(h * HEAD, (h + 1) * HEAD)
            s0 = state[h]
            s0_ref[0, h] = s0
            y, s1 = _wkv_chunk(s0, r_ref[:, sl], lw_ref[:, sl], k_ref[:, sl], v_ref[:, sl], a_ref[:, sl], b_ref[:, sl])
            y_ref[:, sl] = y
            state[h] = s1

    def col(off):
        return pl.BlockSpec((t, w), functools.partial(lambda g, i, off: (i, g + off), off=off))

    return pl.pallas_call(
        body, name="wkv_fwd", grid=(ng, nc),
        in_specs=[col(0), col(2 * ng), col(0), col(0), col(0), col(0)],
        out_specs=[col(0), pl.BlockSpec((1, hb, HEAD, HEAD), lambda g, i: (i, g, 0, 0))],
        out_shape=[SDS((s, RWKV_DIM), F32), SDS((nc, RWKV_HEADS, HEAD, HEAD), F32)],
        scratch_shapes=[pltpu.VMEM((hb, HEAD, HEAD), F32)],
        compiler_params=pltpu.CompilerParams(dimension_semantics=("parallel", "arbitrary"), vmem_limit_bytes=VMEM_LIMIT),
    )(zs, zs, lw, k2, na, b)


def _wkv_bwd(zs, lw, k2, na, b, s0s, dy):
    s = lw.shape[0]
    t, hb = WKV_CHUNK, WKV_HEADS_PER_STEP
    w = hb * HEAD
    nc, ng = s // t, RWKV_HEADS // hb

    def body(r_ref, v_ref, lw_ref, k_ref, a_ref, b_ref, s0_ref, dy_ref, dr_ref, dlw_ref, dk_ref, dv_ref, da_ref, db_ref, dstate):
        @pl.when(pl.program_id(1) == 0)
        def _():
            dstate[...] = jnp.zeros_like(dstate)

        for h in range(hb):
            sl = slice(h * HEAD, (h + 1) * HEAD)
            _, vjp = jax.vjp(_wkv_chunk, s0_ref[0, h], r_ref[:, sl], lw_ref[:, sl], k_ref[:, sl], v_ref[:, sl],
                             a_ref[:, sl], b_ref[:, sl])
            ds0, dr, dlw, dk, dv, da, db = vjp((dy_ref[:, sl], dstate[h]))
            dstate[h] = ds0
            dr_ref[:, sl], dlw_ref[:, sl], dk_ref[:, sl] = dr, dlw, dk
            dv_ref[:, sl], da_ref[:, sl], db_ref[:, sl] = dv, da, db

    def col(off):
        return pl.BlockSpec((t, w), functools.partial(lambda g, i, off: (nc - 1 - i, g + off), off=off))

    return pl.pallas_call(
        body, name="wkv_bwd", grid=(ng, nc),
        in_specs=[col(0), col(2 * ng), col(0), col(0), col(0), col(0),
                  pl.BlockSpec((1, hb, HEAD, HEAD), lambda g, i: (nc - 1 - i, g, 0, 0)), col(0)],
        out_specs=[col(0)] * 6,
        out_shape=[SDS((s, RWKV_DIM), F32)] * 6,
        scratch_shapes=[pltpu.VMEM((hb, HEAD, HEAD), F32)],
        compiler_params=pltpu.CompilerParams(dimension_semantics=("parallel", "arbitrary"), vmem_limit_bytes=VMEM_LIMIT),
    )(zs, zs, lw, k2, na, b, s0s, dy)


def _attn_fwd(q, k, z_a, g, d):
    s = q.shape[0]
    l = s // d
    nb = l // BAND
    assert nb * BAND == l
    qv, kv, zv = q.reshape(l, d * ATTN_DIM), k.reshape(l, d * ATTN_DIM), z_a.reshape(l, d * ATTN_COLS)

    def body(q_ref, kp_ref, kc_ref, vp_ref, vc_ref, o_ref, l_ref):
        has_prev = pl.program_id(1) > 0
        for h in range(GROUP_DIM // HEAD):
            sl = slice(h * HEAD, (h + 1) * HEAD)
            o, lse = _attn_block(q_ref[:, sl].astype(F32), kp_ref[:, sl].astype(F32), kc_ref[:, sl].astype(F32),
                                 vp_ref[:, sl], vc_ref[:, sl], has_prev)
            o_ref[:, sl] = o
            l_ref[:, sl] = lse

    def spec(per_tok, off, prev):
        def imap(rho, i):
            return (jnp.maximum(i - 1, 0) if prev else i, rho * per_tok + off)
        return pl.BlockSpec((BAND, GROUP_DIM), imap)

    o, lse = pl.pallas_call(
        body, name=f"attn_fwd_d{d}", grid=(d, nb),
        in_specs=[spec(3, g, False), spec(3, g, True), spec(3, g, False), spec(9, 6 + g, True), spec(9, 6 + g, False)],
        out_specs=[spec(1, 0, False), spec(1, 0, False)],
        out_shape=[SDS((l, d * GROUP_DIM), F32), SDS((l, d * GROUP_DIM), F32)],
        compiler_params=pltpu.CompilerParams(dimension_semantics=("parallel", "arbitrary"), vmem_limit_bytes=VMEM_LIMIT),
    )(qv, kv, kv, zv, zv)
    return o.reshape(s, GROUP_DIM), lse.reshape(s, GROUP_DIM)


def _attn_bwd(q, k, z_a, g, d, do, dlse):
    s = q.shape[0]
    l = s // d
    nb = l // BAND
    qv, kv, zv = q.reshape(l, d * ATTN_DIM), k.reshape(l, d * ATTN_DIM), z_a.reshape(l, d * ATTN_COLS)
    dov, dlv = do.reshape(l, d * GROUP_DIM), dlse.reshape(l, d * GROUP_DIM)

    def body(q_ref, kp_ref, kc_ref, vp_ref, vc_ref, do_ref, dl_ref, dq_ref, dk_ref, dv_ref, ck, cv):
        step = pl.program_id(1)
        has_prev = step < nb - 1

        @pl.when(step == 0)
        def _():
            ck[...] = jnp.zeros_like(ck)
            cv[...] = jnp.zeros_like(cv)

        for h in range(GROUP_DIM // HEAD):
            sl = slice(h * HEAD, (h + 1) * HEAD)
            _, vjp = jax.vjp(functools.partial(_attn_block, has_prev=has_prev), q_ref[:, sl].astype(F32),
                             kp_ref[:, sl].astype(F32), kc_ref[:, sl].astype(F32), vp_ref[:, sl], vc_ref[:, sl])
            dq, dkp, dkc, dvp, dvc = vjp((do_ref[:, sl], dl_ref[:, sl]))
            dq_ref[:, sl] = dq
            dk_ref[:, sl] = dkc + ck[:, sl]
            dv_ref[:, sl] = dvc + cv[:, sl]
            ck[:, sl] = dkp
            cv[:, sl] = dvp

    def spec(per_tok, off, prev):
        def imap(rho, i):
            blk = nb - 1 - i
            return (jnp.maximum(blk - 1, 0) if prev else blk, rho * per_tok + off)
        return pl.BlockSpec((BAND, GROUP_DIM), imap)

    dq, dk, dv = pl.pallas_call(
        body, name=f"attn_bwd_d{d}", grid=(d, nb),
        in_specs=[spec(3, g, False), spec(3, g, True), spec(3, g, False), spec(9, 6 + g, True), spec(9, 6 + g, False),
                  spec(1, 0, False), spec(1, 0, False)],
        out_specs=[spec(1, 0, False)] * 3,
        out_shape=[SDS((l, d * GROUP_DIM), F32)] * 3,
        scratch_shapes=[pltpu.VMEM((BAND, GROUP_DIM), F32), pltpu.VMEM((BAND, GROUP_DIM), F32)],
        compiler_params=pltpu.CompilerParams(dimension_semantics=("parallel", "arbitrary"), vmem_limit_bytes=VMEM_LIMIT),
    )(qv, kv, kv, zv, zv, dov, dlv)
    return dq.reshape(s, GROUP_DIM), dk.reshape(s, GROUP_DIM), dv.reshape(s, GROUP_DIM)


def _coords():
    return lax.axis_index("x"), lax.axis_index("y"), lax.axis_index("c")


_CHIP_FLIPS = ((1, 0), (0, 1), (1, 1))


def _flip(v, f):
    return 1 - v if f else v


def _gather_weights(shard):
    rows = shard.shape[0]
    half = rows // 2

    def body(src, out, send_sems, recv_sems, local_sem):
        x, y, c = _coords()
        me = 2 * x + y
        mine = pl.ds(pl.multiple_of(c * half, 16), half)
        local = pltpu.make_async_copy(src, out.at[me], local_sem)
        local.start()

        def chip_of(f):
            return _flip(x, f[0]), _flip(y, f[1])

        def over_ici(kk):
            px, py = chip_of(_CHIP_FLIPS[kk])
            return pltpu.make_async_remote_copy(src_ref=src.at[mine], dst_ref=out.at[me, mine], send_sem=send_sems.at[kk],
                                                recv_sem=recv_sems.at[kk], device_id=(px, py, c), device_id_type=MESH)

        def landed(kk):
            px, py = chip_of(_CHIP_FLIPS[kk])
            there = out.at[2 * px + py, mine]
            return pltpu.make_async_remote_copy(src_ref=there, dst_ref=there, send_sem=send_sems.at[kk],
                                                recv_sem=recv_sems.at[kk], device_id=(px, py, c), device_id_type=MESH)

        def passed_on(kk, sent_by_me):
            px, py = chip_of(_CHIP_FLIPS[kk])
            part = mine if sent_by_me else pl.ds(pl.multiple_of((1 - c) * half, 16), half)
            there = out.at[2 * px + py, part]
            return pltpu.make_async_remote_copy(src_ref=there, dst_ref=there, send_sem=send_sems.at[3 + kk],
                                                recv_sem=recv_sems.at[3 + kk], device_id=(x, y, 1 - c), device_id_type=MESH)

        sends = [over_ici(kk) for kk in range(3)]
        for cp in sends:
            cp.start()
        for kk in range(3):
            landed(kk).wait_recv()
            fwd = passed_on(kk, True)
            fwd.start()
            sends.append(fwd)
        for kk in range(3):
            passed_on(kk, False).wait_recv()
        for cp in sends:
            cp.wait_send()
        local.wait()

    return pl.pallas_call(
        body, name="gather_weights",
        in_specs=[pl.BlockSpec(memory_space=pl.ANY)], out_specs=pl.BlockSpec(memory_space=pl.ANY),
        out_shape=SDS((N_CHIPS, rows, PACK_COLS), shard.dtype),
        scratch_shapes=[pltpu.SemaphoreType.DMA((6,)), pltpu.SemaphoreType.DMA((6,)), pltpu.SemaphoreType.DMA],
    )(shard)


def _swap_with_sibling(g):
    n, rows, cols = g.shape
    half = rows // 2

    def body(src, out, send_sem, recv_sem):
        x, y, c = _coords()
        theirs = pl.ds(pl.multiple_of((1 - c) * half, 8), half)
        cp = pltpu.make_async_remote_copy(src_ref=src.at[:, theirs], dst_ref=out, send_sem=send_sem, recv_sem=recv_sem,
                                          device_id=(x, y, 1 - c), device_id_type=MESH)
        cp.start()
        cp.wait()

    return pl.pallas_call(
        body, name="grad_swap_sibling",
        in_specs=[pl.BlockSpec(memory_space=pl.ANY)], out_specs=pl.BlockSpec(memory_space=pl.ANY),
        out_shape=SDS((n, half, cols), g.dtype),
        scratch_shapes=[pltpu.SemaphoreType.DMA, pltpu.SemaphoreType.DMA],
    )(g)


def _send_to_chips(a):
    n, half, cols = a.shape

    def body(src, out, send_sems, recv_sems):
        x, y, c = _coords()
        me = 2 * x + y
        sends = []
        for kk, f in enumerate(_CHIP_FLIPS):
            px, py = _flip(x, f[0]), _flip(y, f[1])
            cp = pltpu.make_async_remote_copy(src_ref=src.at[2 * px + py], dst_ref=out.at[kk], send_sem=send_sems.at[kk],
                                              recv_sem=recv_sems.at[kk], device_id=(px, py, c), device_id_type=MESH)
            cp.start()
            sends.append(cp)
        for cp in sends:
            cp.wait()

    return pl.pallas_call(
        body, name="grad_send_chips",
        in_specs=[pl.BlockSpec(memory_space=pl.ANY)], out_specs=pl.BlockSpec(memory_space=pl.ANY),
        out_shape=SDS((3, half, cols), a.dtype),
        scratch_shapes=[pltpu.SemaphoreType.DMA((3,)), pltpu.SemaphoreType.DMA((3,))],
    )(a)


def _join_halves(r):
    half, cols = r.shape

    def body(src, out, send_sem, recv_sem, local_sem):
        x, y, c = _coords()
        mine = pl.ds(pl.multiple_of(c * half, 8), half)
        local = pltpu.make_async_copy(src, out.at[mine], local_sem)
        local.start()
        cp = pltpu.make_async_remote_copy(src_ref=src, dst_ref=out.at[mine], send_sem=send_sem, recv_sem=recv_sem,
                                          device_id=(x, y, 1 - c), device_id_type=MESH)
        cp.start()
        theirs = out.at[pl.ds(pl.multiple_of((1 - c) * half, 8), half)]
        pltpu.make_async_remote_copy(src_ref=theirs, dst_ref=theirs, send_sem=send_sem, recv_sem=recv_sem,
                                     device_id=(x, y, 1 - c), device_id_type=MESH).wait_recv()
        cp.wait_send()
        local.wait()

    return pl.pallas_call(
        body, name="grad_join_halves",
        in_specs=[pl.BlockSpec(memory_space=pl.ANY)], out_specs=pl.BlockSpec(memory_space=pl.ANY),
        out_shape=SDS((2 * half, cols), r.dtype),
        scratch_shapes=[pltpu.SemaphoreType.DMA, pltpu.SemaphoreType.DMA, pltpu.SemaphoreType.DMA],
    )(r)


def _add_pair(g, recv, c_arr):
    n, _, half, cols = g.shape
    tr = _pick(half, (656, 328, 8))

    def body(c_ref, g_ref, r_ref, o_ref):
        o_ref[...] = g_ref[:, 0] + r_ref[...]

    return pl.pallas_call(
        body, name="grad_add_pair",
        grid_spec=pltpu.PrefetchScalarGridSpec(
            num_scalar_prefetch=1, grid=(n, half // tr),
            in_specs=[pl.BlockSpec((1, 1, tr, cols), lambda j, i, c_ref: (j, c_ref[0], i, 0)),
                      pl.BlockSpec((1, tr, cols), lambda j, i, c_ref: (j, i, 0))],
            out_specs=pl.BlockSpec((1, tr, cols), lambda j, i, c_ref: (j, i, 0))),
        out_shape=SDS((n, half, cols), F32),
        compiler_params=pltpu.CompilerParams(dimension_semantics=("parallel", "parallel"), vmem_limit_bytes=VMEM_LIMIT),
    )(c_arr, g, recv)


def _add_chips(a, recv, me_arr):
    n, half, cols = a.shape
    tr = _pick(half, (656, 328, 8))

    def body(me_ref, a_ref, r_ref, o_ref):
        o_ref[...] = ((a_ref[0] + r_ref[0]) + r_ref[1]) + r_ref[2]

    return pl.pallas_call(
        body, name="grad_add_chips",
        grid_spec=pltpu.PrefetchScalarGridSpec(
            num_scalar_prefetch=1, grid=(half // tr,),
            in_specs=[pl.BlockSpec((1, tr, cols), lambda i, me_ref: (me_ref[0], i, 0)),
                      pl.BlockSpec((3, tr, cols), lambda i, me_ref: (0, i, 0))],
            out_specs=pl.BlockSpec((tr, cols), lambda i, me_ref: (i, 0))),
        out_shape=SDS((half, cols), F32),
        compiler_params=pltpu.CompilerParams(dimension_semantics=("parallel",), vmem_limit_bytes=VMEM_LIMIT),
    )(me_arr, a, recv)


def _all_reduce_small(buf):
    rows, cols = buf.shape

    def body(x_ref, o_ref, gath, send_sems, recv_sems):
        x, y, c = _coords()
        me = 4 * x + 2 * y + c
        gath[me] = x_ref[...]
        sends = []
        for kk in range(1, 8):
            f = (kk >> 2) & 1, (kk >> 1) & 1, kk & 1
            px, py, pc = _flip(x, f[0]), _flip(y, f[1]), _flip(c, f[2])
            cp = pltpu.make_async_remote_copy(src_ref=x_ref, dst_ref=gath.at[me], send_sem=send_sems.at[kk - 1],
                                              recv_sem=recv_sems.at[kk - 1], device_id=(px, py, pc), device_id_type=MESH)
            cp.start()
            sends.append(cp)
        for kk in range(1, 8):
            f = (kk >> 2) & 1, (kk >> 1) & 1, kk & 1
            px, py, pc = _flip(x, f[0]), _flip(y, f[1]), _flip(c, f[2])
            there = gath.at[4 * px + 2 * py + pc]
            pltpu.make_async_remote_copy(src_ref=there, dst_ref=there, send_sem=send_sems.at[kk - 1],
                                         recv_sem=recv_sems.at[kk - 1], device_id=(px, py, pc), device_id_type=MESH).wait_recv()
        for cp in sends:
            cp.wait_send()
        acc = gath[0]
        for j in range(1, 8):
            acc = acc + gath[j]
        o_ref[...] = acc

    return pl.pallas_call(
        body, name="all_reduce_small",
        in_specs=[pl.BlockSpec(memory_space=pltpu.VMEM)], out_specs=pl.BlockSpec(memory_space=pltpu.VMEM),
        out_shape=SDS((rows, cols), F32),
        scratch_shapes=[pltpu.VMEM((8, rows, cols), F32), pltpu.SemaphoreType.DMA((7,)), pltpu.SemaphoreType.DMA((7,))],
    )(buf)


def _adamw_rows(w, g, m, v):
    m = ADAM_B1 * m + (1.0 - ADAM_B1) * g
    v = ADAM_B2 * v + (1.0 - ADAM_B2) * jnp.square(g)
    m_hat = m / (1.0 - ADAM_B1 ** ADAM_STEP)
    v_hat = v / (1.0 - ADAM_B2 ** ADAM_STEP)
    return -ADAM_LR * (m_hat / (jnp.sqrt(v_hat) + ADAM_EPS) + ADAM_WD * w), m, v


def _adamw(w, g, m, v, name):
    rows, cols = w.shape
    tm = _pick(rows, (256, 128, 64, 16, 8))
    return _rows_call(_adamw_rows, [(t, 0, cols) for t in (w, g, m, v)], [], [(cols, F32)] * 3, tm=tm, name=name)


def _pack_big(parts):
    return jnp.concatenate([parts[n].reshape(parts[n].shape[:-2] + (-1, PACK_COLS)) for n, _, _ in BIG], axis=-2)


def _unpack_big(buf):
    out, off = {}, 0
    for n, shp, _ in BIG:
        r = shp[0] * shp[1] // PACK_COLS
        out[n] = buf[..., off:off + r, :].reshape(buf.shape[:-2] + shp)
        off += r
    return out


def _pack_small(parts):
    flat = jnp.concatenate([parts[n].reshape(-1) for n, _ in SMALL])
    return jnp.pad(flat, (0, SMALL_ROWS * PACK_COLS - flat.shape[0])).reshape(SMALL_ROWS, PACK_COLS)


def _unpack_small(buf, shapes):
    flat, out, off = buf.reshape(-1), {}, 0
    for n, sz in SMALL:
        out[n] = flat[off:off + sz].reshape(shapes[n])
        off += sz
    return out


def _whole(blocks, how):
    n, r, c = blocks.shape
    if how == "row":
        return blocks.reshape(n * r, c)
    return blocks.transpose(1, 0, 2).reshape(r, n * c)


def _split(whole, how):
    if how == "row":
        return whole.reshape(N_CHIPS, whole.shape[0] // N_CHIPS, whole.shape[1])
    r, c = whole.shape
    return whole.reshape(r, N_CHIPS, c // N_CHIPS).transpose(1, 0, 2)


def _ffn_fwd(x, gain, wg, wu, wd, tag):
    h = _rows_call(_rms, [(x, 0, D_MODEL)], [gain], [(D_MODEL, BF16)], tm=256, name=f"{tag}_norm")[0]
    gate = _mm(h, wg, name=f"{tag}_gate")
    up = _mm(h, wu, name=f"{tag}_up")
    nblk, s, f = gate.shape
    act = _rows_call(_swiglu_act, [(gate.reshape(nblk * s, f), 0, f), (up.reshape(nblk * s, f), 0, f)], [], [(f, BF16)],
                     tm=512, name=f"{tag}_act")[0].reshape(nblk, s, f)
    x_new = _mm(act, wd, sum_blocks=True, res=x, alpha=0.5, name=f"{tag}_down")
    return x_new, (x, h, gate, up, act)


def _ffn_bwd(dx_new, saved, gain, wg, wu, wd, tag):
    x, h, gate, up, act = saved
    nblk, s, f = gate.shape
    d_wd = _mm(act, dx_new, ta=True, alpha=0.5, name=f"{tag}_down_dw")
    dact = _mm(dx_new, wd, tb=True, alpha=0.5, name=f"{tag}_down_dx")

    def act_bwd(gt, ut, ct):
        _, vjp = jax.vjp(_swiglu_act, gt, ut)
        return vjp(ct)

    dgate, dup = _rows_call(act_bwd, [(t.reshape(nblk * s, f), 0, f) for t in (gate, up, dact)], [], [(f, BF16)] * 2,
                            tm=512, name=f"{tag}_act_bwd")
    dgate, dup = dgate.reshape(nblk, s, f), dup.reshape(nblk, s, f)
    d_wg = _mm(h, dgate, ta=True, name=f"{tag}_gate_dw")
    d_wu = _mm(h, dup, ta=True, name=f"{tag}_up_dw")
    dh = _mm(dgate, wg, tb=True, sum_blocks=True, name=f"{tag}_gate_dx")
    dh = _mm(dup, wu, tb=True, sum_blocks=True, res=dh, name=f"{tag}_up_dx")
    dx, dgain = _norm_bwd(x, gain, dh, dx_new, f"{tag}_norm_bwd")
    return dx, dgain, d_wg, d_wu, d_wd


def _norm_bwd(x, gain, dh, dres, name):
    def f(xt, dht, drt, gt):
        _, vjp = jax.vjp(_rms, xt, gt)
        dxt, dgt = vjp(dht)
        return dxt + drt, dgt

    return _rows_call(f, [(x, 0, D_MODEL), (dh, 0, D_MODEL), (dres, 0, D_MODEL)], [gain], [(D_MODEL, F32)],
                      [(1, D_MODEL)], tm=256, name=name)


def kernel(x, p, positions, ffn1_norm, ffn1_w_gate, ffn1_w_up, ffn1_w_down, mix_norm, w_in, rwkv_mu, rwkv_w0, rwkv_w2, rwkv_a0, rwkv_a2, rwkv_g2, rwkv_k_k, rwkv_k_a, rwkv_r_k, rwkv_gn_w, rwkv_gn_b, q_norm, k_norm, w_br_rwkv, w_br_attn, w_out, ffn2_norm, ffn2_w_gate, ffn2_w_up, ffn2_w_down, ple_norm, ple_w_gate, ple_w_proj, loss_target, m_ffn1_norm, m_ffn1_w_gate, m_ffn1_w_up, m_ffn1_w_down, m_mix_norm, m_w_in, m_rwkv_mu, m_rwkv_w0, m_rwkv_w2, m_rwkv_a0, m_rwkv_a2, m_rwkv_g2, m_rwkv_k_k, m_rwkv_k_a, m_rwkv_r_k, m_rwkv_gn_w, m_rwkv_gn_b, m_q_norm, m_k_norm, m_w_br_rwkv, m_w_br_attn, m_w_out, m_ffn2_norm, m_ffn2_w_gate, m_ffn2_w_up, m_ffn2_w_down, m_ple_norm, m_ple_w_gate, m_ple_w_proj, v_ffn1_norm, v_ffn1_w_gate, v_ffn1_w_up, v_ffn1_w_down, v_mix_norm, v_w_in, v_rwkv_mu, v_rwkv_w0, v_rwkv_w2, v_rwkv_a0, v_rwkv_a2, v_rwkv_g2, v_rwkv_k_k, v_rwkv_k_a, v_rwkv_r_k, v_rwkv_gn_w, v_rwkv_gn_b, v_q_norm, v_k_norm, v_w_br_rwkv, v_w_br_attn, v_w_out, v_ffn2_norm, v_ffn2_w_gate, v_ffn2_w_up, v_ffn2_w_down, v_ple_norm, v_ple_w_gate, v_ple_w_proj):
    args = dict(locals())
    wts = {n: args[n] for n in WEIGHTS}
    mom_m = {n: args["m_" + n] for n in WEIGHTS}
    mom_v = {n: args["v_" + n] for n in WEIGHTS}
    x0, tgt = x[0], loss_target[0]
    s = x0.shape[0]
    p_tok = p[0, 0]

    gathered = _gather_weights(_pack_big({n: wts[n][0] for n, _, _ in BIG}).astype(BF16))
    wb = _unpack_big(gathered)
    w_in_all = _whole(wb["w_in"], "col")
    w_in_r, w_in_a, w_in_g = w_in_all[:, :RWKV_COLS], w_in_all[:, RWKV_COLS:RWKV_COLS + ATTN_COLS], w_in_all[:, RWKV_COLS + ATTN_COLS:]
    w2, a2, g2 = (_whole(wb[n], "col") for n in ("rwkv_w2", "rwkv_a2", "rwkv_g2"))
    w_brr, w_bra, w_pp = (_whole(wb[n], "col") for n in ("w_br_rwkv", "w_br_attn", "ple_w_proj"))
    w_o, w_pg = _whole(wb["w_out"], "row"), _whole(wb["ple_w_gate"], "row")
    vec = {n: wts[n].reshape(1, -1) for n, _ in SMALL}

    inv_freq = 1.0 / (ROPE_THETA ** (jnp.arange(0, HEAD, 2, dtype=F32) / HEAD))
    ang = positions[0].astype(F32)[:, None] * inv_freq
    cos, sin = jnp.cos(ang), jnp.sin(ang)
    cos2, sin2 = jnp.concatenate([cos, cos], axis=1), jnp.concatenate([-sin, sin], axis=1)

    x1, ffn1_saved = _ffn_fwd(x0, vec["ffn1_norm"], wb["ffn1_w_gate"], wb["ffn1_w_up"], wb["ffn1_w_down"], "ffn1")
    h = _rows_call(_rms, [(x1, 0, D_MODEL)], [vec["mix_norm"]], [(D_MODEL, BF16)], tm=256, name="mix_norm")[0]
    z_r = _mm(h, w_in_r, name="in_rwkv")
    z_a = _mm(h, w_in_a, name="in_attn")
    z_g = _mm(h, w_in_g, name="in_gate")

    zs = _shift_fwd(z_r, vec["rwkv_mu"])
    pre_params = [vec["rwkv_w0"], w2, vec["rwkv_a0"], a2, g2, vec["rwkv_k_k"], vec["rwkv_k_a"]]
    def pre_fwd(*t):
        res = _rwkv_pre(*t)
        return res[1], res[2], res[4], res[5], res[6]

    lw, k2, na, kb, gate_r = _rows_call(pre_fwd, [(zs, 0, RWKV_COLS)], pre_params, [(RWKV_DIM, F32)] * 5, tm=256, name="rwkv_pre")
    y_scan, s0s = _wkv_fwd(zs, lw, k2, na, kb)
    post_params = [vec["rwkv_gn_w"], vec["rwkv_gn_b"], vec["rwkv_r_k"]]
    post_rows = [(y_scan, 0, RWKV_DIM), (zs, 0, RWKV_DIM), (k2, 0, RWKV_DIM), (zs, 2, RWKV_DIM), (gate_r, 0, RWKV_DIM)]
    y_rwkv = _rows_call(_rwkv_post, post_rows, post_params, [(RWKV_DIM, BF16)], tm=256, name="rwkv_post")[0]

    def qk_fwd(qt, kt, ct, st, qg, kg):
        return _norm_rope(qt, qg, ct, st), _norm_rope(kt, kg, ct, st)

    qk_rows = [(z_a, 0, ATTN_DIM), (z_a, 1, ATTN_DIM), (cos2, 0, HEAD), (sin2, 0, HEAD)]
    q_rot, k_rot = _rows_call(qk_fwd, qk_rows, [vec["q_norm"], vec["k_norm"]], [(ATTN_DIM, BF16)] * 2, tm=256, name="attn_pre")
    outs, lses = zip(*[_attn_fwd(q_rot, k_rot, z_a, g, d) for g, d in enumerate(ATTN_DILATIONS)])
    comb_rows = [(t, 0, GROUP_DIM) for t in outs + lses]
    y_attn = _rows_call(_attn_combine, comb_rows, [], [(GROUP_DIM, BF16)], tm=256, name="attn_combine")[0]

    br = _mm(y_rwkv, w_brr, name="branch_rwkv")
    ba = _mm(y_attn, w_bra, name="branch_attn")
    merge_rows = [(z_g, 0, D_MODEL), (z_g, 1, D_MODEL), (br, 0, D_MODEL), (ba, 0, D_MODEL)]
    merged = _rows_call(_merge, merge_rows, [], [(D_MODEL, BF16)], tm=256, name="merge")[0]
    x2 = _mm(merged, w_o, res=x1, name="out_proj")
    x3, ffn2_saved = _ffn_fwd(x2, vec["ffn2_norm"], wb["ffn2_w_gate"], wb["ffn2_w_up"], wb["ffn2_w_down"], "ffn2")
    hp = _rows_call(_rms, [(x3, 0, D_MODEL)], [vec["ple_norm"]], [(D_MODEL, BF16)], tm=256, name="ple_norm")[0]
    pg = _mm(hp, w_pg, name="ple_gate")
    pp = _mm(p_tok, w_pp, name="ple_proj")

    def head(x3t, pgt, ppt, tt):
        sg = _sigmoid(pgt)
        err = x3t + sg * ppt - tt
        dx4 = err * (1.0 / D_MODEL)
        loss = 0.5 * jnp.sum(jnp.mean(err * err, axis=-1, keepdims=True), axis=0, keepdims=True)
        return dx4, dx4 * ppt * sg * (1.0 - sg), dx4 * sg, jnp.broadcast_to(loss, (8, 128))

    head_rows = [(x3, 0, D_MODEL), (pg, 0, D_MODEL), (pp, 0, D_MODEL), (tgt, 0, D_MODEL)]
    dx4, dpg, dpp, loss_tile = _rows_call(head, head_rows, [], [(D_MODEL, F32), (D_MODEL, BF16), (D_MODEL, BF16)], [(8, 128)],
                                          tm=256, name="ple_loss")

    gw, gs = {}, {}
    gw["ple_w_proj"] = _mm(p_tok, dpp, ta=True, name="ple_proj_dw")
    gw["ple_w_gate"] = _mm(hp, dpg, ta=True, name="ple_gate_dw")
    dhp = _mm(dpg, w_pg, tb=True, name="ple_gate_dx")
    dx3, gs["ple_norm"] = _norm_bwd(x3, vec["ple_norm"], dhp, dx4, "ple_norm_bwd")
    dx2, gs["ffn2_norm"], gw["ffn2_w_gate"], gw["ffn2_w_up"], gw["ffn2_w_down"] = _ffn_bwd(
        dx3, ffn2_saved, vec["ffn2_norm"], wb["ffn2_w_gate"], wb["ffn2_w_up"], wb["ffn2_w_down"], "ffn2")
    gw["w_out"] = _mm(merged, dx2, ta=True, name="out_proj_dw")
    dmerged = _mm(dx2, w_o, tb=True, name="out_proj_dx")

    def merge_bwd(zgr, zga, brt, bat, ct):
        _, vjp = jax.vjp(_merge, zgr, zga, brt, bat)
        d1, d2, d3, d4 = vjp(ct)
        return jnp.concatenate([d1, d2], axis=1), d3, d4

    dz_g, dbr, dba = _rows_call(merge_bwd, merge_rows + [(dmerged, 0, D_MODEL)], [],
                                [(2 * D_MODEL, BF16), (D_MODEL, BF16), (D_MODEL, BF16)], tm=256, name="merge_bwd")
    gw["w_br_rwkv"] = _mm(y_rwkv, dbr, ta=True, name="branch_rwkv_dw")
    gw["w_br_attn"] = _mm(y_attn, dba, ta=True, name="branch_attn_dw")
    dy_rwkv = _mm(dbr, w_brr, tb=True, name="branch_rwkv_dx")
    dy_attn = _mm(dba, w_bra, tb=True, name="branch_attn_dx")

    def comb_bwd(*t):
        _, vjp = jax.vjp(_attn_combine, *t[:6])
        return vjp(t[6])

    dcomb = _rows_call(comb_bwd, comb_rows + [(dy_attn, 0, GROUP_DIM)], [], [(GROUP_DIM, F32)] * 6, tm=256, name="attn_combine_bwd")
    dqs, dks, dvs = zip(*[_attn_bwd(q_rot, k_rot, z_a, g, d, dcomb[g], dcomb[3 + g]) for g, d in enumerate(ATTN_DILATIONS)])

    def qk_bwd(qt, kt, ct, st, *rest):
        dq = jnp.concatenate(rest[0:3], axis=1)
        dk = jnp.concatenate(rest[3:6], axis=1)
        qg, kg = rest[9], rest[10]
        _, vjp = jax.vjp(lambda a_, b_, c_, d_: qk_fwd(a_, b_, ct, st, c_, d_), qt, kt, qg, kg)
        dqt, dkt, dqg, dkg = vjp((dq, dk))
        return jnp.concatenate((dqt, dkt) + tuple(rest[6:9]), axis=1), dqg, dkg

    dz_a, gs["q_norm"], gs["k_norm"] = _rows_call(
        qk_bwd, qk_rows + [(t, 0, GROUP_DIM) for t in dqs + dks + dvs], [vec["q_norm"], vec["k_norm"]],
        [(ATTN_COLS, BF16)], [(1, HEAD), (1, HEAD)], tm=256, name="attn_pre_bwd")

    def post_bwd(*t):
        _, vjp = jax.vjp(_rwkv_post, *t[:5], *t[6:])
        return vjp(t[5])

    dy_scan, dr_post, dk2_post, dv_post, dgate_r, gs["rwkv_gn_w"], gs["rwkv_gn_b"], gs["rwkv_r_k"] = _rows_call(
        post_bwd, post_rows + [(dy_rwkv, 0, RWKV_DIM)], post_params, [(RWKV_DIM, F32)] * 5, [(1, RWKV_DIM)] * 3,
        tm=256, name="rwkv_post_bwd")
    dr_s, dlw, dk2_s, dv_s, dna, dkb = _wkv_bwd(zs, lw, k2, na, kb, s0s, dy_scan)

    def pre_bwd(zt, c_r1, c_r2, c_lw, c_k1, c_k2, c_v1, c_v2, c_a, c_b, c_g, *params):
        _, vjp = jax.vjp(_rwkv_pre, zt, *params)
        return vjp((c_r1 + c_r2, c_lw, c_k1 + c_k2, c_v1 + c_v2, c_a, c_b, c_g))

    pre_cts = [dr_s, dr_post, dlw, dk2_s, dk2_post, dv_s, dv_post, dna, dkb, dgate_r]
    dzs, gs["rwkv_w0"], g_w2, gs["rwkv_a0"], g_a2, g_g2, gs["rwkv_k_k"], gs["rwkv_k_a"] = _rows_call(
        pre_bwd, [(zs, 0, RWKV_COLS)] + [(t, 0, RWKV_DIM) for t in pre_cts], pre_params, [(RWKV_COLS, F32)],
        [q.shape for q in pre_params], tm=256, name="rwkv_pre_bwd")
    dz_r, gs["rwkv_mu"] = _shift_bwd(z_r, vec["rwkv_mu"], dzs)

    g_w_in = jnp.concatenate([_mm(h, dz_r, ta=True, name="in_rwkv_dw"), _mm(h, dz_a, ta=True, name="in_attn_dw"),
                              _mm(h, dz_g, ta=True, name="in_gate_dw")], axis=1)
    dh = _mm(dz_r, w_in_r, tb=True, name="in_rwkv_dx")
    dh = _mm(dz_a, w_in_a, tb=True, res=dh, name="in_attn_dx")
    dh = _mm(dz_g, w_in_g, tb=True, res=dh, name="in_gate_dx")
    dx1, gs["mix_norm"] = _norm_bwd(x1, vec["mix_norm"], dh, dx2, "mix_norm_bwd")
    dx0, gs["ffn1_norm"], gw["ffn1_w_gate"], gw["ffn1_w_up"], gw["ffn1_w_down"] = _ffn_bwd(
        dx1, ffn1_saved, vec["ffn1_norm"], wb["ffn1_w_gate"], wb["ffn1_w_up"], wb["ffn1_w_down"], "ffn1")

    how = {n: hw for n, _, hw in BIG}
    blocks = {n: gw[n] for n in ("ffn1_w_gate", "ffn1_w_up", "ffn1_w_down", "ffn2_w_gate", "ffn2_w_up", "ffn2_w_down")}
    blocks["w_in"] = _split(g_w_in, "col")
    blocks["rwkv_w2"], blocks["rwkv_a2"], blocks["rwkv_g2"] = _split(g_w2, "col"), _split(g_a2, "col"), _split(g_g2, "col")
    for n in ("w_br_rwkv", "w_br_attn", "w_out", "ple_w_gate", "ple_w_proj"):
        blocks[n] = _split(gw[n], how[n])
    packed = _pack_big(blocks)
    rows = packed.shape[1]
    xi, yi, ci = _coords()
    c_arr = jnp.reshape(ci, (1,)).astype(jnp.int32)
    me_arr = jnp.reshape(2 * xi + yi, (1,)).astype(jnp.int32)
    from_sibling = _swap_with_sibling(packed)
    pair = _add_pair(packed.reshape(N_CHIPS, 2, rows // 2, PACK_COLS), from_sibling, c_arr)
    from_chips = _send_to_chips(pair)
    reduced = _join_halves(_add_chips(pair, from_chips, me_arr))
    grad_big = _unpack_big(reduced)

    flat = jnp.concatenate([gs[n].reshape(-1) for n, _ in SMALL] + [loss_tile[0, 0:1]])
    small_buf = jnp.pad(flat, (0, SMALL_ROWS * PACK_COLS - flat.shape[0])).reshape(SMALL_ROWS, PACK_COLS)
    small_sum = _all_reduce_small(small_buf)
    n_small = sum(sz for _, sz in SMALL)
    loss = small_sum.reshape(-1)[n_small]
    grad_small = _unpack_small(small_sum, {n: wts[n].shape for n, _ in SMALL})

    grads, deltas, new_m, new_v = {}, {}, {}, {}
    for n, shp, _ in BIG:
        g2d = grad_big[n]
        d_, m_, v_ = _adamw(wts[n][0], g2d, mom_m[n][0], mom_v[n][0], name=f"adamw_{n}")
        grads[n], deltas[n], new_m[n], new_v[n] = g2d[None], d_[None], m_[None], v_[None]
    d_s, m_s, v_s = _adamw(_pack_small(wts), small_sum, _pack_small(mom_m), _pack_small(mom_v), name="adamw_small")
    shapes = {n: wts[n].shape for n, _ in SMALL}
    d_s, m_s, v_s = _unpack_small(d_s, shapes), _unpack_small(m_s, shapes), _unpack_small(v_s, shapes)
    for n, _ in SMALL:
        grads[n], deltas[n], new_m[n], new_v[n] = grad_small[n], d_s[n], m_s[n], v_s[n]

    return (loss, dx0[None], *[grads[n] for n in WEIGHTS], *[deltas[n] for n in WEIGHTS],
            *[new_m[n] for n in WEIGHTS], *[new_v[n] for n in WEIGHTS])
```

```python
import functools

import jax
import jax.numpy as jnp
from jax import lax
from jax.experimental import pallas as pl
from jax.experimental.pallas import tpu as pltpu

F32, BF16 = jnp.float32, jnp.bfloat16
HI = lax.Precision.HIGHEST
MESH = pl.DeviceIdType.MESH
SDS = jax.ShapeDtypeStruct

D_MODEL = 1024
HEAD = 64
RWKV_HEADS = 8
RWKV_DIM = RWKV_HEADS * HEAD
DECAY_LORA, ICLR_LORA, GATE_LORA = 64, 64, 128
GN_EPS = 64e-5
RMS_EPS = 1e-6
ATTN_DILATIONS = (1, 4, 16)
BAND = 128
ATTN_DIM = 768
GROUP_DIM = 256
ROPE_THETA = 10000.0
NEG_INF = -1e30
RWKV_COLS = 3 * RWKV_DIM + DECAY_LORA + ICLR_LORA + GATE_LORA
ATTN_COLS = 3 * ATTN_DIM
ADAM_LR, ADAM_B1, ADAM_B2, ADAM_EPS, ADAM_WD, ADAM_STEP = 0.001, 0.9, 0.999, 1e-08, 0.01, 10

WKV_CHUNK = 64
WKV_HEADS_PER_STEP = 8
N_CHIPS = 4
PACK_COLS = 1024
VMEM_LIMIT = 48 * 1024 * 1024

BIG = (
    ("ffn1_w_gate", (1024, 704), "col"), ("ffn1_w_up", (1024, 704), "col"), ("ffn1_w_down", (704, 1024), "row"),
    ("w_in", (1024, 1536), "col"), ("rwkv_w2", (64, 128), "col"), ("rwkv_a2", (64, 128), "col"),
    ("rwkv_g2", (128, 128), "col"), ("w_br_rwkv", (512, 256), "col"), ("w_br_attn", (256, 256), "col"),
    ("w_out", (256, 1024), "row"), ("ffn2_w_gate", (1024, 704), "col"), ("ffn2_w_up", (1024, 704), "col"),
    ("ffn2_w_down", (704, 1024), "row"), ("ple_w_gate", (256, 1024), "row"), ("ple_w_proj", (256, 256), "col"),
)
SMALL = (
    ("ffn1_norm", 1024), ("mix_norm", 1024), ("ffn2_norm", 1024), ("ple_norm", 1024), ("rwkv_mu", 1792),
    ("rwkv_w0", 512), ("rwkv_a0", 512), ("rwkv_k_k", 512), ("rwkv_k_a", 512), ("rwkv_r_k", 512),
    ("rwkv_gn_w", 512), ("rwkv_gn_b", 512), ("q_norm", 64), ("k_norm", 64),
)
SMALL_ROWS = 16
WEIGHTS = (
    "ffn1_norm", "ffn1_w_gate", "ffn1_w_up", "ffn1_w_down", "mix_norm", "w_in", "rwkv_mu", "rwkv_w0", "rwkv_w2",
    "rwkv_a0", "rwkv_a2", "rwkv_g2", "rwkv_k_k", "rwkv_k_a", "rwkv_r_k", "rwkv_gn_w", "rwkv_gn_b", "q_norm", "k_norm",
    "w_br_rwkv", "w_br_attn", "w_out", "ffn2_norm", "ffn2_w_gate", "ffn2_w_up", "ffn2_w_down", "ple_norm",
    "ple_w_gate", "ple_w_proj",
)


def _pick(n, cands):
    for c in cands:
        if n % c == 0:
            return c
    return n


def _mm(a, b, *, ta=False, tb=False, sum_blocks=False, out_dtype=F32, res=None, alpha=1.0, name):
    flat = a.ndim == 2 and b.ndim == 2
    a3 = a if a.ndim == 3 else a[None]
    b3 = b if b.ndim == 3 else b[None]
    na, nbb = a3.shape[0], b3.shape[0]
    nblk = max(na, nbb)
    kdim, m = (a3.shape[1], a3.shape[2]) if ta else (a3.shape[2], a3.shape[1])
    n = b3.shape[1] if tb else b3.shape[2]
    assert (b3.shape[2] if tb else b3.shape[1]) == kdim
    tm = _pick(m, (512, 256, 128))
    tn = _pick(n, (1024, 896, 768, 512, 256, 128))
    tk = _pick(kdim, (512, 256, 128))
    nk = kdim // tk

    if sum_blocks:
        grid = (m // tm, n // tn, nblk, nk)

        def ids(i, c, j, k):
            return i, c, j, k
    else:
        grid = (nblk, m // tm, n // tn, nk)

        def ids(j, i, c, k):
            return i, c, j, k

    def amap(*g):
        i, c, j, k = ids(*g)
        jj = j if na > 1 else 0
        return (jj, k, i) if ta else (jj, i, k)

    def bmap(*g):
        i, c, j, k = ids(*g)
        jj = j if nbb > 1 else 0
        return (jj, c, k) if tb else (jj, k, c)

    if sum_blocks:
        oshape, oblk = (m, n), (tm, tn)

        def omap(*g):
            i, c, j, k = ids(*g)
            return i, c
    else:
        oshape, oblk = (nblk, m, n), (1, tm, tn)

        def omap(*g):
            i, c, j, k = ids(*g)
            return j, i, c

    dn = (((0 if ta else 1,), (1 if tb else 0,)), ((), ()))
    has_res = res is not None

    def body(*refs):
        if has_res:
            a_ref, b_ref, r_ref, o_ref, acc = refs
        else:
            a_ref, b_ref, o_ref, acc = refs
        k = pl.program_id(3)
        if sum_blocks:
            j = pl.program_id(2)
            first = jnp.logical_and(j == 0, k == 0)
            last = jnp.logical_and(j == nblk - 1, k == nk - 1)
        else:
            first, last = k == 0, k == nk - 1

        @pl.when(first)
        def _():
            acc[...] = jnp.zeros_like(acc)

        acc[...] += lax.dot_general(a_ref[0].astype(BF16), b_ref[0].astype(BF16), dn, preferred_element_type=F32)

        @pl.when(last)
        def _():
            v = acc[...]
            if alpha != 1.0:
                v = v * alpha
            if has_res:
                v = v + r_ref[...].reshape(v.shape).astype(F32)
            o_ref[...] = v.reshape(o_ref.shape).astype(o_ref.dtype)

    in_specs = [pl.BlockSpec((1, tk, tm) if ta else (1, tm, tk), amap), pl.BlockSpec((1, tn, tk) if tb else (1, tk, tn), bmap)]
    args = [a3, b3]
    if has_res:
        res3 = res if (sum_blocks or res.ndim == 3) else res[None]
        in_specs.append(pl.BlockSpec(oblk, omap))
        args.append(res3)
    out = pl.pallas_call(
        body,
        name=name,
        grid=grid,
        in_specs=in_specs,
        out_specs=pl.BlockSpec(oblk, omap),
        out_shape=SDS(oshape, out_dtype),
        scratch_shapes=[pltpu.VMEM((tm, tn), F32)],
        compiler_params=pltpu.CompilerParams(
            dimension_semantics=("parallel", "parallel", "arbitrary", "arbitrary") if sum_blocks
            else ("parallel", "parallel", "parallel", "arbitrary"),
            vmem_limit_bytes=VMEM_LIMIT),
    )(*args)
    if flat and not sum_blocks:
        out = out[0]
    return out


def _rows_call(f, rows, params, outs, accs=(), *, tm, name):
    s = rows[0][0].shape[0]
    nr, npar, no = len(rows), len(params), len(outs)
    in_specs = [pl.BlockSpec((tm, w), functools.partial(lambda i, cb: (i, cb), cb=cb)) for (_, cb, w) in rows]
    in_specs += [pl.BlockSpec(p.shape, functools.partial(lambda i, nd: (0,) * nd, nd=p.ndim)) for p in params]
    out_shape = [SDS((s, w), dt) for (w, dt) in outs] + [SDS(tuple(sh), F32) for sh in accs]
    out_specs = [pl.BlockSpec((tm, w), lambda i: (i, 0)) for (w, _) in outs]
    out_specs += [pl.BlockSpec(tuple(sh), functools.partial(lambda i, nd: (0,) * nd, nd=len(sh))) for sh in accs]

    def body(*refs):
        rin, pin = refs[:nr], refs[nr:nr + npar]
        oo, ao = refs[nr + npar:nr + npar + no], refs[nr + npar + no:]
        res = f(*[r[...] for r in rin], *[p[...] for p in pin])
        if not isinstance(res, (tuple, list)):
            res = (res,)
        for o_ref, v in zip(oo, res[:no]):
            o_ref[...] = v.astype(o_ref.dtype)
        i = pl.program_id(0)
        for a_ref, v in zip(ao, res[no:]):
            @pl.when(i == 0)
            def _():
                a_ref[...] = jnp.zeros_like(a_ref)

            a_ref[...] += v.reshape(a_ref.shape)

    res = pl.pallas_call(
        body,
        name=name,
        grid=(s // tm,),
        in_specs=in_specs,
        out_specs=out_specs,
        out_shape=out_shape,
        compiler_params=pltpu.CompilerParams(dimension_semantics=("arbitrary",), vmem_limit_bytes=VMEM_LIMIT),
    )(*[r[0] for r in rows], *params)
    return res


def _mmv(a, b, mode):
    ca = 0 if mode[0] == "t" else 1
    cb = 1 if mode[1] == "t" else 0
    return lax.dot_general(a.astype(BF16), b.astype(BF16), (((ca,), (cb,)), ((), ())), preferred_element_type=F32)


@functools.partial(jax.custom_vjp, nondiff_argnums=(2,))
def _bdot(a, b, mode):
    return _mmv(a, b, mode)


def _bdot_fwd(a, b, mode):
    return _mmv(a, b, mode), (a, b)


def _bdot_bwd(mode, saved, g):
    a, b = saved
    if mode == "nn":
        return _mmv(g, b, "nt"), _mmv(a, g, "tn")
    if mode == "nt":
        return _mmv(g, b, "nn"), _mmv(g, a, "tn")
    return _mmv(b, g, "nt"), _mmv(a, g, "nn")


_bdot.defvjp(_bdot_fwd, _bdot_bwd)


def _hdot(a, b, mode="nn", precision=HI):
    ca = 0 if mode[0] == "t" else 1
    cb = 1 if mode[1] == "t" else 0
    return lax.dot_general(a, b, (((ca,), (cb,)), ((), ())), precision=precision, preferred_element_type=F32)


def _wdot(a, b, mode="nn"):
    return _hdot(a, b, mode, lax.Precision.HIGH)


def _segsum(x):
    c = x.shape[-1]
    r = lax.broadcasted_iota(jnp.int32, (c, c), 0) >> 6
    q = lax.broadcasted_iota(jnp.int32, (c, c), 1) >> 6
    return _hdot(x, jnp.where(r == q, 1.0, 0.0).astype(F32))


def _sigmoid(x):
    return jax.nn.sigmoid(x)


def _softplus(x):
    return jnp.maximum(x, 0.0) + jnp.log(1.0 + jnp.exp(-jnp.abs(x)))


def _rms(x, gain):
    return x * lax.rsqrt(jnp.mean(x * x, axis=-1, keepdims=True) + RMS_EPS) * gain


def _swiglu_act(gate, up):
    return gate * _sigmoid(gate) * up


def _rwkv_pre(zs, w0, w2, a0, a2, g2, k_k, k_a):
    r, k, v = zs[:, 0:512], zs[:, 512:1024], zs[:, 1024:1536]
    lora = zs[:, 1536:1792]
    wd, ad, gd = lora[:, 0:64], lora[:, 64:128], lora[:, 128:256]
    w = -_softplus(-(w0 + _bdot(jnp.tanh(wd), w2, "nn"))) - 0.5
    a = _sigmoid(a0 + _bdot(ad, a2, "nn"))
    g = _bdot(_sigmoid(gd), g2, "nn")
    kk = k * k_k
    kk = kk * lax.rsqrt(jnp.maximum(_segsum(kk * kk), 1e-24))
    k2 = k * (1.0 + (a - 1.0) * k_a)
    return r, -jnp.exp(w), k2, v, -kk, kk * a, g


def _rwkv_post(y, r, k2, v, g, gn_w, gn_b, r_k):
    mean = _segsum(y) * (1.0 / HEAD)
    yc = y - mean
    var = _segsum(yc * yc) * (1.0 / HEAD)
    yn = yc * lax.rsqrt(var + GN_EPS) * gn_w + gn_b
    bonus = _segsum(r * k2 * r_k) * v
    return (yn + bonus) * g


def _swap_halves(x):
    lane = lax.broadcasted_iota(jnp.int32, x.shape, 1)
    return jnp.where((lane & 32) == 0, jnp.roll(x, -32, axis=1), jnp.roll(x, 32, axis=1))


def _norm_rope(x, gain, cos, sin):
    heads = x.shape[1] // HEAD
    def rep(t):
        return jnp.concatenate([t] * heads, axis=1)

    xn = x * lax.rsqrt(_segsum(x * x) * (1.0 / HEAD) + RMS_EPS) * rep(gain)
    return xn * rep(cos) + _swap_halves(xn) * rep(sin)


def _attn_combine(o0, o1, o2, l0, l1, l2):
    m = jnp.maximum(jnp.maximum(l0, l1), l2)
    e0, e1, e2 = jnp.exp(l0 - m), jnp.exp(l1 - m), jnp.exp(l2 - m)
    return (e0 * o0 + e1 * o1 + e2 * o2) / (e0 + e1 + e2)


def _merge(zgr, zga, br, ba):
    return _sigmoid(zgr) * br + _sigmoid(zga) * ba


def _attn_block(q, kp, kc, vp, vc, has_prev):
    iq = lax.broadcasted_iota(jnp.int32, (BAND, BAND), 0)
    ik = lax.broadcasted_iota(jnp.int32, (BAND, BAND), 1)
    s_c = jnp.where(iq >= ik, _bdot(q, kc, "nt") * (HEAD ** -0.5), NEG_INF)
    s_p = jnp.where(jnp.logical_and(iq <= ik, has_prev), _bdot(q, kp, "nt") * (HEAD ** -0.5), NEG_INF)
    m = lax.stop_gradient(jnp.maximum(jnp.max(s_c, axis=-1, keepdims=True), jnp.max(s_p, axis=-1, keepdims=True)))
    e_c, e_p = jnp.exp(s_c - m), jnp.exp(s_p - m)
    l = jnp.sum(e_c, axis=-1, keepdims=True) + jnp.sum(e_p, axis=-1, keepdims=True)
    o = (_bdot(e_c, vc, "nn") + _bdot(e_p, vp, "nn")) / l
    return o, jnp.broadcast_to(m + jnp.log(l), o.shape)


def _bdotb(a, b, mode="nn", precision=lax.Precision.HIGH):
    if mode[0] == "t":
        a = jnp.swapaxes(a, 1, 2)
    cb = 2 if mode[1] == "t" else 1
    return lax.dot_general(a, b, (((2,), (cb,)), ((0,), (0,))), precision=precision, preferred_element_type=F32)


def _tri_inv(a):
    t = a.shape[-1]
    row = lax.broadcasted_iota(jnp.int32, (1, t, t), 1)
    col = lax.broadcasted_iota(jnp.int32, (1, t, t), 2)
    x = jnp.where(row == col, 1.0, 0.0).astype(F32) + jnp.where(jnp.logical_and(row == col + 1, (row & 1) == 1), a, 0.0)
    sh = 1
    while (1 << sh) < t:
        m = jnp.logical_and((row >> sh) == (col >> sh) + 1, (row >> (sh + 1)) == (col >> (sh + 1)))
        x = x + _bdotb(_bdotb(x, jnp.where(m, a, 0.0)), x)
        sh += 1
    return x


def _wkv_chunk(s0, r, lw, k, v, a, b):
    nh, t, _ = r.shape
    row = lax.broadcasted_iota(jnp.int32, (1, t, t), 1)
    col = lax.broadcasted_iota(jnp.int32, (1, t, t), 2)
    incl, strict = row >= col, row > col
    ones = jnp.broadcast_to(jnp.where(incl, 1.0, 0.0).astype(F32), (nh, t, t))
    cum = _bdotb(ones, lw, precision=HI)
    c_end = cum[:, t - 1:t, :]
    e_in, e_ex, e_inv = jnp.exp(cum), jnp.exp(cum - lw), jnp.exp(-cum)
    at, rt, bt, kt = a * e_ex, r * e_in, b * e_inv, k * e_inv
    a_ab = jnp.where(strict, _bdotb(at, bt, "nt"), 0.0)
    a_ak = jnp.where(strict, _bdotb(at, kt, "nt"), 0.0)
    u = _bdotb(_tri_inv(a_ab), _bdotb(at, s0, "nt") + _bdotb(a_ak, v))
    y = (_bdotb(rt, s0, "nt") + _bdotb(jnp.where(incl, _bdotb(rt, bt, "nt"), 0.0), u)
         + _bdotb(jnp.where(incl, _bdotb(rt, kt, "nt"), 0.0), v))
    w_end = jnp.exp(c_end - cum)
    s1 = s0 * jnp.exp(c_end) + _bdotb(u, b * w_end, "tn") + _bdotb(v, k * w_end, "tn")
    return y, s1


def _shift_fwd(z, mu):
    s, c = z.shape
    tc = 256

    def body(z_ref, mu_ref, o_ref):
        zz = z_ref[...]
        row = lax.broadcasted_iota(jnp.int32, zz.shape, 0)
        prev = jnp.where(row == 0, 0.0, pltpu.roll(zz, 1, 0))
        o_ref[...] = zz + (prev - zz) * mu_ref[...]

    return pl.pallas_call(
        body, name="shift_fwd", grid=(c // tc,),
        in_specs=[pl.BlockSpec((s, tc), lambda j: (0, j)), pl.BlockSpec((1, tc), lambda j: (0, j))],
        out_specs=pl.BlockSpec((s, tc), lambda j: (0, j)), out_shape=SDS((s, c), F32),
        compiler_params=pltpu.CompilerParams(dimension_semantics=("parallel",), vmem_limit_bytes=VMEM_LIMIT),
    )(z, mu)


def _shift_bwd(z, mu, dzs):
    s, c = z.shape
    tc = 256

    def body(z_ref, mu_ref, d_ref, dz_ref, dmu_ref):
        zz, d, m = z_ref[...], d_ref[...], mu_ref[...]
        row = lax.broadcasted_iota(jnp.int32, zz.shape, 0)
        prev = jnp.where(row == 0, 0.0, pltpu.roll(zz, 1, 0))
        t = d * m
        nxt = jnp.where(row == s - 1, 0.0, pltpu.roll(t, s - 1, 0))
        dz_ref[...] = (d - t + nxt).astype(dz_ref.dtype)
        dmu_ref[...] = jnp.sum(d * (prev - zz), axis=0, keepdims=True)

    return pl.pallas_call(
        body, name="shift_bwd", grid=(c // tc,),
        in_specs=[pl.BlockSpec((s, tc), lambda j: (0, j)), pl.BlockSpec((1, tc), lambda j: (0, j)),
                  pl.BlockSpec((s, tc), lambda j: (0, j))],
        out_specs=[pl.BlockSpec((s, tc), lambda j: (0, j)), pl.BlockSpec((1, tc), lambda j: (0, j))],
        out_shape=[SDS((s, c), BF16), SDS((1, c), F32)],
        compiler_params=pltpu.CompilerParams(dimension_semantics=("parallel",), vmem_limit_bytes=VMEM_LIMIT),
    )(z, mu, dzs)


def _heads(x, nh):
    return jnp.stack([x[:, h * HEAD:(h + 1) * HEAD] for h in range(nh)], axis=0)


def _unheads(x):
    return jnp.concatenate([x[h] for h in range(x.shape[0])], axis=1)


def _wkv_fwd(zs, lw, k2, na, b):
    s = lw.shape[0]
    t, hb = WKV_CHUNK, WKV_HEADS_PER_STEP
    w = hb * HEAD
    nc, ng = s // t, RWKV_HEADS // hb

    def body(r_ref, v_ref, lw_ref, k_ref, a_ref, b_ref, y_ref, s0_ref, state):
        @pl.when(pl.program_id(1) == 0)
        def _():
            state[...] = jnp.zeros_like(state)

        s0 = state[...]
        s0_ref[0] = s0
        y, s1 = _wkv_chunk(s0, *[_heads(t_ref[...], hb) for t_ref in (r_ref, lw_ref, k_ref, v_ref, a_ref, b_ref)])
        y_ref[...] = _unheads(y)
        state[...] = s1

    def col(off):
        return pl.BlockSpec((t, w), functools.partial(lambda g, i, off: (i, g + off), off=off))

    return pl.pallas_call(
        body, name="wkv_fwd", grid=(ng, nc),
        in_specs=[col(0), col(2 * ng), col(0), col(0), col(0), col(0)],
        out_specs=[col(0), pl.BlockSpec((1, hb, HEAD, HEAD), lambda g, i: (i, g, 0, 0))],
        out_shape=[SDS((s, RWKV_DIM), F32), SDS((nc, RWKV_HEADS, HEAD, HEAD), F32)],
        scratch_shapes=[pltpu.VMEM((hb, HEAD, HEAD), F32)],
        compiler_params=pltpu.CompilerParams(dimension_semantics=("parallel", "arbitrary"), vmem_limit_bytes=VMEM_LIMIT),
    )(zs, zs, lw, k2, na, b)


def _wkv_bwd(zs, lw, k2, na, b, s0s, dy):
    s = lw.shape[0]
    t, hb = WKV_CHUNK, WKV_HEADS_PER_STEP
    w = hb * HEAD
    nc, ng = s // t, RWKV_HEADS // hb

    def body(r_ref, v_ref, lw_ref, k_ref, a_ref, b_ref, s0_ref, dy_ref, dr_ref, dlw_ref, dk_ref, dv_ref, da_ref, db_ref, dstate):
        @pl.when(pl.program_id(1) == 0)
        def _():
            dstate[...] = jnp.zeros_like(dstate)

        _, vjp = jax.vjp(_wkv_chunk, s0_ref[0], *[_heads(t_ref[...], hb) for t_ref in (r_ref, lw_ref, k_ref, v_ref, a_ref, b_ref)])
        grads = vjp((_heads(dy_ref[...], hb), dstate[...]))
        dstate[...] = grads[0]
        for o_ref, gval in zip((dr_ref, dlw_ref, dk_ref, dv_ref, da_ref, db_ref), grads[1:]):
            o_ref[...] = _unheads(gval)

    def col(off):
        return pl.BlockSpec((t, w), functools.partial(lambda g, i, off: (nc - 1 - i, g + off), off=off))

    return pl.pallas_call(
        body, name="wkv_bwd", grid=(ng, nc),
        in_specs=[col(0), col(2 * ng), col(0), col(0), col(0), col(0),
                  pl.BlockSpec((1, hb, HEAD, HEAD), lambda g, i: (nc - 1 - i, g, 0, 0)), col(0)],
        out_specs=[col(0)] * 6,
        out_shape=[SDS((s, RWKV_DIM), F32)] * 6,
        scratch_shapes=[pltpu.VMEM((hb, HEAD, HEAD), F32)],
        compiler_params=pltpu.CompilerParams(dimension_semantics=("parallel", "arbitrary"), vmem_limit_bytes=VMEM_LIMIT),
    )(zs, zs, lw, k2, na, b, s0s, dy)


def _attn_fwd(q, k, z_a, g, d):
    s = q.shape[0]
    l = s // d
    nb = l // BAND
    assert nb * BAND == l
    qv, kv, zv = q.reshape(l, d * ATTN_DIM), k.reshape(l, d * ATTN_DIM), z_a.reshape(l, d * ATTN_COLS)

    def body(q_ref, kp_ref, kc_ref, vp_ref, vc_ref, o_ref, l_ref):
        has_prev = pl.program_id(1) > 0
        for h in range(GROUP_DIM // HEAD):
            sl = slice(h * HEAD, (h + 1) * HEAD)
            o, lse = _attn_block(q_ref[:, sl].astype(F32), kp_ref[:, sl].astype(F32), kc_ref[:, sl].astype(F32),
                                 vp_ref[:, sl], vc_ref[:, sl], has_prev)
            o_ref[:, sl] = o
            l_ref[:, sl] = lse

    def spec(per_tok, off, prev):
        def imap(rho, i):
            return (jnp.maximum(i - 1, 0) if prev else i, rho * per_tok + off)
        return pl.BlockSpec((BAND, GROUP_DIM), imap)

    o, lse = pl.pallas_call(
        body, name=f"attn_fwd_d{d}", grid=(d, nb),
        in_specs=[spec(3, g, False), spec(3, g, True), spec(3, g, False), spec(9, 6 + g, True), spec(9, 6 + g, False)],
        out_specs=[spec(1, 0, False), spec(1, 0, False)],
        out_shape=[SDS((l, d * GROUP_DIM), F32), SDS((l, d * GROUP_DIM), F32)],
        compiler_params=pltpu.CompilerParams(dimension_semantics=("parallel", "arbitrary"), vmem_limit_bytes=VMEM_LIMIT),
    )(qv, kv, kv, zv, zv)
    return o.reshape(s, GROUP_DIM), lse.reshape(s, GROUP_DIM)


def _attn_bwd(q, k, z_a, g, d, do, dlse):
    s = q.shape[0]
    l = s // d
    nb = l // BAND
    qv, kv, zv = q.reshape(l, d * ATTN_DIM), k.reshape(l, d * ATTN_DIM), z_a.reshape(l, d * ATTN_COLS)
    dov, dlv = do.reshape(l, d * GROUP_DIM), dlse.reshape(l, d * GROUP_DIM)

    def body(q_ref, kp_ref, kc_ref, vp_ref, vc_ref, do_ref, dl_ref, dq_ref, dk_ref, dv_ref, ck, cv):
        step = pl.program_id(1)
        has_prev = step < nb - 1

        @pl.when(step == 0)
        def _():
            ck[...] = jnp.zeros_like(ck)
            cv[...] = jnp.zeros_like(cv)

        for h in range(GROUP_DIM // HEAD):
            sl = slice(h * HEAD, (h + 1) * HEAD)
            _, vjp = jax.vjp(functools.partial(_attn_block, has_prev=has_prev), q_ref[:, sl].astype(F32),
                             kp_ref[:, sl].astype(F32), kc_ref[:, sl].astype(F32), vp_ref[:, sl], vc_ref[:, sl])
            dq, dkp, dkc, dvp, dvc = vjp((do_ref[:, sl], dl_ref[:, sl]))
            dq_ref[:, sl] = dq
            dk_ref[:, sl] = dkc + ck[:, sl]
            dv_ref[:, sl] = dvc + cv[:, sl]
            ck[:, sl] = dkp
            cv[:, sl] = dvp

    def spec(per_tok, off, prev):
        def imap(rho, i):
            blk = nb - 1 - i
            return (jnp.maximum(blk - 1, 0) if prev else blk, rho * per_tok + off)
        return pl.BlockSpec((BAND, GROUP_DIM), imap)

    dq, dk, dv = pl.pallas_call(
        body, name=f"attn_bwd_d{d}", grid=(d, nb),
        in_specs=[spec(3, g, False), spec(3, g, True), spec(3, g, False), spec(9, 6 + g, True), spec(9, 6 + g, False),
                  spec(1, 0, False), spec(1, 0, False)],
        out_specs=[spec(1, 0, False)] * 3,
        out_shape=[SDS((l, d * GROUP_DIM), F32)] * 3,
        scratch_shapes=[pltpu.VMEM((BAND, GROUP_DIM), F32), pltpu.VMEM((BAND, GROUP_DIM), F32)],
        compiler_params=pltpu.CompilerParams(dimension_semantics=("parallel", "arbitrary"), vmem_limit_bytes=VMEM_LIMIT),
    )(qv, kv, kv, zv, zv, dov, dlv)
    return dq.reshape(s, GROUP_DIM), dk.reshape(s, GROUP_DIM), dv.reshape(s, GROUP_DIM)


def _coords():
    return lax.axis_index("x"), lax.axis_index("y"), lax.axis_index("c")


_CHIP_FLIPS = ((1, 0), (0, 1), (1, 1))


def _flip(v, f):
    return 1 - v if f else v


def _gather_weights(shard):
    rows = shard.shape[0]
    half = rows // 2

    def body(src, out, send_sems, recv_sems, local_sem):
        x, y, c = _coords()
        me = 2 * x + y
        mine = pl.ds(pl.multiple_of(c * half, 16), half)
        local = pltpu.make_async_copy(src, out.at[me], local_sem)
        local.start()

        def chip_of(f):
            return _flip(x, f[0]), _flip(y, f[1])

        def over_ici(kk):
            px, py = chip_of(_CHIP_FLIPS[kk])
            return pltpu.make_async_remote_copy(src_ref=src.at[mine], dst_ref=out.at[me, mine], send_sem=send_sems.at[kk],
                                                recv_sem=recv_sems.at[kk], device_id=(px, py, c), device_id_type=MESH)

        def landed(kk):
            px, py = chip_of(_CHIP_FLIPS[kk])
            there = out.at[2 * px + py, mine]
            return pltpu.make_async_remote_copy(src_ref=there, dst_ref=there, send_sem=send_sems.at[kk],
                                                recv_sem=recv_sems.at[kk], device_id=(px, py, c), device_id_type=MESH)

        def passed_on(kk, sent_by_me):
            px, py = chip_of(_CHIP_FLIPS[kk])
            part = mine if sent_by_me else pl.ds(pl.multiple_of((1 - c) * half, 16), half)
            there = out.at[2 * px + py, part]
            return pltpu.make_async_remote_copy(src_ref=there, dst_ref=there, send_sem=send_sems.at[3 + kk],
                                                recv_sem=recv_sems.at[3 + kk], device_id=(x, y, 1 - c), device_id_type=MESH)

        sends = [over_ici(kk) for kk in range(3)]
        for cp in sends:
            cp.start()
        for kk in range(3):
            landed(kk).wait_recv()
            fwd = passed_on(kk, True)
            fwd.start()
            sends.append(fwd)
        for kk in range(3):
            passed_on(kk, False).wait_recv()
        for cp in sends:
            cp.wait_send()
        local.wait()

    return pl.pallas_call(
        body, name="gather_weights",
        in_specs=[pl.BlockSpec(memory_space=pl.ANY)], out_specs=pl.BlockSpec(memory_space=pl.ANY),
        out_shape=SDS((N_CHIPS, rows, PACK_COLS), shard.dtype),
        scratch_shapes=[pltpu.SemaphoreType.DMA((6,)), pltpu.SemaphoreType.DMA((6,)), pltpu.SemaphoreType.DMA],
    )(shard)


def _swap_with_sibling(g):
    n, rows, cols = g.shape
    half = rows // 2

    def body(src, out, send_sem, recv_sem):
        x, y, c = _coords()
        theirs = pl.ds(pl.multiple_of((1 - c) * half, 8), half)
        cp = pltpu.make_async_remote_copy(src_ref=src.at[:, theirs], dst_ref=out, send_sem=send_sem, recv_sem=recv_sem,
                                          device_id=(x, y, 1 - c), device_id_type=MESH)
        cp.start()
        cp.wait()

    return pl.pallas_call(
        body, name="grad_swap_sibling",
        in_specs=[pl.BlockSpec(memory_space=pl.ANY)], out_specs=pl.BlockSpec(memory_space=pl.ANY),
        out_shape=SDS((n, half, cols), g.dtype),
        scratch_shapes=[pltpu.SemaphoreType.DMA, pltpu.SemaphoreType.DMA],
    )(g)


def _send_to_chips(a):
    n, half, cols = a.shape

    def body(src, out, send_sems, recv_sems):
        x, y, c = _coords()
        me = 2 * x + y
        sends = []
        for kk, f in enumerate(_CHIP_FLIPS):
            px, py = _flip(x, f[0]), _flip(y, f[1])
            cp = pltpu.make_async_remote_copy(src_ref=src.at[2 * px + py], dst_ref=out.at[kk], send_sem=send_sems.at[kk],
                                              recv_sem=recv_sems.at[kk], device_id=(px, py, c), device_id_type=MESH)
            cp.start()
            sends.append(cp)
        for cp in sends:
            cp.wait()

    return pl.pallas_call(
        body, name="grad_send_chips",
        in_specs=[pl.BlockSpec(memory_space=pl.ANY)], out_specs=pl.BlockSpec(memory_space=pl.ANY),
        out_shape=SDS((3, half, cols), a.dtype),
        scratch_shapes=[pltpu.SemaphoreType.DMA((3,)), pltpu.SemaphoreType.DMA((3,))],
    )(a)


def _join_halves(r):
    half, cols = r.shape

    def body(src, out, send_sem, recv_sem, local_sem):
        x, y, c = _coords()
        mine = pl.ds(pl.multiple_of(c * half, 8), half)
        local = pltpu.make_async_copy(src, out.at[mine], local_sem)
        local.start()
        cp = pltpu.make_async_remote_copy(src_ref=src, dst_ref=out.at[mine], send_sem=send_sem, recv_sem=recv_sem,
                                          device_id=(x, y, 1 - c), device_id_type=MESH)
        cp.start()
        theirs = out.at[pl.ds(pl.multiple_of((1 - c) * half, 8), half)]
        pltpu.make_async_remote_copy(src_ref=theirs, dst_ref=theirs, send_sem=send_sem, recv_sem=recv_sem,
                                     device_id=(x, y, 1 - c), device_id_type=MESH).wait_recv()
        cp.wait_send()
        local.wait()

    return pl.pallas_call(
        body, name="grad_join_halves",
        in_specs=[pl.BlockSpec(memory_space=pl.ANY)], out_specs=pl.BlockSpec(memory_space=pl.ANY),
        out_shape=SDS((2 * half, cols), r.dtype),
        scratch_shapes=[pltpu.SemaphoreType.DMA, pltpu.SemaphoreType.DMA, pltpu.SemaphoreType.DMA],
    )(r)


def _add_pair(g, recv, c_arr):
    n, _, half, cols = g.shape
    tr = _pick(half, (656, 328, 8))

    def body(c_ref, g_ref, r_ref, o_ref, ob_ref):
        v = g_ref[:, 0] + r_ref[...]
        o_ref[...] = v
        ob_ref[...] = v.astype(BF16)

    return pl.pallas_call(
        body, name="grad_add_pair",
        grid_spec=pltpu.PrefetchScalarGridSpec(
            num_scalar_prefetch=1, grid=(n, half // tr),
            in_specs=[pl.BlockSpec((1, 1, tr, cols), lambda j, i, c_ref: (j, c_ref[0], i, 0)),
                      pl.BlockSpec((1, tr, cols), lambda j, i, c_ref: (j, i, 0))],
            out_specs=[pl.BlockSpec((1, tr, cols), lambda j, i, c_ref: (j, i, 0))] * 2),
        out_shape=[SDS((n, half, cols), F32), SDS((n, half, cols), BF16)],
        compiler_params=pltpu.CompilerParams(dimension_semantics=("parallel", "parallel"), vmem_limit_bytes=VMEM_LIMIT),
    )(c_arr, g, recv)


def _add_chips(a, recv, me_arr):
    n, half, cols = a.shape
    tr = _pick(half, (656, 328, 8))

    def body(me_ref, a_ref, r_ref, o_ref):
        o_ref[...] = ((a_ref[0] + r_ref[0].astype(F32)) + r_ref[1].astype(F32)) + r_ref[2].astype(F32)

    return pl.pallas_call(
        body, name="grad_add_chips",
        grid_spec=pltpu.PrefetchScalarGridSpec(
            num_scalar_prefetch=1, grid=(half // tr,),
            in_specs=[pl.BlockSpec((1, tr, cols), lambda i, me_ref: (me_ref[0], i, 0)),
                      pl.BlockSpec((3, tr, cols), lambda i, me_ref: (0, i, 0))],
            out_specs=pl.BlockSpec((tr, cols), lambda i, me_ref: (i, 0))),
        out_shape=SDS((half, cols), F32),
        compiler_params=pltpu.CompilerParams(dimension_semantics=("parallel",), vmem_limit_bytes=VMEM_LIMIT),
    )(me_arr, a, recv)


def _all_reduce_small(buf):
    rows, cols = buf.shape

    def body(x_ref, o_ref, gath, send_sems, recv_sems):
        x, y, c = _coords()
        me = 4 * x + 2 * y + c
        gath[me] = x_ref[...]
        sends = []
        for kk in range(1, 8):
            f = (kk >> 2) & 1, (kk >> 1) & 1, kk & 1
            px, py, pc = _flip(x, f[0]), _flip(y, f[1]), _flip(c, f[2])
            cp = pltpu.make_async_remote_copy(src_ref=x_ref, dst_ref=gath.at[me], send_sem=send_sems.at[kk - 1],
                                              recv_sem=recv_sems.at[kk - 1], device_id=(px, py, pc), device_id_type=MESH)
            cp.start()
            sends.append(cp)
        for kk in range(1, 8):
            f = (kk >> 2) & 1, (kk >> 1) & 1, kk & 1
            px, py, pc = _flip(x, f[0]), _flip(y, f[1]), _flip(c, f[2])
            there = gath.at[4 * px + 2 * py + pc]
            pltpu.make_async_remote_copy(src_ref=there, dst_ref=there, send_sem=send_sems.at[kk - 1],
                                         recv_sem=recv_sems.at[kk - 1], device_id=(px, py, pc), device_id_type=MESH).wait_recv()
        for cp in sends:
            cp.wait_send()
        acc = gath[0]
        for j in range(1, 8):
            acc = acc + gath[j]
        o_ref[...] = acc

    return pl.pallas_call(
        body, name="all_reduce_small",
        in_specs=[pl.BlockSpec(memory_space=pltpu.VMEM)], out_specs=pl.BlockSpec(memory_space=pltpu.VMEM),
        out_shape=SDS((rows, cols), F32),
        scratch_shapes=[pltpu.VMEM((8, rows, cols), F32), pltpu.SemaphoreType.DMA((7,)), pltpu.SemaphoreType.DMA((7,))],
    )(buf)


def _adamw_rows(w, g, m, v):
    m = ADAM_B1 * m + (1.0 - ADAM_B1) * g
    v = ADAM_B2 * v + (1.0 - ADAM_B2) * jnp.square(g)
    m_hat = m / (1.0 - ADAM_B1 ** ADAM_STEP)
    v_hat = v / (1.0 - ADAM_B2 ** ADAM_STEP)
    return -ADAM_LR * (m_hat / (jnp.sqrt(v_hat) + ADAM_EPS) + ADAM_WD * w), m, v


def _adamw(w, g, m, v, name):
    rows, cols = w.shape
    tm = _pick(rows, (256, 128, 64, 16, 8))
    return _rows_call(_adamw_rows, [(t, 0, cols) for t in (w, g, m, v)], [], [(cols, F32)] * 3, tm=tm, name=name)


def _pack_big(parts):
    return jnp.concatenate([parts[n].reshape(parts[n].shape[:-2] + (-1, PACK_COLS)) for n, _, _ in BIG], axis=-2)


def _unpack_big(buf):
    out, off = {}, 0
    for n, shp, _ in BIG:
        r = shp[0] * shp[1] // PACK_COLS
        out[n] = buf[..., off:off + r, :].reshape(buf.shape[:-2] + shp)
        off += r
    return out


def _pack_small(parts):
    flat = jnp.concatenate([parts[n].reshape(-1) for n, _ in SMALL])
    return jnp.pad(flat, (0, SMALL_ROWS * PACK_COLS - flat.shape[0])).reshape(SMALL_ROWS, PACK_COLS)


def _unpack_small(buf, shapes):
    flat, out, off = buf.reshape(-1), {}, 0
    for n, sz in SMALL:
        out[n] = flat[off:off + sz].reshape(shapes[n])
        off += sz
    return out


def _whole(blocks, how):
    n, r, c = blocks.shape
    if how == "row":
        return blocks.reshape(n * r, c)
    return blocks.transpose(1, 0, 2).reshape(r, n * c)


def _split(whole, how):
    if how == "row":
        return whole.reshape(N_CHIPS, whole.shape[0] // N_CHIPS, whole.shape[1])
    r, c = whole.shape
    return whole.reshape(r, N_CHIPS, c // N_CHIPS).transpose(1, 0, 2)


def _ffn_fwd(x, gain, wg, wu, wd, tag):
    h = _rows_call(_rms, [(x, 0, D_MODEL)], [gain], [(D_MODEL, BF16)], tm=256, name=f"{tag}_norm")[0]
    gate = _mm(h, wg, name=f"{tag}_gate")
    up = _mm(h, wu, name=f"{tag}_up")
    nblk, s, f = gate.shape
    act = _rows_call(_swiglu_act, [(gate.reshape(nblk * s, f), 0, f), (up.reshape(nblk * s, f), 0, f)], [], [(f, BF16)],
                     tm=512, name=f"{tag}_act")[0].reshape(nblk, s, f)
    x_new = _mm(act, wd, sum_blocks=True, res=x, alpha=0.5, name=f"{tag}_down")
    return x_new, (x, h, gate, up, act)


def _ffn_bwd(dx_new, saved, gain, wg, wu, wd, tag):
    x, h, gate, up, act = saved
    nblk, s, f = gate.shape
    d_wd = _mm(act, dx_new, ta=True, alpha=0.5, name=f"{tag}_down_dw")
    dact = _mm(dx_new, wd, tb=True, alpha=0.5, name=f"{tag}_down_dx")

    def act_bwd(gt, ut, ct):
        _, vjp = jax.vjp(_swiglu_act, gt, ut)
        return vjp(ct)

    dgate, dup = _rows_call(act_bwd, [(t.reshape(nblk * s, f), 0, f) for t in (gate, up, dact)], [], [(f, BF16)] * 2,
                            tm=512, name=f"{tag}_act_bwd")
    dgate, dup = dgate.reshape(nblk, s, f), dup.reshape(nblk, s, f)
    d_wg = _mm(h, dgate, ta=True, name=f"{tag}_gate_dw")
    d_wu = _mm(h, dup, ta=True, name=f"{tag}_up_dw")
    dh = _mm(dgate, wg, tb=True, sum_blocks=True, name=f"{tag}_gate_dx")
    dh = _mm(dup, wu, tb=True, sum_blocks=True, res=dh, name=f"{tag}_up_dx")
    dx, dgain = _norm_bwd(x, gain, dh, dx_new, f"{tag}_norm_bwd")
    return dx, dgain, d_wg, d_wu, d_wd


def _norm_bwd(x, gain, dh, dres, name):
    def f(xt, dht, drt, gt):
        _, vjp = jax.vjp(_rms, xt, gt)
        dxt, dgt = vjp(dht)
        return dxt + drt, dgt

    return _rows_call(f, [(x, 0, D_MODEL), (dh, 0, D_MODEL), (dres, 0, D_MODEL)], [gain], [(D_MODEL, F32)],
                      [(1, D_MODEL)], tm=256, name=name)


def kernel(x, p, positions, ffn1_norm, ffn1_w_gate, ffn1_w_up, ffn1_w_down, mix_norm, w_in, rwkv_mu, rwkv_w0, rwkv_w2, rwkv_a0, rwkv_a2, rwkv_g2, rwkv_k_k, rwkv_k_a, rwkv_r_k, rwkv_gn_w, rwkv_gn_b, q_norm, k_norm, w_br_rwkv, w_br_attn, w_out, ffn2_norm, ffn2_w_gate, ffn2_w_up, ffn2_w_down, ple_norm, ple_w_gate, ple_w_proj, loss_target, m_ffn1_norm, m_ffn1_w_gate, m_ffn1_w_up, m_ffn1_w_down, m_mix_norm, m_w_in, m_rwkv_mu, m_rwkv_w0, m_rwkv_w2, m_rwkv_a0, m_rwkv_a2, m_rwkv_g2, m_rwkv_k_k, m_rwkv_k_a, m_rwkv_r_k, m_rwkv_gn_w, m_rwkv_gn_b, m_q_norm, m_k_norm, m_w_br_rwkv, m_w_br_attn, m_w_out, m_ffn2_norm, m_ffn2_w_gate, m_ffn2_w_up, m_ffn2_w_down, m_ple_norm, m_ple_w_gate, m_ple_w_proj, v_ffn1_norm, v_ffn1_w_gate, v_ffn1_w_up, v_ffn1_w_down, v_mix_norm, v_w_in, v_rwkv_mu, v_rwkv_w0, v_rwkv_w2, v_rwkv_a0, v_rwkv_a2, v_rwkv_g2, v_rwkv_k_k, v_rwkv_k_a, v_rwkv_r_k, v_rwkv_gn_w, v_rwkv_gn_b, v_q_norm, v_k_norm, v_w_br_rwkv, v_w_br_attn, v_w_out, v_ffn2_norm, v_ffn2_w_gate, v_ffn2_w_up, v_ffn2_w_down, v_ple_norm, v_ple_w_gate, v_ple_w_proj):
    args = dict(locals())
    wts = {n: args[n] for n in WEIGHTS}
    mom_m = {n: args["m_" + n] for n in WEIGHTS}
    mom_v = {n: args["v_" + n] for n in WEIGHTS}
    x0, tgt = x[0], loss_target[0]
    s = x0.shape[0]
    p_tok = p[0, 0]

    gathered = _gather_weights(_pack_big({n: wts[n][0] for n, _, _ in BIG}).astype(BF16))
    wb = _unpack_big(gathered)
    w_in_all = _whole(wb["w_in"], "col")
    w_in_r, w_in_a, w_in_g = w_in_all[:, :RWKV_COLS], w_in_all[:, RWKV_COLS:RWKV_COLS + ATTN_COLS], w_in_all[:, RWKV_COLS + ATTN_COLS:]
    w2, a2, g2 = (_whole(wb[n], "col") for n in ("rwkv_w2", "rwkv_a2", "rwkv_g2"))
    w_brr, w_bra, w_pp = (_whole(wb[n], "col") for n in ("w_br_rwkv", "w_br_attn", "ple_w_proj"))
    w_o, w_pg = _whole(wb["w_out"], "row"), _whole(wb["ple_w_gate"], "row")
    vec = {n: wts[n].reshape(1, -1) for n, _ in SMALL}

    inv_freq = 1.0 / (ROPE_THETA ** (jnp.arange(0, HEAD, 2, dtype=F32) / HEAD))
    ang = positions[0].astype(F32)[:, None] * inv_freq
    cos, sin = jnp.cos(ang), jnp.sin(ang)
    cos2, sin2 = jnp.concatenate([cos, cos], axis=1), jnp.concatenate([-sin, sin], axis=1)

    x1, ffn1_saved = _ffn_fwd(x0, vec["ffn1_norm"], wb["ffn1_w_gate"], wb["ffn1_w_up"], wb["ffn1_w_down"], "ffn1")
    h = _rows_call(_rms, [(x1, 0, D_MODEL)], [vec["mix_norm"]], [(D_MODEL, BF16)], tm=256, name="mix_norm")[0]
    z_r = _mm(h, w_in_r, name="in_rwkv")
    z_a = _mm(h, w_in_a, name="in_attn")
    z_g = _mm(h, w_in_g, name="in_gate")

    zs = _shift_fwd(z_r, vec["rwkv_mu"])
    pre_params = [vec["rwkv_w0"], w2, vec["rwkv_a0"], a2, g2, vec["rwkv_k_k"], vec["rwkv_k_a"]]
    def pre_fwd(*t):
        res = _rwkv_pre(*t)
        return res[1], res[2], res[4], res[5], res[6]

    lw, k2, na, kb, gate_r = _rows_call(pre_fwd, [(zs, 0, RWKV_COLS)], pre_params, [(RWKV_DIM, F32)] * 5, tm=256, name="rwkv_pre")
    y_scan, s0s = _wkv_fwd(zs, lw, k2, na, kb)
    post_params = [vec["rwkv_gn_w"], vec["rwkv_gn_b"], vec["rwkv_r_k"]]
    post_rows = [(y_scan, 0, RWKV_DIM), (zs, 0, RWKV_DIM), (k2, 0, RWKV_DIM), (zs, 2, RWKV_DIM), (gate_r, 0, RWKV_DIM)]
    y_rwkv = _rows_call(_rwkv_post, post_rows, post_params, [(RWKV_DIM, BF16)], tm=256, name="rwkv_post")[0]

    def qk_fwd(qt, kt, ct, st, qg, kg):
        return _norm_rope(qt, qg, ct, st), _norm_rope(kt, kg, ct, st)

    qk_rows = [(z_a, 0, ATTN_DIM), (z_a, 1, ATTN_DIM), (cos2, 0, HEAD), (sin2, 0, HEAD)]
    q_rot, k_rot = _rows_call(qk_fwd, qk_rows, [vec["q_norm"], vec["k_norm"]], [(ATTN_DIM, BF16)] * 2, tm=256, name="attn_pre")
    outs, lses = zip(*[_attn_fwd(q_rot, k_rot, z_a, g, d) for g, d in enumerate(ATTN_DILATIONS)])
    comb_rows = [(t, 0, GROUP_DIM) for t in outs + lses]
    y_attn = _rows_call(_attn_combine, comb_rows, [], [(GROUP_DIM, BF16)], tm=256, name="attn_combine")[0]

    br = _mm(y_rwkv, w_brr, name="branch_rwkv")
    ba = _mm(y_attn, w_bra, name="branch_attn")
    merge_rows = [(z_g, 0, D_MODEL), (z_g, 1, D_MODEL), (br, 0, D_MODEL), (ba, 0, D_MODEL)]
    merged = _rows_call(_merge, merge_rows, [], [(D_MODEL, BF16)], tm=256, name="merge")[0]
    x2 = _mm(merged, w_o, res=x1, name="out_proj")
    x3, ffn2_saved = _ffn_fwd(x2, vec["ffn2_norm"], wb["ffn2_w_gate"], wb["ffn2_w_up"], wb["ffn2_w_down"], "ffn2")
    hp = _rows_call(_rms, [(x3, 0, D_MODEL)], [vec["ple_norm"]], [(D_MODEL, BF16)], tm=256, name="ple_norm")[0]
    pg = _mm(hp, w_pg, name="ple_gate")
    pp = _mm(p_tok, w_pp, name="ple_proj")

    def head(x3t, pgt, ppt, tt):
        sg = _sigmoid(pgt)
        err = x3t + sg * ppt - tt
        dx4 = err * (1.0 / D_MODEL)
        loss = 0.5 * jnp.sum(jnp.mean(err * err, axis=-1, keepdims=True), axis=0, keepdims=True)
        return dx4, dx4 * ppt * sg * (1.0 - sg), dx4 * sg, jnp.broadcast_to(loss, (8, 128))

    head_rows = [(x3, 0, D_MODEL), (pg, 0, D_MODEL), (pp, 0, D_MODEL), (tgt, 0, D_MODEL)]
    dx4, dpg, dpp, loss_tile = _rows_call(head, head_rows, [], [(D_MODEL, F32), (D_MODEL, BF16), (D_MODEL, BF16)], [(8, 128)],
                                          tm=256, name="ple_loss")

    gw, gs = {}, {}
    gw["ple_w_proj"] = _mm(p_tok, dpp, ta=True, name="ple_proj_dw")
    gw["ple_w_gate"] = _mm(hp, dpg, ta=True, name="ple_gate_dw")
    dhp = _mm(dpg, w_pg, tb=True, name="ple_gate_dx")
    dx3, gs["ple_norm"] = _norm_bwd(x3, vec["ple_norm"], dhp, dx4, "ple_norm_bwd")
    dx2, gs["ffn2_norm"], gw["ffn2_w_gate"], gw["ffn2_w_up"], gw["ffn2_w_down"] = _ffn_bwd(
        dx3, ffn2_saved, vec["ffn2_norm"], wb["ffn2_w_gate"], wb["ffn2_w_up"], wb["ffn2_w_down"], "ffn2")
    gw["w_out"] = _mm(merged, dx2, ta=True, name="out_proj_dw")
    dmerged = _mm(dx2, w_o, tb=True, name="out_proj_dx")

    def merge_bwd(zgr, zga, brt, bat, ct):
        _, vjp = jax.vjp(_merge, zgr, zga, brt, bat)
        d1, d2, d3, d4 = vjp(ct)
        return jnp.concatenate([d1, d2], axis=1), d3, d4

    dz_g, dbr, dba = _rows_call(merge_bwd, merge_rows + [(dmerged, 0, D_MODEL)], [],
                                [(2 * D_MODEL, BF16), (D_MODEL, BF16), (D_MODEL, BF16)], tm=256, name="merge_bwd")
    gw["w_br_rwkv"] = _mm(y_rwkv, dbr, ta=True, name="branch_rwkv_dw")
    gw["w_br_attn"] = _mm(y_attn, dba, ta=True, name="branch_attn_dw")
    dy_rwkv = _mm(dbr, w_brr, tb=True, name="branch_rwkv_dx")
    dy_attn = _mm(dba, w_bra, tb=True, name="branch_attn_dx")

    def comb_bwd(*t):
        _, vjp = jax.vjp(_attn_combine, *t[:6])
        return vjp(t[6])

    dcomb = _rows_call(comb_bwd, comb_rows + [(dy_attn, 0, GROUP_DIM)], [], [(GROUP_DIM, F32)] * 6, tm=256, name="attn_combine_bwd")
    dqs, dks, dvs = zip(*[_attn_bwd(q_rot, k_rot, z_a, g, d, dcomb[g], dcomb[3 + g]) for g, d in enumerate(ATTN_DILATIONS)])

    def qk_bwd(qt, kt, ct, st, *rest):
        dq = jnp.concatenate(rest[0:3], axis=1)
        dk = jnp.concatenate(rest[3:6], axis=1)
        qg, kg = rest[9], rest[10]
        _, vjp = jax.vjp(lambda a_, b_, c_, d_: qk_fwd(a_, b_, ct, st, c_, d_), qt, kt, qg, kg)
        dqt, dkt, dqg, dkg = vjp((dq, dk))
        return jnp.concatenate((dqt, dkt) + tuple(rest[6:9]), axis=1), dqg, dkg

    dz_a, gs["q_norm"], gs["k_norm"] = _rows_call(
        qk_bwd, qk_rows + [(t, 0, GROUP_DIM) for t in dqs + dks + dvs], [vec["q_norm"], vec["k_norm"]],
        [(ATTN_COLS, BF16)], [(1, HEAD), (1, HEAD)], tm=256, name="attn_pre_bwd")

    def post_bwd(*t):
        _, vjp = jax.vjp(_rwkv_post, *t[:5], *t[6:])
        return vjp(t[5])

    dy_scan, dr_post, dk2_post, dv_post, dgate_r, gs["rwkv_gn_w"], gs["rwkv_gn_b"], gs["rwkv_r_k"] = _rows_call(
        post_bwd, post_rows + [(dy_rwkv, 0, RWKV_DIM)], post_params, [(RWKV_DIM, F32)] * 5, [(1, RWKV_DIM)] * 3,
        tm=256, name="rwkv_post_bwd")
    dr_s, dlw, dk2_s, dv_s, dna, dkb = _wkv_bwd(zs, lw, k2, na, kb, s0s, dy_scan)

    def pre_bwd(zt, c_r1, c_r2, c_lw, c_k1, c_k2, c_v1, c_v2, c_a, c_b, c_g, *params):
        _, vjp = jax.vjp(_rwkv_pre, zt, *params)
        return vjp((c_r1 + c_r2, c_lw, c_k1 + c_k2, c_v1 + c_v2, c_a, c_b, c_g))

    pre_cts = [dr_s, dr_post, dlw, dk2_s, dk2_post, dv_s, dv_post, dna, dkb, dgate_r]
    dzs, gs["rwkv_w0"], g_w2, gs["rwkv_a0"], g_a2, g_g2, gs["rwkv_k_k"], gs["rwkv_k_a"] = _rows_call(
        pre_bwd, [(zs, 0, RWKV_COLS)] + [(t, 0, RWKV_DIM) for t in pre_cts], pre_params, [(RWKV_COLS, F32)],
        [q.shape for q in pre_params], tm=256, name="rwkv_pre_bwd")
    dz_r, gs["rwkv_mu"] = _shift_bwd(z_r, vec["rwkv_mu"], dzs)

    g_w_in = jnp.concatenate([_mm(h, dz_r, ta=True, name="in_rwkv_dw"), _mm(h, dz_a, ta=True, name="in_attn_dw"),
                              _mm(h, dz_g, ta=True, name="in_gate_dw")], axis=1)
    dh = _mm(dz_r, w_in_r, tb=True, name="in_rwkv_dx")
    dh = _mm(dz_a, w_in_a, tb=True, res=dh, name="in_attn_dx")
    dh = _mm(dz_g, w_in_g, tb=True, res=dh, name="in_gate_dx")
    dx1, gs["mix_norm"] = _norm_bwd(x1, vec["mix_norm"], dh, dx2, "mix_norm_bwd")
    dx0, gs["ffn1_norm"], gw["ffn1_w_gate"], gw["ffn1_w_up"], gw["ffn1_w_down"] = _ffn_bwd(
        dx1, ffn1_saved, vec["ffn1_norm"], wb["ffn1_w_gate"], wb["ffn1_w_up"], wb["ffn1_w_down"], "ffn1")

    how = {n: hw for n, _, hw in BIG}
    blocks = {n: gw[n] for n in ("ffn1_w_gate", "ffn1_w_up", "ffn1_w_down", "ffn2_w_gate", "ffn2_w_up", "ffn2_w_down")}
    blocks["w_in"] = _split(g_w_in, "col")
    blocks["rwkv_w2"], blocks["rwkv_a2"], blocks["rwkv_g2"] = _split(g_w2, "col"), _split(g_a2, "col"), _split(g_g2, "col")
    for n in ("w_br_rwkv", "w_br_attn", "w_out", "ple_w_gate", "ple_w_proj"):
        blocks[n] = _split(gw[n], how[n])
    packed = _pack_big(blocks)
    rows = packed.shape[1]
    xi, yi, ci = _coords()
    c_arr = jnp.reshape(ci, (1,)).astype(jnp.int32)
    me_arr = jnp.reshape(2 * xi + yi, (1,)).astype(jnp.int32)
    from_sibling = _swap_with_sibling(packed)
    pair, pair_bf = _add_pair(packed.reshape(N_CHIPS, 2, rows // 2, PACK_COLS), from_sibling, c_arr)
    from_chips = _send_to_chips(pair_bf)
    reduced = _join_halves(_add_chips(pair, from_chips, me_arr))
    grad_big = _unpack_big(reduced)

    flat = jnp.concatenate([gs[n].reshape(-1) for n, _ in SMALL] + [loss_tile[0, 0:1]])
    small_buf = jnp.pad(flat, (0, SMALL_ROWS * PACK_COLS - flat.shape[0])).reshape(SMALL_ROWS, PACK_COLS)
    small_sum = _all_reduce_small(small_buf)
    n_small = sum(sz for _, sz in SMALL)
    loss = small_sum.reshape(-1)[n_small]
    grad_small = _unpack_small(small_sum, {n: wts[n].shape for n, _ in SMALL})

    grads, deltas, new_m, new_v = {}, {}, {}, {}
    for n, shp, _ in BIG:
        g2d = grad_big[n]
        d_, m_, v_ = _adamw(wts[n][0], g2d, mom_m[n][0], mom_v[n][0], name=f"adamw_{n}")
        grads[n], deltas[n], new_m[n], new_v[n] = g2d[None], d_[None], m_[None], v_[None]
    d_s, m_s, v_s = _adamw(_pack_small(wts), small_sum, _pack_small(mom_m), _pack_small(mom_v), name="adamw_small")
    shapes = {n: wts[n].shape for n, _ in SMALL}
    d_s, m_s, v_s = _unpack_small(d_s, shapes), _unpack_small(m_s, shapes), _unpack_small(v_s, shapes)
    for n, _ in SMALL:
        grads[n], deltas[n], new_m[n], new_v[n] = grad_small[n], d_s[n], m_s[n], v_s[n]

    return (loss, dx0[None], *[grads[n] for n in WEIGHTS], *[deltas[n] for n in WEIGHTS],
            *[new_m[n] for n in WEIGHTS], *[new_v[n] for n in WEIGHTS])
```

```python
import functools

import jax
import jax.numpy as jnp
from jax import lax
from jax.experimental import pallas as pl
from jax.experimental.pallas import tpu as pltpu

F32, BF16 = jnp.float32, jnp.bfloat16
HI = lax.Precision.HIGHEST
MESH = pl.DeviceIdType.MESH
SDS = jax.ShapeDtypeStruct

D_MODEL = 1024
HEAD = 64
RWKV_HEADS = 8
RWKV_DIM = RWKV_HEADS * HEAD
DECAY_LORA, ICLR_LORA, GATE_LORA = 64, 64, 128
GN_EPS = 64e-5
RMS_EPS = 1e-6
ATTN_DILATIONS = (1, 4, 16)
BAND = 128
ATTN_DIM = 768
GROUP_DIM = 256
ROPE_THETA = 10000.0
NEG_INF = -1e30
RWKV_COLS = 3 * RWKV_DIM + DECAY_LORA + ICLR_LORA + GATE_LORA
ATTN_COLS = 3 * ATTN_DIM
ADAM_LR, ADAM_B1, ADAM_B2, ADAM_EPS, ADAM_WD, ADAM_STEP = 0.001, 0.9, 0.999, 1e-08, 0.01, 10

WKV_CHUNK = 64
WKV_HEADS_PER_STEP = 8
N_CHIPS = 4
PACK_COLS = 1024
VMEM_LIMIT = 48 * 1024 * 1024

BIG = (
    ("ffn1_w_gate", (1024, 704), "col"), ("ffn1_w_up", (1024, 704), "col"), ("ffn1_w_down", (704, 1024), "row"),
    ("w_in", (1024, 1536), "col"), ("rwkv_w2", (64, 128), "col"), ("rwkv_a2", (64, 128), "col"),
    ("rwkv_g2", (128, 128), "col"), ("w_br_rwkv", (512, 256), "col"), ("w_br_attn", (256, 256), "col"),
    ("w_out", (256, 1024), "row"), ("ffn2_w_gate", (1024, 704), "col"), ("ffn2_w_up", (1024, 704), "col"),
    ("ffn2_w_down", (704, 1024), "row"), ("ple_w_gate", (256, 1024), "row"), ("ple_w_proj", (256, 256), "col"),
)
SMALL = (
    ("ffn1_norm", 1024), ("mix_norm", 1024), ("ffn2_norm", 1024), ("ple_norm", 1024), ("rwkv_mu", 1792),
    ("rwkv_w0", 512), ("rwkv_a0", 512), ("rwkv_k_k", 512), ("rwkv_k_a", 512), ("rwkv_r_k", 512),
    ("rwkv_gn_w", 512), ("rwkv_gn_b", 512), ("q_norm", 64), ("k_norm", 64),
)
SMALL_ROWS = 16
WEIGHTS = (
    "ffn1_norm", "ffn1_w_gate", "ffn1_w_up", "ffn1_w_down", "mix_norm", "w_in", "rwkv_mu", "rwkv_w0", "rwkv_w2",
    "rwkv_a0", "rwkv_a2", "rwkv_g2", "rwkv_k_k", "rwkv_k_a", "rwkv_r_k", "rwkv_gn_w", "rwkv_gn_b", "q_norm", "k_norm",
    "w_br_rwkv", "w_br_attn", "w_out", "ffn2_norm", "ffn2_w_gate", "ffn2_w_up", "ffn2_w_down", "ple_norm",
    "ple_w_gate", "ple_w_proj",
)


def _pick(n, cands):
    for c in cands:
        if n % c == 0:
            return c
    return n


def _mm(a, b, *, ta=False, tb=False, sum_blocks=False, out_dtype=F32, res=None, alpha=1.0, name):
    flat = a.ndim == 2 and b.ndim == 2
    a3 = a if a.ndim == 3 else a[None]
    b3 = b if b.ndim == 3 else b[None]
    na, nbb = a3.shape[0], b3.shape[0]
    nblk = max(na, nbb)
    kdim, m = (a3.shape[1], a3.shape[2]) if ta else (a3.shape[2], a3.shape[1])
    n = b3.shape[1] if tb else b3.shape[2]
    assert (b3.shape[2] if tb else b3.shape[1]) == kdim
    tm = _pick(m, (1024, 512, 256, 128))
    tn = _pick(n, (1024, 896, 768, 512, 256, 128))
    tk = kdim if kdim <= 2304 else _pick(kdim, (1024, 512, 256, 128))
    nk = kdim // tk
    direct = nk == 1 and not sum_blocks

    if sum_blocks:
        grid = (m // tm, n // tn, nblk, nk)

        def ids(i, c, j, k):
            return i, c, j, k
    else:
        grid = (nblk, m // tm, n // tn, nk)

        def ids(j, i, c, k):
            return i, c, j, k

    def amap(*g):
        i, c, j, k = ids(*g)
        jj = j if na > 1 else 0
        return (jj, k, i) if ta else (jj, i, k)

    def bmap(*g):
        i, c, j, k = ids(*g)
        jj = j if nbb > 1 else 0
        return (jj, c, k) if tb else (jj, k, c)

    if sum_blocks:
        oshape, oblk = (m, n), (tm, tn)

        def omap(*g):
            i, c, j, k = ids(*g)
            return i, c
    else:
        oshape, oblk = (nblk, m, n), (1, tm, tn)

        def omap(*g):
            i, c, j, k = ids(*g)
            return j, i, c

    dn = (((0 if ta else 1,), (1 if tb else 0,)), ((), ()))
    has_res = res is not None

    def body(*refs):
        refs = list(refs)
        acc = None if direct else refs.pop()
        if has_res:
            a_ref, b_ref, r_ref, o_ref = refs
        else:
            a_ref, b_ref, o_ref = refs

        def finish(v):
            if alpha != 1.0:
                v = v * alpha
            if has_res:
                v = v + r_ref[...].reshape(v.shape).astype(F32)
            o_ref[...] = v.reshape(o_ref.shape).astype(o_ref.dtype)

        if direct:
            finish(lax.dot_general(a_ref[0].astype(BF16), b_ref[0].astype(BF16), dn, preferred_element_type=F32))
            return
        k = pl.program_id(3)
        if sum_blocks:
            j = pl.program_id(2)
            first = jnp.logical_and(j == 0, k == 0)
            last = jnp.logical_and(j == nblk - 1, k == nk - 1)
        else:
            first, last = k == 0, k == nk - 1

        @pl.when(first)
        def _():
            acc[...] = jnp.zeros_like(acc)

        acc[...] += lax.dot_general(a_ref[0].astype(BF16), b_ref[0].astype(BF16), dn, preferred_element_type=F32)

        @pl.when(last)
        def _():
            finish(acc[...])

    in_specs = [pl.BlockSpec((1, tk, tm) if ta else (1, tm, tk), amap), pl.BlockSpec((1, tn, tk) if tb else (1, tk, tn), bmap)]
    args = [a3, b3]
    if has_res:
        res3 = res if (sum_blocks or res.ndim == 3) else res[None]
        in_specs.append(pl.BlockSpec(oblk, omap))
        args.append(res3)
    out = pl.pallas_call(
        body,
        name=name,
        grid=grid,
        in_specs=in_specs,
        out_specs=pl.BlockSpec(oblk, omap),
        out_shape=SDS(oshape, out_dtype),
        scratch_shapes=[] if direct else [pltpu.VMEM((tm, tn), F32)],
        compiler_params=pltpu.CompilerParams(
            dimension_semantics=("parallel", "parallel", "arbitrary", "arbitrary") if sum_blocks
            else ("parallel", "parallel", "parallel", "arbitrary"),
            vmem_limit_bytes=VMEM_LIMIT),
    )(*args)
    if flat and not sum_blocks:
        out = out[0]
    return out


def _rows_call(f, rows, params, outs, accs=(), *, tm, name):
    s = rows[0][0].shape[0]
    nr, npar, no = len(rows), len(params), len(outs)
    in_specs = [pl.BlockSpec((tm, w), functools.partial(lambda i, cb: (i, cb), cb=cb)) for (_, cb, w) in rows]
    in_specs += [pl.BlockSpec(p.shape, functools.partial(lambda i, nd: (0,) * nd, nd=p.ndim)) for p in params]
    out_shape = [SDS((s, w), dt) for (w, dt) in outs] + [SDS(tuple(sh), F32) for sh in accs]
    out_specs = [pl.BlockSpec((tm, w), lambda i: (i, 0)) for (w, _) in outs]
    out_specs += [pl.BlockSpec(tuple(sh), functools.partial(lambda i, nd: (0,) * nd, nd=len(sh))) for sh in accs]

    def body(*refs):
        rin, pin = refs[:nr], refs[nr:nr + npar]
        oo, ao = refs[nr + npar:nr + npar + no], refs[nr + npar + no:]
        res = f(*[r[...] for r in rin], *[p[...] for p in pin])
        if not isinstance(res, (tuple, list)):
            res = (res,)
        for o_ref, v in zip(oo, res[:no]):
            o_ref[...] = v.astype(o_ref.dtype)
        i = pl.program_id(0)
        for a_ref, v in zip(ao, res[no:]):
            @pl.when(i == 0)
            def _():
                a_ref[...] = jnp.zeros_like(a_ref)

            a_ref[...] += v.reshape(a_ref.shape)

    res = pl.pallas_call(
        body,
        name=name,
        grid=(s // tm,),
        in_specs=in_specs,
        out_specs=out_specs,
        out_shape=out_shape,
        compiler_params=pltpu.CompilerParams(dimension_semantics=("arbitrary",), vmem_limit_bytes=VMEM_LIMIT),
    )(*[r[0] for r in rows], *params)
    return res


def _mmv(a, b, mode):
    ca = 0 if mode[0] == "t" else 1
    cb = 1 if mode[1] == "t" else 0
    return lax.dot_general(a.astype(BF16), b.astype(BF16), (((ca,), (cb,)), ((), ())), preferred_element_type=F32)


@functools.partial(jax.custom_vjp, nondiff_argnums=(2,))
def _bdot(a, b, mode):
    return _mmv(a, b, mode)


def _bdot_fwd(a, b, mode):
    return _mmv(a, b, mode), (a, b)


def _bdot_bwd(mode, saved, g):
    a, b = saved
    if mode == "nn":
        return _mmv(g, b, "nt"), _mmv(a, g, "tn")
    if mode == "nt":
        return _mmv(g, b, "nn"), _mmv(g, a, "tn")
    return _mmv(b, g, "nt"), _mmv(a, g, "nn")


_bdot.defvjp(_bdot_fwd, _bdot_bwd)


def _hdot(a, b, mode="nn", precision=HI):
    ca = 0 if mode[0] == "t" else 1
    cb = 1 if mode[1] == "t" else 0
    return lax.dot_general(a, b, (((ca,), (cb,)), ((), ())), precision=precision, preferred_element_type=F32)


def _wdot(a, b, mode="nn"):
    return _hdot(a, b, mode, lax.Precision.HIGH)


def _segsum(x):
    c = x.shape[-1]
    r = lax.broadcasted_iota(jnp.int32, (c, c), 0) >> 6
    q = lax.broadcasted_iota(jnp.int32, (c, c), 1) >> 6
    return _hdot(x, jnp.where(r == q, 1.0, 0.0).astype(F32), precision=lax.Precision.HIGH)


def _sigmoid(x):
    return jax.nn.sigmoid(x)


def _softplus(x):
    return jnp.maximum(x, 0.0) + jnp.log(1.0 + jnp.exp(-jnp.abs(x)))


def _rms(x, gain):
    return x * lax.rsqrt(jnp.mean(x * x, axis=-1, keepdims=True) + RMS_EPS) * gain


def _swiglu_act(gate, up):
    return gate * _sigmoid(gate) * up


def _rwkv_pre(zs, w0, w2, a0, a2, g2, k_k, k_a):
    r, k, v = zs[:, 0:512], zs[:, 512:1024], zs[:, 1024:1536]
    lora = zs[:, 1536:1792]
    wd, ad, gd = lora[:, 0:64], lora[:, 64:128], lora[:, 128:256]
    w = -_softplus(-(w0 + _bdot(jnp.tanh(wd), w2, "nn"))) - 0.5
    a = _sigmoid(a0 + _bdot(ad, a2, "nn"))
    g = _bdot(_sigmoid(gd), g2, "nn")
    kk = k * k_k
    kk = kk * lax.rsqrt(jnp.maximum(_segsum(kk * kk), 1e-24))
    k2 = k * (1.0 + (a - 1.0) * k_a)
    return r, -jnp.exp(w), k2, v, -kk, kk * a, g


def _rwkv_post(y, r, k2, v, g, gn_w, gn_b, r_k):
    mean = _segsum(y) * (1.0 / HEAD)
    yc = y - mean
    var = _segsum(yc * yc) * (1.0 / HEAD)
    yn = yc * lax.rsqrt(var + GN_EPS) * gn_w + gn_b
    bonus = _segsum(r * k2 * r_k) * v
    return (yn + bonus) * g


def _swap_halves(x):
    lane = lax.broadcasted_iota(jnp.int32, x.shape, 1)
    return jnp.where((lane & 32) == 0, jnp.roll(x, -32, axis=1), jnp.roll(x, 32, axis=1))


def _norm_rope(x, gain, cos, sin):
    heads = x.shape[1] // HEAD
    def rep(t):
        return jnp.concatenate([t] * heads, axis=1)

    xn = x * lax.rsqrt(_segsum(x * x) * (1.0 / HEAD) + RMS_EPS) * rep(gain)
    return xn * rep(cos) + _swap_halves(xn) * rep(sin)


def _attn_combine(o0, o1, o2, l0, l1, l2):
    m = jnp.maximum(jnp.maximum(l0, l1), l2)
    e0, e1, e2 = jnp.exp(l0 - m), jnp.exp(l1 - m), jnp.exp(l2 - m)
    return (e0 * o0 + e1 * o1 + e2 * o2) / (e0 + e1 + e2)


def _merge(zgr, zga, br, ba):
    return _sigmoid(zgr) * br + _sigmoid(zga) * ba


def _attn_block(q, kp, kc, vp, vc, has_prev):
    iq = lax.broadcasted_iota(jnp.int32, (BAND, BAND), 0)
    ik = lax.broadcasted_iota(jnp.int32, (BAND, BAND), 1)
    s_c = jnp.where(iq >= ik, _bdot(q, kc, "nt") * (HEAD ** -0.5), NEG_INF)
    s_p = jnp.where(jnp.logical_and(iq <= ik, has_prev), _bdot(q, kp, "nt") * (HEAD ** -0.5), NEG_INF)
    m = lax.stop_gradient(jnp.maximum(jnp.max(s_c, axis=-1, keepdims=True), jnp.max(s_p, axis=-1, keepdims=True)))
    e_c, e_p = jnp.exp(s_c - m), jnp.exp(s_p - m)
    l = jnp.sum(e_c, axis=-1, keepdims=True) + jnp.sum(e_p, axis=-1, keepdims=True)
    o = (_bdot(e_c, vc, "nn") + _bdot(e_p, vp, "nn")) / l
    return o, jnp.broadcast_to(m + jnp.log(l), o.shape)


def _bdotb(a, b, mode="nn", precision=lax.Precision.HIGH):
    if mode[0] == "t":
        a = jnp.swapaxes(a, 1, 2)
    cb = 2 if mode[1] == "t" else 1
    return lax.dot_general(a, b, (((2,), (cb,)), ((0,), (0,))), precision=precision, preferred_element_type=F32)


def _tri_inv(a):
    t = a.shape[-1]
    row = lax.broadcasted_iota(jnp.int32, (1, t, t), 1)
    col = lax.broadcasted_iota(jnp.int32, (1, t, t), 2)
    x = jnp.where(row == col, 1.0, 0.0).astype(F32) + jnp.where(jnp.logical_and(row == col + 1, (row & 1) == 1), a, 0.0)
    sh = 1
    while (1 << sh) < t:
        m = jnp.logical_and((row >> sh) == (col >> sh) + 1, (row >> (sh + 1)) == (col >> (sh + 1)))
        x = x + _bdotb(_bdotb(x, jnp.where(m, a, 0.0)), x)
        sh += 1
    return x


def _wkv_chunk(s0, r, lw, k, v, a, b):
    nh, t, _ = r.shape
    row = lax.broadcasted_iota(jnp.int32, (1, t, t), 1)
    col = lax.broadcasted_iota(jnp.int32, (1, t, t), 2)
    incl, strict = row >= col, row > col
    ones = jnp.broadcast_to(jnp.where(incl, 1.0, 0.0).astype(F32), (nh, t, t))
    cum = _bdotb(ones, lw, precision=HI)
    c_end = cum[:, t - 1:t, :]
    e_in, e_ex, e_inv = jnp.exp(cum), jnp.exp(cum - lw), jnp.exp(-cum)
    at, rt, bt, kt = a * e_ex, r * e_in, b * e_inv, k * e_inv
    a_ab = jnp.where(strict, _bdotb(at, bt, "nt"), 0.0)
    a_ak = jnp.where(strict, _bdotb(at, kt, "nt"), 0.0)
    u = _bdotb(_tri_inv(a_ab), _bdotb(at, s0, "nt") + _bdotb(a_ak, v))
    y = (_bdotb(rt, s0, "nt") + _bdotb(jnp.where(incl, _bdotb(rt, bt, "nt"), 0.0), u)
         + _bdotb(jnp.where(incl, _bdotb(rt, kt, "nt"), 0.0), v))
    w_end = jnp.exp(c_end - cum)
    s1 = s0 * jnp.exp(c_end) + _bdotb(u, b * w_end, "tn") + _bdotb(v, k * w_end, "tn")
    return y, s1


def _shift_fwd(z, mu):
    s, c = z.shape
    tc = 256

    def body(z_ref, mu_ref, o_ref):
        zz = z_ref[...]
        row = lax.broadcasted_iota(jnp.int32, zz.shape, 0)
        prev = jnp.where(row == 0, 0.0, pltpu.roll(zz, 1, 0))
        o_ref[...] = zz + (prev - zz) * mu_ref[...]

    return pl.pallas_call(
        body, name="shift_fwd", grid=(c // tc,),
        in_specs=[pl.BlockSpec((s, tc), lambda j: (0, j)), pl.BlockSpec((1, tc), lambda j: (0, j))],
        out_specs=pl.BlockSpec((s, tc), lambda j: (0, j)), out_shape=SDS((s, c), F32),
        compiler_params=pltpu.CompilerParams(dimension_semantics=("parallel",), vmem_limit_bytes=VMEM_LIMIT),
    )(z, mu)


def _shift_bwd(z, mu, dzs):
    s, c = z.shape
    tc = 256

    def body(z_ref, mu_ref, d_ref, dz_ref, dmu_ref):
        zz, d, m = z_ref[...], d_ref[...], mu_ref[...]
        row = lax.broadcasted_iota(jnp.int32, zz.shape, 0)
        prev = jnp.where(row == 0, 0.0, pltpu.roll(zz, 1, 0))
        t = d * m
        nxt = jnp.where(row == s - 1, 0.0, pltpu.roll(t, s - 1, 0))
        dz_ref[...] = (d - t + nxt).astype(dz_ref.dtype)
        dmu_ref[...] = jnp.sum(d * (prev - zz), axis=0, keepdims=True)

    return pl.pallas_call(
        body, name="shift_bwd", grid=(c // tc,),
        in_specs=[pl.BlockSpec((s, tc), lambda j: (0, j)), pl.BlockSpec((1, tc), lambda j: (0, j)),
                  pl.BlockSpec((s, tc), lambda j: (0, j))],
        out_specs=[pl.BlockSpec((s, tc), lambda j: (0, j)), pl.BlockSpec((1, tc), lambda j: (0, j))],
        out_shape=[SDS((s, c), BF16), SDS((1, c), F32)],
        compiler_params=pltpu.CompilerParams(dimension_semantics=("parallel",), vmem_limit_bytes=VMEM_LIMIT),
    )(z, mu, dzs)


def _heads(x, nh):
    return jnp.stack([x[:, h * HEAD:(h + 1) * HEAD] for h in range(nh)], axis=0)


def _unheads(x):
    return jnp.concatenate([x[h] for h in range(x.shape[0])], axis=1)


def _wkv_fwd(zs, lw, k2, na, b):
    s = lw.shape[0]
    t, hb = WKV_CHUNK, WKV_HEADS_PER_STEP
    w = hb * HEAD
    nc, ng = s // t, RWKV_HEADS // hb

    def body(r_ref, v_ref, lw_ref, k_ref, a_ref, b_ref, y_ref, s0_ref, state):
        @pl.when(pl.program_id(1) == 0)
        def _():
            state[...] = jnp.zeros_like(state)

        s0 = state[...]
        s0_ref[0] = s0
        y, s1 = _wkv_chunk(s0, *[_heads(t_ref[...], hb) for t_ref in (r_ref, lw_ref, k_ref, v_ref, a_ref, b_ref)])
        y_ref[...] = _unheads(y)
        state[...] = s1

    def col(off):
        return pl.BlockSpec((t, w), functools.partial(lambda g, i, off: (i, g + off), off=off))

    return pl.pallas_call(
        body, name="wkv_fwd", grid=(ng, nc),
        in_specs=[col(0), col(2 * ng), col(0), col(0), col(0), col(0)],
        out_specs=[col(0), pl.BlockSpec((1, hb, HEAD, HEAD), lambda g, i: (i, g, 0, 0))],
        out_shape=[SDS((s, RWKV_DIM), F32), SDS((nc, RWKV_HEADS, HEAD, HEAD), F32)],
        scratch_shapes=[pltpu.VMEM((hb, HEAD, HEAD), F32)],
        compiler_params=pltpu.CompilerParams(dimension_semantics=("parallel", "arbitrary"), vmem_limit_bytes=VMEM_LIMIT),
    )(zs, zs, lw, k2, na, b)


def _wkv_bwd(zs, lw, k2, na, b, s0s, dy):
    s = lw.shape[0]
    t, hb = WKV_CHUNK, WKV_HEADS_PER_STEP
    w = hb * HEAD
    nc, ng = s // t, RWKV_HEADS // hb

    def body(r_ref, v_ref, lw_ref, k_ref, a_ref, b_ref, s0_ref, dy_ref, dr_ref, dlw_ref, dk_ref, dv_ref, da_ref, db_ref, dstate):
        @pl.when(pl.program_id(1) == 0)
        def _():
            dstate[...] = jnp.zeros_like(dstate)

        _, vjp = jax.vjp(_wkv_chunk, s0_ref[0], *[_heads(t_ref[...], hb) for t_ref in (r_ref, lw_ref, k_ref, v_ref, a_ref, b_ref)])
        grads = vjp((_heads(dy_ref[...], hb), dstate[...]))
        dstate[...] = grads[0]
        for o_ref, gval in zip((dr_ref, dlw_ref, dk_ref, dv_ref, da_ref, db_ref), grads[1:]):
            o_ref[...] = _unheads(gval)

    def col(off):
        return pl.BlockSpec((t, w), functools.partial(lambda g, i, off: (nc - 1 - i, g + off), off=off))

    return pl.pallas_call(
        body, name="wkv_bwd", grid=(ng, nc),
        in_specs=[col(0), col(2 * ng), col(0), col(0), col(0), col(0),
                  pl.BlockSpec((1, hb, HEAD, HEAD), lambda g, i: (nc - 1 - i, g, 0, 0)), col(0)],
        out_specs=[col(0)] * 6,
        out_shape=[SDS((s, RWKV_DIM), F32)] * 6,
        scratch_shapes=[pltpu.VMEM((hb, HEAD, HEAD), F32)],
        compiler_params=pltpu.CompilerParams(dimension_semantics=("parallel", "arbitrary"), vmem_limit_bytes=VMEM_LIMIT),
    )(zs, zs, lw, k2, na, b, s0s, dy)


def _attn_fwd(q, k, z_a, g, d):
    s = q.shape[0]
    l = s // d
    nb = l // BAND
    assert nb * BAND == l
    qv, kv, zv = q.reshape(l, d * ATTN_DIM), k.reshape(l, d * ATTN_DIM), z_a.reshape(l, d * ATTN_COLS)

    def body(q_ref, kp_ref, kc_ref, vp_ref, vc_ref, o_ref, l_ref):
        has_prev = pl.program_id(1) > 0
        for h in range(GROUP_DIM // HEAD):
            sl = slice(h * HEAD, (h + 1) * HEAD)
            o, lse = _attn_block(q_ref[:, sl].astype(F32), kp_ref[:, sl].astype(F32), kc_ref[:, sl].astype(F32),
                                 vp_ref[:, sl], vc_ref[:, sl], has_prev)
            o_ref[:, sl] = o
            l_ref[:, sl] = lse

    def spec(per_tok, off, prev):
        def imap(rho, i):
            return (jnp.maximum(i - 1, 0) if prev else i, rho * per_tok + off)
        return pl.BlockSpec((BAND, GROUP_DIM), imap)

    o, lse = pl.pallas_call(
        body, name=f"attn_fwd_d{d}", grid=(d, nb),
        in_specs=[spec(3, g, False), spec(3, g, True), spec(3, g, False), spec(9, 6 + g, True), spec(9, 6 + g, False)],
        out_specs=[spec(1, 0, False), spec(1, 0, False)],
        out_shape=[SDS((l, d * GROUP_DIM), F32), SDS((l, d * GROUP_DIM), F32)],
        compiler_params=pltpu.CompilerParams(dimension_semantics=("parallel", "arbitrary"), vmem_limit_bytes=VMEM_LIMIT),
    )(qv, kv, kv, zv, zv)
    return o.reshape(s, GROUP_DIM), lse.reshape(s, GROUP_DIM)


def _attn_bwd(q, k, z_a, g, d, do, dlse):
    s = q.shape[0]
    l = s // d
    nb = l // BAND
    qv, kv, zv = q.reshape(l, d * ATTN_DIM), k.reshape(l, d * ATTN_DIM), z_a.reshape(l, d * ATTN_COLS)
    dov, dlv = do.reshape(l, d * GROUP_DIM), dlse.reshape(l, d * GROUP_DIM)

    def body(q_ref, kp_ref, kc_ref, vp_ref, vc_ref, do_ref, dl_ref, dq_ref, dk_ref, dv_ref, ck, cv):
        step = pl.program_id(1)
        has_prev = step < nb - 1

        @pl.when(step == 0)
        def _():
            ck[...] = jnp.zeros_like(ck)
            cv[...] = jnp.zeros_like(cv)

        for h in range(GROUP_DIM // HEAD):
            sl = slice(h * HEAD, (h + 1) * HEAD)
            _, vjp = jax.vjp(functools.partial(_attn_block, has_prev=has_prev), q_ref[:, sl].astype(F32),
                             kp_ref[:, sl].astype(F32), kc_ref[:, sl].astype(F32), vp_ref[:, sl], vc_ref[:, sl])
            dq, dkp, dkc, dvp, dvc = vjp((do_ref[:, sl], dl_ref[:, sl]))
            dq_ref[:, sl] = dq
            dk_ref[:, sl] = dkc + ck[:, sl]
            dv_ref[:, sl] = dvc + cv[:, sl]
            ck[:, sl] = dkp
            cv[:, sl] = dvp

    def spec(per_tok, off, prev):
        def imap(rho, i):
            blk = nb - 1 - i
            return (jnp.maximum(blk - 1, 0) if prev else blk, rho * per_tok + off)
        return pl.BlockSpec((BAND, GROUP_DIM), imap)

    dq, dk, dv = pl.pallas_call(
        body, name=f"attn_bwd_d{d}", grid=(d, nb),
        in_specs=[spec(3, g, False), spec(3, g, True), spec(3, g, False), spec(9, 6 + g, True), spec(9, 6 + g, False),
                  spec(1, 0, False), spec(1, 0, False)],
        out_specs=[spec(1, 0, False)] * 3,
        out_shape=[SDS((l, d * GROUP_DIM), F32)] * 3,
        scratch_shapes=[pltpu.VMEM((BAND, GROUP_DIM), F32), pltpu.VMEM((BAND, GROUP_DIM), F32)],
        compiler_params=pltpu.CompilerParams(dimension_semantics=("parallel", "arbitrary"), vmem_limit_bytes=VMEM_LIMIT),
    )(qv, kv, kv, zv, zv, dov, dlv)
    return dq.reshape(s, GROUP_DIM), dk.reshape(s, GROUP_DIM), dv.reshape(s, GROUP_DIM)


def _coords():
    return lax.axis_index("x"), lax.axis_index("y"), lax.axis_index("c")


_CHIP_FLIPS = ((1, 0), (0, 1), (1, 1))


def _flip(v, f):
    return 1 - v if f else v


def _gather_weights(shard):
    rows = shard.shape[0]
    half = rows // 2

    def body(src, out, send_sems, recv_sems):
        x, y, c = _coords()
        me = 2 * x + y
        mine = pl.ds(pl.multiple_of(c * half, 16), half)

        def chip_of(f):
            return _flip(x, f[0]), _flip(y, f[1])

        def over_ici(kk):
            px, py = chip_of(_CHIP_FLIPS[kk])
            return pltpu.make_async_remote_copy(src_ref=src.at[mine], dst_ref=out.at[me, mine], send_sem=send_sems.at[kk],
                                                recv_sem=recv_sems.at[kk], device_id=(px, py, c), device_id_type=MESH)

        def landed(kk):
            px, py = chip_of(_CHIP_FLIPS[kk])
            there = out.at[2 * px + py, mine]
            return pltpu.make_async_remote_copy(src_ref=there, dst_ref=there, send_sem=send_sems.at[kk],
                                                recv_sem=recv_sems.at[kk], device_id=(px, py, c), device_id_type=MESH)

        def passed_on(kk, sent_by_me):
            px, py = chip_of(_CHIP_FLIPS[kk])
            part = mine if sent_by_me else pl.ds(pl.multiple_of((1 - c) * half, 16), half)
            there = out.at[2 * px + py, part]
            return pltpu.make_async_remote_copy(src_ref=there, dst_ref=there, send_sem=send_sems.at[3 + kk],
                                                recv_sem=recv_sems.at[3 + kk], device_id=(x, y, 1 - c), device_id_type=MESH)

        sends = [over_ici(kk) for kk in range(3)]
        for cp in sends:
            cp.start()
        for kk in range(3):
            landed(kk).wait_recv()
            fwd = passed_on(kk, True)
            fwd.start()
            sends.append(fwd)
        for kk in range(3):
            passed_on(kk, False).wait_recv()
        for cp in sends:
            cp.wait_send()

    out = pl.pallas_call(
        body, name="gather_weights",
        in_specs=[pl.BlockSpec(memory_space=pl.ANY)], out_specs=pl.BlockSpec(memory_space=pl.ANY),
        out_shape=SDS((N_CHIPS, rows, PACK_COLS), shard.dtype),
        scratch_shapes=[pltpu.SemaphoreType.DMA((6,)), pltpu.SemaphoreType.DMA((6,))],
    )(shard)
    x, y, _ = _coords()
    return lax.dynamic_update_slice(out, shard[None], (2 * x + y, 0, 0))


def _swap_with_sibling(g):
    n, rows, cols = g.shape
    half = rows // 2

    def body(src, out, send_sem, recv_sem):
        x, y, c = _coords()
        theirs = pl.ds(pl.multiple_of((1 - c) * half, 8), half)
        cp = pltpu.make_async_remote_copy(src_ref=src.at[:, theirs], dst_ref=out, send_sem=send_sem, recv_sem=recv_sem,
                                          device_id=(x, y, 1 - c), device_id_type=MESH)
        cp.start()
        cp.wait()

    return pl.pallas_call(
        body, name="grad_swap_sibling",
        in_specs=[pl.BlockSpec(memory_space=pl.ANY)], out_specs=pl.BlockSpec(memory_space=pl.ANY),
        out_shape=SDS((n, half, cols), g.dtype),
        scratch_shapes=[pltpu.SemaphoreType.DMA, pltpu.SemaphoreType.DMA],
    )(g)


def _send_to_chips(a):
    n, half, cols = a.shape

    def body(src, out, send_sems, recv_sems):
        x, y, c = _coords()
        me = 2 * x + y
        sends = []
        for kk, f in enumerate(_CHIP_FLIPS):
            px, py = _flip(x, f[0]), _flip(y, f[1])
            cp = pltpu.make_async_remote_copy(src_ref=src.at[2 * px + py], dst_ref=out.at[kk], send_sem=send_sems.at[kk],
                                              recv_sem=recv_sems.at[kk], device_id=(px, py, c), device_id_type=MESH)
            cp.start()
            sends.append(cp)
        for cp in sends:
            cp.wait()

    return pl.pallas_call(
        body, name="grad_send_chips",
        in_specs=[pl.BlockSpec(memory_space=pl.ANY)], out_specs=pl.BlockSpec(memory_space=pl.ANY),
        out_shape=SDS((3, half, cols), a.dtype),
        scratch_shapes=[pltpu.SemaphoreType.DMA((3,)), pltpu.SemaphoreType.DMA((3,))],
    )(a)


def _join_halves(r):
    half, cols = r.shape

    def body(src, out, send_sem, recv_sem):
        x, y, c = _coords()
        mine = pl.ds(pl.multiple_of(c * half, 8), half)
        cp = pltpu.make_async_remote_copy(src_ref=src, dst_ref=out.at[mine], send_sem=send_sem, recv_sem=recv_sem,
                                          device_id=(x, y, 1 - c), device_id_type=MESH)
        cp.start()
        theirs = out.at[pl.ds(pl.multiple_of((1 - c) * half, 8), half)]
        pltpu.make_async_remote_copy(src_ref=theirs, dst_ref=theirs, send_sem=send_sem, recv_sem=recv_sem,
                                     device_id=(x, y, 1 - c), device_id_type=MESH).wait_recv()
        cp.wait_send()

    out = pl.pallas_call(
        body, name="grad_join_halves",
        in_specs=[pl.BlockSpec(memory_space=pl.ANY)], out_specs=pl.BlockSpec(memory_space=pl.ANY),
        out_shape=SDS((2 * half, cols), r.dtype),
        scratch_shapes=[pltpu.SemaphoreType.DMA, pltpu.SemaphoreType.DMA],
    )(r)
    return lax.dynamic_update_slice(out, r, (lax.axis_index("c") * half, 0))


def _add_pair(g, recv, c_arr):
    n, _, half, cols = g.shape
    tr = _pick(half, (656, 328, 8))

    def body(c_ref, g_ref, r_ref, o_ref, ob_ref):
        v = g_ref[:, 0] + r_ref[...]
        o_ref[...] = v
        ob_ref[...] = v.astype(BF16)

    return pl.pallas_call(
        body, name="grad_add_pair",
        grid_spec=pltpu.PrefetchScalarGridSpec(
            num_scalar_prefetch=1, grid=(n, half // tr),
            in_specs=[pl.BlockSpec((1, 1, tr, cols), lambda j, i, c_ref: (j, c_ref[0], i, 0)),
                      pl.BlockSpec((1, tr, cols), lambda j, i, c_ref: (j, i, 0))],
            out_specs=[pl.BlockSpec((1, tr, cols), lambda j, i, c_ref: (j, i, 0))] * 2),
        out_shape=[SDS((n, half, cols), F32), SDS((n, half, cols), BF16)],
        compiler_params=pltpu.CompilerParams(dimension_semantics=("parallel", "parallel"), vmem_limit_bytes=VMEM_LIMIT),
    )(c_arr, g, recv)


def _add_chips(a, recv, me_arr):
    n, half, cols = a.shape
    tr = _pick(half, (656, 328, 8))

    def body(me_ref, a_ref, r_ref, o_ref):
        o_ref[...] = ((a_ref[0] + r_ref[0].astype(F32)) + r_ref[1].astype(F32)) + r_ref[2].astype(F32)

    return pl.pallas_call(
        body, name="grad_add_chips",
        grid_spec=pltpu.PrefetchScalarGridSpec(
            num_scalar_prefetch=1, grid=(half // tr,),
            in_specs=[pl.BlockSpec((1, tr, cols), lambda i, me_ref: (me_ref[0], i, 0)),
                      pl.BlockSpec((3, tr, cols), lambda i, me_ref: (0, i, 0))],
            out_specs=pl.BlockSpec((tr, cols), lambda i, me_ref: (i, 0))),
        out_shape=SDS((half, cols), F32),
        compiler_params=pltpu.CompilerParams(dimension_semantics=("parallel",), vmem_limit_bytes=VMEM_LIMIT),
    )(me_arr, a, recv)


def _all_reduce_small(buf):
    rows, cols = buf.shape

    def body(x_ref, o_ref, gath, send_sems, recv_sems):
        x, y, c = _coords()
        me = 4 * x + 2 * y + c
        gath[me] = x_ref[...]
        sends = []
        for kk in range(1, 8):
            f = (kk >> 2) & 1, (kk >> 1) & 1, kk & 1
            px, py, pc = _flip(x, f[0]), _flip(y, f[1]), _flip(c, f[2])
            cp = pltpu.make_async_remote_copy(src_ref=x_ref, dst_ref=gath.at[me], send_sem=send_sems.at[kk - 1],
                                              recv_sem=recv_sems.at[kk - 1], device_id=(px, py, pc), device_id_type=MESH)
            cp.start()
            sends.append(cp)
        for kk in range(1, 8):
            f = (kk >> 2) & 1, (kk >> 1) & 1, kk & 1
            px, py, pc = _flip(x, f[0]), _flip(y, f[1]), _flip(c, f[2])
            there = gath.at[4 * px + 2 * py + pc]
            pltpu.make_async_remote_copy(src_ref=there, dst_ref=there, send_sem=send_sems.at[kk - 1],
                                         recv_sem=recv_sems.at[kk - 1], device_id=(px, py, pc), device_id_type=MESH).wait_recv()
        for cp in sends:
            cp.wait_send()
        acc = gath[0]
        for j in range(1, 8):
            acc = acc + gath[j]
        o_ref[...] = acc

    return pl.pallas_call(
        body, name="all_reduce_small",
        in_specs=[pl.BlockSpec(memory_space=pltpu.VMEM)], out_specs=pl.BlockSpec(memory_space=pltpu.VMEM),
        out_shape=SDS((rows, cols), F32),
        scratch_shapes=[pltpu.VMEM((8, rows, cols), F32), pltpu.SemaphoreType.DMA((7,)), pltpu.SemaphoreType.DMA((7,))],
    )(buf)


def _adamw_rows(w, g, m, v):
    m = ADAM_B1 * m + (1.0 - ADAM_B1) * g
    v = ADAM_B2 * v + (1.0 - ADAM_B2) * jnp.square(g)
    m_hat = m / (1.0 - ADAM_B1 ** ADAM_STEP)
    v_hat = v / (1.0 - ADAM_B2 ** ADAM_STEP)
    return -ADAM_LR * (m_hat / (jnp.sqrt(v_hat) + ADAM_EPS) + ADAM_WD * w), m, v


def _adamw(w, g, m, v, name):
    rows, cols = w.shape
    tm = _pick(rows, (256, 128, 64, 16, 8))
    return _rows_call(_adamw_rows, [(t, 0, cols) for t in (w, g, m, v)], [], [(cols, F32)] * 3, tm=tm, name=name)


def _pack_big(parts):
    return jnp.concatenate([parts[n].reshape(parts[n].shape[:-2] + (-1, PACK_COLS)) for n, _, _ in BIG], axis=-2)


def _unpack_big(buf):
    out, off = {}, 0
    for n, shp, _ in BIG:
        r = shp[0] * shp[1] // PACK_COLS
        out[n] = buf[..., off:off + r, :].reshape(buf.shape[:-2] + shp)
        off += r
    return out


def _pack_small(parts):
    flat = jnp.concatenate([parts[n].reshape(-1) for n, _ in SMALL])
    return jnp.pad(flat, (0, SMALL_ROWS * PACK_COLS - flat.shape[0])).reshape(SMALL_ROWS, PACK_COLS)


def _unpack_small(buf, shapes):
    flat, out, off = buf.reshape(-1), {}, 0
    for n, sz in SMALL:
        out[n] = flat[off:off + sz].reshape(shapes[n])
        off += sz
    return out


def _whole(blocks, how):
    n, r, c = blocks.shape
    if how == "row":
        return blocks.reshape(n * r, c)
    return blocks.transpose(1, 0, 2).reshape(r, n * c)


def _split(whole, how):
    if how == "row":
        return whole.reshape(N_CHIPS, whole.shape[0] // N_CHIPS, whole.shape[1])
    r, c = whole.shape
    return whole.reshape(r, N_CHIPS, c // N_CHIPS).transpose(1, 0, 2)


def _ffn_fwd(x, gain, wg, wu, wd, tag):
    h = _rows_call(_rms, [(x, 0, D_MODEL)], [gain], [(D_MODEL, BF16)], tm=256, name=f"{tag}_norm")[0]
    gate = _mm(h, wg, name=f"{tag}_gate")
    up = _mm(h, wu, name=f"{tag}_up")
    nblk, s, f = gate.shape
    act = _rows_call(_swiglu_act, [(gate.reshape(nblk * s, f), 0, f), (up.reshape(nblk * s, f), 0, f)], [], [(f, BF16)],
                     tm=512, name=f"{tag}_act")[0].reshape(nblk, s, f)
    x_new = _mm(act, wd, sum_blocks=True, res=x, alpha=0.5, name=f"{tag}_down")
    return x_new, (x, h, gate, up, act)


def _ffn_bwd(dx_new, dx_new_bf, saved, gain, wg, wu, wd, tag):
    x, h, gate, up, act = saved
    nblk, s, f = gate.shape
    d_wd = _mm(act, dx_new_bf, ta=True, alpha=0.5, name=f"{tag}_down_dw")
    dact = _mm(dx_new_bf, wd, tb=True, alpha=0.5, name=f"{tag}_down_dx")

    def act_bwd(gt, ut, ct):
        _, vjp = jax.vjp(_swiglu_act, gt, ut)
        return vjp(ct)

    dgate, dup = _rows_call(act_bwd, [(t.reshape(nblk * s, f), 0, f) for t in (gate, up, dact)], [], [(f, BF16)] * 2,
                            tm=512, name=f"{tag}_act_bwd")
    dgate, dup = dgate.reshape(nblk, s, f), dup.reshape(nblk, s, f)
    d_wg = _mm(h, dgate, ta=True, name=f"{tag}_gate_dw")
    d_wu = _mm(h, dup, ta=True, name=f"{tag}_up_dw")
    dh = _mm(dgate, wg, tb=True, sum_blocks=True, name=f"{tag}_gate_dx")
    dh = _mm(dup, wu, tb=True, sum_blocks=True, res=dh, name=f"{tag}_up_dx")
    dx, dx_bf, dgain = _norm_bwd(x, gain, dh, dx_new, f"{tag}_norm_bwd")
    return dx, dx_bf, dgain, d_wg, d_wu, d_wd


def _norm_bwd(x, gain, dh, dres, name):
    def f(xt, dht, drt, gt):
        _, vjp = jax.vjp(_rms, xt, gt)
        dxt, dgt = vjp(dht)
        return dxt + drt, dxt + drt, dgt

    return _rows_call(f, [(x, 0, D_MODEL), (dh, 0, D_MODEL), (dres, 0, D_MODEL)], [gain], [(D_MODEL, F32), (D_MODEL, BF16)],
                      [(1, D_MODEL)], tm=256, name=name)


def kernel(x, p, positions, ffn1_norm, ffn1_w_gate, ffn1_w_up, ffn1_w_down, mix_norm, w_in, rwkv_mu, rwkv_w0, rwkv_w2, rwkv_a0, rwkv_a2, rwkv_g2, rwkv_k_k, rwkv_k_a, rwkv_r_k, rwkv_gn_w, rwkv_gn_b, q_norm, k_norm, w_br_rwkv, w_br_attn, w_out, ffn2_norm, ffn2_w_gate, ffn2_w_up, ffn2_w_down, ple_norm, ple_w_gate, ple_w_proj, loss_target, m_ffn1_norm, m_ffn1_w_gate, m_ffn1_w_up, m_ffn1_w_down, m_mix_norm, m_w_in, m_rwkv_mu, m_rwkv_w0, m_rwkv_w2, m_rwkv_a0, m_rwkv_a2, m_rwkv_g2, m_rwkv_k_k, m_rwkv_k_a, m_rwkv_r_k, m_rwkv_gn_w, m_rwkv_gn_b, m_q_norm, m_k_norm, m_w_br_rwkv, m_w_br_attn, m_w_out, m_ffn2_norm, m_ffn2_w_gate, m_ffn2_w_up, m_ffn2_w_down, m_ple_norm, m_ple_w_gate, m_ple_w_proj, v_ffn1_norm, v_ffn1_w_gate, v_ffn1_w_up, v_ffn1_w_down, v_mix_norm, v_w_in, v_rwkv_mu, v_rwkv_w0, v_rwkv_w2, v_rwkv_a0, v_rwkv_a2, v_rwkv_g2, v_rwkv_k_k, v_rwkv_k_a, v_rwkv_r_k, v_rwkv_gn_w, v_rwkv_gn_b, v_q_norm, v_k_norm, v_w_br_rwkv, v_w_br_attn, v_w_out, v_ffn2_norm, v_ffn2_w_gate, v_ffn2_w_up, v_ffn2_w_down, v_ple_norm, v_ple_w_gate, v_ple_w_proj):
    args = dict(locals())
    wts = {n: args[n] for n in WEIGHTS}
    mom_m = {n: args["m_" + n] for n in WEIGHTS}
    mom_v = {n: args["v_" + n] for n in WEIGHTS}
    x0, tgt = x[0], loss_target[0]
    s = x0.shape[0]
    p_tok = p[0, 0]

    gathered = _gather_weights(_pack_big({n: wts[n][0] for n, _, _ in BIG}).astype(BF16))
    wb = _unpack_big(gathered)
    w_in_all = _whole(wb["w_in"], "col")
    w_in_r, w_in_a, w_in_g = w_in_all[:, :RWKV_COLS], w_in_all[:, RWKV_COLS:RWKV_COLS + ATTN_COLS], w_in_all[:, RWKV_COLS + ATTN_COLS:]
    w2, a2, g2 = (_whole(wb[n], "col") for n in ("rwkv_w2", "rwkv_a2", "rwkv_g2"))
    w_brr, w_bra, w_pp = (_whole(wb[n], "col") for n in ("w_br_rwkv", "w_br_attn", "ple_w_proj"))
    w_o, w_pg = _whole(wb["w_out"], "row"), _whole(wb["ple_w_gate"], "row")
    vec = {n: wts[n].reshape(1, -1) for n, _ in SMALL}

    inv_freq = 1.0 / (ROPE_THETA ** (jnp.arange(0, HEAD, 2, dtype=F32) / HEAD))
    ang = positions[0].astype(F32)[:, None] * inv_freq
    cos, sin = jnp.cos(ang), jnp.sin(ang)
    cos2, sin2 = jnp.concatenate([cos, cos], axis=1), jnp.concatenate([-sin, sin], axis=1)

    x1, ffn1_saved = _ffn_fwd(x0, vec["ffn1_norm"], wb["ffn1_w_gate"], wb["ffn1_w_up"], wb["ffn1_w_down"], "ffn1")
    h = _rows_call(_rms, [(x1, 0, D_MODEL)], [vec["mix_norm"]], [(D_MODEL, BF16)], tm=256, name="mix_norm")[0]
    z_r = _mm(h, w_in_r, name="in_rwkv")
    z_a = _mm(h, w_in_a, name="in_attn")
    z_g = _mm(h, w_in_g, name="in_gate")

    zs = _shift_fwd(z_r, vec["rwkv_mu"])
    pre_params = [vec["rwkv_w0"], w2, vec["rwkv_a0"], a2, g2, vec["rwkv_k_k"], vec["rwkv_k_a"]]
    def pre_fwd(*t):
        res = _rwkv_pre(*t)
        return res[1], res[2], res[4], res[5], res[6]

    lw, k2, na, kb, gate_r = _rows_call(pre_fwd, [(zs, 0, RWKV_COLS)], pre_params, [(RWKV_DIM, F32)] * 5, tm=256, name="rwkv_pre")
    y_scan, s0s = _wkv_fwd(zs, lw, k2, na, kb)
    post_params = [vec["rwkv_gn_w"], vec["rwkv_gn_b"], vec["rwkv_r_k"]]
    post_rows = [(y_scan, 0, RWKV_DIM), (zs, 0, RWKV_DIM), (k2, 0, RWKV_DIM), (zs, 2, RWKV_DIM), (gate_r, 0, RWKV_DIM)]
    y_rwkv = _rows_call(_rwkv_post, post_rows, post_params, [(RWKV_DIM, BF16)], tm=256, name="rwkv_post")[0]

    def qk_fwd(qt, kt, ct, st, qg, kg):
        return _norm_rope(qt, qg, ct, st), _norm_rope(kt, kg, ct, st)

    qk_rows = [(z_a, 0, ATTN_DIM), (z_a, 1, ATTN_DIM), (cos2, 0, HEAD), (sin2, 0, HEAD)]
    q_rot, k_rot = _rows_call(qk_fwd, qk_rows, [vec["q_norm"], vec["k_norm"]], [(ATTN_DIM, BF16)] * 2, tm=256, name="attn_pre")
    outs, lses = zip(*[_attn_fwd(q_rot, k_rot, z_a, g, d) for g, d in enumerate(ATTN_DILATIONS)])
    comb_rows = [(t, 0, GROUP_DIM) for t in outs + lses]
    y_attn = _rows_call(_attn_combine, comb_rows, [], [(GROUP_DIM, BF16)], tm=256, name="attn_combine")[0]

    br = _mm(y_rwkv, w_brr, name="branch_rwkv")
    ba = _mm(y_attn, w_bra, name="branch_attn")
    merge_rows = [(z_g, 0, D_MODEL), (z_g, 1, D_MODEL), (br, 0, D_MODEL), (ba, 0, D_MODEL)]
    merged = _rows_call(_merge, merge_rows, [], [(D_MODEL, BF16)], tm=256, name="merge")[0]
    x2 = _mm(merged, w_o, res=x1, name="out_proj")
    x3, ffn2_saved = _ffn_fwd(x2, vec["ffn2_norm"], wb["ffn2_w_gate"], wb["ffn2_w_up"], wb["ffn2_w_down"], "ffn2")
    hp = _rows_call(_rms, [(x3, 0, D_MODEL)], [vec["ple_norm"]], [(D_MODEL, BF16)], tm=256, name="ple_norm")[0]
    pg = _mm(hp, w_pg, name="ple_gate")
    pp = _mm(p_tok, w_pp, name="ple_proj")

    def head(x3t, pgt, ppt, tt):
        sg = _sigmoid(pgt)
        err = x3t + sg * ppt - tt
        dx4 = err * (1.0 / D_MODEL)
        loss = 0.5 * jnp.sum(jnp.mean(err * err, axis=-1, keepdims=True), axis=0, keepdims=True)
        return dx4, dx4 * ppt * sg * (1.0 - sg), dx4 * sg, jnp.broadcast_to(loss, (8, 128))

    head_rows = [(x3, 0, D_MODEL), (pg, 0, D_MODEL), (pp, 0, D_MODEL), (tgt, 0, D_MODEL)]
    dx4, dpg, dpp, loss_tile = _rows_call(head, head_rows, [], [(D_MODEL, F32), (D_MODEL, BF16), (D_MODEL, BF16)], [(8, 128)],
                                          tm=256, name="ple_loss")

    gw, gs = {}, {}
    gw["ple_w_proj"] = _mm(p_tok, dpp, ta=True, name="ple_proj_dw")
    gw["ple_w_gate"] = _mm(hp, dpg, ta=True, name="ple_gate_dw")
    dhp = _mm(dpg, w_pg, tb=True, name="ple_gate_dx")
    dx3, dx3_bf, gs["ple_norm"] = _norm_bwd(x3, vec["ple_norm"], dhp, dx4, "ple_norm_bwd")
    dx2, dx2_bf, gs["ffn2_norm"], gw["ffn2_w_gate"], gw["ffn2_w_up"], gw["ffn2_w_down"] = _ffn_bwd(
        dx3, dx3_bf, ffn2_saved, vec["ffn2_norm"], wb["ffn2_w_gate"], wb["ffn2_w_up"], wb["ffn2_w_down"], "ffn2")
    gw["w_out"] = _mm(merged, dx2_bf, ta=True, name="out_proj_dw")
    dmerged = _mm(dx2_bf, w_o, tb=True, name="out_proj_dx")

    def merge_bwd(zgr, zga, brt, bat, ct):
        _, vjp = jax.vjp(_merge, zgr, zga, brt, bat)
        d1, d2, d3, d4 = vjp(ct)
        return jnp.concatenate([d1, d2], axis=1), d3, d4

    dz_g, dbr, dba = _rows_call(merge_bwd, merge_rows + [(dmerged, 0, D_MODEL)], [],
                                [(2 * D_MODEL, BF16), (D_MODEL, BF16), (D_MODEL, BF16)], tm=256, name="merge_bwd")
    gw["w_br_rwkv"] = _mm(y_rwkv, dbr, ta=True, name="branch_rwkv_dw")
    gw["w_br_attn"] = _mm(y_attn, dba, ta=True, name="branch_attn_dw")
    dy_rwkv = _mm(dbr, w_brr, tb=True, name="branch_rwkv_dx")
    dy_attn = _mm(dba, w_bra, tb=True, name="branch_attn_dx")

    def comb_bwd(*t):
        _, vjp = jax.vjp(_attn_combine, *t[:6])
        return vjp(t[6])

    dcomb = _rows_call(comb_bwd, comb_rows + [(dy_attn, 0, GROUP_DIM)], [], [(GROUP_DIM, F32)] * 6, tm=256, name="attn_combine_bwd")
    dqs, dks, dvs = zip(*[_attn_bwd(q_rot, k_rot, z_a, g, d, dcomb[g], dcomb[3 + g]) for g, d in enumerate(ATTN_DILATIONS)])

    def qk_bwd(qt, kt, ct, st, *rest):
        dq = jnp.concatenate(rest[0:3], axis=1)
        dk = jnp.concatenate(rest[3:6], axis=1)
        qg, kg = rest[9], rest[10]
        _, vjp = jax.vjp(lambda a_, b_, c_, d_: qk_fwd(a_, b_, ct, st, c_, d_), qt, kt, qg, kg)
        dqt, dkt, dqg, dkg = vjp((dq, dk))
        return jnp.concatenate((dqt, dkt) + tuple(rest[6:9]), axis=1), dqg, dkg

    dz_a, gs["q_norm"], gs["k_norm"] = _rows_call(
        qk_bwd, qk_rows + [(t, 0, GROUP_DIM) for t in dqs + dks + dvs], [vec["q_norm"], vec["k_norm"]],
        [(ATTN_COLS, BF16)], [(1, HEAD), (1, HEAD)], tm=256, name="attn_pre_bwd")

    def post_bwd(*t):
        _, vjp = jax.vjp(_rwkv_post, *t[:5], *t[6:])
        return vjp(t[5])

    dy_scan, dr_post, dk2_post, dv_post, dgate_r, gs["rwkv_gn_w"], gs["rwkv_gn_b"], gs["rwkv_r_k"] = _rows_call(
        post_bwd, post_rows + [(dy_rwkv, 0, RWKV_DIM)], post_params, [(RWKV_DIM, F32)] * 5, [(1, RWKV_DIM)] * 3,
        tm=256, name="rwkv_post_bwd")
    dr_s, dlw, dk2_s, dv_s, dna, dkb = _wkv_bwd(zs, lw, k2, na, kb, s0s, dy_scan)

    def pre_bwd(zt, c_r1, c_r2, c_lw, c_k1, c_k2, c_v1, c_v2, c_a, c_b, c_g, *params):
        _, vjp = jax.vjp(_rwkv_pre, zt, *params)
        return vjp((c_r1 + c_r2, c_lw, c_k1 + c_k2, c_v1 + c_v2, c_a, c_b, c_g))

    pre_cts = [dr_s, dr_post, dlw, dk2_s, dk2_post, dv_s, dv_post, dna, dkb, dgate_r]
    dzs, gs["rwkv_w0"], g_w2, gs["rwkv_a0"], g_a2, g_g2, gs["rwkv_k_k"], gs["rwkv_k_a"] = _rows_call(
        pre_bwd, [(zs, 0, RWKV_COLS)] + [(t, 0, RWKV_DIM) for t in pre_cts], pre_params, [(RWKV_COLS, F32)],
        [q.shape for q in pre_params], tm=256, name="rwkv_pre_bwd")
    dz_r, gs["rwkv_mu"] = _shift_bwd(z_r, vec["rwkv_mu"], dzs)

    g_w_in = jnp.concatenate([_mm(h, dz_r, ta=True, name="in_rwkv_dw"), _mm(h, dz_a, ta=True, name="in_attn_dw"),
                              _mm(h, dz_g, ta=True, name="in_gate_dw")], axis=1)
    dh = _mm(dz_r, w_in_r, tb=True, name="in_rwkv_dx")
    dh = _mm(dz_a, w_in_a, tb=True, res=dh, name="in_attn_dx")
    dh = _mm(dz_g, w_in_g, tb=True, res=dh, name="in_gate_dx")
    dx1, dx1_bf, gs["mix_norm"] = _norm_bwd(x1, vec["mix_norm"], dh, dx2, "mix_norm_bwd")
    dx0, _, gs["ffn1_norm"], gw["ffn1_w_gate"], gw["ffn1_w_up"], gw["ffn1_w_down"] = _ffn_bwd(
        dx1, dx1_bf, ffn1_saved, vec["ffn1_norm"], wb["ffn1_w_gate"], wb["ffn1_w_up"], wb["ffn1_w_down"], "ffn1")

    how = {n: hw for n, _, hw in BIG}
    blocks = {n: gw[n] for n in ("ffn1_w_gate", "ffn1_w_up", "ffn1_w_down", "ffn2_w_gate", "ffn2_w_up", "ffn2_w_down")}
    blocks["w_in"] = _split(g_w_in, "col")
    blocks["rwkv_w2"], blocks["rwkv_a2"], blocks["rwkv_g2"] = _split(g_w2, "col"), _split(g_a2, "col"), _split(g_g2, "col")
    for n in ("w_br_rwkv", "w_br_attn", "w_out", "ple_w_gate", "ple_w_proj"):
        blocks[n] = _split(gw[n], how[n])
    packed = _pack_big(blocks)
    rows = packed.shape[1]
    xi, yi, ci = _coords()
    c_arr = jnp.reshape(ci, (1,)).astype(jnp.int32)
    me_arr = jnp.reshape(2 * xi + yi, (1,)).astype(jnp.int32)
    from_sibling = _swap_with_sibling(packed)
    pair, pair_bf = _add_pair(packed.reshape(N_CHIPS, 2, rows // 2, PACK_COLS), from_sibling, c_arr)
    from_chips = _send_to_chips(pair_bf)
    reduced = _join_halves(_add_chips(pair, from_chips, me_arr))
    grad_big = _unpack_big(reduced)

    flat = jnp.concatenate([gs[n].reshape(-1) for n, _ in SMALL] + [loss_tile[0, 0:1]])
    small_buf = jnp.pad(flat, (0, SMALL_ROWS * PACK_COLS - flat.shape[0])).reshape(SMALL_ROWS, PACK_COLS)
    small_sum = _all_reduce_small(small_buf)
    n_small = sum(sz for _, sz in SMALL)
    loss = small_sum.reshape(-1)[n_small]
    grad_small = _unpack_small(small_sum, {n: wts[n].shape for n, _ in SMALL})

    grads, deltas, new_m, new_v = {}, {}, {}, {}
    for n, shp, _ in BIG:
        g2d = grad_big[n]
        d_, m_, v_ = _adamw(wts[n][0], g2d, mom_m[n][0], mom_v[n][0], name=f"adamw_{n}")
        grads[n], deltas[n], new_m[n], new_v[n] = g2d[None], d_[None], m_[None], v_[None]
    d_s, m_s, v_s = _adamw(_pack_small(wts), small_sum, _pack_small(mom_m), _pack_small(mom_v), name="adamw_small")
    shapes = {n: wts[n].shape for n, _ in SMALL}
    d_s, m_s, v_s = _unpack_small(d_s, shapes), _unpack_small(m_s, shapes), _unpack_small(v_s, shapes)
    for n, _ in SMALL:
        grads[n], deltas[n], new_m[n], new_v[n] = grad_small[n], d_s[n], m_s[n], v_s[n]

    return (loss, dx0[None], *[grads[n] for n in WEIGHTS], *[deltas[n] for n in WEIGHTS],
            *[new_m[n] for n in WEIGHTS], *[new_v[n] for n in WEIGHTS])
```

```python
import functools

import jax
import jax.numpy as jnp
from jax import lax
from jax.experimental import pallas as pl
from jax.experimental.pallas import tpu as pltpu

F32, BF16 = jnp.float32, jnp.bfloat16
HI = lax.Precision.HIGHEST
MESH = pl.DeviceIdType.MESH
SDS = jax.ShapeDtypeStruct

D_MODEL = 1024
HEAD = 64
RWKV_HEADS = 8
RWKV_DIM = RWKV_HEADS * HEAD
DECAY_LORA, ICLR_LORA, GATE_LORA = 64, 64, 128
GN_EPS = 64e-5
RMS_EPS = 1e-6
ATTN_DILATIONS = (1, 4, 16)
BAND = 128
ATTN_DIM = 768
GROUP_DIM = 256
ROPE_THETA = 10000.0
NEG_INF = -1e30
RWKV_COLS = 3 * RWKV_DIM + DECAY_LORA + ICLR_LORA + GATE_LORA
ATTN_COLS = 3 * ATTN_DIM
ADAM_LR, ADAM_B1, ADAM_B2, ADAM_EPS, ADAM_WD, ADAM_STEP = 0.001, 0.9, 0.999, 1e-08, 0.01, 10

WKV_CHUNK = 64
WKV_HEADS_PER_STEP = 8
N_CHIPS = 4
PACK_COLS = 1024
VMEM_LIMIT = 48 * 1024 * 1024

BIG = (
    ("ffn1_w_gate", (1024, 704), "col"), ("ffn1_w_up", (1024, 704), "col"), ("ffn1_w_down", (704, 1024), "row"),
    ("w_in", (1024, 1536), "col"), ("rwkv_w2", (64, 128), "col"), ("rwkv_a2", (64, 128), "col"),
    ("rwkv_g2", (128, 128), "col"), ("w_br_rwkv", (512, 256), "col"), ("w_br_attn", (256, 256), "col"),
    ("w_out", (256, 1024), "row"), ("ffn2_w_gate", (1024, 704), "col"), ("ffn2_w_up", (1024, 704), "col"),
    ("ffn2_w_down", (704, 1024), "row"), ("ple_w_gate", (256, 1024), "row"), ("ple_w_proj", (256, 256), "col"),
)
SMALL = (
    ("ffn1_norm", 1024), ("mix_norm", 1024), ("ffn2_norm", 1024), ("ple_norm", 1024), ("rwkv_mu", 1792),
    ("rwkv_w0", 512), ("rwkv_a0", 512), ("rwkv_k_k", 512), ("rwkv_k_a", 512), ("rwkv_r_k", 512),
    ("rwkv_gn_w", 512), ("rwkv_gn_b", 512), ("q_norm", 64), ("k_norm", 64),
)
SMALL_ROWS = 16
WEIGHTS = (
    "ffn1_norm", "ffn1_w_gate", "ffn1_w_up", "ffn1_w_down", "mix_norm", "w_in", "rwkv_mu", "rwkv_w0", "rwkv_w2",
    "rwkv_a0", "rwkv_a2", "rwkv_g2", "rwkv_k_k", "rwkv_k_a", "rwkv_r_k", "rwkv_gn_w", "rwkv_gn_b", "q_norm", "k_norm",
    "w_br_rwkv", "w_br_attn", "w_out", "ffn2_norm", "ffn2_w_gate", "ffn2_w_up", "ffn2_w_down", "ple_norm",
    "ple_w_gate", "ple_w_proj",
)


def _pick(n, cands):
    for c in cands:
        if n % c == 0:
            return c
    return n


def _mm(a, b, *, ta=False, tb=False, sum_blocks=False, out_dtype=F32, res=None, alpha=1.0, name):
    flat = a.ndim == 2 and b.ndim == 2
    a3 = a if a.ndim == 3 else a[None]
    b3 = b if b.ndim == 3 else b[None]
    na, nbb = a3.shape[0], b3.shape[0]
    nblk = max(na, nbb)
    kdim, m = (a3.shape[1], a3.shape[2]) if ta else (a3.shape[2], a3.shape[1])
    n = b3.shape[1] if tb else b3.shape[2]
    assert (b3.shape[2] if tb else b3.shape[1]) == kdim
    tm = _pick(m, (1024, 512, 256, 128))
    tn = _pick(n, (1024, 896, 768, 512, 256, 128))
    tk = kdim if kdim <= 2304 else _pick(kdim, (1024, 512, 256, 128))
    nk = kdim // tk
    direct = nk == 1 and not sum_blocks

    if sum_blocks:
        grid = (m // tm, n // tn, nblk, nk)

        def ids(i, c, j, k):
            return i, c, j, k
    else:
        grid = (nblk, m // tm, n // tn, nk)

        def ids(j, i, c, k):
            return i, c, j, k

    def amap(*g):
        i, c, j, k = ids(*g)
        jj = j if na > 1 else 0
        return (jj, k, i) if ta else (jj, i, k)

    def bmap(*g):
        i, c, j, k = ids(*g)
        jj = j if nbb > 1 else 0
        return (jj, c, k) if tb else (jj, k, c)

    if sum_blocks:
        oshape, oblk = (m, n), (tm, tn)

        def omap(*g):
            i, c, j, k = ids(*g)
            return i, c
    else:
        oshape, oblk = (nblk, m, n), (1, tm, tn)

        def omap(*g):
            i, c, j, k = ids(*g)
            return j, i, c

    dn = (((0 if ta else 1,), (1 if tb else 0,)), ((), ()))
    has_res = res is not None

    def body(*refs):
        refs = list(refs)
        acc = None if direct else refs.pop()
        if has_res:
            a_ref, b_ref, r_ref, o_ref = refs
        else:
            a_ref, b_ref, o_ref = refs

        def finish(v):
            if alpha != 1.0:
                v = v * alpha
            if has_res:
                v = v + r_ref[...].reshape(v.shape).astype(F32)
            o_ref[...] = v.reshape(o_ref.shape).astype(o_ref.dtype)

        if direct:
            finish(lax.dot_general(a_ref[0].astype(BF16), b_ref[0].astype(BF16), dn, preferred_element_type=F32))
            return
        k = pl.program_id(3)
        if sum_blocks:
            j = pl.program_id(2)
            first = jnp.logical_and(j == 0, k == 0)
            last = jnp.logical_and(j == nblk - 1, k == nk - 1)
        else:
            first, last = k == 0, k == nk - 1

        @pl.when(first)
        def _():
            acc[...] = jnp.zeros_like(acc)

        acc[...] += lax.dot_general(a_ref[0].astype(BF16), b_ref[0].astype(BF16), dn, preferred_element_type=F32)

        @pl.when(last)
        def _():
            finish(acc[...])

    in_specs = [pl.BlockSpec((1, tk, tm) if ta else (1, tm, tk), amap), pl.BlockSpec((1, tn, tk) if tb else (1, tk, tn), bmap)]
    args = [a3, b3]
    if has_res:
        res3 = res if (sum_blocks or res.ndim == 3) else res[None]
        in_specs.append(pl.BlockSpec(oblk, omap))
        args.append(res3)
    out = pl.pallas_call(
        body,
        name=name,
        grid=grid,
        in_specs=in_specs,
        out_specs=pl.BlockSpec(oblk, omap),
        out_shape=SDS(oshape, out_dtype),
        scratch_shapes=[] if direct else [pltpu.VMEM((tm, tn), F32)],
        compiler_params=pltpu.CompilerParams(
            dimension_semantics=("parallel", "parallel", "arbitrary", "arbitrary") if sum_blocks
            else ("parallel", "parallel", "parallel", "arbitrary"),
            vmem_limit_bytes=VMEM_LIMIT),
    )(*args)
    if flat and not sum_blocks:
        out = out[0]
    return out


def _rows_call(f, rows, params, outs, accs=(), *, tm, name):
    s = rows[0][0].shape[0]
    nr, npar, no = len(rows), len(params), len(outs)
    in_specs = [pl.BlockSpec((tm, w), functools.partial(lambda i, cb: (i, cb), cb=cb)) for (_, cb, w) in rows]
    in_specs += [pl.BlockSpec(p.shape, functools.partial(lambda i, nd: (0,) * nd, nd=p.ndim)) for p in params]
    out_shape = [SDS((s, w), dt) for (w, dt) in outs] + [SDS(tuple(sh), F32) for sh in accs]
    out_specs = [pl.BlockSpec((tm, w), lambda i: (i, 0)) for (w, _) in outs]
    out_specs += [pl.BlockSpec(tuple(sh), functools.partial(lambda i, nd: (0,) * nd, nd=len(sh))) for sh in accs]

    def body(*refs):
        rin, pin = refs[:nr], refs[nr:nr + npar]
        oo, ao = refs[nr + npar:nr + npar + no], refs[nr + npar + no:]
        res = f(*[r[...] for r in rin], *[p[...] for p in pin])
        if not isinstance(res, (tuple, list)):
            res = (res,)
        for o_ref, v in zip(oo, res[:no]):
            o_ref[...] = v.astype(o_ref.dtype)
        i = pl.program_id(0)
        for a_ref, v in zip(ao, res[no:]):
            @pl.when(i == 0)
            def _():
                a_ref[...] = jnp.zeros_like(a_ref)

            a_ref[...] += v.reshape(a_ref.shape)

    res = pl.pallas_call(
        body,
        name=name,
        grid=(s // tm,),
        in_specs=in_specs,
        out_specs=out_specs,
        out_shape=out_shape,
        compiler_params=pltpu.CompilerParams(dimension_semantics=("arbitrary",), vmem_limit_bytes=VMEM_LIMIT),
    )(*[r[0] for r in rows], *params)
    return res


def _mmv(a, b, mode):
    ca = 0 if mode[0] == "t" else 1
    cb = 1 if mode[1] == "t" else 0
    return lax.dot_general(a.astype(BF16), b.astype(BF16), (((ca,), (cb,)), ((), ())), preferred_element_type=F32)


@functools.partial(jax.custom_vjp, nondiff_argnums=(2,))
def _bdot(a, b, mode):
    return _mmv(a, b, mode)


def _bdot_fwd(a, b, mode):
    return _mmv(a, b, mode), (a, b)


def _bdot_bwd(mode, saved, g):
    a, b = saved
    if mode == "nn":
        return _mmv(g, b, "nt"), _mmv(a, g, "tn")
    if mode == "nt":
        return _mmv(g, b, "nn"), _mmv(g, a, "tn")
    return _mmv(b, g, "nt"), _mmv(a, g, "nn")


_bdot.defvjp(_bdot_fwd, _bdot_bwd)


def _hdot(a, b, mode="nn", precision=HI):
    ca = 0 if mode[0] == "t" else 1
    cb = 1 if mode[1] == "t" else 0
    return lax.dot_general(a, b, (((ca,), (cb,)), ((), ())), precision=precision, preferred_element_type=F32)


def _wdot(a, b, mode="nn"):
    return _hdot(a, b, mode, lax.Precision.HIGH)


def _segsum(x):
    c = x.shape[-1]
    r = lax.broadcasted_iota(jnp.int32, (c, c), 0) >> 6
    q = lax.broadcasted_iota(jnp.int32, (c, c), 1) >> 6
    return _hdot(x, jnp.where(r == q, 1.0, 0.0).astype(F32), precision=lax.Precision.HIGH)


def _sigmoid(x):
    return jax.nn.sigmoid(x)


def _softplus(x):
    return jnp.maximum(x, 0.0) + jnp.log(1.0 + jnp.exp(-jnp.abs(x)))


def _rms(x, gain):
    return x * lax.rsqrt(jnp.mean(x * x, axis=-1, keepdims=True) + RMS_EPS) * gain


def _swiglu_act(gate, up):
    return gate * _sigmoid(gate) * up


def _rwkv_pre(zs, w0, w2, a0, a2, g2, k_k, k_a):
    r, k, v = zs[:, 0:512], zs[:, 512:1024], zs[:, 1024:1536]
    lora = zs[:, 1536:1792]
    wd, ad, gd = lora[:, 0:64], lora[:, 64:128], lora[:, 128:256]
    w = -_softplus(-(w0 + _bdot(jnp.tanh(wd), w2, "nn"))) - 0.5
    a = _sigmoid(a0 + _bdot(ad, a2, "nn"))
    g = _bdot(_sigmoid(gd), g2, "nn")
    kk = k * k_k
    kk = kk * lax.rsqrt(jnp.maximum(_segsum(kk * kk), 1e-24))
    k2 = k * (1.0 + (a - 1.0) * k_a)
    return r, -jnp.exp(w), k2, v, -kk, kk * a, g


def _rwkv_post(y, r, k2, v, g, gn_w, gn_b, r_k):
    mean = _segsum(y) * (1.0 / HEAD)
    yc = y - mean
    var = _segsum(yc * yc) * (1.0 / HEAD)
    yn = yc * lax.rsqrt(var + GN_EPS) * gn_w + gn_b
    bonus = _segsum(r * k2 * r_k) * v
    return (yn + bonus) * g


def _swap_halves(x):
    lane = lax.broadcasted_iota(jnp.int32, x.shape, 1)
    return jnp.where((lane & 32) == 0, jnp.roll(x, -32, axis=1), jnp.roll(x, 32, axis=1))


def _norm_rope(x, gain, cos, sin):
    heads = x.shape[1] // HEAD
    def rep(t):
        return jnp.concatenate([t] * heads, axis=1)

    xn = x * lax.rsqrt(_segsum(x * x) * (1.0 / HEAD) + RMS_EPS) * rep(gain)
    return xn * rep(cos) + _swap_halves(xn) * rep(sin)


def _attn_combine(o0, o1, o2, l0, l1, l2):
    m = jnp.maximum(jnp.maximum(l0, l1), l2)
    e0, e1, e2 = jnp.exp(l0 - m), jnp.exp(l1 - m), jnp.exp(l2 - m)
    return (e0 * o0 + e1 * o1 + e2 * o2) / (e0 + e1 + e2)


def _merge(zgr, zga, br, ba):
    return _sigmoid(zgr) * br + _sigmoid(zga) * ba


def _attn_block(q, kp, kc, vp, vc, has_prev):
    iq = lax.broadcasted_iota(jnp.int32, (BAND, BAND), 0)
    ik = lax.broadcasted_iota(jnp.int32, (BAND, BAND), 1)
    s_c = jnp.where(iq >= ik, _bdot(q, kc, "nt") * (HEAD ** -0.5), NEG_INF)
    s_p = jnp.where(jnp.logical_and(iq <= ik, has_prev), _bdot(q, kp, "nt") * (HEAD ** -0.5), NEG_INF)
    m = lax.stop_gradient(jnp.maximum(jnp.max(s_c, axis=-1, keepdims=True), jnp.max(s_p, axis=-1, keepdims=True)))
    e_c, e_p = jnp.exp(s_c - m), jnp.exp(s_p - m)
    l = jnp.sum(e_c, axis=-1, keepdims=True) + jnp.sum(e_p, axis=-1, keepdims=True)
    o = (_bdot(e_c, vc, "nn") + _bdot(e_p, vp, "nn")) / l
    return o, jnp.broadcast_to(m + jnp.log(l), o.shape)


def _bdotb(a, b, mode="nn", precision=lax.Precision.HIGH):
    if mode[0] == "t":
        a = jnp.swapaxes(a, 1, 2)
    cb = 2 if mode[1] == "t" else 1
    return lax.dot_general(a, b, (((2,), (cb,)), ((0,), (0,))), precision=precision, preferred_element_type=F32)


def _tri_inv(a):
    t = a.shape[-1]
    row = lax.broadcasted_iota(jnp.int32, (1, t, t), 1)
    col = lax.broadcasted_iota(jnp.int32, (1, t, t), 2)
    x = jnp.where(row == col, 1.0, 0.0).astype(F32) + jnp.where(jnp.logical_and(row == col + 1, (row & 1) == 1), a, 0.0)
    sh = 1
    while (1 << sh) < t:
        m = jnp.logical_and((row >> sh) == (col >> sh) + 1, (row >> (sh + 1)) == (col >> (sh + 1)))
        x = x + _bdotb(_bdotb(x, jnp.where(m, a, 0.0)), x)
        sh += 1
    return x


def _wkv_chunk(s0, r, lw, k, v, a, b):
    nh, t, _ = r.shape
    row = lax.broadcasted_iota(jnp.int32, (1, t, t), 1)
    col = lax.broadcasted_iota(jnp.int32, (1, t, t), 2)
    incl, strict = row >= col, row > col
    ones = jnp.broadcast_to(jnp.where(incl, 1.0, 0.0).astype(F32), (nh, t, t))
    cum = _bdotb(ones, lw, precision=HI)
    c_end = cum[:, t - 1:t, :]
    e_in, e_ex, e_inv = jnp.exp(cum), jnp.exp(cum - lw), jnp.exp(-cum)
    at, rt, bt, kt = a * e_ex, r * e_in, b * e_inv, k * e_inv
    a_ab = jnp.where(strict, _bdotb(at, bt, "nt"), 0.0)
    a_ak = jnp.where(strict, _bdotb(at, kt, "nt"), 0.0)
    u = _bdotb(_tri_inv(a_ab), _bdotb(at, s0, "nt") + _bdotb(a_ak, v))
    y = (_bdotb(rt, s0, "nt") + _bdotb(jnp.where(incl, _bdotb(rt, bt, "nt"), 0.0), u)
         + _bdotb(jnp.where(incl, _bdotb(rt, kt, "nt"), 0.0), v))
    w_end = jnp.exp(c_end - cum)
    s1 = s0 * jnp.exp(c_end) + _bdotb(u, b * w_end, "tn") + _bdotb(v, k * w_end, "tn")
    return y, s1


def _shift_fwd(z, mu):
    s, c = z.shape
    tc = 256

    def body(z_ref, mu_ref, o_ref):
        zz = z_ref[...]
        row = lax.broadcasted_iota(jnp.int32, zz.shape, 0)
        prev = jnp.where(row == 0, 0.0, pltpu.roll(zz, 1, 0))
        o_ref[...] = zz + (prev - zz) * mu_ref[...]

    return pl.pallas_call(
        body, name="shift_fwd", grid=(c // tc,),
        in_specs=[pl.BlockSpec((s, tc), lambda j: (0, j)), pl.BlockSpec((1, tc), lambda j: (0, j))],
        out_specs=pl.BlockSpec((s, tc), lambda j: (0, j)), out_shape=SDS((s, c), F32),
        compiler_params=pltpu.CompilerParams(dimension_semantics=("parallel",), vmem_limit_bytes=VMEM_LIMIT),
    )(z, mu)


def _shift_bwd(z, mu, dzs):
    s, c = z.shape
    tc = 256

    def body(z_ref, mu_ref, d_ref, dz_ref, dmu_ref):
        zz, d, m = z_ref[...], d_ref[...], mu_ref[...]
        row = lax.broadcasted_iota(jnp.int32, zz.shape, 0)
        prev = jnp.where(row == 0, 0.0, pltpu.roll(zz, 1, 0))
        t = d * m
        nxt = jnp.where(row == s - 1, 0.0, pltpu.roll(t, s - 1, 0))
        dz_ref[...] = (d - t + nxt).astype(dz_ref.dtype)
        dmu_ref[...] = jnp.sum(d * (prev - zz), axis=0, keepdims=True)

    return pl.pallas_call(
        body, name="shift_bwd", grid=(c // tc,),
        in_specs=[pl.BlockSpec((s, tc), lambda j: (0, j)), pl.BlockSpec((1, tc), lambda j: (0, j)),
                  pl.BlockSpec((s, tc), lambda j: (0, j))],
        out_specs=[pl.BlockSpec((s, tc), lambda j: (0, j)), pl.BlockSpec((1, tc), lambda j: (0, j))],
        out_shape=[SDS((s, c), BF16), SDS((1, c), F32)],
        compiler_params=pltpu.CompilerParams(dimension_semantics=("parallel",), vmem_limit_bytes=VMEM_LIMIT),
    )(z, mu, dzs)


def _heads(x, nh):
    return jnp.stack([x[:, h * HEAD:(h + 1) * HEAD] for h in range(nh)], axis=0)


def _unheads(x):
    return jnp.concatenate([x[h] for h in range(x.shape[0])], axis=1)


def _wkv_fwd(zs, lw, k2, na, b):
    s = lw.shape[0]
    t, hb = WKV_CHUNK, WKV_HEADS_PER_STEP
    w = hb * HEAD
    nc, ng = s // t, RWKV_HEADS // hb

    def body(r_ref, v_ref, lw_ref, k_ref, a_ref, b_ref, y_ref, s0_ref, state):
        @pl.when(pl.program_id(1) == 0)
        def _():
            state[...] = jnp.zeros_like(state)

        s0 = state[...]
        s0_ref[0] = s0
        y, s1 = _wkv_chunk(s0, *[_heads(t_ref[...], hb) for t_ref in (r_ref, lw_ref, k_ref, v_ref, a_ref, b_ref)])
        y_ref[...] = _unheads(y)
        state[...] = s1

    def col(off):
        return pl.BlockSpec((t, w), functools.partial(lambda g, i, off: (i, g + off), off=off))

    return pl.pallas_call(
        body, name="wkv_fwd", grid=(ng, nc),
        in_specs=[col(0), col(2 * ng), col(0), col(0), col(0), col(0)],
        out_specs=[col(0), pl.BlockSpec((1, hb, HEAD, HEAD), lambda g, i: (i, g, 0, 0))],
        out_shape=[SDS((s, RWKV_DIM), F32), SDS((nc, RWKV_HEADS, HEAD, HEAD), F32)],
        scratch_shapes=[pltpu.VMEM((hb, HEAD, HEAD), F32)],
        compiler_params=pltpu.CompilerParams(dimension_semantics=("parallel", "arbitrary"), vmem_limit_bytes=VMEM_LIMIT),
    )(zs, zs, lw, k2, na, b)


def _wkv_bwd(zs, lw, k2, na, b, s0s, dy):
    s = lw.shape[0]
    t, hb = WKV_CHUNK, WKV_HEADS_PER_STEP
    w = hb * HEAD
    nc, ng = s // t, RWKV_HEADS // hb

    def body(r_ref, v_ref, lw_ref, k_ref, a_ref, b_ref, s0_ref, dy_ref, dr_ref, dlw_ref, dk_ref, dv_ref, da_ref, db_ref, dstate):
        @pl.when(pl.program_id(1) == 0)
        def _():
            dstate[...] = jnp.zeros_like(dstate)

        _, vjp = jax.vjp(_wkv_chunk, s0_ref[0], *[_heads(t_ref[...], hb) for t_ref in (r_ref, lw_ref, k_ref, v_ref, a_ref, b_ref)])
        grads = vjp((_heads(dy_ref[...], hb), dstate[...]))
        dstate[...] = grads[0]
        for o_ref, gval in zip((dr_ref, dlw_ref, dk_ref, dv_ref, da_ref, db_ref), grads[1:]):
            o_ref[...] = _unheads(gval)

    def col(off):
        return pl.BlockSpec((t, w), functools.partial(lambda g, i, off: (nc - 1 - i, g + off), off=off))

    return pl.pallas_call(
        body, name="wkv_bwd", grid=(ng, nc),
        in_specs=[col(0), col(2 * ng), col(0), col(0), col(0), col(0),
                  pl.BlockSpec((1, hb, HEAD, HEAD), lambda g, i: (nc - 1 - i, g, 0, 0)), col(0)],
        out_specs=[col(0)] * 6,
        out_shape=[SDS((s, RWKV_DIM), F32)] * 6,
        scratch_shapes=[pltpu.VMEM((hb, HEAD, HEAD), F32)],
        compiler_params=pltpu.CompilerParams(dimension_semantics=("parallel", "arbitrary"), vmem_limit_bytes=VMEM_LIMIT),
    )(zs, zs, lw, k2, na, b, s0s, dy)


def _attn_fwd(q, k, z_a, g, d):
    s = q.shape[0]
    l = s // d
    nb = l // BAND
    assert nb * BAND == l
    qv, kv, zv = q.reshape(l, d * ATTN_DIM), k.reshape(l, d * ATTN_DIM), z_a.reshape(l, d * ATTN_COLS)

    def body(q_ref, kp_ref, kc_ref, vp_ref, vc_ref, o_ref, l_ref):
        has_prev = pl.program_id(1) > 0
        for h in range(GROUP_DIM // HEAD):
            sl = slice(h * HEAD, (h + 1) * HEAD)
            o, lse = _attn_block(q_ref[:, sl].astype(F32), kp_ref[:, sl].astype(F32), kc_ref[:, sl].astype(F32),
                                 vp_ref[:, sl], vc_ref[:, sl], has_prev)
            o_ref[:, sl] = o
            l_ref[:, sl] = lse

    def spec(per_tok, off, prev):
        def imap(rho, i):
            return (jnp.maximum(i - 1, 0) if prev else i, rho * per_tok + off)
        return pl.BlockSpec((BAND, GROUP_DIM), imap)

    o, lse = pl.pallas_call(
        body, name=f"attn_fwd_d{d}", grid=(d, nb),
        in_specs=[spec(3, g, False), spec(3, g, True), spec(3, g, False), spec(9, 6 + g, True), spec(9, 6 + g, False)],
        out_specs=[spec(1, 0, False), spec(1, 0, False)],
        out_shape=[SDS((l, d * GROUP_DIM), F32), SDS((l, d * GROUP_DIM), F32)],
        compiler_params=pltpu.CompilerParams(dimension_semantics=("parallel", "arbitrary"), vmem_limit_bytes=VMEM_LIMIT),
    )(qv, kv, kv, zv, zv)
    return o.reshape(s, GROUP_DIM), lse.reshape(s, GROUP_DIM)


def _attn_bwd(q, k, z_a, g, d, do, dlse):
    s = q.shape[0]
    l = s // d
    nb = l // BAND
    qv, kv, zv = q.reshape(l, d * ATTN_DIM), k.reshape(l, d * ATTN_DIM), z_a.reshape(l, d * ATTN_COLS)
    dov, dlv = do.reshape(l, d * GROUP_DIM), dlse.reshape(l, d * GROUP_DIM)

    def body(q_ref, kp_ref, kc_ref, vp_ref, vc_ref, do_ref, dl_ref, dq_ref, dk_ref, dv_ref, ck, cv):
        step = pl.program_id(1)
        has_prev = step < nb - 1

        @pl.when(step == 0)
        def _():
            ck[...] = jnp.zeros_like(ck)
            cv[...] = jnp.zeros_like(cv)

        for h in range(GROUP_DIM // HEAD):
            sl = slice(h * HEAD, (h + 1) * HEAD)
            _, vjp = jax.vjp(functools.partial(_attn_block, has_prev=has_prev), q_ref[:, sl].astype(F32),
                             kp_ref[:, sl].astype(F32), kc_ref[:, sl].astype(F32), vp_ref[:, sl], vc_ref[:, sl])
            dq, dkp, dkc, dvp, dvc = vjp((do_ref[:, sl], dl_ref[:, sl]))
            dq_ref[:, sl] = dq
            dk_ref[:, sl] = dkc + ck[:, sl]
            dv_ref[:, sl] = dvc + cv[:, sl]
            ck[:, sl] = dkp
            cv[:, sl] = dvp

    def spec(per_tok, off, prev):
        def imap(rho, i):
            blk = nb - 1 - i
            return (jnp.maximum(blk - 1, 0) if prev else blk, rho * per_tok + off)
        return pl.BlockSpec((BAND, GROUP_DIM), imap)

    dq, dk, dv = pl.pallas_call(
        body, name=f"attn_bwd_d{d}", grid=(d, nb),
        in_specs=[spec(3, g, False), spec(3, g, True), spec(3, g, False), spec(9, 6 + g, True), spec(9, 6 + g, False),
                  spec(1, 0, False), spec(1, 0, False)],
        out_specs=[spec(1, 0, False)] * 3,
        out_shape=[SDS((l, d * GROUP_DIM), F32)] * 3,
        scratch_shapes=[pltpu.VMEM((BAND, GROUP_DIM), F32), pltpu.VMEM((BAND, GROUP_DIM), F32)],
        compiler_params=pltpu.CompilerParams(dimension_semantics=("parallel", "arbitrary"), vmem_limit_bytes=VMEM_LIMIT),
    )(qv, kv, kv, zv, zv, dov, dlv)
    return dq.reshape(s, GROUP_DIM), dk.reshape(s, GROUP_DIM), dv.reshape(s, GROUP_DIM)


def _coords():
    return lax.axis_index("x"), lax.axis_index("y"), lax.axis_index("c")


_CHIP_FLIPS = ((1, 0), (0, 1), (1, 1))


def _flip(v, f):
    return 1 - v if f else v


def _gather_weights(shard):
    rows = shard.shape[0]
    half = rows // 2

    def body(src, out, send_sems, recv_sems):
        x, y, c = _coords()
        me = 2 * x + y
        mine = pl.ds(pl.multiple_of(c * half, 16), half)

        def chip_of(f):
            return _flip(x, f[0]), _flip(y, f[1])

        def over_ici(kk):
            px, py = chip_of(_CHIP_FLIPS[kk])
            return pltpu.make_async_remote_copy(src_ref=src.at[mine], dst_ref=out.at[me, mine], send_sem=send_sems.at[kk],
                                                recv_sem=recv_sems.at[kk], device_id=(px, py, c), device_id_type=MESH)

        def landed(kk):
            px, py = chip_of(_CHIP_FLIPS[kk])
            there = out.at[2 * px + py, mine]
            return pltpu.make_async_remote_copy(src_ref=there, dst_ref=there, send_sem=send_sems.at[kk],
                                                recv_sem=recv_sems.at[kk], device_id=(px, py, c), device_id_type=MESH)

        def passed_on(kk, sent_by_me):
            px, py = chip_of(_CHIP_FLIPS[kk])
            part = mine if sent_by_me else pl.ds(pl.multiple_of((1 - c) * half, 16), half)
            there = out.at[2 * px + py, part]
            return pltpu.make_async_remote_copy(src_ref=there, dst_ref=there, send_sem=send_sems.at[3 + kk],
                                                recv_sem=recv_sems.at[3 + kk], device_id=(x, y, 1 - c), device_id_type=MESH)

        sends = [over_ici(kk) for kk in range(3)]
        for cp in sends:
            cp.start()
        for kk in range(3):
            landed(kk).wait_recv()
            fwd = passed_on(kk, True)
            fwd.start()
            sends.append(fwd)
        for kk in range(3):
            passed_on(kk, False).wait_recv()
        for cp in sends:
            cp.wait_send()

    out = pl.pallas_call(
        body, name="gather_weights",
        in_specs=[pl.BlockSpec(memory_space=pl.ANY)], out_specs=pl.BlockSpec(memory_space=pl.ANY),
        out_shape=SDS((N_CHIPS, rows, PACK_COLS), shard.dtype),
        scratch_shapes=[pltpu.SemaphoreType.DMA((6,)), pltpu.SemaphoreType.DMA((6,))],
    )(shard)
    x, y, _ = _coords()
    return lax.dynamic_update_slice(out, shard[None], (2 * x + y, 0, 0))


_HBM = pl.BlockSpec(memory_space=pltpu.HBM)
_SEM = pl.BlockSpec(memory_space=pltpu.SEMAPHORE)
_EFFECT = pltpu.SideEffectType.DATAFLOW_SIDE_EFFECTING


def _copies_start(name, bufs, n_sems, issue):
    nb = len(bufs)

    def body(*refs):
        for cp in issue(refs[:nb], refs[nb], refs[nb + 1]):
            cp.start()
        refs[-1][...] = jnp.zeros_like(refs[-1])

    outs = pl.pallas_call(
        body, name=name,
        out_shape=(pltpu.SemaphoreType.DMA((n_sems,)), pltpu.SemaphoreType.DMA((n_sems,)),
                   *[pltpu.HBM(b.shape, b.dtype) for b in bufs], SDS((8, 128), F32)),
        in_specs=[_HBM] * nb, out_specs=(_SEM, _SEM, *[_HBM] * nb, pl.BlockSpec(memory_space=pltpu.VMEM)),
        input_output_aliases={i: 2 + i for i in range(nb)},
        compiler_params=pltpu.CompilerParams(has_side_effects=_EFFECT),
    )(*[pltpu.with_memory_space_constraint(b, pltpu.HBM) for b in bufs])
    return outs[0], outs[1], list(outs[2:2 + nb]), outs[-1]


def _copies_wait(name, bufs, send_sems, recv_sems, after, expect):
    nb = len(bufs)

    def body(*refs):
        sent, received = expect(refs[:nb], refs[nb], refs[nb + 1])
        for cp in sent:
            cp.wait_send()
        for cp in received:
            cp.wait_recv()

    outs = pl.pallas_call(
        body, name=name,
        out_shape=tuple(pltpu.HBM(b.shape, b.dtype) for b in bufs),
        in_specs=(*[_HBM] * nb, _SEM, _SEM, pl.BlockSpec(memory_space=pl.ANY)), out_specs=tuple([_HBM] * nb),
        input_output_aliases={i: i for i in range(nb)},
        compiler_params=pltpu.CompilerParams(has_side_effects=_EFFECT),
    )(*bufs, send_sems, recv_sems, after)
    return list(outs)


def _gather_plan(half, step):
    def parts():
        x, y, c = _coords()
        mine = pl.ds(pl.multiple_of(c * half, 16), half)
        other = pl.ds(pl.multiple_of((1 - c) * half, 16), half)
        chips = [(_flip(x, fx), _flip(y, fy)) for fx, fy in _CHIP_FLIPS]
        return x, y, c, mine, other, chips

    def copy(src, dst, sems, kk, dev):
        return pltpu.make_async_remote_copy(src_ref=src, dst_ref=dst, send_sem=sems[0].at[kk], recv_sem=sems[1].at[kk],
                                            device_id=dev, device_id_type=MESH)

    def issue(refs, send_sems, recv_sems):
        x, y, c, mine, other, chips = parts()
        sems = (send_sems, recv_sems)
        if step == "ici":
            shard, out = refs
            return [copy(shard.at[mine], out.at[2 * x + y, mine], sems, kk, (px, py, c)) for kk, (px, py) in enumerate(chips)]
        (out,) = refs
        return [copy(out.at[2 * px + py, mine], out.at[2 * px + py, mine], sems, kk, (x, y, 1 - c))
                for kk, (px, py) in enumerate(chips)]

    def expect(refs, send_sems, recv_sems):
        x, y, c, mine, other, chips = parts()
        sems = (send_sems, recv_sems)
        sent = issue(refs, send_sems, recv_sems)
        out = refs[-1]
        if step == "ici":
            got = [copy(out.at[2 * px + py, mine], out.at[2 * px + py, mine], sems, kk, (px, py, c))
                   for kk, (px, py) in enumerate(chips)]
        else:
            got = [copy(out.at[2 * px + py, other], out.at[2 * px + py, other], sems, kk, (x, y, 1 - c))
                   for kk, (px, py) in enumerate(chips)]
        return sent, got

    return issue, expect


def _own_shard_in(out, shard):
    x, y, _ = _coords()
    return lax.dynamic_update_slice(out, shard[None], (2 * x + y, 0, 0))


def _swap_with_sibling(g):
    n, rows, cols = g.shape
    half = rows // 2

    def body(src, out, send_sem, recv_sem):
        x, y, c = _coords()
        theirs = pl.ds(pl.multiple_of((1 - c) * half, 8), half)
        cp = pltpu.make_async_remote_copy(src_ref=src.at[:, theirs], dst_ref=out, send_sem=send_sem, recv_sem=recv_sem,
                                          device_id=(x, y, 1 - c), device_id_type=MESH)
        cp.start()
        cp.wait()

    return pl.pallas_call(
        body, name="grad_swap_sibling",
        in_specs=[pl.BlockSpec(memory_space=pl.ANY)], out_specs=pl.BlockSpec(memory_space=pl.ANY),
        out_shape=SDS((n, half, cols), g.dtype),
        scratch_shapes=[pltpu.SemaphoreType.DMA, pltpu.SemaphoreType.DMA],
    )(g)


def _send_to_chips(a):
    n, half, cols = a.shape

    def body(src, out, send_sems, recv_sems):
        x, y, c = _coords()
        me = 2 * x + y
        sends = []
        for kk, f in enumerate(_CHIP_FLIPS):
            px, py = _flip(x, f[0]), _flip(y, f[1])
            cp = pltpu.make_async_remote_copy(src_ref=src.at[2 * px + py], dst_ref=out.at[kk], send_sem=send_sems.at[kk],
                                              recv_sem=recv_sems.at[kk], device_id=(px, py, c), device_id_type=MESH)
            cp.start()
            sends.append(cp)
        for cp in sends:
            cp.wait()

    return pl.pallas_call(
        body, name="grad_send_chips",
        in_specs=[pl.BlockSpec(memory_space=pl.ANY)], out_specs=pl.BlockSpec(memory_space=pl.ANY),
        out_shape=SDS((3, half, cols), a.dtype),
        scratch_shapes=[pltpu.SemaphoreType.DMA((3,)), pltpu.SemaphoreType.DMA((3,))],
    )(a)


def _join_halves(r):
    half, cols = r.shape

    def body(src, out, send_sem, recv_sem):
        x, y, c = _coords()
        mine = pl.ds(pl.multiple_of(c * half, 8), half)
        cp = pltpu.make_async_remote_copy(src_ref=src, dst_ref=out.at[mine], send_sem=send_sem, recv_sem=recv_sem,
                                          device_id=(x, y, 1 - c), device_id_type=MESH)
        cp.start()
        theirs = out.at[pl.ds(pl.multiple_of((1 - c) * half, 8), half)]
        pltpu.make_async_remote_copy(src_ref=theirs, dst_ref=theirs, send_sem=send_sem, recv_sem=recv_sem,
                                     device_id=(x, y, 1 - c), device_id_type=MESH).wait_recv()
        cp.wait_send()

    out = pl.pallas_call(
        body, name="grad_join_halves",
        in_specs=[pl.BlockSpec(memory_space=pl.ANY)], out_specs=pl.BlockSpec(memory_space=pl.ANY),
        out_shape=SDS((2 * half, cols), r.dtype),
        scratch_shapes=[pltpu.SemaphoreType.DMA, pltpu.SemaphoreType.DMA],
    )(r)
    return lax.dynamic_update_slice(out, r, (lax.axis_index("c") * half, 0))


def _add_pair(g, recv, c_arr):
    n, _, half, cols = g.shape
    tr = _pick(half, (656, 328, 8))

    def body(c_ref, g_ref, r_ref, o_ref, ob_ref):
        v = g_ref[:, 0] + r_ref[...]
        o_ref[...] = v
        ob_ref[...] = v.astype(BF16)

    return pl.pallas_call(
        body, name="grad_add_pair",
        grid_spec=pltpu.PrefetchScalarGridSpec(
            num_scalar_prefetch=1, grid=(n, half // tr),
            in_specs=[pl.BlockSpec((1, 1, tr, cols), lambda j, i, c_ref: (j, c_ref[0], i, 0)),
                      pl.BlockSpec((1, tr, cols), lambda j, i, c_ref: (j, i, 0))],
            out_specs=[pl.BlockSpec((1, tr, cols), lambda j, i, c_ref: (j, i, 0))] * 2),
        out_shape=[SDS((n, half, cols), F32), SDS((n, half, cols), BF16)],
        compiler_params=pltpu.CompilerParams(dimension_semantics=("parallel", "parallel"), vmem_limit_bytes=VMEM_LIMIT),
    )(c_arr, g, recv)


def _add_chips(a, recv, me_arr):
    n, half, cols = a.shape
    tr = _pick(half, (656, 328, 8))

    def body(me_ref, a_ref, r_ref, o_ref):
        o_ref[...] = ((a_ref[0] + r_ref[0].astype(F32)) + r_ref[1].astype(F32)) + r_ref[2].astype(F32)

    return pl.pallas_call(
        body, name="grad_add_chips",
        grid_spec=pltpu.PrefetchScalarGridSpec(
            num_scalar_prefetch=1, grid=(half // tr,),
            in_specs=[pl.BlockSpec((1, tr, cols), lambda i, me_ref: (me_ref[0], i, 0)),
                      pl.BlockSpec((3, tr, cols), lambda i, me_ref: (0, i, 0))],
            out_specs=pl.BlockSpec((tr, cols), lambda i, me_ref: (i, 0))),
        out_shape=SDS((half, cols), F32),
        compiler_params=pltpu.CompilerParams(dimension_semantics=("parallel",), vmem_limit_bytes=VMEM_LIMIT),
    )(me_arr, a, recv)


def _all_reduce_small(buf):
    rows, cols = buf.shape

    def body(x_ref, o_ref, gath, send_sems, recv_sems):
        x, y, c = _coords()
        me = 4 * x + 2 * y + c
        gath[me] = x_ref[...]
        sends = []
        for kk in range(1, 8):
            f = (kk >> 2) & 1, (kk >> 1) & 1, kk & 1
            px, py, pc = _flip(x, f[0]), _flip(y, f[1]), _flip(c, f[2])
            cp = pltpu.make_async_remote_copy(src_ref=x_ref, dst_ref=gath.at[me], send_sem=send_sems.at[kk - 1],
                                              recv_sem=recv_sems.at[kk - 1], device_id=(px, py, pc), device_id_type=MESH)
            cp.start()
            sends.append(cp)
        for kk in range(1, 8):
            f = (kk >> 2) & 1, (kk >> 1) & 1, kk & 1
            px, py, pc = _flip(x, f[0]), _flip(y, f[1]), _flip(c, f[2])
            there = gath.at[4 * px + 2 * py + pc]
            pltpu.make_async_remote_copy(src_ref=there, dst_ref=there, send_sem=send_sems.at[kk - 1],
                                         recv_sem=recv_sems.at[kk - 1], device_id=(px, py, pc), device_id_type=MESH).wait_recv()
        for cp in sends:
            cp.wait_send()
        acc = gath[0]
        for j in range(1, 8):
            acc = acc + gath[j]
        o_ref[...] = acc

    return pl.pallas_call(
        body, name="all_reduce_small",
        in_specs=[pl.BlockSpec(memory_space=pltpu.VMEM)], out_specs=pl.BlockSpec(memory_space=pltpu.VMEM),
        out_shape=SDS((rows, cols), F32),
        scratch_shapes=[pltpu.VMEM((8, rows, cols), F32), pltpu.SemaphoreType.DMA((7,)), pltpu.SemaphoreType.DMA((7,))],
    )(buf)


def _adamw_rows(w, g, m, v):
    m = ADAM_B1 * m + (1.0 - ADAM_B1) * g
    v = ADAM_B2 * v + (1.0 - ADAM_B2) * jnp.square(g)
    m_hat = m / (1.0 - ADAM_B1 ** ADAM_STEP)
    v_hat = v / (1.0 - ADAM_B2 ** ADAM_STEP)
    return -ADAM_LR * (m_hat / (jnp.sqrt(v_hat) + ADAM_EPS) + ADAM_WD * w), m, v


def _adamw(w, g, m, v, name):
    rows, cols = w.shape
    tm = _pick(rows, (256, 128, 64, 16, 8))
    return _rows_call(_adamw_rows, [(t, 0, cols) for t in (w, g, m, v)], [], [(cols, F32)] * 3, tm=tm, name=name)


def _pack_big(parts, which=BIG):
    return jnp.concatenate([parts[n].reshape(parts[n].shape[:-2] + (-1, PACK_COLS)) for n, _, _ in which], axis=-2)


def _unpack_big(buf, which=BIG):
    out, off = {}, 0
    for n, shp, _ in which:
        r = shp[0] * shp[1] // PACK_COLS
        out[n] = buf[..., off:off + r, :].reshape(buf.shape[:-2] + shp)
        off += r
    return out


def _pack_small(parts):
    flat = jnp.concatenate([parts[n].reshape(-1) for n, _ in SMALL])
    return jnp.pad(flat, (0, SMALL_ROWS * PACK_COLS - flat.shape[0])).reshape(SMALL_ROWS, PACK_COLS)


def _unpack_small(buf, shapes):
    flat, out, off = buf.reshape(-1), {}, 0
    for n, sz in SMALL:
        out[n] = flat[off:off + sz].reshape(shapes[n])
        off += sz
    return out


def _whole(blocks, how):
    n, r, c = blocks.shape
    if how == "row":
        return blocks.reshape(n * r, c)
    return blocks.transpose(1, 0, 2).reshape(r, n * c)


def _split(whole, how):
    if how == "row":
        return whole.reshape(N_CHIPS, whole.shape[0] // N_CHIPS, whole.shape[1])
    r, c = whole.shape
    return whole.reshape(r, N_CHIPS, c // N_CHIPS).transpose(1, 0, 2)


def _ffn_fwd(x, gain, wg, wu, wd, tag):
    h = _rows_call(_rms, [(x, 0, D_MODEL)], [gain], [(D_MODEL, BF16)], tm=256, name=f"{tag}_norm")[0]
    gate = _mm(h, wg, name=f"{tag}_gate")
    up = _mm(h, wu, name=f"{tag}_up")
    nblk, s, f = gate.shape
    act = _rows_call(_swiglu_act, [(gate.reshape(nblk * s, f), 0, f), (up.reshape(nblk * s, f), 0, f)], [], [(f, BF16)],
                     tm=512, name=f"{tag}_act")[0].reshape(nblk, s, f)
    x_new = _mm(act, wd, sum_blocks=True, res=x, alpha=0.5, name=f"{tag}_down")
    return x_new, (x, h, gate, up, act)


def _ffn_bwd(dx_new, dx_new_bf, saved, gain, wg, wu, wd, tag):
    x, h, gate, up, act = saved
    nblk, s, f = gate.shape
    d_wd = _mm(act, dx_new_bf, ta=True, alpha=0.5, name=f"{tag}_down_dw")
    dact = _mm(dx_new_bf, wd, tb=True, alpha=0.5, name=f"{tag}_down_dx")

    def act_bwd(gt, ut, ct):
        _, vjp = jax.vjp(_swiglu_act, gt, ut)
        return vjp(ct)

    dgate, dup = _rows_call(act_bwd, [(t.reshape(nblk * s, f), 0, f) for t in (gate, up, dact)], [], [(f, BF16)] * 2,
                            tm=512, name=f"{tag}_act_bwd")
    dgate, dup = dgate.reshape(nblk, s, f), dup.reshape(nblk, s, f)
    d_wg = _mm(h, dgate, ta=True, name=f"{tag}_gate_dw")
    d_wu = _mm(h, dup, ta=True, name=f"{tag}_up_dw")
    dh = _mm(dgate, wg, tb=True, sum_blocks=True, name=f"{tag}_gate_dx")
    dh = _mm(dup, wu, tb=True, sum_blocks=True, res=dh, name=f"{tag}_up_dx")
    dx, dx_bf, dgain = _norm_bwd(x, gain, dh, dx_new, f"{tag}_norm_bwd")
    return dx, dx_bf, dgain, d_wg, d_wu, d_wd


def _norm_bwd(x, gain, dh, dres, name):
    def f(xt, dht, drt, gt):
        _, vjp = jax.vjp(_rms, xt, gt)
        dxt, dgt = vjp(dht)
        return dxt + drt, dxt + drt, dgt

    return _rows_call(f, [(x, 0, D_MODEL), (dh, 0, D_MODEL), (dres, 0, D_MODEL)], [gain], [(D_MODEL, F32), (D_MODEL, BF16)],
                      [(1, D_MODEL)], tm=256, name=name)


def kernel(x, p, positions, ffn1_norm, ffn1_w_gate, ffn1_w_up, ffn1_w_down, mix_norm, w_in, rwkv_mu, rwkv_w0, rwkv_w2, rwkv_a0, rwkv_a2, rwkv_g2, rwkv_k_k, rwkv_k_a, rwkv_r_k, rwkv_gn_w, rwkv_gn_b, q_norm, k_norm, w_br_rwkv, w_br_attn, w_out, ffn2_norm, ffn2_w_gate, ffn2_w_up, ffn2_w_down, ple_norm, ple_w_gate, ple_w_proj, loss_target, m_ffn1_norm, m_ffn1_w_gate, m_ffn1_w_up, m_ffn1_w_down, m_mix_norm, m_w_in, m_rwkv_mu, m_rwkv_w0, m_rwkv_w2, m_rwkv_a0, m_rwkv_a2, m_rwkv_g2, m_rwkv_k_k, m_rwkv_k_a, m_rwkv_r_k, m_rwkv_gn_w, m_rwkv_gn_b, m_q_norm, m_k_norm, m_w_br_rwkv, m_w_br_attn, m_w_out, m_ffn2_norm, m_ffn2_w_gate, m_ffn2_w_up, m_ffn2_w_down, m_ple_norm, m_ple_w_gate, m_ple_w_proj, v_ffn1_norm, v_ffn1_w_gate, v_ffn1_w_up, v_ffn1_w_down, v_mix_norm, v_w_in, v_rwkv_mu, v_rwkv_w0, v_rwkv_w2, v_rwkv_a0, v_rwkv_a2, v_rwkv_g2, v_rwkv_k_k, v_rwkv_k_a, v_rwkv_r_k, v_rwkv_gn_w, v_rwkv_gn_b, v_q_norm, v_k_norm, v_w_br_rwkv, v_w_br_attn, v_w_out, v_ffn2_norm, v_ffn2_w_gate, v_ffn2_w_up, v_ffn2_w_down, v_ple_norm, v_ple_w_gate, v_ple_w_proj):
    args = dict(locals())
    wts = {n: args[n] for n in WEIGHTS}
    mom_m = {n: args["m_" + n] for n in WEIGHTS}
    mom_v = {n: args["v_" + n] for n in WEIGHTS}
    x0, tgt = x[0], loss_target[0]
    s = x0.shape[0]
    p_tok = p[0, 0]

    vec = {n: wts[n].reshape(1, -1) for n, _ in SMALL}
    groups = {"f1": BIG[0:3], "mx": BIG[3:10], "f2": BIG[10:15]}
    shard = {g: _pack_big({n: wts[n][0] for n, _, _ in grp}, grp).astype(BF16) for g, grp in groups.items()}
    plans = {(g, st): _gather_plan(shard[g].shape[0] // 2, st) for g in ("mx", "f2") for st in ("ici", "d2d")}

    def landing(g):
        return lax.empty((N_CHIPS,) + shard[g].shape, BF16)

    wb = _unpack_big(_gather_weights(shard["f1"]), groups["f1"])
    ss_a, rs_a, (sh_mx, out_mx), tok_a = _copies_start("gather_mx_ici", [shard["mx"], landing("mx")], 3, plans["mx", "ici"][0])

    inv_freq = 1.0 / (ROPE_THETA ** (jnp.arange(0, HEAD, 2, dtype=F32) / HEAD))
    ang = positions[0].astype(F32)[:, None] * inv_freq
    cos, sin = jnp.cos(ang), jnp.sin(ang)
    cos2, sin2 = jnp.concatenate([cos, cos], axis=1), jnp.concatenate([-sin, sin], axis=1)

    x1, ffn1_saved = _ffn_fwd(x0, vec["ffn1_norm"] + tok_a[0, 0], wb["ffn1_w_gate"], wb["ffn1_w_up"], wb["ffn1_w_down"], "ffn1")
    sh_mx, out_mx = _copies_wait("gather_mx_ici_wait", [sh_mx, out_mx], ss_a, rs_a, x1, plans["mx", "ici"][1])
    ss_b, rs_b, (out_mx,), tok_b = _copies_start("gather_mx_d2d", [out_mx], 3, plans["mx", "d2d"][0])
    ss_c, rs_c, (sh_f2, out_f2), tok_c = _copies_start("gather_f2_ici", [shard["f2"], landing("f2")], 3, plans["f2", "ici"][0])
    h = _rows_call(_rms, [(x1, 0, D_MODEL)], [vec["mix_norm"] + (tok_b[0, 0] + tok_c[0, 0])], [(D_MODEL, BF16)], tm=256,
                   name="mix_norm")[0]
    (out_mx,) = _copies_wait("gather_mx_d2d_wait", [out_mx], ss_b, rs_b, h, plans["mx", "d2d"][1])
    wb.update(_unpack_big(_own_shard_in(out_mx, sh_mx), groups["mx"]))
    w_in_all = _whole(wb["w_in"], "col")
    w_in_r, w_in_a, w_in_g = w_in_all[:, :RWKV_COLS], w_in_all[:, RWKV_COLS:RWKV_COLS + ATTN_COLS], w_in_all[:, RWKV_COLS + ATTN_COLS:]
    w2, a2, g2 = (_whole(wb[n], "col") for n in ("rwkv_w2", "rwkv_a2", "rwkv_g2"))
    w_brr, w_bra = _whole(wb["w_br_rwkv"], "col"), _whole(wb["w_br_attn"], "col")
    w_o = _whole(wb["w_out"], "row")
    z_r = _mm(h, w_in_r, name="in_rwkv")
    z_a = _mm(h, w_in_a, name="in_attn")
    z_g = _mm(h, w_in_g, name="in_gate")

    zs = _shift_fwd(z_r, vec["rwkv_mu"])
    pre_params = [vec["rwkv_w0"], w2, vec["rwkv_a0"], a2, g2, vec["rwkv_k_k"], vec["rwkv_k_a"]]
    def pre_fwd(*t):
        res = _rwkv_pre(*t)
        return res[1], res[2], res[4], res[5], res[6]

    lw, k2, na, kb, gate_r = _rows_call(pre_fwd, [(zs, 0, RWKV_COLS)], pre_params, [(RWKV_DIM, F32)] * 5, tm=256, name="rwkv_pre")
    y_scan, s0s = _wkv_fwd(zs, lw, k2, na, kb)
    sh_f2, out_f2 = _copies_wait("gather_f2_ici_wait", [sh_f2, out_f2], ss_c, rs_c, y_scan, plans["f2", "ici"][1])
    ss_d, rs_d, (out_f2,), tok_d = _copies_start("gather_f2_d2d", [out_f2], 3, plans["f2", "d2d"][0])
    post_params = [vec["rwkv_gn_w"] + tok_d[0, 0], vec["rwkv_gn_b"], vec["rwkv_r_k"]]
    post_rows = [(y_scan, 0, RWKV_DIM), (zs, 0, RWKV_DIM), (k2, 0, RWKV_DIM), (zs, 2, RWKV_DIM), (gate_r, 0, RWKV_DIM)]
    y_rwkv = _rows_call(_rwkv_post, post_rows, post_params, [(RWKV_DIM, BF16)], tm=256, name="rwkv_post")[0]

    def qk_fwd(qt, kt, ct, st, qg, kg):
        return _norm_rope(qt, qg, ct, st), _norm_rope(kt, kg, ct, st)

    qk_rows = [(z_a, 0, ATTN_DIM), (z_a, 1, ATTN_DIM), (cos2, 0, HEAD), (sin2, 0, HEAD)]
    q_rot, k_rot = _rows_call(qk_fwd, qk_rows, [vec["q_norm"], vec["k_norm"]], [(ATTN_DIM, BF16)] * 2, tm=256, name="attn_pre")
    outs, lses = zip(*[_attn_fwd(q_rot, k_rot, z_a, g, d) for g, d in enumerate(ATTN_DILATIONS)])
    comb_rows = [(t, 0, GROUP_DIM) for t in outs + lses]
    y_attn = _rows_call(_attn_combine, comb_rows, [], [(GROUP_DIM, BF16)], tm=256, name="attn_combine")[0]

    br = _mm(y_rwkv, w_brr, name="branch_rwkv")
    ba = _mm(y_attn, w_bra, name="branch_attn")
    merge_rows = [(z_g, 0, D_MODEL), (z_g, 1, D_MODEL), (br, 0, D_MODEL), (ba, 0, D_MODEL)]
    merged = _rows_call(_merge, merge_rows, [], [(D_MODEL, BF16)], tm=256, name="merge")[0]
    x2 = _mm(merged, w_o, res=x1, name="out_proj")
    (out_f2,) = _copies_wait("gather_f2_d2d_wait", [out_f2], ss_d, rs_d, x2, plans["f2", "d2d"][1])
    wb.update(_unpack_big(_own_shard_in(out_f2, sh_f2), groups["f2"]))
    w_pp, w_pg = _whole(wb["ple_w_proj"], "col"), _whole(wb["ple_w_gate"], "row")
    x3, ffn2_saved = _ffn_fwd(x2, vec["ffn2_norm"], wb["ffn2_w_gate"], wb["ffn2_w_up"], wb["ffn2_w_down"], "ffn2")
    hp = _rows_call(_rms, [(x3, 0, D_MODEL)], [vec["ple_norm"]], [(D_MODEL, BF16)], tm=256, name="ple_norm")[0]
    pg = _mm(hp, w_pg, name="ple_gate")
    pp = _mm(p_tok, w_pp, name="ple_proj")

    def head(x3t, pgt, ppt, tt):
        sg = _sigmoid(pgt)
        err = x3t + sg * ppt - tt
        dx4 = err * (1.0 / D_MODEL)
        loss = 0.5 * jnp.sum(jnp.mean(err * err, axis=-1, keepdims=True), axis=0, keepdims=True)
        return dx4, dx4 * ppt * sg * (1.0 - sg), dx4 * sg, jnp.broadcast_to(loss, (8, 128))

    head_rows = [(x3, 0, D_MODEL), (pg, 0, D_MODEL), (pp, 0, D_MODEL), (tgt, 0, D_MODEL)]
    dx4, dpg, dpp, loss_tile = _rows_call(head, head_rows, [], [(D_MODEL, F32), (D_MODEL, BF16), (D_MODEL, BF16)], [(8, 128)],
                                          tm=256, name="ple_loss")

    gw, gs = {}, {}
    gw["ple_w_proj"] = _mm(p_tok, dpp, ta=True, name="ple_proj_dw")
    gw["ple_w_gate"] = _mm(hp, dpg, ta=True, name="ple_gate_dw")
    dhp = _mm(dpg, w_pg, tb=True, name="ple_gate_dx")
    dx3, dx3_bf, gs["ple_norm"] = _norm_bwd(x3, vec["ple_norm"], dhp, dx4, "ple_norm_bwd")
    dx2, dx2_bf, gs["ffn2_norm"], gw["ffn2_w_gate"], gw["ffn2_w_up"], gw["ffn2_w_down"] = _ffn_bwd(
        dx3, dx3_bf, ffn2_saved, vec["ffn2_norm"], wb["ffn2_w_gate"], wb["ffn2_w_up"], wb["ffn2_w_down"], "ffn2")
    gw["w_out"] = _mm(merged, dx2_bf, ta=True, name="out_proj_dw")
    dmerged = _mm(dx2_bf, w_o, tb=True, name="out_proj_dx")

    def merge_bwd(zgr, zga, brt, bat, ct):
        _, vjp = jax.vjp(_merge, zgr, zga, brt, bat)
        d1, d2, d3, d4 = vjp(ct)
        return jnp.concatenate([d1, d2], axis=1), d3, d4

    dz_g, dbr, dba = _rows_call(merge_bwd, merge_rows + [(dmerged, 0, D_MODEL)], [],
                                [(2 * D_MODEL, BF16), (D_MODEL, BF16), (D_MODEL, BF16)], tm=256, name="merge_bwd")
    gw["w_br_rwkv"] = _mm(y_rwkv, dbr, ta=True, name="branch_rwkv_dw")
    gw["w_br_attn"] = _mm(y_attn, dba, ta=True, name="branch_attn_dw")
    dy_rwkv = _mm(dbr, w_brr, tb=True, name="branch_rwkv_dx")
    dy_attn = _mm(dba, w_bra, tb=True, name="branch_attn_dx")

    def comb_bwd(*t):
        _, vjp = jax.vjp(_attn_combine, *t[:6])
        return vjp(t[6])

    dcomb = _rows_call(comb_bwd, comb_rows + [(dy_attn, 0, GROUP_DIM)], [], [(GROUP_DIM, F32)] * 6, tm=256, name="attn_combine_bwd")
    dqs, dks, dvs = zip(*[_attn_bwd(q_rot, k_rot, z_a, g, d, dcomb[g], dcomb[3 + g]) for g, d in enumerate(ATTN_DILATIONS)])

    def qk_bwd(qt, kt, ct, st, *rest):
        dq = jnp.concatenate(rest[0:3], axis=1)
        dk = jnp.concatenate(rest[3:6], axis=1)
        qg, kg = rest[9], rest[10]
        _, vjp = jax.vjp(lambda a_, b_, c_, d_: qk_fwd(a_, b_, ct, st, c_, d_), qt, kt, qg, kg)
        dqt, dkt, dqg, dkg = vjp((dq, dk))
        return jnp.concatenate((dqt, dkt) + tuple(rest[6:9]), axis=1), dqg, dkg

    dz_a, gs["q_norm"], gs["k_norm"] = _rows_call(
        qk_bwd, qk_rows + [(t, 0, GROUP_DIM) for t in dqs + dks + dvs], [vec["q_norm"], vec["k_norm"]],
        [(ATTN_COLS, BF16)], [(1, HEAD), (1, HEAD)], tm=256, name="attn_pre_bwd")

    def post_bwd(*t):
        _, vjp = jax.vjp(_rwkv_post, *t[:5], *t[6:])
        return vjp(t[5])

    dy_scan, dr_post, dk2_post, dv_post, dgate_r, gs["rwkv_gn_w"], gs["rwkv_gn_b"], gs["rwkv_r_k"] = _rows_call(
        post_bwd, post_rows + [(dy_rwkv, 0, RWKV_DIM)], post_params, [(RWKV_DIM, F32)] * 5, [(1, RWKV_DIM)] * 3,
        tm=256, name="rwkv_post_bwd")
    dr_s, dlw, dk2_s, dv_s, dna, dkb = _wkv_bwd(zs, lw, k2, na, kb, s0s, dy_scan)

    def pre_bwd(zt, c_r1, c_r2, c_lw, c_k1, c_k2, c_v1, c_v2, c_a, c_b, c_g, *params):
        _, vjp = jax.vjp(_rwkv_pre, zt, *params)
        return vjp((c_r1 + c_r2, c_lw, c_k1 + c_k2, c_v1 + c_v2, c_a, c_b, c_g))

    pre_cts = [dr_s, dr_post, dlw, dk2_s, dk2_post, dv_s, dv_post, dna, dkb, dgate_r]
    dzs, gs["rwkv_w0"], g_w2, gs["rwkv_a0"], g_a2, g_g2, gs["rwkv_k_k"], gs["rwkv_k_a"] = _rows_call(
        pre_bwd, [(zs, 0, RWKV_COLS)] + [(t, 0, RWKV_DIM) for t in pre_cts], pre_params, [(RWKV_COLS, F32)],
        [q.shape for q in pre_params], tm=256, name="rwkv_pre_bwd")
    dz_r, gs["rwkv_mu"] = _shift_bwd(z_r, vec["rwkv_mu"], dzs)

    g_w_in = jnp.concatenate([_mm(h, dz_r, ta=True, name="in_rwkv_dw"), _mm(h, dz_a, ta=True, name="in_attn_dw"),
                              _mm(h, dz_g, ta=True, name="in_gate_dw")], axis=1)
    dh = _mm(dz_r, w_in_r, tb=True, name="in_rwkv_dx")
    dh = _mm(dz_a, w_in_a, tb=True, res=dh, name="in_attn_dx")
    dh = _mm(dz_g, w_in_g, tb=True, res=dh, name="in_gate_dx")
    dx1, dx1_bf, gs["mix_norm"] = _norm_bwd(x1, vec["mix_norm"], dh, dx2, "mix_norm_bwd")
    dx0, _, gs["ffn1_norm"], gw["ffn1_w_gate"], gw["ffn1_w_up"], gw["ffn1_w_down"] = _ffn_bwd(
        dx1, dx1_bf, ffn1_saved, vec["ffn1_norm"], wb["ffn1_w_gate"], wb["ffn1_w_up"], wb["ffn1_w_down"], "ffn1")

    how = {n: hw for n, _, hw in BIG}
    blocks = {n: gw[n] for n in ("ffn1_w_gate", "ffn1_w_up", "ffn1_w_down", "ffn2_w_gate", "ffn2_w_up", "ffn2_w_down")}
    blocks["w_in"] = _split(g_w_in, "col")
    blocks["rwkv_w2"], blocks["rwkv_a2"], blocks["rwkv_g2"] = _split(g_w2, "col"), _split(g_a2, "col"), _split(g_g2, "col")
    for n in ("w_br_rwkv", "w_br_attn", "w_out", "ple_w_gate", "ple_w_proj"):
        blocks[n] = _split(gw[n], how[n])
    packed = _pack_big(blocks)
    rows = packed.shape[1]
    xi, yi, ci = _coords()
    c_arr = jnp.reshape(ci, (1,)).astype(jnp.int32)
    me_arr = jnp.reshape(2 * xi + yi, (1,)).astype(jnp.int32)
    from_sibling = _swap_with_sibling(packed)
    pair, pair_bf = _add_pair(packed.reshape(N_CHIPS, 2, rows // 2, PACK_COLS), from_sibling, c_arr)
    from_chips = _send_to_chips(pair_bf)
    reduced = _join_halves(_add_chips(pair, from_chips, me_arr))
    grad_big = _unpack_big(reduced)

    flat = jnp.concatenate([gs[n].reshape(-1) for n, _ in SMALL] + [loss_tile[0, 0:1]])
    small_buf = jnp.pad(flat, (0, SMALL_ROWS * PACK_COLS - flat.shape[0])).reshape(SMALL_ROWS, PACK_COLS)
    small_sum = _all_reduce_small(small_buf)
    n_small = sum(sz for _, sz in SMALL)
    loss = small_sum.reshape(-1)[n_small]
    grad_small = _unpack_small(small_sum, {n: wts[n].shape for n, _ in SMALL})

    grads, deltas, new_m, new_v = {}, {}, {}, {}
    for n, shp, _ in BIG:
        g2d = grad_big[n]
        d_, m_, v_ = _adamw(wts[n][0], g2d, mom_m[n][0], mom_v[n][0], name=f"adamw_{n}")
        grads[n], deltas[n], new_m[n], new_v[n] = g2d[None], d_[None], m_[None], v_[None]
    d_s, m_s, v_s = _adamw(_pack_small(wts), small_sum, _pack_small(mom_m), _pack_small(mom_v), name="adamw_small")
    shapes = {n: wts[n].shape for n, _ in SMALL}
    d_s, m_s, v_s = _unpack_small(d_s, shapes), _unpack_small(m_s, shapes), _unpack_small(v_s, shapes)
    for n, _ in SMALL:
        grads[n], deltas[n], new_m[n], new_v[n] = grad_small[n], d_s[n], m_s[n], v_s[n]

    return (loss, dx0[None], *[grads[n] for n in WEIGHTS], *[deltas[n] for n in WEIGHTS],
            *[new_m[n] for n in WEIGHTS], *[new_v[n] for n in WEIGHTS])
```

```python
import functools

import jax
import jax.numpy as jnp
from jax import lax
from jax.experimental import pallas as pl
from jax.experimental.pallas import tpu as pltpu

F32, BF16 = jnp.float32, jnp.bfloat16
HI = lax.Precision.HIGHEST
MESH = pl.DeviceIdType.MESH
SDS = jax.ShapeDtypeStruct

D_MODEL = 1024
HEAD = 64
RWKV_HEADS = 8
RWKV_DIM = RWKV_HEADS * HEAD
DECAY_LORA, ICLR_LORA, GATE_LORA = 64, 64, 128
GN_EPS = 64e-5
RMS_EPS = 1e-6
ATTN_DILATIONS = (1, 4, 16)
BAND = 128
ATTN_DIM = 768
GROUP_DIM = 256
ROPE_THETA = 10000.0
NEG_INF = -1e30
RWKV_COLS = 3 * RWKV_DIM + DECAY_LORA + ICLR_LORA + GATE_LORA
ATTN_COLS = 3 * ATTN_DIM
ADAM_LR, ADAM_B1, ADAM_B2, ADAM_EPS, ADAM_WD, ADAM_STEP = 0.001, 0.9, 0.999, 1e-08, 0.01, 10

WKV_CHUNK = 64
WKV_HEADS_PER_STEP = 8
N_CHIPS = 4
PACK_COLS = 1024
VMEM_LIMIT = 48 * 1024 * 1024

BIG = (
    ("ffn1_w_gate", (1024, 704), "col"), ("ffn1_w_up", (1024, 704), "col"), ("ffn1_w_down", (704, 1024), "row"),
    ("w_in", (1024, 1536), "col"), ("rwkv_w2", (64, 128), "col"), ("rwkv_a2", (64, 128), "col"),
    ("rwkv_g2", (128, 128), "col"), ("w_br_rwkv", (512, 256), "col"), ("w_br_attn", (256, 256), "col"),
    ("w_out", (256, 1024), "row"), ("ffn2_w_gate", (1024, 704), "col"), ("ffn2_w_up", (1024, 704), "col"),
    ("ffn2_w_down", (704, 1024), "row"), ("ple_w_gate", (256, 1024), "row"), ("ple_w_proj", (256, 256), "col"),
)
SMALL = (
    ("ffn1_norm", 1024), ("mix_norm", 1024), ("ffn2_norm", 1024), ("ple_norm", 1024), ("rwkv_mu", 1792),
    ("rwkv_w0", 512), ("rwkv_a0", 512), ("rwkv_k_k", 512), ("rwkv_k_a", 512), ("rwkv_r_k", 512),
    ("rwkv_gn_w", 512), ("rwkv_gn_b", 512), ("q_norm", 64), ("k_norm", 64),
)
SMALL_ROWS = 16
WEIGHTS = (
    "ffn1_norm", "ffn1_w_gate", "ffn1_w_up", "ffn1_w_down", "mix_norm", "w_in", "rwkv_mu", "rwkv_w0", "rwkv_w2",
    "rwkv_a0", "rwkv_a2", "rwkv_g2", "rwkv_k_k", "rwkv_k_a", "rwkv_r_k", "rwkv_gn_w", "rwkv_gn_b", "q_norm", "k_norm",
    "w_br_rwkv", "w_br_attn", "w_out", "ffn2_norm", "ffn2_w_gate", "ffn2_w_up", "ffn2_w_down", "ple_norm",
    "ple_w_gate", "ple_w_proj",
)


def _pick(n, cands):
    for c in cands:
        if n % c == 0:
            return c
    return n


def _mm(a, b, *, ta=False, tb=False, sum_blocks=False, out_dtype=F32, res=None, alpha=1.0, dep=None, name):
    flat = a.ndim == 2 and b.ndim == 2
    a3 = a if a.ndim == 3 else a[None]
    b3 = b if b.ndim == 3 else b[None]
    na, nbb = a3.shape[0], b3.shape[0]
    nblk = max(na, nbb)
    kdim, m = (a3.shape[1], a3.shape[2]) if ta else (a3.shape[2], a3.shape[1])
    n = b3.shape[1] if tb else b3.shape[2]
    assert (b3.shape[2] if tb else b3.shape[1]) == kdim
    tm = _pick(m, (1024, 512, 256, 128))
    tn = _pick(n, (1024, 896, 768, 512, 256, 128))
    tk = kdim if kdim <= 2304 else _pick(kdim, (1024, 512, 256, 128))
    nk = kdim // tk
    direct = nk == 1 and not sum_blocks

    if sum_blocks:
        grid = (m // tm, n // tn, nblk, nk)

        def ids(i, c, j, k):
            return i, c, j, k
    else:
        grid = (nblk, m // tm, n // tn, nk)

        def ids(j, i, c, k):
            return i, c, j, k

    def amap(*g):
        i, c, j, k = ids(*g)
        jj = j if na > 1 else 0
        return (jj, k, i) if ta else (jj, i, k)

    def bmap(*g):
        i, c, j, k = ids(*g)
        jj = j if nbb > 1 else 0
        return (jj, c, k) if tb else (jj, k, c)

    if sum_blocks:
        oshape, oblk = (m, n), (tm, tn)

        def omap(*g):
            i, c, j, k = ids(*g)
            return i, c
    else:
        oshape, oblk = (nblk, m, n), (1, tm, tn)

        def omap(*g):
            i, c, j, k = ids(*g)
            return j, i, c

    dn = (((0 if ta else 1,), (1 if tb else 0,)), ((), ()))
    has_res = res is not None

    def body(*refs):
        refs = list(refs)
        acc = None if direct else refs.pop()
        o_ref = refs.pop()
        a_ref, b_ref = refs[0], refs[1]
        r_ref = refs[2] if has_res else None

        def finish(v):
            if alpha != 1.0:
                v = v * alpha
            if has_res:
                v = v + r_ref[...].reshape(v.shape).astype(F32)
            o_ref[...] = v.reshape(o_ref.shape).astype(o_ref.dtype)

        if direct:
            finish(lax.dot_general(a_ref[0].astype(BF16), b_ref[0].astype(BF16), dn, preferred_element_type=F32))
            return
        k = pl.program_id(3)
        if sum_blocks:
            j = pl.program_id(2)
            first = jnp.logical_and(j == 0, k == 0)
            last = jnp.logical_and(j == nblk - 1, k == nk - 1)
        else:
            first, last = k == 0, k == nk - 1

        @pl.when(first)
        def _():
            acc[...] = jnp.zeros_like(acc)

        acc[...] += lax.dot_general(a_ref[0].astype(BF16), b_ref[0].astype(BF16), dn, preferred_element_type=F32)

        @pl.when(last)
        def _():
            finish(acc[...])

    in_specs = [pl.BlockSpec((1, tk, tm) if ta else (1, tm, tk), amap), pl.BlockSpec((1, tn, tk) if tb else (1, tk, tn), bmap)]
    args = [a3, b3]
    if has_res:
        res3 = res if (sum_blocks or res.ndim == 3) else res[None]
        in_specs.append(pl.BlockSpec(oblk, omap))
        args.append(res3)
    if dep is not None:
        in_specs.append(pl.BlockSpec(memory_space=pl.ANY))
        args.append(dep)
    out = pl.pallas_call(
        body,
        name=name,
        grid=grid,
        in_specs=in_specs,
        out_specs=pl.BlockSpec(oblk, omap),
        out_shape=SDS(oshape, out_dtype),
        scratch_shapes=[] if direct else [pltpu.VMEM((tm, tn), F32)],
        compiler_params=pltpu.CompilerParams(
            dimension_semantics=("parallel", "parallel", "arbitrary", "arbitrary") if sum_blocks
            else ("parallel", "parallel", "parallel", "arbitrary"),
            vmem_limit_bytes=VMEM_LIMIT),
    )(*args)
    if flat and not sum_blocks:
        out = out[0]
    return out


def _rows_call(f, rows, params, outs, accs=(), *, tm, name, dep=None):
    s = rows[0][0].shape[0]
    nr, npar, no = len(rows), len(params), len(outs)
    nin = nr + npar + (0 if dep is None else 1)
    in_specs = [pl.BlockSpec((tm, w), functools.partial(lambda i, cb: (i, cb), cb=cb)) for (_, cb, w) in rows]
    in_specs += [pl.BlockSpec(p.shape, functools.partial(lambda i, nd: (0,) * nd, nd=p.ndim)) for p in params]
    if dep is not None:
        in_specs.append(pl.BlockSpec(memory_space=pl.ANY))
    out_shape = [SDS((s, w), dt) for (w, dt) in outs] + [SDS(tuple(sh), F32) for sh in accs]
    out_specs = [pl.BlockSpec((tm, w), lambda i: (i, 0)) for (w, _) in outs]
    out_specs += [pl.BlockSpec(tuple(sh), functools.partial(lambda i, nd: (0,) * nd, nd=len(sh))) for sh in accs]

    def body(*refs):
        rin, pin = refs[:nr], refs[nr:nr + npar]
        oo, ao = refs[nin:nin + no], refs[nin + no:]
        res = f(*[r[...] for r in rin], *[p[...] for p in pin])
        if not isinstance(res, (tuple, list)):
            res = (res,)
        for o_ref, v in zip(oo, res[:no]):
            o_ref[...] = v.astype(o_ref.dtype)
        i = pl.program_id(0)
        for a_ref, v in zip(ao, res[no:]):
            @pl.when(i == 0)
            def _():
                a_ref[...] = jnp.zeros_like(a_ref)

            a_ref[...] += v.reshape(a_ref.shape)

    res = pl.pallas_call(
        body,
        name=name,
        grid=(s // tm,),
        in_specs=in_specs,
        out_specs=out_specs,
        out_shape=out_shape,
        compiler_params=pltpu.CompilerParams(dimension_semantics=("arbitrary",), vmem_limit_bytes=VMEM_LIMIT),
    )(*[r[0] for r in rows], *params, *([] if dep is None else [dep]))
    return res


def _mmv(a, b, mode):
    ca = 0 if mode[0] == "t" else 1
    cb = 1 if mode[1] == "t" else 0
    return lax.dot_general(a.astype(BF16), b.astype(BF16), (((ca,), (cb,)), ((), ())), preferred_element_type=F32)


@functools.partial(jax.custom_vjp, nondiff_argnums=(2,))
def _bdot(a, b, mode):
    return _mmv(a, b, mode)


def _bdot_fwd(a, b, mode):
    return _mmv(a, b, mode), (a, b)


def _bdot_bwd(mode, saved, g):
    a, b = saved
    if mode == "nn":
        return _mmv(g, b, "nt"), _mmv(a, g, "tn")
    if mode == "nt":
        return _mmv(g, b, "nn"), _mmv(g, a, "tn")
    return _mmv(b, g, "nt"), _mmv(a, g, "nn")


_bdot.defvjp(_bdot_fwd, _bdot_bwd)


def _hdot(a, b, mode="nn", precision=HI):
    ca = 0 if mode[0] == "t" else 1
    cb = 1 if mode[1] == "t" else 0
    return lax.dot_general(a, b, (((ca,), (cb,)), ((), ())), precision=precision, preferred_element_type=F32)


def _wdot(a, b, mode="nn"):
    return _hdot(a, b, mode, lax.Precision.HIGH)


def _segsum(x):
    c = x.shape[-1]
    r = lax.broadcasted_iota(jnp.int32, (c, c), 0) >> 6
    q = lax.broadcasted_iota(jnp.int32, (c, c), 1) >> 6
    return _hdot(x, jnp.where(r == q, 1.0, 0.0).astype(F32), precision=lax.Precision.HIGH)


def _sigmoid(x):
    return jax.nn.sigmoid(x)


def _softplus(x):
    return jnp.maximum(x, 0.0) + jnp.log(1.0 + jnp.exp(-jnp.abs(x)))


def _rms(x, gain):
    return x * lax.rsqrt(jnp.mean(x * x, axis=-1, keepdims=True) + RMS_EPS) * gain


def _swiglu_act(gate, up):
    return gate * _sigmoid(gate) * up


def _rwkv_pre(zs, w0, w2, a0, a2, g2, k_k, k_a):
    r, k, v = zs[:, 0:512], zs[:, 512:1024], zs[:, 1024:1536]
    lora = zs[:, 1536:1792]
    wd, ad, gd = lora[:, 0:64], lora[:, 64:128], lora[:, 128:256]
    w = -_softplus(-(w0 + _bdot(jnp.tanh(wd), w2, "nn"))) - 0.5
    a = _sigmoid(a0 + _bdot(ad, a2, "nn"))
    g = _bdot(_sigmoid(gd), g2, "nn")
    kk = k * k_k
    kk = kk * lax.rsqrt(jnp.maximum(_segsum(kk * kk), 1e-24))
    k2 = k * (1.0 + (a - 1.0) * k_a)
    return r, -jnp.exp(w), k2, v, -kk, kk * a, g


def _rwkv_post(y, r, k2, v, g, gn_w, gn_b, r_k):
    mean = _segsum(y) * (1.0 / HEAD)
    yc = y - mean
    var = _segsum(yc * yc) * (1.0 / HEAD)
    yn = yc * lax.rsqrt(var + GN_EPS) * gn_w + gn_b
    bonus = _segsum(r * k2 * r_k) * v
    return (yn + bonus) * g


def _swap_halves(x):
    lane = lax.broadcasted_iota(jnp.int32, x.shape, 1)
    return jnp.where((lane & 32) == 0, jnp.roll(x, -32, axis=1), jnp.roll(x, 32, axis=1))


def _norm_rope(x, gain, cos, sin):
    heads = x.shape[1] // HEAD
    def rep(t):
        return jnp.concatenate([t] * heads, axis=1)

    xn = x * lax.rsqrt(_segsum(x * x) * (1.0 / HEAD) + RMS_EPS) * rep(gain)
    return xn * rep(cos) + _swap_halves(xn) * rep(sin)


def _attn_combine(o0, o1, o2, l0, l1, l2):
    m = jnp.maximum(jnp.maximum(l0, l1), l2)
    e0, e1, e2 = jnp.exp(l0 - m), jnp.exp(l1 - m), jnp.exp(l2 - m)
    return (e0 * o0 + e1 * o1 + e2 * o2) / (e0 + e1 + e2)


def _merge(zgr, zga, br, ba):
    return _sigmoid(zgr) * br + _sigmoid(zga) * ba


def _attn_block(q, kp, kc, vp, vc, has_prev):
    iq = lax.broadcasted_iota(jnp.int32, (BAND, BAND), 0)
    ik = lax.broadcasted_iota(jnp.int32, (BAND, BAND), 1)
    s_c = jnp.where(iq >= ik, _bdot(q, kc, "nt") * (HEAD ** -0.5), NEG_INF)
    s_p = jnp.where(jnp.logical_and(iq <= ik, has_prev), _bdot(q, kp, "nt") * (HEAD ** -0.5), NEG_INF)
    m = lax.stop_gradient(jnp.maximum(jnp.max(s_c, axis=-1, keepdims=True), jnp.max(s_p, axis=-1, keepdims=True)))
    e_c, e_p = jnp.exp(s_c - m), jnp.exp(s_p - m)
    l = jnp.sum(e_c, axis=-1, keepdims=True) + jnp.sum(e_p, axis=-1, keepdims=True)
    o = (_bdot(e_c, vc, "nn") + _bdot(e_p, vp, "nn")) / l
    return o, jnp.broadcast_to(m + jnp.log(l), o.shape)


def _bdotb(a, b, mode="nn", precision=lax.Precision.HIGH):
    if mode[0] == "t":
        a = jnp.swapaxes(a, 1, 2)
    cb = 2 if mode[1] == "t" else 1
    return lax.dot_general(a, b, (((2,), (cb,)), ((0,), (0,))), precision=precision, preferred_element_type=F32)


def _tri_inv(a):
    t = a.shape[-1]
    row = lax.broadcasted_iota(jnp.int32, (1, t, t), 1)
    col = lax.broadcasted_iota(jnp.int32, (1, t, t), 2)
    x = jnp.where(row == col, 1.0, 0.0).astype(F32) + jnp.where(jnp.logical_and(row == col + 1, (row & 1) == 1), a, 0.0)
    sh = 1
    while (1 << sh) < t:
        m = jnp.logical_and((row >> sh) == (col >> sh) + 1, (row >> (sh + 1)) == (col >> (sh + 1)))
        x = x + _bdotb(_bdotb(x, jnp.where(m, a, 0.0)), x)
        sh += 1
    return x


def _wkv_chunk(s0, r, lw, k, v, a, b):
    nh, t, _ = r.shape
    row = lax.broadcasted_iota(jnp.int32, (1, t, t), 1)
    col = lax.broadcasted_iota(jnp.int32, (1, t, t), 2)
    incl, strict = row >= col, row > col
    ones = jnp.broadcast_to(jnp.where(incl, 1.0, 0.0).astype(F32), (nh, t, t))
    cum = _bdotb(ones, lw, precision=HI)
    c_end = cum[:, t - 1:t, :]
    e_in, e_ex, e_inv = jnp.exp(cum), jnp.exp(cum - lw), jnp.exp(-cum)
    at, rt, bt, kt = a * e_ex, r * e_in, b * e_inv, k * e_inv
    a_ab = jnp.where(strict, _bdotb(at, bt, "nt"), 0.0)
    a_ak = jnp.where(strict, _bdotb(at, kt, "nt"), 0.0)
    u = _bdotb(_tri_inv(a_ab), _bdotb(at, s0, "nt") + _bdotb(a_ak, v))
    y = (_bdotb(rt, s0, "nt") + _bdotb(jnp.where(incl, _bdotb(rt, bt, "nt"), 0.0), u)
         + _bdotb(jnp.where(incl, _bdotb(rt, kt, "nt"), 0.0), v))
    w_end = jnp.exp(c_end - cum)
    s1 = s0 * jnp.exp(c_end) + _bdotb(u, b * w_end, "tn") + _bdotb(v, k * w_end, "tn")
    return y, s1


def _shift_fwd(z, mu):
    s, c = z.shape
    tc = 256

    def body(z_ref, mu_ref, o_ref):
        zz = z_ref[...]
        row = lax.broadcasted_iota(jnp.int32, zz.shape, 0)
        prev = jnp.where(row == 0, 0.0, pltpu.roll(zz, 1, 0))
        o_ref[...] = zz + (prev - zz) * mu_ref[...]

    return pl.pallas_call(
        body, name="shift_fwd", grid=(c // tc,),
        in_specs=[pl.BlockSpec((s, tc), lambda j: (0, j)), pl.BlockSpec((1, tc), lambda j: (0, j))],
        out_specs=pl.BlockSpec((s, tc), lambda j: (0, j)), out_shape=SDS((s, c), F32),
        compiler_params=pltpu.CompilerParams(dimension_semantics=("parallel",), vmem_limit_bytes=VMEM_LIMIT),
    )(z, mu)


def _shift_bwd(z, mu, dzs):
    s, c = z.shape
    tc = 256

    def body(z_ref, mu_ref, d_ref, dz_ref, dmu_ref):
        zz, d, m = z_ref[...], d_ref[...], mu_ref[...]
        row = lax.broadcasted_iota(jnp.int32, zz.shape, 0)
        prev = jnp.where(row == 0, 0.0, pltpu.roll(zz, 1, 0))
        t = d * m
        nxt = jnp.where(row == s - 1, 0.0, pltpu.roll(t, s - 1, 0))
        dz_ref[...] = (d - t + nxt).astype(dz_ref.dtype)
        dmu_ref[...] = jnp.sum(d * (prev - zz), axis=0, keepdims=True)

    return pl.pallas_call(
        body, name="shift_bwd", grid=(c // tc,),
        in_specs=[pl.BlockSpec((s, tc), lambda j: (0, j)), pl.BlockSpec((1, tc), lambda j: (0, j)),
                  pl.BlockSpec((s, tc), lambda j: (0, j))],
        out_specs=[pl.BlockSpec((s, tc), lambda j: (0, j)), pl.BlockSpec((1, tc), lambda j: (0, j))],
        out_shape=[SDS((s, c), BF16), SDS((1, c), F32)],
        compiler_params=pltpu.CompilerParams(dimension_semantics=("parallel",), vmem_limit_bytes=VMEM_LIMIT),
    )(z, mu, dzs)


def _heads(x, nh):
    return jnp.stack([x[:, h * HEAD:(h + 1) * HEAD] for h in range(nh)], axis=0)


def _unheads(x):
    return jnp.concatenate([x[h] for h in range(x.shape[0])], axis=1)


def _wkv_fwd(zs, lw, k2, na, b):
    s = lw.shape[0]
    t, hb = WKV_CHUNK, WKV_HEADS_PER_STEP
    w = hb * HEAD
    nc, ng = s // t, RWKV_HEADS // hb

    def body(r_ref, v_ref, lw_ref, k_ref, a_ref, b_ref, y_ref, s0_ref, state):
        @pl.when(pl.program_id(1) == 0)
        def _():
            state[...] = jnp.zeros_like(state)

        s0 = state[...]
        s0_ref[0] = s0
        y, s1 = _wkv_chunk(s0, *[_heads(t_ref[...], hb) for t_ref in (r_ref, lw_ref, k_ref, v_ref, a_ref, b_ref)])
        y_ref[...] = _unheads(y)
        state[...] = s1

    def col(off):
        return pl.BlockSpec((t, w), functools.partial(lambda g, i, off: (i, g + off), off=off))

    return pl.pallas_call(
        body, name="wkv_fwd", grid=(ng, nc),
        in_specs=[col(0), col(2 * ng), col(0), col(0), col(0), col(0)],
        out_specs=[col(0), pl.BlockSpec((1, hb, HEAD, HEAD), lambda g, i: (i, g, 0, 0))],
        out_shape=[SDS((s, RWKV_DIM), F32), SDS((nc, RWKV_HEADS, HEAD, HEAD), F32)],
        scratch_shapes=[pltpu.VMEM((hb, HEAD, HEAD), F32)],
        compiler_params=pltpu.CompilerParams(dimension_semantics=("parallel", "arbitrary"), vmem_limit_bytes=VMEM_LIMIT),
    )(zs, zs, lw, k2, na, b)


def _wkv_bwd(zs, lw, k2, na, b, s0s, dy):
    s = lw.shape[0]
    t, hb = WKV_CHUNK, WKV_HEADS_PER_STEP
    w = hb * HEAD
    nc, ng = s // t, RWKV_HEADS // hb

    def body(r_ref, v_ref, lw_ref, k_ref, a_ref, b_ref, s0_ref, dy_ref, dr_ref, dlw_ref, dk_ref, dv_ref, da_ref, db_ref, dstate):
        @pl.when(pl.program_id(1) == 0)
        def _():
            dstate[...] = jnp.zeros_like(dstate)

        _, vjp = jax.vjp(_wkv_chunk, s0_ref[0], *[_heads(t_ref[...], hb) for t_ref in (r_ref, lw_ref, k_ref, v_ref, a_ref, b_ref)])
        grads = vjp((_heads(dy_ref[...], hb), dstate[...]))
        dstate[...] = grads[0]
        for o_ref, gval in zip((dr_ref, dlw_ref, dk_ref, dv_ref, da_ref, db_ref), grads[1:]):
            o_ref[...] = _unheads(gval)

    def col(off):
        return pl.BlockSpec((t, w), functools.partial(lambda g, i, off: (nc - 1 - i, g + off), off=off))

    return pl.pallas_call(
        body, name="wkv_bwd", grid=(ng, nc),
        in_specs=[col(0), col(2 * ng), col(0), col(0), col(0), col(0),
                  pl.BlockSpec((1, hb, HEAD, HEAD), lambda g, i: (nc - 1 - i, g, 0, 0)), col(0)],
        out_specs=[col(0)] * 6,
        out_shape=[SDS((s, RWKV_DIM), F32)] * 6,
        scratch_shapes=[pltpu.VMEM((hb, HEAD, HEAD), F32)],
        compiler_params=pltpu.CompilerParams(dimension_semantics=("parallel", "arbitrary"), vmem_limit_bytes=VMEM_LIMIT),
    )(zs, zs, lw, k2, na, b, s0s, dy)


def _attn_fwd(q, k, z_a, g, d):
    s = q.shape[0]
    l = s // d
    nb = l // BAND
    assert nb * BAND == l
    qv, kv, zv = q.reshape(l, d * ATTN_DIM), k.reshape(l, d * ATTN_DIM), z_a.reshape(l, d * ATTN_COLS)

    def body(q_ref, kp_ref, kc_ref, vp_ref, vc_ref, o_ref, l_ref):
        has_prev = pl.program_id(1) > 0
        for h in range(GROUP_DIM // HEAD):
            sl = slice(h * HEAD, (h + 1) * HEAD)
            o, lse = _attn_block(q_ref[:, sl].astype(F32), kp_ref[:, sl].astype(F32), kc_ref[:, sl].astype(F32),
                                 vp_ref[:, sl], vc_ref[:, sl], has_prev)
            o_ref[:, sl] = o
            l_ref[:, sl] = lse

    def spec(per_tok, off, prev):
        def imap(rho, i):
            return (jnp.maximum(i - 1, 0) if prev else i, rho * per_tok + off)
        return pl.BlockSpec((BAND, GROUP_DIM), imap)

    o, lse = pl.pallas_call(
        body, name=f"attn_fwd_d{d}", grid=(d, nb),
        in_specs=[spec(3, g, False), spec(3, g, True), spec(3, g, False), spec(9, 6 + g, True), spec(9, 6 + g, False)],
        out_specs=[spec(1, 0, False), spec(1, 0, False)],
        out_shape=[SDS((l, d * GROUP_DIM), F32), SDS((l, d * GROUP_DIM), F32)],
        compiler_params=pltpu.CompilerParams(dimension_semantics=("parallel", "arbitrary"), vmem_limit_bytes=VMEM_LIMIT),
    )(qv, kv, kv, zv, zv)
    return o.reshape(s, GROUP_DIM), lse.reshape(s, GROUP_DIM)


def _attn_bwd(q, k, z_a, g, d, do, dlse):
    s = q.shape[0]
    l = s // d
    nb = l // BAND
    qv, kv, zv = q.reshape(l, d * ATTN_DIM), k.reshape(l, d * ATTN_DIM), z_a.reshape(l, d * ATTN_COLS)
    dov, dlv = do.reshape(l, d * GROUP_DIM), dlse.reshape(l, d * GROUP_DIM)

    def body(q_ref, kp_ref, kc_ref, vp_ref, vc_ref, do_ref, dl_ref, dq_ref, dk_ref, dv_ref, ck, cv):
        step = pl.program_id(1)
        has_prev = step < nb - 1

        @pl.when(step == 0)
        def _():
            ck[...] = jnp.zeros_like(ck)
            cv[...] = jnp.zeros_like(cv)

        for h in range(GROUP_DIM // HEAD):
            sl = slice(h * HEAD, (h + 1) * HEAD)
            _, vjp = jax.vjp(functools.partial(_attn_block, has_prev=has_prev), q_ref[:, sl].astype(F32),
                             kp_ref[:, sl].astype(F32), kc_ref[:, sl].astype(F32), vp_ref[:, sl], vc_ref[:, sl])
            dq, dkp, dkc, dvp, dvc = vjp((do_ref[:, sl], dl_ref[:, sl]))
            dq_ref[:, sl] = dq
            dk_ref[:, sl] = dkc + ck[:, sl]
            dv_ref[:, sl] = dvc + cv[:, sl]
            ck[:, sl] = dkp
            cv[:, sl] = dvp

    def spec(per_tok, off, prev):
        def imap(rho, i):
            blk = nb - 1 - i
            return (jnp.maximum(blk - 1, 0) if prev else blk, rho * per_tok + off)
        return pl.BlockSpec((BAND, GROUP_DIM), imap)

    dq, dk, dv = pl.pallas_call(
        body, name=f"attn_bwd_d{d}", grid=(d, nb),
        in_specs=[spec(3, g, False), spec(3, g, True), spec(3, g, False), spec(9, 6 + g, True), spec(9, 6 + g, False),
                  spec(1, 0, False), spec(1, 0, False)],
        out_specs=[spec(1, 0, False)] * 3,
        out_shape=[SDS((l, d * GROUP_DIM), F32)] * 3,
        scratch_shapes=[pltpu.VMEM((BAND, GROUP_DIM), F32), pltpu.VMEM((BAND, GROUP_DIM), F32)],
        compiler_params=pltpu.CompilerParams(dimension_semantics=("parallel", "arbitrary"), vmem_limit_bytes=VMEM_LIMIT),
    )(qv, kv, kv, zv, zv, dov, dlv)
    return dq.reshape(s, GROUP_DIM), dk.reshape(s, GROUP_DIM), dv.reshape(s, GROUP_DIM)


def _coords():
    return lax.axis_index("x"), lax.axis_index("y"), lax.axis_index("c")


_CHIP_FLIPS = ((1, 0), (0, 1), (1, 1))


def _flip(v, f):
    return 1 - v if f else v


def _gather_weights(shard):
    rows = shard.shape[0]
    half = rows // 2

    def body(src, out, send_sems, recv_sems):
        x, y, c = _coords()
        me = 2 * x + y
        mine = pl.ds(pl.multiple_of(c * half, 16), half)

        def chip_of(f):
            return _flip(x, f[0]), _flip(y, f[1])

        def over_ici(kk):
            px, py = chip_of(_CHIP_FLIPS[kk])
            return pltpu.make_async_remote_copy(src_ref=src.at[mine], dst_ref=out.at[me, mine], send_sem=send_sems.at[kk],
                                                recv_sem=recv_sems.at[kk], device_id=(px, py, c), device_id_type=MESH)

        def landed(kk):
            px, py = chip_of(_CHIP_FLIPS[kk])
            there = out.at[2 * px + py, mine]
            return pltpu.make_async_remote_copy(src_ref=there, dst_ref=there, send_sem=send_sems.at[kk],
                                                recv_sem=recv_sems.at[kk], device_id=(px, py, c), device_id_type=MESH)

        def passed_on(kk, sent_by_me):
            px, py = chip_of(_CHIP_FLIPS[kk])
            part = mine if sent_by_me else pl.ds(pl.multiple_of((1 - c) * half, 16), half)
            there = out.at[2 * px + py, part]
            return pltpu.make_async_remote_copy(src_ref=there, dst_ref=there, send_sem=send_sems.at[3 + kk],
                                                recv_sem=recv_sems.at[3 + kk], device_id=(x, y, 1 - c), device_id_type=MESH)

        sends = [over_ici(kk) for kk in range(3)]
        for cp in sends:
            cp.start()
        for kk in range(3):
            landed(kk).wait_recv()
            fwd = passed_on(kk, True)
            fwd.start()
            sends.append(fwd)
        for kk in range(3):
            passed_on(kk, False).wait_recv()
        for cp in sends:
            cp.wait_send()

    out = pl.pallas_call(
        body, name="gather_weights",
        in_specs=[pl.BlockSpec(memory_space=pl.ANY)], out_specs=pl.BlockSpec(memory_space=pl.ANY),
        out_shape=SDS((N_CHIPS, rows, PACK_COLS), shard.dtype),
        scratch_shapes=[pltpu.SemaphoreType.DMA((6,)), pltpu.SemaphoreType.DMA((6,))],
    )(shard)
    x, y, _ = _coords()
    return lax.dynamic_update_slice(out, shard[None], (2 * x + y, 0, 0))


_HBM = pl.BlockSpec(memory_space=pltpu.HBM)
_SEM = pl.BlockSpec(memory_space=pltpu.SEMAPHORE)
_EFFECT = pltpu.SideEffectType.DATAFLOW_SIDE_EFFECTING


def _copies_start(name, bufs, n_sems, issue):
    nb = len(bufs)

    def body(*refs):
        for cp in issue(refs[:nb], refs[nb], refs[nb + 1]):
            cp.start()
        refs[-1][...] = jnp.zeros_like(refs[-1])

    outs = pl.pallas_call(
        body, name=name,
        out_shape=(pltpu.SemaphoreType.DMA((n_sems,)), pltpu.SemaphoreType.DMA((n_sems,)),
                   *[pltpu.HBM(b.shape, b.dtype) for b in bufs], SDS((8, 128), F32)),
        in_specs=[_HBM] * nb, out_specs=(_SEM, _SEM, *[_HBM] * nb, pl.BlockSpec(memory_space=pltpu.VMEM)),
        input_output_aliases={i: 2 + i for i in range(nb)},
        compiler_params=pltpu.CompilerParams(has_side_effects=_EFFECT),
    )(*[pltpu.with_memory_space_constraint(b, pltpu.HBM) for b in bufs])
    return outs[0], outs[1], list(outs[2:2 + nb]), outs[-1]


def _copies_wait(name, bufs, send_sems, recv_sems, after, expect):
    nb = len(bufs)

    def body(*refs):
        sent, received = expect(refs[:nb], refs[nb], refs[nb + 1])
        for cp in sent:
            cp.wait_send()
        for cp in received:
            cp.wait_recv()

    outs = pl.pallas_call(
        body, name=name,
        out_shape=tuple(pltpu.HBM(b.shape, b.dtype) for b in bufs),
        in_specs=(*[_HBM] * nb, _SEM, _SEM, pl.BlockSpec(memory_space=pl.ANY)), out_specs=tuple([_HBM] * nb),
        input_output_aliases={i: i for i in range(nb)},
        compiler_params=pltpu.CompilerParams(has_side_effects=_EFFECT),
    )(*bufs, send_sems, recv_sems, after)
    return list(outs)


def _gather_plan(half, step):
    def parts():
        x, y, c = _coords()
        mine = pl.ds(pl.multiple_of(c * half, 16), half)
        other = pl.ds(pl.multiple_of((1 - c) * half, 16), half)
        chips = [(_flip(x, fx), _flip(y, fy)) for fx, fy in _CHIP_FLIPS]
        return x, y, c, mine, other, chips

    def copy(src, dst, sems, kk, dev):
        return pltpu.make_async_remote_copy(src_ref=src, dst_ref=dst, send_sem=sems[0].at[kk], recv_sem=sems[1].at[kk],
                                            device_id=dev, device_id_type=MESH)

    def issue(refs, send_sems, recv_sems):
        x, y, c, mine, other, chips = parts()
        sems = (send_sems, recv_sems)
        if step == "ici":
            shard, out = refs
            return [copy(shard.at[mine], out.at[2 * x + y, mine], sems, kk, (px, py, c)) for kk, (px, py) in enumerate(chips)]
        (out,) = refs
        return [copy(out.at[2 * px + py, mine], out.at[2 * px + py, mine], sems, kk, (x, y, 1 - c))
                for kk, (px, py) in enumerate(chips)]

    def expect(refs, send_sems, recv_sems):
        x, y, c, mine, other, chips = parts()
        sems = (send_sems, recv_sems)
        sent = issue(refs, send_sems, recv_sems)
        out = refs[-1]
        if step == "ici":
            got = [copy(out.at[2 * px + py, mine], out.at[2 * px + py, mine], sems, kk, (px, py, c))
                   for kk, (px, py) in enumerate(chips)]
        else:
            got = [copy(out.at[2 * px + py, other], out.at[2 * px + py, other], sems, kk, (x, y, 1 - c))
                   for kk, (px, py) in enumerate(chips)]
        return sent, got

    return issue, expect


def _own_shard_in(out, shard):
    x, y, _ = _coords()
    return lax.dynamic_update_slice(out, shard[None], (2 * x + y, 0, 0))


def _rcopy(src, dst, send_sems, recv_sems, kk, dev):
    return pltpu.make_async_remote_copy(src_ref=src, dst_ref=dst, send_sem=send_sems.at[kk], recv_sem=recv_sems.at[kk],
                                        device_id=dev, device_id_type=MESH)


class _GroupReduce:
    def __init__(self, tag, which, c_arr, me_arr):
        self.tag, self.which, self.c_arr, self.me_arr = tag, which, c_arr, me_arr

    def _plan(self, step):
        half = self.half

        def issue(refs, ss, rs):
            x, y, c = _coords()
            sib = (x, y, 1 - c)
            if step == "swap":
                packed, recv = refs
                return [_rcopy(packed.at[:, pl.ds(pl.multiple_of((1 - c) * half, 8), half)], recv, ss, rs, 0, sib)]
            if step == "ici":
                pair_bf, recv = refs
                return [_rcopy(pair_bf.at[2 * _flip(x, fx) + _flip(y, fy)], recv.at[kk], ss, rs, kk, (_flip(x, fx), _flip(y, fy), c))
                        for kk, (fx, fy) in enumerate(_CHIP_FLIPS)]
            red, full = refs
            return [_rcopy(red, full.at[pl.ds(pl.multiple_of(c * half, 8), half)], ss, rs, 0, sib)]

        def expect(refs, ss, rs):
            x, y, c = _coords()
            sib = (x, y, 1 - c)
            land = refs[1]
            if step == "swap":
                got = [_rcopy(land, land, ss, rs, 0, sib)]
            elif step == "ici":
                got = [_rcopy(land.at[kk], land.at[kk], ss, rs, kk, (_flip(x, fx), _flip(y, fy), c))
                       for kk, (fx, fy) in enumerate(_CHIP_FLIPS)]
            else:
                there = land.at[pl.ds(pl.multiple_of((1 - c) * half, 8), half)]
                got = [_rcopy(there, there, ss, rs, 0, sib)]
            return issue(refs, ss, rs), got

        return issue, expect

    def swap_start(self, blocks):
        packed = _pack_big(blocks, self.which)
        self.half = packed.shape[1] // 2
        ss, rs, bufs, tok = _copies_start(f"rs_{self.tag}_swap", [packed, lax.empty((N_CHIPS, self.half, PACK_COLS), F32)], 1,
                                          self._plan("swap")[0])
        self.state = (ss, rs, bufs)
        return tok

    def swap_wait_ici_start(self, after):
        ss, rs, bufs = self.state
        packed, recv = _copies_wait(f"rs_{self.tag}_swap_wait", bufs, ss, rs, after, self._plan("swap")[1])
        self.pair, pair_bf = _add_pair(packed.reshape(N_CHIPS, 2, self.half, PACK_COLS), recv, self.c_arr,
                                       name=f"rs_{self.tag}_add_pair")
        ss, rs, bufs, tok = _copies_start(f"rs_{self.tag}_ici", [pair_bf, lax.empty((3, self.half, PACK_COLS), BF16)], 3,
                                          self._plan("ici")[0])
        self.state = (ss, rs, bufs)
        return tok

    def ici_wait_join_start(self, after):
        ss, rs, bufs = self.state
        _, recv = _copies_wait(f"rs_{self.tag}_ici_wait", bufs, ss, rs, after, self._plan("ici")[1])
        red = _add_chips(self.pair, recv, self.me_arr, name=f"rs_{self.tag}_add_chips")
        ss, rs, bufs, tok = _copies_start(f"rs_{self.tag}_join", [red, lax.empty((2 * self.half, PACK_COLS), F32)], 1,
                                          self._plan("join")[0])
        self.state = (ss, rs, bufs)
        return tok

    def join_wait(self, after):
        ss, rs, bufs = self.state
        red, full = _copies_wait(f"rs_{self.tag}_join_wait", bufs, ss, rs, after, self._plan("join")[1])
        full = lax.dynamic_update_slice(full, red, (lax.axis_index("c") * self.half, 0))
        return _unpack_big(full, self.which)


def _add_pair(g, recv, c_arr, name="grad_add_pair"):
    n, _, half, cols = g.shape
    tr = _pick(half, (656, 328, 8))

    def body(c_ref, g_ref, r_ref, o_ref, ob_ref):
        v = g_ref[:, 0] + r_ref[...]
        o_ref[...] = v
        ob_ref[...] = v.astype(BF16)

    return pl.pallas_call(
        body, name=name,
        grid_spec=pltpu.PrefetchScalarGridSpec(
            num_scalar_prefetch=1, grid=(n, half // tr),
            in_specs=[pl.BlockSpec((1, 1, tr, cols), lambda j, i, c_ref: (j, c_ref[0], i, 0)),
                      pl.BlockSpec((1, tr, cols), lambda j, i, c_ref: (j, i, 0))],
            out_specs=[pl.BlockSpec((1, tr, cols), lambda j, i, c_ref: (j, i, 0))] * 2),
        out_shape=[SDS((n, half, cols), F32), SDS((n, half, cols), BF16)],
        compiler_params=pltpu.CompilerParams(dimension_semantics=("parallel", "parallel"), vmem_limit_bytes=VMEM_LIMIT),
    )(c_arr, g, recv)


def _add_chips(a, recv, me_arr, name="grad_add_chips"):
    n, half, cols = a.shape
    tr = _pick(half, (656, 328, 8))

    def body(me_ref, a_ref, r_ref, o_ref):
        o_ref[...] = ((a_ref[0] + r_ref[0].astype(F32)) + r_ref[1].astype(F32)) + r_ref[2].astype(F32)

    return pl.pallas_call(
        body, name=name,
        grid_spec=pltpu.PrefetchScalarGridSpec(
            num_scalar_prefetch=1, grid=(half // tr,),
            in_specs=[pl.BlockSpec((1, tr, cols), lambda i, me_ref: (me_ref[0], i, 0)),
                      pl.BlockSpec((3, tr, cols), lambda i, me_ref: (0, i, 0))],
            out_specs=pl.BlockSpec((tr, cols), lambda i, me_ref: (i, 0))),
        out_shape=SDS((half, cols), F32),
        compiler_params=pltpu.CompilerParams(dimension_semantics=("parallel",), vmem_limit_bytes=VMEM_LIMIT),
    )(me_arr, a, recv)


def _all_reduce_small(buf):
    rows, cols = buf.shape

    def body(x_ref, o_ref, gath, send_sems, recv_sems):
        x, y, c = _coords()
        me = 4 * x + 2 * y + c
        gath[me] = x_ref[...]
        sends = []
        for kk in range(1, 8):
            f = (kk >> 2) & 1, (kk >> 1) & 1, kk & 1
            px, py, pc = _flip(x, f[0]), _flip(y, f[1]), _flip(c, f[2])
            cp = pltpu.make_async_remote_copy(src_ref=x_ref, dst_ref=gath.at[me], send_sem=send_sems.at[kk - 1],
                                              recv_sem=recv_sems.at[kk - 1], device_id=(px, py, pc), device_id_type=MESH)
            cp.start()
            sends.append(cp)
        for kk in range(1, 8):
            f = (kk >> 2) & 1, (kk >> 1) & 1, kk & 1
            px, py, pc = _flip(x, f[0]), _flip(y, f[1]), _flip(c, f[2])
            there = gath.at[4 * px + 2 * py + pc]
            pltpu.make_async_remote_copy(src_ref=there, dst_ref=there, send_sem=send_sems.at[kk - 1],
                                         recv_sem=recv_sems.at[kk - 1], device_id=(px, py, pc), device_id_type=MESH).wait_recv()
        for cp in sends:
            cp.wait_send()
        acc = gath[0]
        for j in range(1, 8):
            acc = acc + gath[j]
        o_ref[...] = acc

    return pl.pallas_call(
        body, name="all_reduce_small",
        in_specs=[pl.BlockSpec(memory_space=pltpu.VMEM)], out_specs=pl.BlockSpec(memory_space=pltpu.VMEM),
        out_shape=SDS((rows, cols), F32),
        scratch_shapes=[pltpu.VMEM((8, rows, cols), F32), pltpu.SemaphoreType.DMA((7,)), pltpu.SemaphoreType.DMA((7,))],
    )(buf)


def _adamw_rows(w, g, m, v):
    m = ADAM_B1 * m + (1.0 - ADAM_B1) * g
    v = ADAM_B2 * v + (1.0 - ADAM_B2) * jnp.square(g)
    m_hat = m / (1.0 - ADAM_B1 ** ADAM_STEP)
    v_hat = v / (1.0 - ADAM_B2 ** ADAM_STEP)
    return -ADAM_LR * (m_hat / (jnp.sqrt(v_hat) + ADAM_EPS) + ADAM_WD * w), m, v


def _adamw(w, g, m, v, name, dep=None):
    rows, cols = w.shape
    tm = _pick(rows, (256, 128, 64, 16, 8))
    return _rows_call(_adamw_rows, [(t, 0, cols) for t in (w, g, m, v)], [], [(cols, F32)] * 3, tm=tm, name=name, dep=dep)


def _pack_big(parts, which=BIG):
    return jnp.concatenate([parts[n].reshape(parts[n].shape[:-2] + (-1, PACK_COLS)) for n, _, _ in which], axis=-2)


def _unpack_big(buf, which=BIG):
    out, off = {}, 0
    for n, shp, _ in which:
        r = shp[0] * shp[1] // PACK_COLS
        out[n] = buf[..., off:off + r, :].reshape(buf.shape[:-2] + shp)
        off += r
    return out


def _pack_small(parts):
    flat = jnp.concatenate([parts[n].reshape(-1) for n, _ in SMALL])
    return jnp.pad(flat, (0, SMALL_ROWS * PACK_COLS - flat.shape[0])).reshape(SMALL_ROWS, PACK_COLS)


def _unpack_small(buf, shapes):
    flat, out, off = buf.reshape(-1), {}, 0
    for n, sz in SMALL:
        out[n] = flat[off:off + sz].reshape(shapes[n])
        off += sz
    return out


def _whole(blocks, how):
    n, r, c = blocks.shape
    if how == "row":
        return blocks.reshape(n * r, c)
    return blocks.transpose(1, 0, 2).reshape(r, n * c)


def _split(whole, how):
    if how == "row":
        return whole.reshape(N_CHIPS, whole.shape[0] // N_CHIPS, whole.shape[1])
    r, c = whole.shape
    return whole.reshape(r, N_CHIPS, c // N_CHIPS).transpose(1, 0, 2)


def _ffn_fwd(x, gain, wg, wu, wd, tag):
    h = _rows_call(_rms, [(x, 0, D_MODEL)], [gain], [(D_MODEL, BF16)], tm=256, name=f"{tag}_norm")[0]
    gate = _mm(h, wg, name=f"{tag}_gate")
    up = _mm(h, wu, name=f"{tag}_up")
    nblk, s, f = gate.shape
    act = _rows_call(_swiglu_act, [(gate.reshape(nblk * s, f), 0, f), (up.reshape(nblk * s, f), 0, f)], [], [(f, BF16)],
                     tm=512, name=f"{tag}_act")[0].reshape(nblk, s, f)
    x_new = _mm(act, wd, sum_blocks=True, res=x, alpha=0.5, name=f"{tag}_down")
    return x_new, (x, h, gate, up, act)


def _ffn_bwd(dx_new, dx_new_bf, saved, gain, wg, wu, wd, tag, dep=None):
    x, h, gate, up, act = saved
    nblk, s, f = gate.shape
    d_wd = _mm(act, dx_new_bf, ta=True, alpha=0.5, dep=dep, name=f"{tag}_down_dw")
    dact = _mm(dx_new_bf, wd, tb=True, alpha=0.5, name=f"{tag}_down_dx")

    def act_bwd(gt, ut, ct):
        _, vjp = jax.vjp(_swiglu_act, gt, ut)
        return vjp(ct)

    dgate, dup = _rows_call(act_bwd, [(t.reshape(nblk * s, f), 0, f) for t in (gate, up, dact)], [], [(f, BF16)] * 2,
                            tm=512, name=f"{tag}_act_bwd")
    dgate, dup = dgate.reshape(nblk, s, f), dup.reshape(nblk, s, f)
    d_wg = _mm(h, dgate, ta=True, name=f"{tag}_gate_dw")
    d_wu = _mm(h, dup, ta=True, name=f"{tag}_up_dw")
    dh = _mm(dgate, wg, tb=True, sum_blocks=True, name=f"{tag}_gate_dx")
    dh = _mm(dup, wu, tb=True, sum_blocks=True, res=dh, name=f"{tag}_up_dx")
    dx, dx_bf, dgain = _norm_bwd(x, gain, dh, dx_new, f"{tag}_norm_bwd")
    return dx, dx_bf, dgain, d_wg, d_wu, d_wd


def _norm_bwd(x, gain, dh, dres, name):
    def f(xt, dht, drt, gt):
        _, vjp = jax.vjp(_rms, xt, gt)
        dxt, dgt = vjp(dht)
        return dxt + drt, dxt + drt, dgt

    return _rows_call(f, [(x, 0, D_MODEL), (dh, 0, D_MODEL), (dres, 0, D_MODEL)], [gain], [(D_MODEL, F32), (D_MODEL, BF16)],
                      [(1, D_MODEL)], tm=256, name=name)


def kernel(x, p, positions, ffn1_norm, ffn1_w_gate, ffn1_w_up, ffn1_w_down, mix_norm, w_in, rwkv_mu, rwkv_w0, rwkv_w2, rwkv_a0, rwkv_a2, rwkv_g2, rwkv_k_k, rwkv_k_a, rwkv_r_k, rwkv_gn_w, rwkv_gn_b, q_norm, k_norm, w_br_rwkv, w_br_attn, w_out, ffn2_norm, ffn2_w_gate, ffn2_w_up, ffn2_w_down, ple_norm, ple_w_gate, ple_w_proj, loss_target, m_ffn1_norm, m_ffn1_w_gate, m_ffn1_w_up, m_ffn1_w_down, m_mix_norm, m_w_in, m_rwkv_mu, m_rwkv_w0, m_rwkv_w2, m_rwkv_a0, m_rwkv_a2, m_rwkv_g2, m_rwkv_k_k, m_rwkv_k_a, m_rwkv_r_k, m_rwkv_gn_w, m_rwkv_gn_b, m_q_norm, m_k_norm, m_w_br_rwkv, m_w_br_attn, m_w_out, m_ffn2_norm, m_ffn2_w_gate, m_ffn2_w_up, m_ffn2_w_down, m_ple_norm, m_ple_w_gate, m_ple_w_proj, v_ffn1_norm, v_ffn1_w_gate, v_ffn1_w_up, v_ffn1_w_down, v_mix_norm, v_w_in, v_rwkv_mu, v_rwkv_w0, v_rwkv_w2, v_rwkv_a0, v_rwkv_a2, v_rwkv_g2, v_rwkv_k_k, v_rwkv_k_a, v_rwkv_r_k, v_rwkv_gn_w, v_rwkv_gn_b, v_q_norm, v_k_norm, v_w_br_rwkv, v_w_br_attn, v_w_out, v_ffn2_norm, v_ffn2_w_gate, v_ffn2_w_up, v_ffn2_w_down, v_ple_norm, v_ple_w_gate, v_ple_w_proj):
    args = dict(locals())
    wts = {n: args[n] for n in WEIGHTS}
    mom_m = {n: args["m_" + n] for n in WEIGHTS}
    mom_v = {n: args["v_" + n] for n in WEIGHTS}
    x0, tgt = x[0], loss_target[0]
    s = x0.shape[0]
    p_tok = p[0, 0]

    vec = {n: wts[n].reshape(1, -1) for n, _ in SMALL}
    groups = {"f1": BIG[0:3], "mx": BIG[3:10], "f2": BIG[10:15]}
    shard = {g: _pack_big({n: wts[n][0] for n, _, _ in grp}, grp).astype(BF16) for g, grp in groups.items()}
    plans = {(g, st): _gather_plan(shard[g].shape[0] // 2, st) for g in ("mx", "f2") for st in ("ici", "d2d")}

    def landing(g):
        return lax.empty((N_CHIPS,) + shard[g].shape, BF16)

    wb = _unpack_big(_gather_weights(shard["f1"]), groups["f1"])
    ss_a, rs_a, (sh_mx, out_mx), tok_a = _copies_start("gather_mx_ici", [shard["mx"], landing("mx")], 3, plans["mx", "ici"][0])

    inv_freq = 1.0 / (ROPE_THETA ** (jnp.arange(0, HEAD, 2, dtype=F32) / HEAD))
    ang = positions[0].astype(F32)[:, None] * inv_freq
    cos, sin = jnp.cos(ang), jnp.sin(ang)
    cos2, sin2 = jnp.concatenate([cos, cos], axis=1), jnp.concatenate([-sin, sin], axis=1)

    x1, ffn1_saved = _ffn_fwd(x0, vec["ffn1_norm"] + tok_a[0, 0], wb["ffn1_w_gate"], wb["ffn1_w_up"], wb["ffn1_w_down"], "ffn1")
    sh_mx, out_mx = _copies_wait("gather_mx_ici_wait", [sh_mx, out_mx], ss_a, rs_a, x1, plans["mx", "ici"][1])
    ss_b, rs_b, (out_mx,), tok_b = _copies_start("gather_mx_d2d", [out_mx], 3, plans["mx", "d2d"][0])
    ss_c, rs_c, (sh_f2, out_f2), tok_c = _copies_start("gather_f2_ici", [shard["f2"], landing("f2")], 3, plans["f2", "ici"][0])
    h = _rows_call(_rms, [(x1, 0, D_MODEL)], [vec["mix_norm"] + (tok_b[0, 0] + tok_c[0, 0])], [(D_MODEL, BF16)], tm=256,
                   name="mix_norm")[0]
    (out_mx,) = _copies_wait("gather_mx_d2d_wait", [out_mx], ss_b, rs_b, h, plans["mx", "d2d"][1])
    wb.update(_unpack_big(_own_shard_in(out_mx, sh_mx), groups["mx"]))
    w_in_all = _whole(wb["w_in"], "col")
    w_in_r, w_in_a, w_in_g = w_in_all[:, :RWKV_COLS], w_in_all[:, RWKV_COLS:RWKV_COLS + ATTN_COLS], w_in_all[:, RWKV_COLS + ATTN_COLS:]
    w2, a2, g2 = (_whole(wb[n], "col") for n in ("rwkv_w2", "rwkv_a2", "rwkv_g2"))
    w_brr, w_bra = _whole(wb["w_br_rwkv"], "col"), _whole(wb["w_br_attn"], "col")
    w_o = _whole(wb["w_out"], "row")
    z_r = _mm(h, w_in_r, name="in_rwkv")
    z_a = _mm(h, w_in_a, name="in_attn")
    z_g = _mm(h, w_in_g, name="in_gate")

    zs = _shift_fwd(z_r, vec["rwkv_mu"])
    pre_params = [vec["rwkv_w0"], w2, vec["rwkv_a0"], a2, g2, vec["rwkv_k_k"], vec["rwkv_k_a"]]
    def pre_fwd(*t):
        res = _rwkv_pre(*t)
        return res[1], res[2], res[4], res[5], res[6]

    lw, k2, na, kb, gate_r = _rows_call(pre_fwd, [(zs, 0, RWKV_COLS)], pre_params, [(RWKV_DIM, F32)] * 5, tm=256, name="rwkv_pre")
    y_scan, s0s = _wkv_fwd(zs, lw, k2, na, kb)
    sh_f2, out_f2 = _copies_wait("gather_f2_ici_wait", [sh_f2, out_f2], ss_c, rs_c, y_scan, plans["f2", "ici"][1])
    ss_d, rs_d, (out_f2,), tok_d = _copies_start("gather_f2_d2d", [out_f2], 3, plans["f2", "d2d"][0])
    post_params = [vec["rwkv_gn_w"] + tok_d[0, 0], vec["rwkv_gn_b"], vec["rwkv_r_k"]]
    post_rows = [(y_scan, 0, RWKV_DIM), (zs, 0, RWKV_DIM), (k2, 0, RWKV_DIM), (zs, 2, RWKV_DIM), (gate_r, 0, RWKV_DIM)]
    y_rwkv = _rows_call(_rwkv_post, post_rows, post_params, [(RWKV_DIM, BF16)], tm=256, name="rwkv_post")[0]

    def qk_fwd(qt, kt, ct, st, qg, kg):
        return _norm_rope(qt, qg, ct, st), _norm_rope(kt, kg, ct, st)

    qk_rows = [(z_a, 0, ATTN_DIM), (z_a, 1, ATTN_DIM), (cos2, 0, HEAD), (sin2, 0, HEAD)]
    q_rot, k_rot = _rows_call(qk_fwd, qk_rows, [vec["q_norm"], vec["k_norm"]], [(ATTN_DIM, BF16)] * 2, tm=256, name="attn_pre")
    outs, lses = zip(*[_attn_fwd(q_rot, k_rot, z_a, g, d) for g, d in enumerate(ATTN_DILATIONS)])
    comb_rows = [(t, 0, GROUP_DIM) for t in outs + lses]
    y_attn = _rows_call(_attn_combine, comb_rows, [], [(GROUP_DIM, BF16)], tm=256, name="attn_combine")[0]

    br = _mm(y_rwkv, w_brr, name="branch_rwkv")
    ba = _mm(y_attn, w_bra, name="branch_attn")
    merge_rows = [(z_g, 0, D_MODEL), (z_g, 1, D_MODEL), (br, 0, D_MODEL), (ba, 0, D_MODEL)]
    merged = _rows_call(_merge, merge_rows, [], [(D_MODEL, BF16)], tm=256, name="merge")[0]
    x2 = _mm(merged, w_o, res=x1, name="out_proj")
    (out_f2,) = _copies_wait("gather_f2_d2d_wait", [out_f2], ss_d, rs_d, x2, plans["f2", "d2d"][1])
    wb.update(_unpack_big(_own_shard_in(out_f2, sh_f2), groups["f2"]))
    w_pp, w_pg = _whole(wb["ple_w_proj"], "col"), _whole(wb["ple_w_gate"], "row")
    x3, ffn2_saved = _ffn_fwd(x2, vec["ffn2_norm"], wb["ffn2_w_gate"], wb["ffn2_w_up"], wb["ffn2_w_down"], "ffn2")
    hp = _rows_call(_rms, [(x3, 0, D_MODEL)], [vec["ple_norm"]], [(D_MODEL, BF16)], tm=256, name="ple_norm")[0]
    pg = _mm(hp, w_pg, name="ple_gate")
    pp = _mm(p_tok, w_pp, name="ple_proj")

    def head(x3t, pgt, ppt, tt):
        sg = _sigmoid(pgt)
        err = x3t + sg * ppt - tt
        dx4 = err * (1.0 / D_MODEL)
        loss = 0.5 * jnp.sum(jnp.mean(err * err, axis=-1, keepdims=True), axis=0, keepdims=True)
        return dx4, dx4 * ppt * sg * (1.0 - sg), dx4 * sg, jnp.broadcast_to(loss, (8, 128))

    head_rows = [(x3, 0, D_MODEL), (pg, 0, D_MODEL), (pp, 0, D_MODEL), (tgt, 0, D_MODEL)]
    dx4, dpg, dpp, loss_tile = _rows_call(head, head_rows, [], [(D_MODEL, F32), (D_MODEL, BF16), (D_MODEL, BF16)], [(8, 128)],
                                          tm=256, name="ple_loss")

    xi, yi, ci = _coords()
    c_arr = jnp.reshape(ci, (1,)).astype(jnp.int32)
    me_arr = jnp.reshape(2 * xi + yi, (1,)).astype(jnp.int32)
    how = {n: hw for n, _, hw in BIG}
    red = {g: _GroupReduce(g, groups[g], c_arr, me_arr) for g in groups}
    gw, gs = {}, {}
    gw["ple_w_proj"] = _mm(p_tok, dpp, ta=True, name="ple_proj_dw")
    gw["ple_w_gate"] = _mm(hp, dpg, ta=True, name="ple_gate_dw")
    dhp = _mm(dpg, w_pg, tb=True, name="ple_gate_dx")
    dx3, dx3_bf, gs["ple_norm"] = _norm_bwd(x3, vec["ple_norm"], dhp, dx4, "ple_norm_bwd")
    dx2, dx2_bf, gs["ffn2_norm"], gw["ffn2_w_gate"], gw["ffn2_w_up"], gw["ffn2_w_down"] = _ffn_bwd(
        dx3, dx3_bf, ffn2_saved, vec["ffn2_norm"], wb["ffn2_w_gate"], wb["ffn2_w_up"], wb["ffn2_w_down"], "ffn2")
    f2_blocks = {n: gw[n] for n in ("ffn2_w_gate", "ffn2_w_up", "ffn2_w_down")}
    f2_blocks.update({n: _split(gw[n], how[n]) for n in ("ple_w_gate", "ple_w_proj")})
    tok = red["f2"].swap_start(f2_blocks)
    gw["w_out"] = _mm(merged, dx2_bf, ta=True, dep=tok, name="out_proj_dw")
    dmerged = _mm(dx2_bf, w_o, tb=True, name="out_proj_dx")

    def merge_bwd(zgr, zga, brt, bat, ct):
        _, vjp = jax.vjp(_merge, zgr, zga, brt, bat)
        d1, d2, d3, d4 = vjp(ct)
        return jnp.concatenate([d1, d2], axis=1), d3, d4

    dz_g, dbr, dba = _rows_call(merge_bwd, merge_rows + [(dmerged, 0, D_MODEL)], [],
                                [(2 * D_MODEL, BF16), (D_MODEL, BF16), (D_MODEL, BF16)], tm=256, name="merge_bwd")
    tok = red["f2"].swap_wait_ici_start(dz_g)
    gw["w_br_rwkv"] = _mm(y_rwkv, dbr, ta=True, dep=tok, name="branch_rwkv_dw")
    gw["w_br_attn"] = _mm(y_attn, dba, ta=True, name="branch_attn_dw")
    dy_rwkv = _mm(dbr, w_brr, tb=True, name="branch_rwkv_dx")
    dy_attn = _mm(dba, w_bra, tb=True, name="branch_attn_dx")

    def comb_bwd(*t):
        _, vjp = jax.vjp(_attn_combine, *t[:6])
        return vjp(t[6])

    dcomb = _rows_call(comb_bwd, comb_rows + [(dy_attn, 0, GROUP_DIM)], [], [(GROUP_DIM, F32)] * 6, tm=256, name="attn_combine_bwd")
    dqs, dks, dvs = zip(*[_attn_bwd(q_rot, k_rot, z_a, g, d, dcomb[g], dcomb[3 + g]) for g, d in enumerate(ATTN_DILATIONS)])

    def qk_bwd(qt, kt, ct, st, *rest):
        dq = jnp.concatenate(rest[0:3], axis=1)
        dk = jnp.concatenate(rest[3:6], axis=1)
        qg, kg = rest[9], rest[10]
        _, vjp = jax.vjp(lambda a_, b_, c_, d_: qk_fwd(a_, b_, ct, st, c_, d_), qt, kt, qg, kg)
        dqt, dkt, dqg, dkg = vjp((dq, dk))
        return jnp.concatenate((dqt, dkt) + tuple(rest[6:9]), axis=1), dqg, dkg

    dz_a, gs["q_norm"], gs["k_norm"] = _rows_call(
        qk_bwd, qk_rows + [(t, 0, GROUP_DIM) for t in dqs + dks + dvs], [vec["q_norm"], vec["k_norm"]],
        [(ATTN_COLS, BF16)], [(1, HEAD), (1, HEAD)], tm=256, name="attn_pre_bwd")
    tok = red["f2"].ici_wait_join_start(dz_a)

    def post_bwd(*t):
        _, vjp = jax.vjp(_rwkv_post, *t[:5], *t[6:])
        return vjp(t[5])

    dy_scan, dr_post, dk2_post, dv_post, dgate_r, gs["rwkv_gn_w"], gs["rwkv_gn_b"], gs["rwkv_r_k"] = _rows_call(
        post_bwd, post_rows + [(dy_rwkv, 0, RWKV_DIM)], post_params, [(RWKV_DIM, F32)] * 5, [(1, RWKV_DIM)] * 3,
        tm=256, name="rwkv_post_bwd", dep=tok)
    grad_big = red["f2"].join_wait(dy_scan)
    dr_s, dlw, dk2_s, dv_s, dna, dkb = _wkv_bwd(zs, lw, k2, na, kb, s0s, dy_scan)

    def pre_bwd(zt, c_r1, c_r2, c_lw, c_k1, c_k2, c_v1, c_v2, c_a, c_b, c_g, *params):
        _, vjp = jax.vjp(_rwkv_pre, zt, *params)
        return vjp((c_r1 + c_r2, c_lw, c_k1 + c_k2, c_v1 + c_v2, c_a, c_b, c_g))

    pre_cts = [dr_s, dr_post, dlw, dk2_s, dk2_post, dv_s, dv_post, dna, dkb, dgate_r]
    dzs, gs["rwkv_w0"], g_w2, gs["rwkv_a0"], g_a2, g_g2, gs["rwkv_k_k"], gs["rwkv_k_a"] = _rows_call(
        pre_bwd, [(zs, 0, RWKV_COLS)] + [(t, 0, RWKV_DIM) for t in pre_cts], pre_params, [(RWKV_COLS, F32)],
        [q.shape for q in pre_params], tm=256, name="rwkv_pre_bwd")
    dz_r, gs["rwkv_mu"] = _shift_bwd(z_r, vec["rwkv_mu"], dzs)

    g_w_in = jnp.concatenate([_mm(h, dz_r, ta=True, name="in_rwkv_dw"), _mm(h, dz_a, ta=True, name="in_attn_dw"),
                              _mm(h, dz_g, ta=True, name="in_gate_dw")], axis=1)
    mx_blocks = {"w_in": _split(g_w_in, "col"), "rwkv_w2": _split(g_w2, "col"), "rwkv_a2": _split(g_a2, "col"),
                 "rwkv_g2": _split(g_g2, "col")}
    mx_blocks.update({n: _split(gw[n], how[n]) for n in ("w_br_rwkv", "w_br_attn", "w_out")})
    tok = red["mx"].swap_start(mx_blocks)
    dh = _mm(dz_r, w_in_r, tb=True, dep=tok, name="in_rwkv_dx")
    dh = _mm(dz_a, w_in_a, tb=True, res=dh, name="in_attn_dx")
    dh = _mm(dz_g, w_in_g, tb=True, res=dh, name="in_gate_dx")
    dx1, dx1_bf, gs["mix_norm"] = _norm_bwd(x1, vec["mix_norm"], dh, dx2, "mix_norm_bwd")
    tok = red["mx"].swap_wait_ici_start(dx1_bf)
    dx0, _, gs["ffn1_norm"], gw["ffn1_w_gate"], gw["ffn1_w_up"], gw["ffn1_w_down"] = _ffn_bwd(
        dx1, dx1_bf, ffn1_saved, vec["ffn1_norm"], wb["ffn1_w_gate"], wb["ffn1_w_up"], wb["ffn1_w_down"], "ffn1", dep=tok)
    tok_mx = red["mx"].ici_wait_join_start(dx0)
    tok = red["f1"].swap_start({n: gw[n] for n in ("ffn1_w_gate", "ffn1_w_up", "ffn1_w_down")})

    grads, deltas, new_m, new_v = {}, {}, {}, {}

    def update(names, dep):
        last = None
        for n in names:
            g2d = grad_big[n]
            d_, m_, v_ = _adamw(wts[n][0], g2d, mom_m[n][0], mom_v[n][0], name=f"adamw_{n}", dep=dep)
            grads[n], deltas[n], new_m[n], new_v[n] = g2d[None], d_[None], m_[None], v_[None]
            dep, last = None, d_
        return last

    last = update([n for n, _, _ in groups["f2"]], tok)
    tok = red["f1"].swap_wait_ici_start(last)
    grad_big.update(red["mx"].join_wait(last))
    last = update([n for n, _, _ in groups["mx"]], tok)

    flat = jnp.concatenate([gs[n].reshape(-1) for n, _ in SMALL] + [loss_tile[0, 0:1]])
    small_buf = jnp.pad(flat, (0, SMALL_ROWS * PACK_COLS - flat.shape[0])).reshape(SMALL_ROWS, PACK_COLS)
    small_sum = _all_reduce_small(small_buf)
    n_small = sum(sz for _, sz in SMALL)
    loss = small_sum.reshape(-1)[n_small]
    grad_small = _unpack_small(small_sum, {n: wts[n].shape for n, _ in SMALL})
    d_s, m_s, v_s = _adamw(_pack_small(wts), small_sum, _pack_small(mom_m), _pack_small(mom_v), name="adamw_small")
    shapes = {n: wts[n].shape for n, _ in SMALL}
    d_s, m_s, v_s = _unpack_small(d_s, shapes), _unpack_small(m_s, shapes), _unpack_small(v_s, shapes)
    for n, _ in SMALL:
        grads[n], deltas[n], new_m[n], new_v[n] = grad_small[n], d_s[n], m_s[n], v_s[n]

    tok = red["f1"].ici_wait_join_start(m_s["ffn1_norm"])
    grad_big.update(red["f1"].join_wait(tok))
    update([n for n, _, _ in groups["f1"]], None)

    return (loss, dx0[None], *[grads[n] for n in WEIGHTS], *[deltas[n] for n in WEIGHTS],
            *[new_m[n] for n in WEIGHTS], *[new_v[n] for n in WEIGHTS])
```

```python
import functools

import jax
import jax.numpy as jnp
from jax import lax
from jax.experimental import pallas as pl
from jax.experimental.pallas import tpu as pltpu

F32, BF16 = jnp.float32, jnp.bfloat16
HI = lax.Precision.HIGHEST
MESH = pl.DeviceIdType.MESH
SDS = jax.ShapeDtypeStruct

D_MODEL = 1024
HEAD = 64
RWKV_HEADS = 8
RWKV_DIM = RWKV_HEADS * HEAD
DECAY_LORA, ICLR_LORA, GATE_LORA = 64, 64, 128
GN_EPS = 64e-5
RMS_EPS = 1e-6
ATTN_DILATIONS = (1, 4, 16)
BAND = 128
ATTN_DIM = 768
GROUP_DIM = 256
ROPE_THETA = 10000.0
NEG_INF = -1e30
RWKV_COLS = 3 * RWKV_DIM + DECAY_LORA + ICLR_LORA + GATE_LORA
ATTN_COLS = 3 * ATTN_DIM
ADAM_LR, ADAM_B1, ADAM_B2, ADAM_EPS, ADAM_WD, ADAM_STEP = 0.001, 0.9, 0.999, 1e-08, 0.01, 10

WKV_CHUNK = 64
WKV_HEADS_PER_STEP = 8
N_CHIPS = 4
PACK_COLS = 1024
VMEM_LIMIT = 48 * 1024 * 1024

BIG = (
    ("ffn1_w_gate", (1024, 704), "col"), ("ffn1_w_up", (1024, 704), "col"), ("ffn1_w_down", (704, 1024), "row"),
    ("w_in", (1024, 1536), "col"), ("rwkv_w2", (64, 128), "col"), ("rwkv_a2", (64, 128), "col"),
    ("rwkv_g2", (128, 128), "col"), ("w_br_rwkv", (512, 256), "col"), ("w_br_attn", (256, 256), "col"),
    ("w_out", (256, 1024), "row"), ("ffn2_w_gate", (1024, 704), "col"), ("ffn2_w_up", (1024, 704), "col"),
    ("ffn2_w_down", (704, 1024), "row"), ("ple_w_gate", (256, 1024), "row"), ("ple_w_proj", (256, 256), "col"),
)
SMALL = (
    ("ffn1_norm", 1024), ("mix_norm", 1024), ("ffn2_norm", 1024), ("ple_norm", 1024), ("rwkv_mu", 1792),
    ("rwkv_w0", 512), ("rwkv_a0", 512), ("rwkv_k_k", 512), ("rwkv_k_a", 512), ("rwkv_r_k", 512),
    ("rwkv_gn_w", 512), ("rwkv_gn_b", 512), ("q_norm", 64), ("k_norm", 64),
)
SMALL_ROWS = 16
WEIGHTS = (
    "ffn1_norm", "ffn1_w_gate", "ffn1_w_up", "ffn1_w_down", "mix_norm", "w_in", "rwkv_mu", "rwkv_w0", "rwkv_w2",
    "rwkv_a0", "rwkv_a2", "rwkv_g2", "rwkv_k_k", "rwkv_k_a", "rwkv_r_k", "rwkv_gn_w", "rwkv_gn_b", "q_norm", "k_norm",
    "w_br_rwkv", "w_br_attn", "w_out", "ffn2_norm", "ffn2_w_gate", "ffn2_w_up", "ffn2_w_down", "ple_norm",
    "ple_w_gate", "ple_w_proj",
)


def _row_tile(n, most=704):
    for t in range(most - most % 16, 0, -16):
        if n % t == 0:
            return t
    return n


def _pick(n, cands):
    for c in cands:
        if n % c == 0:
            return c
    return n


def _mm(a, b, *, ta=False, tb=False, sum_blocks=False, out_dtype=F32, res=None, alpha=1.0, dep=None, name):
    flat = a.ndim == 2 and b.ndim == 2
    a3 = a if a.ndim == 3 else a[None]
    b3 = b if b.ndim == 3 else b[None]
    na, nbb = a3.shape[0], b3.shape[0]
    nblk = max(na, nbb)
    kdim, m = (a3.shape[1], a3.shape[2]) if ta else (a3.shape[2], a3.shape[1])
    n = b3.shape[1] if tb else b3.shape[2]
    assert (b3.shape[2] if tb else b3.shape[1]) == kdim
    tm = _pick(m, (1024, 512, 256, 128))
    tn = _pick(n, (1024, 896, 768, 512, 256, 128))
    tk = kdim if kdim <= 2304 else _pick(kdim, (1024, 512, 256, 128))
    nk = kdim // tk
    direct = nk == 1 and not sum_blocks

    if sum_blocks:
        grid = (m // tm, n // tn, nblk, nk)

        def ids(i, c, j, k):
            return i, c, j, k
    else:
        grid = (nblk, m // tm, n // tn, nk)

        def ids(j, i, c, k):
            return i, c, j, k

    def amap(*g):
        i, c, j, k = ids(*g)
        jj = j if na > 1 else 0
        return (jj, k, i) if ta else (jj, i, k)

    def bmap(*g):
        i, c, j, k = ids(*g)
        jj = j if nbb > 1 else 0
        return (jj, c, k) if tb else (jj, k, c)

    if sum_blocks:
        oshape, oblk = (m, n), (tm, tn)

        def omap(*g):
            i, c, j, k = ids(*g)
            return i, c
    else:
        oshape, oblk = (nblk, m, n), (1, tm, tn)

        def omap(*g):
            i, c, j, k = ids(*g)
            return j, i, c

    dn = (((0 if ta else 1,), (1 if tb else 0,)), ((), ()))
    has_res = res is not None

    def body(*refs):
        refs = list(refs)
        acc = None if direct else refs.pop()
        o_ref = refs.pop()
        a_ref, b_ref = refs[0], refs[1]
        r_ref = refs[2] if has_res else None

        def finish(v):
            if alpha != 1.0:
                v = v * alpha
            if has_res:
                v = v + r_ref[...].reshape(v.shape).astype(F32)
            o_ref[...] = v.reshape(o_ref.shape).astype(o_ref.dtype)

        if direct:
            finish(lax.dot_general(a_ref[0].astype(BF16), b_ref[0].astype(BF16), dn, preferred_element_type=F32))
            return
        k = pl.program_id(3)
        if sum_blocks:
            j = pl.program_id(2)
            first = jnp.logical_and(j == 0, k == 0)
            last = jnp.logical_and(j == nblk - 1, k == nk - 1)
        else:
            first, last = k == 0, k == nk - 1

        @pl.when(first)
        def _():
            acc[...] = jnp.zeros_like(acc)

        acc[...] += lax.dot_general(a_ref[0].astype(BF16), b_ref[0].astype(BF16), dn, preferred_element_type=F32)

        @pl.when(last)
        def _():
            finish(acc[...])

    in_specs = [pl.BlockSpec((1, tk, tm) if ta else (1, tm, tk), amap), pl.BlockSpec((1, tn, tk) if tb else (1, tk, tn), bmap)]
    args = [a3, b3]
    if has_res:
        res3 = res if (sum_blocks or res.ndim == 3) else res[None]
        in_specs.append(pl.BlockSpec(oblk, omap))
        args.append(res3)
    if dep is not None:
        in_specs.append(pl.BlockSpec(memory_space=pl.ANY))
        args.append(dep)
    out = pl.pallas_call(
        body,
        name=name,
        grid=grid,
        in_specs=in_specs,
        out_specs=pl.BlockSpec(oblk, omap),
        out_shape=SDS(oshape, out_dtype),
        scratch_shapes=[] if direct else [pltpu.VMEM((tm, tn), F32)],
        compiler_params=pltpu.CompilerParams(
            dimension_semantics=("parallel", "parallel", "arbitrary", "arbitrary") if sum_blocks
            else ("parallel", "parallel", "parallel", "arbitrary"),
            vmem_limit_bytes=VMEM_LIMIT),
    )(*args)
    if flat and not sum_blocks:
        out = out[0]
    return out


def _rows_call(f, rows, params, outs, accs=(), *, tm, name, dep=None):
    s = rows[0][0].shape[0]
    nr, npar, no = len(rows), len(params), len(outs)
    nin = nr + npar + (0 if dep is None else 1)
    in_specs = [pl.BlockSpec((tm, w), functools.partial(lambda i, cb: (i, cb), cb=cb)) for (_, cb, w) in rows]
    in_specs += [pl.BlockSpec(p.shape, functools.partial(lambda i, nd: (0,) * nd, nd=p.ndim)) for p in params]
    if dep is not None:
        in_specs.append(pl.BlockSpec(memory_space=pl.ANY))
    out_shape = [SDS((s, w), dt) for (w, dt) in outs] + [SDS(tuple(sh), F32) for sh in accs]
    out_specs = [pl.BlockSpec((tm, w), lambda i: (i, 0)) for (w, _) in outs]
    out_specs += [pl.BlockSpec(tuple(sh), functools.partial(lambda i, nd: (0,) * nd, nd=len(sh))) for sh in accs]

    def body(*refs):
        rin, pin = refs[:nr], refs[nr:nr + npar]
        oo, ao = refs[nin:nin + no], refs[nin + no:]
        res = f(*[r[...] for r in rin], *[p[...] for p in pin])
        if not isinstance(res, (tuple, list)):
            res = (res,)
        for o_ref, v in zip(oo, res[:no]):
            o_ref[...] = v.astype(o_ref.dtype)
        i = pl.program_id(0)
        for a_ref, v in zip(ao, res[no:]):
            @pl.when(i == 0)
            def _():
                a_ref[...] = jnp.zeros_like(a_ref)

            a_ref[...] += v.reshape(a_ref.shape)

    res = pl.pallas_call(
        body,
        name=name,
        grid=(s // tm,),
        in_specs=in_specs,
        out_specs=out_specs,
        out_shape=out_shape,
        compiler_params=pltpu.CompilerParams(dimension_semantics=("arbitrary",), vmem_limit_bytes=VMEM_LIMIT),
    )(*[r[0] for r in rows], *params, *([] if dep is None else [dep]))
    return res


def _mmv(a, b, mode):
    ca = 0 if mode[0] == "t" else 1
    cb = 1 if mode[1] == "t" else 0
    return lax.dot_general(a.astype(BF16), b.astype(BF16), (((ca,), (cb,)), ((), ())), preferred_element_type=F32)


@functools.partial(jax.custom_vjp, nondiff_argnums=(2,))
def _bdot(a, b, mode):
    return _mmv(a, b, mode)


def _bdot_fwd(a, b, mode):
    return _mmv(a, b, mode), (a, b)


def _bdot_bwd(mode, saved, g):
    a, b = saved
    if mode == "nn":
        return _mmv(g, b, "nt"), _mmv(a, g, "tn")
    if mode == "nt":
        return _mmv(g, b, "nn"), _mmv(g, a, "tn")
    return _mmv(b, g, "nt"), _mmv(a, g, "nn")


_bdot.defvjp(_bdot_fwd, _bdot_bwd)


def _hdot(a, b, mode="nn", precision=HI):
    ca = 0 if mode[0] == "t" else 1
    cb = 1 if mode[1] == "t" else 0
    return lax.dot_general(a, b, (((ca,), (cb,)), ((), ())), precision=precision, preferred_element_type=F32)


def _wdot(a, b, mode="nn"):
    return _hdot(a, b, mode, lax.Precision.HIGH)


def _segsum(x):
    c = x.shape[-1]
    r = lax.broadcasted_iota(jnp.int32, (c, c), 0) >> 6
    q = lax.broadcasted_iota(jnp.int32, (c, c), 1) >> 6
    return _hdot(x, jnp.where(r == q, 1.0, 0.0).astype(F32), precision=lax.Precision.HIGH)


def _sigmoid(x):
    return jax.nn.sigmoid(x)


def _softplus(x):
    return jnp.maximum(x, 0.0) + jnp.log(1.0 + jnp.exp(-jnp.abs(x)))


def _rms(x, gain):
    return x * lax.rsqrt(jnp.mean(x * x, axis=-1, keepdims=True) + RMS_EPS) * gain


def _swiglu_act(gate, up):
    return gate * _sigmoid(gate) * up


def _rwkv_pre(zs, w0, w2, a0, a2, g2, k_k, k_a):
    r, k, v = zs[:, 0:512], zs[:, 512:1024], zs[:, 1024:1536]
    lora = zs[:, 1536:1792]
    wd, ad, gd = lora[:, 0:64], lora[:, 64:128], lora[:, 128:256]
    w = -_softplus(-(w0 + _bdot(jnp.tanh(wd), w2, "nn"))) - 0.5
    a = _sigmoid(a0 + _bdot(ad, a2, "nn"))
    g = _bdot(_sigmoid(gd), g2, "nn")
    kk = k * k_k
    kk = kk * lax.rsqrt(jnp.maximum(_segsum(kk * kk), 1e-24))
    k2 = k * (1.0 + (a - 1.0) * k_a)
    return r, -jnp.exp(w), k2, v, -kk, kk * a, g


def _rwkv_post(y, r, k2, v, g, gn_w, gn_b, r_k):
    mean = _segsum(y) * (1.0 / HEAD)
    yc = y - mean
    var = _segsum(yc * yc) * (1.0 / HEAD)
    yn = yc * lax.rsqrt(var + GN_EPS) * gn_w + gn_b
    bonus = _segsum(r * k2 * r_k) * v
    return (yn + bonus) * g


def _swap_halves(x):
    lane = lax.broadcasted_iota(jnp.int32, x.shape, 1)
    return jnp.where((lane & 32) == 0, jnp.roll(x, -32, axis=1), jnp.roll(x, 32, axis=1))


def _norm_rope(x, gain, cos, sin):
    heads = x.shape[1] // HEAD
    def rep(t):
        return jnp.concatenate([t] * heads, axis=1)

    xn = x * lax.rsqrt(_segsum(x * x) * (1.0 / HEAD) + RMS_EPS) * rep(gain)
    return xn * rep(cos) + _swap_halves(xn) * rep(sin)


def _attn_combine(o0, o1, o2, l0, l1, l2):
    m = jnp.maximum(jnp.maximum(l0, l1), l2)
    e0, e1, e2 = jnp.exp(l0 - m), jnp.exp(l1 - m), jnp.exp(l2 - m)
    return (e0 * o0 + e1 * o1 + e2 * o2) / (e0 + e1 + e2)


def _merge(zgr, zga, br, ba):
    return _sigmoid(zgr) * br + _sigmoid(zga) * ba


def _attn_block(q, kp, kc, vp, vc, has_prev):
    iq = lax.broadcasted_iota(jnp.int32, (BAND, BAND), 0)
    ik = lax.broadcasted_iota(jnp.int32, (BAND, BAND), 1)
    s_c = jnp.where(iq >= ik, _bdot(q, kc, "nt") * (HEAD ** -0.5), NEG_INF)
    s_p = jnp.where(jnp.logical_and(iq <= ik, has_prev), _bdot(q, kp, "nt") * (HEAD ** -0.5), NEG_INF)
    m = lax.stop_gradient(jnp.maximum(jnp.max(s_c, axis=-1, keepdims=True), jnp.max(s_p, axis=-1, keepdims=True)))
    e_c, e_p = jnp.exp(s_c - m), jnp.exp(s_p - m)
    l = jnp.sum(e_c, axis=-1, keepdims=True) + jnp.sum(e_p, axis=-1, keepdims=True)
    o = (_bdot(e_c, vc, "nn") + _bdot(e_p, vp, "nn")) / l
    return o, jnp.broadcast_to(m + jnp.log(l), o.shape)


def _bdotb(a, b, mode="nn", precision=lax.Precision.HIGH):
    if mode[0] == "t":
        a = jnp.swapaxes(a, 1, 2)
    cb = 2 if mode[1] == "t" else 1
    return lax.dot_general(a, b, (((2,), (cb,)), ((0,), (0,))), precision=precision, preferred_element_type=F32)


def _tri_inv(a):
    t = a.shape[-1]
    row = lax.broadcasted_iota(jnp.int32, (1, t, t), 1)
    col = lax.broadcasted_iota(jnp.int32, (1, t, t), 2)
    x = jnp.where(row == col, 1.0, 0.0).astype(F32) + jnp.where(jnp.logical_and(row == col + 1, (row & 1) == 1), a, 0.0)
    sh = 1
    while (1 << sh) < t:
        m = jnp.logical_and((row >> sh) == (col >> sh) + 1, (row >> (sh + 1)) == (col >> (sh + 1)))
        x = x + _bdotb(_bdotb(x, jnp.where(m, a, 0.0)), x)
        sh += 1
    return x


def _wkv_chunk(s0, r, lw, k, v, a, b):
    nh, t, _ = r.shape
    row = lax.broadcasted_iota(jnp.int32, (1, t, t), 1)
    col = lax.broadcasted_iota(jnp.int32, (1, t, t), 2)
    incl, strict = row >= col, row > col
    ones = jnp.broadcast_to(jnp.where(incl, 1.0, 0.0).astype(F32), (nh, t, t))
    cum = _bdotb(ones, lw, precision=HI)
    c_end = cum[:, t - 1:t, :]
    e_in, e_ex, e_inv = jnp.exp(cum), jnp.exp(cum - lw), jnp.exp(-cum)
    at, rt, bt, kt = a * e_ex, r * e_in, b * e_inv, k * e_inv
    a_ab = jnp.where(strict, _bdotb(at, bt, "nt"), 0.0)
    a_ak = jnp.where(strict, _bdotb(at, kt, "nt"), 0.0)
    u = _bdotb(_tri_inv(a_ab), _bdotb(at, s0, "nt") + _bdotb(a_ak, v))
    y = (_bdotb(rt, s0, "nt") + _bdotb(jnp.where(incl, _bdotb(rt, bt, "nt"), 0.0), u)
         + _bdotb(jnp.where(incl, _bdotb(rt, kt, "nt"), 0.0), v))
    w_end = jnp.exp(c_end - cum)
    s1 = s0 * jnp.exp(c_end) + _bdotb(u, b * w_end, "tn") + _bdotb(v, k * w_end, "tn")
    return y, s1


def _shift_fwd(z, mu):
    s, c = z.shape
    tc = 256

    def body(z_ref, mu_ref, o_ref):
        zz = z_ref[...]
        row = lax.broadcasted_iota(jnp.int32, zz.shape, 0)
        prev = jnp.where(row == 0, 0.0, pltpu.roll(zz, 1, 0))
        o_ref[...] = zz + (prev - zz) * mu_ref[...]

    return pl.pallas_call(
        body, name="shift_fwd", grid=(c // tc,),
        in_specs=[pl.BlockSpec((s, tc), lambda j: (0, j)), pl.BlockSpec((1, tc), lambda j: (0, j))],
        out_specs=pl.BlockSpec((s, tc), lambda j: (0, j)), out_shape=SDS((s, c), F32),
        compiler_params=pltpu.CompilerParams(dimension_semantics=("parallel",), vmem_limit_bytes=VMEM_LIMIT),
    )(z, mu)


def _shift_bwd(z, mu, dzs):
    s, c = z.shape
    tc = 256

    def body(z_ref, mu_ref, d_ref, dz_ref, dmu_ref):
        zz, d, m = z_ref[...], d_ref[...], mu_ref[...]
        row = lax.broadcasted_iota(jnp.int32, zz.shape, 0)
        prev = jnp.where(row == 0, 0.0, pltpu.roll(zz, 1, 0))
        t = d * m
        nxt = jnp.where(row == s - 1, 0.0, pltpu.roll(t, s - 1, 0))
        dz_ref[...] = (d - t + nxt).astype(dz_ref.dtype)
        dmu_ref[...] = jnp.sum(d * (prev - zz), axis=0, keepdims=True)

    return pl.pallas_call(
        body, name="shift_bwd", grid=(c // tc,),
        in_specs=[pl.BlockSpec((s, tc), lambda j: (0, j)), pl.BlockSpec((1, tc), lambda j: (0, j)),
                  pl.BlockSpec((s, tc), lambda j: (0, j))],
        out_specs=[pl.BlockSpec((s, tc), lambda j: (0, j)), pl.BlockSpec((1, tc), lambda j: (0, j))],
        out_shape=[SDS((s, c), BF16), SDS((1, c), F32)],
        compiler_params=pltpu.CompilerParams(dimension_semantics=("parallel",), vmem_limit_bytes=VMEM_LIMIT),
    )(z, mu, dzs)


def _heads(x, nh):
    return jnp.stack([x[:, h * HEAD:(h + 1) * HEAD] for h in range(nh)], axis=0)


def _unheads(x):
    return jnp.concatenate([x[h] for h in range(x.shape[0])], axis=1)


def _wkv_fwd(zs, lw, k2, na, b):
    s = lw.shape[0]
    t, hb = WKV_CHUNK, WKV_HEADS_PER_STEP
    w = hb * HEAD
    nc, ng = s // t, RWKV_HEADS // hb

    def body(r_ref, v_ref, lw_ref, k_ref, a_ref, b_ref, y_ref, s0_ref, state):
        @pl.when(pl.program_id(1) == 0)
        def _():
            state[...] = jnp.zeros_like(state)

        s0 = state[...]
        s0_ref[0] = s0
        y, s1 = _wkv_chunk(s0, *[_heads(t_ref[...], hb) for t_ref in (r_ref, lw_ref, k_ref, v_ref, a_ref, b_ref)])
        y_ref[...] = _unheads(y)
        state[...] = s1

    def col(off):
        return pl.BlockSpec((t, w), functools.partial(lambda g, i, off: (i, g + off), off=off))

    return pl.pallas_call(
        body, name="wkv_fwd", grid=(ng, nc),
        in_specs=[col(0), col(2 * ng), col(0), col(0), col(0), col(0)],
        out_specs=[col(0), pl.BlockSpec((1, hb, HEAD, HEAD), lambda g, i: (i, g, 0, 0))],
        out_shape=[SDS((s, RWKV_DIM), F32), SDS((nc, RWKV_HEADS, HEAD, HEAD), F32)],
        scratch_shapes=[pltpu.VMEM((hb, HEAD, HEAD), F32)],
        compiler_params=pltpu.CompilerParams(dimension_semantics=("parallel", "arbitrary"), vmem_limit_bytes=VMEM_LIMIT),
    )(zs, zs, lw, k2, na, b)


def _wkv_bwd(zs, lw, k2, na, b, s0s, dy):
    s = lw.shape[0]
    t, hb = WKV_CHUNK, WKV_HEADS_PER_STEP
    w = hb * HEAD
    nc, ng = s // t, RWKV_HEADS // hb

    def body(r_ref, v_ref, lw_ref, k_ref, a_ref, b_ref, s0_ref, dy_ref, dr_ref, dlw_ref, dk_ref, dv_ref, da_ref, db_ref, dstate):
        @pl.when(pl.program_id(1) == 0)
        def _():
            dstate[...] = jnp.zeros_like(dstate)

        _, vjp = jax.vjp(_wkv_chunk, s0_ref[0], *[_heads(t_ref[...], hb) for t_ref in (r_ref, lw_ref, k_ref, v_ref, a_ref, b_ref)])
        grads = vjp((_heads(dy_ref[...], hb), dstate[...]))
        dstate[...] = grads[0]
        for o_ref, gval in zip((dr_ref, dlw_ref, dk_ref, dv_ref, da_ref, db_ref), grads[1:]):
            o_ref[...] = _unheads(gval)

    def col(off):
        return pl.BlockSpec((t, w), functools.partial(lambda g, i, off: (nc - 1 - i, g + off), off=off))

    return pl.pallas_call(
        body, name="wkv_bwd", grid=(ng, nc),
        in_specs=[col(0), col(2 * ng), col(0), col(0), col(0), col(0),
                  pl.BlockSpec((1, hb, HEAD, HEAD), lambda g, i: (nc - 1 - i, g, 0, 0)), col(0)],
        out_specs=[col(0)] * 6,
        out_shape=[SDS((s, RWKV_DIM), F32)] * 6,
        scratch_shapes=[pltpu.VMEM((hb, HEAD, HEAD), F32)],
        compiler_params=pltpu.CompilerParams(dimension_semantics=("parallel", "arbitrary"), vmem_limit_bytes=VMEM_LIMIT),
    )(zs, zs, lw, k2, na, b, s0s, dy)


def _attn_fwd(q, k, z_a, g, d):
    s = q.shape[0]
    l = s // d
    nb = l // BAND
    assert nb * BAND == l
    qv, kv, zv = q.reshape(l, d * ATTN_DIM), k.reshape(l, d * ATTN_DIM), z_a.reshape(l, d * ATTN_COLS)

    def body(q_ref, kp_ref, kc_ref, vp_ref, vc_ref, o_ref, l_ref):
        has_prev = pl.program_id(1) > 0
        for h in range(GROUP_DIM // HEAD):
            sl = slice(h * HEAD, (h + 1) * HEAD)
            o, lse = _attn_block(q_ref[:, sl].astype(F32), kp_ref[:, sl].astype(F32), kc_ref[:, sl].astype(F32),
                                 vp_ref[:, sl], vc_ref[:, sl], has_prev)
            o_ref[:, sl] = o
            l_ref[:, sl] = lse

    def spec(per_tok, off, prev):
        def imap(rho, i):
            return (jnp.maximum(i - 1, 0) if prev else i, rho * per_tok + off)
        return pl.BlockSpec((BAND, GROUP_DIM), imap)

    o, lse = pl.pallas_call(
        body, name=f"attn_fwd_d{d}", grid=(d, nb),
        in_specs=[spec(3, g, False), spec(3, g, True), spec(3, g, False), spec(9, 6 + g, True), spec(9, 6 + g, False)],
        out_specs=[spec(1, 0, False), spec(1, 0, False)],
        out_shape=[SDS((l, d * GROUP_DIM), F32), SDS((l, d * GROUP_DIM), F32)],
        compiler_params=pltpu.CompilerParams(dimension_semantics=("parallel", "arbitrary"), vmem_limit_bytes=VMEM_LIMIT),
    )(qv, kv, kv, zv, zv)
    return o.reshape(s, GROUP_DIM), lse.reshape(s, GROUP_DIM)


def _attn_bwd(q, k, z_a, g, d, do, dlse):
    s = q.shape[0]
    l = s // d
    nb = l // BAND
    qv, kv, zv = q.reshape(l, d * ATTN_DIM), k.reshape(l, d * ATTN_DIM), z_a.reshape(l, d * ATTN_COLS)
    dov, dlv = do.reshape(l, d * GROUP_DIM), dlse.reshape(l, d * GROUP_DIM)

    def body(q_ref, kp_ref, kc_ref, vp_ref, vc_ref, do_ref, dl_ref, dq_ref, dk_ref, dv_ref, ck, cv):
        step = pl.program_id(1)
        has_prev = step < nb - 1

        @pl.when(step == 0)
        def _():
            ck[...] = jnp.zeros_like(ck)
            cv[...] = jnp.zeros_like(cv)

        for h in range(GROUP_DIM // HEAD):
            sl = slice(h * HEAD, (h + 1) * HEAD)
            _, vjp = jax.vjp(functools.partial(_attn_block, has_prev=has_prev), q_ref[:, sl].astype(F32),
                             kp_ref[:, sl].astype(F32), kc_ref[:, sl].astype(F32), vp_ref[:, sl], vc_ref[:, sl])
            dq, dkp, dkc, dvp, dvc = vjp((do_ref[:, sl], dl_ref[:, sl]))
            dq_ref[:, sl] = dq
            dk_ref[:, sl] = dkc + ck[:, sl]
            dv_ref[:, sl] = dvc + cv[:, sl]
            ck[:, sl] = dkp
            cv[:, sl] = dvp

    def spec(per_tok, off, prev):
        def imap(rho, i):
            blk = nb - 1 - i
            return (jnp.maximum(blk - 1, 0) if prev else blk, rho * per_tok + off)
        return pl.BlockSpec((BAND, GROUP_DIM), imap)

    dq, dk, dv = pl.pallas_call(
        body, name=f"attn_bwd_d{d}", grid=(d, nb),
        in_specs=[spec(3, g, False), spec(3, g, True), spec(3, g, False), spec(9, 6 + g, True), spec(9, 6 + g, False),
                  spec(1, 0, False), spec(1, 0, False)],
        out_specs=[spec(1, 0, False)] * 3,
        out_shape=[SDS((l, d * GROUP_DIM), F32)] * 3,
        scratch_shapes=[pltpu.VMEM((BAND, GROUP_DIM), F32), pltpu.VMEM((BAND, GROUP_DIM), F32)],
        compiler_params=pltpu.CompilerParams(dimension_semantics=("parallel", "arbitrary"), vmem_limit_bytes=VMEM_LIMIT),
    )(qv, kv, kv, zv, zv, dov, dlv)
    return dq.reshape(s, GROUP_DIM), dk.reshape(s, GROUP_DIM), dv.reshape(s, GROUP_DIM)


def _coords():
    return lax.axis_index("x"), lax.axis_index("y"), lax.axis_index("c")


_CHIP_FLIPS = ((1, 0), (0, 1), (1, 1))


def _flip(v, f):
    return 1 - v if f else v


def _gather_weights(shard):
    rows = shard.shape[0]
    half = rows // 2

    def body(src, out, send_sems, recv_sems):
        x, y, c = _coords()
        me = 2 * x + y
        mine = pl.ds(pl.multiple_of(c * half, 16), half)

        def chip_of(f):
            return _flip(x, f[0]), _flip(y, f[1])

        def over_ici(kk):
            px, py = chip_of(_CHIP_FLIPS[kk])
            return pltpu.make_async_remote_copy(src_ref=src.at[mine], dst_ref=out.at[me, mine], send_sem=send_sems.at[kk],
                                                recv_sem=recv_sems.at[kk], device_id=(px, py, c), device_id_type=MESH)

        def landed(kk):
            px, py = chip_of(_CHIP_FLIPS[kk])
            there = out.at[2 * px + py, mine]
            return pltpu.make_async_remote_copy(src_ref=there, dst_ref=there, send_sem=send_sems.at[kk],
                                                recv_sem=recv_sems.at[kk], device_id=(px, py, c), device_id_type=MESH)

        def passed_on(kk, sent_by_me):
            px, py = chip_of(_CHIP_FLIPS[kk])
            part = mine if sent_by_me else pl.ds(pl.multiple_of((1 - c) * half, 16), half)
            there = out.at[2 * px + py, part]
            return pltpu.make_async_remote_copy(src_ref=there, dst_ref=there, send_sem=send_sems.at[3 + kk],
                                                recv_sem=recv_sems.at[3 + kk], device_id=(x, y, 1 - c), device_id_type=MESH)

        sends = [over_ici(kk) for kk in range(3)]
        for cp in sends:
            cp.start()
        for kk in range(3):
            landed(kk).wait_recv()
            fwd = passed_on(kk, True)
            fwd.start()
            sends.append(fwd)
        for kk in range(3):
            passed_on(kk, False).wait_recv()
        for cp in sends:
            cp.wait_send()

    out = pl.pallas_call(
        body, name="gather_weights",
        in_specs=[pl.BlockSpec(memory_space=pl.ANY)], out_specs=pl.BlockSpec(memory_space=pl.ANY),
        out_shape=SDS((N_CHIPS, rows, PACK_COLS), shard.dtype),
        scratch_shapes=[pltpu.SemaphoreType.DMA((6,)), pltpu.SemaphoreType.DMA((6,))],
    )(shard)
    x, y, _ = _coords()
    return lax.dynamic_update_slice(out, shard[None], (2 * x + y, 0, 0))


_HBM = pl.BlockSpec(memory_space=pltpu.HBM)
_SEM = pl.BlockSpec(memory_space=pltpu.SEMAPHORE)
_EFFECT = pltpu.SideEffectType.DATAFLOW_SIDE_EFFECTING


def _copies_start(name, bufs, n_sems, issue):
    nb = len(bufs)

    def body(*refs):
        for cp in issue(refs[:nb], refs[nb], refs[nb + 1]):
            cp.start()
        refs[-1][...] = jnp.zeros_like(refs[-1])

    outs = pl.pallas_call(
        body, name=name,
        out_shape=(pltpu.SemaphoreType.DMA((n_sems,)), pltpu.SemaphoreType.DMA((n_sems,)),
                   *[pltpu.HBM(b.shape, b.dtype) for b in bufs], SDS((8, 128), F32)),
        in_specs=[_HBM] * nb, out_specs=(_SEM, _SEM, *[_HBM] * nb, pl.BlockSpec(memory_space=pltpu.VMEM)),
        input_output_aliases={i: 2 + i for i in range(nb)},
        compiler_params=pltpu.CompilerParams(has_side_effects=_EFFECT),
    )(*[pltpu.with_memory_space_constraint(b, pltpu.HBM) for b in bufs])
    return outs[0], outs[1], list(outs[2:2 + nb]), outs[-1]


def _copies_wait(name, bufs, send_sems, recv_sems, after, expect):
    nb = len(bufs)

    def body(*refs):
        sent, received = expect(refs[:nb], refs[nb], refs[nb + 1])
        for cp in sent:
            cp.wait_send()
        for cp in received:
            cp.wait_recv()

    outs = pl.pallas_call(
        body, name=name,
        out_shape=tuple(pltpu.HBM(b.shape, b.dtype) for b in bufs),
        in_specs=(*[_HBM] * nb, _SEM, _SEM, pl.BlockSpec(memory_space=pl.ANY)), out_specs=tuple([_HBM] * nb),
        input_output_aliases={i: i for i in range(nb)},
        compiler_params=pltpu.CompilerParams(has_side_effects=_EFFECT),
    )(*bufs, send_sems, recv_sems, after)
    return list(outs)


def _gather_plan(half, step):
    def parts():
        x, y, c = _coords()
        mine = pl.ds(pl.multiple_of(c * half, 16), half)
        other = pl.ds(pl.multiple_of((1 - c) * half, 16), half)
        chips = [(_flip(x, fx), _flip(y, fy)) for fx, fy in _CHIP_FLIPS]
        return x, y, c, mine, other, chips

    def copy(src, dst, sems, kk, dev):
        return pltpu.make_async_remote_copy(src_ref=src, dst_ref=dst, send_sem=sems[0].at[kk], recv_sem=sems[1].at[kk],
                                            device_id=dev, device_id_type=MESH)

    def issue(refs, send_sems, recv_sems):
        x, y, c, mine, other, chips = parts()
        sems = (send_sems, recv_sems)
        if step == "ici":
            shard, out = refs
            return [copy(shard.at[mine], out.at[2 * x + y, mine], sems, kk, (px, py, c)) for kk, (px, py) in enumerate(chips)]
        (out,) = refs
        return [copy(out.at[2 * px + py, mine], out.at[2 * px + py, mine], sems, kk, (x, y, 1 - c))
                for kk, (px, py) in enumerate(chips)]

    def expect(refs, send_sems, recv_sems):
        x, y, c, mine, other, chips = parts()
        sems = (send_sems, recv_sems)
        sent = issue(refs, send_sems, recv_sems)
        out = refs[-1]
        if step == "ici":
            got = [copy(out.at[2 * px + py, mine], out.at[2 * px + py, mine], sems, kk, (px, py, c))
                   for kk, (px, py) in enumerate(chips)]
        else:
            got = [copy(out.at[2 * px + py, other], out.at[2 * px + py, other], sems, kk, (x, y, 1 - c))
                   for kk, (px, py) in enumerate(chips)]
        return sent, got

    return issue, expect


def _own_shard_in(out, shard):
    x, y, _ = _coords()
    return lax.dynamic_update_slice(out, shard[None], (2 * x + y, 0, 0))


def _rcopy(src, dst, send_sems, recv_sems, kk, dev):
    return pltpu.make_async_remote_copy(src_ref=src, dst_ref=dst, send_sem=send_sems.at[kk], recv_sem=recv_sems.at[kk],
                                        device_id=dev, device_id_type=MESH)


class _GroupReduce:
    def __init__(self, tag, which, c_arr, me_arr):
        self.tag, self.which, self.c_arr, self.me_arr = tag, which, c_arr, me_arr

    def _plan(self, step):
        half = self.half

        def issue(refs, ss, rs):
            x, y, c = _coords()
            sib = (x, y, 1 - c)
            if step == "swap":
                packed, recv = refs
                return [_rcopy(packed.at[:, pl.ds(pl.multiple_of((1 - c) * half, 8), half)], recv, ss, rs, 0, sib)]
            if step == "ici":
                pair_bf, recv = refs
                return [_rcopy(pair_bf.at[2 * _flip(x, fx) + _flip(y, fy)], recv.at[kk], ss, rs, kk, (_flip(x, fx), _flip(y, fy), c))
                        for kk, (fx, fy) in enumerate(_CHIP_FLIPS)]
            red, full = refs
            return [_rcopy(red, full.at[pl.ds(pl.multiple_of(c * half, 8), half)], ss, rs, 0, sib)]

        def expect(refs, ss, rs):
            x, y, c = _coords()
            sib = (x, y, 1 - c)
            land = refs[1]
            if step == "swap":
                got = [_rcopy(land, land, ss, rs, 0, sib)]
            elif step == "ici":
                got = [_rcopy(land.at[kk], land.at[kk], ss, rs, kk, (_flip(x, fx), _flip(y, fy), c))
                       for kk, (fx, fy) in enumerate(_CHIP_FLIPS)]
            else:
                there = land.at[pl.ds(pl.multiple_of((1 - c) * half, 8), half)]
                got = [_rcopy(there, there, ss, rs, 0, sib)]
            return issue(refs, ss, rs), got

        return issue, expect

    def swap_start(self, blocks):
        packed = _pack_big(blocks, self.which)
        self.half = packed.shape[1] // 2
        ss, rs, bufs, tok = _copies_start(f"rs_{self.tag}_swap", [packed, lax.empty((N_CHIPS, self.half, PACK_COLS), F32)], 1,
                                          self._plan("swap")[0])
        self.state = (ss, rs, bufs)
        return tok

    def swap_wait_ici_start(self, after):
        ss, rs, bufs = self.state
        packed, recv = _copies_wait(f"rs_{self.tag}_swap_wait", bufs, ss, rs, after, self._plan("swap")[1])
        self.pair, pair_bf = _add_pair(packed.reshape(N_CHIPS, 2, self.half, PACK_COLS), recv, self.c_arr,
                                       name=f"rs_{self.tag}_add_pair")
        ss, rs, bufs, tok = _copies_start(f"rs_{self.tag}_ici", [pair_bf, lax.empty((3, self.half, PACK_COLS), BF16)], 3,
                                          self._plan("ici")[0])
        self.state = (ss, rs, bufs)
        return tok

    def ici_wait_join_start(self, after):
        ss, rs, bufs = self.state
        _, recv = _copies_wait(f"rs_{self.tag}_ici_wait", bufs, ss, rs, after, self._plan("ici")[1])
        red = _add_chips(self.pair, recv, self.me_arr, name=f"rs_{self.tag}_add_chips")
        ss, rs, bufs, tok = _copies_start(f"rs_{self.tag}_join", [red, lax.empty((2 * self.half, PACK_COLS), F32)], 1,
                                          self._plan("join")[0])
        self.state = (ss, rs, bufs)
        return tok

    def join_wait(self, after):
        ss, rs, bufs = self.state
        red, full = _copies_wait(f"rs_{self.tag}_join_wait", bufs, ss, rs, after, self._plan("join")[1])
        full = lax.dynamic_update_slice(full, red, (lax.axis_index("c") * self.half, 0))
        return _unpack_big(full, self.which)


def _add_pair(g, recv, c_arr, name="grad_add_pair"):
    n, _, half, cols = g.shape
    tr = _row_tile(half)

    def body(c_ref, g_ref, r_ref, o_ref, ob_ref):
        v = g_ref[:, 0] + r_ref[...]
        o_ref[...] = v
        ob_ref[...] = v.astype(BF16)

    return pl.pallas_call(
        body, name=name,
        grid_spec=pltpu.PrefetchScalarGridSpec(
            num_scalar_prefetch=1, grid=(n, half // tr),
            in_specs=[pl.BlockSpec((1, 1, tr, cols), lambda j, i, c_ref: (j, c_ref[0], i, 0)),
                      pl.BlockSpec((1, tr, cols), lambda j, i, c_ref: (j, i, 0))],
            out_specs=[pl.BlockSpec((1, tr, cols), lambda j, i, c_ref: (j, i, 0))] * 2),
        out_shape=[SDS((n, half, cols), F32), SDS((n, half, cols), BF16)],
        compiler_params=pltpu.CompilerParams(dimension_semantics=("parallel", "parallel"), vmem_limit_bytes=VMEM_LIMIT),
    )(c_arr, g, recv)


def _add_chips(a, recv, me_arr, name="grad_add_chips"):
    n, half, cols = a.shape
    tr = _row_tile(half)

    def body(me_ref, a_ref, r_ref, o_ref):
        o_ref[...] = ((a_ref[0] + r_ref[0].astype(F32)) + r_ref[1].astype(F32)) + r_ref[2].astype(F32)

    return pl.pallas_call(
        body, name=name,
        grid_spec=pltpu.PrefetchScalarGridSpec(
            num_scalar_prefetch=1, grid=(half // tr,),
            in_specs=[pl.BlockSpec((1, tr, cols), lambda i, me_ref: (me_ref[0], i, 0)),
                      pl.BlockSpec((3, tr, cols), lambda i, me_ref: (0, i, 0))],
            out_specs=pl.BlockSpec((tr, cols), lambda i, me_ref: (i, 0))),
        out_shape=SDS((half, cols), F32),
        compiler_params=pltpu.CompilerParams(dimension_semantics=("parallel",), vmem_limit_bytes=VMEM_LIMIT),
    )(me_arr, a, recv)


def _all_reduce_small(buf):
    rows, cols = buf.shape

    def body(x_ref, o_ref, gath, send_sems, recv_sems):
        x, y, c = _coords()
        me = 4 * x + 2 * y + c
        gath[me] = x_ref[...]
        sends = []
        for kk in range(1, 8):
            f = (kk >> 2) & 1, (kk >> 1) & 1, kk & 1
            px, py, pc = _flip(x, f[0]), _flip(y, f[1]), _flip(c, f[2])
            cp = pltpu.make_async_remote_copy(src_ref=x_ref, dst_ref=gath.at[me], send_sem=send_sems.at[kk - 1],
                                              recv_sem=recv_sems.at[kk - 1], device_id=(px, py, pc), device_id_type=MESH)
            cp.start()
            sends.append(cp)
        for kk in range(1, 8):
            f = (kk >> 2) & 1, (kk >> 1) & 1, kk & 1
            px, py, pc = _flip(x, f[0]), _flip(y, f[1]), _flip(c, f[2])
            there = gath.at[4 * px + 2 * py + pc]
            pltpu.make_async_remote_copy(src_ref=there, dst_ref=there, send_sem=send_sems.at[kk - 1],
                                         recv_sem=recv_sems.at[kk - 1], device_id=(px, py, pc), device_id_type=MESH).wait_recv()
        for cp in sends:
            cp.wait_send()
        acc = gath[0]
        for j in range(1, 8):
            acc = acc + gath[j]
        o_ref[...] = acc

    return pl.pallas_call(
        body, name="all_reduce_small",
        in_specs=[pl.BlockSpec(memory_space=pltpu.VMEM)], out_specs=pl.BlockSpec(memory_space=pltpu.VMEM),
        out_shape=SDS((rows, cols), F32),
        scratch_shapes=[pltpu.VMEM((8, rows, cols), F32), pltpu.SemaphoreType.DMA((7,)), pltpu.SemaphoreType.DMA((7,))],
    )(buf)


def _adamw_rows(w, g, m, v):
    m = ADAM_B1 * m + (1.0 - ADAM_B1) * g
    v = ADAM_B2 * v + (1.0 - ADAM_B2) * jnp.square(g)
    m_hat = m / (1.0 - ADAM_B1 ** ADAM_STEP)
    v_hat = v / (1.0 - ADAM_B2 ** ADAM_STEP)
    return -ADAM_LR * (m_hat / (jnp.sqrt(v_hat) + ADAM_EPS) + ADAM_WD * w), m, v


def _adamw(w, g, m, v, name, dep=None):
    rows, cols = w.shape
    tm = _pick(rows, (256, 128, 64, 16, 8))
    return _rows_call(_adamw_rows, [(t, 0, cols) for t in (w, g, m, v)], [], [(cols, F32)] * 3, tm=tm, name=name, dep=dep)


def _pack_big(parts, which=BIG):
    return jnp.concatenate([parts[n].reshape(parts[n].shape[:-2] + (-1, PACK_COLS)) for n, _, _ in which], axis=-2)


def _unpack_big(buf, which=BIG):
    out, off = {}, 0
    for n, shp, _ in which:
        r = shp[0] * shp[1] // PACK_COLS
        out[n] = buf[..., off:off + r, :].reshape(buf.shape[:-2] + shp)
        off += r
    return out


def _pack_small(parts):
    flat = jnp.concatenate([parts[n].reshape(-1) for n, _ in SMALL])
    return jnp.pad(flat, (0, SMALL_ROWS * PACK_COLS - flat.shape[0])).reshape(SMALL_ROWS, PACK_COLS)


def _unpack_small(buf, shapes):
    flat, out, off = buf.reshape(-1), {}, 0
    for n, sz in SMALL:
        out[n] = flat[off:off + sz].reshape(shapes[n])
        off += sz
    return out


def _whole(blocks, how):
    n, r, c = blocks.shape
    if how == "row":
        return blocks.reshape(n * r, c)
    return blocks.transpose(1, 0, 2).reshape(r, n * c)


def _split(whole, how):
    if how == "row":
        return whole.reshape(N_CHIPS, whole.shape[0] // N_CHIPS, whole.shape[1])
    r, c = whole.shape
    return whole.reshape(r, N_CHIPS, c // N_CHIPS).transpose(1, 0, 2)


def _ffn_fwd(x, gain, wg, wu, wd, tag):
    h = _rows_call(_rms, [(x, 0, D_MODEL)], [gain], [(D_MODEL, BF16)], tm=256, name=f"{tag}_norm")[0]
    gate = _mm(h, wg, name=f"{tag}_gate")
    up = _mm(h, wu, name=f"{tag}_up")
    nblk, s, f = gate.shape
    act = _rows_call(_swiglu_act, [(gate.reshape(nblk * s, f), 0, f), (up.reshape(nblk * s, f), 0, f)], [], [(f, BF16)],
                     tm=512, name=f"{tag}_act")[0].reshape(nblk, s, f)
    x_new = _mm(act, wd, sum_blocks=True, res=x, alpha=0.5, name=f"{tag}_down")
    return x_new, (x, h, gate, up, act)


def _ffn_bwd(dx_new, dx_new_bf, saved, gain, wg, wu, wd, tag, dep=None):
    x, h, gate, up, act = saved
    nblk, s, f = gate.shape
    d_wd = _mm(act, dx_new_bf, ta=True, alpha=0.5, name=f"{tag}_down_dw")
    dact = _mm(dx_new_bf, wd, tb=True, alpha=0.5, dep=dep, name=f"{tag}_down_dx")

    def act_bwd(gt, ut, ct):
        _, vjp = jax.vjp(_swiglu_act, gt, ut)
        return vjp(ct)

    dgate, dup = _rows_call(act_bwd, [(t.reshape(nblk * s, f), 0, f) for t in (gate, up, dact)], [], [(f, BF16)] * 2,
                            tm=512, name=f"{tag}_act_bwd")
    dgate, dup = dgate.reshape(nblk, s, f), dup.reshape(nblk, s, f)
    d_wg = _mm(h, dgate, ta=True, name=f"{tag}_gate_dw")
    d_wu = _mm(h, dup, ta=True, name=f"{tag}_up_dw")
    dh = _mm(dgate, wg, tb=True, sum_blocks=True, name=f"{tag}_gate_dx")
    dh = _mm(dup, wu, tb=True, sum_blocks=True, res=dh, name=f"{tag}_up_dx")
    dx, dx_bf, dgain = _norm_bwd(x, gain, dh, dx_new, f"{tag}_norm_bwd")
    return dx, dx_bf, dgain, d_wg, d_wu, d_wd


def _norm_bwd(x, gain, dh, dres, name):
    def f(xt, dht, drt, gt):
        _, vjp = jax.vjp(_rms, xt, gt)
        dxt, dgt = vjp(dht)
        return dxt + drt, dxt + drt, dgt

    return _rows_call(f, [(x, 0, D_MODEL), (dh, 0, D_MODEL), (dres, 0, D_MODEL)], [gain], [(D_MODEL, F32), (D_MODEL, BF16)],
                      [(1, D_MODEL)], tm=256, name=name)


def kernel(x, p, positions, ffn1_norm, ffn1_w_gate, ffn1_w_up, ffn1_w_down, mix_norm, w_in, rwkv_mu, rwkv_w0, rwkv_w2, rwkv_a0, rwkv_a2, rwkv_g2, rwkv_k_k, rwkv_k_a, rwkv_r_k, rwkv_gn_w, rwkv_gn_b, q_norm, k_norm, w_br_rwkv, w_br_attn, w_out, ffn2_norm, ffn2_w_gate, ffn2_w_up, ffn2_w_down, ple_norm, ple_w_gate, ple_w_proj, loss_target, m_ffn1_norm, m_ffn1_w_gate, m_ffn1_w_up, m_ffn1_w_down, m_mix_norm, m_w_in, m_rwkv_mu, m_rwkv_w0, m_rwkv_w2, m_rwkv_a0, m_rwkv_a2, m_rwkv_g2, m_rwkv_k_k, m_rwkv_k_a, m_rwkv_r_k, m_rwkv_gn_w, m_rwkv_gn_b, m_q_norm, m_k_norm, m_w_br_rwkv, m_w_br_attn, m_w_out, m_ffn2_norm, m_ffn2_w_gate, m_ffn2_w_up, m_ffn2_w_down, m_ple_norm, m_ple_w_gate, m_ple_w_proj, v_ffn1_norm, v_ffn1_w_gate, v_ffn1_w_up, v_ffn1_w_down, v_mix_norm, v_w_in, v_rwkv_mu, v_rwkv_w0, v_rwkv_w2, v_rwkv_a0, v_rwkv_a2, v_rwkv_g2, v_rwkv_k_k, v_rwkv_k_a, v_rwkv_r_k, v_rwkv_gn_w, v_rwkv_gn_b, v_q_norm, v_k_norm, v_w_br_rwkv, v_w_br_attn, v_w_out, v_ffn2_norm, v_ffn2_w_gate, v_ffn2_w_up, v_ffn2_w_down, v_ple_norm, v_ple_w_gate, v_ple_w_proj):
    args = dict(locals())
    wts = {n: args[n] for n in WEIGHTS}
    mom_m = {n: args["m_" + n] for n in WEIGHTS}
    mom_v = {n: args["v_" + n] for n in WEIGHTS}
    x0, tgt = x[0], loss_target[0]
    s = x0.shape[0]
    p_tok = p[0, 0]

    vec = {n: wts[n].reshape(1, -1) for n, _ in SMALL}
    groups = {"f1": BIG[0:3], "mx": BIG[3:10], "f2": BIG[10:15]}
    shard = {g: _pack_big({n: wts[n][0] for n, _, _ in grp}, grp).astype(BF16) for g, grp in groups.items()}
    plans = {(g, st): _gather_plan(shard[g].shape[0] // 2, st) for g in ("mx", "f2") for st in ("ici", "d2d")}

    def landing(g):
        return lax.empty((N_CHIPS,) + shard[g].shape, BF16)

    wb = _unpack_big(_gather_weights(shard["f1"]), groups["f1"])
    ss_a, rs_a, (sh_mx, out_mx), tok_a = _copies_start("gather_mx_ici", [shard["mx"], landing("mx")], 3, plans["mx", "ici"][0])

    inv_freq = 1.0 / (ROPE_THETA ** (jnp.arange(0, HEAD, 2, dtype=F32) / HEAD))
    ang = positions[0].astype(F32)[:, None] * inv_freq
    cos, sin = jnp.cos(ang), jnp.sin(ang)
    cos2, sin2 = jnp.concatenate([cos, cos], axis=1), jnp.concatenate([-sin, sin], axis=1)

    x1, ffn1_saved = _ffn_fwd(x0, vec["ffn1_norm"] + tok_a[0, 0], wb["ffn1_w_gate"], wb["ffn1_w_up"], wb["ffn1_w_down"], "ffn1")
    sh_mx, out_mx = _copies_wait("gather_mx_ici_wait", [sh_mx, out_mx], ss_a, rs_a, x1, plans["mx", "ici"][1])
    ss_b, rs_b, (out_mx,), tok_b = _copies_start("gather_mx_d2d", [out_mx], 3, plans["mx", "d2d"][0])
    ss_c, rs_c, (sh_f2, out_f2), tok_c = _copies_start("gather_f2_ici", [shard["f2"], landing("f2")], 3, plans["f2", "ici"][0])
    h = _rows_call(_rms, [(x1, 0, D_MODEL)], [vec["mix_norm"] + (tok_b[0, 0] + tok_c[0, 0])], [(D_MODEL, BF16)], tm=256,
                   name="mix_norm")[0]
    (out_mx,) = _copies_wait("gather_mx_d2d_wait", [out_mx], ss_b, rs_b, h, plans["mx", "d2d"][1])
    wb.update(_unpack_big(_own_shard_in(out_mx, sh_mx), groups["mx"]))
    w_in_all = _whole(wb["w_in"], "col")
    w_in_r, w_in_a, w_in_g = w_in_all[:, :RWKV_COLS], w_in_all[:, RWKV_COLS:RWKV_COLS + ATTN_COLS], w_in_all[:, RWKV_COLS + ATTN_COLS:]
    w2, a2, g2 = (_whole(wb[n], "col") for n in ("rwkv_w2", "rwkv_a2", "rwkv_g2"))
    w_brr, w_bra = _whole(wb["w_br_rwkv"], "col"), _whole(wb["w_br_attn"], "col")
    w_o = _whole(wb["w_out"], "row")
    z_r = _mm(h, w_in_r, name="in_rwkv")
    z_a = _mm(h, w_in_a, name="in_attn")
    z_g = _mm(h, w_in_g, name="in_gate")

    zs = _shift_fwd(z_r, vec["rwkv_mu"])
    pre_params = [vec["rwkv_w0"], w2, vec["rwkv_a0"], a2, g2, vec["rwkv_k_k"], vec["rwkv_k_a"]]
    def pre_fwd(*t):
        res = _rwkv_pre(*t)
        return res[1], res[2], res[4], res[5], res[6]

    lw, k2, na, kb, gate_r = _rows_call(pre_fwd, [(zs, 0, RWKV_COLS)], pre_params, [(RWKV_DIM, F32)] * 5, tm=256, name="rwkv_pre")
    y_scan, s0s = _wkv_fwd(zs, lw, k2, na, kb)
    sh_f2, out_f2 = _copies_wait("gather_f2_ici_wait", [sh_f2, out_f2], ss_c, rs_c, y_scan, plans["f2", "ici"][1])
    ss_d, rs_d, (out_f2,), tok_d = _copies_start("gather_f2_d2d", [out_f2], 3, plans["f2", "d2d"][0])
    post_params = [vec["rwkv_gn_w"] + tok_d[0, 0], vec["rwkv_gn_b"], vec["rwkv_r_k"]]
    post_rows = [(y_scan, 0, RWKV_DIM), (zs, 0, RWKV_DIM), (k2, 0, RWKV_DIM), (zs, 2, RWKV_DIM), (gate_r, 0, RWKV_DIM)]
    y_rwkv = _rows_call(_rwkv_post, post_rows, post_params, [(RWKV_DIM, BF16)], tm=256, name="rwkv_post")[0]

    def qk_fwd(qt, kt, ct, st, qg, kg):
        return _norm_rope(qt, qg, ct, st), _norm_rope(kt, kg, ct, st)

    qk_rows = [(z_a, 0, ATTN_DIM), (z_a, 1, ATTN_DIM), (cos2, 0, HEAD), (sin2, 0, HEAD)]
    q_rot, k_rot = _rows_call(qk_fwd, qk_rows, [vec["q_norm"], vec["k_norm"]], [(ATTN_DIM, BF16)] * 2, tm=256, name="attn_pre")
    outs, lses = zip(*[_attn_fwd(q_rot, k_rot, z_a, g, d) for g, d in enumerate(ATTN_DILATIONS)])
    comb_rows = [(t, 0, GROUP_DIM) for t in outs + lses]
    y_attn = _rows_call(_attn_combine, comb_rows, [], [(GROUP_DIM, BF16)], tm=256, name="attn_combine")[0]

    br = _mm(y_rwkv, w_brr, name="branch_rwkv")
    ba = _mm(y_attn, w_bra, name="branch_attn")
    merge_rows = [(z_g, 0, D_MODEL), (z_g, 1, D_MODEL), (br, 0, D_MODEL), (ba, 0, D_MODEL)]
    merged = _rows_call(_merge, merge_rows, [], [(D_MODEL, BF16)], tm=256, name="merge")[0]
    x2 = _mm(merged, w_o, res=x1, name="out_proj")
    (out_f2,) = _copies_wait("gather_f2_d2d_wait", [out_f2], ss_d, rs_d, x2, plans["f2", "d2d"][1])
    wb.update(_unpack_big(_own_shard_in(out_f2, sh_f2), groups["f2"]))
    w_pp, w_pg = _whole(wb["ple_w_proj"], "col"), _whole(wb["ple_w_gate"], "row")
    x3, ffn2_saved = _ffn_fwd(x2, vec["ffn2_norm"], wb["ffn2_w_gate"], wb["ffn2_w_up"], wb["ffn2_w_down"], "ffn2")
    hp = _rows_call(_rms, [(x3, 0, D_MODEL)], [vec["ple_norm"]], [(D_MODEL, BF16)], tm=256, name="ple_norm")[0]
    pg = _mm(hp, w_pg, name="ple_gate")
    pp = _mm(p_tok, w_pp, name="ple_proj")

    def head(x3t, pgt, ppt, tt):
        sg = _sigmoid(pgt)
        err = x3t + sg * ppt - tt
        dx4 = err * (1.0 / D_MODEL)
        loss = 0.5 * jnp.sum(jnp.mean(err * err, axis=-1, keepdims=True), axis=0, keepdims=True)
        return dx4, dx4 * ppt * sg * (1.0 - sg), dx4 * sg, jnp.broadcast_to(loss, (8, 128))

    head_rows = [(x3, 0, D_MODEL), (pg, 0, D_MODEL), (pp, 0, D_MODEL), (tgt, 0, D_MODEL)]
    dx4, dpg, dpp, loss_tile = _rows_call(head, head_rows, [], [(D_MODEL, F32), (D_MODEL, BF16), (D_MODEL, BF16)], [(8, 128)],
                                          tm=256, name="ple_loss")

    xi, yi, ci = _coords()
    c_arr = jnp.reshape(ci, (1,)).astype(jnp.int32)
    me_arr = jnp.reshape(2 * xi + yi, (1,)).astype(jnp.int32)
    how = {n: hw for n, _, hw in BIG}
    red = {g: _GroupReduce(g, groups[g], c_arr, me_arr) for g in groups}
    gw, gs = {}, {}
    gw["ple_w_proj"] = _mm(p_tok, dpp, ta=True, name="ple_proj_dw")
    gw["ple_w_gate"] = _mm(hp, dpg, ta=True, name="ple_gate_dw")
    dhp = _mm(dpg, w_pg, tb=True, name="ple_gate_dx")
    dx3, dx3_bf, gs["ple_norm"] = _norm_bwd(x3, vec["ple_norm"], dhp, dx4, "ple_norm_bwd")
    dx2, dx2_bf, gs["ffn2_norm"], gw["ffn2_w_gate"], gw["ffn2_w_up"], gw["ffn2_w_down"] = _ffn_bwd(
        dx3, dx3_bf, ffn2_saved, vec["ffn2_norm"], wb["ffn2_w_gate"], wb["ffn2_w_up"], wb["ffn2_w_down"], "ffn2")
    f2_blocks = {n: gw[n] for n in ("ffn2_w_gate", "ffn2_w_up", "ffn2_w_down")}
    f2_blocks.update({n: _split(gw[n], how[n]) for n in ("ple_w_gate", "ple_w_proj")})
    tok = red["f2"].swap_start(f2_blocks)
    gw["w_out"] = _mm(merged, dx2_bf, ta=True, name="out_proj_dw")
    dmerged = _mm(dx2_bf, w_o, tb=True, dep=tok, name="out_proj_dx")

    def merge_bwd(zgr, zga, brt, bat, ct):
        _, vjp = jax.vjp(_merge, zgr, zga, brt, bat)
        d1, d2, d3, d4 = vjp(ct)
        return jnp.concatenate([d1, d2], axis=1), d3, d4

    dz_g, dbr, dba = _rows_call(merge_bwd, merge_rows + [(dmerged, 0, D_MODEL)], [],
                                [(2 * D_MODEL, BF16), (D_MODEL, BF16), (D_MODEL, BF16)], tm=256, name="merge_bwd")
    tok = red["f2"].swap_wait_ici_start(dz_g)
    gw["w_br_rwkv"] = _mm(y_rwkv, dbr, ta=True, name="branch_rwkv_dw")
    gw["w_br_attn"] = _mm(y_attn, dba, ta=True, name="branch_attn_dw")
    dy_rwkv = _mm(dbr, w_brr, tb=True, dep=tok, name="branch_rwkv_dx")
    dy_attn = _mm(dba, w_bra, tb=True, dep=tok, name="branch_attn_dx")

    def comb_bwd(*t):
        _, vjp = jax.vjp(_attn_combine, *t[:6])
        return vjp(t[6])

    dcomb = _rows_call(comb_bwd, comb_rows + [(dy_attn, 0, GROUP_DIM)], [], [(GROUP_DIM, F32)] * 6, tm=256, name="attn_combine_bwd")
    dqs, dks, dvs = zip(*[_attn_bwd(q_rot, k_rot, z_a, g, d, dcomb[g], dcomb[3 + g]) for g, d in enumerate(ATTN_DILATIONS)])

    def qk_bwd(qt, kt, ct, st, *rest):
        dq = jnp.concatenate(rest[0:3], axis=1)
        dk = jnp.concatenate(rest[3:6], axis=1)
        qg, kg = rest[9], rest[10]
        _, vjp = jax.vjp(lambda a_, b_, c_, d_: qk_fwd(a_, b_, ct, st, c_, d_), qt, kt, qg, kg)
        dqt, dkt, dqg, dkg = vjp((dq, dk))
        return jnp.concatenate((dqt, dkt) + tuple(rest[6:9]), axis=1), dqg, dkg

    dz_a, gs["q_norm"], gs["k_norm"] = _rows_call(
        qk_bwd, qk_rows + [(t, 0, GROUP_DIM) for t in dqs + dks + dvs], [vec["q_norm"], vec["k_norm"]],
        [(ATTN_COLS, BF16)], [(1, HEAD), (1, HEAD)], tm=256, name="attn_pre_bwd")
    tok = red["f2"].ici_wait_join_start(dz_a)

    def post_bwd(*t):
        _, vjp = jax.vjp(_rwkv_post, *t[:5], *t[6:])
        return vjp(t[5])

    dy_scan, dr_post, dk2_post, dv_post, dgate_r, gs["rwkv_gn_w"], gs["rwkv_gn_b"], gs["rwkv_r_k"] = _rows_call(
        post_bwd, post_rows + [(dy_rwkv, 0, RWKV_DIM)], post_params, [(RWKV_DIM, F32)] * 5, [(1, RWKV_DIM)] * 3,
        tm=256, name="rwkv_post_bwd", dep=tok)
    grad_big = red["f2"].join_wait(dy_scan)
    dr_s, dlw, dk2_s, dv_s, dna, dkb = _wkv_bwd(zs, lw, k2, na, kb, s0s, dy_scan)

    def pre_bwd(zt, c_r1, c_r2, c_lw, c_k1, c_k2, c_v1, c_v2, c_a, c_b, c_g, *params):
        _, vjp = jax.vjp(_rwkv_pre, zt, *params)
        return vjp((c_r1 + c_r2, c_lw, c_k1 + c_k2, c_v1 + c_v2, c_a, c_b, c_g))

    pre_cts = [dr_s, dr_post, dlw, dk2_s, dk2_post, dv_s, dv_post, dna, dkb, dgate_r]
    dzs, gs["rwkv_w0"], g_w2, gs["rwkv_a0"], g_a2, g_g2, gs["rwkv_k_k"], gs["rwkv_k_a"] = _rows_call(
        pre_bwd, [(zs, 0, RWKV_COLS)] + [(t, 0, RWKV_DIM) for t in pre_cts], pre_params, [(RWKV_COLS, F32)],
        [q.shape for q in pre_params], tm=256, name="rwkv_pre_bwd")
    dz_r, gs["rwkv_mu"] = _shift_bwd(z_r, vec["rwkv_mu"], dzs)

    g_w_in = jnp.concatenate([_mm(h, dz_r, ta=True, name="in_rwkv_dw"), _mm(h, dz_a, ta=True, name="in_attn_dw"),
                              _mm(h, dz_g, ta=True, name="in_gate_dw")], axis=1)
    mx_blocks = {"w_in": _split(g_w_in, "col"), "rwkv_w2": _split(g_w2, "col"), "rwkv_a2": _split(g_a2, "col"),
                 "rwkv_g2": _split(g_g2, "col")}
    mx_blocks.update({n: _split(gw[n], how[n]) for n in ("w_br_rwkv", "w_br_attn", "w_out")})
    tok = red["mx"].swap_start(mx_blocks)
    dh = _mm(dz_r, w_in_r, tb=True, dep=tok, name="in_rwkv_dx")
    dh = _mm(dz_a, w_in_a, tb=True, res=dh, name="in_attn_dx")
    dh = _mm(dz_g, w_in_g, tb=True, res=dh, name="in_gate_dx")
    dx1, dx1_bf, gs["mix_norm"] = _norm_bwd(x1, vec["mix_norm"], dh, dx2, "mix_norm_bwd")
    tok = red["mx"].swap_wait_ici_start(dx1_bf)
    dx0, _, gs["ffn1_norm"], gw["ffn1_w_gate"], gw["ffn1_w_up"], gw["ffn1_w_down"] = _ffn_bwd(
        dx1, dx1_bf, ffn1_saved, vec["ffn1_norm"], wb["ffn1_w_gate"], wb["ffn1_w_up"], wb["ffn1_w_down"], "ffn1", dep=tok)
    tok_mx = red["mx"].ici_wait_join_start(dx0)
    tok = red["f1"].swap_start({n: gw[n] for n in ("ffn1_w_gate", "ffn1_w_up", "ffn1_w_down")})

    grads, deltas, new_m, new_v = {}, {}, {}, {}

    def update(names, dep):
        last = None
        for n in names:
            g2d = grad_big[n]
            d_, m_, v_ = _adamw(wts[n][0], g2d, mom_m[n][0], mom_v[n][0], name=f"adamw_{n}", dep=dep)
            grads[n], deltas[n], new_m[n], new_v[n] = g2d[None], d_[None], m_[None], v_[None]
            dep, last = None, d_
        return last

    last = update([n for n, _, _ in groups["f2"]], tok)
    tok = red["f1"].swap_wait_ici_start(last)
    grad_big.update(red["mx"].join_wait(last))
    last = update([n for n, _, _ in groups["mx"]], tok)

    flat = jnp.concatenate([gs[n].reshape(-1) for n, _ in SMALL] + [loss_tile[0, 0:1]])
    small_buf = jnp.pad(flat, (0, SMALL_ROWS * PACK_COLS - flat.shape[0])).reshape(SMALL_ROWS, PACK_COLS)
    small_sum = _all_reduce_small(small_buf)
    n_small = sum(sz for _, sz in SMALL)
    loss = small_sum.reshape(-1)[n_small]
    grad_small = _unpack_small(small_sum, {n: wts[n].shape for n, _ in SMALL})
    d_s, m_s, v_s = _adamw(_pack_small(wts), small_sum, _pack_small(mom_m), _pack_small(mom_v), name="adamw_small")
    shapes = {n: wts[n].shape for n, _ in SMALL}
    d_s, m_s, v_s = _unpack_small(d_s, shapes), _unpack_small(m_s, shapes), _unpack_small(v_s, shapes)
    for n, _ in SMALL:
        grads[n], deltas[n], new_m[n], new_v[n] = grad_small[n], d_s[n], m_s[n], v_s[n]

    tok = red["f1"].ici_wait_join_start(m_s["ffn1_norm"])
    grad_big.update(red["f1"].join_wait(tok))
    update([n for n, _, _ in groups["f1"]], None)

    return (loss, dx0[None], *[grads[n] for n in WEIGHTS], *[deltas[n] for n in WEIGHTS],
            *[new_m[n] for n in WEIGHTS], *[new_v[n] for n in WEIGHTS])
```

```python
import functools

import jax
import jax.numpy as jnp
from jax import lax
from jax.experimental import pallas as pl
from jax.experimental.pallas import tpu as pltpu

F32, BF16 = jnp.float32, jnp.bfloat16
HI = lax.Precision.HIGHEST
MESH = pl.DeviceIdType.MESH
SDS = jax.ShapeDtypeStruct

D_MODEL = 1024
HEAD = 64
RWKV_HEADS = 8
RWKV_DIM = RWKV_HEADS * HEAD
DECAY_LORA, ICLR_LORA, GATE_LORA = 64, 64, 128
GN_EPS = 64e-5
RMS_EPS = 1e-6
ATTN_DILATIONS = (1, 4, 16)
BAND = 128
ATTN_DIM = 768
GROUP_DIM = 256
ROPE_THETA = 10000.0
NEG_INF = -1e30
RWKV_COLS = 3 * RWKV_DIM + DECAY_LORA + ICLR_LORA + GATE_LORA
ATTN_COLS = 3 * ATTN_DIM
ADAM_LR, ADAM_B1, ADAM_B2, ADAM_EPS, ADAM_WD, ADAM_STEP = 0.001, 0.9, 0.999, 1e-08, 0.01, 10

WKV_CHUNK = 64
WKV_HEADS_PER_STEP = 8
N_CHIPS = 4
PACK_COLS = 1024
VMEM_LIMIT = 48 * 1024 * 1024

BIG = (
    ("ffn1_w_gate", (1024, 704), "col"), ("ffn1_w_up", (1024, 704), "col"), ("ffn1_w_down", (704, 1024), "row"),
    ("w_in", (1024, 1536), "col"), ("rwkv_w2", (64, 128), "col"), ("rwkv_a2", (64, 128), "col"),
    ("rwkv_g2", (128, 128), "col"), ("w_br_rwkv", (512, 256), "col"), ("w_br_attn", (256, 256), "col"),
    ("w_out", (256, 1024), "row"), ("ffn2_w_gate", (1024, 704), "col"), ("ffn2_w_up", (1024, 704), "col"),
    ("ffn2_w_down", (704, 1024), "row"), ("ple_w_gate", (256, 1024), "row"), ("ple_w_proj", (256, 256), "col"),
)
SMALL = (
    ("ffn1_norm", 1024), ("mix_norm", 1024), ("ffn2_norm", 1024), ("ple_norm", 1024), ("rwkv_mu", 1792),
    ("rwkv_w0", 512), ("rwkv_a0", 512), ("rwkv_k_k", 512), ("rwkv_k_a", 512), ("rwkv_r_k", 512),
    ("rwkv_gn_w", 512), ("rwkv_gn_b", 512), ("q_norm", 64), ("k_norm", 64),
)
SMALL_ROWS = 16
WEIGHTS = (
    "ffn1_norm", "ffn1_w_gate", "ffn1_w_up", "ffn1_w_down", "mix_norm", "w_in", "rwkv_mu", "rwkv_w0", "rwkv_w2",
    "rwkv_a0", "rwkv_a2", "rwkv_g2", "rwkv_k_k", "rwkv_k_a", "rwkv_r_k", "rwkv_gn_w", "rwkv_gn_b", "q_norm", "k_norm",
    "w_br_rwkv", "w_br_attn", "w_out", "ffn2_norm", "ffn2_w_gate", "ffn2_w_up", "ffn2_w_down", "ple_norm",
    "ple_w_gate", "ple_w_proj",
)


def _row_tile(n, most=704):
    for t in range(most - most % 16, 0, -16):
        if n % t == 0:
            return t
    return n


def _pick(n, cands):
    for c in cands:
        if n % c == 0:
            return c
    return n


def _mm(a, b, *, ta=False, tb=False, sum_blocks=False, out_dtype=F32, res=None, alpha=1.0, dep=None, name):
    flat = a.ndim == 2 and b.ndim == 2
    a3 = a if a.ndim == 3 else a[None]
    b3 = b if b.ndim == 3 else b[None]
    na, nbb = a3.shape[0], b3.shape[0]
    nblk = max(na, nbb)
    kdim, m = (a3.shape[1], a3.shape[2]) if ta else (a3.shape[2], a3.shape[1])
    n = b3.shape[1] if tb else b3.shape[2]
    assert (b3.shape[2] if tb else b3.shape[1]) == kdim
    tm = _pick(m, (1024, 512, 256, 128))
    tn = _pick(n, (1024, 896, 768, 512, 256, 128))
    tk = kdim if kdim <= 2304 else _pick(kdim, (1024, 512, 256, 128))
    nk = kdim // tk
    direct = nk == 1 and not sum_blocks

    if sum_blocks:
        grid = (m // tm, n // tn, nblk, nk)

        def ids(i, c, j, k):
            return i, c, j, k
    else:
        grid = (nblk, m // tm, n // tn, nk)

        def ids(j, i, c, k):
            return i, c, j, k

    def amap(*g):
        i, c, j, k = ids(*g)
        jj = j if na > 1 else 0
        return (jj, k, i) if ta else (jj, i, k)

    def bmap(*g):
        i, c, j, k = ids(*g)
        jj = j if nbb > 1 else 0
        return (jj, c, k) if tb else (jj, k, c)

    if sum_blocks:
        oshape, oblk = (m, n), (tm, tn)

        def omap(*g):
            i, c, j, k = ids(*g)
            return i, c
    else:
        oshape, oblk = (nblk, m, n), (1, tm, tn)

        def omap(*g):
            i, c, j, k = ids(*g)
            return j, i, c

    dn = (((0 if ta else 1,), (1 if tb else 0,)), ((), ()))
    has_res = res is not None

    def body(*refs):
        refs = list(refs)
        acc = None if direct else refs.pop()
        o_ref = refs.pop()
        a_ref, b_ref = refs[0], refs[1]
        r_ref = refs[2] if has_res else None

        def finish(v):
            if alpha != 1.0:
                v = v * alpha
            if has_res:
                v = v + r_ref[...].reshape(v.shape).astype(F32)
            o_ref[...] = v.reshape(o_ref.shape).astype(o_ref.dtype)

        if direct:
            finish(lax.dot_general(a_ref[0].astype(BF16), b_ref[0].astype(BF16), dn, preferred_element_type=F32))
            return
        k = pl.program_id(3)
        if sum_blocks:
            j = pl.program_id(2)
            first = jnp.logical_and(j == 0, k == 0)
            last = jnp.logical_and(j == nblk - 1, k == nk - 1)
        else:
            first, last = k == 0, k == nk - 1

        @pl.when(first)
        def _():
            acc[...] = jnp.zeros_like(acc)

        acc[...] += lax.dot_general(a_ref[0].astype(BF16), b_ref[0].astype(BF16), dn, preferred_element_type=F32)

        @pl.when(last)
        def _():
            finish(acc[...])

    in_specs = [pl.BlockSpec((1, tk, tm) if ta else (1, tm, tk), amap), pl.BlockSpec((1, tn, tk) if tb else (1, tk, tn), bmap)]
    args = [a3, b3]
    if has_res:
        res3 = res if (sum_blocks or res.ndim == 3) else res[None]
        in_specs.append(pl.BlockSpec(oblk, omap))
        args.append(res3)
    if dep is not None:
        in_specs.append(pl.BlockSpec(memory_space=pl.ANY))
        args.append(dep)
    out = pl.pallas_call(
        body,
        name=name,
        grid=grid,
        in_specs=in_specs,
        out_specs=pl.BlockSpec(oblk, omap),
        out_shape=SDS(oshape, out_dtype),
        scratch_shapes=[] if direct else [pltpu.VMEM((tm, tn), F32)],
        compiler_params=pltpu.CompilerParams(
            dimension_semantics=("parallel", "parallel", "arbitrary", "arbitrary") if sum_blocks
            else ("parallel", "parallel", "parallel", "arbitrary"),
            vmem_limit_bytes=VMEM_LIMIT),
    )(*args)
    if flat and not sum_blocks:
        out = out[0]
    return out


def _rows_call(f, rows, params, outs, accs=(), *, tm, name, dep=None):
    s = rows[0][0].shape[0]
    nr, npar, no = len(rows), len(params), len(outs)
    nin = nr + npar + (0 if dep is None else 1)
    in_specs = [pl.BlockSpec((tm, w), functools.partial(lambda i, cb: (i, cb), cb=cb)) for (_, cb, w) in rows]
    in_specs += [pl.BlockSpec(p.shape, functools.partial(lambda i, nd: (0,) * nd, nd=p.ndim)) for p in params]
    if dep is not None:
        in_specs.append(pl.BlockSpec(memory_space=pl.ANY))
    out_shape = [SDS((s, w), dt) for (w, dt) in outs] + [SDS(tuple(sh), F32) for sh in accs]
    out_specs = [pl.BlockSpec((tm, w), lambda i: (i, 0)) for (w, _) in outs]
    out_specs += [pl.BlockSpec(tuple(sh), functools.partial(lambda i, nd: (0,) * nd, nd=len(sh))) for sh in accs]

    def body(*refs):
        rin, pin = refs[:nr], refs[nr:nr + npar]
        oo, ao = refs[nin:nin + no], refs[nin + no:]
        res = f(*[r[...] for r in rin], *[p[...] for p in pin])
        if not isinstance(res, (tuple, list)):
            res = (res,)
        for o_ref, v in zip(oo, res[:no]):
            o_ref[...] = v.astype(o_ref.dtype)
        i = pl.program_id(0)
        for a_ref, v in zip(ao, res[no:]):
            @pl.when(i == 0)
            def _():
                a_ref[...] = jnp.zeros_like(a_ref)

            a_ref[...] += v.reshape(a_ref.shape)

    res = pl.pallas_call(
        body,
        name=name,
        grid=(s // tm,),
        in_specs=in_specs,
        out_specs=out_specs,
        out_shape=out_shape,
        compiler_params=pltpu.CompilerParams(dimension_semantics=("arbitrary",), vmem_limit_bytes=VMEM_LIMIT),
    )(*[r[0] for r in rows], *params, *([] if dep is None else [dep]))
    return res


def _mmv(a, b, mode):
    ca = 0 if mode[0] == "t" else 1
    cb = 1 if mode[1] == "t" else 0
    return lax.dot_general(a.astype(BF16), b.astype(BF16), (((ca,), (cb,)), ((), ())), preferred_element_type=F32)


@functools.partial(jax.custom_vjp, nondiff_argnums=(2,))
def _bdot(a, b, mode):
    return _mmv(a, b, mode)


def _bdot_fwd(a, b, mode):
    return _mmv(a, b, mode), (a, b)


def _bdot_bwd(mode, saved, g):
    a, b = saved
    if mode == "nn":
        return _mmv(g, b, "nt"), _mmv(a, g, "tn")
    if mode == "nt":
        return _mmv(g, b, "nn"), _mmv(g, a, "tn")
    return _mmv(b, g, "nt"), _mmv(a, g, "nn")


_bdot.defvjp(_bdot_fwd, _bdot_bwd)


def _hdot(a, b, mode="nn", precision=HI):
    ca = 0 if mode[0] == "t" else 1
    cb = 1 if mode[1] == "t" else 0
    return lax.dot_general(a, b, (((ca,), (cb,)), ((), ())), precision=precision, preferred_element_type=F32)


def _segsum(x):
    c = x.shape[-1]
    r = lax.broadcasted_iota(jnp.int32, (c, c), 0) >> 6
    q = lax.broadcasted_iota(jnp.int32, (c, c), 1) >> 6
    return _hdot(x, jnp.where(r == q, 1.0, 0.0).astype(F32), precision=lax.Precision.HIGH)


def _sigmoid(x):
    return jax.nn.sigmoid(x)


def _softplus(x):
    return jnp.maximum(x, 0.0) + jnp.log(1.0 + jnp.exp(-jnp.abs(x)))


def _rms(x, gain):
    return x * lax.rsqrt(jnp.mean(x * x, axis=-1, keepdims=True) + RMS_EPS) * gain


def _swiglu_act(gate, up):
    return gate * _sigmoid(gate) * up


def _rwkv_pre(zs, w0, w2, a0, a2, g2, k_k, k_a):
    r, k, v = zs[:, 0:512], zs[:, 512:1024], zs[:, 1024:1536]
    lora = zs[:, 1536:1792]
    wd, ad, gd = lora[:, 0:64], lora[:, 64:128], lora[:, 128:256]
    w = -_softplus(-(w0 + _bdot(jnp.tanh(wd), w2, "nn"))) - 0.5
    a = _sigmoid(a0 + _bdot(ad, a2, "nn"))
    g = _bdot(_sigmoid(gd), g2, "nn")
    kk = k * k_k
    kk = kk * lax.rsqrt(jnp.maximum(_segsum(kk * kk), 1e-24))
    k2 = k * (1.0 + (a - 1.0) * k_a)
    return r, -jnp.exp(w), k2, v, -kk, kk * a, g


def _rwkv_post(y, r, k2, v, g, gn_w, gn_b, r_k):
    mean = _segsum(y) * (1.0 / HEAD)
    yc = y - mean
    var = _segsum(yc * yc) * (1.0 / HEAD)
    yn = yc * lax.rsqrt(var + GN_EPS) * gn_w + gn_b
    bonus = _segsum(r * k2 * r_k) * v
    return (yn + bonus) * g


def _swap_halves(x):
    lane = lax.broadcasted_iota(jnp.int32, x.shape, 1)
    return jnp.where((lane & 32) == 0, jnp.roll(x, -32, axis=1), jnp.roll(x, 32, axis=1))


def _norm_rope(x, gain, cos, sin):
    heads = x.shape[1] // HEAD
    def rep(t):
        return jnp.concatenate([t] * heads, axis=1)

    xn = x * lax.rsqrt(_segsum(x * x) * (1.0 / HEAD) + RMS_EPS) * rep(gain)
    return xn * rep(cos) + _swap_halves(xn) * rep(sin)


def _attn_combine(o0, o1, o2, l0, l1, l2):
    m = jnp.maximum(jnp.maximum(l0, l1), l2)
    e0, e1, e2 = jnp.exp(l0 - m), jnp.exp(l1 - m), jnp.exp(l2 - m)
    return (e0 * o0 + e1 * o1 + e2 * o2) / (e0 + e1 + e2)


def _merge(zgr, zga, br, ba):
    return _sigmoid(zgr) * br + _sigmoid(zga) * ba


def _attn_block(q, kp, kc, vp, vc, has_prev):
    iq = lax.broadcasted_iota(jnp.int32, (BAND, BAND), 0)
    ik = lax.broadcasted_iota(jnp.int32, (BAND, BAND), 1)
    s_c = jnp.where(iq >= ik, _bdot(q, kc, "nt") * (HEAD ** -0.5), NEG_INF)
    s_p = jnp.where(jnp.logical_and(iq <= ik, has_prev), _bdot(q, kp, "nt") * (HEAD ** -0.5), NEG_INF)
    m = lax.stop_gradient(jnp.maximum(jnp.max(s_c, axis=-1, keepdims=True), jnp.max(s_p, axis=-1, keepdims=True)))
    e_c, e_p = jnp.exp(s_c - m), jnp.exp(s_p - m)
    l = jnp.sum(e_c, axis=-1, keepdims=True) + jnp.sum(e_p, axis=-1, keepdims=True)
    o = (_bdot(e_c, vc, "nn") + _bdot(e_p, vp, "nn")) / l
    return o, jnp.broadcast_to(m + jnp.log(l), o.shape)


def _mmb(a, b, cb):
    return lax.dot_general(a.astype(BF16), b.astype(BF16), (((2,), (cb,)), ((0,), (0,))), preferred_element_type=F32)


@functools.partial(jax.custom_vjp, nondiff_argnums=(2,))
def _bdotb1(a, b, cb):
    return _mmb(a, b, cb)


def _bdotb1_fwd(a, b, cb):
    return _mmb(a, b, cb), (a, b)


def _bdotb1_bwd(cb, saved, g):
    a, b = saved
    if cb == 1:
        return _mmb(g, b, 2), _mmb(jnp.swapaxes(a, 1, 2), g, 1)
    return _mmb(g, b, 1), _mmb(jnp.swapaxes(g, 1, 2), a, 1)


_bdotb1.defvjp(_bdotb1_fwd, _bdotb1_bwd)


def _bdotb(a, b, mode="nn", precision=None):
    if mode[0] == "t":
        a = jnp.swapaxes(a, 1, 2)
    cb = 2 if mode[1] == "t" else 1
    if precision is None:
        return _bdotb1(a, b, cb)
    return lax.dot_general(a, b, (((2,), (cb,)), ((0,), (0,))), precision=precision, preferred_element_type=F32)


def _tri_inv(a):
    t = a.shape[-1]
    row = lax.broadcasted_iota(jnp.int32, (1, t, t), 1)
    col = lax.broadcasted_iota(jnp.int32, (1, t, t), 2)
    x = jnp.where(row == col, 1.0, 0.0).astype(F32) + jnp.where(jnp.logical_and(row == col + 1, (row & 1) == 1), a, 0.0)
    sh = 1
    while (1 << sh) < t:
        m = jnp.logical_and((row >> sh) == (col >> sh) + 1, (row >> (sh + 1)) == (col >> (sh + 1)))
        x = x + _bdotb(_bdotb(x, jnp.where(m, a, 0.0), precision=lax.Precision.HIGH), x, precision=lax.Precision.HIGH)
        sh += 1
    return x


def _wkv_chunk(s0, r, lw, k, v, a, b):
    nh, t, _ = r.shape
    row = lax.broadcasted_iota(jnp.int32, (1, t, t), 1)
    col = lax.broadcasted_iota(jnp.int32, (1, t, t), 2)
    incl, strict = row >= col, row > col
    ones = jnp.broadcast_to(jnp.where(incl, 1.0, 0.0).astype(F32), (nh, t, t))
    cum = _bdotb(ones, lw, precision=HI)
    c_end = cum[:, t - 1:t, :]
    e_in, e_ex, e_inv = jnp.exp(cum), jnp.exp(cum - lw), jnp.exp(-cum)
    at, rt, bt, kt = a * e_ex, r * e_in, b * e_inv, k * e_inv
    a_ab = jnp.where(strict, _bdotb(at, bt, "nt"), 0.0)
    a_ak = jnp.where(strict, _bdotb(at, kt, "nt"), 0.0)
    u = _bdotb(_tri_inv(a_ab), _bdotb(at, s0, "nt") + _bdotb(a_ak, v))
    y = (_bdotb(rt, s0, "nt") + _bdotb(jnp.where(incl, _bdotb(rt, bt, "nt"), 0.0), u)
         + _bdotb(jnp.where(incl, _bdotb(rt, kt, "nt"), 0.0), v))
    w_end = jnp.exp(c_end - cum)
    s1 = s0 * jnp.exp(c_end) + _bdotb(u, b * w_end, "tn") + _bdotb(v, k * w_end, "tn")
    return y, s1


def _shift_fwd(z, mu):
    s, c = z.shape
    tc = 256

    def body(z_ref, mu_ref, o_ref):
        zz = z_ref[...]
        row = lax.broadcasted_iota(jnp.int32, zz.shape, 0)
        prev = jnp.where(row == 0, 0.0, pltpu.roll(zz, 1, 0))
        o_ref[...] = zz + (prev - zz) * mu_ref[...]

    return pl.pallas_call(
        body, name="shift_fwd", grid=(c // tc,),
        in_specs=[pl.BlockSpec((s, tc), lambda j: (0, j)), pl.BlockSpec((1, tc), lambda j: (0, j))],
        out_specs=pl.BlockSpec((s, tc), lambda j: (0, j)), out_shape=SDS((s, c), F32),
        compiler_params=pltpu.CompilerParams(dimension_semantics=("parallel",), vmem_limit_bytes=VMEM_LIMIT),
    )(z, mu)


def _shift_bwd(z, mu, dzs):
    s, c = z.shape
    tc = 256

    def body(z_ref, mu_ref, d_ref, dz_ref, dmu_ref):
        zz, d, m = z_ref[...], d_ref[...], mu_ref[...]
        row = lax.broadcasted_iota(jnp.int32, zz.shape, 0)
        prev = jnp.where(row == 0, 0.0, pltpu.roll(zz, 1, 0))
        t = d * m
        nxt = jnp.where(row == s - 1, 0.0, pltpu.roll(t, s - 1, 0))
        dz_ref[...] = (d - t + nxt).astype(dz_ref.dtype)
        dmu_ref[...] = jnp.sum(d * (prev - zz), axis=0, keepdims=True)

    return pl.pallas_call(
        body, name="shift_bwd", grid=(c // tc,),
        in_specs=[pl.BlockSpec((s, tc), lambda j: (0, j)), pl.BlockSpec((1, tc), lambda j: (0, j)),
                  pl.BlockSpec((s, tc), lambda j: (0, j))],
        out_specs=[pl.BlockSpec((s, tc), lambda j: (0, j)), pl.BlockSpec((1, tc), lambda j: (0, j))],
        out_shape=[SDS((s, c), BF16), SDS((1, c), F32)],
        compiler_params=pltpu.CompilerParams(dimension_semantics=("parallel",), vmem_limit_bytes=VMEM_LIMIT),
    )(z, mu, dzs)


def _heads(x, nh):
    return jnp.stack([x[:, h * HEAD:(h + 1) * HEAD] for h in range(nh)], axis=0)


def _unheads(x):
    return jnp.concatenate([x[h] for h in range(x.shape[0])], axis=1)


def _wkv_fwd(zs, lw, k2, na, b):
    s = lw.shape[0]
    t, hb = WKV_CHUNK, WKV_HEADS_PER_STEP
    w = hb * HEAD
    nc, ng = s // t, RWKV_HEADS // hb

    def body(r_ref, v_ref, lw_ref, k_ref, a_ref, b_ref, y_ref, s0_ref, state):
        @pl.when(pl.program_id(1) == 0)
        def _():
            state[...] = jnp.zeros_like(state)

        s0 = state[...]
        s0_ref[0] = s0
        y, s1 = _wkv_chunk(s0, *[_heads(t_ref[...], hb) for t_ref in (r_ref, lw_ref, k_ref, v_ref, a_ref, b_ref)])
        y_ref[...] = _unheads(y)
        state[...] = s1

    def col(off):
        return pl.BlockSpec((t, w), functools.partial(lambda g, i, off: (i, g + off), off=off))

    return pl.pallas_call(
        body, name="wkv_fwd", grid=(ng, nc),
        in_specs=[col(0), col(2 * ng), col(0), col(0), col(0), col(0)],
        out_specs=[col(0), pl.BlockSpec((1, hb, HEAD, HEAD), lambda g, i: (i, g, 0, 0))],
        out_shape=[SDS((s, RWKV_DIM), F32), SDS((nc, RWKV_HEADS, HEAD, HEAD), F32)],
        scratch_shapes=[pltpu.VMEM((hb, HEAD, HEAD), F32)],
        compiler_params=pltpu.CompilerParams(dimension_semantics=("parallel", "arbitrary"), vmem_limit_bytes=VMEM_LIMIT),
    )(zs, zs, lw, k2, na, b)


def _wkv_bwd(zs, lw, k2, na, b, s0s, dy):
    s = lw.shape[0]
    t, hb = WKV_CHUNK, WKV_HEADS_PER_STEP
    w = hb * HEAD
    nc, ng = s // t, RWKV_HEADS // hb

    def body(r_ref, v_ref, lw_ref, k_ref, a_ref, b_ref, s0_ref, dy_ref, dr_ref, dlw_ref, dk_ref, dv_ref, da_ref, db_ref, dstate):
        @pl.when(pl.program_id(1) == 0)
        def _():
            dstate[...] = jnp.zeros_like(dstate)

        _, vjp = jax.vjp(_wkv_chunk, s0_ref[0], *[_heads(t_ref[...], hb) for t_ref in (r_ref, lw_ref, k_ref, v_ref, a_ref, b_ref)])
        grads = vjp((_heads(dy_ref[...], hb), dstate[...]))
        dstate[...] = grads[0]
        for o_ref, gval in zip((dr_ref, dlw_ref, dk_ref, dv_ref, da_ref, db_ref), grads[1:]):
            o_ref[...] = _unheads(gval)

    def col(off):
        return pl.BlockSpec((t, w), functools.partial(lambda g, i, off: (nc - 1 - i, g + off), off=off))

    return pl.pallas_call(
        body, name="wkv_bwd", grid=(ng, nc),
        in_specs=[col(0), col(2 * ng), col(0), col(0), col(0), col(0),
                  pl.BlockSpec((1, hb, HEAD, HEAD), lambda g, i: (nc - 1 - i, g, 0, 0)), col(0)],
        out_specs=[col(0)] * 6,
        out_shape=[SDS((s, RWKV_DIM), F32)] * 6,
        scratch_shapes=[pltpu.VMEM((hb, HEAD, HEAD), F32)],
        compiler_params=pltpu.CompilerParams(dimension_semantics=("parallel", "arbitrary"), vmem_limit_bytes=VMEM_LIMIT),
    )(zs, zs, lw, k2, na, b, s0s, dy)


def _attn_fwd(q, k, z_a, g, d):
    s = q.shape[0]
    l = s // d
    nb = l // BAND
    assert nb * BAND == l
    qv, kv, zv = q.reshape(l, d * ATTN_DIM), k.reshape(l, d * ATTN_DIM), z_a.reshape(l, d * ATTN_COLS)

    def body(q_ref, kp_ref, kc_ref, vp_ref, vc_ref, o_ref, l_ref):
        has_prev = pl.program_id(1) > 0
        for h in range(GROUP_DIM // HEAD):
            sl = slice(h * HEAD, (h + 1) * HEAD)
            o, lse = _attn_block(q_ref[:, sl].astype(F32), kp_ref[:, sl].astype(F32), kc_ref[:, sl].astype(F32),
                                 vp_ref[:, sl], vc_ref[:, sl], has_prev)
            o_ref[:, sl] = o
            l_ref[:, sl] = lse

    def spec(per_tok, off, prev):
        def imap(rho, i):
            return (jnp.maximum(i - 1, 0) if prev else i, rho * per_tok + off)
        return pl.BlockSpec((BAND, GROUP_DIM), imap)

    o, lse = pl.pallas_call(
        body, name=f"attn_fwd_d{d}", grid=(d, nb),
        in_specs=[spec(3, g, False), spec(3, g, True), spec(3, g, False), spec(9, 6 + g, True), spec(9, 6 + g, False)],
        out_specs=[spec(1, 0, False), spec(1, 0, False)],
        out_shape=[SDS((l, d * GROUP_DIM), F32), SDS((l, d * GROUP_DIM), F32)],
        compiler_params=pltpu.CompilerParams(dimension_semantics=("parallel", "arbitrary"), vmem_limit_bytes=VMEM_LIMIT),
    )(qv, kv, kv, zv, zv)
    return o.reshape(s, GROUP_DIM), lse.reshape(s, GROUP_DIM)


def _attn_bwd(q, k, z_a, g, d, do, dlse):
    s = q.shape[0]
    l = s // d
    nb = l // BAND
    qv, kv, zv = q.reshape(l, d * ATTN_DIM), k.reshape(l, d * ATTN_DIM), z_a.reshape(l, d * ATTN_COLS)
    dov, dlv = do.reshape(l, d * GROUP_DIM), dlse.reshape(l, d * GROUP_DIM)

    def body(q_ref, kp_ref, kc_ref, vp_ref, vc_ref, do_ref, dl_ref, dq_ref, dk_ref, dv_ref, ck, cv):
        step = pl.program_id(1)
        has_prev = step < nb - 1

        @pl.when(step == 0)
        def _():
            ck[...] = jnp.zeros_like(ck)
            cv[...] = jnp.zeros_like(cv)

        for h in range(GROUP_DIM // HEAD):
            sl = slice(h * HEAD, (h + 1) * HEAD)
            _, vjp = jax.vjp(functools.partial(_attn_block, has_prev=has_prev), q_ref[:, sl].astype(F32),
                             kp_ref[:, sl].astype(F32), kc_ref[:, sl].astype(F32), vp_ref[:, sl], vc_ref[:, sl])
            dq, dkp, dkc, dvp, dvc = vjp((do_ref[:, sl], dl_ref[:, sl]))
            dq_ref[:, sl] = dq
            dk_ref[:, sl] = dkc + ck[:, sl]
            dv_ref[:, sl] = dvc + cv[:, sl]
            ck[:, sl] = dkp
            cv[:, sl] = dvp

    def spec(per_tok, off, prev):
        def imap(rho, i):
            blk = nb - 1 - i
            return (jnp.maximum(blk - 1, 0) if prev else blk, rho * per_tok + off)
        return pl.BlockSpec((BAND, GROUP_DIM), imap)

    dq, dk, dv = pl.pallas_call(
        body, name=f"attn_bwd_d{d}", grid=(d, nb),
        in_specs=[spec(3, g, False), spec(3, g, True), spec(3, g, False), spec(9, 6 + g, True), spec(9, 6 + g, False),
                  spec(1, 0, False), spec(1, 0, False)],
        out_specs=[spec(1, 0, False)] * 3,
        out_shape=[SDS((l, d * GROUP_DIM), F32)] * 3,
        scratch_shapes=[pltpu.VMEM((BAND, GROUP_DIM), F32), pltpu.VMEM((BAND, GROUP_DIM), F32)],
        compiler_params=pltpu.CompilerParams(dimension_semantics=("parallel", "arbitrary"), vmem_limit_bytes=VMEM_LIMIT),
    )(qv, kv, kv, zv, zv, dov, dlv)
    return dq.reshape(s, GROUP_DIM), dk.reshape(s, GROUP_DIM), dv.reshape(s, GROUP_DIM)


def _coords():
    return lax.axis_index("x"), lax.axis_index("y"), lax.axis_index("c")


_CHIP_FLIPS = ((1, 0), (0, 1), (1, 1))


def _flip(v, f):
    return 1 - v if f else v


def _gather_weights(shard):
    rows = shard.shape[0]
    half = rows // 2

    def body(src, out, send_sems, recv_sems):
        x, y, c = _coords()
        me = 2 * x + y
        mine = pl.ds(pl.multiple_of(c * half, 16), half)

        def chip_of(f):
            return _flip(x, f[0]), _flip(y, f[1])

        def over_ici(kk):
            px, py = chip_of(_CHIP_FLIPS[kk])
            return pltpu.make_async_remote_copy(src_ref=src.at[mine], dst_ref=out.at[me, mine], send_sem=send_sems.at[kk],
                                                recv_sem=recv_sems.at[kk], device_id=(px, py, c), device_id_type=MESH)

        def landed(kk):
            px, py = chip_of(_CHIP_FLIPS[kk])
            there = out.at[2 * px + py, mine]
            return pltpu.make_async_remote_copy(src_ref=there, dst_ref=there, send_sem=send_sems.at[kk],
                                                recv_sem=recv_sems.at[kk], device_id=(px, py, c), device_id_type=MESH)

        def passed_on(kk, sent_by_me):
            px, py = chip_of(_CHIP_FLIPS[kk])
            part = mine if sent_by_me else pl.ds(pl.multiple_of((1 - c) * half, 16), half)
            there = out.at[2 * px + py, part]
            return pltpu.make_async_remote_copy(src_ref=there, dst_ref=there, send_sem=send_sems.at[3 + kk],
                                                recv_sem=recv_sems.at[3 + kk], device_id=(x, y, 1 - c), device_id_type=MESH)

        sends = [over_ici(kk) for kk in range(3)]
        for cp in sends:
            cp.start()
        for kk in range(3):
            landed(kk).wait_recv()
            fwd = passed_on(kk, True)
            fwd.start()
            sends.append(fwd)
        for kk in range(3):
            passed_on(kk, False).wait_recv()
        for cp in sends:
            cp.wait_send()

    out = pl.pallas_call(
        body, name="gather_weights",
        in_specs=[pl.BlockSpec(memory_space=pl.ANY)], out_specs=pl.BlockSpec(memory_space=pl.ANY),
        out_shape=SDS((N_CHIPS, rows, PACK_COLS), shard.dtype),
        scratch_shapes=[pltpu.SemaphoreType.DMA((6,)), pltpu.SemaphoreType.DMA((6,))],
    )(shard)
    x, y, _ = _coords()
    return lax.dynamic_update_slice(out, shard[None], (2 * x + y, 0, 0))


_HBM = pl.BlockSpec(memory_space=pltpu.HBM)
_SEM = pl.BlockSpec(memory_space=pltpu.SEMAPHORE)
_EFFECT = pltpu.SideEffectType.DATAFLOW_SIDE_EFFECTING


def _copies_start(name, bufs, n_sems, issue, after=None):
    nb = len(bufs)
    extra = [] if after is None else [after]

    def body(*refs):
        send_sems, recv_sems = refs[nb + len(extra)], refs[nb + len(extra) + 1]
        for cp in issue(refs[:nb], send_sems, recv_sems):
            cp.start()
        refs[-1][...] = jnp.zeros_like(refs[-1])

    outs = pl.pallas_call(
        body, name=name,
        out_shape=(pltpu.SemaphoreType.DMA((n_sems,)), pltpu.SemaphoreType.DMA((n_sems,)),
                   *[pltpu.HBM(b.shape, b.dtype) for b in bufs], SDS((8, 128), F32)),
        in_specs=[_HBM] * nb + [pl.BlockSpec(memory_space=pl.ANY)] * len(extra),
        out_specs=(_SEM, _SEM, *[_HBM] * nb, pl.BlockSpec(memory_space=pltpu.VMEM)),
        input_output_aliases={i: 2 + i for i in range(nb)},
        compiler_params=pltpu.CompilerParams(has_side_effects=_EFFECT),
    )(*[pltpu.with_memory_space_constraint(b, pltpu.HBM) for b in bufs], *extra)
    return outs[0], outs[1], list(outs[2:2 + nb]), outs[-1]


def _copies_wait(name, bufs, send_sems, recv_sems, after, expect):
    nb = len(bufs)

    def body(*refs):
        sent, received = expect(refs[:nb], refs[nb], refs[nb + 1])
        for cp in sent:
            cp.wait_send()
        for cp in received:
            cp.wait_recv()

    outs = pl.pallas_call(
        body, name=name,
        out_shape=tuple(pltpu.HBM(b.shape, b.dtype) for b in bufs),
        in_specs=(*[_HBM] * nb, _SEM, _SEM, pl.BlockSpec(memory_space=pl.ANY)), out_specs=tuple([_HBM] * nb),
        input_output_aliases={i: i for i in range(nb)},
        compiler_params=pltpu.CompilerParams(has_side_effects=_EFFECT),
    )(*bufs, send_sems, recv_sems, after)
    return list(outs)


def _gather_plan(half, step):
    def parts():
        x, y, c = _coords()
        mine = pl.ds(pl.multiple_of(c * half, 16), half)
        other = pl.ds(pl.multiple_of((1 - c) * half, 16), half)
        chips = [(_flip(x, fx), _flip(y, fy)) for fx, fy in _CHIP_FLIPS]
        return x, y, c, mine, other, chips

    def copy(src, dst, sems, kk, dev):
        return pltpu.make_async_remote_copy(src_ref=src, dst_ref=dst, send_sem=sems[0].at[kk], recv_sem=sems[1].at[kk],
                                            device_id=dev, device_id_type=MESH)

    def issue(refs, send_sems, recv_sems):
        x, y, c, mine, other, chips = parts()
        sems = (send_sems, recv_sems)
        if step == "ici":
            shard, out = refs
            return [copy(shard.at[mine], out.at[2 * x + y, mine], sems, kk, (px, py, c)) for kk, (px, py) in enumerate(chips)]
        (out,) = refs
        return [copy(out.at[2 * px + py, mine], out.at[2 * px + py, mine], sems, kk, (x, y, 1 - c))
                for kk, (px, py) in enumerate(chips)]

    def expect(refs, send_sems, recv_sems):
        x, y, c, mine, other, chips = parts()
        sems = (send_sems, recv_sems)
        sent = issue(refs, send_sems, recv_sems)
        out = refs[-1]
        if step == "ici":
            got = [copy(out.at[2 * px + py, mine], out.at[2 * px + py, mine], sems, kk, (px, py, c))
                   for kk, (px, py) in enumerate(chips)]
        else:
            got = [copy(out.at[2 * px + py, other], out.at[2 * px + py, other], sems, kk, (x, y, 1 - c))
                   for kk, (px, py) in enumerate(chips)]
        return sent, got

    return issue, expect


def _own_shard_in(out, shard):
    x, y, _ = _coords()
    return lax.dynamic_update_slice(out, shard[None], (2 * x + y, 0, 0))


def _rcopy(src, dst, send_sems, recv_sems, kk, dev):
    return pltpu.make_async_remote_copy(src_ref=src, dst_ref=dst, send_sem=send_sems.at[kk], recv_sem=recv_sems.at[kk],
                                        device_id=dev, device_id_type=MESH)


class _GroupReduce:
    def __init__(self, tag, which, c_arr, me_arr):
        self.tag, self.which, self.c_arr, self.me_arr = tag, which, c_arr, me_arr

    def _plan(self, step):
        half = self.half

        def issue(refs, ss, rs):
            x, y, c = _coords()
            sib = (x, y, 1 - c)
            if step == "swap":
                packed, recv = refs
                return [_rcopy(packed.at[:, pl.ds(pl.multiple_of((1 - c) * half, 8), half)], recv, ss, rs, 0, sib)]
            if step == "ici":
                pair_bf, recv = refs
                return [_rcopy(pair_bf.at[2 * _flip(x, fx) + _flip(y, fy)], recv.at[kk], ss, rs, kk, (_flip(x, fx), _flip(y, fy), c))
                        for kk, (fx, fy) in enumerate(_CHIP_FLIPS)]
            red, full = refs
            return [_rcopy(red, full.at[pl.ds(pl.multiple_of(c * half, 8), half)], ss, rs, 0, sib)]

        def expect(refs, ss, rs):
            x, y, c = _coords()
            sib = (x, y, 1 - c)
            land = refs[1]
            if step == "swap":
                got = [_rcopy(land, land, ss, rs, 0, sib)]
            elif step == "ici":
                got = [_rcopy(land.at[kk], land.at[kk], ss, rs, kk, (_flip(x, fx), _flip(y, fy), c))
                       for kk, (fx, fy) in enumerate(_CHIP_FLIPS)]
            else:
                there = land.at[pl.ds(pl.multiple_of((1 - c) * half, 8), half)]
                got = [_rcopy(there, there, ss, rs, 0, sib)]
            return issue(refs, ss, rs), got

        return issue, expect

    def swap_start(self, blocks):
        packed = _pack_big(blocks, self.which)
        self.half = packed.shape[1] // 2
        ss, rs, bufs, tok = _copies_start(f"rs_{self.tag}_swap", [packed, lax.empty((N_CHIPS, self.half, PACK_COLS), F32)], 1,
                                          self._plan("swap")[0])
        self.state = (ss, rs, bufs)
        return tok

    def swap_wait_ici_start(self, after):
        ss, rs, bufs = self.state
        packed, recv = _copies_wait(f"rs_{self.tag}_swap_wait", bufs, ss, rs, after, self._plan("swap")[1])
        self.pair, pair_bf = _add_pair(packed.reshape(N_CHIPS, 2, self.half, PACK_COLS), recv, self.c_arr,
                                       name=f"rs_{self.tag}_add_pair")
        ss, rs, bufs, tok = _copies_start(f"rs_{self.tag}_ici", [pair_bf, lax.empty((3, self.half, PACK_COLS), BF16)], 3,
                                          self._plan("ici")[0])
        self.state = (ss, rs, bufs)
        return tok

    def ici_wait_join_start(self, after):
        ss, rs, bufs = self.state
        _, recv = _copies_wait(f"rs_{self.tag}_ici_wait", bufs, ss, rs, after, self._plan("ici")[1])
        red = _add_chips(self.pair, recv, self.me_arr, name=f"rs_{self.tag}_add_chips")
        ss, rs, bufs, tok = _copies_start(f"rs_{self.tag}_join", [red, lax.empty((2 * self.half, PACK_COLS), F32)], 1,
                                          self._plan("join")[0])
        self.state = (ss, rs, bufs)
        return tok

    def join_wait(self, after):
        ss, rs, bufs = self.state
        red, full = _copies_wait(f"rs_{self.tag}_join_wait", bufs, ss, rs, after, self._plan("join")[1])
        full = lax.dynamic_update_slice(full, red, (lax.axis_index("c") * self.half, 0))
        return _unpack_big(full, self.which)


def _add_pair(g, recv, c_arr, name="grad_add_pair"):
    n, _, half, cols = g.shape
    tr = _row_tile(half)

    def body(c_ref, g_ref, r_ref, o_ref, ob_ref):
        v = g_ref[:, 0] + r_ref[...]
        o_ref[...] = v
        ob_ref[...] = v.astype(BF16)

    return pl.pallas_call(
        body, name=name,
        grid_spec=pltpu.PrefetchScalarGridSpec(
            num_scalar_prefetch=1, grid=(n, half // tr),
            in_specs=[pl.BlockSpec((1, 1, tr, cols), lambda j, i, c_ref: (j, c_ref[0], i, 0)),
                      pl.BlockSpec((1, tr, cols), lambda j, i, c_ref: (j, i, 0))],
            out_specs=[pl.BlockSpec((1, tr, cols), lambda j, i, c_ref: (j, i, 0))] * 2),
        out_shape=[SDS((n, half, cols), F32), SDS((n, half, cols), BF16)],
        compiler_params=pltpu.CompilerParams(dimension_semantics=("parallel", "parallel"), vmem_limit_bytes=VMEM_LIMIT),
    )(c_arr, g, recv)


def _add_chips(a, recv, me_arr, name="grad_add_chips"):
    n, half, cols = a.shape
    tr = _row_tile(half)

    def body(me_ref, a_ref, r_ref, o_ref):
        o_ref[...] = ((a_ref[0] + r_ref[0].astype(F32)) + r_ref[1].astype(F32)) + r_ref[2].astype(F32)

    return pl.pallas_call(
        body, name=name,
        grid_spec=pltpu.PrefetchScalarGridSpec(
            num_scalar_prefetch=1, grid=(half // tr,),
            in_specs=[pl.BlockSpec((1, tr, cols), lambda i, me_ref: (me_ref[0], i, 0)),
                      pl.BlockSpec((3, tr, cols), lambda i, me_ref: (0, i, 0))],
            out_specs=pl.BlockSpec((tr, cols), lambda i, me_ref: (i, 0))),
        out_shape=SDS((half, cols), F32),
        compiler_params=pltpu.CompilerParams(dimension_semantics=("parallel",), vmem_limit_bytes=VMEM_LIMIT),
    )(me_arr, a, recv)


def _all_reduce_small(buf):
    rows, cols = buf.shape

    def body(x_ref, o_ref, gath, send_sems, recv_sems):
        x, y, c = _coords()
        me = 4 * x + 2 * y + c
        gath[me] = x_ref[...]
        sends = []
        for kk in range(1, 8):
            f = (kk >> 2) & 1, (kk >> 1) & 1, kk & 1
            px, py, pc = _flip(x, f[0]), _flip(y, f[1]), _flip(c, f[2])
            cp = pltpu.make_async_remote_copy(src_ref=x_ref, dst_ref=gath.at[me], send_sem=send_sems.at[kk - 1],
                                              recv_sem=recv_sems.at[kk - 1], device_id=(px, py, pc), device_id_type=MESH)
            cp.start()
            sends.append(cp)
        for kk in range(1, 8):
            f = (kk >> 2) & 1, (kk >> 1) & 1, kk & 1
            px, py, pc = _flip(x, f[0]), _flip(y, f[1]), _flip(c, f[2])
            there = gath.at[4 * px + 2 * py + pc]
            pltpu.make_async_remote_copy(src_ref=there, dst_ref=there, send_sem=send_sems.at[kk - 1],
                                         recv_sem=recv_sems.at[kk - 1], device_id=(px, py, pc), device_id_type=MESH).wait_recv()
        for cp in sends:
            cp.wait_send()
        acc = gath[0]
        for j in range(1, 8):
            acc = acc + gath[j]
        o_ref[...] = acc

    return pl.pallas_call(
        body, name="all_reduce_small",
        in_specs=[pl.BlockSpec(memory_space=pltpu.VMEM)], out_specs=pl.BlockSpec(memory_space=pltpu.VMEM),
        out_shape=SDS((rows, cols), F32),
        scratch_shapes=[pltpu.VMEM((8, rows, cols), F32), pltpu.SemaphoreType.DMA((7,)), pltpu.SemaphoreType.DMA((7,))],
    )(buf)


def _adamw_rows(w, g, m, v):
    m = ADAM_B1 * m + (1.0 - ADAM_B1) * g
    v = ADAM_B2 * v + (1.0 - ADAM_B2) * jnp.square(g)
    m_hat = m / (1.0 - ADAM_B1 ** ADAM_STEP)
    v_hat = v / (1.0 - ADAM_B2 ** ADAM_STEP)
    return -ADAM_LR * (m_hat / (jnp.sqrt(v_hat) + ADAM_EPS) + ADAM_WD * w), m, v


def _adamw(w, g, m, v, name, dep=None):
    rows, cols = w.shape
    tm = _pick(rows, (256, 128, 64, 16, 8))
    return _rows_call(_adamw_rows, [(t, 0, cols) for t in (w, g, m, v)], [], [(cols, F32)] * 3, tm=tm, name=name, dep=dep)


def _pack_big(parts, which=BIG):
    return jnp.concatenate([parts[n].reshape(parts[n].shape[:-2] + (-1, PACK_COLS)) for n, _, _ in which], axis=-2)


def _unpack_big(buf, which=BIG):
    out, off = {}, 0
    for n, shp, _ in which:
        r = shp[0] * shp[1] // PACK_COLS
        out[n] = buf[..., off:off + r, :].reshape(buf.shape[:-2] + shp)
        off += r
    return out


def _pack_small(parts):
    flat = jnp.concatenate([parts[n].reshape(-1) for n, _ in SMALL])
    return jnp.pad(flat, (0, SMALL_ROWS * PACK_COLS - flat.shape[0])).reshape(SMALL_ROWS, PACK_COLS)


def _unpack_small(buf, shapes):
    flat, out, off = buf.reshape(-1), {}, 0
    for n, sz in SMALL:
        out[n] = flat[off:off + sz].reshape(shapes[n])
        off += sz
    return out


def _whole(blocks, how):
    n, r, c = blocks.shape
    if how == "row":
        return blocks.reshape(n * r, c)
    return blocks.transpose(1, 0, 2).reshape(r, n * c)


def _split(whole, how):
    if how == "row":
        return whole.reshape(N_CHIPS, whole.shape[0] // N_CHIPS, whole.shape[1])
    r, c = whole.shape
    return whole.reshape(r, N_CHIPS, c // N_CHIPS).transpose(1, 0, 2)


def _ffn_fwd(x, gain, wg, wu, wd, tag):
    h = _rows_call(_rms, [(x, 0, D_MODEL)], [gain], [(D_MODEL, BF16)], tm=256, name=f"{tag}_norm")[0]
    gate = _mm(h, wg, name=f"{tag}_gate")
    up = _mm(h, wu, name=f"{tag}_up")
    nblk, s, f = gate.shape
    act = _rows_call(_swiglu_act, [(gate.reshape(nblk * s, f), 0, f), (up.reshape(nblk * s, f), 0, f)], [], [(f, BF16)],
                     tm=512, name=f"{tag}_act")[0].reshape(nblk, s, f)
    x_new = _mm(act, wd, sum_blocks=True, res=x, alpha=0.5, name=f"{tag}_down")
    return x_new, (x, h, gate, up, act)


def _ffn_bwd(dx_new, dx_new_bf, saved, gain, wg, wu, wd, tag, dep=None):
    x, h, gate, up, act = saved
    nblk, s, f = gate.shape
    d_wd = _mm(act, dx_new_bf, ta=True, alpha=0.5, name=f"{tag}_down_dw")
    dact = _mm(dx_new_bf, wd, tb=True, alpha=0.5, dep=dep, name=f"{tag}_down_dx")

    def act_bwd(gt, ut, ct):
        _, vjp = jax.vjp(_swiglu_act, gt, ut)
        return vjp(ct)

    dgate, dup = _rows_call(act_bwd, [(t.reshape(nblk * s, f), 0, f) for t in (gate, up, dact)], [], [(f, BF16)] * 2,
                            tm=512, name=f"{tag}_act_bwd")
    dgate, dup = dgate.reshape(nblk, s, f), dup.reshape(nblk, s, f)
    d_wg = _mm(h, dgate, ta=True, name=f"{tag}_gate_dw")
    d_wu = _mm(h, dup, ta=True, name=f"{tag}_up_dw")
    dh = _mm(dgate, wg, tb=True, sum_blocks=True, name=f"{tag}_gate_dx")
    dh = _mm(dup, wu, tb=True, sum_blocks=True, res=dh, name=f"{tag}_up_dx")
    dx, dx_bf, dgain = _norm_bwd(x, gain, dh, dx_new, f"{tag}_norm_bwd")
    return dx, dx_bf, dgain, d_wg, d_wu, d_wd


def _norm_bwd(x, gain, dh, dres, name):
    def f(xt, dht, drt, gt):
        _, vjp = jax.vjp(_rms, xt, gt)
        dxt, dgt = vjp(dht)
        return dxt + drt, dxt + drt, dgt

    return _rows_call(f, [(x, 0, D_MODEL), (dh, 0, D_MODEL), (dres, 0, D_MODEL)], [gain], [(D_MODEL, F32), (D_MODEL, BF16)],
                      [(1, D_MODEL)], tm=256, name=name)


def kernel(x, p, positions, ffn1_norm, ffn1_w_gate, ffn1_w_up, ffn1_w_down, mix_norm, w_in, rwkv_mu, rwkv_w0, rwkv_w2, rwkv_a0, rwkv_a2, rwkv_g2, rwkv_k_k, rwkv_k_a, rwkv_r_k, rwkv_gn_w, rwkv_gn_b, q_norm, k_norm, w_br_rwkv, w_br_attn, w_out, ffn2_norm, ffn2_w_gate, ffn2_w_up, ffn2_w_down, ple_norm, ple_w_gate, ple_w_proj, loss_target, m_ffn1_norm, m_ffn1_w_gate, m_ffn1_w_up, m_ffn1_w_down, m_mix_norm, m_w_in, m_rwkv_mu, m_rwkv_w0, m_rwkv_w2, m_rwkv_a0, m_rwkv_a2, m_rwkv_g2, m_rwkv_k_k, m_rwkv_k_a, m_rwkv_r_k, m_rwkv_gn_w, m_rwkv_gn_b, m_q_norm, m_k_norm, m_w_br_rwkv, m_w_br_attn, m_w_out, m_ffn2_norm, m_ffn2_w_gate, m_ffn2_w_up, m_ffn2_w_down, m_ple_norm, m_ple_w_gate, m_ple_w_proj, v_ffn1_norm, v_ffn1_w_gate, v_ffn1_w_up, v_ffn1_w_down, v_mix_norm, v_w_in, v_rwkv_mu, v_rwkv_w0, v_rwkv_w2, v_rwkv_a0, v_rwkv_a2, v_rwkv_g2, v_rwkv_k_k, v_rwkv_k_a, v_rwkv_r_k, v_rwkv_gn_w, v_rwkv_gn_b, v_q_norm, v_k_norm, v_w_br_rwkv, v_w_br_attn, v_w_out, v_ffn2_norm, v_ffn2_w_gate, v_ffn2_w_up, v_ffn2_w_down, v_ple_norm, v_ple_w_gate, v_ple_w_proj):
    args = dict(locals())
    wts = {n: args[n] for n in WEIGHTS}
    mom_m = {n: args["m_" + n] for n in WEIGHTS}
    mom_v = {n: args["v_" + n] for n in WEIGHTS}
    x0, tgt = x[0], loss_target[0]
    s = x0.shape[0]
    p_tok = p[0, 0]

    vec = {n: wts[n].reshape(1, -1) for n, _ in SMALL}
    groups = {"f1": BIG[0:3], "mx": BIG[3:10], "f2": BIG[10:15]}
    shard = {g: _pack_big({n: wts[n][0] for n, _, _ in grp}, grp).astype(BF16) for g, grp in groups.items()}
    plans = {(g, st): _gather_plan(shard[g].shape[0] // 2, st) for g in ("mx", "f2") for st in ("ici", "d2d")}

    def landing(g):
        return lax.empty((N_CHIPS,) + shard[g].shape, BF16)

    gathered_f1 = _gather_weights(shard["f1"])
    wb = _unpack_big(gathered_f1, groups["f1"])
    ss_a, rs_a, (sh_mx, out_mx), tok_a = _copies_start("gather_mx_ici", [shard["mx"], landing("mx")], 3, plans["mx", "ici"][0],
                                                      after=gathered_f1)

    inv_freq = 1.0 / (ROPE_THETA ** (jnp.arange(0, HEAD, 2, dtype=F32) / HEAD))
    ang = positions[0].astype(F32)[:, None] * inv_freq
    cos, sin = jnp.cos(ang), jnp.sin(ang)
    cos2, sin2 = jnp.concatenate([cos, cos], axis=1), jnp.concatenate([-sin, sin], axis=1)

    x1, ffn1_saved = _ffn_fwd(x0, vec["ffn1_norm"] + tok_a[0, 0], wb["ffn1_w_gate"], wb["ffn1_w_up"], wb["ffn1_w_down"], "ffn1")
    sh_mx, out_mx = _copies_wait("gather_mx_ici_wait", [sh_mx, out_mx], ss_a, rs_a, x1, plans["mx", "ici"][1])
    ss_b, rs_b, (out_mx,), tok_b = _copies_start("gather_mx_d2d", [out_mx], 3, plans["mx", "d2d"][0])
    ss_c, rs_c, (sh_f2, out_f2), tok_c = _copies_start("gather_f2_ici", [shard["f2"], landing("f2")], 3, plans["f2", "ici"][0])
    h = _rows_call(_rms, [(x1, 0, D_MODEL)], [vec["mix_norm"] + (tok_b[0, 0] + tok_c[0, 0])], [(D_MODEL, BF16)], tm=256,
                   name="mix_norm")[0]
    (out_mx,) = _copies_wait("gather_mx_d2d_wait", [out_mx], ss_b, rs_b, h, plans["mx", "d2d"][1])
    wb.update(_unpack_big(_own_shard_in(out_mx, sh_mx), groups["mx"]))
    w_in_all = _whole(wb["w_in"], "col")
    w_in_r, w_in_a, w_in_g = w_in_all[:, :RWKV_COLS], w_in_all[:, RWKV_COLS:RWKV_COLS + ATTN_COLS], w_in_all[:, RWKV_COLS + ATTN_COLS:]
    w2, a2, g2 = (_whole(wb[n], "col") for n in ("rwkv_w2", "rwkv_a2", "rwkv_g2"))
    w_brr, w_bra = _whole(wb["w_br_rwkv"], "col"), _whole(wb["w_br_attn"], "col")
    w_o = _whole(wb["w_out"], "row")
    z_r = _mm(h, w_in_r, name="in_rwkv")
    z_a = _mm(h, w_in_a, name="in_attn")
    z_g = _mm(h, w_in_g, name="in_gate")

    zs = _shift_fwd(z_r, vec["rwkv_mu"])
    pre_params = [vec["rwkv_w0"], w2, vec["rwkv_a0"], a2, g2, vec["rwkv_k_k"], vec["rwkv_k_a"]]
    def pre_fwd(*t):
        res = _rwkv_pre(*t)
        return res[1], res[2], res[4], res[5], res[6]

    lw, k2, na, kb, gate_r = _rows_call(pre_fwd, [(zs, 0, RWKV_COLS)], pre_params, [(RWKV_DIM, F32)] * 5, tm=256, name="rwkv_pre")
    y_scan, s0s = _wkv_fwd(zs, lw, k2, na, kb)
    sh_f2, out_f2 = _copies_wait("gather_f2_ici_wait", [sh_f2, out_f2], ss_c, rs_c, y_scan, plans["f2", "ici"][1])
    ss_d, rs_d, (out_f2,), tok_d = _copies_start("gather_f2_d2d", [out_f2], 3, plans["f2", "d2d"][0])
    post_params = [vec["rwkv_gn_w"] + tok_d[0, 0], vec["rwkv_gn_b"], vec["rwkv_r_k"]]
    post_rows = [(y_scan, 0, RWKV_DIM), (zs, 0, RWKV_DIM), (k2, 0, RWKV_DIM), (zs, 2, RWKV_DIM), (gate_r, 0, RWKV_DIM)]
    y_rwkv = _rows_call(_rwkv_post, post_rows, post_params, [(RWKV_DIM, BF16)], tm=256, name="rwkv_post")[0]

    def qk_fwd(qt, kt, ct, st, qg, kg):
        return _norm_rope(qt, qg, ct, st), _norm_rope(kt, kg, ct, st)

    qk_rows = [(z_a, 0, ATTN_DIM), (z_a, 1, ATTN_DIM), (cos2, 0, HEAD), (sin2, 0, HEAD)]
    q_rot, k_rot = _rows_call(qk_fwd, qk_rows, [vec["q_norm"], vec["k_norm"]], [(ATTN_DIM, BF16)] * 2, tm=256, name="attn_pre")
    outs, lses = zip(*[_attn_fwd(q_rot, k_rot, z_a, g, d) for g, d in enumerate(ATTN_DILATIONS)])
    comb_rows = [(t, 0, GROUP_DIM) for t in outs + lses]
    y_attn = _rows_call(_attn_combine, comb_rows, [], [(GROUP_DIM, BF16)], tm=256, name="attn_combine")[0]

    br = _mm(y_rwkv, w_brr, name="branch_rwkv")
    ba = _mm(y_attn, w_bra, name="branch_attn")
    merge_rows = [(z_g, 0, D_MODEL), (z_g, 1, D_MODEL), (br, 0, D_MODEL), (ba, 0, D_MODEL)]
    merged = _rows_call(_merge, merge_rows, [], [(D_MODEL, BF16)], tm=256, name="merge")[0]
    x2 = _mm(merged, w_o, res=x1, name="out_proj")
    (out_f2,) = _copies_wait("gather_f2_d2d_wait", [out_f2], ss_d, rs_d, x2, plans["f2", "d2d"][1])
    wb.update(_unpack_big(_own_shard_in(out_f2, sh_f2), groups["f2"]))
    w_pp, w_pg = _whole(wb["ple_w_proj"], "col"), _whole(wb["ple_w_gate"], "row")
    x3, ffn2_saved = _ffn_fwd(x2, vec["ffn2_norm"], wb["ffn2_w_gate"], wb["ffn2_w_up"], wb["ffn2_w_down"], "ffn2")
    hp = _rows_call(_rms, [(x3, 0, D_MODEL)], [vec["ple_norm"]], [(D_MODEL, BF16)], tm=256, name="ple_norm")[0]
    pg = _mm(hp, w_pg, name="ple_gate")
    pp = _mm(p_tok, w_pp, name="ple_proj")

    def head(x3t, pgt, ppt, tt):
        sg = _sigmoid(pgt)
        err = x3t + sg * ppt - tt
        dx4 = err * (1.0 / D_MODEL)
        loss = 0.5 * jnp.sum(jnp.mean(err * err, axis=-1, keepdims=True), axis=0, keepdims=True)
        return dx4, dx4 * ppt * sg * (1.0 - sg), dx4 * sg, jnp.broadcast_to(loss, (8, 128))

    head_rows = [(x3, 0, D_MODEL), (pg, 0, D_MODEL), (pp, 0, D_MODEL), (tgt, 0, D_MODEL)]
    dx4, dpg, dpp, loss_tile = _rows_call(head, head_rows, [], [(D_MODEL, F32), (D_MODEL, BF16), (D_MODEL, BF16)], [(8, 128)],
                                          tm=256, name="ple_loss")

    xi, yi, ci = _coords()
    c_arr = jnp.reshape(ci, (1,)).astype(jnp.int32)
    me_arr = jnp.reshape(2 * xi + yi, (1,)).astype(jnp.int32)
    how = {n: hw for n, _, hw in BIG}
    red = {g: _GroupReduce(g, groups[g], c_arr, me_arr) for g in groups}
    gw, gs = {}, {}
    gw["ple_w_proj"] = _mm(p_tok, dpp, ta=True, name="ple_proj_dw")
    gw["ple_w_gate"] = _mm(hp, dpg, ta=True, name="ple_gate_dw")
    dhp = _mm(dpg, w_pg, tb=True, name="ple_gate_dx")
    dx3, dx3_bf, gs["ple_norm"] = _norm_bwd(x3, vec["ple_norm"], dhp, dx4, "ple_norm_bwd")
    dx2, dx2_bf, gs["ffn2_norm"], gw["ffn2_w_gate"], gw["ffn2_w_up"], gw["ffn2_w_down"] = _ffn_bwd(
        dx3, dx3_bf, ffn2_saved, vec["ffn2_norm"], wb["ffn2_w_gate"], wb["ffn2_w_up"], wb["ffn2_w_down"], "ffn2")
    f2_blocks = {n: gw[n] for n in ("ffn2_w_gate", "ffn2_w_up", "ffn2_w_down")}
    f2_blocks.update({n: _split(gw[n], how[n]) for n in ("ple_w_gate", "ple_w_proj")})
    tok = red["f2"].swap_start(f2_blocks)
    gw["w_out"] = _mm(merged, dx2_bf, ta=True, name="out_proj_dw")
    dmerged = _mm(dx2_bf, w_o, tb=True, dep=tok, name="out_proj_dx")

    def merge_bwd(zgr, zga, brt, bat, ct):
        _, vjp = jax.vjp(_merge, zgr, zga, brt, bat)
        d1, d2, d3, d4 = vjp(ct)
        return jnp.concatenate([d1, d2], axis=1), d3, d4

    dz_g, dbr, dba = _rows_call(merge_bwd, merge_rows + [(dmerged, 0, D_MODEL)], [],
                                [(2 * D_MODEL, BF16), (D_MODEL, BF16), (D_MODEL, BF16)], tm=256, name="merge_bwd")
    tok = red["f2"].swap_wait_ici_start(dz_g)
    gw["w_br_rwkv"] = _mm(y_rwkv, dbr, ta=True, name="branch_rwkv_dw")
    gw["w_br_attn"] = _mm(y_attn, dba, ta=True, name="branch_attn_dw")
    dy_rwkv = _mm(dbr, w_brr, tb=True, dep=tok, name="branch_rwkv_dx")
    dy_attn = _mm(dba, w_bra, tb=True, dep=tok, name="branch_attn_dx")

    def comb_bwd(*t):
        _, vjp = jax.vjp(_attn_combine, *t[:6])
        return vjp(t[6])

    dcomb = _rows_call(comb_bwd, comb_rows + [(dy_attn, 0, GROUP_DIM)], [], [(GROUP_DIM, F32)] * 6, tm=256, name="attn_combine_bwd")
    dqs, dks, dvs = zip(*[_attn_bwd(q_rot, k_rot, z_a, g, d, dcomb[g], dcomb[3 + g]) for g, d in enumerate(ATTN_DILATIONS)])

    def qk_bwd(qt, kt, ct, st, *rest):
        dq = jnp.concatenate(rest[0:3], axis=1)
        dk = jnp.concatenate(rest[3:6], axis=1)
        qg, kg = rest[9], rest[10]
        _, vjp = jax.vjp(lambda a_, b_, c_, d_: qk_fwd(a_, b_, ct, st, c_, d_), qt, kt, qg, kg)
        dqt, dkt, dqg, dkg = vjp((dq, dk))
        return jnp.concatenate((dqt, dkt) + tuple(rest[6:9]), axis=1), dqg, dkg

    dz_a, gs["q_norm"], gs["k_norm"] = _rows_call(
        qk_bwd, qk_rows + [(t, 0, GROUP_DIM) for t in dqs + dks + dvs], [vec["q_norm"], vec["k_norm"]],
        [(ATTN_COLS, BF16)], [(1, HEAD), (1, HEAD)], tm=256, name="attn_pre_bwd")
    tok = red["f2"].ici_wait_join_start(dz_a)

    def post_bwd(*t):
        _, vjp = jax.vjp(_rwkv_post, *t[:5], *t[6:])
        return vjp(t[5])

    dy_scan, dr_post, dk2_post, dv_post, dgate_r, gs["rwkv_gn_w"], gs["rwkv_gn_b"], gs["rwkv_r_k"] = _rows_call(
        post_bwd, post_rows + [(dy_rwkv, 0, RWKV_DIM)], post_params, [(RWKV_DIM, F32)] * 5, [(1, RWKV_DIM)] * 3,
        tm=256, name="rwkv_post_bwd", dep=tok)
    grad_big = red["f2"].join_wait(dy_scan)
    dr_s, dlw, dk2_s, dv_s, dna, dkb = _wkv_bwd(zs, lw, k2, na, kb, s0s, dy_scan)

    def pre_bwd(zt, c_r1, c_r2, c_lw, c_k1, c_k2, c_v1, c_v2, c_a, c_b, c_g, *params):
        _, vjp = jax.vjp(_rwkv_pre, zt, *params)
        return vjp((c_r1 + c_r2, c_lw, c_k1 + c_k2, c_v1 + c_v2, c_a, c_b, c_g))

    pre_cts = [dr_s, dr_post, dlw, dk2_s, dk2_post, dv_s, dv_post, dna, dkb, dgate_r]
    dzs, gs["rwkv_w0"], g_w2, gs["rwkv_a0"], g_a2, g_g2, gs["rwkv_k_k"], gs["rwkv_k_a"] = _rows_call(
        pre_bwd, [(zs, 0, RWKV_COLS)] + [(t, 0, RWKV_DIM) for t in pre_cts], pre_params, [(RWKV_COLS, F32)],
        [q.shape for q in pre_params], tm=256, name="rwkv_pre_bwd")
    dz_r, gs["rwkv_mu"] = _shift_bwd(z_r, vec["rwkv_mu"], dzs)

    g_w_in = jnp.concatenate([_mm(h, dz_r, ta=True, name="in_rwkv_dw"), _mm(h, dz_a, ta=True, name="in_attn_dw"),
                              _mm(h, dz_g, ta=True, name="in_gate_dw")], axis=1)
    mx_blocks = {"w_in": _split(g_w_in, "col"), "rwkv_w2": _split(g_w2, "col"), "rwkv_a2": _split(g_a2, "col"),
                 "rwkv_g2": _split(g_g2, "col")}
    mx_blocks.update({n: _split(gw[n], how[n]) for n in ("w_br_rwkv", "w_br_attn", "w_out")})
    tok = red["mx"].swap_start(mx_blocks)
    dh = _mm(dz_r, w_in_r, tb=True, dep=tok, name="in_rwkv_dx")
    dh = _mm(dz_a, w_in_a, tb=True, res=dh, name="in_attn_dx")
    dh = _mm(dz_g, w_in_g, tb=True, res=dh, name="in_gate_dx")
    dx1, dx1_bf, gs["mix_norm"] = _norm_bwd(x1, vec["mix_norm"], dh, dx2, "mix_norm_bwd")
    tok = red["mx"].swap_wait_ici_start(dx1_bf)
    dx0, _, gs["ffn1_norm"], gw["ffn1_w_gate"], gw["ffn1_w_up"], gw["ffn1_w_down"] = _ffn_bwd(
        dx1, dx1_bf, ffn1_saved, vec["ffn1_norm"], wb["ffn1_w_gate"], wb["ffn1_w_up"], wb["ffn1_w_down"], "ffn1", dep=tok)
    tok_mx = red["mx"].ici_wait_join_start(dx0)
    tok = red["f1"].swap_start({n: gw[n] for n in ("ffn1_w_gate", "ffn1_w_up", "ffn1_w_down")})

    grads, deltas, new_m, new_v = {}, {}, {}, {}

    def update(names, dep):
        last = None
        for n in names:
            g2d = grad_big[n]
            d_, m_, v_ = _adamw(wts[n][0], g2d, mom_m[n][0], mom_v[n][0], name=f"adamw_{n}", dep=dep)
            grads[n], deltas[n], new_m[n], new_v[n] = g2d[None], d_[None], m_[None], v_[None]
            dep, last = d_, d_
        return last

    last = update([n for n, _, _ in groups["f2"]], tok)
    tok = red["f1"].swap_wait_ici_start(last)
    grad_big.update(red["mx"].join_wait(last))
    last = update([n for n, _, _ in groups["mx"]], tok)

    flat = jnp.concatenate([gs[n].reshape(-1) for n, _ in SMALL] + [loss_tile[0, 0:1]])
    small_buf = jnp.pad(flat, (0, SMALL_ROWS * PACK_COLS - flat.shape[0])).reshape(SMALL_ROWS, PACK_COLS)
    small_sum = _all_reduce_small(small_buf)
    n_small = sum(sz for _, sz in SMALL)
    loss = small_sum.reshape(-1)[n_small]
    grad_small = _unpack_small(small_sum, {n: wts[n].shape for n, _ in SMALL})
    d_s, m_s, v_s = _adamw(_pack_small(wts), small_sum, _pack_small(mom_m), _pack_small(mom_v), name="adamw_small")
    shapes = {n: wts[n].shape for n, _ in SMALL}
    d_s, m_s, v_s = _unpack_small(d_s, shapes), _unpack_small(m_s, shapes), _unpack_small(v_s, shapes)
    for n, _ in SMALL:
        grads[n], deltas[n], new_m[n], new_v[n] = grad_small[n], d_s[n], m_s[n], v_s[n]

    tok = red["f1"].ici_wait_join_start(m_s["ffn1_norm"])
    grad_big.update(red["f1"].join_wait(tok))
    update([n for n, _, _ in groups["f1"]], None)

    return (loss, dx0[None], *[grads[n] for n in WEIGHTS], *[deltas[n] for n in WEIGHTS],
            *[new_m[n] for n in WEIGHTS], *[new_v[n] for n in WEIGHTS])
```

```python
import functools

import jax
import jax.numpy as jnp
from jax import lax
from jax.experimental import pallas as pl
from jax.experimental.pallas import tpu as pltpu

F32, BF16 = jnp.float32, jnp.bfloat16
HI = lax.Precision.HIGHEST
MESH = pl.DeviceIdType.MESH
SDS = jax.ShapeDtypeStruct

D_MODEL = 1024
HEAD = 64
RWKV_HEADS = 8
RWKV_DIM = RWKV_HEADS * HEAD
DECAY_LORA, ICLR_LORA, GATE_LORA = 64, 64, 128
GN_EPS = 64e-5
RMS_EPS = 1e-6
ATTN_DILATIONS = (1, 4, 16)
BAND = 128
ATTN_DIM = 768
GROUP_DIM = 256
ROPE_THETA = 10000.0
NEG_INF = -1e30
RWKV_COLS = 3 * RWKV_DIM + DECAY_LORA + ICLR_LORA + GATE_LORA
ATTN_COLS = 3 * ATTN_DIM
ADAM_LR, ADAM_B1, ADAM_B2, ADAM_EPS, ADAM_WD, ADAM_STEP = 0.001, 0.9, 0.999, 1e-08, 0.01, 10

WKV_CHUNK = 64
WKV_HEADS_PER_STEP = 8
N_CHIPS = 4
PACK_COLS = 1024
VMEM_LIMIT = 48 * 1024 * 1024

LORA = (("rwkv_w2", 64), ("rwkv_a2", 64), ("rwkv_g2", 128))
GROUPS = {
    "f1": (("ffn1_w_gate", "blk", 1024, 704), ("ffn1_w_up", "blk", 1024, 704), ("ffn1_w_down", "blk", 704, 1024)),
    "mx": (("w_in", "col", 1024, 1536), ("lora", "col", 256, 128), ("w_br_rwkv", "col", 512, 256),
           ("w_br_attn", "col", 256, 256), ("w_out", "blk", 256, 1024)),
    "f2": (("ffn2_w_gate", "blk", 1024, 704), ("ffn2_w_up", "blk", 1024, 704), ("ffn2_w_down", "blk", 704, 1024),
           ("ple_w_gate", "blk", 256, 1024), ("ple_w_proj", "col", 256, 256)),
}
SMALL = (
    ("ffn1_norm", 1024), ("mix_norm", 1024), ("ffn2_norm", 1024), ("ple_norm", 1024), ("rwkv_mu", 1792),
    ("rwkv_w0", 512), ("rwkv_a0", 512), ("rwkv_k_k", 512), ("rwkv_k_a", 512), ("rwkv_r_k", 512),
    ("rwkv_gn_w", 512), ("rwkv_gn_b", 512), ("q_norm", 64), ("k_norm", 64),
)
SMALL_ROWS = 16
WEIGHTS = (
    "ffn1_norm", "ffn1_w_gate", "ffn1_w_up", "ffn1_w_down", "mix_norm", "w_in", "rwkv_mu", "rwkv_w0", "rwkv_w2",
    "rwkv_a0", "rwkv_a2", "rwkv_g2", "rwkv_k_k", "rwkv_k_a", "rwkv_r_k", "rwkv_gn_w", "rwkv_gn_b", "q_norm", "k_norm",
    "w_br_rwkv", "w_br_attn", "w_out", "ffn2_norm", "ffn2_w_gate", "ffn2_w_up", "ffn2_w_down", "ple_norm",
    "ple_w_gate", "ple_w_proj",
)


def _row_tile(n, most=704):
    for t in range(most - most % 16, 0, -16):
        if n % t == 0:
            return t
    return n


def _pick(n, cands):
    for c in cands:
        if n % c == 0:
            return c
    return n


def _mm(a, b, *, ta=False, tb=False, sum_blocks=False, out_dtype=F32, res=None, alpha=1.0, dep=None, name):
    flat = a.ndim == 2 and b.ndim == 2
    a3 = a if a.ndim == 3 else a[None]
    b3 = b if b.ndim == 3 else b[None]
    na, nbb = a3.shape[0], b3.shape[0]
    nblk = max(na, nbb)
    kdim, m = (a3.shape[1], a3.shape[2]) if ta else (a3.shape[2], a3.shape[1])
    n = b3.shape[1] if tb else b3.shape[2]
    assert (b3.shape[2] if tb else b3.shape[1]) == kdim
    tm = _pick(m, (1024, 512, 256, 128))
    tn = _pick(n, (1024, 896, 768, 512, 256, 128))
    tk = kdim if kdim <= 2304 else _pick(kdim, (1024, 512, 256, 128))
    nk = kdim // tk
    direct = nk == 1 and not sum_blocks

    if sum_blocks:
        grid = (m // tm, n // tn, nblk, nk)

        def ids(i, c, j, k):
            return i, c, j, k
    else:
        grid = (nblk, m // tm, n // tn, nk)

        def ids(j, i, c, k):
            return i, c, j, k

    def amap(*g):
        i, c, j, k = ids(*g)
        jj = j if na > 1 else 0
        return (jj, k, i) if ta else (jj, i, k)

    def bmap(*g):
        i, c, j, k = ids(*g)
        jj = j if nbb > 1 else 0
        return (jj, c, k) if tb else (jj, k, c)

    if sum_blocks:
        oshape, oblk = (m, n), (tm, tn)

        def omap(*g):
            i, c, j, k = ids(*g)
            return i, c
    else:
        oshape, oblk = (nblk, m, n), (1, tm, tn)

        def omap(*g):
            i, c, j, k = ids(*g)
            return j, i, c

    dn = (((0 if ta else 1,), (1 if tb else 0,)), ((), ()))
    has_res = res is not None

    def body(*refs):
        refs = list(refs)
        acc = None if direct else refs.pop()
        o_ref = refs.pop()
        a_ref, b_ref = refs[0], refs[1]
        r_ref = refs[2] if has_res else None

        def finish(v):
            if alpha != 1.0:
                v = v * alpha
            if has_res:
                v = v + r_ref[...].reshape(v.shape).astype(F32)
            o_ref[...] = v.reshape(o_ref.shape).astype(o_ref.dtype)

        if direct:
            finish(lax.dot_general(a_ref[0].astype(BF16), b_ref[0].astype(BF16), dn, preferred_element_type=F32))
            return
        k = pl.program_id(3)
        if sum_blocks:
            j = pl.program_id(2)
            first = jnp.logical_and(j == 0, k == 0)
            last = jnp.logical_and(j == nblk - 1, k == nk - 1)
        else:
            first, last = k == 0, k == nk - 1

        @pl.when(first)
        def _():
            acc[...] = jnp.zeros_like(acc)

        acc[...] += lax.dot_general(a_ref[0].astype(BF16), b_ref[0].astype(BF16), dn, preferred_element_type=F32)

        @pl.when(last)
        def _():
            finish(acc[...])

    in_specs = [pl.BlockSpec((1, tk, tm) if ta else (1, tm, tk), amap), pl.BlockSpec((1, tn, tk) if tb else (1, tk, tn), bmap)]
    args = [a3, b3]
    if has_res:
        res3 = res if (sum_blocks or res.ndim == 3) else res[None]
        in_specs.append(pl.BlockSpec(oblk, omap))
        args.append(res3)
    if dep is not None:
        in_specs.append(pl.BlockSpec(memory_space=pl.ANY))
        args.append(dep)
    out = pl.pallas_call(
        body,
        name=name,
        grid=grid,
        in_specs=in_specs,
        out_specs=pl.BlockSpec(oblk, omap),
        out_shape=SDS(oshape, out_dtype),
        scratch_shapes=[] if direct else [pltpu.VMEM((tm, tn), F32)],
        compiler_params=pltpu.CompilerParams(
            dimension_semantics=("parallel", "parallel", "arbitrary", "arbitrary") if sum_blocks
            else ("parallel", "parallel", "parallel", "arbitrary"),
            vmem_limit_bytes=VMEM_LIMIT),
    )(*args)
    if flat and not sum_blocks:
        out = out[0]
    return out


def _rows_call(f, rows, params, outs, accs=(), *, tm, name, dep=None):
    s = rows[0][0].shape[0]
    nr, npar, no = len(rows), len(params), len(outs)
    nin = nr + npar + (0 if dep is None else 1)
    in_specs = [pl.BlockSpec((tm, w), functools.partial(lambda i, cb: (i, cb), cb=cb)) for (_, cb, w) in rows]
    in_specs += [pl.BlockSpec(p.shape, functools.partial(lambda i, nd: (0,) * nd, nd=p.ndim)) for p in params]
    if dep is not None:
        in_specs.append(pl.BlockSpec(memory_space=pl.ANY))
    out_shape = [SDS((s, w), dt) for (w, dt) in outs] + [SDS(tuple(sh), F32) for sh in accs]
    out_specs = [pl.BlockSpec((tm, w), lambda i: (i, 0)) for (w, _) in outs]
    out_specs += [pl.BlockSpec(tuple(sh), functools.partial(lambda i, nd: (0,) * nd, nd=len(sh))) for sh in accs]

    def body(*refs):
        rin, pin = refs[:nr], refs[nr:nr + npar]
        oo, ao = refs[nin:nin + no], refs[nin + no:]
        res = f(*[r[...] for r in rin], *[p[...] for p in pin])
        if not isinstance(res, (tuple, list)):
            res = (res,)
        for o_ref, v in zip(oo, res[:no]):
            o_ref[...] = v.astype(o_ref.dtype)
        i = pl.program_id(0)
        for a_ref, v in zip(ao, res[no:]):
            @pl.when(i == 0)
            def _():
                a_ref[...] = jnp.zeros_like(a_ref)

            a_ref[...] += v.reshape(a_ref.shape)

    res = pl.pallas_call(
        body,
        name=name,
        grid=(s // tm,),
        in_specs=in_specs,
        out_specs=out_specs,
        out_shape=out_shape,
        compiler_params=pltpu.CompilerParams(dimension_semantics=("arbitrary",), vmem_limit_bytes=VMEM_LIMIT),
    )(*[r[0] for r in rows], *params, *([] if dep is None else [dep]))
    return res


def _mmv(a, b, mode):
    ca = 0 if mode[0] == "t" else 1
    cb = 1 if mode[1] == "t" else 0
    return lax.dot_general(a.astype(BF16), b.astype(BF16), (((ca,), (cb,)), ((), ())), preferred_element_type=F32)


@functools.partial(jax.custom_vjp, nondiff_argnums=(2,))
def _bdot(a, b, mode):
    return _mmv(a, b, mode)


def _bdot_fwd(a, b, mode):
    return _mmv(a, b, mode), (a, b)


def _bdot_bwd(mode, saved, g):
    a, b = saved
    if mode == "nn":
        return _mmv(g, b, "nt"), _mmv(a, g, "tn")
    if mode == "nt":
        return _mmv(g, b, "nn"), _mmv(g, a, "tn")
    return _mmv(b, g, "nt"), _mmv(a, g, "nn")


_bdot.defvjp(_bdot_fwd, _bdot_bwd)


def _hdot(a, b, mode="nn", precision=HI):
    ca = 0 if mode[0] == "t" else 1
    cb = 1 if mode[1] == "t" else 0
    return lax.dot_general(a, b, (((ca,), (cb,)), ((), ())), precision=precision, preferred_element_type=F32)


def _segsum(x):
    c = x.shape[-1]
    r = lax.broadcasted_iota(jnp.int32, (c, c), 0) >> 6
    q = lax.broadcasted_iota(jnp.int32, (c, c), 1) >> 6
    return _hdot(x, jnp.where(r == q, 1.0, 0.0).astype(F32), precision=lax.Precision.HIGH)


def _sigmoid(x):
    return jax.nn.sigmoid(x)


def _softplus(x):
    return jnp.maximum(x, 0.0) + jnp.log(1.0 + jnp.exp(-jnp.abs(x)))


def _rms(x, gain):
    return x * lax.rsqrt(jnp.mean(x * x, axis=-1, keepdims=True) + RMS_EPS) * gain


def _swiglu_act(gate, up):
    return gate * _sigmoid(gate) * up


def _rwkv_pre(zs, w0, w2, a0, a2, g2, k_k, k_a):
    r, k, v = zs[:, 0:512], zs[:, 512:1024], zs[:, 1024:1536]
    lora = zs[:, 1536:1792]
    wd, ad, gd = lora[:, 0:64], lora[:, 64:128], lora[:, 128:256]
    w = -_softplus(-(w0 + _bdot(jnp.tanh(wd), w2, "nn"))) - 0.5
    a = _sigmoid(a0 + _bdot(ad, a2, "nn"))
    g = _bdot(_sigmoid(gd), g2, "nn")
    kk = k * k_k
    kk = kk * lax.rsqrt(jnp.maximum(_segsum(kk * kk), 1e-24))
    k2 = k * (1.0 + (a - 1.0) * k_a)
    return r, -jnp.exp(w), k2, v, -kk, kk * a, g


def _rwkv_post(y, r, k2, v, g, gn_w, gn_b, r_k):
    mean = _segsum(y) * (1.0 / HEAD)
    yc = y - mean
    var = _segsum(yc * yc) * (1.0 / HEAD)
    yn = yc * lax.rsqrt(var + GN_EPS) * gn_w + gn_b
    bonus = _segsum(r * k2 * r_k) * v
    return (yn + bonus) * g


def _swap_halves(x):
    lane = lax.broadcasted_iota(jnp.int32, x.shape, 1)
    return jnp.where((lane & 32) == 0, jnp.roll(x, -32, axis=1), jnp.roll(x, 32, axis=1))


def _norm_rope(x, gain, cos, sin):
    heads = x.shape[1] // HEAD
    def rep(t):
        return jnp.concatenate([t] * heads, axis=1)

    xn = x * lax.rsqrt(_segsum(x * x) * (1.0 / HEAD) + RMS_EPS) * rep(gain)
    return xn * rep(cos) + _swap_halves(xn) * rep(sin)


def _attn_combine(o0, o1, o2, l0, l1, l2):
    m = jnp.maximum(jnp.maximum(l0, l1), l2)
    e0, e1, e2 = jnp.exp(l0 - m), jnp.exp(l1 - m), jnp.exp(l2 - m)
    return (e0 * o0 + e1 * o1 + e2 * o2) / (e0 + e1 + e2)


def _merge(zgr, zga, br, ba):
    return _sigmoid(zgr) * br + _sigmoid(zga) * ba


def _attn_block(q, kp, kc, vp, vc, has_prev):
    iq = lax.broadcasted_iota(jnp.int32, (BAND, BAND), 0)
    ik = lax.broadcasted_iota(jnp.int32, (BAND, BAND), 1)
    s_c = jnp.where(iq >= ik, _bdot(q, kc, "nt") * (HEAD ** -0.5), NEG_INF)
    s_p = jnp.where(jnp.logical_and(iq <= ik, has_prev), _bdot(q, kp, "nt") * (HEAD ** -0.5), NEG_INF)
    m = lax.stop_gradient(jnp.maximum(jnp.max(s_c, axis=-1, keepdims=True), jnp.max(s_p, axis=-1, keepdims=True)))
    e_c, e_p = jnp.exp(s_c - m), jnp.exp(s_p - m)
    l = jnp.sum(e_c, axis=-1, keepdims=True) + jnp.sum(e_p, axis=-1, keepdims=True)
    o = (_bdot(e_c, vc, "nn") + _bdot(e_p, vp, "nn")) / l
    return o, jnp.broadcast_to(m + jnp.log(l), o.shape)


def _mmb(a, b, cb):
    return lax.dot_general(a.astype(BF16), b.astype(BF16), (((2,), (cb,)), ((0,), (0,))), preferred_element_type=F32)


@functools.partial(jax.custom_vjp, nondiff_argnums=(2,))
def _bdotb1(a, b, cb):
    return _mmb(a, b, cb)


def _bdotb1_fwd(a, b, cb):
    return _mmb(a, b, cb), (a, b)


def _bdotb1_bwd(cb, saved, g):
    a, b = saved
    if cb == 1:
        return _mmb(g, b, 2), _mmb(jnp.swapaxes(a, 1, 2), g, 1)
    return _mmb(g, b, 1), _mmb(jnp.swapaxes(g, 1, 2), a, 1)


_bdotb1.defvjp(_bdotb1_fwd, _bdotb1_bwd)


def _bdotb(a, b, mode="nn", precision=None):
    if mode[0] == "t":
        a = jnp.swapaxes(a, 1, 2)
    cb = 2 if mode[1] == "t" else 1
    if precision is None:
        return _bdotb1(a, b, cb)
    return lax.dot_general(a, b, (((2,), (cb,)), ((0,), (0,))), precision=precision, preferred_element_type=F32)


def _tri_inv(a):
    t = a.shape[-1]
    row = lax.broadcasted_iota(jnp.int32, (1, t, t), 1)
    col = lax.broadcasted_iota(jnp.int32, (1, t, t), 2)
    x = jnp.where(row == col, 1.0, 0.0).astype(F32) + jnp.where(jnp.logical_and(row == col + 1, (row & 1) == 1), a, 0.0)
    sh = 1
    while (1 << sh) < t:
        m = jnp.logical_and((row >> sh) == (col >> sh) + 1, (row >> (sh + 1)) == (col >> (sh + 1)))
        x = x + _bdotb(_bdotb(x, jnp.where(m, a, 0.0), precision=lax.Precision.HIGH), x, precision=lax.Precision.HIGH)
        sh += 1
    return x


def _wkv_chunk(s0, r, lw, k, v, a, b):
    nh, t, _ = r.shape
    row = lax.broadcasted_iota(jnp.int32, (1, t, t), 1)
    col = lax.broadcasted_iota(jnp.int32, (1, t, t), 2)
    incl, strict = row >= col, row > col
    ones = jnp.broadcast_to(jnp.where(incl, 1.0, 0.0).astype(F32), (nh, t, t))
    cum = _bdotb(ones, lw, precision=HI)
    c_end = cum[:, t - 1:t, :]
    e_in, e_ex, e_inv = jnp.exp(cum), jnp.exp(cum - lw), jnp.exp(-cum)
    at, rt, bt, kt = a * e_ex, r * e_in, b * e_inv, k * e_inv
    a_ab = jnp.where(strict, _bdotb(at, bt, "nt"), 0.0)
    a_ak = jnp.where(strict, _bdotb(at, kt, "nt"), 0.0)
    u = _bdotb(_tri_inv(a_ab), _bdotb(at, s0, "nt") + _bdotb(a_ak, v))
    y = (_bdotb(rt, s0, "nt") + _bdotb(jnp.where(incl, _bdotb(rt, bt, "nt"), 0.0), u)
         + _bdotb(jnp.where(incl, _bdotb(rt, kt, "nt"), 0.0), v))
    w_end = jnp.exp(c_end - cum)
    s1 = s0 * jnp.exp(c_end) + _bdotb(u, b * w_end, "tn") + _bdotb(v, k * w_end, "tn")
    return y, s1


def _shift_fwd(z, mu):
    s, c = z.shape
    tc = 256

    def body(z_ref, mu_ref, o_ref):
        zz = z_ref[...]
        row = lax.broadcasted_iota(jnp.int32, zz.shape, 0)
        prev = jnp.where(row == 0, 0.0, pltpu.roll(zz, 1, 0))
        o_ref[...] = zz + (prev - zz) * mu_ref[...]

    return pl.pallas_call(
        body, name="shift_fwd", grid=(c // tc,),
        in_specs=[pl.BlockSpec((s, tc), lambda j: (0, j)), pl.BlockSpec((1, tc), lambda j: (0, j))],
        out_specs=pl.BlockSpec((s, tc), lambda j: (0, j)), out_shape=SDS((s, c), F32),
        compiler_params=pltpu.CompilerParams(dimension_semantics=("parallel",), vmem_limit_bytes=VMEM_LIMIT),
    )(z, mu)


def _shift_bwd(z, mu, dzs):
    s, c = z.shape
    tc = 256

    def body(z_ref, mu_ref, d_ref, dz_ref, dmu_ref):
        zz, d, m = z_ref[...], d_ref[...], mu_ref[...]
        row = lax.broadcasted_iota(jnp.int32, zz.shape, 0)
        prev = jnp.where(row == 0, 0.0, pltpu.roll(zz, 1, 0))
        t = d * m
        nxt = jnp.where(row == s - 1, 0.0, pltpu.roll(t, s - 1, 0))
        dz_ref[...] = (d - t + nxt).astype(dz_ref.dtype)
        dmu_ref[...] = jnp.sum(d * (prev - zz), axis=0, keepdims=True)

    return pl.pallas_call(
        body, name="shift_bwd", grid=(c // tc,),
        in_specs=[pl.BlockSpec((s, tc), lambda j: (0, j)), pl.BlockSpec((1, tc), lambda j: (0, j)),
                  pl.BlockSpec((s, tc), lambda j: (0, j))],
        out_specs=[pl.BlockSpec((s, tc), lambda j: (0, j)), pl.BlockSpec((1, tc), lambda j: (0, j))],
        out_shape=[SDS((s, c), BF16), SDS((1, c), F32)],
        compiler_params=pltpu.CompilerParams(dimension_semantics=("parallel",), vmem_limit_bytes=VMEM_LIMIT),
    )(z, mu, dzs)


def _heads(x, nh):
    return jnp.stack([x[:, h * HEAD:(h + 1) * HEAD] for h in range(nh)], axis=0)


def _unheads(x):
    return jnp.concatenate([x[h] for h in range(x.shape[0])], axis=1)


def _wkv_fwd(zs, lw, k2, na, b):
    s = lw.shape[0]
    t, hb = WKV_CHUNK, WKV_HEADS_PER_STEP
    w = hb * HEAD
    nc, ng = s // t, RWKV_HEADS // hb

    def body(r_ref, v_ref, lw_ref, k_ref, a_ref, b_ref, y_ref, s0_ref, state):
        @pl.when(pl.program_id(1) == 0)
        def _():
            state[...] = jnp.zeros_like(state)

        s0 = state[...]
        s0_ref[0] = s0
        y, s1 = _wkv_chunk(s0, *[_heads(t_ref[...], hb) for t_ref in (r_ref, lw_ref, k_ref, v_ref, a_ref, b_ref)])
        y_ref[...] = _unheads(y)
        state[...] = s1

    def col(off):
        return pl.BlockSpec((t, w), functools.partial(lambda g, i, off: (i, g + off), off=off))

    return pl.pallas_call(
        body, name="wkv_fwd", grid=(ng, nc),
        in_specs=[col(0), col(2 * ng), col(0), col(0), col(0), col(0)],
        out_specs=[col(0), pl.BlockSpec((1, hb, HEAD, HEAD), lambda g, i: (i, g, 0, 0))],
        out_shape=[SDS((s, RWKV_DIM), F32), SDS((nc, RWKV_HEADS, HEAD, HEAD), F32)],
        scratch_shapes=[pltpu.VMEM((hb, HEAD, HEAD), F32)],
        compiler_params=pltpu.CompilerParams(dimension_semantics=("parallel", "arbitrary"), vmem_limit_bytes=VMEM_LIMIT),
    )(zs, zs, lw, k2, na, b)


def _wkv_bwd(zs, lw, k2, na, b, s0s, dy):
    s = lw.shape[0]
    t, hb = WKV_CHUNK, WKV_HEADS_PER_STEP
    w = hb * HEAD
    nc, ng = s // t, RWKV_HEADS // hb

    def body(r_ref, v_ref, lw_ref, k_ref, a_ref, b_ref, s0_ref, dy_ref, dr_ref, dlw_ref, dk_ref, dv_ref, da_ref, db_ref, dstate):
        @pl.when(pl.program_id(1) == 0)
        def _():
            dstate[...] = jnp.zeros_like(dstate)

        _, vjp = jax.vjp(_wkv_chunk, s0_ref[0], *[_heads(t_ref[...], hb) for t_ref in (r_ref, lw_ref, k_ref, v_ref, a_ref, b_ref)])
        grads = vjp((_heads(dy_ref[...], hb), dstate[...]))
        dstate[...] = grads[0]
        for o_ref, gval in zip((dr_ref, dlw_ref, dk_ref, dv_ref, da_ref, db_ref), grads[1:]):
            o_ref[...] = _unheads(gval)

    def col(off):
        return pl.BlockSpec((t, w), functools.partial(lambda g, i, off: (nc - 1 - i, g + off), off=off))

    return pl.pallas_call(
        body, name="wkv_bwd", grid=(ng, nc),
        in_specs=[col(0), col(2 * ng), col(0), col(0), col(0), col(0),
                  pl.BlockSpec((1, hb, HEAD, HEAD), lambda g, i: (nc - 1 - i, g, 0, 0)), col(0)],
        out_specs=[col(0)] * 6,
        out_shape=[SDS((s, RWKV_DIM), F32)] * 6,
        scratch_shapes=[pltpu.VMEM((hb, HEAD, HEAD), F32)],
        compiler_params=pltpu.CompilerParams(dimension_semantics=("parallel", "arbitrary"), vmem_limit_bytes=VMEM_LIMIT),
    )(zs, zs, lw, k2, na, b, s0s, dy)


def _attn_fwd(q, k, z_a, g, d):
    s = q.shape[0]
    l = s // d
    nb = l // BAND
    assert nb * BAND == l
    qv, kv, zv = q.reshape(l, d * ATTN_DIM), k.reshape(l, d * ATTN_DIM), z_a.reshape(l, d * ATTN_COLS)

    def body(q_ref, kp_ref, kc_ref, vp_ref, vc_ref, o_ref, l_ref):
        has_prev = pl.program_id(1) > 0
        for h in range(GROUP_DIM // HEAD):
            sl = slice(h * HEAD, (h + 1) * HEAD)
            o, lse = _attn_block(q_ref[:, sl].astype(F32), kp_ref[:, sl].astype(F32), kc_ref[:, sl].astype(F32),
                                 vp_ref[:, sl], vc_ref[:, sl], has_prev)
            o_ref[:, sl] = o
            l_ref[:, sl] = lse

    def spec(per_tok, off, prev):
        def imap(rho, i):
            return (jnp.maximum(i - 1, 0) if prev else i, rho * per_tok + off)
        return pl.BlockSpec((BAND, GROUP_DIM), imap)

    o, lse = pl.pallas_call(
        body, name=f"attn_fwd_d{d}", grid=(d, nb),
        in_specs=[spec(3, g, False), spec(3, g, True), spec(3, g, False), spec(9, 6 + g, True), spec(9, 6 + g, False)],
        out_specs=[spec(1, 0, False), spec(1, 0, False)],
        out_shape=[SDS((l, d * GROUP_DIM), F32), SDS((l, d * GROUP_DIM), F32)],
        compiler_params=pltpu.CompilerParams(dimension_semantics=("parallel", "arbitrary"), vmem_limit_bytes=VMEM_LIMIT),
    )(qv, kv, kv, zv, zv)
    return o.reshape(s, GROUP_DIM), lse.reshape(s, GROUP_DIM)


def _attn_bwd(q, k, z_a, g, d, do, dlse):
    s = q.shape[0]
    l = s // d
    nb = l // BAND
    qv, kv, zv = q.reshape(l, d * ATTN_DIM), k.reshape(l, d * ATTN_DIM), z_a.reshape(l, d * ATTN_COLS)
    dov, dlv = do.reshape(l, d * GROUP_DIM), dlse.reshape(l, d * GROUP_DIM)

    def body(q_ref, kp_ref, kc_ref, vp_ref, vc_ref, do_ref, dl_ref, dq_ref, dk_ref, dv_ref, ck, cv):
        step = pl.program_id(1)
        has_prev = step < nb - 1

        @pl.when(step == 0)
        def _():
            ck[...] = jnp.zeros_like(ck)
            cv[...] = jnp.zeros_like(cv)

        for h in range(GROUP_DIM // HEAD):
            sl = slice(h * HEAD, (h + 1) * HEAD)
            _, vjp = jax.vjp(functools.partial(_attn_block, has_prev=has_prev), q_ref[:, sl].astype(F32),
                             kp_ref[:, sl].astype(F32), kc_ref[:, sl].astype(F32), vp_ref[:, sl], vc_ref[:, sl])
            dq, dkp, dkc, dvp, dvc = vjp((do_ref[:, sl], dl_ref[:, sl]))
            dq_ref[:, sl] = dq
            dk_ref[:, sl] = dkc + ck[:, sl]
            dv_ref[:, sl] = dvc + cv[:, sl]
            ck[:, sl] = dkp
            cv[:, sl] = dvp

    def spec(per_tok, off, prev):
        def imap(rho, i):
            blk = nb - 1 - i
            return (jnp.maximum(blk - 1, 0) if prev else blk, rho * per_tok + off)
        return pl.BlockSpec((BAND, GROUP_DIM), imap)

    dq, dk, dv = pl.pallas_call(
        body, name=f"attn_bwd_d{d}", grid=(d, nb),
        in_specs=[spec(3, g, False), spec(3, g, True), spec(3, g, False), spec(9, 6 + g, True), spec(9, 6 + g, False),
                  spec(1, 0, False), spec(1, 0, False)],
        out_specs=[spec(1, 0, False)] * 3,
        out_shape=[SDS((l, d * GROUP_DIM), F32)] * 3,
        scratch_shapes=[pltpu.VMEM((BAND, GROUP_DIM), F32), pltpu.VMEM((BAND, GROUP_DIM), F32)],
        compiler_params=pltpu.CompilerParams(dimension_semantics=("parallel", "arbitrary"), vmem_limit_bytes=VMEM_LIMIT),
    )(qv, kv, kv, zv, zv, dov, dlv)
    return dq.reshape(s, GROUP_DIM), dk.reshape(s, GROUP_DIM), dv.reshape(s, GROUP_DIM)


def _coords():
    return lax.axis_index("x"), lax.axis_index("y"), lax.axis_index("c")


_CHIP_FLIPS = ((1, 0), (0, 1), (1, 1))


def _flip(v, f):
    return 1 - v if f else v


def _form(kind, r, c):
    return (N_CHIPS, r, c) if kind == "blk" else (r, N_CHIPS * c)


def _slot(ref, kind, j, rows, c):
    if kind == "blk":
        return ref.at[j] if rows is None else ref.at[j, rows]
    cols = pl.ds(pl.multiple_of(j * c, 128), c)
    return ref.at[:, cols] if rows is None else ref.at[rows, cols]


def _half(r, which, align):
    return pl.ds(pl.multiple_of(which * (r // 2), align), r // 2)


def _rcopy(src, dst, send_sems, recv_sems, kk, dev):
    return pltpu.make_async_remote_copy(src_ref=src, dst_ref=dst, send_sem=send_sems.at[kk], recv_sem=recv_sems.at[kk],
                                        device_id=dev, device_id_type=MESH)


def _gather_plan(specs, step):
    def copies(refs, ss, rs, received):
        x, y, c = _coords()
        out = []
        for w, (kind, r, cc) in enumerate(specs):
            mine, other = _half(r, c, 16), _half(r, 1 - c, 16)
            for kk, (fx, fy) in enumerate(_CHIP_FLIPS):
                px, py = _flip(x, fx), _flip(y, fy)
                if step == "ici":
                    sl = _slot(refs[w], kind, 2 * px + py if received else 2 * x + y, mine, cc)
                    dev = (px, py, c)
                else:
                    sl = _slot(refs[w], kind, 2 * px + py, other if received else mine, cc)
                    dev = (x, y, 1 - c)
                out.append(_rcopy(sl, sl, ss, rs, 3 * w + kk, dev))
        return out

    def issue(refs, ss, rs):
        return copies(refs, ss, rs, False)

    def expect(refs, ss, rs):
        return copies(refs, ss, rs, False), copies(refs, ss, rs, True)

    return issue, expect


def _gather_now(bufs, specs):
    nb = len(bufs)
    ici, d2d = _gather_plan(specs, "ici"), _gather_plan(specs, "d2d")

    def body(*refs):
        outs, (s1, r1, s2, r2) = refs[nb:2 * nb], refs[2 * nb:]
        sent, got = ici[1](outs, s1, r1)
        for cp in sent:
            cp.start()
        for cp in got:
            cp.wait_recv()
        sent2, got2 = d2d[1](outs, s2, r2)
        for cp in sent2:
            cp.start()
        for cp in got2:
            cp.wait_recv()
        for cp in sent + sent2:
            cp.wait_send()

    outs = pl.pallas_call(
        body, name="gather_ffn1",
        in_specs=[pl.BlockSpec(memory_space=pl.ANY)] * nb, out_specs=[pl.BlockSpec(memory_space=pl.ANY)] * nb,
        out_shape=[SDS(b.shape, b.dtype) for b in bufs], input_output_aliases={i: i for i in range(nb)},
        scratch_shapes=[pltpu.SemaphoreType.DMA((3 * nb,))] * 4,
    )(*bufs)
    return list(outs)


_HBM = pl.BlockSpec(memory_space=pltpu.HBM)
_SEM = pl.BlockSpec(memory_space=pltpu.SEMAPHORE)
_EFFECT = pltpu.SideEffectType.DATAFLOW_SIDE_EFFECTING


def _copies_start(name, bufs, n_sems, issue, after=None):
    nb = len(bufs)
    extra = [] if after is None else [after]

    def body(*refs):
        send_sems, recv_sems = refs[nb + len(extra)], refs[nb + len(extra) + 1]
        for cp in issue(refs[:nb], send_sems, recv_sems):
            cp.start()
        refs[-1][...] = jnp.zeros_like(refs[-1])

    outs = pl.pallas_call(
        body, name=name,
        out_shape=(pltpu.SemaphoreType.DMA((n_sems,)), pltpu.SemaphoreType.DMA((n_sems,)),
                   *[pltpu.HBM(b.shape, b.dtype) for b in bufs], SDS((8, 128), F32)),
        in_specs=[_HBM] * nb + [pl.BlockSpec(memory_space=pl.ANY)] * len(extra),
        out_specs=(_SEM, _SEM, *[_HBM] * nb, pl.BlockSpec(memory_space=pltpu.VMEM)),
        input_output_aliases={i: 2 + i for i in range(nb)},
        compiler_params=pltpu.CompilerParams(has_side_effects=_EFFECT),
    )(*[pltpu.with_memory_space_constraint(b, pltpu.HBM) for b in bufs], *extra)
    return outs[0], outs[1], list(outs[2:2 + nb]), outs[-1]


def _copies_wait(name, bufs, send_sems, recv_sems, after, expect):
    nb = len(bufs)

    def body(*refs):
        sent, received = expect(refs[:nb], refs[nb], refs[nb + 1])
        for cp in sent:
            cp.wait_send()
        for cp in received:
            cp.wait_recv()

    outs = pl.pallas_call(
        body, name=name,
        out_shape=tuple(pltpu.HBM(b.shape, b.dtype) for b in bufs),
        in_specs=(*[_HBM] * nb, _SEM, _SEM, pl.BlockSpec(memory_space=pl.ANY)), out_specs=tuple([_HBM] * nb),
        input_output_aliases={i: i for i in range(nb)},
        compiler_params=pltpu.CompilerParams(has_side_effects=_EFFECT),
    )(*bufs, send_sems, recv_sems, after)
    return list(outs)


def _add_pair(g, recv, kind, r, c, c_arr, name):
    h = r // 2
    if kind == "blk":
        tr = _row_tile(h, 512)
        grid = (N_CHIPS, h // tr)
        g_spec = pl.BlockSpec((1, 1, tr, c), lambda j, i, c_ref: (j, c_ref[0], i, 0))
        o_spec = pl.BlockSpec((1, tr, c), lambda j, i, c_ref: (j, i, 0))
        gv, oshape = g.reshape(N_CHIPS, 2, h, c), (N_CHIPS, h, c)
    else:
        tr = _row_tile(h, 64)
        grid = (h // tr,)
        g_spec = pl.BlockSpec((1, tr, N_CHIPS * c), lambda i, c_ref: (c_ref[0], i, 0))
        o_spec = pl.BlockSpec((tr, N_CHIPS * c), lambda i, c_ref: (i, 0))
        gv, oshape = g.reshape(2, h, N_CHIPS * c), (h, N_CHIPS * c)

    def body(c_ref, g_ref, r_ref, o_ref, ob_ref):
        v = (g_ref[:, 0] if kind == "blk" else g_ref[0]) + r_ref[...]
        o_ref[...] = v
        ob_ref[...] = v.astype(BF16)

    return pl.pallas_call(
        body, name=name,
        grid_spec=pltpu.PrefetchScalarGridSpec(num_scalar_prefetch=1, grid=grid, in_specs=[g_spec, o_spec], out_specs=[o_spec] * 2),
        out_shape=[SDS(oshape, F32), SDS(oshape, BF16)],
        compiler_params=pltpu.CompilerParams(vmem_limit_bytes=VMEM_LIMIT),
    )(c_arr, gv, recv)


def _add_chips(pair, recv, kind, r, c, me_arr, name):
    h = r // 2
    tr = _row_tile(h, 512)
    if kind == "blk":
        p_spec = pl.BlockSpec((1, tr, c), lambda i, me_ref: (me_ref[0], i, 0))
    else:
        p_spec = pl.BlockSpec((tr, c), lambda i, me_ref: (i, me_ref[0]))

    def body(me_ref, a_ref, r_ref, o_ref):
        own = a_ref[0] if kind == "blk" else a_ref[...]
        o_ref[...] = ((own + r_ref[0].astype(F32)) + r_ref[1].astype(F32)) + r_ref[2].astype(F32)

    return pl.pallas_call(
        body, name=name,
        grid_spec=pltpu.PrefetchScalarGridSpec(
            num_scalar_prefetch=1, grid=(h // tr,), in_specs=[p_spec, pl.BlockSpec((3, tr, c), lambda i, me_ref: (0, i, 0))],
            out_specs=pl.BlockSpec((tr, c), lambda i, me_ref: (i, 0))),
        out_shape=SDS((h, c), F32),
        compiler_params=pltpu.CompilerParams(vmem_limit_bytes=VMEM_LIMIT),
    )(me_arr, pair, recv)


class _GroupReduce:
    def __init__(self, tag, specs, c_arr, me_arr):
        self.tag, self.specs, self.c_arr, self.me_arr = tag, specs, c_arr, me_arr
        self.n = len(specs)

    def _plan(self, step):
        specs, n = self.specs, self.n

        def copies(refs, ss, rs, received):
            x, y, c = _coords()
            sib, out = (x, y, 1 - c), []
            for w, (_, kind, r, cc) in enumerate(specs):
                src, land = refs[w], refs[n + w]
                if step == "swap":
                    rows = _half(r, 1 - c, 8)
                    part = src.at[:, rows] if kind == "blk" else src.at[rows]
                    out.append(_rcopy(land if received else part, land, ss, rs, w, sib))
                elif step == "ici":
                    for kk, (fx, fy) in enumerate(_CHIP_FLIPS):
                        px, py = _flip(x, fx), _flip(y, fy)
                        part = land.at[kk] if received else _slot(src, kind, 2 * px + py, None, cc)
                        out.append(_rcopy(part, land.at[kk], ss, rs, 3 * w + kk, (px, py, c)))
                else:
                    there = land.at[_half(r, 1 - c if received else c, 8)]
                    out.append(_rcopy(there if received else src, there, ss, rs, w, sib))
            return out

        def issue(refs, ss, rs):
            return copies(refs, ss, rs, False)

        def expect(refs, ss, rs):
            return copies(refs, ss, rs, False), copies(refs, ss, rs, True)

        return issue, expect

    def swap_start(self, grads, after=None):
        lands = [lax.empty(_form(kind, r // 2, c), F32) for _, kind, r, c in self.specs]
        ss, rs, bufs, tok = _copies_start(f"rs_{self.tag}_swap", list(grads) + lands, self.n, self._plan("swap")[0], after=after)
        self.state = (ss, rs, bufs)
        return tok

    def swap_wait_ici_start(self, after):
        ss, rs, bufs = self.state
        bufs = _copies_wait(f"rs_{self.tag}_swap_wait", bufs, ss, rs, after, self._plan("swap")[1])
        pairs = [_add_pair(bufs[w], bufs[self.n + w], kind, r, c, self.c_arr, name=f"rs_{self.tag}_pair_{nm}")
                 for w, (nm, kind, r, c) in enumerate(self.specs)]
        self.pair = [pr[0] for pr in pairs]
        lands = [lax.empty((3, r // 2, c), BF16) for _, _, r, c in self.specs]
        ss, rs, bufs, tok = _copies_start(f"rs_{self.tag}_ici", [pr[1] for pr in pairs] + lands, 3 * self.n, self._plan("ici")[0])
        self.state = (ss, rs, bufs)
        return tok

    def ici_wait_join_start(self, after):
        ss, rs, bufs = self.state
        bufs = _copies_wait(f"rs_{self.tag}_ici_wait", bufs, ss, rs, after, self._plan("ici")[1])
        reds = [_add_chips(self.pair[w], bufs[self.n + w], kind, r, c, self.me_arr, name=f"rs_{self.tag}_chips_{nm}")
                for w, (nm, kind, r, c) in enumerate(self.specs)]
        lands = [lax.empty((r, c), F32) for _, _, r, c in self.specs]
        ss, rs, bufs, tok = _copies_start(f"rs_{self.tag}_join", reds + lands, self.n, self._plan("join")[0])
        self.state = (ss, rs, bufs)
        return tok

    def join_wait(self, after):
        ss, rs, bufs = self.state
        bufs = _copies_wait(f"rs_{self.tag}_join_wait", bufs, ss, rs, after, self._plan("join")[1])
        ci = lax.axis_index("c")
        return {nm: lax.dynamic_update_slice(bufs[self.n + w], bufs[w], (ci * (r // 2), 0))
                for w, (nm, _, r, c) in enumerate(self.specs)}


def _all_reduce_small(buf):
    rows, cols = buf.shape

    def body(x_ref, o_ref, gath, send_sems, recv_sems):
        x, y, c = _coords()
        me = 4 * x + 2 * y + c
        gath[me] = x_ref[...]
        sends = []
        for kk in range(1, 8):
            f = (kk >> 2) & 1, (kk >> 1) & 1, kk & 1
            px, py, pc = _flip(x, f[0]), _flip(y, f[1]), _flip(c, f[2])
            cp = pltpu.make_async_remote_copy(src_ref=x_ref, dst_ref=gath.at[me], send_sem=send_sems.at[kk - 1],
                                              recv_sem=recv_sems.at[kk - 1], device_id=(px, py, pc), device_id_type=MESH)
            cp.start()
            sends.append(cp)
        for kk in range(1, 8):
            f = (kk >> 2) & 1, (kk >> 1) & 1, kk & 1
            px, py, pc = _flip(x, f[0]), _flip(y, f[1]), _flip(c, f[2])
            there = gath.at[4 * px + 2 * py + pc]
            pltpu.make_async_remote_copy(src_ref=there, dst_ref=there, send_sem=send_sems.at[kk - 1],
                                         recv_sem=recv_sems.at[kk - 1], device_id=(px, py, pc), device_id_type=MESH).wait_recv()
        for cp in sends:
            cp.wait_send()
        acc = gath[0]
        for j in range(1, 8):
            acc = acc + gath[j]
        o_ref[...] = acc

    return pl.pallas_call(
        body, name="all_reduce_small",
        in_specs=[pl.BlockSpec(memory_space=pltpu.VMEM)], out_specs=pl.BlockSpec(memory_space=pltpu.VMEM),
        out_shape=SDS((rows, cols), F32),
        scratch_shapes=[pltpu.VMEM((8, rows, cols), F32), pltpu.SemaphoreType.DMA((7,)), pltpu.SemaphoreType.DMA((7,))],
    )(buf)


def _adamw_rows(w, g, m, v):
    m = ADAM_B1 * m + (1.0 - ADAM_B1) * g
    v = ADAM_B2 * v + (1.0 - ADAM_B2) * jnp.square(g)
    m_hat = m / (1.0 - ADAM_B1 ** ADAM_STEP)
    v_hat = v / (1.0 - ADAM_B2 ** ADAM_STEP)
    return -ADAM_LR * (m_hat / (jnp.sqrt(v_hat) + ADAM_EPS) + ADAM_WD * w), m, v


def _adamw(w, g, m, v, name, dep=None):
    rows, cols = w.shape
    tm = _pick(rows, (256, 128, 64, 16, 8))
    return _rows_call(_adamw_rows, [(t, 0, cols) for t in (w, g, m, v)], [], [(cols, F32)] * 3, tm=tm, name=name, dep=dep)


def _pack_small(parts):
    flat = jnp.concatenate([parts[n].reshape(-1) for n, _ in SMALL])
    return jnp.pad(flat, (0, SMALL_ROWS * PACK_COLS - flat.shape[0])).reshape(SMALL_ROWS, PACK_COLS)


def _unpack_small(buf, shapes):
    flat, out, off = buf.reshape(-1), {}, 0
    for n, sz in SMALL:
        out[n] = flat[off:off + sz].reshape(shapes[n])
        off += sz
    return out


def _lora_stack(parts):
    return jnp.concatenate([parts[n] for n, _ in LORA], axis=-2)


def _lora_split(stacked):
    out, off = {}, 0
    for n, rows in LORA:
        out[n] = stacked[..., off:off + rows, :]
        off += rows
    return out


def _ffn_fwd(x, gain, wg, wu, wd, tag):
    h = _rows_call(_rms, [(x, 0, D_MODEL)], [gain], [(D_MODEL, BF16)], tm=256, name=f"{tag}_norm")[0]
    gate = _mm(h, wg, name=f"{tag}_gate")
    up = _mm(h, wu, name=f"{tag}_up")
    nblk, s, f = gate.shape
    act = _rows_call(_swiglu_act, [(gate.reshape(nblk * s, f), 0, f), (up.reshape(nblk * s, f), 0, f)], [], [(f, BF16)],
                     tm=512, name=f"{tag}_act")[0].reshape(nblk, s, f)
    x_new = _mm(act, wd, sum_blocks=True, res=x, alpha=0.5, name=f"{tag}_down")
    return x_new, (x, h, gate, up, act)


def _ffn_bwd(dx_new, dx_new_bf, saved, gain, wg, wu, wd, tag, dep=None):
    x, h, gate, up, act = saved
    nblk, s, f = gate.shape
    d_wd = _mm(act, dx_new_bf, ta=True, alpha=0.5, name=f"{tag}_down_dw")
    dact = _mm(dx_new_bf, wd, tb=True, alpha=0.5, dep=dep, name=f"{tag}_down_dx")

    def act_bwd(gt, ut, ct):
        _, vjp = jax.vjp(_swiglu_act, gt, ut)
        return vjp(ct)

    dgate, dup = _rows_call(act_bwd, [(t.reshape(nblk * s, f), 0, f) for t in (gate, up, dact)], [], [(f, BF16)] * 2,
                            tm=512, name=f"{tag}_act_bwd")
    dgate, dup = dgate.reshape(nblk, s, f), dup.reshape(nblk, s, f)
    d_wg = _mm(h, dgate, ta=True, name=f"{tag}_gate_dw")
    d_wu = _mm(h, dup, ta=True, name=f"{tag}_up_dw")
    dh = _mm(dgate, wg, tb=True, sum_blocks=True, name=f"{tag}_gate_dx")
    dh = _mm(dup, wu, tb=True, sum_blocks=True, res=dh, name=f"{tag}_up_dx")
    dx, dx_bf, dgain = _norm_bwd(x, gain, dh, dx_new, f"{tag}_norm_bwd")
    return dx, dx_bf, dgain, d_wg, d_wu, d_wd


def _norm_bwd(x, gain, dh, dres, name):
    def f(xt, dht, drt, gt):
        _, vjp = jax.vjp(_rms, xt, gt)
        dxt, dgt = vjp(dht)
        return dxt + drt, dxt + drt, dgt

    return _rows_call(f, [(x, 0, D_MODEL), (dh, 0, D_MODEL), (dres, 0, D_MODEL)], [gain], [(D_MODEL, F32), (D_MODEL, BF16)],
                      [(1, D_MODEL)], tm=256, name=name)


def kernel(x, p, positions, ffn1_norm, ffn1_w_gate, ffn1_w_up, ffn1_w_down, mix_norm, w_in, rwkv_mu, rwkv_w0, rwkv_w2, rwkv_a0, rwkv_a2, rwkv_g2, rwkv_k_k, rwkv_k_a, rwkv_r_k, rwkv_gn_w, rwkv_gn_b, q_norm, k_norm, w_br_rwkv, w_br_attn, w_out, ffn2_norm, ffn2_w_gate, ffn2_w_up, ffn2_w_down, ple_norm, ple_w_gate, ple_w_proj, loss_target, m_ffn1_norm, m_ffn1_w_gate, m_ffn1_w_up, m_ffn1_w_down, m_mix_norm, m_w_in, m_rwkv_mu, m_rwkv_w0, m_rwkv_w2, m_rwkv_a0, m_rwkv_a2, m_rwkv_g2, m_rwkv_k_k, m_rwkv_k_a, m_rwkv_r_k, m_rwkv_gn_w, m_rwkv_gn_b, m_q_norm, m_k_norm, m_w_br_rwkv, m_w_br_attn, m_w_out, m_ffn2_norm, m_ffn2_w_gate, m_ffn2_w_up, m_ffn2_w_down, m_ple_norm, m_ple_w_gate, m_ple_w_proj, v_ffn1_norm, v_ffn1_w_gate, v_ffn1_w_up, v_ffn1_w_down, v_mix_norm, v_w_in, v_rwkv_mu, v_rwkv_w0, v_rwkv_w2, v_rwkv_a0, v_rwkv_a2, v_rwkv_g2, v_rwkv_k_k, v_rwkv_k_a, v_rwkv_r_k, v_rwkv_gn_w, v_rwkv_gn_b, v_q_norm, v_k_norm, v_w_br_rwkv, v_w_br_attn, v_w_out, v_ffn2_norm, v_ffn2_w_gate, v_ffn2_w_up, v_ffn2_w_down, v_ple_norm, v_ple_w_gate, v_ple_w_proj):
    args = dict(locals())
    wts = {n: args[n] for n in WEIGHTS}
    mom_m = {n: args["m_" + n] for n in WEIGHTS}
    mom_v = {n: args["v_" + n] for n in WEIGHTS}
    x0, tgt = x[0], loss_target[0]
    s = x0.shape[0]
    p_tok = p[0, 0]

    vec = {n: wts[n].reshape(1, -1) for n, _ in SMALL}
    xi, yi, ci = _coords()
    me = 2 * xi + yi
    shard_of = {n: wts[n][0] for g in GROUPS.values() for n, _, _, _ in g if n != "lora"}
    shard_of["lora"] = _lora_stack({n: wts[n][0] for n, _ in LORA})

    def whole_with_own(n, kind, r, c):
        at = (me, 0, 0) if kind == "blk" else (0, me * c)
        own = shard_of[n].astype(BF16)
        return lax.dynamic_update_slice(lax.empty(_form(kind, r, c), BF16), own[None] if kind == "blk" else own, at)

    specs = {g: [(kind, r, c) for _, kind, r, c in grp] for g, grp in GROUPS.items()}
    bufs = {g: [whole_with_own(*w) for w in grp] for g, grp in GROUPS.items()}
    plans = {(g, st): _gather_plan(specs[g], st) for g in ("mx", "f2") for st in ("ici", "d2d")}
    wb = dict(zip([w[0] for w in GROUPS["f1"]], _gather_now(bufs["f1"], specs["f1"])))
    ss_a, rs_a, buf_mx, tok_a = _copies_start("gather_mx_ici", bufs["mx"], 3 * len(bufs["mx"]), plans["mx", "ici"][0],
                                              after=wb["ffn1_w_gate"])

    inv_freq = 1.0 / (ROPE_THETA ** (jnp.arange(0, HEAD, 2, dtype=F32) / HEAD))
    ang = positions[0].astype(F32)[:, None] * inv_freq
    cos, sin = jnp.cos(ang), jnp.sin(ang)
    cos2, sin2 = jnp.concatenate([cos, cos], axis=1), jnp.concatenate([-sin, sin], axis=1)

    x1, ffn1_saved = _ffn_fwd(x0, vec["ffn1_norm"] + tok_a[0, 0], wb["ffn1_w_gate"], wb["ffn1_w_up"], wb["ffn1_w_down"], "ffn1")
    buf_mx = _copies_wait("gather_mx_ici_wait", buf_mx, ss_a, rs_a, x1, plans["mx", "ici"][1])
    ss_b, rs_b, buf_mx, tok_b = _copies_start("gather_mx_d2d", buf_mx, 3 * len(buf_mx), plans["mx", "d2d"][0])
    ss_c, rs_c, buf_f2, tok_c = _copies_start("gather_f2_ici", bufs["f2"], 3 * len(bufs["f2"]), plans["f2", "ici"][0])
    h = _rows_call(_rms, [(x1, 0, D_MODEL)], [vec["mix_norm"] + (tok_b[0, 0] + tok_c[0, 0])], [(D_MODEL, BF16)], tm=256,
                   name="mix_norm")[0]
    buf_mx = _copies_wait("gather_mx_d2d_wait", buf_mx, ss_b, rs_b, h, plans["mx", "d2d"][1])
    wb.update(zip([w[0] for w in GROUPS["mx"]], buf_mx))
    w_in_all = wb["w_in"]
    w_in_r, w_in_a, w_in_g = w_in_all[:, :RWKV_COLS], w_in_all[:, RWKV_COLS:RWKV_COLS + ATTN_COLS], w_in_all[:, RWKV_COLS + ATTN_COLS:]
    lora = _lora_split(wb["lora"])
    w2, a2, g2 = lora["rwkv_w2"], lora["rwkv_a2"], lora["rwkv_g2"]
    w_brr, w_bra = wb["w_br_rwkv"], wb["w_br_attn"]
    w_o = wb["w_out"].reshape(D_MODEL, D_MODEL)
    z_r = _mm(h, w_in_r, name="in_rwkv")
    z_a = _mm(h, w_in_a, name="in_attn")
    z_g = _mm(h, w_in_g, name="in_gate")

    zs = _shift_fwd(z_r, vec["rwkv_mu"])
    pre_params = [vec["rwkv_w0"], w2, vec["rwkv_a0"], a2, g2, vec["rwkv_k_k"], vec["rwkv_k_a"]]
    def pre_fwd(*t):
        res = _rwkv_pre(*t)
        return res[1], res[2], res[4], res[5], res[6]

    lw, k2, na, kb, gate_r = _rows_call(pre_fwd, [(zs, 0, RWKV_COLS)], pre_params, [(RWKV_DIM, F32)] * 5, tm=256, name="rwkv_pre")
    y_scan, s0s = _wkv_fwd(zs, lw, k2, na, kb)
    buf_f2 = _copies_wait("gather_f2_ici_wait", buf_f2, ss_c, rs_c, y_scan, plans["f2", "ici"][1])
    ss_d, rs_d, buf_f2, tok_d = _copies_start("gather_f2_d2d", buf_f2, 3 * len(buf_f2), plans["f2", "d2d"][0])
    post_params = [vec["rwkv_gn_w"] + tok_d[0, 0], vec["rwkv_gn_b"], vec["rwkv_r_k"]]
    post_rows = [(y_scan, 0, RWKV_DIM), (zs, 0, RWKV_DIM), (k2, 0, RWKV_DIM), (zs, 2, RWKV_DIM), (gate_r, 0, RWKV_DIM)]
    y_rwkv = _rows_call(_rwkv_post, post_rows, post_params, [(RWKV_DIM, BF16)], tm=256, name="rwkv_post")[0]

    def qk_fwd(qt, kt, ct, st, qg, kg):
        return _norm_rope(qt, qg, ct, st), _norm_rope(kt, kg, ct, st)

    qk_rows = [(z_a, 0, ATTN_DIM), (z_a, 1, ATTN_DIM), (cos2, 0, HEAD), (sin2, 0, HEAD)]
    q_rot, k_rot = _rows_call(qk_fwd, qk_rows, [vec["q_norm"], vec["k_norm"]], [(ATTN_DIM, BF16)] * 2, tm=256, name="attn_pre")
    outs, lses = zip(*[_attn_fwd(q_rot, k_rot, z_a, g, d) for g, d in enumerate(ATTN_DILATIONS)])
    comb_rows = [(t, 0, GROUP_DIM) for t in outs + lses]
    y_attn = _rows_call(_attn_combine, comb_rows, [], [(GROUP_DIM, BF16)], tm=256, name="attn_combine")[0]

    br = _mm(y_rwkv, w_brr, name="branch_rwkv")
    ba = _mm(y_attn, w_bra, name="branch_attn")
    merge_rows = [(z_g, 0, D_MODEL), (z_g, 1, D_MODEL), (br, 0, D_MODEL), (ba, 0, D_MODEL)]
    merged = _rows_call(_merge, merge_rows, [], [(D_MODEL, BF16)], tm=256, name="merge")[0]
    x2 = _mm(merged, w_o, res=x1, name="out_proj")
    buf_f2 = _copies_wait("gather_f2_d2d_wait", buf_f2, ss_d, rs_d, x2, plans["f2", "d2d"][1])
    wb.update(zip([w[0] for w in GROUPS["f2"]], buf_f2))
    w_pp, w_pg = wb["ple_w_proj"], wb["ple_w_gate"].reshape(D_MODEL, D_MODEL)
    x3, ffn2_saved = _ffn_fwd(x2, vec["ffn2_norm"], wb["ffn2_w_gate"], wb["ffn2_w_up"], wb["ffn2_w_down"], "ffn2")
    hp = _rows_call(_rms, [(x3, 0, D_MODEL)], [vec["ple_norm"]], [(D_MODEL, BF16)], tm=256, name="ple_norm")[0]
    pg = _mm(hp, w_pg, name="ple_gate")
    pp = _mm(p_tok, w_pp, name="ple_proj")

    def head(x3t, pgt, ppt, tt):
        sg = _sigmoid(pgt)
        err = x3t + sg * ppt - tt
        dx4 = err * (1.0 / D_MODEL)
        loss = 0.5 * jnp.sum(jnp.mean(err * err, axis=-1, keepdims=True), axis=0, keepdims=True)
        return dx4, dx4 * ppt * sg * (1.0 - sg), dx4 * sg, jnp.broadcast_to(loss, (8, 128))

    head_rows = [(x3, 0, D_MODEL), (pg, 0, D_MODEL), (pp, 0, D_MODEL), (tgt, 0, D_MODEL)]
    dx4, dpg, dpp, loss_tile = _rows_call(head, head_rows, [], [(D_MODEL, F32), (D_MODEL, BF16), (D_MODEL, BF16)], [(8, 128)],
                                          tm=256, name="ple_loss")

    c_arr = jnp.reshape(ci, (1,)).astype(jnp.int32)
    me_arr = jnp.reshape(me, (1,)).astype(jnp.int32)
    red = {g: _GroupReduce(g, grp, c_arr, me_arr) for g, grp in GROUPS.items()}
    gw, gs = {}, {}
    gw["ple_w_proj"] = _mm(p_tok, dpp, ta=True, name="ple_proj_dw")
    gw["ple_w_gate"] = _mm(hp, dpg, ta=True, name="ple_gate_dw")
    dhp = _mm(dpg, w_pg, tb=True, name="ple_gate_dx")
    dx3, dx3_bf, gs["ple_norm"] = _norm_bwd(x3, vec["ple_norm"], dhp, dx4, "ple_norm_bwd")
    dx2, dx2_bf, gs["ffn2_norm"], gw["ffn2_w_gate"], gw["ffn2_w_up"], gw["ffn2_w_down"] = _ffn_bwd(
        dx3, dx3_bf, ffn2_saved, vec["ffn2_norm"], wb["ffn2_w_gate"], wb["ffn2_w_up"], wb["ffn2_w_down"], "ffn2")
    gw["ple_w_gate"] = gw["ple_w_gate"].reshape(N_CHIPS, D_MODEL // N_CHIPS, D_MODEL)
    tok = red["f2"].swap_start([gw[w[0]] for w in GROUPS["f2"]])
    gw["w_out"] = _mm(merged, dx2_bf, ta=True, name="out_proj_dw")
    dmerged = _mm(dx2_bf, w_o, tb=True, dep=tok, name="out_proj_dx")

    def merge_bwd(zgr, zga, brt, bat, ct):
        _, vjp = jax.vjp(_merge, zgr, zga, brt, bat)
        d1, d2, d3, d4 = vjp(ct)
        return jnp.concatenate([d1, d2], axis=1), d3, d4

    dz_g, dbr, dba = _rows_call(merge_bwd, merge_rows + [(dmerged, 0, D_MODEL)], [],
                                [(2 * D_MODEL, BF16), (D_MODEL, BF16), (D_MODEL, BF16)], tm=256, name="merge_bwd")
    tok = red["f2"].swap_wait_ici_start(dz_g)
    gw["w_br_rwkv"] = _mm(y_rwkv, dbr, ta=True, name="branch_rwkv_dw")
    gw["w_br_attn"] = _mm(y_attn, dba, ta=True, name="branch_attn_dw")
    dy_rwkv = _mm(dbr, w_brr, tb=True, dep=tok, name="branch_rwkv_dx")
    dy_attn = _mm(dba, w_bra, tb=True, dep=tok, name="branch_attn_dx")

    def comb_bwd(*t):
        _, vjp = jax.vjp(_attn_combine, *t[:6])
        return vjp(t[6])

    dcomb = _rows_call(comb_bwd, comb_rows + [(dy_attn, 0, GROUP_DIM)], [], [(GROUP_DIM, F32)] * 6, tm=256, name="attn_combine_bwd")
    dqs, dks, dvs = zip(*[_attn_bwd(q_rot, k_rot, z_a, g, d, dcomb[g], dcomb[3 + g]) for g, d in enumerate(ATTN_DILATIONS)])

    def qk_bwd(qt, kt, ct, st, *rest):
        dq = jnp.concatenate(rest[0:3], axis=1)
        dk = jnp.concatenate(rest[3:6], axis=1)
        qg, kg = rest[9], rest[10]
        _, vjp = jax.vjp(lambda a_, b_, c_, d_: qk_fwd(a_, b_, ct, st, c_, d_), qt, kt, qg, kg)
        dqt, dkt, dqg, dkg = vjp((dq, dk))
        return jnp.concatenate((dqt, dkt) + tuple(rest[6:9]), axis=1), dqg, dkg

    dz_a, gs["q_norm"], gs["k_norm"] = _rows_call(
        qk_bwd, qk_rows + [(t, 0, GROUP_DIM) for t in dqs + dks + dvs], [vec["q_norm"], vec["k_norm"]],
        [(ATTN_COLS, BF16)], [(1, HEAD), (1, HEAD)], tm=256, name="attn_pre_bwd")
    tok = red["f2"].ici_wait_join_start(dz_a)

    def post_bwd(*t):
        _, vjp = jax.vjp(_rwkv_post, *t[:5], *t[6:])
        return vjp(t[5])

    dy_scan, dr_post, dk2_post, dv_post, dgate_r, gs["rwkv_gn_w"], gs["rwkv_gn_b"], gs["rwkv_r_k"] = _rows_call(
        post_bwd, post_rows + [(dy_rwkv, 0, RWKV_DIM)], post_params, [(RWKV_DIM, F32)] * 5, [(1, RWKV_DIM)] * 3,
        tm=256, name="rwkv_post_bwd", dep=tok)
    grad_big = red["f2"].join_wait(dy_scan)
    dr_s, dlw, dk2_s, dv_s, dna, dkb = _wkv_bwd(zs, lw, k2, na, kb, s0s, dy_scan)

    def pre_bwd(zt, c_r1, c_r2, c_lw, c_k1, c_k2, c_v1, c_v2, c_a, c_b, c_g, *params):
        _, vjp = jax.vjp(_rwkv_pre, zt, *params)
        return vjp((c_r1 + c_r2, c_lw, c_k1 + c_k2, c_v1 + c_v2, c_a, c_b, c_g))

    pre_cts = [dr_s, dr_post, dlw, dk2_s, dk2_post, dv_s, dv_post, dna, dkb, dgate_r]
    dzs, gs["rwkv_w0"], g_w2, gs["rwkv_a0"], g_a2, g_g2, gs["rwkv_k_k"], gs["rwkv_k_a"] = _rows_call(
        pre_bwd, [(zs, 0, RWKV_COLS)] + [(t, 0, RWKV_DIM) for t in pre_cts], pre_params, [(RWKV_COLS, F32)],
        [q.shape for q in pre_params], tm=256, name="rwkv_pre_bwd")
    dz_r, gs["rwkv_mu"] = _shift_bwd(z_r, vec["rwkv_mu"], dzs)

    g_w_in = jnp.concatenate([_mm(h, dz_r, ta=True, name="in_rwkv_dw"), _mm(h, dz_a, ta=True, name="in_attn_dw"),
                              _mm(h, dz_g, ta=True, name="in_gate_dw")], axis=1)
    gw["w_in"], gw["lora"] = g_w_in, jnp.concatenate([g_w2, g_a2, g_g2], axis=0)
    gw["w_out"] = gw["w_out"].reshape(N_CHIPS, D_MODEL // N_CHIPS, D_MODEL)
    tok = red["mx"].swap_start([gw[w[0]] for w in GROUPS["mx"]])
    dh = _mm(dz_r, w_in_r, tb=True, dep=tok, name="in_rwkv_dx")
    dh = _mm(dz_a, w_in_a, tb=True, res=dh, name="in_attn_dx")
    dh = _mm(dz_g, w_in_g, tb=True, res=dh, name="in_gate_dx")
    dx1, dx1_bf, gs["mix_norm"] = _norm_bwd(x1, vec["mix_norm"], dh, dx2, "mix_norm_bwd")
    tok = red["mx"].swap_wait_ici_start(dx1_bf)
    dx0, _, gs["ffn1_norm"], gw["ffn1_w_gate"], gw["ffn1_w_up"], gw["ffn1_w_down"] = _ffn_bwd(
        dx1, dx1_bf, ffn1_saved, vec["ffn1_norm"], wb["ffn1_w_gate"], wb["ffn1_w_up"], wb["ffn1_w_down"], "ffn1", dep=tok)
    tok_mx = red["mx"].ici_wait_join_start(dx0)
    tok = red["f1"].swap_start([gw[w[0]] for w in GROUPS["f1"]], after=tok_mx)

    grads, deltas, new_m, new_v = {}, {}, {}, {}

    def update(group, dep):
        last = None
        for n, _, _, _ in GROUPS[group]:
            g2d = grad_big[n]
            if n == "lora":
                w_, m_, v_ = (_lora_stack({k: t[k][0] for k, _ in LORA}) for t in (wts, mom_m, mom_v))
            else:
                w_, m_, v_ = wts[n][0], mom_m[n][0], mom_v[n][0]
            res = (g2d,) + tuple(_adamw(w_, g2d, m_, v_, name=f"adamw_{n}", dep=dep))
            dep = last = res[1]
            for store, val in zip((grads, deltas, new_m, new_v), res):
                store.update({k: t[None] for k, t in _lora_split(val).items()} if n == "lora" else {n: val[None]})
        return last

    last = update("f2", tok)
    tok = red["f1"].swap_wait_ici_start(last)
    grad_big.update(red["mx"].join_wait(last))
    last = update("mx", tok)

    flat = jnp.concatenate([gs[n].reshape(-1) for n, _ in SMALL] + [loss_tile[0, 0:1]])
    small_buf = jnp.pad(flat, (0, SMALL_ROWS * PACK_COLS - flat.shape[0])).reshape(SMALL_ROWS, PACK_COLS)
    small_sum = _all_reduce_small(small_buf)
    n_small = sum(sz for _, sz in SMALL)
    loss = small_sum.reshape(-1)[n_small]
    grad_small = _unpack_small(small_sum, {n: wts[n].shape for n, _ in SMALL})
    d_s, m_s, v_s = _adamw(_pack_small(wts), small_sum, _pack_small(mom_m), _pack_small(mom_v), name="adamw_small", dep=last)
    shapes = {n: wts[n].shape for n, _ in SMALL}
    d_s, m_s, v_s = _unpack_small(d_s, shapes), _unpack_small(m_s, shapes), _unpack_small(v_s, shapes)
    for n, _ in SMALL:
        grads[n], deltas[n], new_m[n], new_v[n] = grad_small[n], d_s[n], m_s[n], v_s[n]

    tok = red["f1"].ici_wait_join_start(m_s["ffn1_norm"])
    grad_big.update(red["f1"].join_wait(tok))
    update("f1", None)

    return (loss, dx0[None], *[grads[n] for n in WEIGHTS], *[deltas[n] for n in WEIGHTS],
            *[new_m[n] for n in WEIGHTS], *[new_v[n] for n in WEIGHTS])
```

```python
import functools

import jax
import jax.numpy as jnp
from jax import lax
from jax.experimental import pallas as pl
from jax.experimental.pallas import tpu as pltpu

F32, BF16 = jnp.float32, jnp.bfloat16
HI = lax.Precision.HIGHEST
MESH = pl.DeviceIdType.MESH
SDS = jax.ShapeDtypeStruct

D_MODEL = 1024
HEAD = 64
RWKV_HEADS = 8
RWKV_DIM = RWKV_HEADS * HEAD
DECAY_LORA, ICLR_LORA, GATE_LORA = 64, 64, 128
GN_EPS = 64e-5
RMS_EPS = 1e-6
ATTN_DILATIONS = (1, 4, 16)
BAND = 128
ATTN_DIM = 768
GROUP_DIM = 256
ROPE_THETA = 10000.0
NEG_INF = -1e30
RWKV_COLS = 3 * RWKV_DIM + DECAY_LORA + ICLR_LORA + GATE_LORA
ATTN_COLS = 3 * ATTN_DIM
ADAM_LR, ADAM_B1, ADAM_B2, ADAM_EPS, ADAM_WD, ADAM_STEP = 0.001, 0.9, 0.999, 1e-08, 0.01, 10

WKV_CHUNK = 64
WKV_HEADS_PER_STEP = 8
N_CHIPS = 4
PACK_COLS = 1024
VMEM_LIMIT = 48 * 1024 * 1024

LORA = (("rwkv_w2", 64), ("rwkv_a2", 64), ("rwkv_g2", 128))
GROUPS = {
    "f1": (("ffn1_w_gate", "blk", 1024, 704), ("ffn1_w_up", "blk", 1024, 704), ("ffn1_w_down", "blk", 704, 1024)),
    "mx": (("w_in", "col", 1024, 1536), ("lora", "col", 256, 128), ("w_br_rwkv", "col", 512, 256),
           ("w_br_attn", "col", 256, 256), ("w_out", "blk", 256, 1024)),
    "f2": (("ffn2_w_gate", "blk", 1024, 704), ("ffn2_w_up", "blk", 1024, 704), ("ffn2_w_down", "blk", 704, 1024),
           ("ple_w_gate", "blk", 256, 1024), ("ple_w_proj", "col", 256, 256)),
}
SMALL = (
    ("ffn1_norm", 1024), ("mix_norm", 1024), ("ffn2_norm", 1024), ("ple_norm", 1024), ("rwkv_mu", 1792),
    ("rwkv_w0", 512), ("rwkv_a0", 512), ("rwkv_k_k", 512), ("rwkv_k_a", 512), ("rwkv_r_k", 512),
    ("rwkv_gn_w", 512), ("rwkv_gn_b", 512), ("q_norm", 64), ("k_norm", 64),
)
SMALL_ROWS = 16
WEIGHTS = (
    "ffn1_norm", "ffn1_w_gate", "ffn1_w_up", "ffn1_w_down", "mix_norm", "w_in", "rwkv_mu", "rwkv_w0", "rwkv_w2",
    "rwkv_a0", "rwkv_a2", "rwkv_g2", "rwkv_k_k", "rwkv_k_a", "rwkv_r_k", "rwkv_gn_w", "rwkv_gn_b", "q_norm", "k_norm",
    "w_br_rwkv", "w_br_attn", "w_out", "ffn2_norm", "ffn2_w_gate", "ffn2_w_up", "ffn2_w_down", "ple_norm",
    "ple_w_gate", "ple_w_proj",
)


def _row_tile(n, most=704):
    for t in range(most - most % 16, 0, -16):
        if n % t == 0:
            return t
    return n


def _pick(n, cands):
    for c in cands:
        if n % c == 0:
            return c
    return n


def _mm(a, b, *, ta=False, tb=False, sum_blocks=False, out_dtype=F32, res=None, alpha=1.0, dep=None, name):
    flat = a.ndim == 2 and b.ndim == 2
    a3 = a if a.ndim == 3 else a[None]
    b3 = b if b.ndim == 3 else b[None]
    na, nbb = a3.shape[0], b3.shape[0]
    nblk = max(na, nbb)
    kdim, m = (a3.shape[1], a3.shape[2]) if ta else (a3.shape[2], a3.shape[1])
    n = b3.shape[1] if tb else b3.shape[2]
    assert (b3.shape[2] if tb else b3.shape[1]) == kdim
    tm = _pick(m, (1024, 512, 256, 128))
    tn = _pick(n, (1024, 896, 768, 512, 256, 128))
    tk = kdim if kdim <= 2304 else _pick(kdim, (1024, 512, 256, 128))
    nk = kdim // tk
    direct = nk == 1 and not sum_blocks

    if sum_blocks:
        grid = (m // tm, n // tn, nblk, nk)

        def ids(i, c, j, k):
            return i, c, j, k
    else:
        grid = (nblk, m // tm, n // tn, nk)

        def ids(j, i, c, k):
            return i, c, j, k

    def amap(*g):
        i, c, j, k = ids(*g)
        jj = j if na > 1 else 0
        return (jj, k, i) if ta else (jj, i, k)

    def bmap(*g):
        i, c, j, k = ids(*g)
        jj = j if nbb > 1 else 0
        return (jj, c, k) if tb else (jj, k, c)

    if sum_blocks:
        oshape, oblk = (m, n), (tm, tn)

        def omap(*g):
            i, c, j, k = ids(*g)
            return i, c
    else:
        oshape, oblk = (nblk, m, n), (1, tm, tn)

        def omap(*g):
            i, c, j, k = ids(*g)
            return j, i, c

    dn = (((0 if ta else 1,), (1 if tb else 0,)), ((), ()))
    has_res = res is not None

    def body(*refs):
        refs = list(refs)
        acc = None if direct else refs.pop()
        o_ref = refs.pop()
        a_ref, b_ref = refs[0], refs[1]
        r_ref = refs[2] if has_res else None

        def finish(v):
            if alpha != 1.0:
                v = v * alpha
            if has_res:
                v = v + r_ref[...].reshape(v.shape).astype(F32)
            o_ref[...] = v.reshape(o_ref.shape).astype(o_ref.dtype)

        if direct:
            finish(lax.dot_general(a_ref[0].astype(BF16), b_ref[0].astype(BF16), dn, preferred_element_type=F32))
            return
        k = pl.program_id(3)
        if sum_blocks:
            j = pl.program_id(2)
            first = jnp.logical_and(j == 0, k == 0)
            last = jnp.logical_and(j == nblk - 1, k == nk - 1)
        else:
            first, last = k == 0, k == nk - 1

        @pl.when(first)
        def _():
            acc[...] = jnp.zeros_like(acc)

        acc[...] += lax.dot_general(a_ref[0].astype(BF16), b_ref[0].astype(BF16), dn, preferred_element_type=F32)

        @pl.when(last)
        def _():
            finish(acc[...])

    in_specs = [pl.BlockSpec((1, tk, tm) if ta else (1, tm, tk), amap), pl.BlockSpec((1, tn, tk) if tb else (1, tk, tn), bmap)]
    args = [a3, b3]
    if has_res:
        res3 = res if (sum_blocks or res.ndim == 3) else res[None]
        in_specs.append(pl.BlockSpec(oblk, omap))
        args.append(res3)
    if dep is not None:
        in_specs.append(pl.BlockSpec(memory_space=pl.ANY))
        args.append(dep)
    out = pl.pallas_call(
        body,
        name=name,
        grid=grid,
        in_specs=in_specs,
        out_specs=pl.BlockSpec(oblk, omap),
        out_shape=SDS(oshape, out_dtype),
        scratch_shapes=[] if direct else [pltpu.VMEM((tm, tn), F32)],
        compiler_params=pltpu.CompilerParams(
            dimension_semantics=("parallel", "parallel", "arbitrary", "arbitrary") if sum_blocks
            else ("parallel", "parallel", "parallel", "arbitrary"),
            vmem_limit_bytes=VMEM_LIMIT),
    )(*args)
    if flat and not sum_blocks:
        out = out[0]
    return out


def _rows_call(f, rows, params, outs, accs=(), *, tm, name, dep=None):
    s = rows[0][0].shape[0]
    nr, npar, no = len(rows), len(params), len(outs)
    nin = nr + npar + (0 if dep is None else 1)
    in_specs = [pl.BlockSpec((tm, w), functools.partial(lambda i, cb: (i, cb), cb=cb)) for (_, cb, w) in rows]
    in_specs += [pl.BlockSpec(p.shape, functools.partial(lambda i, nd: (0,) * nd, nd=p.ndim)) for p in params]
    if dep is not None:
        in_specs.append(pl.BlockSpec(memory_space=pl.ANY))
    out_shape = [SDS((s, w), dt) for (w, dt) in outs] + [SDS(tuple(sh), F32) for sh in accs]
    out_specs = [pl.BlockSpec((tm, w), lambda i: (i, 0)) for (w, _) in outs]
    out_specs += [pl.BlockSpec(tuple(sh), functools.partial(lambda i, nd: (0,) * nd, nd=len(sh))) for sh in accs]

    def body(*refs):
        rin, pin = refs[:nr], refs[nr:nr + npar]
        oo, ao = refs[nin:nin + no], refs[nin + no:]
        res = f(*[r[...] for r in rin], *[p[...] for p in pin])
        if not isinstance(res, (tuple, list)):
            res = (res,)
        for o_ref, v in zip(oo, res[:no]):
            o_ref[...] = v.astype(o_ref.dtype)
        i = pl.program_id(0)
        for a_ref, v in zip(ao, res[no:]):
            @pl.when(i == 0)
            def _():
                a_ref[...] = jnp.zeros_like(a_ref)

            a_ref[...] += v.reshape(a_ref.shape)

    res = pl.pallas_call(
        body,
        name=name,
        grid=(s // tm,),
        in_specs=in_specs,
        out_specs=out_specs,
        out_shape=out_shape,
        compiler_params=pltpu.CompilerParams(dimension_semantics=("arbitrary",), vmem_limit_bytes=VMEM_LIMIT),
    )(*[r[0] for r in rows], *params, *([] if dep is None else [dep]))
    return res


def _mmv(a, b, mode):
    ca = 0 if mode[0] == "t" else 1
    cb = 1 if mode[1] == "t" else 0
    return lax.dot_general(a.astype(BF16), b.astype(BF16), (((ca,), (cb,)), ((), ())), preferred_element_type=F32)


@functools.partial(jax.custom_vjp, nondiff_argnums=(2,))
def _bdot(a, b, mode):
    return _mmv(a, b, mode)


def _bdot_fwd(a, b, mode):
    return _mmv(a, b, mode), (a, b)


def _bdot_bwd(mode, saved, g):
    a, b = saved
    if mode == "nn":
        return _mmv(g, b, "nt"), _mmv(a, g, "tn")
    if mode == "nt":
        return _mmv(g, b, "nn"), _mmv(g, a, "tn")
    return _mmv(b, g, "nt"), _mmv(a, g, "nn")


_bdot.defvjp(_bdot_fwd, _bdot_bwd)


def _hdot(a, b, mode="nn", precision=HI):
    ca = 0 if mode[0] == "t" else 1
    cb = 1 if mode[1] == "t" else 0
    return lax.dot_general(a, b, (((ca,), (cb,)), ((), ())), precision=precision, preferred_element_type=F32)


def _segsum(x):
    c = x.shape[-1]
    blk = min(c, 256)
    r = lax.broadcasted_iota(jnp.int32, (blk, blk), 0) >> 6
    q = lax.broadcasted_iota(jnp.int32, (blk, blk), 1) >> 6
    ones = jnp.where(r == q, 1.0, 0.0).astype(F32)
    parts = [_hdot(x[:, i:i + blk], ones, precision=lax.Precision.HIGH) for i in range(0, c, blk)]
    return parts[0] if len(parts) == 1 else jnp.concatenate(parts, axis=1)


def _sigmoid(x):
    return jax.nn.sigmoid(x)


def _softplus(x):
    return jnp.maximum(x, 0.0) + jnp.log(1.0 + jnp.exp(-jnp.abs(x)))


def _rms(x, gain):
    return x * lax.rsqrt(jnp.mean(x * x, axis=-1, keepdims=True) + RMS_EPS) * gain


def _swiglu_act(gate, up):
    return gate * _sigmoid(gate) * up


def _rwkv_pre(zs, w0, w2, a0, a2, g2, k_k, k_a):
    r, k, v = zs[:, 0:512], zs[:, 512:1024], zs[:, 1024:1536]
    lora = zs[:, 1536:1792]
    wd, ad, gd = lora[:, 0:64], lora[:, 64:128], lora[:, 128:256]
    w = -_softplus(-(w0 + _bdot(jnp.tanh(wd), w2, "nn"))) - 0.5
    a = _sigmoid(a0 + _bdot(ad, a2, "nn"))
    g = _bdot(_sigmoid(gd), g2, "nn")
    kk = k * k_k
    kk = kk * lax.rsqrt(jnp.maximum(_segsum(kk * kk), 1e-24))
    k2 = k * (1.0 + (a - 1.0) * k_a)
    return r, -jnp.exp(w), k2, v, -kk, kk * a, g


def _rwkv_post(y, r, k2, v, g, gn_w, gn_b, r_k):
    mean = _segsum(y) * (1.0 / HEAD)
    yc = y - mean
    var = _segsum(yc * yc) * (1.0 / HEAD)
    yn = yc * lax.rsqrt(var + GN_EPS) * gn_w + gn_b
    bonus = _segsum(r * k2 * r_k) * v
    return (yn + bonus) * g


def _swap_halves(x):
    lane = lax.broadcasted_iota(jnp.int32, x.shape, 1)
    return jnp.where((lane & 32) == 0, jnp.roll(x, -32, axis=1), jnp.roll(x, 32, axis=1))


def _norm_rope(x, gain, cos, sin):
    heads = x.shape[1] // HEAD
    def rep(t):
        return jnp.concatenate([t] * heads, axis=1)

    xn = x * lax.rsqrt(_segsum(x * x) * (1.0 / HEAD) + RMS_EPS) * rep(gain)
    return xn * rep(cos) + _swap_halves(xn) * rep(sin)


def _attn_combine(o0, o1, o2, l0, l1, l2):
    m = jnp.maximum(jnp.maximum(l0, l1), l2)
    e0, e1, e2 = jnp.exp(l0 - m), jnp.exp(l1 - m), jnp.exp(l2 - m)
    return (e0 * o0 + e1 * o1 + e2 * o2) / (e0 + e1 + e2)


def _merge(zgr, zga, br, ba):
    return _sigmoid(zgr) * br + _sigmoid(zga) * ba


def _attn_block(q, kp, kc, vp, vc, has_prev):
    iq = lax.broadcasted_iota(jnp.int32, (1, BAND, BAND), 1)
    ik = lax.broadcasted_iota(jnp.int32, (1, BAND, BAND), 2)
    s_c = jnp.where(iq >= ik, _bdotb(q, kc, "nt") * (HEAD ** -0.5), NEG_INF)
    s_p = jnp.where(jnp.logical_and(iq <= ik, has_prev), _bdotb(q, kp, "nt") * (HEAD ** -0.5), NEG_INF)
    m = lax.stop_gradient(jnp.maximum(jnp.max(s_c, axis=-1, keepdims=True), jnp.max(s_p, axis=-1, keepdims=True)))
    e_c, e_p = jnp.exp(s_c - m), jnp.exp(s_p - m)
    l = jnp.sum(e_c, axis=-1, keepdims=True) + jnp.sum(e_p, axis=-1, keepdims=True)
    o = (_bdotb(e_c, vc) + _bdotb(e_p, vp)) / l
    return o, jnp.broadcast_to(m + jnp.log(l), o.shape)


def _mmb(a, b, cb):
    return lax.dot_general(a.astype(BF16), b.astype(BF16), (((2,), (cb,)), ((0,), (0,))), preferred_element_type=F32)


@functools.partial(jax.custom_vjp, nondiff_argnums=(2,))
def _bdotb1(a, b, cb):
    return _mmb(a, b, cb)


def _bdotb1_fwd(a, b, cb):
    return _mmb(a, b, cb), (a, b)


def _bdotb1_bwd(cb, saved, g):
    a, b = saved
    if cb == 1:
        return _mmb(g, b, 2), _mmb(jnp.swapaxes(a, 1, 2), g, 1)
    return _mmb(g, b, 1), _mmb(jnp.swapaxes(g, 1, 2), a, 1)


_bdotb1.defvjp(_bdotb1_fwd, _bdotb1_bwd)


def _bdotb(a, b, mode="nn", precision=None):
    if mode[0] == "t":
        a = jnp.swapaxes(a, 1, 2)
    cb = 2 if mode[1] == "t" else 1
    if precision is None:
        return _bdotb1(a, b, cb)
    return lax.dot_general(a, b, (((2,), (cb,)), ((0,), (0,))), precision=precision, preferred_element_type=F32)


def _tri_inv(a):
    t = a.shape[-1]
    row = lax.broadcasted_iota(jnp.int32, (1, t, t), 1)
    col = lax.broadcasted_iota(jnp.int32, (1, t, t), 2)
    x = jnp.where(row == col, 1.0, 0.0).astype(F32) + jnp.where(jnp.logical_and(row == col + 1, (row & 1) == 1), a, 0.0)
    sh = 1
    while (1 << sh) < t:
        m = jnp.logical_and((row >> sh) == (col >> sh) + 1, (row >> (sh + 1)) == (col >> (sh + 1)))
        x = x + _bdotb(_bdotb(x, jnp.where(m, a, 0.0), precision=lax.Precision.HIGH), x, precision=lax.Precision.HIGH)
        sh += 1
    return x


def _wkv_chunk(s0, r, lw, k, v, a, b):
    nh, t, _ = r.shape
    row = lax.broadcasted_iota(jnp.int32, (1, t, t), 1)
    col = lax.broadcasted_iota(jnp.int32, (1, t, t), 2)
    incl, strict = row >= col, row > col
    ones = jnp.broadcast_to(jnp.where(incl, 1.0, 0.0).astype(F32), (nh, t, t))
    cum = _bdotb(ones, lw, precision=HI)
    c_end = cum[:, t - 1:t, :]
    e_in, e_ex, e_inv = jnp.exp(cum), jnp.exp(cum - lw), jnp.exp(-cum)
    at, rt, bt, kt = a * e_ex, r * e_in, b * e_inv, k * e_inv
    a_ab = jnp.where(strict, _bdotb(at, bt, "nt"), 0.0)
    a_ak = jnp.where(strict, _bdotb(at, kt, "nt"), 0.0)
    u = _bdotb(_tri_inv(a_ab), _bdotb(at, s0, "nt") + _bdotb(a_ak, v))
    y = (_bdotb(rt, s0, "nt") + _bdotb(jnp.where(incl, _bdotb(rt, bt, "nt"), 0.0), u)
         + _bdotb(jnp.where(incl, _bdotb(rt, kt, "nt"), 0.0), v))
    w_end = jnp.exp(c_end - cum)
    s1 = s0 * jnp.exp(c_end) + _bdotb(u, b * w_end, "tn") + _bdotb(v, k * w_end, "tn")
    return y, s1


def _shift_fwd(z, mu):
    s, c = z.shape
    tc = 256

    def body(z_ref, mu_ref, o_ref):
        zz = z_ref[...]
        row = lax.broadcasted_iota(jnp.int32, zz.shape, 0)
        prev = jnp.where(row == 0, 0.0, pltpu.roll(zz, 1, 0))
        o_ref[...] = zz + (prev - zz) * mu_ref[...]

    return pl.pallas_call(
        body, name="shift_fwd", grid=(c // tc,),
        in_specs=[pl.BlockSpec((s, tc), lambda j: (0, j)), pl.BlockSpec((1, tc), lambda j: (0, j))],
        out_specs=pl.BlockSpec((s, tc), lambda j: (0, j)), out_shape=SDS((s, c), F32),
        compiler_params=pltpu.CompilerParams(dimension_semantics=("parallel",), vmem_limit_bytes=VMEM_LIMIT),
    )(z, mu)


def _shift_bwd(z, mu, dzs):
    s, c = z.shape
    tc = 256

    def body(z_ref, mu_ref, d_ref, dz_ref, dmu_ref):
        zz, d, m = z_ref[...], d_ref[...], mu_ref[...]
        row = lax.broadcasted_iota(jnp.int32, zz.shape, 0)
        prev = jnp.where(row == 0, 0.0, pltpu.roll(zz, 1, 0))
        t = d * m
        nxt = jnp.where(row == s - 1, 0.0, pltpu.roll(t, s - 1, 0))
        dz_ref[...] = (d - t + nxt).astype(dz_ref.dtype)
        dmu_ref[...] = jnp.sum(d * (prev - zz), axis=0, keepdims=True)

    return pl.pallas_call(
        body, name="shift_bwd", grid=(c // tc,),
        in_specs=[pl.BlockSpec((s, tc), lambda j: (0, j)), pl.BlockSpec((1, tc), lambda j: (0, j)),
                  pl.BlockSpec((s, tc), lambda j: (0, j))],
        out_specs=[pl.BlockSpec((s, tc), lambda j: (0, j)), pl.BlockSpec((1, tc), lambda j: (0, j))],
        out_shape=[SDS((s, c), BF16), SDS((1, c), F32)],
        compiler_params=pltpu.CompilerParams(dimension_semantics=("parallel",), vmem_limit_bytes=VMEM_LIMIT),
    )(z, mu, dzs)


def _heads(x, nh):
    return jnp.stack([x[:, h * HEAD:(h + 1) * HEAD] for h in range(nh)], axis=0)


def _unheads(x):
    return jnp.concatenate([x[h] for h in range(x.shape[0])], axis=1)


def _wkv_fwd(zs, lw, k2, na, b):
    s = lw.shape[0]
    t, hb = WKV_CHUNK, WKV_HEADS_PER_STEP
    w = hb * HEAD
    nc, ng = s // t, RWKV_HEADS // hb

    def body(r_ref, v_ref, lw_ref, k_ref, a_ref, b_ref, y_ref, s0_ref, state):
        @pl.when(pl.program_id(1) == 0)
        def _():
            state[...] = jnp.zeros_like(state)

        s0 = state[...]
        s0_ref[0] = s0
        y, s1 = _wkv_chunk(s0, *[_heads(t_ref[...], hb) for t_ref in (r_ref, lw_ref, k_ref, v_ref, a_ref, b_ref)])
        y_ref[...] = _unheads(y)
        state[...] = s1

    def col(off):
        return pl.BlockSpec((t, w), functools.partial(lambda g, i, off: (i, g + off), off=off))

    return pl.pallas_call(
        body, name="wkv_fwd", grid=(ng, nc),
        in_specs=[col(0), col(2 * ng), col(0), col(0), col(0), col(0)],
        out_specs=[col(0), pl.BlockSpec((1, hb, HEAD, HEAD), lambda g, i: (i, g, 0, 0))],
        out_shape=[SDS((s, RWKV_DIM), F32), SDS((nc, RWKV_HEADS, HEAD, HEAD), F32)],
        scratch_shapes=[pltpu.VMEM((hb, HEAD, HEAD), F32)],
        compiler_params=pltpu.CompilerParams(dimension_semantics=("parallel", "arbitrary"), vmem_limit_bytes=VMEM_LIMIT),
    )(zs, zs, lw, k2, na, b)


def _wkv_bwd(zs, lw, k2, na, b, s0s, dy):
    s = lw.shape[0]
    t, hb = WKV_CHUNK, WKV_HEADS_PER_STEP
    w = hb * HEAD
    nc, ng = s // t, RWKV_HEADS // hb

    def body(r_ref, v_ref, lw_ref, k_ref, a_ref, b_ref, s0_ref, dy_ref, dr_ref, dlw_ref, dk_ref, dv_ref, da_ref, db_ref, dstate):
        @pl.when(pl.program_id(1) == 0)
        def _():
            dstate[...] = jnp.zeros_like(dstate)

        _, vjp = jax.vjp(_wkv_chunk, s0_ref[0], *[_heads(t_ref[...], hb) for t_ref in (r_ref, lw_ref, k_ref, v_ref, a_ref, b_ref)])
        grads = vjp((_heads(dy_ref[...], hb), dstate[...]))
        dstate[...] = grads[0]
        for o_ref, gval in zip((dr_ref, dlw_ref, dk_ref, dv_ref, da_ref, db_ref), grads[1:]):
            o_ref[...] = _unheads(gval)

    def col(off):
        return pl.BlockSpec((t, w), functools.partial(lambda g, i, off: (nc - 1 - i, g + off), off=off))

    return pl.pallas_call(
        body, name="wkv_bwd", grid=(ng, nc),
        in_specs=[col(0), col(2 * ng), col(0), col(0), col(0), col(0),
                  pl.BlockSpec((1, hb, HEAD, HEAD), lambda g, i: (nc - 1 - i, g, 0, 0)), col(0)],
        out_specs=[col(0)] * 6,
        out_shape=[SDS((s, RWKV_DIM), F32)] * 6,
        scratch_shapes=[pltpu.VMEM((hb, HEAD, HEAD), F32)],
        compiler_params=pltpu.CompilerParams(dimension_semantics=("parallel", "arbitrary"), vmem_limit_bytes=VMEM_LIMIT),
    )(zs, zs, lw, k2, na, b, s0s, dy)


def _attn_fwd(q, k, v, d):
    s = q.shape[0]
    l = s // d
    nb = l // BAND
    assert nb * BAND == l
    qv, kv, vv = (t.reshape(l, d * GROUP_DIM) for t in (q, k, v))
    nh = GROUP_DIM // HEAD

    def body(q_ref, kp_ref, kc_ref, vp_ref, vc_ref, o_ref, l_ref):
        has_prev = pl.program_id(1) > 0
        o, lse = _attn_block(*[_heads(t_ref[...].astype(F32), nh) for t_ref in (q_ref, kp_ref, kc_ref, vp_ref, vc_ref)], has_prev)
        o_ref[...] = _unheads(o)
        l_ref[...] = _unheads(lse)

    cur = pl.BlockSpec((BAND, GROUP_DIM), lambda rho, i: (i, rho))
    prev = pl.BlockSpec((BAND, GROUP_DIM), lambda rho, i: (jnp.maximum(i - 1, 0), rho))
    o, lse = pl.pallas_call(
        body, name=f"attn_fwd_d{d}", grid=(d, nb),
        in_specs=[cur, prev, cur, prev, cur], out_specs=[cur, cur],
        out_shape=[SDS((l, d * GROUP_DIM), F32), SDS((l, d * GROUP_DIM), F32)],
        compiler_params=pltpu.CompilerParams(dimension_semantics=("parallel", "arbitrary"), vmem_limit_bytes=VMEM_LIMIT),
    )(qv, kv, kv, vv, vv)
    return o.reshape(s, GROUP_DIM), lse.reshape(s, GROUP_DIM)


def _attn_bwd(q, k, v, d, do, dlse):
    s = q.shape[0]
    l = s // d
    nb = l // BAND
    qv, kv, vv, dov, dlv = (t.reshape(l, d * GROUP_DIM) for t in (q, k, v, do, dlse))
    nh = GROUP_DIM // HEAD

    def body(q_ref, kp_ref, kc_ref, vp_ref, vc_ref, do_ref, dl_ref, dq_ref, dk_ref, dv_ref, ck, cv):
        step = pl.program_id(1)
        has_prev = step < nb - 1

        @pl.when(step == 0)
        def _():
            ck[...] = jnp.zeros_like(ck)
            cv[...] = jnp.zeros_like(cv)

        _, vjp = jax.vjp(functools.partial(_attn_block, has_prev=has_prev),
                         *[_heads(t_ref[...].astype(F32), nh) for t_ref in (q_ref, kp_ref, kc_ref, vp_ref, vc_ref)])
        dq, dkp, dkc, dvp, dvc = vjp((_heads(do_ref[...], nh), _heads(dl_ref[...], nh)))
        dq_ref[...] = _unheads(dq)
        dk_ref[...] = _unheads(dkc) + ck[...]
        dv_ref[...] = _unheads(dvc) + cv[...]
        ck[...] = _unheads(dkp)
        cv[...] = _unheads(dvp)

    cur = pl.BlockSpec((BAND, GROUP_DIM), lambda rho, i: (nb - 1 - i, rho))
    prev = pl.BlockSpec((BAND, GROUP_DIM), lambda rho, i: (jnp.maximum(nb - 2 - i, 0), rho))
    dq, dk, dv = pl.pallas_call(
        body, name=f"attn_bwd_d{d}", grid=(d, nb),
        in_specs=[cur, prev, cur, prev, cur, cur, cur], out_specs=[cur] * 3,
        out_shape=[SDS((l, d * GROUP_DIM), F32)] * 3,
        scratch_shapes=[pltpu.VMEM((BAND, GROUP_DIM), F32), pltpu.VMEM((BAND, GROUP_DIM), F32)],
        compiler_params=pltpu.CompilerParams(dimension_semantics=("parallel", "arbitrary"), vmem_limit_bytes=VMEM_LIMIT),
    )(qv, kv, kv, vv, vv, dov, dlv)
    return dq.reshape(s, GROUP_DIM), dk.reshape(s, GROUP_DIM), dv.reshape(s, GROUP_DIM)


def _coords():
    return lax.axis_index("x"), lax.axis_index("y"), lax.axis_index("c")


_CHIP_FLIPS = ((1, 0), (0, 1), (1, 1))


def _flip(v, f):
    return 1 - v if f else v


def _form(kind, r, c):
    return (N_CHIPS, r, c) if kind == "blk" else (r, N_CHIPS * c)


def _slot(ref, kind, j, rows, c):
    if kind == "blk":
        return ref.at[j] if rows is None else ref.at[j, rows]
    cols = pl.ds(pl.multiple_of(j * c, 128), c)
    return ref.at[:, cols] if rows is None else ref.at[rows, cols]


def _half(r, which, align):
    return pl.ds(pl.multiple_of(which * (r // 2), align), r // 2)


def _rcopy(src, dst, send_sems, recv_sems, kk, dev):
    return pltpu.make_async_remote_copy(src_ref=src, dst_ref=dst, send_sem=send_sems.at[kk], recv_sem=recv_sems.at[kk],
                                        device_id=dev, device_id_type=MESH)


def _gather_plan(specs, step):
    def copies(refs, ss, rs, received):
        x, y, c = _coords()
        out = []
        for w, (kind, r, cc) in enumerate(specs):
            mine, other = _half(r, c, 16), _half(r, 1 - c, 16)
            for kk, (fx, fy) in enumerate(_CHIP_FLIPS):
                px, py = _flip(x, fx), _flip(y, fy)
                if step == "ici":
                    sl = _slot(refs[w], kind, 2 * px + py if received else 2 * x + y, mine, cc)
                    dev = (px, py, c)
                else:
                    sl = _slot(refs[w], kind, 2 * px + py, other if received else mine, cc)
                    dev = (x, y, 1 - c)
                out.append(_rcopy(sl, sl, ss, rs, 3 * w + kk, dev))
        return out

    def issue(refs, ss, rs):
        return copies(refs, ss, rs, False)

    def expect(refs, ss, rs):
        return copies(refs, ss, rs, False), copies(refs, ss, rs, True)

    return issue, expect


def _gather_now(bufs, specs):
    nb = len(bufs)
    ici, d2d = _gather_plan(specs, "ici"), _gather_plan(specs, "d2d")

    def body(*refs):
        outs, (s1, r1, s2, r2) = refs[nb:2 * nb], refs[2 * nb:]
        sent, got = ici[1](outs, s1, r1)
        for cp in sent:
            cp.start()
        for cp in got:
            cp.wait_recv()
        sent2, got2 = d2d[1](outs, s2, r2)
        for cp in sent2:
            cp.start()
        for cp in got2:
            cp.wait_recv()
        for cp in sent + sent2:
            cp.wait_send()

    outs = pl.pallas_call(
        body, name="gather_ffn1",
        in_specs=[pl.BlockSpec(memory_space=pl.ANY)] * nb, out_specs=[pl.BlockSpec(memory_space=pl.ANY)] * nb,
        out_shape=[SDS(b.shape, b.dtype) for b in bufs], input_output_aliases={i: i for i in range(nb)},
        scratch_shapes=[pltpu.SemaphoreType.DMA((3 * nb,))] * 4,
    )(*bufs)
    return list(outs)


_HBM = pl.BlockSpec(memory_space=pltpu.HBM)
_SEM = pl.BlockSpec(memory_space=pltpu.SEMAPHORE)
_EFFECT = pltpu.SideEffectType.DATAFLOW_SIDE_EFFECTING


def _copies_start(name, bufs, n_sems, issue, after=None):
    nb = len(bufs)
    extra = [] if after is None else [after]

    def body(*refs):
        send_sems, recv_sems = refs[nb + len(extra)], refs[nb + len(extra) + 1]
        for cp in issue(refs[:nb], send_sems, recv_sems):
            cp.start()
        refs[-1][...] = jnp.zeros_like(refs[-1])

    outs = pl.pallas_call(
        body, name=name,
        out_shape=(pltpu.SemaphoreType.DMA((n_sems,)), pltpu.SemaphoreType.DMA((n_sems,)),
                   *[pltpu.HBM(b.shape, b.dtype) for b in bufs], SDS((8, 128), F32)),
        in_specs=[_HBM] * nb + [pl.BlockSpec(memory_space=pl.ANY)] * len(extra),
        out_specs=(_SEM, _SEM, *[_HBM] * nb, pl.BlockSpec(memory_space=pltpu.VMEM)),
        input_output_aliases={i: 2 + i for i in range(nb)},
        compiler_params=pltpu.CompilerParams(has_side_effects=_EFFECT),
    )(*[pltpu.with_memory_space_constraint(b, pltpu.HBM) for b in bufs], *extra)
    return outs[0], outs[1], list(outs[2:2 + nb]), outs[-1]


def _copies_wait(name, bufs, send_sems, recv_sems, after, expect):
    nb = len(bufs)

    def body(*refs):
        sent, received = expect(refs[:nb], refs[nb], refs[nb + 1])
        for cp in sent:
            cp.wait_send()
        for cp in received:
            cp.wait_recv()

    outs = pl.pallas_call(
        body, name=name,
        out_shape=tuple(pltpu.HBM(b.shape, b.dtype) for b in bufs),
        in_specs=(*[_HBM] * nb, _SEM, _SEM, pl.BlockSpec(memory_space=pl.ANY)), out_specs=tuple([_HBM] * nb),
        input_output_aliases={i: i for i in range(nb)},
        compiler_params=pltpu.CompilerParams(has_side_effects=_EFFECT),
    )(*bufs, send_sems, recv_sems, after)
    return list(outs)


def _add_pair(g, recv, kind, r, c, c_arr, name):
    h = r // 2
    if kind == "blk":
        tr = _row_tile(h, 512)
        grid = (N_CHIPS, h // tr)
        g_spec = pl.BlockSpec((1, 1, tr, c), lambda j, i, c_ref: (j, c_ref[0], i, 0))
        o_spec = pl.BlockSpec((1, tr, c), lambda j, i, c_ref: (j, i, 0))
        gv, oshape = g.reshape(N_CHIPS, 2, h, c), (N_CHIPS, h, c)
    else:
        tr = _row_tile(h, 64)
        grid = (h // tr,)
        g_spec = pl.BlockSpec((1, tr, N_CHIPS * c), lambda i, c_ref: (c_ref[0], i, 0))
        o_spec = pl.BlockSpec((tr, N_CHIPS * c), lambda i, c_ref: (i, 0))
        gv, oshape = g.reshape(2, h, N_CHIPS * c), (h, N_CHIPS * c)

    def body(c_ref, g_ref, r_ref, o_ref, ob_ref):
        v = (g_ref[:, 0] if kind == "blk" else g_ref[0]) + r_ref[...]
        o_ref[...] = v
        ob_ref[...] = v.astype(BF16)

    return pl.pallas_call(
        body, name=name,
        grid_spec=pltpu.PrefetchScalarGridSpec(num_scalar_prefetch=1, grid=grid, in_specs=[g_spec, o_spec], out_specs=[o_spec] * 2),
        out_shape=[SDS(oshape, F32), SDS(oshape, BF16)],
        compiler_params=pltpu.CompilerParams(vmem_limit_bytes=VMEM_LIMIT),
    )(c_arr, gv, recv)


def _add_chips(pair, recv, kind, r, c, me_arr, name):
    h = r // 2
    tr = _row_tile(h, 512)
    if kind == "blk":
        p_spec = pl.BlockSpec((1, tr, c), lambda i, me_ref: (me_ref[0], i, 0))
    else:
        p_spec = pl.BlockSpec((tr, c), lambda i, me_ref: (i, me_ref[0]))

    def body(me_ref, a_ref, r_ref, o_ref):
        own = a_ref[0] if kind == "blk" else a_ref[...]
        o_ref[...] = ((own + r_ref[0].astype(F32)) + r_ref[1].astype(F32)) + r_ref[2].astype(F32)

    return pl.pallas_call(
        body, name=name,
        grid_spec=pltpu.PrefetchScalarGridSpec(
            num_scalar_prefetch=1, grid=(h // tr,), in_specs=[p_spec, pl.BlockSpec((3, tr, c), lambda i, me_ref: (0, i, 0))],
            out_specs=pl.BlockSpec((tr, c), lambda i, me_ref: (i, 0))),
        out_shape=SDS((h, c), F32),
        compiler_params=pltpu.CompilerParams(vmem_limit_bytes=VMEM_LIMIT),
    )(me_arr, pair, recv)


class _GroupReduce:
    def __init__(self, tag, specs, c_arr, me_arr):
        self.tag, self.specs, self.c_arr, self.me_arr = tag, specs, c_arr, me_arr
        self.n = len(specs)

    def _plan(self, step):
        specs, n = self.specs, self.n

        def copies(refs, ss, rs, received):
            x, y, c = _coords()
            sib, out = (x, y, 1 - c), []
            for w, (_, kind, r, cc) in enumerate(specs):
                src, land = refs[w], refs[n + w]
                if step == "swap":
                    rows = _half(r, 1 - c, 8)
                    part = src.at[:, rows] if kind == "blk" else src.at[rows]
                    out.append(_rcopy(land if received else part, land, ss, rs, w, sib))
                elif step == "ici":
                    for kk, (fx, fy) in enumerate(_CHIP_FLIPS):
                        px, py = _flip(x, fx), _flip(y, fy)
                        part = land.at[kk] if received else _slot(src, kind, 2 * px + py, None, cc)
                        out.append(_rcopy(part, land.at[kk], ss, rs, 3 * w + kk, (px, py, c)))
                else:
                    there = land.at[_half(r, 1 - c if received else c, 8)]
                    out.append(_rcopy(there if received else src, there, ss, rs, w, sib))
            return out

        def issue(refs, ss, rs):
            return copies(refs, ss, rs, False)

        def expect(refs, ss, rs):
            return copies(refs, ss, rs, False), copies(refs, ss, rs, True)

        return issue, expect

    def swap_start(self, grads, after=None):
        lands = [lax.empty(_form(kind, r // 2, c), F32) for _, kind, r, c in self.specs]
        ss, rs, bufs, tok = _copies_start(f"rs_{self.tag}_swap", list(grads) + lands, self.n, self._plan("swap")[0], after=after)
        self.state = (ss, rs, bufs)
        return tok

    def swap_wait_ici_start(self, after):
        ss, rs, bufs = self.state
        bufs = _copies_wait(f"rs_{self.tag}_swap_wait", bufs, ss, rs, after, self._plan("swap")[1])
        pairs = [_add_pair(bufs[w], bufs[self.n + w], kind, r, c, self.c_arr, name=f"rs_{self.tag}_pair_{nm}")
                 for w, (nm, kind, r, c) in enumerate(self.specs)]
        self.pair = [pr[0] for pr in pairs]
        lands = [lax.empty((3, r // 2, c), BF16) for _, _, r, c in self.specs]
        ss, rs, bufs, tok = _copies_start(f"rs_{self.tag}_ici", [pr[1] for pr in pairs] + lands, 3 * self.n, self._plan("ici")[0])
        self.state = (ss, rs, bufs)
        return tok

    def ici_wait_join_start(self, after):
        ss, rs, bufs = self.state
        bufs = _copies_wait(f"rs_{self.tag}_ici_wait", bufs, ss, rs, after, self._plan("ici")[1])
        reds = [_add_chips(self.pair[w], bufs[self.n + w], kind, r, c, self.me_arr, name=f"rs_{self.tag}_chips_{nm}")
                for w, (nm, kind, r, c) in enumerate(self.specs)]
        lands = [lax.empty((r, c), F32) for _, _, r, c in self.specs]
        ss, rs, bufs, tok = _copies_start(f"rs_{self.tag}_join", reds + lands, self.n, self._plan("join")[0])
        self.state = (ss, rs, bufs)
        return tok

    def join_wait(self, after):
        ss, rs, bufs = self.state
        bufs = _copies_wait(f"rs_{self.tag}_join_wait", bufs, ss, rs, after, self._plan("join")[1])
        ci = lax.axis_index("c")
        return {nm: lax.dynamic_update_slice(bufs[self.n + w], bufs[w], (ci * (r // 2), 0))
                for w, (nm, _, r, c) in enumerate(self.specs)}


def _all_reduce_small(buf):
    rows, cols = buf.shape

    def body(x_ref, o_ref, gath, send_sems, recv_sems):
        x, y, c = _coords()
        me = 4 * x + 2 * y + c
        gath[me] = x_ref[...]
        sends = []
        for kk in range(1, 8):
            f = (kk >> 2) & 1, (kk >> 1) & 1, kk & 1
            px, py, pc = _flip(x, f[0]), _flip(y, f[1]), _flip(c, f[2])
            cp = pltpu.make_async_remote_copy(src_ref=x_ref, dst_ref=gath.at[me], send_sem=send_sems.at[kk - 1],
                                              recv_sem=recv_sems.at[kk - 1], device_id=(px, py, pc), device_id_type=MESH)
            cp.start()
            sends.append(cp)
        for kk in range(1, 8):
            f = (kk >> 2) & 1, (kk >> 1) & 1, kk & 1
            px, py, pc = _flip(x, f[0]), _flip(y, f[1]), _flip(c, f[2])
            there = gath.at[4 * px + 2 * py + pc]
            pltpu.make_async_remote_copy(src_ref=there, dst_ref=there, send_sem=send_sems.at[kk - 1],
                                         recv_sem=recv_sems.at[kk - 1], device_id=(px, py, pc), device_id_type=MESH).wait_recv()
        for cp in sends:
            cp.wait_send()
        acc = gath[0]
        for j in range(1, 8):
            acc = acc + gath[j]
        o_ref[...] = acc

    return pl.pallas_call(
        body, name="all_reduce_small",
        in_specs=[pl.BlockSpec(memory_space=pltpu.VMEM)], out_specs=pl.BlockSpec(memory_space=pltpu.VMEM),
        out_shape=SDS((rows, cols), F32),
        scratch_shapes=[pltpu.VMEM((8, rows, cols), F32), pltpu.SemaphoreType.DMA((7,)), pltpu.SemaphoreType.DMA((7,))],
    )(buf)


def _adamw_rows(w, g, m, v):
    m = ADAM_B1 * m + (1.0 - ADAM_B1) * g
    v = ADAM_B2 * v + (1.0 - ADAM_B2) * jnp.square(g)
    m_hat = m / (1.0 - ADAM_B1 ** ADAM_STEP)
    v_hat = v / (1.0 - ADAM_B2 ** ADAM_STEP)
    return -ADAM_LR * (m_hat / (jnp.sqrt(v_hat) + ADAM_EPS) + ADAM_WD * w), m, v


def _adamw(w, g, m, v, name, dep=None):
    rows, cols = w.shape
    tm = _pick(rows, (256, 128, 64, 16, 8))
    return _rows_call(_adamw_rows, [(t, 0, cols) for t in (w, g, m, v)], [], [(cols, F32)] * 3, tm=tm, name=name, dep=dep)


def _pack_small(parts):
    flat = jnp.concatenate([parts[n].reshape(-1) for n, _ in SMALL])
    return jnp.pad(flat, (0, SMALL_ROWS * PACK_COLS - flat.shape[0])).reshape(SMALL_ROWS, PACK_COLS)


def _unpack_small(buf, shapes):
    flat, out, off = buf.reshape(-1), {}, 0
    for n, sz in SMALL:
        out[n] = flat[off:off + sz].reshape(shapes[n])
        off += sz
    return out


def _lora_stack(parts):
    return jnp.concatenate([parts[n] for n, _ in LORA], axis=-2)


def _lora_split(stacked):
    out, off = {}, 0
    for n, rows in LORA:
        out[n] = stacked[..., off:off + rows, :]
        off += rows
    return out


def _ffn_fwd(x, gain, wg, wu, wd, tag):
    h = _rows_call(_rms, [(x, 0, D_MODEL)], [gain], [(D_MODEL, BF16)], tm=256, name=f"{tag}_norm")[0]
    gate = _mm(h, wg, name=f"{tag}_gate")
    up = _mm(h, wu, name=f"{tag}_up")
    nblk, s, f = gate.shape
    act = _rows_call(_swiglu_act, [(gate.reshape(nblk * s, f), 0, f), (up.reshape(nblk * s, f), 0, f)], [], [(f, BF16)],
                     tm=512, name=f"{tag}_act")[0].reshape(nblk, s, f)
    x_new = _mm(act, wd, sum_blocks=True, res=x, alpha=0.5, name=f"{tag}_down")
    return x_new, (x, h, gate, up, act)


def _ffn_bwd(dx_new, dx_new_bf, saved, gain, wg, wu, wd, tag, dep=None):
    x, h, gate, up, act = saved
    nblk, s, f = gate.shape
    d_wd = _mm(act, dx_new_bf, ta=True, alpha=0.5, name=f"{tag}_down_dw")
    dact = _mm(dx_new_bf, wd, tb=True, alpha=0.5, dep=dep, name=f"{tag}_down_dx")

    def act_bwd(gt, ut, ct):
        _, vjp = jax.vjp(_swiglu_act, gt, ut)
        return vjp(ct)

    dgate, dup = _rows_call(act_bwd, [(t.reshape(nblk * s, f), 0, f) for t in (gate, up, dact)], [], [(f, BF16)] * 2,
                            tm=512, name=f"{tag}_act_bwd")
    dgate, dup = dgate.reshape(nblk, s, f), dup.reshape(nblk, s, f)
    d_wg = _mm(h, dgate, ta=True, name=f"{tag}_gate_dw")
    d_wu = _mm(h, dup, ta=True, name=f"{tag}_up_dw")
    dh = _mm(dgate, wg, tb=True, sum_blocks=True, name=f"{tag}_gate_dx")
    dh = _mm(dup, wu, tb=True, sum_blocks=True, res=dh, name=f"{tag}_up_dx")
    dx, dx_bf, dgain = _norm_bwd(x, gain, dh, dx_new, f"{tag}_norm_bwd")
    return dx, dx_bf, dgain, d_wg, d_wu, d_wd


def _norm_bwd(x, gain, dh, dres, name):
    def f(xt, dht, drt, gt):
        _, vjp = jax.vjp(_rms, xt, gt)
        dxt, dgt = vjp(dht)
        return dxt + drt, dxt + drt, dgt

    return _rows_call(f, [(x, 0, D_MODEL), (dh, 0, D_MODEL), (dres, 0, D_MODEL)], [gain], [(D_MODEL, F32), (D_MODEL, BF16)],
                      [(1, D_MODEL)], tm=256, name=name)


def kernel(x, p, positions, ffn1_norm, ffn1_w_gate, ffn1_w_up, ffn1_w_down, mix_norm, w_in, rwkv_mu, rwkv_w0, rwkv_w2, rwkv_a0, rwkv_a2, rwkv_g2, rwkv_k_k, rwkv_k_a, rwkv_r_k, rwkv_gn_w, rwkv_gn_b, q_norm, k_norm, w_br_rwkv, w_br_attn, w_out, ffn2_norm, ffn2_w_gate, ffn2_w_up, ffn2_w_down, ple_norm, ple_w_gate, ple_w_proj, loss_target, m_ffn1_norm, m_ffn1_w_gate, m_ffn1_w_up, m_ffn1_w_down, m_mix_norm, m_w_in, m_rwkv_mu, m_rwkv_w0, m_rwkv_w2, m_rwkv_a0, m_rwkv_a2, m_rwkv_g2, m_rwkv_k_k, m_rwkv_k_a, m_rwkv_r_k, m_rwkv_gn_w, m_rwkv_gn_b, m_q_norm, m_k_norm, m_w_br_rwkv, m_w_br_attn, m_w_out, m_ffn2_norm, m_ffn2_w_gate, m_ffn2_w_up, m_ffn2_w_down, m_ple_norm, m_ple_w_gate, m_ple_w_proj, v_ffn1_norm, v_ffn1_w_gate, v_ffn1_w_up, v_ffn1_w_down, v_mix_norm, v_w_in, v_rwkv_mu, v_rwkv_w0, v_rwkv_w2, v_rwkv_a0, v_rwkv_a2, v_rwkv_g2, v_rwkv_k_k, v_rwkv_k_a, v_rwkv_r_k, v_rwkv_gn_w, v_rwkv_gn_b, v_q_norm, v_k_norm, v_w_br_rwkv, v_w_br_attn, v_w_out, v_ffn2_norm, v_ffn2_w_gate, v_ffn2_w_up, v_ffn2_w_down, v_ple_norm, v_ple_w_gate, v_ple_w_proj):
    args = dict(locals())
    wts = {n: args[n] for n in WEIGHTS}
    mom_m = {n: args["m_" + n] for n in WEIGHTS}
    mom_v = {n: args["v_" + n] for n in WEIGHTS}
    x0, tgt = x[0], loss_target[0]
    s = x0.shape[0]
    p_tok = p[0, 0]

    vec = {n: wts[n].reshape(1, -1) for n, _ in SMALL}
    xi, yi, ci = _coords()
    me = 2 * xi + yi
    shard_of = {n: wts[n][0] for g in GROUPS.values() for n, _, _, _ in g if n != "lora"}
    shard_of["lora"] = _lora_stack({n: wts[n][0] for n, _ in LORA})

    def whole_with_own(n, kind, r, c):
        at = (me, 0, 0) if kind == "blk" else (0, me * c)
        own = shard_of[n].astype(BF16)
        return lax.dynamic_update_slice(lax.empty(_form(kind, r, c), BF16), own[None] if kind == "blk" else own, at)

    specs = {g: [(kind, r, c) for _, kind, r, c in grp] for g, grp in GROUPS.items()}
    bufs = {g: [whole_with_own(*w) for w in grp] for g, grp in GROUPS.items()}
    plans = {(g, st): _gather_plan(specs[g], st) for g in ("mx", "f2") for st in ("ici", "d2d")}
    wb = dict(zip([w[0] for w in GROUPS["f1"]], _gather_now(bufs["f1"], specs["f1"])))
    ss_a, rs_a, buf_mx, tok_a = _copies_start("gather_mx_ici", bufs["mx"], 3 * len(bufs["mx"]), plans["mx", "ici"][0],
                                              after=wb["ffn1_w_gate"])

    inv_freq = 1.0 / (ROPE_THETA ** (jnp.arange(0, HEAD, 2, dtype=F32) / HEAD))
    ang = positions[0].astype(F32)[:, None] * inv_freq
    cos, sin = jnp.cos(ang), jnp.sin(ang)
    cos2, sin2 = jnp.concatenate([cos, cos], axis=1), jnp.concatenate([-sin, sin], axis=1)

    x1, ffn1_saved = _ffn_fwd(x0, vec["ffn1_norm"] + tok_a[0, 0], wb["ffn1_w_gate"], wb["ffn1_w_up"], wb["ffn1_w_down"], "ffn1")
    buf_mx = _copies_wait("gather_mx_ici_wait", buf_mx, ss_a, rs_a, x1, plans["mx", "ici"][1])
    ss_b, rs_b, buf_mx, tok_b = _copies_start("gather_mx_d2d", buf_mx, 3 * len(buf_mx), plans["mx", "d2d"][0])
    ss_c, rs_c, buf_f2, tok_c = _copies_start("gather_f2_ici", bufs["f2"], 3 * len(bufs["f2"]), plans["f2", "ici"][0])
    h = _rows_call(_rms, [(x1, 0, D_MODEL)], [vec["mix_norm"] + (tok_b[0, 0] + tok_c[0, 0])], [(D_MODEL, BF16)], tm=256,
                   name="mix_norm")[0]
    buf_mx = _copies_wait("gather_mx_d2d_wait", buf_mx, ss_b, rs_b, h, plans["mx", "d2d"][1])
    wb.update(zip([w[0] for w in GROUPS["mx"]], buf_mx))
    w_in_all = wb["w_in"]
    w_in_r, w_in_a, w_in_g = w_in_all[:, :RWKV_COLS], w_in_all[:, RWKV_COLS:RWKV_COLS + ATTN_COLS], w_in_all[:, RWKV_COLS + ATTN_COLS:]
    lora = _lora_split(wb["lora"])
    w2, a2, g2 = lora["rwkv_w2"], lora["rwkv_a2"], lora["rwkv_g2"]
    w_brr, w_bra = wb["w_br_rwkv"], wb["w_br_attn"]
    w_o = wb["w_out"].reshape(D_MODEL, D_MODEL)
    z_r = _mm(h, w_in_r, name="in_rwkv")
    z_a = _mm(h, w_in_a, name="in_attn")
    z_g = _mm(h, w_in_g, name="in_gate")

    zs = _shift_fwd(z_r, vec["rwkv_mu"])
    pre_params = [vec["rwkv_w0"], w2, vec["rwkv_a0"], a2, g2, vec["rwkv_k_k"], vec["rwkv_k_a"]]
    def pre_fwd(*t):
        res = _rwkv_pre(*t)
        return res[1], res[2], res[4], res[5], res[6]

    lw, k2, na, kb, gate_r = _rows_call(pre_fwd, [(zs, 0, RWKV_COLS)], pre_params, [(RWKV_DIM, F32)] * 5, tm=256, name="rwkv_pre")
    y_scan, s0s = _wkv_fwd(zs, lw, k2, na, kb)
    buf_f2 = _copies_wait("gather_f2_ici_wait", buf_f2, ss_c, rs_c, y_scan, plans["f2", "ici"][1])
    ss_d, rs_d, buf_f2, tok_d = _copies_start("gather_f2_d2d", buf_f2, 3 * len(buf_f2), plans["f2", "d2d"][0])
    post_params = [vec["rwkv_gn_w"] + tok_d[0, 0], vec["rwkv_gn_b"], vec["rwkv_r_k"]]
    post_rows = [(y_scan, 0, RWKV_DIM), (zs, 0, RWKV_DIM), (k2, 0, RWKV_DIM), (zs, 2, RWKV_DIM), (gate_r, 0, RWKV_DIM)]
    y_rwkv = _rows_call(_rwkv_post, post_rows, post_params, [(RWKV_DIM, BF16)], tm=256, name="rwkv_post")[0]

    def qk_fwd(qt, kt, ct, st, qg, kg):
        return _norm_rope(qt, qg, ct, st), _norm_rope(kt, kg, ct, st)

    qk_rows = [(z_a, 0, ATTN_DIM), (z_a, 1, ATTN_DIM), (cos2, 0, HEAD), (sin2, 0, HEAD)]
    q_rot, k_rot = _rows_call(qk_fwd, qk_rows, [vec["q_norm"], vec["k_norm"]], [(ATTN_DIM, BF16)] * 2, tm=256, name="attn_pre")
    def group(t, g, off=0):
        return t[:, off + g * GROUP_DIM:off + (g + 1) * GROUP_DIM].astype(BF16)

    qkv = [(group(q_rot, g), group(k_rot, g), group(z_a, g, 2 * ATTN_DIM)) for g in range(len(ATTN_DILATIONS))]
    outs, lses = zip(*[_attn_fwd(*qkv[g], d) for g, d in enumerate(ATTN_DILATIONS)])
    comb_rows = [(t, 0, GROUP_DIM) for t in outs + lses]
    y_attn = _rows_call(_attn_combine, comb_rows, [], [(GROUP_DIM, BF16)], tm=256, name="attn_combine")[0]

    br = _mm(y_rwkv, w_brr, name="branch_rwkv")
    ba = _mm(y_attn, w_bra, name="branch_attn")
    merge_rows = [(z_g, 0, D_MODEL), (z_g, 1, D_MODEL), (br, 0, D_MODEL), (ba, 0, D_MODEL)]
    merged = _rows_call(_merge, merge_rows, [], [(D_MODEL, BF16)], tm=256, name="merge")[0]
    x2 = _mm(merged, w_o, res=x1, name="out_proj")
    buf_f2 = _copies_wait("gather_f2_d2d_wait", buf_f2, ss_d, rs_d, x2, plans["f2", "d2d"][1])
    wb.update(zip([w[0] for w in GROUPS["f2"]], buf_f2))
    w_pp, w_pg = wb["ple_w_proj"], wb["ple_w_gate"].reshape(D_MODEL, D_MODEL)
    x3, ffn2_saved = _ffn_fwd(x2, vec["ffn2_norm"], wb["ffn2_w_gate"], wb["ffn2_w_up"], wb["ffn2_w_down"], "ffn2")
    hp = _rows_call(_rms, [(x3, 0, D_MODEL)], [vec["ple_norm"]], [(D_MODEL, BF16)], tm=256, name="ple_norm")[0]
    pg = _mm(hp, w_pg, name="ple_gate")
    pp = _mm(p_tok, w_pp, name="ple_proj")

    def head(x3t, pgt, ppt, tt):
        sg = _sigmoid(pgt)
        err = x3t + sg * ppt - tt
        dx4 = err * (1.0 / D_MODEL)
        loss = 0.5 * jnp.sum(jnp.mean(err * err, axis=-1, keepdims=True), axis=0, keepdims=True)
        return dx4, dx4 * ppt * sg * (1.0 - sg), dx4 * sg, jnp.broadcast_to(loss, (8, 128))

    head_rows = [(x3, 0, D_MODEL), (pg, 0, D_MODEL), (pp, 0, D_MODEL), (tgt, 0, D_MODEL)]
    dx4, dpg, dpp, loss_tile = _rows_call(head, head_rows, [], [(D_MODEL, F32), (D_MODEL, BF16), (D_MODEL, BF16)], [(8, 128)],
                                          tm=256, name="ple_loss")

    c_arr = jnp.reshape(ci, (1,)).astype(jnp.int32)
    me_arr = jnp.reshape(me, (1,)).astype(jnp.int32)
    red = {g: _GroupReduce(g, grp, c_arr, me_arr) for g, grp in GROUPS.items()}
    gw, gs = {}, {}
    gw["ple_w_proj"] = _mm(p_tok, dpp, ta=True, name="ple_proj_dw")
    gw["ple_w_gate"] = _mm(hp, dpg, ta=True, name="ple_gate_dw")
    dhp = _mm(dpg, w_pg, tb=True, name="ple_gate_dx")
    dx3, dx3_bf, gs["ple_norm"] = _norm_bwd(x3, vec["ple_norm"], dhp, dx4, "ple_norm_bwd")
    dx2, dx2_bf, gs["ffn2_norm"], gw["ffn2_w_gate"], gw["ffn2_w_up"], gw["ffn2_w_down"] = _ffn_bwd(
        dx3, dx3_bf, ffn2_saved, vec["ffn2_norm"], wb["ffn2_w_gate"], wb["ffn2_w_up"], wb["ffn2_w_down"], "ffn2")
    gw["ple_w_gate"] = gw["ple_w_gate"].reshape(N_CHIPS, D_MODEL // N_CHIPS, D_MODEL)
    tok = red["f2"].swap_start([gw[w[0]] for w in GROUPS["f2"]])
    gw["w_out"] = _mm(merged, dx2_bf, ta=True, name="out_proj_dw")
    dmerged = _mm(dx2_bf, w_o, tb=True, dep=tok, name="out_proj_dx")

    def merge_bwd(zgr, zga, brt, bat, ct):
        _, vjp = jax.vjp(_merge, zgr, zga, brt, bat)
        d1, d2, d3, d4 = vjp(ct)
        return jnp.concatenate([d1, d2], axis=1), d3, d4

    dz_g, dbr, dba = _rows_call(merge_bwd, merge_rows + [(dmerged, 0, D_MODEL)], [],
                                [(2 * D_MODEL, BF16), (D_MODEL, BF16), (D_MODEL, BF16)], tm=256, name="merge_bwd")
    tok = red["f2"].swap_wait_ici_start(dz_g)
    gw["w_br_rwkv"] = _mm(y_rwkv, dbr, ta=True, name="branch_rwkv_dw")
    gw["w_br_attn"] = _mm(y_attn, dba, ta=True, name="branch_attn_dw")
    dy_rwkv = _mm(dbr, w_brr, tb=True, dep=tok, name="branch_rwkv_dx")
    dy_attn = _mm(dba, w_bra, tb=True, dep=tok, name="branch_attn_dx")

    def comb_bwd(*t):
        _, vjp = jax.vjp(_attn_combine, *t[:6])
        return vjp(t[6])

    dcomb = _rows_call(comb_bwd, comb_rows + [(dy_attn, 0, GROUP_DIM)], [], [(GROUP_DIM, F32)] * 6, tm=256, name="attn_combine_bwd")
    dqs, dks, dvs = zip(*[_attn_bwd(*qkv[g], d, dcomb[g], dcomb[3 + g]) for g, d in enumerate(ATTN_DILATIONS)])

    def qk_bwd(qt, kt, ct, st, *rest):
        dq = jnp.concatenate(rest[0:3], axis=1)
        dk = jnp.concatenate(rest[3:6], axis=1)
        qg, kg = rest[9], rest[10]
        _, vjp = jax.vjp(lambda a_, b_, c_, d_: qk_fwd(a_, b_, ct, st, c_, d_), qt, kt, qg, kg)
        dqt, dkt, dqg, dkg = vjp((dq, dk))
        return jnp.concatenate((dqt, dkt) + tuple(rest[6:9]), axis=1), dqg, dkg

    dz_a, gs["q_norm"], gs["k_norm"] = _rows_call(
        qk_bwd, qk_rows + [(t, 0, GROUP_DIM) for t in dqs + dks + dvs], [vec["q_norm"], vec["k_norm"]],
        [(ATTN_COLS, BF16)], [(1, HEAD), (1, HEAD)], tm=256, name="attn_pre_bwd")
    tok = red["f2"].ici_wait_join_start(dz_a)

    def post_bwd(*t):
        _, vjp = jax.vjp(_rwkv_post, *t[:5], *t[6:])
        return vjp(t[5])

    dy_scan, dr_post, dk2_post, dv_post, dgate_r, gs["rwkv_gn_w"], gs["rwkv_gn_b"], gs["rwkv_r_k"] = _rows_call(
        post_bwd, post_rows + [(dy_rwkv, 0, RWKV_DIM)], post_params, [(RWKV_DIM, F32)] * 5, [(1, RWKV_DIM)] * 3,
        tm=256, name="rwkv_post_bwd", dep=tok)
    grad_big = red["f2"].join_wait(dy_scan)
    dr_s, dlw, dk2_s, dv_s, dna, dkb = _wkv_bwd(zs, lw, k2, na, kb, s0s, dy_scan)

    def pre_bwd(zt, c_r1, c_r2, c_lw, c_k1, c_k2, c_v1, c_v2, c_a, c_b, c_g, *params):
        _, vjp = jax.vjp(_rwkv_pre, zt, *params)
        return vjp((c_r1 + c_r2, c_lw, c_k1 + c_k2, c_v1 + c_v2, c_a, c_b, c_g))

    pre_cts = [dr_s, dr_post, dlw, dk2_s, dk2_post, dv_s, dv_post, dna, dkb, dgate_r]
    dzs, gs["rwkv_w0"], g_w2, gs["rwkv_a0"], g_a2, g_g2, gs["rwkv_k_k"], gs["rwkv_k_a"] = _rows_call(
        pre_bwd, [(zs, 0, RWKV_COLS)] + [(t, 0, RWKV_DIM) for t in pre_cts], pre_params, [(RWKV_COLS, F32)],
        [q.shape for q in pre_params], tm=256, name="rwkv_pre_bwd")
    dz_r, gs["rwkv_mu"] = _shift_bwd(z_r, vec["rwkv_mu"], dzs)

    g_w_in = jnp.concatenate([_mm(h, dz_r, ta=True, name="in_rwkv_dw"), _mm(h, dz_a, ta=True, name="in_attn_dw"),
                              _mm(h, dz_g, ta=True, name="in_gate_dw")], axis=1)
    gw["w_in"], gw["lora"] = g_w_in, jnp.concatenate([g_w2, g_a2, g_g2], axis=0)
    gw["w_out"] = gw["w_out"].reshape(N_CHIPS, D_MODEL // N_CHIPS, D_MODEL)
    tok = red["mx"].swap_start([gw[w[0]] for w in GROUPS["mx"]])
    dh = _mm(dz_r, w_in_r, tb=True, dep=tok, name="in_rwkv_dx")
    dh = _mm(dz_a, w_in_a, tb=True, res=dh, name="in_attn_dx")
    dh = _mm(dz_g, w_in_g, tb=True, res=dh, name="in_gate_dx")
    dx1, dx1_bf, gs["mix_norm"] = _norm_bwd(x1, vec["mix_norm"], dh, dx2, "mix_norm_bwd")
    tok = red["mx"].swap_wait_ici_start(dx1_bf)
    dx0, _, gs["ffn1_norm"], gw["ffn1_w_gate"], gw["ffn1_w_up"], gw["ffn1_w_down"] = _ffn_bwd(
        dx1, dx1_bf, ffn1_saved, vec["ffn1_norm"], wb["ffn1_w_gate"], wb["ffn1_w_up"], wb["ffn1_w_down"], "ffn1", dep=tok)
    tok_mx = red["mx"].ici_wait_join_start(dx0)
    tok = red["f1"].swap_start([gw[w[0]] for w in GROUPS["f1"]], after=tok_mx)

    grads, deltas, new_m, new_v = {}, {}, {}, {}

    def update(group, dep):
        last = None
        for n, _, _, _ in GROUPS[group]:
            g2d = grad_big[n]
            if n == "lora":
                w_, m_, v_ = (_lora_stack({k: t[k][0] for k, _ in LORA}) for t in (wts, mom_m, mom_v))
            else:
                w_, m_, v_ = wts[n][0], mom_m[n][0], mom_v[n][0]
            res = (g2d,) + tuple(_adamw(w_, g2d, m_, v_, name=f"adamw_{n}", dep=dep))
            dep = last = res[1]
            for store, val in zip((grads, deltas, new_m, new_v), res):
                store.update({k: t[None] for k, t in _lora_split(val).items()} if n == "lora" else {n: val[None]})
        return last

    last = update("f2", tok)
    tok = red["f1"].swap_wait_ici_start(last)
    grad_big.update(red["mx"].join_wait(last))
    last = update("mx", tok)

    flat = jnp.concatenate([gs[n].reshape(-1) for n, _ in SMALL] + [loss_tile[0, 0:1]])
    small_buf = jnp.pad(flat, (0, SMALL_ROWS * PACK_COLS - flat.shape[0])).reshape(SMALL_ROWS, PACK_COLS)
    small_sum = _all_reduce_small(small_buf)
    n_small = sum(sz for _, sz in SMALL)
    loss = small_sum.reshape(-1)[n_small]
    grad_small = _unpack_small(small_sum, {n: wts[n].shape for n, _ in SMALL})
    d_s, m_s, v_s = _adamw(_pack_small(wts), small_sum, _pack_small(mom_m), _pack_small(mom_v), name="adamw_small", dep=last)
    shapes = {n: wts[n].shape for n, _ in SMALL}
    d_s, m_s, v_s = _unpack_small(d_s, shapes), _unpack_small(m_s, shapes), _unpack_small(v_s, shapes)
    for n, _ in SMALL:
        grads[n], deltas[n], new_m[n], new_v[n] = grad_small[n], d_s[n], m_s[n], v_s[n]

    tok = red["f1"].ici_wait_join_start(m_s["ffn1_norm"])
    grad_big.update(red["f1"].join_wait(tok))
    update("f1", None)

    return (loss, dx0[None], *[grads[n] for n in WEIGHTS], *[deltas[n] for n in WEIGHTS],
            *[new_m[n] for n in WEIGHTS], *[new_v[n] for n in WEIGHTS])
```

```python
import functools

import jax
import jax.numpy as jnp
from jax import lax
from jax.experimental import pallas as pl
from jax.experimental.pallas import tpu as pltpu

F32, BF16 = jnp.float32, jnp.bfloat16
HI = lax.Precision.HIGHEST
MESH = pl.DeviceIdType.MESH
SDS = jax.ShapeDtypeStruct

D_MODEL = 1024
HEAD = 64
RWKV_HEADS = 8
RWKV_DIM = RWKV_HEADS * HEAD
DECAY_LORA, ICLR_LORA, GATE_LORA = 64, 64, 128
GN_EPS = 64e-5
RMS_EPS = 1e-6
ATTN_DILATIONS = (1, 4, 16)
BAND = 128
ATTN_DIM = 768
GROUP_DIM = 256
ROPE_THETA = 10000.0
NEG_INF = -1e30
RWKV_COLS = 3 * RWKV_DIM + DECAY_LORA + ICLR_LORA + GATE_LORA
ATTN_COLS = 3 * ATTN_DIM
ADAM_LR, ADAM_B1, ADAM_B2, ADAM_EPS, ADAM_WD, ADAM_STEP = 0.001, 0.9, 0.999, 1e-08, 0.01, 10

WKV_CHUNK = 64
WKV_HEADS_PER_STEP = 8
N_CHIPS = 4
PACK_COLS = 1024
VMEM_LIMIT = 48 * 1024 * 1024

TRANSPOSED = ("ffn1_w_gate", "ffn1_w_up", "ffn2_w_gate", "ffn2_w_up")
LORA = (("rwkv_w2", 64), ("rwkv_a2", 64), ("rwkv_g2", 128))
GROUPS = {
    "f1": (("ffn1_w_gate", "blk", 704, 1024), ("ffn1_w_up", "blk", 704, 1024), ("ffn1_w_down", "blk", 704, 1024)),
    "mx": (("w_in", "col", 1024, 1536), ("lora", "col", 256, 128), ("w_br_rwkv", "col", 512, 256),
           ("w_br_attn", "col", 256, 256), ("w_out", "blk", 256, 1024)),
    "f2": (("ffn2_w_gate", "blk", 704, 1024), ("ffn2_w_up", "blk", 704, 1024), ("ffn2_w_down", "blk", 704, 1024),
           ("ple_w_gate", "blk", 256, 1024), ("ple_w_proj", "col", 256, 256)),
}
SMALL = (
    ("ffn1_norm", 1024), ("mix_norm", 1024), ("ffn2_norm", 1024), ("ple_norm", 1024), ("rwkv_mu", 1792),
    ("rwkv_w0", 512), ("rwkv_a0", 512), ("rwkv_k_k", 512), ("rwkv_k_a", 512), ("rwkv_r_k", 512),
    ("rwkv_gn_w", 512), ("rwkv_gn_b", 512), ("q_norm", 64), ("k_norm", 64),
)
SMALL_ROWS = 16
WEIGHTS = (
    "ffn1_norm", "ffn1_w_gate", "ffn1_w_up", "ffn1_w_down", "mix_norm", "w_in", "rwkv_mu", "rwkv_w0", "rwkv_w2",
    "rwkv_a0", "rwkv_a2", "rwkv_g2", "rwkv_k_k", "rwkv_k_a", "rwkv_r_k", "rwkv_gn_w", "rwkv_gn_b", "q_norm", "k_norm",
    "w_br_rwkv", "w_br_attn", "w_out", "ffn2_norm", "ffn2_w_gate", "ffn2_w_up", "ffn2_w_down", "ple_norm",
    "ple_w_gate", "ple_w_proj",
)


def _row_tile(n, most=704):
    for t in range(most - most % 16, 0, -16):
        if n % t == 0:
            return t
    return n


def _pick(n, cands):
    for c in cands:
        if n % c == 0:
            return c
    return n


def _mm(a, b, *, ta=False, tb=False, sum_blocks=False, out_dtype=F32, res=None, alpha=1.0, dep=None, name):
    flat = a.ndim == 2 and b.ndim == 2
    a3 = a if a.ndim == 3 else a[None]
    b3 = b if b.ndim == 3 else b[None]
    na, nbb = a3.shape[0], b3.shape[0]
    nblk = max(na, nbb)
    kdim, m = (a3.shape[1], a3.shape[2]) if ta else (a3.shape[2], a3.shape[1])
    n = b3.shape[1] if tb else b3.shape[2]
    assert (b3.shape[2] if tb else b3.shape[1]) == kdim
    tm = _pick(m, (1024, 512, 256, 128))
    tn = _pick(n, (1024, 896, 768, 512, 256, 128))
    tk = kdim if kdim <= 2304 else _pick(kdim, (1024, 512, 256, 128))
    nk = kdim // tk
    direct = nk == 1 and not sum_blocks

    if sum_blocks:
        grid = (m // tm, n // tn, nblk, nk)

        def ids(i, c, j, k):
            return i, c, j, k
    else:
        grid = (nblk, m // tm, n // tn, nk)

        def ids(j, i, c, k):
            return i, c, j, k

    def amap(*g):
        i, c, j, k = ids(*g)
        jj = j if na > 1 else 0
        return (jj, k, i) if ta else (jj, i, k)

    def bmap(*g):
        i, c, j, k = ids(*g)
        jj = j if nbb > 1 else 0
        return (jj, c, k) if tb else (jj, k, c)

    if sum_blocks:
        oshape, oblk = (m, n), (tm, tn)

        def omap(*g):
            i, c, j, k = ids(*g)
            return i, c
    else:
        oshape, oblk = (nblk, m, n), (1, tm, tn)

        def omap(*g):
            i, c, j, k = ids(*g)
            return j, i, c

    dn = (((0 if ta else 1,), (1 if tb else 0,)), ((), ()))
    has_res = res is not None

    def body(*refs):
        refs = list(refs)
        acc = None if direct else refs.pop()
        o_ref = refs.pop()
        a_ref, b_ref = refs[0], refs[1]
        r_ref = refs[2] if has_res else None

        def finish(v):
            if alpha != 1.0:
                v = v * alpha
            if has_res:
                v = v + r_ref[...].reshape(v.shape).astype(F32)
            o_ref[...] = v.reshape(o_ref.shape).astype(o_ref.dtype)

        if direct:
            finish(lax.dot_general(a_ref[0].astype(BF16), b_ref[0].astype(BF16), dn, preferred_element_type=F32))
            return
        k = pl.program_id(3)
        if sum_blocks:
            j = pl.program_id(2)
            first = jnp.logical_and(j == 0, k == 0)
            last = jnp.logical_and(j == nblk - 1, k == nk - 1)
        else:
            first, last = k == 0, k == nk - 1

        @pl.when(first)
        def _():
            acc[...] = jnp.zeros_like(acc)

        acc[...] += lax.dot_general(a_ref[0].astype(BF16), b_ref[0].astype(BF16), dn, preferred_element_type=F32)

        @pl.when(last)
        def _():
            finish(acc[...])

    in_specs = [pl.BlockSpec((1, tk, tm) if ta else (1, tm, tk), amap), pl.BlockSpec((1, tn, tk) if tb else (1, tk, tn), bmap)]
    args = [a3, b3]
    if has_res:
        res3 = res if (sum_blocks or res.ndim == 3) else res[None]
        in_specs.append(pl.BlockSpec(oblk, omap))
        args.append(res3)
    if dep is not None:
        in_specs.append(pl.BlockSpec(memory_space=pl.ANY))
        args.append(dep)
    out = pl.pallas_call(
        body,
        name=name,
        grid=grid,
        in_specs=in_specs,
        out_specs=pl.BlockSpec(oblk, omap),
        out_shape=SDS(oshape, out_dtype),
        scratch_shapes=[] if direct else [pltpu.VMEM((tm, tn), F32)],
        compiler_params=pltpu.CompilerParams(
            dimension_semantics=("parallel", "parallel", "arbitrary", "arbitrary") if sum_blocks
            else ("parallel", "parallel", "parallel", "arbitrary"),
            vmem_limit_bytes=VMEM_LIMIT),
    )(*args)
    if flat and not sum_blocks:
        out = out[0]
    return out


def _rows_call(f, rows, params, outs, accs=(), *, tm, name, dep=None):
    s = rows[0][0].shape[0]
    nr, npar, no = len(rows), len(params), len(outs)
    nin = nr + npar + (0 if dep is None else 1)
    in_specs = [pl.BlockSpec((tm, w), functools.partial(lambda i, cb: (i, cb), cb=cb)) for (_, cb, w) in rows]
    in_specs += [pl.BlockSpec(p.shape, functools.partial(lambda i, nd: (0,) * nd, nd=p.ndim)) for p in params]
    if dep is not None:
        in_specs.append(pl.BlockSpec(memory_space=pl.ANY))
    out_shape = [SDS((s, w), dt) for (w, dt) in outs] + [SDS(tuple(sh), F32) for sh in accs]
    out_specs = [pl.BlockSpec((tm, w), lambda i: (i, 0)) for (w, _) in outs]
    out_specs += [pl.BlockSpec(tuple(sh), functools.partial(lambda i, nd: (0,) * nd, nd=len(sh))) for sh in accs]

    def body(*refs):
        rin, pin = refs[:nr], refs[nr:nr + npar]
        oo, ao = refs[nin:nin + no], refs[nin + no:]
        res = f(*[r[...] for r in rin], *[p[...] for p in pin])
        if not isinstance(res, (tuple, list)):
            res = (res,)
        for o_ref, v in zip(oo, res[:no]):
            o_ref[...] = v.astype(o_ref.dtype)
        i = pl.program_id(0)
        for a_ref, v in zip(ao, res[no:]):
            @pl.when(i == 0)
            def _():
                a_ref[...] = jnp.zeros_like(a_ref)

            a_ref[...] += v.reshape(a_ref.shape)

    res = pl.pallas_call(
        body,
        name=name,
        grid=(s // tm,),
        in_specs=in_specs,
        out_specs=out_specs,
        out_shape=out_shape,
        compiler_params=pltpu.CompilerParams(dimension_semantics=("arbitrary",), vmem_limit_bytes=VMEM_LIMIT),
    )(*[r[0] for r in rows], *params, *([] if dep is None else [dep]))
    return res


def _mmv(a, b, mode):
    ca = 0 if mode[0] == "t" else 1
    cb = 1 if mode[1] == "t" else 0
    return lax.dot_general(a.astype(BF16), b.astype(BF16), (((ca,), (cb,)), ((), ())), preferred_element_type=F32)


@functools.partial(jax.custom_vjp, nondiff_argnums=(2,))
def _bdot(a, b, mode):
    return _mmv(a, b, mode)


def _bdot_fwd(a, b, mode):
    return _mmv(a, b, mode), (a, b)


def _bdot_bwd(mode, saved, g):
    a, b = saved
    if mode == "nn":
        return _mmv(g, b, "nt"), _mmv(a, g, "tn")
    if mode == "nt":
        return _mmv(g, b, "nn"), _mmv(g, a, "tn")
    return _mmv(b, g, "nt"), _mmv(a, g, "nn")


_bdot.defvjp(_bdot_fwd, _bdot_bwd)


def _hdot(a, b, mode="nn", precision=HI):
    ca = 0 if mode[0] == "t" else 1
    cb = 1 if mode[1] == "t" else 0
    return lax.dot_general(a, b, (((ca,), (cb,)), ((), ())), precision=precision, preferred_element_type=F32)


def _segsum(x):
    c = x.shape[-1]
    blk = min(c, 256)
    r = lax.broadcasted_iota(jnp.int32, (blk, blk), 0) >> 6
    q = lax.broadcasted_iota(jnp.int32, (blk, blk), 1) >> 6
    ones = jnp.where(r == q, 1.0, 0.0).astype(F32)
    parts = [_hdot(x[:, i:i + blk], ones, precision=lax.Precision.HIGH) for i in range(0, c, blk)]
    return parts[0] if len(parts) == 1 else jnp.concatenate(parts, axis=1)


def _sigmoid(x):
    return jax.nn.sigmoid(x)


def _softplus(x):
    return jnp.maximum(x, 0.0) + jnp.log(1.0 + jnp.exp(-jnp.abs(x)))


def _rms(x, gain):
    return x * lax.rsqrt(jnp.mean(x * x, axis=-1, keepdims=True) + RMS_EPS) * gain


def _swiglu_act(gate, up):
    return gate * _sigmoid(gate) * up


def _rwkv_pre(zs, w0, w2, a0, a2, g2, k_k, k_a):
    r, k, v = zs[:, 0:512], zs[:, 512:1024], zs[:, 1024:1536]
    lora = zs[:, 1536:1792]
    wd, ad, gd = lora[:, 0:64], lora[:, 64:128], lora[:, 128:256]
    w = -_softplus(-(w0 + _bdot(jnp.tanh(wd), w2, "nn"))) - 0.5
    a = _sigmoid(a0 + _bdot(ad, a2, "nn"))
    g = _bdot(_sigmoid(gd), g2, "nn")
    kk = k * k_k
    kk = kk * lax.rsqrt(jnp.maximum(_segsum(kk * kk), 1e-24))
    k2 = k * (1.0 + (a - 1.0) * k_a)
    return r, -jnp.exp(w), k2, v, -kk, kk * a, g


def _rwkv_post(y, r, k2, v, g, gn_w, gn_b, r_k):
    mean = _segsum(y) * (1.0 / HEAD)
    yc = y - mean
    var = _segsum(yc * yc) * (1.0 / HEAD)
    yn = yc * lax.rsqrt(var + GN_EPS) * gn_w + gn_b
    bonus = _segsum(r * k2 * r_k) * v
    return (yn + bonus) * g


def _swap_halves(x):
    lane = lax.broadcasted_iota(jnp.int32, x.shape, 1)
    return jnp.where((lane & 32) == 0, jnp.roll(x, -32, axis=1), jnp.roll(x, 32, axis=1))


def _norm_rope(x, gain, cos, sin):
    heads = x.shape[1] // HEAD
    def rep(t):
        return jnp.concatenate([t] * heads, axis=1)

    xn = x * lax.rsqrt(_segsum(x * x) * (1.0 / HEAD) + RMS_EPS) * rep(gain)
    return xn * rep(cos) + _swap_halves(xn) * rep(sin)


def _attn_combine(o0, o1, o2, l0, l1, l2):
    m = jnp.maximum(jnp.maximum(l0, l1), l2)
    e0, e1, e2 = jnp.exp(l0 - m), jnp.exp(l1 - m), jnp.exp(l2 - m)
    return (e0 * o0 + e1 * o1 + e2 * o2) / (e0 + e1 + e2)


def _merge(zgr, zga, br, ba):
    return _sigmoid(zgr) * br + _sigmoid(zga) * ba


def _attn_block(q, kp, kc, vp, vc, has_prev):
    iq = lax.broadcasted_iota(jnp.int32, (1, BAND, BAND), 1)
    ik = lax.broadcasted_iota(jnp.int32, (1, BAND, BAND), 2)
    s_c = jnp.where(iq >= ik, _bdotb(q, kc, "nt") * (HEAD ** -0.5), NEG_INF)
    s_p = jnp.where(jnp.logical_and(iq <= ik, has_prev), _bdotb(q, kp, "nt") * (HEAD ** -0.5), NEG_INF)
    m = lax.stop_gradient(jnp.maximum(jnp.max(s_c, axis=-1, keepdims=True), jnp.max(s_p, axis=-1, keepdims=True)))
    e_c, e_p = jnp.exp(s_c - m), jnp.exp(s_p - m)
    l = jnp.sum(e_c, axis=-1, keepdims=True) + jnp.sum(e_p, axis=-1, keepdims=True)
    o = (_bdotb(e_c, vc) + _bdotb(e_p, vp)) / l
    return o, jnp.broadcast_to(m + jnp.log(l), o.shape)


def _mmb(a, b, cb):
    return lax.dot_general(a.astype(BF16), b.astype(BF16), (((2,), (cb,)), ((0,), (0,))), preferred_element_type=F32)


@functools.partial(jax.custom_vjp, nondiff_argnums=(2,))
def _bdotb1(a, b, cb):
    return _mmb(a, b, cb)


def _bdotb1_fwd(a, b, cb):
    return _mmb(a, b, cb), (a, b)


def _bdotb1_bwd(cb, saved, g):
    a, b = saved
    if cb == 1:
        return _mmb(g, b, 2), _mmb(jnp.swapaxes(a, 1, 2), g, 1)
    return _mmb(g, b, 1), _mmb(jnp.swapaxes(g, 1, 2), a, 1)


_bdotb1.defvjp(_bdotb1_fwd, _bdotb1_bwd)


def _bdotb(a, b, mode="nn", precision=None):
    if mode[0] == "t":
        a = jnp.swapaxes(a, 1, 2)
    cb = 2 if mode[1] == "t" else 1
    if precision is None:
        return _bdotb1(a, b, cb)
    return lax.dot_general(a, b, (((2,), (cb,)), ((0,), (0,))), precision=precision, preferred_element_type=F32)


def _tri_inv(a):
    t = a.shape[-1]
    row = lax.broadcasted_iota(jnp.int32, (1, t, t), 1)
    col = lax.broadcasted_iota(jnp.int32, (1, t, t), 2)
    x = jnp.where(row == col, 1.0, 0.0).astype(F32) + jnp.where(jnp.logical_and(row == col + 1, (row & 1) == 1), a, 0.0)
    sh = 1
    while (1 << sh) < t:
        m = jnp.logical_and((row >> sh) == (col >> sh) + 1, (row >> (sh + 1)) == (col >> (sh + 1)))
        x = x + _bdotb(_bdotb(x, jnp.where(m, a, 0.0), precision=lax.Precision.HIGH), x, precision=lax.Precision.HIGH)
        sh += 1
    return x


def _wkv_chunk(s0, r, lw, k, v, a, b):
    nh, t, _ = r.shape
    row = lax.broadcasted_iota(jnp.int32, (1, t, t), 1)
    col = lax.broadcasted_iota(jnp.int32, (1, t, t), 2)
    incl, strict = row >= col, row > col
    ones = jnp.broadcast_to(jnp.where(incl, 1.0, 0.0).astype(F32), (nh, t, t))
    cum = _bdotb(ones, lw, precision=HI)
    c_end = cum[:, t - 1:t, :]
    e_in, e_ex, e_inv = jnp.exp(cum), jnp.exp(cum - lw), jnp.exp(-cum)
    at, rt, bt, kt = a * e_ex, r * e_in, b * e_inv, k * e_inv
    a_ab = jnp.where(strict, _bdotb(at, bt, "nt"), 0.0)
    a_ak = jnp.where(strict, _bdotb(at, kt, "nt"), 0.0)
    u = _bdotb(_tri_inv(a_ab), _bdotb(at, s0, "nt") + _bdotb(a_ak, v))
    y = (_bdotb(rt, s0, "nt") + _bdotb(jnp.where(incl, _bdotb(rt, bt, "nt"), 0.0), u)
         + _bdotb(jnp.where(incl, _bdotb(rt, kt, "nt"), 0.0), v))
    w_end = jnp.exp(c_end - cum)
    s1 = s0 * jnp.exp(c_end) + _bdotb(u, b * w_end, "tn") + _bdotb(v, k * w_end, "tn")
    return y, s1


def _shift_fwd(z, mu):
    s, c = z.shape
    tc = 256

    def body(z_ref, mu_ref, o_ref):
        zz = z_ref[...]
        row = lax.broadcasted_iota(jnp.int32, zz.shape, 0)
        prev = jnp.where(row == 0, 0.0, pltpu.roll(zz, 1, 0))
        o_ref[...] = zz + (prev - zz) * mu_ref[...]

    return pl.pallas_call(
        body, name="shift_fwd", grid=(c // tc,),
        in_specs=[pl.BlockSpec((s, tc), lambda j: (0, j)), pl.BlockSpec((1, tc), lambda j: (0, j))],
        out_specs=pl.BlockSpec((s, tc), lambda j: (0, j)), out_shape=SDS((s, c), F32),
        compiler_params=pltpu.CompilerParams(dimension_semantics=("parallel",), vmem_limit_bytes=VMEM_LIMIT),
    )(z, mu)


def _shift_bwd(z, mu, dzs):
    s, c = z.shape
    tc = 256

    def body(z_ref, mu_ref, d_ref, dz_ref, dmu_ref):
        zz, d, m = z_ref[...], d_ref[...], mu_ref[...]
        row = lax.broadcasted_iota(jnp.int32, zz.shape, 0)
        prev = jnp.where(row == 0, 0.0, pltpu.roll(zz, 1, 0))
        t = d * m
        nxt = jnp.where(row == s - 1, 0.0, pltpu.roll(t, s - 1, 0))
        dz_ref[...] = (d - t + nxt).astype(dz_ref.dtype)
        dmu_ref[...] = jnp.sum(d * (prev - zz), axis=0, keepdims=True)

    return pl.pallas_call(
        body, name="shift_bwd", grid=(c // tc,),
        in_specs=[pl.BlockSpec((s, tc), lambda j: (0, j)), pl.BlockSpec((1, tc), lambda j: (0, j)),
                  pl.BlockSpec((s, tc), lambda j: (0, j))],
        out_specs=[pl.BlockSpec((s, tc), lambda j: (0, j)), pl.BlockSpec((1, tc), lambda j: (0, j))],
        out_shape=[SDS((s, c), BF16), SDS((1, c), F32)],
        compiler_params=pltpu.CompilerParams(dimension_semantics=("parallel",), vmem_limit_bytes=VMEM_LIMIT),
    )(z, mu, dzs)


def _heads(x, nh):
    return jnp.stack([x[:, h * HEAD:(h + 1) * HEAD] for h in range(nh)], axis=0)


def _unheads(x):
    return jnp.concatenate([x[h] for h in range(x.shape[0])], axis=1)


def _wkv_fwd(zs, lw, k2, na, b):
    s = lw.shape[0]
    t, hb = WKV_CHUNK, WKV_HEADS_PER_STEP
    w = hb * HEAD
    nc, ng = s // t, RWKV_HEADS // hb

    def body(r_ref, v_ref, lw_ref, k_ref, a_ref, b_ref, y_ref, s0_ref, state):
        @pl.when(pl.program_id(1) == 0)
        def _():
            state[...] = jnp.zeros_like(state)

        s0 = state[...]
        s0_ref[0] = s0
        y, s1 = _wkv_chunk(s0, *[_heads(t_ref[...], hb) for t_ref in (r_ref, lw_ref, k_ref, v_ref, a_ref, b_ref)])
        y_ref[...] = _unheads(y)
        state[...] = s1

    def col(off):
        return pl.BlockSpec((t, w), functools.partial(lambda g, i, off: (i, g + off), off=off))

    return pl.pallas_call(
        body, name="wkv_fwd", grid=(ng, nc),
        in_specs=[col(0), col(2 * ng), col(0), col(0), col(0), col(0)],
        out_specs=[col(0), pl.BlockSpec((1, hb, HEAD, HEAD), lambda g, i: (i, g, 0, 0))],
        out_shape=[SDS((s, RWKV_DIM), F32), SDS((nc, RWKV_HEADS, HEAD, HEAD), F32)],
        scratch_shapes=[pltpu.VMEM((hb, HEAD, HEAD), F32)],
        compiler_params=pltpu.CompilerParams(dimension_semantics=("parallel", "arbitrary"), vmem_limit_bytes=VMEM_LIMIT),
    )(zs, zs, lw, k2, na, b)


def _wkv_bwd(zs, lw, k2, na, b, s0s, dy):
    s = lw.shape[0]
    t, hb = WKV_CHUNK, WKV_HEADS_PER_STEP
    w = hb * HEAD
    nc, ng = s // t, RWKV_HEADS // hb

    def body(r_ref, v_ref, lw_ref, k_ref, a_ref, b_ref, s0_ref, dy_ref, dr_ref, dlw_ref, dk_ref, dv_ref, da_ref, db_ref, dstate):
        @pl.when(pl.program_id(1) == 0)
        def _():
            dstate[...] = jnp.zeros_like(dstate)

        _, vjp = jax.vjp(_wkv_chunk, s0_ref[0], *[_heads(t_ref[...], hb) for t_ref in (r_ref, lw_ref, k_ref, v_ref, a_ref, b_ref)])
        grads = vjp((_heads(dy_ref[...], hb), dstate[...]))
        dstate[...] = grads[0]
        for o_ref, gval in zip((dr_ref, dlw_ref, dk_ref, dv_ref, da_ref, db_ref), grads[1:]):
            o_ref[...] = _unheads(gval)

    def col(off):
        return pl.BlockSpec((t, w), functools.partial(lambda g, i, off: (nc - 1 - i, g + off), off=off))

    return pl.pallas_call(
        body, name="wkv_bwd", grid=(ng, nc),
        in_specs=[col(0), col(2 * ng), col(0), col(0), col(0), col(0),
                  pl.BlockSpec((1, hb, HEAD, HEAD), lambda g, i: (nc - 1 - i, g, 0, 0)), col(0)],
        out_specs=[col(0)] * 6,
        out_shape=[SDS((s, RWKV_DIM), F32)] * 6,
        scratch_shapes=[pltpu.VMEM((hb, HEAD, HEAD), F32)],
        compiler_params=pltpu.CompilerParams(dimension_semantics=("parallel", "arbitrary"), vmem_limit_bytes=VMEM_LIMIT),
    )(zs, zs, lw, k2, na, b, s0s, dy)


def _attn_fwd(q, k, v, d):
    s = q.shape[0]
    l = s // d
    nb = l // BAND
    assert nb * BAND == l
    qv, kv, vv = (t.reshape(l, d * GROUP_DIM) for t in (q, k, v))
    nh = GROUP_DIM // HEAD

    def body(q_ref, kp_ref, kc_ref, vp_ref, vc_ref, o_ref, l_ref):
        has_prev = pl.program_id(1) > 0
        o, lse = _attn_block(*[_heads(t_ref[...].astype(F32), nh) for t_ref in (q_ref, kp_ref, kc_ref, vp_ref, vc_ref)], has_prev)
        o_ref[...] = _unheads(o)
        l_ref[...] = _unheads(lse)

    cur = pl.BlockSpec((BAND, GROUP_DIM), lambda rho, i: (i, rho))
    prev = pl.BlockSpec((BAND, GROUP_DIM), lambda rho, i: (jnp.maximum(i - 1, 0), rho))
    o, lse = pl.pallas_call(
        body, name=f"attn_fwd_d{d}", grid=(d, nb),
        in_specs=[cur, prev, cur, prev, cur], out_specs=[cur, cur],
        out_shape=[SDS((l, d * GROUP_DIM), F32), SDS((l, d * GROUP_DIM), F32)],
        compiler_params=pltpu.CompilerParams(dimension_semantics=("parallel", "arbitrary"), vmem_limit_bytes=VMEM_LIMIT),
    )(qv, kv, kv, vv, vv)
    return o.reshape(s, GROUP_DIM), lse.reshape(s, GROUP_DIM)


def _attn_bwd(q, k, v, d, do, dlse):
    s = q.shape[0]
    l = s // d
    nb = l // BAND
    qv, kv, vv, dov, dlv = (t.reshape(l, d * GROUP_DIM) for t in (q, k, v, do, dlse))
    nh = GROUP_DIM // HEAD

    def body(q_ref, kp_ref, kc_ref, vp_ref, vc_ref, do_ref, dl_ref, dq_ref, dk_ref, dv_ref, ck, cv):
        step = pl.program_id(1)
        has_prev = step < nb - 1

        @pl.when(step == 0)
        def _():
            ck[...] = jnp.zeros_like(ck)
            cv[...] = jnp.zeros_like(cv)

        _, vjp = jax.vjp(functools.partial(_attn_block, has_prev=has_prev),
                         *[_heads(t_ref[...].astype(F32), nh) for t_ref in (q_ref, kp_ref, kc_ref, vp_ref, vc_ref)])
        dq, dkp, dkc, dvp, dvc = vjp((_heads(do_ref[...], nh), _heads(dl_ref[...], nh)))
        dq_ref[...] = _unheads(dq)
        dk_ref[...] = _unheads(dkc) + ck[...]
        dv_ref[...] = _unheads(dvc) + cv[...]
        ck[...] = _unheads(dkp)
        cv[...] = _unheads(dvp)

    cur = pl.BlockSpec((BAND, GROUP_DIM), lambda rho, i: (nb - 1 - i, rho))
    prev = pl.BlockSpec((BAND, GROUP_DIM), lambda rho, i: (jnp.maximum(nb - 2 - i, 0), rho))
    dq, dk, dv = pl.pallas_call(
        body, name=f"attn_bwd_d{d}", grid=(d, nb),
        in_specs=[cur, prev, cur, prev, cur, cur, cur], out_specs=[cur] * 3,
        out_shape=[SDS((l, d * GROUP_DIM), F32)] * 3,
        scratch_shapes=[pltpu.VMEM((BAND, GROUP_DIM), F32), pltpu.VMEM((BAND, GROUP_DIM), F32)],
        compiler_params=pltpu.CompilerParams(dimension_semantics=("parallel", "arbitrary"), vmem_limit_bytes=VMEM_LIMIT),
    )(qv, kv, kv, vv, vv, dov, dlv)
    return dq.reshape(s, GROUP_DIM), dk.reshape(s, GROUP_DIM), dv.reshape(s, GROUP_DIM)


def _coords():
    return lax.axis_index("x"), lax.axis_index("y"), lax.axis_index("c")


_CHIP_FLIPS = ((1, 0), (0, 1), (1, 1))


def _flip(v, f):
    return 1 - v if f else v


def _form(kind, r, c):
    return (N_CHIPS, r, c) if kind == "blk" else (r, N_CHIPS * c)


def _slot(ref, kind, j, rows, c):
    if kind == "blk":
        return ref.at[j] if rows is None else ref.at[j, rows]
    cols = pl.ds(pl.multiple_of(j * c, 128), c)
    return ref.at[:, cols] if rows is None else ref.at[rows, cols]


def _half(r, which, align):
    return pl.ds(pl.multiple_of(which * (r // 2), align), r // 2)


def _rcopy(src, dst, send_sems, recv_sems, kk, dev):
    return pltpu.make_async_remote_copy(src_ref=src, dst_ref=dst, send_sem=send_sems.at[kk], recv_sem=recv_sems.at[kk],
                                        device_id=dev, device_id_type=MESH)


def _gather_plan(specs, step):
    def copies(refs, ss, rs, received):
        x, y, c = _coords()
        out = []
        for w, (kind, r, cc) in enumerate(specs):
            mine, other = _half(r, c, 16), _half(r, 1 - c, 16)
            for kk, (fx, fy) in enumerate(_CHIP_FLIPS):
                px, py = _flip(x, fx), _flip(y, fy)
                if step == "ici":
                    sl = _slot(refs[w], kind, 2 * px + py if received else 2 * x + y, mine, cc)
                    dev = (px, py, c)
                else:
                    sl = _slot(refs[w], kind, 2 * px + py, other if received else mine, cc)
                    dev = (x, y, 1 - c)
                out.append(_rcopy(sl, sl, ss, rs, 3 * w + kk, dev))
        return out

    def issue(refs, ss, rs):
        return copies(refs, ss, rs, False)

    def expect(refs, ss, rs):
        return copies(refs, ss, rs, False), copies(refs, ss, rs, True)

    return issue, expect


def _gather_now(bufs, specs):
    nb = len(bufs)
    ici, d2d = _gather_plan(specs, "ici"), _gather_plan(specs, "d2d")

    def body(*refs):
        outs, (s1, r1, s2, r2) = refs[nb:2 * nb], refs[2 * nb:]
        sent, got = ici[1](outs, s1, r1)
        for cp in sent:
            cp.start()
        for cp in got:
            cp.wait_recv()
        sent2, got2 = d2d[1](outs, s2, r2)
        for cp in sent2:
            cp.start()
        for cp in got2:
            cp.wait_recv()
        for cp in sent + sent2:
            cp.wait_send()

    outs = pl.pallas_call(
        body, name="gather_ffn1",
        in_specs=[pl.BlockSpec(memory_space=pl.ANY)] * nb, out_specs=[pl.BlockSpec(memory_space=pl.ANY)] * nb,
        out_shape=[SDS(b.shape, b.dtype) for b in bufs], input_output_aliases={i: i for i in range(nb)},
        scratch_shapes=[pltpu.SemaphoreType.DMA((3 * nb,))] * 4,
    )(*bufs)
    return list(outs)


_HBM = pl.BlockSpec(memory_space=pltpu.HBM)
_SEM = pl.BlockSpec(memory_space=pltpu.SEMAPHORE)
_EFFECT = pltpu.SideEffectType.DATAFLOW_SIDE_EFFECTING


def _copies_start(name, bufs, n_sems, issue, after=None):
    nb = len(bufs)
    extra = [] if after is None else [after]

    def body(*refs):
        send_sems, recv_sems = refs[nb + len(extra)], refs[nb + len(extra) + 1]
        for cp in issue(refs[:nb], send_sems, recv_sems):
            cp.start()
        refs[-1][...] = jnp.zeros_like(refs[-1])

    outs = pl.pallas_call(
        body, name=name,
        out_shape=(pltpu.SemaphoreType.DMA((n_sems,)), pltpu.SemaphoreType.DMA((n_sems,)),
                   *[pltpu.HBM(b.shape, b.dtype) for b in bufs], SDS((8, 128), F32)),
        in_specs=[_HBM] * nb + [pl.BlockSpec(memory_space=pl.ANY)] * len(extra),
        out_specs=(_SEM, _SEM, *[_HBM] * nb, pl.BlockSpec(memory_space=pltpu.VMEM)),
        input_output_aliases={i: 2 + i for i in range(nb)},
        compiler_params=pltpu.CompilerParams(has_side_effects=_EFFECT),
    )(*[pltpu.with_memory_space_constraint(b, pltpu.HBM) for b in bufs], *extra)
    return outs[0], outs[1], list(outs[2:2 + nb]), outs[-1]


def _copies_wait(name, bufs, send_sems, recv_sems, after, expect):
    nb = len(bufs)

    def body(*refs):
        sent, received = expect(refs[:nb], refs[nb], refs[nb + 1])
        for cp in sent:
            cp.wait_send()
        for cp in received:
            cp.wait_recv()

    outs = pl.pallas_call(
        body, name=name,
        out_shape=tuple(pltpu.HBM(b.shape, b.dtype) for b in bufs),
        in_specs=(*[_HBM] * nb, _SEM, _SEM, pl.BlockSpec(memory_space=pl.ANY)), out_specs=tuple([_HBM] * nb),
        input_output_aliases={i: i for i in range(nb)},
        compiler_params=pltpu.CompilerParams(has_side_effects=_EFFECT),
    )(*bufs, send_sems, recv_sems, after)
    return list(outs)


def _add_pair(g, recv, kind, r, c, c_arr, name):
    h = r // 2
    if kind == "blk":
        tr = _row_tile(h, 512)
        grid = (N_CHIPS, h // tr)
        g_spec = pl.BlockSpec((1, 1, tr, c), lambda j, i, c_ref: (j, c_ref[0], i, 0))
        o_spec = pl.BlockSpec((1, tr, c), lambda j, i, c_ref: (j, i, 0))
        gv, oshape = g.reshape(N_CHIPS, 2, h, c), (N_CHIPS, h, c)
    else:
        tr = _row_tile(h, 64)
        grid = (h // tr,)
        g_spec = pl.BlockSpec((1, tr, N_CHIPS * c), lambda i, c_ref: (c_ref[0], i, 0))
        o_spec = pl.BlockSpec((tr, N_CHIPS * c), lambda i, c_ref: (i, 0))
        gv, oshape = g.reshape(2, h, N_CHIPS * c), (h, N_CHIPS * c)

    def body(c_ref, g_ref, r_ref, o_ref, ob_ref):
        v = (g_ref[:, 0] if kind == "blk" else g_ref[0]) + r_ref[...]
        o_ref[...] = v
        ob_ref[...] = v.astype(BF16)

    return pl.pallas_call(
        body, name=name,
        grid_spec=pltpu.PrefetchScalarGridSpec(num_scalar_prefetch=1, grid=grid, in_specs=[g_spec, o_spec], out_specs=[o_spec] * 2),
        out_shape=[SDS(oshape, F32), SDS(oshape, BF16)],
        compiler_params=pltpu.CompilerParams(vmem_limit_bytes=VMEM_LIMIT),
    )(c_arr, gv, recv)


def _add_chips(pair, recv, kind, r, c, me_arr, name):
    h = r // 2
    tr = _row_tile(h, 512)
    if kind == "blk":
        p_spec = pl.BlockSpec((1, tr, c), lambda i, me_ref: (me_ref[0], i, 0))
    else:
        p_spec = pl.BlockSpec((tr, c), lambda i, me_ref: (i, me_ref[0]))

    def body(me_ref, a_ref, r_ref, o_ref):
        own = a_ref[0] if kind == "blk" else a_ref[...]
        o_ref[...] = ((own + r_ref[0].astype(F32)) + r_ref[1].astype(F32)) + r_ref[2].astype(F32)

    return pl.pallas_call(
        body, name=name,
        grid_spec=pltpu.PrefetchScalarGridSpec(
            num_scalar_prefetch=1, grid=(h // tr,), in_specs=[p_spec, pl.BlockSpec((3, tr, c), lambda i, me_ref: (0, i, 0))],
            out_specs=pl.BlockSpec((tr, c), lambda i, me_ref: (i, 0))),
        out_shape=SDS((h, c), F32),
        compiler_params=pltpu.CompilerParams(vmem_limit_bytes=VMEM_LIMIT),
    )(me_arr, pair, recv)


class _GroupReduce:
    def __init__(self, tag, specs, c_arr, me_arr):
        self.tag, self.specs, self.c_arr, self.me_arr = tag, specs, c_arr, me_arr
        self.n = len(specs)

    def _plan(self, step):
        specs, n = self.specs, self.n

        def copies(refs, ss, rs, received):
            x, y, c = _coords()
            sib, out = (x, y, 1 - c), []
            for w, (_, kind, r, cc) in enumerate(specs):
                src, land = refs[w], refs[n + w]
                if step == "swap":
                    rows = _half(r, 1 - c, 8)
                    part = src.at[:, rows] if kind == "blk" else src.at[rows]
                    out.append(_rcopy(land if received else part, land, ss, rs, w, sib))
                elif step == "ici":
                    for kk, (fx, fy) in enumerate(_CHIP_FLIPS):
                        px, py = _flip(x, fx), _flip(y, fy)
                        part = land.at[kk] if received else _slot(src, kind, 2 * px + py, None, cc)
                        out.append(_rcopy(part, land.at[kk], ss, rs, 3 * w + kk, (px, py, c)))
                else:
                    there = land.at[_half(r, 1 - c if received else c, 8)]
                    out.append(_rcopy(there if received else src, there, ss, rs, w, sib))
            return out

        def issue(refs, ss, rs):
            return copies(refs, ss, rs, False)

        def expect(refs, ss, rs):
            return copies(refs, ss, rs, False), copies(refs, ss, rs, True)

        return issue, expect

    def swap_start(self, grads, after=None):
        lands = [lax.empty(_form(kind, r // 2, c), F32) for _, kind, r, c in self.specs]
        ss, rs, bufs, tok = _copies_start(f"rs_{self.tag}_swap", list(grads) + lands, self.n, self._plan("swap")[0], after=after)
        self.state = (ss, rs, bufs)
        return tok

    def swap_wait_ici_start(self, after):
        ss, rs, bufs = self.state
        bufs = _copies_wait(f"rs_{self.tag}_swap_wait", bufs, ss, rs, after, self._plan("swap")[1])
        pairs = [_add_pair(bufs[w], bufs[self.n + w], kind, r, c, self.c_arr, name=f"rs_{self.tag}_pair_{nm}")
                 for w, (nm, kind, r, c) in enumerate(self.specs)]
        self.pair = [pr[0] for pr in pairs]
        lands = [lax.empty((3, r // 2, c), BF16) for _, _, r, c in self.specs]
        ss, rs, bufs, tok = _copies_start(f"rs_{self.tag}_ici", [pr[1] for pr in pairs] + lands, 3 * self.n, self._plan("ici")[0])
        self.state = (ss, rs, bufs)
        return tok

    def ici_wait_join_start(self, after):
        ss, rs, bufs = self.state
        bufs = _copies_wait(f"rs_{self.tag}_ici_wait", bufs, ss, rs, after, self._plan("ici")[1])
        reds = [_add_chips(self.pair[w], bufs[self.n + w], kind, r, c, self.me_arr, name=f"rs_{self.tag}_chips_{nm}")
                for w, (nm, kind, r, c) in enumerate(self.specs)]
        lands = [lax.empty((r, c), F32) for _, _, r, c in self.specs]
        ss, rs, bufs, tok = _copies_start(f"rs_{self.tag}_join", reds + lands, self.n, self._plan("join")[0])
        self.state = (ss, rs, bufs)
        return tok

    def join_wait(self, after):
        ss, rs, bufs = self.state
        bufs = _copies_wait(f"rs_{self.tag}_join_wait", bufs, ss, rs, after, self._plan("join")[1])
        ci = lax.axis_index("c")
        return {nm: lax.dynamic_update_slice(bufs[self.n + w], bufs[w], (ci * (r // 2), 0))
                for w, (nm, _, r, c) in enumerate(self.specs)}


def _all_reduce_small(buf):
    rows, cols = buf.shape

    def body(x_ref, o_ref, gath, send_sems, recv_sems):
        x, y, c = _coords()
        me = 4 * x + 2 * y + c
        gath[me] = x_ref[...]
        sends = []
        for kk in range(1, 8):
            f = (kk >> 2) & 1, (kk >> 1) & 1, kk & 1
            px, py, pc = _flip(x, f[0]), _flip(y, f[1]), _flip(c, f[2])
            cp = pltpu.make_async_remote_copy(src_ref=x_ref, dst_ref=gath.at[me], send_sem=send_sems.at[kk - 1],
                                              recv_sem=recv_sems.at[kk - 1], device_id=(px, py, pc), device_id_type=MESH)
            cp.start()
            sends.append(cp)
        for kk in range(1, 8):
            f = (kk >> 2) & 1, (kk >> 1) & 1, kk & 1
            px, py, pc = _flip(x, f[0]), _flip(y, f[1]), _flip(c, f[2])
            there = gath.at[4 * px + 2 * py + pc]
            pltpu.make_async_remote_copy(src_ref=there, dst_ref=there, send_sem=send_sems.at[kk - 1],
                                         recv_sem=recv_sems.at[kk - 1], device_id=(px, py, pc), device_id_type=MESH).wait_recv()
        for cp in sends:
            cp.wait_send()
        acc = gath[0]
        for j in range(1, 8):
            acc = acc + gath[j]
        o_ref[...] = acc

    return pl.pallas_call(
        body, name="all_reduce_small",
        in_specs=[pl.BlockSpec(memory_space=pltpu.VMEM)], out_specs=pl.BlockSpec(memory_space=pltpu.VMEM),
        out_shape=SDS((rows, cols), F32),
        scratch_shapes=[pltpu.VMEM((8, rows, cols), F32), pltpu.SemaphoreType.DMA((7,)), pltpu.SemaphoreType.DMA((7,))],
    )(buf)


def _adamw_rows(w, g, m, v):
    m = ADAM_B1 * m + (1.0 - ADAM_B1) * g
    v = ADAM_B2 * v + (1.0 - ADAM_B2) * jnp.square(g)
    m_hat = m / (1.0 - ADAM_B1 ** ADAM_STEP)
    v_hat = v / (1.0 - ADAM_B2 ** ADAM_STEP)
    return -ADAM_LR * (m_hat / (jnp.sqrt(v_hat) + ADAM_EPS) + ADAM_WD * w), m, v


def _adamw(w, g, m, v, name, dep=None):
    rows, cols = w.shape
    tm = _pick(rows, (256, 128, 64, 16, 8))
    return _rows_call(_adamw_rows, [(t, 0, cols) for t in (w, g, m, v)], [], [(cols, F32)] * 3, tm=tm, name=name, dep=dep)


def _pack_small(parts):
    flat = jnp.concatenate([parts[n].reshape(-1) for n, _ in SMALL])
    return jnp.pad(flat, (0, SMALL_ROWS * PACK_COLS - flat.shape[0])).reshape(SMALL_ROWS, PACK_COLS)


def _unpack_small(buf, shapes):
    flat, out, off = buf.reshape(-1), {}, 0
    for n, sz in SMALL:
        out[n] = flat[off:off + sz].reshape(shapes[n])
        off += sz
    return out


def _lora_stack(parts):
    return jnp.concatenate([parts[n] for n, _ in LORA], axis=-2)


def _lora_split(stacked):
    out, off = {}, 0
    for n, rows in LORA:
        out[n] = stacked[..., off:off + rows, :]
        off += rows
    return out


def _ffn_gate_up(h, wgt, wut, name):
    s, d = h.shape
    nblk, f, _ = wgt.shape
    tm = _pick(s, (1024, 512, 256))
    dn = (((1,), (1,)), ((), ()))

    def body(h_ref, wg_ref, wu_ref, g_ref, u_ref, a_ref):
        hh = h_ref[...]
        g = lax.dot_general(hh, wg_ref[0], dn, preferred_element_type=F32)
        u = lax.dot_general(hh, wu_ref[0], dn, preferred_element_type=F32)
        g_ref[0], u_ref[0] = g, u
        a_ref[0] = _swiglu_act(g, u).astype(BF16)

    w_spec = pl.BlockSpec((1, f, d), lambda j, i: (j, 0, 0))
    o_spec = pl.BlockSpec((1, tm, f), lambda j, i: (j, i, 0))
    return pl.pallas_call(
        body, name=name, grid=(nblk, s // tm),
        in_specs=[pl.BlockSpec((tm, d), lambda j, i: (i, 0)), w_spec, w_spec], out_specs=[o_spec] * 3,
        out_shape=[SDS((nblk, s, f), F32), SDS((nblk, s, f), F32), SDS((nblk, s, f), BF16)],
        compiler_params=pltpu.CompilerParams(dimension_semantics=("parallel", "parallel"), vmem_limit_bytes=VMEM_LIMIT),
    )(h, wgt, wut)


def _ffn_down_dx(dx_bf, wd, gate, up, name, dep=None):
    s, d = dx_bf.shape
    nblk, f, _ = wd.shape
    tm = _pick(s, (1024, 512, 256))
    dn = (((1,), (1,)), ((), ()))

    def body(dx_ref, wd_ref, g_ref, u_ref, *rest):
        dg_ref, du_ref = rest[-2:]
        dact = 0.5 * lax.dot_general(dx_ref[...], wd_ref[0], dn, preferred_element_type=F32)
        _, vjp = jax.vjp(_swiglu_act, g_ref[0], u_ref[0])
        dg, du = vjp(dact)
        dg_ref[0], du_ref[0] = dg.astype(BF16), du.astype(BF16)

    o_spec = pl.BlockSpec((1, tm, f), lambda j, i: (j, i, 0))
    extra = [] if dep is None else [dep]
    return pl.pallas_call(
        body, name=name, grid=(nblk, s // tm),
        in_specs=[pl.BlockSpec((tm, d), lambda j, i: (i, 0)), pl.BlockSpec((1, f, d), lambda j, i: (j, 0, 0)), o_spec, o_spec]
        + [pl.BlockSpec(memory_space=pl.ANY)] * len(extra),
        out_specs=[o_spec] * 2, out_shape=[SDS((nblk, s, f), BF16)] * 2,
        compiler_params=pltpu.CompilerParams(dimension_semantics=("parallel", "parallel"), vmem_limit_bytes=VMEM_LIMIT),
    )(dx_bf, wd, gate, up, *extra)


def _ffn_fwd(x, gain, wgt, wut, wd, tag):
    h = _rows_call(_rms, [(x, 0, D_MODEL)], [gain], [(D_MODEL, BF16)], tm=256, name=f"{tag}_norm")[0]
    gate, up, act = _ffn_gate_up(h, wgt, wut, f"{tag}_gate_up")
    x_new = _mm(act, wd, sum_blocks=True, res=x, alpha=0.5, name=f"{tag}_down")
    return x_new, (x, h, gate, up, act)


def _ffn_bwd(dx_new, dx_new_bf, saved, gain, wgt, wut, wd, tag, dep=None, hooks=None):
    x, h, gate, up, act = saved
    hooks = hooks or {}
    d_wd = _mm(act, dx_new_bf, ta=True, alpha=0.5, name=f"{tag}_down_dw")
    if "down" in hooks:
        dep = hooks["down"](d_wd)
    dgate, dup = _ffn_down_dx(dx_new_bf, wd, gate, up, f"{tag}_down_dx", dep=dep)
    dep = hooks["mid"](dgate) if "mid" in hooks else None
    dh = _mm(dgate, wgt, sum_blocks=True, dep=dep, name=f"{tag}_gate_dx")
    dh = _mm(dup, wut, sum_blocks=True, res=dh, name=f"{tag}_up_dx")
    d_wgt = _mm(dgate, h, ta=True, name=f"{tag}_gate_dw")
    d_wut = _mm(dup, h, ta=True, name=f"{tag}_up_dw")
    dep = hooks["end"](dh) if "end" in hooks else None
    dx, dx_bf, dgain = _norm_bwd(x, gain, dh, dx_new, f"{tag}_norm_bwd", dep=dep)
    return dx, dx_bf, dgain, d_wgt, d_wut, d_wd


def _norm_bwd(x, gain, dh, dres, name, dep=None):
    def f(xt, dht, drt, gt):
        _, vjp = jax.vjp(_rms, xt, gt)
        dxt, dgt = vjp(dht)
        return dxt + drt, dxt + drt, dgt

    return _rows_call(f, [(x, 0, D_MODEL), (dh, 0, D_MODEL), (dres, 0, D_MODEL)], [gain], [(D_MODEL, F32), (D_MODEL, BF16)],
                      [(1, D_MODEL)], tm=256, name=name, dep=dep)


def kernel(x, p, positions, ffn1_norm, ffn1_w_gate, ffn1_w_up, ffn1_w_down, mix_norm, w_in, rwkv_mu, rwkv_w0, rwkv_w2, rwkv_a0, rwkv_a2, rwkv_g2, rwkv_k_k, rwkv_k_a, rwkv_r_k, rwkv_gn_w, rwkv_gn_b, q_norm, k_norm, w_br_rwkv, w_br_attn, w_out, ffn2_norm, ffn2_w_gate, ffn2_w_up, ffn2_w_down, ple_norm, ple_w_gate, ple_w_proj, loss_target, m_ffn1_norm, m_ffn1_w_gate, m_ffn1_w_up, m_ffn1_w_down, m_mix_norm, m_w_in, m_rwkv_mu, m_rwkv_w0, m_rwkv_w2, m_rwkv_a0, m_rwkv_a2, m_rwkv_g2, m_rwkv_k_k, m_rwkv_k_a, m_rwkv_r_k, m_rwkv_gn_w, m_rwkv_gn_b, m_q_norm, m_k_norm, m_w_br_rwkv, m_w_br_attn, m_w_out, m_ffn2_norm, m_ffn2_w_gate, m_ffn2_w_up, m_ffn2_w_down, m_ple_norm, m_ple_w_gate, m_ple_w_proj, v_ffn1_norm, v_ffn1_w_gate, v_ffn1_w_up, v_ffn1_w_down, v_mix_norm, v_w_in, v_rwkv_mu, v_rwkv_w0, v_rwkv_w2, v_rwkv_a0, v_rwkv_a2, v_rwkv_g2, v_rwkv_k_k, v_rwkv_k_a, v_rwkv_r_k, v_rwkv_gn_w, v_rwkv_gn_b, v_q_norm, v_k_norm, v_w_br_rwkv, v_w_br_attn, v_w_out, v_ffn2_norm, v_ffn2_w_gate, v_ffn2_w_up, v_ffn2_w_down, v_ple_norm, v_ple_w_gate, v_ple_w_proj):
    args = dict(locals())
    wts = {n: args[n] for n in WEIGHTS}
    mom_m = {n: args["m_" + n] for n in WEIGHTS}
    mom_v = {n: args["v_" + n] for n in WEIGHTS}
    x0, tgt = x[0], loss_target[0]
    s = x0.shape[0]
    p_tok = p[0, 0]

    vec = {n: wts[n].reshape(1, -1) for n, _ in SMALL}
    xi, yi, ci = _coords()
    me = 2 * xi + yi
    def laid(t, n):
        return jnp.transpose(t[n][0]) if n in TRANSPOSED else t[n][0]

    shard_of = {n: laid(wts, n) for g in GROUPS.values() for n, _, _, _ in g if n != "lora"}
    shard_of["lora"] = _lora_stack({n: wts[n][0] for n, _ in LORA})

    def whole_with_own(n, kind, r, c):
        at = (me, 0, 0) if kind == "blk" else (0, me * c)
        own = shard_of[n].astype(BF16)
        return lax.dynamic_update_slice(lax.empty(_form(kind, r, c), BF16), own[None] if kind == "blk" else own, at)

    specs = {g: [(kind, r, c) for _, kind, r, c in grp] for g, grp in GROUPS.items()}
    bufs = {g: [whole_with_own(*w) for w in grp] for g, grp in GROUPS.items()}
    plans = {(g, st): _gather_plan(specs[g], st) for g in ("mx", "f2") for st in ("ici", "d2d")}
    wb = dict(zip([w[0] for w in GROUPS["f1"]], _gather_now(bufs["f1"], specs["f1"])))
    ss_a, rs_a, buf_mx, tok_a = _copies_start("gather_mx_ici", bufs["mx"], 3 * len(bufs["mx"]), plans["mx", "ici"][0],
                                              after=wb["ffn1_w_gate"])

    inv_freq = 1.0 / (ROPE_THETA ** (jnp.arange(0, HEAD, 2, dtype=F32) / HEAD))
    ang = positions[0].astype(F32)[:, None] * inv_freq
    cos, sin = jnp.cos(ang), jnp.sin(ang)
    cos2, sin2 = jnp.concatenate([cos, cos], axis=1), jnp.concatenate([-sin, sin], axis=1)

    x1, ffn1_saved = _ffn_fwd(x0, vec["ffn1_norm"] + tok_a[0, 0], wb["ffn1_w_gate"], wb["ffn1_w_up"], wb["ffn1_w_down"], "ffn1")
    buf_mx = _copies_wait("gather_mx_ici_wait", buf_mx, ss_a, rs_a, x1, plans["mx", "ici"][1])
    ss_b, rs_b, buf_mx, tok_b = _copies_start("gather_mx_d2d", buf_mx, 3 * len(buf_mx), plans["mx", "d2d"][0])
    ss_c, rs_c, buf_f2, tok_c = _copies_start("gather_f2_ici", bufs["f2"], 3 * len(bufs["f2"]), plans["f2", "ici"][0])
    h = _rows_call(_rms, [(x1, 0, D_MODEL)], [vec["mix_norm"] + (tok_b[0, 0] + tok_c[0, 0])], [(D_MODEL, BF16)], tm=256,
                   name="mix_norm")[0]
    buf_mx = _copies_wait("gather_mx_d2d_wait", buf_mx, ss_b, rs_b, h, plans["mx", "d2d"][1])
    wb.update(zip([w[0] for w in GROUPS["mx"]], buf_mx))
    w_in_all = wb["w_in"]
    w_in_r, w_in_a, w_in_g = w_in_all[:, :RWKV_COLS], w_in_all[:, RWKV_COLS:RWKV_COLS + ATTN_COLS], w_in_all[:, RWKV_COLS + ATTN_COLS:]
    lora = _lora_split(wb["lora"])
    w2, a2, g2 = lora["rwkv_w2"], lora["rwkv_a2"], lora["rwkv_g2"]
    w_brr, w_bra = wb["w_br_rwkv"], wb["w_br_attn"]
    w_o = wb["w_out"].reshape(D_MODEL, D_MODEL)
    z_r = _mm(h, w_in_r, name="in_rwkv")
    z_a = _mm(h, w_in_a, name="in_attn")
    z_g = _mm(h, w_in_g, name="in_gate")

    zs = _shift_fwd(z_r, vec["rwkv_mu"])
    pre_params = [vec["rwkv_w0"], w2, vec["rwkv_a0"], a2, g2, vec["rwkv_k_k"], vec["rwkv_k_a"]]
    def pre_fwd(*t):
        res = _rwkv_pre(*t)
        return res[1], res[2], res[4], res[5], res[6]

    lw, k2, na, kb, gate_r = _rows_call(pre_fwd, [(zs, 0, RWKV_COLS)], pre_params, [(RWKV_DIM, F32)] * 5, tm=256, name="rwkv_pre")
    y_scan, s0s = _wkv_fwd(zs, lw, k2, na, kb)
    buf_f2 = _copies_wait("gather_f2_ici_wait", buf_f2, ss_c, rs_c, y_scan, plans["f2", "ici"][1])
    ss_d, rs_d, buf_f2, tok_d = _copies_start("gather_f2_d2d", buf_f2, 3 * len(buf_f2), plans["f2", "d2d"][0])
    post_params = [vec["rwkv_gn_w"] + tok_d[0, 0], vec["rwkv_gn_b"], vec["rwkv_r_k"]]
    post_rows = [(y_scan, 0, RWKV_DIM), (zs, 0, RWKV_DIM), (k2, 0, RWKV_DIM), (zs, 2, RWKV_DIM), (gate_r, 0, RWKV_DIM)]
    y_rwkv = _rows_call(_rwkv_post, post_rows, post_params, [(RWKV_DIM, BF16)], tm=256, name="rwkv_post")[0]

    def qk_fwd(qt, kt, ct, st, qg, kg):
        return _norm_rope(qt, qg, ct, st), _norm_rope(kt, kg, ct, st)

    qk_rows = [(z_a, 0, ATTN_DIM), (z_a, 1, ATTN_DIM), (cos2, 0, HEAD), (sin2, 0, HEAD)]
    q_rot, k_rot = _rows_call(qk_fwd, qk_rows, [vec["q_norm"], vec["k_norm"]], [(ATTN_DIM, BF16)] * 2, tm=256, name="attn_pre")
    def group(t, g, off=0):
        return t[:, off + g * GROUP_DIM:off + (g + 1) * GROUP_DIM].astype(BF16)

    qkv = [(group(q_rot, g), group(k_rot, g), group(z_a, g, 2 * ATTN_DIM)) for g in range(len(ATTN_DILATIONS))]
    outs, lses = zip(*[_attn_fwd(*qkv[g], d) for g, d in enumerate(ATTN_DILATIONS)])
    comb_rows = [(t, 0, GROUP_DIM) for t in outs + lses]
    y_attn = _rows_call(_attn_combine, comb_rows, [], [(GROUP_DIM, BF16)], tm=256, name="attn_combine")[0]

    br = _mm(y_rwkv, w_brr, name="branch_rwkv")
    ba = _mm(y_attn, w_bra, name="branch_attn")
    merge_rows = [(z_g, 0, D_MODEL), (z_g, 1, D_MODEL), (br, 0, D_MODEL), (ba, 0, D_MODEL)]
    merged = _rows_call(_merge, merge_rows, [], [(D_MODEL, BF16)], tm=256, name="merge")[0]
    x2 = _mm(merged, w_o, res=x1, name="out_proj")
    buf_f2 = _copies_wait("gather_f2_d2d_wait", buf_f2, ss_d, rs_d, x2, plans["f2", "d2d"][1])
    wb.update(zip([w[0] for w in GROUPS["f2"]], buf_f2))
    w_pp, w_pg = wb["ple_w_proj"], wb["ple_w_gate"].reshape(D_MODEL, D_MODEL)
    x3, ffn2_saved = _ffn_fwd(x2, vec["ffn2_norm"], wb["ffn2_w_gate"], wb["ffn2_w_up"], wb["ffn2_w_down"], "ffn2")
    hp = _rows_call(_rms, [(x3, 0, D_MODEL)], [vec["ple_norm"]], [(D_MODEL, BF16)], tm=256, name="ple_norm")[0]
    pg = _mm(hp, w_pg, name="ple_gate")
    pp = _mm(p_tok, w_pp, name="ple_proj")

    def head(x3t, pgt, ppt, tt):
        sg = _sigmoid(pgt)
        err = x3t + sg * ppt - tt
        dx4 = err * (1.0 / D_MODEL)
        loss = 0.5 * jnp.sum(jnp.mean(err * err, axis=-1, keepdims=True), axis=0, keepdims=True)
        return dx4, dx4 * ppt * sg * (1.0 - sg), dx4 * sg, jnp.broadcast_to(loss, (8, 128))

    head_rows = [(x3, 0, D_MODEL), (pg, 0, D_MODEL), (pp, 0, D_MODEL), (tgt, 0, D_MODEL)]
    dx4, dpg, dpp, loss_tile = _rows_call(head, head_rows, [], [(D_MODEL, F32), (D_MODEL, BF16), (D_MODEL, BF16)], [(8, 128)],
                                          tm=256, name="ple_loss")

    c_arr = jnp.reshape(ci, (1,)).astype(jnp.int32)
    me_arr = jnp.reshape(me, (1,)).astype(jnp.int32)
    red_groups = {"f2": GROUPS["f2"], "mx": GROUPS["mx"], "f1d": GROUPS["f1"][2:], "f1g": GROUPS["f1"][:2]}
    red = {g: _GroupReduce(g, grp, c_arr, me_arr) for g, grp in red_groups.items()}
    gw, gs = {}, {}
    gw["ple_w_proj"] = _mm(p_tok, dpp, ta=True, name="ple_proj_dw")
    gw["ple_w_gate"] = _mm(hp, dpg, ta=True, name="ple_gate_dw")
    dhp = _mm(dpg, w_pg, tb=True, name="ple_gate_dx")
    dx3, dx3_bf, gs["ple_norm"] = _norm_bwd(x3, vec["ple_norm"], dhp, dx4, "ple_norm_bwd")
    dx2, dx2_bf, gs["ffn2_norm"], gw["ffn2_w_gate"], gw["ffn2_w_up"], gw["ffn2_w_down"] = _ffn_bwd(
        dx3, dx3_bf, ffn2_saved, vec["ffn2_norm"], wb["ffn2_w_gate"], wb["ffn2_w_up"], wb["ffn2_w_down"], "ffn2")
    gw["ple_w_gate"] = gw["ple_w_gate"].reshape(N_CHIPS, D_MODEL // N_CHIPS, D_MODEL)
    tok = red["f2"].swap_start([gw[w[0]] for w in GROUPS["f2"]])
    gw["w_out"] = _mm(merged, dx2_bf, ta=True, name="out_proj_dw")
    dmerged = _mm(dx2_bf, w_o, tb=True, dep=tok, name="out_proj_dx")

    def merge_bwd(zgr, zga, brt, bat, ct):
        _, vjp = jax.vjp(_merge, zgr, zga, brt, bat)
        d1, d2, d3, d4 = vjp(ct)
        return jnp.concatenate([d1, d2], axis=1), d3, d4

    dz_g, dbr, dba = _rows_call(merge_bwd, merge_rows + [(dmerged, 0, D_MODEL)], [],
                                [(2 * D_MODEL, BF16), (D_MODEL, BF16), (D_MODEL, BF16)], tm=256, name="merge_bwd")
    tok = red["f2"].swap_wait_ici_start(dz_g)
    gw["w_br_rwkv"] = _mm(y_rwkv, dbr, ta=True, name="branch_rwkv_dw")
    gw["w_br_attn"] = _mm(y_attn, dba, ta=True, name="branch_attn_dw")
    dy_rwkv = _mm(dbr, w_brr, tb=True, dep=tok, name="branch_rwkv_dx")
    dy_attn = _mm(dba, w_bra, tb=True, dep=tok, name="branch_attn_dx")

    def comb_bwd(*t):
        _, vjp = jax.vjp(_attn_combine, *t[:6])
        return vjp(t[6])

    dcomb = _rows_call(comb_bwd, comb_rows + [(dy_attn, 0, GROUP_DIM)], [], [(GROUP_DIM, F32)] * 6, tm=256, name="attn_combine_bwd")
    dqs, dks, dvs = zip(*[_attn_bwd(*qkv[g], d, dcomb[g], dcomb[3 + g]) for g, d in enumerate(ATTN_DILATIONS)])

    def qk_bwd(qt, kt, ct, st, *rest):
        dq = jnp.concatenate(rest[0:3], axis=1)
        dk = jnp.concatenate(rest[3:6], axis=1)
        qg, kg = rest[9], rest[10]
        _, vjp = jax.vjp(lambda a_, b_, c_, d_: qk_fwd(a_, b_, ct, st, c_, d_), qt, kt, qg, kg)
        dqt, dkt, dqg, dkg = vjp((dq, dk))
        return jnp.concatenate((dqt, dkt) + tuple(rest[6:9]), axis=1), dqg, dkg

    dz_a, gs["q_norm"], gs["k_norm"] = _rows_call(
        qk_bwd, qk_rows + [(t, 0, GROUP_DIM) for t in dqs + dks + dvs], [vec["q_norm"], vec["k_norm"]],
        [(ATTN_COLS, BF16)], [(1, HEAD), (1, HEAD)], tm=256, name="attn_pre_bwd")
    tok = red["f2"].ici_wait_join_start(dz_a)

    def post_bwd(*t):
        _, vjp = jax.vjp(_rwkv_post, *t[:5], *t[6:])
        return vjp(t[5])

    dy_scan, dr_post, dk2_post, dv_post, dgate_r, gs["rwkv_gn_w"], gs["rwkv_gn_b"], gs["rwkv_r_k"] = _rows_call(
        post_bwd, post_rows + [(dy_rwkv, 0, RWKV_DIM)], post_params, [(RWKV_DIM, F32)] * 5, [(1, RWKV_DIM)] * 3,
        tm=256, name="rwkv_post_bwd", dep=tok)
    grad_big = red["f2"].join_wait(dy_scan)
    dr_s, dlw, dk2_s, dv_s, dna, dkb = _wkv_bwd(zs, lw, k2, na, kb, s0s, dy_scan)

    def pre_bwd(zt, c_r1, c_r2, c_lw, c_k1, c_k2, c_v1, c_v2, c_a, c_b, c_g, *params):
        _, vjp = jax.vjp(_rwkv_pre, zt, *params)
        return vjp((c_r1 + c_r2, c_lw, c_k1 + c_k2, c_v1 + c_v2, c_a, c_b, c_g))

    pre_cts = [dr_s, dr_post, dlw, dk2_s, dk2_post, dv_s, dv_post, dna, dkb, dgate_r]
    dzs, gs["rwkv_w0"], g_w2, gs["rwkv_a0"], g_a2, g_g2, gs["rwkv_k_k"], gs["rwkv_k_a"] = _rows_call(
        pre_bwd, [(zs, 0, RWKV_COLS)] + [(t, 0, RWKV_DIM) for t in pre_cts], pre_params, [(RWKV_COLS, F32)],
        [q.shape for q in pre_params], tm=256, name="rwkv_pre_bwd")
    dz_r, gs["rwkv_mu"] = _shift_bwd(z_r, vec["rwkv_mu"], dzs)

    g_w_in = jnp.concatenate([_mm(h, dz_r, ta=True, name="in_rwkv_dw"), _mm(h, dz_a, ta=True, name="in_attn_dw"),
                              _mm(h, dz_g, ta=True, name="in_gate_dw")], axis=1)
    gw["w_in"], gw["lora"] = g_w_in, jnp.concatenate([g_w2, g_a2, g_g2], axis=0)
    gw["w_out"] = gw["w_out"].reshape(N_CHIPS, D_MODEL // N_CHIPS, D_MODEL)
    tok = red["mx"].swap_start([gw[w[0]] for w in GROUPS["mx"]])
    dh = _mm(dz_r, w_in_r, tb=True, dep=tok, name="in_rwkv_dx")
    dh = _mm(dz_a, w_in_a, tb=True, res=dh, name="in_attn_dx")
    dh = _mm(dz_g, w_in_g, tb=True, res=dh, name="in_gate_dx")
    dx1, dx1_bf, gs["mix_norm"] = _norm_bwd(x1, vec["mix_norm"], dh, dx2, "mix_norm_bwd")
    tok = red["mx"].swap_wait_ici_start(dx1_bf)
    hooks = {"down": lambda d_wd: red["f1d"].swap_start([d_wd], after=tok),
             "mid": lambda dgate: red["f1d"].swap_wait_ici_start(dgate),
             "end": lambda dh_: red["f1d"].ici_wait_join_start(dh_)}
    dx0, _, gs["ffn1_norm"], gw["ffn1_w_gate"], gw["ffn1_w_up"], gw["ffn1_w_down"] = _ffn_bwd(
        dx1, dx1_bf, ffn1_saved, vec["ffn1_norm"], wb["ffn1_w_gate"], wb["ffn1_w_up"], wb["ffn1_w_down"], "ffn1", hooks=hooks)
    tok_mx = red["mx"].ici_wait_join_start(dx0)
    tok = red["f1g"].swap_start([gw["ffn1_w_gate"], gw["ffn1_w_up"]], after=tok_mx)

    grads, deltas, new_m, new_v = {}, {}, {}, {}

    def update(names, dep):
        last = None
        for n in names:
            g2d = grad_big[n]
            if n == "lora":
                w_, m_, v_ = (_lora_stack({k: t[k][0] for k, _ in LORA}) for t in (wts, mom_m, mom_v))
            else:
                w_, m_, v_ = laid(wts, n), laid(mom_m, n), laid(mom_v, n)
            res = (g2d,) + tuple(_adamw(w_, g2d, m_, v_, name=f"adamw_{n}", dep=dep))
            dep = last = res[1]
            for store, val in zip((grads, deltas, new_m, new_v), res):
                if n == "lora":
                    store.update({k: t[None] for k, t in _lora_split(val).items()})
                else:
                    store[n] = (jnp.transpose(val) if n in TRANSPOSED else val)[None]
        return last

    last = update([w[0] for w in GROUPS["f2"]], tok)
    tok = red["f1g"].swap_wait_ici_start(last)
    grad_big.update(red["mx"].join_wait(last))
    grad_big.update(red["f1d"].join_wait(last))
    last = update([w[0] for w in GROUPS["mx"]] + ["ffn1_w_down"], tok)

    flat = jnp.concatenate([gs[n].reshape(-1) for n, _ in SMALL] + [loss_tile[0, 0:1]])
    small_buf = jnp.pad(flat, (0, SMALL_ROWS * PACK_COLS - flat.shape[0])).reshape(SMALL_ROWS, PACK_COLS)
    small_sum = _all_reduce_small(small_buf)
    n_small = sum(sz for _, sz in SMALL)
    loss = small_sum.reshape(-1)[n_small]
    grad_small = _unpack_small(small_sum, {n: wts[n].shape for n, _ in SMALL})
    d_s, m_s, v_s = _adamw(_pack_small(wts), small_sum, _pack_small(mom_m), _pack_small(mom_v), name="adamw_small", dep=last)
    shapes = {n: wts[n].shape for n, _ in SMALL}
    d_s, m_s, v_s = _unpack_small(d_s, shapes), _unpack_small(m_s, shapes), _unpack_small(v_s, shapes)
    for n, _ in SMALL:
        grads[n], deltas[n], new_m[n], new_v[n] = grad_small[n], d_s[n], m_s[n], v_s[n]

    tok = red["f1g"].ici_wait_join_start(m_s["ffn1_norm"])
    grad_big.update(red["f1g"].join_wait(tok))
    update(["ffn1_w_gate", "ffn1_w_up"], None)

    return (loss, dx0[None], *[grads[n] for n in WEIGHTS], *[deltas[n] for n in WEIGHTS],
            *[new_m[n] for n in WEIGHTS], *[new_v[n] for n in WEIGHTS])
```

```python
import functools

import jax
import jax.numpy as jnp
from jax import lax
from jax.experimental import pallas as pl
from jax.experimental.pallas import tpu as pltpu

F32, BF16 = jnp.float32, jnp.bfloat16
HI = lax.Precision.HIGHEST
MESH = pl.DeviceIdType.MESH
SDS = jax.ShapeDtypeStruct

D_MODEL = 1024
HEAD = 64
RWKV_HEADS = 8
RWKV_DIM = RWKV_HEADS * HEAD
DECAY_LORA, ICLR_LORA, GATE_LORA = 64, 64, 128
GN_EPS = 64e-5
RMS_EPS = 1e-6
ATTN_DILATIONS = (1, 4, 16)
BAND = 128
ATTN_DIM = 768
GROUP_DIM = 256
ROPE_THETA = 10000.0
NEG_INF = -1e30
RWKV_COLS = 3 * RWKV_DIM + DECAY_LORA + ICLR_LORA + GATE_LORA
ATTN_COLS = 3 * ATTN_DIM
ADAM_LR, ADAM_B1, ADAM_B2, ADAM_EPS, ADAM_WD, ADAM_STEP = 0.001, 0.9, 0.999, 1e-08, 0.01, 10

WKV_CHUNK = 64
WKV_HEADS_PER_STEP = 8
N_CHIPS = 4
PACK_COLS = 1024
VMEM_LIMIT = 48 * 1024 * 1024

TRANSPOSED = ("ffn1_w_gate", "ffn1_w_up", "ffn2_w_gate", "ffn2_w_up")
LORA = (("rwkv_w2", 64), ("rwkv_a2", 64), ("rwkv_g2", 128))
GROUPS = {
    "f1": (("ffn1_w_gate", "blk", 704, 1024), ("ffn1_w_up", "blk", 704, 1024), ("ffn1_w_down", "blk", 704, 1024)),
    "mx": (("w_in", "col", 1024, 1536), ("lora", "col", 256, 128), ("w_br_rwkv", "col", 512, 256),
           ("w_br_attn", "col", 256, 256), ("w_out", "blk", 256, 1024)),
    "f2": (("ffn2_w_gate", "blk", 704, 1024), ("ffn2_w_up", "blk", 704, 1024), ("ffn2_w_down", "blk", 704, 1024),
           ("ple_w_gate", "blk", 256, 1024), ("ple_w_proj", "col", 256, 256)),
}
SMALL = (
    ("ffn1_norm", 1024), ("mix_norm", 1024), ("ffn2_norm", 1024), ("ple_norm", 1024), ("rwkv_mu", 1792),
    ("rwkv_w0", 512), ("rwkv_a0", 512), ("rwkv_k_k", 512), ("rwkv_k_a", 512), ("rwkv_r_k", 512),
    ("rwkv_gn_w", 512), ("rwkv_gn_b", 512), ("q_norm", 64), ("k_norm", 64),
)
SMALL_ROWS = 16
WEIGHTS = (
    "ffn1_norm", "ffn1_w_gate", "ffn1_w_up", "ffn1_w_down", "mix_norm", "w_in", "rwkv_mu", "rwkv_w0", "rwkv_w2",
    "rwkv_a0", "rwkv_a2", "rwkv_g2", "rwkv_k_k", "rwkv_k_a", "rwkv_r_k", "rwkv_gn_w", "rwkv_gn_b", "q_norm", "k_norm",
    "w_br_rwkv", "w_br_attn", "w_out", "ffn2_norm", "ffn2_w_gate", "ffn2_w_up", "ffn2_w_down", "ple_norm",
    "ple_w_gate", "ple_w_proj",
)


def _row_tile(n, most=704):
    for t in range(most - most % 16, 0, -16):
        if n % t == 0:
            return t
    return n


def _pick(n, cands):
    for c in cands:
        if n % c == 0:
            return c
    return n


def _mm(a, b, *, ta=False, tb=False, sum_blocks=False, out_dtype=F32, res=None, alpha=1.0, dep=None, name):
    flat = a.ndim == 2 and b.ndim == 2
    a3 = a if a.ndim == 3 else a[None]
    b3 = b if b.ndim == 3 else b[None]
    na, nbb = a3.shape[0], b3.shape[0]
    nblk = max(na, nbb)
    kdim, m = (a3.shape[1], a3.shape[2]) if ta else (a3.shape[2], a3.shape[1])
    n = b3.shape[1] if tb else b3.shape[2]
    assert (b3.shape[2] if tb else b3.shape[1]) == kdim
    tm = _pick(m, (1024, 512, 256, 128))
    tn = _pick(n, (1024, 896, 768, 512, 256, 128))
    tk = kdim if kdim <= 2304 else _pick(kdim, (1024, 512, 256, 128))
    nk = kdim // tk
    direct = nk == 1 and not sum_blocks

    if sum_blocks:
        grid = (m // tm, n // tn, nblk, nk)

        def ids(i, c, j, k):
            return i, c, j, k
    else:
        grid = (nblk, m // tm, n // tn, nk)

        def ids(j, i, c, k):
            return i, c, j, k

    def amap(*g):
        i, c, j, k = ids(*g)
        jj = j if na > 1 else 0
        return (jj, k, i) if ta else (jj, i, k)

    def bmap(*g):
        i, c, j, k = ids(*g)
        jj = j if nbb > 1 else 0
        return (jj, c, k) if tb else (jj, k, c)

    if sum_blocks:
        oshape, oblk = (m, n), (tm, tn)

        def omap(*g):
            i, c, j, k = ids(*g)
            return i, c
    else:
        oshape, oblk = (nblk, m, n), (1, tm, tn)

        def omap(*g):
            i, c, j, k = ids(*g)
            return j, i, c

    dn = (((0 if ta else 1,), (1 if tb else 0,)), ((), ()))
    has_res = res is not None

    def body(*refs):
        refs = list(refs)
        acc = None if direct else refs.pop()
        o_ref = refs.pop()
        a_ref, b_ref = refs[0], refs[1]
        r_ref = refs[2] if has_res else None

        def finish(v):
            if alpha != 1.0:
                v = v * alpha
            if has_res:
                v = v + r_ref[...].reshape(v.shape).astype(F32)
            o_ref[...] = v.reshape(o_ref.shape).astype(o_ref.dtype)

        if direct:
            finish(lax.dot_general(a_ref[0].astype(BF16), b_ref[0].astype(BF16), dn, preferred_element_type=F32))
            return
        k = pl.program_id(3)
        if sum_blocks:
            j = pl.program_id(2)
            first = jnp.logical_and(j == 0, k == 0)
            last = jnp.logical_and(j == nblk - 1, k == nk - 1)
        else:
            first, last = k == 0, k == nk - 1

        @pl.when(first)
        def _():
            acc[...] = jnp.zeros_like(acc)

        acc[...] += lax.dot_general(a_ref[0].astype(BF16), b_ref[0].astype(BF16), dn, preferred_element_type=F32)

        @pl.when(last)
        def _():
            finish(acc[...])

    in_specs = [pl.BlockSpec((1, tk, tm) if ta else (1, tm, tk), amap), pl.BlockSpec((1, tn, tk) if tb else (1, tk, tn), bmap)]
    args = [a3, b3]
    if has_res:
        res3 = res if (sum_blocks or res.ndim == 3) else res[None]
        in_specs.append(pl.BlockSpec(oblk, omap))
        args.append(res3)
    if dep is not None:
        in_specs.append(pl.BlockSpec(memory_space=pl.ANY))
        args.append(dep)
    out = pl.pallas_call(
        body,
        name=name,
        grid=grid,
        in_specs=in_specs,
        out_specs=pl.BlockSpec(oblk, omap),
        out_shape=SDS(oshape, out_dtype),
        scratch_shapes=[] if direct else [pltpu.VMEM((tm, tn), F32)],
        compiler_params=pltpu.CompilerParams(
            dimension_semantics=("parallel", "parallel", "arbitrary", "arbitrary") if sum_blocks
            else ("parallel", "parallel", "parallel", "arbitrary"),
            vmem_limit_bytes=VMEM_LIMIT),
    )(*args)
    if flat and not sum_blocks:
        out = out[0]
    return out


def _rows_call(f, rows, params, outs, accs=(), *, tm, name, dep=None):
    s = rows[0][0].shape[0]
    nr, npar, no = len(rows), len(params), len(outs)
    nin = nr + npar + (0 if dep is None else 1)
    in_specs = [pl.BlockSpec((tm, w), functools.partial(lambda i, cb: (i, cb), cb=cb)) for (_, cb, w) in rows]
    in_specs += [pl.BlockSpec(p.shape, functools.partial(lambda i, nd: (0,) * nd, nd=p.ndim)) for p in params]
    if dep is not None:
        in_specs.append(pl.BlockSpec(memory_space=pl.ANY))
    out_shape = [SDS((s, w), dt) for (w, dt) in outs] + [SDS(tuple(sh), F32) for sh in accs]
    out_specs = [pl.BlockSpec((tm, w), lambda i: (i, 0)) for (w, _) in outs]
    out_specs += [pl.BlockSpec(tuple(sh), functools.partial(lambda i, nd: (0,) * nd, nd=len(sh))) for sh in accs]

    def body(*refs):
        rin, pin = refs[:nr], refs[nr:nr + npar]
        oo, ao = refs[nin:nin + no], refs[nin + no:]
        res = f(*[r[...] for r in rin], *[p[...] for p in pin])
        if not isinstance(res, (tuple, list)):
            res = (res,)
        for o_ref, v in zip(oo, res[:no]):
            o_ref[...] = v.astype(o_ref.dtype)
        i = pl.program_id(0)
        for a_ref, v in zip(ao, res[no:]):
            @pl.when(i == 0)
            def _():
                a_ref[...] = jnp.zeros_like(a_ref)

            a_ref[...] += v.reshape(a_ref.shape)

    res = pl.pallas_call(
        body,
        name=name,
        grid=(s // tm,),
        in_specs=in_specs,
        out_specs=out_specs,
        out_shape=out_shape,
        compiler_params=pltpu.CompilerParams(dimension_semantics=("arbitrary",), vmem_limit_bytes=VMEM_LIMIT),
    )(*[r[0] for r in rows], *params, *([] if dep is None else [dep]))
    return res


def _mmv(a, b, mode):
    ca = 0 if mode[0] == "t" else 1
    cb = 1 if mode[1] == "t" else 0
    return lax.dot_general(a.astype(BF16), b.astype(BF16), (((ca,), (cb,)), ((), ())), preferred_element_type=F32)


@functools.partial(jax.custom_vjp, nondiff_argnums=(2,))
def _bdot(a, b, mode):
    return _mmv(a, b, mode)


def _bdot_fwd(a, b, mode):
    return _mmv(a, b, mode), (a, b)


def _bdot_bwd(mode, saved, g):
    a, b = saved
    if mode == "nn":
        return _mmv(g, b, "nt"), _mmv(a, g, "tn")
    if mode == "nt":
        return _mmv(g, b, "nn"), _mmv(g, a, "tn")
    return _mmv(b, g, "nt"), _mmv(a, g, "nn")


_bdot.defvjp(_bdot_fwd, _bdot_bwd)


def _hdot(a, b, mode="nn", precision=HI):
    ca = 0 if mode[0] == "t" else 1
    cb = 1 if mode[1] == "t" else 0
    return lax.dot_general(a, b, (((ca,), (cb,)), ((), ())), precision=precision, preferred_element_type=F32)


def _segsum(x):
    c = x.shape[-1]
    blk = min(c, 256)
    r = lax.broadcasted_iota(jnp.int32, (blk, blk), 0) >> 6
    q = lax.broadcasted_iota(jnp.int32, (blk, blk), 1) >> 6
    ones = jnp.where(r == q, 1.0, 0.0).astype(F32)
    parts = [_hdot(x[:, i:i + blk], ones, precision=lax.Precision.HIGH) for i in range(0, c, blk)]
    return parts[0] if len(parts) == 1 else jnp.concatenate(parts, axis=1)


def _sigmoid(x):
    return jax.nn.sigmoid(x)


def _softplus(x):
    return jnp.maximum(x, 0.0) + jnp.log(1.0 + jnp.exp(-jnp.abs(x)))


def _rms(x, gain):
    return x * lax.rsqrt(jnp.mean(x * x, axis=-1, keepdims=True) + RMS_EPS) * gain


def _swiglu_act(gate, up):
    return gate * _sigmoid(gate) * up


def _rwkv_pre(zs, w0, w2, a0, a2, g2, k_k, k_a):
    r, k, v = zs[:, 0:512], zs[:, 512:1024], zs[:, 1024:1536]
    lora = zs[:, 1536:1792]
    wd, ad, gd = lora[:, 0:64], lora[:, 64:128], lora[:, 128:256]
    w = -_softplus(-(w0 + _bdot(jnp.tanh(wd), w2, "nn"))) - 0.5
    a = _sigmoid(a0 + _bdot(ad, a2, "nn"))
    g = _bdot(_sigmoid(gd), g2, "nn")
    kk = k * k_k
    kk = kk * lax.rsqrt(jnp.maximum(_segsum(kk * kk), 1e-24))
    k2 = k * (1.0 + (a - 1.0) * k_a)
    return r, -jnp.exp(w), k2, v, -kk, kk * a, g


def _rwkv_post(y, r, k2, v, g, gn_w, gn_b, r_k):
    mean = _segsum(y) * (1.0 / HEAD)
    yc = y - mean
    var = _segsum(yc * yc) * (1.0 / HEAD)
    yn = yc * lax.rsqrt(var + GN_EPS) * gn_w + gn_b
    bonus = _segsum(r * k2 * r_k) * v
    return (yn + bonus) * g


def _swap_halves(x):
    lane = lax.broadcasted_iota(jnp.int32, x.shape, 1)
    return jnp.where((lane & 32) == 0, jnp.roll(x, -32, axis=1), jnp.roll(x, 32, axis=1))


def _norm_rope(x, gain, cos, sin):
    heads = x.shape[1] // HEAD
    def rep(t):
        return jnp.concatenate([t] * heads, axis=1)

    xn = x * lax.rsqrt(_segsum(x * x) * (1.0 / HEAD) + RMS_EPS) * rep(gain)
    return xn * rep(cos) + _swap_halves(xn) * rep(sin)


def _attn_combine(o0, o1, o2, l0, l1, l2):
    m = jnp.maximum(jnp.maximum(l0, l1), l2)
    e0, e1, e2 = jnp.exp(l0 - m), jnp.exp(l1 - m), jnp.exp(l2 - m)
    return (e0 * o0 + e1 * o1 + e2 * o2) / (e0 + e1 + e2)


def _merge(zgr, zga, br, ba):
    return _sigmoid(zgr) * br + _sigmoid(zga) * ba


def _attn_block(q, kp, kc, vp, vc, has_prev):
    iq = lax.broadcasted_iota(jnp.int32, (1, BAND, BAND), 1)
    ik = lax.broadcasted_iota(jnp.int32, (1, BAND, BAND), 2)
    s_c = jnp.where(iq >= ik, _bdotb(q, kc, "nt") * (HEAD ** -0.5), NEG_INF)
    s_p = jnp.where(jnp.logical_and(iq <= ik, has_prev), _bdotb(q, kp, "nt") * (HEAD ** -0.5), NEG_INF)
    m = lax.stop_gradient(jnp.maximum(jnp.max(s_c, axis=-1, keepdims=True), jnp.max(s_p, axis=-1, keepdims=True)))
    e_c, e_p = jnp.exp(s_c - m), jnp.exp(s_p - m)
    l = jnp.sum(e_c, axis=-1, keepdims=True) + jnp.sum(e_p, axis=-1, keepdims=True)
    o = (_bdotb(e_c, vc) + _bdotb(e_p, vp)) / l
    return o, jnp.broadcast_to(m + jnp.log(l), o.shape)


def _mmb(a, b, cb):
    return lax.dot_general(a.astype(BF16), b.astype(BF16), (((2,), (cb,)), ((0,), (0,))), preferred_element_type=F32)


@functools.partial(jax.custom_vjp, nondiff_argnums=(2,))
def _bdotb1(a, b, cb):
    return _mmb(a, b, cb)


def _bdotb1_fwd(a, b, cb):
    return _mmb(a, b, cb), (a, b)


def _bdotb1_bwd(cb, saved, g):
    a, b = saved
    if cb == 1:
        return _mmb(g, b, 2), _mmb(jnp.swapaxes(a, 1, 2), g, 1)
    return _mmb(g, b, 1), _mmb(jnp.swapaxes(g, 1, 2), a, 1)


_bdotb1.defvjp(_bdotb1_fwd, _bdotb1_bwd)


def _bdotb(a, b, mode="nn", precision=None):
    if mode[0] == "t":
        a = jnp.swapaxes(a, 1, 2)
    cb = 2 if mode[1] == "t" else 1
    if precision is None:
        return _bdotb1(a, b, cb)
    return lax.dot_general(a, b, (((2,), (cb,)), ((0,), (0,))), precision=precision, preferred_element_type=F32)


def _tri_inv_levels(a):
    t = a.shape[-1]
    row = lax.broadcasted_iota(jnp.int32, (1, t, t), 1)
    col = lax.broadcasted_iota(jnp.int32, (1, t, t), 2)
    x = jnp.where(row == col, 1.0, 0.0).astype(F32) + jnp.where(jnp.logical_and(row == col + 1, (row & 1) == 1), a, 0.0)
    sh = 1
    while (1 << sh) < t:
        m = jnp.logical_and((row >> sh) == (col >> sh) + 1, (row >> (sh + 1)) == (col >> (sh + 1)))
        x = x + _bdotb(_bdotb(x, jnp.where(m, a, 0.0), precision=lax.Precision.HIGH), x, precision=lax.Precision.HIGH)
        sh += 1
    return x


@jax.custom_vjp
def _tri_inv(a):
    return _tri_inv_levels(a)


def _tri_inv_fwd(a):
    x = _tri_inv_levels(a)
    return x, x


def _tri_inv_bwd(x, g):
    xt = jnp.swapaxes(x, 1, 2)
    return (_bdotb(_bdotb(xt, g, precision=lax.Precision.HIGH), xt, precision=lax.Precision.HIGH),)


_tri_inv.defvjp(_tri_inv_fwd, _tri_inv_bwd)


def _wkv_chunk(s0, r, lw, k, v, a, b):
    nh, t, _ = r.shape
    row = lax.broadcasted_iota(jnp.int32, (1, t, t), 1)
    col = lax.broadcasted_iota(jnp.int32, (1, t, t), 2)
    incl, strict = row >= col, row > col
    ones = jnp.broadcast_to(jnp.where(incl, 1.0, 0.0).astype(F32), (nh, t, t))
    cum = _bdotb(ones, lw, precision=HI)
    c_end = cum[:, t - 1:t, :]
    e_in, e_ex, e_inv = jnp.exp(cum), jnp.exp(cum - lw), jnp.exp(-cum)
    at, rt, bt, kt = a * e_ex, r * e_in, b * e_inv, k * e_inv
    a_ab = jnp.where(strict, _bdotb(at, bt, "nt"), 0.0)
    a_ak = jnp.where(strict, _bdotb(at, kt, "nt"), 0.0)
    u = _bdotb(_tri_inv(a_ab), _bdotb(at, s0, "nt") + _bdotb(a_ak, v))
    y = (_bdotb(rt, s0, "nt") + _bdotb(jnp.where(incl, _bdotb(rt, bt, "nt"), 0.0), u)
         + _bdotb(jnp.where(incl, _bdotb(rt, kt, "nt"), 0.0), v))
    w_end = jnp.exp(c_end - cum)
    s1 = s0 * jnp.exp(c_end) + _bdotb(u, b * w_end, "tn") + _bdotb(v, k * w_end, "tn")
    return y, s1


def _shift_fwd(z, mu):
    s, c = z.shape
    tc = 256

    def body(z_ref, mu_ref, o_ref):
        zz = z_ref[...]
        row = lax.broadcasted_iota(jnp.int32, zz.shape, 0)
        prev = jnp.where(row == 0, 0.0, pltpu.roll(zz, 1, 0))
        o_ref[...] = zz + (prev - zz) * mu_ref[...]

    return pl.pallas_call(
        body, name="shift_fwd", grid=(c // tc,),
        in_specs=[pl.BlockSpec((s, tc), lambda j: (0, j)), pl.BlockSpec((1, tc), lambda j: (0, j))],
        out_specs=pl.BlockSpec((s, tc), lambda j: (0, j)), out_shape=SDS((s, c), F32),
        compiler_params=pltpu.CompilerParams(dimension_semantics=("parallel",), vmem_limit_bytes=VMEM_LIMIT),
    )(z, mu)


def _shift_bwd(z, mu, dzs):
    s, c = z.shape
    tc = 256

    def body(z_ref, mu_ref, d_ref, dz_ref, dmu_ref):
        zz, d, m = z_ref[...], d_ref[...], mu_ref[...]
        row = lax.broadcasted_iota(jnp.int32, zz.shape, 0)
        prev = jnp.where(row == 0, 0.0, pltpu.roll(zz, 1, 0))
        t = d * m
        nxt = jnp.where(row == s - 1, 0.0, pltpu.roll(t, s - 1, 0))
        dz_ref[...] = (d - t + nxt).astype(dz_ref.dtype)
        dmu_ref[...] = jnp.sum(d * (prev - zz), axis=0, keepdims=True)

    return pl.pallas_call(
        body, name="shift_bwd", grid=(c // tc,),
        in_specs=[pl.BlockSpec((s, tc), lambda j: (0, j)), pl.BlockSpec((1, tc), lambda j: (0, j)),
                  pl.BlockSpec((s, tc), lambda j: (0, j))],
        out_specs=[pl.BlockSpec((s, tc), lambda j: (0, j)), pl.BlockSpec((1, tc), lambda j: (0, j))],
        out_shape=[SDS((s, c), BF16), SDS((1, c), F32)],
        compiler_params=pltpu.CompilerParams(dimension_semantics=("parallel",), vmem_limit_bytes=VMEM_LIMIT),
    )(z, mu, dzs)


def _heads(x, nh):
    return jnp.stack([x[:, h * HEAD:(h + 1) * HEAD] for h in range(nh)], axis=0)


def _unheads(x):
    return jnp.concatenate([x[h] for h in range(x.shape[0])], axis=1)


def _wkv_fwd(zs, lw, k2, na, b):
    s = lw.shape[0]
    t, hb = WKV_CHUNK, WKV_HEADS_PER_STEP
    w = hb * HEAD
    nc, ng = s // t, RWKV_HEADS // hb

    def body(r_ref, v_ref, lw_ref, k_ref, a_ref, b_ref, y_ref, s0_ref, state):
        @pl.when(pl.program_id(1) == 0)
        def _():
            state[...] = jnp.zeros_like(state)

        s0 = state[...]
        s0_ref[0] = s0
        y, s1 = _wkv_chunk(s0, *[_heads(t_ref[...], hb) for t_ref in (r_ref, lw_ref, k_ref, v_ref, a_ref, b_ref)])
        y_ref[...] = _unheads(y)
        state[...] = s1

    def col(off):
        return pl.BlockSpec((t, w), functools.partial(lambda g, i, off: (i, g + off), off=off))

    return pl.pallas_call(
        body, name="wkv_fwd", grid=(ng, nc),
        in_specs=[col(0), col(2 * ng), col(0), col(0), col(0), col(0)],
        out_specs=[col(0), pl.BlockSpec((1, hb, HEAD, HEAD), lambda g, i: (i, g, 0, 0))],
        out_shape=[SDS((s, RWKV_DIM), F32), SDS((nc, RWKV_HEADS, HEAD, HEAD), F32)],
        scratch_shapes=[pltpu.VMEM((hb, HEAD, HEAD), F32)],
        compiler_params=pltpu.CompilerParams(dimension_semantics=("parallel", "arbitrary"), vmem_limit_bytes=VMEM_LIMIT),
    )(zs, zs, lw, k2, na, b)


def _wkv_bwd(zs, lw, k2, na, b, s0s, dy):
    s = lw.shape[0]
    t, hb = WKV_CHUNK, WKV_HEADS_PER_STEP
    w = hb * HEAD
    nc, ng = s // t, RWKV_HEADS // hb

    def body(r_ref, v_ref, lw_ref, k_ref, a_ref, b_ref, s0_ref, dy_ref, dr_ref, dlw_ref, dk_ref, dv_ref, da_ref, db_ref, dstate):
        @pl.when(pl.program_id(1) == 0)
        def _():
            dstate[...] = jnp.zeros_like(dstate)

        _, vjp = jax.vjp(_wkv_chunk, s0_ref[0], *[_heads(t_ref[...], hb) for t_ref in (r_ref, lw_ref, k_ref, v_ref, a_ref, b_ref)])
        grads = vjp((_heads(dy_ref[...], hb), dstate[...]))
        dstate[...] = grads[0]
        for o_ref, gval in zip((dr_ref, dlw_ref, dk_ref, dv_ref, da_ref, db_ref), grads[1:]):
            o_ref[...] = _unheads(gval)

    def col(off):
        return pl.BlockSpec((t, w), functools.partial(lambda g, i, off: (nc - 1 - i, g + off), off=off))

    return pl.pallas_call(
        body, name="wkv_bwd", grid=(ng, nc),
        in_specs=[col(0), col(2 * ng), col(0), col(0), col(0), col(0),
                  pl.BlockSpec((1, hb, HEAD, HEAD), lambda g, i: (nc - 1 - i, g, 0, 0)), col(0)],
        out_specs=[col(0)] * 6,
        out_shape=[SDS((s, RWKV_DIM), F32)] * 6,
        scratch_shapes=[pltpu.VMEM((hb, HEAD, HEAD), F32)],
        compiler_params=pltpu.CompilerParams(dimension_semantics=("parallel", "arbitrary"), vmem_limit_bytes=VMEM_LIMIT),
    )(zs, zs, lw, k2, na, b, s0s, dy)


def _attn_fwd(q, k, v, d):
    s = q.shape[0]
    l = s // d
    nb = l // BAND
    assert nb * BAND == l
    qv, kv, vv = (t.reshape(l, d * GROUP_DIM) for t in (q, k, v))
    nh = GROUP_DIM // HEAD

    def body(q_ref, kp_ref, kc_ref, vp_ref, vc_ref, o_ref, l_ref):
        has_prev = pl.program_id(1) > 0
        o, lse = _attn_block(*[_heads(t_ref[...].astype(F32), nh) for t_ref in (q_ref, kp_ref, kc_ref, vp_ref, vc_ref)], has_prev)
        o_ref[...] = _unheads(o)
        l_ref[...] = _unheads(lse)

    cur = pl.BlockSpec((BAND, GROUP_DIM), lambda rho, i: (i, rho))
    prev = pl.BlockSpec((BAND, GROUP_DIM), lambda rho, i: (jnp.maximum(i - 1, 0), rho))
    o, lse = pl.pallas_call(
        body, name=f"attn_fwd_d{d}", grid=(d, nb),
        in_specs=[cur, prev, cur, prev, cur], out_specs=[cur, cur],
        out_shape=[SDS((l, d * GROUP_DIM), F32), SDS((l, d * GROUP_DIM), F32)],
        compiler_params=pltpu.CompilerParams(dimension_semantics=("parallel", "arbitrary"), vmem_limit_bytes=VMEM_LIMIT),
    )(qv, kv, kv, vv, vv)
    return o.reshape(s, GROUP_DIM), lse.reshape(s, GROUP_DIM)


def _attn_bwd(q, k, v, d, do, dlse):
    s = q.shape[0]
    l = s // d
    nb = l // BAND
    qv, kv, vv, dov, dlv = (t.reshape(l, d * GROUP_DIM) for t in (q, k, v, do, dlse))
    nh = GROUP_DIM // HEAD

    def body(q_ref, kp_ref, kc_ref, vp_ref, vc_ref, do_ref, dl_ref, dq_ref, dk_ref, dv_ref, ck, cv):
        step = pl.program_id(1)
        has_prev = step < nb - 1

        @pl.when(step == 0)
        def _():
            ck[...] = jnp.zeros_like(ck)
            cv[...] = jnp.zeros_like(cv)

        _, vjp = jax.vjp(functools.partial(_attn_block, has_prev=has_prev),
                         *[_heads(t_ref[...].astype(F32), nh) for t_ref in (q_ref, kp_ref, kc_ref, vp_ref, vc_ref)])
        dq, dkp, dkc, dvp, dvc = vjp((_heads(do_ref[...], nh), _heads(dl_ref[...], nh)))
        dq_ref[...] = _unheads(dq)
        dk_ref[...] = _unheads(dkc) + ck[...]
        dv_ref[...] = _unheads(dvc) + cv[...]
        ck[...] = _unheads(dkp)
        cv[...] = _unheads(dvp)

    cur = pl.BlockSpec((BAND, GROUP_DIM), lambda rho, i: (nb - 1 - i, rho))
    prev = pl.BlockSpec((BAND, GROUP_DIM), lambda rho, i: (jnp.maximum(nb - 2 - i, 0), rho))
    dq, dk, dv = pl.pallas_call(
        body, name=f"attn_bwd_d{d}", grid=(d, nb),
        in_specs=[cur, prev, cur, prev, cur, cur, cur], out_specs=[cur] * 3,
        out_shape=[SDS((l, d * GROUP_DIM), F32)] * 3,
        scratch_shapes=[pltpu.VMEM((BAND, GROUP_DIM), F32), pltpu.VMEM((BAND, GROUP_DIM), F32)],
        compiler_params=pltpu.CompilerParams(dimension_semantics=("parallel", "arbitrary"), vmem_limit_bytes=VMEM_LIMIT),
    )(qv, kv, kv, vv, vv, dov, dlv)
    return dq.reshape(s, GROUP_DIM), dk.reshape(s, GROUP_DIM), dv.reshape(s, GROUP_DIM)


def _coords():
    return lax.axis_index("x"), lax.axis_index("y"), lax.axis_index("c")


_CHIP_FLIPS = ((1, 0), (0, 1), (1, 1))


def _flip(v, f):
    return 1 - v if f else v


def _form(kind, r, c):
    return (N_CHIPS, r, c) if kind == "blk" else (r, N_CHIPS * c)


def _slot(ref, kind, j, rows, c):
    if kind == "blk":
        return ref.at[j] if rows is None else ref.at[j, rows]
    cols = pl.ds(pl.multiple_of(j * c, 128), c)
    return ref.at[:, cols] if rows is None else ref.at[rows, cols]


def _half(r, which, align):
    return pl.ds(pl.multiple_of(which * (r // 2), align), r // 2)


def _rcopy(src, dst, send_sems, recv_sems, kk, dev):
    return pltpu.make_async_remote_copy(src_ref=src, dst_ref=dst, send_sem=send_sems.at[kk], recv_sem=recv_sems.at[kk],
                                        device_id=dev, device_id_type=MESH)


def _gather_plan(specs, step):
    def copies(refs, ss, rs, received):
        x, y, c = _coords()
        out = []
        for w, (kind, r, cc) in enumerate(specs):
            mine, other = _half(r, c, 16), _half(r, 1 - c, 16)
            for kk, (fx, fy) in enumerate(_CHIP_FLIPS):
                px, py = _flip(x, fx), _flip(y, fy)
                if step == "ici":
                    sl = _slot(refs[w], kind, 2 * px + py if received else 2 * x + y, mine, cc)
                    dev = (px, py, c)
                else:
                    sl = _slot(refs[w], kind, 2 * px + py, other if received else mine, cc)
                    dev = (x, y, 1 - c)
                out.append(_rcopy(sl, sl, ss, rs, 3 * w + kk, dev))
        return out

    def issue(refs, ss, rs):
        return copies(refs, ss, rs, False)

    def expect(refs, ss, rs):
        return copies(refs, ss, rs, False), copies(refs, ss, rs, True)

    return issue, expect


_HBM = pl.BlockSpec(memory_space=pltpu.HBM)
_SEM = pl.BlockSpec(memory_space=pltpu.SEMAPHORE)
_EFFECT = pltpu.SideEffectType.DATAFLOW_SIDE_EFFECTING


def _copies_start(name, bufs, n_sems, issue, after=None):
    nb = len(bufs)
    extra = [] if after is None else [after]

    def body(*refs):
        send_sems, recv_sems = refs[nb + len(extra)], refs[nb + len(extra) + 1]
        for cp in issue(refs[:nb], send_sems, recv_sems):
            cp.start()
        refs[-1][...] = jnp.zeros_like(refs[-1])

    outs = pl.pallas_call(
        body, name=name,
        out_shape=(pltpu.SemaphoreType.DMA((n_sems,)), pltpu.SemaphoreType.DMA((n_sems,)),
                   *[pltpu.HBM(b.shape, b.dtype) for b in bufs], SDS((8, 128), F32)),
        in_specs=[_HBM] * nb + [pl.BlockSpec(memory_space=pl.ANY)] * len(extra),
        out_specs=(_SEM, _SEM, *[_HBM] * nb, pl.BlockSpec(memory_space=pltpu.VMEM)),
        input_output_aliases={i: 2 + i for i in range(nb)},
        compiler_params=pltpu.CompilerParams(has_side_effects=_EFFECT),
    )(*[pltpu.with_memory_space_constraint(b, pltpu.HBM) for b in bufs], *extra)
    return outs[0], outs[1], list(outs[2:2 + nb]), outs[-1]


def _copies_wait(name, bufs, send_sems, recv_sems, after, expect):
    nb = len(bufs)

    def body(*refs):
        sent, received = expect(refs[:nb], refs[nb], refs[nb + 1])
        for cp in sent:
            cp.wait_send()
        for cp in received:
            cp.wait_recv()

    outs = pl.pallas_call(
        body, name=name,
        out_shape=tuple(pltpu.HBM(b.shape, b.dtype) for b in bufs),
        in_specs=(*[_HBM] * nb, _SEM, _SEM, pl.BlockSpec(memory_space=pl.ANY)), out_specs=tuple([_HBM] * nb),
        input_output_aliases={i: i for i in range(nb)},
        compiler_params=pltpu.CompilerParams(has_side_effects=_EFFECT),
    )(*bufs, send_sems, recv_sems, after)
    return list(outs)


def _add_pair(g, recv, kind, r, c, c_arr, name):
    h = r // 2
    if kind == "blk":
        tr = _row_tile(h, 512)
        grid = (N_CHIPS, h // tr)
        g_spec = pl.BlockSpec((1, 1, tr, c), lambda j, i, c_ref: (j, c_ref[0], i, 0))
        o_spec = pl.BlockSpec((1, tr, c), lambda j, i, c_ref: (j, i, 0))
        gv, oshape = g.reshape(N_CHIPS, 2, h, c), (N_CHIPS, h, c)
    else:
        tr = _row_tile(h, 64)
        grid = (h // tr,)
        g_spec = pl.BlockSpec((1, tr, N_CHIPS * c), lambda i, c_ref: (c_ref[0], i, 0))
        o_spec = pl.BlockSpec((tr, N_CHIPS * c), lambda i, c_ref: (i, 0))
        gv, oshape = g.reshape(2, h, N_CHIPS * c), (h, N_CHIPS * c)

    def body(c_ref, g_ref, r_ref, o_ref, ob_ref):
        v = (g_ref[:, 0] if kind == "blk" else g_ref[0]) + r_ref[...]
        o_ref[...] = v
        ob_ref[...] = v.astype(BF16)

    return pl.pallas_call(
        body, name=name,
        grid_spec=pltpu.PrefetchScalarGridSpec(num_scalar_prefetch=1, grid=grid, in_specs=[g_spec, o_spec], out_specs=[o_spec] * 2),
        out_shape=[SDS(oshape, F32), SDS(oshape, BF16)],
        compiler_params=pltpu.CompilerParams(vmem_limit_bytes=VMEM_LIMIT),
    )(c_arr, gv, recv)


def _add_chips(pair, recv, kind, r, c, mc_arr, name):
    h = r // 2
    tr = _row_tile(h, 512)
    if kind == "blk":
        p_spec = pl.BlockSpec((1, tr, c), lambda i, mc: (mc[0], i, 0))
    else:
        p_spec = pl.BlockSpec((tr, c), lambda i, mc: (i, mc[0]))

    def body(mc, a_ref, r_ref, o_ref):
        own = a_ref[0] if kind == "blk" else a_ref[...]
        o_ref[...] = ((own + r_ref[0].astype(F32)) + r_ref[1].astype(F32)) + r_ref[2].astype(F32)

    return pl.pallas_call(
        body, name=name,
        grid_spec=pltpu.PrefetchScalarGridSpec(
            num_scalar_prefetch=1, grid=(h // tr,), in_specs=[p_spec, pl.BlockSpec((3, tr, c), lambda i, mc: (0, i, 0))],
            out_specs=pl.BlockSpec((tr, c), lambda i, mc: (mc[1] * (h // tr) + i, 0))),
        out_shape=SDS((r, c), F32),
        compiler_params=pltpu.CompilerParams(vmem_limit_bytes=VMEM_LIMIT),
    )(mc_arr, pair, recv)


class _GroupReduce:
    def __init__(self, tag, specs, c_arr, mc_arr):
        self.tag, self.specs, self.c_arr, self.mc_arr = tag, specs, c_arr, mc_arr
        self.n = len(specs)

    def _plan(self, step):
        specs, n = self.specs, self.n

        def copies(refs, ss, rs, received):
            x, y, c = _coords()
            sib, out = (x, y, 1 - c), []
            for w, (_, kind, r, cc) in enumerate(specs):
                src, land = refs[w], refs[w if step == "join" else n + w]
                if step == "swap":
                    rows = _half(r, 1 - c, 8)
                    part = src.at[:, rows] if kind == "blk" else src.at[rows]
                    out.append(_rcopy(land if received else part, land, ss, rs, w, sib))
                elif step == "ici":
                    for kk, (fx, fy) in enumerate(_CHIP_FLIPS):
                        px, py = _flip(x, fx), _flip(y, fy)
                        part = land.at[kk] if received else _slot(src, kind, 2 * px + py, None, cc)
                        out.append(_rcopy(part, land.at[kk], ss, rs, 3 * w + kk, (px, py, c)))
                else:
                    there = land.at[_half(r, 1 - c if received else c, 8)]
                    out.append(_rcopy(there, there, ss, rs, w, sib))
            return out

        def issue(refs, ss, rs):
            return copies(refs, ss, rs, False)

        def expect(refs, ss, rs):
            return copies(refs, ss, rs, False), copies(refs, ss, rs, True)

        return issue, expect

    def swap_start(self, grads, after=None):
        lands = [lax.empty(_form(kind, r // 2, c), F32) for _, kind, r, c in self.specs]
        ss, rs, bufs, tok = _copies_start(f"rs_{self.tag}_swap", list(grads) + lands, self.n, self._plan("swap")[0], after=after)
        self.state = (ss, rs, bufs)
        return tok

    def swap_wait_ici_start(self, after):
        ss, rs, bufs = self.state
        bufs = _copies_wait(f"rs_{self.tag}_swap_wait", bufs, ss, rs, after, self._plan("swap")[1])
        pairs = [_add_pair(bufs[w], bufs[self.n + w], kind, r, c, self.c_arr, name=f"rs_{self.tag}_pair_{nm}")
                 for w, (nm, kind, r, c) in enumerate(self.specs)]
        self.pair = [pr[0] for pr in pairs]
        lands = [lax.empty((3, r // 2, c), BF16) for _, _, r, c in self.specs]
        ss, rs, bufs, tok = _copies_start(f"rs_{self.tag}_ici", [pr[1] for pr in pairs] + lands, 3 * self.n, self._plan("ici")[0])
        self.state = (ss, rs, bufs)
        return tok

    def ici_wait_join_start(self, after):
        ss, rs, bufs = self.state
        bufs = _copies_wait(f"rs_{self.tag}_ici_wait", bufs, ss, rs, after, self._plan("ici")[1])
        reds = [_add_chips(self.pair[w], bufs[self.n + w], kind, r, c, self.mc_arr, name=f"rs_{self.tag}_chips_{nm}")
                for w, (nm, kind, r, c) in enumerate(self.specs)]
        ss, rs, bufs, tok = _copies_start(f"rs_{self.tag}_join", reds, self.n, self._plan("join")[0])
        self.state = (ss, rs, bufs)
        return tok

    def join_wait(self, after):
        ss, rs, bufs = self.state
        bufs = _copies_wait(f"rs_{self.tag}_join_wait", bufs, ss, rs, after, self._plan("join")[1])
        return {nm: bufs[w] for w, (nm, _, _, _) in enumerate(self.specs)}


def _all_reduce_small(buf):
    rows, cols = buf.shape

    def body(x_ref, o_ref, gath, send_sems, recv_sems):
        x, y, c = _coords()
        me = 4 * x + 2 * y + c
        gath[me] = x_ref[...]
        sends = []
        for kk in range(1, 8):
            f = (kk >> 2) & 1, (kk >> 1) & 1, kk & 1
            px, py, pc = _flip(x, f[0]), _flip(y, f[1]), _flip(c, f[2])
            cp = pltpu.make_async_remote_copy(src_ref=x_ref, dst_ref=gath.at[me], send_sem=send_sems.at[kk - 1],
                                              recv_sem=recv_sems.at[kk - 1], device_id=(px, py, pc), device_id_type=MESH)
            cp.start()
            sends.append(cp)
        for kk in range(1, 8):
            f = (kk >> 2) & 1, (kk >> 1) & 1, kk & 1
            px, py, pc = _flip(x, f[0]), _flip(y, f[1]), _flip(c, f[2])
            there = gath.at[4 * px + 2 * py + pc]
            pltpu.make_async_remote_copy(src_ref=there, dst_ref=there, send_sem=send_sems.at[kk - 1],
                                         recv_sem=recv_sems.at[kk - 1], device_id=(px, py, pc), device_id_type=MESH).wait_recv()
        for cp in sends:
            cp.wait_send()
        acc = gath[0]
        for j in range(1, 8):
            acc = acc + gath[j]
        o_ref[...] = acc

    return pl.pallas_call(
        body, name="all_reduce_small",
        in_specs=[pl.BlockSpec(memory_space=pltpu.VMEM)], out_specs=pl.BlockSpec(memory_space=pltpu.VMEM),
        out_shape=SDS((rows, cols), F32),
        scratch_shapes=[pltpu.VMEM((8, rows, cols), F32), pltpu.SemaphoreType.DMA((7,)), pltpu.SemaphoreType.DMA((7,))],
    )(buf)


def _adamw_rows(w, g, m, v):
    m = ADAM_B1 * m + (1.0 - ADAM_B1) * g
    v = ADAM_B2 * v + (1.0 - ADAM_B2) * jnp.square(g)
    m_hat = m / (1.0 - ADAM_B1 ** ADAM_STEP)
    v_hat = v / (1.0 - ADAM_B2 ** ADAM_STEP)
    return -ADAM_LR * (m_hat / (jnp.sqrt(v_hat) + ADAM_EPS) + ADAM_WD * w), m, v


def _adamw(w, g, m, v, name, dep=None):
    rows, cols = w.shape
    tm = _pick(rows, (256, 128, 64, 16, 8))
    return _rows_call(_adamw_rows, [(t, 0, cols) for t in (w, g, m, v)], [], [(cols, F32)] * 3, tm=tm, name=name, dep=dep)


def _pack_small(parts):
    flat = jnp.concatenate([parts[n].reshape(-1) for n, _ in SMALL])
    return jnp.pad(flat, (0, SMALL_ROWS * PACK_COLS - flat.shape[0])).reshape(SMALL_ROWS, PACK_COLS)


def _unpack_small(buf, shapes):
    flat, out, off = buf.reshape(-1), {}, 0
    for n, sz in SMALL:
        out[n] = flat[off:off + sz].reshape(shapes[n])
        off += sz
    return out


def _lora_stack(parts):
    return jnp.concatenate([parts[n] for n, _ in LORA], axis=-2)


def _lora_split(stacked):
    out, off = {}, 0
    for n, rows in LORA:
        out[n] = stacked[..., off:off + rows, :]
        off += rows
    return out


def _ffn_gate_up(h, wgt, wut, name, dep=None):
    s, d = h.shape
    nblk, f, _ = wgt.shape
    tm = _pick(s, (1024, 512, 256))
    dn = (((1,), (1,)), ((), ()))

    def body(h_ref, wg_ref, wu_ref, *rest):
        g_ref, u_ref, a_ref = rest[-3:]
        hh = h_ref[...]
        g = lax.dot_general(hh, wg_ref[0], dn, preferred_element_type=F32)
        u = lax.dot_general(hh, wu_ref[0], dn, preferred_element_type=F32)
        g_ref[0], u_ref[0] = g, u
        a_ref[0] = _swiglu_act(g, u).astype(BF16)

    w_spec = pl.BlockSpec((1, f, d), lambda j, i: (j, 0, 0))
    o_spec = pl.BlockSpec((1, tm, f), lambda j, i: (j, i, 0))
    extra = [] if dep is None else [dep]
    return pl.pallas_call(
        body, name=name, grid=(nblk, s // tm),
        in_specs=[pl.BlockSpec((tm, d), lambda j, i: (i, 0)), w_spec, w_spec] + [pl.BlockSpec(memory_space=pl.ANY)] * len(extra),
        out_specs=[o_spec] * 3,
        out_shape=[SDS((nblk, s, f), F32), SDS((nblk, s, f), F32), SDS((nblk, s, f), BF16)],
        compiler_params=pltpu.CompilerParams(dimension_semantics=("parallel", "parallel"), vmem_limit_bytes=VMEM_LIMIT),
    )(h, wgt, wut, *extra)


def _ffn_down_dx(dx_bf, wd, gate, up, name, dep=None):
    s, d = dx_bf.shape
    nblk, f, _ = wd.shape
    tm = _pick(s, (1024, 512, 256))
    dn = (((1,), (1,)), ((), ()))

    def body(dx_ref, wd_ref, g_ref, u_ref, *rest):
        dg_ref, du_ref = rest[-2:]
        dact = 0.5 * lax.dot_general(dx_ref[...], wd_ref[0], dn, preferred_element_type=F32)
        _, vjp = jax.vjp(_swiglu_act, g_ref[0], u_ref[0])
        dg, du = vjp(dact)
        dg_ref[0], du_ref[0] = dg.astype(BF16), du.astype(BF16)

    o_spec = pl.BlockSpec((1, tm, f), lambda j, i: (j, i, 0))
    extra = [] if dep is None else [dep]
    return pl.pallas_call(
        body, name=name, grid=(nblk, s // tm),
        in_specs=[pl.BlockSpec((tm, d), lambda j, i: (i, 0)), pl.BlockSpec((1, f, d), lambda j, i: (j, 0, 0)), o_spec, o_spec]
        + [pl.BlockSpec(memory_space=pl.ANY)] * len(extra),
        out_specs=[o_spec] * 2, out_shape=[SDS((nblk, s, f), BF16)] * 2,
        compiler_params=pltpu.CompilerParams(dimension_semantics=("parallel", "parallel"), vmem_limit_bytes=VMEM_LIMIT),
    )(dx_bf, wd, gate, up, *extra)


def _ffn_fwd(x, gain, wgt, wut, wd, tag, h=None, dep=None):
    if h is None:
        h = _rows_call(_rms, [(x, 0, D_MODEL)], [gain], [(D_MODEL, BF16)], tm=256, name=f"{tag}_norm")[0]
    gate, up, act = _ffn_gate_up(h, wgt, wut, f"{tag}_gate_up", dep=dep)
    x_new = _mm(act, wd, sum_blocks=True, res=x, alpha=0.5, name=f"{tag}_down")
    return x_new, (x, h, gate, up, act)


def _ffn_bwd(dx_new, dx_new_bf, saved, gain, wgt, wut, wd, tag, dep=None, hooks=None):
    x, h, gate, up, act = saved
    hooks = hooks or {}
    d_wd = _mm(act, dx_new_bf, ta=True, alpha=0.5, name=f"{tag}_down_dw")
    if "down" in hooks:
        dep = hooks["down"](d_wd)
    dgate, dup = _ffn_down_dx(dx_new_bf, wd, gate, up, f"{tag}_down_dx", dep=dep)
    dep = hooks["mid"](dgate) if "mid" in hooks else None
    dh = _mm(dgate, wgt, sum_blocks=True, dep=dep, name=f"{tag}_gate_dx")
    dh = _mm(dup, wut, sum_blocks=True, res=dh, name=f"{tag}_up_dx")
    d_wgt = _mm(dgate, h, ta=True, name=f"{tag}_gate_dw")
    d_wut = _mm(dup, h, ta=True, name=f"{tag}_up_dw")
    dep = hooks["end"](dh) if "end" in hooks else None
    dx, dx_bf, dgain = _norm_bwd(x, gain, dh, dx_new, f"{tag}_norm_bwd", dep=dep)
    return dx, dx_bf, dgain, d_wgt, d_wut, d_wd


def _norm_bwd(x, gain, dh, dres, name, dep=None):
    def f(xt, dht, drt, gt):
        _, vjp = jax.vjp(_rms, xt, gt)
        dxt, dgt = vjp(dht)
        return dxt + drt, dxt + drt, dgt

    return _rows_call(f, [(x, 0, D_MODEL), (dh, 0, D_MODEL), (dres, 0, D_MODEL)], [gain], [(D_MODEL, F32), (D_MODEL, BF16)],
                      [(1, D_MODEL)], tm=256, name=name, dep=dep)


def kernel(x, p, positions, ffn1_norm, ffn1_w_gate, ffn1_w_up, ffn1_w_down, mix_norm, w_in, rwkv_mu, rwkv_w0, rwkv_w2, rwkv_a0, rwkv_a2, rwkv_g2, rwkv_k_k, rwkv_k_a, rwkv_r_k, rwkv_gn_w, rwkv_gn_b, q_norm, k_norm, w_br_rwkv, w_br_attn, w_out, ffn2_norm, ffn2_w_gate, ffn2_w_up, ffn2_w_down, ple_norm, ple_w_gate, ple_w_proj, loss_target, m_ffn1_norm, m_ffn1_w_gate, m_ffn1_w_up, m_ffn1_w_down, m_mix_norm, m_w_in, m_rwkv_mu, m_rwkv_w0, m_rwkv_w2, m_rwkv_a0, m_rwkv_a2, m_rwkv_g2, m_rwkv_k_k, m_rwkv_k_a, m_rwkv_r_k, m_rwkv_gn_w, m_rwkv_gn_b, m_q_norm, m_k_norm, m_w_br_rwkv, m_w_br_attn, m_w_out, m_ffn2_norm, m_ffn2_w_gate, m_ffn2_w_up, m_ffn2_w_down, m_ple_norm, m_ple_w_gate, m_ple_w_proj, v_ffn1_norm, v_ffn1_w_gate, v_ffn1_w_up, v_ffn1_w_down, v_mix_norm, v_w_in, v_rwkv_mu, v_rwkv_w0, v_rwkv_w2, v_rwkv_a0, v_rwkv_a2, v_rwkv_g2, v_rwkv_k_k, v_rwkv_k_a, v_rwkv_r_k, v_rwkv_gn_w, v_rwkv_gn_b, v_q_norm, v_k_norm, v_w_br_rwkv, v_w_br_attn, v_w_out, v_ffn2_norm, v_ffn2_w_gate, v_ffn2_w_up, v_ffn2_w_down, v_ple_norm, v_ple_w_gate, v_ple_w_proj):
    args = dict(locals())
    wts = {n: args[n] for n in WEIGHTS}
    mom_m = {n: args["m_" + n] for n in WEIGHTS}
    mom_v = {n: args["v_" + n] for n in WEIGHTS}
    x0, tgt = x[0], loss_target[0]
    s = x0.shape[0]
    p_tok = p[0, 0]

    vec = {n: wts[n].reshape(1, -1) for n, _ in SMALL}
    xi, yi, ci = _coords()
    me = 2 * xi + yi
    def laid(t, n):
        return jnp.transpose(t[n][0]) if n in TRANSPOSED else t[n][0]

    shard_of = {n: laid(wts, n) for g in GROUPS.values() for n, _, _, _ in g if n != "lora"}
    shard_of["lora"] = _lora_stack({n: wts[n][0] for n, _ in LORA})

    def whole_with_own(n, kind, r, c, tok=None):
        at = (me, 0, 0) if kind == "blk" else (0, me * c)
        own = (shard_of[n] if tok is None else shard_of[n] + tok[0, 0]).astype(BF16)
        return lax.dynamic_update_slice(lax.empty(_form(kind, r, c), BF16), own[None] if kind == "blk" else own, at)

    specs = {g: [(kind, r, c) for _, kind, r, c in grp] for g, grp in GROUPS.items()}
    plans = {(g, st): _gather_plan(specs[g], st) for g in GROUPS for st in ("ici", "d2d")}
    buf_f1 = [whole_with_own(*w) for w in GROUPS["f1"]]
    ss_0, rs_0, buf_f1, tok_0 = _copies_start("gather_f1_ici", buf_f1, 3 * len(buf_f1), plans["f1", "ici"][0])
    bufs = {g: [whole_with_own(*w, tok=tok_0) for w in GROUPS[g]] for g in ("mx", "f2")}
    buf_f1 = _copies_wait("gather_f1_ici_wait", buf_f1, ss_0, rs_0, bufs["mx"][0], plans["f1", "ici"][1])
    ss_1, rs_1, buf_f1, tok_1 = _copies_start("gather_f1_d2d", buf_f1, 3 * len(buf_f1), plans["f1", "d2d"][0])
    h1 = _rows_call(_rms, [(x0, 0, D_MODEL)], [vec["ffn1_norm"] + tok_1[0, 0]], [(D_MODEL, BF16)], tm=256, name="ffn1_norm")[0]
    buf_f1 = _copies_wait("gather_f1_d2d_wait", buf_f1, ss_1, rs_1, h1, plans["f1", "d2d"][1])
    wb = dict(zip([w[0] for w in GROUPS["f1"]], buf_f1))
    ss_a, rs_a, buf_mx, tok_a = _copies_start("gather_mx_ici", bufs["mx"], 3 * len(bufs["mx"]), plans["mx", "ici"][0],
                                              after=wb["ffn1_w_gate"])

    inv_freq = 1.0 / (ROPE_THETA ** (jnp.arange(0, HEAD, 2, dtype=F32) / HEAD))
    ang = positions[0].astype(F32)[:, None] * inv_freq
    cos, sin = jnp.cos(ang), jnp.sin(ang)
    cos2, sin2 = jnp.concatenate([cos, cos], axis=1), jnp.concatenate([-sin, sin], axis=1)

    x1, ffn1_saved = _ffn_fwd(x0, vec["ffn1_norm"], wb["ffn1_w_gate"], wb["ffn1_w_up"], wb["ffn1_w_down"], "ffn1", h=h1, dep=tok_a)
    buf_mx = _copies_wait("gather_mx_ici_wait", buf_mx, ss_a, rs_a, x1, plans["mx", "ici"][1])
    ss_b, rs_b, buf_mx, tok_b = _copies_start("gather_mx_d2d", buf_mx, 3 * len(buf_mx), plans["mx", "d2d"][0])
    ss_c, rs_c, buf_f2, tok_c = _copies_start("gather_f2_ici", bufs["f2"], 3 * len(bufs["f2"]), plans["f2", "ici"][0])
    h = _rows_call(_rms, [(x1, 0, D_MODEL)], [vec["mix_norm"] + (tok_b[0, 0] + tok_c[0, 0])], [(D_MODEL, BF16)], tm=256,
                   name="mix_norm")[0]
    buf_mx = _copies_wait("gather_mx_d2d_wait", buf_mx, ss_b, rs_b, h, plans["mx", "d2d"][1])
    wb.update(zip([w[0] for w in GROUPS["mx"]], buf_mx))
    w_in_all = wb["w_in"]
    w_in_r, w_in_a, w_in_g = w_in_all[:, :RWKV_COLS], w_in_all[:, RWKV_COLS:RWKV_COLS + ATTN_COLS], w_in_all[:, RWKV_COLS + ATTN_COLS:]
    lora = _lora_split(wb["lora"])
    w2, a2, g2 = lora["rwkv_w2"], lora["rwkv_a2"], lora["rwkv_g2"]
    w_brr, w_bra = wb["w_br_rwkv"], wb["w_br_attn"]
    w_o = wb["w_out"].reshape(D_MODEL, D_MODEL)
    z_r = _mm(h, w_in_r, name="in_rwkv")
    z_a = _mm(h, w_in_a, name="in_attn")
    z_g = _mm(h, w_in_g, name="in_gate")

    zs = _shift_fwd(z_r, vec["rwkv_mu"])
    pre_params = [vec["rwkv_w0"], w2, vec["rwkv_a0"], a2, g2, vec["rwkv_k_k"], vec["rwkv_k_a"]]
    def pre_fwd(*t):
        res = _rwkv_pre(*t)
        return res[1], res[2], res[4], res[5], res[6]

    lw, k2, na, kb, gate_r = _rows_call(pre_fwd, [(zs, 0, RWKV_COLS)], pre_params, [(RWKV_DIM, F32)] * 5, tm=256, name="rwkv_pre")
    y_scan, s0s = _wkv_fwd(zs, lw, k2, na, kb)
    buf_f2 = _copies_wait("gather_f2_ici_wait", buf_f2, ss_c, rs_c, y_scan, plans["f2", "ici"][1])
    ss_d, rs_d, buf_f2, tok_d = _copies_start("gather_f2_d2d", buf_f2, 3 * len(buf_f2), plans["f2", "d2d"][0])
    post_params = [vec["rwkv_gn_w"] + tok_d[0, 0], vec["rwkv_gn_b"], vec["rwkv_r_k"]]
    post_rows = [(y_scan, 0, RWKV_DIM), (zs, 0, RWKV_DIM), (k2, 0, RWKV_DIM), (zs, 2, RWKV_DIM), (gate_r, 0, RWKV_DIM)]
    y_rwkv = _rows_call(_rwkv_post, post_rows, post_params, [(RWKV_DIM, BF16)], tm=256, name="rwkv_post")[0]

    def qk_fwd(qt, kt, ct, st, qg, kg):
        return _norm_rope(qt, qg, ct, st), _norm_rope(kt, kg, ct, st)

    qk_rows = [(z_a, 0, ATTN_DIM), (z_a, 1, ATTN_DIM), (cos2, 0, HEAD), (sin2, 0, HEAD)]
    q_rot, k_rot = _rows_call(qk_fwd, qk_rows, [vec["q_norm"], vec["k_norm"]], [(ATTN_DIM, BF16)] * 2, tm=256, name="attn_pre")
    def group(t, g, off=0):
        return t[:, off + g * GROUP_DIM:off + (g + 1) * GROUP_DIM].astype(BF16)

    qkv = [(group(q_rot, g), group(k_rot, g), group(z_a, g, 2 * ATTN_DIM)) for g in range(len(ATTN_DILATIONS))]
    outs, lses = zip(*[_attn_fwd(*qkv[g], d) for g, d in enumerate(ATTN_DILATIONS)])
    comb_rows = [(t, 0, GROUP_DIM) for t in outs + lses]
    y_attn = _rows_call(_attn_combine, comb_rows, [], [(GROUP_DIM, BF16)], tm=256, name="attn_combine")[0]

    br = _mm(y_rwkv, w_brr, name="branch_rwkv")
    ba = _mm(y_attn, w_bra, name="branch_attn")
    merge_rows = [(z_g, 0, D_MODEL), (z_g, 1, D_MODEL), (br, 0, D_MODEL), (ba, 0, D_MODEL)]
    merged = _rows_call(_merge, merge_rows, [], [(D_MODEL, BF16)], tm=256, name="merge")[0]
    x2 = _mm(merged, w_o, res=x1, name="out_proj")
    buf_f2 = _copies_wait("gather_f2_d2d_wait", buf_f2, ss_d, rs_d, x2, plans["f2", "d2d"][1])
    wb.update(zip([w[0] for w in GROUPS["f2"]], buf_f2))
    w_pp, w_pg = wb["ple_w_proj"], wb["ple_w_gate"].reshape(D_MODEL, D_MODEL)
    x3, ffn2_saved = _ffn_fwd(x2, vec["ffn2_norm"], wb["ffn2_w_gate"], wb["ffn2_w_up"], wb["ffn2_w_down"], "ffn2")
    hp = _rows_call(_rms, [(x3, 0, D_MODEL)], [vec["ple_norm"]], [(D_MODEL, BF16)], tm=256, name="ple_norm")[0]
    pg = _mm(hp, w_pg, name="ple_gate")
    pp = _mm(p_tok, w_pp, name="ple_proj")

    def head(x3t, pgt, ppt, tt):
        sg = _sigmoid(pgt)
        err = x3t + sg * ppt - tt
        dx4 = err * (1.0 / D_MODEL)
        loss = 0.5 * jnp.sum(jnp.mean(err * err, axis=-1, keepdims=True), axis=0, keepdims=True)
        return dx4, dx4 * ppt * sg * (1.0 - sg), dx4 * sg, jnp.broadcast_to(loss, (8, 128))

    head_rows = [(x3, 0, D_MODEL), (pg, 0, D_MODEL), (pp, 0, D_MODEL), (tgt, 0, D_MODEL)]
    dx4, dpg, dpp, loss_tile = _rows_call(head, head_rows, [], [(D_MODEL, F32), (D_MODEL, BF16), (D_MODEL, BF16)], [(8, 128)],
                                          tm=256, name="ple_loss")

    c_arr = jnp.reshape(ci, (1,)).astype(jnp.int32)
    mc_arr = jnp.stack([me, ci]).astype(jnp.int32)
    red_groups = {"f2": GROUPS["f2"], "mx": GROUPS["mx"], "f1d": GROUPS["f1"][2:], "f1g": GROUPS["f1"][:2]}
    red = {g: _GroupReduce(g, grp, c_arr, mc_arr) for g, grp in red_groups.items()}
    gw, gs = {}, {}
    gw["ple_w_proj"] = _mm(p_tok, dpp, ta=True, name="ple_proj_dw")
    gw["ple_w_gate"] = _mm(hp, dpg, ta=True, name="ple_gate_dw")
    dhp = _mm(dpg, w_pg, tb=True, name="ple_gate_dx")
    dx3, dx3_bf, gs["ple_norm"] = _norm_bwd(x3, vec["ple_norm"], dhp, dx4, "ple_norm_bwd")
    dx2, dx2_bf, gs["ffn2_norm"], gw["ffn2_w_gate"], gw["ffn2_w_up"], gw["ffn2_w_down"] = _ffn_bwd(
        dx3, dx3_bf, ffn2_saved, vec["ffn2_norm"], wb["ffn2_w_gate"], wb["ffn2_w_up"], wb["ffn2_w_down"], "ffn2")
    gw["ple_w_gate"] = gw["ple_w_gate"].reshape(N_CHIPS, D_MODEL // N_CHIPS, D_MODEL)
    tok = red["f2"].swap_start([gw[w[0]] for w in GROUPS["f2"]])
    gw["w_out"] = _mm(merged, dx2_bf, ta=True, name="out_proj_dw")
    dmerged = _mm(dx2_bf, w_o, tb=True, dep=tok, name="out_proj_dx")

    def merge_bwd(zgr, zga, brt, bat, ct):
        _, vjp = jax.vjp(_merge, zgr, zga, brt, bat)
        d1, d2, d3, d4 = vjp(ct)
        return jnp.concatenate([d1, d2], axis=1), d3, d4

    dz_g, dbr, dba = _rows_call(merge_bwd, merge_rows + [(dmerged, 0, D_MODEL)], [],
                                [(2 * D_MODEL, BF16), (D_MODEL, BF16), (D_MODEL, BF16)], tm=256, name="merge_bwd")
    tok = red["f2"].swap_wait_ici_start(dz_g)
    gw["w_br_rwkv"] = _mm(y_rwkv, dbr, ta=True, name="branch_rwkv_dw")
    gw["w_br_attn"] = _mm(y_attn, dba, ta=True, name="branch_attn_dw")
    dy_rwkv = _mm(dbr, w_brr, tb=True, dep=tok, name="branch_rwkv_dx")
    dy_attn = _mm(dba, w_bra, tb=True, dep=tok, name="branch_attn_dx")

    def comb_bwd(*t):
        _, vjp = jax.vjp(_attn_combine, *t[:6])
        return vjp(t[6])

    dcomb = _rows_call(comb_bwd, comb_rows + [(dy_attn, 0, GROUP_DIM)], [], [(GROUP_DIM, F32)] * 6, tm=256, name="attn_combine_bwd")
    dqs, dks, dvs = zip(*[_attn_bwd(*qkv[g], d, dcomb[g], dcomb[3 + g]) for g, d in enumerate(ATTN_DILATIONS)])

    def qk_bwd(qt, kt, ct, st, *rest):
        dq = jnp.concatenate(rest[0:3], axis=1)
        dk = jnp.concatenate(rest[3:6], axis=1)
        qg, kg = rest[9], rest[10]
        _, vjp = jax.vjp(lambda a_, b_, c_, d_: qk_fwd(a_, b_, ct, st, c_, d_), qt, kt, qg, kg)
        dqt, dkt, dqg, dkg = vjp((dq, dk))
        return jnp.concatenate((dqt, dkt) + tuple(rest[6:9]), axis=1), dqg, dkg

    dz_a, gs["q_norm"], gs["k_norm"] = _rows_call(
        qk_bwd, qk_rows + [(t, 0, GROUP_DIM) for t in dqs + dks + dvs], [vec["q_norm"], vec["k_norm"]],
        [(ATTN_COLS, BF16)], [(1, HEAD), (1, HEAD)], tm=256, name="attn_pre_bwd")
    tok = red["f2"].ici_wait_join_start(dz_a)

    def post_bwd(*t):
        _, vjp = jax.vjp(_rwkv_post, *t[:5], *t[6:])
        return vjp(t[5])

    dy_scan, dr_post, dk2_post, dv_post, dgate_r, gs["rwkv_gn_w"], gs["rwkv_gn_b"], gs["rwkv_r_k"] = _rows_call(
        post_bwd, post_rows + [(dy_rwkv, 0, RWKV_DIM)], post_params, [(RWKV_DIM, F32)] * 5, [(1, RWKV_DIM)] * 3,
        tm=256, name="rwkv_post_bwd", dep=tok)
    grad_big = red["f2"].join_wait(dy_scan)
    dr_s, dlw, dk2_s, dv_s, dna, dkb = _wkv_bwd(zs, lw, k2, na, kb, s0s, dy_scan)

    def pre_bwd(zt, c_r1, c_r2, c_lw, c_k1, c_k2, c_v1, c_v2, c_a, c_b, c_g, *params):
        _, vjp = jax.vjp(_rwkv_pre, zt, *params)
        return vjp((c_r1 + c_r2, c_lw, c_k1 + c_k2, c_v1 + c_v2, c_a, c_b, c_g))

    pre_cts = [dr_s, dr_post, dlw, dk2_s, dk2_post, dv_s, dv_post, dna, dkb, dgate_r]
    dzs, gs["rwkv_w0"], g_w2, gs["rwkv_a0"], g_a2, g_g2, gs["rwkv_k_k"], gs["rwkv_k_a"] = _rows_call(
        pre_bwd, [(zs, 0, RWKV_COLS)] + [(t, 0, RWKV_DIM) for t in pre_cts], pre_params, [(RWKV_COLS, F32)],
        [q.shape for q in pre_params], tm=256, name="rwkv_pre_bwd")
    dz_r, gs["rwkv_mu"] = _shift_bwd(z_r, vec["rwkv_mu"], dzs)

    g_w_in = jnp.concatenate([_mm(h, dz_r, ta=True, name="in_rwkv_dw"), _mm(h, dz_a, ta=True, name="in_attn_dw"),
                              _mm(h, dz_g, ta=True, name="in_gate_dw")], axis=1)
    gw["w_in"], gw["lora"] = g_w_in, jnp.concatenate([g_w2, g_a2, g_g2], axis=0)
    gw["w_out"] = gw["w_out"].reshape(N_CHIPS, D_MODEL // N_CHIPS, D_MODEL)
    tok = red["mx"].swap_start([gw[w[0]] for w in GROUPS["mx"]])
    dh = _mm(dz_r, w_in_r, tb=True, dep=tok, name="in_rwkv_dx")
    dh = _mm(dz_a, w_in_a, tb=True, res=dh, name="in_attn_dx")
    dh = _mm(dz_g, w_in_g, tb=True, res=dh, name="in_gate_dx")
    dx1, dx1_bf, gs["mix_norm"] = _norm_bwd(x1, vec["mix_norm"], dh, dx2, "mix_norm_bwd")
    tok = red["mx"].swap_wait_ici_start(dx1_bf)
    hooks = {"down": lambda d_wd: red["f1d"].swap_start([d_wd], after=tok),
             "mid": lambda dgate: red["f1d"].swap_wait_ici_start(dgate),
             "end": lambda dh_: red["f1d"].ici_wait_join_start(dh_)}
    dx0, _, gs["ffn1_norm"], gw["ffn1_w_gate"], gw["ffn1_w_up"], gw["ffn1_w_down"] = _ffn_bwd(
        dx1, dx1_bf, ffn1_saved, vec["ffn1_norm"], wb["ffn1_w_gate"], wb["ffn1_w_up"], wb["ffn1_w_down"], "ffn1", hooks=hooks)
    tok_mx = red["mx"].ici_wait_join_start(dx0)
    tok = red["f1g"].swap_start([gw["ffn1_w_gate"], gw["ffn1_w_up"]], after=tok_mx)

    grads, deltas, new_m, new_v = {}, {}, {}, {}

    def update(names, dep):
        last = None
        for n in names:
            g2d = grad_big[n]
            if n == "lora":
                w_, m_, v_ = (_lora_stack({k: t[k][0] for k, _ in LORA}) for t in (wts, mom_m, mom_v))
            else:
                w_, m_, v_ = laid(wts, n), laid(mom_m, n), laid(mom_v, n)
            res = (g2d,) + tuple(_adamw(w_, g2d, m_, v_, name=f"adamw_{n}", dep=dep))
            dep = last = res[1]
            for store, val in zip((grads, deltas, new_m, new_v), res):
                if n == "lora":
                    store.update({k: t[None] for k, t in _lora_split(val).items()})
                else:
                    store[n] = (jnp.transpose(val) if n in TRANSPOSED else val)[None]
        return last

    last = update([w[0] for w in GROUPS["f2"]], tok)
    tok = red["f1g"].swap_wait_ici_start(last)
    grad_big.update(red["mx"].join_wait(last))
    grad_big.update(red["f1d"].join_wait(last))
    last = update([w[0] for w in GROUPS["mx"]] + ["ffn1_w_down"], tok)

    flat = jnp.concatenate([gs[n].reshape(-1) for n, _ in SMALL] + [loss_tile[0, 0:1]])
    small_buf = jnp.pad(flat, (0, SMALL_ROWS * PACK_COLS - flat.shape[0])).reshape(SMALL_ROWS, PACK_COLS)
    small_sum = _all_reduce_small(small_buf)
    n_small = sum(sz for _, sz in SMALL)
    loss = small_sum.reshape(-1)[n_small]
    grad_small = _unpack_small(small_sum, {n: wts[n].shape for n, _ in SMALL})
    d_s, m_s, v_s = _adamw(_pack_small(wts), small_sum, _pack_small(mom_m), _pack_small(mom_v), name="adamw_small", dep=last)
    shapes = {n: wts[n].shape for n, _ in SMALL}
    d_s, m_s, v_s = _unpack_small(d_s, shapes), _unpack_small(m_s, shapes), _unpack_small(v_s, shapes)
    for n, _ in SMALL:
        grads[n], deltas[n], new_m[n], new_v[n] = grad_small[n], d_s[n], m_s[n], v_s[n]

    tok = red["f1g"].ici_wait_join_start(m_s["ffn1_norm"])
    grad_big.update(red["f1g"].join_wait(tok))
    update(["ffn1_w_gate", "ffn1_w_up"], None)

    return (loss, dx0[None], *[grads[n] for n in WEIGHTS], *[deltas[n] for n in WEIGHTS],
            *[new_m[n] for n in WEIGHTS], *[new_v[n] for n in WEIGHTS])
```

```python
import functools

import jax
import jax.numpy as jnp
from jax import lax
from jax.experimental import pallas as pl
from jax.experimental.pallas import tpu as pltpu

F32, BF16 = jnp.float32, jnp.bfloat16
HI = lax.Precision.HIGHEST
MESH = pl.DeviceIdType.MESH
SDS = jax.ShapeDtypeStruct

D_MODEL = 1024
HEAD = 64
RWKV_HEADS = 8
RWKV_DIM = RWKV_HEADS * HEAD
DECAY_LORA, ICLR_LORA, GATE_LORA = 64, 64, 128
GN_EPS = 64e-5
RMS_EPS = 1e-6
ATTN_DILATIONS = (1, 4, 16)
BAND = 128
ATTN_DIM = 768
GROUP_DIM = 256
ROPE_THETA = 10000.0
NEG_INF = -1e30
RWKV_COLS = 3 * RWKV_DIM + DECAY_LORA + ICLR_LORA + GATE_LORA
ATTN_COLS = 3 * ATTN_DIM
ADAM_LR, ADAM_B1, ADAM_B2, ADAM_EPS, ADAM_WD, ADAM_STEP = 0.001, 0.9, 0.999, 1e-08, 0.01, 10

WKV_CHUNK = 64
WKV_HEADS_PER_STEP = 8
N_CHIPS = 4
PACK_COLS = 1024
VMEM_LIMIT = 48 * 1024 * 1024

TRANSPOSED = ("ffn1_w_gate", "ffn1_w_up", "ffn2_w_gate", "ffn2_w_up")
LORA = (("rwkv_w2", 64), ("rwkv_a2", 64), ("rwkv_g2", 128))
_FFN1 = (("ffn1_w_gate", "blk", 704, 1024), ("ffn1_w_up", "blk", 704, 1024), ("ffn1_w_down", "blk", 704, 1024))
_FFN2 = (("ffn2_w_gate", "blk", 704, 1024), ("ffn2_w_up", "blk", 704, 1024), ("ffn2_w_down", "blk", 704, 1024))
_IN = (("w_in", "col", 1024, 1536), ("lora", "col", 256, 128))
_BRANCH = (("w_br_rwkv", "col", 512, 256), ("w_br_attn", "col", 256, 256), ("w_out", "blk", 256, 1024))
_PLE = (("ple_w_gate", "blk", 256, 1024), ("ple_w_proj", "col", 256, 256))
GROUPS = {"f1": _FFN1, "mx": _IN, "f2": _BRANCH + _FFN2 + _PLE}
REDUCE_GROUPS = {"f2": _FFN2 + _PLE, "mx": _IN + _BRANCH, "f1d": _FFN1[2:], "f1g": _FFN1[:2]}
SMALL = (
    ("ffn1_norm", 1024), ("mix_norm", 1024), ("ffn2_norm", 1024), ("ple_norm", 1024), ("rwkv_mu", 1792),
    ("rwkv_w0", 512), ("rwkv_a0", 512), ("rwkv_k_k", 512), ("rwkv_k_a", 512), ("rwkv_r_k", 512),
    ("rwkv_gn_w", 512), ("rwkv_gn_b", 512), ("q_norm", 64), ("k_norm", 64),
)
SMALL_ROWS = 16
WEIGHTS = (
    "ffn1_norm", "ffn1_w_gate", "ffn1_w_up", "ffn1_w_down", "mix_norm", "w_in", "rwkv_mu", "rwkv_w0", "rwkv_w2",
    "rwkv_a0", "rwkv_a2", "rwkv_g2", "rwkv_k_k", "rwkv_k_a", "rwkv_r_k", "rwkv_gn_w", "rwkv_gn_b", "q_norm", "k_norm",
    "w_br_rwkv", "w_br_attn", "w_out", "ffn2_norm", "ffn2_w_gate", "ffn2_w_up", "ffn2_w_down", "ple_norm",
    "ple_w_gate", "ple_w_proj",
)


def _row_tile(n, most=704):
    for t in range(most - most % 16, 0, -16):
        if n % t == 0:
            return t
    return n


def _pick(n, cands):
    for c in cands:
        if n % c == 0:
            return c
    return n


def _mm(a, b, *, ta=False, tb=False, sum_blocks=False, out_dtype=F32, res=None, alpha=1.0, dep=None, name):
    flat = a.ndim == 2 and b.ndim == 2
    a3 = a if a.ndim == 3 else a[None]
    b3 = b if b.ndim == 3 else b[None]
    na, nbb = a3.shape[0], b3.shape[0]
    nblk = max(na, nbb)
    kdim, m = (a3.shape[1], a3.shape[2]) if ta else (a3.shape[2], a3.shape[1])
    n = b3.shape[1] if tb else b3.shape[2]
    assert (b3.shape[2] if tb else b3.shape[1]) == kdim
    tm = _pick(m, (1024, 512, 256, 128))
    tn = _pick(n, (1024, 896, 768, 512, 256, 128))
    tk = kdim if kdim <= 2304 else _pick(kdim, (1024, 512, 256, 128))
    nk = kdim // tk
    direct = nk == 1 and not sum_blocks

    if sum_blocks:
        grid = (m // tm, n // tn, nblk, nk)

        def ids(i, c, j, k):
            return i, c, j, k
    else:
        grid = (nblk, m // tm, n // tn, nk)

        def ids(j, i, c, k):
            return i, c, j, k

    def amap(*g):
        i, c, j, k = ids(*g)
        jj = j if na > 1 else 0
        return (jj, k, i) if ta else (jj, i, k)

    def bmap(*g):
        i, c, j, k = ids(*g)
        jj = j if nbb > 1 else 0
        return (jj, c, k) if tb else (jj, k, c)

    if sum_blocks:
        oshape, oblk = (m, n), (tm, tn)

        def omap(*g):
            i, c, j, k = ids(*g)
            return i, c
    else:
        oshape, oblk = (nblk, m, n), (1, tm, tn)

        def omap(*g):
            i, c, j, k = ids(*g)
            return j, i, c

    dn = (((0 if ta else 1,), (1 if tb else 0,)), ((), ()))
    has_res = res is not None

    def body(*refs):
        refs = list(refs)
        acc = None if direct else refs.pop()
        o_ref = refs.pop()
        a_ref, b_ref = refs[0], refs[1]
        r_ref = refs[2] if has_res else None

        def finish(v):
            if alpha != 1.0:
                v = v * alpha
            if has_res:
                v = v + r_ref[...].reshape(v.shape).astype(F32)
            o_ref[...] = v.reshape(o_ref.shape).astype(o_ref.dtype)

        if direct:
            finish(lax.dot_general(a_ref[0].astype(BF16), b_ref[0].astype(BF16), dn, preferred_element_type=F32))
            return
        k = pl.program_id(3)
        if sum_blocks:
            j = pl.program_id(2)
            first = jnp.logical_and(j == 0, k == 0)
            last = jnp.logical_and(j == nblk - 1, k == nk - 1)
        else:
            first, last = k == 0, k == nk - 1

        @pl.when(first)
        def _():
            acc[...] = jnp.zeros_like(acc)

        acc[...] += lax.dot_general(a_ref[0].astype(BF16), b_ref[0].astype(BF16), dn, preferred_element_type=F32)

        @pl.when(last)
        def _():
            finish(acc[...])

    in_specs = [pl.BlockSpec((1, tk, tm) if ta else (1, tm, tk), amap), pl.BlockSpec((1, tn, tk) if tb else (1, tk, tn), bmap)]
    args = [a3, b3]
    if has_res:
        res3 = res if (sum_blocks or res.ndim == 3) else res[None]
        in_specs.append(pl.BlockSpec(oblk, omap))
        args.append(res3)
    if dep is not None:
        in_specs.append(pl.BlockSpec(memory_space=pl.ANY))
        args.append(dep)
    out = pl.pallas_call(
        body,
        name=name,
        grid=grid,
        in_specs=in_specs,
        out_specs=pl.BlockSpec(oblk, omap),
        out_shape=SDS(oshape, out_dtype),
        scratch_shapes=[] if direct else [pltpu.VMEM((tm, tn), F32)],
        compiler_params=pltpu.CompilerParams(
            dimension_semantics=("parallel", "parallel", "arbitrary", "arbitrary") if sum_blocks
            else ("parallel", "parallel", "parallel", "arbitrary"),
            vmem_limit_bytes=VMEM_LIMIT),
    )(*args)
    if flat and not sum_blocks:
        out = out[0]
    return out


def _rows_call(f, rows, params, outs, accs=(), *, tm, name, dep=None):
    s = rows[0][0].shape[0]
    nr, npar, no = len(rows), len(params), len(outs)
    nin = nr + npar + (0 if dep is None else 1)
    in_specs = [pl.BlockSpec((tm, w), functools.partial(lambda i, cb: (i, cb), cb=cb)) for (_, cb, w) in rows]
    in_specs += [pl.BlockSpec(p.shape, functools.partial(lambda i, nd: (0,) * nd, nd=p.ndim)) for p in params]
    if dep is not None:
        in_specs.append(pl.BlockSpec(memory_space=pl.ANY))
    out_shape = [SDS((s, w), dt) for (w, dt) in outs] + [SDS(tuple(sh), F32) for sh in accs]
    out_specs = [pl.BlockSpec((tm, w), lambda i: (i, 0)) for (w, _) in outs]
    out_specs += [pl.BlockSpec(tuple(sh), functools.partial(lambda i, nd: (0,) * nd, nd=len(sh))) for sh in accs]

    def body(*refs):
        rin, pin = refs[:nr], refs[nr:nr + npar]
        oo, ao = refs[nin:nin + no], refs[nin + no:]
        res = f(*[r[...] for r in rin], *[p[...] for p in pin])
        if not isinstance(res, (tuple, list)):
            res = (res,)
        for o_ref, v in zip(oo, res[:no]):
            o_ref[...] = v.astype(o_ref.dtype)
        i = pl.program_id(0)
        for a_ref, v in zip(ao, res[no:]):
            @pl.when(i == 0)
            def _():
                a_ref[...] = jnp.zeros_like(a_ref)

            a_ref[...] += v.reshape(a_ref.shape)

    res = pl.pallas_call(
        body,
        name=name,
        grid=(s // tm,),
        in_specs=in_specs,
        out_specs=out_specs,
        out_shape=out_shape,
        compiler_params=pltpu.CompilerParams(dimension_semantics=("arbitrary",), vmem_limit_bytes=VMEM_LIMIT),
    )(*[r[0] for r in rows], *params, *([] if dep is None else [dep]))
    return res


def _mmv(a, b, mode):
    ca = 0 if mode[0] == "t" else 1
    cb = 1 if mode[1] == "t" else 0
    return lax.dot_general(a.astype(BF16), b.astype(BF16), (((ca,), (cb,)), ((), ())), preferred_element_type=F32)


@functools.partial(jax.custom_vjp, nondiff_argnums=(2,))
def _bdot(a, b, mode):
    return _mmv(a, b, mode)


def _bdot_fwd(a, b, mode):
    return _mmv(a, b, mode), (a, b)


def _bdot_bwd(mode, saved, g):
    a, b = saved
    if mode == "nn":
        return _mmv(g, b, "nt"), _mmv(a, g, "tn")
    if mode == "nt":
        return _mmv(g, b, "nn"), _mmv(g, a, "tn")
    return _mmv(b, g, "nt"), _mmv(a, g, "nn")


_bdot.defvjp(_bdot_fwd, _bdot_bwd)


def _hdot(a, b, mode="nn", precision=HI):
    ca = 0 if mode[0] == "t" else 1
    cb = 1 if mode[1] == "t" else 0
    return lax.dot_general(a, b, (((ca,), (cb,)), ((), ())), precision=precision, preferred_element_type=F32)


def _segsum(x):
    c = x.shape[-1]
    blk = min(c, 256)
    r = lax.broadcasted_iota(jnp.int32, (blk, blk), 0) >> 6
    q = lax.broadcasted_iota(jnp.int32, (blk, blk), 1) >> 6
    ones = jnp.where(r == q, 1.0, 0.0).astype(F32)
    parts = [_hdot(x[:, i:i + blk], ones, precision=lax.Precision.HIGH) for i in range(0, c, blk)]
    return parts[0] if len(parts) == 1 else jnp.concatenate(parts, axis=1)


def _sigmoid(x):
    return jax.nn.sigmoid(x)


def _softplus(x):
    return jnp.maximum(x, 0.0) + jnp.log(1.0 + jnp.exp(-jnp.abs(x)))


def _rms(x, gain):
    return x * lax.rsqrt(jnp.mean(x * x, axis=-1, keepdims=True) + RMS_EPS) * gain


def _swiglu_act(gate, up):
    return gate * _sigmoid(gate) * up


def _rwkv_pre(zs, w0, w2, a0, a2, g2, k_k, k_a):
    r, k, v = zs[:, 0:512], zs[:, 512:1024], zs[:, 1024:1536]
    lora = zs[:, 1536:1792]
    wd, ad, gd = lora[:, 0:64], lora[:, 64:128], lora[:, 128:256]
    w = -_softplus(-(w0 + _bdot(jnp.tanh(wd), w2, "nn"))) - 0.5
    a = _sigmoid(a0 + _bdot(ad, a2, "nn"))
    g = _bdot(_sigmoid(gd), g2, "nn")
    kk = k * k_k
    kk = kk * lax.rsqrt(jnp.maximum(_segsum(kk * kk), 1e-24))
    k2 = k * (1.0 + (a - 1.0) * k_a)
    return r, -jnp.exp(w), k2, v, -kk, kk * a, g


def _rwkv_post(y, r, k2, v, g, gn_w, gn_b, r_k):
    mean = _segsum(y) * (1.0 / HEAD)
    yc = y - mean
    var = _segsum(yc * yc) * (1.0 / HEAD)
    yn = yc * lax.rsqrt(var + GN_EPS) * gn_w + gn_b
    bonus = _segsum(r * k2 * r_k) * v
    return (yn + bonus) * g


def _swap_halves(x):
    lane = lax.broadcasted_iota(jnp.int32, x.shape, 1)
    return jnp.where((lane & 32) == 0, jnp.roll(x, -32, axis=1), jnp.roll(x, 32, axis=1))


def _norm_rope(x, gain, cos, sin):
    heads = x.shape[1] // HEAD
    def rep(t):
        return jnp.concatenate([t] * heads, axis=1)

    xn = x * lax.rsqrt(_segsum(x * x) * (1.0 / HEAD) + RMS_EPS) * rep(gain)
    return xn * rep(cos) + _swap_halves(xn) * rep(sin)


def _attn_combine(o0, o1, o2, l0, l1, l2):
    m = jnp.maximum(jnp.maximum(l0, l1), l2)
    e0, e1, e2 = jnp.exp(l0 - m), jnp.exp(l1 - m), jnp.exp(l2 - m)
    return (e0 * o0 + e1 * o1 + e2 * o2) / (e0 + e1 + e2)


def _merge(zgr, zga, br, ba):
    return _sigmoid(zgr) * br + _sigmoid(zga) * ba


def _attn_block(q, kp, kc, vp, vc, has_prev):
    iq = lax.broadcasted_iota(jnp.int32, (1, BAND, BAND), 1)
    ik = lax.broadcasted_iota(jnp.int32, (1, BAND, BAND), 2)
    s_c = jnp.where(iq >= ik, _bdotb(q, kc, "nt") * (HEAD ** -0.5), NEG_INF)
    s_p = jnp.where(jnp.logical_and(iq <= ik, has_prev), _bdotb(q, kp, "nt") * (HEAD ** -0.5), NEG_INF)
    m = lax.stop_gradient(jnp.maximum(jnp.max(s_c, axis=-1, keepdims=True), jnp.max(s_p, axis=-1, keepdims=True)))
    e_c, e_p = jnp.exp(s_c - m), jnp.exp(s_p - m)
    l = jnp.sum(e_c, axis=-1, keepdims=True) + jnp.sum(e_p, axis=-1, keepdims=True)
    o = (_bdotb(e_c, vc) + _bdotb(e_p, vp)) / l
    return o, jnp.broadcast_to(m + jnp.log(l), o.shape)


def _mmb(a, b, cb):
    return lax.dot_general(a.astype(BF16), b.astype(BF16), (((2,), (cb,)), ((0,), (0,))), preferred_element_type=F32)


@functools.partial(jax.custom_vjp, nondiff_argnums=(2,))
def _bdotb1(a, b, cb):
    return _mmb(a, b, cb)


def _bdotb1_fwd(a, b, cb):
    return _mmb(a, b, cb), (a, b)


def _bdotb1_bwd(cb, saved, g):
    a, b = saved
    if cb == 1:
        return _mmb(g, b, 2), _mmb(jnp.swapaxes(a, 1, 2), g, 1)
    return _mmb(g, b, 1), _mmb(jnp.swapaxes(g, 1, 2), a, 1)


_bdotb1.defvjp(_bdotb1_fwd, _bdotb1_bwd)


def _bdotb(a, b, mode="nn", precision=None):
    if mode[0] == "t":
        a = jnp.swapaxes(a, 1, 2)
    cb = 2 if mode[1] == "t" else 1
    if precision is None:
        return _bdotb1(a, b, cb)
    return lax.dot_general(a, b, (((2,), (cb,)), ((0,), (0,))), precision=precision, preferred_element_type=F32)


def _tri_inv_levels(a):
    t = a.shape[-1]
    row = lax.broadcasted_iota(jnp.int32, (1, t, t), 1)
    col = lax.broadcasted_iota(jnp.int32, (1, t, t), 2)
    x = jnp.where(row == col, 1.0, 0.0).astype(F32) + jnp.where(jnp.logical_and(row == col + 1, (row & 1) == 1), a, 0.0)
    sh = 1
    while (1 << sh) < t:
        m = jnp.logical_and((row >> sh) == (col >> sh) + 1, (row >> (sh + 1)) == (col >> (sh + 1)))
        x = x + _bdotb(_bdotb(x, jnp.where(m, a, 0.0), precision=lax.Precision.HIGH), x, precision=lax.Precision.HIGH)
        sh += 1
    return x


@jax.custom_vjp
def _tri_inv(a):
    return _tri_inv_levels(a)


def _tri_inv_fwd(a):
    x = _tri_inv_levels(a)
    return x, x


def _tri_inv_bwd(x, g):
    xt = jnp.swapaxes(x, 1, 2)
    return (_bdotb(_bdotb(xt, g, precision=lax.Precision.HIGH), xt, precision=lax.Precision.HIGH),)


_tri_inv.defvjp(_tri_inv_fwd, _tri_inv_bwd)


def _wkv_chunk(s0, r, lw, k, v, a, b):
    nh, t, _ = r.shape
    row = lax.broadcasted_iota(jnp.int32, (1, t, t), 1)
    col = lax.broadcasted_iota(jnp.int32, (1, t, t), 2)
    incl, strict = row >= col, row > col
    ones = jnp.broadcast_to(jnp.where(incl, 1.0, 0.0).astype(F32), (nh, t, t))
    cum = _bdotb(ones, lw, precision=HI)
    c_end = cum[:, t - 1:t, :]
    e_in, e_ex, e_inv = jnp.exp(cum), jnp.exp(cum - lw), jnp.exp(-cum)
    at, rt, bt, kt = a * e_ex, r * e_in, b * e_inv, k * e_inv
    a_ab = jnp.where(strict, _bdotb(at, bt, "nt"), 0.0)
    a_ak = jnp.where(strict, _bdotb(at, kt, "nt"), 0.0)
    u = _bdotb(_tri_inv(a_ab), _bdotb(at, s0, "nt") + _bdotb(a_ak, v))
    y = (_bdotb(rt, s0, "nt") + _bdotb(jnp.where(incl, _bdotb(rt, bt, "nt"), 0.0), u)
         + _bdotb(jnp.where(incl, _bdotb(rt, kt, "nt"), 0.0), v))
    w_end = jnp.exp(c_end - cum)
    s1 = s0 * jnp.exp(c_end) + _bdotb(u, b * w_end, "tn") + _bdotb(v, k * w_end, "tn")
    return y, s1


def _shift_fwd(z, mu):
    s, c = z.shape
    tc = 256

    def body(z_ref, mu_ref, o_ref):
        zz = z_ref[...]
        row = lax.broadcasted_iota(jnp.int32, zz.shape, 0)
        prev = jnp.where(row == 0, 0.0, pltpu.roll(zz, 1, 0))
        o_ref[...] = zz + (prev - zz) * mu_ref[...]

    return pl.pallas_call(
        body, name="shift_fwd", grid=(c // tc,),
        in_specs=[pl.BlockSpec((s, tc), lambda j: (0, j)), pl.BlockSpec((1, tc), lambda j: (0, j))],
        out_specs=pl.BlockSpec((s, tc), lambda j: (0, j)), out_shape=SDS((s, c), F32),
        compiler_params=pltpu.CompilerParams(dimension_semantics=("parallel",), vmem_limit_bytes=VMEM_LIMIT),
    )(z, mu)


def _shift_bwd(z, mu, dzs):
    s, c = z.shape
    tc = 256

    def body(z_ref, mu_ref, d_ref, dz_ref, dmu_ref):
        zz, d, m = z_ref[...], d_ref[...], mu_ref[...]
        row = lax.broadcasted_iota(jnp.int32, zz.shape, 0)
        prev = jnp.where(row == 0, 0.0, pltpu.roll(zz, 1, 0))
        t = d * m
        nxt = jnp.where(row == s - 1, 0.0, pltpu.roll(t, s - 1, 0))
        dz_ref[...] = (d - t + nxt).astype(dz_ref.dtype)
        dmu_ref[...] = jnp.sum(d * (prev - zz), axis=0, keepdims=True)

    return pl.pallas_call(
        body, name="shift_bwd", grid=(c // tc,),
        in_specs=[pl.BlockSpec((s, tc), lambda j: (0, j)), pl.BlockSpec((1, tc), lambda j: (0, j)),
                  pl.BlockSpec((s, tc), lambda j: (0, j))],
        out_specs=[pl.BlockSpec((s, tc), lambda j: (0, j)), pl.BlockSpec((1, tc), lambda j: (0, j))],
        out_shape=[SDS((s, c), BF16), SDS((1, c), F32)],
        compiler_params=pltpu.CompilerParams(dimension_semantics=("parallel",), vmem_limit_bytes=VMEM_LIMIT),
    )(z, mu, dzs)


def _heads(x, nh):
    return jnp.stack([x[:, h * HEAD:(h + 1) * HEAD] for h in range(nh)], axis=0)


def _unheads(x):
    return jnp.concatenate([x[h] for h in range(x.shape[0])], axis=1)


def _wkv_fwd(zs, lw, k2, na, b):
    s = lw.shape[0]
    t, hb = WKV_CHUNK, WKV_HEADS_PER_STEP
    w = hb * HEAD
    nc, ng = s // t, RWKV_HEADS // hb

    def body(r_ref, v_ref, lw_ref, k_ref, a_ref, b_ref, y_ref, s0_ref, state):
        @pl.when(pl.program_id(1) == 0)
        def _():
            state[...] = jnp.zeros_like(state)

        s0 = state[...]
        s0_ref[0] = s0
        y, s1 = _wkv_chunk(s0, *[_heads(t_ref[...], hb) for t_ref in (r_ref, lw_ref, k_ref, v_ref, a_ref, b_ref)])
        y_ref[...] = _unheads(y)
        state[...] = s1

    def col(off):
        return pl.BlockSpec((t, w), functools.partial(lambda g, i, off: (i, g + off), off=off))

    return pl.pallas_call(
        body, name="wkv_fwd", grid=(ng, nc),
        in_specs=[col(0), col(2 * ng), col(0), col(0), col(0), col(0)],
        out_specs=[col(0), pl.BlockSpec((1, hb, HEAD, HEAD), lambda g, i: (i, g, 0, 0))],
        out_shape=[SDS((s, RWKV_DIM), F32), SDS((nc, RWKV_HEADS, HEAD, HEAD), F32)],
        scratch_shapes=[pltpu.VMEM((hb, HEAD, HEAD), F32)],
        compiler_params=pltpu.CompilerParams(dimension_semantics=("parallel", "arbitrary"), vmem_limit_bytes=VMEM_LIMIT),
    )(zs, zs, lw, k2, na, b)


def _wkv_bwd(zs, lw, k2, na, b, s0s, dy):
    s = lw.shape[0]
    t, hb = WKV_CHUNK, WKV_HEADS_PER_STEP
    w = hb * HEAD
    nc, ng = s // t, RWKV_HEADS // hb

    def body(r_ref, v_ref, lw_ref, k_ref, a_ref, b_ref, s0_ref, dy_ref, dr_ref, dlw_ref, dk_ref, dv_ref, da_ref, db_ref, dstate):
        @pl.when(pl.program_id(1) == 0)
        def _():
            dstate[...] = jnp.zeros_like(dstate)

        _, vjp = jax.vjp(_wkv_chunk, s0_ref[0], *[_heads(t_ref[...], hb) for t_ref in (r_ref, lw_ref, k_ref, v_ref, a_ref, b_ref)])
        grads = vjp((_heads(dy_ref[...], hb), dstate[...]))
        dstate[...] = grads[0]
        for o_ref, gval in zip((dr_ref, dlw_ref, dk_ref, dv_ref, da_ref, db_ref), grads[1:]):
            o_ref[...] = _unheads(gval)

    def col(off):
        return pl.BlockSpec((t, w), functools.partial(lambda g, i, off: (nc - 1 - i, g + off), off=off))

    return pl.pallas_call(
        body, name="wkv_bwd", grid=(ng, nc),
        in_specs=[col(0), col(2 * ng), col(0), col(0), col(0), col(0),
                  pl.BlockSpec((1, hb, HEAD, HEAD), lambda g, i: (nc - 1 - i, g, 0, 0)), col(0)],
        out_specs=[col(0)] * 6,
        out_shape=[SDS((s, RWKV_DIM), F32)] * 6,
        scratch_shapes=[pltpu.VMEM((hb, HEAD, HEAD), F32)],
        compiler_params=pltpu.CompilerParams(dimension_semantics=("parallel", "arbitrary"), vmem_limit_bytes=VMEM_LIMIT),
    )(zs, zs, lw, k2, na, b, s0s, dy)


def _attn_fwd(q, k, v, d):
    s = q.shape[0]
    l = s // d
    nb = l // BAND
    assert nb * BAND == l
    qv, kv, vv = (t.reshape(l, d * GROUP_DIM) for t in (q, k, v))
    nh = GROUP_DIM // HEAD

    def body(q_ref, kp_ref, kc_ref, vp_ref, vc_ref, o_ref, l_ref):
        has_prev = pl.program_id(1) > 0
        o, lse = _attn_block(*[_heads(t_ref[...].astype(F32), nh) for t_ref in (q_ref, kp_ref, kc_ref, vp_ref, vc_ref)], has_prev)
        o_ref[...] = _unheads(o)
        l_ref[...] = _unheads(lse)

    cur = pl.BlockSpec((BAND, GROUP_DIM), lambda rho, i: (i, rho))
    prev = pl.BlockSpec((BAND, GROUP_DIM), lambda rho, i: (jnp.maximum(i - 1, 0), rho))
    o, lse = pl.pallas_call(
        body, name=f"attn_fwd_d{d}", grid=(d, nb),
        in_specs=[cur, prev, cur, prev, cur], out_specs=[cur, cur],
        out_shape=[SDS((l, d * GROUP_DIM), F32), SDS((l, d * GROUP_DIM), F32)],
        compiler_params=pltpu.CompilerParams(dimension_semantics=("parallel", "arbitrary"), vmem_limit_bytes=VMEM_LIMIT),
    )(qv, kv, kv, vv, vv)
    return o.reshape(s, GROUP_DIM), lse.reshape(s, GROUP_DIM)


def _attn_bwd(q, k, v, d, do, dlse):
    s = q.shape[0]
    l = s // d
    nb = l // BAND
    qv, kv, vv, dov, dlv = (t.reshape(l, d * GROUP_DIM) for t in (q, k, v, do, dlse))
    nh = GROUP_DIM // HEAD

    def body(q_ref, kp_ref, kc_ref, vp_ref, vc_ref, do_ref, dl_ref, dq_ref, dk_ref, dv_ref, ck, cv):
        step = pl.program_id(1)
        has_prev = step < nb - 1

        @pl.when(step == 0)
        def _():
            ck[...] = jnp.zeros_like(ck)
            cv[...] = jnp.zeros_like(cv)

        _, vjp = jax.vjp(functools.partial(_attn_block, has_prev=has_prev),
                         *[_heads(t_ref[...].astype(F32), nh) for t_ref in (q_ref, kp_ref, kc_ref, vp_ref, vc_ref)])
        dq, dkp, dkc, dvp, dvc = vjp((_heads(do_ref[...], nh), _heads(dl_ref[...], nh)))
        dq_ref[...] = _unheads(dq)
        dk_ref[...] = _unheads(dkc) + ck[...]
        dv_ref[...] = _unheads(dvc) + cv[...]
        ck[...] = _unheads(dkp)
        cv[...] = _unheads(dvp)

    cur = pl.BlockSpec((BAND, GROUP_DIM), lambda rho, i: (nb - 1 - i, rho))
    prev = pl.BlockSpec((BAND, GROUP_DIM), lambda rho, i: (jnp.maximum(nb - 2 - i, 0), rho))
    dq, dk, dv = pl.pallas_call(
        body, name=f"attn_bwd_d{d}", grid=(d, nb),
        in_specs=[cur, prev, cur, prev, cur, cur, cur], out_specs=[cur] * 3,
        out_shape=[SDS((l, d * GROUP_DIM), F32)] * 3,
        scratch_shapes=[pltpu.VMEM((BAND, GROUP_DIM), F32), pltpu.VMEM((BAND, GROUP_DIM), F32)],
        compiler_params=pltpu.CompilerParams(dimension_semantics=("parallel", "arbitrary"), vmem_limit_bytes=VMEM_LIMIT),
    )(qv, kv, kv, vv, vv, dov, dlv)
    return dq.reshape(s, GROUP_DIM), dk.reshape(s, GROUP_DIM), dv.reshape(s, GROUP_DIM)


def _coords():
    return lax.axis_index("x"), lax.axis_index("y"), lax.axis_index("c")


_CHIP_FLIPS = ((1, 0), (0, 1), (1, 1))


def _flip(v, f):
    return 1 - v if f else v


def _form(kind, r, c):
    return (N_CHIPS, r, c) if kind == "blk" else (r, N_CHIPS * c)


def _slot(ref, kind, j, rows, c):
    if kind == "blk":
        return ref.at[j] if rows is None else ref.at[j, rows]
    cols = pl.ds(pl.multiple_of(j * c, 128), c)
    return ref.at[:, cols] if rows is None else ref.at[rows, cols]


def _half(r, which, align):
    return pl.ds(pl.multiple_of(which * (r // 2), align), r // 2)


def _rcopy(src, dst, send_sems, recv_sems, kk, dev):
    return pltpu.make_async_remote_copy(src_ref=src, dst_ref=dst, send_sem=send_sems.at[kk], recv_sem=recv_sems.at[kk],
                                        device_id=dev, device_id_type=MESH)


def _gather_plan(specs, step):
    def copies(refs, ss, rs, received):
        x, y, c = _coords()
        out = []
        for w, (kind, r, cc) in enumerate(specs):
            mine, other = _half(r, c, 16), _half(r, 1 - c, 16)
            for kk, (fx, fy) in enumerate(_CHIP_FLIPS):
                px, py = _flip(x, fx), _flip(y, fy)
                if step == "ici":
                    sl = _slot(refs[w], kind, 2 * px + py if received else 2 * x + y, mine, cc)
                    dev = (px, py, c)
                else:
                    sl = _slot(refs[w], kind, 2 * px + py, other if received else mine, cc)
                    dev = (x, y, 1 - c)
                out.append(_rcopy(sl, sl, ss, rs, 3 * w + kk, dev))
        return out

    def issue(refs, ss, rs):
        return copies(refs, ss, rs, False)

    def expect(refs, ss, rs):
        return copies(refs, ss, rs, False), copies(refs, ss, rs, True)

    return issue, expect


_HBM = pl.BlockSpec(memory_space=pltpu.HBM)
_SEM = pl.BlockSpec(memory_space=pltpu.SEMAPHORE)
_EFFECT = pltpu.SideEffectType.DATAFLOW_SIDE_EFFECTING


def _copies_start(name, bufs, n_sems, issue, after=None):
    nb = len(bufs)
    extra = [] if after is None else [after]

    def body(*refs):
        send_sems, recv_sems = refs[nb + len(extra)], refs[nb + len(extra) + 1]
        for cp in issue(refs[:nb], send_sems, recv_sems):
            cp.start()
        refs[-1][...] = jnp.zeros_like(refs[-1])

    outs = pl.pallas_call(
        body, name=name,
        out_shape=(pltpu.SemaphoreType.DMA((n_sems,)), pltpu.SemaphoreType.DMA((n_sems,)),
                   *[pltpu.HBM(b.shape, b.dtype) for b in bufs], SDS((8, 128), F32)),
        in_specs=[_HBM] * nb + [pl.BlockSpec(memory_space=pl.ANY)] * len(extra),
        out_specs=(_SEM, _SEM, *[_HBM] * nb, pl.BlockSpec(memory_space=pltpu.VMEM)),
        input_output_aliases={i: 2 + i for i in range(nb)},
        compiler_params=pltpu.CompilerParams(has_side_effects=_EFFECT),
    )(*[pltpu.with_memory_space_constraint(b, pltpu.HBM) for b in bufs], *extra)
    return outs[0], outs[1], list(outs[2:2 + nb]), outs[-1]


def _copies_wait(name, bufs, send_sems, recv_sems, after, expect):
    nb = len(bufs)

    def body(*refs):
        sent, received = expect(refs[:nb], refs[nb], refs[nb + 1])
        for cp in sent:
            cp.wait_send()
        for cp in received:
            cp.wait_recv()

    outs = pl.pallas_call(
        body, name=name,
        out_shape=tuple(pltpu.HBM(b.shape, b.dtype) for b in bufs),
        in_specs=(*[_HBM] * nb, _SEM, _SEM, pl.BlockSpec(memory_space=pl.ANY)), out_specs=tuple([_HBM] * nb),
        input_output_aliases={i: i for i in range(nb)},
        compiler_params=pltpu.CompilerParams(has_side_effects=_EFFECT),
    )(*bufs, send_sems, recv_sems, after)
    return list(outs)


def _add_pair(g, recv, kind, r, c, c_arr, name):
    h = r // 2
    if kind == "blk":
        tr = _row_tile(h, 512)
        grid = (N_CHIPS, h // tr)
        g_spec = pl.BlockSpec((1, 1, tr, c), lambda j, i, c_ref: (j, c_ref[0], i, 0))
        o_spec = pl.BlockSpec((1, tr, c), lambda j, i, c_ref: (j, i, 0))
        gv, oshape = g.reshape(N_CHIPS, 2, h, c), (N_CHIPS, h, c)
    else:
        tr = _row_tile(h, 64)
        grid = (h // tr,)
        g_spec = pl.BlockSpec((1, tr, N_CHIPS * c), lambda i, c_ref: (c_ref[0], i, 0))
        o_spec = pl.BlockSpec((tr, N_CHIPS * c), lambda i, c_ref: (i, 0))
        gv, oshape = g.reshape(2, h, N_CHIPS * c), (h, N_CHIPS * c)

    def body(c_ref, g_ref, r_ref, o_ref, ob_ref):
        v = (g_ref[:, 0] if kind == "blk" else g_ref[0]) + r_ref[...]
        o_ref[...] = v
        ob_ref[...] = v.astype(BF16)

    return pl.pallas_call(
        body, name=name,
        grid_spec=pltpu.PrefetchScalarGridSpec(num_scalar_prefetch=1, grid=grid, in_specs=[g_spec, o_spec], out_specs=[o_spec] * 2),
        out_shape=[SDS(oshape, F32), SDS(oshape, BF16)],
        compiler_params=pltpu.CompilerParams(vmem_limit_bytes=VMEM_LIMIT),
    )(c_arr, gv, recv)


def _sum_adamw(pair, recv, w, m, v, kind, r, c, mc_arr, name):
    h = r // 2
    tr = _row_tile(h, 256)
    nt = h // tr
    if kind == "blk":
        p_spec = pl.BlockSpec((1, tr, c), lambda i, mc: (mc[0], i, 0))
    else:
        p_spec = pl.BlockSpec((tr, c), lambda i, mc: (i, mc[0]))
    mine = pl.BlockSpec((tr, c), lambda i, mc: (mc[1] * nt + i, 0))

    def body(mc, a_ref, r_ref, w_ref, m_ref, v_ref, g_out, d_out, m_out, v_out):
        own = a_ref[0] if kind == "blk" else a_ref[...]
        g = ((own + r_ref[0].astype(F32)) + r_ref[1].astype(F32)) + r_ref[2].astype(F32)
        g_out[...] = g
        d_out[...], m_out[...], v_out[...] = _adamw_rows(w_ref[...], g, m_ref[...], v_ref[...])

    return pl.pallas_call(
        body, name=name,
        grid_spec=pltpu.PrefetchScalarGridSpec(
            num_scalar_prefetch=1, grid=(nt,),
            in_specs=[p_spec, pl.BlockSpec((3, tr, c), lambda i, mc: (0, i, 0)), mine, mine, mine], out_specs=[mine] * 4),
        out_shape=[SDS((r, c), F32)] * 4,
        compiler_params=pltpu.CompilerParams(vmem_limit_bytes=VMEM_LIMIT),
    )(mc_arr, pair, recv, w, m, v)


class _GroupReduce:
    def __init__(self, tag, specs, c_arr, mc_arr):
        self.tag, self.specs, self.c_arr, self.mc_arr = tag, specs, c_arr, mc_arr
        self.n = len(specs)

    def _plan(self, step):
        specs, n = self.specs, self.n

        def copies(refs, ss, rs, received):
            x, y, c = _coords()
            sib, out = (x, y, 1 - c), []
            for w, (_, kind, r, cc) in enumerate(specs):
                if step == "join":
                    for q in range(4):
                        there = refs[4 * w + q].at[_half(r, 1 - c if received else c, 8)]
                        out.append(_rcopy(there, there, ss, rs, 4 * w + q, sib))
                    continue
                src, land = refs[w], refs[n + w]
                if step == "swap":
                    rows = _half(r, 1 - c, 8)
                    part = src.at[:, rows] if kind == "blk" else src.at[rows]
                    out.append(_rcopy(land if received else part, land, ss, rs, w, sib))
                else:
                    for kk, (fx, fy) in enumerate(_CHIP_FLIPS):
                        px, py = _flip(x, fx), _flip(y, fy)
                        part = land.at[kk] if received else _slot(src, kind, 2 * px + py, None, cc)
                        out.append(_rcopy(part, land.at[kk], ss, rs, 3 * w + kk, (px, py, c)))
            return out

        def issue(refs, ss, rs):
            return copies(refs, ss, rs, False)

        def expect(refs, ss, rs):
            return copies(refs, ss, rs, False), copies(refs, ss, rs, True)

        return issue, expect

    def swap_start(self, grads, after=None):
        lands = [lax.empty(_form(kind, r // 2, c), F32) for _, kind, r, c in self.specs]
        ss, rs, bufs, tok = _copies_start(f"rs_{self.tag}_swap", list(grads) + lands, self.n, self._plan("swap")[0], after=after)
        self.state = (ss, rs, bufs)
        return tok

    def swap_wait_ici_start(self, after):
        ss, rs, bufs = self.state
        bufs = _copies_wait(f"rs_{self.tag}_swap_wait", bufs, ss, rs, after, self._plan("swap")[1])
        pairs = [_add_pair(bufs[w], bufs[self.n + w], kind, r, c, self.c_arr, name=f"rs_{self.tag}_pair_{nm}")
                 for w, (nm, kind, r, c) in enumerate(self.specs)]
        self.pair = [pr[0] for pr in pairs]
        lands = [lax.empty((3, r // 2, c), BF16) for _, _, r, c in self.specs]
        ss, rs, bufs, tok = _copies_start(f"rs_{self.tag}_ici", [pr[1] for pr in pairs] + lands, 3 * self.n, self._plan("ici")[0])
        self.state = (ss, rs, bufs)
        return tok

    def ici_wait_join_start(self, after, state):
        ss, rs, bufs = self.state
        bufs = _copies_wait(f"rs_{self.tag}_ici_wait", bufs, ss, rs, after, self._plan("ici")[1])
        outs = []
        for w, (nm, kind, r, c) in enumerate(self.specs):
            outs += _sum_adamw(self.pair[w], bufs[self.n + w], *state[nm], kind, r, c, self.mc_arr, name=f"rs_{self.tag}_adamw_{nm}")
        ss, rs, bufs, tok = _copies_start(f"rs_{self.tag}_join", outs, 4 * self.n, self._plan("join")[0])
        self.state = (ss, rs, bufs)
        return tok

    def join_wait(self, after):
        ss, rs, bufs = self.state
        bufs = _copies_wait(f"rs_{self.tag}_join_wait", bufs, ss, rs, after, self._plan("join")[1])
        return {nm: tuple(bufs[4 * w:4 * w + 4]) for w, (nm, _, _, _) in enumerate(self.specs)}


def _all_reduce_small(buf):
    rows, cols = buf.shape

    def body(x_ref, o_ref, gath, send_sems, recv_sems):
        x, y, c = _coords()
        me = 4 * x + 2 * y + c
        gath[me] = x_ref[...]
        sends = []
        for kk in range(1, 8):
            f = (kk >> 2) & 1, (kk >> 1) & 1, kk & 1
            px, py, pc = _flip(x, f[0]), _flip(y, f[1]), _flip(c, f[2])
            cp = pltpu.make_async_remote_copy(src_ref=x_ref, dst_ref=gath.at[me], send_sem=send_sems.at[kk - 1],
                                              recv_sem=recv_sems.at[kk - 1], device_id=(px, py, pc), device_id_type=MESH)
            cp.start()
            sends.append(cp)
        for kk in range(1, 8):
            f = (kk >> 2) & 1, (kk >> 1) & 1, kk & 1
            px, py, pc = _flip(x, f[0]), _flip(y, f[1]), _flip(c, f[2])
            there = gath.at[4 * px + 2 * py + pc]
            pltpu.make_async_remote_copy(src_ref=there, dst_ref=there, send_sem=send_sems.at[kk - 1],
                                         recv_sem=recv_sems.at[kk - 1], device_id=(px, py, pc), device_id_type=MESH).wait_recv()
        for cp in sends:
            cp.wait_send()
        acc = gath[0]
        for j in range(1, 8):
            acc = acc + gath[j]
        o_ref[...] = acc

    return pl.pallas_call(
        body, name="all_reduce_small",
        in_specs=[pl.BlockSpec(memory_space=pltpu.VMEM)], out_specs=pl.BlockSpec(memory_space=pltpu.VMEM),
        out_shape=SDS((rows, cols), F32),
        scratch_shapes=[pltpu.VMEM((8, rows, cols), F32), pltpu.SemaphoreType.DMA((7,)), pltpu.SemaphoreType.DMA((7,))],
    )(buf)


def _adamw_rows(w, g, m, v):
    m = ADAM_B1 * m + (1.0 - ADAM_B1) * g
    v = ADAM_B2 * v + (1.0 - ADAM_B2) * jnp.square(g)
    m_hat = m / (1.0 - ADAM_B1 ** ADAM_STEP)
    v_hat = v / (1.0 - ADAM_B2 ** ADAM_STEP)
    return -ADAM_LR * (m_hat / (jnp.sqrt(v_hat) + ADAM_EPS) + ADAM_WD * w), m, v


def _adamw(w, g, m, v, name, dep=None):
    rows, cols = w.shape
    tm = _pick(rows, (256, 128, 64, 16, 8))
    return _rows_call(_adamw_rows, [(t, 0, cols) for t in (w, g, m, v)], [], [(cols, F32)] * 3, tm=tm, name=name, dep=dep)


def _pack_small(parts):
    flat = jnp.concatenate([parts[n].reshape(-1) for n, _ in SMALL])
    return jnp.pad(flat, (0, SMALL_ROWS * PACK_COLS - flat.shape[0])).reshape(SMALL_ROWS, PACK_COLS)


def _unpack_small(buf, shapes):
    flat, out, off = buf.reshape(-1), {}, 0
    for n, sz in SMALL:
        out[n] = flat[off:off + sz].reshape(shapes[n])
        off += sz
    return out


def _lora_stack(parts):
    return jnp.concatenate([parts[n] for n, _ in LORA], axis=-2)


def _lora_split(stacked):
    out, off = {}, 0
    for n, rows in LORA:
        out[n] = stacked[..., off:off + rows, :]
        off += rows
    return out


def _ffn_gate_up(h, wgt, wut, name, dep=None):
    s, d = h.shape
    nblk, f, _ = wgt.shape
    tm = _pick(s, (1024, 512, 256))
    dn = (((1,), (1,)), ((), ()))

    def body(h_ref, wg_ref, wu_ref, *rest):
        g_ref, u_ref, a_ref = rest[-3:]
        hh = h_ref[...]
        g = lax.dot_general(hh, wg_ref[0], dn, preferred_element_type=F32)
        u = lax.dot_general(hh, wu_ref[0], dn, preferred_element_type=F32)
        g_ref[0], u_ref[0] = g, u
        a_ref[0] = _swiglu_act(g, u).astype(BF16)

    w_spec = pl.BlockSpec((1, f, d), lambda j, i: (j, 0, 0))
    o_spec = pl.BlockSpec((1, tm, f), lambda j, i: (j, i, 0))
    extra = [] if dep is None else [dep]
    return pl.pallas_call(
        body, name=name, grid=(nblk, s // tm),
        in_specs=[pl.BlockSpec((tm, d), lambda j, i: (i, 0)), w_spec, w_spec] + [pl.BlockSpec(memory_space=pl.ANY)] * len(extra),
        out_specs=[o_spec] * 3,
        out_shape=[SDS((nblk, s, f), F32), SDS((nblk, s, f), F32), SDS((nblk, s, f), BF16)],
        compiler_params=pltpu.CompilerParams(dimension_semantics=("parallel", "parallel"), vmem_limit_bytes=VMEM_LIMIT),
    )(h, wgt, wut, *extra)


def _ffn_down_dx(dx_bf, wd, gate, up, name, dep=None):
    s, d = dx_bf.shape
    nblk, f, _ = wd.shape
    tm = _pick(s, (1024, 512, 256))
    dn = (((1,), (1,)), ((), ()))

    def body(dx_ref, wd_ref, g_ref, u_ref, *rest):
        dg_ref, du_ref = rest[-2:]
        dact = 0.5 * lax.dot_general(dx_ref[...], wd_ref[0], dn, preferred_element_type=F32)
        _, vjp = jax.vjp(_swiglu_act, g_ref[0], u_ref[0])
        dg, du = vjp(dact)
        dg_ref[0], du_ref[0] = dg.astype(BF16), du.astype(BF16)

    o_spec = pl.BlockSpec((1, tm, f), lambda j, i: (j, i, 0))
    extra = [] if dep is None else [dep]
    return pl.pallas_call(
        body, name=name, grid=(nblk, s // tm),
        in_specs=[pl.BlockSpec((tm, d), lambda j, i: (i, 0)), pl.BlockSpec((1, f, d), lambda j, i: (j, 0, 0)), o_spec, o_spec]
        + [pl.BlockSpec(memory_space=pl.ANY)] * len(extra),
        out_specs=[o_spec] * 2, out_shape=[SDS((nblk, s, f), BF16)] * 2,
        compiler_params=pltpu.CompilerParams(dimension_semantics=("parallel", "parallel"), vmem_limit_bytes=VMEM_LIMIT),
    )(dx_bf, wd, gate, up, *extra)


def _ffn_fwd(x, gain, wgt, wut, wd, tag, h=None, dep=None):
    if h is None:
        h = _rows_call(_rms, [(x, 0, D_MODEL)], [gain], [(D_MODEL, BF16)], tm=256, name=f"{tag}_norm")[0]
    gate, up, act = _ffn_gate_up(h, wgt, wut, f"{tag}_gate_up", dep=dep)
    x_new = _mm(act, wd, sum_blocks=True, res=x, alpha=0.5, name=f"{tag}_down")
    return x_new, (x, h, gate, up, act)


def _ffn_bwd(dx_new, dx_new_bf, saved, gain, wgt, wut, wd, tag, dep=None, hooks=None):
    x, h, gate, up, act = saved
    hooks = hooks or {}

    def hook(name, *vals):
        return hooks[name](*vals) if name in hooks else None

    d_wd = _mm(act, dx_new_bf, ta=True, alpha=0.5, name=f"{tag}_down_dw")
    dep = hook("down", d_wd) if "down" in hooks else dep
    dgate, dup = _ffn_down_dx(dx_new_bf, wd, gate, up, f"{tag}_down_dx", dep=dep)
    d_wgt = _mm(dgate, h, ta=True, dep=hook("mid", dgate), name=f"{tag}_gate_dw")
    d_wut = _mm(dup, h, ta=True, name=f"{tag}_up_dw")
    dh = _mm(dgate, wgt, sum_blocks=True, dep=hook("dw", d_wgt, d_wut), name=f"{tag}_gate_dx")
    dh = _mm(dup, wut, sum_blocks=True, res=dh, dep=hook("dx", dh), name=f"{tag}_up_dx")
    dx, dx_bf, dgain = _norm_bwd(x, gain, dh, dx_new, f"{tag}_norm_bwd", dep=hook("end", dh))
    return dx, dx_bf, dgain, d_wgt, d_wut, d_wd


def _norm_bwd(x, gain, dh, dres, name, dep=None):
    def f(xt, dht, drt, gt):
        _, vjp = jax.vjp(_rms, xt, gt)
        dxt, dgt = vjp(dht)
        return dxt + drt, dxt + drt, dgt

    return _rows_call(f, [(x, 0, D_MODEL), (dh, 0, D_MODEL), (dres, 0, D_MODEL)], [gain], [(D_MODEL, F32), (D_MODEL, BF16)],
                      [(1, D_MODEL)], tm=256, name=name, dep=dep)


def kernel(x, p, positions, ffn1_norm, ffn1_w_gate, ffn1_w_up, ffn1_w_down, mix_norm, w_in, rwkv_mu, rwkv_w0, rwkv_w2, rwkv_a0, rwkv_a2, rwkv_g2, rwkv_k_k, rwkv_k_a, rwkv_r_k, rwkv_gn_w, rwkv_gn_b, q_norm, k_norm, w_br_rwkv, w_br_attn, w_out, ffn2_norm, ffn2_w_gate, ffn2_w_up, ffn2_w_down, ple_norm, ple_w_gate, ple_w_proj, loss_target, m_ffn1_norm, m_ffn1_w_gate, m_ffn1_w_up, m_ffn1_w_down, m_mix_norm, m_w_in, m_rwkv_mu, m_rwkv_w0, m_rwkv_w2, m_rwkv_a0, m_rwkv_a2, m_rwkv_g2, m_rwkv_k_k, m_rwkv_k_a, m_rwkv_r_k, m_rwkv_gn_w, m_rwkv_gn_b, m_q_norm, m_k_norm, m_w_br_rwkv, m_w_br_attn, m_w_out, m_ffn2_norm, m_ffn2_w_gate, m_ffn2_w_up, m_ffn2_w_down, m_ple_norm, m_ple_w_gate, m_ple_w_proj, v_ffn1_norm, v_ffn1_w_gate, v_ffn1_w_up, v_ffn1_w_down, v_mix_norm, v_w_in, v_rwkv_mu, v_rwkv_w0, v_rwkv_w2, v_rwkv_a0, v_rwkv_a2, v_rwkv_g2, v_rwkv_k_k, v_rwkv_k_a, v_rwkv_r_k, v_rwkv_gn_w, v_rwkv_gn_b, v_q_norm, v_k_norm, v_w_br_rwkv, v_w_br_attn, v_w_out, v_ffn2_norm, v_ffn2_w_gate, v_ffn2_w_up, v_ffn2_w_down, v_ple_norm, v_ple_w_gate, v_ple_w_proj):
    args = dict(locals())
    wts = {n: args[n] for n in WEIGHTS}
    mom_m = {n: args["m_" + n] for n in WEIGHTS}
    mom_v = {n: args["v_" + n] for n in WEIGHTS}
    x0, tgt = x[0], loss_target[0]
    s = x0.shape[0]
    p_tok = p[0, 0]

    vec = {n: wts[n].reshape(1, -1) for n, _ in SMALL}
    xi, yi, ci = _coords()
    me = 2 * xi + yi
    def laid(t, n):
        return jnp.transpose(t[n][0]) if n in TRANSPOSED else t[n][0]

    shard_of = {n: laid(wts, n) for g in GROUPS.values() for n, _, _, _ in g if n != "lora"}
    shard_of["lora"] = _lora_stack({n: wts[n][0] for n, _ in LORA})

    def whole_with_own(n, kind, r, c, tok=None):
        at = (me, 0, 0) if kind == "blk" else (0, me * c)
        own = (shard_of[n] if tok is None else shard_of[n] + tok[0, 0]).astype(BF16)
        return lax.dynamic_update_slice(lax.empty(_form(kind, r, c), BF16), own[None] if kind == "blk" else own, at)

    specs = {g: [(kind, r, c) for _, kind, r, c in grp] for g, grp in GROUPS.items()}
    plans = {(g, st): _gather_plan(specs[g], st) for g in GROUPS for st in ("ici", "d2d")}
    buf_f1 = [whole_with_own(*w) for w in GROUPS["f1"]]
    ss_0, rs_0, buf_f1, tok_0 = _copies_start("gather_f1_ici", buf_f1, 3 * len(buf_f1), plans["f1", "ici"][0])
    bufs = {g: [whole_with_own(*w, tok=tok_0) for w in GROUPS[g]] for g in ("mx", "f2")}
    buf_f1 = _copies_wait("gather_f1_ici_wait", buf_f1, ss_0, rs_0, bufs["mx"][0], plans["f1", "ici"][1])
    ss_1, rs_1, buf_f1, tok_1 = _copies_start("gather_f1_d2d", buf_f1, 3 * len(buf_f1), plans["f1", "d2d"][0])
    h1 = _rows_call(_rms, [(x0, 0, D_MODEL)], [vec["ffn1_norm"] + tok_1[0, 0]], [(D_MODEL, BF16)], tm=256, name="ffn1_norm")[0]
    buf_f1 = _copies_wait("gather_f1_d2d_wait", buf_f1, ss_1, rs_1, h1, plans["f1", "d2d"][1])
    wb = dict(zip([w[0] for w in GROUPS["f1"]], buf_f1))
    ss_a, rs_a, buf_mx, tok_a = _copies_start("gather_mx_ici", bufs["mx"], 3 * len(bufs["mx"]), plans["mx", "ici"][0],
                                              after=wb["ffn1_w_gate"])

    inv_freq = 1.0 / (ROPE_THETA ** (jnp.arange(0, HEAD, 2, dtype=F32) / HEAD))
    ang = positions[0].astype(F32)[:, None] * inv_freq
    cos, sin = jnp.cos(ang), jnp.sin(ang)
    cos2, sin2 = jnp.concatenate([cos, cos], axis=1), jnp.concatenate([-sin, sin], axis=1)

    x1, ffn1_saved = _ffn_fwd(x0, vec["ffn1_norm"], wb["ffn1_w_gate"], wb["ffn1_w_up"], wb["ffn1_w_down"], "ffn1", h=h1, dep=tok_a)
    buf_mx = _copies_wait("gather_mx_ici_wait", buf_mx, ss_a, rs_a, x1, plans["mx", "ici"][1])
    ss_b, rs_b, buf_mx, tok_b = _copies_start("gather_mx_d2d", buf_mx, 3 * len(buf_mx), plans["mx", "d2d"][0])
    ss_c, rs_c, buf_f2, tok_c = _copies_start("gather_f2_ici", bufs["f2"], 3 * len(bufs["f2"]), plans["f2", "ici"][0])
    h = _rows_call(_rms, [(x1, 0, D_MODEL)], [vec["mix_norm"] + (tok_b[0, 0] + tok_c[0, 0])], [(D_MODEL, BF16)], tm=256,
                   name="mix_norm")[0]
    buf_mx = _copies_wait("gather_mx_d2d_wait", buf_mx, ss_b, rs_b, h, plans["mx", "d2d"][1])
    wb.update(zip([w[0] for w in GROUPS["mx"]], buf_mx))
    w_in_all = wb["w_in"]
    w_in_r, w_in_a, w_in_g = w_in_all[:, :RWKV_COLS], w_in_all[:, RWKV_COLS:RWKV_COLS + ATTN_COLS], w_in_all[:, RWKV_COLS + ATTN_COLS:]
    lora = _lora_split(wb["lora"])
    w2, a2, g2 = lora["rwkv_w2"], lora["rwkv_a2"], lora["rwkv_g2"]
    z_r = _mm(h, w_in_r, name="in_rwkv")
    z_a = _mm(h, w_in_a, name="in_attn")
    z_g = _mm(h, w_in_g, name="in_gate")

    zs = _shift_fwd(z_r, vec["rwkv_mu"])
    pre_params = [vec["rwkv_w0"], w2, vec["rwkv_a0"], a2, g2, vec["rwkv_k_k"], vec["rwkv_k_a"]]
    def pre_fwd(*t):
        res = _rwkv_pre(*t)
        return res[1], res[2], res[4], res[5], res[6]

    lw, k2, na, kb, gate_r = _rows_call(pre_fwd, [(zs, 0, RWKV_COLS)], pre_params, [(RWKV_DIM, F32)] * 5, tm=256, name="rwkv_pre")
    y_scan, s0s = _wkv_fwd(zs, lw, k2, na, kb)
    buf_f2 = _copies_wait("gather_f2_ici_wait", buf_f2, ss_c, rs_c, y_scan, plans["f2", "ici"][1])
    ss_d, rs_d, buf_f2, tok_d = _copies_start("gather_f2_d2d", buf_f2, 3 * len(buf_f2), plans["f2", "d2d"][0])
    post_params = [vec["rwkv_gn_w"] + tok_d[0, 0], vec["rwkv_gn_b"], vec["rwkv_r_k"]]
    post_rows = [(y_scan, 0, RWKV_DIM), (zs, 0, RWKV_DIM), (k2, 0, RWKV_DIM), (zs, 2, RWKV_DIM), (gate_r, 0, RWKV_DIM)]
    y_rwkv = _rows_call(_rwkv_post, post_rows, post_params, [(RWKV_DIM, BF16)], tm=256, name="rwkv_post")[0]
    buf_f2 = _copies_wait("gather_f2_d2d_wait", buf_f2, ss_d, rs_d, y_rwkv, plans["f2", "d2d"][1])
    wb.update(zip([w[0] for w in GROUPS["f2"]], buf_f2))
    w_brr, w_bra = wb["w_br_rwkv"], wb["w_br_attn"]
    w_o = wb["w_out"].reshape(D_MODEL, D_MODEL)
    w_pp, w_pg = wb["ple_w_proj"], wb["ple_w_gate"].reshape(D_MODEL, D_MODEL)

    def qk_fwd(qt, kt, ct, st, qg, kg):
        return _norm_rope(qt, qg, ct, st), _norm_rope(kt, kg, ct, st)

    qk_rows = [(z_a, 0, ATTN_DIM), (z_a, 1, ATTN_DIM), (cos2, 0, HEAD), (sin2, 0, HEAD)]
    q_rot, k_rot = _rows_call(qk_fwd, qk_rows, [vec["q_norm"], vec["k_norm"]], [(ATTN_DIM, BF16)] * 2, tm=256, name="attn_pre")
    def group(t, g, off=0):
        return t[:, off + g * GROUP_DIM:off + (g + 1) * GROUP_DIM].astype(BF16)

    qkv = [(group(q_rot, g), group(k_rot, g), group(z_a, g, 2 * ATTN_DIM)) for g in range(len(ATTN_DILATIONS))]
    outs, lses = zip(*[_attn_fwd(*qkv[g], d) for g, d in enumerate(ATTN_DILATIONS)])
    comb_rows = [(t, 0, GROUP_DIM) for t in outs + lses]
    y_attn = _rows_call(_attn_combine, comb_rows, [], [(GROUP_DIM, BF16)], tm=256, name="attn_combine")[0]

    br = _mm(y_rwkv, w_brr, name="branch_rwkv")
    ba = _mm(y_attn, w_bra, name="branch_attn")
    merge_rows = [(z_g, 0, D_MODEL), (z_g, 1, D_MODEL), (br, 0, D_MODEL), (ba, 0, D_MODEL)]
    merged = _rows_call(_merge, merge_rows, [], [(D_MODEL, BF16)], tm=256, name="merge")[0]
    x2 = _mm(merged, w_o, res=x1, name="out_proj")
    x3, ffn2_saved = _ffn_fwd(x2, vec["ffn2_norm"], wb["ffn2_w_gate"], wb["ffn2_w_up"], wb["ffn2_w_down"], "ffn2")
    hp = _rows_call(_rms, [(x3, 0, D_MODEL)], [vec["ple_norm"]], [(D_MODEL, BF16)], tm=256, name="ple_norm")[0]
    pg = _mm(hp, w_pg, name="ple_gate")
    pp = _mm(p_tok, w_pp, name="ple_proj")

    def head(x3t, pgt, ppt, tt):
        sg = _sigmoid(pgt)
        err = x3t + sg * ppt - tt
        dx4 = err * (1.0 / D_MODEL)
        loss = 0.5 * jnp.sum(jnp.mean(err * err, axis=-1, keepdims=True), axis=0, keepdims=True)
        return dx4, dx4 * ppt * sg * (1.0 - sg), dx4 * sg, jnp.broadcast_to(loss, (8, 128))

    head_rows = [(x3, 0, D_MODEL), (pg, 0, D_MODEL), (pp, 0, D_MODEL), (tgt, 0, D_MODEL)]
    dx4, dpg, dpp, loss_tile = _rows_call(head, head_rows, [], [(D_MODEL, F32), (D_MODEL, BF16), (D_MODEL, BF16)], [(8, 128)],
                                          tm=256, name="ple_loss")

    c_arr = jnp.reshape(ci, (1,)).astype(jnp.int32)
    mc_arr = jnp.stack([me, ci]).astype(jnp.int32)
    red = {g: _GroupReduce(g, grp, c_arr, mc_arr) for g, grp in REDUCE_GROUPS.items()}

    def adam_state(names):
        out = {}
        for n in names:
            if n == "lora":
                out[n] = tuple(_lora_stack({k: t[k][0] for k, _ in LORA}) for t in (wts, mom_m, mom_v))
            else:
                out[n] = (laid(wts, n), laid(mom_m, n), laid(mom_v, n))
        return out

    adam = {g: adam_state([w[0] for w in grp]) for g, grp in REDUCE_GROUPS.items()}
    done = {}
    gw, gs = {}, {}
    gw["ple_w_proj"] = _mm(p_tok, dpp, ta=True, name="ple_proj_dw")
    gw["ple_w_gate"] = _mm(hp, dpg, ta=True, name="ple_gate_dw")
    dhp = _mm(dpg, w_pg, tb=True, name="ple_gate_dx")
    dx3, dx3_bf, gs["ple_norm"] = _norm_bwd(x3, vec["ple_norm"], dhp, dx4, "ple_norm_bwd")
    dx2, dx2_bf, gs["ffn2_norm"], gw["ffn2_w_gate"], gw["ffn2_w_up"], gw["ffn2_w_down"] = _ffn_bwd(
        dx3, dx3_bf, ffn2_saved, vec["ffn2_norm"], wb["ffn2_w_gate"], wb["ffn2_w_up"], wb["ffn2_w_down"], "ffn2")
    gw["ple_w_gate"] = gw["ple_w_gate"].reshape(N_CHIPS, D_MODEL // N_CHIPS, D_MODEL)
    tok = red["f2"].swap_start([gw[w[0]] for w in REDUCE_GROUPS["f2"]])
    gw["w_out"] = _mm(merged, dx2_bf, ta=True, name="out_proj_dw")
    dmerged = _mm(dx2_bf, w_o, tb=True, dep=tok, name="out_proj_dx")

    def merge_bwd(zgr, zga, brt, bat, ct):
        _, vjp = jax.vjp(_merge, zgr, zga, brt, bat)
        d1, d2, d3, d4 = vjp(ct)
        return jnp.concatenate([d1, d2], axis=1), d3, d4

    dz_g, dbr, dba = _rows_call(merge_bwd, merge_rows + [(dmerged, 0, D_MODEL)], [],
                                [(2 * D_MODEL, BF16), (D_MODEL, BF16), (D_MODEL, BF16)], tm=256, name="merge_bwd")
    tok = red["f2"].swap_wait_ici_start(dz_g)
    gw["w_br_rwkv"] = _mm(y_rwkv, dbr, ta=True, name="branch_rwkv_dw")
    gw["w_br_attn"] = _mm(y_attn, dba, ta=True, name="branch_attn_dw")
    dy_rwkv = _mm(dbr, w_brr, tb=True, dep=tok, name="branch_rwkv_dx")
    dy_attn = _mm(dba, w_bra, tb=True, dep=tok, name="branch_attn_dx")

    def comb_bwd(*t):
        _, vjp = jax.vjp(_attn_combine, *t[:6])
        return vjp(t[6])

    dcomb = _rows_call(comb_bwd, comb_rows + [(dy_attn, 0, GROUP_DIM)], [], [(GROUP_DIM, F32)] * 6, tm=256, name="attn_combine_bwd")
    dqs, dks, dvs = zip(*[_attn_bwd(*qkv[g], d, dcomb[g], dcomb[3 + g]) for g, d in enumerate(ATTN_DILATIONS)])

    def qk_bwd(qt, kt, ct, st, *rest):
        dq = jnp.concatenate(rest[0:3], axis=1)
        dk = jnp.concatenate(rest[3:6], axis=1)
        qg, kg = rest[9], rest[10]
        _, vjp = jax.vjp(lambda a_, b_, c_, d_: qk_fwd(a_, b_, ct, st, c_, d_), qt, kt, qg, kg)
        dqt, dkt, dqg, dkg = vjp((dq, dk))
        return jnp.concatenate((dqt, dkt) + tuple(rest[6:9]), axis=1), dqg, dkg

    dz_a, gs["q_norm"], gs["k_norm"] = _rows_call(
        qk_bwd, qk_rows + [(t, 0, GROUP_DIM) for t in dqs + dks + dvs], [vec["q_norm"], vec["k_norm"]],
        [(ATTN_COLS, BF16)], [(1, HEAD), (1, HEAD)], tm=256, name="attn_pre_bwd")
    tok = red["f2"].ici_wait_join_start(dz_a, adam["f2"])

    def post_bwd(*t):
        _, vjp = jax.vjp(_rwkv_post, *t[:5], *t[6:])
        return vjp(t[5])

    dy_scan, dr_post, dk2_post, dv_post, dgate_r, gs["rwkv_gn_w"], gs["rwkv_gn_b"], gs["rwkv_r_k"] = _rows_call(
        post_bwd, post_rows + [(dy_rwkv, 0, RWKV_DIM)], post_params, [(RWKV_DIM, F32)] * 5, [(1, RWKV_DIM)] * 3,
        tm=256, name="rwkv_post_bwd", dep=tok)
    done.update(red["f2"].join_wait(dy_scan))
    dr_s, dlw, dk2_s, dv_s, dna, dkb = _wkv_bwd(zs, lw, k2, na, kb, s0s, dy_scan)

    def pre_bwd(zt, c_r1, c_r2, c_lw, c_k1, c_k2, c_v1, c_v2, c_a, c_b, c_g, *params):
        _, vjp = jax.vjp(_rwkv_pre, zt, *params)
        return vjp((c_r1 + c_r2, c_lw, c_k1 + c_k2, c_v1 + c_v2, c_a, c_b, c_g))

    pre_cts = [dr_s, dr_post, dlw, dk2_s, dk2_post, dv_s, dv_post, dna, dkb, dgate_r]
    dzs, gs["rwkv_w0"], g_w2, gs["rwkv_a0"], g_a2, g_g2, gs["rwkv_k_k"], gs["rwkv_k_a"] = _rows_call(
        pre_bwd, [(zs, 0, RWKV_COLS)] + [(t, 0, RWKV_DIM) for t in pre_cts], pre_params, [(RWKV_COLS, F32)],
        [q.shape for q in pre_params], tm=256, name="rwkv_pre_bwd")
    dz_r, gs["rwkv_mu"] = _shift_bwd(z_r, vec["rwkv_mu"], dzs)

    g_w_in = jnp.concatenate([_mm(h, dz_r, ta=True, name="in_rwkv_dw"), _mm(h, dz_a, ta=True, name="in_attn_dw"),
                              _mm(h, dz_g, ta=True, name="in_gate_dw")], axis=1)
    gw["w_in"], gw["lora"] = g_w_in, jnp.concatenate([g_w2, g_a2, g_g2], axis=0)
    gw["w_out"] = gw["w_out"].reshape(N_CHIPS, D_MODEL // N_CHIPS, D_MODEL)
    tok = red["mx"].swap_start([gw[w[0]] for w in REDUCE_GROUPS["mx"]])
    dh = _mm(dz_r, w_in_r, tb=True, dep=tok, name="in_rwkv_dx")
    dh = _mm(dz_a, w_in_a, tb=True, res=dh, name="in_attn_dx")
    dh = _mm(dz_g, w_in_g, tb=True, res=dh, name="in_gate_dx")
    dx1, dx1_bf, gs["mix_norm"] = _norm_bwd(x1, vec["mix_norm"], dh, dx2, "mix_norm_bwd")
    tok_mx = red["mx"].swap_wait_ici_start(dx1_bf)
    hooks = {"down": lambda d_wd: red["f1d"].swap_start([d_wd], after=tok_mx),
             "mid": lambda dgate: red["f1d"].swap_wait_ici_start(dgate),
             "dw": lambda d_wgt, d_wut: red["f1g"].swap_start([d_wgt, d_wut]),
             "dx": lambda part: red["f1g"].swap_wait_ici_start(part) + red["f1d"].ici_wait_join_start(part, adam["f1d"]),
             "end": lambda dh_: red["mx"].ici_wait_join_start(dh_, adam["mx"])}
    dx0, _, gs["ffn1_norm"], gw["ffn1_w_gate"], gw["ffn1_w_up"], gw["ffn1_w_down"] = _ffn_bwd(
        dx1, dx1_bf, ffn1_saved, vec["ffn1_norm"], wb["ffn1_w_gate"], wb["ffn1_w_up"], wb["ffn1_w_down"], "ffn1", hooks=hooks)

    flat = jnp.concatenate([gs[n].reshape(-1) for n, _ in SMALL] + [loss_tile[0, 0:1]])
    small_buf = jnp.pad(flat, (0, SMALL_ROWS * PACK_COLS - flat.shape[0])).reshape(SMALL_ROWS, PACK_COLS)
    small_sum = _all_reduce_small(small_buf)
    n_small = sum(sz for _, sz in SMALL)
    loss = small_sum.reshape(-1)[n_small]
    grad_small = _unpack_small(small_sum, {n: wts[n].shape for n, _ in SMALL})
    d_s, m_s, v_s = _adamw(_pack_small(wts), small_sum, _pack_small(mom_m), _pack_small(mom_v), name="adamw_small", dep=dx0)
    shapes = {n: wts[n].shape for n, _ in SMALL}
    d_s, m_s, v_s = _unpack_small(d_s, shapes), _unpack_small(m_s, shapes), _unpack_small(v_s, shapes)
    grads, deltas, new_m, new_v = {}, {}, {}, {}
    for n, _ in SMALL:
        grads[n], deltas[n], new_m[n], new_v[n] = grad_small[n], d_s[n], m_s[n], v_s[n]

    tok = red["f1g"].ici_wait_join_start(m_s["ffn1_norm"], adam["f1g"])
    for g in ("mx", "f1d", "f1g"):
        done.update(red[g].join_wait(tok))
    for n, res in done.items():
        for store, val in zip((grads, deltas, new_m, new_v), res):
            if n == "lora":
                store.update({k: t[None] for k, t in _lora_split(val).items()})
            else:
                store[n] = (jnp.transpose(val) if n in TRANSPOSED else val)[None]

    return (loss, dx0[None], *[grads[n] for n in WEIGHTS], *[deltas[n] for n in WEIGHTS],
            *[new_m[n] for n in WEIGHTS], *[new_v[n] for n in WEIGHTS])
```

```python
import functools

import jax
import jax.numpy as jnp
from jax import lax
from jax.experimental import pallas as pl
from jax.experimental.pallas import tpu as pltpu

F32, BF16 = jnp.float32, jnp.bfloat16
HI = lax.Precision.HIGHEST
MESH = pl.DeviceIdType.MESH
SDS = jax.ShapeDtypeStruct

D_MODEL = 1024
HEAD = 64
RWKV_HEADS = 8
RWKV_DIM = RWKV_HEADS * HEAD
DECAY_LORA, ICLR_LORA, GATE_LORA = 64, 64, 128
GN_EPS = 64e-5
RMS_EPS = 1e-6
ATTN_DILATIONS = (1, 4, 16)
BAND = 128
ATTN_DIM = 768
GROUP_DIM = 256
ROPE_THETA = 10000.0
NEG_INF = -1e30
RWKV_COLS = 3 * RWKV_DIM + DECAY_LORA + ICLR_LORA + GATE_LORA
ATTN_COLS = 3 * ATTN_DIM
ADAM_LR, ADAM_B1, ADAM_B2, ADAM_EPS, ADAM_WD, ADAM_STEP = 0.001, 0.9, 0.999, 1e-08, 0.01, 10

WKV_CHUNK = 64
WKV_HEADS_PER_STEP = 8
N_CHIPS = 4
PACK_COLS = 1024
VMEM_LIMIT = 48 * 1024 * 1024

TRANSPOSED = ("ffn1_w_gate", "ffn1_w_up", "ffn2_w_gate", "ffn2_w_up")
LORA = (("rwkv_w2", 64), ("rwkv_a2", 64), ("rwkv_g2", 128))
_FFN1 = (("ffn1_w_gate", "blk", 704, 1024), ("ffn1_w_up", "blk", 704, 1024), ("ffn1_w_down", "blk", 704, 1024))
_FFN2 = (("ffn2_w_gate", "blk", 704, 1024), ("ffn2_w_up", "blk", 704, 1024), ("ffn2_w_down", "blk", 704, 1024))
_IN = (("w_in", "col", 1024, 1536), ("lora", "col", 256, 128))
_BRANCH = (("w_br_rwkv", "col", 512, 256), ("w_br_attn", "col", 256, 256), ("w_out", "blk", 256, 1024))
_PLE = (("ple_w_gate", "blk", 256, 1024), ("ple_w_proj", "col", 256, 256))
GROUPS = {"f1": _FFN1, "mx": _IN, "f2": _BRANCH + _FFN2 + _PLE}
REDUCE_GROUPS = {"f2": _FFN2 + _PLE, "mx": _IN + _BRANCH, "f1d": _FFN1[2:], "f1g": _FFN1[:2]}
SMALL = (
    ("ffn1_norm", 1024), ("mix_norm", 1024), ("ffn2_norm", 1024), ("ple_norm", 1024), ("rwkv_mu", 1792),
    ("rwkv_w0", 512), ("rwkv_a0", 512), ("rwkv_k_k", 512), ("rwkv_k_a", 512), ("rwkv_r_k", 512),
    ("rwkv_gn_w", 512), ("rwkv_gn_b", 512), ("q_norm", 64), ("k_norm", 64),
)
SMALL_ROWS = 16
WEIGHTS = (
    "ffn1_norm", "ffn1_w_gate", "ffn1_w_up", "ffn1_w_down", "mix_norm", "w_in", "rwkv_mu", "rwkv_w0", "rwkv_w2",
    "rwkv_a0", "rwkv_a2", "rwkv_g2", "rwkv_k_k", "rwkv_k_a", "rwkv_r_k", "rwkv_gn_w", "rwkv_gn_b", "q_norm", "k_norm",
    "w_br_rwkv", "w_br_attn", "w_out", "ffn2_norm", "ffn2_w_gate", "ffn2_w_up", "ffn2_w_down", "ple_norm",
    "ple_w_gate", "ple_w_proj",
)


def _row_tile(n, most=704):
    for t in range(most - most % 16, 0, -16):
        if n % t == 0:
            return t
    return n


def _pick(n, cands):
    for c in cands:
        if n % c == 0:
            return c
    return n


def _mm(a, b, *, ta=False, tb=False, sum_blocks=False, out_dtype=F32, res=None, alpha=1.0, dep=None, name):
    flat = a.ndim == 2 and b.ndim == 2
    a3 = a if a.ndim == 3 else a[None]
    b3 = b if b.ndim == 3 else b[None]
    na, nbb = a3.shape[0], b3.shape[0]
    nblk = max(na, nbb)
    kdim, m = (a3.shape[1], a3.shape[2]) if ta else (a3.shape[2], a3.shape[1])
    n = b3.shape[1] if tb else b3.shape[2]
    assert (b3.shape[2] if tb else b3.shape[1]) == kdim
    tm = _pick(m, (1024, 512, 256, 128))
    tn = _pick(n, (1024, 896, 768, 512, 256, 128))
    tk = kdim if kdim <= 2304 else _pick(kdim, (1024, 512, 256, 128))
    nk = kdim // tk
    direct = nk == 1 and not sum_blocks

    if sum_blocks:
        grid = (m // tm, n // tn, nblk, nk)

        def ids(i, c, j, k):
            return i, c, j, k
    else:
        grid = (nblk, m // tm, n // tn, nk)

        def ids(j, i, c, k):
            return i, c, j, k

    def amap(*g):
        i, c, j, k = ids(*g)
        jj = j if na > 1 else 0
        return (jj, k, i) if ta else (jj, i, k)

    def bmap(*g):
        i, c, j, k = ids(*g)
        jj = j if nbb > 1 else 0
        return (jj, c, k) if tb else (jj, k, c)

    if sum_blocks:
        oshape, oblk = (m, n), (tm, tn)

        def omap(*g):
            i, c, j, k = ids(*g)
            return i, c
    else:
        oshape, oblk = (nblk, m, n), (1, tm, tn)

        def omap(*g):
            i, c, j, k = ids(*g)
            return j, i, c

    dn = (((0 if ta else 1,), (1 if tb else 0,)), ((), ()))
    has_res = res is not None

    def body(*refs):
        refs = list(refs)
        acc = None if direct else refs.pop()
        o_ref = refs.pop()
        a_ref, b_ref = refs[0], refs[1]
        r_ref = refs[2] if has_res else None

        def finish(v):
            if alpha != 1.0:
                v = v * alpha
            if has_res:
                v = v + r_ref[...].reshape(v.shape).astype(F32)
            o_ref[...] = v.reshape(o_ref.shape).astype(o_ref.dtype)

        if direct:
            finish(lax.dot_general(a_ref[0].astype(BF16), b_ref[0].astype(BF16), dn, preferred_element_type=F32))
            return
        k = pl.program_id(3)
        if sum_blocks:
            j = pl.program_id(2)
            first = jnp.logical_and(j == 0, k == 0)
            last = jnp.logical_and(j == nblk - 1, k == nk - 1)
        else:
            first, last = k == 0, k == nk - 1

        @pl.when(first)
        def _():
            acc[...] = jnp.zeros_like(acc)

        acc[...] += lax.dot_general(a_ref[0].astype(BF16), b_ref[0].astype(BF16), dn, preferred_element_type=F32)

        @pl.when(last)
        def _():
            finish(acc[...])

    in_specs = [pl.BlockSpec((1, tk, tm) if ta else (1, tm, tk), amap), pl.BlockSpec((1, tn, tk) if tb else (1, tk, tn), bmap)]
    args = [a3, b3]
    if has_res:
        res3 = res if (sum_blocks or res.ndim == 3) else res[None]
        in_specs.append(pl.BlockSpec(oblk, omap))
        args.append(res3)
    if dep is not None:
        in_specs.append(pl.BlockSpec(memory_space=pl.ANY))
        args.append(dep)
    out = pl.pallas_call(
        body,
        name=name,
        grid=grid,
        in_specs=in_specs,
        out_specs=pl.BlockSpec(oblk, omap),
        out_shape=SDS(oshape, out_dtype),
        scratch_shapes=[] if direct else [pltpu.VMEM((tm, tn), F32)],
        compiler_params=pltpu.CompilerParams(
            dimension_semantics=("parallel", "parallel", "arbitrary", "arbitrary") if sum_blocks
            else ("parallel", "parallel", "parallel", "arbitrary"),
            vmem_limit_bytes=VMEM_LIMIT),
    )(*args)
    if flat and not sum_blocks:
        out = out[0]
    return out


def _rows_call(f, rows, params, outs, accs=(), *, tm, name, dep=None):
    s = rows[0][0].shape[0]
    nr, npar, no = len(rows), len(params), len(outs)
    nin = nr + npar + (0 if dep is None else 1)
    in_specs = [pl.BlockSpec((tm, w), functools.partial(lambda i, cb: (i, cb), cb=cb)) for (_, cb, w) in rows]
    in_specs += [pl.BlockSpec(p.shape, functools.partial(lambda i, nd: (0,) * nd, nd=p.ndim)) for p in params]
    if dep is not None:
        in_specs.append(pl.BlockSpec(memory_space=pl.ANY))
    out_shape = [SDS((s, w), dt) for (w, dt) in outs] + [SDS(tuple(sh), F32) for sh in accs]
    out_specs = [pl.BlockSpec((tm, w), lambda i: (i, 0)) for (w, _) in outs]
    out_specs += [pl.BlockSpec(tuple(sh), functools.partial(lambda i, nd: (0,) * nd, nd=len(sh))) for sh in accs]

    def body(*refs):
        rin, pin = refs[:nr], refs[nr:nr + npar]
        oo, ao = refs[nin:nin + no], refs[nin + no:]
        res = f(*[r[...] for r in rin], *[p[...] for p in pin])
        if not isinstance(res, (tuple, list)):
            res = (res,)
        for o_ref, v in zip(oo, res[:no]):
            o_ref[...] = v.astype(o_ref.dtype)
        i = pl.program_id(0)
        for a_ref, v in zip(ao, res[no:]):
            @pl.when(i == 0)
            def _():
                a_ref[...] = jnp.zeros_like(a_ref)

            a_ref[...] += v.reshape(a_ref.shape)

    res = pl.pallas_call(
        body,
        name=name,
        grid=(s // tm,),
        in_specs=in_specs,
        out_specs=out_specs,
        out_shape=out_shape,
        compiler_params=pltpu.CompilerParams(dimension_semantics=("arbitrary",), vmem_limit_bytes=VMEM_LIMIT),
    )(*[r[0] for r in rows], *params, *([] if dep is None else [dep]))
    return res


def _mmv(a, b, mode):
    ca = 0 if mode[0] == "t" else 1
    cb = 1 if mode[1] == "t" else 0
    return lax.dot_general(a.astype(BF16), b.astype(BF16), (((ca,), (cb,)), ((), ())), preferred_element_type=F32)


@functools.partial(jax.custom_vjp, nondiff_argnums=(2,))
def _bdot(a, b, mode):
    return _mmv(a, b, mode)


def _bdot_fwd(a, b, mode):
    return _mmv(a, b, mode), (a, b)


def _bdot_bwd(mode, saved, g):
    a, b = saved
    if mode == "nn":
        return _mmv(g, b, "nt"), _mmv(a, g, "tn")
    if mode == "nt":
        return _mmv(g, b, "nn"), _mmv(g, a, "tn")
    return _mmv(b, g, "nt"), _mmv(a, g, "nn")


_bdot.defvjp(_bdot_fwd, _bdot_bwd)


def _hdot(a, b, mode="nn", precision=HI):
    ca = 0 if mode[0] == "t" else 1
    cb = 1 if mode[1] == "t" else 0
    return lax.dot_general(a, b, (((ca,), (cb,)), ((), ())), precision=precision, preferred_element_type=F32)


def _segsum(x):
    c = x.shape[-1]
    blk = min(c, 256)
    r = lax.broadcasted_iota(jnp.int32, (blk, blk), 0) >> 6
    q = lax.broadcasted_iota(jnp.int32, (blk, blk), 1) >> 6
    ones = jnp.where(r == q, 1.0, 0.0).astype(F32)
    parts = [_hdot(x[:, i:i + blk], ones, precision=lax.Precision.HIGH) for i in range(0, c, blk)]
    return parts[0] if len(parts) == 1 else jnp.concatenate(parts, axis=1)


def _sigmoid(x):
    return jax.nn.sigmoid(x)


def _softplus(x):
    return jnp.maximum(x, 0.0) + jnp.log(1.0 + jnp.exp(-jnp.abs(x)))


def _rms(x, gain):
    return x * lax.rsqrt(jnp.mean(x * x, axis=-1, keepdims=True) + RMS_EPS) * gain


def _swiglu_act(gate, up):
    return gate * _sigmoid(gate) * up


def _rwkv_pre(zs, w0, w2, a0, a2, g2, k_k, k_a):
    r, k, v = zs[:, 0:512], zs[:, 512:1024], zs[:, 1024:1536]
    lora = zs[:, 1536:1792]
    wd, ad, gd = lora[:, 0:64], lora[:, 64:128], lora[:, 128:256]
    w = -_softplus(-(w0 + _bdot(jnp.tanh(wd), w2, "nn"))) - 0.5
    a = _sigmoid(a0 + _bdot(ad, a2, "nn"))
    g = _bdot(_sigmoid(gd), g2, "nn")
    kk = k * k_k
    kk = kk * lax.rsqrt(jnp.maximum(_segsum(kk * kk), 1e-24))
    k2 = k * (1.0 + (a - 1.0) * k_a)
    return r, -jnp.exp(w), k2, v, -kk, kk * a, g


def _rwkv_post(y, r, k2, v, g, gn_w, gn_b, r_k):
    mean = _segsum(y) * (1.0 / HEAD)
    yc = y - mean
    var = _segsum(yc * yc) * (1.0 / HEAD)
    yn = yc * lax.rsqrt(var + GN_EPS) * gn_w + gn_b
    bonus = _segsum(r * k2 * r_k) * v
    return (yn + bonus) * g


def _swap_halves(x):
    lane = lax.broadcasted_iota(jnp.int32, x.shape, 1)
    return jnp.where((lane & 32) == 0, jnp.roll(x, -32, axis=1), jnp.roll(x, 32, axis=1))


def _norm_rope(x, gain, cos, sin):
    heads = x.shape[1] // HEAD
    def rep(t):
        return jnp.concatenate([t] * heads, axis=1)

    xn = x * lax.rsqrt(_segsum(x * x) * (1.0 / HEAD) + RMS_EPS) * rep(gain)
    return xn * rep(cos) + _swap_halves(xn) * rep(sin)


def _attn_combine(o0, o1, o2, l0, l1, l2):
    m = jnp.maximum(jnp.maximum(l0, l1), l2)
    e0, e1, e2 = jnp.exp(l0 - m), jnp.exp(l1 - m), jnp.exp(l2 - m)
    return (e0 * o0 + e1 * o1 + e2 * o2) / (e0 + e1 + e2)


def _merge(zgr, zga, br, ba):
    return _sigmoid(zgr) * br + _sigmoid(zga) * ba


def _attn_block(q, kp, kc, vp, vc, has_prev):
    iq = lax.broadcasted_iota(jnp.int32, (1, BAND, BAND), 1)
    ik = lax.broadcasted_iota(jnp.int32, (1, BAND, BAND), 2)
    s_c = jnp.where(iq >= ik, _bdotb(q, kc, "nt") * (HEAD ** -0.5), NEG_INF)
    s_p = jnp.where(jnp.logical_and(iq <= ik, has_prev), _bdotb(q, kp, "nt") * (HEAD ** -0.5), NEG_INF)
    m = lax.stop_gradient(jnp.maximum(jnp.max(s_c, axis=-1, keepdims=True), jnp.max(s_p, axis=-1, keepdims=True)))
    e_c, e_p = jnp.exp(s_c - m), jnp.exp(s_p - m)
    l = jnp.sum(e_c, axis=-1, keepdims=True) + jnp.sum(e_p, axis=-1, keepdims=True)
    o = (_bdotb(e_c, vc) + _bdotb(e_p, vp)) / l
    return o, jnp.broadcast_to(m + jnp.log(l), o.shape)


def _mmb(a, b, cb):
    return lax.dot_general(a.astype(BF16), b.astype(BF16), (((2,), (cb,)), ((0,), (0,))), preferred_element_type=F32)


@functools.partial(jax.custom_vjp, nondiff_argnums=(2,))
def _bdotb1(a, b, cb):
    return _mmb(a, b, cb)


def _bdotb1_fwd(a, b, cb):
    return _mmb(a, b, cb), (a, b)


def _bdotb1_bwd(cb, saved, g):
    a, b = saved
    if cb == 1:
        return _mmb(g, b, 2), _mmb(jnp.swapaxes(a, 1, 2), g, 1)
    return _mmb(g, b, 1), _mmb(jnp.swapaxes(g, 1, 2), a, 1)


_bdotb1.defvjp(_bdotb1_fwd, _bdotb1_bwd)


def _bdotb(a, b, mode="nn", precision=None):
    if mode[0] == "t":
        a = jnp.swapaxes(a, 1, 2)
    cb = 2 if mode[1] == "t" else 1
    if precision is None:
        return _bdotb1(a, b, cb)
    return lax.dot_general(a, b, (((2,), (cb,)), ((0,), (0,))), precision=precision, preferred_element_type=F32)


def _tri_inv_levels(a):
    t = a.shape[-1]
    row = lax.broadcasted_iota(jnp.int32, (1, t, t), 1)
    col = lax.broadcasted_iota(jnp.int32, (1, t, t), 2)
    x = jnp.where(row == col, 1.0, 0.0).astype(F32) + jnp.where(jnp.logical_and(row == col + 1, (row & 1) == 1), a, 0.0)
    sh = 1
    while (1 << sh) < t:
        m = jnp.logical_and((row >> sh) == (col >> sh) + 1, (row >> (sh + 1)) == (col >> (sh + 1)))
        x = x + _bdotb(_bdotb(x, jnp.where(m, a, 0.0), precision=lax.Precision.HIGH), x, precision=lax.Precision.HIGH)
        sh += 1
    return x


@jax.custom_vjp
def _tri_inv(a):
    return _tri_inv_levels(a)


def _tri_inv_fwd(a):
    x = _tri_inv_levels(a)
    return x, x


def _tri_inv_bwd(x, g):
    xt = jnp.swapaxes(x, 1, 2)
    return (_bdotb(_bdotb(xt, g, precision=lax.Precision.HIGH), xt, precision=lax.Precision.HIGH),)


_tri_inv.defvjp(_tri_inv_fwd, _tri_inv_bwd)


@jax.custom_vjp
def _known_inv(a, x):
    return x


def _known_inv_fwd(a, x):
    return x, x


def _known_inv_bwd(x, g):
    return _tri_inv_bwd(x, g)[0], jnp.zeros_like(x)


_known_inv.defvjp(_known_inv_fwd, _known_inv_bwd)


def _wkv_chunk(s0, r, lw, k, v, a, b, inv=None, with_inv=False):
    nh, t, _ = r.shape
    row = lax.broadcasted_iota(jnp.int32, (1, t, t), 1)
    col = lax.broadcasted_iota(jnp.int32, (1, t, t), 2)
    incl, strict = row >= col, row > col
    ones = jnp.broadcast_to(jnp.where(incl, 1.0, 0.0).astype(F32), (nh, t, t))
    cum = _bdotb(ones, lw, precision=HI)
    c_end = cum[:, t - 1:t, :]
    e_in, e_ex, e_inv = jnp.exp(cum), jnp.exp(cum - lw), jnp.exp(-cum)
    at, rt, bt, kt = a * e_ex, r * e_in, b * e_inv, k * e_inv
    a_ab = jnp.where(strict, _bdotb(at, bt, "nt"), 0.0)
    a_ak = jnp.where(strict, _bdotb(at, kt, "nt"), 0.0)
    x = _tri_inv(a_ab) if inv is None else _known_inv(a_ab, inv)
    u = _bdotb(x, _bdotb(at, s0, "nt") + _bdotb(a_ak, v))
    y = (_bdotb(rt, s0, "nt") + _bdotb(jnp.where(incl, _bdotb(rt, bt, "nt"), 0.0), u)
         + _bdotb(jnp.where(incl, _bdotb(rt, kt, "nt"), 0.0), v))
    w_end = jnp.exp(c_end - cum)
    s1 = s0 * jnp.exp(c_end) + _bdotb(u, b * w_end, "tn") + _bdotb(v, k * w_end, "tn")
    return (y, s1, x) if with_inv else (y, s1)


def _shift_fwd(z, mu):
    s, c = z.shape
    tc = 256

    def body(z_ref, mu_ref, o_ref):
        zz = z_ref[...]
        row = lax.broadcasted_iota(jnp.int32, zz.shape, 0)
        prev = jnp.where(row == 0, 0.0, pltpu.roll(zz, 1, 0))
        o_ref[...] = zz + (prev - zz) * mu_ref[...]

    return pl.pallas_call(
        body, name="shift_fwd", grid=(c // tc,),
        in_specs=[pl.BlockSpec((s, tc), lambda j: (0, j)), pl.BlockSpec((1, tc), lambda j: (0, j))],
        out_specs=pl.BlockSpec((s, tc), lambda j: (0, j)), out_shape=SDS((s, c), F32),
        compiler_params=pltpu.CompilerParams(dimension_semantics=("parallel",), vmem_limit_bytes=VMEM_LIMIT),
    )(z, mu)


def _shift_bwd(z, mu, dzs):
    s, c = z.shape
    tc = 256

    def body(z_ref, mu_ref, d_ref, dz_ref, dmu_ref):
        zz, d, m = z_ref[...], d_ref[...], mu_ref[...]
        row = lax.broadcasted_iota(jnp.int32, zz.shape, 0)
        prev = jnp.where(row == 0, 0.0, pltpu.roll(zz, 1, 0))
        t = d * m
        nxt = jnp.where(row == s - 1, 0.0, pltpu.roll(t, s - 1, 0))
        dz_ref[...] = (d - t + nxt).astype(dz_ref.dtype)
        dmu_ref[...] = jnp.sum(d * (prev - zz), axis=0, keepdims=True)

    return pl.pallas_call(
        body, name="shift_bwd", grid=(c // tc,),
        in_specs=[pl.BlockSpec((s, tc), lambda j: (0, j)), pl.BlockSpec((1, tc), lambda j: (0, j)),
                  pl.BlockSpec((s, tc), lambda j: (0, j))],
        out_specs=[pl.BlockSpec((s, tc), lambda j: (0, j)), pl.BlockSpec((1, tc), lambda j: (0, j))],
        out_shape=[SDS((s, c), BF16), SDS((1, c), F32)],
        compiler_params=pltpu.CompilerParams(dimension_semantics=("parallel",), vmem_limit_bytes=VMEM_LIMIT),
    )(z, mu, dzs)


def _heads(x, nh):
    return jnp.stack([x[:, h * HEAD:(h + 1) * HEAD] for h in range(nh)], axis=0)


def _unheads(x):
    return jnp.concatenate([x[h] for h in range(x.shape[0])], axis=1)


def _wkv_fwd(zs, lw, k2, na, b):
    s = lw.shape[0]
    t, hb = WKV_CHUNK, WKV_HEADS_PER_STEP
    w = hb * HEAD
    nc, ng = s // t, RWKV_HEADS // hb

    def body(r_ref, v_ref, lw_ref, k_ref, a_ref, b_ref, y_ref, s0_ref, x_ref, state):
        @pl.when(pl.program_id(1) == 0)
        def _():
            state[...] = jnp.zeros_like(state)

        s0 = state[...]
        s0_ref[0] = s0
        y, s1, x = _wkv_chunk(s0, *[_heads(t_ref[...], hb) for t_ref in (r_ref, lw_ref, k_ref, v_ref, a_ref, b_ref)], with_inv=True)
        y_ref[...] = _unheads(y)
        x_ref[0] = x
        state[...] = s1

    def col(off):
        return pl.BlockSpec((t, w), functools.partial(lambda g, i, off: (i, g + off), off=off))

    return pl.pallas_call(
        body, name="wkv_fwd", grid=(ng, nc),
        in_specs=[col(0), col(2 * ng), col(0), col(0), col(0), col(0)],
        out_specs=[col(0), pl.BlockSpec((1, hb, HEAD, HEAD), lambda g, i: (i, g, 0, 0)),
                   pl.BlockSpec((1, hb, t, t), lambda g, i: (i, g, 0, 0))],
        out_shape=[SDS((s, RWKV_DIM), F32), SDS((nc, RWKV_HEADS, HEAD, HEAD), F32), SDS((nc, RWKV_HEADS, t, t), F32)],
        scratch_shapes=[pltpu.VMEM((hb, HEAD, HEAD), F32)],
        compiler_params=pltpu.CompilerParams(dimension_semantics=("parallel", "arbitrary"), vmem_limit_bytes=VMEM_LIMIT),
    )(zs, zs, lw, k2, na, b)


def _wkv_bwd(zs, lw, k2, na, b, s0s, invs, dy):
    s = lw.shape[0]
    t, hb = WKV_CHUNK, WKV_HEADS_PER_STEP
    w = hb * HEAD
    nc, ng = s // t, RWKV_HEADS // hb

    def body(r_ref, v_ref, lw_ref, k_ref, a_ref, b_ref, s0_ref, x_ref, dy_ref, dr_ref, dlw_ref, dk_ref, dv_ref, da_ref, db_ref, dstate):
        @pl.when(pl.program_id(1) == 0)
        def _():
            dstate[...] = jnp.zeros_like(dstate)

        _, vjp = jax.vjp(functools.partial(_wkv_chunk, inv=x_ref[0]), s0_ref[0],
                         *[_heads(t_ref[...], hb) for t_ref in (r_ref, lw_ref, k_ref, v_ref, a_ref, b_ref)])
        grads = vjp((_heads(dy_ref[...], hb), dstate[...]))
        dstate[...] = grads[0]
        for o_ref, gval in zip((dr_ref, dlw_ref, dk_ref, dv_ref, da_ref, db_ref), grads[1:]):
            o_ref[...] = _unheads(gval)

    def col(off):
        return pl.BlockSpec((t, w), functools.partial(lambda g, i, off: (nc - 1 - i, g + off), off=off))

    return pl.pallas_call(
        body, name="wkv_bwd", grid=(ng, nc),
        in_specs=[col(0), col(2 * ng), col(0), col(0), col(0), col(0),
                  pl.BlockSpec((1, hb, HEAD, HEAD), lambda g, i: (nc - 1 - i, g, 0, 0)),
                  pl.BlockSpec((1, hb, t, t), lambda g, i: (nc - 1 - i, g, 0, 0)), col(0)],
        out_specs=[col(0)] * 6,
        out_shape=[SDS((s, RWKV_DIM), F32)] * 6,
        scratch_shapes=[pltpu.VMEM((hb, HEAD, HEAD), F32)],
        compiler_params=pltpu.CompilerParams(dimension_semantics=("parallel", "arbitrary"), vmem_limit_bytes=VMEM_LIMIT),
    )(zs, zs, lw, k2, na, b, s0s, invs, dy)


def _attn_fwd(q, k, v, d):
    s = q.shape[0]
    l = s // d
    nb = l // BAND
    assert nb * BAND == l
    qv, kv, vv = (t.reshape(l, d * GROUP_DIM) for t in (q, k, v))
    nh = GROUP_DIM // HEAD

    def body(q_ref, kp_ref, kc_ref, vp_ref, vc_ref, o_ref, l_ref):
        has_prev = pl.program_id(1) > 0
        o, lse = _attn_block(*[_heads(t_ref[...].astype(F32), nh) for t_ref in (q_ref, kp_ref, kc_ref, vp_ref, vc_ref)], has_prev)
        o_ref[...] = _unheads(o)
        l_ref[...] = _unheads(lse)

    cur = pl.BlockSpec((BAND, GROUP_DIM), lambda rho, i: (i, rho))
    prev = pl.BlockSpec((BAND, GROUP_DIM), lambda rho, i: (jnp.maximum(i - 1, 0), rho))
    o, lse = pl.pallas_call(
        body, name=f"attn_fwd_d{d}", grid=(d, nb),
        in_specs=[cur, prev, cur, prev, cur], out_specs=[cur, cur],
        out_shape=[SDS((l, d * GROUP_DIM), F32), SDS((l, d * GROUP_DIM), F32)],
        compiler_params=pltpu.CompilerParams(dimension_semantics=("parallel", "arbitrary"), vmem_limit_bytes=VMEM_LIMIT),
    )(qv, kv, kv, vv, vv)
    return o.reshape(s, GROUP_DIM), lse.reshape(s, GROUP_DIM)


def _attn_bwd(q, k, v, d, do, dlse):
    s = q.shape[0]
    l = s // d
    nb = l // BAND
    qv, kv, vv, dov, dlv = (t.reshape(l, d * GROUP_DIM) for t in (q, k, v, do, dlse))
    nh = GROUP_DIM // HEAD

    def body(q_ref, kp_ref, kc_ref, vp_ref, vc_ref, do_ref, dl_ref, dq_ref, dk_ref, dv_ref, ck, cv):
        step = pl.program_id(1)
        has_prev = step < nb - 1

        @pl.when(step == 0)
        def _():
            ck[...] = jnp.zeros_like(ck)
            cv[...] = jnp.zeros_like(cv)

        _, vjp = jax.vjp(functools.partial(_attn_block, has_prev=has_prev),
                         *[_heads(t_ref[...].astype(F32), nh) for t_ref in (q_ref, kp_ref, kc_ref, vp_ref, vc_ref)])
        dq, dkp, dkc, dvp, dvc = vjp((_heads(do_ref[...], nh), _heads(dl_ref[...], nh)))
        dq_ref[...] = _unheads(dq)
        dk_ref[...] = _unheads(dkc) + ck[...]
        dv_ref[...] = _unheads(dvc) + cv[...]
        ck[...] = _unheads(dkp)
        cv[...] = _unheads(dvp)

    cur = pl.BlockSpec((BAND, GROUP_DIM), lambda rho, i: (nb - 1 - i, rho))
    prev = pl.BlockSpec((BAND, GROUP_DIM), lambda rho, i: (jnp.maximum(nb - 2 - i, 0), rho))
    dq, dk, dv = pl.pallas_call(
        body, name=f"attn_bwd_d{d}", grid=(d, nb),
        in_specs=[cur, prev, cur, prev, cur, cur, cur], out_specs=[cur] * 3,
        out_shape=[SDS((l, d * GROUP_DIM), F32)] * 3,
        scratch_shapes=[pltpu.VMEM((BAND, GROUP_DIM), F32), pltpu.VMEM((BAND, GROUP_DIM), F32)],
        compiler_params=pltpu.CompilerParams(dimension_semantics=("parallel", "arbitrary"), vmem_limit_bytes=VMEM_LIMIT),
    )(qv, kv, kv, vv, vv, dov, dlv)
    return dq.reshape(s, GROUP_DIM), dk.reshape(s, GROUP_DIM), dv.reshape(s, GROUP_DIM)


def _coords():
    return lax.axis_index("x"), lax.axis_index("y"), lax.axis_index("c")


_CHIP_FLIPS = ((1, 0), (0, 1), (1, 1))


def _flip(v, f):
    return 1 - v if f else v


def _form(kind, r, c):
    return (N_CHIPS, r, c) if kind == "blk" else (r, N_CHIPS * c)


def _slot(ref, kind, j, rows, c):
    if kind == "blk":
        return ref.at[j] if rows is None else ref.at[j, rows]
    cols = pl.ds(pl.multiple_of(j * c, 128), c)
    return ref.at[:, cols] if rows is None else ref.at[rows, cols]


def _half(r, which, align):
    return pl.ds(pl.multiple_of(which * (r // 2), align), r // 2)


def _rcopy(src, dst, send_sems, recv_sems, kk, dev):
    return pltpu.make_async_remote_copy(src_ref=src, dst_ref=dst, send_sem=send_sems.at[kk], recv_sem=recv_sems.at[kk],
                                        device_id=dev, device_id_type=MESH)


def _gather_plan(specs, step):
    def copies(refs, ss, rs, received):
        x, y, c = _coords()
        out = []
        for w, (kind, r, cc) in enumerate(specs):
            mine, other = _half(r, c, 16), _half(r, 1 - c, 16)
            for kk, (fx, fy) in enumerate(_CHIP_FLIPS):
                px, py = _flip(x, fx), _flip(y, fy)
                if step == "ici":
                    sl = _slot(refs[w], kind, 2 * px + py if received else 2 * x + y, mine, cc)
                    dev = (px, py, c)
                else:
                    sl = _slot(refs[w], kind, 2 * px + py, other if received else mine, cc)
                    dev = (x, y, 1 - c)
                out.append(_rcopy(sl, sl, ss, rs, 3 * w + kk, dev))
        return out

    def issue(refs, ss, rs):
        return copies(refs, ss, rs, False)

    def expect(refs, ss, rs):
        return copies(refs, ss, rs, False), copies(refs, ss, rs, True)

    return issue, expect


_HBM = pl.BlockSpec(memory_space=pltpu.HBM)
_SEM = pl.BlockSpec(memory_space=pltpu.SEMAPHORE)
_EFFECT = pltpu.SideEffectType.DATAFLOW_SIDE_EFFECTING


def _copies_start(name, bufs, n_sems, issue, after=None):
    nb = len(bufs)
    extra = [] if after is None else [after]

    def body(*refs):
        send_sems, recv_sems = refs[nb + len(extra)], refs[nb + len(extra) + 1]
        for cp in issue(refs[:nb], send_sems, recv_sems):
            cp.start()
        refs[-1][...] = jnp.zeros_like(refs[-1])

    outs = pl.pallas_call(
        body, name=name,
        out_shape=(pltpu.SemaphoreType.DMA((n_sems,)), pltpu.SemaphoreType.DMA((n_sems,)),
                   *[pltpu.HBM(b.shape, b.dtype) for b in bufs], SDS((8, 128), F32)),
        in_specs=[_HBM] * nb + [pl.BlockSpec(memory_space=pl.ANY)] * len(extra),
        out_specs=(_SEM, _SEM, *[_HBM] * nb, pl.BlockSpec(memory_space=pltpu.VMEM)),
        input_output_aliases={i: 2 + i for i in range(nb)},
        compiler_params=pltpu.CompilerParams(has_side_effects=_EFFECT),
    )(*[pltpu.with_memory_space_constraint(b, pltpu.HBM) for b in bufs], *extra)
    return outs[0], outs[1], list(outs[2:2 + nb]), outs[-1]


def _copies_wait(name, bufs, send_sems, recv_sems, after, expect):
    nb = len(bufs)

    def body(*refs):
        sent, received = expect(refs[:nb], refs[nb], refs[nb + 1])
        for cp in sent:
            cp.wait_send()
        for cp in received:
            cp.wait_recv()

    outs = pl.pallas_call(
        body, name=name,
        out_shape=tuple(pltpu.HBM(b.shape, b.dtype) for b in bufs),
        in_specs=(*[_HBM] * nb, _SEM, _SEM, pl.BlockSpec(memory_space=pl.ANY)), out_specs=tuple([_HBM] * nb),
        input_output_aliases={i: i for i in range(nb)},
        compiler_params=pltpu.CompilerParams(has_side_effects=_EFFECT),
    )(*bufs, send_sems, recv_sems, after)
    return list(outs)


def _add_pair(g, recv, kind, r, c, c_arr, name):
    h = r // 2
    if kind == "blk":
        tr = _row_tile(h, 512)
        grid = (N_CHIPS, h // tr)
        g_spec = pl.BlockSpec((1, 1, tr, c), lambda j, i, c_ref: (j, c_ref[0], i, 0))
        o_spec = pl.BlockSpec((1, tr, c), lambda j, i, c_ref: (j, i, 0))
        gv, oshape = g.reshape(N_CHIPS, 2, h, c), (N_CHIPS, h, c)
    else:
        tr = _row_tile(h, 64)
        grid = (h // tr,)
        g_spec = pl.BlockSpec((1, tr, N_CHIPS * c), lambda i, c_ref: (c_ref[0], i, 0))
        o_spec = pl.BlockSpec((tr, N_CHIPS * c), lambda i, c_ref: (i, 0))
        gv, oshape = g.reshape(2, h, N_CHIPS * c), (h, N_CHIPS * c)

    def body(c_ref, g_ref, r_ref, o_ref, ob_ref):
        v = (g_ref[:, 0] if kind == "blk" else g_ref[0]) + r_ref[...]
        o_ref[...] = v
        ob_ref[...] = v.astype(BF16)

    return pl.pallas_call(
        body, name=name,
        grid_spec=pltpu.PrefetchScalarGridSpec(num_scalar_prefetch=1, grid=grid, in_specs=[g_spec, o_spec], out_specs=[o_spec] * 2),
        out_shape=[SDS(oshape, F32), SDS(oshape, BF16)],
        compiler_params=pltpu.CompilerParams(vmem_limit_bytes=VMEM_LIMIT),
    )(c_arr, gv, recv)


def _sum_adamw(pair, recv, w, m, v, kind, r, c, mc_arr, name):
    h = r // 2
    tr = _row_tile(h, 256)
    nt = h // tr
    if kind == "blk":
        p_spec = pl.BlockSpec((1, tr, c), lambda i, mc: (mc[0], i, 0))
    else:
        p_spec = pl.BlockSpec((tr, c), lambda i, mc: (i, mc[0]))
    mine = pl.BlockSpec((tr, c), lambda i, mc: (mc[1] * nt + i, 0))

    def body(mc, a_ref, r_ref, w_ref, m_ref, v_ref, g_out, d_out, m_out, v_out):
        own = a_ref[0] if kind == "blk" else a_ref[...]
        g = ((own + r_ref[0].astype(F32)) + r_ref[1].astype(F32)) + r_ref[2].astype(F32)
        g_out[...] = g
        d_out[...], m_out[...], v_out[...] = _adamw_rows(w_ref[...], g, m_ref[...], v_ref[...])

    return pl.pallas_call(
        body, name=name,
        grid_spec=pltpu.PrefetchScalarGridSpec(
            num_scalar_prefetch=1, grid=(nt,),
            in_specs=[p_spec, pl.BlockSpec((3, tr, c), lambda i, mc: (0, i, 0)), mine, mine, mine], out_specs=[mine] * 4),
        out_shape=[SDS((r, c), F32)] * 4,
        compiler_params=pltpu.CompilerParams(vmem_limit_bytes=VMEM_LIMIT),
    )(mc_arr, pair, recv, w, m, v)


class _GroupReduce:
    def __init__(self, tag, specs, c_arr, mc_arr):
        self.tag, self.specs, self.c_arr, self.mc_arr = tag, specs, c_arr, mc_arr
        self.n = len(specs)

    def _plan(self, step):
        specs, n = self.specs, self.n

        def copies(refs, ss, rs, received):
            x, y, c = _coords()
            sib, out = (x, y, 1 - c), []
            for w, (_, kind, r, cc) in enumerate(specs):
                if step == "join":
                    for q in range(4):
                        there = refs[4 * w + q].at[_half(r, 1 - c if received else c, 8)]
                        out.append(_rcopy(there, there, ss, rs, 4 * w + q, sib))
                    continue
                src, land = refs[w], refs[n + w]
                if step == "swap":
                    rows = _half(r, 1 - c, 8)
                    part = src.at[:, rows] if kind == "blk" else src.at[rows]
                    out.append(_rcopy(land if received else part, land, ss, rs, w, sib))
                else:
                    for kk, (fx, fy) in enumerate(_CHIP_FLIPS):
                        px, py = _flip(x, fx), _flip(y, fy)
                        part = land.at[kk] if received else _slot(src, kind, 2 * px + py, None, cc)
                        out.append(_rcopy(part, land.at[kk], ss, rs, 3 * w + kk, (px, py, c)))
            return out

        def issue(refs, ss, rs):
            return copies(refs, ss, rs, False)

        def expect(refs, ss, rs):
            return copies(refs, ss, rs, False), copies(refs, ss, rs, True)

        return issue, expect

    def swap_start(self, grads, after=None):
        lands = [lax.empty(_form(kind, r // 2, c), F32) for _, kind, r, c in self.specs]
        ss, rs, bufs, tok = _copies_start(f"rs_{self.tag}_swap", list(grads) + lands, self.n, self._plan("swap")[0], after=after)
        self.state = (ss, rs, bufs)
        return tok

    def swap_wait_ici_start(self, after):
        ss, rs, bufs = self.state
        bufs = _copies_wait(f"rs_{self.tag}_swap_wait", bufs, ss, rs, after, self._plan("swap")[1])
        pairs = [_add_pair(bufs[w], bufs[self.n + w], kind, r, c, self.c_arr, name=f"rs_{self.tag}_pair_{nm}")
                 for w, (nm, kind, r, c) in enumerate(self.specs)]
        self.pair = [pr[0] for pr in pairs]
        lands = [lax.empty((3, r // 2, c), BF16) for _, _, r, c in self.specs]
        ss, rs, bufs, tok = _copies_start(f"rs_{self.tag}_ici", [pr[1] for pr in pairs] + lands, 3 * self.n, self._plan("ici")[0])
        self.state = (ss, rs, bufs)
        return tok

    def ici_wait_join_start(self, after, state):
        ss, rs, bufs = self.state
        bufs = _copies_wait(f"rs_{self.tag}_ici_wait", bufs, ss, rs, after, self._plan("ici")[1])
        outs = []
        for w, (nm, kind, r, c) in enumerate(self.specs):
            outs += _sum_adamw(self.pair[w], bufs[self.n + w], *state[nm], kind, r, c, self.mc_arr, name=f"rs_{self.tag}_adamw_{nm}")
        ss, rs, bufs, tok = _copies_start(f"rs_{self.tag}_join", outs, 4 * self.n, self._plan("join")[0])
        self.state = (ss, rs, bufs)
        return tok

    def join_wait(self, after):
        ss, rs, bufs = self.state
        bufs = _copies_wait(f"rs_{self.tag}_join_wait", bufs, ss, rs, after, self._plan("join")[1])
        return {nm: tuple(bufs[4 * w:4 * w + 4]) for w, (nm, _, _, _) in enumerate(self.specs)}


def _all_reduce_small(buf):
    rows, cols = buf.shape

    def body(x_ref, o_ref, gath, send_sems, recv_sems):
        x, y, c = _coords()
        me = 4 * x + 2 * y + c
        gath[me] = x_ref[...]
        sends = []
        for kk in range(1, 8):
            f = (kk >> 2) & 1, (kk >> 1) & 1, kk & 1
            px, py, pc = _flip(x, f[0]), _flip(y, f[1]), _flip(c, f[2])
            cp = pltpu.make_async_remote_copy(src_ref=x_ref, dst_ref=gath.at[me], send_sem=send_sems.at[kk - 1],
                                              recv_sem=recv_sems.at[kk - 1], device_id=(px, py, pc), device_id_type=MESH)
            cp.start()
            sends.append(cp)
        for kk in range(1, 8):
            f = (kk >> 2) & 1, (kk >> 1) & 1, kk & 1
            px, py, pc = _flip(x, f[0]), _flip(y, f[1]), _flip(c, f[2])
            there = gath.at[4 * px + 2 * py + pc]
            pltpu.make_async_remote_copy(src_ref=there, dst_ref=there, send_sem=send_sems.at[kk - 1],
                                         recv_sem=recv_sems.at[kk - 1], device_id=(px, py, pc), device_id_type=MESH).wait_recv()
        for cp in sends:
            cp.wait_send()
        acc = gath[0]
        for j in range(1, 8):
            acc = acc + gath[j]
        o_ref[...] = acc

    return pl.pallas_call(
        body, name="all_reduce_small",
        in_specs=[pl.BlockSpec(memory_space=pltpu.VMEM)], out_specs=pl.BlockSpec(memory_space=pltpu.VMEM),
        out_shape=SDS((rows, cols), F32),
        scratch_shapes=[pltpu.VMEM((8, rows, cols), F32), pltpu.SemaphoreType.DMA((7,)), pltpu.SemaphoreType.DMA((7,))],
    )(buf)


def _adamw_rows(w, g, m, v):
    m = ADAM_B1 * m + (1.0 - ADAM_B1) * g
    v = ADAM_B2 * v + (1.0 - ADAM_B2) * jnp.square(g)
    m_hat = m / (1.0 - ADAM_B1 ** ADAM_STEP)
    v_hat = v / (1.0 - ADAM_B2 ** ADAM_STEP)
    return -ADAM_LR * (m_hat / (jnp.sqrt(v_hat) + ADAM_EPS) + ADAM_WD * w), m, v


def _adamw(w, g, m, v, name, dep=None):
    rows, cols = w.shape
    tm = _pick(rows, (256, 128, 64, 16, 8))
    return _rows_call(_adamw_rows, [(t, 0, cols) for t in (w, g, m, v)], [], [(cols, F32)] * 3, tm=tm, name=name, dep=dep)


def _pack_small(parts):
    flat = jnp.concatenate([parts[n].reshape(-1) for n, _ in SMALL])
    return jnp.pad(flat, (0, SMALL_ROWS * PACK_COLS - flat.shape[0])).reshape(SMALL_ROWS, PACK_COLS)


def _unpack_small(buf, shapes):
    flat, out, off = buf.reshape(-1), {}, 0
    for n, sz in SMALL:
        out[n] = flat[off:off + sz].reshape(shapes[n])
        off += sz
    return out


def _lora_stack(parts):
    return jnp.concatenate([parts[n] for n, _ in LORA], axis=-2)


def _lora_split(stacked):
    out, off = {}, 0
    for n, rows in LORA:
        out[n] = stacked[..., off:off + rows, :]
        off += rows
    return out


def _ffn_gate_up(h, wgt, wut, name, dep=None):
    s, d = h.shape
    nblk, f, _ = wgt.shape
    tm = _pick(s, (1024, 512, 256))
    dn = (((1,), (1,)), ((), ()))

    def body(h_ref, wg_ref, wu_ref, *rest):
        g_ref, u_ref, a_ref = rest[-3:]
        hh = h_ref[...]
        g = lax.dot_general(hh, wg_ref[0], dn, preferred_element_type=F32)
        u = lax.dot_general(hh, wu_ref[0], dn, preferred_element_type=F32)
        g_ref[0], u_ref[0] = g, u
        a_ref[0] = _swiglu_act(g, u).astype(BF16)

    w_spec = pl.BlockSpec((1, f, d), lambda j, i: (j, 0, 0))
    o_spec = pl.BlockSpec((1, tm, f), lambda j, i: (j, i, 0))
    extra = [] if dep is None else [dep]
    return pl.pallas_call(
        body, name=name, grid=(nblk, s // tm),
        in_specs=[pl.BlockSpec((tm, d), lambda j, i: (i, 0)), w_spec, w_spec] + [pl.BlockSpec(memory_space=pl.ANY)] * len(extra),
        out_specs=[o_spec] * 3,
        out_shape=[SDS((nblk, s, f), F32), SDS((nblk, s, f), F32), SDS((nblk, s, f), BF16)],
        compiler_params=pltpu.CompilerParams(dimension_semantics=("parallel", "parallel"), vmem_limit_bytes=VMEM_LIMIT),
    )(h, wgt, wut, *extra)


def _ffn_down_dx(dx_bf, wd, gate, up, name, dep=None):
    s, d = dx_bf.shape
    nblk, f, _ = wd.shape
    tm = _pick(s, (1024, 512, 256))
    dn = (((1,), (1,)), ((), ()))

    def body(dx_ref, wd_ref, g_ref, u_ref, *rest):
        dg_ref, du_ref = rest[-2:]
        dact = 0.5 * lax.dot_general(dx_ref[...], wd_ref[0], dn, preferred_element_type=F32)
        _, vjp = jax.vjp(_swiglu_act, g_ref[0], u_ref[0])
        dg, du = vjp(dact)
        dg_ref[0], du_ref[0] = dg.astype(BF16), du.astype(BF16)

    o_spec = pl.BlockSpec((1, tm, f), lambda j, i: (j, i, 0))
    extra = [] if dep is None else [dep]
    return pl.pallas_call(
        body, name=name, grid=(nblk, s // tm),
        in_specs=[pl.BlockSpec((tm, d), lambda j, i: (i, 0)), pl.BlockSpec((1, f, d), lambda j, i: (j, 0, 0)), o_spec, o_spec]
        + [pl.BlockSpec(memory_space=pl.ANY)] * len(extra),
        out_specs=[o_spec] * 2, out_shape=[SDS((nblk, s, f), BF16)] * 2,
        compiler_params=pltpu.CompilerParams(dimension_semantics=("parallel", "parallel"), vmem_limit_bytes=VMEM_LIMIT),
    )(dx_bf, wd, gate, up, *extra)


def _ffn_fwd(x, gain, wgt, wut, wd, tag, h=None, dep=None):
    if h is None:
        h = _rows_call(_rms, [(x, 0, D_MODEL)], [gain], [(D_MODEL, BF16)], tm=512, name=f"{tag}_norm")[0]
    gate, up, act = _ffn_gate_up(h, wgt, wut, f"{tag}_gate_up", dep=dep)
    x_new = _mm(act, wd, sum_blocks=True, res=x, alpha=0.5, name=f"{tag}_down")
    return x_new, (x, h, gate, up, act)


def _ffn_bwd(dx_new, dx_new_bf, saved, gain, wgt, wut, wd, tag, dep=None, hooks=None):
    x, h, gate, up, act = saved
    hooks = hooks or {}

    def hook(name, *vals):
        return hooks[name](*vals) if name in hooks else None

    d_wd = _mm(act, dx_new_bf, ta=True, alpha=0.5, name=f"{tag}_down_dw")
    dep = hook("down", d_wd) if "down" in hooks else dep
    dgate, dup = _ffn_down_dx(dx_new_bf, wd, gate, up, f"{tag}_down_dx", dep=dep)
    d_wgt = _mm(dgate, h, ta=True, dep=hook("mid", dgate), name=f"{tag}_gate_dw")
    d_wut = _mm(dup, h, ta=True, name=f"{tag}_up_dw")
    dh = _mm(dgate, wgt, sum_blocks=True, dep=hook("dw", d_wgt, d_wut), name=f"{tag}_gate_dx")
    dh = _mm(dup, wut, sum_blocks=True, res=dh, dep=hook("dx", dh), name=f"{tag}_up_dx")
    dx, dx_bf, dgain = _norm_bwd(x, gain, dh, dx_new, f"{tag}_norm_bwd", dep=hook("end", dh))
    return dx, dx_bf, dgain, d_wgt, d_wut, d_wd


def _norm_bwd(x, gain, dh, dres, name, dep=None):
    def f(xt, dht, drt, gt):
        _, vjp = jax.vjp(_rms, xt, gt)
        dxt, dgt = vjp(dht)
        return dxt + drt, dxt + drt, dgt

    return _rows_call(f, [(x, 0, D_MODEL), (dh, 0, D_MODEL), (dres, 0, D_MODEL)], [gain], [(D_MODEL, F32), (D_MODEL, BF16)],
                      [(1, D_MODEL)], tm=512, name=name, dep=dep)


def kernel(x, p, positions, ffn1_norm, ffn1_w_gate, ffn1_w_up, ffn1_w_down, mix_norm, w_in, rwkv_mu, rwkv_w0, rwkv_w2, rwkv_a0, rwkv_a2, rwkv_g2, rwkv_k_k, rwkv_k_a, rwkv_r_k, rwkv_gn_w, rwkv_gn_b, q_norm, k_norm, w_br_rwkv, w_br_attn, w_out, ffn2_norm, ffn2_w_gate, ffn2_w_up, ffn2_w_down, ple_norm, ple_w_gate, ple_w_proj, loss_target, m_ffn1_norm, m_ffn1_w_gate, m_ffn1_w_up, m_ffn1_w_down, m_mix_norm, m_w_in, m_rwkv_mu, m_rwkv_w0, m_rwkv_w2, m_rwkv_a0, m_rwkv_a2, m_rwkv_g2, m_rwkv_k_k, m_rwkv_k_a, m_rwkv_r_k, m_rwkv_gn_w, m_rwkv_gn_b, m_q_norm, m_k_norm, m_w_br_rwkv, m_w_br_attn, m_w_out, m_ffn2_norm, m_ffn2_w_gate, m_ffn2_w_up, m_ffn2_w_down, m_ple_norm, m_ple_w_gate, m_ple_w_proj, v_ffn1_norm, v_ffn1_w_gate, v_ffn1_w_up, v_ffn1_w_down, v_mix_norm, v_w_in, v_rwkv_mu, v_rwkv_w0, v_rwkv_w2, v_rwkv_a0, v_rwkv_a2, v_rwkv_g2, v_rwkv_k_k, v_rwkv_k_a, v_rwkv_r_k, v_rwkv_gn_w, v_rwkv_gn_b, v_q_norm, v_k_norm, v_w_br_rwkv, v_w_br_attn, v_w_out, v_ffn2_norm, v_ffn2_w_gate, v_ffn2_w_up, v_ffn2_w_down, v_ple_norm, v_ple_w_gate, v_ple_w_proj):
    args = dict(locals())
    wts = {n: args[n] for n in WEIGHTS}
    mom_m = {n: args["m_" + n] for n in WEIGHTS}
    mom_v = {n: args["v_" + n] for n in WEIGHTS}
    x0, tgt = x[0], loss_target[0]
    s = x0.shape[0]
    p_tok = p[0, 0]

    vec = {n: wts[n].reshape(1, -1) for n, _ in SMALL}
    xi, yi, ci = _coords()
    me = 2 * xi + yi
    def laid(t, n):
        return jnp.transpose(t[n][0]) if n in TRANSPOSED else t[n][0]

    shard_of = {n: laid(wts, n) for g in GROUPS.values() for n, _, _, _ in g if n != "lora"}
    shard_of["lora"] = _lora_stack({n: wts[n][0] for n, _ in LORA})

    def whole_with_own(n, kind, r, c, tok=None):
        at = (me, 0, 0) if kind == "blk" else (0, me * c)
        own = (shard_of[n] if tok is None else shard_of[n] + tok[0, 0]).astype(BF16)
        return lax.dynamic_update_slice(lax.empty(_form(kind, r, c), BF16), own[None] if kind == "blk" else own, at)

    specs = {g: [(kind, r, c) for _, kind, r, c in grp] for g, grp in GROUPS.items()}
    plans = {(g, st): _gather_plan(specs[g], st) for g in GROUPS for st in ("ici", "d2d")}
    buf_f1 = [whole_with_own(*w) for w in GROUPS["f1"]]
    ss_0, rs_0, buf_f1, tok_0 = _copies_start("gather_f1_ici", buf_f1, 3 * len(buf_f1), plans["f1", "ici"][0])
    bufs = {g: [whole_with_own(*w, tok=tok_0) for w in GROUPS[g]] for g in ("mx", "f2")}
    buf_f1 = _copies_wait("gather_f1_ici_wait", buf_f1, ss_0, rs_0, bufs["mx"][0], plans["f1", "ici"][1])
    ss_1, rs_1, buf_f1, tok_1 = _copies_start("gather_f1_d2d", buf_f1, 3 * len(buf_f1), plans["f1", "d2d"][0])
    h1 = _rows_call(_rms, [(x0, 0, D_MODEL)], [vec["ffn1_norm"] + tok_1[0, 0]], [(D_MODEL, BF16)], tm=512, name="ffn1_norm")[0]
    buf_f1 = _copies_wait("gather_f1_d2d_wait", buf_f1, ss_1, rs_1, h1, plans["f1", "d2d"][1])
    wb = dict(zip([w[0] for w in GROUPS["f1"]], buf_f1))
    ss_a, rs_a, buf_mx, tok_a = _copies_start("gather_mx_ici", bufs["mx"], 3 * len(bufs["mx"]), plans["mx", "ici"][0],
                                              after=wb["ffn1_w_gate"])

    inv_freq = 1.0 / (ROPE_THETA ** (jnp.arange(0, HEAD, 2, dtype=F32) / HEAD))
    ang = positions[0].astype(F32)[:, None] * inv_freq
    cos, sin = jnp.cos(ang), jnp.sin(ang)
    cos2, sin2 = jnp.concatenate([cos, cos], axis=1), jnp.concatenate([-sin, sin], axis=1)

    x1, ffn1_saved = _ffn_fwd(x0, vec["ffn1_norm"], wb["ffn1_w_gate"], wb["ffn1_w_up"], wb["ffn1_w_down"], "ffn1", h=h1, dep=tok_a)
    buf_mx = _copies_wait("gather_mx_ici_wait", buf_mx, ss_a, rs_a, x1, plans["mx", "ici"][1])
    ss_b, rs_b, buf_mx, tok_b = _copies_start("gather_mx_d2d", buf_mx, 3 * len(buf_mx), plans["mx", "d2d"][0])
    ss_c, rs_c, buf_f2, tok_c = _copies_start("gather_f2_ici", bufs["f2"], 3 * len(bufs["f2"]), plans["f2", "ici"][0])
    h = _rows_call(_rms, [(x1, 0, D_MODEL)], [vec["mix_norm"] + (tok_b[0, 0] + tok_c[0, 0])], [(D_MODEL, BF16)], tm=256,
                   name="mix_norm")[0]
    buf_mx = _copies_wait("gather_mx_d2d_wait", buf_mx, ss_b, rs_b, h, plans["mx", "d2d"][1])
    wb.update(zip([w[0] for w in GROUPS["mx"]], buf_mx))
    w_in_all = wb["w_in"]
    w_in_r, w_in_a, w_in_g = w_in_all[:, :RWKV_COLS], w_in_all[:, RWKV_COLS:RWKV_COLS + ATTN_COLS], w_in_all[:, RWKV_COLS + ATTN_COLS:]
    lora = _lora_split(wb["lora"])
    w2, a2, g2 = lora["rwkv_w2"], lora["rwkv_a2"], lora["rwkv_g2"]
    z_r = _mm(h, w_in_r, name="in_rwkv")
    z_a = _mm(h, w_in_a, name="in_attn")
    z_g = _mm(h, w_in_g, name="in_gate")

    zs = _shift_fwd(z_r, vec["rwkv_mu"])
    pre_params = [vec["rwkv_w0"], w2, vec["rwkv_a0"], a2, g2, vec["rwkv_k_k"], vec["rwkv_k_a"]]
    def pre_fwd(*t):
        res = _rwkv_pre(*t)
        return res[1], res[2], res[4], res[5], res[6]

    lw, k2, na, kb, gate_r = _rows_call(pre_fwd, [(zs, 0, RWKV_COLS)], pre_params, [(RWKV_DIM, F32)] * 5, tm=512, name="rwkv_pre")
    y_scan, s0s, invs = _wkv_fwd(zs, lw, k2, na, kb)
    buf_f2 = _copies_wait("gather_f2_ici_wait", buf_f2, ss_c, rs_c, y_scan, plans["f2", "ici"][1])
    ss_d, rs_d, buf_f2, tok_d = _copies_start("gather_f2_d2d", buf_f2, 3 * len(buf_f2), plans["f2", "d2d"][0])
    post_params = [vec["rwkv_gn_w"] + tok_d[0, 0], vec["rwkv_gn_b"], vec["rwkv_r_k"]]
    post_rows = [(y_scan, 0, RWKV_DIM), (zs, 0, RWKV_DIM), (k2, 0, RWKV_DIM), (zs, 2, RWKV_DIM), (gate_r, 0, RWKV_DIM)]
    y_rwkv = _rows_call(_rwkv_post, post_rows, post_params, [(RWKV_DIM, BF16)], tm=512, name="rwkv_post")[0]
    buf_f2 = _copies_wait("gather_f2_d2d_wait", buf_f2, ss_d, rs_d, y_rwkv, plans["f2", "d2d"][1])
    wb.update(zip([w[0] for w in GROUPS["f2"]], buf_f2))
    w_brr, w_bra = wb["w_br_rwkv"], wb["w_br_attn"]
    w_o = wb["w_out"].reshape(D_MODEL, D_MODEL)
    w_pp, w_pg = wb["ple_w_proj"], wb["ple_w_gate"].reshape(D_MODEL, D_MODEL)

    def qk_fwd(qt, kt, ct, st, qg, kg):
        return _norm_rope(qt, qg, ct, st), _norm_rope(kt, kg, ct, st)

    qk_rows = [(z_a, 0, ATTN_DIM), (z_a, 1, ATTN_DIM), (cos2, 0, HEAD), (sin2, 0, HEAD)]
    q_rot, k_rot = _rows_call(qk_fwd, qk_rows, [vec["q_norm"], vec["k_norm"]], [(ATTN_DIM, BF16)] * 2, tm=512, name="attn_pre")
    def group(t, g, off=0):
        return t[:, off + g * GROUP_DIM:off + (g + 1) * GROUP_DIM].astype(BF16)

    qkv = [(group(q_rot, g), group(k_rot, g), group(z_a, g, 2 * ATTN_DIM)) for g in range(len(ATTN_DILATIONS))]
    outs, lses = zip(*[_attn_fwd(*qkv[g], d) for g, d in enumerate(ATTN_DILATIONS)])
    comb_rows = [(t, 0, GROUP_DIM) for t in outs + lses]
    y_attn = _rows_call(_attn_combine, comb_rows, [], [(GROUP_DIM, BF16)], tm=512, name="attn_combine")[0]

    br = _mm(y_rwkv, w_brr, name="branch_rwkv")
    ba = _mm(y_attn, w_bra, name="branch_attn")
    merge_rows = [(z_g, 0, D_MODEL), (z_g, 1, D_MODEL), (br, 0, D_MODEL), (ba, 0, D_MODEL)]
    merged = _rows_call(_merge, merge_rows, [], [(D_MODEL, BF16)], tm=512, name="merge")[0]
    x2 = _mm(merged, w_o, res=x1, name="out_proj")
    x3, ffn2_saved = _ffn_fwd(x2, vec["ffn2_norm"], wb["ffn2_w_gate"], wb["ffn2_w_up"], wb["ffn2_w_down"], "ffn2")
    hp = _rows_call(_rms, [(x3, 0, D_MODEL)], [vec["ple_norm"]], [(D_MODEL, BF16)], tm=512, name="ple_norm")[0]
    pg = _mm(hp, w_pg, name="ple_gate")
    pp = _mm(p_tok, w_pp, name="ple_proj")

    def head(x3t, pgt, ppt, tt):
        sg = _sigmoid(pgt)
        err = x3t + sg * ppt - tt
        dx4 = err * (1.0 / D_MODEL)
        loss = 0.5 * jnp.sum(jnp.mean(err * err, axis=-1, keepdims=True), axis=0, keepdims=True)
        return dx4, dx4 * ppt * sg * (1.0 - sg), dx4 * sg, jnp.broadcast_to(loss, (8, 128))

    head_rows = [(x3, 0, D_MODEL), (pg, 0, D_MODEL), (pp, 0, D_MODEL), (tgt, 0, D_MODEL)]
    dx4, dpg, dpp, loss_tile = _rows_call(head, head_rows, [], [(D_MODEL, F32), (D_MODEL, BF16), (D_MODEL, BF16)], [(8, 128)],
                                          tm=512, name="ple_loss")

    c_arr = jnp.reshape(ci, (1,)).astype(jnp.int32)
    mc_arr = jnp.stack([me, ci]).astype(jnp.int32)
    red = {g: _GroupReduce(g, grp, c_arr, mc_arr) for g, grp in REDUCE_GROUPS.items()}

    def adam_state(names):
        out = {}
        for n in names:
            if n == "lora":
                out[n] = tuple(_lora_stack({k: t[k][0] for k, _ in LORA}) for t in (wts, mom_m, mom_v))
            else:
                out[n] = (laid(wts, n), laid(mom_m, n), laid(mom_v, n))
        return out

    adam = {g: adam_state([w[0] for w in grp]) for g, grp in REDUCE_GROUPS.items()}
    done = {}
    gw, gs = {}, {}
    gw["ple_w_proj"] = _mm(p_tok, dpp, ta=True, name="ple_proj_dw")
    gw["ple_w_gate"] = _mm(hp, dpg, ta=True, name="ple_gate_dw")
    dhp = _mm(dpg, w_pg, tb=True, name="ple_gate_dx")
    dx3, dx3_bf, gs["ple_norm"] = _norm_bwd(x3, vec["ple_norm"], dhp, dx4, "ple_norm_bwd")
    dx2, dx2_bf, gs["ffn2_norm"], gw["ffn2_w_gate"], gw["ffn2_w_up"], gw["ffn2_w_down"] = _ffn_bwd(
        dx3, dx3_bf, ffn2_saved, vec["ffn2_norm"], wb["ffn2_w_gate"], wb["ffn2_w_up"], wb["ffn2_w_down"], "ffn2")
    gw["ple_w_gate"] = gw["ple_w_gate"].reshape(N_CHIPS, D_MODEL // N_CHIPS, D_MODEL)
    tok = red["f2"].swap_start([gw[w[0]] for w in REDUCE_GROUPS["f2"]])
    gw["w_out"] = _mm(merged, dx2_bf, ta=True, name="out_proj_dw")
    dmerged = _mm(dx2_bf, w_o, tb=True, dep=tok, name="out_proj_dx")

    def merge_bwd(zgr, zga, brt, bat, ct):
        _, vjp = jax.vjp(_merge, zgr, zga, brt, bat)
        d1, d2, d3, d4 = vjp(ct)
        return jnp.concatenate([d1, d2], axis=1), d3, d4

    dz_g, dbr, dba = _rows_call(merge_bwd, merge_rows + [(dmerged, 0, D_MODEL)], [],
                                [(2 * D_MODEL, BF16), (D_MODEL, BF16), (D_MODEL, BF16)], tm=512, name="merge_bwd")
    tok = red["f2"].swap_wait_ici_start(dz_g)
    gw["w_br_rwkv"] = _mm(y_rwkv, dbr, ta=True, name="branch_rwkv_dw")
    gw["w_br_attn"] = _mm(y_attn, dba, ta=True, name="branch_attn_dw")
    dy_rwkv = _mm(dbr, w_brr, tb=True, dep=tok, name="branch_rwkv_dx")
    dy_attn = _mm(dba, w_bra, tb=True, dep=tok, name="branch_attn_dx")

    def comb_bwd(*t):
        _, vjp = jax.vjp(_attn_combine, *t[:6])
        return vjp(t[6])

    dcomb = _rows_call(comb_bwd, comb_rows + [(dy_attn, 0, GROUP_DIM)], [], [(GROUP_DIM, F32)] * 6, tm=512, name="attn_combine_bwd")
    dqs, dks, dvs = zip(*[_attn_bwd(*qkv[g], d, dcomb[g], dcomb[3 + g]) for g, d in enumerate(ATTN_DILATIONS)])

    def qk_bwd(qt, kt, ct, st, *rest):
        dq = jnp.concatenate(rest[0:3], axis=1)
        dk = jnp.concatenate(rest[3:6], axis=1)
        qg, kg = rest[9], rest[10]
        _, vjp = jax.vjp(lambda a_, b_, c_, d_: qk_fwd(a_, b_, ct, st, c_, d_), qt, kt, qg, kg)
        dqt, dkt, dqg, dkg = vjp((dq, dk))
        return jnp.concatenate((dqt, dkt) + tuple(rest[6:9]), axis=1), dqg, dkg

    dz_a, gs["q_norm"], gs["k_norm"] = _rows_call(
        qk_bwd, qk_rows + [(t, 0, GROUP_DIM) for t in dqs + dks + dvs], [vec["q_norm"], vec["k_norm"]],
        [(ATTN_COLS, BF16)], [(1, HEAD), (1, HEAD)], tm=512, name="attn_pre_bwd")
    tok = red["f2"].ici_wait_join_start(dz_a, adam["f2"])

    def post_bwd(*t):
        _, vjp = jax.vjp(_rwkv_post, *t[:5], *t[6:])
        return vjp(t[5])

    dy_scan, dr_post, dk2_post, dv_post, dgate_r, gs["rwkv_gn_w"], gs["rwkv_gn_b"], gs["rwkv_r_k"] = _rows_call(
        post_bwd, post_rows + [(dy_rwkv, 0, RWKV_DIM)], post_params, [(RWKV_DIM, F32)] * 5, [(1, RWKV_DIM)] * 3,
        tm=512, name="rwkv_post_bwd", dep=tok)
    done.update(red["f2"].join_wait(dy_scan))
    dr_s, dlw, dk2_s, dv_s, dna, dkb = _wkv_bwd(zs, lw, k2, na, kb, s0s, invs, dy_scan)

    def pre_bwd(zt, c_r1, c_r2, c_lw, c_k1, c_k2, c_v1, c_v2, c_a, c_b, c_g, *params):
        _, vjp = jax.vjp(_rwkv_pre, zt, *params)
        return vjp((c_r1 + c_r2, c_lw, c_k1 + c_k2, c_v1 + c_v2, c_a, c_b, c_g))

    pre_cts = [dr_s, dr_post, dlw, dk2_s, dk2_post, dv_s, dv_post, dna, dkb, dgate_r]
    dzs, gs["rwkv_w0"], g_w2, gs["rwkv_a0"], g_a2, g_g2, gs["rwkv_k_k"], gs["rwkv_k_a"] = _rows_call(
        pre_bwd, [(zs, 0, RWKV_COLS)] + [(t, 0, RWKV_DIM) for t in pre_cts], pre_params, [(RWKV_COLS, F32)],
        [q.shape for q in pre_params], tm=512, name="rwkv_pre_bwd")
    dz_r, gs["rwkv_mu"] = _shift_bwd(z_r, vec["rwkv_mu"], dzs)

    g_w_in = jnp.concatenate([_mm(h, dz_r, ta=True, name="in_rwkv_dw"), _mm(h, dz_a, ta=True, name="in_attn_dw"),
                              _mm(h, dz_g, ta=True, name="in_gate_dw")], axis=1)
    gw["w_in"], gw["lora"] = g_w_in, jnp.concatenate([g_w2, g_a2, g_g2], axis=0)
    gw["w_out"] = gw["w_out"].reshape(N_CHIPS, D_MODEL // N_CHIPS, D_MODEL)
    tok = red["mx"].swap_start([gw[w[0]] for w in REDUCE_GROUPS["mx"]])
    dh = _mm(dz_r, w_in_r, tb=True, dep=tok, name="in_rwkv_dx")
    dh = _mm(dz_a, w_in_a, tb=True, res=dh, name="in_attn_dx")
    dh = _mm(dz_g, w_in_g, tb=True, res=dh, name="in_gate_dx")
    dx1, dx1_bf, gs["mix_norm"] = _norm_bwd(x1, vec["mix_norm"], dh, dx2, "mix_norm_bwd")
    tok_mx = red["mx"].swap_wait_ici_start(dx1_bf)
    hooks = {"down": lambda d_wd: red["f1d"].swap_start([d_wd], after=tok_mx),
             "mid": lambda dgate: red["f1d"].swap_wait_ici_start(dgate),
             "dw": lambda d_wgt, d_wut: red["f1g"].swap_start([d_wgt, d_wut]),
             "dx": lambda part: red["f1g"].swap_wait_ici_start(part) + red["f1d"].ici_wait_join_start(part, adam["f1d"]),
             "end": lambda dh_: red["mx"].ici_wait_join_start(dh_, adam["mx"])}
    dx0, _, gs["ffn1_norm"], gw["ffn1_w_gate"], gw["ffn1_w_up"], gw["ffn1_w_down"] = _ffn_bwd(
        dx1, dx1_bf, ffn1_saved, vec["ffn1_norm"], wb["ffn1_w_gate"], wb["ffn1_w_up"], wb["ffn1_w_down"], "ffn1", hooks=hooks)

    flat = jnp.concatenate([gs[n].reshape(-1) for n, _ in SMALL] + [loss_tile[0, 0:1]])
    small_buf = jnp.pad(flat, (0, SMALL_ROWS * PACK_COLS - flat.shape[0])).reshape(SMALL_ROWS, PACK_COLS)
    small_sum = _all_reduce_small(small_buf)
    n_small = sum(sz for _, sz in SMALL)
    loss = small_sum.reshape(-1)[n_small]
    grad_small = _unpack_small(small_sum, {n: wts[n].shape for n, _ in SMALL})
    d_s, m_s, v_s = _adamw(_pack_small(wts), small_sum, _pack_small(mom_m), _pack_small(mom_v), name="adamw_small", dep=dx0)
    shapes = {n: wts[n].shape for n, _ in SMALL}
    d_s, m_s, v_s = _unpack_small(d_s, shapes), _unpack_small(m_s, shapes), _unpack_small(v_s, shapes)
    grads, deltas, new_m, new_v = {}, {}, {}, {}
    for n, _ in SMALL:
        grads[n], deltas[n], new_m[n], new_v[n] = grad_small[n], d_s[n], m_s[n], v_s[n]

    tok = red["f1g"].ici_wait_join_start(m_s["ffn1_norm"], adam["f1g"])
    for g in ("mx", "f1d", "f1g"):
        done.update(red[g].join_wait(tok))
    for n, res in done.items():
        for store, val in zip((grads, deltas, new_m, new_v), res):
            if n == "lora":
                store.update({k: t[None] for k, t in _lora_split(val).items()})
            else:
                store[n] = (jnp.transpose(val) if n in TRANSPOSED else val)[None]

    return (loss, dx0[None], *[grads[n] for n in WEIGHTS], *[deltas[n] for n in WEIGHTS],
            *[new_m[n] for n in WEIGHTS], *[new_v[n] for n in WEIGHTS])
```

```python
import functools

import jax
import jax.numpy as jnp
from jax import lax
from jax.experimental import pallas as pl
from jax.experimental.pallas import tpu as pltpu

F32, BF16 = jnp.float32, jnp.bfloat16
HI = lax.Precision.HIGHEST
MESH = pl.DeviceIdType.MESH
SDS = jax.ShapeDtypeStruct

D_MODEL = 1024
HEAD = 64
RWKV_HEADS = 8
RWKV_DIM = RWKV_HEADS * HEAD
DECAY_LORA, ICLR_LORA, GATE_LORA = 64, 64, 128
GN_EPS = 64e-5
RMS_EPS = 1e-6
ATTN_DILATIONS = (1, 4, 16)
BAND = 128
ATTN_DIM = 768
GROUP_DIM = 256
ROPE_THETA = 10000.0
NEG_INF = -1e30
RWKV_COLS = 3 * RWKV_DIM + DECAY_LORA + ICLR_LORA + GATE_LORA
ATTN_COLS = 3 * ATTN_DIM
ADAM_LR, ADAM_B1, ADAM_B2, ADAM_EPS, ADAM_WD, ADAM_STEP = 0.001, 0.9, 0.999, 1e-08, 0.01, 10

WKV_CHUNK = 64
WKV_HEADS_PER_STEP = 8
N_CHIPS = 4
PACK_COLS = 1024
VMEM_LIMIT = 48 * 1024 * 1024

TRANSPOSED = ("ffn1_w_gate", "ffn1_w_up", "ffn2_w_gate", "ffn2_w_up")
LORA = (("rwkv_w2", 64), ("rwkv_a2", 64), ("rwkv_g2", 128))
_FFN1 = (("ffn1_w_gate", "blk", 704, 1024), ("ffn1_w_up", "blk", 704, 1024), ("ffn1_w_down", "blk", 704, 1024))
_FFN2 = (("ffn2_w_gate", "blk", 704, 1024), ("ffn2_w_up", "blk", 704, 1024), ("ffn2_w_down", "blk", 704, 1024))
_IN = (("w_in", "col", 1024, 1536), ("lora", "col", 256, 128))
_BRANCH = (("w_br_rwkv", "col", 512, 256), ("w_br_attn", "col", 256, 256), ("w_out", "blk", 256, 1024))
_PLE = (("ple_w_gate", "blk", 256, 1024), ("ple_w_proj", "col", 256, 256))
GROUPS = {"f1": _FFN1, "mx": _IN, "f2": _BRANCH + _FFN2 + _PLE}
REDUCE_GROUPS = {"f2": _FFN2 + _PLE, "mx": _IN + _BRANCH, "f1d": _FFN1[2:], "f1g": _FFN1[:2]}
SMALL = (
    ("ffn1_norm", 1024), ("mix_norm", 1024), ("ffn2_norm", 1024), ("ple_norm", 1024), ("rwkv_mu", 1792),
    ("rwkv_w0", 512), ("rwkv_a0", 512), ("rwkv_k_k", 512), ("rwkv_k_a", 512), ("rwkv_r_k", 512),
    ("rwkv_gn_w", 512), ("rwkv_gn_b", 512), ("q_norm", 64), ("k_norm", 64),
)
SMALL_ROWS = 16
WEIGHTS = (
    "ffn1_norm", "ffn1_w_gate", "ffn1_w_up", "ffn1_w_down", "mix_norm", "w_in", "rwkv_mu", "rwkv_w0", "rwkv_w2",
    "rwkv_a0", "rwkv_a2", "rwkv_g2", "rwkv_k_k", "rwkv_k_a", "rwkv_r_k", "rwkv_gn_w", "rwkv_gn_b", "q_norm", "k_norm",
    "w_br_rwkv", "w_br_attn", "w_out", "ffn2_norm", "ffn2_w_gate", "ffn2_w_up", "ffn2_w_down", "ple_norm",
    "ple_w_gate", "ple_w_proj",
)


def _row_tile(n, most=704):
    for t in range(most - most % 16, 0, -16):
        if n % t == 0:
            return t
    return n


def _pick(n, cands):
    for c in cands:
        if n % c == 0:
            return c
    return n


def _mm(a, b, *, ta=False, tb=False, sum_blocks=False, out_dtype=F32, res=None, alpha=1.0, dep=None, post=None, name):
    flat = a.ndim == 2 and b.ndim == 2
    a3 = a if a.ndim == 3 else a[None]
    b3 = b if b.ndim == 3 else b[None]
    na, nbb = a3.shape[0], b3.shape[0]
    nblk = max(na, nbb)
    kdim, m = (a3.shape[1], a3.shape[2]) if ta else (a3.shape[2], a3.shape[1])
    n = b3.shape[1] if tb else b3.shape[2]
    assert (b3.shape[2] if tb else b3.shape[1]) == kdim
    tm = _pick(m, (1024, 512, 256, 128) if post is None else (512, 256, 128))
    tn = _pick(n, (1024, 896, 768, 512, 256, 128))
    tk = kdim if kdim <= 2304 else _pick(kdim, (1024, 512, 256, 128))
    nk = kdim // tk
    direct = nk == 1 and not sum_blocks

    if sum_blocks:
        grid = (m // tm, n // tn, nblk, nk)

        def ids(i, c, j, k):
            return i, c, j, k
    else:
        grid = (nblk, m // tm, n // tn, nk)

        def ids(j, i, c, k):
            return i, c, j, k

    def amap(*g):
        i, c, j, k = ids(*g)
        jj = j if na > 1 else 0
        return (jj, k, i) if ta else (jj, i, k)

    def bmap(*g):
        i, c, j, k = ids(*g)
        jj = j if nbb > 1 else 0
        return (jj, c, k) if tb else (jj, k, c)

    if sum_blocks:
        oshape, oblk = (m, n), (tm, tn)

        def omap(*g):
            i, c, j, k = ids(*g)
            return i, c
    else:
        oshape, oblk = (nblk, m, n), (1, tm, tn)

        def omap(*g):
            i, c, j, k = ids(*g)
            return j, i, c

    dn = (((0 if ta else 1,), (1 if tb else 0,)), ((), ()))
    has_res = res is not None
    p_f, p_rows, p_params, p_dtypes, p_accs = post if post is not None else (None, [], [], [], [])
    assert post is None or sum_blocks or flat
    n_in = 2 + has_res + len(p_rows) + len(p_params) + (dep is not None)

    def tile_map(*g):
        i, c, j, k = ids(*g)
        return i, c

    def body(*refs):
        refs = list(refs)
        acc = None if direct else refs.pop()
        o_refs = refs[n_in:]
        a_ref, b_ref = refs[0], refs[1]
        r_ref = refs[2] if has_res else None
        pr_refs = refs[2 + has_res:2 + has_res + len(p_rows)]
        pp_refs = refs[2 + has_res + len(p_rows):2 + has_res + len(p_rows) + len(p_params)]
        first_tile = jnp.logical_and(pl.program_id(0 if sum_blocks else 1) == 0, pl.program_id(1 if sum_blocks else 2) == 0)

        def finish(v):
            if alpha != 1.0:
                v = v * alpha
            if has_res:
                v = v + r_ref[...].reshape(v.shape).astype(F32)
            if post is None:
                o_refs[0][...] = v.reshape(o_refs[0].shape).astype(o_refs[0].dtype)
                return
            outs = p_f(v, *[t[...] for t in pr_refs], *[t[...] for t in pp_refs])
            for o_ref, val in zip(o_refs, outs[:len(p_dtypes)]):
                o_ref[...] = val.astype(o_ref.dtype)
            for o_ref, val in zip(o_refs[len(p_dtypes):], outs[len(p_dtypes):]):
                @pl.when(first_tile)
                def _():
                    o_ref[...] = jnp.zeros_like(o_ref)

                o_ref[...] += val.reshape(o_ref.shape)

        if direct:
            finish(lax.dot_general(a_ref[0].astype(BF16), b_ref[0].astype(BF16), dn, preferred_element_type=F32))
            return
        k = pl.program_id(3)
        if sum_blocks:
            j = pl.program_id(2)
            first = jnp.logical_and(j == 0, k == 0)
            last = jnp.logical_and(j == nblk - 1, k == nk - 1)
        else:
            first, last = k == 0, k == nk - 1

        @pl.when(first)
        def _():
            acc[...] = jnp.zeros_like(acc)

        acc[...] += lax.dot_general(a_ref[0].astype(BF16), b_ref[0].astype(BF16), dn, preferred_element_type=F32)

        @pl.when(last)
        def _():
            finish(acc[...])

    in_specs = [pl.BlockSpec((1, tk, tm) if ta else (1, tm, tk), amap), pl.BlockSpec((1, tn, tk) if tb else (1, tk, tn), bmap)]
    args = [a3, b3]
    if has_res:
        res3 = res if (sum_blocks or res.ndim == 3) else res[None]
        in_specs.append(pl.BlockSpec(oblk, omap))
        args.append(res3)
    in_specs += [pl.BlockSpec((tm, tn), tile_map) for _ in p_rows]
    in_specs += [pl.BlockSpec(t.shape, functools.partial(lambda *g, nd: (0,) * nd, nd=t.ndim)) for t in p_params]
    args += list(p_rows) + list(p_params)
    if dep is not None:
        in_specs.append(pl.BlockSpec(memory_space=pl.ANY))
        args.append(dep)
    if post is None:
        out_specs, out_shape = pl.BlockSpec(oblk, omap), SDS(oshape, out_dtype)
        semantics = ("parallel", "parallel", "arbitrary", "arbitrary") if sum_blocks else ("parallel", "parallel", "parallel", "arbitrary")
    else:
        out_specs = [pl.BlockSpec((tm, tn), tile_map) for _ in p_dtypes]
        out_specs += [pl.BlockSpec(tuple(sh), functools.partial(lambda *g, nd: (0,) * nd, nd=len(sh))) for sh in p_accs]
        out_shape = [SDS((m, n), dt) for dt in p_dtypes] + [SDS(tuple(sh), F32) for sh in p_accs]
        semantics = ("arbitrary",) * 4
    out = pl.pallas_call(
        body,
        name=name,
        grid=grid,
        in_specs=in_specs,
        out_specs=out_specs,
        out_shape=out_shape,
        scratch_shapes=[] if direct else [pltpu.VMEM((tm, tn), F32)],
        compiler_params=pltpu.CompilerParams(dimension_semantics=semantics, vmem_limit_bytes=VMEM_LIMIT),
    )(*args)
    if post is not None:
        return out
    if flat and not sum_blocks:
        out = out[0]
    return out


def _rows_call(f, rows, params, outs, accs=(), *, tm, name, dep=None):
    s = rows[0][0].shape[0]
    nr, npar, no = len(rows), len(params), len(outs)
    nin = nr + npar + (0 if dep is None else 1)
    in_specs = [pl.BlockSpec((tm, w), functools.partial(lambda i, cb: (i, cb), cb=cb)) for (_, cb, w) in rows]
    in_specs += [pl.BlockSpec(p.shape, functools.partial(lambda i, nd: (0,) * nd, nd=p.ndim)) for p in params]
    if dep is not None:
        in_specs.append(pl.BlockSpec(memory_space=pl.ANY))
    out_shape = [SDS((s, w), dt) for (w, dt) in outs] + [SDS(tuple(sh), F32) for sh in accs]
    out_specs = [pl.BlockSpec((tm, w), lambda i: (i, 0)) for (w, _) in outs]
    out_specs += [pl.BlockSpec(tuple(sh), functools.partial(lambda i, nd: (0,) * nd, nd=len(sh))) for sh in accs]

    def body(*refs):
        rin, pin = refs[:nr], refs[nr:nr + npar]
        oo, ao = refs[nin:nin + no], refs[nin + no:]
        res = f(*[r[...] for r in rin], *[p[...] for p in pin])
        if not isinstance(res, (tuple, list)):
            res = (res,)
        for o_ref, v in zip(oo, res[:no]):
            o_ref[...] = v.astype(o_ref.dtype)
        i = pl.program_id(0)
        for a_ref, v in zip(ao, res[no:]):
            @pl.when(i == 0)
            def _():
                a_ref[...] = jnp.zeros_like(a_ref)

            a_ref[...] += v.reshape(a_ref.shape)

    res = pl.pallas_call(
        body,
        name=name,
        grid=(s // tm,),
        in_specs=in_specs,
        out_specs=out_specs,
        out_shape=out_shape,
        compiler_params=pltpu.CompilerParams(dimension_semantics=("arbitrary",), vmem_limit_bytes=VMEM_LIMIT),
    )(*[r[0] for r in rows], *params, *([] if dep is None else [dep]))
    return res


def _mmv(a, b, mode):
    ca = 0 if mode[0] == "t" else 1
    cb = 1 if mode[1] == "t" else 0
    return lax.dot_general(a.astype(BF16), b.astype(BF16), (((ca,), (cb,)), ((), ())), preferred_element_type=F32)


@functools.partial(jax.custom_vjp, nondiff_argnums=(2,))
def _bdot(a, b, mode):
    return _mmv(a, b, mode)


def _bdot_fwd(a, b, mode):
    return _mmv(a, b, mode), (a, b)


def _bdot_bwd(mode, saved, g):
    a, b = saved
    if mode == "nn":
        return _mmv(g, b, "nt"), _mmv(a, g, "tn")
    if mode == "nt":
        return _mmv(g, b, "nn"), _mmv(g, a, "tn")
    return _mmv(b, g, "nt"), _mmv(a, g, "nn")


_bdot.defvjp(_bdot_fwd, _bdot_bwd)


def _hdot(a, b, mode="nn", precision=HI):
    ca = 0 if mode[0] == "t" else 1
    cb = 1 if mode[1] == "t" else 0
    return lax.dot_general(a, b, (((ca,), (cb,)), ((), ())), precision=precision, preferred_element_type=F32)


def _segsum(x):
    c = x.shape[-1]
    blk = min(c, 256)
    r = lax.broadcasted_iota(jnp.int32, (blk, blk), 0) >> 6
    q = lax.broadcasted_iota(jnp.int32, (blk, blk), 1) >> 6
    ones = jnp.where(r == q, 1.0, 0.0).astype(F32)
    parts = [_hdot(x[:, i:i + blk], ones, precision=lax.Precision.HIGH) for i in range(0, c, blk)]
    return parts[0] if len(parts) == 1 else jnp.concatenate(parts, axis=1)


def _sigmoid(x):
    return jax.nn.sigmoid(x)


def _softplus(x):
    return jnp.maximum(x, 0.0) + jnp.log(1.0 + jnp.exp(-jnp.abs(x)))


def _rms(x, gain):
    return x * lax.rsqrt(jnp.mean(x * x, axis=-1, keepdims=True) + RMS_EPS) * gain


def _swiglu_act(gate, up):
    return gate * _sigmoid(gate) * up


def _rwkv_pre(zs, w0, w2, a0, a2, g2, k_k, k_a):
    r, k, v = zs[:, 0:512], zs[:, 512:1024], zs[:, 1024:1536]
    lora = zs[:, 1536:1792]
    wd, ad, gd = lora[:, 0:64], lora[:, 64:128], lora[:, 128:256]
    w = -_softplus(-(w0 + _bdot(jnp.tanh(wd), w2, "nn"))) - 0.5
    a = _sigmoid(a0 + _bdot(ad, a2, "nn"))
    g = _bdot(_sigmoid(gd), g2, "nn")
    kk = k * k_k
    kk = kk * lax.rsqrt(jnp.maximum(_segsum(kk * kk), 1e-24))
    k2 = k * (1.0 + (a - 1.0) * k_a)
    return r, -jnp.exp(w), k2, v, -kk, kk * a, g


def _rwkv_post(y, r, k2, v, g, gn_w, gn_b, r_k):
    mean = _segsum(y) * (1.0 / HEAD)
    yc = y - mean
    var = _segsum(yc * yc) * (1.0 / HEAD)
    yn = yc * lax.rsqrt(var + GN_EPS) * gn_w + gn_b
    bonus = _segsum(r * k2 * r_k) * v
    return (yn + bonus) * g


def _swap_halves(x):
    lane = lax.broadcasted_iota(jnp.int32, x.shape, 1)
    return jnp.where((lane & 32) == 0, jnp.roll(x, -32, axis=1), jnp.roll(x, 32, axis=1))


def _norm_rope(x, gain, cos, sin):
    heads = x.shape[1] // HEAD
    def rep(t):
        return jnp.concatenate([t] * heads, axis=1)

    xn = x * lax.rsqrt(_segsum(x * x) * (1.0 / HEAD) + RMS_EPS) * rep(gain)
    return xn * rep(cos) + _swap_halves(xn) * rep(sin)


def _attn_combine(o0, o1, o2, l0, l1, l2):
    m = jnp.maximum(jnp.maximum(l0, l1), l2)
    e0, e1, e2 = jnp.exp(l0 - m), jnp.exp(l1 - m), jnp.exp(l2 - m)
    return (e0 * o0 + e1 * o1 + e2 * o2) / (e0 + e1 + e2)


def _merge(zgr, zga, br, ba):
    return _sigmoid(zgr) * br + _sigmoid(zga) * ba


def _attn_block(q, kp, kc, vp, vc, has_prev):
    iq = lax.broadcasted_iota(jnp.int32, (1, BAND, BAND), 1)
    ik = lax.broadcasted_iota(jnp.int32, (1, BAND, BAND), 2)
    s_c = jnp.where(iq >= ik, _bdotb(q, kc, "nt") * (HEAD ** -0.5), NEG_INF)
    s_p = jnp.where(jnp.logical_and(iq <= ik, has_prev), _bdotb(q, kp, "nt") * (HEAD ** -0.5), NEG_INF)
    m = lax.stop_gradient(jnp.maximum(jnp.max(s_c, axis=-1, keepdims=True), jnp.max(s_p, axis=-1, keepdims=True)))
    e_c, e_p = jnp.exp(s_c - m), jnp.exp(s_p - m)
    l = jnp.sum(e_c, axis=-1, keepdims=True) + jnp.sum(e_p, axis=-1, keepdims=True)
    o = (_bdotb(e_c, vc) + _bdotb(e_p, vp)) / l
    return o, jnp.broadcast_to(m + jnp.log(l), o.shape)


def _mmb(a, b, cb):
    return lax.dot_general(a.astype(BF16), b.astype(BF16), (((2,), (cb,)), ((0,), (0,))), preferred_element_type=F32)


@functools.partial(jax.custom_vjp, nondiff_argnums=(2,))
def _bdotb1(a, b, cb):
    return _mmb(a, b, cb)


def _bdotb1_fwd(a, b, cb):
    return _mmb(a, b, cb), (a, b)


def _bdotb1_bwd(cb, saved, g):
    a, b = saved
    if cb == 1:
        return _mmb(g, b, 2), _mmb(jnp.swapaxes(a, 1, 2), g, 1)
    return _mmb(g, b, 1), _mmb(jnp.swapaxes(g, 1, 2), a, 1)


_bdotb1.defvjp(_bdotb1_fwd, _bdotb1_bwd)


def _bdotb(a, b, mode="nn", precision=None):
    if mode[0] == "t":
        a = jnp.swapaxes(a, 1, 2)
    cb = 2 if mode[1] == "t" else 1
    if precision is None:
        return _bdotb1(a, b, cb)
    return lax.dot_general(a, b, (((2,), (cb,)), ((0,), (0,))), precision=precision, preferred_element_type=F32)


def _tri_inv_levels(a):
    t = a.shape[-1]
    row = lax.broadcasted_iota(jnp.int32, (1, t, t), 1)
    col = lax.broadcasted_iota(jnp.int32, (1, t, t), 2)
    x = jnp.where(row == col, 1.0, 0.0).astype(F32) + jnp.where(jnp.logical_and(row == col + 1, (row & 1) == 1), a, 0.0)
    sh = 1
    while (1 << sh) < t:
        m = jnp.logical_and((row >> sh) == (col >> sh) + 1, (row >> (sh + 1)) == (col >> (sh + 1)))
        x = x + _bdotb(_bdotb(x, jnp.where(m, a, 0.0), precision=lax.Precision.HIGH), x, precision=lax.Precision.HIGH)
        sh += 1
    return x


@jax.custom_vjp
def _tri_inv(a):
    return _tri_inv_levels(a)


def _tri_inv_fwd(a):
    x = _tri_inv_levels(a)
    return x, x


def _tri_inv_bwd(x, g):
    xt = jnp.swapaxes(x, 1, 2)
    return (_bdotb(_bdotb(xt, g, precision=lax.Precision.HIGH), xt, precision=lax.Precision.HIGH),)


_tri_inv.defvjp(_tri_inv_fwd, _tri_inv_bwd)


@jax.custom_vjp
def _known_inv(a, x):
    return x


def _known_inv_fwd(a, x):
    return x, x


def _known_inv_bwd(x, g):
    return _tri_inv_bwd(x, g)[0], jnp.zeros_like(x)


_known_inv.defvjp(_known_inv_fwd, _known_inv_bwd)


def _wkv_chunk(s0, r, lw, k, v, a, b, inv=None, with_inv=False):
    nh, t, _ = r.shape
    row = lax.broadcasted_iota(jnp.int32, (1, t, t), 1)
    col = lax.broadcasted_iota(jnp.int32, (1, t, t), 2)
    incl, strict = row >= col, row > col
    ones = jnp.broadcast_to(jnp.where(incl, 1.0, 0.0).astype(F32), (nh, t, t))
    cum = _bdotb(ones, lw, precision=HI)
    c_end = cum[:, t - 1:t, :]
    e_in, e_ex, e_inv = jnp.exp(cum), jnp.exp(cum - lw), jnp.exp(-cum)
    at, rt, bt, kt = a * e_ex, r * e_in, b * e_inv, k * e_inv
    a_ab = jnp.where(strict, _bdotb(at, bt, "nt"), 0.0)
    a_ak = jnp.where(strict, _bdotb(at, kt, "nt"), 0.0)
    x = _tri_inv(a_ab) if inv is None else _known_inv(a_ab, inv)
    u = _bdotb(x, _bdotb(at, s0, "nt") + _bdotb(a_ak, v))
    y = (_bdotb(rt, s0, "nt") + _bdotb(jnp.where(incl, _bdotb(rt, bt, "nt"), 0.0), u)
         + _bdotb(jnp.where(incl, _bdotb(rt, kt, "nt"), 0.0), v))
    w_end = jnp.exp(c_end - cum)
    s1 = s0 * jnp.exp(c_end) + _bdotb(u, b * w_end, "tn") + _bdotb(v, k * w_end, "tn")
    return (y, s1, x) if with_inv else (y, s1)


def _shift_fwd(z, mu):
    s, c = z.shape
    tc = 256

    def body(z_ref, mu_ref, o_ref):
        zz = z_ref[...]
        row = lax.broadcasted_iota(jnp.int32, zz.shape, 0)
        prev = jnp.where(row == 0, 0.0, pltpu.roll(zz, 1, 0))
        o_ref[...] = zz + (prev - zz) * mu_ref[...]

    return pl.pallas_call(
        body, name="shift_fwd", grid=(c // tc,),
        in_specs=[pl.BlockSpec((s, tc), lambda j: (0, j)), pl.BlockSpec((1, tc), lambda j: (0, j))],
        out_specs=pl.BlockSpec((s, tc), lambda j: (0, j)), out_shape=SDS((s, c), F32),
        compiler_params=pltpu.CompilerParams(dimension_semantics=("parallel",), vmem_limit_bytes=VMEM_LIMIT),
    )(z, mu)


def _shift_bwd(z, mu, dzs):
    s, c = z.shape
    tc = 256

    def body(z_ref, mu_ref, d_ref, dz_ref, dmu_ref):
        zz, d, m = z_ref[...], d_ref[...], mu_ref[...]
        row = lax.broadcasted_iota(jnp.int32, zz.shape, 0)
        prev = jnp.where(row == 0, 0.0, pltpu.roll(zz, 1, 0))
        t = d * m
        nxt = jnp.where(row == s - 1, 0.0, pltpu.roll(t, s - 1, 0))
        dz_ref[...] = (d - t + nxt).astype(dz_ref.dtype)
        dmu_ref[...] = jnp.sum(d * (prev - zz), axis=0, keepdims=True)

    return pl.pallas_call(
        body, name="shift_bwd", grid=(c // tc,),
        in_specs=[pl.BlockSpec((s, tc), lambda j: (0, j)), pl.BlockSpec((1, tc), lambda j: (0, j)),
                  pl.BlockSpec((s, tc), lambda j: (0, j))],
        out_specs=[pl.BlockSpec((s, tc), lambda j: (0, j)), pl.BlockSpec((1, tc), lambda j: (0, j))],
        out_shape=[SDS((s, c), BF16), SDS((1, c), F32)],
        compiler_params=pltpu.CompilerParams(dimension_semantics=("parallel",), vmem_limit_bytes=VMEM_LIMIT),
    )(z, mu, dzs)


def _heads(x, nh):
    return jnp.stack([x[:, h * HEAD:(h + 1) * HEAD] for h in range(nh)], axis=0)


def _unheads(x):
    return jnp.concatenate([x[h] for h in range(x.shape[0])], axis=1)


def _wkv_fwd(zs, lw, k2, na, b):
    s = lw.shape[0]
    t, hb = WKV_CHUNK, WKV_HEADS_PER_STEP
    w = hb * HEAD
    nc, ng = s // t, RWKV_HEADS // hb

    def body(r_ref, v_ref, lw_ref, k_ref, a_ref, b_ref, y_ref, s0_ref, x_ref, state):
        @pl.when(pl.program_id(1) == 0)
        def _():
            state[...] = jnp.zeros_like(state)

        s0 = state[...]
        s0_ref[0] = s0
        y, s1, x = _wkv_chunk(s0, *[_heads(t_ref[...], hb) for t_ref in (r_ref, lw_ref, k_ref, v_ref, a_ref, b_ref)], with_inv=True)
        y_ref[...] = _unheads(y)
        x_ref[0] = x
        state[...] = s1

    def col(off):
        return pl.BlockSpec((t, w), functools.partial(lambda g, i, off: (i, g + off), off=off))

    return pl.pallas_call(
        body, name="wkv_fwd", grid=(ng, nc),
        in_specs=[col(0), col(2 * ng), col(0), col(0), col(0), col(0)],
        out_specs=[col(0), pl.BlockSpec((1, hb, HEAD, HEAD), lambda g, i: (i, g, 0, 0)),
                   pl.BlockSpec((1, hb, t, t), lambda g, i: (i, g, 0, 0))],
        out_shape=[SDS((s, RWKV_DIM), F32), SDS((nc, RWKV_HEADS, HEAD, HEAD), F32), SDS((nc, RWKV_HEADS, t, t), F32)],
        scratch_shapes=[pltpu.VMEM((hb, HEAD, HEAD), F32)],
        compiler_params=pltpu.CompilerParams(dimension_semantics=("parallel", "arbitrary"), vmem_limit_bytes=VMEM_LIMIT),
    )(zs, zs, lw, k2, na, b)


def _wkv_bwd(zs, lw, k2, na, b, s0s, invs, dy):
    s = lw.shape[0]
    t, hb = WKV_CHUNK, WKV_HEADS_PER_STEP
    w = hb * HEAD
    nc, ng = s // t, RWKV_HEADS // hb

    def body(r_ref, v_ref, lw_ref, k_ref, a_ref, b_ref, s0_ref, x_ref, dy_ref, dr_ref, dlw_ref, dk_ref, dv_ref, da_ref, db_ref, dstate):
        @pl.when(pl.program_id(1) == 0)
        def _():
            dstate[...] = jnp.zeros_like(dstate)

        _, vjp = jax.vjp(functools.partial(_wkv_chunk, inv=x_ref[0]), s0_ref[0],
                         *[_heads(t_ref[...], hb) for t_ref in (r_ref, lw_ref, k_ref, v_ref, a_ref, b_ref)])
        grads = vjp((_heads(dy_ref[...], hb), dstate[...]))
        dstate[...] = grads[0]
        for o_ref, gval in zip((dr_ref, dlw_ref, dk_ref, dv_ref, da_ref, db_ref), grads[1:]):
            o_ref[...] = _unheads(gval)

    def col(off):
        return pl.BlockSpec((t, w), functools.partial(lambda g, i, off: (nc - 1 - i, g + off), off=off))

    return pl.pallas_call(
        body, name="wkv_bwd", grid=(ng, nc),
        in_specs=[col(0), col(2 * ng), col(0), col(0), col(0), col(0),
                  pl.BlockSpec((1, hb, HEAD, HEAD), lambda g, i: (nc - 1 - i, g, 0, 0)),
                  pl.BlockSpec((1, hb, t, t), lambda g, i: (nc - 1 - i, g, 0, 0)), col(0)],
        out_specs=[col(0)] * 6,
        out_shape=[SDS((s, RWKV_DIM), F32)] * 6,
        scratch_shapes=[pltpu.VMEM((hb, HEAD, HEAD), F32)],
        compiler_params=pltpu.CompilerParams(dimension_semantics=("parallel", "arbitrary"), vmem_limit_bytes=VMEM_LIMIT),
    )(zs, zs, lw, k2, na, b, s0s, invs, dy)


def _attn_fwd(q, k, v, d):
    s = q.shape[0]
    l = s // d
    nb = l // BAND
    assert nb * BAND == l
    qv, kv, vv = (t.reshape(l, d * GROUP_DIM) for t in (q, k, v))
    nh = GROUP_DIM // HEAD

    def body(q_ref, kp_ref, kc_ref, vp_ref, vc_ref, o_ref, l_ref):
        has_prev = pl.program_id(1) > 0
        o, lse = _attn_block(*[_heads(t_ref[...].astype(F32), nh) for t_ref in (q_ref, kp_ref, kc_ref, vp_ref, vc_ref)], has_prev)
        o_ref[...] = _unheads(o)
        l_ref[...] = _unheads(lse)

    cur = pl.BlockSpec((BAND, GROUP_DIM), lambda rho, i: (i, rho))
    prev = pl.BlockSpec((BAND, GROUP_DIM), lambda rho, i: (jnp.maximum(i - 1, 0), rho))
    o, lse = pl.pallas_call(
        body, name=f"attn_fwd_d{d}", grid=(d, nb),
        in_specs=[cur, prev, cur, prev, cur], out_specs=[cur, cur],
        out_shape=[SDS((l, d * GROUP_DIM), F32), SDS((l, d * GROUP_DIM), F32)],
        compiler_params=pltpu.CompilerParams(dimension_semantics=("parallel", "arbitrary"), vmem_limit_bytes=VMEM_LIMIT),
    )(qv, kv, kv, vv, vv)
    return o.reshape(s, GROUP_DIM), lse.reshape(s, GROUP_DIM)


def _attn_bwd(q, k, v, d, do, dlse):
    s = q.shape[0]
    l = s // d
    nb = l // BAND
    qv, kv, vv, dov, dlv = (t.reshape(l, d * GROUP_DIM) for t in (q, k, v, do, dlse))
    nh = GROUP_DIM // HEAD

    def body(q_ref, kp_ref, kc_ref, vp_ref, vc_ref, do_ref, dl_ref, dq_ref, dk_ref, dv_ref, ck, cv):
        step = pl.program_id(1)
        has_prev = step < nb - 1

        @pl.when(step == 0)
        def _():
            ck[...] = jnp.zeros_like(ck)
            cv[...] = jnp.zeros_like(cv)

        _, vjp = jax.vjp(functools.partial(_attn_block, has_prev=has_prev),
                         *[_heads(t_ref[...].astype(F32), nh) for t_ref in (q_ref, kp_ref, kc_ref, vp_ref, vc_ref)])
        dq, dkp, dkc, dvp, dvc = vjp((_heads(do_ref[...], nh), _heads(dl_ref[...], nh)))
        dq_ref[...] = _unheads(dq)
        dk_ref[...] = _unheads(dkc) + ck[...]
        dv_ref[...] = _unheads(dvc) + cv[...]
        ck[...] = _unheads(dkp)
        cv[...] = _unheads(dvp)

    cur = pl.BlockSpec((BAND, GROUP_DIM), lambda rho, i: (nb - 1 - i, rho))
    prev = pl.BlockSpec((BAND, GROUP_DIM), lambda rho, i: (jnp.maximum(nb - 2 - i, 0), rho))
    dq, dk, dv = pl.pallas_call(
        body, name=f"attn_bwd_d{d}", grid=(d, nb),
        in_specs=[cur, prev, cur, prev, cur, cur, cur], out_specs=[cur] * 3,
        out_shape=[SDS((l, d * GROUP_DIM), F32)] * 3,
        scratch_shapes=[pltpu.VMEM((BAND, GROUP_DIM), F32), pltpu.VMEM((BAND, GROUP_DIM), F32)],
        compiler_params=pltpu.CompilerParams(dimension_semantics=("parallel", "arbitrary"), vmem_limit_bytes=VMEM_LIMIT),
    )(qv, kv, kv, vv, vv, dov, dlv)
    return dq.reshape(s, GROUP_DIM), dk.reshape(s, GROUP_DIM), dv.reshape(s, GROUP_DIM)


def _coords():
    return lax.axis_index("x"), lax.axis_index("y"), lax.axis_index("c")


_CHIP_FLIPS = ((1, 0), (0, 1), (1, 1))


def _flip(v, f):
    return 1 - v if f else v


def _form(kind, r, c):
    return (N_CHIPS, r, c) if kind == "blk" else (r, N_CHIPS * c)


def _slot(ref, kind, j, rows, c):
    if kind == "blk":
        return ref.at[j] if rows is None else ref.at[j, rows]
    cols = pl.ds(pl.multiple_of(j * c, 128), c)
    return ref.at[:, cols] if rows is None else ref.at[rows, cols]


def _half(r, which, align):
    return pl.ds(pl.multiple_of(which * (r // 2), align), r // 2)


def _rcopy(src, dst, send_sems, recv_sems, kk, dev):
    return pltpu.make_async_remote_copy(src_ref=src, dst_ref=dst, send_sem=send_sems.at[kk], recv_sem=recv_sems.at[kk],
                                        device_id=dev, device_id_type=MESH)


def _gather_plan(specs, step):
    def copies(refs, ss, rs, received):
        x, y, c = _coords()
        out = []
        for w, (kind, r, cc) in enumerate(specs):
            mine, other = _half(r, c, 16), _half(r, 1 - c, 16)
            for kk, (fx, fy) in enumerate(_CHIP_FLIPS):
                px, py = _flip(x, fx), _flip(y, fy)
                if step == "ici":
                    sl = _slot(refs[w], kind, 2 * px + py if received else 2 * x + y, mine, cc)
                    dev = (px, py, c)
                else:
                    sl = _slot(refs[w], kind, 2 * px + py, other if received else mine, cc)
                    dev = (x, y, 1 - c)
                out.append(_rcopy(sl, sl, ss, rs, 3 * w + kk, dev))
        return out

    def issue(refs, ss, rs):
        return copies(refs, ss, rs, False)

    def expect(refs, ss, rs):
        return copies(refs, ss, rs, False), copies(refs, ss, rs, True)

    return issue, expect


_HBM = pl.BlockSpec(memory_space=pltpu.HBM)
_SEM = pl.BlockSpec(memory_space=pltpu.SEMAPHORE)
_EFFECT = pltpu.SideEffectType.DATAFLOW_SIDE_EFFECTING


def _copies_start(name, bufs, n_sems, issue, after=None):
    nb = len(bufs)
    extra = [] if after is None else [after]

    def body(*refs):
        send_sems, recv_sems = refs[nb + len(extra)], refs[nb + len(extra) + 1]
        for cp in issue(refs[:nb], send_sems, recv_sems):
            cp.start()
        refs[-1][...] = jnp.zeros_like(refs[-1])

    outs = pl.pallas_call(
        body, name=name,
        out_shape=(pltpu.SemaphoreType.DMA((n_sems,)), pltpu.SemaphoreType.DMA((n_sems,)),
                   *[pltpu.HBM(b.shape, b.dtype) for b in bufs], SDS((8, 128), F32)),
        in_specs=[_HBM] * nb + [pl.BlockSpec(memory_space=pl.ANY)] * len(extra),
        out_specs=(_SEM, _SEM, *[_HBM] * nb, pl.BlockSpec(memory_space=pltpu.VMEM)),
        input_output_aliases={i: 2 + i for i in range(nb)},
        compiler_params=pltpu.CompilerParams(has_side_effects=_EFFECT),
    )(*[pltpu.with_memory_space_constraint(b, pltpu.HBM) for b in bufs], *extra)
    return outs[0], outs[1], list(outs[2:2 + nb]), outs[-1]


def _copies_wait(name, bufs, send_sems, recv_sems, after, expect):
    nb = len(bufs)

    def body(*refs):
        sent, received = expect(refs[:nb], refs[nb], refs[nb + 1])
        for cp in sent:
            cp.wait_send()
        for cp in received:
            cp.wait_recv()

    outs = pl.pallas_call(
        body, name=name,
        out_shape=tuple(pltpu.HBM(b.shape, b.dtype) for b in bufs),
        in_specs=(*[_HBM] * nb, _SEM, _SEM, pl.BlockSpec(memory_space=pl.ANY)), out_specs=tuple([_HBM] * nb),
        input_output_aliases={i: i for i in range(nb)},
        compiler_params=pltpu.CompilerParams(has_side_effects=_EFFECT),
    )(*bufs, send_sems, recv_sems, after)
    return list(outs)


def _add_pair(g, recv, kind, r, c, c_arr, name):
    h = r // 2
    if kind == "blk":
        tr = _row_tile(h, 512)
        grid = (N_CHIPS, h // tr)
        g_spec = pl.BlockSpec((1, 1, tr, c), lambda j, i, c_ref: (j, c_ref[0], i, 0))
        o_spec = pl.BlockSpec((1, tr, c), lambda j, i, c_ref: (j, i, 0))
        gv, oshape = g.reshape(N_CHIPS, 2, h, c), (N_CHIPS, h, c)
    else:
        tr = _row_tile(h, 64)
        grid = (h // tr,)
        g_spec = pl.BlockSpec((1, tr, N_CHIPS * c), lambda i, c_ref: (c_ref[0], i, 0))
        o_spec = pl.BlockSpec((tr, N_CHIPS * c), lambda i, c_ref: (i, 0))
        gv, oshape = g.reshape(2, h, N_CHIPS * c), (h, N_CHIPS * c)

    def body(c_ref, g_ref, r_ref, o_ref, ob_ref):
        v = (g_ref[:, 0] if kind == "blk" else g_ref[0]) + r_ref[...]
        o_ref[...] = v
        ob_ref[...] = v.astype(BF16)

    return pl.pallas_call(
        body, name=name,
        grid_spec=pltpu.PrefetchScalarGridSpec(num_scalar_prefetch=1, grid=grid, in_specs=[g_spec, o_spec], out_specs=[o_spec] * 2),
        out_shape=[SDS(oshape, F32), SDS(oshape, BF16)],
        compiler_params=pltpu.CompilerParams(vmem_limit_bytes=VMEM_LIMIT),
    )(c_arr, gv, recv)


def _sum_adamw(pair, recv, w, m, v, kind, r, c, mc_arr, name):
    h = r // 2
    tr = _row_tile(h, 256)
    nt = h // tr
    if kind == "blk":
        p_spec = pl.BlockSpec((1, tr, c), lambda i, mc: (mc[0], i, 0))
    else:
        p_spec = pl.BlockSpec((tr, c), lambda i, mc: (i, mc[0]))
    mine = pl.BlockSpec((tr, c), lambda i, mc: (mc[1] * nt + i, 0))

    def body(mc, a_ref, r_ref, w_ref, m_ref, v_ref, g_out, d_out, m_out, v_out):
        own = a_ref[0] if kind == "blk" else a_ref[...]
        g = ((own + r_ref[0].astype(F32)) + r_ref[1].astype(F32)) + r_ref[2].astype(F32)
        g_out[...] = g
        d_out[...], m_out[...], v_out[...] = _adamw_rows(w_ref[...], g, m_ref[...], v_ref[...])

    return pl.pallas_call(
        body, name=name,
        grid_spec=pltpu.PrefetchScalarGridSpec(
            num_scalar_prefetch=1, grid=(nt,),
            in_specs=[p_spec, pl.BlockSpec((3, tr, c), lambda i, mc: (0, i, 0)), mine, mine, mine], out_specs=[mine] * 4),
        out_shape=[SDS((r, c), F32)] * 4,
        compiler_params=pltpu.CompilerParams(vmem_limit_bytes=VMEM_LIMIT),
    )(mc_arr, pair, recv, w, m, v)


class _GroupReduce:
    def __init__(self, tag, specs, c_arr, mc_arr):
        self.tag, self.specs, self.c_arr, self.mc_arr = tag, specs, c_arr, mc_arr
        self.n = len(specs)

    def _plan(self, step):
        specs, n = self.specs, self.n

        def copies(refs, ss, rs, received):
            x, y, c = _coords()
            sib, out = (x, y, 1 - c), []
            for w, (_, kind, r, cc) in enumerate(specs):
                if step == "join":
                    for q in range(4):
                        there = refs[4 * w + q].at[_half(r, 1 - c if received else c, 8)]
                        out.append(_rcopy(there, there, ss, rs, 4 * w + q, sib))
                    continue
                src, land = refs[w], refs[n + w]
                if step == "swap":
                    rows = _half(r, 1 - c, 8)
                    part = src.at[:, rows] if kind == "blk" else src.at[rows]
                    out.append(_rcopy(land if received else part, land, ss, rs, w, sib))
                else:
                    for kk, (fx, fy) in enumerate(_CHIP_FLIPS):
                        px, py = _flip(x, fx), _flip(y, fy)
                        part = land.at[kk] if received else _slot(src, kind, 2 * px + py, None, cc)
                        out.append(_rcopy(part, land.at[kk], ss, rs, 3 * w + kk, (px, py, c)))
            return out

        def issue(refs, ss, rs):
            return copies(refs, ss, rs, False)

        def expect(refs, ss, rs):
            return copies(refs, ss, rs, False), copies(refs, ss, rs, True)

        return issue, expect

    def swap_start(self, grads, after=None):
        lands = [lax.empty(_form(kind, r // 2, c), F32) for _, kind, r, c in self.specs]
        ss, rs, bufs, tok = _copies_start(f"rs_{self.tag}_swap", list(grads) + lands, self.n, self._plan("swap")[0], after=after)
        self.state = (ss, rs, bufs)
        return tok

    def swap_wait_ici_start(self, after):
        ss, rs, bufs = self.state
        bufs = _copies_wait(f"rs_{self.tag}_swap_wait", bufs, ss, rs, after, self._plan("swap")[1])
        pairs = [_add_pair(bufs[w], bufs[self.n + w], kind, r, c, self.c_arr, name=f"rs_{self.tag}_pair_{nm}")
                 for w, (nm, kind, r, c) in enumerate(self.specs)]
        self.pair = [pr[0] for pr in pairs]
        lands = [lax.empty((3, r // 2, c), BF16) for _, _, r, c in self.specs]
        ss, rs, bufs, tok = _copies_start(f"rs_{self.tag}_ici", [pr[1] for pr in pairs] + lands, 3 * self.n, self._plan("ici")[0])
        self.state = (ss, rs, bufs)
        return tok

    def ici_wait_join_start(self, after, state):
        ss, rs, bufs = self.state
        bufs = _copies_wait(f"rs_{self.tag}_ici_wait", bufs, ss, rs, after, self._plan("ici")[1])
        outs = []
        for w, (nm, kind, r, c) in enumerate(self.specs):
            outs += _sum_adamw(self.pair[w], bufs[self.n + w], *state[nm], kind, r, c, self.mc_arr, name=f"rs_{self.tag}_adamw_{nm}")
        ss, rs, bufs, tok = _copies_start(f"rs_{self.tag}_join", outs, 4 * self.n, self._plan("join")[0])
        self.state = (ss, rs, bufs)
        return tok

    def join_wait(self, after):
        ss, rs, bufs = self.state
        bufs = _copies_wait(f"rs_{self.tag}_join_wait", bufs, ss, rs, after, self._plan("join")[1])
        return {nm: tuple(bufs[4 * w:4 * w + 4]) for w, (nm, _, _, _) in enumerate(self.specs)}


def _all_reduce_small(buf):
    rows, cols = buf.shape

    def body(x_ref, o_ref, gath, send_sems, recv_sems):
        x, y, c = _coords()
        me = 4 * x + 2 * y + c
        gath[me] = x_ref[...]
        sends = []
        for kk in range(1, 8):
            f = (kk >> 2) & 1, (kk >> 1) & 1, kk & 1
            px, py, pc = _flip(x, f[0]), _flip(y, f[1]), _flip(c, f[2])
            cp = pltpu.make_async_remote_copy(src_ref=x_ref, dst_ref=gath.at[me], send_sem=send_sems.at[kk - 1],
                                              recv_sem=recv_sems.at[kk - 1], device_id=(px, py, pc), device_id_type=MESH)
            cp.start()
            sends.append(cp)
        for kk in range(1, 8):
            f = (kk >> 2) & 1, (kk >> 1) & 1, kk & 1
            px, py, pc = _flip(x, f[0]), _flip(y, f[1]), _flip(c, f[2])
            there = gath.at[4 * px + 2 * py + pc]
            pltpu.make_async_remote_copy(src_ref=there, dst_ref=there, send_sem=send_sems.at[kk - 1],
                                         recv_sem=recv_sems.at[kk - 1], device_id=(px, py, pc), device_id_type=MESH).wait_recv()
        for cp in sends:
            cp.wait_send()
        acc = gath[0]
        for j in range(1, 8):
            acc = acc + gath[j]
        o_ref[...] = acc

    return pl.pallas_call(
        body, name="all_reduce_small",
        in_specs=[pl.BlockSpec(memory_space=pltpu.VMEM)], out_specs=pl.BlockSpec(memory_space=pltpu.VMEM),
        out_shape=SDS((rows, cols), F32),
        scratch_shapes=[pltpu.VMEM((8, rows, cols), F32), pltpu.SemaphoreType.DMA((7,)), pltpu.SemaphoreType.DMA((7,))],
    )(buf)


def _adamw_rows(w, g, m, v):
    m = ADAM_B1 * m + (1.0 - ADAM_B1) * g
    v = ADAM_B2 * v + (1.0 - ADAM_B2) * jnp.square(g)
    m_hat = m / (1.0 - ADAM_B1 ** ADAM_STEP)
    v_hat = v / (1.0 - ADAM_B2 ** ADAM_STEP)
    return -ADAM_LR * (m_hat / (jnp.sqrt(v_hat) + ADAM_EPS) + ADAM_WD * w), m, v


def _adamw(w, g, m, v, name, dep=None):
    rows, cols = w.shape
    tm = _pick(rows, (256, 128, 64, 16, 8))
    return _rows_call(_adamw_rows, [(t, 0, cols) for t in (w, g, m, v)], [], [(cols, F32)] * 3, tm=tm, name=name, dep=dep)


def _pack_small(parts):
    flat = jnp.concatenate([parts[n].reshape(-1) for n, _ in SMALL])
    return jnp.pad(flat, (0, SMALL_ROWS * PACK_COLS - flat.shape[0])).reshape(SMALL_ROWS, PACK_COLS)


def _unpack_small(buf, shapes):
    flat, out, off = buf.reshape(-1), {}, 0
    for n, sz in SMALL:
        out[n] = flat[off:off + sz].reshape(shapes[n])
        off += sz
    return out


def _lora_stack(parts):
    return jnp.concatenate([parts[n] for n, _ in LORA], axis=-2)


def _lora_split(stacked):
    out, off = {}, 0
    for n, rows in LORA:
        out[n] = stacked[..., off:off + rows, :]
        off += rows
    return out


def _ffn_gate_up(h, wgt, wut, name, dep=None):
    s, d = h.shape
    nblk, f, _ = wgt.shape
    tm = _pick(s, (1024, 512, 256))
    dn = (((1,), (1,)), ((), ()))

    def body(h_ref, wg_ref, wu_ref, *rest):
        g_ref, u_ref, a_ref = rest[-3:]
        hh = h_ref[...]
        g = lax.dot_general(hh, wg_ref[0], dn, preferred_element_type=F32)
        u = lax.dot_general(hh, wu_ref[0], dn, preferred_element_type=F32)
        g_ref[0], u_ref[0] = g, u
        a_ref[0] = _swiglu_act(g, u).astype(BF16)

    w_spec = pl.BlockSpec((1, f, d), lambda j, i: (j, 0, 0))
    o_spec = pl.BlockSpec((1, tm, f), lambda j, i: (j, i, 0))
    extra = [] if dep is None else [dep]
    return pl.pallas_call(
        body, name=name, grid=(nblk, s // tm),
        in_specs=[pl.BlockSpec((tm, d), lambda j, i: (i, 0)), w_spec, w_spec] + [pl.BlockSpec(memory_space=pl.ANY)] * len(extra),
        out_specs=[o_spec] * 3,
        out_shape=[SDS((nblk, s, f), F32), SDS((nblk, s, f), F32), SDS((nblk, s, f), BF16)],
        compiler_params=pltpu.CompilerParams(dimension_semantics=("parallel", "parallel"), vmem_limit_bytes=VMEM_LIMIT),
    )(h, wgt, wut, *extra)


def _ffn_down_dx(dx_bf, wd, gate, up, name, dep=None):
    s, d = dx_bf.shape
    nblk, f, _ = wd.shape
    tm = _pick(s, (1024, 512, 256))
    dn = (((1,), (1,)), ((), ()))

    def body(dx_ref, wd_ref, g_ref, u_ref, *rest):
        dg_ref, du_ref = rest[-2:]
        dact = 0.5 * lax.dot_general(dx_ref[...], wd_ref[0], dn, preferred_element_type=F32)
        _, vjp = jax.vjp(_swiglu_act, g_ref[0], u_ref[0])
        dg, du = vjp(dact)
        dg_ref[0], du_ref[0] = dg.astype(BF16), du.astype(BF16)

    o_spec = pl.BlockSpec((1, tm, f), lambda j, i: (j, i, 0))
    extra = [] if dep is None else [dep]
    return pl.pallas_call(
        body, name=name, grid=(nblk, s // tm),
        in_specs=[pl.BlockSpec((tm, d), lambda j, i: (i, 0)), pl.BlockSpec((1, f, d), lambda j, i: (j, 0, 0)), o_spec, o_spec]
        + [pl.BlockSpec(memory_space=pl.ANY)] * len(extra),
        out_specs=[o_spec] * 2, out_shape=[SDS((nblk, s, f), BF16)] * 2,
        compiler_params=pltpu.CompilerParams(dimension_semantics=("parallel", "parallel"), vmem_limit_bytes=VMEM_LIMIT),
    )(dx_bf, wd, gate, up, *extra)


def _ffn_fwd(x, gain, wgt, wut, wd, tag, h=None, dep=None):
    if h is None:
        h = _rows_call(_rms, [(x, 0, D_MODEL)], [gain], [(D_MODEL, BF16)], tm=512, name=f"{tag}_norm")[0]
    gate, up, act = _ffn_gate_up(h, wgt, wut, f"{tag}_gate_up", dep=dep)
    x_new = _mm(act, wd, sum_blocks=True, res=x, alpha=0.5, name=f"{tag}_down")
    return x_new, (x, h, gate, up, act)


def _ffn_bwd(dx_new, dx_new_bf, saved, gain, wgt, wut, wd, tag, dep=None, hooks=None):
    x, h, gate, up, act = saved
    hooks = hooks or {}

    def hook(name, *vals):
        return hooks[name](*vals) if name in hooks else None

    d_wd = _mm(act, dx_new_bf, ta=True, alpha=0.5, name=f"{tag}_down_dw")
    dep = hook("down", d_wd) if "down" in hooks else dep
    dgate, dup = _ffn_down_dx(dx_new_bf, wd, gate, up, f"{tag}_down_dx", dep=dep)
    d_wgt = _mm(dgate, h, ta=True, dep=hook("mid", dgate), name=f"{tag}_gate_dw")
    d_wut = _mm(dup, h, ta=True, name=f"{tag}_up_dw")
    dh = _mm(dgate, wgt, sum_blocks=True, dep=hook("dw", d_wgt, d_wut), name=f"{tag}_gate_dx")
    dx, dx_bf, dgain = _mm(dup, wut, sum_blocks=True, res=dh, dep=hook("dx", dh), post=_norm_bwd_post(x, gain, dx_new),
                           name=f"{tag}_up_dx")
    hook("end", dx_bf)
    return dx, dx_bf, dgain, d_wgt, d_wut, d_wd


def _norm_bwd_post(x, gain, dres):
    def f(dht, xt, drt, gt):
        _, vjp = jax.vjp(_rms, xt, gt)
        dxt, dgt = vjp(dht)
        return dxt + drt, dxt + drt, dgt

    return f, [x, dres], [gain], [F32, BF16], [(1, D_MODEL)]


def kernel(x, p, positions, ffn1_norm, ffn1_w_gate, ffn1_w_up, ffn1_w_down, mix_norm, w_in, rwkv_mu, rwkv_w0, rwkv_w2, rwkv_a0, rwkv_a2, rwkv_g2, rwkv_k_k, rwkv_k_a, rwkv_r_k, rwkv_gn_w, rwkv_gn_b, q_norm, k_norm, w_br_rwkv, w_br_attn, w_out, ffn2_norm, ffn2_w_gate, ffn2_w_up, ffn2_w_down, ple_norm, ple_w_gate, ple_w_proj, loss_target, m_ffn1_norm, m_ffn1_w_gate, m_ffn1_w_up, m_ffn1_w_down, m_mix_norm, m_w_in, m_rwkv_mu, m_rwkv_w0, m_rwkv_w2, m_rwkv_a0, m_rwkv_a2, m_rwkv_g2, m_rwkv_k_k, m_rwkv_k_a, m_rwkv_r_k, m_rwkv_gn_w, m_rwkv_gn_b, m_q_norm, m_k_norm, m_w_br_rwkv, m_w_br_attn, m_w_out, m_ffn2_norm, m_ffn2_w_gate, m_ffn2_w_up, m_ffn2_w_down, m_ple_norm, m_ple_w_gate, m_ple_w_proj, v_ffn1_norm, v_ffn1_w_gate, v_ffn1_w_up, v_ffn1_w_down, v_mix_norm, v_w_in, v_rwkv_mu, v_rwkv_w0, v_rwkv_w2, v_rwkv_a0, v_rwkv_a2, v_rwkv_g2, v_rwkv_k_k, v_rwkv_k_a, v_rwkv_r_k, v_rwkv_gn_w, v_rwkv_gn_b, v_q_norm, v_k_norm, v_w_br_rwkv, v_w_br_attn, v_w_out, v_ffn2_norm, v_ffn2_w_gate, v_ffn2_w_up, v_ffn2_w_down, v_ple_norm, v_ple_w_gate, v_ple_w_proj):
    args = dict(locals())
    wts = {n: args[n] for n in WEIGHTS}
    mom_m = {n: args["m_" + n] for n in WEIGHTS}
    mom_v = {n: args["v_" + n] for n in WEIGHTS}
    x0, tgt = x[0], loss_target[0]
    s = x0.shape[0]
    p_tok = p[0, 0]

    vec = {n: wts[n].reshape(1, -1) for n, _ in SMALL}
    xi, yi, ci = _coords()
    me = 2 * xi + yi
    def laid(t, n):
        return jnp.transpose(t[n][0]) if n in TRANSPOSED else t[n][0]

    shard_of = {n: laid(wts, n) for g in GROUPS.values() for n, _, _, _ in g if n != "lora"}
    shard_of["lora"] = _lora_stack({n: wts[n][0] for n, _ in LORA})

    def whole_with_own(n, kind, r, c, tok=None):
        at = (me, 0, 0) if kind == "blk" else (0, me * c)
        own = (shard_of[n] if tok is None else shard_of[n] + tok[0, 0]).astype(BF16)
        return lax.dynamic_update_slice(lax.empty(_form(kind, r, c), BF16), own[None] if kind == "blk" else own, at)

    specs = {g: [(kind, r, c) for _, kind, r, c in grp] for g, grp in GROUPS.items()}
    plans = {(g, st): _gather_plan(specs[g], st) for g in GROUPS for st in ("ici", "d2d")}
    buf_f1 = [whole_with_own(*w) for w in GROUPS["f1"]]
    ss_0, rs_0, buf_f1, tok_0 = _copies_start("gather_f1_ici", buf_f1, 3 * len(buf_f1), plans["f1", "ici"][0])
    bufs = {g: [whole_with_own(*w, tok=tok_0) for w in GROUPS[g]] for g in ("mx", "f2")}
    buf_f1 = _copies_wait("gather_f1_ici_wait", buf_f1, ss_0, rs_0, bufs["mx"][0], plans["f1", "ici"][1])
    ss_1, rs_1, buf_f1, tok_1 = _copies_start("gather_f1_d2d", buf_f1, 3 * len(buf_f1), plans["f1", "d2d"][0])
    h1 = _rows_call(_rms, [(x0, 0, D_MODEL)], [vec["ffn1_norm"] + tok_1[0, 0]], [(D_MODEL, BF16)], tm=512, name="ffn1_norm")[0]
    buf_f1 = _copies_wait("gather_f1_d2d_wait", buf_f1, ss_1, rs_1, h1, plans["f1", "d2d"][1])
    wb = dict(zip([w[0] for w in GROUPS["f1"]], buf_f1))
    ss_a, rs_a, buf_mx, tok_a = _copies_start("gather_mx_ici", bufs["mx"], 3 * len(bufs["mx"]), plans["mx", "ici"][0],
                                              after=wb["ffn1_w_gate"])

    inv_freq = 1.0 / (ROPE_THETA ** (jnp.arange(0, HEAD, 2, dtype=F32) / HEAD))
    ang = positions[0].astype(F32)[:, None] * inv_freq
    cos, sin = jnp.cos(ang), jnp.sin(ang)
    cos2, sin2 = jnp.concatenate([cos, cos], axis=1), jnp.concatenate([-sin, sin], axis=1)

    x1, ffn1_saved = _ffn_fwd(x0, vec["ffn1_norm"], wb["ffn1_w_gate"], wb["ffn1_w_up"], wb["ffn1_w_down"], "ffn1", h=h1, dep=tok_a)
    buf_mx = _copies_wait("gather_mx_ici_wait", buf_mx, ss_a, rs_a, x1, plans["mx", "ici"][1])
    ss_b, rs_b, buf_mx, tok_b = _copies_start("gather_mx_d2d", buf_mx, 3 * len(buf_mx), plans["mx", "d2d"][0])
    ss_c, rs_c, buf_f2, tok_c = _copies_start("gather_f2_ici", bufs["f2"], 3 * len(bufs["f2"]), plans["f2", "ici"][0])
    h = _rows_call(_rms, [(x1, 0, D_MODEL)], [vec["mix_norm"] + (tok_b[0, 0] + tok_c[0, 0])], [(D_MODEL, BF16)], tm=256,
                   name="mix_norm")[0]
    buf_mx = _copies_wait("gather_mx_d2d_wait", buf_mx, ss_b, rs_b, h, plans["mx", "d2d"][1])
    wb.update(zip([w[0] for w in GROUPS["mx"]], buf_mx))
    w_in_all = wb["w_in"]
    w_in_r, w_in_a, w_in_g = w_in_all[:, :RWKV_COLS], w_in_all[:, RWKV_COLS:RWKV_COLS + ATTN_COLS], w_in_all[:, RWKV_COLS + ATTN_COLS:]
    lora = _lora_split(wb["lora"])
    w2, a2, g2 = lora["rwkv_w2"], lora["rwkv_a2"], lora["rwkv_g2"]
    z_r = _mm(h, w_in_r, name="in_rwkv")
    z_a = _mm(h, w_in_a, name="in_attn")
    z_g = _mm(h, w_in_g, name="in_gate")

    zs = _shift_fwd(z_r, vec["rwkv_mu"])
    pre_params = [vec["rwkv_w0"], w2, vec["rwkv_a0"], a2, g2, vec["rwkv_k_k"], vec["rwkv_k_a"]]
    def pre_fwd(*t):
        res = _rwkv_pre(*t)
        return res[1], res[2], res[4], res[5], res[6]

    lw, k2, na, kb, gate_r = _rows_call(pre_fwd, [(zs, 0, RWKV_COLS)], pre_params, [(RWKV_DIM, F32)] * 5, tm=512, name="rwkv_pre")
    y_scan, s0s, invs = _wkv_fwd(zs, lw, k2, na, kb)
    buf_f2 = _copies_wait("gather_f2_ici_wait", buf_f2, ss_c, rs_c, y_scan, plans["f2", "ici"][1])
    ss_d, rs_d, buf_f2, tok_d = _copies_start("gather_f2_d2d", buf_f2, 3 * len(buf_f2), plans["f2", "d2d"][0])
    post_params = [vec["rwkv_gn_w"] + tok_d[0, 0], vec["rwkv_gn_b"], vec["rwkv_r_k"]]
    post_rows = [(y_scan, 0, RWKV_DIM), (zs, 0, RWKV_DIM), (k2, 0, RWKV_DIM), (zs, 2, RWKV_DIM), (gate_r, 0, RWKV_DIM)]
    y_rwkv = _rows_call(_rwkv_post, post_rows, post_params, [(RWKV_DIM, BF16)], tm=512, name="rwkv_post")[0]
    buf_f2 = _copies_wait("gather_f2_d2d_wait", buf_f2, ss_d, rs_d, y_rwkv, plans["f2", "d2d"][1])
    wb.update(zip([w[0] for w in GROUPS["f2"]], buf_f2))
    w_brr, w_bra = wb["w_br_rwkv"], wb["w_br_attn"]
    w_o = wb["w_out"].reshape(D_MODEL, D_MODEL)
    w_pp, w_pg = wb["ple_w_proj"], wb["ple_w_gate"].reshape(D_MODEL, D_MODEL)

    def qk_fwd(qt, kt, ct, st, qg, kg):
        return _norm_rope(qt, qg, ct, st), _norm_rope(kt, kg, ct, st)

    qk_rows = [(z_a, 0, ATTN_DIM), (z_a, 1, ATTN_DIM), (cos2, 0, HEAD), (sin2, 0, HEAD)]
    q_rot, k_rot = _rows_call(qk_fwd, qk_rows, [vec["q_norm"], vec["k_norm"]], [(ATTN_DIM, BF16)] * 2, tm=512, name="attn_pre")
    def group(t, g, off=0):
        return t[:, off + g * GROUP_DIM:off + (g + 1) * GROUP_DIM].astype(BF16)

    qkv = [(group(q_rot, g), group(k_rot, g), group(z_a, g, 2 * ATTN_DIM)) for g in range(len(ATTN_DILATIONS))]
    outs, lses = zip(*[_attn_fwd(*qkv[g], d) for g, d in enumerate(ATTN_DILATIONS)])
    comb_rows = [(t, 0, GROUP_DIM) for t in outs + lses]
    y_attn = _rows_call(_attn_combine, comb_rows, [], [(GROUP_DIM, BF16)], tm=512, name="attn_combine")[0]

    br = _mm(y_rwkv, w_brr, name="branch_rwkv")
    ba = _mm(y_attn, w_bra, name="branch_attn")
    merge_rows = [(z_g, 0, D_MODEL), (z_g, 1, D_MODEL), (br, 0, D_MODEL), (ba, 0, D_MODEL)]
    merged = _rows_call(_merge, merge_rows, [], [(D_MODEL, BF16)], tm=512, name="merge")[0]
    x2 = _mm(merged, w_o, res=x1, name="out_proj")
    x3, ffn2_saved = _ffn_fwd(x2, vec["ffn2_norm"], wb["ffn2_w_gate"], wb["ffn2_w_up"], wb["ffn2_w_down"], "ffn2")
    hp = _rows_call(_rms, [(x3, 0, D_MODEL)], [vec["ple_norm"]], [(D_MODEL, BF16)], tm=512, name="ple_norm")[0]
    pg = _mm(hp, w_pg, name="ple_gate")
    pp = _mm(p_tok, w_pp, name="ple_proj")

    def head(x3t, pgt, ppt, tt):
        sg = _sigmoid(pgt)
        err = x3t + sg * ppt - tt
        dx4 = err * (1.0 / D_MODEL)
        loss = 0.5 * jnp.sum(jnp.mean(err * err, axis=-1, keepdims=True), axis=0, keepdims=True)
        return dx4, dx4 * ppt * sg * (1.0 - sg), dx4 * sg, jnp.broadcast_to(loss, (8, 128))

    head_rows = [(x3, 0, D_MODEL), (pg, 0, D_MODEL), (pp, 0, D_MODEL), (tgt, 0, D_MODEL)]
    dx4, dpg, dpp, loss_tile = _rows_call(head, head_rows, [], [(D_MODEL, F32), (D_MODEL, BF16), (D_MODEL, BF16)], [(8, 128)],
                                          tm=512, name="ple_loss")

    c_arr = jnp.reshape(ci, (1,)).astype(jnp.int32)
    mc_arr = jnp.stack([me, ci]).astype(jnp.int32)
    red = {g: _GroupReduce(g, grp, c_arr, mc_arr) for g, grp in REDUCE_GROUPS.items()}

    def adam_state(names):
        out = {}
        for n in names:
            if n == "lora":
                out[n] = tuple(_lora_stack({k: t[k][0] for k, _ in LORA}) for t in (wts, mom_m, mom_v))
            else:
                out[n] = (laid(wts, n), laid(mom_m, n), laid(mom_v, n))
        return out

    adam = {g: adam_state([w[0] for w in grp]) for g, grp in REDUCE_GROUPS.items()}
    done = {}
    gw, gs = {}, {}
    gw["ple_w_proj"] = _mm(p_tok, dpp, ta=True, name="ple_proj_dw")
    gw["ple_w_gate"] = _mm(hp, dpg, ta=True, name="ple_gate_dw")
    dx3, dx3_bf, gs["ple_norm"] = _mm(dpg, w_pg, tb=True, post=_norm_bwd_post(x3, vec["ple_norm"], dx4), name="ple_gate_dx")
    dx2, dx2_bf, gs["ffn2_norm"], gw["ffn2_w_gate"], gw["ffn2_w_up"], gw["ffn2_w_down"] = _ffn_bwd(
        dx3, dx3_bf, ffn2_saved, vec["ffn2_norm"], wb["ffn2_w_gate"], wb["ffn2_w_up"], wb["ffn2_w_down"], "ffn2")
    gw["ple_w_gate"] = gw["ple_w_gate"].reshape(N_CHIPS, D_MODEL // N_CHIPS, D_MODEL)
    tok = red["f2"].swap_start([gw[w[0]] for w in REDUCE_GROUPS["f2"]])
    gw["w_out"] = _mm(merged, dx2_bf, ta=True, name="out_proj_dw")
    dmerged = _mm(dx2_bf, w_o, tb=True, dep=tok, name="out_proj_dx")

    def merge_bwd(zgr, zga, brt, bat, ct):
        _, vjp = jax.vjp(_merge, zgr, zga, brt, bat)
        d1, d2, d3, d4 = vjp(ct)
        return jnp.concatenate([d1, d2], axis=1), d3, d4

    dz_g, dbr, dba = _rows_call(merge_bwd, merge_rows + [(dmerged, 0, D_MODEL)], [],
                                [(2 * D_MODEL, BF16), (D_MODEL, BF16), (D_MODEL, BF16)], tm=512, name="merge_bwd")
    tok = red["f2"].swap_wait_ici_start(dz_g)
    gw["w_br_rwkv"] = _mm(y_rwkv, dbr, ta=True, name="branch_rwkv_dw")
    gw["w_br_attn"] = _mm(y_attn, dba, ta=True, name="branch_attn_dw")
    dy_rwkv = _mm(dbr, w_brr, tb=True, dep=tok, name="branch_rwkv_dx")
    dy_attn = _mm(dba, w_bra, tb=True, dep=tok, name="branch_attn_dx")

    def comb_bwd(*t):
        _, vjp = jax.vjp(_attn_combine, *t[:6])
        return vjp(t[6])

    dcomb = _rows_call(comb_bwd, comb_rows + [(dy_attn, 0, GROUP_DIM)], [], [(GROUP_DIM, F32)] * 6, tm=512, name="attn_combine_bwd")
    dqs, dks, dvs = zip(*[_attn_bwd(*qkv[g], d, dcomb[g], dcomb[3 + g]) for g, d in enumerate(ATTN_DILATIONS)])

    def qk_bwd(qt, kt, ct, st, *rest):
        dq = jnp.concatenate(rest[0:3], axis=1)
        dk = jnp.concatenate(rest[3:6], axis=1)
        qg, kg = rest[9], rest[10]
        _, vjp = jax.vjp(lambda a_, b_, c_, d_: qk_fwd(a_, b_, ct, st, c_, d_), qt, kt, qg, kg)
        dqt, dkt, dqg, dkg = vjp((dq, dk))
        return jnp.concatenate((dqt, dkt) + tuple(rest[6:9]), axis=1), dqg, dkg

    dz_a, gs["q_norm"], gs["k_norm"] = _rows_call(
        qk_bwd, qk_rows + [(t, 0, GROUP_DIM) for t in dqs + dks + dvs], [vec["q_norm"], vec["k_norm"]],
        [(ATTN_COLS, BF16)], [(1, HEAD), (1, HEAD)], tm=512, name="attn_pre_bwd")
    tok = red["f2"].ici_wait_join_start(dz_a, adam["f2"])

    def post_bwd(*t):
        _, vjp = jax.vjp(_rwkv_post, *t[:5], *t[6:])
        return vjp(t[5])

    dy_scan, dr_post, dk2_post, dv_post, dgate_r, gs["rwkv_gn_w"], gs["rwkv_gn_b"], gs["rwkv_r_k"] = _rows_call(
        post_bwd, post_rows + [(dy_rwkv, 0, RWKV_DIM)], post_params, [(RWKV_DIM, F32)] * 5, [(1, RWKV_DIM)] * 3,
        tm=512, name="rwkv_post_bwd", dep=tok)
    done.update(red["f2"].join_wait(dy_scan))
    dr_s, dlw, dk2_s, dv_s, dna, dkb = _wkv_bwd(zs, lw, k2, na, kb, s0s, invs, dy_scan)

    def pre_bwd(zt, c_r1, c_r2, c_lw, c_k1, c_k2, c_v1, c_v2, c_a, c_b, c_g, *params):
        _, vjp = jax.vjp(_rwkv_pre, zt, *params)
        return vjp((c_r1 + c_r2, c_lw, c_k1 + c_k2, c_v1 + c_v2, c_a, c_b, c_g))

    pre_cts = [dr_s, dr_post, dlw, dk2_s, dk2_post, dv_s, dv_post, dna, dkb, dgate_r]
    dzs, gs["rwkv_w0"], g_w2, gs["rwkv_a0"], g_a2, g_g2, gs["rwkv_k_k"], gs["rwkv_k_a"] = _rows_call(
        pre_bwd, [(zs, 0, RWKV_COLS)] + [(t, 0, RWKV_DIM) for t in pre_cts], pre_params, [(RWKV_COLS, F32)],
        [q.shape for q in pre_params], tm=512, name="rwkv_pre_bwd")
    dz_r, gs["rwkv_mu"] = _shift_bwd(z_r, vec["rwkv_mu"], dzs)

    g_w_in = jnp.concatenate([_mm(h, dz_r, ta=True, name="in_rwkv_dw"), _mm(h, dz_a, ta=True, name="in_attn_dw"),
                              _mm(h, dz_g, ta=True, name="in_gate_dw")], axis=1)
    gw["w_in"], gw["lora"] = g_w_in, jnp.concatenate([g_w2, g_a2, g_g2], axis=0)
    gw["w_out"] = gw["w_out"].reshape(N_CHIPS, D_MODEL // N_CHIPS, D_MODEL)
    tok = red["mx"].swap_start([gw[w[0]] for w in REDUCE_GROUPS["mx"]])
    dh = _mm(dz_r, w_in_r, tb=True, dep=tok, name="in_rwkv_dx")
    dh = _mm(dz_a, w_in_a, tb=True, res=dh, name="in_attn_dx")
    dx1, dx1_bf, gs["mix_norm"] = _mm(dz_g, w_in_g, tb=True, res=dh, post=_norm_bwd_post(x1, vec["mix_norm"], dx2), name="in_gate_dx")
    tok_mx = red["mx"].swap_wait_ici_start(dx1_bf)
    hooks = {"down": lambda d_wd: red["f1d"].swap_start([d_wd], after=tok_mx),
             "mid": lambda dgate: red["f1d"].swap_wait_ici_start(dgate),
             "dw": lambda d_wgt, d_wut: red["f1g"].swap_start([d_wgt, d_wut]),
             "dx": lambda part: red["f1g"].swap_wait_ici_start(part) + red["f1d"].ici_wait_join_start(part, adam["f1d"]),
             "end": lambda dh_: red["mx"].ici_wait_join_start(dh_, adam["mx"])}
    dx0, _, gs["ffn1_norm"], gw["ffn1_w_gate"], gw["ffn1_w_up"], gw["ffn1_w_down"] = _ffn_bwd(
        dx1, dx1_bf, ffn1_saved, vec["ffn1_norm"], wb["ffn1_w_gate"], wb["ffn1_w_up"], wb["ffn1_w_down"], "ffn1", hooks=hooks)

    flat = jnp.concatenate([gs[n].reshape(-1) for n, _ in SMALL] + [loss_tile[0, 0:1]])
    small_buf = jnp.pad(flat, (0, SMALL_ROWS * PACK_COLS - flat.shape[0])).reshape(SMALL_ROWS, PACK_COLS)
    small_sum = _all_reduce_small(small_buf)
    n_small = sum(sz for _, sz in SMALL)
    loss = small_sum.reshape(-1)[n_small]
    grad_small = _unpack_small(small_sum, {n: wts[n].shape for n, _ in SMALL})
    d_s, m_s, v_s = _adamw(_pack_small(wts), small_sum, _pack_small(mom_m), _pack_small(mom_v), name="adamw_small", dep=dx0)
    shapes = {n: wts[n].shape for n, _ in SMALL}
    d_s, m_s, v_s = _unpack_small(d_s, shapes), _unpack_small(m_s, shapes), _unpack_small(v_s, shapes)
    grads, deltas, new_m, new_v = {}, {}, {}, {}
    for n, _ in SMALL:
        grads[n], deltas[n], new_m[n], new_v[n] = grad_small[n], d_s[n], m_s[n], v_s[n]

    tok = red["f1g"].ici_wait_join_start(m_s["ffn1_norm"], adam["f1g"])
    for g in ("mx", "f1d", "f1g"):
        done.update(red[g].join_wait(tok))
    for n, res in done.items():
        for store, val in zip((grads, deltas, new_m, new_v), res):
            if n == "lora":
                store.update({k: t[None] for k, t in _lora_split(val).items()})
            else:
                store[n] = (jnp.transpose(val) if n in TRANSPOSED else val)[None]

    return (loss, dx0[None], *[grads[n] for n in WEIGHTS], *[deltas[n] for n in WEIGHTS],
            *[new_m[n] for n in WEIGHTS], *[new_v[n] for n in WEIGHTS])
```

```python
import functools

import jax
import jax.numpy as jnp
from jax import lax
from jax.experimental import pallas as pl
from jax.experimental.pallas import tpu as pltpu

F32, BF16 = jnp.float32, jnp.bfloat16
HI = lax.Precision.HIGHEST
MESH = pl.DeviceIdType.MESH
SDS = jax.ShapeDtypeStruct

D_MODEL = 1024
HEAD = 64
RWKV_HEADS = 8
RWKV_DIM = RWKV_HEADS * HEAD
DECAY_LORA, ICLR_LORA, GATE_LORA = 64, 64, 128
GN_EPS = 64e-5
RMS_EPS = 1e-6
ATTN_DILATIONS = (1, 4, 16)
BAND = 128
ATTN_DIM = 768
GROUP_DIM = 256
ROPE_THETA = 10000.0
NEG_INF = -1e30
RWKV_COLS = 3 * RWKV_DIM + DECAY_LORA + ICLR_LORA + GATE_LORA
ATTN_COLS = 3 * ATTN_DIM
ADAM_LR, ADAM_B1, ADAM_B2, ADAM_EPS, ADAM_WD, ADAM_STEP = 0.001, 0.9, 0.999, 1e-08, 0.01, 10

WKV_CHUNK = 64
WKV_HEADS_PER_STEP = 8
N_CHIPS = 4
PACK_COLS = 1024
VMEM_LIMIT = 48 * 1024 * 1024

TRANSPOSED = ("ffn1_w_gate", "ffn1_w_up", "ffn2_w_gate", "ffn2_w_up")
LORA = (("rwkv_w2", 64), ("rwkv_a2", 64), ("rwkv_g2", 128))
_FFN1 = (("ffn1_w_gate", "blk", 704, 1024), ("ffn1_w_up", "blk", 704, 1024), ("ffn1_w_down", "blk", 704, 1024))
_FFN2 = (("ffn2_w_gate", "blk", 704, 1024), ("ffn2_w_up", "blk", 704, 1024), ("ffn2_w_down", "blk", 704, 1024))
_IN = (("w_in", "col", 1024, 1536), ("lora", "col", 256, 128))
_BRANCH = (("w_br_rwkv", "col", 512, 256), ("w_br_attn", "col", 256, 256), ("w_out", "blk", 256, 1024))
_PLE = (("ple_w_gate", "blk", 256, 1024), ("ple_w_proj", "col", 256, 256))
GROUPS = {"f1": _FFN1, "mx": _IN, "f2": _BRANCH + _FFN2 + _PLE}
REDUCE_GROUPS = {"f2": _FFN2 + _PLE, "mx": _IN + _BRANCH, "f1d": _FFN1[2:], "f1g": _FFN1[:2]}
SMALL = (
    ("ffn1_norm", 1024), ("mix_norm", 1024), ("ffn2_norm", 1024), ("ple_norm", 1024), ("rwkv_mu", 1792),
    ("rwkv_w0", 512), ("rwkv_a0", 512), ("rwkv_k_k", 512), ("rwkv_k_a", 512), ("rwkv_r_k", 512),
    ("rwkv_gn_w", 512), ("rwkv_gn_b", 512), ("q_norm", 64), ("k_norm", 64),
)
SMALL_ROWS = 16
WEIGHTS = (
    "ffn1_norm", "ffn1_w_gate", "ffn1_w_up", "ffn1_w_down", "mix_norm", "w_in", "rwkv_mu", "rwkv_w0", "rwkv_w2",
    "rwkv_a0", "rwkv_a2", "rwkv_g2", "rwkv_k_k", "rwkv_k_a", "rwkv_r_k", "rwkv_gn_w", "rwkv_gn_b", "q_norm", "k_norm",
    "w_br_rwkv", "w_br_attn", "w_out", "ffn2_norm", "ffn2_w_gate", "ffn2_w_up", "ffn2_w_down", "ple_norm",
    "ple_w_gate", "ple_w_proj",
)


def _row_tile(n, most=704):
    for t in range(most - most % 16, 0, -16):
        if n % t == 0:
            return t
    return n


def _pick(n, cands):
    for c in cands:
        if n % c == 0:
            return c
    return n


def _mm(a, b, *, ta=False, tb=False, sum_blocks=False, out_dtype=F32, res=None, alpha=1.0, dep=None, post=None, name):
    flat = a.ndim == 2 and b.ndim == 2
    a3 = a if a.ndim == 3 else a[None]
    b3 = b if b.ndim == 3 else b[None]
    na, nbb = a3.shape[0], b3.shape[0]
    nblk = max(na, nbb)
    kdim, m = (a3.shape[1], a3.shape[2]) if ta else (a3.shape[2], a3.shape[1])
    n = b3.shape[1] if tb else b3.shape[2]
    assert (b3.shape[2] if tb else b3.shape[1]) == kdim
    tm = _pick(m, (1024, 512, 256, 128) if post is None else (512, 256, 128))
    tn = _pick(n, (1024, 896, 768, 512, 256, 128))
    tk = kdim if kdim <= 2304 else _pick(kdim, (1024, 512, 256, 128))
    nk = kdim // tk
    direct = nk == 1 and not sum_blocks

    if sum_blocks:
        grid = (m // tm, n // tn, nblk, nk)

        def ids(i, c, j, k):
            return i, c, j, k
    else:
        grid = (nblk, m // tm, n // tn, nk)

        def ids(j, i, c, k):
            return i, c, j, k

    def amap(*g):
        i, c, j, k = ids(*g)
        jj = j if na > 1 else 0
        return (jj, k, i) if ta else (jj, i, k)

    def bmap(*g):
        i, c, j, k = ids(*g)
        jj = j if nbb > 1 else 0
        return (jj, c, k) if tb else (jj, k, c)

    if sum_blocks:
        oshape, oblk = (m, n), (tm, tn)

        def omap(*g):
            i, c, j, k = ids(*g)
            return i, c
    else:
        oshape, oblk = (nblk, m, n), (1, tm, tn)

        def omap(*g):
            i, c, j, k = ids(*g)
            return j, i, c

    dn = (((0 if ta else 1,), (1 if tb else 0,)), ((), ()))
    has_res = res is not None
    p_f, p_rows, p_params, p_dtypes, p_accs = post if post is not None else (None, [], [], [], [])
    assert post is None or sum_blocks or flat
    n_in = 2 + has_res + len(p_rows) + len(p_params) + (dep is not None)

    def tile_map(*g):
        i, c, j, k = ids(*g)
        return i, c

    def body(*refs):
        refs = list(refs)
        acc = None if direct else refs.pop()
        o_refs = refs[n_in:]
        a_ref, b_ref = refs[0], refs[1]
        r_ref = refs[2] if has_res else None
        pr_refs = refs[2 + has_res:2 + has_res + len(p_rows)]
        pp_refs = refs[2 + has_res + len(p_rows):2 + has_res + len(p_rows) + len(p_params)]
        first_tile = jnp.logical_and(pl.program_id(0 if sum_blocks else 1) == 0, pl.program_id(1 if sum_blocks else 2) == 0)

        def finish(v):
            if alpha != 1.0:
                v = v * alpha
            if has_res:
                v = v + r_ref[...].reshape(v.shape).astype(F32)
            if post is None:
                o_refs[0][...] = v.reshape(o_refs[0].shape).astype(o_refs[0].dtype)
                return
            outs = p_f(v, *[t[...] for t in pr_refs], *[t[...] for t in pp_refs])
            for o_ref, val in zip(o_refs, outs[:len(p_dtypes)]):
                o_ref[...] = val.astype(o_ref.dtype)
            for o_ref, val in zip(o_refs[len(p_dtypes):], outs[len(p_dtypes):]):
                @pl.when(first_tile)
                def _():
                    o_ref[...] = jnp.zeros_like(o_ref)

                o_ref[...] += val.reshape(o_ref.shape)

        if direct:
            finish(lax.dot_general(a_ref[0].astype(BF16), b_ref[0].astype(BF16), dn, preferred_element_type=F32))
            return
        k = pl.program_id(3)
        if sum_blocks:
            j = pl.program_id(2)
            first = jnp.logical_and(j == 0, k == 0)
            last = jnp.logical_and(j == nblk - 1, k == nk - 1)
        else:
            first, last = k == 0, k == nk - 1

        @pl.when(first)
        def _():
            acc[...] = jnp.zeros_like(acc)

        acc[...] += lax.dot_general(a_ref[0].astype(BF16), b_ref[0].astype(BF16), dn, preferred_element_type=F32)

        @pl.when(last)
        def _():
            finish(acc[...])

    in_specs = [pl.BlockSpec((1, tk, tm) if ta else (1, tm, tk), amap), pl.BlockSpec((1, tn, tk) if tb else (1, tk, tn), bmap)]
    args = [a3, b3]
    if has_res:
        res3 = res if (sum_blocks or res.ndim == 3) else res[None]
        in_specs.append(pl.BlockSpec(oblk, omap))
        args.append(res3)
    in_specs += [pl.BlockSpec((tm, tn), tile_map) for _ in p_rows]
    in_specs += [pl.BlockSpec(t.shape, functools.partial(lambda *g, nd: (0,) * nd, nd=t.ndim)) for t in p_params]
    args += list(p_rows) + list(p_params)
    if dep is not None:
        in_specs.append(pl.BlockSpec(memory_space=pl.ANY))
        args.append(dep)
    if post is None:
        out_specs, out_shape = pl.BlockSpec(oblk, omap), SDS(oshape, out_dtype)
        semantics = ("parallel", "parallel", "arbitrary", "arbitrary") if sum_blocks else ("parallel", "parallel", "parallel", "arbitrary")
    else:
        out_specs = [pl.BlockSpec((tm, tn), tile_map) for _ in p_dtypes]
        out_specs += [pl.BlockSpec(tuple(sh), functools.partial(lambda *g, nd: (0,) * nd, nd=len(sh))) for sh in p_accs]
        out_shape = [SDS((m, n), dt) for dt in p_dtypes] + [SDS(tuple(sh), F32) for sh in p_accs]
        semantics = ("arbitrary",) * 4
    out = pl.pallas_call(
        body,
        name=name,
        grid=grid,
        in_specs=in_specs,
        out_specs=out_specs,
        out_shape=out_shape,
        scratch_shapes=[] if direct else [pltpu.VMEM((tm, tn), F32)],
        compiler_params=pltpu.CompilerParams(dimension_semantics=semantics, vmem_limit_bytes=VMEM_LIMIT),
    )(*args)
    if post is not None:
        return out
    if flat and not sum_blocks:
        out = out[0]
    return out


def _rows_call(f, rows, params, outs, accs=(), *, tm, name, dep=None):
    s = rows[0][0].shape[0]
    nr, npar, no = len(rows), len(params), len(outs)
    nin = nr + npar + (0 if dep is None else 1)
    in_specs = [pl.BlockSpec((tm, w), functools.partial(lambda i, cb: (i, cb), cb=cb)) for (_, cb, w) in rows]
    in_specs += [pl.BlockSpec(p.shape, functools.partial(lambda i, nd: (0,) * nd, nd=p.ndim)) for p in params]
    if dep is not None:
        in_specs.append(pl.BlockSpec(memory_space=pl.ANY))
    out_shape = [SDS((s, w), dt) for (w, dt) in outs] + [SDS(tuple(sh), F32) for sh in accs]
    out_specs = [pl.BlockSpec((tm, w), lambda i: (i, 0)) for (w, _) in outs]
    out_specs += [pl.BlockSpec(tuple(sh), functools.partial(lambda i, nd: (0,) * nd, nd=len(sh))) for sh in accs]

    def body(*refs):
        rin, pin = refs[:nr], refs[nr:nr + npar]
        oo, ao = refs[nin:nin + no], refs[nin + no:]
        res = f(*[r[...] for r in rin], *[p[...] for p in pin])
        if not isinstance(res, (tuple, list)):
            res = (res,)
        for o_ref, v in zip(oo, res[:no]):
            o_ref[...] = v.astype(o_ref.dtype)
        i = pl.program_id(0)
        for a_ref, v in zip(ao, res[no:]):
            @pl.when(i == 0)
            def _():
                a_ref[...] = jnp.zeros_like(a_ref)

            a_ref[...] += v.reshape(a_ref.shape)

    res = pl.pallas_call(
        body,
        name=name,
        grid=(s // tm,),
        in_specs=in_specs,
        out_specs=out_specs,
        out_shape=out_shape,
        compiler_params=pltpu.CompilerParams(dimension_semantics=("arbitrary",), vmem_limit_bytes=VMEM_LIMIT),
    )(*[r[0] for r in rows], *params, *([] if dep is None else [dep]))
    return res


def _mmv(a, b, mode):
    ca = 0 if mode[0] == "t" else 1
    cb = 1 if mode[1] == "t" else 0
    return lax.dot_general(a.astype(BF16), b.astype(BF16), (((ca,), (cb,)), ((), ())), preferred_element_type=F32)


@functools.partial(jax.custom_vjp, nondiff_argnums=(2,))
def _bdot(a, b, mode):
    return _mmv(a, b, mode)


def _bdot_fwd(a, b, mode):
    return _mmv(a, b, mode), (a, b)


def _bdot_bwd(mode, saved, g):
    a, b = saved
    if mode == "nn":
        return _mmv(g, b, "nt"), _mmv(a, g, "tn")
    if mode == "nt":
        return _mmv(g, b, "nn"), _mmv(g, a, "tn")
    return _mmv(b, g, "nt"), _mmv(a, g, "nn")


_bdot.defvjp(_bdot_fwd, _bdot_bwd)


def _hdot(a, b, mode="nn", precision=HI):
    ca = 0 if mode[0] == "t" else 1
    cb = 1 if mode[1] == "t" else 0
    return lax.dot_general(a, b, (((ca,), (cb,)), ((), ())), precision=precision, preferred_element_type=F32)


def _segsum(x):
    c = x.shape[-1]
    blk = min(c, 256)
    r = lax.broadcasted_iota(jnp.int32, (blk, blk), 0) >> 6
    q = lax.broadcasted_iota(jnp.int32, (blk, blk), 1) >> 6
    ones = jnp.where(r == q, 1.0, 0.0).astype(F32)
    parts = [_hdot(x[:, i:i + blk], ones, precision=lax.Precision.HIGH) for i in range(0, c, blk)]
    return parts[0] if len(parts) == 1 else jnp.concatenate(parts, axis=1)


def _sigmoid(x):
    return jax.nn.sigmoid(x)


def _softplus(x):
    return jnp.maximum(x, 0.0) + jnp.log(1.0 + jnp.exp(-jnp.abs(x)))


def _rms(x, gain):
    return x * lax.rsqrt(jnp.mean(x * x, axis=-1, keepdims=True) + RMS_EPS) * gain


def _swiglu_act(gate, up):
    return gate * _sigmoid(gate) * up


def _rwkv_pre(zs, w0, w2, a0, a2, g2, k_k, k_a):
    r, k, v = zs[:, 0:512], zs[:, 512:1024], zs[:, 1024:1536]
    lora = zs[:, 1536:1792]
    wd, ad, gd = lora[:, 0:64], lora[:, 64:128], lora[:, 128:256]
    w = -_softplus(-(w0 + _bdot(jnp.tanh(wd), w2, "nn"))) - 0.5
    a = _sigmoid(a0 + _bdot(ad, a2, "nn"))
    g = _bdot(_sigmoid(gd), g2, "nn")
    kk = k * k_k
    kk = kk * lax.rsqrt(jnp.maximum(_segsum(kk * kk), 1e-24))
    k2 = k * (1.0 + (a - 1.0) * k_a)
    return r, -jnp.exp(w), k2, v, -kk, kk * a, g


def _rwkv_post(y, r, k2, v, g, gn_w, gn_b, r_k):
    mean = _segsum(y) * (1.0 / HEAD)
    yc = y - mean
    var = _segsum(yc * yc) * (1.0 / HEAD)
    yn = yc * lax.rsqrt(var + GN_EPS) * gn_w + gn_b
    bonus = _segsum(r * k2 * r_k) * v
    return (yn + bonus) * g


def _swap_halves(x):
    lane = lax.broadcasted_iota(jnp.int32, x.shape, 1)
    return jnp.where((lane & 32) == 0, jnp.roll(x, -32, axis=1), jnp.roll(x, 32, axis=1))


def _norm_rope(x, gain, cos, sin):
    heads = x.shape[1] // HEAD
    def rep(t):
        return jnp.concatenate([t] * heads, axis=1)

    xn = x * lax.rsqrt(_segsum(x * x) * (1.0 / HEAD) + RMS_EPS) * rep(gain)
    return xn * rep(cos) + _swap_halves(xn) * rep(sin)


def _attn_combine(o0, o1, o2, l0, l1, l2):
    m = jnp.maximum(jnp.maximum(l0, l1), l2)
    e0, e1, e2 = jnp.exp(l0 - m), jnp.exp(l1 - m), jnp.exp(l2 - m)
    return (e0 * o0 + e1 * o1 + e2 * o2) / (e0 + e1 + e2)


def _merge(zgr, zga, br, ba):
    return _sigmoid(zgr) * br + _sigmoid(zga) * ba


def _attn_block(q, kp, kc, vp, vc, has_prev):
    iq = lax.broadcasted_iota(jnp.int32, (1, BAND, BAND), 1)
    ik = lax.broadcasted_iota(jnp.int32, (1, BAND, BAND), 2)
    s_c = jnp.where(iq >= ik, _bdotb(q, kc, "nt") * (HEAD ** -0.5), NEG_INF)
    s_p = jnp.where(jnp.logical_and(iq <= ik, has_prev), _bdotb(q, kp, "nt") * (HEAD ** -0.5), NEG_INF)
    m = lax.stop_gradient(jnp.maximum(jnp.max(s_c, axis=-1, keepdims=True), jnp.max(s_p, axis=-1, keepdims=True)))
    e_c, e_p = jnp.exp(s_c - m), jnp.exp(s_p - m)
    l = jnp.sum(e_c, axis=-1, keepdims=True) + jnp.sum(e_p, axis=-1, keepdims=True)
    o = (_bdotb(e_c, vc) + _bdotb(e_p, vp)) / l
    return o, jnp.broadcast_to(m + jnp.log(l), o.shape)


def _mmb(a, b, cb):
    return lax.dot_general(a.astype(BF16), b.astype(BF16), (((2,), (cb,)), ((0,), (0,))), preferred_element_type=F32)


@functools.partial(jax.custom_vjp, nondiff_argnums=(2,))
def _bdotb1(a, b, cb):
    return _mmb(a, b, cb)


def _bdotb1_fwd(a, b, cb):
    return _mmb(a, b, cb), (a, b)


def _bdotb1_bwd(cb, saved, g):
    a, b = saved
    if cb == 1:
        return _mmb(g, b, 2), _mmb(jnp.swapaxes(a, 1, 2), g, 1)
    return _mmb(g, b, 1), _mmb(jnp.swapaxes(g, 1, 2), a, 1)


_bdotb1.defvjp(_bdotb1_fwd, _bdotb1_bwd)


def _bdotb(a, b, mode="nn", precision=None):
    if mode[0] == "t":
        a = jnp.swapaxes(a, 1, 2)
    cb = 2 if mode[1] == "t" else 1
    if precision is None:
        return _bdotb1(a, b, cb)
    return lax.dot_general(a, b, (((2,), (cb,)), ((0,), (0,))), precision=precision, preferred_element_type=F32)


def _tri_inv_levels(a):
    t = a.shape[-1]
    row = lax.broadcasted_iota(jnp.int32, (1, t, t), 1)
    col = lax.broadcasted_iota(jnp.int32, (1, t, t), 2)
    x = jnp.where(row == col, 1.0, 0.0).astype(F32) + jnp.where(jnp.logical_and(row == col + 1, (row & 1) == 1), a, 0.0)
    sh = 1
    while (1 << sh) < t:
        m = jnp.logical_and((row >> sh) == (col >> sh) + 1, (row >> (sh + 1)) == (col >> (sh + 1)))
        x = x + _bdotb(_bdotb(x, jnp.where(m, a, 0.0), precision=lax.Precision.HIGH), x, precision=lax.Precision.HIGH)
        sh += 1
    return x


@jax.custom_vjp
def _tri_inv(a):
    return _tri_inv_levels(a)


def _tri_inv_fwd(a):
    x = _tri_inv_levels(a)
    return x, x


def _tri_inv_bwd(x, g):
    xt = jnp.swapaxes(x, 1, 2)
    return (_bdotb(_bdotb(xt, g, precision=lax.Precision.HIGH), xt, precision=lax.Precision.HIGH),)


_tri_inv.defvjp(_tri_inv_fwd, _tri_inv_bwd)


@jax.custom_vjp
def _known_inv(a, x):
    return x


def _known_inv_fwd(a, x):
    return x, x


def _known_inv_bwd(x, g):
    return _tri_inv_bwd(x, g)[0], jnp.zeros_like(x)


_known_inv.defvjp(_known_inv_fwd, _known_inv_bwd)


def _wkv_chunk(s0, r, lw, k, v, a, b, inv=None, with_inv=False):
    nh, t, _ = r.shape
    row = lax.broadcasted_iota(jnp.int32, (1, t, t), 1)
    col = lax.broadcasted_iota(jnp.int32, (1, t, t), 2)
    incl, strict = row >= col, row > col
    ones = jnp.broadcast_to(jnp.where(incl, 1.0, 0.0).astype(F32), (nh, t, t))
    cum = _bdotb(ones, lw, precision=HI)
    c_end = cum[:, t - 1:t, :]
    e_in, e_ex, e_inv = jnp.exp(cum), jnp.exp(cum - lw), jnp.exp(-cum)
    at, rt, bt, kt = a * e_ex, r * e_in, b * e_inv, k * e_inv
    a_ab = jnp.where(strict, _bdotb(at, bt, "nt"), 0.0)
    a_ak = jnp.where(strict, _bdotb(at, kt, "nt"), 0.0)
    x = _tri_inv(a_ab) if inv is None else _known_inv(a_ab, inv)
    u = _bdotb(x, _bdotb(at, s0, "nt") + _bdotb(a_ak, v))
    y = (_bdotb(rt, s0, "nt") + _bdotb(jnp.where(incl, _bdotb(rt, bt, "nt"), 0.0), u)
         + _bdotb(jnp.where(incl, _bdotb(rt, kt, "nt"), 0.0), v))
    w_end = jnp.exp(c_end - cum)
    s1 = s0 * jnp.exp(c_end) + _bdotb(u, b * w_end, "tn") + _bdotb(v, k * w_end, "tn")
    return (y, s1, x) if with_inv else (y, s1)


def _shift_fwd(z, mu):
    s, c = z.shape
    tc = 256

    def body(z_ref, mu_ref, o_ref):
        zz = z_ref[...]
        row = lax.broadcasted_iota(jnp.int32, zz.shape, 0)
        prev = jnp.where(row == 0, 0.0, pltpu.roll(zz, 1, 0))
        o_ref[...] = zz + (prev - zz) * mu_ref[...]

    return pl.pallas_call(
        body, name="shift_fwd", grid=(c // tc,),
        in_specs=[pl.BlockSpec((s, tc), lambda j: (0, j)), pl.BlockSpec((1, tc), lambda j: (0, j))],
        out_specs=pl.BlockSpec((s, tc), lambda j: (0, j)), out_shape=SDS((s, c), F32),
        compiler_params=pltpu.CompilerParams(dimension_semantics=("parallel",), vmem_limit_bytes=VMEM_LIMIT),
    )(z, mu)


def _shift_bwd(z, mu, dzs):
    s, c = z.shape
    tc = 256

    def body(z_ref, mu_ref, d_ref, dz_ref, dmu_ref):
        zz, d, m = z_ref[...], d_ref[...], mu_ref[...]
        row = lax.broadcasted_iota(jnp.int32, zz.shape, 0)
        prev = jnp.where(row == 0, 0.0, pltpu.roll(zz, 1, 0))
        t = d * m
        nxt = jnp.where(row == s - 1, 0.0, pltpu.roll(t, s - 1, 0))
        dz_ref[...] = (d - t + nxt).astype(dz_ref.dtype)
        dmu_ref[...] = jnp.sum(d * (prev - zz), axis=0, keepdims=True)

    return pl.pallas_call(
        body, name="shift_bwd", grid=(c // tc,),
        in_specs=[pl.BlockSpec((s, tc), lambda j: (0, j)), pl.BlockSpec((1, tc), lambda j: (0, j)),
                  pl.BlockSpec((s, tc), lambda j: (0, j))],
        out_specs=[pl.BlockSpec((s, tc), lambda j: (0, j)), pl.BlockSpec((1, tc), lambda j: (0, j))],
        out_shape=[SDS((s, c), BF16), SDS((1, c), F32)],
        compiler_params=pltpu.CompilerParams(dimension_semantics=("parallel",), vmem_limit_bytes=VMEM_LIMIT),
    )(z, mu, dzs)


def _heads(x, nh):
    return jnp.stack([x[:, h * HEAD:(h + 1) * HEAD] for h in range(nh)], axis=0)


def _unheads(x):
    return jnp.concatenate([x[h] for h in range(x.shape[0])], axis=1)


def _wkv_fwd(zs, lw, k2, na, b):
    s = lw.shape[0]
    t, hb = WKV_CHUNK, WKV_HEADS_PER_STEP
    w = hb * HEAD
    nc, ng = s // t, RWKV_HEADS // hb

    def body(r_ref, v_ref, lw_ref, k_ref, a_ref, b_ref, y_ref, s0_ref, x_ref, state):
        @pl.when(pl.program_id(1) == 0)
        def _():
            state[...] = jnp.zeros_like(state)

        s0 = state[...]
        s0_ref[0] = s0
        y, s1, x = _wkv_chunk(s0, *[_heads(t_ref[...], hb) for t_ref in (r_ref, lw_ref, k_ref, v_ref, a_ref, b_ref)], with_inv=True)
        y_ref[...] = _unheads(y)
        x_ref[0] = x
        state[...] = s1

    def col(off):
        return pl.BlockSpec((t, w), functools.partial(lambda g, i, off: (i, g + off), off=off))

    return pl.pallas_call(
        body, name="wkv_fwd", grid=(ng, nc),
        in_specs=[col(0), col(2 * ng), col(0), col(0), col(0), col(0)],
        out_specs=[col(0), pl.BlockSpec((1, hb, HEAD, HEAD), lambda g, i: (i, g, 0, 0)),
                   pl.BlockSpec((1, hb, t, t), lambda g, i: (i, g, 0, 0))],
        out_shape=[SDS((s, RWKV_DIM), F32), SDS((nc, RWKV_HEADS, HEAD, HEAD), F32), SDS((nc, RWKV_HEADS, t, t), F32)],
        scratch_shapes=[pltpu.VMEM((hb, HEAD, HEAD), F32)],
        compiler_params=pltpu.CompilerParams(dimension_semantics=("parallel", "arbitrary"), vmem_limit_bytes=VMEM_LIMIT),
    )(zs, zs, lw, k2, na, b)


def _wkv_bwd(zs, lw, k2, na, b, s0s, invs, dy):
    s = lw.shape[0]
    t, hb = WKV_CHUNK, WKV_HEADS_PER_STEP
    w = hb * HEAD
    nc, ng = s // t, RWKV_HEADS // hb

    def body(r_ref, v_ref, lw_ref, k_ref, a_ref, b_ref, s0_ref, x_ref, dy_ref, dr_ref, dlw_ref, dk_ref, dv_ref, da_ref, db_ref, dstate):
        @pl.when(pl.program_id(1) == 0)
        def _():
            dstate[...] = jnp.zeros_like(dstate)

        _, vjp = jax.vjp(functools.partial(_wkv_chunk, inv=x_ref[0]), s0_ref[0],
                         *[_heads(t_ref[...], hb) for t_ref in (r_ref, lw_ref, k_ref, v_ref, a_ref, b_ref)])
        grads = vjp((_heads(dy_ref[...], hb), dstate[...]))
        dstate[...] = grads[0]
        for o_ref, gval in zip((dr_ref, dlw_ref, dk_ref, dv_ref, da_ref, db_ref), grads[1:]):
            o_ref[...] = _unheads(gval)

    def col(off):
        return pl.BlockSpec((t, w), functools.partial(lambda g, i, off: (nc - 1 - i, g + off), off=off))

    return pl.pallas_call(
        body, name="wkv_bwd", grid=(ng, nc),
        in_specs=[col(0), col(2 * ng), col(0), col(0), col(0), col(0),
                  pl.BlockSpec((1, hb, HEAD, HEAD), lambda g, i: (nc - 1 - i, g, 0, 0)),
                  pl.BlockSpec((1, hb, t, t), lambda g, i: (nc - 1 - i, g, 0, 0)), col(0)],
        out_specs=[col(0)] * 6,
        out_shape=[SDS((s, RWKV_DIM), F32)] * 6,
        scratch_shapes=[pltpu.VMEM((hb, HEAD, HEAD), F32)],
        compiler_params=pltpu.CompilerParams(dimension_semantics=("parallel", "arbitrary"), vmem_limit_bytes=VMEM_LIMIT),
    )(zs, zs, lw, k2, na, b, s0s, invs, dy)


def _attn_fwd(q, k, v, d):
    s = q.shape[0]
    l = s // d
    nb = l // BAND
    assert nb * BAND == l
    qv, kv, vv = (t.reshape(l, d * GROUP_DIM) for t in (q, k, v))
    nh = GROUP_DIM // HEAD

    def body(q_ref, kp_ref, kc_ref, vp_ref, vc_ref, o_ref, l_ref):
        has_prev = pl.program_id(1) > 0
        o, lse = _attn_block(*[_heads(t_ref[...].astype(F32), nh) for t_ref in (q_ref, kp_ref, kc_ref, vp_ref, vc_ref)], has_prev)
        o_ref[...] = _unheads(o)
        l_ref[...] = _unheads(lse)

    cur = pl.BlockSpec((BAND, GROUP_DIM), lambda rho, i: (i, rho))
    prev = pl.BlockSpec((BAND, GROUP_DIM), lambda rho, i: (jnp.maximum(i - 1, 0), rho))
    o, lse = pl.pallas_call(
        body, name=f"attn_fwd_d{d}", grid=(d, nb),
        in_specs=[cur, prev, cur, prev, cur], out_specs=[cur, cur],
        out_shape=[SDS((l, d * GROUP_DIM), F32), SDS((l, d * GROUP_DIM), F32)],
        compiler_params=pltpu.CompilerParams(dimension_semantics=("parallel", "arbitrary"), vmem_limit_bytes=VMEM_LIMIT),
    )(qv, kv, kv, vv, vv)
    return o.reshape(s, GROUP_DIM), lse.reshape(s, GROUP_DIM)


def _attn_bwd(q, k, v, d, do, dlse):
    s = q.shape[0]
    l = s // d
    nb = l // BAND
    qv, kv, vv, dov, dlv = (t.reshape(l, d * GROUP_DIM) for t in (q, k, v, do, dlse))
    nh = GROUP_DIM // HEAD

    def body(q_ref, kp_ref, kc_ref, vp_ref, vc_ref, do_ref, dl_ref, dq_ref, dk_ref, dv_ref, ck, cv):
        step = pl.program_id(1)
        has_prev = step < nb - 1

        @pl.when(step == 0)
        def _():
            ck[...] = jnp.zeros_like(ck)
            cv[...] = jnp.zeros_like(cv)

        _, vjp = jax.vjp(functools.partial(_attn_block, has_prev=has_prev),
                         *[_heads(t_ref[...].astype(F32), nh) for t_ref in (q_ref, kp_ref, kc_ref, vp_ref, vc_ref)])
        dq, dkp, dkc, dvp, dvc = vjp((_heads(do_ref[...], nh), _heads(dl_ref[...], nh)))
        dq_ref[...] = _unheads(dq)
        dk_ref[...] = _unheads(dkc) + ck[...]
        dv_ref[...] = _unheads(dvc) + cv[...]
        ck[...] = _unheads(dkp)
        cv[...] = _unheads(dvp)

    cur = pl.BlockSpec((BAND, GROUP_DIM), lambda rho, i: (nb - 1 - i, rho))
    prev = pl.BlockSpec((BAND, GROUP_DIM), lambda rho, i: (jnp.maximum(nb - 2 - i, 0), rho))
    dq, dk, dv = pl.pallas_call(
        body, name=f"attn_bwd_d{d}", grid=(d, nb),
        in_specs=[cur, prev, cur, prev, cur, cur, cur], out_specs=[cur] * 3,
        out_shape=[SDS((l, d * GROUP_DIM), F32)] * 3,
        scratch_shapes=[pltpu.VMEM((BAND, GROUP_DIM), F32), pltpu.VMEM((BAND, GROUP_DIM), F32)],
        compiler_params=pltpu.CompilerParams(dimension_semantics=("parallel", "arbitrary"), vmem_limit_bytes=VMEM_LIMIT),
    )(qv, kv, kv, vv, vv, dov, dlv)
    return dq.reshape(s, GROUP_DIM), dk.reshape(s, GROUP_DIM), dv.reshape(s, GROUP_DIM)


def _coords():
    return lax.axis_index("x"), lax.axis_index("y"), lax.axis_index("c")


_CHIP_FLIPS = ((1, 0), (0, 1), (1, 1))


def _flip(v, f):
    return 1 - v if f else v


def _form(kind, r, c):
    return (N_CHIPS, r, c) if kind == "blk" else (r, N_CHIPS * c)


def _slot(ref, kind, j, rows, c):
    if kind == "blk":
        return ref.at[j] if rows is None else ref.at[j, rows]
    cols = pl.ds(pl.multiple_of(j * c, 128), c)
    return ref.at[:, cols] if rows is None else ref.at[rows, cols]


def _half(r, which, align):
    return pl.ds(pl.multiple_of(which * (r // 2), align), r // 2)


def _rcopy(src, dst, send_sems, recv_sems, kk, dev):
    return pltpu.make_async_remote_copy(src_ref=src, dst_ref=dst, send_sem=send_sems.at[kk], recv_sem=recv_sems.at[kk],
                                        device_id=dev, device_id_type=MESH)


def _gather_plan(specs, step):
    def copies(refs, ss, rs, received):
        x, y, c = _coords()
        out = []
        for w, (kind, r, cc) in enumerate(specs):
            mine, other = _half(r, c, 16), _half(r, 1 - c, 16)
            for kk, (fx, fy) in enumerate(_CHIP_FLIPS):
                px, py = _flip(x, fx), _flip(y, fy)
                if step == "ici":
                    sl = _slot(refs[w], kind, 2 * px + py if received else 2 * x + y, mine, cc)
                    dev = (px, py, c)
                else:
                    sl = _slot(refs[w], kind, 2 * px + py, other if received else mine, cc)
                    dev = (x, y, 1 - c)
                out.append(_rcopy(sl, sl, ss, rs, 3 * w + kk, dev))
        return out

    def issue(refs, ss, rs):
        return copies(refs, ss, rs, False)

    def expect(refs, ss, rs):
        return copies(refs, ss, rs, False), copies(refs, ss, rs, True)

    return issue, expect


_HBM = pl.BlockSpec(memory_space=pltpu.HBM)
_SEM = pl.BlockSpec(memory_space=pltpu.SEMAPHORE)
_EFFECT = pltpu.SideEffectType.DATAFLOW_SIDE_EFFECTING


def _copies_start(name, bufs, n_sems, issue, after=None):
    nb = len(bufs)
    extra = [] if after is None else [after]

    def body(*refs):
        send_sems, recv_sems = refs[nb + len(extra)], refs[nb + len(extra) + 1]
        for cp in issue(refs[:nb], send_sems, recv_sems):
            cp.start()
        refs[-1][...] = jnp.zeros_like(refs[-1])

    outs = pl.pallas_call(
        body, name=name,
        out_shape=(pltpu.SemaphoreType.DMA((n_sems,)), pltpu.SemaphoreType.DMA((n_sems,)),
                   *[pltpu.HBM(b.shape, b.dtype) for b in bufs], SDS((8, 128), F32)),
        in_specs=[_HBM] * nb + [pl.BlockSpec(memory_space=pl.ANY)] * len(extra),
        out_specs=(_SEM, _SEM, *[_HBM] * nb, pl.BlockSpec(memory_space=pltpu.VMEM)),
        input_output_aliases={i: 2 + i for i in range(nb)},
        compiler_params=pltpu.CompilerParams(has_side_effects=_EFFECT),
    )(*[pltpu.with_memory_space_constraint(b, pltpu.HBM) for b in bufs], *extra)
    return outs[0], outs[1], list(outs[2:2 + nb]), outs[-1]


def _copies_wait(name, bufs, send_sems, recv_sems, after, expect):
    nb = len(bufs)

    def body(*refs):
        sent, received = expect(refs[:nb], refs[nb], refs[nb + 1])
        for cp in sent:
            cp.wait_send()
        for cp in received:
            cp.wait_recv()

    outs = pl.pallas_call(
        body, name=name,
        out_shape=tuple(pltpu.HBM(b.shape, b.dtype) for b in bufs),
        in_specs=(*[_HBM] * nb, _SEM, _SEM, pl.BlockSpec(memory_space=pl.ANY)), out_specs=tuple([_HBM] * nb),
        input_output_aliases={i: i for i in range(nb)},
        compiler_params=pltpu.CompilerParams(has_side_effects=_EFFECT),
    )(*bufs, send_sems, recv_sems, after)
    return list(outs)


def _add_pair(g, recv, kind, r, c, c_arr, name):
    h = r // 2
    if kind == "blk":
        tr = _row_tile(h, 512)
        grid = (N_CHIPS, h // tr)
        g_spec = pl.BlockSpec((1, 1, tr, c), lambda j, i, c_ref: (j, c_ref[0], i, 0))
        o_spec = pl.BlockSpec((1, tr, c), lambda j, i, c_ref: (j, i, 0))
        gv, oshape = g.reshape(N_CHIPS, 2, h, c), (N_CHIPS, h, c)
    else:
        tr = _row_tile(h, 64)
        grid = (h // tr,)
        g_spec = pl.BlockSpec((1, tr, N_CHIPS * c), lambda i, c_ref: (c_ref[0], i, 0))
        o_spec = pl.BlockSpec((tr, N_CHIPS * c), lambda i, c_ref: (i, 0))
        gv, oshape = g.reshape(2, h, N_CHIPS * c), (h, N_CHIPS * c)

    def body(c_ref, g_ref, r_ref, o_ref, ob_ref):
        v = (g_ref[:, 0] if kind == "blk" else g_ref[0]) + r_ref[...]
        o_ref[...] = v
        ob_ref[...] = v.astype(BF16)

    return pl.pallas_call(
        body, name=name,
        grid_spec=pltpu.PrefetchScalarGridSpec(num_scalar_prefetch=1, grid=grid, in_specs=[g_spec, o_spec], out_specs=[o_spec] * 2),
        out_shape=[SDS(oshape, F32), SDS(oshape, BF16)],
        compiler_params=pltpu.CompilerParams(vmem_limit_bytes=VMEM_LIMIT),
    )(c_arr, gv, recv)


def _sum_adamw(pair, recv, w, m, v, kind, r, c, mc_arr, name):
    h = r // 2
    tr = _row_tile(h, 256)
    nt = h // tr
    if kind == "blk":
        p_spec = pl.BlockSpec((1, tr, c), lambda i, mc: (mc[0], i, 0))
    else:
        p_spec = pl.BlockSpec((tr, c), lambda i, mc: (i, mc[0]))
    mine = pl.BlockSpec((tr, c), lambda i, mc: (mc[1] * nt + i, 0))

    def body(mc, a_ref, r_ref, w_ref, m_ref, v_ref, g_out, d_out, m_out, v_out):
        own = a_ref[0] if kind == "blk" else a_ref[...]
        g = ((own + r_ref[0].astype(F32)) + r_ref[1].astype(F32)) + r_ref[2].astype(F32)
        g_out[...] = g
        d_out[...], m_out[...], v_out[...] = _adamw_rows(w_ref[...], g, m_ref[...], v_ref[...])

    return pl.pallas_call(
        body, name=name,
        grid_spec=pltpu.PrefetchScalarGridSpec(
            num_scalar_prefetch=1, grid=(nt,),
            in_specs=[p_spec, pl.BlockSpec((3, tr, c), lambda i, mc: (0, i, 0)), mine, mine, mine], out_specs=[mine] * 4),
        out_shape=[SDS((r, c), F32)] * 4,
        compiler_params=pltpu.CompilerParams(vmem_limit_bytes=VMEM_LIMIT),
    )(mc_arr, pair, recv, w, m, v)


class _GroupReduce:
    def __init__(self, tag, specs, c_arr, mc_arr):
        self.tag, self.specs, self.c_arr, self.mc_arr = tag, specs, c_arr, mc_arr
        self.n = len(specs)

    def _plan(self, step):
        specs, n = self.specs, self.n

        def copies(refs, ss, rs, received):
            x, y, c = _coords()
            sib, out = (x, y, 1 - c), []
            for w, (_, kind, r, cc) in enumerate(specs):
                if step == "join":
                    for q in range(4):
                        there = refs[4 * w + q].at[_half(r, 1 - c if received else c, 8)]
                        out.append(_rcopy(there, there, ss, rs, 4 * w + q, sib))
                    continue
                src, land = refs[w], refs[n + w]
                if step == "swap":
                    rows = _half(r, 1 - c, 8)
                    part = src.at[:, rows] if kind == "blk" else src.at[rows]
                    out.append(_rcopy(land if received else part, land, ss, rs, w, sib))
                else:
                    for kk, (fx, fy) in enumerate(_CHIP_FLIPS):
                        px, py = _flip(x, fx), _flip(y, fy)
                        part = land.at[kk] if received else _slot(src, kind, 2 * px + py, None, cc)
                        out.append(_rcopy(part, land.at[kk], ss, rs, 3 * w + kk, (px, py, c)))
            return out

        def issue(refs, ss, rs):
            return copies(refs, ss, rs, False)

        def expect(refs, ss, rs):
            return copies(refs, ss, rs, False), copies(refs, ss, rs, True)

        return issue, expect

    def swap_start(self, grads, after=None):
        lands = [lax.empty(_form(kind, r // 2, c), F32) for _, kind, r, c in self.specs]
        ss, rs, bufs, tok = _copies_start(f"rs_{self.tag}_swap", list(grads) + lands, self.n, self._plan("swap")[0], after=after)
        self.state = (ss, rs, bufs)
        return tok

    def swap_wait_ici_start(self, after):
        ss, rs, bufs = self.state
        bufs = _copies_wait(f"rs_{self.tag}_swap_wait", bufs, ss, rs, after, self._plan("swap")[1])
        pairs = [_add_pair(bufs[w], bufs[self.n + w], kind, r, c, self.c_arr, name=f"rs_{self.tag}_pair_{nm}")
                 for w, (nm, kind, r, c) in enumerate(self.specs)]
        self.pair = [pr[0] for pr in pairs]
        lands = [lax.empty((3, r // 2, c), BF16) for _, _, r, c in self.specs]
        ss, rs, bufs, tok = _copies_start(f"rs_{self.tag}_ici", [pr[1] for pr in pairs] + lands, 3 * self.n, self._plan("ici")[0])
        self.state = (ss, rs, bufs)
        return tok

    def ici_wait_join_start(self, after, state):
        ss, rs, bufs = self.state
        bufs = _copies_wait(f"rs_{self.tag}_ici_wait", bufs, ss, rs, after, self._plan("ici")[1])
        outs = []
        for w, (nm, kind, r, c) in enumerate(self.specs):
            outs += _sum_adamw(self.pair[w], bufs[self.n + w], *state[nm], kind, r, c, self.mc_arr, name=f"rs_{self.tag}_adamw_{nm}")
        ss, rs, bufs, tok = _copies_start(f"rs_{self.tag}_join", outs, 4 * self.n, self._plan("join")[0])
        self.state = (ss, rs, bufs)
        return tok

    def join_wait(self, after):
        ss, rs, bufs = self.state
        bufs = _copies_wait(f"rs_{self.tag}_join_wait", bufs, ss, rs, after, self._plan("join")[1])
        return {nm: tuple(bufs[4 * w:4 * w + 4]) for w, (nm, _, _, _) in enumerate(self.specs)}


def _all_reduce_small(buf):
    rows, cols = buf.shape

    def body(x_ref, o_ref, gath, send_sems, recv_sems):
        x, y, c = _coords()
        me = 4 * x + 2 * y + c
        gath[me] = x_ref[...]
        sends = []
        for kk in range(1, 8):
            f = (kk >> 2) & 1, (kk >> 1) & 1, kk & 1
            px, py, pc = _flip(x, f[0]), _flip(y, f[1]), _flip(c, f[2])
            cp = pltpu.make_async_remote_copy(src_ref=x_ref, dst_ref=gath.at[me], send_sem=send_sems.at[kk - 1],
                                              recv_sem=recv_sems.at[kk - 1], device_id=(px, py, pc), device_id_type=MESH)
            cp.start()
            sends.append(cp)
        for kk in range(1, 8):
            f = (kk >> 2) & 1, (kk >> 1) & 1, kk & 1
            px, py, pc = _flip(x, f[0]), _flip(y, f[1]), _flip(c, f[2])
            there = gath.at[4 * px + 2 * py + pc]
            pltpu.make_async_remote_copy(src_ref=there, dst_ref=there, send_sem=send_sems.at[kk - 1],
                                         recv_sem=recv_sems.at[kk - 1], device_id=(px, py, pc), device_id_type=MESH).wait_recv()
        for cp in sends:
            cp.wait_send()
        acc = gath[0]
        for j in range(1, 8):
            acc = acc + gath[j]
        o_ref[...] = acc

    return pl.pallas_call(
        body, name="all_reduce_small",
        in_specs=[pl.BlockSpec(memory_space=pltpu.VMEM)], out_specs=pl.BlockSpec(memory_space=pltpu.VMEM),
        out_shape=SDS((rows, cols), F32),
        scratch_shapes=[pltpu.VMEM((8, rows, cols), F32), pltpu.SemaphoreType.DMA((7,)), pltpu.SemaphoreType.DMA((7,))],
    )(buf)


def _adamw_rows(w, g, m, v):
    m = ADAM_B1 * m + (1.0 - ADAM_B1) * g
    v = ADAM_B2 * v + (1.0 - ADAM_B2) * jnp.square(g)
    m_hat = m / (1.0 - ADAM_B1 ** ADAM_STEP)
    v_hat = v / (1.0 - ADAM_B2 ** ADAM_STEP)
    return -ADAM_LR * (m_hat / (jnp.sqrt(v_hat) + ADAM_EPS) + ADAM_WD * w), m, v


def _adamw(w, g, m, v, name, dep=None):
    rows, cols = w.shape
    tm = _pick(rows, (256, 128, 64, 16, 8))
    return _rows_call(_adamw_rows, [(t, 0, cols) for t in (w, g, m, v)], [], [(cols, F32)] * 3, tm=tm, name=name, dep=dep)


def _pack_small(parts):
    flat = jnp.concatenate([parts[n].reshape(-1) for n, _ in SMALL])
    return jnp.pad(flat, (0, SMALL_ROWS * PACK_COLS - flat.shape[0])).reshape(SMALL_ROWS, PACK_COLS)


def _unpack_small(buf, shapes):
    flat, out, off = buf.reshape(-1), {}, 0
    for n, sz in SMALL:
        out[n] = flat[off:off + sz].reshape(shapes[n])
        off += sz
    return out


def _lora_stack(parts):
    return jnp.concatenate([parts[n] for n, _ in LORA], axis=-2)


def _lora_split(stacked):
    out, off = {}, 0
    for n, rows in LORA:
        out[n] = stacked[..., off:off + rows, :]
        off += rows
    return out


def _ffn_gate_up(h, wgt, wut, name, dep=None):
    s, d = h.shape
    nblk, f, _ = wgt.shape
    tm = _pick(s, (1024, 512, 256))
    dn = (((1,), (1,)), ((), ()))

    def body(h_ref, wg_ref, wu_ref, *rest):
        g_ref, u_ref, a_ref = rest[-3:]
        hh = h_ref[...]
        g = lax.dot_general(hh, wg_ref[0], dn, preferred_element_type=F32)
        u = lax.dot_general(hh, wu_ref[0], dn, preferred_element_type=F32)
        g_ref[0], u_ref[0] = g, u
        a_ref[0] = _swiglu_act(g, u).astype(BF16)

    w_spec = pl.BlockSpec((1, f, d), lambda j, i: (j, 0, 0))
    o_spec = pl.BlockSpec((1, tm, f), lambda j, i: (j, i, 0))
    extra = [] if dep is None else [dep]
    return pl.pallas_call(
        body, name=name, grid=(nblk, s // tm),
        in_specs=[pl.BlockSpec((tm, d), lambda j, i: (i, 0)), w_spec, w_spec] + [pl.BlockSpec(memory_space=pl.ANY)] * len(extra),
        out_specs=[o_spec] * 3,
        out_shape=[SDS((nblk, s, f), F32), SDS((nblk, s, f), F32), SDS((nblk, s, f), BF16)],
        compiler_params=pltpu.CompilerParams(dimension_semantics=("parallel", "parallel"), vmem_limit_bytes=VMEM_LIMIT),
    )(h, wgt, wut, *extra)


def _ffn_down_dx(dx_bf, wd, gate, up, name, dep=None):
    s, d = dx_bf.shape
    nblk, f, _ = wd.shape
    tm = _pick(s, (1024, 512, 256))
    dn = (((1,), (1,)), ((), ()))

    def body(dx_ref, wd_ref, g_ref, u_ref, *rest):
        dg_ref, du_ref = rest[-2:]
        dact = 0.5 * lax.dot_general(dx_ref[...], wd_ref[0], dn, preferred_element_type=F32)
        _, vjp = jax.vjp(_swiglu_act, g_ref[0], u_ref[0])
        dg, du = vjp(dact)
        dg_ref[0], du_ref[0] = dg.astype(BF16), du.astype(BF16)

    o_spec = pl.BlockSpec((1, tm, f), lambda j, i: (j, i, 0))
    extra = [] if dep is None else [dep]
    return pl.pallas_call(
        body, name=name, grid=(nblk, s // tm),
        in_specs=[pl.BlockSpec((tm, d), lambda j, i: (i, 0)), pl.BlockSpec((1, f, d), lambda j, i: (j, 0, 0)), o_spec, o_spec]
        + [pl.BlockSpec(memory_space=pl.ANY)] * len(extra),
        out_specs=[o_spec] * 2, out_shape=[SDS((nblk, s, f), BF16)] * 2,
        compiler_params=pltpu.CompilerParams(dimension_semantics=("parallel", "parallel"), vmem_limit_bytes=VMEM_LIMIT),
    )(dx_bf, wd, gate, up, *extra)


def _ffn_fwd(x, gain, wgt, wut, wd, tag, h=None, dep=None):
    if h is None:
        h = _rows_call(_rms, [(x, 0, D_MODEL)], [gain], [(D_MODEL, BF16)], tm=512, name=f"{tag}_norm")[0]
    gate, up, act = _ffn_gate_up(h, wgt, wut, f"{tag}_gate_up", dep=dep)
    x_new = _mm(act, wd, sum_blocks=True, res=x, alpha=0.5, name=f"{tag}_down")
    return x_new, (x, h, gate, up, act)


def _ffn_bwd(dx_new, dx_new_bf, saved, gain, wgt, wut, wd, tag, dep=None, hooks=None):
    x, h, gate, up, act = saved
    hooks = hooks or {}

    def hook(name, *vals):
        return hooks[name](*vals) if name in hooks else None

    d_wd = _mm(act, dx_new_bf, ta=True, alpha=0.5, name=f"{tag}_down_dw")
    dep = hook("down", d_wd) if "down" in hooks else dep
    dgate, dup = _ffn_down_dx(dx_new_bf, wd, gate, up, f"{tag}_down_dx", dep=dep)
    d_wgt = _mm(dgate, h, ta=True, dep=hook("mid", dgate), name=f"{tag}_gate_dw")
    d_wut = _mm(dup, h, ta=True, name=f"{tag}_up_dw")
    dh = _mm(dgate, wgt, sum_blocks=True, dep=hook("dw", d_wgt, d_wut), name=f"{tag}_gate_dx")
    dx, dx_bf, dgain = _mm(dup, wut, sum_blocks=True, res=dh, dep=hook("dx", dh), post=_norm_bwd_post(x, gain, dx_new),
                           name=f"{tag}_up_dx")
    hook("end", dx_bf)
    return dx, dx_bf, dgain, d_wgt, d_wut, d_wd


def _norm_bwd_post(x, gain, dres):
    def f(dht, xt, drt, gt):
        _, vjp = jax.vjp(_rms, xt, gt)
        dxt, dgt = vjp(dht)
        return dxt + drt, dxt + drt, dgt

    return f, [x, dres], [gain], [F32, BF16], [(1, D_MODEL)]


def kernel(x, p, positions, ffn1_norm, ffn1_w_gate, ffn1_w_up, ffn1_w_down, mix_norm, w_in, rwkv_mu, rwkv_w0, rwkv_w2, rwkv_a0, rwkv_a2, rwkv_g2, rwkv_k_k, rwkv_k_a, rwkv_r_k, rwkv_gn_w, rwkv_gn_b, q_norm, k_norm, w_br_rwkv, w_br_attn, w_out, ffn2_norm, ffn2_w_gate, ffn2_w_up, ffn2_w_down, ple_norm, ple_w_gate, ple_w_proj, loss_target, m_ffn1_norm, m_ffn1_w_gate, m_ffn1_w_up, m_ffn1_w_down, m_mix_norm, m_w_in, m_rwkv_mu, m_rwkv_w0, m_rwkv_w2, m_rwkv_a0, m_rwkv_a2, m_rwkv_g2, m_rwkv_k_k, m_rwkv_k_a, m_rwkv_r_k, m_rwkv_gn_w, m_rwkv_gn_b, m_q_norm, m_k_norm, m_w_br_rwkv, m_w_br_attn, m_w_out, m_ffn2_norm, m_ffn2_w_gate, m_ffn2_w_up, m_ffn2_w_down, m_ple_norm, m_ple_w_gate, m_ple_w_proj, v_ffn1_norm, v_ffn1_w_gate, v_ffn1_w_up, v_ffn1_w_down, v_mix_norm, v_w_in, v_rwkv_mu, v_rwkv_w0, v_rwkv_w2, v_rwkv_a0, v_rwkv_a2, v_rwkv_g2, v_rwkv_k_k, v_rwkv_k_a, v_rwkv_r_k, v_rwkv_gn_w, v_rwkv_gn_b, v_q_norm, v_k_norm, v_w_br_rwkv, v_w_br_attn, v_w_out, v_ffn2_norm, v_ffn2_w_gate, v_ffn2_w_up, v_ffn2_w_down, v_ple_norm, v_ple_w_gate, v_ple_w_proj):
    args = dict(locals())
    wts = {n: args[n] for n in WEIGHTS}
    mom_m = {n: args["m_" + n] for n in WEIGHTS}
    mom_v = {n: args["v_" + n] for n in WEIGHTS}
    x0, tgt = x[0], loss_target[0]
    s = x0.shape[0]
    p_tok = p[0, 0]

    vec = {n: wts[n].reshape(1, -1) for n, _ in SMALL}
    xi, yi, ci = _coords()
    me = 2 * xi + yi
    def laid(t, n):
        return jnp.transpose(t[n][0]) if n in TRANSPOSED else t[n][0]

    shard_of = {n: laid(wts, n) for g in GROUPS.values() for n, _, _, _ in g if n != "lora"}
    shard_of["lora"] = _lora_stack({n: wts[n][0] for n, _ in LORA})

    def whole_with_own(n, kind, r, c, tok=None):
        at = (me, 0, 0) if kind == "blk" else (0, me * c)
        own = (shard_of[n] if tok is None else shard_of[n] + tok[0, 0]).astype(BF16)
        return lax.dynamic_update_slice(lax.empty(_form(kind, r, c), BF16), own[None] if kind == "blk" else own, at)

    specs = {g: [(kind, r, c) for _, kind, r, c in grp] for g, grp in GROUPS.items()}
    plans = {(g, st): _gather_plan(specs[g], st) for g in GROUPS for st in ("ici", "d2d")}
    buf_f1 = [whole_with_own(*w) for w in GROUPS["f1"]]
    ss_0, rs_0, buf_f1, tok_0 = _copies_start("gather_f1_ici", buf_f1, 3 * len(buf_f1), plans["f1", "ici"][0])
    bufs = {g: [whole_with_own(*w, tok=tok_0) for w in GROUPS[g]] for g in ("mx", "f2")}
    buf_f1 = _copies_wait("gather_f1_ici_wait", buf_f1, ss_0, rs_0, bufs["mx"][0], plans["f1", "ici"][1])
    ss_1, rs_1, buf_f1, tok_1 = _copies_start("gather_f1_d2d", buf_f1, 3 * len(buf_f1), plans["f1", "d2d"][0])
    h1 = _rows_call(_rms, [(x0, 0, D_MODEL)], [vec["ffn1_norm"] + tok_1[0, 0]], [(D_MODEL, BF16)], tm=512, name="ffn1_norm")[0]
    buf_f1 = _copies_wait("gather_f1_d2d_wait", buf_f1, ss_1, rs_1, h1, plans["f1", "d2d"][1])
    wb = dict(zip([w[0] for w in GROUPS["f1"]], buf_f1))
    ss_a, rs_a, buf_mx, tok_a = _copies_start("gather_mx_ici", bufs["mx"], 3 * len(bufs["mx"]), plans["mx", "ici"][0],
                                              after=wb["ffn1_w_gate"])

    inv_freq = 1.0 / (ROPE_THETA ** (jnp.arange(0, HEAD, 2, dtype=F32) / HEAD))
    ang = positions[0].astype(F32)[:, None] * inv_freq
    cos, sin = jnp.cos(ang), jnp.sin(ang)
    cos2, sin2 = jnp.concatenate([cos, cos], axis=1), jnp.concatenate([-sin, sin], axis=1)

    x1, ffn1_saved = _ffn_fwd(x0, vec["ffn1_norm"], wb["ffn1_w_gate"], wb["ffn1_w_up"], wb["ffn1_w_down"], "ffn1", h=h1, dep=tok_a)
    buf_mx = _copies_wait("gather_mx_ici_wait", buf_mx, ss_a, rs_a, x1, plans["mx", "ici"][1])
    ss_b, rs_b, buf_mx, tok_b = _copies_start("gather_mx_d2d", buf_mx, 3 * len(buf_mx), plans["mx", "d2d"][0])
    ss_c, rs_c, buf_f2, tok_c = _copies_start("gather_f2_ici", bufs["f2"], 3 * len(bufs["f2"]), plans["f2", "ici"][0])
    h = _rows_call(_rms, [(x1, 0, D_MODEL)], [vec["mix_norm"] + (tok_b[0, 0] + tok_c[0, 0])], [(D_MODEL, BF16)], tm=256,
                   name="mix_norm")[0]
    buf_mx = _copies_wait("gather_mx_d2d_wait", buf_mx, ss_b, rs_b, h, plans["mx", "d2d"][1])
    wb.update(zip([w[0] for w in GROUPS["mx"]], buf_mx))
    w_in_all = wb["w_in"]
    w_in_r, w_in_a, w_in_g = w_in_all[:, :RWKV_COLS], w_in_all[:, RWKV_COLS:RWKV_COLS + ATTN_COLS], w_in_all[:, RWKV_COLS + ATTN_COLS:]
    lora = _lora_split(wb["lora"])
    w2, a2, g2 = lora["rwkv_w2"], lora["rwkv_a2"], lora["rwkv_g2"]
    z_r = _mm(h, w_in_r, name="in_rwkv")
    z_a = _mm(h, w_in_a, name="in_attn")
    z_g = _mm(h, w_in_g, name="in_gate")

    zs = _shift_fwd(z_r, vec["rwkv_mu"])
    pre_params = [vec["rwkv_w0"], w2, vec["rwkv_a0"], a2, g2, vec["rwkv_k_k"], vec["rwkv_k_a"]]
    def pre_fwd(*t):
        res = _rwkv_pre(*t)
        return res[1], res[2], res[4], res[5], res[6]

    lw, k2, na, kb, gate_r = _rows_call(pre_fwd, [(zs, 0, RWKV_COLS)], pre_params, [(RWKV_DIM, F32)] * 5, tm=512, name="rwkv_pre")
    y_scan, s0s, invs = _wkv_fwd(zs, lw, k2, na, kb)
    buf_f2 = _copies_wait("gather_f2_ici_wait", buf_f2, ss_c, rs_c, y_scan, plans["f2", "ici"][1])
    ss_d, rs_d, buf_f2, tok_d = _copies_start("gather_f2_d2d", buf_f2, 3 * len(buf_f2), plans["f2", "d2d"][0])
    post_params = [vec["rwkv_gn_w"] + tok_d[0, 0], vec["rwkv_gn_b"], vec["rwkv_r_k"]]
    post_rows = [(y_scan, 0, RWKV_DIM), (zs, 0, RWKV_DIM), (k2, 0, RWKV_DIM), (zs, 2, RWKV_DIM), (gate_r, 0, RWKV_DIM)]
    y_rwkv = _rows_call(_rwkv_post, post_rows, post_params, [(RWKV_DIM, BF16)], tm=512, name="rwkv_post")[0]
    buf_f2 = _copies_wait("gather_f2_d2d_wait", buf_f2, ss_d, rs_d, y_rwkv, plans["f2", "d2d"][1])
    wb.update(zip([w[0] for w in GROUPS["f2"]], buf_f2))
    w_brr, w_bra = wb["w_br_rwkv"], wb["w_br_attn"]
    w_o = wb["w_out"].reshape(D_MODEL, D_MODEL)
    w_pp, w_pg = wb["ple_w_proj"], wb["ple_w_gate"].reshape(D_MODEL, D_MODEL)

    def qk_fwd(qt, kt, ct, st, qg, kg):
        return _norm_rope(qt, qg, ct, st), _norm_rope(kt, kg, ct, st)

    qk_rows = [(z_a, 0, ATTN_DIM), (z_a, 1, ATTN_DIM), (cos2, 0, HEAD), (sin2, 0, HEAD)]
    q_rot, k_rot = _rows_call(qk_fwd, qk_rows, [vec["q_norm"], vec["k_norm"]], [(ATTN_DIM, BF16)] * 2, tm=512, name="attn_pre")
    def group(t, g, off=0):
        return t[:, off + g * GROUP_DIM:off + (g + 1) * GROUP_DIM].astype(BF16)

    qkv = [(group(q_rot, g), group(k_rot, g), group(z_a, g, 2 * ATTN_DIM)) for g in range(len(ATTN_DILATIONS))]
    outs, lses = zip(*[_attn_fwd(*qkv[g], d) for g, d in enumerate(ATTN_DILATIONS)])
    comb_rows = [(t, 0, GROUP_DIM) for t in outs + lses]
    y_attn = _rows_call(_attn_combine, comb_rows, [], [(GROUP_DIM, BF16)], tm=512, name="attn_combine")[0]

    br = _mm(y_rwkv, w_brr, name="branch_rwkv")
    ba = _mm(y_attn, w_bra, name="branch_attn")
    merge_rows = [(z_g, 0, D_MODEL), (z_g, 1, D_MODEL), (br, 0, D_MODEL), (ba, 0, D_MODEL)]
    merged = _rows_call(_merge, merge_rows, [], [(D_MODEL, BF16)], tm=512, name="merge")[0]
    x2 = _mm(merged, w_o, res=x1, name="out_proj")
    x3, ffn2_saved = _ffn_fwd(x2, vec["ffn2_norm"], wb["ffn2_w_gate"], wb["ffn2_w_up"], wb["ffn2_w_down"], "ffn2")
    hp = _rows_call(_rms, [(x3, 0, D_MODEL)], [vec["ple_norm"]], [(D_MODEL, BF16)], tm=512, name="ple_norm")[0]
    pg = _mm(hp, w_pg, name="ple_gate")
    pp = _mm(p_tok, w_pp, name="ple_proj")

    def head(x3t, pgt, ppt, tt):
        sg = _sigmoid(pgt)
        err = x3t + sg * ppt - tt
        dx4 = err * (1.0 / D_MODEL)
        loss = 0.5 * jnp.sum(jnp.mean(err * err, axis=-1, keepdims=True), axis=0, keepdims=True)
        return dx4, dx4 * ppt * sg * (1.0 - sg), dx4 * sg, jnp.broadcast_to(loss, (8, 128))

    head_rows = [(x3, 0, D_MODEL), (pg, 0, D_MODEL), (pp, 0, D_MODEL), (tgt, 0, D_MODEL)]
    dx4, dpg, dpp, loss_tile = _rows_call(head, head_rows, [], [(D_MODEL, F32), (D_MODEL, BF16), (D_MODEL, BF16)], [(8, 128)],
                                          tm=512, name="ple_loss")

    c_arr = jnp.reshape(ci, (1,)).astype(jnp.int32)
    mc_arr = jnp.stack([me, ci]).astype(jnp.int32)
    red = {g: _GroupReduce(g, grp, c_arr, mc_arr) for g, grp in REDUCE_GROUPS.items()}

    def adam_state(names):
        out = {}
        for n in names:
            if n == "lora":
                out[n] = tuple(_lora_stack({k: t[k][0] for k, _ in LORA}) for t in (wts, mom_m, mom_v))
            else:
                out[n] = (laid(wts, n), laid(mom_m, n), laid(mom_v, n))
        return out

    adam = {g: adam_state([w[0] for w in grp]) for g, grp in REDUCE_GROUPS.items()}
    done = {}
    gw, gs = {}, {}
    gw["ple_w_proj"] = _mm(p_tok, dpp, ta=True, name="ple_proj_dw")
    gw["ple_w_gate"] = _mm(hp, dpg, ta=True, name="ple_gate_dw")
    dx3, dx3_bf, gs["ple_norm"] = _mm(dpg, w_pg, tb=True, post=_norm_bwd_post(x3, vec["ple_norm"], dx4), name="ple_gate_dx")
    dx2, dx2_bf, gs["ffn2_norm"], gw["ffn2_w_gate"], gw["ffn2_w_up"], gw["ffn2_w_down"] = _ffn_bwd(
        dx3, dx3_bf, ffn2_saved, vec["ffn2_norm"], wb["ffn2_w_gate"], wb["ffn2_w_up"], wb["ffn2_w_down"], "ffn2")
    gw["ple_w_gate"] = gw["ple_w_gate"].reshape(N_CHIPS, D_MODEL // N_CHIPS, D_MODEL)
    tok = red["f2"].swap_start([gw[w[0]] for w in REDUCE_GROUPS["f2"]])
    gw["w_out"] = _mm(merged, dx2_bf, ta=True, name="out_proj_dw")
    dmerged = _mm(dx2_bf, w_o, tb=True, dep=tok, name="out_proj_dx")

    def merge_bwd(zgr, zga, brt, bat, ct):
        _, vjp = jax.vjp(_merge, zgr, zga, brt, bat)
        d1, d2, d3, d4 = vjp(ct)
        return jnp.concatenate([d1, d2], axis=1), d3, d4

    dz_g, dbr, dba = _rows_call(merge_bwd, merge_rows + [(dmerged, 0, D_MODEL)], [],
                                [(2 * D_MODEL, BF16), (D_MODEL, BF16), (D_MODEL, BF16)], tm=512, name="merge_bwd")
    tok = red["f2"].swap_wait_ici_start(dz_g)
    gw["w_br_rwkv"] = _mm(y_rwkv, dbr, ta=True, name="branch_rwkv_dw")
    gw["w_br_attn"] = _mm(y_attn, dba, ta=True, name="branch_attn_dw")
    dy_rwkv = _mm(dbr, w_brr, tb=True, dep=tok, name="branch_rwkv_dx")
    dy_attn = _mm(dba, w_bra, tb=True, dep=tok, name="branch_attn_dx")

    def comb_bwd(*t):
        _, vjp = jax.vjp(_attn_combine, *t[:6])
        return vjp(t[6])

    dcomb = _rows_call(comb_bwd, comb_rows + [(dy_attn, 0, GROUP_DIM)], [], [(GROUP_DIM, F32)] * 6, tm=512, name="attn_combine_bwd")
    dqs, dks, dvs = zip(*[_attn_bwd(*qkv[g], d, dcomb[g], dcomb[3 + g]) for g, d in enumerate(ATTN_DILATIONS)])

    def qk_bwd(qt, kt, ct, st, *rest):
        dq = jnp.concatenate(rest[0:3], axis=1)
        dk = jnp.concatenate(rest[3:6], axis=1)
        qg, kg = rest[9], rest[10]
        _, vjp = jax.vjp(lambda a_, b_, c_, d_: qk_fwd(a_, b_, ct, st, c_, d_), qt, kt, qg, kg)
        dqt, dkt, dqg, dkg = vjp((dq, dk))
        return jnp.concatenate((dqt, dkt) + tuple(rest[6:9]), axis=1), dqg, dkg

    dz_a, gs["q_norm"], gs["k_norm"] = _rows_call(
        qk_bwd, qk_rows + [(t, 0, GROUP_DIM) for t in dqs + dks + dvs], [vec["q_norm"], vec["k_norm"]],
        [(ATTN_COLS, BF16)], [(1, HEAD), (1, HEAD)], tm=512, name="attn_pre_bwd")
    tok = red["f2"].ici_wait_join_start(dz_a, adam["f2"])

    def post_bwd(*t):
        _, vjp = jax.vjp(_rwkv_post, *t[:5], *t[6:])
        return vjp(t[5])

    dy_scan, dr_post, dk2_post, dv_post, dgate_r, gs["rwkv_gn_w"], gs["rwkv_gn_b"], gs["rwkv_r_k"] = _rows_call(
        post_bwd, post_rows + [(dy_rwkv, 0, RWKV_DIM)], post_params, [(RWKV_DIM, F32)] * 5, [(1, RWKV_DIM)] * 3,
        tm=512, name="rwkv_post_bwd", dep=tok)
    done.update(red["f2"].join_wait(dy_scan))
    dr_s, dlw, dk2_s, dv_s, dna, dkb = _wkv_bwd(zs, lw, k2, na, kb, s0s, invs, dy_scan)

    def pre_bwd(zt, c_r1, c_r2, c_lw, c_k1, c_k2, c_v1, c_v2, c_a, c_b, c_g, *params):
        _, vjp = jax.vjp(_rwkv_pre, zt, *params)
        return vjp((c_r1 + c_r2, c_lw, c_k1 + c_k2, c_v1 + c_v2, c_a, c_b, c_g))

    pre_cts = [dr_s, dr_post, dlw, dk2_s, dk2_post, dv_s, dv_post, dna, dkb, dgate_r]
    dzs, gs["rwkv_w0"], g_w2, gs["rwkv_a0"], g_a2, g_g2, gs["rwkv_k_k"], gs["rwkv_k_a"] = _rows_call(
        pre_bwd, [(zs, 0, RWKV_COLS)] + [(t, 0, RWKV_DIM) for t in pre_cts], pre_params, [(RWKV_COLS, F32)],
        [q.shape for q in pre_params], tm=512, name="rwkv_pre_bwd")
    dz_r, gs["rwkv_mu"] = _shift_bwd(z_r, vec["rwkv_mu"], dzs)

    g_w_in = jnp.concatenate([_mm(h, dz_r, ta=True, name="in_rwkv_dw"), _mm(h, dz_a, ta=True, name="in_attn_dw"),
                              _mm(h, dz_g, ta=True, name="in_gate_dw")], axis=1)
    gw["w_in"], gw["lora"] = g_w_in, jnp.concatenate([g_w2, g_a2, g_g2], axis=0)
    gw["w_out"] = gw["w_out"].reshape(N_CHIPS, D_MODEL // N_CHIPS, D_MODEL)
    tok = red["mx"].swap_start([gw[w[0]] for w in REDUCE_GROUPS["mx"]])
    dh = _mm(dz_r, w_in_r, tb=True, dep=tok, name="in_rwkv_dx")
    dh = _mm(dz_a, w_in_a, tb=True, res=dh, name="in_attn_dx")
    dx1, dx1_bf, gs["mix_norm"] = _mm(dz_g, w_in_g, tb=True, res=dh, post=_norm_bwd_post(x1, vec["mix_norm"], dx2), name="in_gate_dx")
    tok_mx = red["mx"].swap_wait_ici_start(dx1_bf)
    hooks = {"down": lambda d_wd: red["f1d"].swap_start([d_wd], after=tok_mx),
             "mid": lambda dgate: red["f1d"].swap_wait_ici_start(dgate),
             "dw": lambda d_wgt, d_wut: red["f1g"].swap_start([d_wgt, d_wut]),
             "dx": lambda part: red["f1g"].swap_wait_ici_start(part) + red["f1d"].ici_wait_join_start(part, adam["f1d"]),
             "end": lambda dx_: tokens.setdefault("mx_join", red["mx"].ici_wait_join_start(dx_, adam["mx"]))}
    tokens = {}
    dx0, _, gs["ffn1_norm"], gw["ffn1_w_gate"], gw["ffn1_w_up"], gw["ffn1_w_down"] = _ffn_bwd(
        dx1, dx1_bf, ffn1_saved, vec["ffn1_norm"], wb["ffn1_w_gate"], wb["ffn1_w_up"], wb["ffn1_w_down"], "ffn1", hooks=hooks)

    flat = jnp.concatenate([gs[n].reshape(-1) for n, _ in SMALL] + [loss_tile[0, 0:1]])
    small_buf = jnp.pad(flat, (0, SMALL_ROWS * PACK_COLS - flat.shape[0])).reshape(SMALL_ROWS, PACK_COLS)
    small_sum = _all_reduce_small(small_buf)
    n_small = sum(sz for _, sz in SMALL)
    loss = small_sum.reshape(-1)[n_small]
    grad_small = _unpack_small(small_sum, {n: wts[n].shape for n, _ in SMALL})
    d_s, m_s, v_s = _adamw(_pack_small(wts), small_sum, _pack_small(mom_m), _pack_small(mom_v), name="adamw_small",
                           dep=tokens["mx_join"])
    shapes = {n: wts[n].shape for n, _ in SMALL}
    d_s, m_s, v_s = _unpack_small(d_s, shapes), _unpack_small(m_s, shapes), _unpack_small(v_s, shapes)
    grads, deltas, new_m, new_v = {}, {}, {}, {}
    for n, _ in SMALL:
        grads[n], deltas[n], new_m[n], new_v[n] = grad_small[n], d_s[n], m_s[n], v_s[n]

    tok = red["f1g"].ici_wait_join_start(m_s["ffn1_norm"], adam["f1g"])
    for g in ("mx", "f1d", "f1g"):
        done.update(red[g].join_wait(tok))
    for n, res in done.items():
        for store, val in zip((grads, deltas, new_m, new_v), res):
            if n == "lora":
                store.update({k: t[None] for k, t in _lora_split(val).items()})
            else:
                store[n] = (jnp.transpose(val) if n in TRANSPOSED else val)[None]

    return (loss, dx0[None], *[grads[n] for n in WEIGHTS], *[deltas[n] for n in WEIGHTS],
            *[new_m[n] for n in WEIGHTS], *[new_v[n] for n in WEIGHTS])
```

```python
import functools

import jax
import jax.numpy as jnp
from jax import lax
from jax.experimental import pallas as pl
from jax.experimental.pallas import tpu as pltpu

F32, BF16 = jnp.float32, jnp.bfloat16
HI = lax.Precision.HIGHEST
MESH = pl.DeviceIdType.MESH
SDS = jax.ShapeDtypeStruct

D_MODEL = 1024
HEAD = 64
RWKV_HEADS = 8
RWKV_DIM = RWKV_HEADS * HEAD
DECAY_LORA, ICLR_LORA, GATE_LORA = 64, 64, 128
GN_EPS = 64e-5
RMS_EPS = 1e-6
ATTN_DILATIONS = (1, 4, 16)
BAND = 128
ATTN_DIM = 768
GROUP_DIM = 256
ROPE_THETA = 10000.0
NEG_INF = -1e30
RWKV_COLS = 3 * RWKV_DIM + DECAY_LORA + ICLR_LORA + GATE_LORA
ATTN_COLS = 3 * ATTN_DIM
ADAM_LR, ADAM_B1, ADAM_B2, ADAM_EPS, ADAM_WD, ADAM_STEP = 0.001, 0.9, 0.999, 1e-08, 0.01, 10

WKV_CHUNK = 64
WKV_HEADS_PER_STEP = 8
N_CHIPS = 4
PACK_COLS = 1024
VMEM_LIMIT = 48 * 1024 * 1024

TRANSPOSED = ("ffn1_w_gate", "ffn1_w_up", "ffn2_w_gate", "ffn2_w_up")
LORA = (("rwkv_w2", 64), ("rwkv_a2", 64), ("rwkv_g2", 128))
_FFN1 = (("ffn1_w_gate", "blk", 704, 1024), ("ffn1_w_up", "blk", 704, 1024), ("ffn1_w_down", "blk", 704, 1024))
_FFN2 = (("ffn2_w_gate", "blk", 704, 1024), ("ffn2_w_up", "blk", 704, 1024), ("ffn2_w_down", "blk", 704, 1024))
_IN = (("w_in", "col", 1024, 1536), ("lora", "col", 256, 128))
_BRANCH = (("w_br_rwkv", "col", 512, 256), ("w_br_attn", "col", 256, 256), ("w_out", "blk", 256, 1024))
_PLE = (("ple_w_gate", "blk", 256, 1024), ("ple_w_proj", "col", 256, 256))
GROUPS = {"f1": _FFN1, "mx": _IN, "f2": _BRANCH + _FFN2 + _PLE}
REDUCE_GROUPS = {"f2": _FFN2 + _PLE, "mx": _IN + _BRANCH, "f1d": _FFN1[2:], "f1g": _FFN1[:2]}
SMALL = (
    ("ffn1_norm", 1024), ("mix_norm", 1024), ("ffn2_norm", 1024), ("ple_norm", 1024), ("rwkv_mu", 1792),
    ("rwkv_w0", 512), ("rwkv_a0", 512), ("rwkv_k_k", 512), ("rwkv_k_a", 512), ("rwkv_r_k", 512),
    ("rwkv_gn_w", 512), ("rwkv_gn_b", 512), ("q_norm", 64), ("k_norm", 64),
)
SMALL_ROWS = 16
WEIGHTS = (
    "ffn1_norm", "ffn1_w_gate", "ffn1_w_up", "ffn1_w_down", "mix_norm", "w_in", "rwkv_mu", "rwkv_w0", "rwkv_w2",
    "rwkv_a0", "rwkv_a2", "rwkv_g2", "rwkv_k_k", "rwkv_k_a", "rwkv_r_k", "rwkv_gn_w", "rwkv_gn_b", "q_norm", "k_norm",
    "w_br_rwkv", "w_br_attn", "w_out", "ffn2_norm", "ffn2_w_gate", "ffn2_w_up", "ffn2_w_down", "ple_norm",
    "ple_w_gate", "ple_w_proj",
)


def _row_tile(n, most=704):
    for t in range(most - most % 16, 0, -16):
        if n % t == 0:
            return t
    return n


def _pick(n, cands):
    for c in cands:
        if n % c == 0:
            return c
    return n


def _mm(a, b, *, ta=False, tb=False, sum_blocks=False, out_dtype=F32, res=None, alpha=1.0, dep=None, post=None, name):
    flat = a.ndim == 2 and b.ndim == 2
    a3 = a if a.ndim == 3 else a[None]
    b3 = b if b.ndim == 3 else b[None]
    na, nbb = a3.shape[0], b3.shape[0]
    nblk = max(na, nbb)
    kdim, m = (a3.shape[1], a3.shape[2]) if ta else (a3.shape[2], a3.shape[1])
    n = b3.shape[1] if tb else b3.shape[2]
    assert (b3.shape[2] if tb else b3.shape[1]) == kdim
    tm = _pick(m, (1024, 512, 256, 128) if post is None else (512, 256, 128))
    tn = _pick(n, (1024, 896, 768, 512, 256, 128))
    tk = kdim if kdim <= 2304 else _pick(kdim, (1024, 512, 256, 128))
    nk = kdim // tk
    direct = nk == 1 and not sum_blocks

    if sum_blocks:
        grid = (m // tm, n // tn, nblk, nk)

        def ids(i, c, j, k):
            return i, c, j, k
    else:
        grid = (nblk, m // tm, n // tn, nk)

        def ids(j, i, c, k):
            return i, c, j, k

    def amap(*g):
        i, c, j, k = ids(*g)
        jj = j if na > 1 else 0
        return (jj, k, i) if ta else (jj, i, k)

    def bmap(*g):
        i, c, j, k = ids(*g)
        jj = j if nbb > 1 else 0
        return (jj, c, k) if tb else (jj, k, c)

    if sum_blocks:
        oshape, oblk = (m, n), (tm, tn)

        def omap(*g):
            i, c, j, k = ids(*g)
            return i, c
    else:
        oshape, oblk = (nblk, m, n), (1, tm, tn)

        def omap(*g):
            i, c, j, k = ids(*g)
            return j, i, c

    dn = (((0 if ta else 1,), (1 if tb else 0,)), ((), ()))
    has_res = res is not None
    p_f, p_rows, p_params, p_dtypes, p_accs = post if post is not None else (None, [], [], [], [])
    assert post is None or sum_blocks or flat
    n_in = 2 + has_res + len(p_rows) + len(p_params) + (dep is not None)

    def tile_map(*g):
        i, c, j, k = ids(*g)
        return i, c

    def body(*refs):
        refs = list(refs)
        acc = None if direct else refs.pop()
        o_refs = refs[n_in:]
        a_ref, b_ref = refs[0], refs[1]
        r_ref = refs[2] if has_res else None
        pr_refs = refs[2 + has_res:2 + has_res + len(p_rows)]
        pp_refs = refs[2 + has_res + len(p_rows):2 + has_res + len(p_rows) + len(p_params)]
        first_tile = jnp.logical_and(pl.program_id(0 if sum_blocks else 1) == 0, pl.program_id(1 if sum_blocks else 2) == 0)

        def finish(v):
            if alpha != 1.0:
                v = v * alpha
            if has_res:
                v = v + r_ref[...].reshape(v.shape).astype(F32)
            if post is None:
                o_refs[0][...] = v.reshape(o_refs[0].shape).astype(o_refs[0].dtype)
                return
            outs = p_f(v, *[t[...] for t in pr_refs], *[t[...] for t in pp_refs])
            for o_ref, val in zip(o_refs, outs[:len(p_dtypes)]):
                o_ref[...] = val.astype(o_ref.dtype)
            for o_ref, val in zip(o_refs[len(p_dtypes):], outs[len(p_dtypes):]):
                @pl.when(first_tile)
                def _():
                    o_ref[...] = jnp.zeros_like(o_ref)

                o_ref[...] += val.reshape(o_ref.shape)

        if direct:
            finish(lax.dot_general(a_ref[0].astype(BF16), b_ref[0].astype(BF16), dn, preferred_element_type=F32))
            return
        k = pl.program_id(3)
        if sum_blocks:
            j = pl.program_id(2)
            first = jnp.logical_and(j == 0, k == 0)
            last = jnp.logical_and(j == nblk - 1, k == nk - 1)
        else:
            first, last = k == 0, k == nk - 1

        @pl.when(first)
        def _():
            acc[...] = jnp.zeros_like(acc)

        acc[...] += lax.dot_general(a_ref[0].astype(BF16), b_ref[0].astype(BF16), dn, preferred_element_type=F32)

        @pl.when(last)
        def _():
            finish(acc[...])

    in_specs = [pl.BlockSpec((1, tk, tm) if ta else (1, tm, tk), amap), pl.BlockSpec((1, tn, tk) if tb else (1, tk, tn), bmap)]
    args = [a3, b3]
    if has_res:
        res3 = res if (sum_blocks or res.ndim == 3) else res[None]
        in_specs.append(pl.BlockSpec(oblk, omap))
        args.append(res3)
    in_specs += [pl.BlockSpec((tm, tn), tile_map) for _ in p_rows]
    in_specs += [pl.BlockSpec(t.shape, functools.partial(lambda *g, nd: (0,) * nd, nd=t.ndim)) for t in p_params]
    args += list(p_rows) + list(p_params)
    if dep is not None:
        in_specs.append(pl.BlockSpec(memory_space=pl.ANY))
        args.append(dep)
    if post is None:
        out_specs, out_shape = pl.BlockSpec(oblk, omap), SDS(oshape, out_dtype)
        semantics = ("parallel", "parallel", "arbitrary", "arbitrary") if sum_blocks else ("parallel", "parallel", "parallel", "arbitrary")
    else:
        out_specs = [pl.BlockSpec((tm, tn), tile_map) for _ in p_dtypes]
        out_specs += [pl.BlockSpec(tuple(sh), functools.partial(lambda *g, nd: (0,) * nd, nd=len(sh))) for sh in p_accs]
        out_shape = [SDS((m, n), dt) for dt in p_dtypes] + [SDS(tuple(sh), F32) for sh in p_accs]
        semantics = ("arbitrary",) * 4
    out = pl.pallas_call(
        body,
        name=name,
        grid=grid,
        in_specs=in_specs,
        out_specs=out_specs,
        out_shape=out_shape,
        scratch_shapes=[] if direct else [pltpu.VMEM((tm, tn), F32)],
        compiler_params=pltpu.CompilerParams(dimension_semantics=semantics, vmem_limit_bytes=VMEM_LIMIT),
    )(*args)
    if post is not None:
        return out
    if flat and not sum_blocks:
        out = out[0]
    return out


def _rows_call(f, rows, params, outs, accs=(), *, tm, name, dep=None):
    s = rows[0][0].shape[0]
    nr, npar, no = len(rows), len(params), len(outs)
    nin = nr + npar + (0 if dep is None else 1)
    in_specs = [pl.BlockSpec((tm, w), functools.partial(lambda i, cb: (i, cb), cb=cb)) for (_, cb, w) in rows]
    in_specs += [pl.BlockSpec(p.shape, functools.partial(lambda i, nd: (0,) * nd, nd=p.ndim)) for p in params]
    if dep is not None:
        in_specs.append(pl.BlockSpec(memory_space=pl.ANY))
    out_shape = [SDS((s, w), dt) for (w, dt) in outs] + [SDS(tuple(sh), F32) for sh in accs]
    out_specs = [pl.BlockSpec((tm, w), lambda i: (i, 0)) for (w, _) in outs]
    out_specs += [pl.BlockSpec(tuple(sh), functools.partial(lambda i, nd: (0,) * nd, nd=len(sh))) for sh in accs]

    def body(*refs):
        rin, pin = refs[:nr], refs[nr:nr + npar]
        oo, ao = refs[nin:nin + no], refs[nin + no:]
        res = f(*[r[...] for r in rin], *[p[...] for p in pin])
        if not isinstance(res, (tuple, list)):
            res = (res,)
        for o_ref, v in zip(oo, res[:no]):
            o_ref[...] = v.astype(o_ref.dtype)
        i = pl.program_id(0)
        for a_ref, v in zip(ao, res[no:]):
            @pl.when(i == 0)
            def _():
                a_ref[...] = jnp.zeros_like(a_ref)

            a_ref[...] += v.reshape(a_ref.shape)

    res = pl.pallas_call(
        body,
        name=name,
        grid=(s // tm,),
        in_specs=in_specs,
        out_specs=out_specs,
        out_shape=out_shape,
        compiler_params=pltpu.CompilerParams(dimension_semantics=("arbitrary",), vmem_limit_bytes=VMEM_LIMIT),
    )(*[r[0] for r in rows], *params, *([] if dep is None else [dep]))
    return res


def _mmv(a, b, mode):
    ca = 0 if mode[0] == "t" else 1
    cb = 1 if mode[1] == "t" else 0
    return lax.dot_general(a.astype(BF16), b.astype(BF16), (((ca,), (cb,)), ((), ())), preferred_element_type=F32)


@functools.partial(jax.custom_vjp, nondiff_argnums=(2,))
def _bdot(a, b, mode):
    return _mmv(a, b, mode)


def _bdot_fwd(a, b, mode):
    return _mmv(a, b, mode), (a, b)


def _bdot_bwd(mode, saved, g):
    a, b = saved
    if mode == "nn":
        return _mmv(g, b, "nt"), _mmv(a, g, "tn")
    if mode == "nt":
        return _mmv(g, b, "nn"), _mmv(g, a, "tn")
    return _mmv(b, g, "nt"), _mmv(a, g, "nn")


_bdot.defvjp(_bdot_fwd, _bdot_bwd)


def _hdot(a, b, mode="nn", precision=HI):
    ca = 0 if mode[0] == "t" else 1
    cb = 1 if mode[1] == "t" else 0
    return lax.dot_general(a, b, (((ca,), (cb,)), ((), ())), precision=precision, preferred_element_type=F32)


def _segsum(x):
    c = x.shape[-1]
    blk = min(c, 256)
    r = lax.broadcasted_iota(jnp.int32, (blk, blk), 0) >> 6
    q = lax.broadcasted_iota(jnp.int32, (blk, blk), 1) >> 6
    ones = jnp.where(r == q, 1.0, 0.0).astype(F32)
    parts = [_hdot(x[:, i:i + blk], ones, precision=lax.Precision.HIGH) for i in range(0, c, blk)]
    return parts[0] if len(parts) == 1 else jnp.concatenate(parts, axis=1)


def _sigmoid(x):
    return jax.nn.sigmoid(x)


def _softplus(x):
    return jnp.maximum(x, 0.0) + jnp.log(1.0 + jnp.exp(-jnp.abs(x)))


def _rms(x, gain):
    return x * lax.rsqrt(jnp.mean(x * x, axis=-1, keepdims=True) + RMS_EPS) * gain


def _swiglu_act(gate, up):
    return gate * _sigmoid(gate) * up


def _rwkv_pre(zs, w0, w2, a0, a2, g2, k_k, k_a):
    r, k, v = zs[:, 0:512], zs[:, 512:1024], zs[:, 1024:1536]
    lora = zs[:, 1536:1792]
    wd, ad, gd = lora[:, 0:64], lora[:, 64:128], lora[:, 128:256]
    w = -_softplus(-(w0 + _bdot(jnp.tanh(wd), w2, "nn"))) - 0.5
    a = _sigmoid(a0 + _bdot(ad, a2, "nn"))
    g = _bdot(_sigmoid(gd), g2, "nn")
    kk = k * k_k
    kk = kk * lax.rsqrt(jnp.maximum(_segsum(kk * kk), 1e-24))
    k2 = k * (1.0 + (a - 1.0) * k_a)
    return r, -jnp.exp(w), k2, v, -kk, kk * a, g


def _rwkv_post(y, r, k2, v, g, gn_w, gn_b, r_k):
    mean = _segsum(y) * (1.0 / HEAD)
    yc = y - mean
    var = _segsum(yc * yc) * (1.0 / HEAD)
    yn = yc * lax.rsqrt(var + GN_EPS) * gn_w + gn_b
    bonus = _segsum(r * k2 * r_k) * v
    return (yn + bonus) * g


def _swap_halves(x):
    lane = lax.broadcasted_iota(jnp.int32, x.shape, 1)
    return jnp.where((lane & 32) == 0, jnp.roll(x, -32, axis=1), jnp.roll(x, 32, axis=1))


def _norm_rope(x, gain, cos, sin):
    heads = x.shape[1] // HEAD
    def rep(t):
        return jnp.concatenate([t] * heads, axis=1)

    xn = x * lax.rsqrt(_segsum(x * x) * (1.0 / HEAD) + RMS_EPS) * rep(gain)
    return xn * rep(cos) + _swap_halves(xn) * rep(sin)


def _attn_combine(o0, o1, o2, l0, l1, l2):
    m = jnp.maximum(jnp.maximum(l0, l1), l2)
    e0, e1, e2 = jnp.exp(l0 - m), jnp.exp(l1 - m), jnp.exp(l2 - m)
    return (e0 * o0 + e1 * o1 + e2 * o2) / (e0 + e1 + e2)


def _merge(zgr, zga, br, ba):
    return _sigmoid(zgr) * br + _sigmoid(zga) * ba


def _attn_block(q, kp, kc, vp, vc, has_prev):
    iq = lax.broadcasted_iota(jnp.int32, (1, BAND, BAND), 1)
    ik = lax.broadcasted_iota(jnp.int32, (1, BAND, BAND), 2)
    s_c = jnp.where(iq >= ik, _bdotb(q, kc, "nt") * (HEAD ** -0.5), NEG_INF)
    s_p = jnp.where(jnp.logical_and(iq <= ik, has_prev), _bdotb(q, kp, "nt") * (HEAD ** -0.5), NEG_INF)
    m = lax.stop_gradient(jnp.maximum(jnp.max(s_c, axis=-1, keepdims=True), jnp.max(s_p, axis=-1, keepdims=True)))
    e_c, e_p = jnp.exp(s_c - m), jnp.exp(s_p - m)
    l = jnp.sum(e_c, axis=-1, keepdims=True) + jnp.sum(e_p, axis=-1, keepdims=True)
    o = (_bdotb(e_c, vc) + _bdotb(e_p, vp)) / l
    return o, jnp.broadcast_to(m + jnp.log(l), o.shape)


def _mmb(a, b, cb):
    return lax.dot_general(a.astype(BF16), b.astype(BF16), (((2,), (cb,)), ((0,), (0,))), preferred_element_type=F32)


@functools.partial(jax.custom_vjp, nondiff_argnums=(2,))
def _bdotb1(a, b, cb):
    return _mmb(a, b, cb)


def _bdotb1_fwd(a, b, cb):
    return _mmb(a, b, cb), (a, b)


def _bdotb1_bwd(cb, saved, g):
    a, b = saved
    if cb == 1:
        return _mmb(g, b, 2), _mmb(jnp.swapaxes(a, 1, 2), g, 1)
    return _mmb(g, b, 1), _mmb(jnp.swapaxes(g, 1, 2), a, 1)


_bdotb1.defvjp(_bdotb1_fwd, _bdotb1_bwd)


def _bdotb(a, b, mode="nn", precision=None):
    if mode[0] == "t":
        a = jnp.swapaxes(a, 1, 2)
    cb = 2 if mode[1] == "t" else 1
    if precision is None:
        return _bdotb1(a, b, cb)
    return lax.dot_general(a, b, (((2,), (cb,)), ((0,), (0,))), precision=precision, preferred_element_type=F32)


def _tri_inv_levels(a):
    t = a.shape[-1]
    row = lax.broadcasted_iota(jnp.int32, (1, t, t), 1)
    col = lax.broadcasted_iota(jnp.int32, (1, t, t), 2)
    x = jnp.where(row == col, 1.0, 0.0).astype(F32) + jnp.where(jnp.logical_and(row == col + 1, (row & 1) == 1), a, 0.0)
    sh = 1
    while (1 << sh) < t:
        m = jnp.logical_and((row >> sh) == (col >> sh) + 1, (row >> (sh + 1)) == (col >> (sh + 1)))
        x = x + _bdotb(_bdotb(x, jnp.where(m, a, 0.0)), x)
        sh += 1
    return x


@jax.custom_vjp
def _tri_inv(a):
    return _tri_inv_levels(a)


def _tri_inv_fwd(a):
    x = _tri_inv_levels(a)
    return x, x


def _tri_inv_bwd(x, g):
    xt = jnp.swapaxes(x, 1, 2)
    return (_bdotb(_bdotb(xt, g, precision=lax.Precision.HIGH), xt, precision=lax.Precision.HIGH),)


_tri_inv.defvjp(_tri_inv_fwd, _tri_inv_bwd)


@jax.custom_vjp
def _known_inv(a, x):
    return x


def _known_inv_fwd(a, x):
    return x, x


def _known_inv_bwd(x, g):
    return _tri_inv_bwd(x, g)[0], jnp.zeros_like(x)


_known_inv.defvjp(_known_inv_fwd, _known_inv_bwd)


def _wkv_chunk(s0, r, lw, k, v, a, b, inv=None, with_inv=False):
    nh, t, _ = r.shape
    row = lax.broadcasted_iota(jnp.int32, (1, t, t), 1)
    col = lax.broadcasted_iota(jnp.int32, (1, t, t), 2)
    incl, strict = row >= col, row > col
    ones = jnp.broadcast_to(jnp.where(incl, 1.0, 0.0).astype(F32), (nh, t, t))
    cum = _bdotb(ones, lw, precision=HI)
    c_end = cum[:, t - 1:t, :]
    e_in, e_ex, e_inv = jnp.exp(cum), jnp.exp(cum - lw), jnp.exp(-cum)
    at, rt, bt, kt = a * e_ex, r * e_in, b * e_inv, k * e_inv
    a_ab = jnp.where(strict, _bdotb(at, bt, "nt"), 0.0)
    a_ak = jnp.where(strict, _bdotb(at, kt, "nt"), 0.0)
    x = _tri_inv(a_ab) if inv is None else _known_inv(a_ab, inv)
    u = _bdotb(x, _bdotb(at, s0, "nt") + _bdotb(a_ak, v))
    y = (_bdotb(rt, s0, "nt") + _bdotb(jnp.where(incl, _bdotb(rt, bt, "nt"), 0.0), u)
         + _bdotb(jnp.where(incl, _bdotb(rt, kt, "nt"), 0.0), v))
    w_end = jnp.exp(c_end - cum)
    s1 = s0 * jnp.exp(c_end) + _bdotb(u, b * w_end, "tn") + _bdotb(v, k * w_end, "tn")
    return (y, s1, x) if with_inv else (y, s1)


def _shift_fwd(z, mu):
    s, c = z.shape
    tc = 256

    def body(z_ref, mu_ref, o_ref):
        zz = z_ref[...]
        row = lax.broadcasted_iota(jnp.int32, zz.shape, 0)
        prev = jnp.where(row == 0, 0.0, pltpu.roll(zz, 1, 0))
        o_ref[...] = zz + (prev - zz) * mu_ref[...]

    return pl.pallas_call(
        body, name="shift_fwd", grid=(c // tc,),
        in_specs=[pl.BlockSpec((s, tc), lambda j: (0, j)), pl.BlockSpec((1, tc), lambda j: (0, j))],
        out_specs=pl.BlockSpec((s, tc), lambda j: (0, j)), out_shape=SDS((s, c), F32),
        compiler_params=pltpu.CompilerParams(dimension_semantics=("parallel",), vmem_limit_bytes=VMEM_LIMIT),
    )(z, mu)


def _shift_bwd(z, mu, dzs):
    s, c = z.shape
    tc = 256

    def body(z_ref, mu_ref, d_ref, dz_ref, dmu_ref):
        zz, d, m = z_ref[...], d_ref[...], mu_ref[...]
        row = lax.broadcasted_iota(jnp.int32, zz.shape, 0)
        prev = jnp.where(row == 0, 0.0, pltpu.roll(zz, 1, 0))
        t = d * m
        nxt = jnp.where(row == s - 1, 0.0, pltpu.roll(t, s - 1, 0))
        dz_ref[...] = (d - t + nxt).astype(dz_ref.dtype)
        dmu_ref[...] = jnp.sum(d * (prev - zz), axis=0, keepdims=True)

    return pl.pallas_call(
        body, name="shift_bwd", grid=(c // tc,),
        in_specs=[pl.BlockSpec((s, tc), lambda j: (0, j)), pl.BlockSpec((1, tc), lambda j: (0, j)),
                  pl.BlockSpec((s, tc), lambda j: (0, j))],
        out_specs=[pl.BlockSpec((s, tc), lambda j: (0, j)), pl.BlockSpec((1, tc), lambda j: (0, j))],
        out_shape=[SDS((s, c), BF16), SDS((1, c), F32)],
        compiler_params=pltpu.CompilerParams(dimension_semantics=("parallel",), vmem_limit_bytes=VMEM_LIMIT),
    )(z, mu, dzs)


def _heads(x, nh):
    return jnp.stack([x[:, h * HEAD:(h + 1) * HEAD] for h in range(nh)], axis=0)


def _unheads(x):
    return jnp.concatenate([x[h] for h in range(x.shape[0])], axis=1)


def _wkv_fwd(zs, lw, k2, na, b):
    s = lw.shape[0]
    t, hb = WKV_CHUNK, WKV_HEADS_PER_STEP
    w = hb * HEAD
    nc, ng = s // t, RWKV_HEADS // hb

    def body(r_ref, v_ref, lw_ref, k_ref, a_ref, b_ref, y_ref, s0_ref, x_ref, state):
        @pl.when(pl.program_id(1) == 0)
        def _():
            state[...] = jnp.zeros_like(state)

        s0 = state[...]
        s0_ref[0] = s0
        y, s1, x = _wkv_chunk(s0, *[_heads(t_ref[...], hb) for t_ref in (r_ref, lw_ref, k_ref, v_ref, a_ref, b_ref)], with_inv=True)
        y_ref[...] = _unheads(y)
        x_ref[0] = x
        state[...] = s1

    def col(off):
        return pl.BlockSpec((t, w), functools.partial(lambda g, i, off: (i, g + off), off=off))

    return pl.pallas_call(
        body, name="wkv_fwd", grid=(ng, nc),
        in_specs=[col(0), col(2 * ng), col(0), col(0), col(0), col(0)],
        out_specs=[col(0), pl.BlockSpec((1, hb, HEAD, HEAD), lambda g, i: (i, g, 0, 0)),
                   pl.BlockSpec((1, hb, t, t), lambda g, i: (i, g, 0, 0))],
        out_shape=[SDS((s, RWKV_DIM), F32), SDS((nc, RWKV_HEADS, HEAD, HEAD), F32), SDS((nc, RWKV_HEADS, t, t), F32)],
        scratch_shapes=[pltpu.VMEM((hb, HEAD, HEAD), F32)],
        compiler_params=pltpu.CompilerParams(dimension_semantics=("parallel", "arbitrary"), vmem_limit_bytes=VMEM_LIMIT),
    )(zs, zs, lw, k2, na, b)


def _wkv_bwd(zs, lw, k2, na, b, s0s, invs, dy):
    s = lw.shape[0]
    t, hb = WKV_CHUNK, WKV_HEADS_PER_STEP
    w = hb * HEAD
    nc, ng = s // t, RWKV_HEADS // hb

    def body(r_ref, v_ref, lw_ref, k_ref, a_ref, b_ref, s0_ref, x_ref, dy_ref, dr_ref, dlw_ref, dk_ref, dv_ref, da_ref, db_ref, dstate):
        @pl.when(pl.program_id(1) == 0)
        def _():
            dstate[...] = jnp.zeros_like(dstate)

        _, vjp = jax.vjp(functools.partial(_wkv_chunk, inv=x_ref[0]), s0_ref[0],
                         *[_heads(t_ref[...], hb) for t_ref in (r_ref, lw_ref, k_ref, v_ref, a_ref, b_ref)])
        grads = vjp((_heads(dy_ref[...], hb), dstate[...]))
        dstate[...] = grads[0]
        for o_ref, gval in zip((dr_ref, dlw_ref, dk_ref, dv_ref, da_ref, db_ref), grads[1:]):
            o_ref[...] = _unheads(gval)

    def col(off):
        return pl.BlockSpec((t, w), functools.partial(lambda g, i, off: (nc - 1 - i, g + off), off=off))

    return pl.pallas_call(
        body, name="wkv_bwd", grid=(ng, nc),
        in_specs=[col(0), col(2 * ng), col(0), col(0), col(0), col(0),
                  pl.BlockSpec((1, hb, HEAD, HEAD), lambda g, i: (nc - 1 - i, g, 0, 0)),
                  pl.BlockSpec((1, hb, t, t), lambda g, i: (nc - 1 - i, g, 0, 0)), col(0)],
        out_specs=[col(0)] * 6,
        out_shape=[SDS((s, RWKV_DIM), F32)] * 6,
        scratch_shapes=[pltpu.VMEM((hb, HEAD, HEAD), F32)],
        compiler_params=pltpu.CompilerParams(dimension_semantics=("parallel", "arbitrary"), vmem_limit_bytes=VMEM_LIMIT),
    )(zs, zs, lw, k2, na, b, s0s, invs, dy)


def _attn_fwd(q, k, v, d):
    s = q.shape[0]
    l = s // d
    nb = l // BAND
    assert nb * BAND == l
    qv, kv, vv = (t.reshape(l, d * GROUP_DIM) for t in (q, k, v))
    nh = GROUP_DIM // HEAD

    def body(q_ref, kp_ref, kc_ref, vp_ref, vc_ref, o_ref, l_ref):
        has_prev = pl.program_id(1) > 0
        o, lse = _attn_block(*[_heads(t_ref[...].astype(F32), nh) for t_ref in (q_ref, kp_ref, kc_ref, vp_ref, vc_ref)], has_prev)
        o_ref[...] = _unheads(o)
        l_ref[...] = _unheads(lse)

    cur = pl.BlockSpec((BAND, GROUP_DIM), lambda rho, i: (i, rho))
    prev = pl.BlockSpec((BAND, GROUP_DIM), lambda rho, i: (jnp.maximum(i - 1, 0), rho))
    o, lse = pl.pallas_call(
        body, name=f"attn_fwd_d{d}", grid=(d, nb),
        in_specs=[cur, prev, cur, prev, cur], out_specs=[cur, cur],
        out_shape=[SDS((l, d * GROUP_DIM), F32), SDS((l, d * GROUP_DIM), F32)],
        compiler_params=pltpu.CompilerParams(dimension_semantics=("parallel", "arbitrary"), vmem_limit_bytes=VMEM_LIMIT),
    )(qv, kv, kv, vv, vv)
    return o.reshape(s, GROUP_DIM), lse.reshape(s, GROUP_DIM)


def _attn_bwd(q, k, v, d, do, dlse):
    s = q.shape[0]
    l = s // d
    nb = l // BAND
    qv, kv, vv, dov, dlv = (t.reshape(l, d * GROUP_DIM) for t in (q, k, v, do, dlse))
    nh = GROUP_DIM // HEAD

    def body(q_ref, kp_ref, kc_ref, vp_ref, vc_ref, do_ref, dl_ref, dq_ref, dk_ref, dv_ref, ck, cv):
        step = pl.program_id(1)
        has_prev = step < nb - 1

        @pl.when(step == 0)
        def _():
            ck[...] = jnp.zeros_like(ck)
            cv[...] = jnp.zeros_like(cv)

        _, vjp = jax.vjp(functools.partial(_attn_block, has_prev=has_prev),
                         *[_heads(t_ref[...].astype(F32), nh) for t_ref in (q_ref, kp_ref, kc_ref, vp_ref, vc_ref)])
        dq, dkp, dkc, dvp, dvc = vjp((_heads(do_ref[...], nh), _heads(dl_ref[...], nh)))
        dq_ref[...] = _unheads(dq)
        dk_ref[...] = _unheads(dkc) + ck[...]
        dv_ref[...] = _unheads(dvc) + cv[...]
        ck[...] = _unheads(dkp)
        cv[...] = _unheads(dvp)

    cur = pl.BlockSpec((BAND, GROUP_DIM), lambda rho, i: (nb - 1 - i, rho))
    prev = pl.BlockSpec((BAND, GROUP_DIM), lambda rho, i: (jnp.maximum(nb - 2 - i, 0), rho))
    dq, dk, dv = pl.pallas_call(
        body, name=f"attn_bwd_d{d}", grid=(d, nb),
        in_specs=[cur, prev, cur, prev, cur, cur, cur], out_specs=[cur] * 3,
        out_shape=[SDS((l, d * GROUP_DIM), F32)] * 3,
        scratch_shapes=[pltpu.VMEM((BAND, GROUP_DIM), F32), pltpu.VMEM((BAND, GROUP_DIM), F32)],
        compiler_params=pltpu.CompilerParams(dimension_semantics=("parallel", "arbitrary"), vmem_limit_bytes=VMEM_LIMIT),
    )(qv, kv, kv, vv, vv, dov, dlv)
    return dq.reshape(s, GROUP_DIM), dk.reshape(s, GROUP_DIM), dv.reshape(s, GROUP_DIM)


def _coords():
    return lax.axis_index("x"), lax.axis_index("y"), lax.axis_index("c")


_CHIP_FLIPS = ((1, 0), (0, 1), (1, 1))


def _flip(v, f):
    return 1 - v if f else v


def _form(kind, r, c):
    return (N_CHIPS, r, c) if kind == "blk" else (r, N_CHIPS * c)


def _slot(ref, kind, j, rows, c):
    if kind == "blk":
        return ref.at[j] if rows is None else ref.at[j, rows]
    cols = pl.ds(pl.multiple_of(j * c, 128), c)
    return ref.at[:, cols] if rows is None else ref.at[rows, cols]


def _half(r, which, align):
    return pl.ds(pl.multiple_of(which * (r // 2), align), r // 2)


def _rcopy(src, dst, send_sems, recv_sems, kk, dev):
    return pltpu.make_async_remote_copy(src_ref=src, dst_ref=dst, send_sem=send_sems.at[kk], recv_sem=recv_sems.at[kk],
                                        device_id=dev, device_id_type=MESH)


def _gather_plan(specs, step):
    def copies(refs, ss, rs, received):
        x, y, c = _coords()
        out = []
        for w, (kind, r, cc) in enumerate(specs):
            mine, other = _half(r, c, 16), _half(r, 1 - c, 16)
            for kk, (fx, fy) in enumerate(_CHIP_FLIPS):
                px, py = _flip(x, fx), _flip(y, fy)
                if step == "ici":
                    sl = _slot(refs[w], kind, 2 * px + py if received else 2 * x + y, mine, cc)
                    dev = (px, py, c)
                else:
                    sl = _slot(refs[w], kind, 2 * px + py, other if received else mine, cc)
                    dev = (x, y, 1 - c)
                out.append(_rcopy(sl, sl, ss, rs, 3 * w + kk, dev))
        return out

    def issue(refs, ss, rs):
        return copies(refs, ss, rs, False)

    def expect(refs, ss, rs):
        return copies(refs, ss, rs, False), copies(refs, ss, rs, True)

    return issue, expect


_HBM = pl.BlockSpec(memory_space=pltpu.HBM)
_SEM = pl.BlockSpec(memory_space=pltpu.SEMAPHORE)
_EFFECT = pltpu.SideEffectType.DATAFLOW_SIDE_EFFECTING


def _copies_start(name, bufs, n_sems, issue, after=None):
    nb = len(bufs)
    extra = [] if after is None else [after]

    def body(*refs):
        send_sems, recv_sems = refs[nb + len(extra)], refs[nb + len(extra) + 1]
        for cp in issue(refs[:nb], send_sems, recv_sems):
            cp.start()
        refs[-1][...] = jnp.zeros_like(refs[-1])

    outs = pl.pallas_call(
        body, name=name,
        out_shape=(pltpu.SemaphoreType.DMA((n_sems,)), pltpu.SemaphoreType.DMA((n_sems,)),
                   *[pltpu.HBM(b.shape, b.dtype) for b in bufs], SDS((8, 128), F32)),
        in_specs=[_HBM] * nb + [pl.BlockSpec(memory_space=pl.ANY)] * len(extra),
        out_specs=(_SEM, _SEM, *[_HBM] * nb, pl.BlockSpec(memory_space=pltpu.VMEM)),
        input_output_aliases={i: 2 + i for i in range(nb)},
        compiler_params=pltpu.CompilerParams(has_side_effects=_EFFECT),
    )(*[pltpu.with_memory_space_constraint(b, pltpu.HBM) for b in bufs], *extra)
    return outs[0], outs[1], list(outs[2:2 + nb]), outs[-1]


def _copies_wait(name, bufs, send_sems, recv_sems, after, expect):
    nb = len(bufs)

    def body(*refs):
        sent, received = expect(refs[:nb], refs[nb], refs[nb + 1])
        for cp in sent:
            cp.wait_send()
        for cp in received:
            cp.wait_recv()

    outs = pl.pallas_call(
        body, name=name,
        out_shape=tuple(pltpu.HBM(b.shape, b.dtype) for b in bufs),
        in_specs=(*[_HBM] * nb, _SEM, _SEM, pl.BlockSpec(memory_space=pl.ANY)), out_specs=tuple([_HBM] * nb),
        input_output_aliases={i: i for i in range(nb)},
        compiler_params=pltpu.CompilerParams(has_side_effects=_EFFECT),
    )(*bufs, send_sems, recv_sems, after)
    return list(outs)


def _add_pair(g, recv, kind, r, c, c_arr, name):
    h = r // 2
    if kind == "blk":
        tr = _row_tile(h, 512)
        grid = (N_CHIPS, h // tr)
        g_spec = pl.BlockSpec((1, 1, tr, c), lambda j, i, c_ref: (j, c_ref[0], i, 0))
        o_spec = pl.BlockSpec((1, tr, c), lambda j, i, c_ref: (j, i, 0))
        gv, oshape = g.reshape(N_CHIPS, 2, h, c), (N_CHIPS, h, c)
    else:
        tr = _row_tile(h, 64)
        grid = (h // tr,)
        g_spec = pl.BlockSpec((1, tr, N_CHIPS * c), lambda i, c_ref: (c_ref[0], i, 0))
        o_spec = pl.BlockSpec((tr, N_CHIPS * c), lambda i, c_ref: (i, 0))
        gv, oshape = g.reshape(2, h, N_CHIPS * c), (h, N_CHIPS * c)

    def body(c_ref, g_ref, r_ref, o_ref, ob_ref):
        v = (g_ref[:, 0] if kind == "blk" else g_ref[0]) + r_ref[...]
        o_ref[...] = v
        ob_ref[...] = v.astype(BF16)

    return pl.pallas_call(
        body, name=name,
        grid_spec=pltpu.PrefetchScalarGridSpec(num_scalar_prefetch=1, grid=grid, in_specs=[g_spec, o_spec], out_specs=[o_spec] * 2),
        out_shape=[SDS(oshape, F32), SDS(oshape, BF16)],
        compiler_params=pltpu.CompilerParams(vmem_limit_bytes=VMEM_LIMIT),
    )(c_arr, gv, recv)


def _sum_adamw(pair, recv, w, m, v, kind, r, c, mc_arr, name):
    h = r // 2
    tr = _row_tile(h, 256)
    nt = h // tr
    if kind == "blk":
        p_spec = pl.BlockSpec((1, tr, c), lambda i, mc: (mc[0], i, 0))
    else:
        p_spec = pl.BlockSpec((tr, c), lambda i, mc: (i, mc[0]))
    mine = pl.BlockSpec((tr, c), lambda i, mc: (mc[1] * nt + i, 0))

    def body(mc, a_ref, r_ref, w_ref, m_ref, v_ref, g_out, d_out, m_out, v_out):
        own = a_ref[0] if kind == "blk" else a_ref[...]
        g = ((own + r_ref[0].astype(F32)) + r_ref[1].astype(F32)) + r_ref[2].astype(F32)
        g_out[...] = g
        d_out[...], m_out[...], v_out[...] = _adamw_rows(w_ref[...], g, m_ref[...], v_ref[...])

    return pl.pallas_call(
        body, name=name,
        grid_spec=pltpu.PrefetchScalarGridSpec(
            num_scalar_prefetch=1, grid=(nt,),
            in_specs=[p_spec, pl.BlockSpec((3, tr, c), lambda i, mc: (0, i, 0)), mine, mine, mine], out_specs=[mine] * 4),
        out_shape=[SDS((r, c), F32)] * 4,
        compiler_params=pltpu.CompilerParams(vmem_limit_bytes=VMEM_LIMIT),
    )(mc_arr, pair, recv, w, m, v)


class _GroupReduce:
    def __init__(self, tag, specs, c_arr, mc_arr):
        self.tag, self.specs, self.c_arr, self.mc_arr = tag, specs, c_arr, mc_arr
        self.n = len(specs)

    def _plan(self, step):
        specs, n = self.specs, self.n

        def copies(refs, ss, rs, received):
            x, y, c = _coords()
            sib, out = (x, y, 1 - c), []
            for w, (_, kind, r, cc) in enumerate(specs):
                if step == "join":
                    for q in range(4):
                        there = refs[4 * w + q].at[_half(r, 1 - c if received else c, 8)]
                        out.append(_rcopy(there, there, ss, rs, 4 * w + q, sib))
                    continue
                src, land = refs[w], refs[n + w]
                if step == "swap":
                    rows = _half(r, 1 - c, 8)
                    part = src.at[:, rows] if kind == "blk" else src.at[rows]
                    out.append(_rcopy(land if received else part, land, ss, rs, w, sib))
                else:
                    for kk, (fx, fy) in enumerate(_CHIP_FLIPS):
                        px, py = _flip(x, fx), _flip(y, fy)
                        part = land.at[kk] if received else _slot(src, kind, 2 * px + py, None, cc)
                        out.append(_rcopy(part, land.at[kk], ss, rs, 3 * w + kk, (px, py, c)))
            return out

        def issue(refs, ss, rs):
            return copies(refs, ss, rs, False)

        def expect(refs, ss, rs):
            return copies(refs, ss, rs, False), copies(refs, ss, rs, True)

        return issue, expect

    def swap_start(self, grads, after=None):
        lands = [lax.empty(_form(kind, r // 2, c), F32) for _, kind, r, c in self.specs]
        ss, rs, bufs, tok = _copies_start(f"rs_{self.tag}_swap", list(grads) + lands, self.n, self._plan("swap")[0], after=after)
        self.state = (ss, rs, bufs)
        return tok

    def swap_wait_ici_start(self, after):
        ss, rs, bufs = self.state
        bufs = _copies_wait(f"rs_{self.tag}_swap_wait", bufs, ss, rs, after, self._plan("swap")[1])
        pairs = [_add_pair(bufs[w], bufs[self.n + w], kind, r, c, self.c_arr, name=f"rs_{self.tag}_pair_{nm}")
                 for w, (nm, kind, r, c) in enumerate(self.specs)]
        self.pair = [pr[0] for pr in pairs]
        lands = [lax.empty((3, r // 2, c), BF16) for _, _, r, c in self.specs]
        ss, rs, bufs, tok = _copies_start(f"rs_{self.tag}_ici", [pr[1] for pr in pairs] + lands, 3 * self.n, self._plan("ici")[0])
        self.state = (ss, rs, bufs)
        return tok

    def ici_wait_join_start(self, after, state):
        ss, rs, bufs = self.state
        bufs = _copies_wait(f"rs_{self.tag}_ici_wait", bufs, ss, rs, after, self._plan("ici")[1])
        outs = []
        for w, (nm, kind, r, c) in enumerate(self.specs):
            outs += _sum_adamw(self.pair[w], bufs[self.n + w], *state[nm], kind, r, c, self.mc_arr, name=f"rs_{self.tag}_adamw_{nm}")
        ss, rs, bufs, tok = _copies_start(f"rs_{self.tag}_join", outs, 4 * self.n, self._plan("join")[0])
        self.state = (ss, rs, bufs)
        return tok

    def join_wait(self, after):
        ss, rs, bufs = self.state
        bufs = _copies_wait(f"rs_{self.tag}_join_wait", bufs, ss, rs, after, self._plan("join")[1])
        return {nm: tuple(bufs[4 * w:4 * w + 4]) for w, (nm, _, _, _) in enumerate(self.specs)}


def _all_reduce_small(buf):
    rows, cols = buf.shape

    def body(x_ref, o_ref, gath, send_sems, recv_sems):
        x, y, c = _coords()
        me = 4 * x + 2 * y + c
        gath[me] = x_ref[...]
        sends = []
        for kk in range(1, 8):
            f = (kk >> 2) & 1, (kk >> 1) & 1, kk & 1
            px, py, pc = _flip(x, f[0]), _flip(y, f[1]), _flip(c, f[2])
            cp = pltpu.make_async_remote_copy(src_ref=x_ref, dst_ref=gath.at[me], send_sem=send_sems.at[kk - 1],
                                              recv_sem=recv_sems.at[kk - 1], device_id=(px, py, pc), device_id_type=MESH)
            cp.start()
            sends.append(cp)
        for kk in range(1, 8):
            f = (kk >> 2) & 1, (kk >> 1) & 1, kk & 1
            px, py, pc = _flip(x, f[0]), _flip(y, f[1]), _flip(c, f[2])
            there = gath.at[4 * px + 2 * py + pc]
            pltpu.make_async_remote_copy(src_ref=there, dst_ref=there, send_sem=send_sems.at[kk - 1],
                                         recv_sem=recv_sems.at[kk - 1], device_id=(px, py, pc), device_id_type=MESH).wait_recv()
        for cp in sends:
            cp.wait_send()
        acc = gath[0]
        for j in range(1, 8):
            acc = acc + gath[j]
        o_ref[...] = acc

    return pl.pallas_call(
        body, name="all_reduce_small",
        in_specs=[pl.BlockSpec(memory_space=pltpu.VMEM)], out_specs=pl.BlockSpec(memory_space=pltpu.VMEM),
        out_shape=SDS((rows, cols), F32),
        scratch_shapes=[pltpu.VMEM((8, rows, cols), F32), pltpu.SemaphoreType.DMA((7,)), pltpu.SemaphoreType.DMA((7,))],
    )(buf)


def _adamw_rows(w, g, m, v):
    m = ADAM_B1 * m + (1.0 - ADAM_B1) * g
    v = ADAM_B2 * v + (1.0 - ADAM_B2) * jnp.square(g)
    m_hat = m / (1.0 - ADAM_B1 ** ADAM_STEP)
    v_hat = v / (1.0 - ADAM_B2 ** ADAM_STEP)
    return -ADAM_LR * (m_hat / (jnp.sqrt(v_hat) + ADAM_EPS) + ADAM_WD * w), m, v


def _adamw(w, g, m, v, name, dep=None):
    rows, cols = w.shape
    tm = _pick(rows, (256, 128, 64, 16, 8))
    return _rows_call(_adamw_rows, [(t, 0, cols) for t in (w, g, m, v)], [], [(cols, F32)] * 3, tm=tm, name=name, dep=dep)


def _pack_small(parts):
    flat = jnp.concatenate([parts[n].reshape(-1) for n, _ in SMALL])
    return jnp.pad(flat, (0, SMALL_ROWS * PACK_COLS - flat.shape[0])).reshape(SMALL_ROWS, PACK_COLS)


def _unpack_small(buf, shapes):
    flat, out, off = buf.reshape(-1), {}, 0
    for n, sz in SMALL:
        out[n] = flat[off:off + sz].reshape(shapes[n])
        off += sz
    return out


def _lora_stack(parts):
    return jnp.concatenate([parts[n] for n, _ in LORA], axis=-2)


def _lora_split(stacked):
    out, off = {}, 0
    for n, rows in LORA:
        out[n] = stacked[..., off:off + rows, :]
        off += rows
    return out


def _ffn_gate_up(h, wgt, wut, name, dep=None):
    s, d = h.shape
    nblk, f, _ = wgt.shape
    tm = _pick(s, (1024, 512, 256))
    dn = (((1,), (1,)), ((), ()))

    def body(h_ref, wg_ref, wu_ref, *rest):
        g_ref, u_ref, a_ref = rest[-3:]
        hh = h_ref[...]
        g = lax.dot_general(hh, wg_ref[0], dn, preferred_element_type=F32)
        u = lax.dot_general(hh, wu_ref[0], dn, preferred_element_type=F32)
        g_ref[0], u_ref[0] = g, u
        a_ref[0] = _swiglu_act(g, u).astype(BF16)

    w_spec = pl.BlockSpec((1, f, d), lambda j, i: (j, 0, 0))
    o_spec = pl.BlockSpec((1, tm, f), lambda j, i: (j, i, 0))
    extra = [] if dep is None else [dep]
    return pl.pallas_call(
        body, name=name, grid=(nblk, s // tm),
        in_specs=[pl.BlockSpec((tm, d), lambda j, i: (i, 0)), w_spec, w_spec] + [pl.BlockSpec(memory_space=pl.ANY)] * len(extra),
        out_specs=[o_spec] * 3,
        out_shape=[SDS((nblk, s, f), F32), SDS((nblk, s, f), F32), SDS((nblk, s, f), BF16)],
        compiler_params=pltpu.CompilerParams(dimension_semantics=("parallel", "parallel"), vmem_limit_bytes=VMEM_LIMIT),
    )(h, wgt, wut, *extra)


def _ffn_down_dx(dx_bf, wd, gate, up, name, dep=None):
    s, d = dx_bf.shape
    nblk, f, _ = wd.shape
    tm = _pick(s, (1024, 512, 256))
    dn = (((1,), (1,)), ((), ()))

    def body(dx_ref, wd_ref, g_ref, u_ref, *rest):
        dg_ref, du_ref = rest[-2:]
        dact = 0.5 * lax.dot_general(dx_ref[...], wd_ref[0], dn, preferred_element_type=F32)
        _, vjp = jax.vjp(_swiglu_act, g_ref[0], u_ref[0])
        dg, du = vjp(dact)
        dg_ref[0], du_ref[0] = dg.astype(BF16), du.astype(BF16)

    o_spec = pl.BlockSpec((1, tm, f), lambda j, i: (j, i, 0))
    extra = [] if dep is None else [dep]
    return pl.pallas_call(
        body, name=name, grid=(nblk, s // tm),
        in_specs=[pl.BlockSpec((tm, d), lambda j, i: (i, 0)), pl.BlockSpec((1, f, d), lambda j, i: (j, 0, 0)), o_spec, o_spec]
        + [pl.BlockSpec(memory_space=pl.ANY)] * len(extra),
        out_specs=[o_spec] * 2, out_shape=[SDS((nblk, s, f), BF16)] * 2,
        compiler_params=pltpu.CompilerParams(dimension_semantics=("parallel", "parallel"), vmem_limit_bytes=VMEM_LIMIT),
    )(dx_bf, wd, gate, up, *extra)


def _ffn_fwd(x, gain, wgt, wut, wd, tag, h=None, dep=None):
    if h is None:
        h = _rows_call(_rms, [(x, 0, D_MODEL)], [gain], [(D_MODEL, BF16)], tm=512, name=f"{tag}_norm")[0]
    gate, up, act = _ffn_gate_up(h, wgt, wut, f"{tag}_gate_up", dep=dep)
    x_new = _mm(act, wd, sum_blocks=True, res=x, alpha=0.5, name=f"{tag}_down")
    return x_new, (x, h, gate, up, act)


def _ffn_bwd(dx_new, dx_new_bf, saved, gain, wgt, wut, wd, tag, dep=None, hooks=None):
    x, h, gate, up, act = saved
    hooks = hooks or {}

    def hook(name, *vals):
        return hooks[name](*vals) if name in hooks else None

    d_wd = _mm(act, dx_new_bf, ta=True, alpha=0.5, name=f"{tag}_down_dw")
    dep = hook("down", d_wd) if "down" in hooks else dep
    dgate, dup = _ffn_down_dx(dx_new_bf, wd, gate, up, f"{tag}_down_dx", dep=dep)
    d_wgt = _mm(dgate, h, ta=True, dep=hook("mid", dgate), name=f"{tag}_gate_dw")
    d_wut = _mm(dup, h, ta=True, name=f"{tag}_up_dw")
    dh = _mm(dgate, wgt, sum_blocks=True, dep=hook("dw", d_wgt, d_wut), name=f"{tag}_gate_dx")
    dx, dx_bf, dgain = _mm(dup, wut, sum_blocks=True, res=dh, dep=hook("dx", dh), post=_norm_bwd_post(x, gain, dx_new),
                           name=f"{tag}_up_dx")
    hook("end", dx_bf)
    return dx, dx_bf, dgain, d_wgt, d_wut, d_wd


def _norm_bwd_post(x, gain, dres):
    def f(dht, xt, drt, gt):
        _, vjp = jax.vjp(_rms, xt, gt)
        dxt, dgt = vjp(dht)
        return dxt + drt, dxt + drt, dgt

    return f, [x, dres], [gain], [F32, BF16], [(1, D_MODEL)]


def kernel(x, p, positions, ffn1_norm, ffn1_w_gate, ffn1_w_up, ffn1_w_down, mix_norm, w_in, rwkv_mu, rwkv_w0, rwkv_w2, rwkv_a0, rwkv_a2, rwkv_g2, rwkv_k_k, rwkv_k_a, rwkv_r_k, rwkv_gn_w, rwkv_gn_b, q_norm, k_norm, w_br_rwkv, w_br_attn, w_out, ffn2_norm, ffn2_w_gate, ffn2_w_up, ffn2_w_down, ple_norm, ple_w_gate, ple_w_proj, loss_target, m_ffn1_norm, m_ffn1_w_gate, m_ffn1_w_up, m_ffn1_w_down, m_mix_norm, m_w_in, m_rwkv_mu, m_rwkv_w0, m_rwkv_w2, m_rwkv_a0, m_rwkv_a2, m_rwkv_g2, m_rwkv_k_k, m_rwkv_k_a, m_rwkv_r_k, m_rwkv_gn_w, m_rwkv_gn_b, m_q_norm, m_k_norm, m_w_br_rwkv, m_w_br_attn, m_w_out, m_ffn2_norm, m_ffn2_w_gate, m_ffn2_w_up, m_ffn2_w_down, m_ple_norm, m_ple_w_gate, m_ple_w_proj, v_ffn1_norm, v_ffn1_w_gate, v_ffn1_w_up, v_ffn1_w_down, v_mix_norm, v_w_in, v_rwkv_mu, v_rwkv_w0, v_rwkv_w2, v_rwkv_a0, v_rwkv_a2, v_rwkv_g2, v_rwkv_k_k, v_rwkv_k_a, v_rwkv_r_k, v_rwkv_gn_w, v_rwkv_gn_b, v_q_norm, v_k_norm, v_w_br_rwkv, v_w_br_attn, v_w_out, v_ffn2_norm, v_ffn2_w_gate, v_ffn2_w_up, v_ffn2_w_down, v_ple_norm, v_ple_w_gate, v_ple_w_proj):
    args = dict(locals())
    wts = {n: args[n] for n in WEIGHTS}
    mom_m = {n: args["m_" + n] for n in WEIGHTS}
    mom_v = {n: args["v_" + n] for n in WEIGHTS}
    x0, tgt = x[0], loss_target[0]
    s = x0.shape[0]
    p_tok = p[0, 0]

    vec = {n: wts[n].reshape(1, -1) for n, _ in SMALL}
    xi, yi, ci = _coords()
    me = 2 * xi + yi
    def laid(t, n):
        return jnp.transpose(t[n][0]) if n in TRANSPOSED else t[n][0]

    shard_of = {n: laid(wts, n) for g in GROUPS.values() for n, _, _, _ in g if n != "lora"}
    shard_of["lora"] = _lora_stack({n: wts[n][0] for n, _ in LORA})

    def whole_with_own(n, kind, r, c, tok=None):
        at = (me, 0, 0) if kind == "blk" else (0, me * c)
        own = (shard_of[n] if tok is None else shard_of[n] + tok[0, 0]).astype(BF16)
        return lax.dynamic_update_slice(lax.empty(_form(kind, r, c), BF16), own[None] if kind == "blk" else own, at)

    specs = {g: [(kind, r, c) for _, kind, r, c in grp] for g, grp in GROUPS.items()}
    plans = {(g, st): _gather_plan(specs[g], st) for g in GROUPS for st in ("ici", "d2d")}
    buf_f1 = [whole_with_own(*w) for w in GROUPS["f1"]]
    ss_0, rs_0, buf_f1, tok_0 = _copies_start("gather_f1_ici", buf_f1, 3 * len(buf_f1), plans["f1", "ici"][0])
    bufs = {g: [whole_with_own(*w, tok=tok_0) for w in GROUPS[g]] for g in ("mx", "f2")}
    buf_f1 = _copies_wait("gather_f1_ici_wait", buf_f1, ss_0, rs_0, bufs["mx"][0], plans["f1", "ici"][1])
    ss_1, rs_1, buf_f1, tok_1 = _copies_start("gather_f1_d2d", buf_f1, 3 * len(buf_f1), plans["f1", "d2d"][0])
    h1 = _rows_call(_rms, [(x0, 0, D_MODEL)], [vec["ffn1_norm"] + tok_1[0, 0]], [(D_MODEL, BF16)], tm=512, name="ffn1_norm")[0]
    buf_f1 = _copies_wait("gather_f1_d2d_wait", buf_f1, ss_1, rs_1, h1, plans["f1", "d2d"][1])
    wb = dict(zip([w[0] for w in GROUPS["f1"]], buf_f1))
    ss_a, rs_a, buf_mx, tok_a = _copies_start("gather_mx_ici", bufs["mx"], 3 * len(bufs["mx"]), plans["mx", "ici"][0],
                                              after=wb["ffn1_w_gate"])

    inv_freq = 1.0 / (ROPE_THETA ** (jnp.arange(0, HEAD, 2, dtype=F32) / HEAD))
    ang = positions[0].astype(F32)[:, None] * inv_freq
    cos, sin = jnp.cos(ang), jnp.sin(ang)
    cos2, sin2 = jnp.concatenate([cos, cos], axis=1), jnp.concatenate([-sin, sin], axis=1)

    x1, ffn1_saved = _ffn_fwd(x0, vec["ffn1_norm"], wb["ffn1_w_gate"], wb["ffn1_w_up"], wb["ffn1_w_down"], "ffn1", h=h1, dep=tok_a)
    buf_mx = _copies_wait("gather_mx_ici_wait", buf_mx, ss_a, rs_a, x1, plans["mx", "ici"][1])
    ss_b, rs_b, buf_mx, tok_b = _copies_start("gather_mx_d2d", buf_mx, 3 * len(buf_mx), plans["mx", "d2d"][0])
    ss_c, rs_c, buf_f2, tok_c = _copies_start("gather_f2_ici", bufs["f2"], 3 * len(bufs["f2"]), plans["f2", "ici"][0])
    h = _rows_call(_rms, [(x1, 0, D_MODEL)], [vec["mix_norm"] + (tok_b[0, 0] + tok_c[0, 0])], [(D_MODEL, BF16)], tm=256,
                   name="mix_norm")[0]
    buf_mx = _copies_wait("gather_mx_d2d_wait", buf_mx, ss_b, rs_b, h, plans["mx", "d2d"][1])
    wb.update(zip([w[0] for w in GROUPS["mx"]], buf_mx))
    w_in_all = wb["w_in"]
    w_in_r, w_in_a, w_in_g = w_in_all[:, :RWKV_COLS], w_in_all[:, RWKV_COLS:RWKV_COLS + ATTN_COLS], w_in_all[:, RWKV_COLS + ATTN_COLS:]
    lora = _lora_split(wb["lora"])
    w2, a2, g2 = lora["rwkv_w2"], lora["rwkv_a2"], lora["rwkv_g2"]
    z_r = _mm(h, w_in_r, name="in_rwkv")
    z_a = _mm(h, w_in_a, name="in_attn")
    z_g = _mm(h, w_in_g, name="in_gate")

    zs = _shift_fwd(z_r, vec["rwkv_mu"])
    pre_params = [vec["rwkv_w0"], w2, vec["rwkv_a0"], a2, g2, vec["rwkv_k_k"], vec["rwkv_k_a"]]
    def pre_fwd(*t):
        res = _rwkv_pre(*t)
        return res[1], res[2], res[4], res[5], res[6]

    lw, k2, na, kb, gate_r = _rows_call(pre_fwd, [(zs, 0, RWKV_COLS)], pre_params, [(RWKV_DIM, F32)] * 5, tm=512, name="rwkv_pre")
    y_scan, s0s, invs = _wkv_fwd(zs, lw, k2, na, kb)
    buf_f2 = _copies_wait("gather_f2_ici_wait", buf_f2, ss_c, rs_c, y_scan, plans["f2", "ici"][1])
    ss_d, rs_d, buf_f2, tok_d = _copies_start("gather_f2_d2d", buf_f2, 3 * len(buf_f2), plans["f2", "d2d"][0])
    post_params = [vec["rwkv_gn_w"] + tok_d[0, 0], vec["rwkv_gn_b"], vec["rwkv_r_k"]]
    post_rows = [(y_scan, 0, RWKV_DIM), (zs, 0, RWKV_DIM), (k2, 0, RWKV_DIM), (zs, 2, RWKV_DIM), (gate_r, 0, RWKV_DIM)]
    y_rwkv = _rows_call(_rwkv_post, post_rows, post_params, [(RWKV_DIM, BF16)], tm=512, name="rwkv_post")[0]
    buf_f2 = _copies_wait("gather_f2_d2d_wait", buf_f2, ss_d, rs_d, y_rwkv, plans["f2", "d2d"][1])
    wb.update(zip([w[0] for w in GROUPS["f2"]], buf_f2))
    w_brr, w_bra = wb["w_br_rwkv"], wb["w_br_attn"]
    w_o = wb["w_out"].reshape(D_MODEL, D_MODEL)
    w_pp, w_pg = wb["ple_w_proj"], wb["ple_w_gate"].reshape(D_MODEL, D_MODEL)

    def qk_fwd(qt, kt, ct, st, qg, kg):
        return _norm_rope(qt, qg, ct, st), _norm_rope(kt, kg, ct, st)

    qk_rows = [(z_a, 0, ATTN_DIM), (z_a, 1, ATTN_DIM), (cos2, 0, HEAD), (sin2, 0, HEAD)]
    q_rot, k_rot = _rows_call(qk_fwd, qk_rows, [vec["q_norm"], vec["k_norm"]], [(ATTN_DIM, BF16)] * 2, tm=512, name="attn_pre")
    def group(t, g, off=0):
        return t[:, off + g * GROUP_DIM:off + (g + 1) * GROUP_DIM].astype(BF16)

    qkv = [(group(q_rot, g), group(k_rot, g), group(z_a, g, 2 * ATTN_DIM)) for g in range(len(ATTN_DILATIONS))]
    outs, lses = zip(*[_attn_fwd(*qkv[g], d) for g, d in enumerate(ATTN_DILATIONS)])
    comb_rows = [(t, 0, GROUP_DIM) for t in outs + lses]
    y_attn = _rows_call(_attn_combine, comb_rows, [], [(GROUP_DIM, BF16)], tm=512, name="attn_combine")[0]

    br = _mm(y_rwkv, w_brr, name="branch_rwkv")
    ba = _mm(y_attn, w_bra, name="branch_attn")
    merge_rows = [(z_g, 0, D_MODEL), (z_g, 1, D_MODEL), (br, 0, D_MODEL), (ba, 0, D_MODEL)]
    merged = _rows_call(_merge, merge_rows, [], [(D_MODEL, BF16)], tm=512, name="merge")[0]
    x2 = _mm(merged, w_o, res=x1, name="out_proj")
    x3, ffn2_saved = _ffn_fwd(x2, vec["ffn2_norm"], wb["ffn2_w_gate"], wb["ffn2_w_up"], wb["ffn2_w_down"], "ffn2")
    hp = _rows_call(_rms, [(x3, 0, D_MODEL)], [vec["ple_norm"]], [(D_MODEL, BF16)], tm=512, name="ple_norm")[0]
    pg = _mm(hp, w_pg, name="ple_gate")
    pp = _mm(p_tok, w_pp, name="ple_proj")

    def head(x3t, pgt, ppt, tt):
        sg = _sigmoid(pgt)
        err = x3t + sg * ppt - tt
        dx4 = err * (1.0 / D_MODEL)
        loss = 0.5 * jnp.sum(jnp.mean(err * err, axis=-1, keepdims=True), axis=0, keepdims=True)
        return dx4, dx4 * ppt * sg * (1.0 - sg), dx4 * sg, jnp.broadcast_to(loss, (8, 128))

    head_rows = [(x3, 0, D_MODEL), (pg, 0, D_MODEL), (pp, 0, D_MODEL), (tgt, 0, D_MODEL)]
    dx4, dpg, dpp, loss_tile = _rows_call(head, head_rows, [], [(D_MODEL, F32), (D_MODEL, BF16), (D_MODEL, BF16)], [(8, 128)],
                                          tm=512, name="ple_loss")

    c_arr = jnp.reshape(ci, (1,)).astype(jnp.int32)
    mc_arr = jnp.stack([me, ci]).astype(jnp.int32)
    red = {g: _GroupReduce(g, grp, c_arr, mc_arr) for g, grp in REDUCE_GROUPS.items()}

    def adam_state(names):
        out = {}
        for n in names:
            if n == "lora":
                out[n] = tuple(_lora_stack({k: t[k][0] for k, _ in LORA}) for t in (wts, mom_m, mom_v))
            else:
                out[n] = (laid(wts, n), laid(mom_m, n), laid(mom_v, n))
        return out

    adam = {g: adam_state([w[0] for w in grp]) for g, grp in REDUCE_GROUPS.items()}
    done = {}
    gw, gs = {}, {}
    gw["ple_w_proj"] = _mm(p_tok, dpp, ta=True, name="ple_proj_dw")
    gw["ple_w_gate"] = _mm(hp, dpg, ta=True, name="ple_gate_dw")
    dx3, dx3_bf, gs["ple_norm"] = _mm(dpg, w_pg, tb=True, post=_norm_bwd_post(x3, vec["ple_norm"], dx4), name="ple_gate_dx")
    dx2, dx2_bf, gs["ffn2_norm"], gw["ffn2_w_gate"], gw["ffn2_w_up"], gw["ffn2_w_down"] = _ffn_bwd(
        dx3, dx3_bf, ffn2_saved, vec["ffn2_norm"], wb["ffn2_w_gate"], wb["ffn2_w_up"], wb["ffn2_w_down"], "ffn2")
    gw["ple_w_gate"] = gw["ple_w_gate"].reshape(N_CHIPS, D_MODEL // N_CHIPS, D_MODEL)
    tok = red["f2"].swap_start([gw[w[0]] for w in REDUCE_GROUPS["f2"]])
    gw["w_out"] = _mm(merged, dx2_bf, ta=True, name="out_proj_dw")
    dmerged = _mm(dx2_bf, w_o, tb=True, dep=tok, name="out_proj_dx")

    def merge_bwd(zgr, zga, brt, bat, ct):
        _, vjp = jax.vjp(_merge, zgr, zga, brt, bat)
        d1, d2, d3, d4 = vjp(ct)
        return jnp.concatenate([d1, d2], axis=1), d3, d4

    dz_g, dbr, dba = _rows_call(merge_bwd, merge_rows + [(dmerged, 0, D_MODEL)], [],
                                [(2 * D_MODEL, BF16), (D_MODEL, BF16), (D_MODEL, BF16)], tm=512, name="merge_bwd")
    tok = red["f2"].swap_wait_ici_start(dz_g)
    gw["w_br_rwkv"] = _mm(y_rwkv, dbr, ta=True, name="branch_rwkv_dw")
    gw["w_br_attn"] = _mm(y_attn, dba, ta=True, name="branch_attn_dw")
    dy_rwkv = _mm(dbr, w_brr, tb=True, dep=tok, name="branch_rwkv_dx")
    dy_attn = _mm(dba, w_bra, tb=True, dep=tok, name="branch_attn_dx")

    def comb_bwd(*t):
        _, vjp = jax.vjp(_attn_combine, *t[:6])
        return vjp(t[6])

    dcomb = _rows_call(comb_bwd, comb_rows + [(dy_attn, 0, GROUP_DIM)], [], [(GROUP_DIM, F32)] * 6, tm=512, name="attn_combine_bwd")
    dqs, dks, dvs = zip(*[_attn_bwd(*qkv[g], d, dcomb[g], dcomb[3 + g]) for g, d in enumerate(ATTN_DILATIONS)])

    def qk_bwd(qt, kt, ct, st, *rest):
        dq = jnp.concatenate(rest[0:3], axis=1)
        dk = jnp.concatenate(rest[3:6], axis=1)
        qg, kg = rest[9], rest[10]
        _, vjp = jax.vjp(lambda a_, b_, c_, d_: qk_fwd(a_, b_, ct, st, c_, d_), qt, kt, qg, kg)
        dqt, dkt, dqg, dkg = vjp((dq, dk))
        return jnp.concatenate((dqt, dkt) + tuple(rest[6:9]), axis=1), dqg, dkg

    dz_a, gs["q_norm"], gs["k_norm"] = _rows_call(
        qk_bwd, qk_rows + [(t, 0, GROUP_DIM) for t in dqs + dks + dvs], [vec["q_norm"], vec["k_norm"]],
        [(ATTN_COLS, BF16)], [(1, HEAD), (1, HEAD)], tm=512, name="attn_pre_bwd")
    tok = red["f2"].ici_wait_join_start(dz_a, adam["f2"])

    def post_bwd(*t):
        _, vjp = jax.vjp(_rwkv_post, *t[:5], *t[6:])
        return vjp(t[5])

    dy_scan, dr_post, dk2_post, dv_post, dgate_r, gs["rwkv_gn_w"], gs["rwkv_gn_b"], gs["rwkv_r_k"] = _rows_call(
        post_bwd, post_rows + [(dy_rwkv, 0, RWKV_DIM)], post_params, [(RWKV_DIM, F32)] * 5, [(1, RWKV_DIM)] * 3,
        tm=512, name="rwkv_post_bwd", dep=tok)
    done.update(red["f2"].join_wait(dy_scan))
    dr_s, dlw, dk2_s, dv_s, dna, dkb = _wkv_bwd(zs, lw, k2, na, kb, s0s, invs, dy_scan)

    def pre_bwd(zt, c_r1, c_r2, c_lw, c_k1, c_k2, c_v1, c_v2, c_a, c_b, c_g, *params):
        _, vjp = jax.vjp(_rwkv_pre, zt, *params)
        return vjp((c_r1 + c_r2, c_lw, c_k1 + c_k2, c_v1 + c_v2, c_a, c_b, c_g))

    pre_cts = [dr_s, dr_post, dlw, dk2_s, dk2_post, dv_s, dv_post, dna, dkb, dgate_r]
    dzs, gs["rwkv_w0"], g_w2, gs["rwkv_a0"], g_a2, g_g2, gs["rwkv_k_k"], gs["rwkv_k_a"] = _rows_call(
        pre_bwd, [(zs, 0, RWKV_COLS)] + [(t, 0, RWKV_DIM) for t in pre_cts], pre_params, [(RWKV_COLS, F32)],
        [q.shape for q in pre_params], tm=512, name="rwkv_pre_bwd")
    dz_r, gs["rwkv_mu"] = _shift_bwd(z_r, vec["rwkv_mu"], dzs)

    g_w_in = jnp.concatenate([_mm(h, dz_r, ta=True, name="in_rwkv_dw"), _mm(h, dz_a, ta=True, name="in_attn_dw"),
                              _mm(h, dz_g, ta=True, name="in_gate_dw")], axis=1)
    gw["w_in"], gw["lora"] = g_w_in, jnp.concatenate([g_w2, g_a2, g_g2], axis=0)
    gw["w_out"] = gw["w_out"].reshape(N_CHIPS, D_MODEL // N_CHIPS, D_MODEL)
    tok = red["mx"].swap_start([gw[w[0]] for w in REDUCE_GROUPS["mx"]])
    dh = _mm(dz_r, w_in_r, tb=True, dep=tok, name="in_rwkv_dx")
    dh = _mm(dz_a, w_in_a, tb=True, res=dh, name="in_attn_dx")
    dx1, dx1_bf, gs["mix_norm"] = _mm(dz_g, w_in_g, tb=True, res=dh, post=_norm_bwd_post(x1, vec["mix_norm"], dx2), name="in_gate_dx")
    tok_mx = red["mx"].swap_wait_ici_start(dx1_bf)
    hooks = {"down": lambda d_wd: red["f1d"].swap_start([d_wd], after=tok_mx),
             "mid": lambda dgate: red["f1d"].swap_wait_ici_start(dgate),
             "dw": lambda d_wgt, d_wut: red["f1g"].swap_start([d_wgt, d_wut]),
             "dx": lambda part: red["f1g"].swap_wait_ici_start(part) + red["f1d"].ici_wait_join_start(part, adam["f1d"]),
             "end": lambda dx_: tokens.setdefault("mx_join", red["mx"].ici_wait_join_start(dx_, adam["mx"]))}
    tokens = {}
    dx0, _, gs["ffn1_norm"], gw["ffn1_w_gate"], gw["ffn1_w_up"], gw["ffn1_w_down"] = _ffn_bwd(
        dx1, dx1_bf, ffn1_saved, vec["ffn1_norm"], wb["ffn1_w_gate"], wb["ffn1_w_up"], wb["ffn1_w_down"], "ffn1", hooks=hooks)

    flat = jnp.concatenate([gs[n].reshape(-1) for n, _ in SMALL] + [loss_tile[0, 0:1]])
    small_buf = jnp.pad(flat, (0, SMALL_ROWS * PACK_COLS - flat.shape[0])).reshape(SMALL_ROWS, PACK_COLS)
    small_sum = _all_reduce_small(small_buf)
    n_small = sum(sz for _, sz in SMALL)
    loss = small_sum.reshape(-1)[n_small]
    grad_small = _unpack_small(small_sum, {n: wts[n].shape for n, _ in SMALL})
    d_s, m_s, v_s = _adamw(_pack_small(wts), small_sum, _pack_small(mom_m), _pack_small(mom_v), name="adamw_small",
                           dep=tokens["mx_join"])
    shapes = {n: wts[n].shape for n, _ in SMALL}
    d_s, m_s, v_s = _unpack_small(d_s, shapes), _unpack_small(m_s, shapes), _unpack_small(v_s, shapes)
    grads, deltas, new_m, new_v = {}, {}, {}, {}
    for n, _ in SMALL:
        grads[n], deltas[n], new_m[n], new_v[n] = grad_small[n], d_s[n], m_s[n], v_s[n]

    tok = red["f1g"].ici_wait_join_start(m_s["ffn1_norm"], adam["f1g"])
    for g in ("mx", "f1d", "f1g"):
        done.update(red[g].join_wait(tok))
    for n, res in done.items():
        for store, val in zip((grads, deltas, new_m, new_v), res):
            if n == "lora":
                store.update({k: t[None] for k, t in _lora_split(val).items()})
            else:
                store[n] = (jnp.transpose(val) if n in TRANSPOSED else val)[None]

    return (loss, dx0[None], *[grads[n] for n in WEIGHTS], *[deltas[n] for n in WEIGHTS],
            *[new_m[n] for n in WEIGHTS], *[new_v[n] for n in WEIGHTS])
```

```python
import functools

import jax
import jax.numpy as jnp
from jax import lax
from jax.experimental import pallas as pl
from jax.experimental.pallas import tpu as pltpu

F32, BF16 = jnp.float32, jnp.bfloat16
HI = lax.Precision.HIGHEST
MESH = pl.DeviceIdType.MESH
SDS = jax.ShapeDtypeStruct

D_MODEL = 1024
HEAD = 64
RWKV_HEADS = 8
RWKV_DIM = RWKV_HEADS * HEAD
DECAY_LORA, ICLR_LORA, GATE_LORA = 64, 64, 128
GN_EPS = 64e-5
RMS_EPS = 1e-6
ATTN_DILATIONS = (1, 4, 16)
BAND = 128
ATTN_DIM = 768
GROUP_DIM = 256
ROPE_THETA = 10000.0
NEG_INF = -1e30
RWKV_COLS = 3 * RWKV_DIM + DECAY_LORA + ICLR_LORA + GATE_LORA
ATTN_COLS = 3 * ATTN_DIM
ADAM_LR, ADAM_B1, ADAM_B2, ADAM_EPS, ADAM_WD, ADAM_STEP = 0.001, 0.9, 0.999, 1e-08, 0.01, 10

WKV_CHUNK = 64
WKV_HEADS_PER_STEP = 8
N_CHIPS = 4
PACK_COLS = 1024
VMEM_LIMIT = 48 * 1024 * 1024

TRANSPOSED = ("ffn1_w_gate", "ffn1_w_up", "ffn2_w_gate", "ffn2_w_up")
LORA = (("rwkv_w2", 64), ("rwkv_a2", 64), ("rwkv_g2", 128))
_FFN1 = (("ffn1_w_gate", "blk", 704, 1024), ("ffn1_w_up", "blk", 704, 1024), ("ffn1_w_down", "blk", 704, 1024))
_FFN2 = (("ffn2_w_gate", "blk", 704, 1024), ("ffn2_w_up", "blk", 704, 1024), ("ffn2_w_down", "blk", 704, 1024))
_IN = (("w_in", "col", 1024, 1536), ("lora", "col", 256, 128))
_BRANCH = (("w_br_rwkv", "col", 512, 256), ("w_br_attn", "col", 256, 256), ("w_out", "blk", 256, 1024))
_PLE = (("ple_w_gate", "blk", 256, 1024), ("ple_w_proj", "col", 256, 256))
GROUPS = {"f1": _FFN1, "mx": _IN, "f2": _BRANCH + _FFN2 + _PLE}
REDUCE_GROUPS = {"f2": _FFN2 + _PLE, "mx": _IN + _BRANCH, "f1d": _FFN1[2:], "f1g": _FFN1[:2]}
SMALL = (
    ("ffn1_norm", 1024), ("mix_norm", 1024), ("ffn2_norm", 1024), ("ple_norm", 1024), ("rwkv_mu", 1792),
    ("rwkv_w0", 512), ("rwkv_a0", 512), ("rwkv_k_k", 512), ("rwkv_k_a", 512), ("rwkv_r_k", 512),
    ("rwkv_gn_w", 512), ("rwkv_gn_b", 512), ("q_norm", 64), ("k_norm", 64),
)
SMALL_ROWS = 16
WEIGHTS = (
    "ffn1_norm", "ffn1_w_gate", "ffn1_w_up", "ffn1_w_down", "mix_norm", "w_in", "rwkv_mu", "rwkv_w0", "rwkv_w2",
    "rwkv_a0", "rwkv_a2", "rwkv_g2", "rwkv_k_k", "rwkv_k_a", "rwkv_r_k", "rwkv_gn_w", "rwkv_gn_b", "q_norm", "k_norm",
    "w_br_rwkv", "w_br_attn", "w_out", "ffn2_norm", "ffn2_w_gate", "ffn2_w_up", "ffn2_w_down", "ple_norm",
    "ple_w_gate", "ple_w_proj",
)


def _row_tile(n, most=704):
    for t in range(most - most % 16, 0, -16):
        if n % t == 0:
            return t
    return n


def _pick(n, cands):
    for c in cands:
        if n % c == 0:
            return c
    return n


def _mm(a, b, *, ta=False, tb=False, sum_blocks=False, out_dtype=F32, res=None, alpha=1.0, dep=None, post=None, name):
    flat = a.ndim == 2 and b.ndim == 2
    a3 = a if a.ndim == 3 else a[None]
    b3 = b if b.ndim == 3 else b[None]
    na, nbb = a3.shape[0], b3.shape[0]
    nblk = max(na, nbb)
    kdim, m = (a3.shape[1], a3.shape[2]) if ta else (a3.shape[2], a3.shape[1])
    n = b3.shape[1] if tb else b3.shape[2]
    assert (b3.shape[2] if tb else b3.shape[1]) == kdim
    tm = _pick(m, (1024, 512, 256, 128) if post is None else (512, 256, 128))
    tn = _pick(n, (1024, 896, 768, 512, 256, 128))
    tk = kdim if kdim <= 2304 else _pick(kdim, (1024, 512, 256, 128))
    nk = kdim // tk
    direct = nk == 1 and not sum_blocks

    if sum_blocks:
        grid = (m // tm, n // tn, nblk, nk)

        def ids(i, c, j, k):
            return i, c, j, k
    else:
        grid = (nblk, m // tm, n // tn, nk)

        def ids(j, i, c, k):
            return i, c, j, k

    def amap(*g):
        i, c, j, k = ids(*g)
        jj = j if na > 1 else 0
        return (jj, k, i) if ta else (jj, i, k)

    def bmap(*g):
        i, c, j, k = ids(*g)
        jj = j if nbb > 1 else 0
        return (jj, c, k) if tb else (jj, k, c)

    if sum_blocks:
        oshape, oblk = (m, n), (tm, tn)

        def omap(*g):
            i, c, j, k = ids(*g)
            return i, c
    else:
        oshape, oblk = (nblk, m, n), (1, tm, tn)

        def omap(*g):
            i, c, j, k = ids(*g)
            return j, i, c

    dn = (((0 if ta else 1,), (1 if tb else 0,)), ((), ()))
    has_res = res is not None
    p_f, p_rows, p_params, p_dtypes, p_accs = post if post is not None else (None, [], [], [], [])
    assert post is None or sum_blocks or flat
    n_in = 2 + has_res + len(p_rows) + len(p_params) + (dep is not None)

    def tile_map(*g):
        i, c, j, k = ids(*g)
        return i, c

    def body(*refs):
        refs = list(refs)
        acc = None if direct else refs.pop()
        o_refs = refs[n_in:]
        a_ref, b_ref = refs[0], refs[1]
        r_ref = refs[2] if has_res else None
        pr_refs = refs[2 + has_res:2 + has_res + len(p_rows)]
        pp_refs = refs[2 + has_res + len(p_rows):2 + has_res + len(p_rows) + len(p_params)]
        first_tile = jnp.logical_and(pl.program_id(0 if sum_blocks else 1) == 0, pl.program_id(1 if sum_blocks else 2) == 0)

        def finish(v):
            if alpha != 1.0:
                v = v * alpha
            if has_res:
                v = v + r_ref[...].reshape(v.shape).astype(F32)
            if post is None:
                o_refs[0][...] = v.reshape(o_refs[0].shape).astype(o_refs[0].dtype)
                return
            outs = p_f(v, *[t[...] for t in pr_refs], *[t[...] for t in pp_refs])
            for o_ref, val in zip(o_refs, outs[:len(p_dtypes)]):
                o_ref[...] = val.astype(o_ref.dtype)
            for o_ref, val in zip(o_refs[len(p_dtypes):], outs[len(p_dtypes):]):
                @pl.when(first_tile)
                def _():
                    o_ref[...] = jnp.zeros_like(o_ref)

                o_ref[...] += val.reshape(o_ref.shape)

        if direct:
            finish(lax.dot_general(a_ref[0].astype(BF16), b_ref[0].astype(BF16), dn, preferred_element_type=F32))
            return
        k = pl.program_id(3)
        if sum_blocks:
            j = pl.program_id(2)
            first = jnp.logical_and(j == 0, k == 0)
            last = jnp.logical_and(j == nblk - 1, k == nk - 1)
        else:
            first, last = k == 0, k == nk - 1

        @pl.when(first)
        def _():
            acc[...] = jnp.zeros_like(acc)

        acc[...] += lax.dot_general(a_ref[0].astype(BF16), b_ref[0].astype(BF16), dn, preferred_element_type=F32)

        @pl.when(last)
        def _():
            finish(acc[...])

    in_specs = [pl.BlockSpec((1, tk, tm) if ta else (1, tm, tk), amap), pl.BlockSpec((1, tn, tk) if tb else (1, tk, tn), bmap)]
    args = [a3, b3]
    if has_res:
        res3 = res if (sum_blocks or res.ndim == 3) else res[None]
        in_specs.append(pl.BlockSpec(oblk, omap))
        args.append(res3)
    in_specs += [pl.BlockSpec((tm, tn), tile_map) for _ in p_rows]
    in_specs += [pl.BlockSpec(t.shape, functools.partial(lambda *g, nd: (0,) * nd, nd=t.ndim)) for t in p_params]
    args += list(p_rows) + list(p_params)
    if dep is not None:
        in_specs.append(pl.BlockSpec(memory_space=pl.ANY))
        args.append(dep)
    if post is None:
        out_specs, out_shape = pl.BlockSpec(oblk, omap), SDS(oshape, out_dtype)
        semantics = ("parallel", "parallel", "arbitrary", "arbitrary") if sum_blocks else ("parallel", "parallel", "parallel", "arbitrary")
    else:
        out_specs = [pl.BlockSpec((tm, tn), tile_map) for _ in p_dtypes]
        out_specs += [pl.BlockSpec(tuple(sh), functools.partial(lambda *g, nd: (0,) * nd, nd=len(sh))) for sh in p_accs]
        out_shape = [SDS((m, n), dt) for dt in p_dtypes] + [SDS(tuple(sh), F32) for sh in p_accs]
        semantics = ("arbitrary",) * 4
    out = pl.pallas_call(
        body,
        name=name,
        grid=grid,
        in_specs=in_specs,
        out_specs=out_specs,
        out_shape=out_shape,
        scratch_shapes=[] if direct else [pltpu.VMEM((tm, tn), F32)],
        compiler_params=pltpu.CompilerParams(dimension_semantics=semantics, vmem_limit_bytes=VMEM_LIMIT),
    )(*args)
    if post is not None:
        return out
    if flat and not sum_blocks:
        out = out[0]
    return out


def _rows_call(f, rows, params, outs, accs=(), *, tm, name, dep=None):
    s = rows[0][0].shape[0]
    nr, npar, no = len(rows), len(params), len(outs)
    nin = nr + npar + (0 if dep is None else 1)
    in_specs = [pl.BlockSpec((tm, w), functools.partial(lambda i, cb: (i, cb), cb=cb)) for (_, cb, w) in rows]
    in_specs += [pl.BlockSpec(p.shape, functools.partial(lambda i, nd: (0,) * nd, nd=p.ndim)) for p in params]
    if dep is not None:
        in_specs.append(pl.BlockSpec(memory_space=pl.ANY))
    out_shape = [SDS((s, w), dt) for (w, dt) in outs] + [SDS(tuple(sh), F32) for sh in accs]
    out_specs = [pl.BlockSpec((tm, w), lambda i: (i, 0)) for (w, _) in outs]
    out_specs += [pl.BlockSpec(tuple(sh), functools.partial(lambda i, nd: (0,) * nd, nd=len(sh))) for sh in accs]

    def body(*refs):
        rin, pin = refs[:nr], refs[nr:nr + npar]
        oo, ao = refs[nin:nin + no], refs[nin + no:]
        res = f(*[r[...] for r in rin], *[p[...] for p in pin])
        if not isinstance(res, (tuple, list)):
            res = (res,)
        for o_ref, v in zip(oo, res[:no]):
            o_ref[...] = v.astype(o_ref.dtype)
        i = pl.program_id(0)
        for a_ref, v in zip(ao, res[no:]):
            @pl.when(i == 0)
            def _():
                a_ref[...] = jnp.zeros_like(a_ref)

            a_ref[...] += v.reshape(a_ref.shape)

    res = pl.pallas_call(
        body,
        name=name,
        grid=(s // tm,),
        in_specs=in_specs,
        out_specs=out_specs,
        out_shape=out_shape,
        compiler_params=pltpu.CompilerParams(dimension_semantics=("arbitrary",), vmem_limit_bytes=VMEM_LIMIT),
    )(*[r[0] for r in rows], *params, *([] if dep is None else [dep]))
    return res


def _mmv(a, b, mode):
    ca = 0 if mode[0] == "t" else 1
    cb = 1 if mode[1] == "t" else 0
    return lax.dot_general(a.astype(BF16), b.astype(BF16), (((ca,), (cb,)), ((), ())), preferred_element_type=F32)


@functools.partial(jax.custom_vjp, nondiff_argnums=(2,))
def _bdot(a, b, mode):
    return _mmv(a, b, mode)


def _bdot_fwd(a, b, mode):
    return _mmv(a, b, mode), (a, b)


def _bdot_bwd(mode, saved, g):
    a, b = saved
    if mode == "nn":
        return _mmv(g, b, "nt"), _mmv(a, g, "tn")
    if mode == "nt":
        return _mmv(g, b, "nn"), _mmv(g, a, "tn")
    return _mmv(b, g, "nt"), _mmv(a, g, "nn")


_bdot.defvjp(_bdot_fwd, _bdot_bwd)


def _hdot(a, b, mode="nn", precision=HI):
    ca = 0 if mode[0] == "t" else 1
    cb = 1 if mode[1] == "t" else 0
    return lax.dot_general(a, b, (((ca,), (cb,)), ((), ())), precision=precision, preferred_element_type=F32)


def _segsum(x):
    c = x.shape[-1]
    blk = min(c, 256)
    r = lax.broadcasted_iota(jnp.int32, (blk, blk), 0) >> 6
    q = lax.broadcasted_iota(jnp.int32, (blk, blk), 1) >> 6
    ones = jnp.where(r == q, 1.0, 0.0).astype(F32)
    parts = [_hdot(x[:, i:i + blk], ones, precision=lax.Precision.HIGH) for i in range(0, c, blk)]
    return parts[0] if len(parts) == 1 else jnp.concatenate(parts, axis=1)


def _sigmoid(x):
    return jax.nn.sigmoid(x)


def _softplus(x):
    return jnp.maximum(x, 0.0) + jnp.log(1.0 + jnp.exp(-jnp.abs(x)))


def _rms(x, gain):
    return x * lax.rsqrt(jnp.mean(x * x, axis=-1, keepdims=True) + RMS_EPS) * gain


def _swiglu_act(gate, up):
    return gate * _sigmoid(gate) * up


def _rwkv_pre(zs, w0, w2, a0, a2, g2, k_k, k_a):
    r, k, v = zs[:, 0:512], zs[:, 512:1024], zs[:, 1024:1536]
    lora = zs[:, 1536:1792]
    wd, ad, gd = lora[:, 0:64], lora[:, 64:128], lora[:, 128:256]
    w = -_softplus(-(w0 + _bdot(jnp.tanh(wd), w2, "nn"))) - 0.5
    a = _sigmoid(a0 + _bdot(ad, a2, "nn"))
    g = _bdot(_sigmoid(gd), g2, "nn")
    kk = k * k_k
    kk = kk * lax.rsqrt(jnp.maximum(_segsum(kk * kk), 1e-24))
    k2 = k * (1.0 + (a - 1.0) * k_a)
    return r, -jnp.exp(w), k2, v, -kk, kk * a, g


def _rwkv_post(y, r, k2, v, g, gn_w, gn_b, r_k):
    mean = _segsum(y) * (1.0 / HEAD)
    yc = y - mean
    var = _segsum(yc * yc) * (1.0 / HEAD)
    yn = yc * lax.rsqrt(var + GN_EPS) * gn_w + gn_b
    bonus = _segsum(r * k2 * r_k) * v
    return (yn + bonus) * g


def _swap_halves(x):
    lane = lax.broadcasted_iota(jnp.int32, x.shape, 1)
    return jnp.where((lane & 32) == 0, jnp.roll(x, -32, axis=1), jnp.roll(x, 32, axis=1))


def _norm_rope(x, gain, cos, sin):
    heads = x.shape[1] // HEAD
    def rep(t):
        return jnp.concatenate([t] * heads, axis=1)

    xn = x * lax.rsqrt(_segsum(x * x) * (1.0 / HEAD) + RMS_EPS) * rep(gain)
    return xn * rep(cos) + _swap_halves(xn) * rep(sin)


def _attn_combine(o0, o1, o2, l0, l1, l2):
    m = jnp.maximum(jnp.maximum(l0, l1), l2)
    e0, e1, e2 = jnp.exp(l0 - m), jnp.exp(l1 - m), jnp.exp(l2 - m)
    return (e0 * o0 + e1 * o1 + e2 * o2) / (e0 + e1 + e2)


def _merge(zgr, zga, br, ba):
    return _sigmoid(zgr) * br + _sigmoid(zga) * ba


def _attn_block(q, kp, kc, vp, vc, has_prev):
    iq = lax.broadcasted_iota(jnp.int32, (1, BAND, BAND), 1)
    ik = lax.broadcasted_iota(jnp.int32, (1, BAND, BAND), 2)
    s_c = jnp.where(iq >= ik, _bdotb(q, kc, "nt") * (HEAD ** -0.5), NEG_INF)
    s_p = jnp.where(jnp.logical_and(iq <= ik, has_prev), _bdotb(q, kp, "nt") * (HEAD ** -0.5), NEG_INF)
    m = lax.stop_gradient(jnp.maximum(jnp.max(s_c, axis=-1, keepdims=True), jnp.max(s_p, axis=-1, keepdims=True)))
    e_c, e_p = jnp.exp(s_c - m), jnp.exp(s_p - m)
    l = jnp.sum(e_c, axis=-1, keepdims=True) + jnp.sum(e_p, axis=-1, keepdims=True)
    o = (_bdotb(e_c, vc) + _bdotb(e_p, vp)) / l
    return o, jnp.broadcast_to(m + jnp.log(l), o.shape)


def _mmb(a, b, cb):
    return lax.dot_general(a.astype(BF16), b.astype(BF16), (((2,), (cb,)), ((0,), (0,))), preferred_element_type=F32)


@functools.partial(jax.custom_vjp, nondiff_argnums=(2,))
def _bdotb1(a, b, cb):
    return _mmb(a, b, cb)


def _bdotb1_fwd(a, b, cb):
    return _mmb(a, b, cb), (a, b)


def _bdotb1_bwd(cb, saved, g):
    a, b = saved
    if cb == 1:
        return _mmb(g, b, 2), _mmb(jnp.swapaxes(a, 1, 2), g, 1)
    return _mmb(g, b, 1), _mmb(jnp.swapaxes(g, 1, 2), a, 1)


_bdotb1.defvjp(_bdotb1_fwd, _bdotb1_bwd)


def _bdotb(a, b, mode="nn", precision=None):
    if mode[0] == "t":
        a = jnp.swapaxes(a, 1, 2)
    cb = 2 if mode[1] == "t" else 1
    if precision is None:
        return _bdotb1(a, b, cb)
    return lax.dot_general(a, b, (((2,), (cb,)), ((0,), (0,))), precision=precision, preferred_element_type=F32)


def _tri_inv_levels(a):
    t = a.shape[-1]
    row = lax.broadcasted_iota(jnp.int32, (1, t, t), 1)
    col = lax.broadcasted_iota(jnp.int32, (1, t, t), 2)
    x = jnp.where(row == col, 1.0, 0.0).astype(F32) + jnp.where(jnp.logical_and(row == col + 1, (row & 1) == 1), a, 0.0)
    sh = 1
    while (1 << sh) < t:
        m = jnp.logical_and((row >> sh) == (col >> sh) + 1, (row >> (sh + 1)) == (col >> (sh + 1)))
        x = x + _bdotb(_bdotb(x, jnp.where(m, a, 0.0)), x)
        sh += 1
    return x


@jax.custom_vjp
def _tri_inv(a):
    return _tri_inv_levels(a)


def _tri_inv_fwd(a):
    x = _tri_inv_levels(a)
    return x, x


def _tri_inv_bwd(x, g):
    xt = jnp.swapaxes(x, 1, 2)
    return (_bdotb(_bdotb(xt, g, precision=lax.Precision.HIGH), xt, precision=lax.Precision.HIGH),)


_tri_inv.defvjp(_tri_inv_fwd, _tri_inv_bwd)


@jax.custom_vjp
def _known_inv(a, x):
    return x


def _known_inv_fwd(a, x):
    return x, x


def _known_inv_bwd(x, g):
    return _tri_inv_bwd(x, g)[0], jnp.zeros_like(x)


_known_inv.defvjp(_known_inv_fwd, _known_inv_bwd)


def _wkv_chunk(s0, r, lw, k, v, a, b, inv=None, with_inv=False):
    nh, t, _ = r.shape
    row = lax.broadcasted_iota(jnp.int32, (1, t, t), 1)
    col = lax.broadcasted_iota(jnp.int32, (1, t, t), 2)
    incl, strict = row >= col, row > col
    ones = jnp.broadcast_to(jnp.where(incl, 1.0, 0.0).astype(F32), (nh, t, t))
    cum = _bdotb(ones, lw, precision=HI)
    c_end = cum[:, t - 1:t, :]
    e_in, e_ex, e_inv = jnp.exp(cum), jnp.exp(cum - lw), jnp.exp(-cum)
    at, rt, bt, kt = a * e_ex, r * e_in, b * e_inv, k * e_inv
    a_ab = jnp.where(strict, _bdotb(at, bt, "nt"), 0.0)
    a_ak = jnp.where(strict, _bdotb(at, kt, "nt"), 0.0)
    x = _tri_inv(a_ab) if inv is None else _known_inv(a_ab, inv)
    u = _bdotb(x, _bdotb(at, s0, "nt") + _bdotb(a_ak, v))
    y = (_bdotb(rt, s0, "nt") + _bdotb(jnp.where(incl, _bdotb(rt, bt, "nt"), 0.0), u)
         + _bdotb(jnp.where(incl, _bdotb(rt, kt, "nt"), 0.0), v))
    w_end = jnp.exp(c_end - cum)
    s1 = s0 * jnp.exp(c_end) + _bdotb(u, b * w_end, "tn") + _bdotb(v, k * w_end, "tn")
    return (y, s1, x) if with_inv else (y, s1)


def _shift_fwd(z, mu):
    s, c = z.shape
    tc = 256

    def body(z_ref, mu_ref, o_ref):
        zz = z_ref[...]
        row = lax.broadcasted_iota(jnp.int32, zz.shape, 0)
        prev = jnp.where(row == 0, 0.0, pltpu.roll(zz, 1, 0))
        o_ref[...] = zz + (prev - zz) * mu_ref[...]

    return pl.pallas_call(
        body, name="shift_fwd", grid=(c // tc,),
        in_specs=[pl.BlockSpec((s, tc), lambda j: (0, j)), pl.BlockSpec((1, tc), lambda j: (0, j))],
        out_specs=pl.BlockSpec((s, tc), lambda j: (0, j)), out_shape=SDS((s, c), F32),
        compiler_params=pltpu.CompilerParams(dimension_semantics=("parallel",), vmem_limit_bytes=VMEM_LIMIT),
    )(z, mu)


def _shift_bwd(z, mu, dzs):
    s, c = z.shape
    tc = 256

    def body(z_ref, mu_ref, d_ref, dz_ref, dmu_ref):
        zz, d, m = z_ref[...], d_ref[...], mu_ref[...]
        row = lax.broadcasted_iota(jnp.int32, zz.shape, 0)
        prev = jnp.where(row == 0, 0.0, pltpu.roll(zz, 1, 0))
        t = d * m
        nxt = jnp.where(row == s - 1, 0.0, pltpu.roll(t, s - 1, 0))
        dz_ref[...] = (d - t + nxt).astype(dz_ref.dtype)
        dmu_ref[...] = jnp.sum(d * (prev - zz), axis=0, keepdims=True)

    return pl.pallas_call(
        body, name="shift_bwd", grid=(c // tc,),
        in_specs=[pl.BlockSpec((s, tc), lambda j: (0, j)), pl.BlockSpec((1, tc), lambda j: (0, j)),
                  pl.BlockSpec((s, tc), lambda j: (0, j))],
        out_specs=[pl.BlockSpec((s, tc), lambda j: (0, j)), pl.BlockSpec((1, tc), lambda j: (0, j))],
        out_shape=[SDS((s, c), BF16), SDS((1, c), F32)],
        compiler_params=pltpu.CompilerParams(dimension_semantics=("parallel",), vmem_limit_bytes=VMEM_LIMIT),
    )(z, mu, dzs)


def _heads(x, nh):
    return jnp.stack([x[:, h * HEAD:(h + 1) * HEAD] for h in range(nh)], axis=0)


def _unheads(x):
    return jnp.concatenate([x[h] for h in range(x.shape[0])], axis=1)


def _wkv_fwd(zs, lw, k2, na, b):
    s = lw.shape[0]
    t, hb = WKV_CHUNK, WKV_HEADS_PER_STEP
    w = hb * HEAD
    nc, ng = s // t, RWKV_HEADS // hb

    def body(r_ref, v_ref, lw_ref, k_ref, a_ref, b_ref, y_ref, s0_ref, x_ref, state):
        @pl.when(pl.program_id(1) == 0)
        def _():
            state[...] = jnp.zeros_like(state)

        s0 = state[...]
        s0_ref[0] = s0
        y, s1, x = _wkv_chunk(s0, *[_heads(t_ref[...], hb) for t_ref in (r_ref, lw_ref, k_ref, v_ref, a_ref, b_ref)], with_inv=True)
        y_ref[...] = _unheads(y)
        x_ref[0] = x
        state[...] = s1

    def col(off):
        return pl.BlockSpec((t, w), functools.partial(lambda g, i, off: (i, g + off), off=off))

    return pl.pallas_call(
        body, name="wkv_fwd", grid=(ng, nc),
        in_specs=[col(0), col(2 * ng), col(0), col(0), col(0), col(0)],
        out_specs=[col(0), pl.BlockSpec((1, hb, HEAD, HEAD), lambda g, i: (i, g, 0, 0)),
                   pl.BlockSpec((1, hb, t, t), lambda g, i: (i, g, 0, 0))],
        out_shape=[SDS((s, RWKV_DIM), F32), SDS((nc, RWKV_HEADS, HEAD, HEAD), F32), SDS((nc, RWKV_HEADS, t, t), F32)],
        scratch_shapes=[pltpu.VMEM((hb, HEAD, HEAD), F32)],
        compiler_params=pltpu.CompilerParams(dimension_semantics=("parallel", "arbitrary"), vmem_limit_bytes=VMEM_LIMIT),
    )(zs, zs, lw, k2, na, b)


def _wkv_bwd(zs, lw, k2, na, b, s0s, invs, dy):
    s = lw.shape[0]
    t, hb = WKV_CHUNK, WKV_HEADS_PER_STEP
    w = hb * HEAD
    nc, ng = s // t, RWKV_HEADS // hb

    def body(r_ref, v_ref, lw_ref, k_ref, a_ref, b_ref, s0_ref, x_ref, dy_ref, dr_ref, dlw_ref, dk_ref, dv_ref, da_ref, db_ref, dstate):
        @pl.when(pl.program_id(1) == 0)
        def _():
            dstate[...] = jnp.zeros_like(dstate)

        _, vjp = jax.vjp(functools.partial(_wkv_chunk, inv=x_ref[0]), s0_ref[0],
                         *[_heads(t_ref[...], hb) for t_ref in (r_ref, lw_ref, k_ref, v_ref, a_ref, b_ref)])
        grads = vjp((_heads(dy_ref[...], hb), dstate[...]))
        dstate[...] = grads[0]
        for o_ref, gval in zip((dr_ref, dlw_ref, dk_ref, dv_ref, da_ref, db_ref), grads[1:]):
            o_ref[...] = _unheads(gval)

    def col(off):
        return pl.BlockSpec((t, w), functools.partial(lambda g, i, off: (nc - 1 - i, g + off), off=off))

    return pl.pallas_call(
        body, name="wkv_bwd", grid=(ng, nc),
        in_specs=[col(0), col(2 * ng), col(0), col(0), col(0), col(0),
                  pl.BlockSpec((1, hb, HEAD, HEAD), lambda g, i: (nc - 1 - i, g, 0, 0)),
                  pl.BlockSpec((1, hb, t, t), lambda g, i: (nc - 1 - i, g, 0, 0)), col(0)],
        out_specs=[col(0)] * 6,
        out_shape=[SDS((s, RWKV_DIM), F32)] * 6,
        scratch_shapes=[pltpu.VMEM((hb, HEAD, HEAD), F32)],
        compiler_params=pltpu.CompilerParams(dimension_semantics=("parallel", "arbitrary"), vmem_limit_bytes=VMEM_LIMIT),
    )(zs, zs, lw, k2, na, b, s0s, invs, dy)


def _attn_fwd(q, k, v, d):
    s = q.shape[0]
    l = s // d
    nb = l // BAND
    assert nb * BAND == l
    qv, kv, vv = (t.reshape(l, d * GROUP_DIM) for t in (q, k, v))
    nh = GROUP_DIM // HEAD

    def body(q_ref, kp_ref, kc_ref, vp_ref, vc_ref, o_ref, l_ref):
        has_prev = pl.program_id(1) > 0
        o, lse = _attn_block(*[_heads(t_ref[...].astype(F32), nh) for t_ref in (q_ref, kp_ref, kc_ref, vp_ref, vc_ref)], has_prev)
        o_ref[...] = _unheads(o)
        l_ref[...] = _unheads(lse)

    cur = pl.BlockSpec((BAND, GROUP_DIM), lambda rho, i: (i, rho))
    prev = pl.BlockSpec((BAND, GROUP_DIM), lambda rho, i: (jnp.maximum(i - 1, 0), rho))
    o, lse = pl.pallas_call(
        body, name=f"attn_fwd_d{d}", grid=(d, nb),
        in_specs=[cur, prev, cur, prev, cur], out_specs=[cur, cur],
        out_shape=[SDS((l, d * GROUP_DIM), F32), SDS((l, d * GROUP_DIM), F32)],
        compiler_params=pltpu.CompilerParams(dimension_semantics=("parallel", "arbitrary"), vmem_limit_bytes=VMEM_LIMIT),
    )(qv, kv, kv, vv, vv)
    return o.reshape(s, GROUP_DIM), lse.reshape(s, GROUP_DIM)


def _attn_bwd(q, k, v, d, do, dlse):
    s = q.shape[0]
    l = s // d
    nb = l // BAND
    qv, kv, vv, dov, dlv = (t.reshape(l, d * GROUP_DIM) for t in (q, k, v, do, dlse))
    nh = GROUP_DIM // HEAD

    def body(q_ref, kp_ref, kc_ref, vp_ref, vc_ref, do_ref, dl_ref, dq_ref, dk_ref, dv_ref, ck, cv):
        step = pl.program_id(1)
        has_prev = step < nb - 1

        @pl.when(step == 0)
        def _():
            ck[...] = jnp.zeros_like(ck)
            cv[...] = jnp.zeros_like(cv)

        _, vjp = jax.vjp(functools.partial(_attn_block, has_prev=has_prev),
                         *[_heads(t_ref[...].astype(F32), nh) for t_ref in (q_ref, kp_ref, kc_ref, vp_ref, vc_ref)])
        dq, dkp, dkc, dvp, dvc = vjp((_heads(do_ref[...], nh), _heads(dl_ref[...], nh)))
        dq_ref[...] = _unheads(dq)
        dk_ref[...] = _unheads(dkc) + ck[...]
        dv_ref[...] = _unheads(dvc) + cv[...]
        ck[...] = _unheads(dkp)
        cv[...] = _unheads(dvp)

    cur = pl.BlockSpec((BAND, GROUP_DIM), lambda rho, i: (nb - 1 - i, rho))
    prev = pl.BlockSpec((BAND, GROUP_DIM), lambda rho, i: (jnp.maximum(nb - 2 - i, 0), rho))
    dq, dk, dv = pl.pallas_call(
        body, name=f"attn_bwd_d{d}", grid=(d, nb),
        in_specs=[cur, prev, cur, prev, cur, cur, cur], out_specs=[cur] * 3,
        out_shape=[SDS((l, d * GROUP_DIM), F32)] * 3,
        scratch_shapes=[pltpu.VMEM((BAND, GROUP_DIM), F32), pltpu.VMEM((BAND, GROUP_DIM), F32)],
        compiler_params=pltpu.CompilerParams(dimension_semantics=("parallel", "arbitrary"), vmem_limit_bytes=VMEM_LIMIT),
    )(qv, kv, kv, vv, vv, dov, dlv)
    return dq.reshape(s, GROUP_DIM), dk.reshape(s, GROUP_DIM), dv.reshape(s, GROUP_DIM)


def _coords():
    return lax.axis_index("x"), lax.axis_index("y"), lax.axis_index("c")


_CHIP_FLIPS = ((1, 0), (0, 1), (1, 1))


def _flip(v, f):
    return 1 - v if f else v


def _form(kind, r, c):
    return (N_CHIPS, r, c) if kind == "blk" else (r, N_CHIPS * c)


def _slot(ref, kind, j, rows, c):
    if kind == "blk":
        return ref.at[j] if rows is None else ref.at[j, rows]
    cols = pl.ds(pl.multiple_of(j * c, 128), c)
    return ref.at[:, cols] if rows is None else ref.at[rows, cols]


def _half(r, which, align):
    return pl.ds(pl.multiple_of(which * (r // 2), align), r // 2)


def _rcopy(src, dst, send_sems, recv_sems, kk, dev):
    return pltpu.make_async_remote_copy(src_ref=src, dst_ref=dst, send_sem=send_sems.at[kk], recv_sem=recv_sems.at[kk],
                                        device_id=dev, device_id_type=MESH)


def _gather_plan(specs, step):
    def copies(refs, ss, rs, received):
        x, y, c = _coords()
        out = []
        for w, (kind, r, cc) in enumerate(specs):
            mine, other = _half(r, c, 16), _half(r, 1 - c, 16)
            for kk, (fx, fy) in enumerate(_CHIP_FLIPS):
                px, py = _flip(x, fx), _flip(y, fy)
                if step == "ici":
                    sl = _slot(refs[w], kind, 2 * px + py if received else 2 * x + y, mine, cc)
                    dev = (px, py, c)
                else:
                    sl = _slot(refs[w], kind, 2 * px + py, other if received else mine, cc)
                    dev = (x, y, 1 - c)
                out.append(_rcopy(sl, sl, ss, rs, 3 * w + kk, dev))
        return out

    def issue(refs, ss, rs):
        return copies(refs, ss, rs, False)

    def expect(refs, ss, rs):
        return copies(refs, ss, rs, False), copies(refs, ss, rs, True)

    return issue, expect


_HBM = pl.BlockSpec(memory_space=pltpu.HBM)
_SEM = pl.BlockSpec(memory_space=pltpu.SEMAPHORE)
_EFFECT = pltpu.SideEffectType.DATAFLOW_SIDE_EFFECTING


def _copies_start(name, bufs, n_sems, issue, after=None):
    nb = len(bufs)
    extra = [] if after is None else [after]

    def body(*refs):
        send_sems, recv_sems = refs[nb + len(extra)], refs[nb + len(extra) + 1]
        for cp in issue(refs[:nb], send_sems, recv_sems):
            cp.start()
        refs[-1][...] = jnp.zeros_like(refs[-1])

    outs = pl.pallas_call(
        body, name=name,
        out_shape=(pltpu.SemaphoreType.DMA((n_sems,)), pltpu.SemaphoreType.DMA((n_sems,)),
                   *[pltpu.HBM(b.shape, b.dtype) for b in bufs], SDS((8, 128), F32)),
        in_specs=[_HBM] * nb + [pl.BlockSpec(memory_space=pl.ANY)] * len(extra),
        out_specs=(_SEM, _SEM, *[_HBM] * nb, pl.BlockSpec(memory_space=pltpu.VMEM)),
        input_output_aliases={i: 2 + i for i in range(nb)},
        compiler_params=pltpu.CompilerParams(has_side_effects=_EFFECT),
    )(*[pltpu.with_memory_space_constraint(b, pltpu.HBM) for b in bufs], *extra)
    return outs[0], outs[1], list(outs[2:2 + nb]), outs[-1]


def _copies_wait(name, bufs, send_sems, recv_sems, after, expect):
    nb = len(bufs)

    def body(*refs):
        sent, received = expect(refs[:nb], refs[nb], refs[nb + 1])
        for cp in sent:
            cp.wait_send()
        for cp in received:
            cp.wait_recv()

    outs = pl.pallas_call(
        body, name=name,
        out_shape=tuple(pltpu.HBM(b.shape, b.dtype) for b in bufs),
        in_specs=(*[_HBM] * nb, _SEM, _SEM, pl.BlockSpec(memory_space=pl.ANY)), out_specs=tuple([_HBM] * nb),
        input_output_aliases={i: i for i in range(nb)},
        compiler_params=pltpu.CompilerParams(has_side_effects=_EFFECT),
    )(*bufs, send_sems, recv_sems, after)
    return list(outs)


def _add_pair(g, recv, kind, r, c, c_arr, name):
    h = r // 2
    if kind == "blk":
        tr = _row_tile(h, 512)
        grid = (N_CHIPS, h // tr)
        g_spec = pl.BlockSpec((1, 1, tr, c), lambda j, i, c_ref: (j, c_ref[0], i, 0))
        o_spec = pl.BlockSpec((1, tr, c), lambda j, i, c_ref: (j, i, 0))
        gv, oshape = g.reshape(N_CHIPS, 2, h, c), (N_CHIPS, h, c)
    else:
        tr = _row_tile(h, 64)
        grid = (h // tr,)
        g_spec = pl.BlockSpec((1, tr, N_CHIPS * c), lambda i, c_ref: (c_ref[0], i, 0))
        o_spec = pl.BlockSpec((tr, N_CHIPS * c), lambda i, c_ref: (i, 0))
        gv, oshape = g.reshape(2, h, N_CHIPS * c), (h, N_CHIPS * c)

    def body(c_ref, g_ref, r_ref, o_ref, ob_ref):
        v = (g_ref[:, 0] if kind == "blk" else g_ref[0]) + r_ref[...]
        o_ref[...] = v
        ob_ref[...] = v.astype(BF16)

    return pl.pallas_call(
        body, name=name,
        grid_spec=pltpu.PrefetchScalarGridSpec(num_scalar_prefetch=1, grid=grid, in_specs=[g_spec, o_spec], out_specs=[o_spec] * 2),
        out_shape=[SDS(oshape, F32), SDS(oshape, BF16)],
        compiler_params=pltpu.CompilerParams(vmem_limit_bytes=VMEM_LIMIT),
    )(c_arr, gv, recv)


def _sum_chips(pair, recv, kind, r, c, mc_arr, name):
    h = r // 2
    tr = _row_tile(h, 512)
    nt = h // tr
    if kind == "blk":
        p_spec = pl.BlockSpec((1, tr, c), lambda i, mc: (mc[0], i, 0))
    else:
        p_spec = pl.BlockSpec((tr, c), lambda i, mc: (i, mc[0]))

    def body(mc, a_ref, r_ref, g_out):
        own = a_ref[0] if kind == "blk" else a_ref[...]
        g_out[...] = ((own + r_ref[0].astype(F32)) + r_ref[1].astype(F32)) + r_ref[2].astype(F32)

    return pl.pallas_call(
        body, name=name,
        grid_spec=pltpu.PrefetchScalarGridSpec(
            num_scalar_prefetch=1, grid=(nt,), in_specs=[p_spec, pl.BlockSpec((3, tr, c), lambda i, mc: (0, i, 0))],
            out_specs=pl.BlockSpec((tr, c), lambda i, mc: (mc[1] * nt + i, 0))),
        out_shape=SDS((r, c), F32),
        compiler_params=pltpu.CompilerParams(vmem_limit_bytes=VMEM_LIMIT),
    )(mc_arr, pair, recv)


class _GroupReduce:
    def __init__(self, tag, specs, c_arr, mc_arr):
        self.tag, self.specs, self.c_arr, self.mc_arr = tag, specs, c_arr, mc_arr
        self.n = len(specs)

    def _plan(self, step):
        specs, n = self.specs, self.n

        def copies(refs, ss, rs, received):
            x, y, c = _coords()
            sib, out = (x, y, 1 - c), []
            for w, (_, kind, r, cc) in enumerate(specs):
                if step == "join":
                    there = refs[w].at[_half(r, 1 - c if received else c, 8)]
                    out.append(_rcopy(there, there, ss, rs, w, sib))
                    continue
                src, land = refs[w], refs[n + w]
                if step == "swap":
                    rows = _half(r, 1 - c, 8)
                    part = src.at[:, rows] if kind == "blk" else src.at[rows]
                    out.append(_rcopy(land if received else part, land, ss, rs, w, sib))
                else:
                    for kk, (fx, fy) in enumerate(_CHIP_FLIPS):
                        px, py = _flip(x, fx), _flip(y, fy)
                        part = land.at[kk] if received else _slot(src, kind, 2 * px + py, None, cc)
                        out.append(_rcopy(part, land.at[kk], ss, rs, 3 * w + kk, (px, py, c)))
            return out

        def issue(refs, ss, rs):
            return copies(refs, ss, rs, False)

        def expect(refs, ss, rs):
            return copies(refs, ss, rs, False), copies(refs, ss, rs, True)

        return issue, expect

    def swap_start(self, grads, after=None):
        lands = [lax.empty(_form(kind, r // 2, c), F32) for _, kind, r, c in self.specs]
        ss, rs, bufs, tok = _copies_start(f"rs_{self.tag}_swap", list(grads) + lands, self.n, self._plan("swap")[0], after=after)
        self.state = (ss, rs, bufs)
        return tok

    def swap_wait_ici_start(self, after):
        ss, rs, bufs = self.state
        bufs = _copies_wait(f"rs_{self.tag}_swap_wait", bufs, ss, rs, after, self._plan("swap")[1])
        pairs = [_add_pair(bufs[w], bufs[self.n + w], kind, r, c, self.c_arr, name=f"rs_{self.tag}_pair_{nm}")
                 for w, (nm, kind, r, c) in enumerate(self.specs)]
        self.pair = [pr[0] for pr in pairs]
        lands = [lax.empty((3, r // 2, c), BF16) for _, _, r, c in self.specs]
        ss, rs, bufs, tok = _copies_start(f"rs_{self.tag}_ici", [pr[1] for pr in pairs] + lands, 3 * self.n, self._plan("ici")[0])
        self.state = (ss, rs, bufs)
        return tok

    def ici_wait_join_start(self, after):
        ss, rs, bufs = self.state
        bufs = _copies_wait(f"rs_{self.tag}_ici_wait", bufs, ss, rs, after, self._plan("ici")[1])
        outs = [_sum_chips(self.pair[w], bufs[self.n + w], kind, r, c, self.mc_arr, name=f"rs_{self.tag}_sum_{nm}")
                for w, (nm, kind, r, c) in enumerate(self.specs)]
        ss, rs, bufs, tok = _copies_start(f"rs_{self.tag}_join", outs, self.n, self._plan("join")[0])
        self.state = (ss, rs, bufs)
        return tok

    def join_wait(self, after):
        ss, rs, bufs = self.state
        bufs = _copies_wait(f"rs_{self.tag}_join_wait", bufs, ss, rs, after, self._plan("join")[1])
        return {nm: bufs[w] for w, (nm, _, _, _) in enumerate(self.specs)}


def _all_reduce_small(buf):
    rows, cols = buf.shape

    def body(x_ref, o_ref, gath, send_sems, recv_sems):
        x, y, c = _coords()
        me = 4 * x + 2 * y + c
        gath[me] = x_ref[...]
        sends = []
        for kk in range(1, 8):
            f = (kk >> 2) & 1, (kk >> 1) & 1, kk & 1
            px, py, pc = _flip(x, f[0]), _flip(y, f[1]), _flip(c, f[2])
            cp = pltpu.make_async_remote_copy(src_ref=x_ref, dst_ref=gath.at[me], send_sem=send_sems.at[kk - 1],
                                              recv_sem=recv_sems.at[kk - 1], device_id=(px, py, pc), device_id_type=MESH)
            cp.start()
            sends.append(cp)
        for kk in range(1, 8):
            f = (kk >> 2) & 1, (kk >> 1) & 1, kk & 1
            px, py, pc = _flip(x, f[0]), _flip(y, f[1]), _flip(c, f[2])
            there = gath.at[4 * px + 2 * py + pc]
            pltpu.make_async_remote_copy(src_ref=there, dst_ref=there, send_sem=send_sems.at[kk - 1],
                                         recv_sem=recv_sems.at[kk - 1], device_id=(px, py, pc), device_id_type=MESH).wait_recv()
        for cp in sends:
            cp.wait_send()
        acc = gath[0]
        for j in range(1, 8):
            acc = acc + gath[j]
        o_ref[...] = acc

    return pl.pallas_call(
        body, name="all_reduce_small",
        in_specs=[pl.BlockSpec(memory_space=pltpu.VMEM)], out_specs=pl.BlockSpec(memory_space=pltpu.VMEM),
        out_shape=SDS((rows, cols), F32),
        scratch_shapes=[pltpu.VMEM((8, rows, cols), F32), pltpu.SemaphoreType.DMA((7,)), pltpu.SemaphoreType.DMA((7,))],
    )(buf)


def _adamw_rows(w, g, m, v):
    m = ADAM_B1 * m + (1.0 - ADAM_B1) * g
    v = ADAM_B2 * v + (1.0 - ADAM_B2) * jnp.square(g)
    m_hat = m / (1.0 - ADAM_B1 ** ADAM_STEP)
    v_hat = v / (1.0 - ADAM_B2 ** ADAM_STEP)
    return -ADAM_LR * (m_hat / (jnp.sqrt(v_hat) + ADAM_EPS) + ADAM_WD * w), m, v


def _adamw(w, g, m, v, name, dep=None, with_grad=False):
    rows, cols = w.shape
    tm = _pick(rows, (256, 128, 64, 16, 8))
    f = (lambda wt, gt, mt, vt: (gt,) + _adamw_rows(wt, gt, mt, vt)) if with_grad else _adamw_rows
    return _rows_call(f, [(t, 0, cols) for t in (w, g, m, v)], [], [(cols, F32)] * (3 + with_grad), tm=tm, name=name, dep=dep)


def _pack_small(parts):
    flat = jnp.concatenate([parts[n].reshape(-1) for n, _ in SMALL])
    return jnp.pad(flat, (0, SMALL_ROWS * PACK_COLS - flat.shape[0])).reshape(SMALL_ROWS, PACK_COLS)


def _unpack_small(buf, shapes):
    flat, out, off = buf.reshape(-1), {}, 0
    for n, sz in SMALL:
        out[n] = flat[off:off + sz].reshape(shapes[n])
        off += sz
    return out


def _lora_stack(parts):
    return jnp.concatenate([parts[n] for n, _ in LORA], axis=-2)


def _lora_split(stacked):
    out, off = {}, 0
    for n, rows in LORA:
        out[n] = stacked[..., off:off + rows, :]
        off += rows
    return out


def _ffn_gate_up(h, wgt, wut, name, dep=None):
    s, d = h.shape
    nblk, f, _ = wgt.shape
    tm = _pick(s, (1024, 512, 256))
    dn = (((1,), (1,)), ((), ()))

    def body(h_ref, wg_ref, wu_ref, *rest):
        g_ref, u_ref, a_ref = rest[-3:]
        hh = h_ref[...]
        g = lax.dot_general(hh, wg_ref[0], dn, preferred_element_type=F32)
        u = lax.dot_general(hh, wu_ref[0], dn, preferred_element_type=F32)
        g_ref[0], u_ref[0] = g.astype(BF16), u.astype(BF16)
        a_ref[0] = _swiglu_act(g, u).astype(BF16)

    w_spec = pl.BlockSpec((1, f, d), lambda j, i: (j, 0, 0))
    o_spec = pl.BlockSpec((1, tm, f), lambda j, i: (j, i, 0))
    extra = [] if dep is None else [dep]
    return pl.pallas_call(
        body, name=name, grid=(nblk, s // tm),
        in_specs=[pl.BlockSpec((tm, d), lambda j, i: (i, 0)), w_spec, w_spec] + [pl.BlockSpec(memory_space=pl.ANY)] * len(extra),
        out_specs=[o_spec] * 3,
        out_shape=[SDS((nblk, s, f), BF16)] * 3,
        compiler_params=pltpu.CompilerParams(dimension_semantics=("parallel", "parallel"), vmem_limit_bytes=VMEM_LIMIT),
    )(h, wgt, wut, *extra)


def _ffn_down_dx(dx_bf, wd, gate, up, name, dep=None):
    s, d = dx_bf.shape
    nblk, f, _ = wd.shape
    tm = _pick(s, (1024, 512, 256))
    dn = (((1,), (1,)), ((), ()))

    def body(dx_ref, wd_ref, g_ref, u_ref, *rest):
        dg_ref, du_ref = rest[-2:]
        dact = 0.5 * lax.dot_general(dx_ref[...], wd_ref[0], dn, preferred_element_type=F32)
        _, vjp = jax.vjp(_swiglu_act, g_ref[0].astype(F32), u_ref[0].astype(F32))
        dg, du = vjp(dact)
        dg_ref[0], du_ref[0] = dg.astype(BF16), du.astype(BF16)

    o_spec = pl.BlockSpec((1, tm, f), lambda j, i: (j, i, 0))
    extra = [] if dep is None else [dep]
    return pl.pallas_call(
        body, name=name, grid=(nblk, s // tm),
        in_specs=[pl.BlockSpec((tm, d), lambda j, i: (i, 0)), pl.BlockSpec((1, f, d), lambda j, i: (j, 0, 0)), o_spec, o_spec]
        + [pl.BlockSpec(memory_space=pl.ANY)] * len(extra),
        out_specs=[o_spec] * 2, out_shape=[SDS((nblk, s, f), BF16)] * 2,
        compiler_params=pltpu.CompilerParams(dimension_semantics=("parallel", "parallel"), vmem_limit_bytes=VMEM_LIMIT),
    )(dx_bf, wd, gate, up, *extra)


def _ffn_fwd(x, gain, wgt, wut, wd, tag, h=None, dep=None):
    if h is None:
        h = _rows_call(_rms, [(x, 0, D_MODEL)], [gain], [(D_MODEL, BF16)], tm=512, name=f"{tag}_norm")[0]
    gate, up, act = _ffn_gate_up(h, wgt, wut, f"{tag}_gate_up", dep=dep)
    x_new = _mm(act, wd, sum_blocks=True, res=x, alpha=0.5, name=f"{tag}_down")
    return x_new, (x, h, gate, up, act)


def _ffn_bwd(dx_new, dx_new_bf, saved, gain, wgt, wut, wd, tag, dep=None, hooks=None):
    x, h, gate, up, act = saved
    hooks = hooks or {}

    def hook(name, *vals):
        return hooks[name](*vals) if name in hooks else None

    d_wd = _mm(act, dx_new_bf, ta=True, alpha=0.5, name=f"{tag}_down_dw")
    dep = hook("down", d_wd) if "down" in hooks else dep
    dgate, dup = _ffn_down_dx(dx_new_bf, wd, gate, up, f"{tag}_down_dx", dep=dep)
    d_wgt = _mm(dgate, h, ta=True, dep=hook("mid", dgate), name=f"{tag}_gate_dw")
    d_wut = _mm(dup, h, ta=True, name=f"{tag}_up_dw")
    dh = _mm(dgate, wgt, sum_blocks=True, dep=hook("dw", d_wgt, d_wut), name=f"{tag}_gate_dx")
    dx, dx_bf, dgain = _mm(dup, wut, sum_blocks=True, res=dh, dep=hook("dx", dh), post=_norm_bwd_post(x, gain, dx_new),
                           name=f"{tag}_up_dx")
    hook("end", dx_bf)
    return dx, dx_bf, dgain, d_wgt, d_wut, d_wd


def _norm_bwd_post(x, gain, dres):
    def f(dht, xt, drt, gt):
        _, vjp = jax.vjp(_rms, xt, gt)
        dxt, dgt = vjp(dht)
        return dxt + drt, dxt + drt, dgt

    return f, [x, dres], [gain], [F32, BF16], [(1, D_MODEL)]


def kernel(x, p, positions, ffn1_norm, ffn1_w_gate, ffn1_w_up, ffn1_w_down, mix_norm, w_in, rwkv_mu, rwkv_w0, rwkv_w2, rwkv_a0, rwkv_a2, rwkv_g2, rwkv_k_k, rwkv_k_a, rwkv_r_k, rwkv_gn_w, rwkv_gn_b, q_norm, k_norm, w_br_rwkv, w_br_attn, w_out, ffn2_norm, ffn2_w_gate, ffn2_w_up, ffn2_w_down, ple_norm, ple_w_gate, ple_w_proj, loss_target, m_ffn1_norm, m_ffn1_w_gate, m_ffn1_w_up, m_ffn1_w_down, m_mix_norm, m_w_in, m_rwkv_mu, m_rwkv_w0, m_rwkv_w2, m_rwkv_a0, m_rwkv_a2, m_rwkv_g2, m_rwkv_k_k, m_rwkv_k_a, m_rwkv_r_k, m_rwkv_gn_w, m_rwkv_gn_b, m_q_norm, m_k_norm, m_w_br_rwkv, m_w_br_attn, m_w_out, m_ffn2_norm, m_ffn2_w_gate, m_ffn2_w_up, m_ffn2_w_down, m_ple_norm, m_ple_w_gate, m_ple_w_proj, v_ffn1_norm, v_ffn1_w_gate, v_ffn1_w_up, v_ffn1_w_down, v_mix_norm, v_w_in, v_rwkv_mu, v_rwkv_w0, v_rwkv_w2, v_rwkv_a0, v_rwkv_a2, v_rwkv_g2, v_rwkv_k_k, v_rwkv_k_a, v_rwkv_r_k, v_rwkv_gn_w, v_rwkv_gn_b, v_q_norm, v_k_norm, v_w_br_rwkv, v_w_br_attn, v_w_out, v_ffn2_norm, v_ffn2_w_gate, v_ffn2_w_up, v_ffn2_w_down, v_ple_norm, v_ple_w_gate, v_ple_w_proj):
    args = dict(locals())
    wts = {n: args[n] for n in WEIGHTS}
    mom_m = {n: args["m_" + n] for n in WEIGHTS}
    mom_v = {n: args["v_" + n] for n in WEIGHTS}
    x0, tgt = x[0], loss_target[0]
    s = x0.shape[0]
    p_tok = p[0, 0]

    vec = {n: wts[n].reshape(1, -1) for n, _ in SMALL}
    xi, yi, ci = _coords()
    me = 2 * xi + yi
    def laid(t, n):
        return jnp.transpose(t[n][0]) if n in TRANSPOSED else t[n][0]

    shard_of = {n: laid(wts, n) for g in GROUPS.values() for n, _, _, _ in g if n != "lora"}
    shard_of["lora"] = _lora_stack({n: wts[n][0] for n, _ in LORA})

    def whole_with_own(n, kind, r, c, tok=None):
        at = (me, 0, 0) if kind == "blk" else (0, me * c)
        own = (shard_of[n] if tok is None else shard_of[n] + tok[0, 0]).astype(BF16)
        return lax.dynamic_update_slice(lax.empty(_form(kind, r, c), BF16), own[None] if kind == "blk" else own, at)

    specs = {g: [(kind, r, c) for _, kind, r, c in grp] for g, grp in GROUPS.items()}
    plans = {(g, st): _gather_plan(specs[g], st) for g in GROUPS for st in ("ici", "d2d")}
    buf_f1 = [whole_with_own(*w) for w in GROUPS["f1"]]
    ss_0, rs_0, buf_f1, tok_0 = _copies_start("gather_f1_ici", buf_f1, 3 * len(buf_f1), plans["f1", "ici"][0])
    bufs = {g: [whole_with_own(*w, tok=tok_0) for w in GROUPS[g]] for g in ("mx", "f2")}
    buf_f1 = _copies_wait("gather_f1_ici_wait", buf_f1, ss_0, rs_0, bufs["mx"][0], plans["f1", "ici"][1])
    ss_1, rs_1, buf_f1, tok_1 = _copies_start("gather_f1_d2d", buf_f1, 3 * len(buf_f1), plans["f1", "d2d"][0])
    h1 = _rows_call(_rms, [(x0, 0, D_MODEL)], [vec["ffn1_norm"] + tok_1[0, 0]], [(D_MODEL, BF16)], tm=512, name="ffn1_norm")[0]
    buf_f1 = _copies_wait("gather_f1_d2d_wait", buf_f1, ss_1, rs_1, h1, plans["f1", "d2d"][1])
    wb = dict(zip([w[0] for w in GROUPS["f1"]], buf_f1))
    ss_a, rs_a, buf_mx, tok_a = _copies_start("gather_mx_ici", bufs["mx"], 3 * len(bufs["mx"]), plans["mx", "ici"][0],
                                              after=wb["ffn1_w_gate"])

    inv_freq = 1.0 / (ROPE_THETA ** (jnp.arange(0, HEAD, 2, dtype=F32) / HEAD))
    ang = positions[0].astype(F32)[:, None] * inv_freq
    cos, sin = jnp.cos(ang), jnp.sin(ang)
    cos2, sin2 = jnp.concatenate([cos, cos], axis=1), jnp.concatenate([-sin, sin], axis=1)

    x1, ffn1_saved = _ffn_fwd(x0, vec["ffn1_norm"], wb["ffn1_w_gate"], wb["ffn1_w_up"], wb["ffn1_w_down"], "ffn1", h=h1, dep=tok_a)
    buf_mx = _copies_wait("gather_mx_ici_wait", buf_mx, ss_a, rs_a, x1, plans["mx", "ici"][1])
    ss_b, rs_b, buf_mx, tok_b = _copies_start("gather_mx_d2d", buf_mx, 3 * len(buf_mx), plans["mx", "d2d"][0])
    ss_c, rs_c, buf_f2, tok_c = _copies_start("gather_f2_ici", bufs["f2"], 3 * len(bufs["f2"]), plans["f2", "ici"][0])
    h = _rows_call(_rms, [(x1, 0, D_MODEL)], [vec["mix_norm"] + (tok_b[0, 0] + tok_c[0, 0])], [(D_MODEL, BF16)], tm=256,
                   name="mix_norm")[0]
    buf_mx = _copies_wait("gather_mx_d2d_wait", buf_mx, ss_b, rs_b, h, plans["mx", "d2d"][1])
    wb.update(zip([w[0] for w in GROUPS["mx"]], buf_mx))
    w_in_all = wb["w_in"]
    w_in_r, w_in_a, w_in_g = w_in_all[:, :RWKV_COLS], w_in_all[:, RWKV_COLS:RWKV_COLS + ATTN_COLS], w_in_all[:, RWKV_COLS + ATTN_COLS:]
    lora = _lora_split(wb["lora"])
    w2, a2, g2 = lora["rwkv_w2"], lora["rwkv_a2"], lora["rwkv_g2"]
    z_r = _mm(h, w_in_r, name="in_rwkv")
    z_a = _mm(h, w_in_a, name="in_attn")
    z_g = _mm(h, w_in_g, name="in_gate")

    zs = _shift_fwd(z_r, vec["rwkv_mu"])
    pre_params = [vec["rwkv_w0"], w2, vec["rwkv_a0"], a2, g2, vec["rwkv_k_k"], vec["rwkv_k_a"]]
    def pre_fwd(*t):
        res = _rwkv_pre(*t)
        return res[1], res[2], res[4], res[5], res[6]

    lw, k2, na, kb, gate_r = _rows_call(pre_fwd, [(zs, 0, RWKV_COLS)], pre_params, [(RWKV_DIM, F32)] * 5, tm=512, name="rwkv_pre")
    y_scan, s0s, invs = _wkv_fwd(zs, lw, k2, na, kb)
    buf_f2 = _copies_wait("gather_f2_ici_wait", buf_f2, ss_c, rs_c, y_scan, plans["f2", "ici"][1])
    ss_d, rs_d, buf_f2, tok_d = _copies_start("gather_f2_d2d", buf_f2, 3 * len(buf_f2), plans["f2", "d2d"][0])
    post_params = [vec["rwkv_gn_w"] + tok_d[0, 0], vec["rwkv_gn_b"], vec["rwkv_r_k"]]
    post_rows = [(y_scan, 0, RWKV_DIM), (zs, 0, RWKV_DIM), (k2, 0, RWKV_DIM), (zs, 2, RWKV_DIM), (gate_r, 0, RWKV_DIM)]
    y_rwkv = _rows_call(_rwkv_post, post_rows, post_params, [(RWKV_DIM, BF16)], tm=512, name="rwkv_post")[0]
    buf_f2 = _copies_wait("gather_f2_d2d_wait", buf_f2, ss_d, rs_d, y_rwkv, plans["f2", "d2d"][1])
    wb.update(zip([w[0] for w in GROUPS["f2"]], buf_f2))
    w_brr, w_bra = wb["w_br_rwkv"], wb["w_br_attn"]
    w_o = wb["w_out"].reshape(D_MODEL, D_MODEL)
    w_pp, w_pg = wb["ple_w_proj"], wb["ple_w_gate"].reshape(D_MODEL, D_MODEL)

    def qk_fwd(qt, kt, ct, st, qg, kg):
        return _norm_rope(qt, qg, ct, st), _norm_rope(kt, kg, ct, st)

    qk_rows = [(z_a, 0, ATTN_DIM), (z_a, 1, ATTN_DIM), (cos2, 0, HEAD), (sin2, 0, HEAD)]
    q_rot, k_rot = _rows_call(qk_fwd, qk_rows, [vec["q_norm"], vec["k_norm"]], [(ATTN_DIM, BF16)] * 2, tm=512, name="attn_pre")
    def group(t, g, off=0):
        return t[:, off + g * GROUP_DIM:off + (g + 1) * GROUP_DIM].astype(BF16)

    qkv = [(group(q_rot, g), group(k_rot, g), group(z_a, g, 2 * ATTN_DIM)) for g in range(len(ATTN_DILATIONS))]
    outs, lses = zip(*[_attn_fwd(*qkv[g], d) for g, d in enumerate(ATTN_DILATIONS)])
    comb_rows = [(t, 0, GROUP_DIM) for t in outs + lses]
    y_attn = _rows_call(_attn_combine, comb_rows, [], [(GROUP_DIM, BF16)], tm=512, name="attn_combine")[0]

    br = _mm(y_rwkv, w_brr, name="branch_rwkv")
    ba = _mm(y_attn, w_bra, name="branch_attn")
    merge_rows = [(z_g, 0, D_MODEL), (z_g, 1, D_MODEL), (br, 0, D_MODEL), (ba, 0, D_MODEL)]
    merged = _rows_call(_merge, merge_rows, [], [(D_MODEL, BF16)], tm=512, name="merge")[0]
    x2 = _mm(merged, w_o, res=x1, name="out_proj")
    x3, ffn2_saved = _ffn_fwd(x2, vec["ffn2_norm"], wb["ffn2_w_gate"], wb["ffn2_w_up"], wb["ffn2_w_down"], "ffn2")
    hp = _rows_call(_rms, [(x3, 0, D_MODEL)], [vec["ple_norm"]], [(D_MODEL, BF16)], tm=512, name="ple_norm")[0]
    pg = _mm(hp, w_pg, name="ple_gate")
    pp = _mm(p_tok, w_pp, name="ple_proj")

    def head(x3t, pgt, ppt, tt):
        sg = _sigmoid(pgt)
        err = x3t + sg * ppt - tt
        dx4 = err * (1.0 / D_MODEL)
        loss = 0.5 * jnp.sum(jnp.mean(err * err, axis=-1, keepdims=True), axis=0, keepdims=True)
        return dx4, dx4 * ppt * sg * (1.0 - sg), dx4 * sg, jnp.broadcast_to(loss, (8, 128))

    head_rows = [(x3, 0, D_MODEL), (pg, 0, D_MODEL), (pp, 0, D_MODEL), (tgt, 0, D_MODEL)]
    dx4, dpg, dpp, loss_tile = _rows_call(head, head_rows, [], [(D_MODEL, F32), (D_MODEL, BF16), (D_MODEL, BF16)], [(8, 128)],
                                          tm=512, name="ple_loss")

    c_arr = jnp.reshape(ci, (1,)).astype(jnp.int32)
    mc_arr = jnp.stack([me, ci]).astype(jnp.int32)
    red = {g: _GroupReduce(g, grp, c_arr, mc_arr) for g, grp in REDUCE_GROUPS.items()}

    done = {}

    def update(summed):
        for n, g2d in summed.items():
            if n == "lora":
                w_, m_, v_ = (_lora_stack({k: t[k][0] for k, _ in LORA}) for t in (wts, mom_m, mom_v))
            else:
                w_, m_, v_ = laid(wts, n), laid(mom_m, n), laid(mom_v, n)
            done[n] = _adamw(w_, g2d, m_, v_, name=f"adamw_{n}", with_grad=True)
    gw, gs = {}, {}
    gw["ple_w_proj"] = _mm(p_tok, dpp, ta=True, name="ple_proj_dw")
    gw["ple_w_gate"] = _mm(hp, dpg, ta=True, name="ple_gate_dw")
    dx3, dx3_bf, gs["ple_norm"] = _mm(dpg, w_pg, tb=True, post=_norm_bwd_post(x3, vec["ple_norm"], dx4), name="ple_gate_dx")
    dx2, dx2_bf, gs["ffn2_norm"], gw["ffn2_w_gate"], gw["ffn2_w_up"], gw["ffn2_w_down"] = _ffn_bwd(
        dx3, dx3_bf, ffn2_saved, vec["ffn2_norm"], wb["ffn2_w_gate"], wb["ffn2_w_up"], wb["ffn2_w_down"], "ffn2")
    gw["ple_w_gate"] = gw["ple_w_gate"].reshape(N_CHIPS, D_MODEL // N_CHIPS, D_MODEL)
    tok = red["f2"].swap_start([gw[w[0]] for w in REDUCE_GROUPS["f2"]])
    gw["w_out"] = _mm(merged, dx2_bf, ta=True, name="out_proj_dw")
    dmerged = _mm(dx2_bf, w_o, tb=True, dep=tok, name="out_proj_dx")

    def merge_bwd(zgr, zga, brt, bat, ct):
        _, vjp = jax.vjp(_merge, zgr, zga, brt, bat)
        d1, d2, d3, d4 = vjp(ct)
        return jnp.concatenate([d1, d2], axis=1), d3, d4

    dz_g, dbr, dba = _rows_call(merge_bwd, merge_rows + [(dmerged, 0, D_MODEL)], [],
                                [(2 * D_MODEL, BF16), (D_MODEL, BF16), (D_MODEL, BF16)], tm=512, name="merge_bwd")
    tok = red["f2"].swap_wait_ici_start(dz_g)
    gw["w_br_rwkv"] = _mm(y_rwkv, dbr, ta=True, name="branch_rwkv_dw")
    gw["w_br_attn"] = _mm(y_attn, dba, ta=True, name="branch_attn_dw")
    dy_rwkv = _mm(dbr, w_brr, tb=True, dep=tok, name="branch_rwkv_dx")
    dy_attn = _mm(dba, w_bra, tb=True, dep=tok, name="branch_attn_dx")

    def comb_bwd(*t):
        _, vjp = jax.vjp(_attn_combine, *t[:6])
        return vjp(t[6])

    dcomb = _rows_call(comb_bwd, comb_rows + [(dy_attn, 0, GROUP_DIM)], [], [(GROUP_DIM, F32)] * 6, tm=512, name="attn_combine_bwd")
    dqs, dks, dvs = zip(*[_attn_bwd(*qkv[g], d, dcomb[g], dcomb[3 + g]) for g, d in enumerate(ATTN_DILATIONS)])

    def qk_bwd(qt, kt, ct, st, *rest):
        dq = jnp.concatenate(rest[0:3], axis=1)
        dk = jnp.concatenate(rest[3:6], axis=1)
        qg, kg = rest[9], rest[10]
        _, vjp = jax.vjp(lambda a_, b_, c_, d_: qk_fwd(a_, b_, ct, st, c_, d_), qt, kt, qg, kg)
        dqt, dkt, dqg, dkg = vjp((dq, dk))
        return jnp.concatenate((dqt, dkt) + tuple(rest[6:9]), axis=1), dqg, dkg

    dz_a, gs["q_norm"], gs["k_norm"] = _rows_call(
        qk_bwd, qk_rows + [(t, 0, GROUP_DIM) for t in dqs + dks + dvs], [vec["q_norm"], vec["k_norm"]],
        [(ATTN_COLS, BF16)], [(1, HEAD), (1, HEAD)], tm=512, name="attn_pre_bwd")
    tok = red["f2"].ici_wait_join_start(dz_a)

    def post_bwd(*t):
        _, vjp = jax.vjp(_rwkv_post, *t[:5], *t[6:])
        return vjp(t[5])

    dy_scan, dr_post, dk2_post, dv_post, dgate_r, gs["rwkv_gn_w"], gs["rwkv_gn_b"], gs["rwkv_r_k"] = _rows_call(
        post_bwd, post_rows + [(dy_rwkv, 0, RWKV_DIM)], post_params, [(RWKV_DIM, F32)] * 5, [(1, RWKV_DIM)] * 3,
        tm=512, name="rwkv_post_bwd", dep=tok)
    update(red["f2"].join_wait(dy_scan))
    dr_s, dlw, dk2_s, dv_s, dna, dkb = _wkv_bwd(zs, lw, k2, na, kb, s0s, invs, dy_scan)

    def pre_bwd(zt, c_r1, c_r2, c_lw, c_k1, c_k2, c_v1, c_v2, c_a, c_b, c_g, *params):
        _, vjp = jax.vjp(_rwkv_pre, zt, *params)
        return vjp((c_r1 + c_r2, c_lw, c_k1 + c_k2, c_v1 + c_v2, c_a, c_b, c_g))

    pre_cts = [dr_s, dr_post, dlw, dk2_s, dk2_post, dv_s, dv_post, dna, dkb, dgate_r]
    dzs, gs["rwkv_w0"], g_w2, gs["rwkv_a0"], g_a2, g_g2, gs["rwkv_k_k"], gs["rwkv_k_a"] = _rows_call(
        pre_bwd, [(zs, 0, RWKV_COLS)] + [(t, 0, RWKV_DIM) for t in pre_cts], pre_params, [(RWKV_COLS, F32)],
        [q.shape for q in pre_params], tm=512, name="rwkv_pre_bwd")
    dz_r, gs["rwkv_mu"] = _shift_bwd(z_r, vec["rwkv_mu"], dzs)

    g_w_in = jnp.concatenate([_mm(h, dz_r, ta=True, name="in_rwkv_dw"), _mm(h, dz_a, ta=True, name="in_attn_dw"),
                              _mm(h, dz_g, ta=True, name="in_gate_dw")], axis=1)
    gw["w_in"], gw["lora"] = g_w_in, jnp.concatenate([g_w2, g_a2, g_g2], axis=0)
    gw["w_out"] = gw["w_out"].reshape(N_CHIPS, D_MODEL // N_CHIPS, D_MODEL)
    tok = red["mx"].swap_start([gw[w[0]] for w in REDUCE_GROUPS["mx"]])
    dh = _mm(dz_r, w_in_r, tb=True, dep=tok, name="in_rwkv_dx")
    dh = _mm(dz_a, w_in_a, tb=True, res=dh, name="in_attn_dx")
    dx1, dx1_bf, gs["mix_norm"] = _mm(dz_g, w_in_g, tb=True, res=dh, post=_norm_bwd_post(x1, vec["mix_norm"], dx2), name="in_gate_dx")
    tok_mx = red["mx"].swap_wait_ici_start(dx1_bf)
    hooks = {"down": lambda d_wd: red["f1d"].swap_start([d_wd], after=tok_mx),
             "mid": lambda dgate: red["f1d"].swap_wait_ici_start(dgate),
             "dw": lambda d_wgt, d_wut: red["f1g"].swap_start([d_wgt, d_wut]),
             "dx": lambda part: red["f1g"].swap_wait_ici_start(part) + red["f1d"].ici_wait_join_start(part),
             "end": lambda dx_: tokens.setdefault("mx_join", red["mx"].ici_wait_join_start(dx_))}
    tokens = {}
    dx0, _, gs["ffn1_norm"], gw["ffn1_w_gate"], gw["ffn1_w_up"], gw["ffn1_w_down"] = _ffn_bwd(
        dx1, dx1_bf, ffn1_saved, vec["ffn1_norm"], wb["ffn1_w_gate"], wb["ffn1_w_up"], wb["ffn1_w_down"], "ffn1", hooks=hooks)

    flat = jnp.concatenate([gs[n].reshape(-1) for n, _ in SMALL] + [loss_tile[0, 0:1]])
    small_buf = jnp.pad(flat, (0, SMALL_ROWS * PACK_COLS - flat.shape[0])).reshape(SMALL_ROWS, PACK_COLS)
    small_sum = _all_reduce_small(small_buf)
    n_small = sum(sz for _, sz in SMALL)
    loss = small_sum.reshape(-1)[n_small]
    grad_small = _unpack_small(small_sum, {n: wts[n].shape for n, _ in SMALL})
    d_s, m_s, v_s = _adamw(_pack_small(wts), small_sum, _pack_small(mom_m), _pack_small(mom_v), name="adamw_small",
                           dep=tokens["mx_join"])
    shapes = {n: wts[n].shape for n, _ in SMALL}
    d_s, m_s, v_s = _unpack_small(d_s, shapes), _unpack_small(m_s, shapes), _unpack_small(v_s, shapes)
    grads, deltas, new_m, new_v = {}, {}, {}, {}
    for n, _ in SMALL:
        grads[n], deltas[n], new_m[n], new_v[n] = grad_small[n], d_s[n], m_s[n], v_s[n]

    for g in ("mx", "f1d"):
        update(red[g].join_wait(m_s["ffn1_norm"]))
    tok = red["f1g"].ici_wait_join_start(done["w_in"][1])
    update(red["f1g"].join_wait(tok))
    for n, res in done.items():
        for store, val in zip((grads, deltas, new_m, new_v), res):
            if n == "lora":
                store.update({k: t[None] for k, t in _lora_split(val).items()})
            else:
                store[n] = (jnp.transpose(val) if n in TRANSPOSED else val)[None]

    return (loss, dx0[None], *[grads[n] for n in WEIGHTS], *[deltas[n] for n in WEIGHTS],
            *[new_m[n] for n in WEIGHTS], *[new_v[n] for n in WEIGHTS])
```

```python
import functools

import jax
import jax.numpy as jnp
from jax import lax
from jax.experimental import pallas as pl
from jax.experimental.pallas import tpu as pltpu

F32, BF16 = jnp.float32, jnp.bfloat16
HI = lax.Precision.HIGHEST
MESH = pl.DeviceIdType.MESH
SDS = jax.ShapeDtypeStruct

D_MODEL = 1024
HEAD = 64
RWKV_HEADS = 8
RWKV_DIM = RWKV_HEADS * HEAD
DECAY_LORA, ICLR_LORA, GATE_LORA = 64, 64, 128
GN_EPS = 64e-5
RMS_EPS = 1e-6
ATTN_DILATIONS = (1, 4, 16)
BAND = 128
ATTN_DIM = 768
GROUP_DIM = 256
ROPE_THETA = 10000.0
NEG_INF = -1e30
RWKV_COLS = 3 * RWKV_DIM + DECAY_LORA + ICLR_LORA + GATE_LORA
ATTN_COLS = 3 * ATTN_DIM
ADAM_LR, ADAM_B1, ADAM_B2, ADAM_EPS, ADAM_WD, ADAM_STEP = 0.001, 0.9, 0.999, 1e-08, 0.01, 10

WKV_CHUNK = 64
WKV_HEADS_PER_STEP = 8
N_CHIPS = 4
PACK_COLS = 1024
VMEM_LIMIT = 48 * 1024 * 1024

TRANSPOSED = ("ffn1_w_gate", "ffn1_w_up", "ffn2_w_gate", "ffn2_w_up")
LORA = (("rwkv_w2", 64), ("rwkv_a2", 64), ("rwkv_g2", 128))
_FFN1 = (("ffn1_w_gate", "blk", 704, 1024), ("ffn1_w_up", "blk", 704, 1024), ("ffn1_w_down", "blk", 704, 1024))
_FFN2 = (("ffn2_w_gate", "blk", 704, 1024), ("ffn2_w_up", "blk", 704, 1024), ("ffn2_w_down", "blk", 704, 1024))
_IN = (("w_in", "col", 1024, 1536), ("lora", "col", 256, 128))
_BRANCH = (("w_br_rwkv", "col", 512, 256), ("w_br_attn", "col", 256, 256), ("w_out", "blk", 256, 1024))
_PLE = (("ple_w_gate", "blk", 256, 1024), ("ple_w_proj", "col", 256, 256))
GROUPS = {"f1": _FFN1, "mx": _IN, "f2": _BRANCH + _FFN2 + _PLE}
REDUCE_GROUPS = {"f2": _FFN2 + _PLE, "mx": _IN + _BRANCH, "f1d": _FFN1[2:], "f1g": _FFN1[:2]}
SMALL = (
    ("ffn1_norm", 1024), ("mix_norm", 1024), ("ffn2_norm", 1024), ("ple_norm", 1024), ("rwkv_mu", 1792),
    ("rwkv_w0", 512), ("rwkv_a0", 512), ("rwkv_k_k", 512), ("rwkv_k_a", 512), ("rwkv_r_k", 512),
    ("rwkv_gn_w", 512), ("rwkv_gn_b", 512), ("q_norm", 64), ("k_norm", 64),
)
SMALL_ROWS = 16
WEIGHTS = (
    "ffn1_norm", "ffn1_w_gate", "ffn1_w_up", "ffn1_w_down", "mix_norm", "w_in", "rwkv_mu", "rwkv_w0", "rwkv_w2",
    "rwkv_a0", "rwkv_a2", "rwkv_g2", "rwkv_k_k", "rwkv_k_a", "rwkv_r_k", "rwkv_gn_w", "rwkv_gn_b", "q_norm", "k_norm",
    "w_br_rwkv", "w_br_attn", "w_out", "ffn2_norm", "ffn2_w_gate", "ffn2_w_up", "ffn2_w_down", "ple_norm",
    "ple_w_gate", "ple_w_proj",
)


def _row_tile(n, most=704):
    for t in range(most - most % 16, 0, -16):
        if n % t == 0:
            return t
    return n


def _pick(n, cands):
    for c in cands:
        if n % c == 0:
            return c
    return n


def _mm(a, b, *, ta=False, tb=False, sum_blocks=False, out_dtype=F32, res=None, alpha=1.0, dep=None, post=None, name):
    flat = a.ndim == 2 and b.ndim == 2
    a3 = a if a.ndim == 3 else a[None]
    b3 = b if b.ndim == 3 else b[None]
    na, nbb = a3.shape[0], b3.shape[0]
    nblk = max(na, nbb)
    kdim, m = (a3.shape[1], a3.shape[2]) if ta else (a3.shape[2], a3.shape[1])
    n = b3.shape[1] if tb else b3.shape[2]
    assert (b3.shape[2] if tb else b3.shape[1]) == kdim
    tm = _pick(m, (1024, 512, 256, 128) if post is None else (512, 256, 128))
    tn = _pick(n, (1024, 896, 768, 512, 256, 128))
    tk = kdim if kdim <= 2304 else _pick(kdim, (1024, 512, 256, 128))
    nk = kdim // tk
    direct = nk == 1 and not sum_blocks

    if sum_blocks:
        grid = (m // tm, n // tn, nblk, nk)

        def ids(i, c, j, k):
            return i, c, j, k
    else:
        grid = (nblk, m // tm, n // tn, nk)

        def ids(j, i, c, k):
            return i, c, j, k

    def amap(*g):
        i, c, j, k = ids(*g)
        jj = j if na > 1 else 0
        return (jj, k, i) if ta else (jj, i, k)

    def bmap(*g):
        i, c, j, k = ids(*g)
        jj = j if nbb > 1 else 0
        return (jj, c, k) if tb else (jj, k, c)

    if sum_blocks:
        oshape, oblk = (m, n), (tm, tn)

        def omap(*g):
            i, c, j, k = ids(*g)
            return i, c
    else:
        oshape, oblk = (nblk, m, n), (1, tm, tn)

        def omap(*g):
            i, c, j, k = ids(*g)
            return j, i, c

    dn = (((0 if ta else 1,), (1 if tb else 0,)), ((), ()))
    has_res = res is not None
    p_f, p_rows, p_params, p_dtypes, p_accs = post if post is not None else (None, [], [], [], [])
    assert post is None or sum_blocks or flat
    n_in = 2 + has_res + len(p_rows) + len(p_params) + (dep is not None)

    def tile_map(*g):
        i, c, j, k = ids(*g)
        return i, c

    def body(*refs):
        refs = list(refs)
        acc = None if direct else refs.pop()
        o_refs = refs[n_in:]
        a_ref, b_ref = refs[0], refs[1]
        r_ref = refs[2] if has_res else None
        pr_refs = refs[2 + has_res:2 + has_res + len(p_rows)]
        pp_refs = refs[2 + has_res + len(p_rows):2 + has_res + len(p_rows) + len(p_params)]
        first_tile = jnp.logical_and(pl.program_id(0 if sum_blocks else 1) == 0, pl.program_id(1 if sum_blocks else 2) == 0)

        def finish(v):
            if alpha != 1.0:
                v = v * alpha
            if has_res:
                v = v + r_ref[...].reshape(v.shape).astype(F32)
            if post is None:
                o_refs[0][...] = v.reshape(o_refs[0].shape).astype(o_refs[0].dtype)
                return
            outs = p_f(v, *[t[...] for t in pr_refs], *[t[...] for t in pp_refs])
            for o_ref, val in zip(o_refs, outs[:len(p_dtypes)]):
                o_ref[...] = val.astype(o_ref.dtype)
            for o_ref, val in zip(o_refs[len(p_dtypes):], outs[len(p_dtypes):]):
                @pl.when(first_tile)
                def _():
                    o_ref[...] = jnp.zeros_like(o_ref)

                o_ref[...] += val.reshape(o_ref.shape)

        if direct:
            finish(lax.dot_general(a_ref[0].astype(BF16), b_ref[0].astype(BF16), dn, preferred_element_type=F32))
            return
        k = pl.program_id(3)
        if sum_blocks:
            j = pl.program_id(2)
            first = jnp.logical_and(j == 0, k == 0)
            last = jnp.logical_and(j == nblk - 1, k == nk - 1)
        else:
            first, last = k == 0, k == nk - 1

        @pl.when(first)
        def _():
            acc[...] = jnp.zeros_like(acc)

        acc[...] += lax.dot_general(a_ref[0].astype(BF16), b_ref[0].astype(BF16), dn, preferred_element_type=F32)

        @pl.when(last)
        def _():
            finish(acc[...])

    in_specs = [pl.BlockSpec((1, tk, tm) if ta else (1, tm, tk), amap), pl.BlockSpec((1, tn, tk) if tb else (1, tk, tn), bmap)]
    args = [a3, b3]
    if has_res:
        res3 = res if (sum_blocks or res.ndim == 3) else res[None]
        in_specs.append(pl.BlockSpec(oblk, omap))
        args.append(res3)
    in_specs += [pl.BlockSpec((tm, tn), tile_map) for _ in p_rows]
    in_specs += [pl.BlockSpec(t.shape, functools.partial(lambda *g, nd: (0,) * nd, nd=t.ndim)) for t in p_params]
    args += list(p_rows) + list(p_params)
    if dep is not None:
        in_specs.append(pl.BlockSpec(memory_space=pl.ANY))
        args.append(dep)
    if post is None:
        out_specs, out_shape = pl.BlockSpec(oblk, omap), SDS(oshape, out_dtype)
        semantics = ("parallel", "parallel", "arbitrary", "arbitrary") if sum_blocks else ("parallel", "parallel", "parallel", "arbitrary")
    else:
        out_specs = [pl.BlockSpec((tm, tn), tile_map) for _ in p_dtypes]
        out_specs += [pl.BlockSpec(tuple(sh), functools.partial(lambda *g, nd: (0,) * nd, nd=len(sh))) for sh in p_accs]
        out_shape = [SDS((m, n), dt) for dt in p_dtypes] + [SDS(tuple(sh), F32) for sh in p_accs]
        semantics = ("arbitrary",) * 4
    out = pl.pallas_call(
        body,
        name=name,
        grid=grid,
        in_specs=in_specs,
        out_specs=out_specs,
        out_shape=out_shape,
        scratch_shapes=[] if direct else [pltpu.VMEM((tm, tn), F32)],
        compiler_params=pltpu.CompilerParams(dimension_semantics=semantics, vmem_limit_bytes=VMEM_LIMIT),
    )(*args)
    if post is not None:
        return out
    if flat and not sum_blocks:
        out = out[0]
    return out


def _rows_call(f, rows, params, outs, accs=(), *, tm, name, dep=None):
    s = rows[0][0].shape[0]
    nr, npar, no = len(rows), len(params), len(outs)
    nin = nr + npar + (0 if dep is None else 1)
    in_specs = [pl.BlockSpec((tm, w), functools.partial(lambda i, cb: (i, cb), cb=cb)) for (_, cb, w) in rows]
    in_specs += [pl.BlockSpec(p.shape, functools.partial(lambda i, nd: (0,) * nd, nd=p.ndim)) for p in params]
    if dep is not None:
        in_specs.append(pl.BlockSpec(memory_space=pl.ANY))
    out_shape = [SDS((s, w), dt) for (w, dt) in outs] + [SDS(tuple(sh), F32) for sh in accs]
    out_specs = [pl.BlockSpec((tm, w), lambda i: (i, 0)) for (w, _) in outs]
    out_specs += [pl.BlockSpec(tuple(sh), functools.partial(lambda i, nd: (0,) * nd, nd=len(sh))) for sh in accs]

    def body(*refs):
        rin, pin = refs[:nr], refs[nr:nr + npar]
        oo, ao = refs[nin:nin + no], refs[nin + no:]
        res = f(*[r[...] for r in rin], *[p[...] for p in pin])
        if not isinstance(res, (tuple, list)):
            res = (res,)
        for o_ref, v in zip(oo, res[:no]):
            o_ref[...] = v.astype(o_ref.dtype)
        i = pl.program_id(0)
        for a_ref, v in zip(ao, res[no:]):
            @pl.when(i == 0)
            def _():
                a_ref[...] = jnp.zeros_like(a_ref)

            a_ref[...] += v.reshape(a_ref.shape)

    res = pl.pallas_call(
        body,
        name=name,
        grid=(s // tm,),
        in_specs=in_specs,
        out_specs=out_specs,
        out_shape=out_shape,
        compiler_params=pltpu.CompilerParams(dimension_semantics=("arbitrary",), vmem_limit_bytes=VMEM_LIMIT),
    )(*[r[0] for r in rows], *params, *([] if dep is None else [dep]))
    return res


def _mmv(a, b, mode):
    ca = 0 if mode[0] == "t" else 1
    cb = 1 if mode[1] == "t" else 0
    return lax.dot_general(a.astype(BF16), b.astype(BF16), (((ca,), (cb,)), ((), ())), preferred_element_type=F32)


@functools.partial(jax.custom_vjp, nondiff_argnums=(2,))
def _bdot(a, b, mode):
    return _mmv(a, b, mode)


def _bdot_fwd(a, b, mode):
    return _mmv(a, b, mode), (a, b)


def _bdot_bwd(mode, saved, g):
    a, b = saved
    if mode == "nn":
        return _mmv(g, b, "nt"), _mmv(a, g, "tn")
    if mode == "nt":
        return _mmv(g, b, "nn"), _mmv(g, a, "tn")
    return _mmv(b, g, "nt"), _mmv(a, g, "nn")


_bdot.defvjp(_bdot_fwd, _bdot_bwd)


def _hdot(a, b, mode="nn", precision=HI):
    ca = 0 if mode[0] == "t" else 1
    cb = 1 if mode[1] == "t" else 0
    return lax.dot_general(a, b, (((ca,), (cb,)), ((), ())), precision=precision, preferred_element_type=F32)


def _segsum(x):
    c = x.shape[-1]
    blk = min(c, 256)
    r = lax.broadcasted_iota(jnp.int32, (blk, blk), 0) >> 6
    q = lax.broadcasted_iota(jnp.int32, (blk, blk), 1) >> 6
    ones = jnp.where(r == q, 1.0, 0.0).astype(F32)
    parts = [_hdot(x[:, i:i + blk], ones, precision=lax.Precision.HIGH) for i in range(0, c, blk)]
    return parts[0] if len(parts) == 1 else jnp.concatenate(parts, axis=1)


def _sigmoid(x):
    return jax.nn.sigmoid(x)


def _softplus(x):
    return jnp.maximum(x, 0.0) + jnp.log(1.0 + jnp.exp(-jnp.abs(x)))


def _rms(x, gain):
    return x * lax.rsqrt(jnp.mean(x * x, axis=-1, keepdims=True) + RMS_EPS) * gain


def _swiglu_act(gate, up):
    return gate * _sigmoid(gate) * up


def _rwkv_pre(zs, w0, w2, a0, a2, g2, k_k, k_a):
    r, k, v = zs[:, 0:512], zs[:, 512:1024], zs[:, 1024:1536]
    lora = zs[:, 1536:1792]
    wd, ad, gd = lora[:, 0:64], lora[:, 64:128], lora[:, 128:256]
    w = -_softplus(-(w0 + _bdot(jnp.tanh(wd), w2, "nn"))) - 0.5
    a = _sigmoid(a0 + _bdot(ad, a2, "nn"))
    g = _bdot(_sigmoid(gd), g2, "nn")
    kk = k * k_k
    kk = kk * lax.rsqrt(jnp.maximum(_segsum(kk * kk), 1e-24))
    k2 = k * (1.0 + (a - 1.0) * k_a)
    return r, -jnp.exp(w), k2, v, -kk, kk * a, g


def _rwkv_post(y, r, k2, v, g, gn_w, gn_b, r_k):
    mean = _segsum(y) * (1.0 / HEAD)
    yc = y - mean
    var = _segsum(yc * yc) * (1.0 / HEAD)
    yn = yc * lax.rsqrt(var + GN_EPS) * gn_w + gn_b
    bonus = _segsum(r * k2 * r_k) * v
    return (yn + bonus) * g


def _swap_halves(x):
    lane = lax.broadcasted_iota(jnp.int32, x.shape, 1)
    return jnp.where((lane & 32) == 0, jnp.roll(x, -32, axis=1), jnp.roll(x, 32, axis=1))


def _norm_rope(x, gain, cos, sin):
    heads = x.shape[1] // HEAD
    def rep(t):
        return jnp.concatenate([t] * heads, axis=1)

    xn = x * lax.rsqrt(_segsum(x * x) * (1.0 / HEAD) + RMS_EPS) * rep(gain)
    return xn * rep(cos) + _swap_halves(xn) * rep(sin)


def _attn_combine(o0, o1, o2, l0, l1, l2):
    m = jnp.maximum(jnp.maximum(l0, l1), l2)
    e0, e1, e2 = jnp.exp(l0 - m), jnp.exp(l1 - m), jnp.exp(l2 - m)
    return (e0 * o0 + e1 * o1 + e2 * o2) / (e0 + e1 + e2)


def _merge(zgr, zga, br, ba):
    return _sigmoid(zgr) * br + _sigmoid(zga) * ba


def _attn_block(q, kp, kc, vp, vc, has_prev):
    iq = lax.broadcasted_iota(jnp.int32, (1, BAND, BAND), 1)
    ik = lax.broadcasted_iota(jnp.int32, (1, BAND, BAND), 2)
    s_c = jnp.where(iq >= ik, _bdotb(q, kc, "nt") * (HEAD ** -0.5), NEG_INF)
    s_p = jnp.where(jnp.logical_and(iq <= ik, has_prev), _bdotb(q, kp, "nt") * (HEAD ** -0.5), NEG_INF)
    m = lax.stop_gradient(jnp.maximum(jnp.max(s_c, axis=-1, keepdims=True), jnp.max(s_p, axis=-1, keepdims=True)))
    e_c, e_p = jnp.exp(s_c - m), jnp.exp(s_p - m)
    l = jnp.sum(e_c, axis=-1, keepdims=True) + jnp.sum(e_p, axis=-1, keepdims=True)
    o = (_bdotb(e_c, vc) + _bdotb(e_p, vp)) / l
    return o, jnp.broadcast_to(m + jnp.log(l), o.shape)


def _mmb(a, b, cb):
    return lax.dot_general(a.astype(BF16), b.astype(BF16), (((2,), (cb,)), ((0,), (0,))), preferred_element_type=F32)


@functools.partial(jax.custom_vjp, nondiff_argnums=(2,))
def _bdotb1(a, b, cb):
    return _mmb(a, b, cb)


def _bdotb1_fwd(a, b, cb):
    return _mmb(a, b, cb), (a, b)


def _bdotb1_bwd(cb, saved, g):
    a, b = saved
    if cb == 1:
        return _mmb(g, b, 2), _mmb(jnp.swapaxes(a, 1, 2), g, 1)
    return _mmb(g, b, 1), _mmb(jnp.swapaxes(g, 1, 2), a, 1)


_bdotb1.defvjp(_bdotb1_fwd, _bdotb1_bwd)


def _bdotb(a, b, mode="nn", precision=None):
    if mode[0] == "t":
        a = jnp.swapaxes(a, 1, 2)
    cb = 2 if mode[1] == "t" else 1
    if precision is None:
        return _bdotb1(a, b, cb)
    return lax.dot_general(a, b, (((2,), (cb,)), ((0,), (0,))), precision=precision, preferred_element_type=F32)


def _tri_inv_levels(a):
    t = a.shape[-1]
    row = lax.broadcasted_iota(jnp.int32, (1, t, t), 1)
    col = lax.broadcasted_iota(jnp.int32, (1, t, t), 2)
    x = jnp.where(row == col, 1.0, 0.0).astype(F32) + jnp.where(jnp.logical_and(row == col + 1, (row & 1) == 1), a, 0.0)
    sh = 1
    while (1 << sh) < t:
        m = jnp.logical_and((row >> sh) == (col >> sh) + 1, (row >> (sh + 1)) == (col >> (sh + 1)))
        x = x + _bdotb(_bdotb(x, jnp.where(m, a, 0.0)), x)
        sh += 1
    return x


@jax.custom_vjp
def _tri_inv(a):
    return _tri_inv_levels(a)


def _tri_inv_fwd(a):
    x = _tri_inv_levels(a)
    return x, x


def _tri_inv_bwd(x, g):
    xt = jnp.swapaxes(x, 1, 2)
    return (_bdotb(_bdotb(xt, g, precision=lax.Precision.HIGH), xt, precision=lax.Precision.HIGH),)


_tri_inv.defvjp(_tri_inv_fwd, _tri_inv_bwd)


@jax.custom_vjp
def _known_inv(a, x):
    return x


def _known_inv_fwd(a, x):
    return x, x


def _known_inv_bwd(x, g):
    return _tri_inv_bwd(x, g)[0], jnp.zeros_like(x)


_known_inv.defvjp(_known_inv_fwd, _known_inv_bwd)


def _wkv_chunk(s0, r, lw, k, v, a, b, inv=None, with_inv=False):
    nh, t, _ = r.shape
    row = lax.broadcasted_iota(jnp.int32, (1, t, t), 1)
    col = lax.broadcasted_iota(jnp.int32, (1, t, t), 2)
    incl, strict = row >= col, row > col
    ones = jnp.broadcast_to(jnp.where(incl, 1.0, 0.0).astype(F32), (nh, t, t))
    cum = _bdotb(ones, lw, precision=HI)
    c_end = cum[:, t - 1:t, :]
    e_in, e_ex, e_inv = jnp.exp(cum), jnp.exp(cum - lw), jnp.exp(-cum)
    at, rt, bt, kt = a * e_ex, r * e_in, b * e_inv, k * e_inv
    a_ab = jnp.where(strict, _bdotb(at, bt, "nt"), 0.0)
    a_ak = jnp.where(strict, _bdotb(at, kt, "nt"), 0.0)
    x = _tri_inv(a_ab) if inv is None else _known_inv(a_ab, inv)
    u = _bdotb(x, _bdotb(at, s0, "nt") + _bdotb(a_ak, v))
    y = (_bdotb(rt, s0, "nt") + _bdotb(jnp.where(incl, _bdotb(rt, bt, "nt"), 0.0), u)
         + _bdotb(jnp.where(incl, _bdotb(rt, kt, "nt"), 0.0), v))
    w_end = jnp.exp(c_end - cum)
    s1 = s0 * jnp.exp(c_end) + _bdotb(u, b * w_end, "tn") + _bdotb(v, k * w_end, "tn")
    return (y, s1, x) if with_inv else (y, s1)


def _shift_fwd(z, mu):
    s, c = z.shape
    tc = 256

    def body(z_ref, mu_ref, o_ref):
        zz = z_ref[...]
        row = lax.broadcasted_iota(jnp.int32, zz.shape, 0)
        prev = jnp.where(row == 0, 0.0, pltpu.roll(zz, 1, 0))
        o_ref[...] = zz + (prev - zz) * mu_ref[...]

    return pl.pallas_call(
        body, name="shift_fwd", grid=(c // tc,),
        in_specs=[pl.BlockSpec((s, tc), lambda j: (0, j)), pl.BlockSpec((1, tc), lambda j: (0, j))],
        out_specs=pl.BlockSpec((s, tc), lambda j: (0, j)), out_shape=SDS((s, c), F32),
        compiler_params=pltpu.CompilerParams(dimension_semantics=("parallel",), vmem_limit_bytes=VMEM_LIMIT),
    )(z, mu)


def _shift_bwd(z, mu, dzs):
    s, c = z.shape
    tc = 256

    def body(z_ref, mu_ref, d_ref, dz_ref, dmu_ref):
        zz, d, m = z_ref[...], d_ref[...], mu_ref[...]
        row = lax.broadcasted_iota(jnp.int32, zz.shape, 0)
        prev = jnp.where(row == 0, 0.0, pltpu.roll(zz, 1, 0))
        t = d * m
        nxt = jnp.where(row == s - 1, 0.0, pltpu.roll(t, s - 1, 0))
        dz_ref[...] = (d - t + nxt).astype(dz_ref.dtype)
        dmu_ref[...] = jnp.sum(d * (prev - zz), axis=0, keepdims=True)

    return pl.pallas_call(
        body, name="shift_bwd", grid=(c // tc,),
        in_specs=[pl.BlockSpec((s, tc), lambda j: (0, j)), pl.BlockSpec((1, tc), lambda j: (0, j)),
                  pl.BlockSpec((s, tc), lambda j: (0, j))],
        out_specs=[pl.BlockSpec((s, tc), lambda j: (0, j)), pl.BlockSpec((1, tc), lambda j: (0, j))],
        out_shape=[SDS((s, c), BF16), SDS((1, c), F32)],
        compiler_params=pltpu.CompilerParams(dimension_semantics=("parallel",), vmem_limit_bytes=VMEM_LIMIT),
    )(z, mu, dzs)


def _heads(x, nh):
    return jnp.stack([x[:, h * HEAD:(h + 1) * HEAD] for h in range(nh)], axis=0)


def _unheads(x):
    return jnp.concatenate([x[h] for h in range(x.shape[0])], axis=1)


def _wkv_fwd(zs, lw, k2, na, b):
    s = lw.shape[0]
    t, hb = WKV_CHUNK, WKV_HEADS_PER_STEP
    w = hb * HEAD
    nc, ng = s // t, RWKV_HEADS // hb

    def body(r_ref, v_ref, lw_ref, k_ref, a_ref, b_ref, y_ref, s0_ref, x_ref, state):
        @pl.when(pl.program_id(1) == 0)
        def _():
            state[...] = jnp.zeros_like(state)

        s0 = state[...]
        s0_ref[0] = s0
        y, s1, x = _wkv_chunk(s0, *[_heads(t_ref[...], hb) for t_ref in (r_ref, lw_ref, k_ref, v_ref, a_ref, b_ref)], with_inv=True)
        y_ref[...] = _unheads(y)
        x_ref[0] = x
        state[...] = s1

    def col(off):
        return pl.BlockSpec((t, w), functools.partial(lambda g, i, off: (i, g + off), off=off))

    return pl.pallas_call(
        body, name="wkv_fwd", grid=(ng, nc),
        in_specs=[col(0), col(2 * ng), col(0), col(0), col(0), col(0)],
        out_specs=[col(0), pl.BlockSpec((1, hb, HEAD, HEAD), lambda g, i: (i, g, 0, 0)),
                   pl.BlockSpec((1, hb, t, t), lambda g, i: (i, g, 0, 0))],
        out_shape=[SDS((s, RWKV_DIM), F32), SDS((nc, RWKV_HEADS, HEAD, HEAD), F32), SDS((nc, RWKV_HEADS, t, t), F32)],
        scratch_shapes=[pltpu.VMEM((hb, HEAD, HEAD), F32)],
        compiler_params=pltpu.CompilerParams(dimension_semantics=("parallel", "arbitrary"), vmem_limit_bytes=VMEM_LIMIT),
    )(zs, zs, lw, k2, na, b)


def _wkv_bwd(zs, lw, k2, na, b, s0s, invs, dy):
    s = lw.shape[0]
    t, hb = WKV_CHUNK, WKV_HEADS_PER_STEP
    w = hb * HEAD
    nc, ng = s // t, RWKV_HEADS // hb

    def body(r_ref, v_ref, lw_ref, k_ref, a_ref, b_ref, s0_ref, x_ref, dy_ref, dr_ref, dlw_ref, dk_ref, dv_ref, da_ref, db_ref, dstate):
        @pl.when(pl.program_id(1) == 0)
        def _():
            dstate[...] = jnp.zeros_like(dstate)

        _, vjp = jax.vjp(functools.partial(_wkv_chunk, inv=x_ref[0]), s0_ref[0],
                         *[_heads(t_ref[...], hb) for t_ref in (r_ref, lw_ref, k_ref, v_ref, a_ref, b_ref)])
        grads = vjp((_heads(dy_ref[...], hb), dstate[...]))
        dstate[...] = grads[0]
        for o_ref, gval in zip((dr_ref, dlw_ref, dk_ref, dv_ref, da_ref, db_ref), grads[1:]):
            o_ref[...] = _unheads(gval)

    def col(off):
        return pl.BlockSpec((t, w), functools.partial(lambda g, i, off: (nc - 1 - i, g + off), off=off))

    return pl.pallas_call(
        body, name="wkv_bwd", grid=(ng, nc),
        in_specs=[col(0), col(2 * ng), col(0), col(0), col(0), col(0),
                  pl.BlockSpec((1, hb, HEAD, HEAD), lambda g, i: (nc - 1 - i, g, 0, 0)),
                  pl.BlockSpec((1, hb, t, t), lambda g, i: (nc - 1 - i, g, 0, 0)), col(0)],
        out_specs=[col(0)] * 6,
        out_shape=[SDS((s, RWKV_DIM), F32)] * 6,
        scratch_shapes=[pltpu.VMEM((hb, HEAD, HEAD), F32)],
        compiler_params=pltpu.CompilerParams(dimension_semantics=("parallel", "arbitrary"), vmem_limit_bytes=VMEM_LIMIT),
    )(zs, zs, lw, k2, na, b, s0s, invs, dy)


def _attn_fwd(q, k, v, d):
    s = q.shape[0]
    l = s // d
    nb = l // BAND
    assert nb * BAND == l
    qv, kv, vv = (t.reshape(l, d * GROUP_DIM) for t in (q, k, v))
    nh = GROUP_DIM // HEAD

    def body(q_ref, kp_ref, kc_ref, vp_ref, vc_ref, o_ref, l_ref):
        has_prev = pl.program_id(1) > 0
        o, lse = _attn_block(*[_heads(t_ref[...].astype(F32), nh) for t_ref in (q_ref, kp_ref, kc_ref, vp_ref, vc_ref)], has_prev)
        o_ref[...] = _unheads(o)
        l_ref[...] = _unheads(lse)

    cur = pl.BlockSpec((BAND, GROUP_DIM), lambda rho, i: (i, rho))
    prev = pl.BlockSpec((BAND, GROUP_DIM), lambda rho, i: (jnp.maximum(i - 1, 0), rho))
    o, lse = pl.pallas_call(
        body, name=f"attn_fwd_d{d}", grid=(d, nb),
        in_specs=[cur, prev, cur, prev, cur], out_specs=[cur, cur],
        out_shape=[SDS((l, d * GROUP_DIM), F32), SDS((l, d * GROUP_DIM), F32)],
        compiler_params=pltpu.CompilerParams(dimension_semantics=("parallel", "arbitrary"), vmem_limit_bytes=VMEM_LIMIT),
    )(qv, kv, kv, vv, vv)
    return o.reshape(s, GROUP_DIM), lse.reshape(s, GROUP_DIM)


def _attn_bwd(q, k, v, d, do, dlse):
    s = q.shape[0]
    l = s // d
    nb = l // BAND
    qv, kv, vv, dov, dlv = (t.reshape(l, d * GROUP_DIM) for t in (q, k, v, do, dlse))
    nh = GROUP_DIM // HEAD

    def body(q_ref, kp_ref, kc_ref, vp_ref, vc_ref, do_ref, dl_ref, dq_ref, dk_ref, dv_ref, ck, cv):
        step = pl.program_id(1)
        has_prev = step < nb - 1

        @pl.when(step == 0)
        def _():
            ck[...] = jnp.zeros_like(ck)
            cv[...] = jnp.zeros_like(cv)

        _, vjp = jax.vjp(functools.partial(_attn_block, has_prev=has_prev),
                         *[_heads(t_ref[...].astype(F32), nh) for t_ref in (q_ref, kp_ref, kc_ref, vp_ref, vc_ref)])
        dq, dkp, dkc, dvp, dvc = vjp((_heads(do_ref[...], nh), _heads(dl_ref[...], nh)))
        dq_ref[...] = _unheads(dq)
        dk_ref[...] = _unheads(dkc) + ck[...]
        dv_ref[...] = _unheads(dvc) + cv[...]
        ck[...] = _unheads(dkp)
        cv[...] = _unheads(dvp)

    cur = pl.BlockSpec((BAND, GROUP_DIM), lambda rho, i: (nb - 1 - i, rho))
    prev = pl.BlockSpec((BAND, GROUP_DIM), lambda rho, i: (jnp.maximum(nb - 2 - i, 0), rho))
    dq, dk, dv = pl.pallas_call(
        body, name=f"attn_bwd_d{d}", grid=(d, nb),
        in_specs=[cur, prev, cur, prev, cur, cur, cur], out_specs=[cur] * 3,
        out_shape=[SDS((l, d * GROUP_DIM), F32)] * 3,
        scratch_shapes=[pltpu.VMEM((BAND, GROUP_DIM), F32), pltpu.VMEM((BAND, GROUP_DIM), F32)],
        compiler_params=pltpu.CompilerParams(dimension_semantics=("parallel", "arbitrary"), vmem_limit_bytes=VMEM_LIMIT),
    )(qv, kv, kv, vv, vv, dov, dlv)
    return dq.reshape(s, GROUP_DIM), dk.reshape(s, GROUP_DIM), dv.reshape(s, GROUP_DIM)


def _coords():
    return lax.axis_index("x"), lax.axis_index("y"), lax.axis_index("c")


_CHIP_FLIPS = ((1, 0), (0, 1), (1, 1))


def _flip(v, f):
    return 1 - v if f else v


def _form(kind, r, c):
    return (N_CHIPS, r, c) if kind == "blk" else (r, N_CHIPS * c)


def _slot(ref, kind, j, rows, c):
    if kind == "blk":
        return ref.at[j] if rows is None else ref.at[j, rows]
    cols = pl.ds(pl.multiple_of(j * c, 128), c)
    return ref.at[:, cols] if rows is None else ref.at[rows, cols]


def _half(r, which, align):
    return pl.ds(pl.multiple_of(which * (r // 2), align), r // 2)


def _rcopy(src, dst, send_sems, recv_sems, kk, dev):
    return pltpu.make_async_remote_copy(src_ref=src, dst_ref=dst, send_sem=send_sems.at[kk], recv_sem=recv_sems.at[kk],
                                        device_id=dev, device_id_type=MESH)


def _gather_plan(specs, step):
    def copies(refs, ss, rs, received):
        x, y, c = _coords()
        out = []
        for w, (kind, r, cc) in enumerate(specs):
            mine, other = _half(r, c, 16), _half(r, 1 - c, 16)
            for kk, (fx, fy) in enumerate(_CHIP_FLIPS):
                px, py = _flip(x, fx), _flip(y, fy)
                if step == "ici":
                    sl = _slot(refs[w], kind, 2 * px + py if received else 2 * x + y, mine, cc)
                    dev = (px, py, c)
                else:
                    sl = _slot(refs[w], kind, 2 * px + py, other if received else mine, cc)
                    dev = (x, y, 1 - c)
                out.append(_rcopy(sl, sl, ss, rs, 3 * w + kk, dev))
        return out

    def issue(refs, ss, rs):
        return copies(refs, ss, rs, False)

    def expect(refs, ss, rs):
        return copies(refs, ss, rs, False), copies(refs, ss, rs, True)

    return issue, expect


_HBM = pl.BlockSpec(memory_space=pltpu.HBM)
_SEM = pl.BlockSpec(memory_space=pltpu.SEMAPHORE)
_EFFECT = pltpu.SideEffectType.DATAFLOW_SIDE_EFFECTING


def _copies_start(name, bufs, n_sems, issue, after=None):
    nb = len(bufs)
    extra = [] if after is None else [after]

    def body(*refs):
        send_sems, recv_sems = refs[nb + len(extra)], refs[nb + len(extra) + 1]
        for cp in issue(refs[:nb], send_sems, recv_sems):
            cp.start()
        refs[-1][...] = jnp.zeros_like(refs[-1])

    outs = pl.pallas_call(
        body, name=name,
        out_shape=(pltpu.SemaphoreType.DMA((n_sems,)), pltpu.SemaphoreType.DMA((n_sems,)),
                   *[pltpu.HBM(b.shape, b.dtype) for b in bufs], SDS((8, 128), F32)),
        in_specs=[_HBM] * nb + [pl.BlockSpec(memory_space=pl.ANY)] * len(extra),
        out_specs=(_SEM, _SEM, *[_HBM] * nb, pl.BlockSpec(memory_space=pltpu.VMEM)),
        input_output_aliases={i: 2 + i for i in range(nb)},
        compiler_params=pltpu.CompilerParams(has_side_effects=_EFFECT),
    )(*[pltpu.with_memory_space_constraint(b, pltpu.HBM) for b in bufs], *extra)
    return outs[0], outs[1], list(outs[2:2 + nb]), outs[-1]


def _copies_wait(name, bufs, send_sems, recv_sems, after, expect):
    nb = len(bufs)

    def body(*refs):
        sent, received = expect(refs[:nb], refs[nb], refs[nb + 1])
        for cp in sent:
            cp.wait_send()
        for cp in received:
            cp.wait_recv()

    outs = pl.pallas_call(
        body, name=name,
        out_shape=tuple(pltpu.HBM(b.shape, b.dtype) for b in bufs),
        in_specs=(*[_HBM] * nb, _SEM, _SEM, pl.BlockSpec(memory_space=pl.ANY)), out_specs=tuple([_HBM] * nb),
        input_output_aliases={i: i for i in range(nb)},
        compiler_params=pltpu.CompilerParams(has_side_effects=_EFFECT),
    )(*bufs, send_sems, recv_sems, after)
    return list(outs)


def _add_pair(g, recv, kind, r, c, c_arr, name):
    h = r // 2
    if kind == "blk":
        tr = _row_tile(h, 512)
        grid = (N_CHIPS, h // tr)
        g_spec = pl.BlockSpec((1, 1, tr, c), lambda j, i, c_ref: (j, c_ref[0], i, 0))
        o_spec = pl.BlockSpec((1, tr, c), lambda j, i, c_ref: (j, i, 0))
        gv, oshape = g.reshape(N_CHIPS, 2, h, c), (N_CHIPS, h, c)
    else:
        tr = _row_tile(h, 64)
        grid = (h // tr,)
        g_spec = pl.BlockSpec((1, tr, N_CHIPS * c), lambda i, c_ref: (c_ref[0], i, 0))
        o_spec = pl.BlockSpec((tr, N_CHIPS * c), lambda i, c_ref: (i, 0))
        gv, oshape = g.reshape(2, h, N_CHIPS * c), (h, N_CHIPS * c)

    def body(c_ref, g_ref, r_ref, o_ref, ob_ref):
        v = (g_ref[:, 0] if kind == "blk" else g_ref[0]) + r_ref[...]
        o_ref[...] = v
        ob_ref[...] = v.astype(BF16)

    return pl.pallas_call(
        body, name=name,
        grid_spec=pltpu.PrefetchScalarGridSpec(num_scalar_prefetch=1, grid=grid, in_specs=[g_spec, o_spec], out_specs=[o_spec] * 2),
        out_shape=[SDS(oshape, F32), SDS(oshape, BF16)],
        compiler_params=pltpu.CompilerParams(vmem_limit_bytes=VMEM_LIMIT),
    )(c_arr, gv, recv)


def _sum_chips(pair, recv, kind, r, c, mc_arr, name):
    h = r // 2
    tr = _row_tile(h, 512)
    nt = h // tr
    if kind == "blk":
        p_spec = pl.BlockSpec((1, tr, c), lambda i, mc: (mc[0], i, 0))
    else:
        p_spec = pl.BlockSpec((tr, c), lambda i, mc: (i, mc[0]))

    def body(mc, a_ref, r_ref, g_out):
        own = a_ref[0] if kind == "blk" else a_ref[...]
        g_out[...] = ((own + r_ref[0].astype(F32)) + r_ref[1].astype(F32)) + r_ref[2].astype(F32)

    return pl.pallas_call(
        body, name=name,
        grid_spec=pltpu.PrefetchScalarGridSpec(
            num_scalar_prefetch=1, grid=(nt,), in_specs=[p_spec, pl.BlockSpec((3, tr, c), lambda i, mc: (0, i, 0))],
            out_specs=pl.BlockSpec((tr, c), lambda i, mc: (mc[1] * nt + i, 0))),
        out_shape=SDS((r, c), F32),
        compiler_params=pltpu.CompilerParams(vmem_limit_bytes=VMEM_LIMIT),
    )(mc_arr, pair, recv)


class _GroupReduce:
    def __init__(self, tag, specs, c_arr, mc_arr):
        self.tag, self.specs, self.c_arr, self.mc_arr = tag, specs, c_arr, mc_arr
        self.n = len(specs)

    def _plan(self, step):
        specs, n = self.specs, self.n

        def copies(refs, ss, rs, received):
            x, y, c = _coords()
            sib, out = (x, y, 1 - c), []
            for w, (_, kind, r, cc) in enumerate(specs):
                if step == "join":
                    there = refs[w].at[_half(r, 1 - c if received else c, 8)]
                    out.append(_rcopy(there, there, ss, rs, w, sib))
                    continue
                src, land = refs[w], refs[n + w]
                if step == "swap":
                    rows = _half(r, 1 - c, 8)
                    part = src.at[:, rows] if kind == "blk" else src.at[rows]
                    out.append(_rcopy(land if received else part, land, ss, rs, w, sib))
                else:
                    for kk, (fx, fy) in enumerate(_CHIP_FLIPS):
                        px, py = _flip(x, fx), _flip(y, fy)
                        part = land.at[kk] if received else _slot(src, kind, 2 * px + py, None, cc)
                        out.append(_rcopy(part, land.at[kk], ss, rs, 3 * w + kk, (px, py, c)))
            return out

        def issue(refs, ss, rs):
            return copies(refs, ss, rs, False)

        def expect(refs, ss, rs):
            return copies(refs, ss, rs, False), copies(refs, ss, rs, True)

        return issue, expect

    def swap_start(self, grads, after=None):
        lands = [lax.empty(_form(kind, r // 2, c), F32) for _, kind, r, c in self.specs]
        ss, rs, bufs, tok = _copies_start(f"rs_{self.tag}_swap", list(grads) + lands, self.n, self._plan("swap")[0], after=after)
        self.state = (ss, rs, bufs)
        return tok

    def swap_wait_ici_start(self, after):
        ss, rs, bufs = self.state
        bufs = _copies_wait(f"rs_{self.tag}_swap_wait", bufs, ss, rs, after, self._plan("swap")[1])
        pairs = [_add_pair(bufs[w], bufs[self.n + w], kind, r, c, self.c_arr, name=f"rs_{self.tag}_pair_{nm}")
                 for w, (nm, kind, r, c) in enumerate(self.specs)]
        self.pair = [pr[0] for pr in pairs]
        lands = [lax.empty((3, r // 2, c), BF16) for _, _, r, c in self.specs]
        ss, rs, bufs, tok = _copies_start(f"rs_{self.tag}_ici", [pr[1] for pr in pairs] + lands, 3 * self.n, self._plan("ici")[0])
        self.state = (ss, rs, bufs)
        return tok

    def ici_wait_join_start(self, after):
        ss, rs, bufs = self.state
        bufs = _copies_wait(f"rs_{self.tag}_ici_wait", bufs, ss, rs, after, self._plan("ici")[1])
        outs = [_sum_chips(self.pair[w], bufs[self.n + w], kind, r, c, self.mc_arr, name=f"rs_{self.tag}_sum_{nm}")
                for w, (nm, kind, r, c) in enumerate(self.specs)]
        ss, rs, bufs, tok = _copies_start(f"rs_{self.tag}_join", outs, self.n, self._plan("join")[0])
        self.state = (ss, rs, bufs)
        return tok

    def join_wait(self, after):
        ss, rs, bufs = self.state
        bufs = _copies_wait(f"rs_{self.tag}_join_wait", bufs, ss, rs, after, self._plan("join")[1])
        return {nm: bufs[w] for w, (nm, _, _, _) in enumerate(self.specs)}


def _all_reduce_small(buf):
    rows, cols = buf.shape

    def body(x_ref, o_ref, gath, send_sems, recv_sems):
        x, y, c = _coords()
        me = 4 * x + 2 * y + c
        gath[me] = x_ref[...]
        sends = []
        for kk in range(1, 8):
            f = (kk >> 2) & 1, (kk >> 1) & 1, kk & 1
            px, py, pc = _flip(x, f[0]), _flip(y, f[1]), _flip(c, f[2])
            cp = pltpu.make_async_remote_copy(src_ref=x_ref, dst_ref=gath.at[me], send_sem=send_sems.at[kk - 1],
                                              recv_sem=recv_sems.at[kk - 1], device_id=(px, py, pc), device_id_type=MESH)
            cp.start()
            sends.append(cp)
        for kk in range(1, 8):
            f = (kk >> 2) & 1, (kk >> 1) & 1, kk & 1
            px, py, pc = _flip(x, f[0]), _flip(y, f[1]), _flip(c, f[2])
            there = gath.at[4 * px + 2 * py + pc]
            pltpu.make_async_remote_copy(src_ref=there, dst_ref=there, send_sem=send_sems.at[kk - 1],
                                         recv_sem=recv_sems.at[kk - 1], device_id=(px, py, pc), device_id_type=MESH).wait_recv()
        for cp in sends:
            cp.wait_send()
        acc = gath[0]
        for j in range(1, 8):
            acc = acc + gath[j]
        o_ref[...] = acc

    return pl.pallas_call(
        body, name="all_reduce_small",
        in_specs=[pl.BlockSpec(memory_space=pltpu.VMEM)], out_specs=pl.BlockSpec(memory_space=pltpu.VMEM),
        out_shape=SDS((rows, cols), F32),
        scratch_shapes=[pltpu.VMEM((8, rows, cols), F32), pltpu.SemaphoreType.DMA((7,)), pltpu.SemaphoreType.DMA((7,))],
    )(buf)


def _adamw_rows(w, g, m, v):
    m = ADAM_B1 * m + (1.0 - ADAM_B1) * g
    v = ADAM_B2 * v + (1.0 - ADAM_B2) * jnp.square(g)
    m_hat = m / (1.0 - ADAM_B1 ** ADAM_STEP)
    v_hat = v / (1.0 - ADAM_B2 ** ADAM_STEP)
    return -ADAM_LR * (m_hat / (jnp.sqrt(v_hat) + ADAM_EPS) + ADAM_WD * w), m, v


def _adamw(w, g, m, v, name, dep=None, with_grad=False):
    rows, cols = w.shape
    tm = _pick(rows, (256, 128, 64, 16, 8))
    f = (lambda wt, gt, mt, vt: (gt,) + _adamw_rows(wt, gt, mt, vt)) if with_grad else _adamw_rows
    return _rows_call(f, [(t, 0, cols) for t in (w, g, m, v)], [], [(cols, F32)] * (3 + with_grad), tm=tm, name=name, dep=dep)


def _pack_small(parts):
    flat = jnp.concatenate([parts[n].reshape(-1) for n, _ in SMALL])
    return jnp.pad(flat, (0, SMALL_ROWS * PACK_COLS - flat.shape[0])).reshape(SMALL_ROWS, PACK_COLS)


def _unpack_small(buf, shapes):
    flat, out, off = buf.reshape(-1), {}, 0
    for n, sz in SMALL:
        out[n] = flat[off:off + sz].reshape(shapes[n])
        off += sz
    return out


def _lora_stack(parts):
    return jnp.concatenate([parts[n] for n, _ in LORA], axis=-2)


def _lora_split(stacked):
    out, off = {}, 0
    for n, rows in LORA:
        out[n] = stacked[..., off:off + rows, :]
        off += rows
    return out


def _ffn_gate_up(h, wgt, wut, name, dep=None):
    s, d = h.shape
    nblk, f, _ = wgt.shape
    tm = _pick(s, (1024, 512, 256))
    dn = (((1,), (1,)), ((), ()))

    def body(h_ref, wg_ref, wu_ref, *rest):
        g_ref, u_ref, a_ref = rest[-3:]
        hh = h_ref[...]
        g = lax.dot_general(hh, wg_ref[0], dn, preferred_element_type=F32)
        u = lax.dot_general(hh, wu_ref[0], dn, preferred_element_type=F32)
        g_ref[0], u_ref[0] = g.astype(BF16), u.astype(BF16)
        a_ref[0] = _swiglu_act(g, u).astype(BF16)

    w_spec = pl.BlockSpec((1, f, d), lambda j, i: (j, 0, 0))
    o_spec = pl.BlockSpec((1, tm, f), lambda j, i: (j, i, 0))
    extra = [] if dep is None else [dep]
    return pl.pallas_call(
        body, name=name, grid=(nblk, s // tm),
        in_specs=[pl.BlockSpec((tm, d), lambda j, i: (i, 0)), w_spec, w_spec] + [pl.BlockSpec(memory_space=pl.ANY)] * len(extra),
        out_specs=[o_spec] * 3,
        out_shape=[SDS((nblk, s, f), BF16)] * 3,
        compiler_params=pltpu.CompilerParams(dimension_semantics=("parallel", "parallel"), vmem_limit_bytes=VMEM_LIMIT),
    )(h, wgt, wut, *extra)


def _ffn_down_dx(dx_bf, wd, gate, up, name, dep=None):
    s, d = dx_bf.shape
    nblk, f, _ = wd.shape
    tm = _pick(s, (1024, 512, 256))
    dn = (((1,), (1,)), ((), ()))

    def body(dx_ref, wd_ref, g_ref, u_ref, *rest):
        dg_ref, du_ref = rest[-2:]
        dact = 0.5 * lax.dot_general(dx_ref[...], wd_ref[0], dn, preferred_element_type=F32)
        _, vjp = jax.vjp(_swiglu_act, g_ref[0].astype(F32), u_ref[0].astype(F32))
        dg, du = vjp(dact)
        dg_ref[0], du_ref[0] = dg.astype(BF16), du.astype(BF16)

    o_spec = pl.BlockSpec((1, tm, f), lambda j, i: (j, i, 0))
    extra = [] if dep is None else [dep]
    return pl.pallas_call(
        body, name=name, grid=(nblk, s // tm),
        in_specs=[pl.BlockSpec((tm, d), lambda j, i: (i, 0)), pl.BlockSpec((1, f, d), lambda j, i: (j, 0, 0)), o_spec, o_spec]
        + [pl.BlockSpec(memory_space=pl.ANY)] * len(extra),
        out_specs=[o_spec] * 2, out_shape=[SDS((nblk, s, f), BF16)] * 2,
        compiler_params=pltpu.CompilerParams(dimension_semantics=("parallel", "parallel"), vmem_limit_bytes=VMEM_LIMIT),
    )(dx_bf, wd, gate, up, *extra)


def _norm_post(gain):
    return (lambda v, g: (v, _rms(v, g))), [], [gain], [F32, BF16], []


def _ffn_fwd(x, gain, wgt, wut, wd, tag, h=None, dep=None, next_gain=None):
    if h is None:
        h = _rows_call(_rms, [(x, 0, D_MODEL)], [gain], [(D_MODEL, BF16)], tm=512, name=f"{tag}_norm")[0]
    gate, up, act = _ffn_gate_up(h, wgt, wut, f"{tag}_gate_up", dep=dep)
    post = None if next_gain is None else _norm_post(next_gain)
    x_new = _mm(act, wd, sum_blocks=True, res=x, alpha=0.5, post=post, name=f"{tag}_down")
    return x_new, (x, h, gate, up, act)


def _ffn_bwd(dx_new, dx_new_bf, saved, gain, wgt, wut, wd, tag, dep=None, hooks=None):
    x, h, gate, up, act = saved
    hooks = hooks or {}

    def hook(name, *vals):
        return hooks[name](*vals) if name in hooks else None

    d_wd = _mm(act, dx_new_bf, ta=True, alpha=0.5, name=f"{tag}_down_dw")
    dep = hook("down", d_wd) if "down" in hooks else dep
    dgate, dup = _ffn_down_dx(dx_new_bf, wd, gate, up, f"{tag}_down_dx", dep=dep)
    d_wgt = _mm(dgate, h, ta=True, dep=hook("mid", dgate), name=f"{tag}_gate_dw")
    d_wut = _mm(dup, h, ta=True, name=f"{tag}_up_dw")
    dh = _mm(dgate, wgt, sum_blocks=True, dep=hook("dw", d_wgt, d_wut), name=f"{tag}_gate_dx")
    dx, dx_bf, dgain = _mm(dup, wut, sum_blocks=True, res=dh, dep=hook("dx", dh), post=_norm_bwd_post(x, gain, dx_new),
                           name=f"{tag}_up_dx")
    hook("end", dx_bf)
    return dx, dx_bf, dgain, d_wgt, d_wut, d_wd


def _norm_bwd_post(x, gain, dres):
    def f(dht, xt, drt, gt):
        _, vjp = jax.vjp(_rms, xt, gt)
        dxt, dgt = vjp(dht)
        return dxt + drt, dxt + drt, dgt

    return f, [x, dres], [gain], [F32, BF16], [(1, D_MODEL)]


def kernel(x, p, positions, ffn1_norm, ffn1_w_gate, ffn1_w_up, ffn1_w_down, mix_norm, w_in, rwkv_mu, rwkv_w0, rwkv_w2, rwkv_a0, rwkv_a2, rwkv_g2, rwkv_k_k, rwkv_k_a, rwkv_r_k, rwkv_gn_w, rwkv_gn_b, q_norm, k_norm, w_br_rwkv, w_br_attn, w_out, ffn2_norm, ffn2_w_gate, ffn2_w_up, ffn2_w_down, ple_norm, ple_w_gate, ple_w_proj, loss_target, m_ffn1_norm, m_ffn1_w_gate, m_ffn1_w_up, m_ffn1_w_down, m_mix_norm, m_w_in, m_rwkv_mu, m_rwkv_w0, m_rwkv_w2, m_rwkv_a0, m_rwkv_a2, m_rwkv_g2, m_rwkv_k_k, m_rwkv_k_a, m_rwkv_r_k, m_rwkv_gn_w, m_rwkv_gn_b, m_q_norm, m_k_norm, m_w_br_rwkv, m_w_br_attn, m_w_out, m_ffn2_norm, m_ffn2_w_gate, m_ffn2_w_up, m_ffn2_w_down, m_ple_norm, m_ple_w_gate, m_ple_w_proj, v_ffn1_norm, v_ffn1_w_gate, v_ffn1_w_up, v_ffn1_w_down, v_mix_norm, v_w_in, v_rwkv_mu, v_rwkv_w0, v_rwkv_w2, v_rwkv_a0, v_rwkv_a2, v_rwkv_g2, v_rwkv_k_k, v_rwkv_k_a, v_rwkv_r_k, v_rwkv_gn_w, v_rwkv_gn_b, v_q_norm, v_k_norm, v_w_br_rwkv, v_w_br_attn, v_w_out, v_ffn2_norm, v_ffn2_w_gate, v_ffn2_w_up, v_ffn2_w_down, v_ple_norm, v_ple_w_gate, v_ple_w_proj):
    args = dict(locals())
    wts = {n: args[n] for n in WEIGHTS}
    mom_m = {n: args["m_" + n] for n in WEIGHTS}
    mom_v = {n: args["v_" + n] for n in WEIGHTS}
    x0, tgt = x[0], loss_target[0]
    s = x0.shape[0]
    p_tok = p[0, 0]

    vec = {n: wts[n].reshape(1, -1) for n, _ in SMALL}
    xi, yi, ci = _coords()
    me = 2 * xi + yi
    def laid(t, n):
        return jnp.transpose(t[n][0]) if n in TRANSPOSED else t[n][0]

    shard_of = {n: laid(wts, n) for g in GROUPS.values() for n, _, _, _ in g if n != "lora"}
    shard_of["lora"] = _lora_stack({n: wts[n][0] for n, _ in LORA})

    def whole_with_own(n, kind, r, c, tok=None):
        at = (me, 0, 0) if kind == "blk" else (0, me * c)
        own = (shard_of[n] if tok is None else shard_of[n] + tok[0, 0]).astype(BF16)
        return lax.dynamic_update_slice(lax.empty(_form(kind, r, c), BF16), own[None] if kind == "blk" else own, at)

    specs = {g: [(kind, r, c) for _, kind, r, c in grp] for g, grp in GROUPS.items()}
    plans = {(g, st): _gather_plan(specs[g], st) for g in GROUPS for st in ("ici", "d2d")}
    buf_f1 = [whole_with_own(*w) for w in GROUPS["f1"]]
    ss_0, rs_0, buf_f1, tok_0 = _copies_start("gather_f1_ici", buf_f1, 3 * len(buf_f1), plans["f1", "ici"][0])
    bufs = {g: [whole_with_own(*w, tok=tok_0) for w in GROUPS[g]] for g in ("mx", "f2")}
    buf_f1 = _copies_wait("gather_f1_ici_wait", buf_f1, ss_0, rs_0, bufs["mx"][0], plans["f1", "ici"][1])
    ss_1, rs_1, buf_f1, tok_1 = _copies_start("gather_f1_d2d", buf_f1, 3 * len(buf_f1), plans["f1", "d2d"][0])
    h1 = _rows_call(_rms, [(x0, 0, D_MODEL)], [vec["ffn1_norm"] + tok_1[0, 0]], [(D_MODEL, BF16)], tm=512, name="ffn1_norm")[0]
    buf_f1 = _copies_wait("gather_f1_d2d_wait", buf_f1, ss_1, rs_1, h1, plans["f1", "d2d"][1])
    wb = dict(zip([w[0] for w in GROUPS["f1"]], buf_f1))
    ss_a, rs_a, buf_mx, tok_a = _copies_start("gather_mx_ici", bufs["mx"], 3 * len(bufs["mx"]), plans["mx", "ici"][0],
                                              after=wb["ffn1_w_gate"])

    inv_freq = 1.0 / (ROPE_THETA ** (jnp.arange(0, HEAD, 2, dtype=F32) / HEAD))
    ang = positions[0].astype(F32)[:, None] * inv_freq
    cos, sin = jnp.cos(ang), jnp.sin(ang)
    cos2, sin2 = jnp.concatenate([cos, cos], axis=1), jnp.concatenate([-sin, sin], axis=1)

    x1, ffn1_saved = _ffn_fwd(x0, vec["ffn1_norm"], wb["ffn1_w_gate"], wb["ffn1_w_up"], wb["ffn1_w_down"], "ffn1", h=h1, dep=tok_a)
    buf_mx = _copies_wait("gather_mx_ici_wait", buf_mx, ss_a, rs_a, x1, plans["mx", "ici"][1])
    ss_b, rs_b, buf_mx, tok_b = _copies_start("gather_mx_d2d", buf_mx, 3 * len(buf_mx), plans["mx", "d2d"][0])
    ss_c, rs_c, buf_f2, tok_c = _copies_start("gather_f2_ici", bufs["f2"], 3 * len(bufs["f2"]), plans["f2", "ici"][0])
    h = _rows_call(_rms, [(x1, 0, D_MODEL)], [vec["mix_norm"] + (tok_b[0, 0] + tok_c[0, 0])], [(D_MODEL, BF16)], tm=256,
                   name="mix_norm")[0]
    buf_mx = _copies_wait("gather_mx_d2d_wait", buf_mx, ss_b, rs_b, h, plans["mx", "d2d"][1])
    wb.update(zip([w[0] for w in GROUPS["mx"]], buf_mx))
    w_in_all = wb["w_in"]
    w_in_r, w_in_a, w_in_g = w_in_all[:, :RWKV_COLS], w_in_all[:, RWKV_COLS:RWKV_COLS + ATTN_COLS], w_in_all[:, RWKV_COLS + ATTN_COLS:]
    lora = _lora_split(wb["lora"])
    w2, a2, g2 = lora["rwkv_w2"], lora["rwkv_a2"], lora["rwkv_g2"]
    z_r = _mm(h, w_in_r, name="in_rwkv")
    z_a = _mm(h, w_in_a, name="in_attn")
    z_g = _mm(h, w_in_g, name="in_gate")

    zs = _shift_fwd(z_r, vec["rwkv_mu"])
    pre_params = [vec["rwkv_w0"], w2, vec["rwkv_a0"], a2, g2, vec["rwkv_k_k"], vec["rwkv_k_a"]]
    def pre_fwd(*t):
        res = _rwkv_pre(*t)
        return res[1], res[2], res[4], res[5], res[6]

    lw, k2, na, kb, gate_r = _rows_call(pre_fwd, [(zs, 0, RWKV_COLS)], pre_params, [(RWKV_DIM, F32)] * 5, tm=512, name="rwkv_pre")
    y_scan, s0s, invs = _wkv_fwd(zs, lw, k2, na, kb)
    buf_f2 = _copies_wait("gather_f2_ici_wait", buf_f2, ss_c, rs_c, y_scan, plans["f2", "ici"][1])
    ss_d, rs_d, buf_f2, tok_d = _copies_start("gather_f2_d2d", buf_f2, 3 * len(buf_f2), plans["f2", "d2d"][0])
    post_params = [vec["rwkv_gn_w"] + tok_d[0, 0], vec["rwkv_gn_b"], vec["rwkv_r_k"]]
    post_rows = [(y_scan, 0, RWKV_DIM), (zs, 0, RWKV_DIM), (k2, 0, RWKV_DIM), (zs, 2, RWKV_DIM), (gate_r, 0, RWKV_DIM)]
    y_rwkv = _rows_call(_rwkv_post, post_rows, post_params, [(RWKV_DIM, BF16)], tm=512, name="rwkv_post")[0]
    buf_f2 = _copies_wait("gather_f2_d2d_wait", buf_f2, ss_d, rs_d, y_rwkv, plans["f2", "d2d"][1])
    wb.update(zip([w[0] for w in GROUPS["f2"]], buf_f2))
    w_brr, w_bra = wb["w_br_rwkv"], wb["w_br_attn"]
    w_o = wb["w_out"].reshape(D_MODEL, D_MODEL)
    w_pp, w_pg = wb["ple_w_proj"], wb["ple_w_gate"].reshape(D_MODEL, D_MODEL)

    def qk_fwd(qt, kt, ct, st, qg, kg):
        return _norm_rope(qt, qg, ct, st), _norm_rope(kt, kg, ct, st)

    qk_rows = [(z_a, 0, ATTN_DIM), (z_a, 1, ATTN_DIM), (cos2, 0, HEAD), (sin2, 0, HEAD)]
    q_rot, k_rot = _rows_call(qk_fwd, qk_rows, [vec["q_norm"], vec["k_norm"]], [(ATTN_DIM, BF16)] * 2, tm=512, name="attn_pre")
    def group(t, g, off=0):
        return t[:, off + g * GROUP_DIM:off + (g + 1) * GROUP_DIM].astype(BF16)

    qkv = [(group(q_rot, g), group(k_rot, g), group(z_a, g, 2 * ATTN_DIM)) for g in range(len(ATTN_DILATIONS))]
    outs, lses = zip(*[_attn_fwd(*qkv[g], d) for g, d in enumerate(ATTN_DILATIONS)])
    comb_rows = [(t, 0, GROUP_DIM) for t in outs + lses]
    y_attn = _rows_call(_attn_combine, comb_rows, [], [(GROUP_DIM, BF16)], tm=512, name="attn_combine")[0]

    br = _mm(y_rwkv, w_brr, name="branch_rwkv")
    ba = _mm(y_attn, w_bra, name="branch_attn")
    merge_rows = [(z_g, 0, D_MODEL), (z_g, 1, D_MODEL), (br, 0, D_MODEL), (ba, 0, D_MODEL)]
    merged = _rows_call(_merge, merge_rows, [], [(D_MODEL, BF16)], tm=512, name="merge")[0]
    x2, h2 = _mm(merged, w_o, res=x1, post=_norm_post(vec["ffn2_norm"]), name="out_proj")
    (x3, hp), ffn2_saved = _ffn_fwd(x2, vec["ffn2_norm"], wb["ffn2_w_gate"], wb["ffn2_w_up"], wb["ffn2_w_down"], "ffn2", h=h2,
                                   next_gain=vec["ple_norm"])
    pg = _mm(hp, w_pg, name="ple_gate")
    pp = _mm(p_tok, w_pp, name="ple_proj")

    def head(x3t, pgt, ppt, tt):
        sg = _sigmoid(pgt)
        err = x3t + sg * ppt - tt
        dx4 = err * (1.0 / D_MODEL)
        loss = 0.5 * jnp.sum(jnp.mean(err * err, axis=-1, keepdims=True), axis=0, keepdims=True)
        return dx4, dx4 * ppt * sg * (1.0 - sg), dx4 * sg, jnp.broadcast_to(loss, (8, 128))

    head_rows = [(x3, 0, D_MODEL), (pg, 0, D_MODEL), (pp, 0, D_MODEL), (tgt, 0, D_MODEL)]
    dx4, dpg, dpp, loss_tile = _rows_call(head, head_rows, [], [(D_MODEL, F32), (D_MODEL, BF16), (D_MODEL, BF16)], [(8, 128)],
                                          tm=512, name="ple_loss")

    c_arr = jnp.reshape(ci, (1,)).astype(jnp.int32)
    mc_arr = jnp.stack([me, ci]).astype(jnp.int32)
    red = {g: _GroupReduce(g, grp, c_arr, mc_arr) for g, grp in REDUCE_GROUPS.items()}

    done = {}

    def update(summed):
        for n, g2d in summed.items():
            if n == "lora":
                w_, m_, v_ = (_lora_stack({k: t[k][0] for k, _ in LORA}) for t in (wts, mom_m, mom_v))
            else:
                w_, m_, v_ = laid(wts, n), laid(mom_m, n), laid(mom_v, n)
            done[n] = _adamw(w_, g2d, m_, v_, name=f"adamw_{n}", with_grad=True)
    gw, gs = {}, {}
    gw["ple_w_proj"] = _mm(p_tok, dpp, ta=True, name="ple_proj_dw")
    gw["ple_w_gate"] = _mm(hp, dpg, ta=True, name="ple_gate_dw")
    dx3, dx3_bf, gs["ple_norm"] = _mm(dpg, w_pg, tb=True, post=_norm_bwd_post(x3, vec["ple_norm"], dx4), name="ple_gate_dx")
    dx2, dx2_bf, gs["ffn2_norm"], gw["ffn2_w_gate"], gw["ffn2_w_up"], gw["ffn2_w_down"] = _ffn_bwd(
        dx3, dx3_bf, ffn2_saved, vec["ffn2_norm"], wb["ffn2_w_gate"], wb["ffn2_w_up"], wb["ffn2_w_down"], "ffn2")
    gw["ple_w_gate"] = gw["ple_w_gate"].reshape(N_CHIPS, D_MODEL // N_CHIPS, D_MODEL)
    tok = red["f2"].swap_start([gw[w[0]] for w in REDUCE_GROUPS["f2"]])
    gw["w_out"] = _mm(merged, dx2_bf, ta=True, name="out_proj_dw")
    dmerged = _mm(dx2_bf, w_o, tb=True, dep=tok, name="out_proj_dx")

    def merge_bwd(zgr, zga, brt, bat, ct):
        _, vjp = jax.vjp(_merge, zgr, zga, brt, bat)
        d1, d2, d3, d4 = vjp(ct)
        return jnp.concatenate([d1, d2], axis=1), d3, d4

    dz_g, dbr, dba = _rows_call(merge_bwd, merge_rows + [(dmerged, 0, D_MODEL)], [],
                                [(2 * D_MODEL, BF16), (D_MODEL, BF16), (D_MODEL, BF16)], tm=512, name="merge_bwd")
    tok = red["f2"].swap_wait_ici_start(dz_g)
    gw["w_br_rwkv"] = _mm(y_rwkv, dbr, ta=True, name="branch_rwkv_dw")
    gw["w_br_attn"] = _mm(y_attn, dba, ta=True, name="branch_attn_dw")
    dy_rwkv = _mm(dbr, w_brr, tb=True, dep=tok, name="branch_rwkv_dx")
    dy_attn = _mm(dba, w_bra, tb=True, dep=tok, name="branch_attn_dx")

    def comb_bwd(*t):
        _, vjp = jax.vjp(_attn_combine, *t[:6])
        return vjp(t[6])

    dcomb = _rows_call(comb_bwd, comb_rows + [(dy_attn, 0, GROUP_DIM)], [], [(GROUP_DIM, F32)] * 6, tm=512, name="attn_combine_bwd")
    dqs, dks, dvs = zip(*[_attn_bwd(*qkv[g], d, dcomb[g], dcomb[3 + g]) for g, d in enumerate(ATTN_DILATIONS)])

    def qk_bwd(qt, kt, ct, st, *rest):
        dq = jnp.concatenate(rest[0:3], axis=1)
        dk = jnp.concatenate(rest[3:6], axis=1)
        qg, kg = rest[9], rest[10]
        _, vjp = jax.vjp(lambda a_, b_, c_, d_: qk_fwd(a_, b_, ct, st, c_, d_), qt, kt, qg, kg)
        dqt, dkt, dqg, dkg = vjp((dq, dk))
        return jnp.concatenate((dqt, dkt) + tuple(rest[6:9]), axis=1), dqg, dkg

    dz_a, gs["q_norm"], gs["k_norm"] = _rows_call(
        qk_bwd, qk_rows + [(t, 0, GROUP_DIM) for t in dqs + dks + dvs], [vec["q_norm"], vec["k_norm"]],
        [(ATTN_COLS, BF16)], [(1, HEAD), (1, HEAD)], tm=512, name="attn_pre_bwd")
    tok = red["f2"].ici_wait_join_start(dz_a)

    def post_bwd(*t):
        _, vjp = jax.vjp(_rwkv_post, *t[:5], *t[6:])
        return vjp(t[5])

    dy_scan, dr_post, dk2_post, dv_post, dgate_r, gs["rwkv_gn_w"], gs["rwkv_gn_b"], gs["rwkv_r_k"] = _rows_call(
        post_bwd, post_rows + [(dy_rwkv, 0, RWKV_DIM)], post_params, [(RWKV_DIM, F32)] * 5, [(1, RWKV_DIM)] * 3,
        tm=512, name="rwkv_post_bwd", dep=tok)
    update(red["f2"].join_wait(dy_scan))
    dr_s, dlw, dk2_s, dv_s, dna, dkb = _wkv_bwd(zs, lw, k2, na, kb, s0s, invs, dy_scan)

    def pre_bwd(zt, c_r1, c_r2, c_lw, c_k1, c_k2, c_v1, c_v2, c_a, c_b, c_g, *params):
        _, vjp = jax.vjp(_rwkv_pre, zt, *params)
        return vjp((c_r1 + c_r2, c_lw, c_k1 + c_k2, c_v1 + c_v2, c_a, c_b, c_g))

    pre_cts = [dr_s, dr_post, dlw, dk2_s, dk2_post, dv_s, dv_post, dna, dkb, dgate_r]
    dzs, gs["rwkv_w0"], g_w2, gs["rwkv_a0"], g_a2, g_g2, gs["rwkv_k_k"], gs["rwkv_k_a"] = _rows_call(
        pre_bwd, [(zs, 0, RWKV_COLS)] + [(t, 0, RWKV_DIM) for t in pre_cts], pre_params, [(RWKV_COLS, F32)],
        [q.shape for q in pre_params], tm=512, name="rwkv_pre_bwd")
    dz_r, gs["rwkv_mu"] = _shift_bwd(z_r, vec["rwkv_mu"], dzs)

    g_w_in = jnp.concatenate([_mm(h, dz_r, ta=True, name="in_rwkv_dw"), _mm(h, dz_a, ta=True, name="in_attn_dw"),
                              _mm(h, dz_g, ta=True, name="in_gate_dw")], axis=1)
    gw["w_in"], gw["lora"] = g_w_in, jnp.concatenate([g_w2, g_a2, g_g2], axis=0)
    gw["w_out"] = gw["w_out"].reshape(N_CHIPS, D_MODEL // N_CHIPS, D_MODEL)
    tok = red["mx"].swap_start([gw[w[0]] for w in REDUCE_GROUPS["mx"]])
    dh = _mm(dz_r, w_in_r, tb=True, dep=tok, name="in_rwkv_dx")
    dh = _mm(dz_a, w_in_a, tb=True, res=dh, name="in_attn_dx")
    dx1, dx1_bf, gs["mix_norm"] = _mm(dz_g, w_in_g, tb=True, res=dh, post=_norm_bwd_post(x1, vec["mix_norm"], dx2), name="in_gate_dx")
    tok_mx = red["mx"].swap_wait_ici_start(dx1_bf)
    hooks = {"down": lambda d_wd: red["f1d"].swap_start([d_wd], after=tok_mx),
             "mid": lambda dgate: red["f1d"].swap_wait_ici_start(dgate),
             "dw": lambda d_wgt, d_wut: red["f1g"].swap_start([d_wgt, d_wut]),
             "dx": lambda part: red["f1g"].swap_wait_ici_start(part) + red["f1d"].ici_wait_join_start(part),
             "end": lambda dx_: tokens.setdefault("mx_join", red["mx"].ici_wait_join_start(dx_))}
    tokens = {}
    dx0, _, gs["ffn1_norm"], gw["ffn1_w_gate"], gw["ffn1_w_up"], gw["ffn1_w_down"] = _ffn_bwd(
        dx1, dx1_bf, ffn1_saved, vec["ffn1_norm"], wb["ffn1_w_gate"], wb["ffn1_w_up"], wb["ffn1_w_down"], "ffn1", hooks=hooks)

    flat = jnp.concatenate([gs[n].reshape(-1) for n, _ in SMALL] + [loss_tile[0, 0:1]])
    small_buf = jnp.pad(flat, (0, SMALL_ROWS * PACK_COLS - flat.shape[0])).reshape(SMALL_ROWS, PACK_COLS)
    small_sum = _all_reduce_small(small_buf)
    n_small = sum(sz for _, sz in SMALL)
    loss = small_sum.reshape(-1)[n_small]
    grad_small = _unpack_small(small_sum, {n: wts[n].shape for n, _ in SMALL})
    d_s, m_s, v_s = _adamw(_pack_small(wts), small_sum, _pack_small(mom_m), _pack_small(mom_v), name="adamw_small",
                           dep=tokens["mx_join"])
    shapes = {n: wts[n].shape for n, _ in SMALL}
    d_s, m_s, v_s = _unpack_small(d_s, shapes), _unpack_small(m_s, shapes), _unpack_small(v_s, shapes)
    grads, deltas, new_m, new_v = {}, {}, {}, {}
    for n, _ in SMALL:
        grads[n], deltas[n], new_m[n], new_v[n] = grad_small[n], d_s[n], m_s[n], v_s[n]

    for g in ("mx", "f1d"):
        update(red[g].join_wait(m_s["ffn1_norm"]))
    tok = red["f1g"].ici_wait_join_start(done["w_in"][1])
    update(red["f1g"].join_wait(tok))
    for n, res in done.items():
        for store, val in zip((grads, deltas, new_m, new_v), res):
            if n == "lora":
                store.update({k: t[None] for k, t in _lora_split(val).items()})
            else:
                store[n] = (jnp.transpose(val) if n in TRANSPOSED else val)[None]

    return (loss, dx0[None], *[grads[n] for n in WEIGHTS], *[deltas[n] for n in WEIGHTS],
            *[new_m[n] for n in WEIGHTS], *[new_v[n] for n in WEIGHTS])
```

```python
import functools

import jax
import jax.numpy as jnp
from jax import lax
from jax.experimental import pallas as pl
from jax.experimental.pallas import tpu as pltpu

F32, BF16 = jnp.float32, jnp.bfloat16
HI = lax.Precision.HIGHEST
MESH = pl.DeviceIdType.MESH
SDS = jax.ShapeDtypeStruct

D_MODEL = 1024
HEAD = 64
RWKV_HEADS = 8
RWKV_DIM = RWKV_HEADS * HEAD
DECAY_LORA, ICLR_LORA, GATE_LORA = 64, 64, 128
GN_EPS = 64e-5
RMS_EPS = 1e-6
ATTN_DILATIONS = (1, 4, 16)
BAND = 128
ATTN_DIM = 768
GROUP_DIM = 256
ATTN_CLASSES_PER_STEP = 4
ROPE_THETA = 10000.0
NEG_INF = -1e30
RWKV_COLS = 3 * RWKV_DIM + DECAY_LORA + ICLR_LORA + GATE_LORA
ATTN_COLS = 3 * ATTN_DIM
ADAM_LR, ADAM_B1, ADAM_B2, ADAM_EPS, ADAM_WD, ADAM_STEP = 0.001, 0.9, 0.999, 1e-08, 0.01, 10

WKV_CHUNK = 64
WKV_HEADS_PER_STEP = 8
N_CHIPS = 4
PACK_COLS = 1024
VMEM_LIMIT = 48 * 1024 * 1024

TRANSPOSED = ("ffn1_w_gate", "ffn1_w_up", "ffn2_w_gate", "ffn2_w_up")
LORA = (("rwkv_w2", 64), ("rwkv_a2", 64), ("rwkv_g2", 128))
_FFN1 = (("ffn1_w_gate", "blk", 704, 1024), ("ffn1_w_up", "blk", 704, 1024), ("ffn1_w_down", "blk", 704, 1024))
_FFN2 = (("ffn2_w_gate", "blk", 704, 1024), ("ffn2_w_up", "blk", 704, 1024), ("ffn2_w_down", "blk", 704, 1024))
_IN = (("w_in", "col", 1024, 1536), ("lora", "col", 256, 128))
_BRANCH = (("w_br_rwkv", "col", 512, 256), ("w_br_attn", "col", 256, 256), ("w_out", "blk", 256, 1024))
_PLE = (("ple_w_gate", "blk", 256, 1024), ("ple_w_proj", "col", 256, 256))
GROUPS = {"f1": _FFN1, "mx": _IN, "f2": _BRANCH + _FFN2 + _PLE}
REDUCE_GROUPS = {"f2": _FFN2 + _PLE, "mx": _IN + _BRANCH, "f1d": _FFN1[2:], "f1g": _FFN1[:2]}
SMALL = (
    ("ffn1_norm", 1024), ("mix_norm", 1024), ("ffn2_norm", 1024), ("ple_norm", 1024), ("rwkv_mu", 1792),
    ("rwkv_w0", 512), ("rwkv_a0", 512), ("rwkv_k_k", 512), ("rwkv_k_a", 512), ("rwkv_r_k", 512),
    ("rwkv_gn_w", 512), ("rwkv_gn_b", 512), ("q_norm", 64), ("k_norm", 64),
)
SMALL_ROWS = 16
WEIGHTS = (
    "ffn1_norm", "ffn1_w_gate", "ffn1_w_up", "ffn1_w_down", "mix_norm", "w_in", "rwkv_mu", "rwkv_w0", "rwkv_w2",
    "rwkv_a0", "rwkv_a2", "rwkv_g2", "rwkv_k_k", "rwkv_k_a", "rwkv_r_k", "rwkv_gn_w", "rwkv_gn_b", "q_norm", "k_norm",
    "w_br_rwkv", "w_br_attn", "w_out", "ffn2_norm", "ffn2_w_gate", "ffn2_w_up", "ffn2_w_down", "ple_norm",
    "ple_w_gate", "ple_w_proj",
)


def _row_tile(n, most=704):
    for t in range(most - most % 16, 0, -16):
        if n % t == 0:
            return t
    return n


def _pick(n, cands):
    for c in cands:
        if n % c == 0:
            return c
    return n


def _mm(a, b, *, ta=False, tb=False, sum_blocks=False, out_dtype=F32, res=None, alpha=1.0, dep=None, post=None, name):
    flat = a.ndim == 2 and b.ndim == 2
    a3 = a if a.ndim == 3 else a[None]
    b3 = b if b.ndim == 3 else b[None]
    na, nbb = a3.shape[0], b3.shape[0]
    nblk = max(na, nbb)
    kdim, m = (a3.shape[1], a3.shape[2]) if ta else (a3.shape[2], a3.shape[1])
    n = b3.shape[1] if tb else b3.shape[2]
    assert (b3.shape[2] if tb else b3.shape[1]) == kdim
    tm = _pick(m, (1024, 512, 256, 128) if post is None else (512, 256, 128))
    tn = _pick(n, (1024, 896, 768, 512, 256, 128))
    tk = kdim if kdim <= 2304 else _pick(kdim, (1024, 512, 256, 128))
    nk = kdim // tk
    direct = nk == 1 and not sum_blocks

    if sum_blocks:
        grid = (m // tm, n // tn, nblk, nk)

        def ids(i, c, j, k):
            return i, c, j, k
    else:
        grid = (nblk, m // tm, n // tn, nk)

        def ids(j, i, c, k):
            return i, c, j, k

    def amap(*g):
        i, c, j, k = ids(*g)
        jj = j if na > 1 else 0
        return (jj, k, i) if ta else (jj, i, k)

    def bmap(*g):
        i, c, j, k = ids(*g)
        jj = j if nbb > 1 else 0
        return (jj, c, k) if tb else (jj, k, c)

    if sum_blocks:
        oshape, oblk = (m, n), (tm, tn)

        def omap(*g):
            i, c, j, k = ids(*g)
            return i, c
    else:
        oshape, oblk = (nblk, m, n), (1, tm, tn)

        def omap(*g):
            i, c, j, k = ids(*g)
            return j, i, c

    dn = (((0 if ta else 1,), (1 if tb else 0,)), ((), ()))
    has_res = res is not None
    p_f, p_rows, p_params, p_dtypes, p_accs = post if post is not None else (None, [], [], [], [])
    assert post is None or sum_blocks or flat
    n_in = 2 + has_res + len(p_rows) + len(p_params) + (dep is not None)

    def tile_map(*g):
        i, c, j, k = ids(*g)
        return i, c

    def body(*refs):
        refs = list(refs)
        acc = None if direct else refs.pop()
        o_refs = refs[n_in:]
        a_ref, b_ref = refs[0], refs[1]
        r_ref = refs[2] if has_res else None
        pr_refs = refs[2 + has_res:2 + has_res + len(p_rows)]
        pp_refs = refs[2 + has_res + len(p_rows):2 + has_res + len(p_rows) + len(p_params)]
        first_tile = jnp.logical_and(pl.program_id(0 if sum_blocks else 1) == 0, pl.program_id(1 if sum_blocks else 2) == 0)

        def finish(v):
            if alpha != 1.0:
                v = v * alpha
            if has_res:
                v = v + r_ref[...].reshape(v.shape).astype(F32)
            if post is None:
                o_refs[0][...] = v.reshape(o_refs[0].shape).astype(o_refs[0].dtype)
                return
            outs = p_f(v, *[t[...] for t in pr_refs], *[t[...] for t in pp_refs])
            for o_ref, val in zip(o_refs, outs[:len(p_dtypes)]):
                o_ref[...] = val.astype(o_ref.dtype)
            for o_ref, val in zip(o_refs[len(p_dtypes):], outs[len(p_dtypes):]):
                @pl.when(first_tile)
                def _():
                    o_ref[...] = jnp.zeros_like(o_ref)

                o_ref[...] += val.reshape(o_ref.shape)

        if direct:
            finish(lax.dot_general(a_ref[0].astype(BF16), b_ref[0].astype(BF16), dn, preferred_element_type=F32))
            return
        k = pl.program_id(3)
        if sum_blocks:
            j = pl.program_id(2)
            first = jnp.logical_and(j == 0, k == 0)
            last = jnp.logical_and(j == nblk - 1, k == nk - 1)
        else:
            first, last = k == 0, k == nk - 1

        @pl.when(first)
        def _():
            acc[...] = jnp.zeros_like(acc)

        acc[...] += lax.dot_general(a_ref[0].astype(BF16), b_ref[0].astype(BF16), dn, preferred_element_type=F32)

        @pl.when(last)
        def _():
            finish(acc[...])

    in_specs = [pl.BlockSpec((1, tk, tm) if ta else (1, tm, tk), amap), pl.BlockSpec((1, tn, tk) if tb else (1, tk, tn), bmap)]
    args = [a3, b3]
    if has_res:
        res3 = res if (sum_blocks or res.ndim == 3) else res[None]
        in_specs.append(pl.BlockSpec(oblk, omap))
        args.append(res3)
    in_specs += [pl.BlockSpec((tm, tn), tile_map) for _ in p_rows]
    in_specs += [pl.BlockSpec(t.shape, functools.partial(lambda *g, nd: (0,) * nd, nd=t.ndim)) for t in p_params]
    args += list(p_rows) + list(p_params)
    if dep is not None:
        in_specs.append(pl.BlockSpec(memory_space=pl.ANY))
        args.append(dep)
    if post is None:
        out_specs, out_shape = pl.BlockSpec(oblk, omap), SDS(oshape, out_dtype)
        semantics = ("parallel", "parallel", "arbitrary", "arbitrary") if sum_blocks else ("parallel", "parallel", "parallel", "arbitrary")
    else:
        out_specs = [pl.BlockSpec((tm, tn), tile_map) for _ in p_dtypes]
        out_specs += [pl.BlockSpec(tuple(sh), functools.partial(lambda *g, nd: (0,) * nd, nd=len(sh))) for sh in p_accs]
        out_shape = [SDS((m, n), dt) for dt in p_dtypes] + [SDS(tuple(sh), F32) for sh in p_accs]
        semantics = ("arbitrary",) * 4
    out = pl.pallas_call(
        body,
        name=name,
        grid=grid,
        in_specs=in_specs,
        out_specs=out_specs,
        out_shape=out_shape,
        scratch_shapes=[] if direct else [pltpu.VMEM((tm, tn), F32)],
        compiler_params=pltpu.CompilerParams(dimension_semantics=semantics, vmem_limit_bytes=VMEM_LIMIT),
    )(*args)
    if post is not None:
        return out
    if flat and not sum_blocks:
        out = out[0]
    return out


def _rows_call(f, rows, params, outs, accs=(), *, tm, name, dep=None):
    s = rows[0][0].shape[0]
    nr, npar, no = len(rows), len(params), len(outs)
    nin = nr + npar + (0 if dep is None else 1)
    in_specs = [pl.BlockSpec((tm, w), functools.partial(lambda i, cb: (i, cb), cb=cb)) for (_, cb, w) in rows]
    in_specs += [pl.BlockSpec(p.shape, functools.partial(lambda i, nd: (0,) * nd, nd=p.ndim)) for p in params]
    if dep is not None:
        in_specs.append(pl.BlockSpec(memory_space=pl.ANY))
    out_shape = [SDS((s, w), dt) for (w, dt) in outs] + [SDS(tuple(sh), F32) for sh in accs]
    out_specs = [pl.BlockSpec((tm, w), lambda i: (i, 0)) for (w, _) in outs]
    out_specs += [pl.BlockSpec(tuple(sh), functools.partial(lambda i, nd: (0,) * nd, nd=len(sh))) for sh in accs]

    def body(*refs):
        rin, pin = refs[:nr], refs[nr:nr + npar]
        oo, ao = refs[nin:nin + no], refs[nin + no:]
        res = f(*[r[...] for r in rin], *[p[...] for p in pin])
        if not isinstance(res, (tuple, list)):
            res = (res,)
        for o_ref, v in zip(oo, res[:no]):
            o_ref[...] = v.astype(o_ref.dtype)
        i = pl.program_id(0)
        for a_ref, v in zip(ao, res[no:]):
            @pl.when(i == 0)
            def _():
                a_ref[...] = jnp.zeros_like(a_ref)

            a_ref[...] += v.reshape(a_ref.shape)

    res = pl.pallas_call(
        body,
        name=name,
        grid=(s // tm,),
        in_specs=in_specs,
        out_specs=out_specs,
        out_shape=out_shape,
        compiler_params=pltpu.CompilerParams(dimension_semantics=("arbitrary",), vmem_limit_bytes=VMEM_LIMIT),
    )(*[r[0] for r in rows], *params, *([] if dep is None else [dep]))
    return res


def _mmv(a, b, mode):
    ca = 0 if mode[0] == "t" else 1
    cb = 1 if mode[1] == "t" else 0
    return lax.dot_general(a.astype(BF16), b.astype(BF16), (((ca,), (cb,)), ((), ())), preferred_element_type=F32)


@functools.partial(jax.custom_vjp, nondiff_argnums=(2,))
def _bdot(a, b, mode):
    return _mmv(a, b, mode)


def _bdot_fwd(a, b, mode):
    return _mmv(a, b, mode), (a, b)


def _bdot_bwd(mode, saved, g):
    a, b = saved
    if mode == "nn":
        return _mmv(g, b, "nt"), _mmv(a, g, "tn")
    if mode == "nt":
        return _mmv(g, b, "nn"), _mmv(g, a, "tn")
    return _mmv(b, g, "nt"), _mmv(a, g, "nn")


_bdot.defvjp(_bdot_fwd, _bdot_bwd)


def _hdot(a, b, mode="nn", precision=HI):
    ca = 0 if mode[0] == "t" else 1
    cb = 1 if mode[1] == "t" else 0
    return lax.dot_general(a, b, (((ca,), (cb,)), ((), ())), precision=precision, preferred_element_type=F32)


def _segsum(x):
    c = x.shape[-1]
    blk = min(c, 256)
    r = lax.broadcasted_iota(jnp.int32, (blk, blk), 0) >> 6
    q = lax.broadcasted_iota(jnp.int32, (blk, blk), 1) >> 6
    ones = jnp.where(r == q, 1.0, 0.0).astype(F32)
    parts = [_hdot(x[:, i:i + blk], ones, precision=lax.Precision.HIGH) for i in range(0, c, blk)]
    return parts[0] if len(parts) == 1 else jnp.concatenate(parts, axis=1)


def _sigmoid(x):
    return jax.nn.sigmoid(x)


def _softplus(x):
    return jnp.maximum(x, 0.0) + jnp.log(1.0 + jnp.exp(-jnp.abs(x)))


def _rms(x, gain):
    return x * lax.rsqrt(jnp.mean(x * x, axis=-1, keepdims=True) + RMS_EPS) * gain


def _swiglu_act(gate, up):
    return gate * _sigmoid(gate) * up


def _rwkv_pre(zs, w0, w2, a0, a2, g2, k_k, k_a):
    r, k, v = zs[:, 0:512], zs[:, 512:1024], zs[:, 1024:1536]
    lora = zs[:, 1536:1792]
    wd, ad, gd = lora[:, 0:64], lora[:, 64:128], lora[:, 128:256]
    w = -_softplus(-(w0 + _bdot(jnp.tanh(wd), w2, "nn"))) - 0.5
    a = _sigmoid(a0 + _bdot(ad, a2, "nn"))
    g = _bdot(_sigmoid(gd), g2, "nn")
    kk = k * k_k
    kk = kk * lax.rsqrt(jnp.maximum(_segsum(kk * kk), 1e-24))
    k2 = k * (1.0 + (a - 1.0) * k_a)
    return r, -jnp.exp(w), k2, v, -kk, kk * a, g


def _rwkv_post(y, r, k2, v, g, gn_w, gn_b, r_k):
    mean = _segsum(y) * (1.0 / HEAD)
    yc = y - mean
    var = _segsum(yc * yc) * (1.0 / HEAD)
    yn = yc * lax.rsqrt(var + GN_EPS) * gn_w + gn_b
    bonus = _segsum(r * k2 * r_k) * v
    return (yn + bonus) * g


def _swap_halves(x):
    lane = lax.broadcasted_iota(jnp.int32, x.shape, 1)
    return jnp.where((lane & 32) == 0, jnp.roll(x, -32, axis=1), jnp.roll(x, 32, axis=1))


def _norm_rope(x, gain, cos, sin):
    heads = x.shape[1] // HEAD
    def rep(t):
        return jnp.concatenate([t] * heads, axis=1)

    xn = x * lax.rsqrt(_segsum(x * x) * (1.0 / HEAD) + RMS_EPS) * rep(gain)
    return xn * rep(cos) + _swap_halves(xn) * rep(sin)


def _attn_combine(o0, o1, o2, l0, l1, l2):
    m = jnp.maximum(jnp.maximum(l0, l1), l2)
    e0, e1, e2 = jnp.exp(l0 - m), jnp.exp(l1 - m), jnp.exp(l2 - m)
    return (e0 * o0 + e1 * o1 + e2 * o2) / (e0 + e1 + e2)


def _merge(zgr, zga, br, ba):
    return _sigmoid(zgr) * br + _sigmoid(zga) * ba


def _attn_block(q, kp, kc, vp, vc, has_prev):
    iq = lax.broadcasted_iota(jnp.int32, (1, BAND, BAND), 1)
    ik = lax.broadcasted_iota(jnp.int32, (1, BAND, BAND), 2)
    s_c = jnp.where(iq >= ik, _bdotb(q, kc, "nt") * (HEAD ** -0.5), NEG_INF)
    s_p = jnp.where(jnp.logical_and(iq <= ik, has_prev), _bdotb(q, kp, "nt") * (HEAD ** -0.5), NEG_INF)
    m = lax.stop_gradient(jnp.maximum(jnp.max(s_c, axis=-1, keepdims=True), jnp.max(s_p, axis=-1, keepdims=True)))
    e_c, e_p = jnp.exp(s_c - m), jnp.exp(s_p - m)
    l = jnp.sum(e_c, axis=-1, keepdims=True) + jnp.sum(e_p, axis=-1, keepdims=True)
    o = (_bdotb(e_c, vc) + _bdotb(e_p, vp)) / l
    return o, jnp.broadcast_to(m + jnp.log(l), o.shape)


def _mmb(a, b, cb):
    return lax.dot_general(a.astype(BF16), b.astype(BF16), (((2,), (cb,)), ((0,), (0,))), preferred_element_type=F32)


@functools.partial(jax.custom_vjp, nondiff_argnums=(2,))
def _bdotb1(a, b, cb):
    return _mmb(a, b, cb)


def _bdotb1_fwd(a, b, cb):
    return _mmb(a, b, cb), (a, b)


def _bdotb1_bwd(cb, saved, g):
    a, b = saved
    if cb == 1:
        return _mmb(g, b, 2), _mmb(jnp.swapaxes(a, 1, 2), g, 1)
    return _mmb(g, b, 1), _mmb(jnp.swapaxes(g, 1, 2), a, 1)


_bdotb1.defvjp(_bdotb1_fwd, _bdotb1_bwd)


def _bdotb(a, b, mode="nn", precision=None):
    if mode[0] == "t":
        a = jnp.swapaxes(a, 1, 2)
    cb = 2 if mode[1] == "t" else 1
    if precision is None:
        return _bdotb1(a, b, cb)
    return lax.dot_general(a, b, (((2,), (cb,)), ((0,), (0,))), precision=precision, preferred_element_type=F32)


def _tri_inv_levels(a):
    t = a.shape[-1]
    row = lax.broadcasted_iota(jnp.int32, (1, t, t), 1)
    col = lax.broadcasted_iota(jnp.int32, (1, t, t), 2)
    x = jnp.where(row == col, 1.0, 0.0).astype(F32) + jnp.where(jnp.logical_and(row == col + 1, (row & 1) == 1), a, 0.0)
    sh = 1
    while (1 << sh) < t:
        m = jnp.logical_and((row >> sh) == (col >> sh) + 1, (row >> (sh + 1)) == (col >> (sh + 1)))
        x = x + _bdotb(_bdotb(x, jnp.where(m, a, 0.0)), x)
        sh += 1
    return x


@jax.custom_vjp
def _tri_inv(a):
    return _tri_inv_levels(a)


def _tri_inv_fwd(a):
    x = _tri_inv_levels(a)
    return x, x


def _tri_inv_bwd(x, g):
    xt = jnp.swapaxes(x, 1, 2)
    return (_bdotb(_bdotb(xt, g, precision=lax.Precision.HIGH), xt, precision=lax.Precision.HIGH),)


_tri_inv.defvjp(_tri_inv_fwd, _tri_inv_bwd)


@jax.custom_vjp
def _known_inv(a, x):
    return x


def _known_inv_fwd(a, x):
    return x, x


def _known_inv_bwd(x, g):
    return _tri_inv_bwd(x, g)[0], jnp.zeros_like(x)


_known_inv.defvjp(_known_inv_fwd, _known_inv_bwd)


def _wkv_chunk(s0, r, lw, k, v, a, b, inv=None, with_inv=False):
    nh, t, _ = r.shape
    row = lax.broadcasted_iota(jnp.int32, (1, t, t), 1)
    col = lax.broadcasted_iota(jnp.int32, (1, t, t), 2)
    incl, strict = row >= col, row > col
    ones = jnp.broadcast_to(jnp.where(incl, 1.0, 0.0).astype(F32), (nh, t, t))
    cum = _bdotb(ones, lw, precision=HI)
    c_end = cum[:, t - 1:t, :]
    e_in, e_ex, e_inv = jnp.exp(cum), jnp.exp(cum - lw), jnp.exp(-cum)
    at, rt, bt, kt = a * e_ex, r * e_in, b * e_inv, k * e_inv
    a_ab = jnp.where(strict, _bdotb(at, bt, "nt"), 0.0)
    a_ak = jnp.where(strict, _bdotb(at, kt, "nt"), 0.0)
    x = _tri_inv(a_ab) if inv is None else _known_inv(a_ab, inv)
    u = _bdotb(x, _bdotb(at, s0, "nt") + _bdotb(a_ak, v))
    y = (_bdotb(rt, s0, "nt") + _bdotb(jnp.where(incl, _bdotb(rt, bt, "nt"), 0.0), u)
         + _bdotb(jnp.where(incl, _bdotb(rt, kt, "nt"), 0.0), v))
    w_end = jnp.exp(c_end - cum)
    s1 = s0 * jnp.exp(c_end) + _bdotb(u, b * w_end, "tn") + _bdotb(v, k * w_end, "tn")
    return (y, s1, x) if with_inv else (y, s1)


def _shift_fwd(z, mu):
    s, c = z.shape
    tc = 256

    def body(z_ref, mu_ref, o_ref):
        zz = z_ref[...]
        row = lax.broadcasted_iota(jnp.int32, zz.shape, 0)
        prev = jnp.where(row == 0, 0.0, pltpu.roll(zz, 1, 0))
        o_ref[...] = zz + (prev - zz) * mu_ref[...]

    return pl.pallas_call(
        body, name="shift_fwd", grid=(c // tc,),
        in_specs=[pl.BlockSpec((s, tc), lambda j: (0, j)), pl.BlockSpec((1, tc), lambda j: (0, j))],
        out_specs=pl.BlockSpec((s, tc), lambda j: (0, j)), out_shape=SDS((s, c), F32),
        compiler_params=pltpu.CompilerParams(dimension_semantics=("parallel",), vmem_limit_bytes=VMEM_LIMIT),
    )(z, mu)


def _shift_bwd(z, mu, dzs):
    s, c = z.shape
    tc = 256

    def body(z_ref, mu_ref, d_ref, dz_ref, dmu_ref):
        zz, d, m = z_ref[...], d_ref[...], mu_ref[...]
        row = lax.broadcasted_iota(jnp.int32, zz.shape, 0)
        prev = jnp.where(row == 0, 0.0, pltpu.roll(zz, 1, 0))
        t = d * m
        nxt = jnp.where(row == s - 1, 0.0, pltpu.roll(t, s - 1, 0))
        dz_ref[...] = (d - t + nxt).astype(dz_ref.dtype)
        dmu_ref[...] = jnp.sum(d * (prev - zz), axis=0, keepdims=True)

    return pl.pallas_call(
        body, name="shift_bwd", grid=(c // tc,),
        in_specs=[pl.BlockSpec((s, tc), lambda j: (0, j)), pl.BlockSpec((1, tc), lambda j: (0, j)),
                  pl.BlockSpec((s, tc), lambda j: (0, j))],
        out_specs=[pl.BlockSpec((s, tc), lambda j: (0, j)), pl.BlockSpec((1, tc), lambda j: (0, j))],
        out_shape=[SDS((s, c), BF16), SDS((1, c), F32)],
        compiler_params=pltpu.CompilerParams(dimension_semantics=("parallel",), vmem_limit_bytes=VMEM_LIMIT),
    )(z, mu, dzs)


def _heads(x, nh):
    return jnp.stack([x[:, h * HEAD:(h + 1) * HEAD] for h in range(nh)], axis=0)


def _unheads(x):
    return jnp.concatenate([x[h] for h in range(x.shape[0])], axis=1)


def _wkv_fwd(zs, lw, k2, na, b):
    s = lw.shape[0]
    t, hb = WKV_CHUNK, WKV_HEADS_PER_STEP
    w = hb * HEAD
    nc, ng = s // t, RWKV_HEADS // hb

    def body(r_ref, v_ref, lw_ref, k_ref, a_ref, b_ref, y_ref, s0_ref, x_ref, state):
        @pl.when(pl.program_id(1) == 0)
        def _():
            state[...] = jnp.zeros_like(state)

        s0 = state[...]
        s0_ref[0] = s0
        y, s1, x = _wkv_chunk(s0, *[_heads(t_ref[...], hb) for t_ref in (r_ref, lw_ref, k_ref, v_ref, a_ref, b_ref)], with_inv=True)
        y_ref[...] = _unheads(y)
        x_ref[0] = x
        state[...] = s1

    def col(off):
        return pl.BlockSpec((t, w), functools.partial(lambda g, i, off: (i, g + off), off=off))

    return pl.pallas_call(
        body, name="wkv_fwd", grid=(ng, nc),
        in_specs=[col(0), col(2 * ng), col(0), col(0), col(0), col(0)],
        out_specs=[col(0), pl.BlockSpec((1, hb, HEAD, HEAD), lambda g, i: (i, g, 0, 0)),
                   pl.BlockSpec((1, hb, t, t), lambda g, i: (i, g, 0, 0))],
        out_shape=[SDS((s, RWKV_DIM), F32), SDS((nc, RWKV_HEADS, HEAD, HEAD), F32), SDS((nc, RWKV_HEADS, t, t), F32)],
        scratch_shapes=[pltpu.VMEM((hb, HEAD, HEAD), F32)],
        compiler_params=pltpu.CompilerParams(dimension_semantics=("parallel", "arbitrary"), vmem_limit_bytes=VMEM_LIMIT),
    )(zs, zs, lw, k2, na, b)


def _wkv_bwd(zs, lw, k2, na, b, s0s, invs, dy):
    s = lw.shape[0]
    t, hb = WKV_CHUNK, WKV_HEADS_PER_STEP
    w = hb * HEAD
    nc, ng = s // t, RWKV_HEADS // hb

    def body(r_ref, v_ref, lw_ref, k_ref, a_ref, b_ref, s0_ref, x_ref, dy_ref, dr_ref, dlw_ref, dk_ref, dv_ref, da_ref, db_ref, dstate):
        @pl.when(pl.program_id(1) == 0)
        def _():
            dstate[...] = jnp.zeros_like(dstate)

        _, vjp = jax.vjp(functools.partial(_wkv_chunk, inv=x_ref[0]), s0_ref[0],
                         *[_heads(t_ref[...], hb) for t_ref in (r_ref, lw_ref, k_ref, v_ref, a_ref, b_ref)])
        grads = vjp((_heads(dy_ref[...], hb), dstate[...]))
        dstate[...] = grads[0]
        for o_ref, gval in zip((dr_ref, dlw_ref, dk_ref, dv_ref, da_ref, db_ref), grads[1:]):
            o_ref[...] = _unheads(gval)

    def col(off):
        return pl.BlockSpec((t, w), functools.partial(lambda g, i, off: (nc - 1 - i, g + off), off=off))

    return pl.pallas_call(
        body, name="wkv_bwd", grid=(ng, nc),
        in_specs=[col(0), col(2 * ng), col(0), col(0), col(0), col(0),
                  pl.BlockSpec((1, hb, HEAD, HEAD), lambda g, i: (nc - 1 - i, g, 0, 0)),
                  pl.BlockSpec((1, hb, t, t), lambda g, i: (nc - 1 - i, g, 0, 0)), col(0)],
        out_specs=[col(0)] * 6,
        out_shape=[SDS((s, RWKV_DIM), F32)] * 6,
        scratch_shapes=[pltpu.VMEM((hb, HEAD, HEAD), F32)],
        compiler_params=pltpu.CompilerParams(dimension_semantics=("parallel", "arbitrary"), vmem_limit_bytes=VMEM_LIMIT),
    )(zs, zs, lw, k2, na, b, s0s, invs, dy)


def _attn_fwd(q, k, v, d):
    s = q.shape[0]
    l = s // d
    nb = l // BAND
    assert nb * BAND == l
    qv, kv, vv = (t.reshape(l, d * GROUP_DIM) for t in (q, k, v))
    width = min(d, ATTN_CLASSES_PER_STEP) * GROUP_DIM
    nh = width // HEAD

    def body(q_ref, kp_ref, kc_ref, vp_ref, vc_ref, o_ref, l_ref):
        has_prev = pl.program_id(1) > 0
        o, lse = _attn_block(*[_heads(t_ref[...].astype(F32), nh) for t_ref in (q_ref, kp_ref, kc_ref, vp_ref, vc_ref)], has_prev)
        o_ref[...] = _unheads(o)
        l_ref[...] = _unheads(lse)

    cur = pl.BlockSpec((BAND, width), lambda rho, i: (i, rho))
    prev = pl.BlockSpec((BAND, width), lambda rho, i: (jnp.maximum(i - 1, 0), rho))
    o, lse = pl.pallas_call(
        body, name=f"attn_fwd_d{d}", grid=(d * GROUP_DIM // width, nb),
        in_specs=[cur, prev, cur, prev, cur], out_specs=[cur, cur],
        out_shape=[SDS((l, d * GROUP_DIM), F32), SDS((l, d * GROUP_DIM), F32)],
        compiler_params=pltpu.CompilerParams(dimension_semantics=("parallel", "arbitrary"), vmem_limit_bytes=VMEM_LIMIT),
    )(qv, kv, kv, vv, vv)
    return o.reshape(s, GROUP_DIM), lse.reshape(s, GROUP_DIM)


def _attn_bwd(q, k, v, d, do, dlse):
    s = q.shape[0]
    l = s // d
    nb = l // BAND
    qv, kv, vv, dov, dlv = (t.reshape(l, d * GROUP_DIM) for t in (q, k, v, do, dlse))
    width = min(d, ATTN_CLASSES_PER_STEP) * GROUP_DIM
    nh = width // HEAD

    def body(q_ref, kp_ref, kc_ref, vp_ref, vc_ref, do_ref, dl_ref, dq_ref, dk_ref, dv_ref, ck, cv):
        step = pl.program_id(1)
        has_prev = step < nb - 1

        @pl.when(step == 0)
        def _():
            ck[...] = jnp.zeros_like(ck)
            cv[...] = jnp.zeros_like(cv)

        _, vjp = jax.vjp(functools.partial(_attn_block, has_prev=has_prev),
                         *[_heads(t_ref[...].astype(F32), nh) for t_ref in (q_ref, kp_ref, kc_ref, vp_ref, vc_ref)])
        dq, dkp, dkc, dvp, dvc = vjp((_heads(do_ref[...], nh), _heads(dl_ref[...], nh)))
        dq_ref[...] = _unheads(dq)
        dk_ref[...] = _unheads(dkc) + ck[...]
        dv_ref[...] = _unheads(dvc) + cv[...]
        ck[...] = _unheads(dkp)
        cv[...] = _unheads(dvp)

    cur = pl.BlockSpec((BAND, width), lambda rho, i: (nb - 1 - i, rho))
    prev = pl.BlockSpec((BAND, width), lambda rho, i: (jnp.maximum(nb - 2 - i, 0), rho))
    dq, dk, dv = pl.pallas_call(
        body, name=f"attn_bwd_d{d}", grid=(d * GROUP_DIM // width, nb),
        in_specs=[cur, prev, cur, prev, cur, cur, cur], out_specs=[cur] * 3,
        out_shape=[SDS((l, d * GROUP_DIM), F32)] * 3,
        scratch_shapes=[pltpu.VMEM((BAND, width), F32), pltpu.VMEM((BAND, width), F32)],
        compiler_params=pltpu.CompilerParams(dimension_semantics=("parallel", "arbitrary"), vmem_limit_bytes=VMEM_LIMIT),
    )(qv, kv, kv, vv, vv, dov, dlv)
    return dq.reshape(s, GROUP_DIM), dk.reshape(s, GROUP_DIM), dv.reshape(s, GROUP_DIM)


def _coords():
    return lax.axis_index("x"), lax.axis_index("y"), lax.axis_index("c")


_CHIP_FLIPS = ((1, 0), (0, 1), (1, 1))


def _flip(v, f):
    return 1 - v if f else v


def _form(kind, r, c):
    return (N_CHIPS, r, c) if kind == "blk" else (r, N_CHIPS * c)


def _slot(ref, kind, j, rows, c):
    if kind == "blk":
        return ref.at[j] if rows is None else ref.at[j, rows]
    cols = pl.ds(pl.multiple_of(j * c, 128), c)
    return ref.at[:, cols] if rows is None else ref.at[rows, cols]


def _half(r, which, align):
    return pl.ds(pl.multiple_of(which * (r // 2), align), r // 2)


def _rcopy(src, dst, send_sems, recv_sems, kk, dev):
    return pltpu.make_async_remote_copy(src_ref=src, dst_ref=dst, send_sem=send_sems.at[kk], recv_sem=recv_sems.at[kk],
                                        device_id=dev, device_id_type=MESH)


def _gather_plan(specs, step):
    def copies(refs, ss, rs, received):
        x, y, c = _coords()
        out = []
        for w, (kind, r, cc) in enumerate(specs):
            mine, other = _half(r, c, 16), _half(r, 1 - c, 16)
            for kk, (fx, fy) in enumerate(_CHIP_FLIPS):
                px, py = _flip(x, fx), _flip(y, fy)
                if step == "ici":
                    sl = _slot(refs[w], kind, 2 * px + py if received else 2 * x + y, mine, cc)
                    dev = (px, py, c)
                else:
                    sl = _slot(refs[w], kind, 2 * px + py, other if received else mine, cc)
                    dev = (x, y, 1 - c)
                out.append(_rcopy(sl, sl, ss, rs, 3 * w + kk, dev))
        return out

    def issue(refs, ss, rs):
        return copies(refs, ss, rs, False)

    def expect(refs, ss, rs):
        return copies(refs, ss, rs, False), copies(refs, ss, rs, True)

    return issue, expect


_HBM = pl.BlockSpec(memory_space=pltpu.HBM)
_SEM = pl.BlockSpec(memory_space=pltpu.SEMAPHORE)
_EFFECT = pltpu.SideEffectType.DATAFLOW_SIDE_EFFECTING


def _copies_start(name, bufs, n_sems, issue, after=None):
    nb = len(bufs)
    extra = [] if after is None else [after]

    def body(*refs):
        send_sems, recv_sems = refs[nb + len(extra)], refs[nb + len(extra) + 1]
        for cp in issue(refs[:nb], send_sems, recv_sems):
            cp.start()
        refs[-1][...] = jnp.zeros_like(refs[-1])

    outs = pl.pallas_call(
        body, name=name,
        out_shape=(pltpu.SemaphoreType.DMA((n_sems,)), pltpu.SemaphoreType.DMA((n_sems,)),
                   *[pltpu.HBM(b.shape, b.dtype) for b in bufs], SDS((8, 128), F32)),
        in_specs=[_HBM] * nb + [pl.BlockSpec(memory_space=pl.ANY)] * len(extra),
        out_specs=(_SEM, _SEM, *[_HBM] * nb, pl.BlockSpec(memory_space=pltpu.VMEM)),
        input_output_aliases={i: 2 + i for i in range(nb)},
        compiler_params=pltpu.CompilerParams(has_side_effects=_EFFECT),
    )(*[pltpu.with_memory_space_constraint(b, pltpu.HBM) for b in bufs], *extra)
    return outs[0], outs[1], list(outs[2:2 + nb]), outs[-1]


def _copies_wait(name, bufs, send_sems, recv_sems, after, expect):
    nb = len(bufs)

    def body(*refs):
        sent, received = expect(refs[:nb], refs[nb], refs[nb + 1])
        for cp in sent:
            cp.wait_send()
        for cp in received:
            cp.wait_recv()

    outs = pl.pallas_call(
        body, name=name,
        out_shape=tuple(pltpu.HBM(b.shape, b.dtype) for b in bufs),
        in_specs=(*[_HBM] * nb, _SEM, _SEM, pl.BlockSpec(memory_space=pl.ANY)), out_specs=tuple([_HBM] * nb),
        input_output_aliases={i: i for i in range(nb)},
        compiler_params=pltpu.CompilerParams(has_side_effects=_EFFECT),
    )(*bufs, send_sems, recv_sems, after)
    return list(outs)


def _add_pair(g, recv, kind, r, c, c_arr, name):
    h = r // 2
    if kind == "blk":
        tr = _row_tile(h, 512)
        grid = (N_CHIPS, h // tr)
        g_spec = pl.BlockSpec((1, 1, tr, c), lambda j, i, c_ref: (j, c_ref[0], i, 0))
        o_spec = pl.BlockSpec((1, tr, c), lambda j, i, c_ref: (j, i, 0))
        gv, oshape = g.reshape(N_CHIPS, 2, h, c), (N_CHIPS, h, c)
    else:
        tr = _row_tile(h, 64)
        grid = (h // tr,)
        g_spec = pl.BlockSpec((1, tr, N_CHIPS * c), lambda i, c_ref: (c_ref[0], i, 0))
        o_spec = pl.BlockSpec((tr, N_CHIPS * c), lambda i, c_ref: (i, 0))
        gv, oshape = g.reshape(2, h, N_CHIPS * c), (h, N_CHIPS * c)

    def body(c_ref, g_ref, r_ref, o_ref, ob_ref):
        v = (g_ref[:, 0] if kind == "blk" else g_ref[0]) + r_ref[...]
        o_ref[...] = v
        ob_ref[...] = v.astype(BF16)

    return pl.pallas_call(
        body, name=name,
        grid_spec=pltpu.PrefetchScalarGridSpec(num_scalar_prefetch=1, grid=grid, in_specs=[g_spec, o_spec], out_specs=[o_spec] * 2),
        out_shape=[SDS(oshape, F32), SDS(oshape, BF16)],
        compiler_params=pltpu.CompilerParams(vmem_limit_bytes=VMEM_LIMIT),
    )(c_arr, gv, recv)


def _sum_chips(pair, recv, kind, r, c, mc_arr, name):
    h = r // 2
    tr = _row_tile(h, 512)
    nt = h // tr
    if kind == "blk":
        p_spec = pl.BlockSpec((1, tr, c), lambda i, mc: (mc[0], i, 0))
    else:
        p_spec = pl.BlockSpec((tr, c), lambda i, mc: (i, mc[0]))

    def body(mc, a_ref, r_ref, g_out):
        own = a_ref[0] if kind == "blk" else a_ref[...]
        g_out[...] = ((own + r_ref[0].astype(F32)) + r_ref[1].astype(F32)) + r_ref[2].astype(F32)

    return pl.pallas_call(
        body, name=name,
        grid_spec=pltpu.PrefetchScalarGridSpec(
            num_scalar_prefetch=1, grid=(nt,), in_specs=[p_spec, pl.BlockSpec((3, tr, c), lambda i, mc: (0, i, 0))],
            out_specs=pl.BlockSpec((tr, c), lambda i, mc: (mc[1] * nt + i, 0))),
        out_shape=SDS((r, c), F32),
        compiler_params=pltpu.CompilerParams(vmem_limit_bytes=VMEM_LIMIT),
    )(mc_arr, pair, recv)


class _GroupReduce:
    def __init__(self, tag, specs, c_arr, mc_arr):
        self.tag, self.specs, self.c_arr, self.mc_arr = tag, specs, c_arr, mc_arr
        self.n = len(specs)

    def _plan(self, step):
        specs, n = self.specs, self.n

        def copies(refs, ss, rs, received):
            x, y, c = _coords()
            sib, out = (x, y, 1 - c), []
            for w, (_, kind, r, cc) in enumerate(specs):
                if step == "join":
                    there = refs[w].at[_half(r, 1 - c if received else c, 8)]
                    out.append(_rcopy(there, there, ss, rs, w, sib))
                    continue
                src, land = refs[w], refs[n + w]
                if step == "swap":
                    rows = _half(r, 1 - c, 8)
                    part = src.at[:, rows] if kind == "blk" else src.at[rows]
                    out.append(_rcopy(land if received else part, land, ss, rs, w, sib))
                else:
                    for kk, (fx, fy) in enumerate(_CHIP_FLIPS):
                        px, py = _flip(x, fx), _flip(y, fy)
                        part = land.at[kk] if received else _slot(src, kind, 2 * px + py, None, cc)
                        out.append(_rcopy(part, land.at[kk], ss, rs, 3 * w + kk, (px, py, c)))
            return out

        def issue(refs, ss, rs):
            return copies(refs, ss, rs, False)

        def expect(refs, ss, rs):
            return copies(refs, ss, rs, False), copies(refs, ss, rs, True)

        return issue, expect

    def swap_start(self, grads, after=None):
        lands = [lax.empty(_form(kind, r // 2, c), F32) for _, kind, r, c in self.specs]
        ss, rs, bufs, tok = _copies_start(f"rs_{self.tag}_swap", list(grads) + lands, self.n, self._plan("swap")[0], after=after)
        self.state = (ss, rs, bufs)
        return tok

    def swap_wait_ici_start(self, after):
        ss, rs, bufs = self.state
        bufs = _copies_wait(f"rs_{self.tag}_swap_wait", bufs, ss, rs, after, self._plan("swap")[1])
        pairs = [_add_pair(bufs[w], bufs[self.n + w], kind, r, c, self.c_arr, name=f"rs_{self.tag}_pair_{nm}")
                 for w, (nm, kind, r, c) in enumerate(self.specs)]
        self.pair = [pr[0] for pr in pairs]
        lands = [lax.empty((3, r // 2, c), BF16) for _, _, r, c in self.specs]
        ss, rs, bufs, tok = _copies_start(f"rs_{self.tag}_ici", [pr[1] for pr in pairs] + lands, 3 * self.n, self._plan("ici")[0])
        self.state = (ss, rs, bufs)
        return tok

    def ici_wait_join_start(self, after):
        ss, rs, bufs = self.state
        bufs = _copies_wait(f"rs_{self.tag}_ici_wait", bufs, ss, rs, after, self._plan("ici")[1])
        outs = [_sum_chips(self.pair[w], bufs[self.n + w], kind, r, c, self.mc_arr, name=f"rs_{self.tag}_sum_{nm}")
                for w, (nm, kind, r, c) in enumerate(self.specs)]
        ss, rs, bufs, tok = _copies_start(f"rs_{self.tag}_join", outs, self.n, self._plan("join")[0])
        self.state = (ss, rs, bufs)
        return tok

    def join_wait(self, after):
        ss, rs, bufs = self.state
        bufs = _copies_wait(f"rs_{self.tag}_join_wait", bufs, ss, rs, after, self._plan("join")[1])
        return {nm: bufs[w] for w, (nm, _, _, _) in enumerate(self.specs)}


def _all_reduce_small(buf):
    rows, cols = buf.shape

    def body(x_ref, o_ref, gath, send_sems, recv_sems):
        x, y, c = _coords()
        me = 4 * x + 2 * y + c
        gath[me] = x_ref[...]
        sends = []
        for kk in range(1, 8):
            f = (kk >> 2) & 1, (kk >> 1) & 1, kk & 1
            px, py, pc = _flip(x, f[0]), _flip(y, f[1]), _flip(c, f[2])
            cp = pltpu.make_async_remote_copy(src_ref=x_ref, dst_ref=gath.at[me], send_sem=send_sems.at[kk - 1],
                                              recv_sem=recv_sems.at[kk - 1], device_id=(px, py, pc), device_id_type=MESH)
            cp.start()
            sends.append(cp)
        for kk in range(1, 8):
            f = (kk >> 2) & 1, (kk >> 1) & 1, kk & 1
            px, py, pc = _flip(x, f[0]), _flip(y, f[1]), _flip(c, f[2])
            there = gath.at[4 * px + 2 * py + pc]
            pltpu.make_async_remote_copy(src_ref=there, dst_ref=there, send_sem=send_sems.at[kk - 1],
                                         recv_sem=recv_sems.at[kk - 1], device_id=(px, py, pc), device_id_type=MESH).wait_recv()
        for cp in sends:
            cp.wait_send()
        acc = gath[0]
        for j in range(1, 8):
            acc = acc + gath[j]
        o_ref[...] = acc

    return pl.pallas_call(
        body, name="all_reduce_small",
        in_specs=[pl.BlockSpec(memory_space=pltpu.VMEM)], out_specs=pl.BlockSpec(memory_space=pltpu.VMEM),
        out_shape=SDS((rows, cols), F32),
        scratch_shapes=[pltpu.VMEM((8, rows, cols), F32), pltpu.SemaphoreType.DMA((7,)), pltpu.SemaphoreType.DMA((7,))],
    )(buf)


def _adamw_rows(w, g, m, v):
    m = ADAM_B1 * m + (1.0 - ADAM_B1) * g
    v = ADAM_B2 * v + (1.0 - ADAM_B2) * jnp.square(g)
    m_hat = m / (1.0 - ADAM_B1 ** ADAM_STEP)
    v_hat = v / (1.0 - ADAM_B2 ** ADAM_STEP)
    return -ADAM_LR * (m_hat / (jnp.sqrt(v_hat) + ADAM_EPS) + ADAM_WD * w), m, v


def _adamw(w, g, m, v, name, dep=None, with_grad=False):
    rows, cols = w.shape
    tm = _pick(rows, (256, 128, 64, 16, 8))
    f = (lambda wt, gt, mt, vt: (gt,) + _adamw_rows(wt, gt, mt, vt)) if with_grad else _adamw_rows
    return _rows_call(f, [(t, 0, cols) for t in (w, g, m, v)], [], [(cols, F32)] * (3 + with_grad), tm=tm, name=name, dep=dep)


def _pack_small(parts):
    flat = jnp.concatenate([parts[n].reshape(-1) for n, _ in SMALL])
    return jnp.pad(flat, (0, SMALL_ROWS * PACK_COLS - flat.shape[0])).reshape(SMALL_ROWS, PACK_COLS)


def _unpack_small(buf, shapes):
    flat, out, off = buf.reshape(-1), {}, 0
    for n, sz in SMALL:
        out[n] = flat[off:off + sz].reshape(shapes[n])
        off += sz
    return out


def _lora_stack(parts):
    return jnp.concatenate([parts[n] for n, _ in LORA], axis=-2)


def _lora_split(stacked):
    out, off = {}, 0
    for n, rows in LORA:
        out[n] = stacked[..., off:off + rows, :]
        off += rows
    return out


def _ffn_gate_up(h, wgt, wut, name, dep=None):
    s, d = h.shape
    nblk, f, _ = wgt.shape
    tm = _pick(s, (1024, 512, 256))
    dn = (((1,), (1,)), ((), ()))

    def body(h_ref, wg_ref, wu_ref, *rest):
        g_ref, u_ref, a_ref = rest[-3:]
        hh = h_ref[...]
        g = lax.dot_general(hh, wg_ref[0], dn, preferred_element_type=F32)
        u = lax.dot_general(hh, wu_ref[0], dn, preferred_element_type=F32)
        g_ref[0], u_ref[0] = g.astype(BF16), u.astype(BF16)
        a_ref[0] = _swiglu_act(g, u).astype(BF16)

    w_spec = pl.BlockSpec((1, f, d), lambda j, i: (j, 0, 0))
    o_spec = pl.BlockSpec((1, tm, f), lambda j, i: (j, i, 0))
    extra = [] if dep is None else [dep]
    return pl.pallas_call(
        body, name=name, grid=(nblk, s // tm),
        in_specs=[pl.BlockSpec((tm, d), lambda j, i: (i, 0)), w_spec, w_spec] + [pl.BlockSpec(memory_space=pl.ANY)] * len(extra),
        out_specs=[o_spec] * 3,
        out_shape=[SDS((nblk, s, f), BF16)] * 3,
        compiler_params=pltpu.CompilerParams(dimension_semantics=("parallel", "parallel"), vmem_limit_bytes=VMEM_LIMIT),
    )(h, wgt, wut, *extra)


def _ffn_down_dx(dx_bf, wd, gate, up, name, dep=None):
    s, d = dx_bf.shape
    nblk, f, _ = wd.shape
    tm = _pick(s, (1024, 512, 256))
    dn = (((1,), (1,)), ((), ()))

    def body(dx_ref, wd_ref, g_ref, u_ref, *rest):
        dg_ref, du_ref = rest[-2:]
        dact = 0.5 * lax.dot_general(dx_ref[...], wd_ref[0], dn, preferred_element_type=F32)
        _, vjp = jax.vjp(_swiglu_act, g_ref[0].astype(F32), u_ref[0].astype(F32))
        dg, du = vjp(dact)
        dg_ref[0], du_ref[0] = dg.astype(BF16), du.astype(BF16)

    o_spec = pl.BlockSpec((1, tm, f), lambda j, i: (j, i, 0))
    extra = [] if dep is None else [dep]
    return pl.pallas_call(
        body, name=name, grid=(nblk, s // tm),
        in_specs=[pl.BlockSpec((tm, d), lambda j, i: (i, 0)), pl.BlockSpec((1, f, d), lambda j, i: (j, 0, 0)), o_spec, o_spec]
        + [pl.BlockSpec(memory_space=pl.ANY)] * len(extra),
        out_specs=[o_spec] * 2, out_shape=[SDS((nblk, s, f), BF16)] * 2,
        compiler_params=pltpu.CompilerParams(dimension_semantics=("parallel", "parallel"), vmem_limit_bytes=VMEM_LIMIT),
    )(dx_bf, wd, gate, up, *extra)


def _ffn_fwd(x, gain, wgt, wut, wd, tag, h=None, dep=None):
    if h is None:
        h = _rows_call(_rms, [(x, 0, D_MODEL)], [gain], [(D_MODEL, BF16)], tm=512, name=f"{tag}_norm")[0]
    gate, up, act = _ffn_gate_up(h, wgt, wut, f"{tag}_gate_up", dep=dep)
    x_new = _mm(act, wd, sum_blocks=True, res=x, alpha=0.5, name=f"{tag}_down")
    return x_new, (x, h, gate, up, act)


def _ffn_bwd(dx_new, dx_new_bf, saved, gain, wgt, wut, wd, tag, dep=None, hooks=None):
    x, h, gate, up, act = saved
    hooks = hooks or {}

    def hook(name, *vals):
        return hooks[name](*vals) if name in hooks else None

    d_wd = _mm(act, dx_new_bf, ta=True, alpha=0.5, name=f"{tag}_down_dw")
    dep = hook("down", d_wd) if "down" in hooks else dep
    dgate, dup = _ffn_down_dx(dx_new_bf, wd, gate, up, f"{tag}_down_dx", dep=dep)
    d_wgt = _mm(dgate, h, ta=True, dep=hook("mid", dgate), name=f"{tag}_gate_dw")
    d_wut = _mm(dup, h, ta=True, name=f"{tag}_up_dw")
    dh = _mm(dgate, wgt, sum_blocks=True, dep=hook("dw", d_wgt, d_wut), name=f"{tag}_gate_dx")
    dx, dx_bf, dgain = _mm(dup, wut, sum_blocks=True, res=dh, dep=hook("dx", dh), post=_norm_bwd_post(x, gain, dx_new),
                           name=f"{tag}_up_dx")
    hook("end", dx_bf)
    return dx, dx_bf, dgain, d_wgt, d_wut, d_wd


def _norm_bwd_post(x, gain, dres):
    def f(dht, xt, drt, gt):
        _, vjp = jax.vjp(_rms, xt, gt)
        dxt, dgt = vjp(dht)
        return dxt + drt, dxt + drt, dgt

    return f, [x, dres], [gain], [F32, BF16], [(1, D_MODEL)]


def kernel(x, p, positions, ffn1_norm, ffn1_w_gate, ffn1_w_up, ffn1_w_down, mix_norm, w_in, rwkv_mu, rwkv_w0, rwkv_w2, rwkv_a0, rwkv_a2, rwkv_g2, rwkv_k_k, rwkv_k_a, rwkv_r_k, rwkv_gn_w, rwkv_gn_b, q_norm, k_norm, w_br_rwkv, w_br_attn, w_out, ffn2_norm, ffn2_w_gate, ffn2_w_up, ffn2_w_down, ple_norm, ple_w_gate, ple_w_proj, loss_target, m_ffn1_norm, m_ffn1_w_gate, m_ffn1_w_up, m_ffn1_w_down, m_mix_norm, m_w_in, m_rwkv_mu, m_rwkv_w0, m_rwkv_w2, m_rwkv_a0, m_rwkv_a2, m_rwkv_g2, m_rwkv_k_k, m_rwkv_k_a, m_rwkv_r_k, m_rwkv_gn_w, m_rwkv_gn_b, m_q_norm, m_k_norm, m_w_br_rwkv, m_w_br_attn, m_w_out, m_ffn2_norm, m_ffn2_w_gate, m_ffn2_w_up, m_ffn2_w_down, m_ple_norm, m_ple_w_gate, m_ple_w_proj, v_ffn1_norm, v_ffn1_w_gate, v_ffn1_w_up, v_ffn1_w_down, v_mix_norm, v_w_in, v_rwkv_mu, v_rwkv_w0, v_rwkv_w2, v_rwkv_a0, v_rwkv_a2, v_rwkv_g2, v_rwkv_k_k, v_rwkv_k_a, v_rwkv_r_k, v_rwkv_gn_w, v_rwkv_gn_b, v_q_norm, v_k_norm, v_w_br_rwkv, v_w_br_attn, v_w_out, v_ffn2_norm, v_ffn2_w_gate, v_ffn2_w_up, v_ffn2_w_down, v_ple_norm, v_ple_w_gate, v_ple_w_proj):
    args = dict(locals())
    wts = {n: args[n] for n in WEIGHTS}
    mom_m = {n: args["m_" + n] for n in WEIGHTS}
    mom_v = {n: args["v_" + n] for n in WEIGHTS}
    x0, tgt = x[0], loss_target[0]
    s = x0.shape[0]
    p_tok = p[0, 0]

    vec = {n: wts[n].reshape(1, -1) for n, _ in SMALL}
    xi, yi, ci = _coords()
    me = 2 * xi + yi
    def laid(t, n):
        return jnp.transpose(t[n][0]) if n in TRANSPOSED else t[n][0]

    shard_of = {n: laid(wts, n) for g in GROUPS.values() for n, _, _, _ in g if n != "lora"}
    shard_of["lora"] = _lora_stack({n: wts[n][0] for n, _ in LORA})

    def whole_with_own(n, kind, r, c, tok=None):
        at = (me, 0, 0) if kind == "blk" else (0, me * c)
        own = (shard_of[n] if tok is None else shard_of[n] + tok[0, 0]).astype(BF16)
        return lax.dynamic_update_slice(lax.empty(_form(kind, r, c), BF16), own[None] if kind == "blk" else own, at)

    specs = {g: [(kind, r, c) for _, kind, r, c in grp] for g, grp in GROUPS.items()}
    plans = {(g, st): _gather_plan(specs[g], st) for g in GROUPS for st in ("ici", "d2d")}
    buf_f1 = [whole_with_own(*w) for w in GROUPS["f1"]]
    ss_0, rs_0, buf_f1, tok_0 = _copies_start("gather_f1_ici", buf_f1, 3 * len(buf_f1), plans["f1", "ici"][0])
    bufs = {g: [whole_with_own(*w, tok=tok_0) for w in GROUPS[g]] for g in ("mx", "f2")}
    buf_f1 = _copies_wait("gather_f1_ici_wait", buf_f1, ss_0, rs_0, bufs["mx"][0], plans["f1", "ici"][1])
    ss_1, rs_1, buf_f1, tok_1 = _copies_start("gather_f1_d2d", buf_f1, 3 * len(buf_f1), plans["f1", "d2d"][0])
    h1 = _rows_call(_rms, [(x0, 0, D_MODEL)], [vec["ffn1_norm"] + tok_1[0, 0]], [(D_MODEL, BF16)], tm=512, name="ffn1_norm")[0]
    buf_f1 = _copies_wait("gather_f1_d2d_wait", buf_f1, ss_1, rs_1, h1, plans["f1", "d2d"][1])
    wb = dict(zip([w[0] for w in GROUPS["f1"]], buf_f1))
    ss_a, rs_a, buf_mx, tok_a = _copies_start("gather_mx_ici", bufs["mx"], 3 * len(bufs["mx"]), plans["mx", "ici"][0],
                                              after=wb["ffn1_w_gate"])

    inv_freq = 1.0 / (ROPE_THETA ** (jnp.arange(0, HEAD, 2, dtype=F32) / HEAD))
    ang = positions[0].astype(F32)[:, None] * inv_freq
    cos, sin = jnp.cos(ang), jnp.sin(ang)
    cos2, sin2 = jnp.concatenate([cos, cos], axis=1), jnp.concatenate([-sin, sin], axis=1)

    x1, ffn1_saved = _ffn_fwd(x0, vec["ffn1_norm"], wb["ffn1_w_gate"], wb["ffn1_w_up"], wb["ffn1_w_down"], "ffn1", h=h1, dep=tok_a)
    buf_mx = _copies_wait("gather_mx_ici_wait", buf_mx, ss_a, rs_a, x1, plans["mx", "ici"][1])
    ss_b, rs_b, buf_mx, tok_b = _copies_start("gather_mx_d2d", buf_mx, 3 * len(buf_mx), plans["mx", "d2d"][0])
    ss_c, rs_c, buf_f2, tok_c = _copies_start("gather_f2_ici", bufs["f2"], 3 * len(bufs["f2"]), plans["f2", "ici"][0])
    h = _rows_call(_rms, [(x1, 0, D_MODEL)], [vec["mix_norm"] + (tok_b[0, 0] + tok_c[0, 0])], [(D_MODEL, BF16)], tm=256,
                   name="mix_norm")[0]
    buf_mx = _copies_wait("gather_mx_d2d_wait", buf_mx, ss_b, rs_b, h, plans["mx", "d2d"][1])
    wb.update(zip([w[0] for w in GROUPS["mx"]], buf_mx))
    w_in_all = wb["w_in"]
    w_in_r, w_in_a, w_in_g = w_in_all[:, :RWKV_COLS], w_in_all[:, RWKV_COLS:RWKV_COLS + ATTN_COLS], w_in_all[:, RWKV_COLS + ATTN_COLS:]
    lora = _lora_split(wb["lora"])
    w2, a2, g2 = lora["rwkv_w2"], lora["rwkv_a2"], lora["rwkv_g2"]
    z_r = _mm(h, w_in_r, name="in_rwkv")
    z_a = _mm(h, w_in_a, name="in_attn")
    z_g = _mm(h, w_in_g, name="in_gate")

    zs = _shift_fwd(z_r, vec["rwkv_mu"])
    pre_params = [vec["rwkv_w0"], w2, vec["rwkv_a0"], a2, g2, vec["rwkv_k_k"], vec["rwkv_k_a"]]
    def pre_fwd(*t):
        res = _rwkv_pre(*t)
        return res[1], res[2], res[4], res[5], res[6]

    lw, k2, na, kb, gate_r = _rows_call(pre_fwd, [(zs, 0, RWKV_COLS)], pre_params, [(RWKV_DIM, F32)] * 5, tm=512, name="rwkv_pre")
    y_scan, s0s, invs = _wkv_fwd(zs, lw, k2, na, kb)
    buf_f2 = _copies_wait("gather_f2_ici_wait", buf_f2, ss_c, rs_c, y_scan, plans["f2", "ici"][1])
    ss_d, rs_d, buf_f2, tok_d = _copies_start("gather_f2_d2d", buf_f2, 3 * len(buf_f2), plans["f2", "d2d"][0])
    post_params = [vec["rwkv_gn_w"] + tok_d[0, 0], vec["rwkv_gn_b"], vec["rwkv_r_k"]]
    post_rows = [(y_scan, 0, RWKV_DIM), (zs, 0, RWKV_DIM), (k2, 0, RWKV_DIM), (zs, 2, RWKV_DIM), (gate_r, 0, RWKV_DIM)]
    y_rwkv = _rows_call(_rwkv_post, post_rows, post_params, [(RWKV_DIM, BF16)], tm=512, name="rwkv_post")[0]
    buf_f2 = _copies_wait("gather_f2_d2d_wait", buf_f2, ss_d, rs_d, y_rwkv, plans["f2", "d2d"][1])
    wb.update(zip([w[0] for w in GROUPS["f2"]], buf_f2))
    w_brr, w_bra = wb["w_br_rwkv"], wb["w_br_attn"]
    w_o = wb["w_out"].reshape(D_MODEL, D_MODEL)
    w_pp, w_pg = wb["ple_w_proj"], wb["ple_w_gate"].reshape(D_MODEL, D_MODEL)

    def qk_fwd(qt, kt, ct, st, qg, kg):
        return _norm_rope(qt, qg, ct, st), _norm_rope(kt, kg, ct, st)

    qk_rows = [(z_a, 0, ATTN_DIM), (z_a, 1, ATTN_DIM), (cos2, 0, HEAD), (sin2, 0, HEAD)]
    q_rot, k_rot = _rows_call(qk_fwd, qk_rows, [vec["q_norm"], vec["k_norm"]], [(ATTN_DIM, BF16)] * 2, tm=512, name="attn_pre")
    def group(t, g, off=0):
        return t[:, off + g * GROUP_DIM:off + (g + 1) * GROUP_DIM].astype(BF16)

    qkv = [(group(q_rot, g), group(k_rot, g), group(z_a, g, 2 * ATTN_DIM)) for g in range(len(ATTN_DILATIONS))]
    outs, lses = zip(*[_attn_fwd(*qkv[g], d) for g, d in enumerate(ATTN_DILATIONS)])
    comb_rows = [(t, 0, GROUP_DIM) for t in outs + lses]
    y_attn = _rows_call(_attn_combine, comb_rows, [], [(GROUP_DIM, BF16)], tm=512, name="attn_combine")[0]

    br = _mm(y_rwkv, w_brr, name="branch_rwkv")
    ba = _mm(y_attn, w_bra, name="branch_attn")
    merge_rows = [(z_g, 0, D_MODEL), (z_g, 1, D_MODEL), (br, 0, D_MODEL), (ba, 0, D_MODEL)]
    merged = _rows_call(_merge, merge_rows, [], [(D_MODEL, BF16)], tm=512, name="merge")[0]
    x2 = _mm(merged, w_o, res=x1, name="out_proj")
    x3, ffn2_saved = _ffn_fwd(x2, vec["ffn2_norm"], wb["ffn2_w_gate"], wb["ffn2_w_up"], wb["ffn2_w_down"], "ffn2")
    hp = _rows_call(_rms, [(x3, 0, D_MODEL)], [vec["ple_norm"]], [(D_MODEL, BF16)], tm=512, name="ple_norm")[0]
    pg = _mm(hp, w_pg, name="ple_gate")
    pp = _mm(p_tok, w_pp, name="ple_proj")

    def head(x3t, pgt, ppt, tt):
        sg = _sigmoid(pgt)
        err = x3t + sg * ppt - tt
        dx4 = err * (1.0 / D_MODEL)
        loss = 0.5 * jnp.sum(jnp.mean(err * err, axis=-1, keepdims=True), axis=0, keepdims=True)
        return dx4, dx4 * ppt * sg * (1.0 - sg), dx4 * sg, jnp.broadcast_to(loss, (8, 128))

    head_rows = [(x3, 0, D_MODEL), (pg, 0, D_MODEL), (pp, 0, D_MODEL), (tgt, 0, D_MODEL)]
    dx4, dpg, dpp, loss_tile = _rows_call(head, head_rows, [], [(D_MODEL, F32), (D_MODEL, BF16), (D_MODEL, BF16)], [(8, 128)],
                                          tm=512, name="ple_loss")

    c_arr = jnp.reshape(ci, (1,)).astype(jnp.int32)
    mc_arr = jnp.stack([me, ci]).astype(jnp.int32)
    red = {g: _GroupReduce(g, grp, c_arr, mc_arr) for g, grp in REDUCE_GROUPS.items()}

    done = {}

    def update(summed):
        for n, g2d in summed.items():
            if n == "lora":
                w_, m_, v_ = (_lora_stack({k: t[k][0] for k, _ in LORA}) for t in (wts, mom_m, mom_v))
            else:
                w_, m_, v_ = laid(wts, n), laid(mom_m, n), laid(mom_v, n)
            done[n] = _adamw(w_, g2d, m_, v_, name=f"adamw_{n}", with_grad=True)
    gw, gs = {}, {}
    gw["ple_w_proj"] = _mm(p_tok, dpp, ta=True, name="ple_proj_dw")
    gw["ple_w_gate"] = _mm(hp, dpg, ta=True, name="ple_gate_dw")
    dx3, dx3_bf, gs["ple_norm"] = _mm(dpg, w_pg, tb=True, post=_norm_bwd_post(x3, vec["ple_norm"], dx4), name="ple_gate_dx")
    dx2, dx2_bf, gs["ffn2_norm"], gw["ffn2_w_gate"], gw["ffn2_w_up"], gw["ffn2_w_down"] = _ffn_bwd(
        dx3, dx3_bf, ffn2_saved, vec["ffn2_norm"], wb["ffn2_w_gate"], wb["ffn2_w_up"], wb["ffn2_w_down"], "ffn2")
    gw["ple_w_gate"] = gw["ple_w_gate"].reshape(N_CHIPS, D_MODEL // N_CHIPS, D_MODEL)
    tok = red["f2"].swap_start([gw[w[0]] for w in REDUCE_GROUPS["f2"]])
    gw["w_out"] = _mm(merged, dx2_bf, ta=True, name="out_proj_dw")
    dmerged = _mm(dx2_bf, w_o, tb=True, dep=tok, name="out_proj_dx")

    def merge_bwd(zgr, zga, brt, bat, ct):
        _, vjp = jax.vjp(_merge, zgr, zga, brt, bat)
        d1, d2, d3, d4 = vjp(ct)
        return jnp.concatenate([d1, d2], axis=1), d3, d4

    dz_g, dbr, dba = _rows_call(merge_bwd, merge_rows + [(dmerged, 0, D_MODEL)], [],
                                [(2 * D_MODEL, BF16), (D_MODEL, BF16), (D_MODEL, BF16)], tm=512, name="merge_bwd")
    tok = red["f2"].swap_wait_ici_start(dz_g)
    gw["w_br_rwkv"] = _mm(y_rwkv, dbr, ta=True, name="branch_rwkv_dw")
    gw["w_br_attn"] = _mm(y_attn, dba, ta=True, name="branch_attn_dw")
    dy_rwkv = _mm(dbr, w_brr, tb=True, dep=tok, name="branch_rwkv_dx")
    dy_attn = _mm(dba, w_bra, tb=True, dep=tok, name="branch_attn_dx")

    def comb_bwd(*t):
        _, vjp = jax.vjp(_attn_combine, *t[:6])
        return vjp(t[6])

    dcomb = _rows_call(comb_bwd, comb_rows + [(dy_attn, 0, GROUP_DIM)], [], [(GROUP_DIM, F32)] * 6, tm=512, name="attn_combine_bwd")
    dqs, dks, dvs = zip(*[_attn_bwd(*qkv[g], d, dcomb[g], dcomb[3 + g]) for g, d in enumerate(ATTN_DILATIONS)])

    def qk_bwd(qt, kt, ct, st, *rest):
        dq = jnp.concatenate(rest[0:3], axis=1)
        dk = jnp.concatenate(rest[3:6], axis=1)
        qg, kg = rest[9], rest[10]
        _, vjp = jax.vjp(lambda a_, b_, c_, d_: qk_fwd(a_, b_, ct, st, c_, d_), qt, kt, qg, kg)
        dqt, dkt, dqg, dkg = vjp((dq, dk))
        return jnp.concatenate((dqt, dkt) + tuple(rest[6:9]), axis=1), dqg, dkg

    dz_a, gs["q_norm"], gs["k_norm"] = _rows_call(
        qk_bwd, qk_rows + [(t, 0, GROUP_DIM) for t in dqs + dks + dvs], [vec["q_norm"], vec["k_norm"]],
        [(ATTN_COLS, BF16)], [(1, HEAD), (1, HEAD)], tm=512, name="attn_pre_bwd")
    tok = red["f2"].ici_wait_join_start(dz_a)

    def post_bwd(*t):
        _, vjp = jax.vjp(_rwkv_post, *t[:5], *t[6:])
        return vjp(t[5])

    dy_scan, dr_post, dk2_post, dv_post, dgate_r, gs["rwkv_gn_w"], gs["rwkv_gn_b"], gs["rwkv_r_k"] = _rows_call(
        post_bwd, post_rows + [(dy_rwkv, 0, RWKV_DIM)], post_params, [(RWKV_DIM, F32)] * 5, [(1, RWKV_DIM)] * 3,
        tm=512, name="rwkv_post_bwd", dep=tok)
    update(red["f2"].join_wait(dy_scan))
    dr_s, dlw, dk2_s, dv_s, dna, dkb = _wkv_bwd(zs, lw, k2, na, kb, s0s, invs, dy_scan)

    def pre_bwd(zt, c_r1, c_r2, c_lw, c_k1, c_k2, c_v1, c_v2, c_a, c_b, c_g, *params):
        _, vjp = jax.vjp(_rwkv_pre, zt, *params)
        return vjp((c_r1 + c_r2, c_lw, c_k1 + c_k2, c_v1 + c_v2, c_a, c_b, c_g))

    pre_cts = [dr_s, dr_post, dlw, dk2_s, dk2_post, dv_s, dv_post, dna, dkb, dgate_r]
    dzs, gs["rwkv_w0"], g_w2, gs["rwkv_a0"], g_a2, g_g2, gs["rwkv_k_k"], gs["rwkv_k_a"] = _rows_call(
        pre_bwd, [(zs, 0, RWKV_COLS)] + [(t, 0, RWKV_DIM) for t in pre_cts], pre_params, [(RWKV_COLS, F32)],
        [q.shape for q in pre_params], tm=512, name="rwkv_pre_bwd")
    dz_r, gs["rwkv_mu"] = _shift_bwd(z_r, vec["rwkv_mu"], dzs)

    g_w_in = jnp.concatenate([_mm(h, dz_r, ta=True, name="in_rwkv_dw"), _mm(h, dz_a, ta=True, name="in_attn_dw"),
                              _mm(h, dz_g, ta=True, name="in_gate_dw")], axis=1)
    gw["w_in"], gw["lora"] = g_w_in, jnp.concatenate([g_w2, g_a2, g_g2], axis=0)
    gw["w_out"] = gw["w_out"].reshape(N_CHIPS, D_MODEL // N_CHIPS, D_MODEL)
    tok = red["mx"].swap_start([gw[w[0]] for w in REDUCE_GROUPS["mx"]])
    dh = _mm(dz_r, w_in_r, tb=True, dep=tok, name="in_rwkv_dx")
    dh = _mm(dz_a, w_in_a, tb=True, res=dh, name="in_attn_dx")
    dx1, dx1_bf, gs["mix_norm"] = _mm(dz_g, w_in_g, tb=True, res=dh, post=_norm_bwd_post(x1, vec["mix_norm"], dx2), name="in_gate_dx")
    tok_mx = red["mx"].swap_wait_ici_start(dx1_bf)
    hooks = {"down": lambda d_wd: red["f1d"].swap_start([d_wd], after=tok_mx),
             "mid": lambda dgate: red["f1d"].swap_wait_ici_start(dgate),
             "dw": lambda d_wgt, d_wut: red["f1g"].swap_start([d_wgt, d_wut]),
             "dx": lambda part: red["f1g"].swap_wait_ici_start(part) + red["f1d"].ici_wait_join_start(part),
             "end": lambda dx_: tokens.setdefault("mx_join", red["mx"].ici_wait_join_start(dx_))}
    tokens = {}
    dx0, _, gs["ffn1_norm"], gw["ffn1_w_gate"], gw["ffn1_w_up"], gw["ffn1_w_down"] = _ffn_bwd(
        dx1, dx1_bf, ffn1_saved, vec["ffn1_norm"], wb["ffn1_w_gate"], wb["ffn1_w_up"], wb["ffn1_w_down"], "ffn1", hooks=hooks)

    flat = jnp.concatenate([gs[n].reshape(-1) for n, _ in SMALL] + [loss_tile[0, 0:1]])
    small_buf = jnp.pad(flat, (0, SMALL_ROWS * PACK_COLS - flat.shape[0])).reshape(SMALL_ROWS, PACK_COLS)
    small_sum = _all_reduce_small(small_buf)
    n_small = sum(sz for _, sz in SMALL)
    loss = small_sum.reshape(-1)[n_small]
    grad_small = _unpack_small(small_sum, {n: wts[n].shape for n, _ in SMALL})
    d_s, m_s, v_s = _adamw(_pack_small(wts), small_sum, _pack_small(mom_m), _pack_small(mom_v), name="adamw_small",
                           dep=tokens["mx_join"])
    shapes = {n: wts[n].shape for n, _ in SMALL}
    d_s, m_s, v_s = _unpack_small(d_s, shapes), _unpack_small(m_s, shapes), _unpack_small(v_s, shapes)
    grads, deltas, new_m, new_v = {}, {}, {}, {}
    for n, _ in SMALL:
        grads[n], deltas[n], new_m[n], new_v[n] = grad_small[n], d_s[n], m_s[n], v_s[n]

    for g in ("mx", "f1d"):
        update(red[g].join_wait(m_s["ffn1_norm"]))
    tok = red["f1g"].ici_wait_join_start(done["w_in"][1])
    update(red["f1g"].join_wait(tok))
    for n, res in done.items():
        for store, val in zip((grads, deltas, new_m, new_v), res):
            if n == "lora":
                store.update({k: t[None] for k, t in _lora_split(val).items()})
            else:
                store[n] = (jnp.transpose(val) if n in TRANSPOSED else val)[None]

    return (loss, dx0[None], *[grads[n] for n in WEIGHTS], *[deltas[n] for n in WEIGHTS],
            *[new_m[n] for n in WEIGHTS], *[new_v[n] for n in WEIGHTS])
```

```python
import functools

import jax
import jax.numpy as jnp
from jax import lax
from jax.experimental import pallas as pl
from jax.experimental.pallas import tpu as pltpu

F32, BF16 = jnp.float32, jnp.bfloat16
HI = lax.Precision.HIGHEST
MESH = pl.DeviceIdType.MESH
SDS = jax.ShapeDtypeStruct

D_MODEL = 1024
HEAD = 64
RWKV_HEADS = 8
RWKV_DIM = RWKV_HEADS * HEAD
DECAY_LORA, ICLR_LORA, GATE_LORA = 64, 64, 128
GN_EPS = 64e-5
RMS_EPS = 1e-6
ATTN_DILATIONS = (1, 4, 16)
BAND = 128
ATTN_DIM = 768
GROUP_DIM = 256
ATTN_CLASSES_PER_STEP = 4
ROPE_THETA = 10000.0
NEG_INF = -1e30
RWKV_COLS = 3 * RWKV_DIM + DECAY_LORA + ICLR_LORA + GATE_LORA
ATTN_COLS = 3 * ATTN_DIM
ADAM_LR, ADAM_B1, ADAM_B2, ADAM_EPS, ADAM_WD, ADAM_STEP = 0.001, 0.9, 0.999, 1e-08, 0.01, 10

WKV_CHUNK = 64
WKV_HEADS_PER_STEP = 8
N_CHIPS = 4
PACK_COLS = 1024
VMEM_LIMIT = 48 * 1024 * 1024

TRANSPOSED = ("ffn1_w_gate", "ffn1_w_up", "ffn2_w_gate", "ffn2_w_up")
LORA = (("rwkv_w2", 64), ("rwkv_a2", 64), ("rwkv_g2", 128))
_FFN1 = (("ffn1_w_gate", "blk", 704, 1024), ("ffn1_w_up", "blk", 704, 1024), ("ffn1_w_down", "blk", 704, 1024))
_FFN2 = (("ffn2_w_gate", "blk", 704, 1024), ("ffn2_w_up", "blk", 704, 1024), ("ffn2_w_down", "blk", 704, 1024))
_IN = (("w_in", "col", 1024, 1536), ("lora", "col", 256, 128))
_BRANCH = (("w_br_rwkv", "col", 512, 256), ("w_br_attn", "col", 256, 256), ("w_out", "blk", 256, 1024))
_PLE = (("ple_w_gate", "blk", 256, 1024), ("ple_w_proj", "col", 256, 256))
GROUPS = {"f1": _FFN1, "mx": _IN, "f2": _BRANCH + _FFN2 + _PLE}
REDUCE_GROUPS = {"f2": _FFN2 + _PLE, "mx": _IN + _BRANCH, "f1d": _FFN1[2:], "f1g": _FFN1[:2]}
SMALL = (
    ("ffn1_norm", 1024), ("mix_norm", 1024), ("ffn2_norm", 1024), ("ple_norm", 1024), ("rwkv_mu", 1792),
    ("rwkv_w0", 512), ("rwkv_a0", 512), ("rwkv_k_k", 512), ("rwkv_k_a", 512), ("rwkv_r_k", 512),
    ("rwkv_gn_w", 512), ("rwkv_gn_b", 512), ("q_norm", 64), ("k_norm", 64),
)
SMALL_ROWS = 16
WEIGHTS = (
    "ffn1_norm", "ffn1_w_gate", "ffn1_w_up", "ffn1_w_down", "mix_norm", "w_in", "rwkv_mu", "rwkv_w0", "rwkv_w2",
    "rwkv_a0", "rwkv_a2", "rwkv_g2", "rwkv_k_k", "rwkv_k_a", "rwkv_r_k", "rwkv_gn_w", "rwkv_gn_b", "q_norm", "k_norm",
    "w_br_rwkv", "w_br_attn", "w_out", "ffn2_norm", "ffn2_w_gate", "ffn2_w_up", "ffn2_w_down", "ple_norm",
    "ple_w_gate", "ple_w_proj",
)


def _row_tile(n, most=704):
    for t in range(most - most % 16, 0, -16):
        if n % t == 0:
            return t
    return n


def _pick(n, cands):
    for c in cands:
        if n % c == 0:
            return c
    return n


def _mm(a, b, *, ta=False, tb=False, sum_blocks=False, out_dtype=F32, res=None, alpha=1.0, dep=None, post=None, name):
    flat = a.ndim == 2 and b.ndim == 2
    a3 = a if a.ndim == 3 else a[None]
    b3 = b if b.ndim == 3 else b[None]
    na, nbb = a3.shape[0], b3.shape[0]
    nblk = max(na, nbb)
    kdim, m = (a3.shape[1], a3.shape[2]) if ta else (a3.shape[2], a3.shape[1])
    n = b3.shape[1] if tb else b3.shape[2]
    assert (b3.shape[2] if tb else b3.shape[1]) == kdim
    tm = _pick(m, (1024, 512, 256, 128) if post is None else (512, 256, 128))
    tn = _pick(n, (1024, 896, 768, 512, 256, 128))
    tk = kdim if kdim <= 2304 else _pick(kdim, (1024, 512, 256, 128))
    nk = kdim // tk
    direct = nk == 1 and not sum_blocks

    if sum_blocks:
        grid = (m // tm, n // tn, nblk, nk)

        def ids(i, c, j, k):
            return i, c, j, k
    else:
        grid = (nblk, m // tm, n // tn, nk)

        def ids(j, i, c, k):
            return i, c, j, k

    def amap(*g):
        i, c, j, k = ids(*g)
        jj = j if na > 1 else 0
        return (jj, k, i) if ta else (jj, i, k)

    def bmap(*g):
        i, c, j, k = ids(*g)
        jj = j if nbb > 1 else 0
        return (jj, c, k) if tb else (jj, k, c)

    if sum_blocks:
        oshape, oblk = (m, n), (tm, tn)

        def omap(*g):
            i, c, j, k = ids(*g)
            return i, c
    else:
        oshape, oblk = (nblk, m, n), (1, tm, tn)

        def omap(*g):
            i, c, j, k = ids(*g)
            return j, i, c

    dn = (((0 if ta else 1,), (1 if tb else 0,)), ((), ()))
    has_res = res is not None
    p_f, p_rows, p_params, p_dtypes, p_accs = post if post is not None else (None, [], [], [], [])
    assert post is None or sum_blocks or flat
    n_in = 2 + has_res + len(p_rows) + len(p_params) + (dep is not None)

    def tile_map(*g):
        i, c, j, k = ids(*g)
        return i, c

    def body(*refs):
        refs = list(refs)
        acc = None if direct else refs.pop()
        o_refs = refs[n_in:]
        a_ref, b_ref = refs[0], refs[1]
        r_ref = refs[2] if has_res else None
        pr_refs = refs[2 + has_res:2 + has_res + len(p_rows)]
        pp_refs = refs[2 + has_res + len(p_rows):2 + has_res + len(p_rows) + len(p_params)]
        first_tile = jnp.logical_and(pl.program_id(0 if sum_blocks else 1) == 0, pl.program_id(1 if sum_blocks else 2) == 0)

        def finish(v):
            if alpha != 1.0:
                v = v * alpha
            if has_res:
                v = v + r_ref[...].reshape(v.shape).astype(F32)
            if post is None:
                o_refs[0][...] = v.reshape(o_refs[0].shape).astype(o_refs[0].dtype)
                return
            outs = p_f(v, *[t[...] for t in pr_refs], *[t[...] for t in pp_refs])
            for o_ref, val in zip(o_refs, outs[:len(p_dtypes)]):
                o_ref[...] = val.astype(o_ref.dtype)
            for o_ref, val in zip(o_refs[len(p_dtypes):], outs[len(p_dtypes):]):
                @pl.when(first_tile)
                def _():
                    o_ref[...] = jnp.zeros_like(o_ref)

                o_ref[...] += val.reshape(o_ref.shape)

        if direct:
            finish(lax.dot_general(a_ref[0].astype(BF16), b_ref[0].astype(BF16), dn, preferred_element_type=F32))
            return
        k = pl.program_id(3)
        if sum_blocks:
            j = pl.program_id(2)
            first = jnp.logical_and(j == 0, k == 0)
            last = jnp.logical_and(j == nblk - 1, k == nk - 1)
        else:
            first, last = k == 0, k == nk - 1

        @pl.when(first)
        def _():
            acc[...] = jnp.zeros_like(acc)

        acc[...] += lax.dot_general(a_ref[0].astype(BF16), b_ref[0].astype(BF16), dn, preferred_element_type=F32)

        @pl.when(last)
        def _():
            finish(acc[...])

    in_specs = [pl.BlockSpec((1, tk, tm) if ta else (1, tm, tk), amap), pl.BlockSpec((1, tn, tk) if tb else (1, tk, tn), bmap)]
    args = [a3, b3]
    if has_res:
        res3 = res if (sum_blocks or res.ndim == 3) else res[None]
        in_specs.append(pl.BlockSpec(oblk, omap))
        args.append(res3)
    in_specs += [pl.BlockSpec((tm, tn), tile_map) for _ in p_rows]
    in_specs += [pl.BlockSpec(t.shape, functools.partial(lambda *g, nd: (0,) * nd, nd=t.ndim)) for t in p_params]
    args += list(p_rows) + list(p_params)
    if dep is not None:
        in_specs.append(pl.BlockSpec(memory_space=pl.ANY))
        args.append(dep)
    if post is None:
        out_specs, out_shape = pl.BlockSpec(oblk, omap), SDS(oshape, out_dtype)
        semantics = ("parallel", "parallel", "arbitrary", "arbitrary") if sum_blocks else ("parallel", "parallel", "parallel", "arbitrary")
    else:
        out_specs = [pl.BlockSpec((tm, tn), tile_map) for _ in p_dtypes]
        out_specs += [pl.BlockSpec(tuple(sh), functools.partial(lambda *g, nd: (0,) * nd, nd=len(sh))) for sh in p_accs]
        out_shape = [SDS((m, n), dt) for dt in p_dtypes] + [SDS(tuple(sh), F32) for sh in p_accs]
        semantics = ("arbitrary",) * 4
    out = pl.pallas_call(
        body,
        name=name,
        grid=grid,
        in_specs=in_specs,
        out_specs=out_specs,
        out_shape=out_shape,
        scratch_shapes=[] if direct else [pltpu.VMEM((tm, tn), F32)],
        compiler_params=pltpu.CompilerParams(dimension_semantics=semantics, vmem_limit_bytes=VMEM_LIMIT),
    )(*args)
    if post is not None:
        return out
    if flat and not sum_blocks:
        out = out[0]
    return out


def _rows_call(f, rows, params, outs, accs=(), *, tm, name, dep=None):
    s = rows[0][0].shape[0]
    nr, npar, no = len(rows), len(params), len(outs)
    nin = nr + npar + (0 if dep is None else 1)
    in_specs = [pl.BlockSpec((tm, w), functools.partial(lambda i, cb: (i, cb), cb=cb)) for (_, cb, w) in rows]
    in_specs += [pl.BlockSpec(p.shape, functools.partial(lambda i, nd: (0,) * nd, nd=p.ndim)) for p in params]
    if dep is not None:
        in_specs.append(pl.BlockSpec(memory_space=pl.ANY))
    out_shape = [SDS((s, w), dt) for (w, dt) in outs] + [SDS(tuple(sh), F32) for sh in accs]
    out_specs = [pl.BlockSpec((tm, w), lambda i: (i, 0)) for (w, _) in outs]
    out_specs += [pl.BlockSpec(tuple(sh), functools.partial(lambda i, nd: (0,) * nd, nd=len(sh))) for sh in accs]

    def body(*refs):
        rin, pin = refs[:nr], refs[nr:nr + npar]
        oo, ao = refs[nin:nin + no], refs[nin + no:]
        res = f(*[r[...] for r in rin], *[p[...] for p in pin])
        if not isinstance(res, (tuple, list)):
            res = (res,)
        for o_ref, v in zip(oo, res[:no]):
            o_ref[...] = v.astype(o_ref.dtype)
        i = pl.program_id(0)
        for a_ref, v in zip(ao, res[no:]):
            @pl.when(i == 0)
            def _():
                a_ref[...] = jnp.zeros_like(a_ref)

            a_ref[...] += v.reshape(a_ref.shape)

    res = pl.pallas_call(
        body,
        name=name,
        grid=(s // tm,),
        in_specs=in_specs,
        out_specs=out_specs,
        out_shape=out_shape,
        compiler_params=pltpu.CompilerParams(dimension_semantics=("arbitrary",), vmem_limit_bytes=VMEM_LIMIT),
    )(*[r[0] for r in rows], *params, *([] if dep is None else [dep]))
    return res


def _mmv(a, b, mode):
    ca = 0 if mode[0] == "t" else 1
    cb = 1 if mode[1] == "t" else 0
    return lax.dot_general(a.astype(BF16), b.astype(BF16), (((ca,), (cb,)), ((), ())), preferred_element_type=F32)


@functools.partial(jax.custom_vjp, nondiff_argnums=(2,))
def _bdot(a, b, mode):
    return _mmv(a, b, mode)


def _bdot_fwd(a, b, mode):
    return _mmv(a, b, mode), (a, b)


def _bdot_bwd(mode, saved, g):
    a, b = saved
    if mode == "nn":
        return _mmv(g, b, "nt"), _mmv(a, g, "tn")
    if mode == "nt":
        return _mmv(g, b, "nn"), _mmv(g, a, "tn")
    return _mmv(b, g, "nt"), _mmv(a, g, "nn")


_bdot.defvjp(_bdot_fwd, _bdot_bwd)


def _hdot(a, b, mode="nn", precision=HI):
    ca = 0 if mode[0] == "t" else 1
    cb = 1 if mode[1] == "t" else 0
    return lax.dot_general(a, b, (((ca,), (cb,)), ((), ())), precision=precision, preferred_element_type=F32)


def _segsum(x):
    c = x.shape[-1]
    blk = min(c, 256)
    r = lax.broadcasted_iota(jnp.int32, (blk, blk), 0) >> 6
    q = lax.broadcasted_iota(jnp.int32, (blk, blk), 1) >> 6
    ones = jnp.where(r == q, 1.0, 0.0).astype(F32)
    parts = [_hdot(x[:, i:i + blk], ones, precision=lax.Precision.HIGH) for i in range(0, c, blk)]
    return parts[0] if len(parts) == 1 else jnp.concatenate(parts, axis=1)


def _sigmoid(x):
    return jax.nn.sigmoid(x)


def _softplus(x):
    return jnp.maximum(x, 0.0) + jnp.log(1.0 + jnp.exp(-jnp.abs(x)))


def _rms(x, gain):
    return x * lax.rsqrt(jnp.mean(x * x, axis=-1, keepdims=True) + RMS_EPS) * gain


def _swiglu_act(gate, up):
    return gate * _sigmoid(gate) * up


def _rwkv_pre(zs, w0, w2, a0, a2, g2, k_k, k_a):
    r, k, v = zs[:, 0:512], zs[:, 512:1024], zs[:, 1024:1536]
    lora = zs[:, 1536:1792]
    wd, ad, gd = lora[:, 0:64], lora[:, 64:128], lora[:, 128:256]
    w = -_softplus(-(w0 + _bdot(jnp.tanh(wd), w2, "nn"))) - 0.5
    a = _sigmoid(a0 + _bdot(ad, a2, "nn"))
    g = _bdot(_sigmoid(gd), g2, "nn")
    kk = k * k_k
    kk = kk * lax.rsqrt(jnp.maximum(_segsum(kk * kk), 1e-24))
    k2 = k * (1.0 + (a - 1.0) * k_a)
    return r, -jnp.exp(w), k2, v, -kk, kk * a, g


def _rwkv_post(y, r, k2, v, g, gn_w, gn_b, r_k):
    mean = _segsum(y) * (1.0 / HEAD)
    yc = y - mean
    var = _segsum(yc * yc) * (1.0 / HEAD)
    yn = yc * lax.rsqrt(var + GN_EPS) * gn_w + gn_b
    bonus = _segsum(r * k2 * r_k) * v
    return (yn + bonus) * g


def _swap_halves(x):
    lane = lax.broadcasted_iota(jnp.int32, x.shape, 1)
    return jnp.where((lane & 32) == 0, jnp.roll(x, -32, axis=1), jnp.roll(x, 32, axis=1))


def _norm_rope(x, gain, cos, sin):
    heads = x.shape[1] // HEAD
    def rep(t):
        return jnp.concatenate([t] * heads, axis=1)

    xn = x * lax.rsqrt(_segsum(x * x) * (1.0 / HEAD) + RMS_EPS) * rep(gain)
    return xn * rep(cos) + _swap_halves(xn) * rep(sin)


def _attn_combine(o0, o1, o2, l0, l1, l2):
    m = jnp.maximum(jnp.maximum(l0, l1), l2)
    e0, e1, e2 = jnp.exp(l0 - m), jnp.exp(l1 - m), jnp.exp(l2 - m)
    return (e0 * o0 + e1 * o1 + e2 * o2) / (e0 + e1 + e2)


def _merge(zgr, zga, br, ba):
    return _sigmoid(zgr) * br + _sigmoid(zga) * ba


def _attn_block(q, kp, kc, vp, vc, has_prev):
    iq = lax.broadcasted_iota(jnp.int32, (1, BAND, BAND), 1)
    ik = lax.broadcasted_iota(jnp.int32, (1, BAND, BAND), 2)
    s_c = jnp.where(iq >= ik, _bdotb(q, kc, "nt") * (HEAD ** -0.5), NEG_INF)
    s_p = jnp.where(jnp.logical_and(iq <= ik, has_prev), _bdotb(q, kp, "nt") * (HEAD ** -0.5), NEG_INF)
    m = lax.stop_gradient(jnp.maximum(jnp.max(s_c, axis=-1, keepdims=True), jnp.max(s_p, axis=-1, keepdims=True)))
    e_c, e_p = jnp.exp(s_c - m), jnp.exp(s_p - m)
    l = jnp.sum(e_c, axis=-1, keepdims=True) + jnp.sum(e_p, axis=-1, keepdims=True)
    o = (_bdotb(e_c, vc) + _bdotb(e_p, vp)) / l
    return o, jnp.broadcast_to(m + jnp.log(l), o.shape)


def _mmb(a, b, cb):
    return lax.dot_general(a.astype(BF16), b.astype(BF16), (((2,), (cb,)), ((0,), (0,))), preferred_element_type=F32)


@functools.partial(jax.custom_vjp, nondiff_argnums=(2,))
def _bdotb1(a, b, cb):
    return _mmb(a, b, cb)


def _bdotb1_fwd(a, b, cb):
    return _mmb(a, b, cb), (a, b)


def _bdotb1_bwd(cb, saved, g):
    a, b = saved
    if cb == 1:
        return _mmb(g, b, 2), _mmb(jnp.swapaxes(a, 1, 2), g, 1)
    return _mmb(g, b, 1), _mmb(jnp.swapaxes(g, 1, 2), a, 1)


_bdotb1.defvjp(_bdotb1_fwd, _bdotb1_bwd)


def _bdotb(a, b, mode="nn", precision=None):
    if mode[0] == "t":
        a = jnp.swapaxes(a, 1, 2)
    cb = 2 if mode[1] == "t" else 1
    if precision is None:
        return _bdotb1(a, b, cb)
    return lax.dot_general(a, b, (((2,), (cb,)), ((0,), (0,))), precision=precision, preferred_element_type=F32)


def _tri_inv_levels(a):
    t = a.shape[-1]
    row = lax.broadcasted_iota(jnp.int32, (1, t, t), 1)
    col = lax.broadcasted_iota(jnp.int32, (1, t, t), 2)
    x = jnp.where(row == col, 1.0, 0.0).astype(F32) + jnp.where(jnp.logical_and(row == col + 1, (row & 1) == 1), a, 0.0)
    sh = 1
    while (1 << sh) < t:
        m = jnp.logical_and((row >> sh) == (col >> sh) + 1, (row >> (sh + 1)) == (col >> (sh + 1)))
        x = x + _bdotb(_bdotb(x, jnp.where(m, a, 0.0)), x)
        sh += 1
    return x


@jax.custom_vjp
def _tri_inv(a):
    return _tri_inv_levels(a)


def _tri_inv_fwd(a):
    x = _tri_inv_levels(a)
    return x, x


def _tri_inv_bwd(x, g):
    xt = jnp.swapaxes(x, 1, 2)
    return (_bdotb(_bdotb(xt, g, precision=lax.Precision.HIGH), xt, precision=lax.Precision.HIGH),)


_tri_inv.defvjp(_tri_inv_fwd, _tri_inv_bwd)


@jax.custom_vjp
def _known_inv(a, x):
    return x


def _known_inv_fwd(a, x):
    return x, x


def _known_inv_bwd(x, g):
    return _tri_inv_bwd(x, g)[0], jnp.zeros_like(x)


_known_inv.defvjp(_known_inv_fwd, _known_inv_bwd)


def _wkv_chunk(s0, r, lw, k, v, a, b, inv=None, with_inv=False):
    nh, t, _ = r.shape
    row = lax.broadcasted_iota(jnp.int32, (1, t, t), 1)
    col = lax.broadcasted_iota(jnp.int32, (1, t, t), 2)
    incl, strict = row >= col, row > col
    ones = jnp.broadcast_to(jnp.where(incl, 1.0, 0.0).astype(F32), (nh, t, t))
    cum = _bdotb(ones, lw, precision=HI)
    c_end = cum[:, t - 1:t, :]
    e_in, e_ex, e_inv = jnp.exp(cum), jnp.exp(cum - lw), jnp.exp(-cum)
    at, rt, bt, kt = a * e_ex, r * e_in, b * e_inv, k * e_inv
    a_ab = jnp.where(strict, _bdotb(at, bt, "nt"), 0.0)
    a_ak = jnp.where(strict, _bdotb(at, kt, "nt"), 0.0)
    x = _tri_inv(a_ab) if inv is None else _known_inv(a_ab, inv)
    u = _bdotb(x, _bdotb(at, s0, "nt") + _bdotb(a_ak, v))
    y = (_bdotb(rt, s0, "nt") + _bdotb(jnp.where(incl, _bdotb(rt, bt, "nt"), 0.0), u)
         + _bdotb(jnp.where(incl, _bdotb(rt, kt, "nt"), 0.0), v))
    w_end = jnp.exp(c_end - cum)
    s1 = s0 * jnp.exp(c_end) + _bdotb(u, b * w_end, "tn") + _bdotb(v, k * w_end, "tn")
    return (y, s1, x) if with_inv else (y, s1)


def _shift_fwd(z, mu):
    s, c = z.shape
    tc = 256

    def body(z_ref, mu_ref, o_ref):
        zz = z_ref[...]
        row = lax.broadcasted_iota(jnp.int32, zz.shape, 0)
        prev = jnp.where(row == 0, 0.0, pltpu.roll(zz, 1, 0))
        o_ref[...] = zz + (prev - zz) * mu_ref[...]

    return pl.pallas_call(
        body, name="shift_fwd", grid=(c // tc,),
        in_specs=[pl.BlockSpec((s, tc), lambda j: (0, j)), pl.BlockSpec((1, tc), lambda j: (0, j))],
        out_specs=pl.BlockSpec((s, tc), lambda j: (0, j)), out_shape=SDS((s, c), F32),
        compiler_params=pltpu.CompilerParams(dimension_semantics=("parallel",), vmem_limit_bytes=VMEM_LIMIT),
    )(z, mu)


def _shift_bwd(z, mu, dzs):
    s, c = z.shape
    tc = 256

    def body(z_ref, mu_ref, d_ref, dz_ref, dmu_ref):
        zz, d, m = z_ref[...], d_ref[...], mu_ref[...]
        row = lax.broadcasted_iota(jnp.int32, zz.shape, 0)
        prev = jnp.where(row == 0, 0.0, pltpu.roll(zz, 1, 0))
        t = d * m
        nxt = jnp.where(row == s - 1, 0.0, pltpu.roll(t, s - 1, 0))
        dz_ref[...] = (d - t + nxt).astype(dz_ref.dtype)
        dmu_ref[...] = jnp.sum(d * (prev - zz), axis=0, keepdims=True)

    return pl.pallas_call(
        body, name="shift_bwd", grid=(c // tc,),
        in_specs=[pl.BlockSpec((s, tc), lambda j: (0, j)), pl.BlockSpec((1, tc), lambda j: (0, j)),
                  pl.BlockSpec((s, tc), lambda j: (0, j))],
        out_specs=[pl.BlockSpec((s, tc), lambda j: (0, j)), pl.BlockSpec((1, tc), lambda j: (0, j))],
        out_shape=[SDS((s, c), BF16), SDS((1, c), F32)],
        compiler_params=pltpu.CompilerParams(dimension_semantics=("parallel",), vmem_limit_bytes=VMEM_LIMIT),
    )(z, mu, dzs)


def _heads(x, nh):
    return jnp.stack([x[:, h * HEAD:(h + 1) * HEAD] for h in range(nh)], axis=0)


def _unheads(x):
    return jnp.concatenate([x[h] for h in range(x.shape[0])], axis=1)


def _wkv_fwd(zs, lw, k2, na, b):
    s = lw.shape[0]
    t, hb = WKV_CHUNK, WKV_HEADS_PER_STEP
    w = hb * HEAD
    nc, ng = s // t, RWKV_HEADS // hb

    def body(r_ref, v_ref, lw_ref, k_ref, a_ref, b_ref, y_ref, s0_ref, x_ref, state):
        @pl.when(pl.program_id(1) == 0)
        def _():
            state[...] = jnp.zeros_like(state)

        s0 = state[...]
        s0_ref[0] = s0
        y, s1, x = _wkv_chunk(s0, *[_heads(t_ref[...], hb) for t_ref in (r_ref, lw_ref, k_ref, v_ref, a_ref, b_ref)], with_inv=True)
        y_ref[...] = _unheads(y)
        x_ref[0] = x
        state[...] = s1

    def col(off):
        return pl.BlockSpec((t, w), functools.partial(lambda g, i, off: (i, g + off), off=off))

    return pl.pallas_call(
        body, name="wkv_fwd", grid=(ng, nc),
        in_specs=[col(0), col(2 * ng), col(0), col(0), col(0), col(0)],
        out_specs=[col(0), pl.BlockSpec((1, hb, HEAD, HEAD), lambda g, i: (i, g, 0, 0)),
                   pl.BlockSpec((1, hb, t, t), lambda g, i: (i, g, 0, 0))],
        out_shape=[SDS((s, RWKV_DIM), F32), SDS((nc, RWKV_HEADS, HEAD, HEAD), F32), SDS((nc, RWKV_HEADS, t, t), F32)],
        scratch_shapes=[pltpu.VMEM((hb, HEAD, HEAD), F32)],
        compiler_params=pltpu.CompilerParams(dimension_semantics=("parallel", "arbitrary"), vmem_limit_bytes=VMEM_LIMIT),
    )(zs, zs, lw, k2, na, b)


def _wkv_bwd(zs, lw, k2, na, b, s0s, invs, dy):
    s = lw.shape[0]
    t, hb = WKV_CHUNK, WKV_HEADS_PER_STEP
    w = hb * HEAD
    nc, ng = s // t, RWKV_HEADS // hb

    def body(r_ref, v_ref, lw_ref, k_ref, a_ref, b_ref, s0_ref, x_ref, dy_ref, dr_ref, dlw_ref, dk_ref, dv_ref, da_ref, db_ref, dstate):
        @pl.when(pl.program_id(1) == 0)
        def _():
            dstate[...] = jnp.zeros_like(dstate)

        _, vjp = jax.vjp(functools.partial(_wkv_chunk, inv=x_ref[0]), s0_ref[0],
                         *[_heads(t_ref[...], hb) for t_ref in (r_ref, lw_ref, k_ref, v_ref, a_ref, b_ref)])
        grads = vjp((_heads(dy_ref[...], hb), dstate[...]))
        dstate[...] = grads[0]
        for o_ref, gval in zip((dr_ref, dlw_ref, dk_ref, dv_ref, da_ref, db_ref), grads[1:]):
            o_ref[...] = _unheads(gval)

    def col(off):
        return pl.BlockSpec((t, w), functools.partial(lambda g, i, off: (nc - 1 - i, g + off), off=off))

    return pl.pallas_call(
        body, name="wkv_bwd", grid=(ng, nc),
        in_specs=[col(0), col(2 * ng), col(0), col(0), col(0), col(0),
                  pl.BlockSpec((1, hb, HEAD, HEAD), lambda g, i: (nc - 1 - i, g, 0, 0)),
                  pl.BlockSpec((1, hb, t, t), lambda g, i: (nc - 1 - i, g, 0, 0)), col(0)],
        out_specs=[col(0)] * 6,
        out_shape=[SDS((s, RWKV_DIM), F32)] * 6,
        scratch_shapes=[pltpu.VMEM((hb, HEAD, HEAD), F32)],
        compiler_params=pltpu.CompilerParams(dimension_semantics=("parallel", "arbitrary"), vmem_limit_bytes=VMEM_LIMIT),
    )(zs, zs, lw, k2, na, b, s0s, invs, dy)


def _attn_batch(refs, bps, nh, first_has_prev):
    q_ref, kp_ref, kc_ref, vp_ref, vc_ref = refs

    def blocks(cur_ref, prev_ref=None):
        out = []
        for b in range(bps):
            if prev_ref is None:
                t = cur_ref[b * BAND:(b + 1) * BAND, :]
            else:
                t = prev_ref[...] if b == 0 else cur_ref[(b - 1) * BAND:b * BAND, :]
            out.append(_heads(t.astype(F32), nh))
        return out[0] if bps == 1 else jnp.concatenate(out, axis=0)

    batch = lax.broadcasted_iota(jnp.int32, (bps * nh, 1, 1), 0)
    has_prev = jnp.logical_or(batch >= nh, first_has_prev)
    return (blocks(q_ref), blocks(kc_ref, kp_ref), blocks(kc_ref), blocks(vc_ref, vp_ref), blocks(vc_ref)), has_prev


def _attn_rows(x, bps, nh):
    parts = [_unheads(x[b * nh:(b + 1) * nh]) for b in range(bps)]
    return parts[0] if bps == 1 else jnp.concatenate(parts, axis=0)


def _attn_fwd(q, k, v, d):
    s = q.shape[0]
    l = s // d
    nb = l // BAND
    assert nb * BAND == l
    qv, kv, vv = (t.reshape(l, d * GROUP_DIM) for t in (q, k, v))
    width = min(d, ATTN_CLASSES_PER_STEP) * GROUP_DIM
    bps = ATTN_CLASSES_PER_STEP * GROUP_DIM // width
    nh = width // HEAD

    def body(q_ref, kp_ref, kc_ref, vp_ref, vc_ref, o_ref, l_ref):
        ops, has_prev = _attn_batch((q_ref, kp_ref, kc_ref, vp_ref, vc_ref), bps, nh, pl.program_id(1) > 0)
        o, lse = _attn_block(*ops, has_prev)
        o_ref[...] = _attn_rows(o, bps, nh)
        l_ref[...] = _attn_rows(lse, bps, nh)

    cur = pl.BlockSpec((bps * BAND, width), lambda rho, i: (i, rho))
    prev = pl.BlockSpec((BAND, width), lambda rho, i: (jnp.maximum(i * bps - 1, 0), rho))
    o, lse = pl.pallas_call(
        body, name=f"attn_fwd_d{d}", grid=(d * GROUP_DIM // width, nb // bps),
        in_specs=[cur, prev, cur, prev, cur], out_specs=[cur, cur],
        out_shape=[SDS((l, d * GROUP_DIM), F32), SDS((l, d * GROUP_DIM), F32)],
        compiler_params=pltpu.CompilerParams(dimension_semantics=("parallel", "arbitrary"), vmem_limit_bytes=VMEM_LIMIT),
    )(qv, kv, kv, vv, vv)
    return o.reshape(s, GROUP_DIM), lse.reshape(s, GROUP_DIM)


def _attn_bwd(q, k, v, d, do, dlse):
    s = q.shape[0]
    l = s // d
    nb = l // BAND
    qv, kv, vv, dov, dlv = (t.reshape(l, d * GROUP_DIM) for t in (q, k, v, do, dlse))
    width = min(d, ATTN_CLASSES_PER_STEP) * GROUP_DIM
    bps = ATTN_CLASSES_PER_STEP * GROUP_DIM // width
    nh = width // HEAD
    ns = nb // bps

    def body(q_ref, kp_ref, kc_ref, vp_ref, vc_ref, do_ref, dl_ref, dq_ref, dk_ref, dv_ref, ck, cv):
        step = pl.program_id(1)

        @pl.when(step == 0)
        def _():
            ck[...] = jnp.zeros_like(ck)
            cv[...] = jnp.zeros_like(cv)

        ops, has_prev = _attn_batch((q_ref, kp_ref, kc_ref, vp_ref, vc_ref), bps, nh, step < ns - 1)
        _, vjp = jax.vjp(functools.partial(_attn_block, has_prev=has_prev), *ops)
        cts = [jnp.concatenate([_heads(t_ref[b * BAND:(b + 1) * BAND, :], nh) for b in range(bps)], axis=0) if bps > 1
               else _heads(t_ref[...], nh) for t_ref in (do_ref, dl_ref)]
        dq, dkp, dkc, dvp, dvc = vjp(tuple(cts))
        dq_ref[...] = _attn_rows(dq, bps, nh)
        for out_ref, cur_part, prev_part, carry in ((dk_ref, dkc, dkp, ck), (dv_ref, dvc, dvp, cv)):
            for b in range(bps):
                after = carry[...] if b == bps - 1 else _unheads(prev_part[(b + 1) * nh:(b + 2) * nh])
                out_ref[b * BAND:(b + 1) * BAND, :] = _unheads(cur_part[b * nh:(b + 1) * nh]) + after
            carry[...] = _unheads(prev_part[0:nh])

    cur = pl.BlockSpec((bps * BAND, width), lambda rho, i: (ns - 1 - i, rho))
    prev = pl.BlockSpec((BAND, width), lambda rho, i: (jnp.maximum((ns - 1 - i) * bps - 1, 0), rho))
    dq, dk, dv = pl.pallas_call(
        body, name=f"attn_bwd_d{d}", grid=(d * GROUP_DIM // width, ns),
        in_specs=[cur, prev, cur, prev, cur, cur, cur], out_specs=[cur] * 3,
        out_shape=[SDS((l, d * GROUP_DIM), F32)] * 3,
        scratch_shapes=[pltpu.VMEM((BAND, width), F32), pltpu.VMEM((BAND, width), F32)],
        compiler_params=pltpu.CompilerParams(dimension_semantics=("parallel", "arbitrary"), vmem_limit_bytes=VMEM_LIMIT),
    )(qv, kv, kv, vv, vv, dov, dlv)
    return dq.reshape(s, GROUP_DIM), dk.reshape(s, GROUP_DIM), dv.reshape(s, GROUP_DIM)


def _coords():
    return lax.axis_index("x"), lax.axis_index("y"), lax.axis_index("c")


_CHIP_FLIPS = ((1, 0), (0, 1), (1, 1))


def _flip(v, f):
    return 1 - v if f else v


def _form(kind, r, c):
    return (N_CHIPS, r, c) if kind == "blk" else (r, N_CHIPS * c)


def _slot(ref, kind, j, rows, c):
    if kind == "blk":
        return ref.at[j] if rows is None else ref.at[j, rows]
    cols = pl.ds(pl.multiple_of(j * c, 128), c)
    return ref.at[:, cols] if rows is None else ref.at[rows, cols]


def _half(r, which, align):
    return pl.ds(pl.multiple_of(which * (r // 2), align), r // 2)


def _rcopy(src, dst, send_sems, recv_sems, kk, dev):
    return pltpu.make_async_remote_copy(src_ref=src, dst_ref=dst, send_sem=send_sems.at[kk], recv_sem=recv_sems.at[kk],
                                        device_id=dev, device_id_type=MESH)


def _gather_plan(specs, step):
    def copies(refs, ss, rs, received):
        x, y, c = _coords()
        out = []
        for w, (kind, r, cc) in enumerate(specs):
            mine, other = _half(r, c, 16), _half(r, 1 - c, 16)
            for kk, (fx, fy) in enumerate(_CHIP_FLIPS):
                px, py = _flip(x, fx), _flip(y, fy)
                if step == "ici":
                    sl = _slot(refs[w], kind, 2 * px + py if received else 2 * x + y, mine, cc)
                    dev = (px, py, c)
                else:
                    sl = _slot(refs[w], kind, 2 * px + py, other if received else mine, cc)
                    dev = (x, y, 1 - c)
                out.append(_rcopy(sl, sl, ss, rs, 3 * w + kk, dev))
        return out

    def issue(refs, ss, rs):
        return copies(refs, ss, rs, False)

    def expect(refs, ss, rs):
        return copies(refs, ss, rs, False), copies(refs, ss, rs, True)

    return issue, expect


_HBM = pl.BlockSpec(memory_space=pltpu.HBM)
_SEM = pl.BlockSpec(memory_space=pltpu.SEMAPHORE)
_EFFECT = pltpu.SideEffectType.DATAFLOW_SIDE_EFFECTING


def _copies_start(name, bufs, n_sems, issue, after=None):
    nb = len(bufs)
    extra = [] if after is None else [after]

    def body(*refs):
        send_sems, recv_sems = refs[nb + len(extra)], refs[nb + len(extra) + 1]
        for cp in issue(refs[:nb], send_sems, recv_sems):
            cp.start()
        refs[-1][...] = jnp.zeros_like(refs[-1])

    outs = pl.pallas_call(
        body, name=name,
        out_shape=(pltpu.SemaphoreType.DMA((n_sems,)), pltpu.SemaphoreType.DMA((n_sems,)),
                   *[pltpu.HBM(b.shape, b.dtype) for b in bufs], SDS((8, 128), F32)),
        in_specs=[_HBM] * nb + [pl.BlockSpec(memory_space=pl.ANY)] * len(extra),
        out_specs=(_SEM, _SEM, *[_HBM] * nb, pl.BlockSpec(memory_space=pltpu.VMEM)),
        input_output_aliases={i: 2 + i for i in range(nb)},
        compiler_params=pltpu.CompilerParams(has_side_effects=_EFFECT),
    )(*[pltpu.with_memory_space_constraint(b, pltpu.HBM) for b in bufs], *extra)
    return outs[0], outs[1], list(outs[2:2 + nb]), outs[-1]


def _copies_wait(name, bufs, send_sems, recv_sems, after, expect):
    nb = len(bufs)

    def body(*refs):
        sent, received = expect(refs[:nb], refs[nb], refs[nb + 1])
        for cp in sent:
            cp.wait_send()
        for cp in received:
            cp.wait_recv()

    outs = pl.pallas_call(
        body, name=name,
        out_shape=tuple(pltpu.HBM(b.shape, b.dtype) for b in bufs),
        in_specs=(*[_HBM] * nb, _SEM, _SEM, pl.BlockSpec(memory_space=pl.ANY)), out_specs=tuple([_HBM] * nb),
        input_output_aliases={i: i for i in range(nb)},
        compiler_params=pltpu.CompilerParams(has_side_effects=_EFFECT),
    )(*bufs, send_sems, recv_sems, after)
    return list(outs)


def _add_pair(g, recv, kind, r, c, c_arr, name):
    h = r // 2
    if kind == "blk":
        tr = _row_tile(h, 512)
        grid = (N_CHIPS, h // tr)
        g_spec = pl.BlockSpec((1, 1, tr, c), lambda j, i, c_ref: (j, c_ref[0], i, 0))
        o_spec = pl.BlockSpec((1, tr, c), lambda j, i, c_ref: (j, i, 0))
        gv, oshape = g.reshape(N_CHIPS, 2, h, c), (N_CHIPS, h, c)
    else:
        tr = _row_tile(h, 64)
        grid = (h // tr,)
        g_spec = pl.BlockSpec((1, tr, N_CHIPS * c), lambda i, c_ref: (c_ref[0], i, 0))
        o_spec = pl.BlockSpec((tr, N_CHIPS * c), lambda i, c_ref: (i, 0))
        gv, oshape = g.reshape(2, h, N_CHIPS * c), (h, N_CHIPS * c)

    def body(c_ref, g_ref, r_ref, o_ref, ob_ref):
        v = (g_ref[:, 0] if kind == "blk" else g_ref[0]) + r_ref[...]
        o_ref[...] = v
        ob_ref[...] = v.astype(BF16)

    return pl.pallas_call(
        body, name=name,
        grid_spec=pltpu.PrefetchScalarGridSpec(num_scalar_prefetch=1, grid=grid, in_specs=[g_spec, o_spec], out_specs=[o_spec] * 2),
        out_shape=[SDS(oshape, F32), SDS(oshape, BF16)],
        compiler_params=pltpu.CompilerParams(vmem_limit_bytes=VMEM_LIMIT),
    )(c_arr, gv, recv)


def _sum_chips(pair, recv, kind, r, c, mc_arr, name):
    h = r // 2
    tr = _row_tile(h, 512)
    nt = h // tr
    if kind == "blk":
        p_spec = pl.BlockSpec((1, tr, c), lambda i, mc: (mc[0], i, 0))
    else:
        p_spec = pl.BlockSpec((tr, c), lambda i, mc: (i, mc[0]))

    def body(mc, a_ref, r_ref, g_out):
        own = a_ref[0] if kind == "blk" else a_ref[...]
        g_out[...] = ((own + r_ref[0].astype(F32)) + r_ref[1].astype(F32)) + r_ref[2].astype(F32)

    return pl.pallas_call(
        body, name=name,
        grid_spec=pltpu.PrefetchScalarGridSpec(
            num_scalar_prefetch=1, grid=(nt,), in_specs=[p_spec, pl.BlockSpec((3, tr, c), lambda i, mc: (0, i, 0))],
            out_specs=pl.BlockSpec((tr, c), lambda i, mc: (mc[1] * nt + i, 0))),
        out_shape=SDS((r, c), F32),
        compiler_params=pltpu.CompilerParams(vmem_limit_bytes=VMEM_LIMIT),
    )(mc_arr, pair, recv)


class _GroupReduce:
    def __init__(self, tag, specs, c_arr, mc_arr):
        self.tag, self.specs, self.c_arr, self.mc_arr = tag, specs, c_arr, mc_arr
        self.n = len(specs)

    def _plan(self, step):
        specs, n = self.specs, self.n

        def copies(refs, ss, rs, received):
            x, y, c = _coords()
            sib, out = (x, y, 1 - c), []
            for w, (_, kind, r, cc) in enumerate(specs):
                if step == "join":
                    there = refs[w].at[_half(r, 1 - c if received else c, 8)]
                    out.append(_rcopy(there, there, ss, rs, w, sib))
                    continue
                src, land = refs[w], refs[n + w]
                if step == "swap":
                    rows = _half(r, 1 - c, 8)
                    part = src.at[:, rows] if kind == "blk" else src.at[rows]
                    out.append(_rcopy(land if received else part, land, ss, rs, w, sib))
                else:
                    for kk, (fx, fy) in enumerate(_CHIP_FLIPS):
                        px, py = _flip(x, fx), _flip(y, fy)
                        part = land.at[kk] if received else _slot(src, kind, 2 * px + py, None, cc)
                        out.append(_rcopy(part, land.at[kk], ss, rs, 3 * w + kk, (px, py, c)))
            return out

        def issue(refs, ss, rs):
            return copies(refs, ss, rs, False)

        def expect(refs, ss, rs):
            return copies(refs, ss, rs, False), copies(refs, ss, rs, True)

        return issue, expect

    def swap_start(self, grads, after=None):
        lands = [lax.empty(_form(kind, r // 2, c), F32) for _, kind, r, c in self.specs]
        ss, rs, bufs, tok = _copies_start(f"rs_{self.tag}_swap", list(grads) + lands, self.n, self._plan("swap")[0], after=after)
        self.state = (ss, rs, bufs)
        return tok

    def swap_wait_ici_start(self, after):
        ss, rs, bufs = self.state
        bufs = _copies_wait(f"rs_{self.tag}_swap_wait", bufs, ss, rs, after, self._plan("swap")[1])
        pairs = [_add_pair(bufs[w], bufs[self.n + w], kind, r, c, self.c_arr, name=f"rs_{self.tag}_pair_{nm}")
                 for w, (nm, kind, r, c) in enumerate(self.specs)]
        self.pair = [pr[0] for pr in pairs]
        lands = [lax.empty((3, r // 2, c), BF16) for _, _, r, c in self.specs]
        ss, rs, bufs, tok = _copies_start(f"rs_{self.tag}_ici", [pr[1] for pr in pairs] + lands, 3 * self.n, self._plan("ici")[0])
        self.state = (ss, rs, bufs)
        return tok

    def ici_wait_join_start(self, after):
        ss, rs, bufs = self.state
        bufs = _copies_wait(f"rs_{self.tag}_ici_wait", bufs, ss, rs, after, self._plan("ici")[1])
        outs = [_sum_chips(self.pair[w], bufs[self.n + w], kind, r, c, self.mc_arr, name=f"rs_{self.tag}_sum_{nm}")
                for w, (nm, kind, r, c) in enumerate(self.specs)]
        ss, rs, bufs, tok = _copies_start(f"rs_{self.tag}_join", outs, self.n, self._plan("join")[0])
        self.state = (ss, rs, bufs)
        return tok

    def join_wait(self, after):
        ss, rs, bufs = self.state
        bufs = _copies_wait(f"rs_{self.tag}_join_wait", bufs, ss, rs, after, self._plan("join")[1])
        return {nm: bufs[w] for w, (nm, _, _, _) in enumerate(self.specs)}


def _all_reduce_small(buf):
    rows, cols = buf.shape

    def body(x_ref, o_ref, gath, send_sems, recv_sems):
        x, y, c = _coords()
        me = 4 * x + 2 * y + c
        gath[me] = x_ref[...]
        sends = []
        for kk in range(1, 8):
            f = (kk >> 2) & 1, (kk >> 1) & 1, kk & 1
            px, py, pc = _flip(x, f[0]), _flip(y, f[1]), _flip(c, f[2])
            cp = pltpu.make_async_remote_copy(src_ref=x_ref, dst_ref=gath.at[me], send_sem=send_sems.at[kk - 1],
                                              recv_sem=recv_sems.at[kk - 1], device_id=(px, py, pc), device_id_type=MESH)
            cp.start()
            sends.append(cp)
        for kk in range(1, 8):
            f = (kk >> 2) & 1, (kk >> 1) & 1, kk & 1
            px, py, pc = _flip(x, f[0]), _flip(y, f[1]), _flip(c, f[2])
            there = gath.at[4 * px + 2 * py + pc]
            pltpu.make_async_remote_copy(src_ref=there, dst_ref=there, send_sem=send_sems.at[kk - 1],
                                         recv_sem=recv_sems.at[kk - 1], device_id=(px, py, pc), device_id_type=MESH).wait_recv()
        for cp in sends:
            cp.wait_send()
        acc = gath[0]
        for j in range(1, 8):
            acc = acc + gath[j]
        o_ref[...] = acc

    return pl.pallas_call(
        body, name="all_reduce_small",
        in_specs=[pl.BlockSpec(memory_space=pltpu.VMEM)], out_specs=pl.BlockSpec(memory_space=pltpu.VMEM),
        out_shape=SDS((rows, cols), F32),
        scratch_shapes=[pltpu.VMEM((8, rows, cols), F32), pltpu.SemaphoreType.DMA((7,)), pltpu.SemaphoreType.DMA((7,))],
    )(buf)


def _adamw_rows(w, g, m, v):
    m = ADAM_B1 * m + (1.0 - ADAM_B1) * g
    v = ADAM_B2 * v + (1.0 - ADAM_B2) * jnp.square(g)
    m_hat = m / (1.0 - ADAM_B1 ** ADAM_STEP)
    v_hat = v / (1.0 - ADAM_B2 ** ADAM_STEP)
    return -ADAM_LR * (m_hat / (jnp.sqrt(v_hat) + ADAM_EPS) + ADAM_WD * w), m, v


def _adamw(w, g, m, v, name, dep=None, with_grad=False):
    rows, cols = w.shape
    tm = _pick(rows, (256, 128, 64, 16, 8))
    f = (lambda wt, gt, mt, vt: (gt,) + _adamw_rows(wt, gt, mt, vt)) if with_grad else _adamw_rows
    return _rows_call(f, [(t, 0, cols) for t in (w, g, m, v)], [], [(cols, F32)] * (3 + with_grad), tm=tm, name=name, dep=dep)


def _pack_small(parts):
    flat = jnp.concatenate([parts[n].reshape(-1) for n, _ in SMALL])
    return jnp.pad(flat, (0, SMALL_ROWS * PACK_COLS - flat.shape[0])).reshape(SMALL_ROWS, PACK_COLS)


def _unpack_small(buf, shapes):
    flat, out, off = buf.reshape(-1), {}, 0
    for n, sz in SMALL:
        out[n] = flat[off:off + sz].reshape(shapes[n])
        off += sz
    return out


def _lora_stack(parts):
    return jnp.concatenate([parts[n] for n, _ in LORA], axis=-2)


def _lora_split(stacked):
    out, off = {}, 0
    for n, rows in LORA:
        out[n] = stacked[..., off:off + rows, :]
        off += rows
    return out


def _ffn_gate_up(h, wgt, wut, name, dep=None):
    s, d = h.shape
    nblk, f, _ = wgt.shape
    tm = _pick(s, (1024, 512, 256))
    dn = (((1,), (1,)), ((), ()))

    def body(h_ref, wg_ref, wu_ref, *rest):
        g_ref, u_ref, a_ref = rest[-3:]
        hh = h_ref[...]
        g = lax.dot_general(hh, wg_ref[0], dn, preferred_element_type=F32)
        u = lax.dot_general(hh, wu_ref[0], dn, preferred_element_type=F32)
        g_ref[0], u_ref[0] = g.astype(BF16), u.astype(BF16)
        a_ref[0] = _swiglu_act(g, u).astype(BF16)

    w_spec = pl.BlockSpec((1, f, d), lambda j, i: (j, 0, 0))
    o_spec = pl.BlockSpec((1, tm, f), lambda j, i: (j, i, 0))
    extra = [] if dep is None else [dep]
    return pl.pallas_call(
        body, name=name, grid=(nblk, s // tm),
        in_specs=[pl.BlockSpec((tm, d), lambda j, i: (i, 0)), w_spec, w_spec] + [pl.BlockSpec(memory_space=pl.ANY)] * len(extra),
        out_specs=[o_spec] * 3,
        out_shape=[SDS((nblk, s, f), BF16)] * 3,
        compiler_params=pltpu.CompilerParams(dimension_semantics=("parallel", "parallel"), vmem_limit_bytes=VMEM_LIMIT),
    )(h, wgt, wut, *extra)


def _ffn_down_dx(dx_bf, wd, gate, up, name, dep=None):
    s, d = dx_bf.shape
    nblk, f, _ = wd.shape
    tm = _pick(s, (1024, 512, 256))
    dn = (((1,), (1,)), ((), ()))

    def body(dx_ref, wd_ref, g_ref, u_ref, *rest):
        dg_ref, du_ref = rest[-2:]
        dact = 0.5 * lax.dot_general(dx_ref[...], wd_ref[0], dn, preferred_element_type=F32)
        _, vjp = jax.vjp(_swiglu_act, g_ref[0].astype(F32), u_ref[0].astype(F32))
        dg, du = vjp(dact)
        dg_ref[0], du_ref[0] = dg.astype(BF16), du.astype(BF16)

    o_spec = pl.BlockSpec((1, tm, f), lambda j, i: (j, i, 0))
    extra = [] if dep is None else [dep]
    return pl.pallas_call(
        body, name=name, grid=(nblk, s // tm),
        in_specs=[pl.BlockSpec((tm, d), lambda j, i: (i, 0)), pl.BlockSpec((1, f, d), lambda j, i: (j, 0, 0)), o_spec, o_spec]
        + [pl.BlockSpec(memory_space=pl.ANY)] * len(extra),
        out_specs=[o_spec] * 2, out_shape=[SDS((nblk, s, f), BF16)] * 2,
        compiler_params=pltpu.CompilerParams(dimension_semantics=("parallel", "parallel"), vmem_limit_bytes=VMEM_LIMIT),
    )(dx_bf, wd, gate, up, *extra)


def _ffn_fwd(x, gain, wgt, wut, wd, tag, h=None, dep=None):
    if h is None:
        h = _rows_call(_rms, [(x, 0, D_MODEL)], [gain], [(D_MODEL, BF16)], tm=512, name=f"{tag}_norm")[0]
    gate, up, act = _ffn_gate_up(h, wgt, wut, f"{tag}_gate_up", dep=dep)
    x_new = _mm(act, wd, sum_blocks=True, res=x, alpha=0.5, name=f"{tag}_down")
    return x_new, (x, h, gate, up, act)


def _ffn_bwd(dx_new, dx_new_bf, saved, gain, wgt, wut, wd, tag, dep=None, hooks=None):
    x, h, gate, up, act = saved
    hooks = hooks or {}

    def hook(name, *vals):
        return hooks[name](*vals) if name in hooks else None

    d_wd = _mm(act, dx_new_bf, ta=True, alpha=0.5, name=f"{tag}_down_dw")
    dep = hook("down", d_wd) if "down" in hooks else dep
    dgate, dup = _ffn_down_dx(dx_new_bf, wd, gate, up, f"{tag}_down_dx", dep=dep)
    d_wgt = _mm(dgate, h, ta=True, dep=hook("mid", dgate), name=f"{tag}_gate_dw")
    d_wut = _mm(dup, h, ta=True, name=f"{tag}_up_dw")
    dh = _mm(dgate, wgt, sum_blocks=True, dep=hook("dw", d_wgt, d_wut), name=f"{tag}_gate_dx")
    dx, dx_bf, dgain = _mm(dup, wut, sum_blocks=True, res=dh, dep=hook("dx", dh), post=_norm_bwd_post(x, gain, dx_new),
                           name=f"{tag}_up_dx")
    hook("end", dx_bf)
    return dx, dx_bf, dgain, d_wgt, d_wut, d_wd


def _norm_bwd_post(x, gain, dres):
    def f(dht, xt, drt, gt):
        _, vjp = jax.vjp(_rms, xt, gt)
        dxt, dgt = vjp(dht)
        return dxt + drt, dxt + drt, dgt

    return f, [x, dres], [gain], [F32, BF16], [(1, D_MODEL)]


def kernel(x, p, positions, ffn1_norm, ffn1_w_gate, ffn1_w_up, ffn1_w_down, mix_norm, w_in, rwkv_mu, rwkv_w0, rwkv_w2, rwkv_a0, rwkv_a2, rwkv_g2, rwkv_k_k, rwkv_k_a, rwkv_r_k, rwkv_gn_w, rwkv_gn_b, q_norm, k_norm, w_br_rwkv, w_br_attn, w_out, ffn2_norm, ffn2_w_gate, ffn2_w_up, ffn2_w_down, ple_norm, ple_w_gate, ple_w_proj, loss_target, m_ffn1_norm, m_ffn1_w_gate, m_ffn1_w_up, m_ffn1_w_down, m_mix_norm, m_w_in, m_rwkv_mu, m_rwkv_w0, m_rwkv_w2, m_rwkv_a0, m_rwkv_a2, m_rwkv_g2, m_rwkv_k_k, m_rwkv_k_a, m_rwkv_r_k, m_rwkv_gn_w, m_rwkv_gn_b, m_q_norm, m_k_norm, m_w_br_rwkv, m_w_br_attn, m_w_out, m_ffn2_norm, m_ffn2_w_gate, m_ffn2_w_up, m_ffn2_w_down, m_ple_norm, m_ple_w_gate, m_ple_w_proj, v_ffn1_norm, v_ffn1_w_gate, v_ffn1_w_up, v_ffn1_w_down, v_mix_norm, v_w_in, v_rwkv_mu, v_rwkv_w0, v_rwkv_w2, v_rwkv_a0, v_rwkv_a2, v_rwkv_g2, v_rwkv_k_k, v_rwkv_k_a, v_rwkv_r_k, v_rwkv_gn_w, v_rwkv_gn_b, v_q_norm, v_k_norm, v_w_br_rwkv, v_w_br_attn, v_w_out, v_ffn2_norm, v_ffn2_w_gate, v_ffn2_w_up, v_ffn2_w_down, v_ple_norm, v_ple_w_gate, v_ple_w_proj):
    args = dict(locals())
    wts = {n: args[n] for n in WEIGHTS}
    mom_m = {n: args["m_" + n] for n in WEIGHTS}
    mom_v = {n: args["v_" + n] for n in WEIGHTS}
    x0, tgt = x[0], loss_target[0]
    s = x0.shape[0]
    p_tok = p[0, 0]

    vec = {n: wts[n].reshape(1, -1) for n, _ in SMALL}
    xi, yi, ci = _coords()
    me = 2 * xi + yi
    def laid(t, n):
        return jnp.transpose(t[n][0]) if n in TRANSPOSED else t[n][0]

    shard_of = {n: laid(wts, n) for g in GROUPS.values() for n, _, _, _ in g if n != "lora"}
    shard_of["lora"] = _lora_stack({n: wts[n][0] for n, _ in LORA})

    def whole_with_own(n, kind, r, c, tok=None):
        at = (me, 0, 0) if kind == "blk" else (0, me * c)
        own = (shard_of[n] if tok is None else shard_of[n] + tok[0, 0]).astype(BF16)
        return lax.dynamic_update_slice(lax.empty(_form(kind, r, c), BF16), own[None] if kind == "blk" else own, at)

    specs = {g: [(kind, r, c) for _, kind, r, c in grp] for g, grp in GROUPS.items()}
    plans = {(g, st): _gather_plan(specs[g], st) for g in GROUPS for st in ("ici", "d2d")}
    buf_f1 = [whole_with_own(*w) for w in GROUPS["f1"]]
    ss_0, rs_0, buf_f1, tok_0 = _copies_start("gather_f1_ici", buf_f1, 3 * len(buf_f1), plans["f1", "ici"][0])
    bufs = {g: [whole_with_own(*w, tok=tok_0) for w in GROUPS[g]] for g in ("mx", "f2")}
    buf_f1 = _copies_wait("gather_f1_ici_wait", buf_f1, ss_0, rs_0, bufs["mx"][0], plans["f1", "ici"][1])
    ss_1, rs_1, buf_f1, tok_1 = _copies_start("gather_f1_d2d", buf_f1, 3 * len(buf_f1), plans["f1", "d2d"][0])
    h1 = _rows_call(_rms, [(x0, 0, D_MODEL)], [vec["ffn1_norm"] + tok_1[0, 0]], [(D_MODEL, BF16)], tm=512, name="ffn1_norm")[0]
    buf_f1 = _copies_wait("gather_f1_d2d_wait", buf_f1, ss_1, rs_1, h1, plans["f1", "d2d"][1])
    wb = dict(zip([w[0] for w in GROUPS["f1"]], buf_f1))
    ss_a, rs_a, buf_mx, tok_a = _copies_start("gather_mx_ici", bufs["mx"], 3 * len(bufs["mx"]), plans["mx", "ici"][0],
                                              after=wb["ffn1_w_gate"])

    inv_freq = 1.0 / (ROPE_THETA ** (jnp.arange(0, HEAD, 2, dtype=F32) / HEAD))
    ang = positions[0].astype(F32)[:, None] * inv_freq
    cos, sin = jnp.cos(ang), jnp.sin(ang)
    cos2, sin2 = jnp.concatenate([cos, cos], axis=1), jnp.concatenate([-sin, sin], axis=1)

    x1, ffn1_saved = _ffn_fwd(x0, vec["ffn1_norm"], wb["ffn1_w_gate"], wb["ffn1_w_up"], wb["ffn1_w_down"], "ffn1", h=h1, dep=tok_a)
    buf_mx = _copies_wait("gather_mx_ici_wait", buf_mx, ss_a, rs_a, x1, plans["mx", "ici"][1])
    ss_b, rs_b, buf_mx, tok_b = _copies_start("gather_mx_d2d", buf_mx, 3 * len(buf_mx), plans["mx", "d2d"][0])
    ss_c, rs_c, buf_f2, tok_c = _copies_start("gather_f2_ici", bufs["f2"], 3 * len(bufs["f2"]), plans["f2", "ici"][0])
    h = _rows_call(_rms, [(x1, 0, D_MODEL)], [vec["mix_norm"] + (tok_b[0, 0] + tok_c[0, 0])], [(D_MODEL, BF16)], tm=256,
                   name="mix_norm")[0]
    buf_mx = _copies_wait("gather_mx_d2d_wait", buf_mx, ss_b, rs_b, h, plans["mx", "d2d"][1])
    wb.update(zip([w[0] for w in GROUPS["mx"]], buf_mx))
    w_in_all = wb["w_in"]
    w_in_r, w_in_a, w_in_g = w_in_all[:, :RWKV_COLS], w_in_all[:, RWKV_COLS:RWKV_COLS + ATTN_COLS], w_in_all[:, RWKV_COLS + ATTN_COLS:]
    lora = _lora_split(wb["lora"])
    w2, a2, g2 = lora["rwkv_w2"], lora["rwkv_a2"], lora["rwkv_g2"]
    z_r = _mm(h, w_in_r, name="in_rwkv")
    z_a = _mm(h, w_in_a, name="in_attn")
    z_g = _mm(h, w_in_g, name="in_gate")

    zs = _shift_fwd(z_r, vec["rwkv_mu"])
    pre_params = [vec["rwkv_w0"], w2, vec["rwkv_a0"], a2, g2, vec["rwkv_k_k"], vec["rwkv_k_a"]]
    def pre_fwd(*t):
        res = _rwkv_pre(*t)
        return res[1], res[2], res[4], res[5], res[6]

    lw, k2, na, kb, gate_r = _rows_call(pre_fwd, [(zs, 0, RWKV_COLS)], pre_params, [(RWKV_DIM, F32)] * 5, tm=512, name="rwkv_pre")
    y_scan, s0s, invs = _wkv_fwd(zs, lw, k2, na, kb)
    buf_f2 = _copies_wait("gather_f2_ici_wait", buf_f2, ss_c, rs_c, y_scan, plans["f2", "ici"][1])
    ss_d, rs_d, buf_f2, tok_d = _copies_start("gather_f2_d2d", buf_f2, 3 * len(buf_f2), plans["f2", "d2d"][0])
    post_params = [vec["rwkv_gn_w"] + tok_d[0, 0], vec["rwkv_gn_b"], vec["rwkv_r_k"]]
    post_rows = [(y_scan, 0, RWKV_DIM), (zs, 0, RWKV_DIM), (k2, 0, RWKV_DIM), (zs, 2, RWKV_DIM), (gate_r, 0, RWKV_DIM)]
    y_rwkv = _rows_call(_rwkv_post, post_rows, post_params, [(RWKV_DIM, BF16)], tm=512, name="rwkv_post")[0]
    buf_f2 = _copies_wait("gather_f2_d2d_wait", buf_f2, ss_d, rs_d, y_rwkv, plans["f2", "d2d"][1])
    wb.update(zip([w[0] for w in GROUPS["f2"]], buf_f2))
    w_brr, w_bra = wb["w_br_rwkv"], wb["w_br_attn"]
    w_o = wb["w_out"].reshape(D_MODEL, D_MODEL)
    w_pp, w_pg = wb["ple_w_proj"], wb["ple_w_gate"].reshape(D_MODEL, D_MODEL)

    def qk_fwd(qt, kt, ct, st, qg, kg):
        return _norm_rope(qt, qg, ct, st), _norm_rope(kt, kg, ct, st)

    qk_rows = [(z_a, 0, ATTN_DIM), (z_a, 1, ATTN_DIM), (cos2, 0, HEAD), (sin2, 0, HEAD)]
    q_rot, k_rot = _rows_call(qk_fwd, qk_rows, [vec["q_norm"], vec["k_norm"]], [(ATTN_DIM, BF16)] * 2, tm=512, name="attn_pre")
    def group(t, g, off=0):
        return t[:, off + g * GROUP_DIM:off + (g + 1) * GROUP_DIM].astype(BF16)

    qkv = [(group(q_rot, g), group(k_rot, g), group(z_a, g, 2 * ATTN_DIM)) for g in range(len(ATTN_DILATIONS))]
    outs, lses = zip(*[_attn_fwd(*qkv[g], d) for g, d in enumerate(ATTN_DILATIONS)])
    comb_rows = [(t, 0, GROUP_DIM) for t in outs + lses]
    y_attn = _rows_call(_attn_combine, comb_rows, [], [(GROUP_DIM, BF16)], tm=512, name="attn_combine")[0]

    br = _mm(y_rwkv, w_brr, name="branch_rwkv")
    ba = _mm(y_attn, w_bra, name="branch_attn")
    merge_rows = [(z_g, 0, D_MODEL), (z_g, 1, D_MODEL), (br, 0, D_MODEL), (ba, 0, D_MODEL)]
    merged = _rows_call(_merge, merge_rows, [], [(D_MODEL, BF16)], tm=512, name="merge")[0]
    x2 = _mm(merged, w_o, res=x1, name="out_proj")
    x3, ffn2_saved = _ffn_fwd(x2, vec["ffn2_norm"], wb["ffn2_w_gate"], wb["ffn2_w_up"], wb["ffn2_w_down"], "ffn2")
    hp = _rows_call(_rms, [(x3, 0, D_MODEL)], [vec["ple_norm"]], [(D_MODEL, BF16)], tm=512, name="ple_norm")[0]
    pg = _mm(hp, w_pg, name="ple_gate")
    pp = _mm(p_tok, w_pp, name="ple_proj")

    def head(x3t, pgt, ppt, tt):
        sg = _sigmoid(pgt)
        err = x3t + sg * ppt - tt
        dx4 = err * (1.0 / D_MODEL)
        loss = 0.5 * jnp.sum(jnp.mean(err * err, axis=-1, keepdims=True), axis=0, keepdims=True)
        return dx4, dx4 * ppt * sg * (1.0 - sg), dx4 * sg, jnp.broadcast_to(loss, (8, 128))

    head_rows = [(x3, 0, D_MODEL), (pg, 0, D_MODEL), (pp, 0, D_MODEL), (tgt, 0, D_MODEL)]
    dx4, dpg, dpp, loss_tile = _rows_call(head, head_rows, [], [(D_MODEL, F32), (D_MODEL, BF16), (D_MODEL, BF16)], [(8, 128)],
                                          tm=512, name="ple_loss")

    c_arr = jnp.reshape(ci, (1,)).astype(jnp.int32)
    mc_arr = jnp.stack([me, ci]).astype(jnp.int32)
    red = {g: _GroupReduce(g, grp, c_arr, mc_arr) for g, grp in REDUCE_GROUPS.items()}

    done = {}

    def update(summed):
        for n, g2d in summed.items():
            if n == "lora":
                w_, m_, v_ = (_lora_stack({k: t[k][0] for k, _ in LORA}) for t in (wts, mom_m, mom_v))
            else:
                w_, m_, v_ = laid(wts, n), laid(mom_m, n), laid(mom_v, n)
            done[n] = _adamw(w_, g2d, m_, v_, name=f"adamw_{n}", with_grad=True)
    gw, gs = {}, {}
    gw["ple_w_proj"] = _mm(p_tok, dpp, ta=True, name="ple_proj_dw")
    gw["ple_w_gate"] = _mm(hp, dpg, ta=True, name="ple_gate_dw")
    dx3, dx3_bf, gs["ple_norm"] = _mm(dpg, w_pg, tb=True, post=_norm_bwd_post(x3, vec["ple_norm"], dx4), name="ple_gate_dx")
    dx2, dx2_bf, gs["ffn2_norm"], gw["ffn2_w_gate"], gw["ffn2_w_up"], gw["ffn2_w_down"] = _ffn_bwd(
        dx3, dx3_bf, ffn2_saved, vec["ffn2_norm"], wb["ffn2_w_gate"], wb["ffn2_w_up"], wb["ffn2_w_down"], "ffn2")
    gw["ple_w_gate"] = gw["ple_w_gate"].reshape(N_CHIPS, D_MODEL // N_CHIPS, D_MODEL)
    tok = red["f2"].swap_start([gw[w[0]] for w in REDUCE_GROUPS["f2"]])
    gw["w_out"] = _mm(merged, dx2_bf, ta=True, name="out_proj_dw")
    dmerged = _mm(dx2_bf, w_o, tb=True, dep=tok, name="out_proj_dx")

    def merge_bwd(zgr, zga, brt, bat, ct):
        _, vjp = jax.vjp(_merge, zgr, zga, brt, bat)
        d1, d2, d3, d4 = vjp(ct)
        return jnp.concatenate([d1, d2], axis=1), d3, d4

    dz_g, dbr, dba = _rows_call(merge_bwd, merge_rows + [(dmerged, 0, D_MODEL)], [],
                                [(2 * D_MODEL, BF16), (D_MODEL, BF16), (D_MODEL, BF16)], tm=512, name="merge_bwd")
    tok = red["f2"].swap_wait_ici_start(dz_g)
    gw["w_br_rwkv"] = _mm(y_rwkv, dbr, ta=True, name="branch_rwkv_dw")
    gw["w_br_attn"] = _mm(y_attn, dba, ta=True, name="branch_attn_dw")
    dy_rwkv = _mm(dbr, w_brr, tb=True, dep=tok, name="branch_rwkv_dx")
    dy_attn = _mm(dba, w_bra, tb=True, dep=tok, name="branch_attn_dx")

    def comb_bwd(*t):
        _, vjp = jax.vjp(_attn_combine, *t[:6])
        return vjp(t[6])

    dcomb = _rows_call(comb_bwd, comb_rows + [(dy_attn, 0, GROUP_DIM)], [], [(GROUP_DIM, F32)] * 6, tm=512, name="attn_combine_bwd")
    dqs, dks, dvs = zip(*[_attn_bwd(*qkv[g], d, dcomb[g], dcomb[3 + g]) for g, d in enumerate(ATTN_DILATIONS)])

    def qk_bwd(qt, kt, ct, st, *rest):
        dq = jnp.concatenate(rest[0:3], axis=1)
        dk = jnp.concatenate(rest[3:6], axis=1)
        qg, kg = rest[9], rest[10]
        _, vjp = jax.vjp(lambda a_, b_, c_, d_: qk_fwd(a_, b_, ct, st, c_, d_), qt, kt, qg, kg)
        dqt, dkt, dqg, dkg = vjp((dq, dk))
        return jnp.concatenate((dqt, dkt) + tuple(rest[6:9]), axis=1), dqg, dkg

    dz_a, gs["q_norm"], gs["k_norm"] = _rows_call(
        qk_bwd, qk_rows + [(t, 0, GROUP_DIM) for t in dqs + dks + dvs], [vec["q_norm"], vec["k_norm"]],
        [(ATTN_COLS, BF16)], [(1, HEAD), (1, HEAD)], tm=512, name="attn_pre_bwd")
    tok = red["f2"].ici_wait_join_start(dz_a)

    def post_bwd(*t):
        _, vjp = jax.vjp(_rwkv_post, *t[:5], *t[6:])
        return vjp(t[5])

    dy_scan, dr_post, dk2_post, dv_post, dgate_r, gs["rwkv_gn_w"], gs["rwkv_gn_b"], gs["rwkv_r_k"] = _rows_call(
        post_bwd, post_rows + [(dy_rwkv, 0, RWKV_DIM)], post_params, [(RWKV_DIM, F32)] * 5, [(1, RWKV_DIM)] * 3,
        tm=512, name="rwkv_post_bwd", dep=tok)
    update(red["f2"].join_wait(dy_scan))
    dr_s, dlw, dk2_s, dv_s, dna, dkb = _wkv_bwd(zs, lw, k2, na, kb, s0s, invs, dy_scan)

    def pre_bwd(zt, c_r1, c_r2, c_lw, c_k1, c_k2, c_v1, c_v2, c_a, c_b, c_g, *params):
        _, vjp = jax.vjp(_rwkv_pre, zt, *params)
        return vjp((c_r1 + c_r2, c_lw, c_k1 + c_k2, c_v1 + c_v2, c_a, c_b, c_g))

    pre_cts = [dr_s, dr_post, dlw, dk2_s, dk2_post, dv_s, dv_post, dna, dkb, dgate_r]
    dzs, gs["rwkv_w0"], g_w2, gs["rwkv_a0"], g_a2, g_g2, gs["rwkv_k_k"], gs["rwkv_k_a"] = _rows_call(
        pre_bwd, [(zs, 0, RWKV_COLS)] + [(t, 0, RWKV_DIM) for t in pre_cts], pre_params, [(RWKV_COLS, F32)],
        [q.shape for q in pre_params], tm=512, name="rwkv_pre_bwd")
    dz_r, gs["rwkv_mu"] = _shift_bwd(z_r, vec["rwkv_mu"], dzs)

    g_w_in = jnp.concatenate([_mm(h, dz_r, ta=True, name="in_rwkv_dw"), _mm(h, dz_a, ta=True, name="in_attn_dw"),
                              _mm(h, dz_g, ta=True, name="in_gate_dw")], axis=1)
    gw["w_in"], gw["lora"] = g_w_in, jnp.concatenate([g_w2, g_a2, g_g2], axis=0)
    gw["w_out"] = gw["w_out"].reshape(N_CHIPS, D_MODEL // N_CHIPS, D_MODEL)
    tok = red["mx"].swap_start([gw[w[0]] for w in REDUCE_GROUPS["mx"]])
    dh = _mm(dz_r, w_in_r, tb=True, dep=tok, name="in_rwkv_dx")
    dh = _mm(dz_a, w_in_a, tb=True, res=dh, name="in_attn_dx")
    dx1, dx1_bf, gs["mix_norm"] = _mm(dz_g, w_in_g, tb=True, res=dh, post=_norm_bwd_post(x1, vec["mix_norm"], dx2), name="in_gate_dx")
    tok_mx = red["mx"].swap_wait_ici_start(dx1_bf)
    hooks = {"down": lambda d_wd: red["f1d"].swap_start([d_wd], after=tok_mx),
             "mid": lambda dgate: red["f1d"].swap_wait_ici_start(dgate),
             "dw": lambda d_wgt, d_wut: red["f1g"].swap_start([d_wgt, d_wut]),
             "dx": lambda part: red["f1g"].swap_wait_ici_start(part) + red["f1d"].ici_wait_join_start(part),
             "end": lambda dx_: tokens.setdefault("mx_join", red["mx"].ici_wait_join_start(dx_))}
    tokens = {}
    dx0, _, gs["ffn1_norm"], gw["ffn1_w_gate"], gw["ffn1_w_up"], gw["ffn1_w_down"] = _ffn_bwd(
        dx1, dx1_bf, ffn1_saved, vec["ffn1_norm"], wb["ffn1_w_gate"], wb["ffn1_w_up"], wb["ffn1_w_down"], "ffn1", hooks=hooks)

    flat = jnp.concatenate([gs[n].reshape(-1) for n, _ in SMALL] + [loss_tile[0, 0:1]])
    small_buf = jnp.pad(flat, (0, SMALL_ROWS * PACK_COLS - flat.shape[0])).reshape(SMALL_ROWS, PACK_COLS)
    small_sum = _all_reduce_small(small_buf)
    n_small = sum(sz for _, sz in SMALL)
    loss = small_sum.reshape(-1)[n_small]
    grad_small = _unpack_small(small_sum, {n: wts[n].shape for n, _ in SMALL})
    d_s, m_s, v_s = _adamw(_pack_small(wts), small_sum, _pack_small(mom_m), _pack_small(mom_v), name="adamw_small",
                           dep=tokens["mx_join"])
    shapes = {n: wts[n].shape for n, _ in SMALL}
    d_s, m_s, v_s = _unpack_small(d_s, shapes), _unpack_small(m_s, shapes), _unpack_small(v_s, shapes)
    grads, deltas, new_m, new_v = {}, {}, {}, {}
    for n, _ in SMALL:
        grads[n], deltas[n], new_m[n], new_v[n] = grad_small[n], d_s[n], m_s[n], v_s[n]

    for g in ("mx", "f1d"):
        update(red[g].join_wait(m_s["ffn1_norm"]))
    tok = red["f1g"].ici_wait_join_start(done["w_in"][1])
    update(red["f1g"].join_wait(tok))
    for n, res in done.items():
        for store, val in zip((grads, deltas, new_m, new_v), res):
            if n == "lora":
                store.update({k: t[None] for k, t in _lora_split(val).items()})
            else:
                store[n] = (jnp.transpose(val) if n in TRANSPOSED else val)[None]

    return (loss, dx0[None], *[grads[n] for n in WEIGHTS], *[deltas[n] for n in WEIGHTS],
            *[new_m[n] for n in WEIGHTS], *[new_v[n] for n in WEIGHTS])
```

```python
import functools

import jax
import jax.numpy as jnp
from jax import lax
from jax.experimental import pallas as pl
from jax.experimental.pallas import tpu as pltpu

F32, BF16 = jnp.float32, jnp.bfloat16
HI = lax.Precision.HIGHEST
MESH = pl.DeviceIdType.MESH
SDS = jax.ShapeDtypeStruct

D_MODEL = 1024
HEAD = 64
RWKV_HEADS = 8
RWKV_DIM = RWKV_HEADS * HEAD
DECAY_LORA, ICLR_LORA, GATE_LORA = 64, 64, 128
GN_EPS = 64e-5
RMS_EPS = 1e-6
ATTN_DILATIONS = (1, 4, 16)
BAND = 128
ATTN_DIM = 768
GROUP_DIM = 256
ATTN_CLASSES_PER_STEP = 4
ROPE_THETA = 10000.0
NEG_INF = -1e30
RWKV_COLS = 3 * RWKV_DIM + DECAY_LORA + ICLR_LORA + GATE_LORA
ATTN_COLS = 3 * ATTN_DIM
ADAM_LR, ADAM_B1, ADAM_B2, ADAM_EPS, ADAM_WD, ADAM_STEP = 0.001, 0.9, 0.999, 1e-08, 0.01, 10

WKV_CHUNK = 64
WKV_HEADS_PER_STEP = 8
WKV_CHUNKS_PER_STEP = 4
N_CHIPS = 4
PACK_COLS = 1024
VMEM_LIMIT = 48 * 1024 * 1024

TRANSPOSED = ("ffn1_w_gate", "ffn1_w_up", "ffn2_w_gate", "ffn2_w_up")
LORA = (("rwkv_w2", 64), ("rwkv_a2", 64), ("rwkv_g2", 128))
_FFN1 = (("ffn1_w_gate", "blk", 704, 1024), ("ffn1_w_up", "blk", 704, 1024), ("ffn1_w_down", "blk", 704, 1024))
_FFN2 = (("ffn2_w_gate", "blk", 704, 1024), ("ffn2_w_up", "blk", 704, 1024), ("ffn2_w_down", "blk", 704, 1024))
_IN = (("w_in", "col", 1024, 1536), ("lora", "col", 256, 128))
_BRANCH = (("w_br_rwkv", "col", 512, 256), ("w_br_attn", "col", 256, 256), ("w_out", "blk", 256, 1024))
_PLE = (("ple_w_gate", "blk", 256, 1024), ("ple_w_proj", "col", 256, 256))
GROUPS = {"f1": _FFN1, "mx": _IN, "f2": _BRANCH + _FFN2 + _PLE}
REDUCE_GROUPS = {"f2": _FFN2 + _PLE, "mx": _IN + _BRANCH, "f1d": _FFN1[2:], "f1g": _FFN1[:2]}
SMALL = (
    ("ffn1_norm", 1024), ("mix_norm", 1024), ("ffn2_norm", 1024), ("ple_norm", 1024), ("rwkv_mu", 1792),
    ("rwkv_w0", 512), ("rwkv_a0", 512), ("rwkv_k_k", 512), ("rwkv_k_a", 512), ("rwkv_r_k", 512),
    ("rwkv_gn_w", 512), ("rwkv_gn_b", 512), ("q_norm", 64), ("k_norm", 64),
)
SMALL_ROWS = 16
WEIGHTS = (
    "ffn1_norm", "ffn1_w_gate", "ffn1_w_up", "ffn1_w_down", "mix_norm", "w_in", "rwkv_mu", "rwkv_w0", "rwkv_w2",
    "rwkv_a0", "rwkv_a2", "rwkv_g2", "rwkv_k_k", "rwkv_k_a", "rwkv_r_k", "rwkv_gn_w", "rwkv_gn_b", "q_norm", "k_norm",
    "w_br_rwkv", "w_br_attn", "w_out", "ffn2_norm", "ffn2_w_gate", "ffn2_w_up", "ffn2_w_down", "ple_norm",
    "ple_w_gate", "ple_w_proj",
)


def _row_tile(n, most=704):
    for t in range(most - most % 16, 0, -16):
        if n % t == 0:
            return t
    return n


def _pick(n, cands):
    for c in cands:
        if n % c == 0:
            return c
    return n


def _mm(a, b, *, ta=False, tb=False, sum_blocks=False, out_dtype=F32, res=None, alpha=1.0, dep=None, post=None, name):
    flat = a.ndim == 2 and b.ndim == 2
    a3 = a if a.ndim == 3 else a[None]
    b3 = b if b.ndim == 3 else b[None]
    na, nbb = a3.shape[0], b3.shape[0]
    nblk = max(na, nbb)
    kdim, m = (a3.shape[1], a3.shape[2]) if ta else (a3.shape[2], a3.shape[1])
    n = b3.shape[1] if tb else b3.shape[2]
    assert (b3.shape[2] if tb else b3.shape[1]) == kdim
    tm = _pick(m, (1024, 512, 256, 128) if post is None else (512, 256, 128))
    tn = _pick(n, (1024, 896, 768, 512, 256, 128))
    tk = kdim if kdim <= 2304 else _pick(kdim, (1024, 512, 256, 128))
    nk = kdim // tk
    direct = nk == 1 and not sum_blocks

    if sum_blocks:
        grid = (m // tm, n // tn, nblk, nk)

        def ids(i, c, j, k):
            return i, c, j, k
    else:
        grid = (nblk, m // tm, n // tn, nk)

        def ids(j, i, c, k):
            return i, c, j, k

    def amap(*g):
        i, c, j, k = ids(*g)
        jj = j if na > 1 else 0
        return (jj, k, i) if ta else (jj, i, k)

    def bmap(*g):
        i, c, j, k = ids(*g)
        jj = j if nbb > 1 else 0
        return (jj, c, k) if tb else (jj, k, c)

    if sum_blocks:
        oshape, oblk = (m, n), (tm, tn)

        def omap(*g):
            i, c, j, k = ids(*g)
            return i, c
    else:
        oshape, oblk = (nblk, m, n), (1, tm, tn)

        def omap(*g):
            i, c, j, k = ids(*g)
            return j, i, c

    dn = (((0 if ta else 1,), (1 if tb else 0,)), ((), ()))
    has_res = res is not None
    p_f, p_rows, p_params, p_dtypes, p_accs = post if post is not None else (None, [], [], [], [])
    assert post is None or sum_blocks or flat
    n_in = 2 + has_res + len(p_rows) + len(p_params) + (dep is not None)

    def tile_map(*g):
        i, c, j, k = ids(*g)
        return i, c

    def body(*refs):
        refs = list(refs)
        acc = None if direct else refs.pop()
        o_refs = refs[n_in:]
        a_ref, b_ref = refs[0], refs[1]
        r_ref = refs[2] if has_res else None
        pr_refs = refs[2 + has_res:2 + has_res + len(p_rows)]
        pp_refs = refs[2 + has_res + len(p_rows):2 + has_res + len(p_rows) + len(p_params)]
        first_tile = jnp.logical_and(pl.program_id(0 if sum_blocks else 1) == 0, pl.program_id(1 if sum_blocks else 2) == 0)

        def finish(v):
            if alpha != 1.0:
                v = v * alpha
            if has_res:
                v = v + r_ref[...].reshape(v.shape).astype(F32)
            if post is None:
                o_refs[0][...] = v.reshape(o_refs[0].shape).astype(o_refs[0].dtype)
                return
            outs = p_f(v, *[t[...] for t in pr_refs], *[t[...] for t in pp_refs])
            for o_ref, val in zip(o_refs, outs[:len(p_dtypes)]):
                o_ref[...] = val.astype(o_ref.dtype)
            for o_ref, val in zip(o_refs[len(p_dtypes):], outs[len(p_dtypes):]):
                @pl.when(first_tile)
                def _():
                    o_ref[...] = jnp.zeros_like(o_ref)

                o_ref[...] += val.reshape(o_ref.shape)

        if direct:
            finish(lax.dot_general(a_ref[0].astype(BF16), b_ref[0].astype(BF16), dn, preferred_element_type=F32))
            return
        k = pl.program_id(3)
        if sum_blocks:
            j = pl.program_id(2)
            first = jnp.logical_and(j == 0, k == 0)
            last = jnp.logical_and(j == nblk - 1, k == nk - 1)
        else:
            first, last = k == 0, k == nk - 1

        @pl.when(first)
        def _():
            acc[...] = jnp.zeros_like(acc)

        acc[...] += lax.dot_general(a_ref[0].astype(BF16), b_ref[0].astype(BF16), dn, preferred_element_type=F32)

        @pl.when(last)
        def _():
            finish(acc[...])

    in_specs = [pl.BlockSpec((1, tk, tm) if ta else (1, tm, tk), amap), pl.BlockSpec((1, tn, tk) if tb else (1, tk, tn), bmap)]
    args = [a3, b3]
    if has_res:
        res3 = res if (sum_blocks or res.ndim == 3) else res[None]
        in_specs.append(pl.BlockSpec(oblk, omap))
        args.append(res3)
    in_specs += [pl.BlockSpec((tm, tn), tile_map) for _ in p_rows]
    in_specs += [pl.BlockSpec(t.shape, functools.partial(lambda *g, nd: (0,) * nd, nd=t.ndim)) for t in p_params]
    args += list(p_rows) + list(p_params)
    if dep is not None:
        in_specs.append(pl.BlockSpec(memory_space=pl.ANY))
        args.append(dep)
    if post is None:
        out_specs, out_shape = pl.BlockSpec(oblk, omap), SDS(oshape, out_dtype)
        semantics = ("parallel", "parallel", "arbitrary", "arbitrary") if sum_blocks else ("parallel", "parallel", "parallel", "arbitrary")
    else:
        out_specs = [pl.BlockSpec((tm, tn), tile_map) for _ in p_dtypes]
        out_specs += [pl.BlockSpec(tuple(sh), functools.partial(lambda *g, nd: (0,) * nd, nd=len(sh))) for sh in p_accs]
        out_shape = [SDS((m, n), dt) for dt in p_dtypes] + [SDS(tuple(sh), F32) for sh in p_accs]
        semantics = ("arbitrary",) * 4
    out = pl.pallas_call(
        body,
        name=name,
        grid=grid,
        in_specs=in_specs,
        out_specs=out_specs,
        out_shape=out_shape,
        scratch_shapes=[] if direct else [pltpu.VMEM((tm, tn), F32)],
        compiler_params=pltpu.CompilerParams(dimension_semantics=semantics, vmem_limit_bytes=VMEM_LIMIT),
    )(*args)
    if post is not None:
        return out
    if flat and not sum_blocks:
        out = out[0]
    return out


def _rows_call(f, rows, params, outs, accs=(), *, tm, name, dep=None):
    s = rows[0][0].shape[0]
    nr, npar, no = len(rows), len(params), len(outs)
    nin = nr + npar + (0 if dep is None else 1)
    in_specs = [pl.BlockSpec((tm, w), functools.partial(lambda i, cb: (i, cb), cb=cb)) for (_, cb, w) in rows]
    in_specs += [pl.BlockSpec(p.shape, functools.partial(lambda i, nd: (0,) * nd, nd=p.ndim)) for p in params]
    if dep is not None:
        in_specs.append(pl.BlockSpec(memory_space=pl.ANY))
    out_shape = [SDS((s, w), dt) for (w, dt) in outs] + [SDS(tuple(sh), F32) for sh in accs]
    out_specs = [pl.BlockSpec((tm, w), lambda i: (i, 0)) for (w, _) in outs]
    out_specs += [pl.BlockSpec(tuple(sh), functools.partial(lambda i, nd: (0,) * nd, nd=len(sh))) for sh in accs]

    def body(*refs):
        rin, pin = refs[:nr], refs[nr:nr + npar]
        oo, ao = refs[nin:nin + no], refs[nin + no:]
        res = f(*[r[...] for r in rin], *[p[...] for p in pin])
        if not isinstance(res, (tuple, list)):
            res = (res,)
        for o_ref, v in zip(oo, res[:no]):
            o_ref[...] = v.astype(o_ref.dtype)
        i = pl.program_id(0)
        for a_ref, v in zip(ao, res[no:]):
            @pl.when(i == 0)
            def _():
                a_ref[...] = jnp.zeros_like(a_ref)

            a_ref[...] += v.reshape(a_ref.shape)

    res = pl.pallas_call(
        body,
        name=name,
        grid=(s // tm,),
        in_specs=in_specs,
        out_specs=out_specs,
        out_shape=out_shape,
        compiler_params=pltpu.CompilerParams(dimension_semantics=("arbitrary",), vmem_limit_bytes=VMEM_LIMIT),
    )(*[r[0] for r in rows], *params, *([] if dep is None else [dep]))
    return res


def _mmv(a, b, mode):
    ca = 0 if mode[0] == "t" else 1
    cb = 1 if mode[1] == "t" else 0
    return lax.dot_general(a.astype(BF16), b.astype(BF16), (((ca,), (cb,)), ((), ())), preferred_element_type=F32)


@functools.partial(jax.custom_vjp, nondiff_argnums=(2,))
def _bdot(a, b, mode):
    return _mmv(a, b, mode)


def _bdot_fwd(a, b, mode):
    return _mmv(a, b, mode), (a, b)


def _bdot_bwd(mode, saved, g):
    a, b = saved
    if mode == "nn":
        return _mmv(g, b, "nt"), _mmv(a, g, "tn")
    if mode == "nt":
        return _mmv(g, b, "nn"), _mmv(g, a, "tn")
    return _mmv(b, g, "nt"), _mmv(a, g, "nn")


_bdot.defvjp(_bdot_fwd, _bdot_bwd)


def _hdot(a, b, mode="nn", precision=HI):
    ca = 0 if mode[0] == "t" else 1
    cb = 1 if mode[1] == "t" else 0
    return lax.dot_general(a, b, (((ca,), (cb,)), ((), ())), precision=precision, preferred_element_type=F32)


def _segsum(x):
    c = x.shape[-1]
    blk = min(c, 256)
    r = lax.broadcasted_iota(jnp.int32, (blk, blk), 0) >> 6
    q = lax.broadcasted_iota(jnp.int32, (blk, blk), 1) >> 6
    ones = jnp.where(r == q, 1.0, 0.0).astype(F32)
    parts = [_hdot(x[:, i:i + blk], ones, precision=lax.Precision.HIGH) for i in range(0, c, blk)]
    return parts[0] if len(parts) == 1 else jnp.concatenate(parts, axis=1)


def _sigmoid(x):
    return jax.nn.sigmoid(x)


def _softplus(x):
    return jnp.maximum(x, 0.0) + jnp.log(1.0 + jnp.exp(-jnp.abs(x)))


def _rms(x, gain):
    return x * lax.rsqrt(jnp.mean(x * x, axis=-1, keepdims=True) + RMS_EPS) * gain


def _swiglu_act(gate, up):
    return gate * _sigmoid(gate) * up


def _rwkv_pre(zs, w0, w2, a0, a2, g2, k_k, k_a):
    r, k, v = zs[:, 0:512], zs[:, 512:1024], zs[:, 1024:1536]
    lora = zs[:, 1536:1792]
    wd, ad, gd = lora[:, 0:64], lora[:, 64:128], lora[:, 128:256]
    w = -_softplus(-(w0 + _bdot(jnp.tanh(wd), w2, "nn"))) - 0.5
    a = _sigmoid(a0 + _bdot(ad, a2, "nn"))
    g = _bdot(_sigmoid(gd), g2, "nn")
    kk = k * k_k
    kk = kk * lax.rsqrt(jnp.maximum(_segsum(kk * kk), 1e-24))
    k2 = k * (1.0 + (a - 1.0) * k_a)
    return r, -jnp.exp(w), k2, v, -kk, kk * a, g


def _rwkv_post(y, r, k2, v, g, gn_w, gn_b, r_k):
    mean = _segsum(y) * (1.0 / HEAD)
    yc = y - mean
    var = _segsum(yc * yc) * (1.0 / HEAD)
    yn = yc * lax.rsqrt(var + GN_EPS) * gn_w + gn_b
    bonus = _segsum(r * k2 * r_k) * v
    return (yn + bonus) * g


def _swap_halves(x):
    lane = lax.broadcasted_iota(jnp.int32, x.shape, 1)
    return jnp.where((lane & 32) == 0, jnp.roll(x, -32, axis=1), jnp.roll(x, 32, axis=1))


def _norm_rope(x, gain, cos, sin):
    heads = x.shape[1] // HEAD
    def rep(t):
        return jnp.concatenate([t] * heads, axis=1)

    xn = x * lax.rsqrt(_segsum(x * x) * (1.0 / HEAD) + RMS_EPS) * rep(gain)
    return xn * rep(cos) + _swap_halves(xn) * rep(sin)


def _attn_combine(o0, o1, o2, l0, l1, l2):
    m = jnp.maximum(jnp.maximum(l0, l1), l2)
    e0, e1, e2 = jnp.exp(l0 - m), jnp.exp(l1 - m), jnp.exp(l2 - m)
    return (e0 * o0 + e1 * o1 + e2 * o2) / (e0 + e1 + e2)


def _merge(zgr, zga, br, ba):
    return _sigmoid(zgr) * br + _sigmoid(zga) * ba


def _attn_block(q, kp, kc, vp, vc, has_prev):
    iq = lax.broadcasted_iota(jnp.int32, (1, BAND, BAND), 1)
    ik = lax.broadcasted_iota(jnp.int32, (1, BAND, BAND), 2)
    s_c = jnp.where(iq >= ik, _bdotb(q, kc, "nt") * (HEAD ** -0.5), NEG_INF)
    s_p = jnp.where(jnp.logical_and(iq <= ik, has_prev), _bdotb(q, kp, "nt") * (HEAD ** -0.5), NEG_INF)
    m = lax.stop_gradient(jnp.maximum(jnp.max(s_c, axis=-1, keepdims=True), jnp.max(s_p, axis=-1, keepdims=True)))
    e_c, e_p = jnp.exp(s_c - m), jnp.exp(s_p - m)
    l = jnp.sum(e_c, axis=-1, keepdims=True) + jnp.sum(e_p, axis=-1, keepdims=True)
    o = (_bdotb(e_c, vc) + _bdotb(e_p, vp)) / l
    return o, jnp.broadcast_to(m + jnp.log(l), o.shape)


def _mmb(a, b, cb):
    return lax.dot_general(a.astype(BF16), b.astype(BF16), (((2,), (cb,)), ((0,), (0,))), preferred_element_type=F32)


@functools.partial(jax.custom_vjp, nondiff_argnums=(2,))
def _bdotb1(a, b, cb):
    return _mmb(a, b, cb)


def _bdotb1_fwd(a, b, cb):
    return _mmb(a, b, cb), (a, b)


def _bdotb1_bwd(cb, saved, g):
    a, b = saved
    if cb == 1:
        return _mmb(g, b, 2), _mmb(jnp.swapaxes(a, 1, 2), g, 1)
    return _mmb(g, b, 1), _mmb(jnp.swapaxes(g, 1, 2), a, 1)


_bdotb1.defvjp(_bdotb1_fwd, _bdotb1_bwd)


def _bdotb(a, b, mode="nn", precision=None):
    if mode[0] == "t":
        a = jnp.swapaxes(a, 1, 2)
    cb = 2 if mode[1] == "t" else 1
    if precision is None:
        return _bdotb1(a, b, cb)
    return lax.dot_general(a, b, (((2,), (cb,)), ((0,), (0,))), precision=precision, preferred_element_type=F32)


def _tri_inv_levels(a):
    t = a.shape[-1]
    row = lax.broadcasted_iota(jnp.int32, (1, t, t), 1)
    col = lax.broadcasted_iota(jnp.int32, (1, t, t), 2)
    x = jnp.where(row == col, 1.0, 0.0).astype(F32) + jnp.where(jnp.logical_and(row == col + 1, (row & 1) == 1), a, 0.0)
    sh = 1
    while (1 << sh) < t:
        m = jnp.logical_and((row >> sh) == (col >> sh) + 1, (row >> (sh + 1)) == (col >> (sh + 1)))
        x = x + _bdotb(_bdotb(x, jnp.where(m, a, 0.0)), x)
        sh += 1
    return x


@jax.custom_vjp
def _tri_inv(a):
    return _tri_inv_levels(a)


def _tri_inv_fwd(a):
    x = _tri_inv_levels(a)
    return x, x


def _tri_inv_bwd(x, g):
    xt = jnp.swapaxes(x, 1, 2)
    return (_bdotb(_bdotb(xt, g, precision=lax.Precision.HIGH), xt, precision=lax.Precision.HIGH),)


_tri_inv.defvjp(_tri_inv_fwd, _tri_inv_bwd)


@jax.custom_vjp
def _known_inv(a, x):
    return x


def _known_inv_fwd(a, x):
    return x, x


def _known_inv_bwd(x, g):
    return _tri_inv_bwd(x, g)[0], jnp.zeros_like(x)


_known_inv.defvjp(_known_inv_fwd, _known_inv_bwd)


def _wkv_chunk(s0, r, lw, k, v, a, b, inv=None, with_inv=False):
    nh = r.shape[0]
    t = WKV_CHUNK
    n = r.shape[1] // t

    def chunked(x):
        return x if n == 1 else jnp.concatenate([x[:, c * t:(c + 1) * t] for c in range(n)], axis=0)

    r, lw, k, v, a, b = (chunked(x) for x in (r, lw, k, v, a, b))
    row = lax.broadcasted_iota(jnp.int32, (1, t, t), 1)
    col = lax.broadcasted_iota(jnp.int32, (1, t, t), 2)
    incl, strict = row >= col, row > col
    ones = jnp.broadcast_to(jnp.where(incl, 1.0, 0.0).astype(F32), (n * nh, t, t))
    cum = _bdotb(ones, lw, precision=HI)
    c_end = cum[:, t - 1:t, :]
    e_in, e_ex, e_inv = jnp.exp(cum), jnp.exp(cum - lw), jnp.exp(-cum)
    at, rt, bt, kt = a * e_ex, r * e_in, b * e_inv, k * e_inv
    a_ab = jnp.where(strict, _bdotb(at, bt, "nt"), 0.0)
    a_ak = jnp.where(strict, _bdotb(at, kt, "nt"), 0.0)
    x = _tri_inv(a_ab) if inv is None else _known_inv(a_ab, inv)
    r_b = jnp.where(incl, _bdotb(rt, bt, "nt"), 0.0)
    akv = _bdotb(a_ak, v)
    rkv = _bdotb(jnp.where(incl, _bdotb(rt, kt, "nt"), 0.0), v)
    w_end = jnp.exp(c_end - cum)
    bw, kw, decay = b * w_end, k * w_end, jnp.exp(c_end)
    ys = []
    for c in range(n):
        hs = slice(c * nh, (c + 1) * nh)
        u = _bdotb(x[hs], _bdotb(at[hs], s0, "nt") + akv[hs])
        ys.append(_bdotb(rt[hs], s0, "nt") + _bdotb(r_b[hs], u) + rkv[hs])
        s0 = s0 * decay[hs] + _bdotb(u, bw[hs], "tn") + _bdotb(v[hs], kw[hs], "tn")
    y = ys[0] if n == 1 else jnp.concatenate(ys, axis=1)
    return (y, s0, x) if with_inv else (y, s0)


def _shift_fwd(z, mu):
    s, c = z.shape
    tc = 256

    def body(z_ref, mu_ref, o_ref):
        zz = z_ref[...]
        row = lax.broadcasted_iota(jnp.int32, zz.shape, 0)
        prev = jnp.where(row == 0, 0.0, pltpu.roll(zz, 1, 0))
        o_ref[...] = zz + (prev - zz) * mu_ref[...]

    return pl.pallas_call(
        body, name="shift_fwd", grid=(c // tc,),
        in_specs=[pl.BlockSpec((s, tc), lambda j: (0, j)), pl.BlockSpec((1, tc), lambda j: (0, j))],
        out_specs=pl.BlockSpec((s, tc), lambda j: (0, j)), out_shape=SDS((s, c), F32),
        compiler_params=pltpu.CompilerParams(dimension_semantics=("parallel",), vmem_limit_bytes=VMEM_LIMIT),
    )(z, mu)


def _shift_bwd(z, mu, dzs):
    s, c = z.shape
    tc = 256

    def body(z_ref, mu_ref, d_ref, dz_ref, dmu_ref):
        zz, d, m = z_ref[...], d_ref[...], mu_ref[...]
        row = lax.broadcasted_iota(jnp.int32, zz.shape, 0)
        prev = jnp.where(row == 0, 0.0, pltpu.roll(zz, 1, 0))
        t = d * m
        nxt = jnp.where(row == s - 1, 0.0, pltpu.roll(t, s - 1, 0))
        dz_ref[...] = (d - t + nxt).astype(dz_ref.dtype)
        dmu_ref[...] = jnp.sum(d * (prev - zz), axis=0, keepdims=True)

    return pl.pallas_call(
        body, name="shift_bwd", grid=(c // tc,),
        in_specs=[pl.BlockSpec((s, tc), lambda j: (0, j)), pl.BlockSpec((1, tc), lambda j: (0, j)),
                  pl.BlockSpec((s, tc), lambda j: (0, j))],
        out_specs=[pl.BlockSpec((s, tc), lambda j: (0, j)), pl.BlockSpec((1, tc), lambda j: (0, j))],
        out_shape=[SDS((s, c), BF16), SDS((1, c), F32)],
        compiler_params=pltpu.CompilerParams(dimension_semantics=("parallel",), vmem_limit_bytes=VMEM_LIMIT),
    )(z, mu, dzs)


def _heads(x, nh):
    return jnp.stack([x[:, h * HEAD:(h + 1) * HEAD] for h in range(nh)], axis=0)


def _unheads(x):
    return jnp.concatenate([x[h] for h in range(x.shape[0])], axis=1)


def _wkv_fwd(zs, lw, k2, na, b):
    s = lw.shape[0]
    t, hb = WKV_CHUNK * WKV_CHUNKS_PER_STEP, WKV_HEADS_PER_STEP
    w = hb * HEAD
    nc, ng = s // t, RWKV_HEADS // hb
    nx = WKV_CHUNKS_PER_STEP * RWKV_HEADS

    def body(r_ref, v_ref, lw_ref, k_ref, a_ref, b_ref, y_ref, s0_ref, x_ref, state):
        @pl.when(pl.program_id(1) == 0)
        def _():
            state[...] = jnp.zeros_like(state)

        s0 = state[...]
        s0_ref[0] = s0
        y, s1, x = _wkv_chunk(s0, *[_heads(t_ref[...], hb) for t_ref in (r_ref, lw_ref, k_ref, v_ref, a_ref, b_ref)], with_inv=True)
        y_ref[...] = _unheads(y)
        x_ref[0] = x
        state[...] = s1

    def col(off):
        return pl.BlockSpec((t, w), functools.partial(lambda g, i, off: (i, g + off), off=off))

    return pl.pallas_call(
        body, name="wkv_fwd", grid=(ng, nc),
        in_specs=[col(0), col(2 * ng), col(0), col(0), col(0), col(0)],
        out_specs=[col(0), pl.BlockSpec((1, hb, HEAD, HEAD), lambda g, i: (i, g, 0, 0)),
                   pl.BlockSpec((1, nx, WKV_CHUNK, WKV_CHUNK), lambda g, i: (i, 0, 0, 0))],
        out_shape=[SDS((s, RWKV_DIM), F32), SDS((nc, RWKV_HEADS, HEAD, HEAD), F32), SDS((nc, nx, WKV_CHUNK, WKV_CHUNK), F32)],
        scratch_shapes=[pltpu.VMEM((hb, HEAD, HEAD), F32)],
        compiler_params=pltpu.CompilerParams(dimension_semantics=("parallel", "arbitrary"), vmem_limit_bytes=VMEM_LIMIT),
    )(zs, zs, lw, k2, na, b)


def _wkv_bwd(zs, lw, k2, na, b, s0s, invs, dy):
    s = lw.shape[0]
    t, hb = WKV_CHUNK * WKV_CHUNKS_PER_STEP, WKV_HEADS_PER_STEP
    w = hb * HEAD
    nc, ng = s // t, RWKV_HEADS // hb
    nx = WKV_CHUNKS_PER_STEP * RWKV_HEADS

    def body(r_ref, v_ref, lw_ref, k_ref, a_ref, b_ref, s0_ref, x_ref, dy_ref, dr_ref, dlw_ref, dk_ref, dv_ref, da_ref, db_ref, dstate):
        @pl.when(pl.program_id(1) == 0)
        def _():
            dstate[...] = jnp.zeros_like(dstate)

        _, vjp = jax.vjp(functools.partial(_wkv_chunk, inv=x_ref[0]), s0_ref[0],
                         *[_heads(t_ref[...], hb) for t_ref in (r_ref, lw_ref, k_ref, v_ref, a_ref, b_ref)])
        grads = vjp((_heads(dy_ref[...], hb), dstate[...]))
        dstate[...] = grads[0]
        for o_ref, gval in zip((dr_ref, dlw_ref, dk_ref, dv_ref, da_ref, db_ref), grads[1:]):
            o_ref[...] = _unheads(gval)

    def col(off):
        return pl.BlockSpec((t, w), functools.partial(lambda g, i, off: (nc - 1 - i, g + off), off=off))

    return pl.pallas_call(
        body, name="wkv_bwd", grid=(ng, nc),
        in_specs=[col(0), col(2 * ng), col(0), col(0), col(0), col(0),
                  pl.BlockSpec((1, hb, HEAD, HEAD), lambda g, i: (nc - 1 - i, g, 0, 0)),
                  pl.BlockSpec((1, nx, WKV_CHUNK, WKV_CHUNK), lambda g, i: (nc - 1 - i, 0, 0, 0)), col(0)],
        out_specs=[col(0)] * 6,
        out_shape=[SDS((s, RWKV_DIM), F32)] * 6,
        scratch_shapes=[pltpu.VMEM((hb, HEAD, HEAD), F32)],
        compiler_params=pltpu.CompilerParams(dimension_semantics=("parallel", "arbitrary"), vmem_limit_bytes=VMEM_LIMIT),
    )(zs, zs, lw, k2, na, b, s0s, invs, dy)


def _attn_batch(refs, bps, nh, first_has_prev):
    q_ref, kp_ref, kc_ref, vp_ref, vc_ref = refs

    def blocks(cur_ref, prev_ref=None):
        out = []
        for b in range(bps):
            if prev_ref is None:
                t = cur_ref[b * BAND:(b + 1) * BAND, :]
            else:
                t = prev_ref[...] if b == 0 else cur_ref[(b - 1) * BAND:b * BAND, :]
            out.append(_heads(t.astype(F32), nh))
        return out[0] if bps == 1 else jnp.concatenate(out, axis=0)

    batch = lax.broadcasted_iota(jnp.int32, (bps * nh, 1, 1), 0)
    has_prev = jnp.logical_or(batch >= nh, first_has_prev)
    return (blocks(q_ref), blocks(kc_ref, kp_ref), blocks(kc_ref), blocks(vc_ref, vp_ref), blocks(vc_ref)), has_prev


def _attn_rows(x, bps, nh):
    parts = [_unheads(x[b * nh:(b + 1) * nh]) for b in range(bps)]
    return parts[0] if bps == 1 else jnp.concatenate(parts, axis=0)


def _attn_fwd(q, k, v, d):
    s = q.shape[0]
    l = s // d
    nb = l // BAND
    assert nb * BAND == l
    qv, kv, vv = (t.reshape(l, d * GROUP_DIM) for t in (q, k, v))
    width = min(d, ATTN_CLASSES_PER_STEP) * GROUP_DIM
    bps = ATTN_CLASSES_PER_STEP * GROUP_DIM // width
    nh = width // HEAD

    def body(q_ref, kp_ref, kc_ref, vp_ref, vc_ref, o_ref, l_ref):
        ops, has_prev = _attn_batch((q_ref, kp_ref, kc_ref, vp_ref, vc_ref), bps, nh, pl.program_id(1) > 0)
        o, lse = _attn_block(*ops, has_prev)
        o_ref[...] = _attn_rows(o, bps, nh)
        l_ref[...] = _attn_rows(lse, bps, nh)

    cur = pl.BlockSpec((bps * BAND, width), lambda rho, i: (i, rho))
    prev = pl.BlockSpec((BAND, width), lambda rho, i: (jnp.maximum(i * bps - 1, 0), rho))
    o, lse = pl.pallas_call(
        body, name=f"attn_fwd_d{d}", grid=(d * GROUP_DIM // width, nb // bps),
        in_specs=[cur, prev, cur, prev, cur], out_specs=[cur, cur],
        out_shape=[SDS((l, d * GROUP_DIM), F32), SDS((l, d * GROUP_DIM), F32)],
        compiler_params=pltpu.CompilerParams(dimension_semantics=("parallel", "arbitrary"), vmem_limit_bytes=VMEM_LIMIT),
    )(qv, kv, kv, vv, vv)
    return o.reshape(s, GROUP_DIM), lse.reshape(s, GROUP_DIM)


def _attn_bwd(q, k, v, d, do, dlse):
    s = q.shape[0]
    l = s // d
    nb = l // BAND
    qv, kv, vv, dov, dlv = (t.reshape(l, d * GROUP_DIM) for t in (q, k, v, do, dlse))
    width = min(d, ATTN_CLASSES_PER_STEP) * GROUP_DIM
    bps = ATTN_CLASSES_PER_STEP * GROUP_DIM // width
    nh = width // HEAD
    ns = nb // bps

    def body(q_ref, kp_ref, kc_ref, vp_ref, vc_ref, do_ref, dl_ref, dq_ref, dk_ref, dv_ref, ck, cv):
        step = pl.program_id(1)

        @pl.when(step == 0)
        def _():
            ck[...] = jnp.zeros_like(ck)
            cv[...] = jnp.zeros_like(cv)

        ops, has_prev = _attn_batch((q_ref, kp_ref, kc_ref, vp_ref, vc_ref), bps, nh, step < ns - 1)
        _, vjp = jax.vjp(functools.partial(_attn_block, has_prev=has_prev), *ops)
        cts = [jnp.concatenate([_heads(t_ref[b * BAND:(b + 1) * BAND, :], nh) for b in range(bps)], axis=0) if bps > 1
               else _heads(t_ref[...], nh) for t_ref in (do_ref, dl_ref)]
        dq, dkp, dkc, dvp, dvc = vjp(tuple(cts))
        dq_ref[...] = _attn_rows(dq, bps, nh)
        for out_ref, cur_part, prev_part, carry in ((dk_ref, dkc, dkp, ck), (dv_ref, dvc, dvp, cv)):
            for b in range(bps):
                after = carry[...] if b == bps - 1 else _unheads(prev_part[(b + 1) * nh:(b + 2) * nh])
                out_ref[b * BAND:(b + 1) * BAND, :] = _unheads(cur_part[b * nh:(b + 1) * nh]) + after
            carry[...] = _unheads(prev_part[0:nh])

    cur = pl.BlockSpec((bps * BAND, width), lambda rho, i: (ns - 1 - i, rho))
    prev = pl.BlockSpec((BAND, width), lambda rho, i: (jnp.maximum((ns - 1 - i) * bps - 1, 0), rho))
    dq, dk, dv = pl.pallas_call(
        body, name=f"attn_bwd_d{d}", grid=(d * GROUP_DIM // width, ns),
        in_specs=[cur, prev, cur, prev, cur, cur, cur], out_specs=[cur] * 3,
        out_shape=[SDS((l, d * GROUP_DIM), F32)] * 3,
        scratch_shapes=[pltpu.VMEM((BAND, width), F32), pltpu.VMEM((BAND, width), F32)],
        compiler_params=pltpu.CompilerParams(dimension_semantics=("parallel", "arbitrary"), vmem_limit_bytes=VMEM_LIMIT),
    )(qv, kv, kv, vv, vv, dov, dlv)
    return dq.reshape(s, GROUP_DIM), dk.reshape(s, GROUP_DIM), dv.reshape(s, GROUP_DIM)


def _coords():
    return lax.axis_index("x"), lax.axis_index("y"), lax.axis_index("c")


_CHIP_FLIPS = ((1, 0), (0, 1), (1, 1))


def _flip(v, f):
    return 1 - v if f else v


def _form(kind, r, c):
    return (N_CHIPS, r, c) if kind == "blk" else (r, N_CHIPS * c)


def _slot(ref, kind, j, rows, c):
    if kind == "blk":
        return ref.at[j] if rows is None else ref.at[j, rows]
    cols = pl.ds(pl.multiple_of(j * c, 128), c)
    return ref.at[:, cols] if rows is None else ref.at[rows, cols]


def _half(r, which, align):
    return pl.ds(pl.multiple_of(which * (r // 2), align), r // 2)


def _rcopy(src, dst, send_sems, recv_sems, kk, dev):
    return pltpu.make_async_remote_copy(src_ref=src, dst_ref=dst, send_sem=send_sems.at[kk], recv_sem=recv_sems.at[kk],
                                        device_id=dev, device_id_type=MESH)


def _gather_plan(specs, step):
    def copies(refs, ss, rs, received):
        x, y, c = _coords()
        out = []
        for w, (kind, r, cc) in enumerate(specs):
            mine, other = _half(r, c, 16), _half(r, 1 - c, 16)
            for kk, (fx, fy) in enumerate(_CHIP_FLIPS):
                px, py = _flip(x, fx), _flip(y, fy)
                if step == "ici":
                    sl = _slot(refs[w], kind, 2 * px + py if received else 2 * x + y, mine, cc)
                    dev = (px, py, c)
                else:
                    sl = _slot(refs[w], kind, 2 * px + py, other if received else mine, cc)
                    dev = (x, y, 1 - c)
                out.append(_rcopy(sl, sl, ss, rs, 3 * w + kk, dev))
        return out

    def issue(refs, ss, rs):
        return copies(refs, ss, rs, False)

    def expect(refs, ss, rs):
        return copies(refs, ss, rs, False), copies(refs, ss, rs, True)

    return issue, expect


_HBM = pl.BlockSpec(memory_space=pltpu.HBM)
_SEM = pl.BlockSpec(memory_space=pltpu.SEMAPHORE)
_EFFECT = pltpu.SideEffectType.DATAFLOW_SIDE_EFFECTING


def _copies_start(name, bufs, n_sems, issue, after=None):
    nb = len(bufs)
    extra = [] if after is None else [after]

    def body(*refs):
        send_sems, recv_sems = refs[nb + len(extra)], refs[nb + len(extra) + 1]
        for cp in issue(refs[:nb], send_sems, recv_sems):
            cp.start()
        refs[-1][...] = jnp.zeros_like(refs[-1])

    outs = pl.pallas_call(
        body, name=name,
        out_shape=(pltpu.SemaphoreType.DMA((n_sems,)), pltpu.SemaphoreType.DMA((n_sems,)),
                   *[pltpu.HBM(b.shape, b.dtype) for b in bufs], SDS((8, 128), F32)),
        in_specs=[_HBM] * nb + [pl.BlockSpec(memory_space=pl.ANY)] * len(extra),
        out_specs=(_SEM, _SEM, *[_HBM] * nb, pl.BlockSpec(memory_space=pltpu.VMEM)),
        input_output_aliases={i: 2 + i for i in range(nb)},
        compiler_params=pltpu.CompilerParams(has_side_effects=_EFFECT),
    )(*[pltpu.with_memory_space_constraint(b, pltpu.HBM) for b in bufs], *extra)
    return outs[0], outs[1], list(outs[2:2 + nb]), outs[-1]


def _copies_wait(name, bufs, send_sems, recv_sems, after, expect):
    nb = len(bufs)

    def body(*refs):
        sent, received = expect(refs[:nb], refs[nb], refs[nb + 1])
        for cp in sent:
            cp.wait_send()
        for cp in received:
            cp.wait_recv()

    outs = pl.pallas_call(
        body, name=name,
        out_shape=tuple(pltpu.HBM(b.shape, b.dtype) for b in bufs),
        in_specs=(*[_HBM] * nb, _SEM, _SEM, pl.BlockSpec(memory_space=pl.ANY)), out_specs=tuple([_HBM] * nb),
        input_output_aliases={i: i for i in range(nb)},
        compiler_params=pltpu.CompilerParams(has_side_effects=_EFFECT),
    )(*bufs, send_sems, recv_sems, after)
    return list(outs)


def _add_pair(g, recv, kind, r, c, c_arr, name):
    h = r // 2
    if kind == "blk":
        tr = _row_tile(h, 512)
        grid = (N_CHIPS, h // tr)
        g_spec = pl.BlockSpec((1, 1, tr, c), lambda j, i, c_ref: (j, c_ref[0], i, 0))
        o_spec = pl.BlockSpec((1, tr, c), lambda j, i, c_ref: (j, i, 0))
        gv, oshape = g.reshape(N_CHIPS, 2, h, c), (N_CHIPS, h, c)
    else:
        tr = _row_tile(h, 64)
        grid = (h // tr,)
        g_spec = pl.BlockSpec((1, tr, N_CHIPS * c), lambda i, c_ref: (c_ref[0], i, 0))
        o_spec = pl.BlockSpec((tr, N_CHIPS * c), lambda i, c_ref: (i, 0))
        gv, oshape = g.reshape(2, h, N_CHIPS * c), (h, N_CHIPS * c)

    def body(c_ref, g_ref, r_ref, o_ref, ob_ref):
        v = (g_ref[:, 0] if kind == "blk" else g_ref[0]) + r_ref[...]
        o_ref[...] = v
        ob_ref[...] = v.astype(BF16)

    return pl.pallas_call(
        body, name=name,
        grid_spec=pltpu.PrefetchScalarGridSpec(num_scalar_prefetch=1, grid=grid, in_specs=[g_spec, o_spec], out_specs=[o_spec] * 2),
        out_shape=[SDS(oshape, F32), SDS(oshape, BF16)],
        compiler_params=pltpu.CompilerParams(vmem_limit_bytes=VMEM_LIMIT),
    )(c_arr, gv, recv)


def _sum_chips(pair, recv, kind, r, c, mc_arr, name):
    h = r // 2
    tr = _row_tile(h, 512)
    nt = h // tr
    if kind == "blk":
        p_spec = pl.BlockSpec((1, tr, c), lambda i, mc: (mc[0], i, 0))
    else:
        p_spec = pl.BlockSpec((tr, c), lambda i, mc: (i, mc[0]))

    def body(mc, a_ref, r_ref, g_out):
        own = a_ref[0] if kind == "blk" else a_ref[...]
        g_out[...] = ((own + r_ref[0].astype(F32)) + r_ref[1].astype(F32)) + r_ref[2].astype(F32)

    return pl.pallas_call(
        body, name=name,
        grid_spec=pltpu.PrefetchScalarGridSpec(
            num_scalar_prefetch=1, grid=(nt,), in_specs=[p_spec, pl.BlockSpec((3, tr, c), lambda i, mc: (0, i, 0))],
            out_specs=pl.BlockSpec((tr, c), lambda i, mc: (mc[1] * nt + i, 0))),
        out_shape=SDS((r, c), F32),
        compiler_params=pltpu.CompilerParams(vmem_limit_bytes=VMEM_LIMIT),
    )(mc_arr, pair, recv)


class _GroupReduce:
    def __init__(self, tag, specs, c_arr, mc_arr):
        self.tag, self.specs, self.c_arr, self.mc_arr = tag, specs, c_arr, mc_arr
        self.n = len(specs)

    def _plan(self, step):
        specs, n = self.specs, self.n

        def copies(refs, ss, rs, received):
            x, y, c = _coords()
            sib, out = (x, y, 1 - c), []
            for w, (_, kind, r, cc) in enumerate(specs):
                if step == "join":
                    there = refs[w].at[_half(r, 1 - c if received else c, 8)]
                    out.append(_rcopy(there, there, ss, rs, w, sib))
                    continue
                src, land = refs[w], refs[n + w]
                if step == "swap":
                    rows = _half(r, 1 - c, 8)
                    part = src.at[:, rows] if kind == "blk" else src.at[rows]
                    out.append(_rcopy(land if received else part, land, ss, rs, w, sib))
                else:
                    for kk, (fx, fy) in enumerate(_CHIP_FLIPS):
                        px, py = _flip(x, fx), _flip(y, fy)
                        part = land.at[kk] if received else _slot(src, kind, 2 * px + py, None, cc)
                        out.append(_rcopy(part, land.at[kk], ss, rs, 3 * w + kk, (px, py, c)))
            return out

        def issue(refs, ss, rs):
            return copies(refs, ss, rs, False)

        def expect(refs, ss, rs):
            return copies(refs, ss, rs, False), copies(refs, ss, rs, True)

        return issue, expect

    def swap_start(self, grads, after=None):
        lands = [lax.empty(_form(kind, r // 2, c), F32) for _, kind, r, c in self.specs]
        ss, rs, bufs, tok = _copies_start(f"rs_{self.tag}_swap", list(grads) + lands, self.n, self._plan("swap")[0], after=after)
        self.state = (ss, rs, bufs)
        return tok

    def swap_wait_ici_start(self, after):
        ss, rs, bufs = self.state
        bufs = _copies_wait(f"rs_{self.tag}_swap_wait", bufs, ss, rs, after, self._plan("swap")[1])
        pairs = [_add_pair(bufs[w], bufs[self.n + w], kind, r, c, self.c_arr, name=f"rs_{self.tag}_pair_{nm}")
                 for w, (nm, kind, r, c) in enumerate(self.specs)]
        self.pair = [pr[0] for pr in pairs]
        lands = [lax.empty((3, r // 2, c), BF16) for _, _, r, c in self.specs]
        ss, rs, bufs, tok = _copies_start(f"rs_{self.tag}_ici", [pr[1] for pr in pairs] + lands, 3 * self.n, self._plan("ici")[0])
        self.state = (ss, rs, bufs)
        return tok

    def ici_wait_join_start(self, after):
        ss, rs, bufs = self.state
        bufs = _copies_wait(f"rs_{self.tag}_ici_wait", bufs, ss, rs, after, self._plan("ici")[1])
        outs = [_sum_chips(self.pair[w], bufs[self.n + w], kind, r, c, self.mc_arr, name=f"rs_{self.tag}_sum_{nm}")
                for w, (nm, kind, r, c) in enumerate(self.specs)]
        ss, rs, bufs, tok = _copies_start(f"rs_{self.tag}_join", outs, self.n, self._plan("join")[0])
        self.state = (ss, rs, bufs)
        return tok

    def join_wait(self, after):
        ss, rs, bufs = self.state
        bufs = _copies_wait(f"rs_{self.tag}_join_wait", bufs, ss, rs, after, self._plan("join")[1])
        return {nm: bufs[w] for w, (nm, _, _, _) in enumerate(self.specs)}


def _all_reduce_small(buf):
    rows, cols = buf.shape

    def body(x_ref, o_ref, gath, send_sems, recv_sems):
        x, y, c = _coords()
        me = 4 * x + 2 * y + c
        gath[me] = x_ref[...]
        sends = []
        for kk in range(1, 8):
            f = (kk >> 2) & 1, (kk >> 1) & 1, kk & 1
            px, py, pc = _flip(x, f[0]), _flip(y, f[1]), _flip(c, f[2])
            cp = pltpu.make_async_remote_copy(src_ref=x_ref, dst_ref=gath.at[me], send_sem=send_sems.at[kk - 1],
                                              recv_sem=recv_sems.at[kk - 1], device_id=(px, py, pc), device_id_type=MESH)
            cp.start()
            sends.append(cp)
        for kk in range(1, 8):
            f = (kk >> 2) & 1, (kk >> 1) & 1, kk & 1
            px, py, pc = _flip(x, f[0]), _flip(y, f[1]), _flip(c, f[2])
            there = gath.at[4 * px + 2 * py + pc]
            pltpu.make_async_remote_copy(src_ref=there, dst_ref=there, send_sem=send_sems.at[kk - 1],
                                         recv_sem=recv_sems.at[kk - 1], device_id=(px, py, pc), device_id_type=MESH).wait_recv()
        for cp in sends:
            cp.wait_send()
        acc = gath[0]
        for j in range(1, 8):
            acc = acc + gath[j]
        o_ref[...] = acc

    return pl.pallas_call(
        body, name="all_reduce_small",
        in_specs=[pl.BlockSpec(memory_space=pltpu.VMEM)], out_specs=pl.BlockSpec(memory_space=pltpu.VMEM),
        out_shape=SDS((rows, cols), F32),
        scratch_shapes=[pltpu.VMEM((8, rows, cols), F32), pltpu.SemaphoreType.DMA((7,)), pltpu.SemaphoreType.DMA((7,))],
    )(buf)


def _adamw_rows(w, g, m, v):
    m = ADAM_B1 * m + (1.0 - ADAM_B1) * g
    v = ADAM_B2 * v + (1.0 - ADAM_B2) * jnp.square(g)
    m_hat = m / (1.0 - ADAM_B1 ** ADAM_STEP)
    v_hat = v / (1.0 - ADAM_B2 ** ADAM_STEP)
    return -ADAM_LR * (m_hat / (jnp.sqrt(v_hat) + ADAM_EPS) + ADAM_WD * w), m, v


def _adamw(w, g, m, v, name, dep=None, with_grad=False):
    rows, cols = w.shape
    tm = _pick(rows, (256, 128, 64, 16, 8))
    f = (lambda wt, gt, mt, vt: (gt,) + _adamw_rows(wt, gt, mt, vt)) if with_grad else _adamw_rows
    return _rows_call(f, [(t, 0, cols) for t in (w, g, m, v)], [], [(cols, F32)] * (3 + with_grad), tm=tm, name=name, dep=dep)


def _pack_small(parts):
    flat = jnp.concatenate([parts[n].reshape(-1) for n, _ in SMALL])
    return jnp.pad(flat, (0, SMALL_ROWS * PACK_COLS - flat.shape[0])).reshape(SMALL_ROWS, PACK_COLS)


def _unpack_small(buf, shapes):
    flat, out, off = buf.reshape(-1), {}, 0
    for n, sz in SMALL:
        out[n] = flat[off:off + sz].reshape(shapes[n])
        off += sz
    return out


def _lora_stack(parts):
    return jnp.concatenate([parts[n] for n, _ in LORA], axis=-2)


def _lora_split(stacked):
    out, off = {}, 0
    for n, rows in LORA:
        out[n] = stacked[..., off:off + rows, :]
        off += rows
    return out


def _ffn_gate_up(h, wgt, wut, name, dep=None):
    s, d = h.shape
    nblk, f, _ = wgt.shape
    tm = _pick(s, (1024, 512, 256))
    dn = (((1,), (1,)), ((), ()))

    def body(h_ref, wg_ref, wu_ref, *rest):
        g_ref, u_ref, a_ref = rest[-3:]
        hh = h_ref[...]
        g = lax.dot_general(hh, wg_ref[0], dn, preferred_element_type=F32)
        u = lax.dot_general(hh, wu_ref[0], dn, preferred_element_type=F32)
        g_ref[0], u_ref[0] = g.astype(BF16), u.astype(BF16)
        a_ref[0] = _swiglu_act(g, u).astype(BF16)

    w_spec = pl.BlockSpec((1, f, d), lambda j, i: (j, 0, 0))
    o_spec = pl.BlockSpec((1, tm, f), lambda j, i: (j, i, 0))
    extra = [] if dep is None else [dep]
    return pl.pallas_call(
        body, name=name, grid=(nblk, s // tm),
        in_specs=[pl.BlockSpec((tm, d), lambda j, i: (i, 0)), w_spec, w_spec] + [pl.BlockSpec(memory_space=pl.ANY)] * len(extra),
        out_specs=[o_spec] * 3,
        out_shape=[SDS((nblk, s, f), BF16)] * 3,
        compiler_params=pltpu.CompilerParams(dimension_semantics=("parallel", "parallel"), vmem_limit_bytes=VMEM_LIMIT),
    )(h, wgt, wut, *extra)


def _ffn_down_dx(dx_bf, wd, gate, up, name, dep=None):
    s, d = dx_bf.shape
    nblk, f, _ = wd.shape
    tm = _pick(s, (1024, 512, 256))
    dn = (((1,), (1,)), ((), ()))

    def body(dx_ref, wd_ref, g_ref, u_ref, *rest):
        dg_ref, du_ref = rest[-2:]
        dact = 0.5 * lax.dot_general(dx_ref[...], wd_ref[0], dn, preferred_element_type=F32)
        _, vjp = jax.vjp(_swiglu_act, g_ref[0].astype(F32), u_ref[0].astype(F32))
        dg, du = vjp(dact)
        dg_ref[0], du_ref[0] = dg.astype(BF16), du.astype(BF16)

    o_spec = pl.BlockSpec((1, tm, f), lambda j, i: (j, i, 0))
    extra = [] if dep is None else [dep]
    return pl.pallas_call(
        body, name=name, grid=(nblk, s // tm),
        in_specs=[pl.BlockSpec((tm, d), lambda j, i: (i, 0)), pl.BlockSpec((1, f, d), lambda j, i: (j, 0, 0)), o_spec, o_spec]
        + [pl.BlockSpec(memory_space=pl.ANY)] * len(extra),
        out_specs=[o_spec] * 2, out_shape=[SDS((nblk, s, f), BF16)] * 2,
        compiler_params=pltpu.CompilerParams(dimension_semantics=("parallel", "parallel"), vmem_limit_bytes=VMEM_LIMIT),
    )(dx_bf, wd, gate, up, *extra)


def _ffn_fwd(x, gain, wgt, wut, wd, tag, h=None, dep=None):
    if h is None:
        h = _rows_call(_rms, [(x, 0, D_MODEL)], [gain], [(D_MODEL, BF16)], tm=512, name=f"{tag}_norm")[0]
    gate, up, act = _ffn_gate_up(h, wgt, wut, f"{tag}_gate_up", dep=dep)
    x_new = _mm(act, wd, sum_blocks=True, res=x, alpha=0.5, name=f"{tag}_down")
    return x_new, (x, h, gate, up, act)


def _ffn_bwd(dx_new, dx_new_bf, saved, gain, wgt, wut, wd, tag, dep=None, hooks=None):
    x, h, gate, up, act = saved
    hooks = hooks or {}

    def hook(name, *vals):
        return hooks[name](*vals) if name in hooks else None

    d_wd = _mm(act, dx_new_bf, ta=True, alpha=0.5, name=f"{tag}_down_dw")
    dep = hook("down", d_wd) if "down" in hooks else dep
    dgate, dup = _ffn_down_dx(dx_new_bf, wd, gate, up, f"{tag}_down_dx", dep=dep)
    d_wgt = _mm(dgate, h, ta=True, dep=hook("mid", dgate), name=f"{tag}_gate_dw")
    d_wut = _mm(dup, h, ta=True, name=f"{tag}_up_dw")
    dh = _mm(dgate, wgt, sum_blocks=True, dep=hook("dw", d_wgt, d_wut), name=f"{tag}_gate_dx")
    dx, dx_bf, dgain = _mm(dup, wut, sum_blocks=True, res=dh, dep=hook("dx", dh), post=_norm_bwd_post(x, gain, dx_new),
                           name=f"{tag}_up_dx")
    hook("end", dx_bf)
    return dx, dx_bf, dgain, d_wgt, d_wut, d_wd


def _norm_bwd_post(x, gain, dres):
    def f(dht, xt, drt, gt):
        _, vjp = jax.vjp(_rms, xt, gt)
        dxt, dgt = vjp(dht)
        return dxt + drt, dxt + drt, dgt

    return f, [x, dres], [gain], [F32, BF16], [(1, D_MODEL)]


def kernel(x, p, positions, ffn1_norm, ffn1_w_gate, ffn1_w_up, ffn1_w_down, mix_norm, w_in, rwkv_mu, rwkv_w0, rwkv_w2, rwkv_a0, rwkv_a2, rwkv_g2, rwkv_k_k, rwkv_k_a, rwkv_r_k, rwkv_gn_w, rwkv_gn_b, q_norm, k_norm, w_br_rwkv, w_br_attn, w_out, ffn2_norm, ffn2_w_gate, ffn2_w_up, ffn2_w_down, ple_norm, ple_w_gate, ple_w_proj, loss_target, m_ffn1_norm, m_ffn1_w_gate, m_ffn1_w_up, m_ffn1_w_down, m_mix_norm, m_w_in, m_rwkv_mu, m_rwkv_w0, m_rwkv_w2, m_rwkv_a0, m_rwkv_a2, m_rwkv_g2, m_rwkv_k_k, m_rwkv_k_a, m_rwkv_r_k, m_rwkv_gn_w, m_rwkv_gn_b, m_q_norm, m_k_norm, m_w_br_rwkv, m_w_br_attn, m_w_out, m_ffn2_norm, m_ffn2_w_gate, m_ffn2_w_up, m_ffn2_w_down, m_ple_norm, m_ple_w_gate, m_ple_w_proj, v_ffn1_norm, v_ffn1_w_gate, v_ffn1_w_up, v_ffn1_w_down, v_mix_norm, v_w_in, v_rwkv_mu, v_rwkv_w0, v_rwkv_w2, v_rwkv_a0, v_rwkv_a2, v_rwkv_g2, v_rwkv_k_k, v_rwkv_k_a, v_rwkv_r_k, v_rwkv_gn_w, v_rwkv_gn_b, v_q_norm, v_k_norm, v_w_br_rwkv, v_w_br_attn, v_w_out, v_ffn2_norm, v_ffn2_w_gate, v_ffn2_w_up, v_ffn2_w_down, v_ple_norm, v_ple_w_gate, v_ple_w_proj):
    args = dict(locals())
    wts = {n: args[n] for n in WEIGHTS}
    mom_m = {n: args["m_" + n] for n in WEIGHTS}
    mom_v = {n: args["v_" + n] for n in WEIGHTS}
    x0, tgt = x[0], loss_target[0]
    s = x0.shape[0]
    p_tok = p[0, 0]

    vec = {n: wts[n].reshape(1, -1) for n, _ in SMALL}
    xi, yi, ci = _coords()
    me = 2 * xi + yi
    def laid(t, n):
        return jnp.transpose(t[n][0]) if n in TRANSPOSED else t[n][0]

    shard_of = {n: laid(wts, n) for g in GROUPS.values() for n, _, _, _ in g if n != "lora"}
    shard_of["lora"] = _lora_stack({n: wts[n][0] for n, _ in LORA})

    def whole_with_own(n, kind, r, c, tok=None):
        at = (me, 0, 0) if kind == "blk" else (0, me * c)
        own = (shard_of[n] if tok is None else shard_of[n] + tok[0, 0]).astype(BF16)
        return lax.dynamic_update_slice(lax.empty(_form(kind, r, c), BF16), own[None] if kind == "blk" else own, at)

    specs = {g: [(kind, r, c) for _, kind, r, c in grp] for g, grp in GROUPS.items()}
    plans = {(g, st): _gather_plan(specs[g], st) for g in GROUPS for st in ("ici", "d2d")}
    buf_f1 = [whole_with_own(*w) for w in GROUPS["f1"]]
    ss_0, rs_0, buf_f1, tok_0 = _copies_start("gather_f1_ici", buf_f1, 3 * len(buf_f1), plans["f1", "ici"][0])
    bufs = {g: [whole_with_own(*w, tok=tok_0) for w in GROUPS[g]] for g in ("mx", "f2")}
    buf_f1 = _copies_wait("gather_f1_ici_wait", buf_f1, ss_0, rs_0, bufs["mx"][0], plans["f1", "ici"][1])
    ss_1, rs_1, buf_f1, tok_1 = _copies_start("gather_f1_d2d", buf_f1, 3 * len(buf_f1), plans["f1", "d2d"][0])
    h1 = _rows_call(_rms, [(x0, 0, D_MODEL)], [vec["ffn1_norm"] + tok_1[0, 0]], [(D_MODEL, BF16)], tm=512, name="ffn1_norm")[0]
    buf_f1 = _copies_wait("gather_f1_d2d_wait", buf_f1, ss_1, rs_1, h1, plans["f1", "d2d"][1])
    wb = dict(zip([w[0] for w in GROUPS["f1"]], buf_f1))
    ss_a, rs_a, buf_mx, tok_a = _copies_start("gather_mx_ici", bufs["mx"], 3 * len(bufs["mx"]), plans["mx", "ici"][0],
                                              after=wb["ffn1_w_gate"])

    inv_freq = 1.0 / (ROPE_THETA ** (jnp.arange(0, HEAD, 2, dtype=F32) / HEAD))
    ang = positions[0].astype(F32)[:, None] * inv_freq
    cos, sin = jnp.cos(ang), jnp.sin(ang)
    cos2, sin2 = jnp.concatenate([cos, cos], axis=1), jnp.concatenate([-sin, sin], axis=1)

    x1, ffn1_saved = _ffn_fwd(x0, vec["ffn1_norm"], wb["ffn1_w_gate"], wb["ffn1_w_up"], wb["ffn1_w_down"], "ffn1", h=h1, dep=tok_a)
    buf_mx = _copies_wait("gather_mx_ici_wait", buf_mx, ss_a, rs_a, x1, plans["mx", "ici"][1])
    ss_b, rs_b, buf_mx, tok_b = _copies_start("gather_mx_d2d", buf_mx, 3 * len(buf_mx), plans["mx", "d2d"][0])
    ss_c, rs_c, buf_f2, tok_c = _copies_start("gather_f2_ici", bufs["f2"], 3 * len(bufs["f2"]), plans["f2", "ici"][0])
    h = _rows_call(_rms, [(x1, 0, D_MODEL)], [vec["mix_norm"] + (tok_b[0, 0] + tok_c[0, 0])], [(D_MODEL, BF16)], tm=256,
                   name="mix_norm")[0]
    buf_mx = _copies_wait("gather_mx_d2d_wait", buf_mx, ss_b, rs_b, h, plans["mx", "d2d"][1])
    wb.update(zip([w[0] for w in GROUPS["mx"]], buf_mx))
    w_in_all = wb["w_in"]
    w_in_r, w_in_a, w_in_g = w_in_all[:, :RWKV_COLS], w_in_all[:, RWKV_COLS:RWKV_COLS + ATTN_COLS], w_in_all[:, RWKV_COLS + ATTN_COLS:]
    lora = _lora_split(wb["lora"])
    w2, a2, g2 = lora["rwkv_w2"], lora["rwkv_a2"], lora["rwkv_g2"]
    z_r = _mm(h, w_in_r, name="in_rwkv")
    z_a = _mm(h, w_in_a, name="in_attn")
    z_g = _mm(h, w_in_g, name="in_gate")

    zs = _shift_fwd(z_r, vec["rwkv_mu"])
    pre_params = [vec["rwkv_w0"], w2, vec["rwkv_a0"], a2, g2, vec["rwkv_k_k"], vec["rwkv_k_a"]]
    def pre_fwd(*t):
        res = _rwkv_pre(*t)
        return res[1], res[2], res[4], res[5], res[6]

    lw, k2, na, kb, gate_r = _rows_call(pre_fwd, [(zs, 0, RWKV_COLS)], pre_params, [(RWKV_DIM, F32)] * 5, tm=512, name="rwkv_pre")
    y_scan, s0s, invs = _wkv_fwd(zs, lw, k2, na, kb)
    buf_f2 = _copies_wait("gather_f2_ici_wait", buf_f2, ss_c, rs_c, y_scan, plans["f2", "ici"][1])
    ss_d, rs_d, buf_f2, tok_d = _copies_start("gather_f2_d2d", buf_f2, 3 * len(buf_f2), plans["f2", "d2d"][0])
    post_params = [vec["rwkv_gn_w"] + tok_d[0, 0], vec["rwkv_gn_b"], vec["rwkv_r_k"]]
    post_rows = [(y_scan, 0, RWKV_DIM), (zs, 0, RWKV_DIM), (k2, 0, RWKV_DIM), (zs, 2, RWKV_DIM), (gate_r, 0, RWKV_DIM)]
    y_rwkv = _rows_call(_rwkv_post, post_rows, post_params, [(RWKV_DIM, BF16)], tm=512, name="rwkv_post")[0]
    buf_f2 = _copies_wait("gather_f2_d2d_wait", buf_f2, ss_d, rs_d, y_rwkv, plans["f2", "d2d"][1])
    wb.update(zip([w[0] for w in GROUPS["f2"]], buf_f2))
    w_brr, w_bra = wb["w_br_rwkv"], wb["w_br_attn"]
    w_o = wb["w_out"].reshape(D_MODEL, D_MODEL)
    w_pp, w_pg = wb["ple_w_proj"], wb["ple_w_gate"].reshape(D_MODEL, D_MODEL)

    def qk_fwd(qt, kt, ct, st, qg, kg):
        return _norm_rope(qt, qg, ct, st), _norm_rope(kt, kg, ct, st)

    qk_rows = [(z_a, 0, ATTN_DIM), (z_a, 1, ATTN_DIM), (cos2, 0, HEAD), (sin2, 0, HEAD)]
    q_rot, k_rot = _rows_call(qk_fwd, qk_rows, [vec["q_norm"], vec["k_norm"]], [(ATTN_DIM, BF16)] * 2, tm=512, name="attn_pre")
    def group(t, g, off=0):
        return t[:, off + g * GROUP_DIM:off + (g + 1) * GROUP_DIM].astype(BF16)

    qkv = [(group(q_rot, g), group(k_rot, g), group(z_a, g, 2 * ATTN_DIM)) for g in range(len(ATTN_DILATIONS))]
    outs, lses = zip(*[_attn_fwd(*qkv[g], d) for g, d in enumerate(ATTN_DILATIONS)])
    comb_rows = [(t, 0, GROUP_DIM) for t in outs + lses]
    y_attn = _rows_call(_attn_combine, comb_rows, [], [(GROUP_DIM, BF16)], tm=512, name="attn_combine")[0]

    br = _mm(y_rwkv, w_brr, name="branch_rwkv")
    ba = _mm(y_attn, w_bra, name="branch_attn")
    merge_rows = [(z_g, 0, D_MODEL), (z_g, 1, D_MODEL), (br, 0, D_MODEL), (ba, 0, D_MODEL)]
    merged = _rows_call(_merge, merge_rows, [], [(D_MODEL, BF16)], tm=512, name="merge")[0]
    x2 = _mm(merged, w_o, res=x1, name="out_proj")
    x3, ffn2_saved = _ffn_fwd(x2, vec["ffn2_norm"], wb["ffn2_w_gate"], wb["ffn2_w_up"], wb["ffn2_w_down"], "ffn2")
    hp = _rows_call(_rms, [(x3, 0, D_MODEL)], [vec["ple_norm"]], [(D_MODEL, BF16)], tm=512, name="ple_norm")[0]
    pg = _mm(hp, w_pg, name="ple_gate")
    pp = _mm(p_tok, w_pp, name="ple_proj")

    def head(x3t, pgt, ppt, tt):
        sg = _sigmoid(pgt)
        err = x3t + sg * ppt - tt
        dx4 = err * (1.0 / D_MODEL)
        loss = 0.5 * jnp.sum(jnp.mean(err * err, axis=-1, keepdims=True), axis=0, keepdims=True)
        return dx4, dx4 * ppt * sg * (1.0 - sg), dx4 * sg, jnp.broadcast_to(loss, (8, 128))

    head_rows = [(x3, 0, D_MODEL), (pg, 0, D_MODEL), (pp, 0, D_MODEL), (tgt, 0, D_MODEL)]
    dx4, dpg, dpp, loss_tile = _rows_call(head, head_rows, [], [(D_MODEL, F32), (D_MODEL, BF16), (D_MODEL, BF16)], [(8, 128)],
                                          tm=512, name="ple_loss")

    c_arr = jnp.reshape(ci, (1,)).astype(jnp.int32)
    mc_arr = jnp.stack([me, ci]).astype(jnp.int32)
    red = {g: _GroupReduce(g, grp, c_arr, mc_arr) for g, grp in REDUCE_GROUPS.items()}

    done = {}

    def update(summed):
        for n, g2d in summed.items():
            if n == "lora":
                w_, m_, v_ = (_lora_stack({k: t[k][0] for k, _ in LORA}) for t in (wts, mom_m, mom_v))
            else:
                w_, m_, v_ = laid(wts, n), laid(mom_m, n), laid(mom_v, n)
            done[n] = _adamw(w_, g2d, m_, v_, name=f"adamw_{n}", with_grad=True)
    gw, gs = {}, {}
    gw["ple_w_proj"] = _mm(p_tok, dpp, ta=True, name="ple_proj_dw")
    gw["ple_w_gate"] = _mm(hp, dpg, ta=True, name="ple_gate_dw")
    dx3, dx3_bf, gs["ple_norm"] = _mm(dpg, w_pg, tb=True, post=_norm_bwd_post(x3, vec["ple_norm"], dx4), name="ple_gate_dx")
    dx2, dx2_bf, gs["ffn2_norm"], gw["ffn2_w_gate"], gw["ffn2_w_up"], gw["ffn2_w_down"] = _ffn_bwd(
        dx3, dx3_bf, ffn2_saved, vec["ffn2_norm"], wb["ffn2_w_gate"], wb["ffn2_w_up"], wb["ffn2_w_down"], "ffn2")
    gw["ple_w_gate"] = gw["ple_w_gate"].reshape(N_CHIPS, D_MODEL // N_CHIPS, D_MODEL)
    tok = red["f2"].swap_start([gw[w[0]] for w in REDUCE_GROUPS["f2"]])
    gw["w_out"] = _mm(merged, dx2_bf, ta=True, name="out_proj_dw")
    dmerged = _mm(dx2_bf, w_o, tb=True, dep=tok, name="out_proj_dx")

    def merge_bwd(zgr, zga, brt, bat, ct):
        _, vjp = jax.vjp(_merge, zgr, zga, brt, bat)
        d1, d2, d3, d4 = vjp(ct)
        return jnp.concatenate([d1, d2], axis=1), d3, d4

    dz_g, dbr, dba = _rows_call(merge_bwd, merge_rows + [(dmerged, 0, D_MODEL)], [],
                                [(2 * D_MODEL, BF16), (D_MODEL, BF16), (D_MODEL, BF16)], tm=512, name="merge_bwd")
    tok = red["f2"].swap_wait_ici_start(dz_g)
    gw["w_br_rwkv"] = _mm(y_rwkv, dbr, ta=True, name="branch_rwkv_dw")
    gw["w_br_attn"] = _mm(y_attn, dba, ta=True, name="branch_attn_dw")
    dy_rwkv = _mm(dbr, w_brr, tb=True, dep=tok, name="branch_rwkv_dx")
    dy_attn = _mm(dba, w_bra, tb=True, dep=tok, name="branch_attn_dx")

    def comb_bwd(*t):
        _, vjp = jax.vjp(_attn_combine, *t[:6])
        return vjp(t[6])

    dcomb = _rows_call(comb_bwd, comb_rows + [(dy_attn, 0, GROUP_DIM)], [], [(GROUP_DIM, F32)] * 6, tm=512, name="attn_combine_bwd")
    dqs, dks, dvs = zip(*[_attn_bwd(*qkv[g], d, dcomb[g], dcomb[3 + g]) for g, d in enumerate(ATTN_DILATIONS)])

    def qk_bwd(qt, kt, ct, st, *rest):
        dq = jnp.concatenate(rest[0:3], axis=1)
        dk = jnp.concatenate(rest[3:6], axis=1)
        qg, kg = rest[9], rest[10]
        _, vjp = jax.vjp(lambda a_, b_, c_, d_: qk_fwd(a_, b_, ct, st, c_, d_), qt, kt, qg, kg)
        dqt, dkt, dqg, dkg = vjp((dq, dk))
        return jnp.concatenate((dqt, dkt) + tuple(rest[6:9]), axis=1), dqg, dkg

    dz_a, gs["q_norm"], gs["k_norm"] = _rows_call(
        qk_bwd, qk_rows + [(t, 0, GROUP_DIM) for t in dqs + dks + dvs], [vec["q_norm"], vec["k_norm"]],
        [(ATTN_COLS, BF16)], [(1, HEAD), (1, HEAD)], tm=512, name="attn_pre_bwd")
    tok = red["f2"].ici_wait_join_start(dz_a)

    def post_bwd(*t):
        _, vjp = jax.vjp(_rwkv_post, *t[:5], *t[6:])
        return vjp(t[5])

    dy_scan, dr_post, dk2_post, dv_post, dgate_r, gs["rwkv_gn_w"], gs["rwkv_gn_b"], gs["rwkv_r_k"] = _rows_call(
        post_bwd, post_rows + [(dy_rwkv, 0, RWKV_DIM)], post_params, [(RWKV_DIM, F32)] * 5, [(1, RWKV_DIM)] * 3,
        tm=512, name="rwkv_post_bwd", dep=tok)
    update(red["f2"].join_wait(dy_scan))
    dr_s, dlw, dk2_s, dv_s, dna, dkb = _wkv_bwd(zs, lw, k2, na, kb, s0s, invs, dy_scan)

    def pre_bwd(zt, c_r1, c_r2, c_lw, c_k1, c_k2, c_v1, c_v2, c_a, c_b, c_g, *params):
        _, vjp = jax.vjp(_rwkv_pre, zt, *params)
        return vjp((c_r1 + c_r2, c_lw, c_k1 + c_k2, c_v1 + c_v2, c_a, c_b, c_g))

    pre_cts = [dr_s, dr_post, dlw, dk2_s, dk2_post, dv_s, dv_post, dna, dkb, dgate_r]
    dzs, gs["rwkv_w0"], g_w2, gs["rwkv_a0"], g_a2, g_g2, gs["rwkv_k_k"], gs["rwkv_k_a"] = _rows_call(
        pre_bwd, [(zs, 0, RWKV_COLS)] + [(t, 0, RWKV_DIM) for t in pre_cts], pre_params, [(RWKV_COLS, F32)],
        [q.shape for q in pre_params], tm=512, name="rwkv_pre_bwd")
    dz_r, gs["rwkv_mu"] = _shift_bwd(z_r, vec["rwkv_mu"], dzs)

    g_w_in = jnp.concatenate([_mm(h, dz_r, ta=True, name="in_rwkv_dw"), _mm(h, dz_a, ta=True, name="in_attn_dw"),
                              _mm(h, dz_g, ta=True, name="in_gate_dw")], axis=1)
    gw["w_in"], gw["lora"] = g_w_in, jnp.concatenate([g_w2, g_a2, g_g2], axis=0)
    gw["w_out"] = gw["w_out"].reshape(N_CHIPS, D_MODEL // N_CHIPS, D_MODEL)
    tok = red["mx"].swap_start([gw[w[0]] for w in REDUCE_GROUPS["mx"]])
    dh = _mm(dz_r, w_in_r, tb=True, dep=tok, name="in_rwkv_dx")
    dh = _mm(dz_a, w_in_a, tb=True, res=dh, name="in_attn_dx")
    dx1, dx1_bf, gs["mix_norm"] = _mm(dz_g, w_in_g, tb=True, res=dh, post=_norm_bwd_post(x1, vec["mix_norm"], dx2), name="in_gate_dx")
    tok_mx = red["mx"].swap_wait_ici_start(dx1_bf)
    hooks = {"down": lambda d_wd: red["f1d"].swap_start([d_wd], after=tok_mx),
             "mid": lambda dgate: red["f1d"].swap_wait_ici_start(dgate),
             "dw": lambda d_wgt, d_wut: red["f1g"].swap_start([d_wgt, d_wut]),
             "dx": lambda part: red["f1g"].swap_wait_ici_start(part) + red["f1d"].ici_wait_join_start(part),
             "end": lambda dx_: tokens.setdefault("mx_join", red["mx"].ici_wait_join_start(dx_))}
    tokens = {}
    dx0, _, gs["ffn1_norm"], gw["ffn1_w_gate"], gw["ffn1_w_up"], gw["ffn1_w_down"] = _ffn_bwd(
        dx1, dx1_bf, ffn1_saved, vec["ffn1_norm"], wb["ffn1_w_gate"], wb["ffn1_w_up"], wb["ffn1_w_down"], "ffn1", hooks=hooks)

    flat = jnp.concatenate([gs[n].reshape(-1) for n, _ in SMALL] + [loss_tile[0, 0:1]])
    small_buf = jnp.pad(flat, (0, SMALL_ROWS * PACK_COLS - flat.shape[0])).reshape(SMALL_ROWS, PACK_COLS)
    small_sum = _all_reduce_small(small_buf)
    n_small = sum(sz for _, sz in SMALL)
    loss = small_sum.reshape(-1)[n_small]
    grad_small = _unpack_small(small_sum, {n: wts[n].shape for n, _ in SMALL})
    d_s, m_s, v_s = _adamw(_pack_small(wts), small_sum, _pack_small(mom_m), _pack_small(mom_v), name="adamw_small",
                           dep=tokens["mx_join"])
    shapes = {n: wts[n].shape for n, _ in SMALL}
    d_s, m_s, v_s = _unpack_small(d_s, shapes), _unpack_small(m_s, shapes), _unpack_small(v_s, shapes)
    grads, deltas, new_m, new_v = {}, {}, {}, {}
    for n, _ in SMALL:
        grads[n], deltas[n], new_m[n], new_v[n] = grad_small[n], d_s[n], m_s[n], v_s[n]

    for g in ("mx", "f1d"):
        update(red[g].join_wait(m_s["ffn1_norm"]))
    tok = red["f1g"].ici_wait_join_start(done["w_in"][1])
    update(red["f1g"].join_wait(tok))
    for n, res in done.items():
        for store, val in zip((grads, deltas, new_m, new_v), res):
            if n == "lora":
                store.update({k: t[None] for k, t in _lora_split(val).items()})
            else:
                store[n] = (jnp.transpose(val) if n in TRANSPOSED else val)[None]

    return (loss, dx0[None], *[grads[n] for n in WEIGHTS], *[deltas[n] for n in WEIGHTS],
            *[new_m[n] for n in WEIGHTS], *[new_v[n] for n in WEIGHTS])
```

```python
import functools

import jax
import jax.numpy as jnp
from jax import lax
from jax.experimental import pallas as pl
from jax.experimental.pallas import tpu as pltpu

F32, BF16 = jnp.float32, jnp.bfloat16
HI = lax.Precision.HIGHEST
MESH = pl.DeviceIdType.MESH
SDS = jax.ShapeDtypeStruct

D_MODEL = 1024
HEAD = 64
RWKV_HEADS = 8
RWKV_DIM = RWKV_HEADS * HEAD
DECAY_LORA, ICLR_LORA, GATE_LORA = 64, 64, 128
GN_EPS = 64e-5
RMS_EPS = 1e-6
ATTN_DILATIONS = (1, 4, 16)
BAND = 128
ATTN_DIM = 768
GROUP_DIM = 256
ATTN_CLASSES_PER_STEP = 4
ROPE_THETA = 10000.0
NEG_INF = -1e30
RWKV_COLS = 3 * RWKV_DIM + DECAY_LORA + ICLR_LORA + GATE_LORA
ATTN_COLS = 3 * ATTN_DIM
ADAM_LR, ADAM_B1, ADAM_B2, ADAM_EPS, ADAM_WD, ADAM_STEP = 0.001, 0.9, 0.999, 1e-08, 0.01, 10

WKV_CHUNK = 64
WKV_HEADS_PER_STEP = 8
WKV_CHUNKS_PER_STEP = 4
N_CHIPS = 4
PACK_COLS = 1024
VMEM_LIMIT = 48 * 1024 * 1024

TRANSPOSED = ("ffn1_w_gate", "ffn1_w_up", "ffn2_w_gate", "ffn2_w_up")
LORA = (("rwkv_w2", 64), ("rwkv_a2", 64), ("rwkv_g2", 128))
_FFN1 = (("ffn1_w_gate", "blk", 704, 1024), ("ffn1_w_up", "blk", 704, 1024), ("ffn1_w_down", "blk", 704, 1024))
_FFN2 = (("ffn2_w_gate", "blk", 704, 1024), ("ffn2_w_up", "blk", 704, 1024), ("ffn2_w_down", "blk", 704, 1024))
_IN = (("w_in", "col", 1024, 1536), ("lora", "col", 256, 128))
_BRANCH = (("w_br_rwkv", "col", 512, 256), ("w_br_attn", "col", 256, 256), ("w_out", "blk", 256, 1024))
_PLE = (("ple_w_gate", "blk", 256, 1024), ("ple_w_proj", "col", 256, 256))
GROUPS = {"f1": _FFN1, "mx": _IN, "f2": _BRANCH + _FFN2 + _PLE}
REDUCE_GROUPS = {"f2": _FFN2 + _PLE, "mx": _IN + _BRANCH, "f1d": _FFN1[2:], "f1g": _FFN1[:2]}
SMALL = (
    ("ffn1_norm", 1024), ("mix_norm", 1024), ("ffn2_norm", 1024), ("ple_norm", 1024), ("rwkv_mu", 1792),
    ("rwkv_w0", 512), ("rwkv_a0", 512), ("rwkv_k_k", 512), ("rwkv_k_a", 512), ("rwkv_r_k", 512),
    ("rwkv_gn_w", 512), ("rwkv_gn_b", 512), ("q_norm", 64), ("k_norm", 64),
)
SMALL_ROWS = 16
WEIGHTS = (
    "ffn1_norm", "ffn1_w_gate", "ffn1_w_up", "ffn1_w_down", "mix_norm", "w_in", "rwkv_mu", "rwkv_w0", "rwkv_w2",
    "rwkv_a0", "rwkv_a2", "rwkv_g2", "rwkv_k_k", "rwkv_k_a", "rwkv_r_k", "rwkv_gn_w", "rwkv_gn_b", "q_norm", "k_norm",
    "w_br_rwkv", "w_br_attn", "w_out", "ffn2_norm", "ffn2_w_gate", "ffn2_w_up", "ffn2_w_down", "ple_norm",
    "ple_w_gate", "ple_w_proj",
)


def _row_tile(n, most=704):
    for t in range(most - most % 16, 0, -16):
        if n % t == 0:
            return t
    return n


def _pick(n, cands):
    for c in cands:
        if n % c == 0:
            return c
    return n


def _mm(a, b, *, ta=False, tb=False, sum_blocks=False, out_dtype=F32, res=None, alpha=1.0, dep=None, post=None, name):
    flat = a.ndim == 2 and b.ndim == 2
    a3 = a if a.ndim == 3 else a[None]
    b3 = b if b.ndim == 3 else b[None]
    na, nbb = a3.shape[0], b3.shape[0]
    nblk = max(na, nbb)
    kdim, m = (a3.shape[1], a3.shape[2]) if ta else (a3.shape[2], a3.shape[1])
    n = b3.shape[1] if tb else b3.shape[2]
    assert (b3.shape[2] if tb else b3.shape[1]) == kdim
    tm = _pick(m, (1024, 512, 256, 128) if post is None else (512, 256, 128))
    tn = _pick(n, (1024, 896, 768, 512, 256, 128))
    tk = kdim if kdim <= 2304 else _pick(kdim, (1024, 512, 256, 128))
    nk = kdim // tk
    direct = nk == 1 and not sum_blocks

    if sum_blocks:
        grid = (m // tm, n // tn, nblk, nk)

        def ids(i, c, j, k):
            return i, c, j, k
    else:
        grid = (nblk, m // tm, n // tn, nk)

        def ids(j, i, c, k):
            return i, c, j, k

    def amap(*g):
        i, c, j, k = ids(*g)
        jj = j if na > 1 else 0
        return (jj, k, i) if ta else (jj, i, k)

    def bmap(*g):
        i, c, j, k = ids(*g)
        jj = j if nbb > 1 else 0
        return (jj, c, k) if tb else (jj, k, c)

    if sum_blocks:
        oshape, oblk = (m, n), (tm, tn)

        def omap(*g):
            i, c, j, k = ids(*g)
            return i, c
    else:
        oshape, oblk = (nblk, m, n), (1, tm, tn)

        def omap(*g):
            i, c, j, k = ids(*g)
            return j, i, c

    dn = (((0 if ta else 1,), (1 if tb else 0,)), ((), ()))
    has_res = res is not None
    p_f, p_rows, p_params, p_dtypes, p_accs = post if post is not None else (None, [], [], [], [])
    assert post is None or sum_blocks or flat
    n_in = 2 + has_res + len(p_rows) + len(p_params) + (dep is not None)

    def tile_map(*g):
        i, c, j, k = ids(*g)
        return i, c

    def body(*refs):
        refs = list(refs)
        acc = None if direct else refs.pop()
        o_refs = refs[n_in:]
        a_ref, b_ref = refs[0], refs[1]
        r_ref = refs[2] if has_res else None
        pr_refs = refs[2 + has_res:2 + has_res + len(p_rows)]
        pp_refs = refs[2 + has_res + len(p_rows):2 + has_res + len(p_rows) + len(p_params)]
        first_tile = jnp.logical_and(pl.program_id(0 if sum_blocks else 1) == 0, pl.program_id(1 if sum_blocks else 2) == 0)

        def finish(v):
            if alpha != 1.0:
                v = v * alpha
            if has_res:
                v = v + r_ref[...].reshape(v.shape).astype(F32)
            if post is None:
                o_refs[0][...] = v.reshape(o_refs[0].shape).astype(o_refs[0].dtype)
                return
            outs = p_f(v, *[t[...] for t in pr_refs], *[t[...] for t in pp_refs])
            for o_ref, val in zip(o_refs, outs[:len(p_dtypes)]):
                o_ref[...] = val.astype(o_ref.dtype)
            for o_ref, val in zip(o_refs[len(p_dtypes):], outs[len(p_dtypes):]):
                @pl.when(first_tile)
                def _():
                    o_ref[...] = jnp.zeros_like(o_ref)

                o_ref[...] += val.reshape(o_ref.shape)

        if direct:
            finish(lax.dot_general(a_ref[0].astype(BF16), b_ref[0].astype(BF16), dn, preferred_element_type=F32))
            return
        k = pl.program_id(3)
        if sum_blocks:
            j = pl.program_id(2)
            first = jnp.logical_and(j == 0, k == 0)
            last = jnp.logical_and(j == nblk - 1, k == nk - 1)
        else:
            first, last = k == 0, k == nk - 1

        @pl.when(first)
        def _():
            acc[...] = jnp.zeros_like(acc)

        acc[...] += lax.dot_general(a_ref[0].astype(BF16), b_ref[0].astype(BF16), dn, preferred_element_type=F32)

        @pl.when(last)
        def _():
            finish(acc[...])

    in_specs = [pl.BlockSpec((1, tk, tm) if ta else (1, tm, tk), amap), pl.BlockSpec((1, tn, tk) if tb else (1, tk, tn), bmap)]
    args = [a3, b3]
    if has_res:
        res3 = res if (sum_blocks or res.ndim == 3) else res[None]
        in_specs.append(pl.BlockSpec(oblk, omap))
        args.append(res3)
    in_specs += [pl.BlockSpec((tm, tn), tile_map) for _ in p_rows]
    in_specs += [pl.BlockSpec(t.shape, functools.partial(lambda *g, nd: (0,) * nd, nd=t.ndim)) for t in p_params]
    args += list(p_rows) + list(p_params)
    if dep is not None:
        in_specs.append(pl.BlockSpec(memory_space=pl.ANY))
        args.append(dep)
    if post is None:
        out_specs, out_shape = pl.BlockSpec(oblk, omap), SDS(oshape, out_dtype)
        semantics = ("parallel", "parallel", "arbitrary", "arbitrary") if sum_blocks else ("parallel", "parallel", "parallel", "arbitrary")
    else:
        out_specs = [pl.BlockSpec((tm, tn), tile_map) for _ in p_dtypes]
        out_specs += [pl.BlockSpec(tuple(sh), functools.partial(lambda *g, nd: (0,) * nd, nd=len(sh))) for sh in p_accs]
        out_shape = [SDS((m, n), dt) for dt in p_dtypes] + [SDS(tuple(sh), F32) for sh in p_accs]
        semantics = ("arbitrary",) * 4
    out = pl.pallas_call(
        body,
        name=name,
        grid=grid,
        in_specs=in_specs,
        out_specs=out_specs,
        out_shape=out_shape,
        scratch_shapes=[] if direct else [pltpu.VMEM((tm, tn), F32)],
        compiler_params=pltpu.CompilerParams(dimension_semantics=semantics, vmem_limit_bytes=VMEM_LIMIT),
    )(*args)
    if post is not None:
        return out
    if flat and not sum_blocks:
        out = out[0]
    return out


def _rows_call(f, rows, params, outs, accs=(), *, tm, name, dep=None):
    s = rows[0][0].shape[0]
    nr, npar, no = len(rows), len(params), len(outs)
    nin = nr + npar + (0 if dep is None else 1)
    in_specs = [pl.BlockSpec((tm, w), functools.partial(lambda i, cb: (i, cb), cb=cb)) for (_, cb, w) in rows]
    in_specs += [pl.BlockSpec(p.shape, functools.partial(lambda i, nd: (0,) * nd, nd=p.ndim)) for p in params]
    if dep is not None:
        in_specs.append(pl.BlockSpec(memory_space=pl.ANY))
    out_shape = [SDS((s, w), dt) for (w, dt) in outs] + [SDS(tuple(sh), F32) for sh in accs]
    out_specs = [pl.BlockSpec((tm, w), lambda i: (i, 0)) for (w, _) in outs]
    out_specs += [pl.BlockSpec(tuple(sh), functools.partial(lambda i, nd: (0,) * nd, nd=len(sh))) for sh in accs]

    def body(*refs):
        rin, pin = refs[:nr], refs[nr:nr + npar]
        oo, ao = refs[nin:nin + no], refs[nin + no:]
        res = f(*[r[...] for r in rin], *[p[...] for p in pin])
        if not isinstance(res, (tuple, list)):
            res = (res,)
        for o_ref, v in zip(oo, res[:no]):
            o_ref[...] = v.astype(o_ref.dtype)
        i = pl.program_id(0)
        for a_ref, v in zip(ao, res[no:]):
            @pl.when(i == 0)
            def _():
                a_ref[...] = jnp.zeros_like(a_ref)

            a_ref[...] += v.reshape(a_ref.shape)

    res = pl.pallas_call(
        body,
        name=name,
        grid=(s // tm,),
        in_specs=in_specs,
        out_specs=out_specs,
        out_shape=out_shape,
        compiler_params=pltpu.CompilerParams(dimension_semantics=("arbitrary",), vmem_limit_bytes=VMEM_LIMIT),
    )(*[r[0] for r in rows], *params, *([] if dep is None else [dep]))
    return res


def _mmv(a, b, mode):
    ca = 0 if mode[0] == "t" else 1
    cb = 1 if mode[1] == "t" else 0
    return lax.dot_general(a.astype(BF16), b.astype(BF16), (((ca,), (cb,)), ((), ())), preferred_element_type=F32)


@functools.partial(jax.custom_vjp, nondiff_argnums=(2,))
def _bdot(a, b, mode):
    return _mmv(a, b, mode)


def _bdot_fwd(a, b, mode):
    return _mmv(a, b, mode), (a, b)


def _bdot_bwd(mode, saved, g):
    a, b = saved
    if mode == "nn":
        return _mmv(g, b, "nt"), _mmv(a, g, "tn")
    if mode == "nt":
        return _mmv(g, b, "nn"), _mmv(g, a, "tn")
    return _mmv(b, g, "nt"), _mmv(a, g, "nn")


_bdot.defvjp(_bdot_fwd, _bdot_bwd)


def _hdot(a, b, mode="nn", precision=HI):
    ca = 0 if mode[0] == "t" else 1
    cb = 1 if mode[1] == "t" else 0
    return lax.dot_general(a, b, (((ca,), (cb,)), ((), ())), precision=precision, preferred_element_type=F32)


def _segsum(x):
    c = x.shape[-1]
    blk = min(c, 256)
    r = lax.broadcasted_iota(jnp.int32, (blk, blk), 0) >> 6
    q = lax.broadcasted_iota(jnp.int32, (blk, blk), 1) >> 6
    ones = jnp.where(r == q, 1.0, 0.0).astype(F32)
    parts = [_hdot(x[:, i:i + blk], ones, precision=lax.Precision.HIGH) for i in range(0, c, blk)]
    return parts[0] if len(parts) == 1 else jnp.concatenate(parts, axis=1)


def _sigmoid(x):
    return jax.nn.sigmoid(x)


def _softplus(x):
    return jnp.maximum(x, 0.0) + jnp.log(1.0 + jnp.exp(-jnp.abs(x)))


def _rms(x, gain):
    return x * lax.rsqrt(jnp.mean(x * x, axis=-1, keepdims=True) + RMS_EPS) * gain


def _swiglu_act(gate, up):
    return gate * _sigmoid(gate) * up


def _rwkv_pre(zs, w0, w2, a0, a2, g2, k_k, k_a):
    r, k, v = zs[:, 0:512], zs[:, 512:1024], zs[:, 1024:1536]
    lora = zs[:, 1536:1792]
    wd, ad, gd = lora[:, 0:64], lora[:, 64:128], lora[:, 128:256]
    w = -_softplus(-(w0 + _bdot(jnp.tanh(wd), w2, "nn"))) - 0.5
    a = _sigmoid(a0 + _bdot(ad, a2, "nn"))
    g = _bdot(_sigmoid(gd), g2, "nn")
    kk = k * k_k
    kk = kk * lax.rsqrt(jnp.maximum(_segsum(kk * kk), 1e-24))
    k2 = k * (1.0 + (a - 1.0) * k_a)
    return r, -jnp.exp(w), k2, v, -kk, kk * a, g


def _rwkv_post(y, r, k2, v, g, gn_w, gn_b, r_k):
    mean = _segsum(y) * (1.0 / HEAD)
    yc = y - mean
    var = _segsum(yc * yc) * (1.0 / HEAD)
    yn = yc * lax.rsqrt(var + GN_EPS) * gn_w + gn_b
    bonus = _segsum(r * k2 * r_k) * v
    return (yn + bonus) * g


def _swap_halves(x):
    lane = lax.broadcasted_iota(jnp.int32, x.shape, 1)
    return jnp.where((lane & 32) == 0, jnp.roll(x, -32, axis=1), jnp.roll(x, 32, axis=1))


def _norm_rope(x, gain, cos, sin):
    heads = x.shape[1] // HEAD
    def rep(t):
        return jnp.concatenate([t] * heads, axis=1)

    xn = x * lax.rsqrt(_segsum(x * x) * (1.0 / HEAD) + RMS_EPS) * rep(gain)
    return xn * rep(cos) + _swap_halves(xn) * rep(sin)


def _attn_combine(o0, o1, o2, l0, l1, l2):
    m = jnp.maximum(jnp.maximum(l0, l1), l2)
    e0, e1, e2 = jnp.exp(l0 - m), jnp.exp(l1 - m), jnp.exp(l2 - m)
    return (e0 * o0 + e1 * o1 + e2 * o2) / (e0 + e1 + e2)


def _merge(zgr, zga, br, ba):
    return _sigmoid(zgr) * br + _sigmoid(zga) * ba


def _attn_block(q, kp, kc, vp, vc, has_prev):
    iq = lax.broadcasted_iota(jnp.int32, (1, BAND, BAND), 1)
    ik = lax.broadcasted_iota(jnp.int32, (1, BAND, BAND), 2)
    s_c = jnp.where(iq >= ik, _bdotb(q, kc, "nt") * (HEAD ** -0.5), NEG_INF)
    s_p = jnp.where(jnp.logical_and(iq <= ik, has_prev), _bdotb(q, kp, "nt") * (HEAD ** -0.5), NEG_INF)
    m = lax.stop_gradient(jnp.maximum(jnp.max(s_c, axis=-1, keepdims=True), jnp.max(s_p, axis=-1, keepdims=True)))
    e_c, e_p = jnp.exp(s_c - m), jnp.exp(s_p - m)
    l = jnp.sum(e_c, axis=-1, keepdims=True) + jnp.sum(e_p, axis=-1, keepdims=True)
    o = (_bdotb(e_c, vc) + _bdotb(e_p, vp)) / l
    return o, jnp.broadcast_to(m + jnp.log(l), o.shape)


def _mmb(a, b, cb):
    return lax.dot_general(a.astype(BF16), b.astype(BF16), (((2,), (cb,)), ((0,), (0,))), preferred_element_type=F32)


@functools.partial(jax.custom_vjp, nondiff_argnums=(2,))
def _bdotb1(a, b, cb):
    return _mmb(a, b, cb)


def _bdotb1_fwd(a, b, cb):
    return _mmb(a, b, cb), (a, b)


def _bdotb1_bwd(cb, saved, g):
    a, b = saved
    if cb == 1:
        return _mmb(g, b, 2), _mmb(jnp.swapaxes(a, 1, 2), g, 1)
    return _mmb(g, b, 1), _mmb(jnp.swapaxes(g, 1, 2), a, 1)


_bdotb1.defvjp(_bdotb1_fwd, _bdotb1_bwd)


def _bdotb(a, b, mode="nn", precision=None):
    if mode[0] == "t":
        a = jnp.swapaxes(a, 1, 2)
    cb = 2 if mode[1] == "t" else 1
    if precision is None:
        return _bdotb1(a, b, cb)
    return lax.dot_general(a, b, (((2,), (cb,)), ((0,), (0,))), precision=precision, preferred_element_type=F32)


def _tri_inv_levels(a):
    t = a.shape[-1]
    row = lax.broadcasted_iota(jnp.int32, (1, t, t), 1)
    col = lax.broadcasted_iota(jnp.int32, (1, t, t), 2)
    x = jnp.where(row == col, 1.0, 0.0).astype(F32) + jnp.where(jnp.logical_and(row == col + 1, (row & 1) == 1), a, 0.0)
    sh = 1
    while (1 << sh) < t:
        m = jnp.logical_and((row >> sh) == (col >> sh) + 1, (row >> (sh + 1)) == (col >> (sh + 1)))
        x = x + _bdotb(_bdotb(x, jnp.where(m, a, 0.0)), x)
        sh += 1
    return x


@jax.custom_vjp
def _tri_inv(a):
    return _tri_inv_levels(a)


def _tri_inv_fwd(a):
    x = _tri_inv_levels(a)
    return x, x


def _tri_inv_bwd(x, g):
    xt = jnp.swapaxes(x, 1, 2)
    return (_bdotb(_bdotb(xt, g, precision=lax.Precision.HIGH), xt, precision=lax.Precision.HIGH),)


_tri_inv.defvjp(_tri_inv_fwd, _tri_inv_bwd)


@jax.custom_vjp
def _known_inv(a, x):
    return x


def _known_inv_fwd(a, x):
    return x, x


def _known_inv_bwd(x, g):
    return _tri_inv_bwd(x, g)[0], jnp.zeros_like(x)


_known_inv.defvjp(_known_inv_fwd, _known_inv_bwd)


def _wkv_chunk(s0, r, lw, k, v, a, b, inv=None, with_inv=False):
    nh = r.shape[0]
    t = WKV_CHUNK
    n = r.shape[1] // t

    def chunked(x):
        return x if n == 1 else jnp.concatenate([x[:, c * t:(c + 1) * t] for c in range(n)], axis=0)

    r, lw, k, v, a, b = (chunked(x) for x in (r, lw, k, v, a, b))
    row = lax.broadcasted_iota(jnp.int32, (1, t, t), 1)
    col = lax.broadcasted_iota(jnp.int32, (1, t, t), 2)
    incl, strict = row >= col, row > col
    ones = jnp.broadcast_to(jnp.where(incl, 1.0, 0.0).astype(F32), (n * nh, t, t))
    cum = _bdotb(ones, lw, precision=HI)
    c_end = cum[:, t - 1:t, :]
    e_in, e_ex, e_inv = jnp.exp(cum), jnp.exp(cum - lw), jnp.exp(-cum)
    at, rt, bt, kt = a * e_ex, r * e_in, b * e_inv, k * e_inv
    a_ab = jnp.where(strict, _bdotb(at, bt, "nt"), 0.0)
    a_ak = jnp.where(strict, _bdotb(at, kt, "nt"), 0.0)
    x = _tri_inv(a_ab) if inv is None else _known_inv(a_ab, inv)
    r_b = jnp.where(incl, _bdotb(rt, bt, "nt"), 0.0)
    akv = _bdotb(a_ak, v)
    rkv = _bdotb(jnp.where(incl, _bdotb(rt, kt, "nt"), 0.0), v)
    w_end = jnp.exp(c_end - cum)
    bw, kw, decay = b * w_end, k * w_end, jnp.exp(c_end)
    ys = []
    for c in range(n):
        hs = slice(c * nh, (c + 1) * nh)
        u = _bdotb(x[hs], _bdotb(at[hs], s0, "nt") + akv[hs])
        ys.append(_bdotb(rt[hs], s0, "nt") + _bdotb(r_b[hs], u) + rkv[hs])
        s0 = s0 * decay[hs] + _bdotb(u, bw[hs], "tn") + _bdotb(v[hs], kw[hs], "tn")
    y = ys[0] if n == 1 else jnp.concatenate(ys, axis=1)
    return (y, s0, x) if with_inv else (y, s0)


def _shift_fwd(z, mu):
    s, c = z.shape
    tc = 256

    def body(z_ref, mu_ref, o_ref):
        zz = z_ref[...]
        row = lax.broadcasted_iota(jnp.int32, zz.shape, 0)
        prev = jnp.where(row == 0, 0.0, pltpu.roll(zz, 1, 0))
        o_ref[...] = zz + (prev - zz) * mu_ref[...]

    return pl.pallas_call(
        body, name="shift_fwd", grid=(c // tc,),
        in_specs=[pl.BlockSpec((s, tc), lambda j: (0, j)), pl.BlockSpec((1, tc), lambda j: (0, j))],
        out_specs=pl.BlockSpec((s, tc), lambda j: (0, j)), out_shape=SDS((s, c), F32),
        compiler_params=pltpu.CompilerParams(dimension_semantics=("parallel",), vmem_limit_bytes=VMEM_LIMIT),
    )(z, mu)


def _shift_bwd(z, mu, dzs):
    s, c = z.shape
    tc = 256

    def body(z_ref, mu_ref, d_ref, dz_ref, dmu_ref):
        zz, d, m = z_ref[...], d_ref[...], mu_ref[...]
        row = lax.broadcasted_iota(jnp.int32, zz.shape, 0)
        prev = jnp.where(row == 0, 0.0, pltpu.roll(zz, 1, 0))
        t = d * m
        nxt = jnp.where(row == s - 1, 0.0, pltpu.roll(t, s - 1, 0))
        dz_ref[...] = (d - t + nxt).astype(dz_ref.dtype)
        dmu_ref[...] = jnp.sum(d * (prev - zz), axis=0, keepdims=True)

    return pl.pallas_call(
        body, name="shift_bwd", grid=(c // tc,),
        in_specs=[pl.BlockSpec((s, tc), lambda j: (0, j)), pl.BlockSpec((1, tc), lambda j: (0, j)),
                  pl.BlockSpec((s, tc), lambda j: (0, j))],
        out_specs=[pl.BlockSpec((s, tc), lambda j: (0, j)), pl.BlockSpec((1, tc), lambda j: (0, j))],
        out_shape=[SDS((s, c), BF16), SDS((1, c), F32)],
        compiler_params=pltpu.CompilerParams(dimension_semantics=("parallel",), vmem_limit_bytes=VMEM_LIMIT),
    )(z, mu, dzs)


def _heads(x, nh):
    return jnp.stack([x[:, h * HEAD:(h + 1) * HEAD] for h in range(nh)], axis=0)


def _unheads(x):
    return jnp.concatenate([x[h] for h in range(x.shape[0])], axis=1)


def _wkv_fwd(zs, lw, k2, na, b):
    s = lw.shape[0]
    t, hb = WKV_CHUNK * WKV_CHUNKS_PER_STEP, WKV_HEADS_PER_STEP
    w = hb * HEAD
    nc, ng = s // t, RWKV_HEADS // hb
    nx = WKV_CHUNKS_PER_STEP * RWKV_HEADS

    def body(r_ref, v_ref, lw_ref, k_ref, a_ref, b_ref, y_ref, s0_ref, x_ref, state):
        @pl.when(pl.program_id(1) == 0)
        def _():
            state[...] = jnp.zeros_like(state)

        s0 = state[...]
        s0_ref[0] = s0
        y, s1, x = _wkv_chunk(s0, *[_heads(t_ref[...], hb) for t_ref in (r_ref, lw_ref, k_ref, v_ref, a_ref, b_ref)], with_inv=True)
        y_ref[...] = _unheads(y)
        x_ref[0] = x
        state[...] = s1

    def col(off):
        return pl.BlockSpec((t, w), functools.partial(lambda g, i, off: (i, g + off), off=off))

    return pl.pallas_call(
        body, name="wkv_fwd", grid=(ng, nc),
        in_specs=[col(0), col(2 * ng), col(0), col(0), col(0), col(0)],
        out_specs=[col(0), pl.BlockSpec((1, hb, HEAD, HEAD), lambda g, i: (i, g, 0, 0)),
                   pl.BlockSpec((1, nx, WKV_CHUNK, WKV_CHUNK), lambda g, i: (i, 0, 0, 0))],
        out_shape=[SDS((s, RWKV_DIM), F32), SDS((nc, RWKV_HEADS, HEAD, HEAD), F32), SDS((nc, nx, WKV_CHUNK, WKV_CHUNK), F32)],
        scratch_shapes=[pltpu.VMEM((hb, HEAD, HEAD), F32)],
        compiler_params=pltpu.CompilerParams(dimension_semantics=("parallel", "arbitrary"), vmem_limit_bytes=VMEM_LIMIT),
    )(zs, zs, lw, k2, na, b)


def _wkv_bwd(zs, lw, k2, na, b, s0s, invs, dy):
    s = lw.shape[0]
    t, hb = WKV_CHUNK * WKV_CHUNKS_PER_STEP, WKV_HEADS_PER_STEP
    w = hb * HEAD
    nc, ng = s // t, RWKV_HEADS // hb
    nx = WKV_CHUNKS_PER_STEP * RWKV_HEADS

    def body(r_ref, v_ref, lw_ref, k_ref, a_ref, b_ref, s0_ref, x_ref, dy_ref, dr_ref, dlw_ref, dk_ref, dv_ref, da_ref, db_ref, dstate):
        @pl.when(pl.program_id(1) == 0)
        def _():
            dstate[...] = jnp.zeros_like(dstate)

        _, vjp = jax.vjp(functools.partial(_wkv_chunk, inv=x_ref[0]), s0_ref[0],
                         *[_heads(t_ref[...], hb) for t_ref in (r_ref, lw_ref, k_ref, v_ref, a_ref, b_ref)])
        grads = vjp((_heads(dy_ref[...], hb), dstate[...]))
        dstate[...] = grads[0]
        for o_ref, gval in zip((dr_ref, dlw_ref, dk_ref, dv_ref, da_ref, db_ref), grads[1:]):
            o_ref[...] = _unheads(gval)

    def col(off):
        return pl.BlockSpec((t, w), functools.partial(lambda g, i, off: (nc - 1 - i, g + off), off=off))

    return pl.pallas_call(
        body, name="wkv_bwd", grid=(ng, nc),
        in_specs=[col(0), col(2 * ng), col(0), col(0), col(0), col(0),
                  pl.BlockSpec((1, hb, HEAD, HEAD), lambda g, i: (nc - 1 - i, g, 0, 0)),
                  pl.BlockSpec((1, nx, WKV_CHUNK, WKV_CHUNK), lambda g, i: (nc - 1 - i, 0, 0, 0)), col(0)],
        out_specs=[col(0)] * 6,
        out_shape=[SDS((s, RWKV_DIM), F32)] * 6,
        scratch_shapes=[pltpu.VMEM((hb, HEAD, HEAD), F32)],
        compiler_params=pltpu.CompilerParams(dimension_semantics=("parallel", "arbitrary"), vmem_limit_bytes=VMEM_LIMIT),
    )(zs, zs, lw, k2, na, b, s0s, invs, dy)


def _attn_batch(refs, bps, nh, first_has_prev):
    q_ref, kp_ref, kc_ref, vp_ref, vc_ref = refs

    def blocks(cur_ref, prev_ref=None):
        out = []
        for b in range(bps):
            if prev_ref is None:
                t = cur_ref[b * BAND:(b + 1) * BAND, :]
            else:
                t = prev_ref[...] if b == 0 else cur_ref[(b - 1) * BAND:b * BAND, :]
            out.append(_heads(t.astype(F32), nh))
        return out[0] if bps == 1 else jnp.concatenate(out, axis=0)

    batch = lax.broadcasted_iota(jnp.int32, (bps * nh, 1, 1), 0)
    has_prev = jnp.logical_or(batch >= nh, first_has_prev)
    return (blocks(q_ref), blocks(kc_ref, kp_ref), blocks(kc_ref), blocks(vc_ref, vp_ref), blocks(vc_ref)), has_prev


def _attn_rows(x, bps, nh):
    parts = [_unheads(x[b * nh:(b + 1) * nh]) for b in range(bps)]
    return parts[0] if bps == 1 else jnp.concatenate(parts, axis=0)


def _attn_fwd(q, k, v, d):
    s = q.shape[0]
    l = s // d
    nb = l // BAND
    assert nb * BAND == l
    qv, kv, vv = (t.reshape(l, d * GROUP_DIM) for t in (q, k, v))
    width = min(d, ATTN_CLASSES_PER_STEP) * GROUP_DIM
    bps = ATTN_CLASSES_PER_STEP * GROUP_DIM // width
    nh = width // HEAD

    def body(q_ref, kp_ref, kc_ref, vp_ref, vc_ref, o_ref, l_ref):
        ops, has_prev = _attn_batch((q_ref, kp_ref, kc_ref, vp_ref, vc_ref), bps, nh, pl.program_id(1) > 0)
        o, lse = _attn_block(*ops, has_prev)
        o_ref[...] = _attn_rows(o, bps, nh)
        l_ref[...] = _attn_rows(lse, bps, nh)

    cur = pl.BlockSpec((bps * BAND, width), lambda rho, i: (i, rho))
    prev = pl.BlockSpec((BAND, width), lambda rho, i: (jnp.maximum(i * bps - 1, 0), rho))
    o, lse = pl.pallas_call(
        body, name=f"attn_fwd_d{d}", grid=(d * GROUP_DIM // width, nb // bps),
        in_specs=[cur, prev, cur, prev, cur], out_specs=[cur, cur],
        out_shape=[SDS((l, d * GROUP_DIM), F32), SDS((l, d * GROUP_DIM), F32)],
        compiler_params=pltpu.CompilerParams(dimension_semantics=("parallel", "arbitrary"), vmem_limit_bytes=VMEM_LIMIT),
    )(qv, kv, kv, vv, vv)
    return o.reshape(s, GROUP_DIM), lse.reshape(s, GROUP_DIM)


def _attn_bwd(q, k, v, d, do, dlse):
    s = q.shape[0]
    l = s // d
    nb = l // BAND
    qv, kv, vv, dov, dlv = (t.reshape(l, d * GROUP_DIM) for t in (q, k, v, do, dlse))
    width = min(d, ATTN_CLASSES_PER_STEP) * GROUP_DIM
    bps = ATTN_CLASSES_PER_STEP * GROUP_DIM // width
    nh = width // HEAD
    ns = nb // bps

    def body(q_ref, kp_ref, kc_ref, vp_ref, vc_ref, do_ref, dl_ref, dq_ref, dk_ref, dv_ref, ck, cv):
        step = pl.program_id(1)

        @pl.when(step == 0)
        def _():
            ck[...] = jnp.zeros_like(ck)
            cv[...] = jnp.zeros_like(cv)

        ops, has_prev = _attn_batch((q_ref, kp_ref, kc_ref, vp_ref, vc_ref), bps, nh, step < ns - 1)
        _, vjp = jax.vjp(functools.partial(_attn_block, has_prev=has_prev), *ops)
        cts = [jnp.concatenate([_heads(t_ref[b * BAND:(b + 1) * BAND, :], nh) for b in range(bps)], axis=0) if bps > 1
               else _heads(t_ref[...], nh) for t_ref in (do_ref, dl_ref)]
        dq, dkp, dkc, dvp, dvc = vjp(tuple(cts))
        dq_ref[...] = _attn_rows(dq, bps, nh)
        for out_ref, cur_part, prev_part, carry in ((dk_ref, dkc, dkp, ck), (dv_ref, dvc, dvp, cv)):
            for b in range(bps):
                after = carry[...] if b == bps - 1 else _unheads(prev_part[(b + 1) * nh:(b + 2) * nh])
                out_ref[b * BAND:(b + 1) * BAND, :] = _unheads(cur_part[b * nh:(b + 1) * nh]) + after
            carry[...] = _unheads(prev_part[0:nh])

    cur = pl.BlockSpec((bps * BAND, width), lambda rho, i: (ns - 1 - i, rho))
    prev = pl.BlockSpec((BAND, width), lambda rho, i: (jnp.maximum((ns - 1 - i) * bps - 1, 0), rho))
    dq, dk, dv = pl.pallas_call(
        body, name=f"attn_bwd_d{d}", grid=(d * GROUP_DIM // width, ns),
        in_specs=[cur, prev, cur, prev, cur, cur, cur], out_specs=[cur] * 3,
        out_shape=[SDS((l, d * GROUP_DIM), F32)] * 3,
        scratch_shapes=[pltpu.VMEM((BAND, width), F32), pltpu.VMEM((BAND, width), F32)],
        compiler_params=pltpu.CompilerParams(dimension_semantics=("parallel", "arbitrary"), vmem_limit_bytes=VMEM_LIMIT),
    )(qv, kv, kv, vv, vv, dov, dlv)
    return dq.reshape(s, GROUP_DIM), dk.reshape(s, GROUP_DIM), dv.reshape(s, GROUP_DIM)


def _coords():
    return lax.axis_index("x"), lax.axis_index("y"), lax.axis_index("c")


_CHIP_FLIPS = ((1, 0), (0, 1), (1, 1))


def _flip(v, f):
    return 1 - v if f else v


def _form(kind, r, c):
    return (N_CHIPS, r, c) if kind == "blk" else (r, N_CHIPS * c)


def _slot(ref, kind, j, rows, c):
    if kind == "blk":
        return ref.at[j] if rows is None else ref.at[j, rows]
    cols = pl.ds(pl.multiple_of(j * c, 128), c)
    return ref.at[:, cols] if rows is None else ref.at[rows, cols]


def _half(r, which, align):
    return pl.ds(pl.multiple_of(which * (r // 2), align), r // 2)


def _rcopy(src, dst, send_sems, recv_sems, kk, dev):
    return pltpu.make_async_remote_copy(src_ref=src, dst_ref=dst, send_sem=send_sems.at[kk], recv_sem=recv_sems.at[kk],
                                        device_id=dev, device_id_type=MESH)


def _gather_plan(specs, step):
    def copies(refs, ss, rs, received):
        x, y, c = _coords()
        out = []
        for w, (kind, r, cc) in enumerate(specs):
            mine, other = _half(r, c, 16), _half(r, 1 - c, 16)
            for kk, (fx, fy) in enumerate(_CHIP_FLIPS):
                px, py = _flip(x, fx), _flip(y, fy)
                if step == "ici":
                    sl = _slot(refs[w], kind, 2 * px + py if received else 2 * x + y, mine, cc)
                    dev = (px, py, c)
                else:
                    sl = _slot(refs[w], kind, 2 * px + py, other if received else mine, cc)
                    dev = (x, y, 1 - c)
                out.append(_rcopy(sl, sl, ss, rs, 3 * w + kk, dev))
        return out

    def issue(refs, ss, rs):
        return copies(refs, ss, rs, False)

    def expect(refs, ss, rs):
        return copies(refs, ss, rs, False), copies(refs, ss, rs, True)

    return issue, expect


_HBM = pl.BlockSpec(memory_space=pltpu.HBM)
_SEM = pl.BlockSpec(memory_space=pltpu.SEMAPHORE)
_EFFECT = pltpu.SideEffectType.DATAFLOW_SIDE_EFFECTING


def _copies_start(name, bufs, n_sems, issue, after=None):
    nb = len(bufs)
    extra = [] if after is None else [after]

    def body(*refs):
        send_sems, recv_sems = refs[nb + len(extra)], refs[nb + len(extra) + 1]
        for cp in issue(refs[:nb], send_sems, recv_sems):
            cp.start()
        refs[-1][...] = jnp.zeros_like(refs[-1])

    outs = pl.pallas_call(
        body, name=name,
        out_shape=(pltpu.SemaphoreType.DMA((n_sems,)), pltpu.SemaphoreType.DMA((n_sems,)),
                   *[pltpu.HBM(b.shape, b.dtype) for b in bufs], SDS((8, 128), F32)),
        in_specs=[_HBM] * nb + [pl.BlockSpec(memory_space=pl.ANY)] * len(extra),
        out_specs=(_SEM, _SEM, *[_HBM] * nb, pl.BlockSpec(memory_space=pltpu.VMEM)),
        input_output_aliases={i: 2 + i for i in range(nb)},
        compiler_params=pltpu.CompilerParams(has_side_effects=_EFFECT),
    )(*[pltpu.with_memory_space_constraint(b, pltpu.HBM) for b in bufs], *extra)
    return outs[0], outs[1], list(outs[2:2 + nb]), outs[-1]


def _copies_wait(name, bufs, send_sems, recv_sems, after, expect):
    nb = len(bufs)

    def body(*refs):
        sent, received = expect(refs[:nb], refs[nb], refs[nb + 1])
        for cp in sent:
            cp.wait_send()
        for cp in received:
            cp.wait_recv()

    outs = pl.pallas_call(
        body, name=name,
        out_shape=tuple(pltpu.HBM(b.shape, b.dtype) for b in bufs),
        in_specs=(*[_HBM] * nb, _SEM, _SEM, pl.BlockSpec(memory_space=pl.ANY)), out_specs=tuple([_HBM] * nb),
        input_output_aliases={i: i for i in range(nb)},
        compiler_params=pltpu.CompilerParams(has_side_effects=_EFFECT),
    )(*bufs, send_sems, recv_sems, after)
    return list(outs)


def _add_pair(g, recv, kind, r, c, c_arr, name):
    h = r // 2
    if kind == "blk":
        tr = _row_tile(h, 512)
        grid = (N_CHIPS, h // tr)
        g_spec = pl.BlockSpec((1, 1, tr, c), lambda j, i, c_ref: (j, c_ref[0], i, 0))
        o_spec = pl.BlockSpec((1, tr, c), lambda j, i, c_ref: (j, i, 0))
        gv, oshape = g.reshape(N_CHIPS, 2, h, c), (N_CHIPS, h, c)
    else:
        tr = _row_tile(h, 64)
        grid = (h // tr,)
        g_spec = pl.BlockSpec((1, tr, N_CHIPS * c), lambda i, c_ref: (c_ref[0], i, 0))
        o_spec = pl.BlockSpec((tr, N_CHIPS * c), lambda i, c_ref: (i, 0))
        gv, oshape = g.reshape(2, h, N_CHIPS * c), (h, N_CHIPS * c)

    def body(c_ref, g_ref, r_ref, o_ref, ob_ref):
        v = (g_ref[:, 0] if kind == "blk" else g_ref[0]) + r_ref[...]
        o_ref[...] = v
        ob_ref[...] = v.astype(BF16)

    return pl.pallas_call(
        body, name=name,
        grid_spec=pltpu.PrefetchScalarGridSpec(num_scalar_prefetch=1, grid=grid, in_specs=[g_spec, o_spec], out_specs=[o_spec] * 2),
        out_shape=[SDS(oshape, F32), SDS(oshape, BF16)],
        compiler_params=pltpu.CompilerParams(vmem_limit_bytes=VMEM_LIMIT),
    )(c_arr, gv, recv)


def _sum_chips(pair, recv, kind, r, c, mc_arr, name):
    h = r // 2
    tr = _row_tile(h, 512)
    nt = h // tr
    if kind == "blk":
        p_spec = pl.BlockSpec((1, tr, c), lambda i, mc: (mc[0], i, 0))
    else:
        p_spec = pl.BlockSpec((tr, c), lambda i, mc: (i, mc[0]))

    def body(mc, a_ref, r_ref, g_out):
        own = a_ref[0] if kind == "blk" else a_ref[...]
        g_out[...] = ((own + r_ref[0].astype(F32)) + r_ref[1].astype(F32)) + r_ref[2].astype(F32)

    return pl.pallas_call(
        body, name=name,
        grid_spec=pltpu.PrefetchScalarGridSpec(
            num_scalar_prefetch=1, grid=(nt,), in_specs=[p_spec, pl.BlockSpec((3, tr, c), lambda i, mc: (0, i, 0))],
            out_specs=pl.BlockSpec((tr, c), lambda i, mc: (mc[1] * nt + i, 0))),
        out_shape=SDS((r, c), F32),
        compiler_params=pltpu.CompilerParams(vmem_limit_bytes=VMEM_LIMIT),
    )(mc_arr, pair, recv)


class _GroupReduce:
    def __init__(self, tag, specs, c_arr, mc_arr):
        self.tag, self.specs, self.c_arr, self.mc_arr = tag, specs, c_arr, mc_arr
        self.n = len(specs)

    def _plan(self, step):
        specs, n = self.specs, self.n

        def copies(refs, ss, rs, received):
            x, y, c = _coords()
            sib, out = (x, y, 1 - c), []
            for w, (_, kind, r, cc) in enumerate(specs):
                if step == "join":
                    there = refs[w].at[_half(r, 1 - c if received else c, 8)]
                    out.append(_rcopy(there, there, ss, rs, w, sib))
                    continue
                src, land = refs[w], refs[n + w]
                if step == "swap":
                    rows = _half(r, 1 - c, 8)
                    part = src.at[:, rows] if kind == "blk" else src.at[rows]
                    out.append(_rcopy(land if received else part, land, ss, rs, w, sib))
                else:
                    for kk, (fx, fy) in enumerate(_CHIP_FLIPS):
                        px, py = _flip(x, fx), _flip(y, fy)
                        part = land.at[kk] if received else _slot(src, kind, 2 * px + py, None, cc)
                        out.append(_rcopy(part, land.at[kk], ss, rs, 3 * w + kk, (px, py, c)))
            return out

        def issue(refs, ss, rs):
            return copies(refs, ss, rs, False)

        def expect(refs, ss, rs):
            return copies(refs, ss, rs, False), copies(refs, ss, rs, True)

        return issue, expect

    def swap_start(self, grads, after=None):
        lands = [lax.empty(_form(kind, r // 2, c), F32) for _, kind, r, c in self.specs]
        ss, rs, bufs, tok = _copies_start(f"rs_{self.tag}_swap", list(grads) + lands, self.n, self._plan("swap")[0], after=after)
        self.state = (ss, rs, bufs)
        return tok

    def swap_wait_ici_start(self, after):
        ss, rs, bufs = self.state
        bufs = _copies_wait(f"rs_{self.tag}_swap_wait", bufs, ss, rs, after, self._plan("swap")[1])
        pairs = [_add_pair(bufs[w], bufs[self.n + w], kind, r, c, self.c_arr, name=f"rs_{self.tag}_pair_{nm}")
                 for w, (nm, kind, r, c) in enumerate(self.specs)]
        self.pair = [pr[0] for pr in pairs]
        lands = [lax.empty((3, r // 2, c), BF16) for _, _, r, c in self.specs]
        ss, rs, bufs, tok = _copies_start(f"rs_{self.tag}_ici", [pr[1] for pr in pairs] + lands, 3 * self.n, self._plan("ici")[0])
        self.state = (ss, rs, bufs)
        return tok

    def ici_wait_join_start(self, after):
        ss, rs, bufs = self.state
        bufs = _copies_wait(f"rs_{self.tag}_ici_wait", bufs, ss, rs, after, self._plan("ici")[1])
        outs = [_sum_chips(self.pair[w], bufs[self.n + w], kind, r, c, self.mc_arr, name=f"rs_{self.tag}_sum_{nm}")
                for w, (nm, kind, r, c) in enumerate(self.specs)]
        ss, rs, bufs, tok = _copies_start(f"rs_{self.tag}_join", outs, self.n, self._plan("join")[0])
        self.state = (ss, rs, bufs)
        return tok

    def join_wait(self, after):
        ss, rs, bufs = self.state
        bufs = _copies_wait(f"rs_{self.tag}_join_wait", bufs, ss, rs, after, self._plan("join")[1])
        return {nm: bufs[w] for w, (nm, _, _, _) in enumerate(self.specs)}


def _small_gather_plan():
    def copies(refs, ss, rs, received):
        x, y, c = _coords()
        own, land = refs
        out = []
        for kk in range(1, 8):
            px, py, pc = _flip(x, (kk >> 2) & 1), _flip(y, (kk >> 1) & 1), _flip(c, kk & 1)
            there = land.at[4 * px + 2 * py + pc]
            out.append(_rcopy(there if received else own, there if received else land.at[4 * x + 2 * y + c], ss, rs, kk - 1,
                              (px, py, pc)))
        return out

    def issue(refs, ss, rs):
        return copies(refs, ss, rs, False)

    def expect(refs, ss, rs):
        return copies(refs, ss, rs, False), copies(refs, ss, rs, True)

    return issue, expect


def _sum_slots(slots):
    n, rows, cols = slots.shape

    def body(s_ref, o_ref):
        acc = s_ref[0]
        for j in range(1, n):
            acc = acc + s_ref[j]
        o_ref[...] = acc

    return pl.pallas_call(
        body, name="sum_small",
        in_specs=[pl.BlockSpec(memory_space=pltpu.VMEM)], out_specs=pl.BlockSpec(memory_space=pltpu.VMEM),
        out_shape=SDS((rows, cols), F32),
    )(slots)


def _adamw_rows(w, g, m, v):
    m = ADAM_B1 * m + (1.0 - ADAM_B1) * g
    v = ADAM_B2 * v + (1.0 - ADAM_B2) * jnp.square(g)
    m_hat = m / (1.0 - ADAM_B1 ** ADAM_STEP)
    v_hat = v / (1.0 - ADAM_B2 ** ADAM_STEP)
    return -ADAM_LR * (m_hat / (jnp.sqrt(v_hat) + ADAM_EPS) + ADAM_WD * w), m, v


def _adamw(w, g, m, v, name, dep=None, with_grad=False):
    rows, cols = w.shape
    tm = _pick(rows, (256, 128, 64, 16, 8))
    f = (lambda wt, gt, mt, vt: (gt,) + _adamw_rows(wt, gt, mt, vt)) if with_grad else _adamw_rows
    return _rows_call(f, [(t, 0, cols) for t in (w, g, m, v)], [], [(cols, F32)] * (3 + with_grad), tm=tm, name=name, dep=dep)


def _pack_small(parts):
    flat = jnp.concatenate([parts[n].reshape(-1) for n, _ in SMALL])
    return jnp.pad(flat, (0, SMALL_ROWS * PACK_COLS - flat.shape[0])).reshape(SMALL_ROWS, PACK_COLS)


def _unpack_small(buf, shapes):
    flat, out, off = buf.reshape(-1), {}, 0
    for n, sz in SMALL:
        out[n] = flat[off:off + sz].reshape(shapes[n])
        off += sz
    return out


def _lora_stack(parts):
    return jnp.concatenate([parts[n] for n, _ in LORA], axis=-2)


def _lora_split(stacked):
    out, off = {}, 0
    for n, rows in LORA:
        out[n] = stacked[..., off:off + rows, :]
        off += rows
    return out


def _ffn_gate_up(h, wgt, wut, name, dep=None):
    s, d = h.shape
    nblk, f, _ = wgt.shape
    tm = _pick(s, (1024, 512, 256))
    dn = (((1,), (1,)), ((), ()))

    def body(h_ref, wg_ref, wu_ref, *rest):
        g_ref, u_ref, a_ref = rest[-3:]
        hh = h_ref[...]
        g = lax.dot_general(hh, wg_ref[0], dn, preferred_element_type=F32)
        u = lax.dot_general(hh, wu_ref[0], dn, preferred_element_type=F32)
        g_ref[0], u_ref[0] = g.astype(BF16), u.astype(BF16)
        a_ref[0] = _swiglu_act(g, u).astype(BF16)

    w_spec = pl.BlockSpec((1, f, d), lambda j, i: (j, 0, 0))
    o_spec = pl.BlockSpec((1, tm, f), lambda j, i: (j, i, 0))
    extra = [] if dep is None else [dep]
    return pl.pallas_call(
        body, name=name, grid=(nblk, s // tm),
        in_specs=[pl.BlockSpec((tm, d), lambda j, i: (i, 0)), w_spec, w_spec] + [pl.BlockSpec(memory_space=pl.ANY)] * len(extra),
        out_specs=[o_spec] * 3,
        out_shape=[SDS((nblk, s, f), BF16)] * 3,
        compiler_params=pltpu.CompilerParams(dimension_semantics=("parallel", "parallel"), vmem_limit_bytes=VMEM_LIMIT),
    )(h, wgt, wut, *extra)


def _ffn_down_dx(dx_bf, wd, gate, up, name, dep=None):
    s, d = dx_bf.shape
    nblk, f, _ = wd.shape
    tm = _pick(s, (1024, 512, 256))
    dn = (((1,), (1,)), ((), ()))

    def body(dx_ref, wd_ref, g_ref, u_ref, *rest):
        dg_ref, du_ref = rest[-2:]
        dact = 0.5 * lax.dot_general(dx_ref[...], wd_ref[0], dn, preferred_element_type=F32)
        _, vjp = jax.vjp(_swiglu_act, g_ref[0].astype(F32), u_ref[0].astype(F32))
        dg, du = vjp(dact)
        dg_ref[0], du_ref[0] = dg.astype(BF16), du.astype(BF16)

    o_spec = pl.BlockSpec((1, tm, f), lambda j, i: (j, i, 0))
    extra = [] if dep is None else [dep]
    return pl.pallas_call(
        body, name=name, grid=(nblk, s // tm),
        in_specs=[pl.BlockSpec((tm, d), lambda j, i: (i, 0)), pl.BlockSpec((1, f, d), lambda j, i: (j, 0, 0)), o_spec, o_spec]
        + [pl.BlockSpec(memory_space=pl.ANY)] * len(extra),
        out_specs=[o_spec] * 2, out_shape=[SDS((nblk, s, f), BF16)] * 2,
        compiler_params=pltpu.CompilerParams(dimension_semantics=("parallel", "parallel"), vmem_limit_bytes=VMEM_LIMIT),
    )(dx_bf, wd, gate, up, *extra)


def _ffn_fwd(x, gain, wgt, wut, wd, tag, h=None, dep=None):
    if h is None:
        h = _rows_call(_rms, [(x, 0, D_MODEL)], [gain], [(D_MODEL, BF16)], tm=512, name=f"{tag}_norm")[0]
    gate, up, act = _ffn_gate_up(h, wgt, wut, f"{tag}_gate_up", dep=dep)
    x_new = _mm(act, wd, sum_blocks=True, res=x, alpha=0.5, name=f"{tag}_down")
    return x_new, (x, h, gate, up, act)


def _ffn_bwd(dx_new, dx_new_bf, saved, gain, wgt, wut, wd, tag, dep=None, hooks=None):
    x, h, gate, up, act = saved
    hooks = hooks or {}

    def hook(name, *vals):
        return hooks[name](*vals) if name in hooks else None

    d_wd = _mm(act, dx_new_bf, ta=True, alpha=0.5, name=f"{tag}_down_dw")
    dep = hook("down", d_wd) if "down" in hooks else dep
    dgate, dup = _ffn_down_dx(dx_new_bf, wd, gate, up, f"{tag}_down_dx", dep=dep)
    d_wgt = _mm(dgate, h, ta=True, dep=hook("mid", dgate), name=f"{tag}_gate_dw")
    d_wut = _mm(dup, h, ta=True, name=f"{tag}_up_dw")
    dh = _mm(dgate, wgt, sum_blocks=True, dep=hook("dw", d_wgt, d_wut), name=f"{tag}_gate_dx")
    dx, dx_bf, dgain = _mm(dup, wut, sum_blocks=True, res=dh, dep=hook("dx", dh), post=_norm_bwd_post(x, gain, dx_new),
                           name=f"{tag}_up_dx")
    hook("end", dx_bf)
    return dx, dx_bf, dgain, d_wgt, d_wut, d_wd


def _norm_bwd_post(x, gain, dres):
    def f(dht, xt, drt, gt):
        _, vjp = jax.vjp(_rms, xt, gt)
        dxt, dgt = vjp(dht)
        return dxt + drt, dxt + drt, dgt

    return f, [x, dres], [gain], [F32, BF16], [(1, D_MODEL)]


def kernel(x, p, positions, ffn1_norm, ffn1_w_gate, ffn1_w_up, ffn1_w_down, mix_norm, w_in, rwkv_mu, rwkv_w0, rwkv_w2, rwkv_a0, rwkv_a2, rwkv_g2, rwkv_k_k, rwkv_k_a, rwkv_r_k, rwkv_gn_w, rwkv_gn_b, q_norm, k_norm, w_br_rwkv, w_br_attn, w_out, ffn2_norm, ffn2_w_gate, ffn2_w_up, ffn2_w_down, ple_norm, ple_w_gate, ple_w_proj, loss_target, m_ffn1_norm, m_ffn1_w_gate, m_ffn1_w_up, m_ffn1_w_down, m_mix_norm, m_w_in, m_rwkv_mu, m_rwkv_w0, m_rwkv_w2, m_rwkv_a0, m_rwkv_a2, m_rwkv_g2, m_rwkv_k_k, m_rwkv_k_a, m_rwkv_r_k, m_rwkv_gn_w, m_rwkv_gn_b, m_q_norm, m_k_norm, m_w_br_rwkv, m_w_br_attn, m_w_out, m_ffn2_norm, m_ffn2_w_gate, m_ffn2_w_up, m_ffn2_w_down, m_ple_norm, m_ple_w_gate, m_ple_w_proj, v_ffn1_norm, v_ffn1_w_gate, v_ffn1_w_up, v_ffn1_w_down, v_mix_norm, v_w_in, v_rwkv_mu, v_rwkv_w0, v_rwkv_w2, v_rwkv_a0, v_rwkv_a2, v_rwkv_g2, v_rwkv_k_k, v_rwkv_k_a, v_rwkv_r_k, v_rwkv_gn_w, v_rwkv_gn_b, v_q_norm, v_k_norm, v_w_br_rwkv, v_w_br_attn, v_w_out, v_ffn2_norm, v_ffn2_w_gate, v_ffn2_w_up, v_ffn2_w_down, v_ple_norm, v_ple_w_gate, v_ple_w_proj):
    args = dict(locals())
    wts = {n: args[n] for n in WEIGHTS}
    mom_m = {n: args["m_" + n] for n in WEIGHTS}
    mom_v = {n: args["v_" + n] for n in WEIGHTS}
    x0, tgt = x[0], loss_target[0]
    s = x0.shape[0]
    p_tok = p[0, 0]

    vec = {n: wts[n].reshape(1, -1) for n, _ in SMALL}
    xi, yi, ci = _coords()
    me = 2 * xi + yi
    def laid(t, n):
        return jnp.transpose(t[n][0]) if n in TRANSPOSED else t[n][0]

    shard_of = {n: laid(wts, n) for g in GROUPS.values() for n, _, _, _ in g if n != "lora"}
    shard_of["lora"] = _lora_stack({n: wts[n][0] for n, _ in LORA})

    def whole_with_own(n, kind, r, c, tok=None):
        at = (me, 0, 0) if kind == "blk" else (0, me * c)
        own = (shard_of[n] if tok is None else shard_of[n] + tok[0, 0]).astype(BF16)
        return lax.dynamic_update_slice(lax.empty(_form(kind, r, c), BF16), own[None] if kind == "blk" else own, at)

    specs = {g: [(kind, r, c) for _, kind, r, c in grp] for g, grp in GROUPS.items()}
    plans = {(g, st): _gather_plan(specs[g], st) for g in GROUPS for st in ("ici", "d2d")}
    buf_f1 = [whole_with_own(*w) for w in GROUPS["f1"]]
    ss_0, rs_0, buf_f1, tok_0 = _copies_start("gather_f1_ici", buf_f1, 3 * len(buf_f1), plans["f1", "ici"][0])
    bufs = {g: [whole_with_own(*w, tok=tok_0) for w in GROUPS[g]] for g in ("mx", "f2")}
    buf_f1 = _copies_wait("gather_f1_ici_wait", buf_f1, ss_0, rs_0, bufs["mx"][0], plans["f1", "ici"][1])
    ss_1, rs_1, buf_f1, tok_1 = _copies_start("gather_f1_d2d", buf_f1, 3 * len(buf_f1), plans["f1", "d2d"][0])
    h1 = _rows_call(_rms, [(x0, 0, D_MODEL)], [vec["ffn1_norm"] + tok_1[0, 0]], [(D_MODEL, BF16)], tm=512, name="ffn1_norm")[0]
    buf_f1 = _copies_wait("gather_f1_d2d_wait", buf_f1, ss_1, rs_1, h1, plans["f1", "d2d"][1])
    wb = dict(zip([w[0] for w in GROUPS["f1"]], buf_f1))
    ss_a, rs_a, buf_mx, tok_a = _copies_start("gather_mx_ici", bufs["mx"], 3 * len(bufs["mx"]), plans["mx", "ici"][0],
                                              after=wb["ffn1_w_gate"])

    inv_freq = 1.0 / (ROPE_THETA ** (jnp.arange(0, HEAD, 2, dtype=F32) / HEAD))
    ang = positions[0].astype(F32)[:, None] * inv_freq
    cos, sin = jnp.cos(ang), jnp.sin(ang)
    cos2, sin2 = jnp.concatenate([cos, cos], axis=1), jnp.concatenate([-sin, sin], axis=1)

    x1, ffn1_saved = _ffn_fwd(x0, vec["ffn1_norm"], wb["ffn1_w_gate"], wb["ffn1_w_up"], wb["ffn1_w_down"], "ffn1", h=h1, dep=tok_a)
    buf_mx = _copies_wait("gather_mx_ici_wait", buf_mx, ss_a, rs_a, x1, plans["mx", "ici"][1])
    ss_b, rs_b, buf_mx, tok_b = _copies_start("gather_mx_d2d", buf_mx, 3 * len(buf_mx), plans["mx", "d2d"][0])
    ss_c, rs_c, buf_f2, tok_c = _copies_start("gather_f2_ici", bufs["f2"], 3 * len(bufs["f2"]), plans["f2", "ici"][0])
    h = _rows_call(_rms, [(x1, 0, D_MODEL)], [vec["mix_norm"] + (tok_b[0, 0] + tok_c[0, 0])], [(D_MODEL, BF16)], tm=256,
                   name="mix_norm")[0]
    buf_mx = _copies_wait("gather_mx_d2d_wait", buf_mx, ss_b, rs_b, h, plans["mx", "d2d"][1])
    wb.update(zip([w[0] for w in GROUPS["mx"]], buf_mx))
    w_in_all = wb["w_in"]
    w_in_r, w_in_a, w_in_g = w_in_all[:, :RWKV_COLS], w_in_all[:, RWKV_COLS:RWKV_COLS + ATTN_COLS], w_in_all[:, RWKV_COLS + ATTN_COLS:]
    lora = _lora_split(wb["lora"])
    w2, a2, g2 = lora["rwkv_w2"], lora["rwkv_a2"], lora["rwkv_g2"]
    z_r = _mm(h, w_in_r, name="in_rwkv")
    z_a = _mm(h, w_in_a, name="in_attn")
    z_g = _mm(h, w_in_g, name="in_gate")

    zs = _shift_fwd(z_r, vec["rwkv_mu"])
    pre_params = [vec["rwkv_w0"], w2, vec["rwkv_a0"], a2, g2, vec["rwkv_k_k"], vec["rwkv_k_a"]]
    def pre_fwd(*t):
        res = _rwkv_pre(*t)
        return res[1], res[2], res[4], res[5], res[6]

    lw, k2, na, kb, gate_r = _rows_call(pre_fwd, [(zs, 0, RWKV_COLS)], pre_params, [(RWKV_DIM, F32)] * 5, tm=512, name="rwkv_pre")
    y_scan, s0s, invs = _wkv_fwd(zs, lw, k2, na, kb)
    buf_f2 = _copies_wait("gather_f2_ici_wait", buf_f2, ss_c, rs_c, y_scan, plans["f2", "ici"][1])
    ss_d, rs_d, buf_f2, tok_d = _copies_start("gather_f2_d2d", buf_f2, 3 * len(buf_f2), plans["f2", "d2d"][0])
    post_params = [vec["rwkv_gn_w"] + tok_d[0, 0], vec["rwkv_gn_b"], vec["rwkv_r_k"]]
    post_rows = [(y_scan, 0, RWKV_DIM), (zs, 0, RWKV_DIM), (k2, 0, RWKV_DIM), (zs, 2, RWKV_DIM), (gate_r, 0, RWKV_DIM)]
    y_rwkv = _rows_call(_rwkv_post, post_rows, post_params, [(RWKV_DIM, BF16)], tm=512, name="rwkv_post")[0]
    buf_f2 = _copies_wait("gather_f2_d2d_wait", buf_f2, ss_d, rs_d, y_rwkv, plans["f2", "d2d"][1])
    wb.update(zip([w[0] for w in GROUPS["f2"]], buf_f2))
    w_brr, w_bra = wb["w_br_rwkv"], wb["w_br_attn"]
    w_o = wb["w_out"].reshape(D_MODEL, D_MODEL)
    w_pp, w_pg = wb["ple_w_proj"], wb["ple_w_gate"].reshape(D_MODEL, D_MODEL)

    def qk_fwd(qt, kt, ct, st, qg, kg):
        return _norm_rope(qt, qg, ct, st), _norm_rope(kt, kg, ct, st)

    qk_rows = [(z_a, 0, ATTN_DIM), (z_a, 1, ATTN_DIM), (cos2, 0, HEAD), (sin2, 0, HEAD)]
    q_rot, k_rot = _rows_call(qk_fwd, qk_rows, [vec["q_norm"], vec["k_norm"]], [(ATTN_DIM, BF16)] * 2, tm=512, name="attn_pre")
    def group(t, g, off=0):
        return t[:, off + g * GROUP_DIM:off + (g + 1) * GROUP_DIM].astype(BF16)

    qkv = [(group(q_rot, g), group(k_rot, g), group(z_a, g, 2 * ATTN_DIM)) for g in range(len(ATTN_DILATIONS))]
    outs, lses = zip(*[_attn_fwd(*qkv[g], d) for g, d in enumerate(ATTN_DILATIONS)])
    comb_rows = [(t, 0, GROUP_DIM) for t in outs + lses]
    y_attn = _rows_call(_attn_combine, comb_rows, [], [(GROUP_DIM, BF16)], tm=512, name="attn_combine")[0]

    br = _mm(y_rwkv, w_brr, name="branch_rwkv")
    ba = _mm(y_attn, w_bra, name="branch_attn")
    merge_rows = [(z_g, 0, D_MODEL), (z_g, 1, D_MODEL), (br, 0, D_MODEL), (ba, 0, D_MODEL)]
    merged = _rows_call(_merge, merge_rows, [], [(D_MODEL, BF16)], tm=512, name="merge")[0]
    x2 = _mm(merged, w_o, res=x1, name="out_proj")
    x3, ffn2_saved = _ffn_fwd(x2, vec["ffn2_norm"], wb["ffn2_w_gate"], wb["ffn2_w_up"], wb["ffn2_w_down"], "ffn2")
    hp = _rows_call(_rms, [(x3, 0, D_MODEL)], [vec["ple_norm"]], [(D_MODEL, BF16)], tm=512, name="ple_norm")[0]
    pg = _mm(hp, w_pg, name="ple_gate")
    pp = _mm(p_tok, w_pp, name="ple_proj")

    def head(x3t, pgt, ppt, tt):
        sg = _sigmoid(pgt)
        err = x3t + sg * ppt - tt
        dx4 = err * (1.0 / D_MODEL)
        loss = 0.5 * jnp.sum(jnp.mean(err * err, axis=-1, keepdims=True), axis=0, keepdims=True)
        return dx4, dx4 * ppt * sg * (1.0 - sg), dx4 * sg, jnp.broadcast_to(loss, (8, 128))

    head_rows = [(x3, 0, D_MODEL), (pg, 0, D_MODEL), (pp, 0, D_MODEL), (tgt, 0, D_MODEL)]
    dx4, dpg, dpp, loss_tile = _rows_call(head, head_rows, [], [(D_MODEL, F32), (D_MODEL, BF16), (D_MODEL, BF16)], [(8, 128)],
                                          tm=512, name="ple_loss")

    c_arr = jnp.reshape(ci, (1,)).astype(jnp.int32)
    mc_arr = jnp.stack([me, ci]).astype(jnp.int32)
    red = {g: _GroupReduce(g, grp, c_arr, mc_arr) for g, grp in REDUCE_GROUPS.items()}

    done = {}

    def update(summed):
        for n, g2d in summed.items():
            if n == "lora":
                w_, m_, v_ = (_lora_stack({k: t[k][0] for k, _ in LORA}) for t in (wts, mom_m, mom_v))
            else:
                w_, m_, v_ = laid(wts, n), laid(mom_m, n), laid(mom_v, n)
            done[n] = _adamw(w_, g2d, m_, v_, name=f"adamw_{n}", with_grad=True)
    gw, gs = {}, {}
    gw["ple_w_proj"] = _mm(p_tok, dpp, ta=True, name="ple_proj_dw")
    gw["ple_w_gate"] = _mm(hp, dpg, ta=True, name="ple_gate_dw")
    dx3, dx3_bf, gs["ple_norm"] = _mm(dpg, w_pg, tb=True, post=_norm_bwd_post(x3, vec["ple_norm"], dx4), name="ple_gate_dx")
    dx2, dx2_bf, gs["ffn2_norm"], gw["ffn2_w_gate"], gw["ffn2_w_up"], gw["ffn2_w_down"] = _ffn_bwd(
        dx3, dx3_bf, ffn2_saved, vec["ffn2_norm"], wb["ffn2_w_gate"], wb["ffn2_w_up"], wb["ffn2_w_down"], "ffn2")
    gw["ple_w_gate"] = gw["ple_w_gate"].reshape(N_CHIPS, D_MODEL // N_CHIPS, D_MODEL)
    tok = red["f2"].swap_start([gw[w[0]] for w in REDUCE_GROUPS["f2"]])
    gw["w_out"] = _mm(merged, dx2_bf, ta=True, name="out_proj_dw")
    dmerged = _mm(dx2_bf, w_o, tb=True, dep=tok, name="out_proj_dx")

    def merge_bwd(zgr, zga, brt, bat, ct):
        _, vjp = jax.vjp(_merge, zgr, zga, brt, bat)
        d1, d2, d3, d4 = vjp(ct)
        return jnp.concatenate([d1, d2], axis=1), d3, d4

    dz_g, dbr, dba = _rows_call(merge_bwd, merge_rows + [(dmerged, 0, D_MODEL)], [],
                                [(2 * D_MODEL, BF16), (D_MODEL, BF16), (D_MODEL, BF16)], tm=512, name="merge_bwd")
    tok = red["f2"].swap_wait_ici_start(dz_g)
    gw["w_br_rwkv"] = _mm(y_rwkv, dbr, ta=True, name="branch_rwkv_dw")
    gw["w_br_attn"] = _mm(y_attn, dba, ta=True, name="branch_attn_dw")
    dy_rwkv = _mm(dbr, w_brr, tb=True, dep=tok, name="branch_rwkv_dx")
    dy_attn = _mm(dba, w_bra, tb=True, dep=tok, name="branch_attn_dx")

    def comb_bwd(*t):
        _, vjp = jax.vjp(_attn_combine, *t[:6])
        return vjp(t[6])

    dcomb = _rows_call(comb_bwd, comb_rows + [(dy_attn, 0, GROUP_DIM)], [], [(GROUP_DIM, F32)] * 6, tm=512, name="attn_combine_bwd")
    dqs, dks, dvs = zip(*[_attn_bwd(*qkv[g], d, dcomb[g], dcomb[3 + g]) for g, d in enumerate(ATTN_DILATIONS)])

    def qk_bwd(qt, kt, ct, st, *rest):
        dq = jnp.concatenate(rest[0:3], axis=1)
        dk = jnp.concatenate(rest[3:6], axis=1)
        qg, kg = rest[9], rest[10]
        _, vjp = jax.vjp(lambda a_, b_, c_, d_: qk_fwd(a_, b_, ct, st, c_, d_), qt, kt, qg, kg)
        dqt, dkt, dqg, dkg = vjp((dq, dk))
        return jnp.concatenate((dqt, dkt) + tuple(rest[6:9]), axis=1), dqg, dkg

    dz_a, gs["q_norm"], gs["k_norm"] = _rows_call(
        qk_bwd, qk_rows + [(t, 0, GROUP_DIM) for t in dqs + dks + dvs], [vec["q_norm"], vec["k_norm"]],
        [(ATTN_COLS, BF16)], [(1, HEAD), (1, HEAD)], tm=512, name="attn_pre_bwd")
    tok = red["f2"].ici_wait_join_start(dz_a)

    def post_bwd(*t):
        _, vjp = jax.vjp(_rwkv_post, *t[:5], *t[6:])
        return vjp(t[5])

    dy_scan, dr_post, dk2_post, dv_post, dgate_r, gs["rwkv_gn_w"], gs["rwkv_gn_b"], gs["rwkv_r_k"] = _rows_call(
        post_bwd, post_rows + [(dy_rwkv, 0, RWKV_DIM)], post_params, [(RWKV_DIM, F32)] * 5, [(1, RWKV_DIM)] * 3,
        tm=512, name="rwkv_post_bwd", dep=tok)
    update(red["f2"].join_wait(dy_scan))
    dr_s, dlw, dk2_s, dv_s, dna, dkb = _wkv_bwd(zs, lw, k2, na, kb, s0s, invs, dy_scan)

    def pre_bwd(zt, c_r1, c_r2, c_lw, c_k1, c_k2, c_v1, c_v2, c_a, c_b, c_g, *params):
        _, vjp = jax.vjp(_rwkv_pre, zt, *params)
        return vjp((c_r1 + c_r2, c_lw, c_k1 + c_k2, c_v1 + c_v2, c_a, c_b, c_g))

    pre_cts = [dr_s, dr_post, dlw, dk2_s, dk2_post, dv_s, dv_post, dna, dkb, dgate_r]
    dzs, gs["rwkv_w0"], g_w2, gs["rwkv_a0"], g_a2, g_g2, gs["rwkv_k_k"], gs["rwkv_k_a"] = _rows_call(
        pre_bwd, [(zs, 0, RWKV_COLS)] + [(t, 0, RWKV_DIM) for t in pre_cts], pre_params, [(RWKV_COLS, F32)],
        [q.shape for q in pre_params], tm=512, name="rwkv_pre_bwd")
    dz_r, gs["rwkv_mu"] = _shift_bwd(z_r, vec["rwkv_mu"], dzs)

    g_w_in = jnp.concatenate([_mm(h, dz_r, ta=True, name="in_rwkv_dw"), _mm(h, dz_a, ta=True, name="in_attn_dw"),
                              _mm(h, dz_g, ta=True, name="in_gate_dw")], axis=1)
    gw["w_in"], gw["lora"] = g_w_in, jnp.concatenate([g_w2, g_a2, g_g2], axis=0)
    gw["w_out"] = gw["w_out"].reshape(N_CHIPS, D_MODEL // N_CHIPS, D_MODEL)
    tok = red["mx"].swap_start([gw[w[0]] for w in REDUCE_GROUPS["mx"]])
    dh = _mm(dz_r, w_in_r, tb=True, dep=tok, name="in_rwkv_dx")
    dh = _mm(dz_a, w_in_a, tb=True, res=dh, name="in_attn_dx")
    dx1, dx1_bf, gs["mix_norm"] = _mm(dz_g, w_in_g, tb=True, res=dh, post=_norm_bwd_post(x1, vec["mix_norm"], dx2), name="in_gate_dx")
    tok_mx = red["mx"].swap_wait_ici_start(dx1_bf)
    hooks = {"down": lambda d_wd: red["f1d"].swap_start([d_wd], after=tok_mx),
             "mid": lambda dgate: red["f1d"].swap_wait_ici_start(dgate),
             "dw": lambda d_wgt, d_wut: red["f1g"].swap_start([d_wgt, d_wut]),
             "dx": lambda part: red["f1g"].swap_wait_ici_start(part) + red["f1d"].ici_wait_join_start(part),
             "end": lambda dx_: tokens.setdefault("mx_join", red["mx"].ici_wait_join_start(dx_))}
    tokens = {}
    dx0, _, gs["ffn1_norm"], gw["ffn1_w_gate"], gw["ffn1_w_up"], gw["ffn1_w_down"] = _ffn_bwd(
        dx1, dx1_bf, ffn1_saved, vec["ffn1_norm"], wb["ffn1_w_gate"], wb["ffn1_w_up"], wb["ffn1_w_down"], "ffn1", hooks=hooks)

    flat = jnp.concatenate([gs[n].reshape(-1) for n, _ in SMALL] + [loss_tile[0, 0:1]])
    small_buf = jnp.pad(flat, (0, SMALL_ROWS * PACK_COLS - flat.shape[0])).reshape(SMALL_ROWS, PACK_COLS)
    small_issue, small_expect = _small_gather_plan()
    ss_s, rs_s, small_bufs, tok_s = _copies_start("small_gather", [small_buf, lax.empty((8, SMALL_ROWS, PACK_COLS), F32)], 7,
                                                  small_issue, after=tokens["mx_join"])
    for g in ("mx", "f1d"):
        update(red[g].join_wait(tok_s))
    tok = red["f1g"].ici_wait_join_start(done["w_in"][1])
    small_buf, slots = _copies_wait("small_gather_wait", small_bufs, ss_s, rs_s, done["ffn1_w_down"][1], small_expect)
    small_sum = _sum_slots(lax.dynamic_update_slice(slots, small_buf[None], (4 * xi + 2 * yi + ci, 0, 0)))
    n_small = sum(sz for _, sz in SMALL)
    loss = small_sum.reshape(-1)[n_small]
    grad_small = _unpack_small(small_sum, {n: wts[n].shape for n, _ in SMALL})
    d_s, m_s, v_s = _adamw(_pack_small(wts), small_sum, _pack_small(mom_m), _pack_small(mom_v), name="adamw_small", dep=tok)
    shapes = {n: wts[n].shape for n, _ in SMALL}
    d_s, m_s, v_s = _unpack_small(d_s, shapes), _unpack_small(m_s, shapes), _unpack_small(v_s, shapes)
    grads, deltas, new_m, new_v = {}, {}, {}, {}
    for n, _ in SMALL:
        grads[n], deltas[n], new_m[n], new_v[n] = grad_small[n], d_s[n], m_s[n], v_s[n]
    update(red["f1g"].join_wait(m_s["ffn1_norm"]))
    for n, res in done.items():
        for store, val in zip((grads, deltas, new_m, new_v), res):
            if n == "lora":
                store.update({k: t[None] for k, t in _lora_split(val).items()})
            else:
                store[n] = (jnp.transpose(val) if n in TRANSPOSED else val)[None]

    return (loss, dx0[None], *[grads[n] for n in WEIGHTS], *[deltas[n] for n in WEIGHTS],
            *[new_m[n] for n in WEIGHTS], *[new_v[n] for n in WEIGHTS])
```

```python
import functools

import jax
import jax.numpy as jnp
from jax import lax
from jax.experimental import pallas as pl
from jax.experimental.pallas import tpu as pltpu

F32, BF16 = jnp.float32, jnp.bfloat16
HI = lax.Precision.HIGHEST
MESH = pl.DeviceIdType.MESH
SDS = jax.ShapeDtypeStruct

D_MODEL = 1024
HEAD = 64
RWKV_HEADS = 8
RWKV_DIM = RWKV_HEADS * HEAD
DECAY_LORA, ICLR_LORA, GATE_LORA = 64, 64, 128
GN_EPS = 64e-5
RMS_EPS = 1e-6
ATTN_DILATIONS = (1, 4, 16)
BAND = 128
ATTN_DIM = 768
GROUP_DIM = 256
ATTN_CLASSES_PER_STEP = 4
ROPE_THETA = 10000.0
NEG_INF = -1e30
RWKV_COLS = 3 * RWKV_DIM + DECAY_LORA + ICLR_LORA + GATE_LORA
ATTN_COLS = 3 * ATTN_DIM
ADAM_LR, ADAM_B1, ADAM_B2, ADAM_EPS, ADAM_WD, ADAM_STEP = 0.001, 0.9, 0.999, 1e-08, 0.01, 10

WKV_CHUNK = 64
WKV_HEADS_PER_STEP = 8
WKV_CHUNKS_PER_STEP = 4
N_CHIPS = 4
PACK_COLS = 1024
VMEM_LIMIT = 48 * 1024 * 1024

TRANSPOSED = ("ffn1_w_gate", "ffn1_w_up", "ffn2_w_gate", "ffn2_w_up")
LORA = (("rwkv_w2", 64), ("rwkv_a2", 64), ("rwkv_g2", 128))
_FFN1 = (("ffn1_w_gate", "blk", 704, 1024), ("ffn1_w_up", "blk", 704, 1024), ("ffn1_w_down", "blk", 704, 1024))
_FFN2 = (("ffn2_w_gate", "blk", 704, 1024), ("ffn2_w_up", "blk", 704, 1024), ("ffn2_w_down", "blk", 704, 1024))
_IN = (("w_in", "col", 1024, 1536), ("lora", "col", 256, 128))
_BRANCH = (("w_br_rwkv", "col", 512, 256), ("w_br_attn", "col", 256, 256), ("w_out", "blk", 256, 1024))
_PLE = (("ple_w_gate", "blk", 256, 1024), ("ple_w_proj", "col", 256, 256))
GROUPS = {"f1": _FFN1, "mx": _IN, "f2": _BRANCH + _FFN2 + _PLE}
REDUCE_GROUPS = {"f2": _FFN2 + _PLE, "mx": _IN + _BRANCH, "f1d": _FFN1[2:], "f1g": _FFN1[:2]}
SMALL = (
    ("ffn1_norm", 1024), ("mix_norm", 1024), ("ffn2_norm", 1024), ("ple_norm", 1024), ("rwkv_mu", 1792),
    ("rwkv_w0", 512), ("rwkv_a0", 512), ("rwkv_k_k", 512), ("rwkv_k_a", 512), ("rwkv_r_k", 512),
    ("rwkv_gn_w", 512), ("rwkv_gn_b", 512), ("q_norm", 64), ("k_norm", 64),
)
SMALL_ROWS = 16
WEIGHTS = (
    "ffn1_norm", "ffn1_w_gate", "ffn1_w_up", "ffn1_w_down", "mix_norm", "w_in", "rwkv_mu", "rwkv_w0", "rwkv_w2",
    "rwkv_a0", "rwkv_a2", "rwkv_g2", "rwkv_k_k", "rwkv_k_a", "rwkv_r_k", "rwkv_gn_w", "rwkv_gn_b", "q_norm", "k_norm",
    "w_br_rwkv", "w_br_attn", "w_out", "ffn2_norm", "ffn2_w_gate", "ffn2_w_up", "ffn2_w_down", "ple_norm",
    "ple_w_gate", "ple_w_proj",
)


def _row_tile(n, most=704):
    for t in range(most - most % 16, 0, -16):
        if n % t == 0:
            return t
    return n


def _pick(n, cands):
    for c in cands:
        if n % c == 0:
            return c
    return n


def _mm(a, b, *, ta=False, tb=False, sum_blocks=False, out_dtype=F32, res=None, alpha=1.0, dep=None, post=None, name):
    flat = a.ndim == 2 and b.ndim == 2
    a3 = a if a.ndim == 3 else a[None]
    b3 = b if b.ndim == 3 else b[None]
    na, nbb = a3.shape[0], b3.shape[0]
    nblk = max(na, nbb)
    kdim, m = (a3.shape[1], a3.shape[2]) if ta else (a3.shape[2], a3.shape[1])
    n = b3.shape[1] if tb else b3.shape[2]
    assert (b3.shape[2] if tb else b3.shape[1]) == kdim
    tm = _pick(m, (1024, 512, 256, 128) if post is None else (512, 256, 128))
    tn = _pick(n, (1024, 896, 768, 512, 256, 128))
    tk = kdim if kdim <= 2304 else _pick(kdim, (1024, 512, 256, 128))
    nk = kdim // tk
    direct = nk == 1 and not sum_blocks

    if sum_blocks:
        grid = (m // tm, n // tn, nblk, nk)

        def ids(i, c, j, k):
            return i, c, j, k
    else:
        grid = (nblk, m // tm, n // tn, nk)

        def ids(j, i, c, k):
            return i, c, j, k

    def amap(*g):
        i, c, j, k = ids(*g)
        jj = j if na > 1 else 0
        return (jj, k, i) if ta else (jj, i, k)

    def bmap(*g):
        i, c, j, k = ids(*g)
        jj = j if nbb > 1 else 0
        return (jj, c, k) if tb else (jj, k, c)

    if sum_blocks:
        oshape, oblk = (m, n), (tm, tn)

        def omap(*g):
            i, c, j, k = ids(*g)
            return i, c
    else:
        oshape, oblk = (nblk, m, n), (1, tm, tn)

        def omap(*g):
            i, c, j, k = ids(*g)
            return j, i, c

    dn = (((0 if ta else 1,), (1 if tb else 0,)), ((), ()))
    has_res = res is not None
    p_f, p_rows, p_params, p_dtypes, p_accs = post if post is not None else (None, [], [], [], [])
    assert post is None or sum_blocks or flat
    n_in = 2 + has_res + len(p_rows) + len(p_params) + (dep is not None)

    def tile_map(*g):
        i, c, j, k = ids(*g)
        return i, c

    def body(*refs):
        refs = list(refs)
        acc = None if direct else refs.pop()
        o_refs = refs[n_in:]
        a_ref, b_ref = refs[0], refs[1]
        r_ref = refs[2] if has_res else None
        pr_refs = refs[2 + has_res:2 + has_res + len(p_rows)]
        pp_refs = refs[2 + has_res + len(p_rows):2 + has_res + len(p_rows) + len(p_params)]
        first_tile = jnp.logical_and(pl.program_id(0 if sum_blocks else 1) == 0, pl.program_id(1 if sum_blocks else 2) == 0)

        def finish(v):
            if alpha != 1.0:
                v = v * alpha
            if has_res:
                v = v + r_ref[...].reshape(v.shape).astype(F32)
            if post is None:
                o_refs[0][...] = v.reshape(o_refs[0].shape).astype(o_refs[0].dtype)
                return
            outs = p_f(v, *[t[...] for t in pr_refs], *[t[...] for t in pp_refs])
            for o_ref, val in zip(o_refs, outs[:len(p_dtypes)]):
                o_ref[...] = val.astype(o_ref.dtype)
            for o_ref, val in zip(o_refs[len(p_dtypes):], outs[len(p_dtypes):]):
                @pl.when(first_tile)
                def _():
                    o_ref[...] = jnp.zeros_like(o_ref)

                o_ref[...] += val.reshape(o_ref.shape)

        if direct:
            finish(lax.dot_general(a_ref[0].astype(BF16), b_ref[0].astype(BF16), dn, preferred_element_type=F32))
            return
        k = pl.program_id(3)
        if sum_blocks:
            j = pl.program_id(2)
            first = jnp.logical_and(j == 0, k == 0)
            last = jnp.logical_and(j == nblk - 1, k == nk - 1)
        else:
            first, last = k == 0, k == nk - 1

        @pl.when(first)
        def _():
            acc[...] = jnp.zeros_like(acc)

        acc[...] += lax.dot_general(a_ref[0].astype(BF16), b_ref[0].astype(BF16), dn, preferred_element_type=F32)

        @pl.when(last)
        def _():
            finish(acc[...])

    in_specs = [pl.BlockSpec((1, tk, tm) if ta else (1, tm, tk), amap), pl.BlockSpec((1, tn, tk) if tb else (1, tk, tn), bmap)]
    args = [a3, b3]
    if has_res:
        res3 = res if (sum_blocks or res.ndim == 3) else res[None]
        in_specs.append(pl.BlockSpec(oblk, omap))
        args.append(res3)
    in_specs += [pl.BlockSpec((tm, tn), tile_map) for _ in p_rows]
    in_specs += [pl.BlockSpec(t.shape, functools.partial(lambda *g, nd: (0,) * nd, nd=t.ndim)) for t in p_params]
    args += list(p_rows) + list(p_params)
    if dep is not None:
        in_specs.append(pl.BlockSpec(memory_space=pl.ANY))
        args.append(dep)
    if post is None:
        out_specs, out_shape = pl.BlockSpec(oblk, omap), SDS(oshape, out_dtype)
        semantics = ("parallel", "parallel", "arbitrary", "arbitrary") if sum_blocks else ("parallel", "parallel", "parallel", "arbitrary")
    else:
        out_specs = [pl.BlockSpec((tm, tn), tile_map) for _ in p_dtypes]
        out_specs += [pl.BlockSpec(tuple(sh), functools.partial(lambda *g, nd: (0,) * nd, nd=len(sh))) for sh in p_accs]
        out_shape = [SDS((m, n), dt) for dt in p_dtypes] + [SDS(tuple(sh), F32) for sh in p_accs]
        semantics = ("arbitrary",) * 4
    out = pl.pallas_call(
        body,
        name=name,
        grid=grid,
        in_specs=in_specs,
        out_specs=out_specs,
        out_shape=out_shape,
        scratch_shapes=[] if direct else [pltpu.VMEM((tm, tn), F32)],
        compiler_params=pltpu.CompilerParams(dimension_semantics=semantics, vmem_limit_bytes=VMEM_LIMIT),
    )(*args)
    if post is not None:
        return out
    if flat and not sum_blocks:
        out = out[0]
    return out


def _rows_call(f, rows, params, outs, accs=(), *, tm, name, dep=None):
    s = rows[0][0].shape[0]
    nr, npar, no = len(rows), len(params), len(outs)
    nin = nr + npar + (0 if dep is None else 1)
    in_specs = [pl.BlockSpec((tm, w), functools.partial(lambda i, cb: (i, cb), cb=cb)) for (_, cb, w) in rows]
    in_specs += [pl.BlockSpec(p.shape, functools.partial(lambda i, nd: (0,) * nd, nd=p.ndim)) for p in params]
    if dep is not None:
        in_specs.append(pl.BlockSpec(memory_space=pl.ANY))
    out_shape = [SDS((s, w), dt) for (w, dt) in outs] + [SDS(tuple(sh), F32) for sh in accs]
    out_specs = [pl.BlockSpec((tm, w), lambda i: (i, 0)) for (w, _) in outs]
    out_specs += [pl.BlockSpec(tuple(sh), functools.partial(lambda i, nd: (0,) * nd, nd=len(sh))) for sh in accs]

    def body(*refs):
        rin, pin = refs[:nr], refs[nr:nr + npar]
        oo, ao = refs[nin:nin + no], refs[nin + no:]
        res = f(*[r[...] for r in rin], *[p[...] for p in pin])
        if not isinstance(res, (tuple, list)):
            res = (res,)
        for o_ref, v in zip(oo, res[:no]):
            o_ref[...] = v.astype(o_ref.dtype)
        i = pl.program_id(0)
        for a_ref, v in zip(ao, res[no:]):
            @pl.when(i == 0)
            def _():
                a_ref[...] = jnp.zeros_like(a_ref)

            a_ref[...] += v.reshape(a_ref.shape)

    res = pl.pallas_call(
        body,
        name=name,
        grid=(s // tm,),
        in_specs=in_specs,
        out_specs=out_specs,
        out_shape=out_shape,
        compiler_params=pltpu.CompilerParams(dimension_semantics=("arbitrary",), vmem_limit_bytes=VMEM_LIMIT),
    )(*[r[0] for r in rows], *params, *([] if dep is None else [dep]))
    return res


def _mmv(a, b, mode):
    ca = 0 if mode[0] == "t" else 1
    cb = 1 if mode[1] == "t" else 0
    return lax.dot_general(a.astype(BF16), b.astype(BF16), (((ca,), (cb,)), ((), ())), preferred_element_type=F32)


@functools.partial(jax.custom_vjp, nondiff_argnums=(2,))
def _bdot(a, b, mode):
    return _mmv(a, b, mode)


def _bdot_fwd(a, b, mode):
    return _mmv(a, b, mode), (a, b)


def _bdot_bwd(mode, saved, g):
    a, b = saved
    if mode == "nn":
        return _mmv(g, b, "nt"), _mmv(a, g, "tn")
    if mode == "nt":
        return _mmv(g, b, "nn"), _mmv(g, a, "tn")
    return _mmv(b, g, "nt"), _mmv(a, g, "nn")


_bdot.defvjp(_bdot_fwd, _bdot_bwd)


def _hdot(a, b, mode="nn", precision=HI):
    ca = 0 if mode[0] == "t" else 1
    cb = 1 if mode[1] == "t" else 0
    return lax.dot_general(a, b, (((ca,), (cb,)), ((), ())), precision=precision, preferred_element_type=F32)


def _segsum(x):
    c = x.shape[-1]
    blk = min(c, 256)
    r = lax.broadcasted_iota(jnp.int32, (blk, blk), 0) >> 6
    q = lax.broadcasted_iota(jnp.int32, (blk, blk), 1) >> 6
    ones = jnp.where(r == q, 1.0, 0.0).astype(F32)
    parts = [_hdot(x[:, i:i + blk], ones, precision=lax.Precision.HIGH) for i in range(0, c, blk)]
    return parts[0] if len(parts) == 1 else jnp.concatenate(parts, axis=1)


def _sigmoid(x):
    return jax.nn.sigmoid(x)


def _softplus(x):
    return jnp.maximum(x, 0.0) + jnp.log(1.0 + jnp.exp(-jnp.abs(x)))


def _rms(x, gain):
    return x * lax.rsqrt(jnp.mean(x * x, axis=-1, keepdims=True) + RMS_EPS) * gain


def _swiglu_act(gate, up):
    return gate * _sigmoid(gate) * up


def _rwkv_pre(zs, w0, w2, a0, a2, g2, k_k, k_a):
    r, k, v = zs[:, 0:512], zs[:, 512:1024], zs[:, 1024:1536]
    lora = zs[:, 1536:1792]
    wd, ad, gd = lora[:, 0:64], lora[:, 64:128], lora[:, 128:256]
    w = -_softplus(-(w0 + _bdot(jnp.tanh(wd), w2, "nn"))) - 0.5
    a = _sigmoid(a0 + _bdot(ad, a2, "nn"))
    g = _bdot(_sigmoid(gd), g2, "nn")
    kk = k * k_k
    kk = kk * lax.rsqrt(jnp.maximum(_segsum(kk * kk), 1e-24))
    k2 = k * (1.0 + (a - 1.0) * k_a)
    return r, -jnp.exp(w), k2, v, -kk, kk * a, g


def _rwkv_post(y, r, k2, v, g, gn_w, gn_b, r_k):
    mean = _segsum(y) * (1.0 / HEAD)
    yc = y - mean
    var = _segsum(yc * yc) * (1.0 / HEAD)
    yn = yc * lax.rsqrt(var + GN_EPS) * gn_w + gn_b
    bonus = _segsum(r * k2 * r_k) * v
    return (yn + bonus) * g


def _swap_halves(x):
    lane = lax.broadcasted_iota(jnp.int32, x.shape, 1)
    return jnp.where((lane & 32) == 0, jnp.roll(x, -32, axis=1), jnp.roll(x, 32, axis=1))


def _norm_rope(x, gain, cos, sin):
    heads = x.shape[1] // HEAD
    def rep(t):
        return jnp.concatenate([t] * heads, axis=1)

    xn = x * lax.rsqrt(_segsum(x * x) * (1.0 / HEAD) + RMS_EPS) * rep(gain)
    return xn * rep(cos) + _swap_halves(xn) * rep(sin)


def _attn_combine(o0, o1, o2, l0, l1, l2):
    m = jnp.maximum(jnp.maximum(l0, l1), l2)
    e0, e1, e2 = jnp.exp(l0 - m), jnp.exp(l1 - m), jnp.exp(l2 - m)
    return (e0 * o0 + e1 * o1 + e2 * o2) / (e0 + e1 + e2)


def _merge(zgr, zga, br, ba):
    return _sigmoid(zgr) * br + _sigmoid(zga) * ba


def _attn_block(q, kp, kc, vp, vc, has_prev):
    iq = lax.broadcasted_iota(jnp.int32, (1, BAND, BAND), 1)
    ik = lax.broadcasted_iota(jnp.int32, (1, BAND, BAND), 2)
    s_c = jnp.where(iq >= ik, _bdotb(q, kc, "nt") * (HEAD ** -0.5), NEG_INF)
    s_p = jnp.where(jnp.logical_and(iq <= ik, has_prev), _bdotb(q, kp, "nt") * (HEAD ** -0.5), NEG_INF)
    m = lax.stop_gradient(jnp.maximum(jnp.max(s_c, axis=-1, keepdims=True), jnp.max(s_p, axis=-1, keepdims=True)))
    e_c, e_p = jnp.exp(s_c - m), jnp.exp(s_p - m)
    l = jnp.sum(e_c, axis=-1, keepdims=True) + jnp.sum(e_p, axis=-1, keepdims=True)
    o = (_bdotb(e_c, vc) + _bdotb(e_p, vp)) / l
    return o, jnp.broadcast_to(m + jnp.log(l), o.shape)


def _mmb(a, b, cb):
    return lax.dot_general(a.astype(BF16), b.astype(BF16), (((2,), (cb,)), ((0,), (0,))), preferred_element_type=F32)


@functools.partial(jax.custom_vjp, nondiff_argnums=(2,))
def _bdotb1(a, b, cb):
    return _mmb(a, b, cb)


def _bdotb1_fwd(a, b, cb):
    return _mmb(a, b, cb), (a, b)


def _bdotb1_bwd(cb, saved, g):
    a, b = saved
    if cb == 1:
        return _mmb(g, b, 2), _mmb(jnp.swapaxes(a, 1, 2), g, 1)
    return _mmb(g, b, 1), _mmb(jnp.swapaxes(g, 1, 2), a, 1)


_bdotb1.defvjp(_bdotb1_fwd, _bdotb1_bwd)


def _bdotb(a, b, mode="nn", precision=None):
    if mode[0] == "t":
        a = jnp.swapaxes(a, 1, 2)
    cb = 2 if mode[1] == "t" else 1
    if precision is None:
        return _bdotb1(a, b, cb)
    return lax.dot_general(a, b, (((2,), (cb,)), ((0,), (0,))), precision=precision, preferred_element_type=F32)


def _tri_inv_levels(a):
    t = a.shape[-1]
    row = lax.broadcasted_iota(jnp.int32, (1, t, t), 1)
    col = lax.broadcasted_iota(jnp.int32, (1, t, t), 2)
    x = jnp.where(row == col, 1.0, 0.0).astype(F32) + jnp.where(jnp.logical_and(row == col + 1, (row & 1) == 1), a, 0.0)
    sh = 1
    while (1 << sh) < t:
        m = jnp.logical_and((row >> sh) == (col >> sh) + 1, (row >> (sh + 1)) == (col >> (sh + 1)))
        x = x + _bdotb(_bdotb(x, jnp.where(m, a, 0.0)), x)
        sh += 1
    return x


@jax.custom_vjp
def _tri_inv(a):
    return _tri_inv_levels(a)


def _tri_inv_fwd(a):
    x = _tri_inv_levels(a)
    return x, x


def _tri_inv_bwd(x, g):
    xt = jnp.swapaxes(x, 1, 2)
    return (_bdotb(_bdotb(xt, g, precision=lax.Precision.HIGH), xt, precision=lax.Precision.HIGH),)


_tri_inv.defvjp(_tri_inv_fwd, _tri_inv_bwd)


@jax.custom_vjp
def _known_inv(a, x):
    return x


def _known_inv_fwd(a, x):
    return x, x


def _known_inv_bwd(x, g):
    return _tri_inv_bwd(x, g)[0], jnp.zeros_like(x)


_known_inv.defvjp(_known_inv_fwd, _known_inv_bwd)


def _wkv_chunk(s0, r, lw, k, v, a, b, inv=None, with_inv=False):
    nh = r.shape[0]
    t = WKV_CHUNK
    n = r.shape[1] // t

    def chunked(x):
        return x if n == 1 else jnp.concatenate([x[:, c * t:(c + 1) * t] for c in range(n)], axis=0)

    r, lw, k, v, a, b = (chunked(x) for x in (r, lw, k, v, a, b))
    row = lax.broadcasted_iota(jnp.int32, (1, t, t), 1)
    col = lax.broadcasted_iota(jnp.int32, (1, t, t), 2)
    incl, strict = row >= col, row > col
    ones = jnp.broadcast_to(jnp.where(incl, 1.0, 0.0).astype(F32), (n * nh, t, t))
    cum = _bdotb(ones, lw, precision=HI)
    c_end = cum[:, t - 1:t, :]
    e_in, e_ex, e_inv = jnp.exp(cum), jnp.exp(cum - lw), jnp.exp(-cum)
    at, rt, bt, kt = a * e_ex, r * e_in, b * e_inv, k * e_inv
    a_ab = jnp.where(strict, _bdotb(at, bt, "nt"), 0.0)
    a_ak = jnp.where(strict, _bdotb(at, kt, "nt"), 0.0)
    x = _tri_inv(a_ab) if inv is None else _known_inv(a_ab, inv)
    r_b = jnp.where(incl, _bdotb(rt, bt, "nt"), 0.0)
    akv = _bdotb(a_ak, v)
    rkv = _bdotb(jnp.where(incl, _bdotb(rt, kt, "nt"), 0.0), v)
    w_end = jnp.exp(c_end - cum)
    bw, kw, decay = b * w_end, k * w_end, jnp.exp(c_end)
    ys = []
    for c in range(n):
        hs = slice(c * nh, (c + 1) * nh)
        u = _bdotb(x[hs], _bdotb(at[hs], s0, "nt") + akv[hs])
        ys.append(_bdotb(rt[hs], s0, "nt") + _bdotb(r_b[hs], u) + rkv[hs])
        s0 = s0 * decay[hs] + _bdotb(u, bw[hs], "tn") + _bdotb(v[hs], kw[hs], "tn")
    y = ys[0] if n == 1 else jnp.concatenate(ys, axis=1)
    return (y, s0, x) if with_inv else (y, s0)


def _shift_fwd(z, mu):
    s, c = z.shape
    tc = 256

    def body(z_ref, mu_ref, o_ref):
        zz = z_ref[...]
        row = lax.broadcasted_iota(jnp.int32, zz.shape, 0)
        prev = jnp.where(row == 0, 0.0, pltpu.roll(zz, 1, 0))
        o_ref[...] = zz + (prev - zz) * mu_ref[...]

    return pl.pallas_call(
        body, name="shift_fwd", grid=(c // tc,),
        in_specs=[pl.BlockSpec((s, tc), lambda j: (0, j)), pl.BlockSpec((1, tc), lambda j: (0, j))],
        out_specs=pl.BlockSpec((s, tc), lambda j: (0, j)), out_shape=SDS((s, c), F32),
        compiler_params=pltpu.CompilerParams(dimension_semantics=("parallel",), vmem_limit_bytes=VMEM_LIMIT),
    )(z, mu)


def _shift_bwd(z, mu, dzs):
    s, c = z.shape
    tc = 256

    def body(z_ref, mu_ref, d_ref, dz_ref, dmu_ref):
        zz, d, m = z_ref[...], d_ref[...], mu_ref[...]
        row = lax.broadcasted_iota(jnp.int32, zz.shape, 0)
        prev = jnp.where(row == 0, 0.0, pltpu.roll(zz, 1, 0))
        t = d * m
        nxt = jnp.where(row == s - 1, 0.0, pltpu.roll(t, s - 1, 0))
        dz_ref[...] = (d - t + nxt).astype(dz_ref.dtype)
        dmu_ref[...] = jnp.sum(d * (prev - zz), axis=0, keepdims=True)

    return pl.pallas_call(
        body, name="shift_bwd", grid=(c // tc,),
        in_specs=[pl.BlockSpec((s, tc), lambda j: (0, j)), pl.BlockSpec((1, tc), lambda j: (0, j)),
                  pl.BlockSpec((s, tc), lambda j: (0, j))],
        out_specs=[pl.BlockSpec((s, tc), lambda j: (0, j)), pl.BlockSpec((1, tc), lambda j: (0, j))],
        out_shape=[SDS((s, c), BF16), SDS((1, c), F32)],
        compiler_params=pltpu.CompilerParams(dimension_semantics=("parallel",), vmem_limit_bytes=VMEM_LIMIT),
    )(z, mu, dzs)


def _heads(x, nh):
    return jnp.stack([x[:, h * HEAD:(h + 1) * HEAD] for h in range(nh)], axis=0)


def _unheads(x):
    return jnp.concatenate([x[h] for h in range(x.shape[0])], axis=1)


def _wkv_fwd(zs, lw, k2, na, b):
    s = lw.shape[0]
    t, hb = WKV_CHUNK * WKV_CHUNKS_PER_STEP, WKV_HEADS_PER_STEP
    w = hb * HEAD
    nc, ng = s // t, RWKV_HEADS // hb
    nx = WKV_CHUNKS_PER_STEP * RWKV_HEADS

    def body(r_ref, v_ref, lw_ref, k_ref, a_ref, b_ref, y_ref, s0_ref, x_ref, state):
        @pl.when(pl.program_id(1) == 0)
        def _():
            state[...] = jnp.zeros_like(state)

        s0 = state[...]
        s0_ref[0] = s0
        y, s1, x = _wkv_chunk(s0, *[_heads(t_ref[...], hb) for t_ref in (r_ref, lw_ref, k_ref, v_ref, a_ref, b_ref)], with_inv=True)
        y_ref[...] = _unheads(y)
        x_ref[0] = x
        state[...] = s1

    def col(off):
        return pl.BlockSpec((t, w), functools.partial(lambda g, i, off: (i, g + off), off=off))

    return pl.pallas_call(
        body, name="wkv_fwd", grid=(ng, nc),
        in_specs=[col(0), col(2 * ng), col(0), col(0), col(0), col(0)],
        out_specs=[col(0), pl.BlockSpec((1, hb, HEAD, HEAD), lambda g, i: (i, g, 0, 0)),
                   pl.BlockSpec((1, nx, WKV_CHUNK, WKV_CHUNK), lambda g, i: (i, 0, 0, 0))],
        out_shape=[SDS((s, RWKV_DIM), F32), SDS((nc, RWKV_HEADS, HEAD, HEAD), F32), SDS((nc, nx, WKV_CHUNK, WKV_CHUNK), F32)],
        scratch_shapes=[pltpu.VMEM((hb, HEAD, HEAD), F32)],
        compiler_params=pltpu.CompilerParams(dimension_semantics=("parallel", "arbitrary"), vmem_limit_bytes=VMEM_LIMIT),
    )(zs, zs, lw, k2, na, b)


def _wkv_bwd(zs, lw, k2, na, b, s0s, invs, dy):
    s = lw.shape[0]
    t, hb = WKV_CHUNK * WKV_CHUNKS_PER_STEP, WKV_HEADS_PER_STEP
    w = hb * HEAD
    nc, ng = s // t, RWKV_HEADS // hb
    nx = WKV_CHUNKS_PER_STEP * RWKV_HEADS

    def body(r_ref, v_ref, lw_ref, k_ref, a_ref, b_ref, s0_ref, x_ref, dy_ref, dr_ref, dlw_ref, dk_ref, dv_ref, da_ref, db_ref, dstate):
        @pl.when(pl.program_id(1) == 0)
        def _():
            dstate[...] = jnp.zeros_like(dstate)

        _, vjp = jax.vjp(functools.partial(_wkv_chunk, inv=x_ref[0]), s0_ref[0],
                         *[_heads(t_ref[...], hb) for t_ref in (r_ref, lw_ref, k_ref, v_ref, a_ref, b_ref)])
        grads = vjp((_heads(dy_ref[...], hb), dstate[...]))
        dstate[...] = grads[0]
        for o_ref, gval in zip((dr_ref, dlw_ref, dk_ref, dv_ref, da_ref, db_ref), grads[1:]):
            o_ref[...] = _unheads(gval)

    def col(off):
        return pl.BlockSpec((t, w), functools.partial(lambda g, i, off: (nc - 1 - i, g + off), off=off))

    return pl.pallas_call(
        body, name="wkv_bwd", grid=(ng, nc),
        in_specs=[col(0), col(2 * ng), col(0), col(0), col(0), col(0),
                  pl.BlockSpec((1, hb, HEAD, HEAD), lambda g, i: (nc - 1 - i, g, 0, 0)),
                  pl.BlockSpec((1, nx, WKV_CHUNK, WKV_CHUNK), lambda g, i: (nc - 1 - i, 0, 0, 0)), col(0)],
        out_specs=[col(0)] * 6,
        out_shape=[SDS((s, RWKV_DIM), F32)] * 6,
        scratch_shapes=[pltpu.VMEM((hb, HEAD, HEAD), F32)],
        compiler_params=pltpu.CompilerParams(dimension_semantics=("parallel", "arbitrary"), vmem_limit_bytes=VMEM_LIMIT),
    )(zs, zs, lw, k2, na, b, s0s, invs, dy)


def _attn_batch(refs, bps, nh, first_has_prev):
    q_ref, kp_ref, kc_ref, vp_ref, vc_ref = refs

    def blocks(cur_ref, prev_ref=None):
        out = []
        for b in range(bps):
            if prev_ref is None:
                t = cur_ref[b * BAND:(b + 1) * BAND, :]
            else:
                t = prev_ref[...] if b == 0 else cur_ref[(b - 1) * BAND:b * BAND, :]
            out.append(_heads(t.astype(F32), nh))
        return out[0] if bps == 1 else jnp.concatenate(out, axis=0)

    batch = lax.broadcasted_iota(jnp.int32, (bps * nh, 1, 1), 0)
    has_prev = jnp.logical_or(batch >= nh, first_has_prev)
    return (blocks(q_ref), blocks(kc_ref, kp_ref), blocks(kc_ref), blocks(vc_ref, vp_ref), blocks(vc_ref)), has_prev


def _attn_rows(x, bps, nh):
    parts = [_unheads(x[b * nh:(b + 1) * nh]) for b in range(bps)]
    return parts[0] if bps == 1 else jnp.concatenate(parts, axis=0)


def _attn_fwd(q, k, v, d):
    s = q.shape[0]
    l = s // d
    nb = l // BAND
    assert nb * BAND == l
    qv, kv, vv = (t.reshape(l, d * GROUP_DIM) for t in (q, k, v))
    width = min(d, ATTN_CLASSES_PER_STEP) * GROUP_DIM
    bps = ATTN_CLASSES_PER_STEP * GROUP_DIM // width
    nh = width // HEAD

    def body(q_ref, kp_ref, kc_ref, vp_ref, vc_ref, o_ref, l_ref):
        ops, has_prev = _attn_batch((q_ref, kp_ref, kc_ref, vp_ref, vc_ref), bps, nh, pl.program_id(1) > 0)
        o, lse = _attn_block(*ops, has_prev)
        o_ref[...] = _attn_rows(o, bps, nh)
        l_ref[...] = _attn_rows(lse, bps, nh)

    cur = pl.BlockSpec((bps * BAND, width), lambda rho, i: (i, rho))
    prev = pl.BlockSpec((BAND, width), lambda rho, i: (jnp.maximum(i * bps - 1, 0), rho))
    o, lse = pl.pallas_call(
        body, name=f"attn_fwd_d{d}", grid=(d * GROUP_DIM // width, nb // bps),
        in_specs=[cur, prev, cur, prev, cur], out_specs=[cur, cur],
        out_shape=[SDS((l, d * GROUP_DIM), F32), SDS((l, d * GROUP_DIM), F32)],
        compiler_params=pltpu.CompilerParams(dimension_semantics=("parallel", "arbitrary"), vmem_limit_bytes=VMEM_LIMIT),
    )(qv, kv, kv, vv, vv)
    return o.reshape(s, GROUP_DIM), lse.reshape(s, GROUP_DIM)


def _attn_bwd(q, k, v, d, do, dlse):
    s = q.shape[0]
    l = s // d
    nb = l // BAND
    qv, kv, vv, dov, dlv = (t.reshape(l, d * GROUP_DIM) for t in (q, k, v, do, dlse))
    width = min(d, ATTN_CLASSES_PER_STEP) * GROUP_DIM
    bps = ATTN_CLASSES_PER_STEP * GROUP_DIM // width
    nh = width // HEAD
    ns = nb // bps

    def body(q_ref, kp_ref, kc_ref, vp_ref, vc_ref, do_ref, dl_ref, dq_ref, dk_ref, dv_ref, ck, cv):
        step = pl.program_id(1)

        @pl.when(step == 0)
        def _():
            ck[...] = jnp.zeros_like(ck)
            cv[...] = jnp.zeros_like(cv)

        ops, has_prev = _attn_batch((q_ref, kp_ref, kc_ref, vp_ref, vc_ref), bps, nh, step < ns - 1)
        _, vjp = jax.vjp(functools.partial(_attn_block, has_prev=has_prev), *ops)
        cts = [jnp.concatenate([_heads(t_ref[b * BAND:(b + 1) * BAND, :], nh) for b in range(bps)], axis=0) if bps > 1
               else _heads(t_ref[...], nh) for t_ref in (do_ref, dl_ref)]
        dq, dkp, dkc, dvp, dvc = vjp(tuple(cts))
        dq_ref[...] = _attn_rows(dq, bps, nh)
        for out_ref, cur_part, prev_part, carry in ((dk_ref, dkc, dkp, ck), (dv_ref, dvc, dvp, cv)):
            for b in range(bps):
                after = carry[...] if b == bps - 1 else _unheads(prev_part[(b + 1) * nh:(b + 2) * nh])
                out_ref[b * BAND:(b + 1) * BAND, :] = _unheads(cur_part[b * nh:(b + 1) * nh]) + after
            carry[...] = _unheads(prev_part[0:nh])

    cur = pl.BlockSpec((bps * BAND, width), lambda rho, i: (ns - 1 - i, rho))
    prev = pl.BlockSpec((BAND, width), lambda rho, i: (jnp.maximum((ns - 1 - i) * bps - 1, 0), rho))
    dq, dk, dv = pl.pallas_call(
        body, name=f"attn_bwd_d{d}", grid=(d * GROUP_DIM // width, ns),
        in_specs=[cur, prev, cur, prev, cur, cur, cur], out_specs=[cur] * 3,
        out_shape=[SDS((l, d * GROUP_DIM), F32)] * 3,
        scratch_shapes=[pltpu.VMEM((BAND, width), F32), pltpu.VMEM((BAND, width), F32)],
        compiler_params=pltpu.CompilerParams(dimension_semantics=("parallel", "arbitrary"), vmem_limit_bytes=VMEM_LIMIT),
    )(qv, kv, kv, vv, vv, dov, dlv)
    return dq.reshape(s, GROUP_DIM), dk.reshape(s, GROUP_DIM), dv.reshape(s, GROUP_DIM)


def _coords():
    return lax.axis_index("x"), lax.axis_index("y"), lax.axis_index("c")


_CHIP_FLIPS = ((1, 0), (0, 1), (1, 1))


def _flip(v, f):
    return 1 - v if f else v


def _form(kind, r, c):
    return (N_CHIPS, r, c) if kind == "blk" else (r, N_CHIPS * c)


def _slot(ref, kind, j, rows, c):
    if kind == "blk":
        return ref.at[j] if rows is None else ref.at[j, rows]
    cols = pl.ds(pl.multiple_of(j * c, 128), c)
    return ref.at[:, cols] if rows is None else ref.at[rows, cols]


def _half(r, which, align):
    return pl.ds(pl.multiple_of(which * (r // 2), align), r // 2)


def _rcopy(src, dst, send_sems, recv_sems, kk, dev):
    return pltpu.make_async_remote_copy(src_ref=src, dst_ref=dst, send_sem=send_sems.at[kk], recv_sem=recv_sems.at[kk],
                                        device_id=dev, device_id_type=MESH)


def _gather_plan(specs, step):
    def copies(refs, ss, rs, received):
        x, y, c = _coords()
        out = []
        for w, (kind, r, cc) in enumerate(specs):
            mine, other = _half(r, c, 16), _half(r, 1 - c, 16)
            for kk, (fx, fy) in enumerate(_CHIP_FLIPS):
                px, py = _flip(x, fx), _flip(y, fy)
                if step == "ici":
                    sl = _slot(refs[w], kind, 2 * px + py if received else 2 * x + y, mine, cc)
                    dev = (px, py, c)
                else:
                    sl = _slot(refs[w], kind, 2 * px + py, other if received else mine, cc)
                    dev = (x, y, 1 - c)
                out.append(_rcopy(sl, sl, ss, rs, 3 * w + kk, dev))
        return out

    def issue(refs, ss, rs):
        return copies(refs, ss, rs, False)

    def expect(refs, ss, rs):
        return copies(refs, ss, rs, False), copies(refs, ss, rs, True)

    return issue, expect


_HBM = pl.BlockSpec(memory_space=pltpu.HBM)
_SEM = pl.BlockSpec(memory_space=pltpu.SEMAPHORE)
_EFFECT = pltpu.SideEffectType.DATAFLOW_SIDE_EFFECTING


def _copies_start(name, bufs, n_sems, issue, after=None):
    nb = len(bufs)
    extra = [] if after is None else [after]

    def body(*refs):
        send_sems, recv_sems = refs[nb + len(extra)], refs[nb + len(extra) + 1]
        for cp in issue(refs[:nb], send_sems, recv_sems):
            cp.start()
        refs[-1][...] = jnp.zeros_like(refs[-1])

    outs = pl.pallas_call(
        body, name=name,
        out_shape=(pltpu.SemaphoreType.DMA((n_sems,)), pltpu.SemaphoreType.DMA((n_sems,)),
                   *[pltpu.HBM(b.shape, b.dtype) for b in bufs], SDS((8, 128), F32)),
        in_specs=[_HBM] * nb + [pl.BlockSpec(memory_space=pl.ANY)] * len(extra),
        out_specs=(_SEM, _SEM, *[_HBM] * nb, pl.BlockSpec(memory_space=pltpu.VMEM)),
        input_output_aliases={i: 2 + i for i in range(nb)},
        compiler_params=pltpu.CompilerParams(has_side_effects=_EFFECT),
    )(*[pltpu.with_memory_space_constraint(b, pltpu.HBM) for b in bufs], *extra)
    return outs[0], outs[1], list(outs[2:2 + nb]), outs[-1]


def _copies_wait(name, bufs, send_sems, recv_sems, after, expect):
    nb = len(bufs)

    def body(*refs):
        sent, received = expect(refs[:nb], refs[nb], refs[nb + 1])
        for cp in sent:
            cp.wait_send()
        for cp in received:
            cp.wait_recv()

    outs = pl.pallas_call(
        body, name=name,
        out_shape=tuple(pltpu.HBM(b.shape, b.dtype) for b in bufs),
        in_specs=(*[_HBM] * nb, _SEM, _SEM, pl.BlockSpec(memory_space=pl.ANY)), out_specs=tuple([_HBM] * nb),
        input_output_aliases={i: i for i in range(nb)},
        compiler_params=pltpu.CompilerParams(has_side_effects=_EFFECT),
    )(*bufs, send_sems, recv_sems, after)
    return list(outs)


def _add_pair(g, recv, kind, r, c, c_arr, name):
    h = r // 2
    if kind == "blk":
        tr = _row_tile(h, 512)
        grid = (N_CHIPS, h // tr)
        g_spec = pl.BlockSpec((1, 1, tr, c), lambda j, i, c_ref: (j, c_ref[0], i, 0))
        o_spec = pl.BlockSpec((1, tr, c), lambda j, i, c_ref: (j, i, 0))
        gv, oshape = g.reshape(N_CHIPS, 2, h, c), (N_CHIPS, h, c)
    else:
        tr = _row_tile(h, 64)
        grid = (h // tr,)
        g_spec = pl.BlockSpec((1, tr, N_CHIPS * c), lambda i, c_ref: (c_ref[0], i, 0))
        o_spec = pl.BlockSpec((tr, N_CHIPS * c), lambda i, c_ref: (i, 0))
        gv, oshape = g.reshape(2, h, N_CHIPS * c), (h, N_CHIPS * c)

    def body(c_ref, g_ref, r_ref, o_ref, ob_ref):
        v = (g_ref[:, 0] if kind == "blk" else g_ref[0]) + r_ref[...]
        o_ref[...] = v
        ob_ref[...] = v.astype(BF16)

    return pl.pallas_call(
        body, name=name,
        grid_spec=pltpu.PrefetchScalarGridSpec(num_scalar_prefetch=1, grid=grid, in_specs=[g_spec, o_spec], out_specs=[o_spec] * 2),
        out_shape=[SDS(oshape, F32), SDS(oshape, BF16)],
        compiler_params=pltpu.CompilerParams(vmem_limit_bytes=VMEM_LIMIT),
    )(c_arr, gv, recv)


def _sum_chips(pair, recv, kind, r, c, mc_arr, name):
    h = r // 2
    tr = _row_tile(h, 512)
    nt = h // tr
    if kind == "blk":
        p_spec = pl.BlockSpec((1, tr, c), lambda i, mc: (mc[0], i, 0))
    else:
        p_spec = pl.BlockSpec((tr, c), lambda i, mc: (i, mc[0]))

    def body(mc, a_ref, r_ref, g_out):
        own = a_ref[0] if kind == "blk" else a_ref[...]
        g_out[...] = ((own + r_ref[0].astype(F32)) + r_ref[1].astype(F32)) + r_ref[2].astype(F32)

    return pl.pallas_call(
        body, name=name,
        grid_spec=pltpu.PrefetchScalarGridSpec(
            num_scalar_prefetch=1, grid=(nt,), in_specs=[p_spec, pl.BlockSpec((3, tr, c), lambda i, mc: (0, i, 0))],
            out_specs=pl.BlockSpec((tr, c), lambda i, mc: (mc[1] * nt + i, 0))),
        out_shape=SDS((r, c), F32),
        compiler_params=pltpu.CompilerParams(vmem_limit_bytes=VMEM_LIMIT),
    )(mc_arr, pair, recv)


class _GroupReduce:
    def __init__(self, tag, specs, c_arr, mc_arr):
        self.tag, self.specs, self.c_arr, self.mc_arr = tag, specs, c_arr, mc_arr
        self.n = len(specs)

    def _plan(self, step):
        specs, n = self.specs, self.n

        def copies(refs, ss, rs, received):
            x, y, c = _coords()
            sib, out = (x, y, 1 - c), []
            for w, (_, kind, r, cc) in enumerate(specs):
                if step == "join":
                    there = refs[w].at[_half(r, 1 - c if received else c, 8)]
                    out.append(_rcopy(there, there, ss, rs, w, sib))
                    continue
                src, land = refs[w], refs[n + w]
                if step == "swap":
                    rows = _half(r, 1 - c, 8)
                    part = src.at[:, rows] if kind == "blk" else src.at[rows]
                    out.append(_rcopy(land if received else part, land, ss, rs, w, sib))
                else:
                    for kk, (fx, fy) in enumerate(_CHIP_FLIPS):
                        px, py = _flip(x, fx), _flip(y, fy)
                        part = land.at[kk] if received else _slot(src, kind, 2 * px + py, None, cc)
                        out.append(_rcopy(part, land.at[kk], ss, rs, 3 * w + kk, (px, py, c)))
            return out

        def issue(refs, ss, rs):
            return copies(refs, ss, rs, False)

        def expect(refs, ss, rs):
            return copies(refs, ss, rs, False), copies(refs, ss, rs, True)

        return issue, expect

    def swap_start(self, grads, after=None):
        lands = [lax.empty(_form(kind, r // 2, c), F32) for _, kind, r, c in self.specs]
        ss, rs, bufs, tok = _copies_start(f"rs_{self.tag}_swap", list(grads) + lands, self.n, self._plan("swap")[0], after=after)
        self.state = (ss, rs, bufs)
        return tok

    def swap_wait_ici_start(self, after):
        ss, rs, bufs = self.state
        bufs = _copies_wait(f"rs_{self.tag}_swap_wait", bufs, ss, rs, after, self._plan("swap")[1])
        pairs = [_add_pair(bufs[w], bufs[self.n + w], kind, r, c, self.c_arr, name=f"rs_{self.tag}_pair_{nm}")
                 for w, (nm, kind, r, c) in enumerate(self.specs)]
        self.pair = [pr[0] for pr in pairs]
        lands = [lax.empty((3, r // 2, c), BF16) for _, _, r, c in self.specs]
        ss, rs, bufs, tok = _copies_start(f"rs_{self.tag}_ici", [pr[1] for pr in pairs] + lands, 3 * self.n, self._plan("ici")[0])
        self.state = (ss, rs, bufs)
        return tok

    def ici_wait_join_start(self, after):
        ss, rs, bufs = self.state
        bufs = _copies_wait(f"rs_{self.tag}_ici_wait", bufs, ss, rs, after, self._plan("ici")[1])
        outs = [_sum_chips(self.pair[w], bufs[self.n + w], kind, r, c, self.mc_arr, name=f"rs_{self.tag}_sum_{nm}")
                for w, (nm, kind, r, c) in enumerate(self.specs)]
        ss, rs, bufs, tok = _copies_start(f"rs_{self.tag}_join", outs, self.n, self._plan("join")[0])
        self.state = (ss, rs, bufs)
        return tok

    def join_wait(self, after):
        ss, rs, bufs = self.state
        bufs = _copies_wait(f"rs_{self.tag}_join_wait", bufs, ss, rs, after, self._plan("join")[1])
        return {nm: bufs[w] for w, (nm, _, _, _) in enumerate(self.specs)}


def _small_gather_plan():
    def copies(refs, ss, rs, received):
        x, y, c = _coords()
        own, land = refs
        out = []
        for kk in range(1, 8):
            px, py, pc = _flip(x, (kk >> 2) & 1), _flip(y, (kk >> 1) & 1), _flip(c, kk & 1)
            there = land.at[4 * px + 2 * py + pc]
            out.append(_rcopy(there if received else own, there if received else land.at[4 * x + 2 * y + c], ss, rs, kk - 1,
                              (px, py, pc)))
        return out

    def issue(refs, ss, rs):
        return copies(refs, ss, rs, False)

    def expect(refs, ss, rs):
        return copies(refs, ss, rs, False), copies(refs, ss, rs, True)

    return issue, expect


def _sum_slots(slots):
    n, rows, cols = slots.shape

    def body(s_ref, o_ref):
        acc = s_ref[0]
        for j in range(1, n):
            acc = acc + s_ref[j]
        o_ref[...] = acc

    return pl.pallas_call(
        body, name="sum_small",
        in_specs=[pl.BlockSpec(memory_space=pltpu.VMEM)], out_specs=pl.BlockSpec(memory_space=pltpu.VMEM),
        out_shape=SDS((rows, cols), F32),
    )(slots)


def _adamw_rows(w, g, m, v):
    m = ADAM_B1 * m + (1.0 - ADAM_B1) * g
    v = ADAM_B2 * v + (1.0 - ADAM_B2) * jnp.square(g)
    m_hat = m / (1.0 - ADAM_B1 ** ADAM_STEP)
    v_hat = v / (1.0 - ADAM_B2 ** ADAM_STEP)
    return -ADAM_LR * (m_hat / (jnp.sqrt(v_hat) + ADAM_EPS) + ADAM_WD * w), m, v


def _adamw(w, g, m, v, name, dep=None, with_grad=False):
    rows, cols = w.shape
    tm = _pick(rows, (256, 128, 64, 16, 8))
    f = (lambda wt, gt, mt, vt: (gt,) + _adamw_rows(wt, gt, mt, vt)) if with_grad else _adamw_rows
    return _rows_call(f, [(t, 0, cols) for t in (w, g, m, v)], [], [(cols, F32)] * (3 + with_grad), tm=tm, name=name, dep=dep)


def _pack_small(parts):
    flat = jnp.concatenate([parts[n].reshape(-1) for n, _ in SMALL])
    return jnp.pad(flat, (0, SMALL_ROWS * PACK_COLS - flat.shape[0])).reshape(SMALL_ROWS, PACK_COLS)


def _unpack_small(buf, shapes):
    flat, out, off = buf.reshape(-1), {}, 0
    for n, sz in SMALL:
        out[n] = flat[off:off + sz].reshape(shapes[n])
        off += sz
    return out


def _lora_stack(parts):
    return jnp.concatenate([parts[n] for n, _ in LORA], axis=-2)


def _lora_split(stacked):
    out, off = {}, 0
    for n, rows in LORA:
        out[n] = stacked[..., off:off + rows, :]
        off += rows
    return out


def _ffn_gate_up(h, wgt, wut, name, dep=None):
    s, d = h.shape
    nblk, f, _ = wgt.shape
    tm = _pick(s, (1024, 512, 256))
    dn = (((1,), (1,)), ((), ()))

    def body(h_ref, wg_ref, wu_ref, *rest):
        g_ref, u_ref, a_ref = rest[-3:]
        hh = h_ref[...]
        g = lax.dot_general(hh, wg_ref[0], dn, preferred_element_type=F32)
        u = lax.dot_general(hh, wu_ref[0], dn, preferred_element_type=F32)
        g_ref[0], u_ref[0] = g.astype(BF16), u.astype(BF16)
        a_ref[0] = _swiglu_act(g, u).astype(BF16)

    w_spec = pl.BlockSpec((1, f, d), lambda j, i: (j, 0, 0))
    o_spec = pl.BlockSpec((1, tm, f), lambda j, i: (j, i, 0))
    extra = [] if dep is None else [dep]
    return pl.pallas_call(
        body, name=name, grid=(nblk, s // tm),
        in_specs=[pl.BlockSpec((tm, d), lambda j, i: (i, 0)), w_spec, w_spec] + [pl.BlockSpec(memory_space=pl.ANY)] * len(extra),
        out_specs=[o_spec] * 3,
        out_shape=[SDS((nblk, s, f), BF16)] * 3,
        compiler_params=pltpu.CompilerParams(dimension_semantics=("parallel", "parallel"), vmem_limit_bytes=VMEM_LIMIT),
    )(h, wgt, wut, *extra)


def _ffn_down_dx(dx_bf, wd, gate, up, name, dep=None):
    s, d = dx_bf.shape
    nblk, f, _ = wd.shape
    tm = _pick(s, (1024, 512, 256))
    dn = (((1,), (1,)), ((), ()))

    def body(dx_ref, wd_ref, g_ref, u_ref, *rest):
        dg_ref, du_ref = rest[-2:]
        dact = 0.5 * lax.dot_general(dx_ref[...], wd_ref[0], dn, preferred_element_type=F32)
        _, vjp = jax.vjp(_swiglu_act, g_ref[0].astype(F32), u_ref[0].astype(F32))
        dg, du = vjp(dact)
        dg_ref[0], du_ref[0] = dg.astype(BF16), du.astype(BF16)

    o_spec = pl.BlockSpec((1, tm, f), lambda j, i: (j, i, 0))
    extra = [] if dep is None else [dep]
    return pl.pallas_call(
        body, name=name, grid=(nblk, s // tm),
        in_specs=[pl.BlockSpec((tm, d), lambda j, i: (i, 0)), pl.BlockSpec((1, f, d), lambda j, i: (j, 0, 0)), o_spec, o_spec]
        + [pl.BlockSpec(memory_space=pl.ANY)] * len(extra),
        out_specs=[o_spec] * 2, out_shape=[SDS((nblk, s, f), BF16)] * 2,
        compiler_params=pltpu.CompilerParams(dimension_semantics=("parallel", "parallel"), vmem_limit_bytes=VMEM_LIMIT),
    )(dx_bf, wd, gate, up, *extra)


def _ffn_fwd(x, gain, wgt, wut, wd, tag, h=None, dep=None):
    if h is None:
        h = _rows_call(_rms, [(x, 0, D_MODEL)], [gain], [(D_MODEL, BF16)], tm=512, name=f"{tag}_norm")[0]
    gate, up, act = _ffn_gate_up(h, wgt, wut, f"{tag}_gate_up", dep=dep)
    x_new = _mm(act, wd, sum_blocks=True, res=x, alpha=0.5, name=f"{tag}_down")
    return x_new, (x, h, gate, up, act)


def _ffn_bwd(dx_new, dx_new_bf, saved, gain, wgt, wut, wd, tag, dep=None, hooks=None):
    x, h, gate, up, act = saved
    hooks = hooks or {}

    def hook(name, *vals):
        return hooks[name](*vals) if name in hooks else None

    d_wd = _mm(act, dx_new_bf, ta=True, alpha=0.5, name=f"{tag}_down_dw")
    dep = hook("down", d_wd) if "down" in hooks else dep
    dgate, dup = _ffn_down_dx(dx_new_bf, wd, gate, up, f"{tag}_down_dx", dep=dep)
    d_wgt = _mm(dgate, h, ta=True, dep=hook("mid", dgate), name=f"{tag}_gate_dw")
    d_wut = _mm(dup, h, ta=True, name=f"{tag}_up_dw")
    dh = _mm(dgate, wgt, sum_blocks=True, dep=hook("dw", d_wgt, d_wut), name=f"{tag}_gate_dx")
    dx, dx_bf, dgain = _mm(dup, wut, sum_blocks=True, res=dh, dep=hook("dx", dh), post=_norm_bwd_post(x, gain, dx_new),
                           name=f"{tag}_up_dx")
    hook("end", dx_bf)
    return dx, dx_bf, dgain, d_wgt, d_wut, d_wd


def _norm_bwd_post(x, gain, dres):
    def f(dht, xt, drt, gt):
        _, vjp = jax.vjp(_rms, xt, gt)
        dxt, dgt = vjp(dht)
        return dxt + drt, dxt + drt, dgt

    return f, [x, dres], [gain], [F32, BF16], [(1, D_MODEL)]


def kernel(x, p, positions, ffn1_norm, ffn1_w_gate, ffn1_w_up, ffn1_w_down, mix_norm, w_in, rwkv_mu, rwkv_w0, rwkv_w2, rwkv_a0, rwkv_a2, rwkv_g2, rwkv_k_k, rwkv_k_a, rwkv_r_k, rwkv_gn_w, rwkv_gn_b, q_norm, k_norm, w_br_rwkv, w_br_attn, w_out, ffn2_norm, ffn2_w_gate, ffn2_w_up, ffn2_w_down, ple_norm, ple_w_gate, ple_w_proj, loss_target, m_ffn1_norm, m_ffn1_w_gate, m_ffn1_w_up, m_ffn1_w_down, m_mix_norm, m_w_in, m_rwkv_mu, m_rwkv_w0, m_rwkv_w2, m_rwkv_a0, m_rwkv_a2, m_rwkv_g2, m_rwkv_k_k, m_rwkv_k_a, m_rwkv_r_k, m_rwkv_gn_w, m_rwkv_gn_b, m_q_norm, m_k_norm, m_w_br_rwkv, m_w_br_attn, m_w_out, m_ffn2_norm, m_ffn2_w_gate, m_ffn2_w_up, m_ffn2_w_down, m_ple_norm, m_ple_w_gate, m_ple_w_proj, v_ffn1_norm, v_ffn1_w_gate, v_ffn1_w_up, v_ffn1_w_down, v_mix_norm, v_w_in, v_rwkv_mu, v_rwkv_w0, v_rwkv_w2, v_rwkv_a0, v_rwkv_a2, v_rwkv_g2, v_rwkv_k_k, v_rwkv_k_a, v_rwkv_r_k, v_rwkv_gn_w, v_rwkv_gn_b, v_q_norm, v_k_norm, v_w_br_rwkv, v_w_br_attn, v_w_out, v_ffn2_norm, v_ffn2_w_gate, v_ffn2_w_up, v_ffn2_w_down, v_ple_norm, v_ple_w_gate, v_ple_w_proj):
    args = dict(locals())
    wts = {n: args[n] for n in WEIGHTS}
    mom_m = {n: args["m_" + n] for n in WEIGHTS}
    mom_v = {n: args["v_" + n] for n in WEIGHTS}
    x0, tgt = x[0], loss_target[0]
    s = x0.shape[0]
    p_tok = p[0, 0]

    vec = {n: wts[n].reshape(1, -1) for n, _ in SMALL}
    xi, yi, ci = _coords()
    me = 2 * xi + yi
    def laid(t, n):
        return jnp.transpose(t[n][0]) if n in TRANSPOSED else t[n][0]

    shard_of = {n: laid(wts, n) for g in GROUPS.values() for n, _, _, _ in g if n != "lora"}
    shard_of["lora"] = _lora_stack({n: wts[n][0] for n, _ in LORA})

    def whole_with_own(n, kind, r, c, tok=None):
        at = (me, 0, 0) if kind == "blk" else (0, me * c)
        own = (shard_of[n] if tok is None else shard_of[n] + tok[0, 0]).astype(BF16)
        return lax.dynamic_update_slice(lax.empty(_form(kind, r, c), BF16), own[None] if kind == "blk" else own, at)

    specs = {g: [(kind, r, c) for _, kind, r, c in grp] for g, grp in GROUPS.items()}
    plans = {(g, st): _gather_plan(specs[g], st) for g in GROUPS for st in ("ici", "d2d")}
    buf_f1 = [whole_with_own(*w) for w in GROUPS["f1"]]
    ss_0, rs_0, buf_f1, tok_0 = _copies_start("gather_f1_ici", buf_f1, 3 * len(buf_f1), plans["f1", "ici"][0])
    bufs = {g: [whole_with_own(*w, tok=tok_0) for w in GROUPS[g]] for g in ("mx", "f2")}
    buf_f1 = _copies_wait("gather_f1_ici_wait", buf_f1, ss_0, rs_0, bufs["mx"][0], plans["f1", "ici"][1])
    ss_1, rs_1, buf_f1, tok_1 = _copies_start("gather_f1_d2d", buf_f1, 3 * len(buf_f1), plans["f1", "d2d"][0])
    ss_a, rs_a, buf_mx, tok_a = _copies_start("gather_mx_ici", bufs["mx"], 3 * len(bufs["mx"]), plans["mx", "ici"][0],
                                              after=tok_1)
    h1 = _rows_call(_rms, [(x0, 0, D_MODEL)], [vec["ffn1_norm"] + tok_a[0, 0]], [(D_MODEL, BF16)], tm=512, name="ffn1_norm")[0]
    buf_f1 = _copies_wait("gather_f1_d2d_wait", buf_f1, ss_1, rs_1, h1, plans["f1", "d2d"][1])
    wb = dict(zip([w[0] for w in GROUPS["f1"]], buf_f1))

    inv_freq = 1.0 / (ROPE_THETA ** (jnp.arange(0, HEAD, 2, dtype=F32) / HEAD))
    ang = positions[0].astype(F32)[:, None] * inv_freq
    cos, sin = jnp.cos(ang), jnp.sin(ang)
    cos2, sin2 = jnp.concatenate([cos, cos], axis=1), jnp.concatenate([-sin, sin], axis=1)

    x1, ffn1_saved = _ffn_fwd(x0, vec["ffn1_norm"], wb["ffn1_w_gate"], wb["ffn1_w_up"], wb["ffn1_w_down"], "ffn1", h=h1, dep=tok_a)
    buf_mx = _copies_wait("gather_mx_ici_wait", buf_mx, ss_a, rs_a, x1, plans["mx", "ici"][1])
    ss_b, rs_b, buf_mx, tok_b = _copies_start("gather_mx_d2d", buf_mx, 3 * len(buf_mx), plans["mx", "d2d"][0])
    ss_c, rs_c, buf_f2, tok_c = _copies_start("gather_f2_ici", bufs["f2"], 3 * len(bufs["f2"]), plans["f2", "ici"][0])
    h = _rows_call(_rms, [(x1, 0, D_MODEL)], [vec["mix_norm"] + (tok_b[0, 0] + tok_c[0, 0])], [(D_MODEL, BF16)], tm=256,
                   name="mix_norm")[0]
    buf_mx = _copies_wait("gather_mx_d2d_wait", buf_mx, ss_b, rs_b, h, plans["mx", "d2d"][1])
    wb.update(zip([w[0] for w in GROUPS["mx"]], buf_mx))
    w_in_all = wb["w_in"]
    w_in_r, w_in_a, w_in_g = w_in_all[:, :RWKV_COLS], w_in_all[:, RWKV_COLS:RWKV_COLS + ATTN_COLS], w_in_all[:, RWKV_COLS + ATTN_COLS:]
    lora = _lora_split(wb["lora"])
    w2, a2, g2 = lora["rwkv_w2"], lora["rwkv_a2"], lora["rwkv_g2"]
    z_r = _mm(h, w_in_r, name="in_rwkv")
    z_a = _mm(h, w_in_a, name="in_attn")
    z_g = _mm(h, w_in_g, name="in_gate")

    zs = _shift_fwd(z_r, vec["rwkv_mu"])
    pre_params = [vec["rwkv_w0"], w2, vec["rwkv_a0"], a2, g2, vec["rwkv_k_k"], vec["rwkv_k_a"]]
    def pre_fwd(*t):
        res = _rwkv_pre(*t)
        return res[1], res[2], res[4], res[5], res[6]

    lw, k2, na, kb, gate_r = _rows_call(pre_fwd, [(zs, 0, RWKV_COLS)], pre_params, [(RWKV_DIM, F32)] * 5, tm=512, name="rwkv_pre")
    y_scan, s0s, invs = _wkv_fwd(zs, lw, k2, na, kb)
    buf_f2 = _copies_wait("gather_f2_ici_wait", buf_f2, ss_c, rs_c, y_scan, plans["f2", "ici"][1])
    ss_d, rs_d, buf_f2, tok_d = _copies_start("gather_f2_d2d", buf_f2, 3 * len(buf_f2), plans["f2", "d2d"][0])
    post_params = [vec["rwkv_gn_w"] + tok_d[0, 0], vec["rwkv_gn_b"], vec["rwkv_r_k"]]
    post_rows = [(y_scan, 0, RWKV_DIM), (zs, 0, RWKV_DIM), (k2, 0, RWKV_DIM), (zs, 2, RWKV_DIM), (gate_r, 0, RWKV_DIM)]
    y_rwkv = _rows_call(_rwkv_post, post_rows, post_params, [(RWKV_DIM, BF16)], tm=512, name="rwkv_post")[0]
    buf_f2 = _copies_wait("gather_f2_d2d_wait", buf_f2, ss_d, rs_d, y_rwkv, plans["f2", "d2d"][1])
    wb.update(zip([w[0] for w in GROUPS["f2"]], buf_f2))
    w_brr, w_bra = wb["w_br_rwkv"], wb["w_br_attn"]
    w_o = wb["w_out"].reshape(D_MODEL, D_MODEL)
    w_pp, w_pg = wb["ple_w_proj"], wb["ple_w_gate"].reshape(D_MODEL, D_MODEL)

    def qk_fwd(qt, kt, ct, st, qg, kg):
        return _norm_rope(qt, qg, ct, st), _norm_rope(kt, kg, ct, st)

    qk_rows = [(z_a, 0, ATTN_DIM), (z_a, 1, ATTN_DIM), (cos2, 0, HEAD), (sin2, 0, HEAD)]
    q_rot, k_rot = _rows_call(qk_fwd, qk_rows, [vec["q_norm"], vec["k_norm"]], [(ATTN_DIM, BF16)] * 2, tm=512, name="attn_pre")
    def group(t, g, off=0):
        return t[:, off + g * GROUP_DIM:off + (g + 1) * GROUP_DIM].astype(BF16)

    qkv = [(group(q_rot, g), group(k_rot, g), group(z_a, g, 2 * ATTN_DIM)) for g in range(len(ATTN_DILATIONS))]
    outs, lses = zip(*[_attn_fwd(*qkv[g], d) for g, d in enumerate(ATTN_DILATIONS)])
    comb_rows = [(t, 0, GROUP_DIM) for t in outs + lses]
    y_attn = _rows_call(_attn_combine, comb_rows, [], [(GROUP_DIM, BF16)], tm=512, name="attn_combine")[0]

    br = _mm(y_rwkv, w_brr, name="branch_rwkv")
    ba = _mm(y_attn, w_bra, name="branch_attn")
    merge_rows = [(z_g, 0, D_MODEL), (z_g, 1, D_MODEL), (br, 0, D_MODEL), (ba, 0, D_MODEL)]
    merged = _rows_call(_merge, merge_rows, [], [(D_MODEL, BF16)], tm=512, name="merge")[0]
    x2 = _mm(merged, w_o, res=x1, name="out_proj")
    x3, ffn2_saved = _ffn_fwd(x2, vec["ffn2_norm"], wb["ffn2_w_gate"], wb["ffn2_w_up"], wb["ffn2_w_down"], "ffn2")
    hp = _rows_call(_rms, [(x3, 0, D_MODEL)], [vec["ple_norm"]], [(D_MODEL, BF16)], tm=512, name="ple_norm")[0]
    pg = _mm(hp, w_pg, name="ple_gate")
    pp = _mm(p_tok, w_pp, name="ple_proj")

    def head(x3t, pgt, ppt, tt):
        sg = _sigmoid(pgt)
        err = x3t + sg * ppt - tt
        dx4 = err * (1.0 / D_MODEL)
        loss = 0.5 * jnp.sum(jnp.mean(err * err, axis=-1, keepdims=True), axis=0, keepdims=True)
        return dx4, dx4 * ppt * sg * (1.0 - sg), dx4 * sg, jnp.broadcast_to(loss, (8, 128))

    head_rows = [(x3, 0, D_MODEL), (pg, 0, D_MODEL), (pp, 0, D_MODEL), (tgt, 0, D_MODEL)]
    dx4, dpg, dpp, loss_tile = _rows_call(head, head_rows, [], [(D_MODEL, F32), (D_MODEL, BF16), (D_MODEL, BF16)], [(8, 128)],
                                          tm=512, name="ple_loss")

    c_arr = jnp.reshape(ci, (1,)).astype(jnp.int32)
    mc_arr = jnp.stack([me, ci]).astype(jnp.int32)
    red = {g: _GroupReduce(g, grp, c_arr, mc_arr) for g, grp in REDUCE_GROUPS.items()}

    done = {}

    def update(summed):
        for n, g2d in summed.items():
            if n == "lora":
                w_, m_, v_ = (_lora_stack({k: t[k][0] for k, _ in LORA}) for t in (wts, mom_m, mom_v))
            else:
                w_, m_, v_ = laid(wts, n), laid(mom_m, n), laid(mom_v, n)
            done[n] = _adamw(w_, g2d, m_, v_, name=f"adamw_{n}", with_grad=True)
    gw, gs = {}, {}
    gw["ple_w_proj"] = _mm(p_tok, dpp, ta=True, name="ple_proj_dw")
    gw["ple_w_gate"] = _mm(hp, dpg, ta=True, name="ple_gate_dw")
    dx3, dx3_bf, gs["ple_norm"] = _mm(dpg, w_pg, tb=True, post=_norm_bwd_post(x3, vec["ple_norm"], dx4), name="ple_gate_dx")
    dx2, dx2_bf, gs["ffn2_norm"], gw["ffn2_w_gate"], gw["ffn2_w_up"], gw["ffn2_w_down"] = _ffn_bwd(
        dx3, dx3_bf, ffn2_saved, vec["ffn2_norm"], wb["ffn2_w_gate"], wb["ffn2_w_up"], wb["ffn2_w_down"], "ffn2")
    gw["ple_w_gate"] = gw["ple_w_gate"].reshape(N_CHIPS, D_MODEL // N_CHIPS, D_MODEL)
    tok = red["f2"].swap_start([gw[w[0]] for w in REDUCE_GROUPS["f2"]])
    gw["w_out"] = _mm(merged, dx2_bf, ta=True, name="out_proj_dw")
    dmerged = _mm(dx2_bf, w_o, tb=True, dep=tok, name="out_proj_dx")

    def merge_bwd(zgr, zga, brt, bat, ct):
        _, vjp = jax.vjp(_merge, zgr, zga, brt, bat)
        d1, d2, d3, d4 = vjp(ct)
        return jnp.concatenate([d1, d2], axis=1), d3, d4

    dz_g, dbr, dba = _rows_call(merge_bwd, merge_rows + [(dmerged, 0, D_MODEL)], [],
                                [(2 * D_MODEL, BF16), (D_MODEL, BF16), (D_MODEL, BF16)], tm=512, name="merge_bwd")
    tok = red["f2"].swap_wait_ici_start(dz_g)
    gw["w_br_rwkv"] = _mm(y_rwkv, dbr, ta=True, name="branch_rwkv_dw")
    gw["w_br_attn"] = _mm(y_attn, dba, ta=True, name="branch_attn_dw")
    dy_rwkv = _mm(dbr, w_brr, tb=True, dep=tok, name="branch_rwkv_dx")
    dy_attn = _mm(dba, w_bra, tb=True, dep=tok, name="branch_attn_dx")

    def comb_bwd(*t):
        _, vjp = jax.vjp(_attn_combine, *t[:6])
        return vjp(t[6])

    dcomb = _rows_call(comb_bwd, comb_rows + [(dy_attn, 0, GROUP_DIM)], [], [(GROUP_DIM, F32)] * 6, tm=512, name="attn_combine_bwd")
    dqs, dks, dvs = zip(*[_attn_bwd(*qkv[g], d, dcomb[g], dcomb[3 + g]) for g, d in enumerate(ATTN_DILATIONS)])

    def qk_bwd(qt, kt, ct, st, *rest):
        dq = jnp.concatenate(rest[0:3], axis=1)
        dk = jnp.concatenate(rest[3:6], axis=1)
        qg, kg = rest[9], rest[10]
        _, vjp = jax.vjp(lambda a_, b_, c_, d_: qk_fwd(a_, b_, ct, st, c_, d_), qt, kt, qg, kg)
        dqt, dkt, dqg, dkg = vjp((dq, dk))
        return jnp.concatenate((dqt, dkt) + tuple(rest[6:9]), axis=1), dqg, dkg

    dz_a, gs["q_norm"], gs["k_norm"] = _rows_call(
        qk_bwd, qk_rows + [(t, 0, GROUP_DIM) for t in dqs + dks + dvs], [vec["q_norm"], vec["k_norm"]],
        [(ATTN_COLS, BF16)], [(1, HEAD), (1, HEAD)], tm=512, name="attn_pre_bwd")
    tok = red["f2"].ici_wait_join_start(dz_a)

    def post_bwd(*t):
        _, vjp = jax.vjp(_rwkv_post, *t[:5], *t[6:])
        return vjp(t[5])

    dy_scan, dr_post, dk2_post, dv_post, dgate_r, gs["rwkv_gn_w"], gs["rwkv_gn_b"], gs["rwkv_r_k"] = _rows_call(
        post_bwd, post_rows + [(dy_rwkv, 0, RWKV_DIM)], post_params, [(RWKV_DIM, F32)] * 5, [(1, RWKV_DIM)] * 3,
        tm=512, name="rwkv_post_bwd", dep=tok)
    update(red["f2"].join_wait(dy_scan))
    dr_s, dlw, dk2_s, dv_s, dna, dkb = _wkv_bwd(zs, lw, k2, na, kb, s0s, invs, dy_scan)

    def pre_bwd(zt, c_r1, c_r2, c_lw, c_k1, c_k2, c_v1, c_v2, c_a, c_b, c_g, *params):
        _, vjp = jax.vjp(_rwkv_pre, zt, *params)
        return vjp((c_r1 + c_r2, c_lw, c_k1 + c_k2, c_v1 + c_v2, c_a, c_b, c_g))

    pre_cts = [dr_s, dr_post, dlw, dk2_s, dk2_post, dv_s, dv_post, dna, dkb, dgate_r]
    dzs, gs["rwkv_w0"], g_w2, gs["rwkv_a0"], g_a2, g_g2, gs["rwkv_k_k"], gs["rwkv_k_a"] = _rows_call(
        pre_bwd, [(zs, 0, RWKV_COLS)] + [(t, 0, RWKV_DIM) for t in pre_cts], pre_params, [(RWKV_COLS, F32)],
        [q.shape for q in pre_params], tm=512, name="rwkv_pre_bwd")
    dz_r, gs["rwkv_mu"] = _shift_bwd(z_r, vec["rwkv_mu"], dzs)

    g_w_in = jnp.concatenate([_mm(h, dz_r, ta=True, name="in_rwkv_dw"), _mm(h, dz_a, ta=True, name="in_attn_dw"),
                              _mm(h, dz_g, ta=True, name="in_gate_dw")], axis=1)
    gw["w_in"], gw["lora"] = g_w_in, jnp.concatenate([g_w2, g_a2, g_g2], axis=0)
    gw["w_out"] = gw["w_out"].reshape(N_CHIPS, D_MODEL // N_CHIPS, D_MODEL)
    tok = red["mx"].swap_start([gw[w[0]] for w in REDUCE_GROUPS["mx"]])
    dh = _mm(dz_r, w_in_r, tb=True, dep=tok, name="in_rwkv_dx")
    dh = _mm(dz_a, w_in_a, tb=True, res=dh, name="in_attn_dx")
    dx1, dx1_bf, gs["mix_norm"] = _mm(dz_g, w_in_g, tb=True, res=dh, post=_norm_bwd_post(x1, vec["mix_norm"], dx2), name="in_gate_dx")
    tok_mx = red["mx"].swap_wait_ici_start(dx1_bf)
    hooks = {"down": lambda d_wd: red["f1d"].swap_start([d_wd], after=tok_mx),
             "mid": lambda dgate: red["f1d"].swap_wait_ici_start(dgate),
             "dw": lambda d_wgt, d_wut: red["f1g"].swap_start([d_wgt, d_wut]),
             "dx": lambda part: red["f1g"].swap_wait_ici_start(part) + red["f1d"].ici_wait_join_start(part),
             "end": lambda dx_: tokens.setdefault("mx_join", red["mx"].ici_wait_join_start(dx_))}
    tokens = {}
    dx0, _, gs["ffn1_norm"], gw["ffn1_w_gate"], gw["ffn1_w_up"], gw["ffn1_w_down"] = _ffn_bwd(
        dx1, dx1_bf, ffn1_saved, vec["ffn1_norm"], wb["ffn1_w_gate"], wb["ffn1_w_up"], wb["ffn1_w_down"], "ffn1", hooks=hooks)

    flat = jnp.concatenate([gs[n].reshape(-1) for n, _ in SMALL] + [loss_tile[0, 0:1]])
    small_buf = jnp.pad(flat, (0, SMALL_ROWS * PACK_COLS - flat.shape[0])).reshape(SMALL_ROWS, PACK_COLS)
    small_issue, small_expect = _small_gather_plan()
    ss_s, rs_s, small_bufs, tok_s = _copies_start("small_gather", [small_buf, lax.empty((8, SMALL_ROWS, PACK_COLS), F32)], 7,
                                                  small_issue, after=tokens["mx_join"])
    for g in ("mx", "f1d"):
        update(red[g].join_wait(tok_s))
    tok = red["f1g"].ici_wait_join_start(done["w_in"][1])
    small_buf, slots = _copies_wait("small_gather_wait", small_bufs, ss_s, rs_s, done["ffn1_w_down"][1], small_expect)
    small_sum = _sum_slots(lax.dynamic_update_slice(slots, small_buf[None], (4 * xi + 2 * yi + ci, 0, 0)))
    n_small = sum(sz for _, sz in SMALL)
    loss = small_sum.reshape(-1)[n_small]
    grad_small = _unpack_small(small_sum, {n: wts[n].shape for n, _ in SMALL})
    d_s, m_s, v_s = _adamw(_pack_small(wts), small_sum, _pack_small(mom_m), _pack_small(mom_v), name="adamw_small", dep=tok)
    shapes = {n: wts[n].shape for n, _ in SMALL}
    d_s, m_s, v_s = _unpack_small(d_s, shapes), _unpack_small(m_s, shapes), _unpack_small(v_s, shapes)
    grads, deltas, new_m, new_v = {}, {}, {}, {}
    for n, _ in SMALL:
        grads[n], deltas[n], new_m[n], new_v[n] = grad_small[n], d_s[n], m_s[n], v_s[n]
    update(red["f1g"].join_wait(m_s["ffn1_norm"]))
    for n, res in done.items():
        for store, val in zip((grads, deltas, new_m, new_v), res):
            if n == "lora":
                store.update({k: t[None] for k, t in _lora_split(val).items()})
            else:
                store[n] = (jnp.transpose(val) if n in TRANSPOSED else val)[None]

    return (loss, dx0[None], *[grads[n] for n in WEIGHTS], *[deltas[n] for n in WEIGHTS],
            *[new_m[n] for n in WEIGHTS], *[new_v[n] for n in WEIGHTS])
```

```python
import functools

import jax
import jax.numpy as jnp
from jax import lax
from jax.experimental import pallas as pl
from jax.experimental.pallas import tpu as pltpu

F32, BF16 = jnp.float32, jnp.bfloat16
HI = lax.Precision.HIGHEST
MESH = pl.DeviceIdType.MESH
SDS = jax.ShapeDtypeStruct

D_MODEL = 1024
HEAD = 64
RWKV_HEADS = 8
RWKV_DIM = RWKV_HEADS * HEAD
DECAY_LORA, ICLR_LORA, GATE_LORA = 64, 64, 128
GN_EPS = 64e-5
RMS_EPS = 1e-6
ATTN_DILATIONS = (1, 4, 16)
BAND = 128
ATTN_DIM = 768
GROUP_DIM = 256
ATTN_CLASSES_PER_STEP = 4
ROPE_THETA = 10000.0
NEG_INF = -1e30
RWKV_COLS = 3 * RWKV_DIM + DECAY_LORA + ICLR_LORA + GATE_LORA
ATTN_COLS = 3 * ATTN_DIM
ADAM_LR, ADAM_B1, ADAM_B2, ADAM_EPS, ADAM_WD, ADAM_STEP = 0.001, 0.9, 0.999, 1e-08, 0.01, 10

WKV_CHUNK = 64
WKV_HEADS_PER_STEP = 8
WKV_CHUNKS_PER_STEP = 4
N_CHIPS = 4
PACK_COLS = 1024
VMEM_LIMIT = 48 * 1024 * 1024

TRANSPOSED = ("ffn1_w_gate", "ffn1_w_up", "ffn2_w_gate", "ffn2_w_up")
LORA = (("rwkv_w2", 64), ("rwkv_a2", 64), ("rwkv_g2", 128))
_FFN1 = (("ffn1_w_gate", "blk", 704, 1024), ("ffn1_w_up", "blk", 704, 1024), ("ffn1_w_down", "blk", 704, 1024))
_FFN2 = (("ffn2_w_gate", "blk", 704, 1024), ("ffn2_w_up", "blk", 704, 1024), ("ffn2_w_down", "blk", 704, 1024))
_IN = (("w_in", "col", 1024, 1536), ("lora", "col", 256, 128))
_BRANCH = (("w_br_rwkv", "col", 512, 256), ("w_br_attn", "col", 256, 256), ("w_out", "blk", 256, 1024))
_PLE = (("ple_w_gate", "blk", 256, 1024), ("ple_w_proj", "col", 256, 256))
GROUPS = {"f1": _FFN1, "mx": _IN, "f2": _BRANCH + _FFN2 + _PLE}
REDUCE_GROUPS = {"f2": _FFN2 + _PLE, "mx": _IN + _BRANCH, "f1d": _FFN1[2:], "f1g": _FFN1[:2]}
SMALL = (
    ("ffn1_norm", 1024), ("mix_norm", 1024), ("ffn2_norm", 1024), ("ple_norm", 1024), ("rwkv_mu", 1792),
    ("rwkv_w0", 512), ("rwkv_a0", 512), ("rwkv_k_k", 512), ("rwkv_k_a", 512), ("rwkv_r_k", 512),
    ("rwkv_gn_w", 512), ("rwkv_gn_b", 512), ("q_norm", 64), ("k_norm", 64),
)
SMALL_ROWS = 16
WEIGHTS = (
    "ffn1_norm", "ffn1_w_gate", "ffn1_w_up", "ffn1_w_down", "mix_norm", "w_in", "rwkv_mu", "rwkv_w0", "rwkv_w2",
    "rwkv_a0", "rwkv_a2", "rwkv_g2", "rwkv_k_k", "rwkv_k_a", "rwkv_r_k", "rwkv_gn_w", "rwkv_gn_b", "q_norm", "k_norm",
    "w_br_rwkv", "w_br_attn", "w_out", "ffn2_norm", "ffn2_w_gate", "ffn2_w_up", "ffn2_w_down", "ple_norm",
    "ple_w_gate", "ple_w_proj",
)


def _row_tile(n, most=704):
    for t in range(most - most % 16, 0, -16):
        if n % t == 0:
            return t
    return n


def _pick(n, cands):
    for c in cands:
        if n % c == 0:
            return c
    return n


def _mm(a, b, *, ta=False, tb=False, sum_blocks=False, out_dtype=F32, res=None, alpha=1.0, dep=None, post=None, name):
    flat = a.ndim == 2 and b.ndim == 2
    a3 = a if a.ndim == 3 else a[None]
    b3 = b if b.ndim == 3 else b[None]
    na, nbb = a3.shape[0], b3.shape[0]
    nblk = max(na, nbb)
    kdim, m = (a3.shape[1], a3.shape[2]) if ta else (a3.shape[2], a3.shape[1])
    n = b3.shape[1] if tb else b3.shape[2]
    assert (b3.shape[2] if tb else b3.shape[1]) == kdim
    tm = _pick(m, (1024, 512, 256, 128) if post is None else (512, 256, 128))
    tn = _pick(n, (1024, 896, 768, 512, 256, 128))
    tk = kdim if kdim <= 2304 else _pick(kdim, (1024, 512, 256, 128))
    nk = kdim // tk
    direct = nk == 1 and not sum_blocks

    if sum_blocks:
        grid = (m // tm, n // tn, nblk, nk)

        def ids(i, c, j, k):
            return i, c, j, k
    else:
        grid = (nblk, m // tm, n // tn, nk)

        def ids(j, i, c, k):
            return i, c, j, k

    def amap(*g):
        i, c, j, k = ids(*g)
        jj = j if na > 1 else 0
        return (jj, k, i) if ta else (jj, i, k)

    def bmap(*g):
        i, c, j, k = ids(*g)
        jj = j if nbb > 1 else 0
        return (jj, c, k) if tb else (jj, k, c)

    if sum_blocks:
        oshape, oblk = (m, n), (tm, tn)

        def omap(*g):
            i, c, j, k = ids(*g)
            return i, c
    else:
        oshape, oblk = (nblk, m, n), (1, tm, tn)

        def omap(*g):
            i, c, j, k = ids(*g)
            return j, i, c

    dn = (((0 if ta else 1,), (1 if tb else 0,)), ((), ()))
    has_res = res is not None
    p_f, p_rows, p_params, p_dtypes, p_accs = post if post is not None else (None, [], [], [], [])
    assert post is None or sum_blocks or flat
    n_in = 2 + has_res + len(p_rows) + len(p_params) + (dep is not None)

    def tile_map(*g):
        i, c, j, k = ids(*g)
        return i, c

    def body(*refs):
        refs = list(refs)
        acc = None if direct else refs.pop()
        o_refs = refs[n_in:]
        a_ref, b_ref = refs[0], refs[1]
        r_ref = refs[2] if has_res else None
        pr_refs = refs[2 + has_res:2 + has_res + len(p_rows)]
        pp_refs = refs[2 + has_res + len(p_rows):2 + has_res + len(p_rows) + len(p_params)]
        first_tile = jnp.logical_and(pl.program_id(0 if sum_blocks else 1) == 0, pl.program_id(1 if sum_blocks else 2) == 0)

        def finish(v):
            if alpha != 1.0:
                v = v * alpha
            if has_res:
                v = v + r_ref[...].reshape(v.shape).astype(F32)
            if post is None:
                o_refs[0][...] = v.reshape(o_refs[0].shape).astype(o_refs[0].dtype)
                return
            outs = p_f(v, *[t[...] for t in pr_refs], *[t[...] for t in pp_refs])
            for o_ref, val in zip(o_refs, outs[:len(p_dtypes)]):
                o_ref[...] = val.astype(o_ref.dtype)
            for o_ref, val in zip(o_refs[len(p_dtypes):], outs[len(p_dtypes):]):
                @pl.when(first_tile)
                def _():
                    o_ref[...] = jnp.zeros_like(o_ref)

                o_ref[...] += val.reshape(o_ref.shape)

        if direct:
            finish(lax.dot_general(a_ref[0].astype(BF16), b_ref[0].astype(BF16), dn, preferred_element_type=F32))
            return
        k = pl.program_id(3)
        if sum_blocks:
            j = pl.program_id(2)
            first = jnp.logical_and(j == 0, k == 0)
            last = jnp.logical_and(j == nblk - 1, k == nk - 1)
        else:
            first, last = k == 0, k == nk - 1

        @pl.when(first)
        def _():
            acc[...] = jnp.zeros_like(acc)

        acc[...] += lax.dot_general(a_ref[0].astype(BF16), b_ref[0].astype(BF16), dn, preferred_element_type=F32)

        @pl.when(last)
        def _():
            finish(acc[...])

    in_specs = [pl.BlockSpec((1, tk, tm) if ta else (1, tm, tk), amap), pl.BlockSpec((1, tn, tk) if tb else (1, tk, tn), bmap)]
    args = [a3, b3]
    if has_res:
        res3 = res if (sum_blocks or res.ndim == 3) else res[None]
        in_specs.append(pl.BlockSpec(oblk, omap))
        args.append(res3)
    in_specs += [pl.BlockSpec((tm, tn), tile_map) for _ in p_rows]
    in_specs += [pl.BlockSpec(t.shape, functools.partial(lambda *g, nd: (0,) * nd, nd=t.ndim)) for t in p_params]
    args += list(p_rows) + list(p_params)
    if dep is not None:
        in_specs.append(pl.BlockSpec(memory_space=pl.ANY))
        args.append(dep)
    if post is None:
        out_specs, out_shape = pl.BlockSpec(oblk, omap), SDS(oshape, out_dtype)
        semantics = ("parallel", "parallel", "arbitrary", "arbitrary") if sum_blocks else ("parallel", "parallel", "parallel", "arbitrary")
    else:
        out_specs = [pl.BlockSpec((tm, tn), tile_map) for _ in p_dtypes]
        out_specs += [pl.BlockSpec(tuple(sh), functools.partial(lambda *g, nd: (0,) * nd, nd=len(sh))) for sh in p_accs]
        out_shape = [SDS((m, n), dt) for dt in p_dtypes] + [SDS(tuple(sh), F32) for sh in p_accs]
        semantics = ("arbitrary",) * 4
    out = pl.pallas_call(
        body,
        name=name,
        grid=grid,
        in_specs=in_specs,
        out_specs=out_specs,
        out_shape=out_shape,
        scratch_shapes=[] if direct else [pltpu.VMEM((tm, tn), F32)],
        compiler_params=pltpu.CompilerParams(dimension_semantics=semantics, vmem_limit_bytes=VMEM_LIMIT),
    )(*args)
    if post is not None:
        return out
    if flat and not sum_blocks:
        out = out[0]
    return out


def _rows_call(f, rows, params, outs, accs=(), *, tm, name, dep=None):
    s = rows[0][0].shape[0]
    nr, npar, no = len(rows), len(params), len(outs)
    nin = nr + npar + (0 if dep is None else 1)
    in_specs = [pl.BlockSpec((tm, w), functools.partial(lambda i, cb: (i, cb), cb=cb)) for (_, cb, w) in rows]
    in_specs += [pl.BlockSpec(p.shape, functools.partial(lambda i, nd: (0,) * nd, nd=p.ndim)) for p in params]
    if dep is not None:
        in_specs.append(pl.BlockSpec(memory_space=pl.ANY))
    out_shape = [SDS((s, w), dt) for (w, dt) in outs] + [SDS(tuple(sh), F32) for sh in accs]
    out_specs = [pl.BlockSpec((tm, w), lambda i: (i, 0)) for (w, _) in outs]
    out_specs += [pl.BlockSpec(tuple(sh), functools.partial(lambda i, nd: (0,) * nd, nd=len(sh))) for sh in accs]

    def body(*refs):
        rin, pin = refs[:nr], refs[nr:nr + npar]
        oo, ao = refs[nin:nin + no], refs[nin + no:]
        res = f(*[r[...] for r in rin], *[p[...] for p in pin])
        if not isinstance(res, (tuple, list)):
            res = (res,)
        for o_ref, v in zip(oo, res[:no]):
            o_ref[...] = v.astype(o_ref.dtype)
        i = pl.program_id(0)
        for a_ref, v in zip(ao, res[no:]):
            @pl.when(i == 0)
            def _():
                a_ref[...] = jnp.zeros_like(a_ref)

            a_ref[...] += v.reshape(a_ref.shape)

    res = pl.pallas_call(
        body,
        name=name,
        grid=(s // tm,),
        in_specs=in_specs,
        out_specs=out_specs,
        out_shape=out_shape,
        compiler_params=pltpu.CompilerParams(dimension_semantics=("arbitrary",), vmem_limit_bytes=VMEM_LIMIT),
    )(*[r[0] for r in rows], *params, *([] if dep is None else [dep]))
    return res


def _mmv(a, b, mode):
    ca = 0 if mode[0] == "t" else 1
    cb = 1 if mode[1] == "t" else 0
    return lax.dot_general(a.astype(BF16), b.astype(BF16), (((ca,), (cb,)), ((), ())), preferred_element_type=F32)


@functools.partial(jax.custom_vjp, nondiff_argnums=(2,))
def _bdot(a, b, mode):
    return _mmv(a, b, mode)


def _bdot_fwd(a, b, mode):
    return _mmv(a, b, mode), (a, b)


def _bdot_bwd(mode, saved, g):
    a, b = saved
    if mode == "nn":
        return _mmv(g, b, "nt"), _mmv(a, g, "tn")
    if mode == "nt":
        return _mmv(g, b, "nn"), _mmv(g, a, "tn")
    return _mmv(b, g, "nt"), _mmv(a, g, "nn")


_bdot.defvjp(_bdot_fwd, _bdot_bwd)


def _hdot(a, b, mode="nn", precision=HI):
    ca = 0 if mode[0] == "t" else 1
    cb = 1 if mode[1] == "t" else 0
    return lax.dot_general(a, b, (((ca,), (cb,)), ((), ())), precision=precision, preferred_element_type=F32)


def _segsum(x):
    c = x.shape[-1]
    blk = min(c, 256)
    r = lax.broadcasted_iota(jnp.int32, (blk, blk), 0) >> 6
    q = lax.broadcasted_iota(jnp.int32, (blk, blk), 1) >> 6
    ones = jnp.where(r == q, 1.0, 0.0).astype(F32)
    parts = [_hdot(x[:, i:i + blk], ones, precision=lax.Precision.HIGH) for i in range(0, c, blk)]
    return parts[0] if len(parts) == 1 else jnp.concatenate(parts, axis=1)


def _sigmoid(x):
    return jax.nn.sigmoid(x)


def _softplus(x):
    return jnp.maximum(x, 0.0) + jnp.log(1.0 + jnp.exp(-jnp.abs(x)))


def _rms(x, gain):
    return x * lax.rsqrt(jnp.mean(x * x, axis=-1, keepdims=True) + RMS_EPS) * gain


def _swiglu_act(gate, up):
    return gate * _sigmoid(gate) * up


def _rwkv_pre(zs, w0, w2, a0, a2, g2, k_k, k_a):
    r, k, v = zs[:, 0:512], zs[:, 512:1024], zs[:, 1024:1536]
    lora = zs[:, 1536:1792]
    wd, ad, gd = lora[:, 0:64], lora[:, 64:128], lora[:, 128:256]
    w = -_softplus(-(w0 + _bdot(jnp.tanh(wd), w2, "nn"))) - 0.5
    a = _sigmoid(a0 + _bdot(ad, a2, "nn"))
    g = _bdot(_sigmoid(gd), g2, "nn")
    kk = k * k_k
    kk = kk * lax.rsqrt(jnp.maximum(_segsum(kk * kk), 1e-24))
    k2 = k * (1.0 + (a - 1.0) * k_a)
    return r, -jnp.exp(w), k2, v, -kk, kk * a, g


def _rwkv_post(y, r, k2, v, g, gn_w, gn_b, r_k):
    mean = _segsum(y) * (1.0 / HEAD)
    yc = y - mean
    var = _segsum(yc * yc) * (1.0 / HEAD)
    yn = yc * lax.rsqrt(var + GN_EPS) * gn_w + gn_b
    bonus = _segsum(r * k2 * r_k) * v
    return (yn + bonus) * g


def _swap_halves(x):
    lane = lax.broadcasted_iota(jnp.int32, x.shape, 1)
    return jnp.where((lane & 32) == 0, jnp.roll(x, -32, axis=1), jnp.roll(x, 32, axis=1))


def _norm_rope(x, gain, cos, sin):
    heads = x.shape[1] // HEAD
    def rep(t):
        return jnp.concatenate([t] * heads, axis=1)

    xn = x * lax.rsqrt(_segsum(x * x) * (1.0 / HEAD) + RMS_EPS) * rep(gain)
    return xn * rep(cos) + _swap_halves(xn) * rep(sin)


def _attn_combine(o0, o1, o2, l0, l1, l2):
    m = jnp.maximum(jnp.maximum(l0, l1), l2)
    e0, e1, e2 = jnp.exp(l0 - m), jnp.exp(l1 - m), jnp.exp(l2 - m)
    return (e0 * o0 + e1 * o1 + e2 * o2) / (e0 + e1 + e2)


def _merge(zgr, zga, br, ba):
    return _sigmoid(zgr) * br + _sigmoid(zga) * ba


def _attn_block(q, kp, kc, vp, vc, has_prev):
    iq = lax.broadcasted_iota(jnp.int32, (1, BAND, BAND), 1)
    ik = lax.broadcasted_iota(jnp.int32, (1, BAND, BAND), 2)
    s_c = jnp.where(iq >= ik, _bdotb(q, kc, "nt") * (HEAD ** -0.5), NEG_INF)
    s_p = jnp.where(jnp.logical_and(iq <= ik, has_prev), _bdotb(q, kp, "nt") * (HEAD ** -0.5), NEG_INF)
    m = lax.stop_gradient(jnp.maximum(jnp.max(s_c, axis=-1, keepdims=True), jnp.max(s_p, axis=-1, keepdims=True)))
    e_c, e_p = jnp.exp(s_c - m), jnp.exp(s_p - m)
    l = jnp.sum(e_c, axis=-1, keepdims=True) + jnp.sum(e_p, axis=-1, keepdims=True)
    o = (_bdotb(e_c, vc) + _bdotb(e_p, vp)) / l
    return o, jnp.broadcast_to(m + jnp.log(l), o.shape)


def _mmb(a, b, cb):
    return lax.dot_general(a.astype(BF16), b.astype(BF16), (((2,), (cb,)), ((0,), (0,))), preferred_element_type=F32)


@functools.partial(jax.custom_vjp, nondiff_argnums=(2,))
def _bdotb1(a, b, cb):
    return _mmb(a, b, cb)


def _bdotb1_fwd(a, b, cb):
    return _mmb(a, b, cb), (a, b)


def _bdotb1_bwd(cb, saved, g):
    a, b = saved
    if cb == 1:
        return _mmb(g, b, 2), _mmb(jnp.swapaxes(a, 1, 2), g, 1)
    return _mmb(g, b, 1), _mmb(jnp.swapaxes(g, 1, 2), a, 1)


_bdotb1.defvjp(_bdotb1_fwd, _bdotb1_bwd)


def _bdotb(a, b, mode="nn", precision=None):
    if mode[0] == "t":
        a = jnp.swapaxes(a, 1, 2)
    cb = 2 if mode[1] == "t" else 1
    if precision is None:
        return _bdotb1(a, b, cb)
    return lax.dot_general(a, b, (((2,), (cb,)), ((0,), (0,))), precision=precision, preferred_element_type=F32)


def _tri_inv_levels(a):
    t = a.shape[-1]
    row = lax.broadcasted_iota(jnp.int32, (1, t, t), 1)
    col = lax.broadcasted_iota(jnp.int32, (1, t, t), 2)
    x = jnp.where(row == col, 1.0, 0.0).astype(F32) + jnp.where(jnp.logical_and(row == col + 1, (row & 1) == 1), a, 0.0)
    sh = 1
    while (1 << sh) < t:
        m = jnp.logical_and((row >> sh) == (col >> sh) + 1, (row >> (sh + 1)) == (col >> (sh + 1)))
        x = x + _bdotb(_bdotb(x, jnp.where(m, a, 0.0)), x)
        sh += 1
    return x


@jax.custom_vjp
def _tri_inv(a):
    return _tri_inv_levels(a)


def _tri_inv_fwd(a):
    x = _tri_inv_levels(a)
    return x, x


def _tri_inv_bwd(x, g):
    xt = jnp.swapaxes(x, 1, 2)
    return (_bdotb(_bdotb(xt, g, precision=lax.Precision.HIGH), xt, precision=lax.Precision.HIGH),)


_tri_inv.defvjp(_tri_inv_fwd, _tri_inv_bwd)


@jax.custom_vjp
def _known_inv(a, x):
    return x


def _known_inv_fwd(a, x):
    return x, x


def _known_inv_bwd(x, g):
    return _tri_inv_bwd(x, g)[0], jnp.zeros_like(x)


_known_inv.defvjp(_known_inv_fwd, _known_inv_bwd)


def _wkv_chunk(s0, r, lw, k, v, a, b, inv=None, with_inv=False):
    nh = r.shape[0]
    t = WKV_CHUNK
    n = r.shape[1] // t

    def chunked(x):
        return x if n == 1 else jnp.concatenate([x[:, c * t:(c + 1) * t] for c in range(n)], axis=0)

    r, lw, k, v, a, b = (chunked(x) for x in (r, lw, k, v, a, b))
    row = lax.broadcasted_iota(jnp.int32, (1, t, t), 1)
    col = lax.broadcasted_iota(jnp.int32, (1, t, t), 2)
    incl, strict = row >= col, row > col
    ones = jnp.broadcast_to(jnp.where(incl, 1.0, 0.0).astype(F32), (n * nh, t, t))
    cum = _bdotb(ones, lw, precision=HI)
    c_end = cum[:, t - 1:t, :]
    e_in, e_ex, e_inv = jnp.exp(cum), jnp.exp(cum - lw), jnp.exp(-cum)
    at, rt, bt, kt = a * e_ex, r * e_in, b * e_inv, k * e_inv
    a_ab = jnp.where(strict, _bdotb(at, bt, "nt"), 0.0)
    a_ak = jnp.where(strict, _bdotb(at, kt, "nt"), 0.0)
    x = _tri_inv(a_ab) if inv is None else _known_inv(a_ab, inv)
    r_b = jnp.where(incl, _bdotb(rt, bt, "nt"), 0.0)
    akv = _bdotb(a_ak, v)
    rkv = _bdotb(jnp.where(incl, _bdotb(rt, kt, "nt"), 0.0), v)
    w_end = jnp.exp(c_end - cum)
    bw, kw, decay = b * w_end, k * w_end, jnp.exp(c_end)
    ys = []
    for c in range(n):
        hs = slice(c * nh, (c + 1) * nh)
        u = _bdotb(x[hs], _bdotb(at[hs], s0, "nt") + akv[hs])
        ys.append(_bdotb(rt[hs], s0, "nt") + _bdotb(r_b[hs], u) + rkv[hs])
        s0 = s0 * decay[hs] + _bdotb(u, bw[hs], "tn") + _bdotb(v[hs], kw[hs], "tn")
    y = ys[0] if n == 1 else jnp.concatenate(ys, axis=1)
    return (y, s0, x) if with_inv else (y, s0)


def _shift_fwd(z, mu):
    s, c = z.shape
    tc = 256

    def body(z_ref, mu_ref, o_ref):
        zz = z_ref[...]
        row = lax.broadcasted_iota(jnp.int32, zz.shape, 0)
        prev = jnp.where(row == 0, 0.0, pltpu.roll(zz, 1, 0))
        o_ref[...] = zz + (prev - zz) * mu_ref[...]

    return pl.pallas_call(
        body, name="shift_fwd", grid=(c // tc,),
        in_specs=[pl.BlockSpec((s, tc), lambda j: (0, j)), pl.BlockSpec((1, tc), lambda j: (0, j))],
        out_specs=pl.BlockSpec((s, tc), lambda j: (0, j)), out_shape=SDS((s, c), F32),
        compiler_params=pltpu.CompilerParams(dimension_semantics=("parallel",), vmem_limit_bytes=VMEM_LIMIT),
    )(z, mu)


def _shift_bwd(z, mu, dzs):
    s, c = z.shape
    tc = 256

    def body(z_ref, mu_ref, d_ref, dz_ref, dmu_ref):
        zz, d, m = z_ref[...], d_ref[...], mu_ref[...]
        row = lax.broadcasted_iota(jnp.int32, zz.shape, 0)
        prev = jnp.where(row == 0, 0.0, pltpu.roll(zz, 1, 0))
        t = d * m
        nxt = jnp.where(row == s - 1, 0.0, pltpu.roll(t, s - 1, 0))
        dz_ref[...] = (d - t + nxt).astype(dz_ref.dtype)
        dmu_ref[...] = jnp.sum(d * (prev - zz), axis=0, keepdims=True)

    return pl.pallas_call(
        body, name="shift_bwd", grid=(c // tc,),
        in_specs=[pl.BlockSpec((s, tc), lambda j: (0, j)), pl.BlockSpec((1, tc), lambda j: (0, j)),
                  pl.BlockSpec((s, tc), lambda j: (0, j))],
        out_specs=[pl.BlockSpec((s, tc), lambda j: (0, j)), pl.BlockSpec((1, tc), lambda j: (0, j))],
        out_shape=[SDS((s, c), BF16), SDS((1, c), F32)],
        compiler_params=pltpu.CompilerParams(dimension_semantics=("parallel",), vmem_limit_bytes=VMEM_LIMIT),
    )(z, mu, dzs)


def _heads(x, nh):
    return jnp.stack([x[:, h * HEAD:(h + 1) * HEAD] for h in range(nh)], axis=0)


def _unheads(x):
    return jnp.concatenate([x[h] for h in range(x.shape[0])], axis=1)


def _wkv_fwd(zs, lw, k2, na, b):
    s = lw.shape[0]
    t, hb = WKV_CHUNK * WKV_CHUNKS_PER_STEP, WKV_HEADS_PER_STEP
    w = hb * HEAD
    nc, ng = s // t, RWKV_HEADS // hb
    nx = WKV_CHUNKS_PER_STEP * RWKV_HEADS

    def body(r_ref, v_ref, lw_ref, k_ref, a_ref, b_ref, y_ref, s0_ref, x_ref, state):
        @pl.when(pl.program_id(1) == 0)
        def _():
            state[...] = jnp.zeros_like(state)

        s0 = state[...]
        s0_ref[0] = s0
        y, s1, x = _wkv_chunk(s0, *[_heads(t_ref[...], hb) for t_ref in (r_ref, lw_ref, k_ref, v_ref, a_ref, b_ref)], with_inv=True)
        y_ref[...] = _unheads(y)
        x_ref[0] = x
        state[...] = s1

    def col(off):
        return pl.BlockSpec((t, w), functools.partial(lambda g, i, off: (i, g + off), off=off))

    return pl.pallas_call(
        body, name="wkv_fwd", grid=(ng, nc),
        in_specs=[col(0), col(2 * ng), col(0), col(0), col(0), col(0)],
        out_specs=[col(0), pl.BlockSpec((1, hb, HEAD, HEAD), lambda g, i: (i, g, 0, 0)),
                   pl.BlockSpec((1, nx, WKV_CHUNK, WKV_CHUNK), lambda g, i: (i, 0, 0, 0))],
        out_shape=[SDS((s, RWKV_DIM), F32), SDS((nc, RWKV_HEADS, HEAD, HEAD), F32), SDS((nc, nx, WKV_CHUNK, WKV_CHUNK), F32)],
        scratch_shapes=[pltpu.VMEM((hb, HEAD, HEAD), F32)],
        compiler_params=pltpu.CompilerParams(dimension_semantics=("parallel", "arbitrary"), vmem_limit_bytes=VMEM_LIMIT),
    )(zs, zs, lw, k2, na, b)


def _wkv_bwd(zs, lw, k2, na, b, s0s, invs, dy):
    s = lw.shape[0]
    t, hb = WKV_CHUNK * WKV_CHUNKS_PER_STEP, WKV_HEADS_PER_STEP
    w = hb * HEAD
    nc, ng = s // t, RWKV_HEADS // hb
    nx = WKV_CHUNKS_PER_STEP * RWKV_HEADS

    def body(r_ref, v_ref, lw_ref, k_ref, a_ref, b_ref, s0_ref, x_ref, dy_ref, dr_ref, dlw_ref, dk_ref, dv_ref, da_ref, db_ref, dstate):
        @pl.when(pl.program_id(1) == 0)
        def _():
            dstate[...] = jnp.zeros_like(dstate)

        _, vjp = jax.vjp(functools.partial(_wkv_chunk, inv=x_ref[0]), s0_ref[0],
                         *[_heads(t_ref[...], hb) for t_ref in (r_ref, lw_ref, k_ref, v_ref, a_ref, b_ref)])
        grads = vjp((_heads(dy_ref[...], hb), dstate[...]))
        dstate[...] = grads[0]
        for o_ref, gval in zip((dr_ref, dlw_ref, dk_ref, dv_ref, da_ref, db_ref), grads[1:]):
            o_ref[...] = _unheads(gval)

    def col(off):
        return pl.BlockSpec((t, w), functools.partial(lambda g, i, off: (nc - 1 - i, g + off), off=off))

    return pl.pallas_call(
        body, name="wkv_bwd", grid=(ng, nc),
        in_specs=[col(0), col(2 * ng), col(0), col(0), col(0), col(0),
                  pl.BlockSpec((1, hb, HEAD, HEAD), lambda g, i: (nc - 1 - i, g, 0, 0)),
                  pl.BlockSpec((1, nx, WKV_CHUNK, WKV_CHUNK), lambda g, i: (nc - 1 - i, 0, 0, 0)), col(0)],
        out_specs=[col(0)] * 6,
        out_shape=[SDS((s, RWKV_DIM), F32)] * 6,
        scratch_shapes=[pltpu.VMEM((hb, HEAD, HEAD), F32)],
        compiler_params=pltpu.CompilerParams(dimension_semantics=("parallel", "arbitrary"), vmem_limit_bytes=VMEM_LIMIT),
    )(zs, zs, lw, k2, na, b, s0s, invs, dy)


def _attn_batch(refs, bps, nh, first_has_prev):
    q_ref, kp_ref, kc_ref, vp_ref, vc_ref = refs

    def blocks(cur_ref, prev_ref=None):
        out = []
        for b in range(bps):
            if prev_ref is None:
                t = cur_ref[b * BAND:(b + 1) * BAND, :]
            else:
                t = prev_ref[...] if b == 0 else cur_ref[(b - 1) * BAND:b * BAND, :]
            out.append(_heads(t.astype(F32), nh))
        return out[0] if bps == 1 else jnp.concatenate(out, axis=0)

    batch = lax.broadcasted_iota(jnp.int32, (bps * nh, 1, 1), 0)
    has_prev = jnp.logical_or(batch >= nh, first_has_prev)
    return (blocks(q_ref), blocks(kc_ref, kp_ref), blocks(kc_ref), blocks(vc_ref, vp_ref), blocks(vc_ref)), has_prev


def _attn_rows(x, bps, nh):
    parts = [_unheads(x[b * nh:(b + 1) * nh]) for b in range(bps)]
    return parts[0] if bps == 1 else jnp.concatenate(parts, axis=0)


def _attn_fwd(q, k, v, d):
    s = q.shape[0]
    l = s // d
    nb = l // BAND
    assert nb * BAND == l
    qv, kv, vv = (t.reshape(l, d * GROUP_DIM) for t in (q, k, v))
    width = min(d, ATTN_CLASSES_PER_STEP) * GROUP_DIM
    bps = ATTN_CLASSES_PER_STEP * GROUP_DIM // width
    nh = width // HEAD

    def body(q_ref, kp_ref, kc_ref, vp_ref, vc_ref, o_ref, l_ref):
        ops, has_prev = _attn_batch((q_ref, kp_ref, kc_ref, vp_ref, vc_ref), bps, nh, pl.program_id(1) > 0)
        o, lse = _attn_block(*ops, has_prev)
        o_ref[...] = _attn_rows(o, bps, nh)
        l_ref[...] = _attn_rows(lse, bps, nh)

    cur = pl.BlockSpec((bps * BAND, width), lambda rho, i: (i, rho))
    prev = pl.BlockSpec((BAND, width), lambda rho, i: (jnp.maximum(i * bps - 1, 0), rho))
    o, lse = pl.pallas_call(
        body, name=f"attn_fwd_d{d}", grid=(d * GROUP_DIM // width, nb // bps),
        in_specs=[cur, prev, cur, prev, cur], out_specs=[cur, cur],
        out_shape=[SDS((l, d * GROUP_DIM), F32), SDS((l, d * GROUP_DIM), F32)],
        compiler_params=pltpu.CompilerParams(dimension_semantics=("parallel", "arbitrary"), vmem_limit_bytes=VMEM_LIMIT),
    )(qv, kv, kv, vv, vv)
    return o.reshape(s, GROUP_DIM), lse.reshape(s, GROUP_DIM)


def _attn_bwd(q, k, v, d, do, dlse):
    s = q.shape[0]
    l = s // d
    nb = l // BAND
    qv, kv, vv, dov, dlv = (t.reshape(l, d * GROUP_DIM) for t in (q, k, v, do, dlse))
    width = min(d, ATTN_CLASSES_PER_STEP) * GROUP_DIM
    bps = ATTN_CLASSES_PER_STEP * GROUP_DIM // width
    nh = width // HEAD
    ns = nb // bps

    def body(q_ref, kp_ref, kc_ref, vp_ref, vc_ref, do_ref, dl_ref, dq_ref, dk_ref, dv_ref, ck, cv):
        step = pl.program_id(1)

        @pl.when(step == 0)
        def _():
            ck[...] = jnp.zeros_like(ck)
            cv[...] = jnp.zeros_like(cv)

        ops, has_prev = _attn_batch((q_ref, kp_ref, kc_ref, vp_ref, vc_ref), bps, nh, step < ns - 1)
        _, vjp = jax.vjp(functools.partial(_attn_block, has_prev=has_prev), *ops)
        cts = [jnp.concatenate([_heads(t_ref[b * BAND:(b + 1) * BAND, :], nh) for b in range(bps)], axis=0) if bps > 1
               else _heads(t_ref[...], nh) for t_ref in (do_ref, dl_ref)]
        dq, dkp, dkc, dvp, dvc = vjp(tuple(cts))
        dq_ref[...] = _attn_rows(dq, bps, nh)
        for out_ref, cur_part, prev_part, carry in ((dk_ref, dkc, dkp, ck), (dv_ref, dvc, dvp, cv)):
            for b in range(bps):
                after = carry[...] if b == bps - 1 else _unheads(prev_part[(b + 1) * nh:(b + 2) * nh])
                out_ref[b * BAND:(b + 1) * BAND, :] = _unheads(cur_part[b * nh:(b + 1) * nh]) + after
            carry[...] = _unheads(prev_part[0:nh])

    cur = pl.BlockSpec((bps * BAND, width), lambda rho, i: (ns - 1 - i, rho))
    prev = pl.BlockSpec((BAND, width), lambda rho, i: (jnp.maximum((ns - 1 - i) * bps - 1, 0), rho))
    dq, dk, dv = pl.pallas_call(
        body, name=f"attn_bwd_d{d}", grid=(d * GROUP_DIM // width, ns),
        in_specs=[cur, prev, cur, prev, cur, cur, cur], out_specs=[cur] * 3,
        out_shape=[SDS((l, d * GROUP_DIM), F32)] * 3,
        scratch_shapes=[pltpu.VMEM((BAND, width), F32), pltpu.VMEM((BAND, width), F32)],
        compiler_params=pltpu.CompilerParams(dimension_semantics=("parallel", "arbitrary"), vmem_limit_bytes=VMEM_LIMIT),
    )(qv, kv, kv, vv, vv, dov, dlv)
    return dq.reshape(s, GROUP_DIM), dk.reshape(s, GROUP_DIM), dv.reshape(s, GROUP_DIM)


def _coords():
    return lax.axis_index("x"), lax.axis_index("y"), lax.axis_index("c")


_CHIP_FLIPS = ((1, 0), (0, 1), (1, 1))


def _flip(v, f):
    return 1 - v if f else v


def _form(kind, r, c):
    return (N_CHIPS, r, c) if kind == "blk" else (r, N_CHIPS * c)


def _slot(ref, kind, j, rows, c):
    if kind == "blk":
        return ref.at[j] if rows is None else ref.at[j, rows]
    cols = pl.ds(pl.multiple_of(j * c, 128), c)
    return ref.at[:, cols] if rows is None else ref.at[rows, cols]


def _half(r, which, align):
    return pl.ds(pl.multiple_of(which * (r // 2), align), r // 2)


def _rcopy(src, dst, send_sems, recv_sems, kk, dev):
    return pltpu.make_async_remote_copy(src_ref=src, dst_ref=dst, send_sem=send_sems.at[kk], recv_sem=recv_sems.at[kk],
                                        device_id=dev, device_id_type=MESH)


def _gather_plan(specs, step):
    def copies(refs, ss, rs, received):
        x, y, c = _coords()
        out = []
        for w, (kind, r, cc) in enumerate(specs):
            mine, other = _half(r, c, 16), _half(r, 1 - c, 16)
            for kk, (fx, fy) in enumerate(_CHIP_FLIPS):
                px, py = _flip(x, fx), _flip(y, fy)
                if step == "ici":
                    sl = _slot(refs[w], kind, 2 * px + py if received else 2 * x + y, mine, cc)
                    dev = (px, py, c)
                else:
                    sl = _slot(refs[w], kind, 2 * px + py, other if received else mine, cc)
                    dev = (x, y, 1 - c)
                out.append(_rcopy(sl, sl, ss, rs, 3 * w + kk, dev))
        return out

    def issue(refs, ss, rs):
        return copies(refs, ss, rs, False)

    def expect(refs, ss, rs):
        return copies(refs, ss, rs, False), copies(refs, ss, rs, True)

    return issue, expect


_HBM = pl.BlockSpec(memory_space=pltpu.HBM)
_SEM = pl.BlockSpec(memory_space=pltpu.SEMAPHORE)
_EFFECT = pltpu.SideEffectType.DATAFLOW_SIDE_EFFECTING


def _copies_start(name, bufs, n_sems, issue, after=None):
    nb = len(bufs)
    extra = [] if after is None else [after]

    def body(*refs):
        send_sems, recv_sems = refs[nb + len(extra)], refs[nb + len(extra) + 1]
        for cp in issue(refs[:nb], send_sems, recv_sems):
            cp.start()
        refs[-1][...] = jnp.zeros_like(refs[-1])

    outs = pl.pallas_call(
        body, name=name,
        out_shape=(pltpu.SemaphoreType.DMA((n_sems,)), pltpu.SemaphoreType.DMA((n_sems,)),
                   *[pltpu.HBM(b.shape, b.dtype) for b in bufs], SDS((8, 128), F32)),
        in_specs=[_HBM] * nb + [pl.BlockSpec(memory_space=pl.ANY)] * len(extra),
        out_specs=(_SEM, _SEM, *[_HBM] * nb, pl.BlockSpec(memory_space=pltpu.VMEM)),
        input_output_aliases={i: 2 + i for i in range(nb)},
        compiler_params=pltpu.CompilerParams(has_side_effects=_EFFECT),
    )(*[pltpu.with_memory_space_constraint(b, pltpu.HBM) for b in bufs], *extra)
    return outs[0], outs[1], list(outs[2:2 + nb]), outs[-1]


def _copies_wait(name, bufs, send_sems, recv_sems, after, expect):
    nb = len(bufs)
    extra = list(after) if isinstance(after, (list, tuple)) else [after]

    def body(*refs):
        sent, received = expect(refs[:nb], refs[nb], refs[nb + 1])
        for cp in sent:
            cp.wait_send()
        for cp in received:
            cp.wait_recv()

    outs = pl.pallas_call(
        body, name=name,
        out_shape=tuple(pltpu.HBM(b.shape, b.dtype) for b in bufs),
        in_specs=(*[_HBM] * nb, _SEM, _SEM, *[pl.BlockSpec(memory_space=pl.ANY)] * len(extra)), out_specs=tuple([_HBM] * nb),
        input_output_aliases={i: i for i in range(nb)},
        compiler_params=pltpu.CompilerParams(has_side_effects=_EFFECT),
    )(*bufs, send_sems, recv_sems, *extra)
    return list(outs)


def _add_pair(g, recv, kind, r, c, c_arr, name):
    h = r // 2
    if kind == "blk":
        tr = _row_tile(h, 512)
        grid = (N_CHIPS, h // tr)
        g_spec = pl.BlockSpec((1, 1, tr, c), lambda j, i, c_ref: (j, c_ref[0], i, 0))
        o_spec = pl.BlockSpec((1, tr, c), lambda j, i, c_ref: (j, i, 0))
        gv, oshape = g.reshape(N_CHIPS, 2, h, c), (N_CHIPS, h, c)
    else:
        tr = _row_tile(h, 64)
        grid = (h // tr,)
        g_spec = pl.BlockSpec((1, tr, N_CHIPS * c), lambda i, c_ref: (c_ref[0], i, 0))
        o_spec = pl.BlockSpec((tr, N_CHIPS * c), lambda i, c_ref: (i, 0))
        gv, oshape = g.reshape(2, h, N_CHIPS * c), (h, N_CHIPS * c)

    def body(c_ref, g_ref, r_ref, o_ref, ob_ref):
        v = (g_ref[:, 0] if kind == "blk" else g_ref[0]) + r_ref[...]
        o_ref[...] = v
        ob_ref[...] = v.astype(BF16)

    return pl.pallas_call(
        body, name=name,
        grid_spec=pltpu.PrefetchScalarGridSpec(num_scalar_prefetch=1, grid=grid, in_specs=[g_spec, o_spec], out_specs=[o_spec] * 2),
        out_shape=[SDS(oshape, F32), SDS(oshape, BF16)],
        compiler_params=pltpu.CompilerParams(vmem_limit_bytes=VMEM_LIMIT),
    )(c_arr, gv, recv)


def _sum_chips(pair, recv, kind, r, c, mc_arr, name):
    h = r // 2
    tr = _row_tile(h, 512)
    nt = h // tr
    if kind == "blk":
        p_spec = pl.BlockSpec((1, tr, c), lambda i, mc: (mc[0], i, 0))
    else:
        p_spec = pl.BlockSpec((tr, c), lambda i, mc: (i, mc[0]))

    def body(mc, a_ref, r_ref, g_out):
        own = a_ref[0] if kind == "blk" else a_ref[...]
        g_out[...] = ((own + r_ref[0].astype(F32)) + r_ref[1].astype(F32)) + r_ref[2].astype(F32)

    return pl.pallas_call(
        body, name=name,
        grid_spec=pltpu.PrefetchScalarGridSpec(
            num_scalar_prefetch=1, grid=(nt,), in_specs=[p_spec, pl.BlockSpec((3, tr, c), lambda i, mc: (0, i, 0))],
            out_specs=pl.BlockSpec((tr, c), lambda i, mc: (mc[1] * nt + i, 0))),
        out_shape=SDS((r, c), F32),
        compiler_params=pltpu.CompilerParams(vmem_limit_bytes=VMEM_LIMIT),
    )(mc_arr, pair, recv)


class _GroupReduce:
    def __init__(self, tag, specs, c_arr, mc_arr):
        self.tag, self.specs, self.c_arr, self.mc_arr = tag, specs, c_arr, mc_arr
        self.n = len(specs)

    def _plan(self, step):
        specs, n = self.specs, self.n

        def copies(refs, ss, rs, received):
            x, y, c = _coords()
            sib, out = (x, y, 1 - c), []
            for w, (_, kind, r, cc) in enumerate(specs):
                if step == "join":
                    there = refs[w].at[_half(r, 1 - c if received else c, 8)]
                    out.append(_rcopy(there, there, ss, rs, w, sib))
                    continue
                src, land = refs[w], refs[n + w]
                if step == "swap":
                    rows = _half(r, 1 - c, 8)
                    part = src.at[:, rows] if kind == "blk" else src.at[rows]
                    out.append(_rcopy(land if received else part, land, ss, rs, w, sib))
                else:
                    for kk, (fx, fy) in enumerate(_CHIP_FLIPS):
                        px, py = _flip(x, fx), _flip(y, fy)
                        part = land.at[kk] if received else _slot(src, kind, 2 * px + py, None, cc)
                        out.append(_rcopy(part, land.at[kk], ss, rs, 3 * w + kk, (px, py, c)))
            return out

        def issue(refs, ss, rs):
            return copies(refs, ss, rs, False)

        def expect(refs, ss, rs):
            return copies(refs, ss, rs, False), copies(refs, ss, rs, True)

        return issue, expect

    def swap_start(self, grads, after=None):
        lands = [lax.empty(_form(kind, r // 2, c), F32) for _, kind, r, c in self.specs]
        ss, rs, bufs, tok = _copies_start(f"rs_{self.tag}_swap", list(grads) + lands, self.n, self._plan("swap")[0], after=after)
        self.state = (ss, rs, bufs)
        return tok

    def swap_wait_ici_start(self, after):
        ss, rs, bufs = self.state
        bufs = _copies_wait(f"rs_{self.tag}_swap_wait", bufs, ss, rs, after, self._plan("swap")[1])
        pairs = [_add_pair(bufs[w], bufs[self.n + w], kind, r, c, self.c_arr, name=f"rs_{self.tag}_pair_{nm}")
                 for w, (nm, kind, r, c) in enumerate(self.specs)]
        self.pair = [pr[0] for pr in pairs]
        lands = [lax.empty((3, r // 2, c), BF16) for _, _, r, c in self.specs]
        ss, rs, bufs, tok = _copies_start(f"rs_{self.tag}_ici", [pr[1] for pr in pairs] + lands, 3 * self.n, self._plan("ici")[0])
        self.state = (ss, rs, bufs)
        return tok

    def ici_wait_join_start(self, after):
        ss, rs, bufs = self.state
        bufs = _copies_wait(f"rs_{self.tag}_ici_wait", bufs, ss, rs, after, self._plan("ici")[1])
        outs = [_sum_chips(self.pair[w], bufs[self.n + w], kind, r, c, self.mc_arr, name=f"rs_{self.tag}_sum_{nm}")
                for w, (nm, kind, r, c) in enumerate(self.specs)]
        ss, rs, bufs, tok = _copies_start(f"rs_{self.tag}_join", outs, self.n, self._plan("join")[0])
        self.state = (ss, rs, bufs)
        return tok

    def join_wait(self, after):
        ss, rs, bufs = self.state
        bufs = _copies_wait(f"rs_{self.tag}_join_wait", bufs, ss, rs, after, self._plan("join")[1])
        return {nm: bufs[w] for w, (nm, _, _, _) in enumerate(self.specs)}


def _small_gather_plan():
    def copies(refs, ss, rs, received):
        x, y, c = _coords()
        own, land = refs
        out = []
        for kk in range(1, 8):
            px, py, pc = _flip(x, (kk >> 2) & 1), _flip(y, (kk >> 1) & 1), _flip(c, kk & 1)
            there = land.at[4 * px + 2 * py + pc]
            out.append(_rcopy(there if received else own, there if received else land.at[4 * x + 2 * y + c], ss, rs, kk - 1,
                              (px, py, pc)))
        return out

    def issue(refs, ss, rs):
        return copies(refs, ss, rs, False)

    def expect(refs, ss, rs):
        return copies(refs, ss, rs, False), copies(refs, ss, rs, True)

    return issue, expect


def _sum_slots(slots):
    n, rows, cols = slots.shape

    def body(s_ref, o_ref):
        acc = s_ref[0]
        for j in range(1, n):
            acc = acc + s_ref[j]
        o_ref[...] = acc

    return pl.pallas_call(
        body, name="sum_small",
        in_specs=[pl.BlockSpec(memory_space=pltpu.VMEM)], out_specs=pl.BlockSpec(memory_space=pltpu.VMEM),
        out_shape=SDS((rows, cols), F32),
    )(slots)


def _adamw_rows(w, g, m, v):
    m = ADAM_B1 * m + (1.0 - ADAM_B1) * g
    v = ADAM_B2 * v + (1.0 - ADAM_B2) * jnp.square(g)
    m_hat = m / (1.0 - ADAM_B1 ** ADAM_STEP)
    v_hat = v / (1.0 - ADAM_B2 ** ADAM_STEP)
    return -ADAM_LR * (m_hat / (jnp.sqrt(v_hat) + ADAM_EPS) + ADAM_WD * w), m, v


def _adamw(w, g, m, v, name, dep=None, with_grad=False):
    rows, cols = w.shape
    tm = _pick(rows, (256, 128, 64, 16, 8))
    f = (lambda wt, gt, mt, vt: (gt,) + _adamw_rows(wt, gt, mt, vt)) if with_grad else _adamw_rows
    return _rows_call(f, [(t, 0, cols) for t in (w, g, m, v)], [], [(cols, F32)] * (3 + with_grad), tm=tm, name=name, dep=dep)


def _pack_small(parts):
    flat = jnp.concatenate([parts[n].reshape(-1) for n, _ in SMALL])
    return jnp.pad(flat, (0, SMALL_ROWS * PACK_COLS - flat.shape[0])).reshape(SMALL_ROWS, PACK_COLS)


def _unpack_small(buf, shapes):
    flat, out, off = buf.reshape(-1), {}, 0
    for n, sz in SMALL:
        out[n] = flat[off:off + sz].reshape(shapes[n])
        off += sz
    return out


def _lora_stack(parts):
    return jnp.concatenate([parts[n] for n, _ in LORA], axis=-2)


def _lora_split(stacked):
    out, off = {}, 0
    for n, rows in LORA:
        out[n] = stacked[..., off:off + rows, :]
        off += rows
    return out


def _ffn_gate_up(h, wgt, wut, name, dep=None):
    s, d = h.shape
    nblk, f, _ = wgt.shape
    tm = _pick(s, (1024, 512, 256))
    dn = (((1,), (1,)), ((), ()))

    def body(h_ref, wg_ref, wu_ref, *rest):
        g_ref, u_ref, a_ref = rest[-3:]
        hh = h_ref[...]
        g = lax.dot_general(hh, wg_ref[0], dn, preferred_element_type=F32)
        u = lax.dot_general(hh, wu_ref[0], dn, preferred_element_type=F32)
        g_ref[0], u_ref[0] = g.astype(BF16), u.astype(BF16)
        a_ref[0] = _swiglu_act(g, u).astype(BF16)

    w_spec = pl.BlockSpec((1, f, d), lambda j, i: (j, 0, 0))
    o_spec = pl.BlockSpec((1, tm, f), lambda j, i: (j, i, 0))
    extra = [] if dep is None else [dep]
    return pl.pallas_call(
        body, name=name, grid=(nblk, s // tm),
        in_specs=[pl.BlockSpec((tm, d), lambda j, i: (i, 0)), w_spec, w_spec] + [pl.BlockSpec(memory_space=pl.ANY)] * len(extra),
        out_specs=[o_spec] * 3,
        out_shape=[SDS((nblk, s, f), BF16)] * 3,
        compiler_params=pltpu.CompilerParams(dimension_semantics=("parallel", "parallel"), vmem_limit_bytes=VMEM_LIMIT),
    )(h, wgt, wut, *extra)


def _ffn_down_dx(dx_bf, wd, gate, up, name, dep=None):
    s, d = dx_bf.shape
    nblk, f, _ = wd.shape
    tm = _pick(s, (1024, 512, 256))
    dn = (((1,), (1,)), ((), ()))

    def body(dx_ref, wd_ref, g_ref, u_ref, *rest):
        dg_ref, du_ref = rest[-2:]
        dact = 0.5 * lax.dot_general(dx_ref[...], wd_ref[0], dn, preferred_element_type=F32)
        _, vjp = jax.vjp(_swiglu_act, g_ref[0].astype(F32), u_ref[0].astype(F32))
        dg, du = vjp(dact)
        dg_ref[0], du_ref[0] = dg.astype(BF16), du.astype(BF16)

    o_spec = pl.BlockSpec((1, tm, f), lambda j, i: (j, i, 0))
    extra = [] if dep is None else [dep]
    return pl.pallas_call(
        body, name=name, grid=(nblk, s // tm),
        in_specs=[pl.BlockSpec((tm, d), lambda j, i: (i, 0)), pl.BlockSpec((1, f, d), lambda j, i: (j, 0, 0)), o_spec, o_spec]
        + [pl.BlockSpec(memory_space=pl.ANY)] * len(extra),
        out_specs=[o_spec] * 2, out_shape=[SDS((nblk, s, f), BF16)] * 2,
        compiler_params=pltpu.CompilerParams(dimension_semantics=("parallel", "parallel"), vmem_limit_bytes=VMEM_LIMIT),
    )(dx_bf, wd, gate, up, *extra)


def _ffn_fwd(x, gain, wgt, wut, wd, tag, h=None, dep=None):
    if h is None:
        h = _rows_call(_rms, [(x, 0, D_MODEL)], [gain], [(D_MODEL, BF16)], tm=512, name=f"{tag}_norm")[0]
    gate, up, act = _ffn_gate_up(h, wgt, wut, f"{tag}_gate_up", dep=dep)
    x_new = _mm(act, wd, sum_blocks=True, res=x, alpha=0.5, name=f"{tag}_down")
    return x_new, (x, h, gate, up, act)


def _ffn_bwd(dx_new, dx_new_bf, saved, gain, wgt, wut, wd, tag, dep=None, hooks=None):
    x, h, gate, up, act = saved
    hooks = hooks or {}

    def hook(name, *vals):
        return hooks[name](*vals) if name in hooks else None

    d_wd = _mm(act, dx_new_bf, ta=True, alpha=0.5, name=f"{tag}_down_dw")
    dep = hook("down", d_wd) if "down" in hooks else dep
    dgate, dup = _ffn_down_dx(dx_new_bf, wd, gate, up, f"{tag}_down_dx", dep=dep)
    d_wgt = _mm(dgate, h, ta=True, dep=hook("mid", dgate), name=f"{tag}_gate_dw")
    d_wut = _mm(dup, h, ta=True, name=f"{tag}_up_dw")
    dh = _mm(dgate, wgt, sum_blocks=True, dep=hook("dw", d_wgt, d_wut), name=f"{tag}_gate_dx")
    dx, dx_bf, dgain = _mm(dup, wut, sum_blocks=True, res=dh, dep=hook("dx", dh), post=_norm_bwd_post(x, gain, dx_new),
                           name=f"{tag}_up_dx")
    hook("end", dx_bf)
    return dx, dx_bf, dgain, d_wgt, d_wut, d_wd


def _norm_bwd_post(x, gain, dres):
    def f(dht, xt, drt, gt):
        _, vjp = jax.vjp(_rms, xt, gt)
        dxt, dgt = vjp(dht)
        return dxt + drt, dxt + drt, dgt

    return f, [x, dres], [gain], [F32, BF16], [(1, D_MODEL)]


def kernel(x, p, positions, ffn1_norm, ffn1_w_gate, ffn1_w_up, ffn1_w_down, mix_norm, w_in, rwkv_mu, rwkv_w0, rwkv_w2, rwkv_a0, rwkv_a2, rwkv_g2, rwkv_k_k, rwkv_k_a, rwkv_r_k, rwkv_gn_w, rwkv_gn_b, q_norm, k_norm, w_br_rwkv, w_br_attn, w_out, ffn2_norm, ffn2_w_gate, ffn2_w_up, ffn2_w_down, ple_norm, ple_w_gate, ple_w_proj, loss_target, m_ffn1_norm, m_ffn1_w_gate, m_ffn1_w_up, m_ffn1_w_down, m_mix_norm, m_w_in, m_rwkv_mu, m_rwkv_w0, m_rwkv_w2, m_rwkv_a0, m_rwkv_a2, m_rwkv_g2, m_rwkv_k_k, m_rwkv_k_a, m_rwkv_r_k, m_rwkv_gn_w, m_rwkv_gn_b, m_q_norm, m_k_norm, m_w_br_rwkv, m_w_br_attn, m_w_out, m_ffn2_norm, m_ffn2_w_gate, m_ffn2_w_up, m_ffn2_w_down, m_ple_norm, m_ple_w_gate, m_ple_w_proj, v_ffn1_norm, v_ffn1_w_gate, v_ffn1_w_up, v_ffn1_w_down, v_mix_norm, v_w_in, v_rwkv_mu, v_rwkv_w0, v_rwkv_w2, v_rwkv_a0, v_rwkv_a2, v_rwkv_g2, v_rwkv_k_k, v_rwkv_k_a, v_rwkv_r_k, v_rwkv_gn_w, v_rwkv_gn_b, v_q_norm, v_k_norm, v_w_br_rwkv, v_w_br_attn, v_w_out, v_ffn2_norm, v_ffn2_w_gate, v_ffn2_w_up, v_ffn2_w_down, v_ple_norm, v_ple_w_gate, v_ple_w_proj):
    args = dict(locals())
    wts = {n: args[n] for n in WEIGHTS}
    mom_m = {n: args["m_" + n] for n in WEIGHTS}
    mom_v = {n: args["v_" + n] for n in WEIGHTS}
    x0, tgt = x[0], loss_target[0]
    s = x0.shape[0]
    p_tok = p[0, 0]

    vec = {n: wts[n].reshape(1, -1) for n, _ in SMALL}
    xi, yi, ci = _coords()
    me = 2 * xi + yi
    def laid(t, n):
        return jnp.transpose(t[n][0]) if n in TRANSPOSED else t[n][0]

    shard_of = {n: laid(wts, n) for g in GROUPS.values() for n, _, _, _ in g if n != "lora"}
    shard_of["lora"] = _lora_stack({n: wts[n][0] for n, _ in LORA})

    def whole_with_own(n, kind, r, c, tok=None):
        at = (me, 0, 0) if kind == "blk" else (0, me * c)
        own = (shard_of[n] if tok is None else shard_of[n] + tok[0, 0]).astype(BF16)
        return lax.dynamic_update_slice(lax.empty(_form(kind, r, c), BF16), own[None] if kind == "blk" else own, at)

    specs = {g: [(kind, r, c) for _, kind, r, c in grp] for g, grp in GROUPS.items()}
    plans = {(g, st): _gather_plan(specs[g], st) for g in GROUPS for st in ("ici", "d2d")}
    buf_f1 = [whole_with_own(*w) for w in GROUPS["f1"]]
    ss_0, rs_0, buf_f1, tok_0 = _copies_start("gather_f1_ici", buf_f1, 3 * len(buf_f1), plans["f1", "ici"][0])
    bufs = {g: [whole_with_own(*w, tok=tok_0) for w in GROUPS[g]] for g in ("mx", "f2")}
    buf_f1 = _copies_wait("gather_f1_ici_wait", buf_f1, ss_0, rs_0, bufs["mx"] + bufs["f2"], plans["f1", "ici"][1])
    ss_1, rs_1, buf_f1, tok_1 = _copies_start("gather_f1_d2d", buf_f1, 3 * len(buf_f1), plans["f1", "d2d"][0])
    ss_a, rs_a, buf_mx, tok_a = _copies_start("gather_mx_ici", bufs["mx"], 3 * len(bufs["mx"]), plans["mx", "ici"][0],
                                              after=tok_1)
    h1 = _rows_call(_rms, [(x0, 0, D_MODEL)], [vec["ffn1_norm"] + tok_a[0, 0]], [(D_MODEL, BF16)], tm=512, name="ffn1_norm")[0]
    buf_f1 = _copies_wait("gather_f1_d2d_wait", buf_f1, ss_1, rs_1, h1, plans["f1", "d2d"][1])
    wb = dict(zip([w[0] for w in GROUPS["f1"]], buf_f1))

    inv_freq = 1.0 / (ROPE_THETA ** (jnp.arange(0, HEAD, 2, dtype=F32) / HEAD))
    ang = positions[0].astype(F32)[:, None] * inv_freq
    cos, sin = jnp.cos(ang), jnp.sin(ang)
    cos2, sin2 = jnp.concatenate([cos, cos], axis=1), jnp.concatenate([-sin, sin], axis=1)

    x1, ffn1_saved = _ffn_fwd(x0, vec["ffn1_norm"], wb["ffn1_w_gate"], wb["ffn1_w_up"], wb["ffn1_w_down"], "ffn1", h=h1, dep=tok_a)
    buf_mx = _copies_wait("gather_mx_ici_wait", buf_mx, ss_a, rs_a, x1, plans["mx", "ici"][1])
    ss_b, rs_b, buf_mx, tok_b = _copies_start("gather_mx_d2d", buf_mx, 3 * len(buf_mx), plans["mx", "d2d"][0])
    ss_c, rs_c, buf_f2, tok_c = _copies_start("gather_f2_ici", bufs["f2"], 3 * len(bufs["f2"]), plans["f2", "ici"][0])
    h = _rows_call(_rms, [(x1, 0, D_MODEL)], [vec["mix_norm"] + (tok_b[0, 0] + tok_c[0, 0])], [(D_MODEL, BF16)], tm=256,
                   name="mix_norm")[0]
    buf_mx = _copies_wait("gather_mx_d2d_wait", buf_mx, ss_b, rs_b, h, plans["mx", "d2d"][1])
    wb.update(zip([w[0] for w in GROUPS["mx"]], buf_mx))
    w_in_all = wb["w_in"]
    w_in_r, w_in_a, w_in_g = w_in_all[:, :RWKV_COLS], w_in_all[:, RWKV_COLS:RWKV_COLS + ATTN_COLS], w_in_all[:, RWKV_COLS + ATTN_COLS:]
    lora = _lora_split(wb["lora"])
    w2, a2, g2 = lora["rwkv_w2"], lora["rwkv_a2"], lora["rwkv_g2"]
    z_r = _mm(h, w_in_r, name="in_rwkv")
    z_a = _mm(h, w_in_a, name="in_attn")
    z_g = _mm(h, w_in_g, name="in_gate")

    zs = _shift_fwd(z_r, vec["rwkv_mu"])
    pre_params = [vec["rwkv_w0"], w2, vec["rwkv_a0"], a2, g2, vec["rwkv_k_k"], vec["rwkv_k_a"]]
    def pre_fwd(*t):
        res = _rwkv_pre(*t)
        return res[1], res[2], res[4], res[5], res[6]

    lw, k2, na, kb, gate_r = _rows_call(pre_fwd, [(zs, 0, RWKV_COLS)], pre_params, [(RWKV_DIM, F32)] * 5, tm=512, name="rwkv_pre")
    y_scan, s0s, invs = _wkv_fwd(zs, lw, k2, na, kb)
    buf_f2 = _copies_wait("gather_f2_ici_wait", buf_f2, ss_c, rs_c, y_scan, plans["f2", "ici"][1])
    ss_d, rs_d, buf_f2, tok_d = _copies_start("gather_f2_d2d", buf_f2, 3 * len(buf_f2), plans["f2", "d2d"][0])
    post_params = [vec["rwkv_gn_w"] + tok_d[0, 0], vec["rwkv_gn_b"], vec["rwkv_r_k"]]
    post_rows = [(y_scan, 0, RWKV_DIM), (zs, 0, RWKV_DIM), (k2, 0, RWKV_DIM), (zs, 2, RWKV_DIM), (gate_r, 0, RWKV_DIM)]
    y_rwkv = _rows_call(_rwkv_post, post_rows, post_params, [(RWKV_DIM, BF16)], tm=512, name="rwkv_post")[0]
    buf_f2 = _copies_wait("gather_f2_d2d_wait", buf_f2, ss_d, rs_d, y_rwkv, plans["f2", "d2d"][1])
    wb.update(zip([w[0] for w in GROUPS["f2"]], buf_f2))
    w_brr, w_bra = wb["w_br_rwkv"], wb["w_br_attn"]
    w_o = wb["w_out"].reshape(D_MODEL, D_MODEL)
    w_pp, w_pg = wb["ple_w_proj"], wb["ple_w_gate"].reshape(D_MODEL, D_MODEL)

    def qk_fwd(qt, kt, ct, st, qg, kg):
        return _norm_rope(qt, qg, ct, st), _norm_rope(kt, kg, ct, st)

    qk_rows = [(z_a, 0, ATTN_DIM), (z_a, 1, ATTN_DIM), (cos2, 0, HEAD), (sin2, 0, HEAD)]
    q_rot, k_rot = _rows_call(qk_fwd, qk_rows, [vec["q_norm"], vec["k_norm"]], [(ATTN_DIM, BF16)] * 2, tm=512, name="attn_pre")
    def group(t, g, off=0):
        return t[:, off + g * GROUP_DIM:off + (g + 1) * GROUP_DIM].astype(BF16)

    qkv = [(group(q_rot, g), group(k_rot, g), group(z_a, g, 2 * ATTN_DIM)) for g in range(len(ATTN_DILATIONS))]
    outs, lses = zip(*[_attn_fwd(*qkv[g], d) for g, d in enumerate(ATTN_DILATIONS)])
    comb_rows = [(t, 0, GROUP_DIM) for t in outs + lses]
    y_attn = _rows_call(_attn_combine, comb_rows, [], [(GROUP_DIM, BF16)], tm=512, name="attn_combine")[0]

    br = _mm(y_rwkv, w_brr, name="branch_rwkv")
    ba = _mm(y_attn, w_bra, name="branch_attn")
    merge_rows = [(z_g, 0, D_MODEL), (z_g, 1, D_MODEL), (br, 0, D_MODEL), (ba, 0, D_MODEL)]
    merged = _rows_call(_merge, merge_rows, [], [(D_MODEL, BF16)], tm=512, name="merge")[0]
    x2 = _mm(merged, w_o, res=x1, name="out_proj")
    x3, ffn2_saved = _ffn_fwd(x2, vec["ffn2_norm"], wb["ffn2_w_gate"], wb["ffn2_w_up"], wb["ffn2_w_down"], "ffn2")
    hp = _rows_call(_rms, [(x3, 0, D_MODEL)], [vec["ple_norm"]], [(D_MODEL, BF16)], tm=512, name="ple_norm")[0]
    pg = _mm(hp, w_pg, name="ple_gate")
    pp = _mm(p_tok, w_pp, name="ple_proj")

    def head(x3t, pgt, ppt, tt):
        sg = _sigmoid(pgt)
        err = x3t + sg * ppt - tt
        dx4 = err * (1.0 / D_MODEL)
        loss = 0.5 * jnp.sum(jnp.mean(err * err, axis=-1, keepdims=True), axis=0, keepdims=True)
        return dx4, dx4 * ppt * sg * (1.0 - sg), dx4 * sg, jnp.broadcast_to(loss, (8, 128))

    head_rows = [(x3, 0, D_MODEL), (pg, 0, D_MODEL), (pp, 0, D_MODEL), (tgt, 0, D_MODEL)]
    dx4, dpg, dpp, loss_tile = _rows_call(head, head_rows, [], [(D_MODEL, F32), (D_MODEL, BF16), (D_MODEL, BF16)], [(8, 128)],
                                          tm=512, name="ple_loss")

    c_arr = jnp.reshape(ci, (1,)).astype(jnp.int32)
    mc_arr = jnp.stack([me, ci]).astype(jnp.int32)
    red = {g: _GroupReduce(g, grp, c_arr, mc_arr) for g, grp in REDUCE_GROUPS.items()}

    done = {}

    def update(summed):
        for n, g2d in summed.items():
            if n == "lora":
                w_, m_, v_ = (_lora_stack({k: t[k][0] for k, _ in LORA}) for t in (wts, mom_m, mom_v))
            else:
                w_, m_, v_ = laid(wts, n), laid(mom_m, n), laid(mom_v, n)
            done[n] = _adamw(w_, g2d, m_, v_, name=f"adamw_{n}", with_grad=True)
    gw, gs = {}, {}
    gw["ple_w_proj"] = _mm(p_tok, dpp, ta=True, name="ple_proj_dw")
    gw["ple_w_gate"] = _mm(hp, dpg, ta=True, name="ple_gate_dw")
    dx3, dx3_bf, gs["ple_norm"] = _mm(dpg, w_pg, tb=True, post=_norm_bwd_post(x3, vec["ple_norm"], dx4), name="ple_gate_dx")
    dx2, dx2_bf, gs["ffn2_norm"], gw["ffn2_w_gate"], gw["ffn2_w_up"], gw["ffn2_w_down"] = _ffn_bwd(
        dx3, dx3_bf, ffn2_saved, vec["ffn2_norm"], wb["ffn2_w_gate"], wb["ffn2_w_up"], wb["ffn2_w_down"], "ffn2")
    gw["ple_w_gate"] = gw["ple_w_gate"].reshape(N_CHIPS, D_MODEL // N_CHIPS, D_MODEL)
    tok = red["f2"].swap_start([gw[w[0]] for w in REDUCE_GROUPS["f2"]])
    gw["w_out"] = _mm(merged, dx2_bf, ta=True, name="out_proj_dw")
    dmerged = _mm(dx2_bf, w_o, tb=True, dep=tok, name="out_proj_dx")

    def merge_bwd(zgr, zga, brt, bat, ct):
        _, vjp = jax.vjp(_merge, zgr, zga, brt, bat)
        d1, d2, d3, d4 = vjp(ct)
        return jnp.concatenate([d1, d2], axis=1), d3, d4

    dz_g, dbr, dba = _rows_call(merge_bwd, merge_rows + [(dmerged, 0, D_MODEL)], [],
                                [(2 * D_MODEL, BF16), (D_MODEL, BF16), (D_MODEL, BF16)], tm=512, name="merge_bwd")
    tok = red["f2"].swap_wait_ici_start(dz_g)
    gw["w_br_rwkv"] = _mm(y_rwkv, dbr, ta=True, name="branch_rwkv_dw")
    gw["w_br_attn"] = _mm(y_attn, dba, ta=True, name="branch_attn_dw")
    dy_rwkv = _mm(dbr, w_brr, tb=True, dep=tok, name="branch_rwkv_dx")
    dy_attn = _mm(dba, w_bra, tb=True, dep=tok, name="branch_attn_dx")

    def comb_bwd(*t):
        _, vjp = jax.vjp(_attn_combine, *t[:6])
        return vjp(t[6])

    dcomb = _rows_call(comb_bwd, comb_rows + [(dy_attn, 0, GROUP_DIM)], [], [(GROUP_DIM, F32)] * 6, tm=512, name="attn_combine_bwd")
    dqs, dks, dvs = zip(*[_attn_bwd(*qkv[g], d, dcomb[g], dcomb[3 + g]) for g, d in enumerate(ATTN_DILATIONS)])

    def qk_bwd(qt, kt, ct, st, *rest):
        dq = jnp.concatenate(rest[0:3], axis=1)
        dk = jnp.concatenate(rest[3:6], axis=1)
        qg, kg = rest[9], rest[10]
        _, vjp = jax.vjp(lambda a_, b_, c_, d_: qk_fwd(a_, b_, ct, st, c_, d_), qt, kt, qg, kg)
        dqt, dkt, dqg, dkg = vjp((dq, dk))
        return jnp.concatenate((dqt, dkt) + tuple(rest[6:9]), axis=1), dqg, dkg

    dz_a, gs["q_norm"], gs["k_norm"] = _rows_call(
        qk_bwd, qk_rows + [(t, 0, GROUP_DIM) for t in dqs + dks + dvs], [vec["q_norm"], vec["k_norm"]],
        [(ATTN_COLS, BF16)], [(1, HEAD), (1, HEAD)], tm=512, name="attn_pre_bwd")
    tok = red["f2"].ici_wait_join_start(dz_a)

    def post_bwd(*t):
        _, vjp = jax.vjp(_rwkv_post, *t[:5], *t[6:])
        return vjp(t[5])

    dy_scan, dr_post, dk2_post, dv_post, dgate_r, gs["rwkv_gn_w"], gs["rwkv_gn_b"], gs["rwkv_r_k"] = _rows_call(
        post_bwd, post_rows + [(dy_rwkv, 0, RWKV_DIM)], post_params, [(RWKV_DIM, F32)] * 5, [(1, RWKV_DIM)] * 3,
        tm=512, name="rwkv_post_bwd", dep=tok)
    update(red["f2"].join_wait(dy_scan))
    dr_s, dlw, dk2_s, dv_s, dna, dkb = _wkv_bwd(zs, lw, k2, na, kb, s0s, invs, dy_scan)

    def pre_bwd(zt, c_r1, c_r2, c_lw, c_k1, c_k2, c_v1, c_v2, c_a, c_b, c_g, *params):
        _, vjp = jax.vjp(_rwkv_pre, zt, *params)
        return vjp((c_r1 + c_r2, c_lw, c_k1 + c_k2, c_v1 + c_v2, c_a, c_b, c_g))

    pre_cts = [dr_s, dr_post, dlw, dk2_s, dk2_post, dv_s, dv_post, dna, dkb, dgate_r]
    dzs, gs["rwkv_w0"], g_w2, gs["rwkv_a0"], g_a2, g_g2, gs["rwkv_k_k"], gs["rwkv_k_a"] = _rows_call(
        pre_bwd, [(zs, 0, RWKV_COLS)] + [(t, 0, RWKV_DIM) for t in pre_cts], pre_params, [(RWKV_COLS, F32)],
        [q.shape for q in pre_params], tm=512, name="rwkv_pre_bwd")
    dz_r, gs["rwkv_mu"] = _shift_bwd(z_r, vec["rwkv_mu"], dzs)

    g_w_in = jnp.concatenate([_mm(h, dz_r, ta=True, name="in_rwkv_dw"), _mm(h, dz_a, ta=True, name="in_attn_dw"),
                              _mm(h, dz_g, ta=True, name="in_gate_dw")], axis=1)
    gw["w_in"], gw["lora"] = g_w_in, jnp.concatenate([g_w2, g_a2, g_g2], axis=0)
    gw["w_out"] = gw["w_out"].reshape(N_CHIPS, D_MODEL // N_CHIPS, D_MODEL)
    tok = red["mx"].swap_start([gw[w[0]] for w in REDUCE_GROUPS["mx"]])
    dh = _mm(dz_r, w_in_r, tb=True, dep=tok, name="in_rwkv_dx")
    dh = _mm(dz_a, w_in_a, tb=True, res=dh, name="in_attn_dx")
    dx1, dx1_bf, gs["mix_norm"] = _mm(dz_g, w_in_g, tb=True, res=dh, post=_norm_bwd_post(x1, vec["mix_norm"], dx2), name="in_gate_dx")
    tok_mx = red["mx"].swap_wait_ici_start(dx1_bf)
    hooks = {"down": lambda d_wd: red["f1d"].swap_start([d_wd], after=tok_mx),
             "mid": lambda dgate: red["f1d"].swap_wait_ici_start(dgate),
             "dw": lambda d_wgt, d_wut: red["f1g"].swap_start([d_wgt, d_wut]),
             "dx": lambda part: red["f1g"].swap_wait_ici_start(part) + red["f1d"].ici_wait_join_start(part),
             "end": lambda dx_: tokens.setdefault("mx_join", red["mx"].ici_wait_join_start(dx_))}
    tokens = {}
    dx0, _, gs["ffn1_norm"], gw["ffn1_w_gate"], gw["ffn1_w_up"], gw["ffn1_w_down"] = _ffn_bwd(
        dx1, dx1_bf, ffn1_saved, vec["ffn1_norm"], wb["ffn1_w_gate"], wb["ffn1_w_up"], wb["ffn1_w_down"], "ffn1", hooks=hooks)

    flat = jnp.concatenate([gs[n].reshape(-1) for n, _ in SMALL] + [loss_tile[0, 0:1]])
    small_buf = jnp.pad(flat, (0, SMALL_ROWS * PACK_COLS - flat.shape[0])).reshape(SMALL_ROWS, PACK_COLS)
    small_issue, small_expect = _small_gather_plan()
    ss_s, rs_s, small_bufs, tok_s = _copies_start("small_gather", [small_buf, lax.empty((8, SMALL_ROWS, PACK_COLS), F32)], 7,
                                                  small_issue, after=tokens["mx_join"])
    for g in ("mx", "f1d"):
        update(red[g].join_wait(tok_s))
    tok = red["f1g"].ici_wait_join_start(done["w_in"][1])
    small_buf, slots = _copies_wait("small_gather_wait", small_bufs, ss_s, rs_s, done["ffn1_w_down"][1], small_expect)
    small_sum = _sum_slots(lax.dynamic_update_slice(slots, small_buf[None], (4 * xi + 2 * yi + ci, 0, 0)))
    n_small = sum(sz for _, sz in SMALL)
    loss = small_sum.reshape(-1)[n_small]
    grad_small = _unpack_small(small_sum, {n: wts[n].shape for n, _ in SMALL})
    d_s, m_s, v_s = _adamw(_pack_small(wts), small_sum, _pack_small(mom_m), _pack_small(mom_v), name="adamw_small", dep=tok)
    shapes = {n: wts[n].shape for n, _ in SMALL}
    d_s, m_s, v_s = _unpack_small(d_s, shapes), _unpack_small(m_s, shapes), _unpack_small(v_s, shapes)
    grads, deltas, new_m, new_v = {}, {}, {}, {}
    for n, _ in SMALL:
        grads[n], deltas[n], new_m[n], new_v[n] = grad_small[n], d_s[n], m_s[n], v_s[n]
    update(red["f1g"].join_wait(m_s["ffn1_norm"]))
    for n, res in done.items():
        for store, val in zip((grads, deltas, new_m, new_v), res):
            if n == "lora":
                store.update({k: t[None] for k, t in _lora_split(val).items()})
            else:
                store[n] = (jnp.transpose(val) if n in TRANSPOSED else val)[None]

    return (loss, dx0[None], *[grads[n] for n in WEIGHTS], *[deltas[n] for n in WEIGHTS],
            *[new_m[n] for n in WEIGHTS], *[new_v[n] for n in WEIGHTS])
```

```python
import functools

import jax
import jax.numpy as jnp
from jax import lax
from jax.experimental import pallas as pl
from jax.experimental.pallas import tpu as pltpu

F32, BF16 = jnp.float32, jnp.bfloat16
HI = lax.Precision.HIGHEST
MESH = pl.DeviceIdType.MESH
SDS = jax.ShapeDtypeStruct

D_MODEL = 1024
HEAD = 64
RWKV_HEADS = 8
RWKV_DIM = RWKV_HEADS * HEAD
DECAY_LORA, ICLR_LORA, GATE_LORA = 64, 64, 128
GN_EPS = 64e-5
RMS_EPS = 1e-6
ATTN_DILATIONS = (1, 4, 16)
BAND = 128
ATTN_DIM = 768
GROUP_DIM = 256
ATTN_CLASSES_PER_STEP = 4
ROPE_THETA = 10000.0
NEG_INF = -1e30
RWKV_COLS = 3 * RWKV_DIM + DECAY_LORA + ICLR_LORA + GATE_LORA
ATTN_COLS = 3 * ATTN_DIM
ADAM_LR, ADAM_B1, ADAM_B2, ADAM_EPS, ADAM_WD, ADAM_STEP = 0.001, 0.9, 0.999, 1e-08, 0.01, 10

WKV_CHUNK = 64
WKV_HEADS_PER_STEP = 8
WKV_CHUNKS_PER_STEP = 4
N_CHIPS = 4
PACK_COLS = 1024
VMEM_LIMIT = 48 * 1024 * 1024

TRANSPOSED = ("ffn1_w_gate", "ffn1_w_up", "ffn2_w_gate", "ffn2_w_up")
LORA = (("rwkv_w2", 64), ("rwkv_a2", 64), ("rwkv_g2", 128))
_FFN1 = (("ffn1_w_gate", "blk", 704, 1024), ("ffn1_w_up", "blk", 704, 1024), ("ffn1_w_down", "blk", 704, 1024))
_FFN2 = (("ffn2_w_gate", "blk", 704, 1024), ("ffn2_w_up", "blk", 704, 1024), ("ffn2_w_down", "blk", 704, 1024))
_IN = (("w_in", "col", 1024, 1536), ("lora", "col", 256, 128))
_BRANCH = (("w_br_rwkv", "col", 512, 256), ("w_br_attn", "col", 256, 256), ("w_out", "blk", 256, 1024))
_PLE = (("ple_w_gate", "blk", 256, 1024), ("ple_w_proj", "col", 256, 256))
GROUPS = {"f1": _FFN1, "mx": _IN, "f2": _BRANCH + _FFN2 + _PLE}
REDUCE_GROUPS = {"f2": _FFN2 + _PLE, "mx": _IN + _BRANCH, "f1d": _FFN1[2:], "f1g": _FFN1[:2]}
SMALL = (
    ("ffn1_norm", 1024), ("mix_norm", 1024), ("ffn2_norm", 1024), ("ple_norm", 1024), ("rwkv_mu", 1792),
    ("rwkv_w0", 512), ("rwkv_a0", 512), ("rwkv_k_k", 512), ("rwkv_k_a", 512), ("rwkv_r_k", 512),
    ("rwkv_gn_w", 512), ("rwkv_gn_b", 512), ("q_norm", 64), ("k_norm", 64),
)
SMALL_ROWS = 16
WEIGHTS = (
    "ffn1_norm", "ffn1_w_gate", "ffn1_w_up", "ffn1_w_down", "mix_norm", "w_in", "rwkv_mu", "rwkv_w0", "rwkv_w2",
    "rwkv_a0", "rwkv_a2", "rwkv_g2", "rwkv_k_k", "rwkv_k_a", "rwkv_r_k", "rwkv_gn_w", "rwkv_gn_b", "q_norm", "k_norm",
    "w_br_rwkv", "w_br_attn", "w_out", "ffn2_norm", "ffn2_w_gate", "ffn2_w_up", "ffn2_w_down", "ple_norm",
    "ple_w_gate", "ple_w_proj",
)


def _row_tile(n, most=704):
    for t in range(most - most % 16, 0, -16):
        if n % t == 0:
            return t
    return n


def _pick(n, cands):
    for c in cands:
        if n % c == 0:
            return c
    return n


def _mm(a, b, *, ta=False, tb=False, sum_blocks=False, out_dtype=F32, res=None, alpha=1.0, dep=None, post=None, name):
    flat = a.ndim == 2 and b.ndim == 2
    a3 = a if a.ndim == 3 else a[None]
    b3 = b if b.ndim == 3 else b[None]
    na, nbb = a3.shape[0], b3.shape[0]
    nblk = max(na, nbb)
    kdim, m = (a3.shape[1], a3.shape[2]) if ta else (a3.shape[2], a3.shape[1])
    n = b3.shape[1] if tb else b3.shape[2]
    assert (b3.shape[2] if tb else b3.shape[1]) == kdim
    tm = _pick(m, (1024, 512, 256, 128) if post is None else (512, 256, 128))
    tn = _pick(n, (1024, 896, 768, 512, 256, 128))
    tk = kdim if kdim <= 2304 else _pick(kdim, (1024, 512, 256, 128))
    nk = kdim // tk
    direct = nk == 1 and not sum_blocks

    if sum_blocks:
        grid = (m // tm, n // tn, nblk, nk)

        def ids(i, c, j, k):
            return i, c, j, k
    else:
        grid = (nblk, m // tm, n // tn, nk)

        def ids(j, i, c, k):
            return i, c, j, k

    def amap(*g):
        i, c, j, k = ids(*g)
        jj = j if na > 1 else 0
        return (jj, k, i) if ta else (jj, i, k)

    def bmap(*g):
        i, c, j, k = ids(*g)
        jj = j if nbb > 1 else 0
        return (jj, c, k) if tb else (jj, k, c)

    if sum_blocks:
        oshape, oblk = (m, n), (tm, tn)

        def omap(*g):
            i, c, j, k = ids(*g)
            return i, c
    else:
        oshape, oblk = (nblk, m, n), (1, tm, tn)

        def omap(*g):
            i, c, j, k = ids(*g)
            return j, i, c

    dn = (((0 if ta else 1,), (1 if tb else 0,)), ((), ()))
    has_res = res is not None
    p_f, p_rows, p_params, p_dtypes, p_accs = post if post is not None else (None, [], [], [], [])
    assert post is None or sum_blocks or flat
    n_in = 2 + has_res + len(p_rows) + len(p_params) + (dep is not None)

    def tile_map(*g):
        i, c, j, k = ids(*g)
        return i, c

    def body(*refs):
        refs = list(refs)
        acc = None if direct else refs.pop()
        o_refs = refs[n_in:]
        a_ref, b_ref = refs[0], refs[1]
        r_ref = refs[2] if has_res else None
        pr_refs = refs[2 + has_res:2 + has_res + len(p_rows)]
        pp_refs = refs[2 + has_res + len(p_rows):2 + has_res + len(p_rows) + len(p_params)]
        first_tile = jnp.logical_and(pl.program_id(0 if sum_blocks else 1) == 0, pl.program_id(1 if sum_blocks else 2) == 0)

        def finish(v):
            if alpha != 1.0:
                v = v * alpha
            if has_res:
                v = v + r_ref[...].reshape(v.shape).astype(F32)
            if post is None:
                o_refs[0][...] = v.reshape(o_refs[0].shape).astype(o_refs[0].dtype)
                return
            outs = p_f(v, *[t[...] for t in pr_refs], *[t[...] for t in pp_refs])
            for o_ref, val in zip(o_refs, outs[:len(p_dtypes)]):
                o_ref[...] = val.astype(o_ref.dtype)
            for o_ref, val in zip(o_refs[len(p_dtypes):], outs[len(p_dtypes):]):
                @pl.when(first_tile)
                def _():
                    o_ref[...] = jnp.zeros_like(o_ref)

                o_ref[...] += val.reshape(o_ref.shape)

        if direct:
            finish(lax.dot_general(a_ref[0].astype(BF16), b_ref[0].astype(BF16), dn, preferred_element_type=F32))
            return
        k = pl.program_id(3)
        if sum_blocks:
            j = pl.program_id(2)
            first = jnp.logical_and(j == 0, k == 0)
            last = jnp.logical_and(j == nblk - 1, k == nk - 1)
        else:
            first, last = k == 0, k == nk - 1

        @pl.when(first)
        def _():
            acc[...] = jnp.zeros_like(acc)

        acc[...] += lax.dot_general(a_ref[0].astype(BF16), b_ref[0].astype(BF16), dn, preferred_element_type=F32)

        @pl.when(last)
        def _():
            finish(acc[...])

    in_specs = [pl.BlockSpec((1, tk, tm) if ta else (1, tm, tk), amap), pl.BlockSpec((1, tn, tk) if tb else (1, tk, tn), bmap)]
    args = [a3, b3]
    if has_res:
        res3 = res if (sum_blocks or res.ndim == 3) else res[None]
        in_specs.append(pl.BlockSpec(oblk, omap))
        args.append(res3)
    in_specs += [pl.BlockSpec((tm, tn), tile_map) for _ in p_rows]
    in_specs += [pl.BlockSpec(t.shape, functools.partial(lambda *g, nd: (0,) * nd, nd=t.ndim)) for t in p_params]
    args += list(p_rows) + list(p_params)
    if dep is not None:
        in_specs.append(pl.BlockSpec(memory_space=pl.ANY))
        args.append(dep)
    if post is None:
        out_specs, out_shape = pl.BlockSpec(oblk, omap), SDS(oshape, out_dtype)
        semantics = ("parallel", "parallel", "arbitrary", "arbitrary") if sum_blocks else ("parallel", "parallel", "parallel", "arbitrary")
    else:
        out_specs = [pl.BlockSpec((tm, tn), tile_map) for _ in p_dtypes]
        out_specs += [pl.BlockSpec(tuple(sh), functools.partial(lambda *g, nd: (0,) * nd, nd=len(sh))) for sh in p_accs]
        out_shape = [SDS((m, n), dt) for dt in p_dtypes] + [SDS(tuple(sh), F32) for sh in p_accs]
        semantics = ("arbitrary",) * 4
    out = pl.pallas_call(
        body,
        name=name,
        grid=grid,
        in_specs=in_specs,
        out_specs=out_specs,
        out_shape=out_shape,
        scratch_shapes=[] if direct else [pltpu.VMEM((tm, tn), F32)],
        compiler_params=pltpu.CompilerParams(dimension_semantics=semantics, vmem_limit_bytes=VMEM_LIMIT),
    )(*args)
    if post is not None:
        return out
    if flat and not sum_blocks:
        out = out[0]
    return out


def _rows_call(f, rows, params, outs, accs=(), *, tm, name, dep=None):
    s = rows[0][0].shape[0]
    nr, npar, no = len(rows), len(params), len(outs)
    nin = nr + npar + (0 if dep is None else 1)
    in_specs = [pl.BlockSpec((tm, w), functools.partial(lambda i, cb: (i, cb), cb=cb)) for (_, cb, w) in rows]
    in_specs += [pl.BlockSpec(p.shape, functools.partial(lambda i, nd: (0,) * nd, nd=p.ndim)) for p in params]
    if dep is not None:
        in_specs.append(pl.BlockSpec(memory_space=pl.ANY))
    out_shape = [SDS((s, w), dt) for (w, dt) in outs] + [SDS(tuple(sh), F32) for sh in accs]
    out_specs = [pl.BlockSpec((tm, w), lambda i: (i, 0)) for (w, _) in outs]
    out_specs += [pl.BlockSpec(tuple(sh), functools.partial(lambda i, nd: (0,) * nd, nd=len(sh))) for sh in accs]

    def body(*refs):
        rin, pin = refs[:nr], refs[nr:nr + npar]
        oo, ao = refs[nin:nin + no], refs[nin + no:]
        res = f(*[r[...] for r in rin], *[p[...] for p in pin])
        if not isinstance(res, (tuple, list)):
            res = (res,)
        for o_ref, v in zip(oo, res[:no]):
            o_ref[...] = v.astype(o_ref.dtype)
        i = pl.program_id(0)
        for a_ref, v in zip(ao, res[no:]):
            @pl.when(i == 0)
            def _():
                a_ref[...] = jnp.zeros_like(a_ref)

            a_ref[...] += v.reshape(a_ref.shape)

    res = pl.pallas_call(
        body,
        name=name,
        grid=(s // tm,),
        in_specs=in_specs,
        out_specs=out_specs,
        out_shape=out_shape,
        compiler_params=pltpu.CompilerParams(dimension_semantics=("arbitrary",), vmem_limit_bytes=VMEM_LIMIT),
    )(*[r[0] for r in rows], *params, *([] if dep is None else [dep]))
    return res


def _mmv(a, b, mode):
    ca = 0 if mode[0] == "t" else 1
    cb = 1 if mode[1] == "t" else 0
    return lax.dot_general(a.astype(BF16), b.astype(BF16), (((ca,), (cb,)), ((), ())), preferred_element_type=F32)


@functools.partial(jax.custom_vjp, nondiff_argnums=(2,))
def _bdot(a, b, mode):
    return _mmv(a, b, mode)


def _bdot_fwd(a, b, mode):
    return _mmv(a, b, mode), (a, b)


def _bdot_bwd(mode, saved, g):
    a, b = saved
    if mode == "nn":
        return _mmv(g, b, "nt"), _mmv(a, g, "tn")
    if mode == "nt":
        return _mmv(g, b, "nn"), _mmv(g, a, "tn")
    return _mmv(b, g, "nt"), _mmv(a, g, "nn")


_bdot.defvjp(_bdot_fwd, _bdot_bwd)


def _hdot(a, b, mode="nn", precision=HI):
    ca = 0 if mode[0] == "t" else 1
    cb = 1 if mode[1] == "t" else 0
    return lax.dot_general(a, b, (((ca,), (cb,)), ((), ())), precision=precision, preferred_element_type=F32)


def _segsum(x):
    c = x.shape[-1]
    blk = min(c, 256)
    r = lax.broadcasted_iota(jnp.int32, (blk, blk), 0) >> 6
    q = lax.broadcasted_iota(jnp.int32, (blk, blk), 1) >> 6
    ones = jnp.where(r == q, 1.0, 0.0).astype(F32)
    parts = [_hdot(x[:, i:i + blk], ones, precision=lax.Precision.HIGH) for i in range(0, c, blk)]
    return parts[0] if len(parts) == 1 else jnp.concatenate(parts, axis=1)


def _sigmoid(x):
    return jax.nn.sigmoid(x)


def _softplus(x):
    return jnp.maximum(x, 0.0) + jnp.log(1.0 + jnp.exp(-jnp.abs(x)))


def _rms(x, gain):
    return x * lax.rsqrt(jnp.mean(x * x, axis=-1, keepdims=True) + RMS_EPS) * gain


def _swiglu_act(gate, up):
    return gate * _sigmoid(gate) * up


def _rwkv_pre(zs, w0, w2, a0, a2, g2, k_k, k_a):
    r, k, v = zs[:, 0:512], zs[:, 512:1024], zs[:, 1024:1536]
    lora = zs[:, 1536:1792]
    wd, ad, gd = lora[:, 0:64], lora[:, 64:128], lora[:, 128:256]
    w = -_softplus(-(w0 + _bdot(jnp.tanh(wd), w2, "nn"))) - 0.5
    a = _sigmoid(a0 + _bdot(ad, a2, "nn"))
    g = _bdot(_sigmoid(gd), g2, "nn")
    kk = k * k_k
    kk = kk * lax.rsqrt(jnp.maximum(_segsum(kk * kk), 1e-24))
    k2 = k * (1.0 + (a - 1.0) * k_a)
    return r, -jnp.exp(w), k2, v, -kk, kk * a, g


def _rwkv_post(y, r, k2, v, g, gn_w, gn_b, r_k):
    mean = _segsum(y) * (1.0 / HEAD)
    yc = y - mean
    var = _segsum(yc * yc) * (1.0 / HEAD)
    yn = yc * lax.rsqrt(var + GN_EPS) * gn_w + gn_b
    bonus = _segsum(r * k2 * r_k) * v
    return (yn + bonus) * g


def _swap_halves(x):
    lane = lax.broadcasted_iota(jnp.int32, x.shape, 1)
    return jnp.where((lane & 32) == 0, jnp.roll(x, -32, axis=1), jnp.roll(x, 32, axis=1))


def _norm_rope(x, gain, cos, sin):
    heads = x.shape[1] // HEAD
    def rep(t):
        return jnp.concatenate([t] * heads, axis=1)

    xn = x * lax.rsqrt(_segsum(x * x) * (1.0 / HEAD) + RMS_EPS) * rep(gain)
    return xn * rep(cos) + _swap_halves(xn) * rep(sin)


def _attn_combine(o0, o1, o2, l0, l1, l2):
    m = jnp.maximum(jnp.maximum(l0, l1), l2)
    e0, e1, e2 = jnp.exp(l0 - m), jnp.exp(l1 - m), jnp.exp(l2 - m)
    return (e0 * o0 + e1 * o1 + e2 * o2) / (e0 + e1 + e2)


def _merge(zgr, zga, br, ba):
    return _sigmoid(zgr) * br + _sigmoid(zga) * ba


def _attn_block(q, kp, kc, vp, vc, has_prev):
    iq = lax.broadcasted_iota(jnp.int32, (1, BAND, BAND), 1)
    ik = lax.broadcasted_iota(jnp.int32, (1, BAND, BAND), 2)
    s_c = jnp.where(iq >= ik, _bdotb(q, kc, "nt") * (HEAD ** -0.5), NEG_INF)
    s_p = jnp.where(jnp.logical_and(iq <= ik, has_prev), _bdotb(q, kp, "nt") * (HEAD ** -0.5), NEG_INF)
    m = lax.stop_gradient(jnp.maximum(jnp.max(s_c, axis=-1, keepdims=True), jnp.max(s_p, axis=-1, keepdims=True)))
    e_c, e_p = jnp.exp(s_c - m), jnp.exp(s_p - m)
    l = jnp.sum(e_c, axis=-1, keepdims=True) + jnp.sum(e_p, axis=-1, keepdims=True)
    o = (_bdotb(e_c, vc) + _bdotb(e_p, vp)) / l
    return o, jnp.broadcast_to(m + jnp.log(l), o.shape)


def _mmb(a, b, cb):
    return lax.dot_general(a.astype(BF16), b.astype(BF16), (((2,), (cb,)), ((0,), (0,))), preferred_element_type=F32)


@functools.partial(jax.custom_vjp, nondiff_argnums=(2,))
def _bdotb1(a, b, cb):
    return _mmb(a, b, cb)


def _bdotb1_fwd(a, b, cb):
    return _mmb(a, b, cb), (a, b)


def _bdotb1_bwd(cb, saved, g):
    a, b = saved
    if cb == 1:
        return _mmb(g, b, 2), _mmb(jnp.swapaxes(a, 1, 2), g, 1)
    return _mmb(g, b, 1), _mmb(jnp.swapaxes(g, 1, 2), a, 1)


_bdotb1.defvjp(_bdotb1_fwd, _bdotb1_bwd)


def _bdotb(a, b, mode="nn", precision=None):
    if mode[0] == "t":
        a = jnp.swapaxes(a, 1, 2)
    cb = 2 if mode[1] == "t" else 1
    if precision is None:
        return _bdotb1(a, b, cb)
    return lax.dot_general(a, b, (((2,), (cb,)), ((0,), (0,))), precision=precision, preferred_element_type=F32)


def _tri_inv_levels(a):
    t = a.shape[-1]
    row = lax.broadcasted_iota(jnp.int32, (1, t, t), 1)
    col = lax.broadcasted_iota(jnp.int32, (1, t, t), 2)
    x = jnp.where(row == col, 1.0, 0.0).astype(F32) + jnp.where(jnp.logical_and(row == col + 1, (row & 1) == 1), a, 0.0)
    sh = 1
    while (1 << sh) < t:
        m = jnp.logical_and((row >> sh) == (col >> sh) + 1, (row >> (sh + 1)) == (col >> (sh + 1)))
        x = x + _bdotb(_bdotb(x, jnp.where(m, a, 0.0)), x)
        sh += 1
    return x


@jax.custom_vjp
def _tri_inv(a):
    return _tri_inv_levels(a)


def _tri_inv_fwd(a):
    x = _tri_inv_levels(a)
    return x, x


def _tri_inv_bwd(x, g):
    xt = jnp.swapaxes(x, 1, 2)
    return (_bdotb(_bdotb(xt, g, precision=lax.Precision.HIGH), xt, precision=lax.Precision.HIGH),)


_tri_inv.defvjp(_tri_inv_fwd, _tri_inv_bwd)


@jax.custom_vjp
def _known_inv(a, x):
    return x


def _known_inv_fwd(a, x):
    return x, x


def _known_inv_bwd(x, g):
    return _tri_inv_bwd(x, g)[0], jnp.zeros_like(x)


_known_inv.defvjp(_known_inv_fwd, _known_inv_bwd)


def _wkv_chunk(s0, r, lw, k, v, a, b, inv=None, with_inv=False):
    nh = r.shape[0]
    t = WKV_CHUNK
    n = r.shape[1] // t

    def chunked(x):
        return x if n == 1 else jnp.concatenate([x[:, c * t:(c + 1) * t] for c in range(n)], axis=0)

    r, lw, k, v, a, b = (chunked(x) for x in (r, lw, k, v, a, b))
    row = lax.broadcasted_iota(jnp.int32, (1, t, t), 1)
    col = lax.broadcasted_iota(jnp.int32, (1, t, t), 2)
    incl, strict = row >= col, row > col
    ones = jnp.broadcast_to(jnp.where(incl, 1.0, 0.0).astype(F32), (n * nh, t, t))
    cum = _bdotb(ones, lw, precision=HI)
    c_end = cum[:, t - 1:t, :]
    e_in, e_ex, e_inv = jnp.exp(cum), jnp.exp(cum - lw), jnp.exp(-cum)
    at, rt, bt, kt = a * e_ex, r * e_in, b * e_inv, k * e_inv
    a_ab = jnp.where(strict, _bdotb(at, bt, "nt"), 0.0)
    a_ak = jnp.where(strict, _bdotb(at, kt, "nt"), 0.0)
    x = _tri_inv(a_ab) if inv is None else _known_inv(a_ab, inv)
    r_b = jnp.where(incl, _bdotb(rt, bt, "nt"), 0.0)
    akv = _bdotb(a_ak, v)
    rkv = _bdotb(jnp.where(incl, _bdotb(rt, kt, "nt"), 0.0), v)
    w_end = jnp.exp(c_end - cum)
    bw, kw, decay = b * w_end, k * w_end, jnp.exp(c_end)
    ys = []
    for c in range(n):
        hs = slice(c * nh, (c + 1) * nh)
        u = _bdotb(x[hs], _bdotb(at[hs], s0, "nt") + akv[hs])
        ys.append(_bdotb(rt[hs], s0, "nt") + _bdotb(r_b[hs], u) + rkv[hs])
        s0 = s0 * decay[hs] + _bdotb(u, bw[hs], "tn") + _bdotb(v[hs], kw[hs], "tn")
    y = ys[0] if n == 1 else jnp.concatenate(ys, axis=1)
    return (y, s0, x) if with_inv else (y, s0)


def _shift_fwd(z, mu):
    s, c = z.shape
    tc = 256

    def body(z_ref, mu_ref, o_ref):
        zz = z_ref[...]
        row = lax.broadcasted_iota(jnp.int32, zz.shape, 0)
        prev = jnp.where(row == 0, 0.0, pltpu.roll(zz, 1, 0))
        o_ref[...] = zz + (prev - zz) * mu_ref[...]

    return pl.pallas_call(
        body, name="shift_fwd", grid=(c // tc,),
        in_specs=[pl.BlockSpec((s, tc), lambda j: (0, j)), pl.BlockSpec((1, tc), lambda j: (0, j))],
        out_specs=pl.BlockSpec((s, tc), lambda j: (0, j)), out_shape=SDS((s, c), F32),
        compiler_params=pltpu.CompilerParams(dimension_semantics=("parallel",), vmem_limit_bytes=VMEM_LIMIT),
    )(z, mu)


def _shift_bwd(z, mu, dzs):
    s, c = z.shape
    tc = 256

    def body(z_ref, mu_ref, d_ref, dz_ref, dmu_ref):
        zz, d, m = z_ref[...], d_ref[...], mu_ref[...]
        row = lax.broadcasted_iota(jnp.int32, zz.shape, 0)
        prev = jnp.where(row == 0, 0.0, pltpu.roll(zz, 1, 0))
        t = d * m
        nxt = jnp.where(row == s - 1, 0.0, pltpu.roll(t, s - 1, 0))
        dz_ref[...] = (d - t + nxt).astype(dz_ref.dtype)
        dmu_ref[...] = jnp.sum(d * (prev - zz), axis=0, keepdims=True)

    return pl.pallas_call(
        body, name="shift_bwd", grid=(c // tc,),
        in_specs=[pl.BlockSpec((s, tc), lambda j: (0, j)), pl.BlockSpec((1, tc), lambda j: (0, j)),
                  pl.BlockSpec((s, tc), lambda j: (0, j))],
        out_specs=[pl.BlockSpec((s, tc), lambda j: (0, j)), pl.BlockSpec((1, tc), lambda j: (0, j))],
        out_shape=[SDS((s, c), BF16), SDS((1, c), F32)],
        compiler_params=pltpu.CompilerParams(dimension_semantics=("parallel",), vmem_limit_bytes=VMEM_LIMIT),
    )(z, mu, dzs)


def _heads(x, nh):
    return jnp.stack([x[:, h * HEAD:(h + 1) * HEAD] for h in range(nh)], axis=0)


def _unheads(x):
    return jnp.concatenate([x[h] for h in range(x.shape[0])], axis=1)


def _wkv_fwd(zs, lw, k2, na, b):
    s = lw.shape[0]
    t, hb = WKV_CHUNK * WKV_CHUNKS_PER_STEP, WKV_HEADS_PER_STEP
    w = hb * HEAD
    nc, ng = s // t, RWKV_HEADS // hb
    nx = WKV_CHUNKS_PER_STEP * RWKV_HEADS

    def body(r_ref, v_ref, lw_ref, k_ref, a_ref, b_ref, y_ref, s0_ref, x_ref, state):
        @pl.when(pl.program_id(1) == 0)
        def _():
            state[...] = jnp.zeros_like(state)

        s0 = state[...]
        s0_ref[0] = s0
        y, s1, x = _wkv_chunk(s0, *[_heads(t_ref[...], hb) for t_ref in (r_ref, lw_ref, k_ref, v_ref, a_ref, b_ref)], with_inv=True)
        y_ref[...] = _unheads(y)
        x_ref[0] = x
        state[...] = s1

    def col(off):
        return pl.BlockSpec((t, w), functools.partial(lambda g, i, off: (i, g + off), off=off))

    return pl.pallas_call(
        body, name="wkv_fwd", grid=(ng, nc),
        in_specs=[col(0), col(2 * ng), col(0), col(0), col(0), col(0)],
        out_specs=[col(0), pl.BlockSpec((1, hb, HEAD, HEAD), lambda g, i: (i, g, 0, 0)),
                   pl.BlockSpec((1, nx, WKV_CHUNK, WKV_CHUNK), lambda g, i: (i, 0, 0, 0))],
        out_shape=[SDS((s, RWKV_DIM), F32), SDS((nc, RWKV_HEADS, HEAD, HEAD), F32), SDS((nc, nx, WKV_CHUNK, WKV_CHUNK), F32)],
        scratch_shapes=[pltpu.VMEM((hb, HEAD, HEAD), F32)],
        compiler_params=pltpu.CompilerParams(dimension_semantics=("parallel", "arbitrary"), vmem_limit_bytes=VMEM_LIMIT),
    )(zs, zs, lw, k2, na, b)


def _wkv_bwd(zs, lw, k2, na, b, s0s, invs, dy):
    s = lw.shape[0]
    t, hb = WKV_CHUNK * WKV_CHUNKS_PER_STEP, WKV_HEADS_PER_STEP
    w = hb * HEAD
    nc, ng = s // t, RWKV_HEADS // hb
    nx = WKV_CHUNKS_PER_STEP * RWKV_HEADS

    def body(r_ref, v_ref, lw_ref, k_ref, a_ref, b_ref, s0_ref, x_ref, dy_ref, dr_ref, dlw_ref, dk_ref, dv_ref, da_ref, db_ref, dstate):
        @pl.when(pl.program_id(1) == 0)
        def _():
            dstate[...] = jnp.zeros_like(dstate)

        _, vjp = jax.vjp(functools.partial(_wkv_chunk, inv=x_ref[0]), s0_ref[0],
                         *[_heads(t_ref[...], hb) for t_ref in (r_ref, lw_ref, k_ref, v_ref, a_ref, b_ref)])
        grads = vjp((_heads(dy_ref[...], hb), dstate[...]))
        dstate[...] = grads[0]
        for o_ref, gval in zip((dr_ref, dlw_ref, dk_ref, dv_ref, da_ref, db_ref), grads[1:]):
            o_ref[...] = _unheads(gval)

    def col(off):
        return pl.BlockSpec((t, w), functools.partial(lambda g, i, off: (nc - 1 - i, g + off), off=off))

    return pl.pallas_call(
        body, name="wkv_bwd", grid=(ng, nc),
        in_specs=[col(0), col(2 * ng), col(0), col(0), col(0), col(0),
                  pl.BlockSpec((1, hb, HEAD, HEAD), lambda g, i: (nc - 1 - i, g, 0, 0)),
                  pl.BlockSpec((1, nx, WKV_CHUNK, WKV_CHUNK), lambda g, i: (nc - 1 - i, 0, 0, 0)), col(0)],
        out_specs=[col(0)] * 6,
        out_shape=[SDS((s, RWKV_DIM), F32)] * 6,
        scratch_shapes=[pltpu.VMEM((hb, HEAD, HEAD), F32)],
        compiler_params=pltpu.CompilerParams(dimension_semantics=("parallel", "arbitrary"), vmem_limit_bytes=VMEM_LIMIT),
    )(zs, zs, lw, k2, na, b, s0s, invs, dy)


def _attn_batch(refs, bps, nh, first_has_prev):
    q_ref, kp_ref, kc_ref, vp_ref, vc_ref = refs

    def blocks(cur_ref, prev_ref=None):
        out = []
        for b in range(bps):
            if prev_ref is None:
                t = cur_ref[b * BAND:(b + 1) * BAND, :]
            else:
                t = prev_ref[...] if b == 0 else cur_ref[(b - 1) * BAND:b * BAND, :]
            out.append(_heads(t.astype(F32), nh))
        return out[0] if bps == 1 else jnp.concatenate(out, axis=0)

    batch = lax.broadcasted_iota(jnp.int32, (bps * nh, 1, 1), 0)
    has_prev = jnp.logical_or(batch >= nh, first_has_prev)
    return (blocks(q_ref), blocks(kc_ref, kp_ref), blocks(kc_ref), blocks(vc_ref, vp_ref), blocks(vc_ref)), has_prev


def _attn_rows(x, bps, nh):
    parts = [_unheads(x[b * nh:(b + 1) * nh]) for b in range(bps)]
    return parts[0] if bps == 1 else jnp.concatenate(parts, axis=0)


def _attn_fwd(q, k, v, d):
    s = q.shape[0]
    l = s // d
    nb = l // BAND
    assert nb * BAND == l
    qv, kv, vv = (t.reshape(l, d * GROUP_DIM) for t in (q, k, v))
    width = min(d, ATTN_CLASSES_PER_STEP) * GROUP_DIM
    bps = ATTN_CLASSES_PER_STEP * GROUP_DIM // width
    nh = width // HEAD

    def body(q_ref, kp_ref, kc_ref, vp_ref, vc_ref, o_ref, l_ref):
        ops, has_prev = _attn_batch((q_ref, kp_ref, kc_ref, vp_ref, vc_ref), bps, nh, pl.program_id(1) > 0)
        o, lse = _attn_block(*ops, has_prev)
        o_ref[...] = _attn_rows(o, bps, nh)
        l_ref[...] = _attn_rows(lse, bps, nh)

    cur = pl.BlockSpec((bps * BAND, width), lambda rho, i: (i, rho))
    prev = pl.BlockSpec((BAND, width), lambda rho, i: (jnp.maximum(i * bps - 1, 0), rho))
    o, lse = pl.pallas_call(
        body, name=f"attn_fwd_d{d}", grid=(d * GROUP_DIM // width, nb // bps),
        in_specs=[cur, prev, cur, prev, cur], out_specs=[cur, cur],
        out_shape=[SDS((l, d * GROUP_DIM), F32), SDS((l, d * GROUP_DIM), F32)],
        compiler_params=pltpu.CompilerParams(dimension_semantics=("parallel", "arbitrary"), vmem_limit_bytes=VMEM_LIMIT),
    )(qv, kv, kv, vv, vv)
    return o.reshape(s, GROUP_DIM), lse.reshape(s, GROUP_DIM)


def _attn_bwd(q, k, v, d, do, dlse):
    s = q.shape[0]
    l = s // d
    nb = l // BAND
    qv, kv, vv, dov, dlv = (t.reshape(l, d * GROUP_DIM) for t in (q, k, v, do, dlse))
    width = min(d, ATTN_CLASSES_PER_STEP) * GROUP_DIM
    bps = ATTN_CLASSES_PER_STEP * GROUP_DIM // width
    nh = width // HEAD
    ns = nb // bps

    def body(q_ref, kp_ref, kc_ref, vp_ref, vc_ref, do_ref, dl_ref, dq_ref, dk_ref, dv_ref, ck, cv):
        step = pl.program_id(1)

        @pl.when(step == 0)
        def _():
            ck[...] = jnp.zeros_like(ck)
            cv[...] = jnp.zeros_like(cv)

        ops, has_prev = _attn_batch((q_ref, kp_ref, kc_ref, vp_ref, vc_ref), bps, nh, step < ns - 1)
        _, vjp = jax.vjp(functools.partial(_attn_block, has_prev=has_prev), *ops)
        cts = [jnp.concatenate([_heads(t_ref[b * BAND:(b + 1) * BAND, :], nh) for b in range(bps)], axis=0) if bps > 1
               else _heads(t_ref[...], nh) for t_ref in (do_ref, dl_ref)]
        dq, dkp, dkc, dvp, dvc = vjp(tuple(cts))
        dq_ref[...] = _attn_rows(dq, bps, nh)
        for out_ref, cur_part, prev_part, carry in ((dk_ref, dkc, dkp, ck), (dv_ref, dvc, dvp, cv)):
            for b in range(bps):
                after = carry[...] if b == bps - 1 else _unheads(prev_part[(b + 1) * nh:(b + 2) * nh])
                out_ref[b * BAND:(b + 1) * BAND, :] = _unheads(cur_part[b * nh:(b + 1) * nh]) + after
            carry[...] = _unheads(prev_part[0:nh])

    cur = pl.BlockSpec((bps * BAND, width), lambda rho, i: (ns - 1 - i, rho))
    prev = pl.BlockSpec((BAND, width), lambda rho, i: (jnp.maximum((ns - 1 - i) * bps - 1, 0), rho))
    dq, dk, dv = pl.pallas_call(
        body, name=f"attn_bwd_d{d}", grid=(d * GROUP_DIM // width, ns),
        in_specs=[cur, prev, cur, prev, cur, cur, cur], out_specs=[cur] * 3,
        out_shape=[SDS((l, d * GROUP_DIM), F32)] * 3,
        scratch_shapes=[pltpu.VMEM((BAND, width), F32), pltpu.VMEM((BAND, width), F32)],
        compiler_params=pltpu.CompilerParams(dimension_semantics=("parallel", "arbitrary"), vmem_limit_bytes=VMEM_LIMIT),
    )(qv, kv, kv, vv, vv, dov, dlv)
    return dq.reshape(s, GROUP_DIM), dk.reshape(s, GROUP_DIM), dv.reshape(s, GROUP_DIM)


def _coords():
    return lax.axis_index("x"), lax.axis_index("y"), lax.axis_index("c")


_CHIP_FLIPS = ((1, 0), (0, 1), (1, 1))


def _flip(v, f):
    return 1 - v if f else v


def _form(kind, r, c):
    return (N_CHIPS, r, c) if kind == "blk" else (r, N_CHIPS * c)


def _slot(ref, kind, j, rows, c):
    if kind == "blk":
        return ref.at[j] if rows is None else ref.at[j, rows]
    cols = pl.ds(pl.multiple_of(j * c, 128), c)
    return ref.at[:, cols] if rows is None else ref.at[rows, cols]


def _half(r, which, align):
    return pl.ds(pl.multiple_of(which * (r // 2), align), r // 2)


def _rcopy(src, dst, send_sems, recv_sems, kk, dev):
    return pltpu.make_async_remote_copy(src_ref=src, dst_ref=dst, send_sem=send_sems.at[kk], recv_sem=recv_sems.at[kk],
                                        device_id=dev, device_id_type=MESH)


def _gather_plan(specs, step):
    def copies(refs, ss, rs, received):
        x, y, c = _coords()
        out = []
        for w, (kind, r, cc) in enumerate(specs):
            mine, other = _half(r, c, 16), _half(r, 1 - c, 16)
            for kk, (fx, fy) in enumerate(_CHIP_FLIPS):
                px, py = _flip(x, fx), _flip(y, fy)
                if step == "ici":
                    sl = _slot(refs[w], kind, 2 * px + py if received else 2 * x + y, mine, cc)
                    dev = (px, py, c)
                else:
                    sl = _slot(refs[w], kind, 2 * px + py, other if received else mine, cc)
                    dev = (x, y, 1 - c)
                out.append(_rcopy(sl, sl, ss, rs, 3 * w + kk, dev))
        return out

    def issue(refs, ss, rs):
        return copies(refs, ss, rs, False)

    def expect(refs, ss, rs):
        return copies(refs, ss, rs, False), copies(refs, ss, rs, True)

    return issue, expect


_HBM = pl.BlockSpec(memory_space=pltpu.HBM)
_SEM = pl.BlockSpec(memory_space=pltpu.SEMAPHORE)
_EFFECT = pltpu.SideEffectType.DATAFLOW_SIDE_EFFECTING


def _copies_start(name, bufs, n_sems, issue, after=None):
    nb = len(bufs)
    extra = [] if after is None else [after]

    def body(*refs):
        send_sems, recv_sems = refs[nb + len(extra)], refs[nb + len(extra) + 1]
        for cp in issue(refs[:nb], send_sems, recv_sems):
            cp.start()
        refs[-1][...] = jnp.zeros_like(refs[-1])

    outs = pl.pallas_call(
        body, name=name,
        out_shape=(pltpu.SemaphoreType.DMA((n_sems,)), pltpu.SemaphoreType.DMA((n_sems,)),
                   *[pltpu.HBM(b.shape, b.dtype) for b in bufs], SDS((8, 128), F32)),
        in_specs=[_HBM] * nb + [pl.BlockSpec(memory_space=pl.ANY)] * len(extra),
        out_specs=(_SEM, _SEM, *[_HBM] * nb, pl.BlockSpec(memory_space=pltpu.VMEM)),
        input_output_aliases={i: 2 + i for i in range(nb)},
        compiler_params=pltpu.CompilerParams(has_side_effects=_EFFECT),
    )(*[pltpu.with_memory_space_constraint(b, pltpu.HBM) for b in bufs], *extra)
    return outs[0], outs[1], list(outs[2:2 + nb]), outs[-1]


def _copies_wait(name, bufs, send_sems, recv_sems, after, expect):
    nb = len(bufs)
    extra = list(after) if isinstance(after, (list, tuple)) else [after]

    def body(*refs):
        sent, received = expect(refs[:nb], refs[nb], refs[nb + 1])
        for cp in sent:
            cp.wait_send()
        for cp in received:
            cp.wait_recv()

    outs = pl.pallas_call(
        body, name=name,
        out_shape=tuple(pltpu.HBM(b.shape, b.dtype) for b in bufs),
        in_specs=(*[_HBM] * nb, _SEM, _SEM, *[pl.BlockSpec(memory_space=pl.ANY)] * len(extra)), out_specs=tuple([_HBM] * nb),
        input_output_aliases={i: i for i in range(nb)},
        compiler_params=pltpu.CompilerParams(has_side_effects=_EFFECT),
    )(*bufs, send_sems, recv_sems, *extra)
    return list(outs)


def _add_pair(g, recv, kind, r, c, c_arr, name):
    h = r // 2
    if kind == "blk":
        tr = _row_tile(h, 512)
        grid = (N_CHIPS, h // tr)
        g_spec = pl.BlockSpec((1, 1, tr, c), lambda j, i, c_ref: (j, c_ref[0], i, 0))
        o_spec = pl.BlockSpec((1, tr, c), lambda j, i, c_ref: (j, i, 0))
        gv, oshape = g.reshape(N_CHIPS, 2, h, c), (N_CHIPS, h, c)
    else:
        tr = _row_tile(h, 64)
        grid = (h // tr,)
        g_spec = pl.BlockSpec((1, tr, N_CHIPS * c), lambda i, c_ref: (c_ref[0], i, 0))
        o_spec = pl.BlockSpec((tr, N_CHIPS * c), lambda i, c_ref: (i, 0))
        gv, oshape = g.reshape(2, h, N_CHIPS * c), (h, N_CHIPS * c)

    def body(c_ref, g_ref, r_ref, o_ref, ob_ref):
        v = (g_ref[:, 0] if kind == "blk" else g_ref[0]) + r_ref[...]
        o_ref[...] = v
        ob_ref[...] = v.astype(BF16)

    return pl.pallas_call(
        body, name=name,
        grid_spec=pltpu.PrefetchScalarGridSpec(num_scalar_prefetch=1, grid=grid, in_specs=[g_spec, o_spec], out_specs=[o_spec] * 2),
        out_shape=[SDS(oshape, F32), SDS(oshape, BF16)],
        compiler_params=pltpu.CompilerParams(vmem_limit_bytes=VMEM_LIMIT),
    )(c_arr, gv, recv)


def _sum_chips(pair, recv, kind, r, c, mc_arr, name):
    h = r // 2
    tr = _row_tile(h, 512)
    nt = h // tr
    if kind == "blk":
        p_spec = pl.BlockSpec((1, tr, c), lambda i, mc: (mc[0], i, 0))
    else:
        p_spec = pl.BlockSpec((tr, c), lambda i, mc: (i, mc[0]))

    def body(mc, a_ref, r_ref, g_out):
        own = a_ref[0] if kind == "blk" else a_ref[...]
        g_out[...] = ((own + r_ref[0].astype(F32)) + r_ref[1].astype(F32)) + r_ref[2].astype(F32)

    return pl.pallas_call(
        body, name=name,
        grid_spec=pltpu.PrefetchScalarGridSpec(
            num_scalar_prefetch=1, grid=(nt,), in_specs=[p_spec, pl.BlockSpec((3, tr, c), lambda i, mc: (0, i, 0))],
            out_specs=pl.BlockSpec((tr, c), lambda i, mc: (mc[1] * nt + i, 0))),
        out_shape=SDS((r, c), F32),
        compiler_params=pltpu.CompilerParams(vmem_limit_bytes=VMEM_LIMIT),
    )(mc_arr, pair, recv)


class _GroupReduce:
    def __init__(self, tag, specs, c_arr, mc_arr):
        self.tag, self.specs, self.c_arr, self.mc_arr = tag, specs, c_arr, mc_arr
        self.n = len(specs)

    def _plan(self, step):
        specs, n = self.specs, self.n

        def copies(refs, ss, rs, received):
            x, y, c = _coords()
            sib, out = (x, y, 1 - c), []
            for w, (_, kind, r, cc) in enumerate(specs):
                if step == "join":
                    there = refs[w].at[_half(r, 1 - c if received else c, 8)]
                    out.append(_rcopy(there, there, ss, rs, w, sib))
                    continue
                src, land = refs[w], refs[n + w]
                if step == "swap":
                    rows = _half(r, 1 - c, 8)
                    part = src.at[:, rows] if kind == "blk" else src.at[rows]
                    out.append(_rcopy(land if received else part, land, ss, rs, w, sib))
                else:
                    for kk, (fx, fy) in enumerate(_CHIP_FLIPS):
                        px, py = _flip(x, fx), _flip(y, fy)
                        part = land.at[kk] if received else _slot(src, kind, 2 * px + py, None, cc)
                        out.append(_rcopy(part, land.at[kk], ss, rs, 3 * w + kk, (px, py, c)))
            return out

        def issue(refs, ss, rs):
            return copies(refs, ss, rs, False)

        def expect(refs, ss, rs):
            return copies(refs, ss, rs, False), copies(refs, ss, rs, True)

        return issue, expect

    def swap_start(self, grads, after=None):
        lands = [lax.empty(_form(kind, r // 2, c), F32) for _, kind, r, c in self.specs]
        ss, rs, bufs, tok = _copies_start(f"rs_{self.tag}_swap", list(grads) + lands, self.n, self._plan("swap")[0], after=after)
        self.state = (ss, rs, bufs)
        return tok

    def swap_wait_ici_start(self, after):
        ss, rs, bufs = self.state
        bufs = _copies_wait(f"rs_{self.tag}_swap_wait", bufs, ss, rs, after, self._plan("swap")[1])
        pairs = [_add_pair(bufs[w], bufs[self.n + w], kind, r, c, self.c_arr, name=f"rs_{self.tag}_pair_{nm}")
                 for w, (nm, kind, r, c) in enumerate(self.specs)]
        self.pair = [pr[0] for pr in pairs]
        lands = [lax.empty((3, r // 2, c), BF16) for _, _, r, c in self.specs]
        ss, rs, bufs, tok = _copies_start(f"rs_{self.tag}_ici", [pr[1] for pr in pairs] + lands, 3 * self.n, self._plan("ici")[0])
        self.state = (ss, rs, bufs)
        return tok

    def ici_wait_join_start(self, after):
        ss, rs, bufs = self.state
        bufs = _copies_wait(f"rs_{self.tag}_ici_wait", bufs, ss, rs, after, self._plan("ici")[1])
        outs = [_sum_chips(self.pair[w], bufs[self.n + w], kind, r, c, self.mc_arr, name=f"rs_{self.tag}_sum_{nm}")
                for w, (nm, kind, r, c) in enumerate(self.specs)]
        ss, rs, bufs, tok = _copies_start(f"rs_{self.tag}_join", outs, self.n, self._plan("join")[0])
        self.state = (ss, rs, bufs)
        return tok

    def join_wait(self, after):
        ss, rs, bufs = self.state
        bufs = _copies_wait(f"rs_{self.tag}_join_wait", bufs, ss, rs, after, self._plan("join")[1])
        return {nm: bufs[w] for w, (nm, _, _, _) in enumerate(self.specs)}


def _small_gather_plan():
    def copies(refs, ss, rs, received):
        x, y, c = _coords()
        own, land = refs
        out = []
        for kk in range(1, 8):
            px, py, pc = _flip(x, (kk >> 2) & 1), _flip(y, (kk >> 1) & 1), _flip(c, kk & 1)
            there = land.at[4 * px + 2 * py + pc]
            out.append(_rcopy(there if received else own, there if received else land.at[4 * x + 2 * y + c], ss, rs, kk - 1,
                              (px, py, pc)))
        return out

    def issue(refs, ss, rs):
        return copies(refs, ss, rs, False)

    def expect(refs, ss, rs):
        return copies(refs, ss, rs, False), copies(refs, ss, rs, True)

    return issue, expect


def _sum_slots(slots):
    n, rows, cols = slots.shape

    def body(s_ref, o_ref):
        acc = s_ref[0]
        for j in range(1, n):
            acc = acc + s_ref[j]
        o_ref[...] = acc

    return pl.pallas_call(
        body, name="sum_small",
        in_specs=[pl.BlockSpec(memory_space=pltpu.VMEM)], out_specs=pl.BlockSpec(memory_space=pltpu.VMEM),
        out_shape=SDS((rows, cols), F32),
    )(slots)


def _adamw_rows(w, g, m, v):
    m = ADAM_B1 * m + (1.0 - ADAM_B1) * g
    v = ADAM_B2 * v + (1.0 - ADAM_B2) * jnp.square(g)
    m_hat = m / (1.0 - ADAM_B1 ** ADAM_STEP)
    v_hat = v / (1.0 - ADAM_B2 ** ADAM_STEP)
    return -ADAM_LR * (m_hat / (jnp.sqrt(v_hat) + ADAM_EPS) + ADAM_WD * w), m, v


def _adamw(w, g, m, v, name, dep=None, with_grad=False):
    rows, cols = w.shape
    tm = _pick(rows, (256, 128, 64, 16, 8))
    f = (lambda wt, gt, mt, vt: (gt,) + _adamw_rows(wt, gt, mt, vt)) if with_grad else _adamw_rows
    return _rows_call(f, [(t, 0, cols) for t in (w, g, m, v)], [], [(cols, F32)] * (3 + with_grad), tm=tm, name=name, dep=dep)


def _pack_small(parts):
    flat = jnp.concatenate([parts[n].reshape(-1) for n, _ in SMALL])
    return jnp.pad(flat, (0, SMALL_ROWS * PACK_COLS - flat.shape[0])).reshape(SMALL_ROWS, PACK_COLS)


def _unpack_small(buf, shapes):
    flat, out, off = buf.reshape(-1), {}, 0
    for n, sz in SMALL:
        out[n] = flat[off:off + sz].reshape(shapes[n])
        off += sz
    return out


def _lora_stack(parts):
    return jnp.concatenate([parts[n] for n, _ in LORA], axis=-2)


def _lora_split(stacked):
    out, off = {}, 0
    for n, rows in LORA:
        out[n] = stacked[..., off:off + rows, :]
        off += rows
    return out


def _ffn_gate_up(h, wgt, wut, name, dep=None):
    s, d = h.shape
    nblk, f, _ = wgt.shape
    tm = _pick(s, (1024, 512, 256))
    dn = (((1,), (1,)), ((), ()))

    def body(h_ref, wg_ref, wu_ref, *rest):
        g_ref, u_ref, a_ref = rest[-3:]
        hh = h_ref[...]
        g = lax.dot_general(hh, wg_ref[0], dn, preferred_element_type=F32)
        u = lax.dot_general(hh, wu_ref[0], dn, preferred_element_type=F32)
        g_ref[0], u_ref[0] = g.astype(BF16), u.astype(BF16)
        a_ref[0] = _swiglu_act(g, u).astype(BF16)

    w_spec = pl.BlockSpec((1, f, d), lambda j, i: (j, 0, 0))
    o_spec = pl.BlockSpec((1, tm, f), lambda j, i: (j, i, 0))
    extra = [] if dep is None else [dep]
    return pl.pallas_call(
        body, name=name, grid=(nblk, s // tm),
        in_specs=[pl.BlockSpec((tm, d), lambda j, i: (i, 0)), w_spec, w_spec] + [pl.BlockSpec(memory_space=pl.ANY)] * len(extra),
        out_specs=[o_spec] * 3,
        out_shape=[SDS((nblk, s, f), BF16)] * 3,
        compiler_params=pltpu.CompilerParams(dimension_semantics=("parallel", "parallel"), vmem_limit_bytes=VMEM_LIMIT),
    )(h, wgt, wut, *extra)


def _ffn_down_dx(dx_bf, wd, gate, up, name, dep=None):
    s, d = dx_bf.shape
    nblk, f, _ = wd.shape
    tm = _pick(s, (1024, 512, 256))
    dn = (((1,), (1,)), ((), ()))

    def body(dx_ref, wd_ref, g_ref, u_ref, *rest):
        dg_ref, du_ref = rest[-2:]
        dact = 0.5 * lax.dot_general(dx_ref[...], wd_ref[0], dn, preferred_element_type=F32)
        _, vjp = jax.vjp(_swiglu_act, g_ref[0].astype(F32), u_ref[0].astype(F32))
        dg, du = vjp(dact)
        dg_ref[0], du_ref[0] = dg.astype(BF16), du.astype(BF16)

    o_spec = pl.BlockSpec((1, tm, f), lambda j, i: (j, i, 0))
    extra = [] if dep is None else [dep]
    return pl.pallas_call(
        body, name=name, grid=(nblk, s // tm),
        in_specs=[pl.BlockSpec((tm, d), lambda j, i: (i, 0)), pl.BlockSpec((1, f, d), lambda j, i: (j, 0, 0)), o_spec, o_spec]
        + [pl.BlockSpec(memory_space=pl.ANY)] * len(extra),
        out_specs=[o_spec] * 2, out_shape=[SDS((nblk, s, f), BF16)] * 2,
        compiler_params=pltpu.CompilerParams(dimension_semantics=("parallel", "parallel"), vmem_limit_bytes=VMEM_LIMIT),
    )(dx_bf, wd, gate, up, *extra)


def _ffn_down_dx_dh(dx_bf, wd, gate, up, wgt, wut, name):
    s, d = dx_bf.shape
    nblk, f, _ = wd.shape
    tm = _pick(s, (1024, 512, 256))
    dn = (((1,), (1,)), ((), ()))

    def body(dx_ref, wd_ref, g_ref, u_ref, wg_ref, wu_ref, dg_ref, du_ref, dh_ref):
        dact = 0.5 * lax.dot_general(dx_ref[...], wd_ref[0], dn, preferred_element_type=F32)
        _, vjp = jax.vjp(_swiglu_act, g_ref[0].astype(F32), u_ref[0].astype(F32))
        dg, du = vjp(dact)
        dg, du = dg.astype(BF16), du.astype(BF16)
        dg_ref[0], du_ref[0] = dg, du

        @pl.when(pl.program_id(1) == 0)
        def _():
            dh_ref[...] = jnp.zeros_like(dh_ref)

        dh_ref[...] += (jnp.dot(dg, wg_ref[0], preferred_element_type=F32) + jnp.dot(du, wu_ref[0], preferred_element_type=F32))

    t_spec = pl.BlockSpec((1, tm, f), lambda i, j: (j, i, 0))
    w_spec = pl.BlockSpec((1, f, d), lambda i, j: (j, 0, 0))
    row_spec = pl.BlockSpec((tm, d), lambda i, j: (i, 0))
    return pl.pallas_call(
        body, name=name, grid=(s // tm, nblk),
        in_specs=[row_spec, w_spec, t_spec, t_spec, w_spec, w_spec], out_specs=[t_spec, t_spec, row_spec],
        out_shape=[SDS((nblk, s, f), BF16), SDS((nblk, s, f), BF16), SDS((s, d), F32)],
        compiler_params=pltpu.CompilerParams(dimension_semantics=("parallel", "arbitrary"), vmem_limit_bytes=VMEM_LIMIT),
    )(dx_bf, wd, gate, up, wgt, wut)


def _ffn_fwd(x, gain, wgt, wut, wd, tag, h=None, dep=None):
    if h is None:
        h = _rows_call(_rms, [(x, 0, D_MODEL)], [gain], [(D_MODEL, BF16)], tm=512, name=f"{tag}_norm")[0]
    gate, up, act = _ffn_gate_up(h, wgt, wut, f"{tag}_gate_up", dep=dep)
    x_new = _mm(act, wd, sum_blocks=True, res=x, alpha=0.5, name=f"{tag}_down")
    return x_new, (x, h, gate, up, act)


def _ffn_bwd(dx_new, dx_new_bf, saved, gain, wgt, wut, wd, tag, dep=None, hooks=None):
    x, h, gate, up, act = saved
    hooks = hooks or {}

    def hook(name, *vals):
        return hooks[name](*vals) if name in hooks else None

    d_wd = _mm(act, dx_new_bf, ta=True, alpha=0.5, name=f"{tag}_down_dw")
    if not hooks:
        dgate, dup, dh = _ffn_down_dx_dh(dx_new_bf, wd, gate, up, wgt, wut, f"{tag}_down_dx_dh")
        f, rows, params, _, accs = _norm_bwd_post(x, gain, dx_new)
        dx, dx_bf, dgain = _rows_call(f, [(dh, 0, D_MODEL)] + [(t, 0, D_MODEL) for t in rows], params,
                                      [(D_MODEL, F32), (D_MODEL, BF16)], accs, tm=512, name=f"{tag}_norm_bwd")
        d_wgt = _mm(dgate, h, ta=True, name=f"{tag}_gate_dw")
        d_wut = _mm(dup, h, ta=True, name=f"{tag}_up_dw")
        return dx, dx_bf, dgain, d_wgt, d_wut, d_wd
    dep = hook("down", d_wd) if "down" in hooks else dep
    dgate, dup = _ffn_down_dx(dx_new_bf, wd, gate, up, f"{tag}_down_dx", dep=dep)
    d_wgt = _mm(dgate, h, ta=True, dep=hook("mid", dgate), name=f"{tag}_gate_dw")
    d_wut = _mm(dup, h, ta=True, name=f"{tag}_up_dw")
    dh = _mm(dgate, wgt, sum_blocks=True, dep=hook("dw", d_wgt, d_wut), name=f"{tag}_gate_dx")
    dx, dx_bf, dgain = _mm(dup, wut, sum_blocks=True, res=dh, dep=hook("dx", dh), post=_norm_bwd_post(x, gain, dx_new),
                           name=f"{tag}_up_dx")
    hook("end", dx_bf)
    return dx, dx_bf, dgain, d_wgt, d_wut, d_wd


def _norm_bwd_post(x, gain, dres):
    def f(dht, xt, drt, gt):
        _, vjp = jax.vjp(_rms, xt, gt)
        dxt, dgt = vjp(dht)
        return dxt + drt, dxt + drt, dgt

    return f, [x, dres], [gain], [F32, BF16], [(1, D_MODEL)]


def kernel(x, p, positions, ffn1_norm, ffn1_w_gate, ffn1_w_up, ffn1_w_down, mix_norm, w_in, rwkv_mu, rwkv_w0, rwkv_w2, rwkv_a0, rwkv_a2, rwkv_g2, rwkv_k_k, rwkv_k_a, rwkv_r_k, rwkv_gn_w, rwkv_gn_b, q_norm, k_norm, w_br_rwkv, w_br_attn, w_out, ffn2_norm, ffn2_w_gate, ffn2_w_up, ffn2_w_down, ple_norm, ple_w_gate, ple_w_proj, loss_target, m_ffn1_norm, m_ffn1_w_gate, m_ffn1_w_up, m_ffn1_w_down, m_mix_norm, m_w_in, m_rwkv_mu, m_rwkv_w0, m_rwkv_w2, m_rwkv_a0, m_rwkv_a2, m_rwkv_g2, m_rwkv_k_k, m_rwkv_k_a, m_rwkv_r_k, m_rwkv_gn_w, m_rwkv_gn_b, m_q_norm, m_k_norm, m_w_br_rwkv, m_w_br_attn, m_w_out, m_ffn2_norm, m_ffn2_w_gate, m_ffn2_w_up, m_ffn2_w_down, m_ple_norm, m_ple_w_gate, m_ple_w_proj, v_ffn1_norm, v_ffn1_w_gate, v_ffn1_w_up, v_ffn1_w_down, v_mix_norm, v_w_in, v_rwkv_mu, v_rwkv_w0, v_rwkv_w2, v_rwkv_a0, v_rwkv_a2, v_rwkv_g2, v_rwkv_k_k, v_rwkv_k_a, v_rwkv_r_k, v_rwkv_gn_w, v_rwkv_gn_b, v_q_norm, v_k_norm, v_w_br_rwkv, v_w_br_attn, v_w_out, v_ffn2_norm, v_ffn2_w_gate, v_ffn2_w_up, v_ffn2_w_down, v_ple_norm, v_ple_w_gate, v_ple_w_proj):
    args = dict(locals())
    wts = {n: args[n] for n in WEIGHTS}
    mom_m = {n: args["m_" + n] for n in WEIGHTS}
    mom_v = {n: args["v_" + n] for n in WEIGHTS}
    x0, tgt = x[0], loss_target[0]
    s = x0.shape[0]
    p_tok = p[0, 0]

    vec = {n: wts[n].reshape(1, -1) for n, _ in SMALL}
    xi, yi, ci = _coords()
    me = 2 * xi + yi
    def laid(t, n):
        return jnp.transpose(t[n][0]) if n in TRANSPOSED else t[n][0]

    shard_of = {n: laid(wts, n) for g in GROUPS.values() for n, _, _, _ in g if n != "lora"}
    shard_of["lora"] = _lora_stack({n: wts[n][0] for n, _ in LORA})

    def whole_with_own(n, kind, r, c, tok=None):
        at = (me, 0, 0) if kind == "blk" else (0, me * c)
        own = (shard_of[n] if tok is None else shard_of[n] + tok[0, 0]).astype(BF16)
        return lax.dynamic_update_slice(lax.empty(_form(kind, r, c), BF16), own[None] if kind == "blk" else own, at)

    specs = {g: [(kind, r, c) for _, kind, r, c in grp] for g, grp in GROUPS.items()}
    plans = {(g, st): _gather_plan(specs[g], st) for g in GROUPS for st in ("ici", "d2d")}
    buf_f1 = [whole_with_own(*w) for w in GROUPS["f1"]]
    ss_0, rs_0, buf_f1, tok_0 = _copies_start("gather_f1_ici", buf_f1, 3 * len(buf_f1), plans["f1", "ici"][0])
    bufs = {g: [whole_with_own(*w, tok=tok_0) for w in GROUPS[g]] for g in ("mx", "f2")}
    buf_f1 = _copies_wait("gather_f1_ici_wait", buf_f1, ss_0, rs_0, bufs["mx"] + bufs["f2"], plans["f1", "ici"][1])
    ss_1, rs_1, buf_f1, tok_1 = _copies_start("gather_f1_d2d", buf_f1, 3 * len(buf_f1), plans["f1", "d2d"][0])
    ss_a, rs_a, buf_mx, tok_a = _copies_start("gather_mx_ici", bufs["mx"], 3 * len(bufs["mx"]), plans["mx", "ici"][0],
                                              after=tok_1)
    h1 = _rows_call(_rms, [(x0, 0, D_MODEL)], [vec["ffn1_norm"] + tok_a[0, 0]], [(D_MODEL, BF16)], tm=512, name="ffn1_norm")[0]
    buf_f1 = _copies_wait("gather_f1_d2d_wait", buf_f1, ss_1, rs_1, h1, plans["f1", "d2d"][1])
    wb = dict(zip([w[0] for w in GROUPS["f1"]], buf_f1))

    inv_freq = 1.0 / (ROPE_THETA ** (jnp.arange(0, HEAD, 2, dtype=F32) / HEAD))
    ang = positions[0].astype(F32)[:, None] * inv_freq
    cos, sin = jnp.cos(ang), jnp.sin(ang)
    cos2, sin2 = jnp.concatenate([cos, cos], axis=1), jnp.concatenate([-sin, sin], axis=1)

    x1, ffn1_saved = _ffn_fwd(x0, vec["ffn1_norm"], wb["ffn1_w_gate"], wb["ffn1_w_up"], wb["ffn1_w_down"], "ffn1", h=h1, dep=tok_a)
    buf_mx = _copies_wait("gather_mx_ici_wait", buf_mx, ss_a, rs_a, x1, plans["mx", "ici"][1])
    ss_b, rs_b, buf_mx, tok_b = _copies_start("gather_mx_d2d", buf_mx, 3 * len(buf_mx), plans["mx", "d2d"][0])
    ss_c, rs_c, buf_f2, tok_c = _copies_start("gather_f2_ici", bufs["f2"], 3 * len(bufs["f2"]), plans["f2", "ici"][0])
    h = _rows_call(_rms, [(x1, 0, D_MODEL)], [vec["mix_norm"] + (tok_b[0, 0] + tok_c[0, 0])], [(D_MODEL, BF16)], tm=256,
                   name="mix_norm")[0]
    buf_mx = _copies_wait("gather_mx_d2d_wait", buf_mx, ss_b, rs_b, h, plans["mx", "d2d"][1])
    wb.update(zip([w[0] for w in GROUPS["mx"]], buf_mx))
    w_in_all = wb["w_in"]
    w_in_r, w_in_a, w_in_g = w_in_all[:, :RWKV_COLS], w_in_all[:, RWKV_COLS:RWKV_COLS + ATTN_COLS], w_in_all[:, RWKV_COLS + ATTN_COLS:]
    lora = _lora_split(wb["lora"])
    w2, a2, g2 = lora["rwkv_w2"], lora["rwkv_a2"], lora["rwkv_g2"]
    z_r = _mm(h, w_in_r, name="in_rwkv")
    z_a = _mm(h, w_in_a, name="in_attn")
    z_g = _mm(h, w_in_g, name="in_gate")

    zs = _shift_fwd(z_r, vec["rwkv_mu"])
    pre_params = [vec["rwkv_w0"], w2, vec["rwkv_a0"], a2, g2, vec["rwkv_k_k"], vec["rwkv_k_a"]]
    def pre_fwd(*t):
        res = _rwkv_pre(*t)
        return res[1], res[2], res[4], res[5], res[6]

    lw, k2, na, kb, gate_r = _rows_call(pre_fwd, [(zs, 0, RWKV_COLS)], pre_params, [(RWKV_DIM, F32)] * 5, tm=512, name="rwkv_pre")
    y_scan, s0s, invs = _wkv_fwd(zs, lw, k2, na, kb)
    buf_f2 = _copies_wait("gather_f2_ici_wait", buf_f2, ss_c, rs_c, y_scan, plans["f2", "ici"][1])
    ss_d, rs_d, buf_f2, tok_d = _copies_start("gather_f2_d2d", buf_f2, 3 * len(buf_f2), plans["f2", "d2d"][0])
    post_params = [vec["rwkv_gn_w"] + tok_d[0, 0], vec["rwkv_gn_b"], vec["rwkv_r_k"]]
    post_rows = [(y_scan, 0, RWKV_DIM), (zs, 0, RWKV_DIM), (k2, 0, RWKV_DIM), (zs, 2, RWKV_DIM), (gate_r, 0, RWKV_DIM)]
    y_rwkv = _rows_call(_rwkv_post, post_rows, post_params, [(RWKV_DIM, BF16)], tm=512, name="rwkv_post")[0]
    buf_f2 = _copies_wait("gather_f2_d2d_wait", buf_f2, ss_d, rs_d, y_rwkv, plans["f2", "d2d"][1])
    wb.update(zip([w[0] for w in GROUPS["f2"]], buf_f2))
    w_brr, w_bra = wb["w_br_rwkv"], wb["w_br_attn"]
    w_o = wb["w_out"].reshape(D_MODEL, D_MODEL)
    w_pp, w_pg = wb["ple_w_proj"], wb["ple_w_gate"].reshape(D_MODEL, D_MODEL)

    def qk_fwd(qt, kt, ct, st, qg, kg):
        return _norm_rope(qt, qg, ct, st), _norm_rope(kt, kg, ct, st)

    qk_rows = [(z_a, 0, ATTN_DIM), (z_a, 1, ATTN_DIM), (cos2, 0, HEAD), (sin2, 0, HEAD)]
    q_rot, k_rot = _rows_call(qk_fwd, qk_rows, [vec["q_norm"], vec["k_norm"]], [(ATTN_DIM, BF16)] * 2, tm=512, name="attn_pre")
    def group(t, g, off=0):
        return t[:, off + g * GROUP_DIM:off + (g + 1) * GROUP_DIM].astype(BF16)

    qkv = [(group(q_rot, g), group(k_rot, g), group(z_a, g, 2 * ATTN_DIM)) for g in range(len(ATTN_DILATIONS))]
    outs, lses = zip(*[_attn_fwd(*qkv[g], d) for g, d in enumerate(ATTN_DILATIONS)])
    comb_rows = [(t, 0, GROUP_DIM) for t in outs + lses]
    y_attn = _rows_call(_attn_combine, comb_rows, [], [(GROUP_DIM, BF16)], tm=512, name="attn_combine")[0]

    br = _mm(y_rwkv, w_brr, name="branch_rwkv")
    ba = _mm(y_attn, w_bra, name="branch_attn")
    merge_rows = [(z_g, 0, D_MODEL), (z_g, 1, D_MODEL), (br, 0, D_MODEL), (ba, 0, D_MODEL)]
    merged = _rows_call(_merge, merge_rows, [], [(D_MODEL, BF16)], tm=512, name="merge")[0]
    x2 = _mm(merged, w_o, res=x1, name="out_proj")
    x3, ffn2_saved = _ffn_fwd(x2, vec["ffn2_norm"], wb["ffn2_w_gate"], wb["ffn2_w_up"], wb["ffn2_w_down"], "ffn2")
    hp = _rows_call(_rms, [(x3, 0, D_MODEL)], [vec["ple_norm"]], [(D_MODEL, BF16)], tm=512, name="ple_norm")[0]
    pg = _mm(hp, w_pg, name="ple_gate")
    pp = _mm(p_tok, w_pp, name="ple_proj")

    def head(x3t, pgt, ppt, tt):
        sg = _sigmoid(pgt)
        err = x3t + sg * ppt - tt
        dx4 = err * (1.0 / D_MODEL)
        loss = 0.5 * jnp.sum(jnp.mean(err * err, axis=-1, keepdims=True), axis=0, keepdims=True)
        return dx4, dx4 * ppt * sg * (1.0 - sg), dx4 * sg, jnp.broadcast_to(loss, (8, 128))

    head_rows = [(x3, 0, D_MODEL), (pg, 0, D_MODEL), (pp, 0, D_MODEL), (tgt, 0, D_MODEL)]
    dx4, dpg, dpp, loss_tile = _rows_call(head, head_rows, [], [(D_MODEL, F32), (D_MODEL, BF16), (D_MODEL, BF16)], [(8, 128)],
                                          tm=512, name="ple_loss")

    c_arr = jnp.reshape(ci, (1,)).astype(jnp.int32)
    mc_arr = jnp.stack([me, ci]).astype(jnp.int32)
    red = {g: _GroupReduce(g, grp, c_arr, mc_arr) for g, grp in REDUCE_GROUPS.items()}

    done = {}

    def update(summed):
        for n, g2d in summed.items():
            if n == "lora":
                w_, m_, v_ = (_lora_stack({k: t[k][0] for k, _ in LORA}) for t in (wts, mom_m, mom_v))
            else:
                w_, m_, v_ = laid(wts, n), laid(mom_m, n), laid(mom_v, n)
            done[n] = _adamw(w_, g2d, m_, v_, name=f"adamw_{n}", with_grad=True)
    gw, gs = {}, {}
    gw["ple_w_proj"] = _mm(p_tok, dpp, ta=True, name="ple_proj_dw")
    gw["ple_w_gate"] = _mm(hp, dpg, ta=True, name="ple_gate_dw")
    dx3, dx3_bf, gs["ple_norm"] = _mm(dpg, w_pg, tb=True, post=_norm_bwd_post(x3, vec["ple_norm"], dx4), name="ple_gate_dx")
    dx2, dx2_bf, gs["ffn2_norm"], gw["ffn2_w_gate"], gw["ffn2_w_up"], gw["ffn2_w_down"] = _ffn_bwd(
        dx3, dx3_bf, ffn2_saved, vec["ffn2_norm"], wb["ffn2_w_gate"], wb["ffn2_w_up"], wb["ffn2_w_down"], "ffn2")
    gw["ple_w_gate"] = gw["ple_w_gate"].reshape(N_CHIPS, D_MODEL // N_CHIPS, D_MODEL)
    tok = red["f2"].swap_start([gw[w[0]] for w in REDUCE_GROUPS["f2"]])
    gw["w_out"] = _mm(merged, dx2_bf, ta=True, name="out_proj_dw")
    dmerged = _mm(dx2_bf, w_o, tb=True, dep=tok, name="out_proj_dx")

    def merge_bwd(zgr, zga, brt, bat, ct):
        _, vjp = jax.vjp(_merge, zgr, zga, brt, bat)
        d1, d2, d3, d4 = vjp(ct)
        return jnp.concatenate([d1, d2], axis=1), d3, d4

    dz_g, dbr, dba = _rows_call(merge_bwd, merge_rows + [(dmerged, 0, D_MODEL)], [],
                                [(2 * D_MODEL, BF16), (D_MODEL, BF16), (D_MODEL, BF16)], tm=512, name="merge_bwd")
    tok = red["f2"].swap_wait_ici_start(dz_g)
    gw["w_br_rwkv"] = _mm(y_rwkv, dbr, ta=True, name="branch_rwkv_dw")
    gw["w_br_attn"] = _mm(y_attn, dba, ta=True, name="branch_attn_dw")
    dy_rwkv = _mm(dbr, w_brr, tb=True, dep=tok, name="branch_rwkv_dx")
    dy_attn = _mm(dba, w_bra, tb=True, dep=tok, name="branch_attn_dx")

    def comb_bwd(*t):
        _, vjp = jax.vjp(_attn_combine, *t[:6])
        return vjp(t[6])

    dcomb = _rows_call(comb_bwd, comb_rows + [(dy_attn, 0, GROUP_DIM)], [], [(GROUP_DIM, F32)] * 6, tm=512, name="attn_combine_bwd")
    dqs, dks, dvs = zip(*[_attn_bwd(*qkv[g], d, dcomb[g], dcomb[3 + g]) for g, d in enumerate(ATTN_DILATIONS)])

    def qk_bwd(qt, kt, ct, st, *rest):
        dq = jnp.concatenate(rest[0:3], axis=1)
        dk = jnp.concatenate(rest[3:6], axis=1)
        qg, kg = rest[9], rest[10]
        _, vjp = jax.vjp(lambda a_, b_, c_, d_: qk_fwd(a_, b_, ct, st, c_, d_), qt, kt, qg, kg)
        dqt, dkt, dqg, dkg = vjp((dq, dk))
        return jnp.concatenate((dqt, dkt) + tuple(rest[6:9]), axis=1), dqg, dkg

    dz_a, gs["q_norm"], gs["k_norm"] = _rows_call(
        qk_bwd, qk_rows + [(t, 0, GROUP_DIM) for t in dqs + dks + dvs], [vec["q_norm"], vec["k_norm"]],
        [(ATTN_COLS, BF16)], [(1, HEAD), (1, HEAD)], tm=512, name="attn_pre_bwd")
    tok = red["f2"].ici_wait_join_start(dz_a)

    def post_bwd(*t):
        _, vjp = jax.vjp(_rwkv_post, *t[:5], *t[6:])
        return vjp(t[5])

    dy_scan, dr_post, dk2_post, dv_post, dgate_r, gs["rwkv_gn_w"], gs["rwkv_gn_b"], gs["rwkv_r_k"] = _rows_call(
        post_bwd, post_rows + [(dy_rwkv, 0, RWKV_DIM)], post_params, [(RWKV_DIM, F32)] * 5, [(1, RWKV_DIM)] * 3,
        tm=512, name="rwkv_post_bwd", dep=tok)
    update(red["f2"].join_wait(dy_scan))
    dr_s, dlw, dk2_s, dv_s, dna, dkb = _wkv_bwd(zs, lw, k2, na, kb, s0s, invs, dy_scan)

    def pre_bwd(zt, c_r1, c_r2, c_lw, c_k1, c_k2, c_v1, c_v2, c_a, c_b, c_g, *params):
        _, vjp = jax.vjp(_rwkv_pre, zt, *params)
        return vjp((c_r1 + c_r2, c_lw, c_k1 + c_k2, c_v1 + c_v2, c_a, c_b, c_g))

    pre_cts = [dr_s, dr_post, dlw, dk2_s, dk2_post, dv_s, dv_post, dna, dkb, dgate_r]
    dzs, gs["rwkv_w0"], g_w2, gs["rwkv_a0"], g_a2, g_g2, gs["rwkv_k_k"], gs["rwkv_k_a"] = _rows_call(
        pre_bwd, [(zs, 0, RWKV_COLS)] + [(t, 0, RWKV_DIM) for t in pre_cts], pre_params, [(RWKV_COLS, F32)],
        [q.shape for q in pre_params], tm=512, name="rwkv_pre_bwd")
    dz_r, gs["rwkv_mu"] = _shift_bwd(z_r, vec["rwkv_mu"], dzs)

    g_w_in = jnp.concatenate([_mm(h, dz_r, ta=True, name="in_rwkv_dw"), _mm(h, dz_a, ta=True, name="in_attn_dw"),
                              _mm(h, dz_g, ta=True, name="in_gate_dw")], axis=1)
    gw["w_in"], gw["lora"] = g_w_in, jnp.concatenate([g_w2, g_a2, g_g2], axis=0)
    gw["w_out"] = gw["w_out"].reshape(N_CHIPS, D_MODEL // N_CHIPS, D_MODEL)
    tok = red["mx"].swap_start([gw[w[0]] for w in REDUCE_GROUPS["mx"]])
    dh = _mm(dz_r, w_in_r, tb=True, dep=tok, name="in_rwkv_dx")
    dh = _mm(dz_a, w_in_a, tb=True, res=dh, name="in_attn_dx")
    dx1, dx1_bf, gs["mix_norm"] = _mm(dz_g, w_in_g, tb=True, res=dh, post=_norm_bwd_post(x1, vec["mix_norm"], dx2), name="in_gate_dx")
    tok_mx = red["mx"].swap_wait_ici_start(dx1_bf)
    hooks = {"down": lambda d_wd: red["f1d"].swap_start([d_wd], after=tok_mx),
             "mid": lambda dgate: red["f1d"].swap_wait_ici_start(dgate),
             "dw": lambda d_wgt, d_wut: red["f1g"].swap_start([d_wgt, d_wut]),
             "dx": lambda part: red["f1g"].swap_wait_ici_start(part) + red["f1d"].ici_wait_join_start(part),
             "end": lambda dx_: tokens.setdefault("mx_join", red["mx"].ici_wait_join_start(dx_))}
    tokens = {}
    dx0, _, gs["ffn1_norm"], gw["ffn1_w_gate"], gw["ffn1_w_up"], gw["ffn1_w_down"] = _ffn_bwd(
        dx1, dx1_bf, ffn1_saved, vec["ffn1_norm"], wb["ffn1_w_gate"], wb["ffn1_w_up"], wb["ffn1_w_down"], "ffn1", hooks=hooks)

    flat = jnp.concatenate([gs[n].reshape(-1) for n, _ in SMALL] + [loss_tile[0, 0:1]])
    small_buf = jnp.pad(flat, (0, SMALL_ROWS * PACK_COLS - flat.shape[0])).reshape(SMALL_ROWS, PACK_COLS)
    small_issue, small_expect = _small_gather_plan()
    ss_s, rs_s, small_bufs, tok_s = _copies_start("small_gather", [small_buf, lax.empty((8, SMALL_ROWS, PACK_COLS), F32)], 7,
                                                  small_issue, after=tokens["mx_join"])
    for g in ("mx", "f1d"):
        update(red[g].join_wait(tok_s))
    tok = red["f1g"].ici_wait_join_start(done["w_in"][1])
    small_buf, slots = _copies_wait("small_gather_wait", small_bufs, ss_s, rs_s, done["ffn1_w_down"][1], small_expect)
    small_sum = _sum_slots(lax.dynamic_update_slice(slots, small_buf[None], (4 * xi + 2 * yi + ci, 0, 0)))
    n_small = sum(sz for _, sz in SMALL)
    loss = small_sum.reshape(-1)[n_small]
    grad_small = _unpack_small(small_sum, {n: wts[n].shape for n, _ in SMALL})
    d_s, m_s, v_s = _adamw(_pack_small(wts), small_sum, _pack_small(mom_m), _pack_small(mom_v), name="adamw_small", dep=tok)
    shapes = {n: wts[n].shape for n, _ in SMALL}
    d_s, m_s, v_s = _unpack_small(d_s, shapes), _unpack_small(m_s, shapes), _unpack_small(v_s, shapes)
    grads, deltas, new_m, new_v = {}, {}, {}, {}
    for n, _ in SMALL:
        grads[n], deltas[n], new_m[n], new_v[n] = grad_small[n], d_s[n], m_s[n], v_s[n]
    update(red["f1g"].join_wait(m_s["ffn1_norm"]))
    for n, res in done.items():
        for store, val in zip((grads, deltas, new_m, new_v), res):
            if n == "lora":
                store.update({k: t[None] for k, t in _lora_split(val).items()})
            else:
                store[n] = (jnp.transpose(val) if n in TRANSPOSED else val)[None]

    return (loss, dx0[None], *[grads[n] for n in WEIGHTS], *[deltas[n] for n in WEIGHTS],
            *[new_m[n] for n in WEIGHTS], *[new_v[n] for n in WEIGHTS])
```
